```python
import math
import jax, jax.numpy as jnp
from jax import lax
import numpy as np

D_MODEL = 1024
BATCH = 8
SEQ = 8192
DEPTH = 2

HEAD_DIM = 64
Q_HEADS = 8
KV_HEADS = 2
GROUP = Q_HEADS // KV_HEADS
ATTN_WIDTH = Q_HEADS * HEAD_DIM
KV_WIDTH = KV_HEADS * HEAD_DIM
WINDOW = 128
BLOCK = 128
POOL_WIDTH = D_MODEL - ATTN_WIDTH
POOL_WINDOWS = (2, 4, 8, 16)
POOL_GROUPS = len(POOL_WINDOWS)
POOL_GC = POOL_WIDTH // POOL_GROUPS
EVEN_IN = ATTN_WIDTH + 2 * KV_WIDTH + ATTN_WIDTH + POOL_WIDTH + POOL_WIDTH
EVEN_MIX = ATTN_WIDTH + POOL_WIDTH
CONV_WIDTH = D_MODEL
CONV_K = 31
ODD_IN = 3 * CONV_WIDTH
EPS = 1e-6
NEG = -1e30
N_EVEN = (DEPTH + 1) // 2
N_ODD = DEPTH // 2

kernel_name = "hybrid_swa_pool_conformer_sandwich"


def rms_norm(x, g):
    xf = x.astype(jnp.float32)
    y = xf * lax.rsqrt(jnp.mean(xf * xf, axis=-1, keepdims=True) + EPS)
    return (y * g.astype(jnp.float32)).astype(x.dtype)


def alibi_slopes(n):
    return jnp.exp2(-8.0 * jnp.arange(1, n + 1, dtype=jnp.float32) / n)


def sliding_window_attention(q, k, v, sinks):
    B, S, _ = q.shape
    nb = S // BLOCK
    q = q.reshape(B, nb, BLOCK, KV_HEADS, GROUP, HEAD_DIM)
    k = k.reshape(B, nb, BLOCK, KV_HEADS, HEAD_DIM)
    v = v.reshape(B, nb, BLOCK, KV_HEADS, HEAD_DIM)
    kpad = jnp.zeros_like(k[:, :1])
    vpad = jnp.zeros_like(v[:, :1])
    kk = jnp.concatenate([jnp.concatenate([kpad, k[:, :-1]], axis=1), k], axis=2)
    vv = jnp.concatenate([jnp.concatenate([vpad, v[:, :-1]], axis=1), v], axis=2)
    scores = jnp.einsum('bnqkgd,bnskd->bnkgqs', q, kk).astype(jnp.float32) * (HEAD_DIM ** -0.5)
    qi = jnp.arange(BLOCK)[:, None] + BLOCK
    sj = jnp.arange(2 * BLOCK)[None, :]
    dist = qi - sj
    key_pos = jnp.arange(nb)[:, None, None] * BLOCK + sj[None] - BLOCK
    valid = (dist >= 0)[None] & (dist < WINDOW)[None] & (key_pos >= 0)
    slopes = alibi_slopes(Q_HEADS).reshape(KV_HEADS, GROUP)
    bias = -slopes[:, :, None, None] * dist.astype(jnp.float32)
    scores = jnp.where(valid[None, :, None, None], scores + bias, NEG)
    sink = sinks.astype(jnp.float32).reshape(KV_HEADS, GROUP)[None, None, :, :, None, None]
    mx = jnp.maximum(jnp.max(scores, axis=-1, keepdims=True), sink)
    p = jnp.exp(scores - mx)
    p = p / (jnp.sum(p, axis=-1, keepdims=True) + jnp.exp(sink - mx))
    out = jnp.einsum('bnkgqs,bnskd->bnqkgd', p.astype(vv.dtype), vv)
    return out.reshape(B, S, ATTN_WIDTH)


def multiscale_pool(u, pool_w, pool_scale):
    B, S, _ = u.shape
    uf = u.astype(jnp.float32).reshape(B, S, POOL_GROUPS, POOL_GC)
    cs = jnp.concatenate([jnp.zeros_like(uf[:, :1]), jnp.cumsum(uf, axis=1)], axis=1)
    t = jnp.arange(S)[:, None]
    win = jnp.array(POOL_WINDOWS, dtype=jnp.int32)[None, :]
    lo = jnp.maximum(t + 1 - win, 0)
    cnt = (t + 1 - lo).astype(jnp.float32)
    lower = cs[:, lo, jnp.arange(POOL_GROUPS)[None, :]]
    pooled = (cs[:, 1:] - lower) / cnt[None, :, :, None] - uf
    y = jnp.einsum('bsgc,gcd->bsgd', pooled, pool_w.astype(jnp.float32))
    y = y * pool_scale.astype(jnp.float32).reshape(POOL_GROUPS, POOL_GC)
    return y.reshape(B, S, POOL_WIDTH).astype(u.dtype)


def even_mixer(h, w_in, sinks, pool_w, pool_scale, w_out):
    proj = h @ w_in
    splits = np.cumsum([ATTN_WIDTH, KV_WIDTH, KV_WIDTH, ATTN_WIDTH, POOL_WIDTH]).tolist()
    q, k, v, ga, u, gb = jnp.split(proj, splits, axis=-1)
    ya = sliding_window_attention(q, k, v, sinks).astype(h.dtype) * jax.nn.silu(ga)
    yb = multiscale_pool(u, pool_w, pool_scale) * jax.nn.silu(gb)
    return jnp.concatenate([ya, yb], axis=-1) @ w_out


def odd_mixer(h, w_in, dw_w, dw_b, ln_g, ln_b, w_out):
    proj = h @ w_in
    a, b, gate = jnp.split(proj, [CONV_WIDTH, 2 * CONV_WIDTH], axis=-1)
    glu = a * jax.nn.sigmoid(b)
    conv = lax.conv_general_dilated(
        glu, dw_w.astype(glu.dtype), window_strides=(1,), padding=[(CONV_K - 1, 0)],
        dimension_numbers=('NWC', 'WIO', 'NWC'), feature_group_count=CONV_WIDTH)
    cf = conv.astype(jnp.float32) + dw_b.astype(jnp.float32)
    mu = jnp.mean(cf, axis=-1, keepdims=True)
    var = jnp.mean(jnp.square(cf - mu), axis=-1, keepdims=True)
    cn = (cf - mu) * lax.rsqrt(var + EPS) * ln_g.astype(jnp.float32) + ln_b.astype(jnp.float32)
    y = jax.nn.silu(cn).astype(h.dtype) * jax.nn.silu(gate)
    return y @ w_out


def _fwd_setup_inputs(seed: int = 0) -> dict:
    key = jax.random.key(seed)
    ks = jax.random.split(key, 16)
    f32 = jnp.float32
    nrm = lambda k, shape, s: jax.random.normal(k, shape, f32) * s
    return {
        'x': nrm(ks[0], (BATCH, SEQ, D_MODEL), 1.0),
        'pre_norm': 1.0 + nrm(ks[1], (DEPTH, D_MODEL), 0.05),
        'post_norm': 1.0 + nrm(ks[2], (DEPTH, D_MODEL), 0.05),
        'a_w_in': nrm(ks[3], (N_EVEN, D_MODEL, EVEN_IN), D_MODEL ** -0.5),
        'a_sinks': nrm(ks[4], (N_EVEN, Q_HEADS), 0.5),
        'b_pool_w': nrm(ks[5], (N_EVEN, POOL_GROUPS, POOL_GC, POOL_GC), POOL_GC ** -0.5),
        'b_pool_scale': 1.0 + nrm(ks[6], (N_EVEN, POOL_WIDTH), 0.1),
        'ab_w_out': nrm(ks[7], (N_EVEN, EVEN_MIX, D_MODEL), EVEN_MIX ** -0.5),
        'c_w_in': nrm(ks[8], (N_ODD, D_MODEL, ODD_IN), D_MODEL ** -0.5),
        'c_dw_w': nrm(ks[9], (N_ODD, CONV_K, 1, CONV_WIDTH), CONV_K ** -0.5),
        'c_dw_b': nrm(ks[10], (N_ODD, CONV_WIDTH), 0.02),
        'c_ln_g': 1.0 + nrm(ks[11], (N_ODD, CONV_WIDTH), 0.05),
        'c_ln_b': nrm(ks[12], (N_ODD, CONV_WIDTH), 0.02),
        'c_w_out': nrm(ks[13], (N_ODD, CONV_WIDTH, D_MODEL), CONV_WIDTH ** -0.5),
    }


def _fwd_reference(x, pre_norm, post_norm, a_w_in, a_sinks, b_pool_w, b_pool_scale, ab_w_out,
              c_w_in, c_dw_w, c_dw_b, c_ln_g, c_ln_b, c_w_out):
    for layer in range(DEPTH):
        h = rms_norm(x, pre_norm[layer])
        if layer % 2 == 0:
            i = layer // 2
            y = even_mixer(h, a_w_in[i], a_sinks[i], b_pool_w[i], b_pool_scale[i], ab_w_out[i])
        else:
            i = layer // 2
            y = odd_mixer(h, c_w_in[i], c_dw_w[i], c_dw_b[i], c_ln_g[i], c_ln_b[i], c_w_out[i])
        x = x + rms_norm(y, post_norm[layer])
    return x


import jax as _jax
import jax.numpy as _jnp

TWIN_FORMAT = 'train_step'
FWD_PARAMS = ['x', 'pre_norm', 'post_norm', 'a_w_in', 'a_sinks', 'b_pool_w', 'b_pool_scale', 'ab_w_out', 'c_w_in', 'c_dw_w', 'c_dw_b', 'c_ln_g', 'c_ln_b', 'c_w_out']
TWIN_WEIGHTS = ['pre_norm', 'post_norm', 'a_w_in', 'a_sinks', 'b_pool_w', 'b_pool_scale', 'ab_w_out', 'c_w_in', 'c_dw_w', 'c_dw_b', 'c_ln_g', 'c_ln_b', 'c_w_out']
TWIN_DIFF_INPUT = 'x'
TWIN_INPUTS = ['x', 'pre_norm', 'post_norm', 'a_w_in', 'a_sinks', 'b_pool_w', 'b_pool_scale', 'ab_w_out', 'c_w_in', 'c_dw_w', 'c_dw_b', 'c_ln_g', 'c_ln_b', 'c_w_out', 'loss_target', 'm_pre_norm', 'm_post_norm', 'm_a_w_in', 'm_a_sinks', 'm_b_pool_w', 'm_b_pool_scale', 'm_ab_w_out', 'm_c_w_in', 'm_c_dw_w', 'm_c_dw_b', 'm_c_ln_g', 'm_c_ln_b', 'm_c_w_out', 'v_pre_norm', 'v_post_norm', 'v_a_w_in', 'v_a_sinks', 'v_b_pool_w', 'v_b_pool_scale', 'v_ab_w_out', 'v_c_w_in', 'v_c_dw_w', 'v_c_dw_b', 'v_c_ln_g', 'v_c_ln_b', 'v_c_w_out']
TWIN_OUTPUTS = ['loss', 'grad_x', 'grad_pre_norm', 'grad_post_norm', 'grad_a_w_in', 'grad_a_sinks', 'grad_b_pool_w', 'grad_b_pool_scale', 'grad_ab_w_out', 'grad_c_w_in', 'grad_c_dw_w', 'grad_c_dw_b', 'grad_c_ln_g', 'grad_c_ln_b', 'grad_c_w_out', 'delta_pre_norm', 'delta_post_norm', 'delta_a_w_in', 'delta_a_sinks', 'delta_b_pool_w', 'delta_b_pool_scale', 'delta_ab_w_out', 'delta_c_w_in', 'delta_c_dw_w', 'delta_c_dw_b', 'delta_c_ln_g', 'delta_c_ln_b', 'delta_c_w_out', 'new_m_pre_norm', 'new_m_post_norm', 'new_m_a_w_in', 'new_m_a_sinks', 'new_m_b_pool_w', 'new_m_b_pool_scale', 'new_m_ab_w_out', 'new_m_c_w_in', 'new_m_c_dw_w', 'new_m_c_dw_b', 'new_m_c_ln_g', 'new_m_c_ln_b', 'new_m_c_w_out', 'new_v_pre_norm', 'new_v_post_norm', 'new_v_a_w_in', 'new_v_a_sinks', 'new_v_b_pool_w', 'new_v_b_pool_scale', 'new_v_ab_w_out', 'new_v_c_w_in', 'new_v_c_dw_w', 'new_v_c_dw_b', 'new_v_c_ln_g', 'new_v_c_ln_b', 'new_v_c_w_out']
TWIN_LEAF_KINDS = {'loss': 'loss', 'grad_x': 'grad_x', 'grad_pre_norm': 'grad_w', 'grad_post_norm': 'grad_w', 'grad_a_w_in': 'grad_w', 'grad_a_sinks': 'grad_w', 'grad_b_pool_w': 'grad_w', 'grad_b_pool_scale': 'grad_w', 'grad_ab_w_out': 'grad_w', 'grad_c_w_in': 'grad_w', 'grad_c_dw_w': 'grad_w', 'grad_c_dw_b': 'grad_w', 'grad_c_ln_g': 'grad_w', 'grad_c_ln_b': 'grad_w', 'grad_c_w_out': 'grad_w', 'delta_pre_norm': 'delta_w', 'delta_post_norm': 'delta_w', 'delta_a_w_in': 'delta_w', 'delta_a_sinks': 'delta_w', 'delta_b_pool_w': 'delta_w', 'delta_b_pool_scale': 'delta_w', 'delta_ab_w_out': 'delta_w', 'delta_c_w_in': 'delta_w', 'delta_c_dw_w': 'delta_w', 'delta_c_dw_b': 'delta_w', 'delta_c_ln_g': 'delta_w', 'delta_c_ln_b': 'delta_w', 'delta_c_w_out': 'delta_w', 'new_m_pre_norm': 'new_m', 'new_m_post_norm': 'new_m', 'new_m_a_w_in': 'new_m', 'new_m_a_sinks': 'new_m', 'new_m_b_pool_w': 'new_m', 'new_m_b_pool_scale': 'new_m', 'new_m_ab_w_out': 'new_m', 'new_m_c_w_in': 'new_m', 'new_m_c_dw_w': 'new_m', 'new_m_c_dw_b': 'new_m', 'new_m_c_ln_g': 'new_m', 'new_m_c_ln_b': 'new_m', 'new_m_c_w_out': 'new_m', 'new_v_pre_norm': 'new_v', 'new_v_post_norm': 'new_v', 'new_v_a_w_in': 'new_v', 'new_v_a_sinks': 'new_v', 'new_v_b_pool_w': 'new_v', 'new_v_b_pool_scale': 'new_v', 'new_v_ab_w_out': 'new_v', 'new_v_c_w_in': 'new_v', 'new_v_c_dw_w': 'new_v', 'new_v_c_dw_b': 'new_v', 'new_v_c_ln_g': 'new_v', 'new_v_c_ln_b': 'new_v', 'new_v_c_w_out': 'new_v'}


def _forward(args):
    return _fwd_reference(*[args[k] for k in FWD_PARAMS])


def _output_shape():
    def fwd():
        inp = _fwd_setup_inputs(0)
        return _fwd_reference(*[inp[k] for k in FWD_PARAMS])
    out = _jax.eval_shape(fwd)
    return out.shape, out.dtype

N_MICROBATCH = 1
ADAM_LR = 0.001
ADAM_B1 = 0.9
ADAM_B2 = 0.999
ADAM_EPS = 1e-08
ADAM_WD = 0.01
ADAM_STEP = 10
PER_EXAMPLE_BATCH_AXIS = {'x': 0, 'loss_target': 0}
SHARED_INPUTS = []
_WEIGHT_DTYPES = {'pre_norm': _jnp.float32, 'post_norm': _jnp.float32, 'a_w_in': _jnp.float32, 'a_sinks': _jnp.float32, 'b_pool_w': _jnp.float32, 'b_pool_scale': _jnp.float32, 'ab_w_out': _jnp.float32, 'c_w_in': _jnp.float32, 'c_dw_w': _jnp.float32, 'c_dw_b': _jnp.float32, 'c_ln_g': _jnp.float32, 'c_ln_b': _jnp.float32, 'c_w_out': _jnp.float32}
MOMENT_SCALE = {'pre_norm': 9.750580e-01, 'post_norm': 6.411096e+01, 'a_w_in': 6.790567e-01, 'a_sinks': 3.596695e-01, 'b_pool_w': 9.966613e-01, 'b_pool_scale': 1.296887e+00, 'ab_w_out': 8.788402e-01, 'c_w_in': 5.091705e-01, 'c_dw_w': 4.916737e-01, 'c_dw_b': 3.317294e+00, 'c_ln_g': 1.287479e+00, 'c_ln_b': 1.987374e+00, 'c_w_out': 7.784762e-01}


def _to_microbatches(a, axis):
    t = _jnp.moveaxis(a, axis, 0)
    t = t.reshape((N_MICROBATCH, t.shape[0] // N_MICROBATCH) + t.shape[1:])
    return _jnp.moveaxis(t, 1, axis + 1)


def setup_inputs(seed: int = 0) -> dict:
    inp = _fwd_setup_inputs(seed)
    key = _jax.random.fold_in(_jax.random.key(seed), 7919)
    shape, _ = _output_shape()
    out = dict(inp)
    out["loss_target"] = _jax.random.normal(_jax.random.fold_in(key, 0), shape, _jnp.float32)
    for i, name in enumerate(TWIN_WEIGHTS):
        w = inp[name].astype(_jnp.float32)
        if MOMENT_SCALE is None:
            s = _jnp.sqrt(_jnp.mean(_jnp.square(w)) + 1e-30)
        else:
            s = MOMENT_SCALE[name]
        km, kv = _jax.random.split(_jax.random.fold_in(key, i + 1))
        out[name] = w
        out["m_" + name] = s * _jax.random.normal(km, w.shape, _jnp.float32)
        out["v_" + name] = (s * s) * _jax.random.uniform(kv, w.shape, _jnp.float32, 0.5, 1.5)
    if N_MICROBATCH > 1:
        for name, axis in PER_EXAMPLE_BATCH_AXIS.items():
            out[name] = _to_microbatches(out[name], axis)
    return {'x': out['x'], 'pre_norm': out['pre_norm'], 'post_norm': out['post_norm'], 'a_w_in': out['a_w_in'], 'a_sinks': out['a_sinks'], 'b_pool_w': out['b_pool_w'], 'b_pool_scale': out['b_pool_scale'], 'ab_w_out': out['ab_w_out'], 'c_w_in': out['c_w_in'], 'c_dw_w': out['c_dw_w'], 'c_dw_b': out['c_dw_b'], 'c_ln_g': out['c_ln_g'], 'c_ln_b': out['c_ln_b'], 'c_w_out': out['c_w_out'], 'loss_target': out['loss_target'], 'm_pre_norm': out['m_pre_norm'], 'm_post_norm': out['m_post_norm'], 'm_a_w_in': out['m_a_w_in'], 'm_a_sinks': out['m_a_sinks'], 'm_b_pool_w': out['m_b_pool_w'], 'm_b_pool_scale': out['m_b_pool_scale'], 'm_ab_w_out': out['m_ab_w_out'], 'm_c_w_in': out['m_c_w_in'], 'm_c_dw_w': out['m_c_dw_w'], 'm_c_dw_b': out['m_c_dw_b'], 'm_c_ln_g': out['m_c_ln_g'], 'm_c_ln_b': out['m_c_ln_b'], 'm_c_w_out': out['m_c_w_out'], 'v_pre_norm': out['v_pre_norm'], 'v_post_norm': out['v_post_norm'], 'v_a_w_in': out['v_a_w_in'], 'v_a_sinks': out['v_a_sinks'], 'v_b_pool_w': out['v_b_pool_w'], 'v_b_pool_scale': out['v_b_pool_scale'], 'v_ab_w_out': out['v_ab_w_out'], 'v_c_w_in': out['v_c_w_in'], 'v_c_dw_w': out['v_c_dw_w'], 'v_c_dw_b': out['v_c_dw_b'], 'v_c_ln_g': out['v_c_ln_g'], 'v_c_ln_b': out['v_c_ln_b'], 'v_c_w_out': out['v_c_w_out']}


def _loss(weights, diff, rest, loss_target):
    with _jax.named_scope("forward"):
        args = {**rest, TWIN_DIFF_INPUT: diff, **{k: w.astype(_WEIGHT_DTYPES[k]) for k, w in weights.items()}}
        y = _forward(args)
    with _jax.named_scope("loss_head"):
        err = _jnp.square(y.astype(_jnp.float32) - loss_target)
        return 0.5 * _jnp.sum(_jnp.mean(err, axis=-1)) if err.ndim else 0.5 * err


def _adamw(w, g, m, v):
    m = ADAM_B1 * m + (1.0 - ADAM_B1) * g
    v = ADAM_B2 * v + (1.0 - ADAM_B2) * _jnp.square(g)
    m_hat = m / (1.0 - ADAM_B1 ** ADAM_STEP)
    v_hat = v / (1.0 - ADAM_B2 ** ADAM_STEP)
    delta = -ADAM_LR * (m_hat / (_jnp.sqrt(v_hat) + ADAM_EPS) + ADAM_WD * w)
    return delta, m, v


def reference(x, pre_norm, post_norm, a_w_in, a_sinks, b_pool_w, b_pool_scale, ab_w_out, c_w_in, c_dw_w, c_dw_b, c_ln_g, c_ln_b, c_w_out, loss_target, m_pre_norm, m_post_norm, m_a_w_in, m_a_sinks, m_b_pool_w, m_b_pool_scale, m_ab_w_out, m_c_w_in, m_c_dw_w, m_c_dw_b, m_c_ln_g, m_c_ln_b, m_c_w_out, v_pre_norm, v_post_norm, v_a_w_in, v_a_sinks, v_b_pool_w, v_b_pool_scale, v_ab_w_out, v_c_w_in, v_c_dw_w, v_c_dw_b, v_c_ln_g, v_c_ln_b, v_c_w_out):
    given = dict(x=x, pre_norm=pre_norm, post_norm=post_norm, a_w_in=a_w_in, a_sinks=a_sinks, b_pool_w=b_pool_w, b_pool_scale=b_pool_scale, ab_w_out=ab_w_out, c_w_in=c_w_in, c_dw_w=c_dw_w, c_dw_b=c_dw_b, c_ln_g=c_ln_g, c_ln_b=c_ln_b, c_w_out=c_w_out, loss_target=loss_target, m_pre_norm=m_pre_norm, m_post_norm=m_post_norm, m_a_w_in=m_a_w_in, m_a_sinks=m_a_sinks, m_b_pool_w=m_b_pool_w, m_b_pool_scale=m_b_pool_scale, m_ab_w_out=m_ab_w_out, m_c_w_in=m_c_w_in, m_c_dw_w=m_c_dw_w, m_c_dw_b=m_c_dw_b, m_c_ln_g=m_c_ln_g, m_c_ln_b=m_c_ln_b, m_c_w_out=m_c_w_out, v_pre_norm=v_pre_norm, v_post_norm=v_post_norm, v_a_w_in=v_a_w_in, v_a_sinks=v_a_sinks, v_b_pool_w=v_b_pool_w, v_b_pool_scale=v_b_pool_scale, v_ab_w_out=v_ab_w_out, v_c_w_in=v_c_w_in, v_c_dw_w=v_c_dw_w, v_c_dw_b=v_c_dw_b, v_c_ln_g=v_c_ln_g, v_c_ln_b=v_c_ln_b, v_c_w_out=v_c_w_out)
    weights = {n: given[n] for n in TWIN_WEIGHTS}
    shared = {n: given[n] for n in SHARED_INPUTS}
    per_example = {n: given[n] for n in ['x']}
    grad_fn = _jax.value_and_grad(_loss, argnums=(0, 1))

    def one_microbatch(ex, loss_target):
        ex = dict(ex)
        diff = ex.pop(TWIN_DIFF_INPUT)
        return grad_fn(weights, diff, {**shared, **ex}, loss_target)

    if N_MICROBATCH == 1:
        loss, (grad_w, grad_x) = one_microbatch(per_example, given["loss_target"])
    else:
        def body(carry, xs):
            loss_sum, grad_sum = carry
            l_k, (gw_k, gx_k) = one_microbatch(xs[0], xs[1])
            with _jax.named_scope("update"):
                return (loss_sum + l_k, _jax.tree.map(_jnp.add, grad_sum, gw_k)), gx_k

        init = (_jnp.zeros((), _jnp.float32), _jax.tree.map(_jnp.zeros_like, weights))
        (loss, grad_w), grad_x = _jax.lax.scan(body, init, (per_example, given["loss_target"]))
    with _jax.named_scope("update"):
        delta_w, new_m, new_v = {}, {}, {}
        for n in TWIN_WEIGHTS:
            delta_w[n], new_m[n], new_v[n] = _adamw(weights[n], grad_w[n], given["m_" + n], given["v_" + n])
    return (loss, grad_x, *[grad_w[n] for n in TWIN_WEIGHTS], *[delta_w[n] for n in TWIN_WEIGHTS],
            *[new_m[n] for n in TWIN_WEIGHTS], *[new_v[n] for n in TWIN_WEIGHTS])
```

```python
import functools

import jax
import jax.numpy as jnp
from jax import lax
from jax.experimental import pallas as pl
from jax.experimental.pallas import tpu as pltpu

F32 = jnp.float32
BF16 = jnp.bfloat16

D_MODEL = 1024
EPS = 1e-6
NEG = -1e30
HEAD_DIM = 64
GROUP = 4
KV_HEADS = 2
BLOCK = 128
EVEN_IN = 2304
COL_Q, COL_K, COL_GA, COL_U, COL_GB = 0, 512, 768, 1280, 1792
POOL_GROUPS = 4
POOL_GC = 128
POOL_HALO = 16
CONV_K = 31
CONV_HALO = 32
N_DEV = 8

ADAM_LR = 0.001
ADAM_B1 = 0.9
ADAM_B2 = 0.999
ADAM_EPS = 1e-08
ADAM_WD = 0.01
ADAM_STEP = 10

VMEM_LIMIT_BYTES = 56 * 1024 * 1024

NT = (((1,), (1,)), ((), ()))
TN = (((0,), (0,)), ((), ()))
MESH_ID = pl.DeviceIdType.MESH


def _params(*sem):
    return pltpu.CompilerParams(dimension_semantics=sem, vmem_limit_bytes=VMEM_LIMIT_BYTES)


def _const_spec(shape):
    nd = len(shape)
    return pl.BlockSpec(shape, lambda *_: (0,) * nd, pipeline_mode=pl.Buffered(1))


def _sigmoid(v):
    return 1.0 / (1.0 + jnp.exp(-v))


def _silu(v):
    return v * _sigmoid(v)


def _dsilu(v):
    s = _sigmoid(v)
    return s * (1.0 + v * (1.0 - s))


def _norm_matmul(x, gain, wt, name, tm=512):
    t, n = x.shape[0], wt.shape[0]

    def body(x_ref, g_ref, wt_ref, o_ref):
        xv = x_ref[...]
        r = lax.rsqrt(jnp.mean(xv * xv, axis=-1, keepdims=True) + EPS)
        h = (xv * r * g_ref[...]).astype(BF16)
        o_ref[...] = lax.dot_general(h, wt_ref[...], NT, preferred_element_type=F32)

    return pl.pallas_call(
        body, name=name, grid=(t // tm,),
        out_shape=jax.ShapeDtypeStruct((t, n), F32),
        in_specs=[pl.BlockSpec((tm, D_MODEL), lambda i: (i, 0)), _const_spec((1, D_MODEL)), _const_spec((n, D_MODEL))],
        out_specs=pl.BlockSpec((tm, n), lambda i: (i, 0)),
        compiler_params=_params("parallel"),
    )(x, gain, wt)


def _out_norm_res(a, w, x_in, post, name, target=None, tm=512):
    t = a.shape[0]
    with_loss = target is not None

    def body(*refs):
        if with_loss:
            a_ref, w_ref, x_ref, p_ref, t_ref, y_ref, o_ref, l_ref = refs
        else:
            a_ref, w_ref, x_ref, p_ref, y_ref, o_ref = refs
        y = jnp.dot(a_ref[...], w_ref[...], preferred_element_type=F32)
        y_ref[...] = y
        ry = lax.rsqrt(jnp.mean(y * y, axis=-1, keepdims=True) + EPS)
        xo = x_ref[...] + (y * ry) * p_ref[...]
        if with_loss:
            d = xo - t_ref[...]
            o_ref[...] = d * (1.0 / D_MODEL)

            @pl.when(pl.program_id(0) == 0)
            def _():
                l_ref[...] = jnp.zeros_like(l_ref)

            l_ref[...] += 0.5 * jnp.sum(jnp.mean(d * d, axis=-1, keepdims=True))
        else:
            o_ref[...] = xo

    row = pl.BlockSpec((tm, D_MODEL), lambda i: (i, 0))
    in_specs = [row, _const_spec((D_MODEL, D_MODEL)), row, _const_spec((1, D_MODEL))]
    out_shape = [jax.ShapeDtypeStruct((t, D_MODEL), F32), jax.ShapeDtypeStruct((t, D_MODEL), F32)]
    out_specs = [row, row]
    args = [a, w, x_in, post]
    if with_loss:
        in_specs.append(row)
        args.append(target)
        out_shape.append(jax.ShapeDtypeStruct((8, 128), F32))
        out_specs.append(pl.BlockSpec((8, 128), lambda i: (0, 0)))
    return pl.pallas_call(
        body, name=name, grid=(t // tm,), out_shape=out_shape, in_specs=in_specs, out_specs=out_specs,
        compiler_params=_params("arbitrary"),
    )(*args)


def _post_bwd(g, y, post, w, a, name, tm=512):
    t = g.shape[0]

    def body(g_ref, y_ref, p_ref, w_ref, a_ref, da_ref, dw_ref, dp_ref):
        @pl.when(pl.program_id(0) == 0)
        def _():
            dw_ref[...] = jnp.zeros_like(dw_ref)
            dp_ref[...] = jnp.zeros_like(dp_ref)

        gv = g_ref[...]
        yv = y_ref[...]
        ry = lax.rsqrt(jnp.mean(yv * yv, axis=-1, keepdims=True) + EPS)
        nv = yv * ry
        dp_ref[...] += jnp.sum(gv * nv, axis=0, keepdims=True)
        dn = gv * p_ref[...]
        dy = (ry * (dn - nv * jnp.mean(dn * nv, axis=-1, keepdims=True))).astype(BF16)
        da_ref[...] = lax.dot_general(dy, w_ref[...], NT, preferred_element_type=F32)
        dw_ref[...] += lax.dot_general(a_ref[...], dy, TN, preferred_element_type=F32)

    row = pl.BlockSpec((tm, D_MODEL), lambda i: (i, 0))
    return pl.pallas_call(
        body, name=name, grid=(t // tm,),
        out_shape=[jax.ShapeDtypeStruct((t, D_MODEL), F32), jax.ShapeDtypeStruct((D_MODEL, D_MODEL), F32),
                   jax.ShapeDtypeStruct((1, D_MODEL), F32)],
        in_specs=[row, row, _const_spec((1, D_MODEL)), _const_spec((D_MODEL, D_MODEL)), row],
        out_specs=[row, pl.BlockSpec((D_MODEL, D_MODEL), lambda i: (0, 0)), pl.BlockSpec((1, D_MODEL), lambda i: (0, 0))],
        compiler_params=_params("arbitrary"),
    )(g, y, post, w, a)


def _pre_bwd(dproj, wt, x_in, pre, g, name, tm=256):
    t, n = dproj.shape

    def body(dp_ref, wt_ref, x_ref, pre_ref, g_ref, dx_ref, dwt_ref, dpre_ref):
        @pl.when(pl.program_id(0) == 0)
        def _():
            dwt_ref[...] = jnp.zeros_like(dwt_ref)
            dpre_ref[...] = jnp.zeros_like(dpre_ref)

        dpv = dp_ref[...]
        dh = jnp.dot(dpv, wt_ref[...], preferred_element_type=F32)
        xv = x_ref[...]
        r = lax.rsqrt(jnp.mean(xv * xv, axis=-1, keepdims=True) + EPS)
        xn = xv * r
        pv = pre_ref[...]
        dpre_ref[...] += jnp.sum(dh * xn, axis=0, keepdims=True)
        dxn = dh * pv
        dx_ref[...] = g_ref[...] + r * (dxn - xn * jnp.mean(dxn * xn, axis=-1, keepdims=True))
        h = (xn * pv).astype(BF16)
        dwt_ref[...] += lax.dot_general(dpv, h, TN, preferred_element_type=F32)

    row = pl.BlockSpec((tm, D_MODEL), lambda i: (i, 0))
    return pl.pallas_call(
        body, name=name, grid=(t // tm,),
        out_shape=[jax.ShapeDtypeStruct((t, D_MODEL), F32), jax.ShapeDtypeStruct((n, D_MODEL), F32),
                   jax.ShapeDtypeStruct((1, D_MODEL), F32)],
        in_specs=[pl.BlockSpec((tm, n), lambda i: (i, 0)), _const_spec((n, D_MODEL)), row, _const_spec((1, D_MODEL)), row],
        out_specs=[row, pl.BlockSpec((n, D_MODEL), lambda i: (0, 0)), pl.BlockSpec((1, D_MODEL), lambda i: (0, 0))],
        compiler_params=_params("arbitrary"),
    )(dproj, wt, x_in, pre, g)


def _group_masks():
    lane = lax.broadcasted_iota(jnp.int32, (1, GROUP * HEAD_DIM), 1)
    return [(lane // HEAD_DIM == g).astype(F32) for g in range(GROUP)]


def _stack_groups(v, masks):
    return jnp.concatenate([v * m for m in masks], axis=0)


def _unstack_groups(v, masks):
    out = v[0:BLOCK] * masks[0]
    for g in range(1, GROUP):
        out = out + v[g * BLOCK:(g + 1) * BLOCK] * masks[g]
    return out


def _repeat_head(kv2, kvh):
    first = lax.broadcasted_iota(jnp.int32, kv2.shape, 1) < HEAD_DIM
    rolled = pltpu.roll(kv2, HEAD_DIM, 1)
    one = jnp.where(first, kv2, rolled) if kvh == 0 else jnp.where(first, rolled, kv2)
    return jnp.concatenate([one, one], axis=1)


def _fold_head(v4):
    a = v4[:, 0:128] + v4[:, 128:256]
    return a + pltpu.roll(a, HEAD_DIM, 1)


def _row_consts(kvh, sink_ref):
    rb = lax.broadcasted_iota(jnp.int32, (GROUP * BLOCK, 1), 0) // BLOCK
    slope = jnp.zeros((GROUP * BLOCK, 1), F32)
    sink = jnp.zeros((GROUP * BLOCK, 1), F32)
    for g in range(GROUP):
        h = kvh * GROUP + g
        slope = jnp.where(rb == g, 2.0 ** (-(h + 1)), slope)
        sink = jnp.where(rb == g, sink_ref[0, h], sink)
    return slope, sink


def _attn_probs(qk, k4, kvh, sink_ref, dist, valid, masks):
    qs = _stack_groups(qk, masks).astype(BF16)
    slope, sink = _row_consts(kvh, sink_ref)
    s = lax.dot_general(qs, k4, NT, preferred_element_type=F32) * (HEAD_DIM ** -0.5)
    s = jnp.where(valid, s - slope * dist, NEG)
    mx = jnp.maximum(jnp.max(s, axis=-1, keepdims=True), sink)
    e = jnp.exp(s - mx)
    es = jnp.exp(sink - mx)
    inv = 1.0 / (jnp.sum(e, axis=-1, keepdims=True) + es)
    return qs, e * inv, es * inv


def _dist_valid(key_pos0):
    row = lax.broadcasted_iota(jnp.int32, (GROUP * BLOCK, 2 * BLOCK), 0)
    col = lax.broadcasted_iota(jnp.int32, (GROUP * BLOCK, 2 * BLOCK), 1)
    dist = (row % BLOCK) + BLOCK - col
    valid = (dist >= 0) & (dist < BLOCK) & (col + key_pos0 >= 0)
    return dist.astype(F32), valid


def _pool_forward(u_ext, g, t0):
    n = u_ext.shape[0] - POOL_HALO
    s = u_ext
    for step in range(g + 1):
        s = s + pltpu.roll(s, 1 << step, 0)
    w = 2 << g
    t = t0 + lax.broadcasted_iota(jnp.int32, (n, 1), 0)
    cnt = jnp.minimum(t + 1, w).astype(F32)
    return s[POOL_HALO:] / cnt - u_ext[POOL_HALO:]


def _mix0_fwd(proj, sinks, pool_w, pool_scale, tq=512):
    t = proj.shape[0]
    nblk = tq // BLOCK

    def body(main_ref, halo_ref, sink_ref, pw_ref, ps_ref, o_ref, kv_ref):
        i = pl.program_id(0)
        t0 = i * tq
        masks = _group_masks()
        kv_ref[0:BLOCK, :] = halo_ref[:, COL_K:COL_K + 256]
        kv_ref[BLOCK:, :] = main_ref[:, COL_K:COL_K + 256]

        def block(jb, carry):
            r0 = pl.multiple_of(jb * BLOCK, BLOCK)
            dist, valid = _dist_valid(t0 + r0 - BLOCK)
            q = main_ref[pl.ds(r0, BLOCK), COL_Q:COL_Q + 512]
            ga = main_ref[pl.ds(r0, BLOCK), COL_GA:COL_GA + 512]
            kk = kv_ref[pl.ds(r0, 2 * BLOCK), 0:128]
            vv = kv_ref[pl.ds(r0, 2 * BLOCK), 128:256]
            outs = []
            for kvh in range(KV_HEADS):
                k4 = _repeat_head(kk, kvh).astype(BF16)
                v4 = _repeat_head(vv, kvh).astype(BF16)
                _, p, _ = _attn_probs(q[:, kvh * 256:(kvh + 1) * 256], k4, kvh, sink_ref, dist, valid, masks)
                pv = jnp.dot(p.astype(BF16), v4, preferred_element_type=F32)
                outs.append(_unstack_groups(pv, masks))
            attn = jnp.concatenate(outs, axis=1)
            o_ref[pl.ds(r0, BLOCK), 0:512] = (attn * _silu(ga)).astype(BF16)
            return carry

        lax.fori_loop(0, nblk, block, 0)

        for g in range(POOL_GROUPS):
            cu = COL_U + g * POOL_GC
            cg = COL_GB + g * POOL_GC
            halo_u = jnp.where(i == 0, 0.0, halo_ref[BLOCK - POOL_HALO:BLOCK, cu:cu + POOL_GC])
            u_ext = jnp.concatenate([halo_u, main_ref[:, cu:cu + POOL_GC]], axis=0)
            pooled = _pool_forward(u_ext, g, t0)
            y = jnp.dot(pooled.astype(BF16), pw_ref[g].astype(BF16), preferred_element_type=F32)
            y = y * ps_ref[:, g * POOL_GC:(g + 1) * POOL_GC]
            o_ref[:, 512 + g * POOL_GC:512 + (g + 1) * POOL_GC] = (y * _silu(main_ref[:, cg:cg + POOL_GC])).astype(BF16)

    return pl.pallas_call(
        body, name="mix0_fwd", grid=(t // tq,),
        out_shape=jax.ShapeDtypeStruct((t, D_MODEL), BF16),
        in_specs=[pl.BlockSpec((tq, EVEN_IN), lambda i: (i, 0)),
                  pl.BlockSpec((BLOCK, EVEN_IN), lambda i: (jnp.maximum(i * nblk - 1, 0), 0)),
                  pl.BlockSpec(memory_space=pltpu.SMEM),
                  _const_spec((POOL_GROUPS, POOL_GC, POOL_GC)), _const_spec((1, 512))],
        out_specs=pl.BlockSpec((tq, D_MODEL), lambda i: (i, 0)),
        scratch_shapes=[pltpu.VMEM((tq + BLOCK, 256), F32)],
        compiler_params=_params("parallel"),
    )(proj, proj, sinks, pool_w, pool_scale)


def _mix0_bwd(proj, dmix, sinks, pool_w, pool_scale, tq=512):
    t = proj.shape[0]
    nt = t // tq
    nblk = tq // BLOCK

    def body(main_ref, halo_ref, next_ref, dm_ref, dmn_ref, sink_ref, pw_ref, ps_ref,
             o_ref, dsk_ref, dpw_ref, dps_ref, kv_ref, dkv_ref, carry_ref):
        i = pl.program_id(0)
        ii = nt - 1 - i
        t0 = ii * tq
        masks = _group_masks()

        @pl.when(i == 0)
        def _():
            dsk_ref[...] = jnp.zeros_like(dsk_ref)
            dpw_ref[...] = jnp.zeros_like(dpw_ref)
            dps_ref[...] = jnp.zeros_like(dps_ref)
            carry_ref[...] = jnp.zeros_like(carry_ref)

        kv_ref[0:BLOCK, :] = halo_ref[:, COL_K:COL_K + 256]
        kv_ref[BLOCK:, :] = main_ref[:, COL_K:COL_K + 256]
        dkv_ref[0:tq, :] = jnp.zeros((tq, 256), F32)
        dkv_ref[tq:, :] = carry_ref[...]

        def block(jb, carry):
            r0 = pl.multiple_of(jb * BLOCK, BLOCK)
            dist, valid = _dist_valid(t0 + r0 - BLOCK)
            q = main_ref[pl.ds(r0, BLOCK), COL_Q:COL_Q + 512]
            ga = main_ref[pl.ds(r0, BLOCK), COL_GA:COL_GA + 512]
            dya = dm_ref[pl.ds(r0, BLOCK), 0:512]
            kk = kv_ref[pl.ds(r0, 2 * BLOCK), 0:128]
            vv = kv_ref[pl.ds(r0, 2 * BLOCK), 128:256]
            do = dya * _silu(ga)
            first = lax.broadcasted_iota(jnp.int32, (2 * BLOCK, 128), 1) < HEAD_DIM
            attn, dq, dk, dv = [], [], [], []
            for kvh in range(KV_HEADS):
                k4 = _repeat_head(kk, kvh).astype(BF16)
                v4 = _repeat_head(vv, kvh).astype(BF16)
                qs, p, ps = _attn_probs(q[:, kvh * 256:(kvh + 1) * 256], k4, kvh, sink_ref, dist, valid, masks)
                pb = p.astype(BF16)
                o_k = _unstack_groups(jnp.dot(pb, v4, preferred_element_type=F32), masks)
                do_k = do[:, kvh * 256:(kvh + 1) * 256]
                dos = _stack_groups(do_k, masks).astype(BF16)
                prod = do_k * o_k
                delta = jnp.concatenate([jnp.sum(prod * m, axis=-1, keepdims=True) for m in masks], axis=0)
                dp = lax.dot_general(dos, v4, NT, preferred_element_type=F32)
                ds = (p * (dp - delta)).astype(BF16)
                sink_term = ps * delta
                for g in range(GROUP):
                    h = kvh * GROUP + g
                    dsk_ref[h:h + 1, :] -= jnp.sum(sink_term[g * BLOCK:(g + 1) * BLOCK], keepdims=True)
                scale = HEAD_DIM ** -0.5
                dq.append(_unstack_groups(jnp.dot(ds, k4, preferred_element_type=F32), masks) * scale)
                dk.append(_fold_head(lax.dot_general(ds, qs, TN, preferred_element_type=F32)) * scale)
                dv.append(_fold_head(lax.dot_general(pb, dos, TN, preferred_element_type=F32)))
                attn.append(o_k)
            o_ref[pl.ds(r0, BLOCK), COL_Q:COL_Q + 512] = jnp.concatenate(dq, axis=1).astype(BF16)
            o_all = jnp.concatenate(attn, axis=1)
            o_ref[pl.ds(r0, BLOCK), COL_GA:COL_GA + 512] = (dya * o_all * _dsilu(ga)).astype(BF16)
            dkv = jnp.concatenate([jnp.where(first, dk[0], dk[1]), jnp.where(first, dv[0], dv[1])], axis=1)
            dkv_ref[pl.ds(r0, 2 * BLOCK), :] += dkv
            return carry

        lax.fori_loop(0, nblk, block, 0)
        carry_ref[...] = dkv_ref[0:BLOCK, :]
        o_ref[:, COL_K:COL_K + 256] = dkv_ref[BLOCK:, :].astype(BF16)

        last = ii == nt - 1
        for g in range(POOL_GROUPS):
            cu = COL_U + g * POOL_GC
            cg = COL_GB + g * POOL_GC
            cm = 512 + g * POOL_GC
            pw = pw_ref[g].astype(BF16)
            sc = ps_ref[:, g * POOL_GC:(g + 1) * POOL_GC]
            halo_u = jnp.where(ii == 0, 0.0, halo_ref[BLOCK - POOL_HALO:BLOCK, cu:cu + POOL_GC])
            u_ext = jnp.concatenate([halo_u, main_ref[:, cu:cu + POOL_GC]], axis=0)
            pooled = _pool_forward(u_ext, g, t0).astype(BF16)
            y_raw = jnp.dot(pooled, pw, preferred_element_type=F32)
            gb = main_ref[:, cg:cg + POOL_GC]
            dyb = dm_ref[:, cm:cm + POOL_GC]
            dypool = dyb * _silu(gb)
            dps_ref[:, g * POOL_GC:(g + 1) * POOL_GC] += jnp.sum(dypool * y_raw, axis=0, keepdims=True)
            o_ref[:, cg:cg + POOL_GC] = (dyb * (y_raw * sc) * _dsilu(gb)).astype(BF16)
            dyraw = dypool * sc
            dyraw_n = jnp.where(last, 0.0, dmn_ref[:, cm:cm + POOL_GC] * _silu(next_ref[:, cg:cg + POOL_GC]) * sc)
            dpw_ref[g * POOL_GC:(g + 1) * POOL_GC, :] += lax.dot_general(pooled, dyraw.astype(BF16), TN,
                                                                         preferred_element_type=F32)
            dyraw_ext = jnp.concatenate([dyraw, dyraw_n], axis=0).astype(BF16)
            dpooled = lax.dot_general(dyraw_ext, pw, NT, preferred_element_type=F32)
            w = 2 << g
            tt = t0 + lax.broadcasted_iota(jnp.int32, (tq + POOL_HALO, 1), 0)
            s = dpooled / jnp.minimum(tt + 1, w).astype(F32)
            for step in range(g + 1):
                s = s + pltpu.roll(s, tq + POOL_HALO - (1 << step), 0)
            o_ref[:, cu:cu + POOL_GC] = (s[0:tq] - dpooled[0:tq]).astype(BF16)

    rev = lambda i: nt - 1 - i
    return pl.pallas_call(
        body, name="mix0_bwd", grid=(nt,),
        out_shape=[jax.ShapeDtypeStruct((t, EVEN_IN), BF16), jax.ShapeDtypeStruct((8, 128), F32),
                   jax.ShapeDtypeStruct((POOL_GROUPS * POOL_GC, POOL_GC), F32), jax.ShapeDtypeStruct((1, 512), F32)],
        in_specs=[pl.BlockSpec((tq, EVEN_IN), lambda i: (rev(i), 0)),
                  pl.BlockSpec((BLOCK, EVEN_IN), lambda i: (jnp.maximum(rev(i) * nblk - 1, 0), 0)),
                  pl.BlockSpec((POOL_HALO, EVEN_IN),
                               lambda i: (jnp.minimum((rev(i) + 1) * (tq // POOL_HALO), t // POOL_HALO - 1), 0)),
                  pl.BlockSpec((tq, D_MODEL), lambda i: (rev(i), 0)),
                  pl.BlockSpec((POOL_HALO, D_MODEL),
                               lambda i: (jnp.minimum((rev(i) + 1) * (tq // POOL_HALO), t // POOL_HALO - 1), 0)),
                  pl.BlockSpec(memory_space=pltpu.SMEM),
                  _const_spec((POOL_GROUPS, POOL_GC, POOL_GC)), _const_spec((1, 512))],
        out_specs=[pl.BlockSpec((tq, EVEN_IN), lambda i: (rev(i), 0)),
                   pl.BlockSpec((8, 128), lambda i: (0, 0)),
                   pl.BlockSpec((POOL_GROUPS * POOL_GC, POOL_GC), lambda i: (0, 0)),
                   pl.BlockSpec((1, 512), lambda i: (0, 0))],
        scratch_shapes=[pltpu.VMEM((tq + BLOCK, 256), F32), pltpu.VMEM((tq + BLOCK, 256), F32),
                        pltpu.VMEM((BLOCK, 256), F32)],
        compiler_params=_params("arbitrary"),
    )(proj, proj, proj, dmix, dmix, sinks, pool_w, pool_scale)


CONV_RC = 32
CONV_CC = 256


def _fill_shifted(s_ref, rows):
    for b in range(1, 8):
        s_ref[b, 0:rows - 8, :] = s_ref[0, b:b + rows - 8, :]


def _conv_taps(s_ref, w_ref, r, c, lead, reverse):
    acc = jnp.zeros((CONV_RC, CONV_CC), F32)
    for k in range(CONV_K):
        a, b = divmod(lead + k, 8)
        kw = CONV_K - 1 - k if reverse else k
        acc = acc + s_ref[b, pl.ds(r + 8 * a, CONV_RC), c * CONV_CC:(c + 1) * CONV_CC] * w_ref[kw:kw + 1, c * CONV_CC:(c + 1) * CONV_CC]
    return acc


def _layer_norm_fwd(cf, lng, lnb):
    mu = jnp.mean(cf, axis=-1, keepdims=True)
    xc = cf - mu
    rstd = lax.rsqrt(jnp.mean(xc * xc, axis=-1, keepdims=True) + EPS)
    chat = xc * rstd
    return chat, rstd, chat * lng + lnb


def _conv_fwd(proj, dw, dwb, lng, lnb, tt=256):
    t = proj.shape[0]
    lead = CONV_HALO - (CONV_K - 1)

    def body(main_ref, halo_ref, w_ref, b_ref, g_ref, lb_ref, o_ref, gs_ref, c_ref):
        i = pl.program_id(0)
        hv = halo_ref[...]
        gs_ref[0, 0:CONV_HALO, :] = jnp.where(i == 0, 0.0, hv[:, 0:1024] * _sigmoid(hv[:, 1024:2048]))
        gs_ref[0, CONV_HALO:CONV_HALO + tt, :] = main_ref[:, 0:1024] * _sigmoid(main_ref[:, 1024:2048])
        _fill_shifted(gs_ref, tt + CONV_HALO)

        for c in range(D_MODEL // CONV_CC):
            def chunk(j, carry):
                r = pl.multiple_of(j * CONV_RC, CONV_RC)
                acc = _conv_taps(gs_ref, w_ref, r, c, lead, False)
                c_ref[pl.ds(r, CONV_RC), c * CONV_CC:(c + 1) * CONV_CC] = acc + b_ref[:, c * CONV_CC:(c + 1) * CONV_CC]
                return carry
            lax.fori_loop(0, tt // CONV_RC, chunk, 0)

        _, _, cn = _layer_norm_fwd(c_ref[...], g_ref[...], lb_ref[...])
        o_ref[...] = (_silu(cn) * _silu(main_ref[:, 2048:3072])).astype(BF16)

    vec = _const_spec((1, D_MODEL))
    return pl.pallas_call(
        body, name="conv_fwd", grid=(t // tt,),
        out_shape=jax.ShapeDtypeStruct((t, D_MODEL), BF16),
        in_specs=[pl.BlockSpec((tt, 3 * D_MODEL), lambda i: (i, 0)),
                  pl.BlockSpec((CONV_HALO, 3 * D_MODEL), lambda i: (jnp.maximum(i * (tt // CONV_HALO) - 1, 0), 0)),
                  _const_spec((CONV_K, D_MODEL)), vec, vec, vec],
        out_specs=pl.BlockSpec((tt, D_MODEL), lambda i: (i, 0)),
        scratch_shapes=[pltpu.VMEM((8, tt + CONV_HALO, D_MODEL), F32), pltpu.VMEM((tt, D_MODEL), F32)],
        compiler_params=_params("parallel"),
    )(proj, proj, dw, dwb, lng, lnb)


def _conv_bwd(proj, dz, dw, dwb, lng, lnb, tt=256):
    t = proj.shape[0]
    nt = t // tt
    lead = CONV_HALO - (CONV_K - 1)
    te = tt + CONV_HALO

    def body(main_ref, prev_ref, next_ref, dz_ref, dzn_ref, w_ref, b_ref, g_ref, lb_ref,
             o_ref, ddw_ref, ddb_ref, dg_ref, dlb_ref, gs_ref, ds_ref, c_ref):
        i = pl.program_id(0)

        @pl.when(i == 0)
        def _():
            ddw_ref[...] = jnp.zeros_like(ddw_ref)
            ddb_ref[...] = jnp.zeros_like(ddb_ref)
            dg_ref[...] = jnp.zeros_like(dg_ref)
            dlb_ref[...] = jnp.zeros_like(dlb_ref)

        pv = prev_ref[...]
        nv = next_ref[...]
        gs_ref[0, 0:CONV_HALO, :] = jnp.where(i == 0, 0.0, pv[:, 0:1024] * _sigmoid(pv[:, 1024:2048]))
        gs_ref[0, CONV_HALO:CONV_HALO + tt, :] = main_ref[:, 0:1024] * _sigmoid(main_ref[:, 1024:2048])
        gs_ref[0, CONV_HALO + tt:, :] = nv[:, 0:1024] * _sigmoid(nv[:, 1024:2048])
        _fill_shifted(gs_ref, tt + 2 * CONV_HALO)

        for c in range(D_MODEL // CONV_CC):
            def chunk(j, carry):
                r = pl.multiple_of(j * CONV_RC, CONV_RC)
                acc = _conv_taps(gs_ref, w_ref, r, c, lead, False)
                c_ref[pl.ds(r, CONV_RC), c * CONV_CC:(c + 1) * CONV_CC] = acc + b_ref[:, c * CONV_CC:(c + 1) * CONV_CC]
                return carry
            lax.fori_loop(0, te // CONV_RC, chunk, 0)

        lng = g_ref[...]
        chat, rstd, cn = _layer_norm_fwd(c_ref[...], lng, lb_ref[...])
        gate = jnp.concatenate([main_ref[:, 2048:3072], nv[:, 2048:3072]], axis=0)
        dzv = jnp.concatenate([dz_ref[...], dzn_ref[...]], axis=0)
        rows = lax.broadcasted_iota(jnp.int32, (te, 1), 0)
        live = (rows < tt) | (i < nt - 1)
        own = (rows < tt).astype(F32)
        dzv = jnp.where(live, dzv, 0.0)
        o_ref[:, 2048:3072] = (dzv * _silu(cn) * _dsilu(gate))[0:tt].astype(BF16)
        dcn = dzv * _silu(gate) * _dsilu(cn)
        dg_ref[...] += jnp.sum(dcn * chat * own, axis=0, keepdims=True)
        dlb_ref[...] += jnp.sum(dcn * own, axis=0, keepdims=True)
        dchat = dcn * lng
        dcf = rstd * (dchat - jnp.mean(dchat, axis=-1, keepdims=True) - chat * jnp.mean(dchat * chat, axis=-1, keepdims=True))
        ddb_ref[...] += jnp.sum(dcf * own, axis=0, keepdims=True)
        ds_ref[0, 0:te, :] = dcf
        ds_ref[0, te:, :] = jnp.zeros((8, D_MODEL), F32)
        _fill_shifted(ds_ref, te + 8)

        for c in range(D_MODEL // CONV_CC):
            cols = slice(c * CONV_CC, (c + 1) * CONV_CC)

            def chunk(j, carry):
                r = pl.multiple_of(j * CONV_RC, CONV_RC)
                dglu = _conv_taps(ds_ref, w_ref, r, c, 0, True)
                a = main_ref[pl.ds(r, CONV_RC), cols]
                sb = _sigmoid(main_ref[pl.ds(r, CONV_RC), c * CONV_CC + 1024:(c + 1) * CONV_CC + 1024])
                o_ref[pl.ds(r, CONV_RC), cols] = (dglu * sb).astype(BF16)
                o_ref[pl.ds(r, CONV_RC), c * CONV_CC + 1024:(c + 1) * CONV_CC + 1024] = (dglu * a * sb * (1.0 - sb)).astype(BF16)
                return carry
            lax.fori_loop(0, tt // CONV_RC, chunk, 0)

            for k in range(CONV_K):
                a8, b8 = divmod(lead + k, 8)

                def tap(j, acc):
                    r = pl.multiple_of(j * CONV_RC, CONV_RC)
                    return acc + ds_ref[0, pl.ds(r, CONV_RC), cols] * gs_ref[b8, pl.ds(r + 8 * a8, CONV_RC), cols]
                acc = lax.fori_loop(0, tt // CONV_RC, tap, jnp.zeros((CONV_RC, CONV_CC), F32))
                ddw_ref[k:k + 1, cols] += jnp.sum(acc, axis=0, keepdims=True)

    vec = _const_spec((1, D_MODEL))
    vec_out = pl.BlockSpec((1, D_MODEL), lambda i: (0, 0))
    nxt = lambda i: (jnp.minimum((i + 1) * (tt // CONV_HALO), t // CONV_HALO - 1), 0)
    return pl.pallas_call(
        body, name="conv_bwd", grid=(nt,),
        out_shape=[jax.ShapeDtypeStruct((t, 3 * D_MODEL), BF16), jax.ShapeDtypeStruct((CONV_K, D_MODEL), F32),
                   jax.ShapeDtypeStruct((1, D_MODEL), F32), jax.ShapeDtypeStruct((1, D_MODEL), F32),
                   jax.ShapeDtypeStruct((1, D_MODEL), F32)],
        in_specs=[pl.BlockSpec((tt, 3 * D_MODEL), lambda i: (i, 0)),
                  pl.BlockSpec((CONV_HALO, 3 * D_MODEL), lambda i: (jnp.maximum(i * (tt // CONV_HALO) - 1, 0), 0)),
                  pl.BlockSpec((CONV_HALO, 3 * D_MODEL), nxt),
                  pl.BlockSpec((tt, D_MODEL), lambda i: (i, 0)),
                  pl.BlockSpec((CONV_HALO, D_MODEL), nxt),
                  _const_spec((CONV_K, D_MODEL)), vec, vec, vec],
        out_specs=[pl.BlockSpec((tt, 3 * D_MODEL), lambda i: (i, 0)),
                   pl.BlockSpec((CONV_K, D_MODEL), lambda i: (0, 0)), vec_out, vec_out, vec_out],
        scratch_shapes=[pltpu.VMEM((8, tt + 2 * CONV_HALO, D_MODEL), F32), pltpu.VMEM((8, te + 8, D_MODEL), F32),
                        pltpu.VMEM((te, D_MODEL), F32)],
        compiler_params=_params("arbitrary"),
    )(proj, proj, proj, dz, dz, dw, dwb, lng, lnb)


def _local_step(x, target, pre, post, wa_t, sinks, pool_w, pool_scale, w_ab, wc_t, dw, dwb, lng, lnb, w_c):
    pre0, pre1 = pre[0:1], pre[1:2]
    post0, post1 = post[0:1], post[1:2]
    proj0 = _norm_matmul(x, pre0, wa_t, "proj0_fwd")
    mix0 = _mix0_fwd(proj0, sinks, pool_w, pool_scale)
    y0, x1 = _out_norm_res(mix0, w_ab, x, post0, "out0_fwd")
    proj1 = _norm_matmul(x1, pre1, wc_t, "proj1_fwd")
    z1 = _conv_fwd(proj1, dw, dwb, lng, lnb)
    y1, g2, loss = _out_norm_res(z1, w_c, x1, post1, "out1_fwd", target=target)

    dz1, d_wc, d_post1 = _post_bwd(g2, y1, post1, w_c, z1, "out1_bwd")
    dproj1, d_dw, d_dwb, d_lng, d_lnb = _conv_bwd(proj1, dz1, dw, dwb, lng, lnb)
    g1, d_wct, d_pre1 = _pre_bwd(dproj1, wc_t, x1, pre1, g2, "proj1_bwd")
    dmix0, d_wab, d_post0 = _post_bwd(g1, y0, post0, w_ab, mix0, "out0_bwd")
    dproj0, d_sinks, d_pw, d_ps = _mix0_bwd(proj0, dmix0, sinks, pool_w, pool_scale)
    gx, d_wat, d_pre0 = _pre_bwd(dproj0, wa_t, x, pre0, g1, "proj0_bwd")
    grads = dict(pre=jnp.concatenate([d_pre0, d_pre1], axis=0), post=jnp.concatenate([d_post0, d_post1], axis=0),
                 wa_t=d_wat, sinks=d_sinks, pool_w=d_pw, pool_scale=d_ps, w_ab=d_wab, wc_t=d_wct,
                 dw=d_dw, dwb=d_dwb, lng=d_lng, lnb=d_lnb, w_c=d_wc)
    return loss[0, 0], gx, grads


def _place():
    x, y, c = lax.axis_index("x"), lax.axis_index("y"), lax.axis_index("c")
    chips = [(1 - x, y), (x, 1 - y), (1 - x, 1 - y)]
    return x, y, c, chips


ANY = pl.BlockSpec(memory_space=pl.ANY)


def _all_gather(blocks, name, halve):
    n = len(blocks)
    shapes = [((b.shape[0] // 2) if halve else b.shape[0], b.shape[1]) for b in blocks]

    def body(*refs):
        ins, outs = refs[:n], refs[n:2 * n]
        send_sems, recv_sems, local_sems = refs[2 * n:]
        x, y, c, chips = _place()
        me, sibling = (x, y, c), (x, y, 1 - c)

        def piece(t, px, py, pc):
            return outs[t].at[4 * px + 2 * py + pc]

        def own(t):
            return ins[t].at[pl.ds(c * shapes[t][0], shapes[t][0])] if halve else ins[t]

        def copy(t, k, block, to, src=None):
            return pltpu.make_async_remote_copy(
                src_ref=piece(t, *block) if src is None else src, dst_ref=piece(t, *block),
                send_sem=send_sems.at[7 * t + k], recv_sem=recv_sems.at[7 * t + k],
                device_id=to, device_id_type=MESH_ID)

        mine = [pltpu.make_async_copy(own(t), piece(t, *me), local_sems.at[t]) for t in range(n)]
        for cp in mine:
            cp.start()
        first = []
        for t in range(n):
            first.append(copy(t, 0, me, sibling, src=own(t)))
            first += [copy(t, 1 + j, me, (*chip, c), src=own(t)) for j, chip in enumerate(chips)]
        for cp in first:
            cp.start()
        passed = []
        for j, chip in enumerate(chips):
            for t in range(n):
                copy(t, 1 + j, (*chip, c), me).wait_recv()
                fwd = copy(t, 4 + j, (*chip, c), sibling)
                fwd.start()
                passed.append(fwd)
        for t in range(n):
            copy(t, 0, sibling, me).wait_recv()
            for j, chip in enumerate(chips):
                copy(t, 4 + j, (*chip, 1 - c), me).wait_recv()
        for cp in first + passed:
            cp.wait_send()
        for cp in mine:
            cp.wait()

    return pl.pallas_call(
        body, name=name,
        out_shape=[jax.ShapeDtypeStruct((N_DEV, r, cols), b.dtype) for (r, cols), b in zip(shapes, blocks)],
        in_specs=[ANY] * n, out_specs=[ANY] * n,
        scratch_shapes=[pltpu.SemaphoreType.DMA((7 * n,)), pltpu.SemaphoreType.DMA((7 * n,)),
                        pltpu.SemaphoreType.DMA((n,))],
    )(*blocks)


def _exchange_halves(grads, name):
    n = len(grads)

    def body(*refs):
        ins, outs = refs[:n], refs[n:2 * n]
        send_sems, recv_sems = refs[2 * n:]
        x, y, c, _ = _place()
        copies = []
        for t in range(n):
            for b in range(4):
                copies.append(pltpu.make_async_remote_copy(
                    src_ref=ins[t].at[2 * b + 1 - c], dst_ref=outs[t].at[b],
                    send_sem=send_sems.at[4 * t + b], recv_sem=recv_sems.at[4 * t + b],
                    device_id=(x, y, 1 - c), device_id_type=MESH_ID))
        for cp in copies:
            cp.start()
        for cp in copies:
            cp.wait_recv()
        for cp in copies:
            cp.wait_send()

    return pl.pallas_call(
        body, name=name,
        out_shape=[jax.ShapeDtypeStruct((4,) + g.shape[1:], g.dtype) for g in grads],
        in_specs=[ANY] * n, out_specs=[ANY] * n,
        scratch_shapes=[pltpu.SemaphoreType.DMA((4 * n,)), pltpu.SemaphoreType.DMA((4 * n,))],
    )(*grads)


def _pair_sum(g, a, core, name):
    r = rt = g.shape[1]

    def body(c_ref, g_ref, a_ref, o_ref):
        o_ref[...] = (g_ref[...] + a_ref[...]).astype(BF16)

    blk = (1, rt, D_MODEL)
    return pl.pallas_call(
        body, name=name,
        grid_spec=pltpu.PrefetchScalarGridSpec(
            num_scalar_prefetch=1, grid=(4, r // rt),
            in_specs=[pl.BlockSpec(blk, lambda b, j, c_ref: (2 * b + c_ref[0], j, 0)),
                      pl.BlockSpec(blk, lambda b, j, c_ref: (b, j, 0))],
            out_specs=pl.BlockSpec(blk, lambda b, j, c_ref: (b, j, 0))),
        out_shape=jax.ShapeDtypeStruct((4, r, D_MODEL), BF16),
        compiler_params=_params("parallel", "parallel"),
    )(core, g, a)


def _send_to_owners(sums, name):
    n = len(sums)

    def body(*refs):
        ins, outs = refs[:n], refs[n:2 * n]
        send_sems, recv_sems = refs[2 * n:]
        x, y, c, chips = _place()
        copies = []
        for t in range(n):
            for j, (cx, cy) in enumerate(chips):
                copies.append(pltpu.make_async_remote_copy(
                    src_ref=ins[t].at[2 * cx + cy], dst_ref=outs[t].at[j],
                    send_sem=send_sems.at[3 * t + j], recv_sem=recv_sems.at[3 * t + j],
                    device_id=(cx, cy, c), device_id_type=MESH_ID))
        for cp in copies:
            cp.start()
        for cp in copies:
            cp.wait_recv()
        for cp in copies:
            cp.wait_send()

    return pl.pallas_call(
        body, name=name,
        out_shape=[jax.ShapeDtypeStruct((3,) + s.shape[1:], s.dtype) for s in sums],
        in_specs=[ANY] * n, out_specs=[ANY] * n,
        scratch_shapes=[pltpu.SemaphoreType.DMA((3 * n,)), pltpu.SemaphoreType.DMA((3 * n,))],
    )(*sums)


def _final_sum(g, a, recv, place, name):
    r = rt = g.shape[1]

    def body(p_ref, g_ref, a_ref, r_ref, o_ref):
        acc = g_ref[0] + a_ref[0]
        for j in range(3):
            acc = acc + r_ref[j].astype(F32)
        o_ref[...] = acc

    return pl.pallas_call(
        body, name=name,
        grid_spec=pltpu.PrefetchScalarGridSpec(
            num_scalar_prefetch=1, grid=(r // rt,),
            in_specs=[pl.BlockSpec((1, rt, D_MODEL), lambda j, p_ref: (p_ref[0], j, 0)),
                      pl.BlockSpec((1, rt, D_MODEL), lambda j, p_ref: (p_ref[1], j, 0)),
                      pl.BlockSpec((3, rt, D_MODEL), lambda j, p_ref: (0, j, 0))],
            out_specs=pl.BlockSpec((rt, D_MODEL), lambda j, p_ref: (j, 0))),
        out_shape=jax.ShapeDtypeStruct((r, D_MODEL), F32),
        compiler_params=_params("parallel"),
    )(place, g, a, recv)


def _share_with_sibling(pieces, name):
    n = len(pieces)

    def body(*refs):
        ins, outs = refs[:n], refs[n:2 * n]
        send_sems, recv_sems, local_sems = refs[2 * n:]
        x, y, c, _ = _place()
        local = [pltpu.make_async_copy(ins[t], outs[t].at[c], local_sems.at[t]) for t in range(n)]
        remote = [pltpu.make_async_remote_copy(
            src_ref=ins[t], dst_ref=outs[t].at[c], send_sem=send_sems.at[t], recv_sem=recv_sems.at[t],
            device_id=(x, y, 1 - c), device_id_type=MESH_ID) for t in range(n)]
        for cp in local + remote:
            cp.start()
        for cp in remote:
            cp.wait_recv()
        for cp in remote:
            cp.wait_send()
        for cp in local:
            cp.wait()

    return pl.pallas_call(
        body, name=name,
        out_shape=[jax.ShapeDtypeStruct((2,) + p.shape, p.dtype) for p in pieces],
        in_specs=[ANY] * n, out_specs=[ANY] * n,
        scratch_shapes=[pltpu.SemaphoreType.DMA((n,)), pltpu.SemaphoreType.DMA((n,)), pltpu.SemaphoreType.DMA((n,))],
    )(*pieces)


def _reduce_scatter(grads):
    x, y, c, _ = _place()
    core = jnp.reshape(c, (1,)).astype(jnp.int32)
    block = 2 * x + y
    place = jnp.stack([2 * block + c, block]).astype(jnp.int32)
    g8 = [g.reshape(N_DEV, g.shape[0] // N_DEV, D_MODEL) for g in grads]
    other = _exchange_halves(g8, "rs_pair_exchange")
    sums = [_pair_sum(g, a, core, f"rs_pair_sum{t}") for t, (g, a) in enumerate(zip(g8, other))]
    recv = _send_to_owners(sums, "rs_chip_exchange")
    pieces = [_final_sum(g, a, r, place, f"rs_final_sum{t}") for t, (g, a, r) in enumerate(zip(g8, other, recv))]
    both = _share_with_sibling(pieces, "rs_share")
    return [b.reshape(2 * b.shape[1], D_MODEL) for b in both]


def _sum8(parts, name):
    r = parts.shape[1]

    def body(p_ref, o_ref):
        acc = p_ref[0]
        for k in range(1, N_DEV):
            acc = acc + p_ref[k]
        o_ref[...] = acc

    return pl.pallas_call(
        body, name=name, out_shape=jax.ShapeDtypeStruct((r, 128), F32),
        in_specs=[pl.BlockSpec(memory_space=pltpu.VMEM)], out_specs=pl.BlockSpec(memory_space=pltpu.VMEM),
    )(parts)


def _adamw(w, g, m, v, name):
    shape = w.shape
    cols = shape[-1]
    rows = w.size // cols
    rt = 256 if rows % 256 == 0 else rows

    def body(w_ref, g_ref, m_ref, v_ref, d_ref, nm_ref, nv_ref):
        gv = g_ref[...]
        mn = ADAM_B1 * m_ref[...] + (1.0 - ADAM_B1) * gv
        vn = ADAM_B2 * v_ref[...] + (1.0 - ADAM_B2) * (gv * gv)
        m_hat = mn / (1.0 - ADAM_B1 ** ADAM_STEP)
        v_hat = vn / (1.0 - ADAM_B2 ** ADAM_STEP)
        d_ref[...] = -ADAM_LR * (m_hat / (jnp.sqrt(v_hat) + ADAM_EPS) + ADAM_WD * w_ref[...])
        nm_ref[...] = mn
        nv_ref[...] = vn

    spec = pl.BlockSpec((rt, cols), lambda i: (i, 0))
    outs = pl.pallas_call(
        body, name=name, grid=(rows // rt,),
        out_shape=[jax.ShapeDtypeStruct((rows, cols), F32)] * 3,
        in_specs=[spec] * 4, out_specs=[spec] * 3,
        compiler_params=_params("parallel"),
    )(*[a.reshape(rows, cols) for a in (w, g, m, v)])
    return [o.reshape(shape) for o in outs]


SMALL_ROWS = 832


def _pack_small(g):
    parts = [g["pre"].reshape(16, 128), g["post"].reshape(16, 128), g["sinks"], g["pool_scale"].reshape(4, 128),
             g["pool_w"], g["dw"].reshape(248, 128), g["dwb"].reshape(8, 128), g["lng"].reshape(8, 128),
             g["lnb"].reshape(8, 128)]
    used = sum(p.shape[0] for p in parts)
    return jnp.concatenate(parts + [jnp.zeros((SMALL_ROWS - used, 128), F32)], axis=0)


def _unpack_small(s):
    out, r = {}, 0
    for key, rows, shape in (("pre", 16, (2, D_MODEL)), ("post", 16, (2, D_MODEL)), ("sinks", 8, (8, 128)),
                             ("pool_scale", 4, (1, 512)), ("pool_w", 512, (1, 4, 128, 128)),
                             ("dw", 248, (CONV_K, D_MODEL)), ("dwb", 8, (1, D_MODEL)), ("lng", 8, (1, D_MODEL)),
                             ("lnb", 8, (1, D_MODEL))):
        out[key] = s[r:r + rows].reshape(shape)
        r += rows
    return out


def kernel(x, pre_norm, post_norm, a_w_in, a_sinks, b_pool_w, b_pool_scale, ab_w_out, c_w_in, c_dw_w, c_dw_b, c_ln_g, c_ln_b, c_w_out, loss_target, m_pre_norm, m_post_norm, m_a_w_in, m_a_sinks, m_b_pool_w, m_b_pool_scale, m_ab_w_out, m_c_w_in, m_c_dw_w, m_c_dw_b, m_c_ln_g, m_c_ln_b, m_c_w_out, v_pre_norm, v_post_norm, v_a_w_in, v_a_sinks, v_b_pool_w, v_b_pool_scale, v_ab_w_out, v_c_w_in, v_c_dw_w, v_c_dw_b, v_c_ln_g, v_c_ln_b, v_c_w_out):
    ix, iy = lax.axis_index("x"), lax.axis_index("y")
    chip_cols = (2 * ix + iy) * 256

    vec_shard = jnp.concatenate([c_dw_w.reshape(CONV_K, 256), c_dw_b, c_ln_g, c_ln_b,
                                 jnp.zeros((48 - CONV_K - 3, 256), F32)], axis=0)
    mats = [a_w_in[0].T.astype(BF16), ab_w_out[0].astype(BF16), c_w_in[0].T.astype(BF16), c_w_out[0].astype(BF16)]
    wa_t, w_ab, wc_t, w_c, vecs = _all_gather(mats + [vec_shard], "weights_all_gather", halve=True)
    wa_t = wa_t.reshape(EVEN_IN, D_MODEL)
    w_ab = w_ab.reshape(D_MODEL, D_MODEL)
    wc_t = wc_t.reshape(3 * D_MODEL, D_MODEL)
    w_c = w_c.reshape(D_MODEL, D_MODEL)
    vecs = vecs.reshape(4, 48, 256).transpose(1, 0, 2).reshape(48, D_MODEL)
    dw, dwb, lng, lnb = vecs[0:CONV_K], vecs[CONV_K:CONV_K + 1], vecs[CONV_K + 1:CONV_K + 2], vecs[CONV_K + 2:CONV_K + 3]

    loss, gx, g = _local_step(x[0], loss_target[0], pre_norm, post_norm, wa_t, a_sinks, b_pool_w[0], b_pool_scale,
                              w_ab, wc_t, dw, dwb, lng, lnb, w_c)
    loss = lax.psum(loss, ("x", "y", "c"))

    g_wa_t, g_wab, g_wc_t, g_wc = _reduce_scatter([g["wa_t"], g["w_ab"], g["wc_t"], g["w_c"]])
    g_a_w_in = g_wa_t.T[None]
    g_c_w_in = g_wc_t.T[None]
    g_ab_w_out = g_wab[None]
    g_c_w_out = g_wc[None]

    small = _sum8(_all_gather([_pack_small(g)], "small_all_gather", halve=False)[0], "small_sum")
    s = _unpack_small(small)
    g_pre, g_post = s["pre"], s["post"]
    g_sinks = s["sinks"][:, 0].reshape(1, 8)
    g_pool_w, g_pool_scale = s["pool_w"], s["pool_scale"]
    g_dw = lax.dynamic_slice_in_dim(s["dw"], chip_cols, 256, axis=1).reshape(1, CONV_K, 1, 256)
    g_dwb = lax.dynamic_slice_in_dim(s["dwb"], chip_cols, 256, axis=1)
    g_lng = lax.dynamic_slice_in_dim(s["lng"], chip_cols, 256, axis=1)
    g_lnb = lax.dynamic_slice_in_dim(s["lnb"], chip_cols, 256, axis=1)

    grads = [g_pre, g_post, g_a_w_in, g_sinks, g_pool_w, g_pool_scale, g_ab_w_out, g_c_w_in, g_dw, g_dwb, g_lng, g_lnb,
             g_c_w_out]
    weights = [pre_norm, post_norm, a_w_in, a_sinks, b_pool_w, b_pool_scale, ab_w_out, c_w_in, c_dw_w, c_dw_b, c_ln_g,
               c_ln_b, c_w_out]
    moms = [m_pre_norm, m_post_norm, m_a_w_in, m_a_sinks, m_b_pool_w, m_b_pool_scale, m_ab_w_out, m_c_w_in, m_c_dw_w,
            m_c_dw_b, m_c_ln_g, m_c_ln_b, m_c_w_out]
    vars_ = [v_pre_norm, v_post_norm, v_a_w_in, v_a_sinks, v_b_pool_w, v_b_pool_scale, v_ab_w_out, v_c_w_in, v_c_dw_w,
             v_c_dw_b, v_c_ln_g, v_c_ln_b, v_c_w_out]
    deltas, new_m, new_v = [], [], []
    for k, (w, gr, m, v) in enumerate(zip(weights, grads, moms, vars_)):
        d, nm, nv = _adamw(w, gr, m, v, f"adamw{k}")
        deltas.append(d)
        new_m.append(nm)
        new_v.append(nv)
    return (loss, gx[None], *grads, *deltas, *new_m, *new_v)
```

```python
import functools

import jax
import jax.numpy as jnp
from jax import lax
from jax.experimental import pallas as pl
from jax.experimental.pallas import tpu as pltpu

F32 = jnp.float32
BF16 = jnp.bfloat16

D_MODEL = 1024
EPS = 1e-6
NEG = -1e30
HEAD_DIM = 64
GROUP = 4
KV_HEADS = 2
BLOCK = 128
EVEN_IN = 2304
COL_Q, COL_K, COL_GA, COL_U, COL_GB = 0, 512, 768, 1280, 1792
POOL_GROUPS = 4
POOL_GC = 128
POOL_HALO = 16
CONV_K = 31
CONV_HALO = 32
N_DEV = 8

ADAM_LR = 0.001
ADAM_B1 = 0.9
ADAM_B2 = 0.999
ADAM_EPS = 1e-08
ADAM_WD = 0.01
ADAM_STEP = 10

VMEM_LIMIT_BYTES = 56 * 1024 * 1024

NT = (((1,), (1,)), ((), ()))
TN = (((0,), (0,)), ((), ()))
MESH_ID = pl.DeviceIdType.MESH


def _params(*sem):
    return pltpu.CompilerParams(dimension_semantics=sem, vmem_limit_bytes=VMEM_LIMIT_BYTES)


def _const_spec(shape):
    nd = len(shape)
    return pl.BlockSpec(shape, lambda *_: (0,) * nd, pipeline_mode=pl.Buffered(1))


def _sigmoid(v):
    return 0.5 * jnp.tanh(0.5 * v) + 0.5


def _silu(v):
    return v * _sigmoid(v)


def _silu_and_grad(v):
    s = _sigmoid(v)
    return v * s, s * (1.0 + v * (1.0 - s))


def _norm_matmul(x, gain, wt, name, tm=512):
    t, n = x.shape[0], wt.shape[0]

    def body(x_ref, g_ref, wt_ref, o_ref):
        xv = x_ref[...]
        r = lax.rsqrt(jnp.mean(xv * xv, axis=-1, keepdims=True) + EPS)
        h = (xv * r * g_ref[...]).astype(BF16)
        o_ref[...] = lax.dot_general(h, wt_ref[...], NT, preferred_element_type=F32)

    return pl.pallas_call(
        body, name=name, grid=(t // tm,),
        out_shape=jax.ShapeDtypeStruct((t, n), F32),
        in_specs=[pl.BlockSpec((tm, D_MODEL), lambda i: (i, 0)), _const_spec((1, D_MODEL)), _const_spec((n, D_MODEL))],
        out_specs=pl.BlockSpec((tm, n), lambda i: (i, 0)),
        compiler_params=_params("parallel"),
    )(x, gain, wt)


def _out_norm_res(a, w, x_in, post, name, target=None, tm=512):
    t = a.shape[0]
    with_loss = target is not None

    def body(*refs):
        if with_loss:
            a_ref, w_ref, x_ref, p_ref, t_ref, y_ref, o_ref, l_ref = refs
        else:
            a_ref, w_ref, x_ref, p_ref, y_ref, o_ref = refs
        y = jnp.dot(a_ref[...], w_ref[...], preferred_element_type=F32)
        y_ref[...] = y
        ry = lax.rsqrt(jnp.mean(y * y, axis=-1, keepdims=True) + EPS)
        xo = x_ref[...] + (y * ry) * p_ref[...]
        if with_loss:
            d = xo - t_ref[...]
            o_ref[...] = d * (1.0 / D_MODEL)

            @pl.when(pl.program_id(0) == 0)
            def _():
                l_ref[...] = jnp.zeros_like(l_ref)

            l_ref[...] += 0.5 * jnp.sum(jnp.mean(d * d, axis=-1, keepdims=True))
        else:
            o_ref[...] = xo

    row = pl.BlockSpec((tm, D_MODEL), lambda i: (i, 0))
    in_specs = [row, _const_spec((D_MODEL, D_MODEL)), row, _const_spec((1, D_MODEL))]
    out_shape = [jax.ShapeDtypeStruct((t, D_MODEL), F32), jax.ShapeDtypeStruct((t, D_MODEL), F32)]
    out_specs = [row, row]
    args = [a, w, x_in, post]
    if with_loss:
        in_specs.append(row)
        args.append(target)
        out_shape.append(jax.ShapeDtypeStruct((8, 128), F32))
        out_specs.append(pl.BlockSpec((8, 128), lambda i: (0, 0)))
    return pl.pallas_call(
        body, name=name, grid=(t // tm,), out_shape=out_shape, in_specs=in_specs, out_specs=out_specs,
        compiler_params=_params("arbitrary"),
    )(*args)


def _post_bwd(g, y, post, w, a, name, tm=512):
    t = g.shape[0]

    def body(g_ref, y_ref, p_ref, w_ref, a_ref, da_ref, dw_ref, dp_ref):
        @pl.when(pl.program_id(0) == 0)
        def _():
            dw_ref[...] = jnp.zeros_like(dw_ref)
            dp_ref[...] = jnp.zeros_like(dp_ref)

        gv = g_ref[...]
        yv = y_ref[...]
        ry = lax.rsqrt(jnp.mean(yv * yv, axis=-1, keepdims=True) + EPS)
        nv = yv * ry
        dp_ref[...] += jnp.sum(gv * nv, axis=0, keepdims=True)
        dn = gv * p_ref[...]
        dy = (ry * (dn - nv * jnp.mean(dn * nv, axis=-1, keepdims=True))).astype(BF16)
        da_ref[...] = lax.dot_general(dy, w_ref[...], NT, preferred_element_type=F32)
        dw_ref[...] += lax.dot_general(a_ref[...], dy, TN, preferred_element_type=F32)

    row = pl.BlockSpec((tm, D_MODEL), lambda i: (i, 0))
    return pl.pallas_call(
        body, name=name, grid=(t // tm,),
        out_shape=[jax.ShapeDtypeStruct((t, D_MODEL), F32), jax.ShapeDtypeStruct((D_MODEL, D_MODEL), F32),
                   jax.ShapeDtypeStruct((1, D_MODEL), F32)],
        in_specs=[row, row, _const_spec((1, D_MODEL)), _const_spec((D_MODEL, D_MODEL)), row],
        out_specs=[row, pl.BlockSpec((D_MODEL, D_MODEL), lambda i: (0, 0)), pl.BlockSpec((1, D_MODEL), lambda i: (0, 0))],
        compiler_params=_params("arbitrary"),
    )(g, y, post, w, a)


def _pre_bwd(dproj, wt, x_in, pre, g, name, tm=256):
    t, n = dproj.shape

    def body(dp_ref, wt_ref, x_ref, pre_ref, g_ref, dx_ref, dwt_ref, dpre_ref):
        @pl.when(pl.program_id(0) == 0)
        def _():
            dwt_ref[...] = jnp.zeros_like(dwt_ref)
            dpre_ref[...] = jnp.zeros_like(dpre_ref)

        dpv = dp_ref[...]
        dh = jnp.dot(dpv, wt_ref[...], preferred_element_type=F32)
        xv = x_ref[...]
        r = lax.rsqrt(jnp.mean(xv * xv, axis=-1, keepdims=True) + EPS)
        xn = xv * r
        pv = pre_ref[...]
        dpre_ref[...] += jnp.sum(dh * xn, axis=0, keepdims=True)
        dxn = dh * pv
        dx_ref[...] = g_ref[...] + r * (dxn - xn * jnp.mean(dxn * xn, axis=-1, keepdims=True))
        h = (xn * pv).astype(BF16)
        dwt_ref[...] += lax.dot_general(dpv, h, TN, preferred_element_type=F32)

    row = pl.BlockSpec((tm, D_MODEL), lambda i: (i, 0))
    return pl.pallas_call(
        body, name=name, grid=(t // tm,),
        out_shape=[jax.ShapeDtypeStruct((t, D_MODEL), F32), jax.ShapeDtypeStruct((n, D_MODEL), F32),
                   jax.ShapeDtypeStruct((1, D_MODEL), F32)],
        in_specs=[pl.BlockSpec((tm, n), lambda i: (i, 0)), _const_spec((n, D_MODEL)), row, _const_spec((1, D_MODEL)), row],
        out_specs=[row, pl.BlockSpec((n, D_MODEL), lambda i: (0, 0)), pl.BlockSpec((1, D_MODEL), lambda i: (0, 0))],
        compiler_params=_params("arbitrary"),
    )(dproj, wt, x_in, pre, g)


def _group_masks():
    lane = lax.broadcasted_iota(jnp.int32, (1, GROUP * HEAD_DIM), 1)
    return [(lane // HEAD_DIM == g).astype(F32) for g in range(GROUP)]


def _stack_groups(v, masks):
    return jnp.concatenate([v * m for m in masks], axis=0)


def _unstack_groups(v, masks):
    out = v[0:BLOCK] * masks[0]
    for g in range(1, GROUP):
        out = out + v[g * BLOCK:(g + 1) * BLOCK] * masks[g]
    return out


def _repeat_head(kv2, kvh):
    first = lax.broadcasted_iota(jnp.int32, kv2.shape, 1) < HEAD_DIM
    rolled = pltpu.roll(kv2, HEAD_DIM, 1)
    one = jnp.where(first, kv2, rolled) if kvh == 0 else jnp.where(first, rolled, kv2)
    return jnp.concatenate([one, one], axis=1)


def _fold_head(v4):
    a = v4[:, 0:128] + v4[:, 128:256]
    return a + pltpu.roll(a, HEAD_DIM, 1)


def _row_consts(kvh, sink_ref):
    rb = lax.broadcasted_iota(jnp.int32, (GROUP * BLOCK, 1), 0) // BLOCK
    slope = jnp.zeros((GROUP * BLOCK, 1), F32)
    sink = jnp.zeros((GROUP * BLOCK, 1), F32)
    for g in range(GROUP):
        h = kvh * GROUP + g
        slope = jnp.where(rb == g, 2.0 ** (-(h + 1)), slope)
        sink = jnp.where(rb == g, sink_ref[0, h], sink)
    return slope, sink


def _attn_probs(qk, k4, kvh, sink_ref, dist, valid, masks):
    qs = _stack_groups(qk, masks).astype(BF16)
    slope, sink = _row_consts(kvh, sink_ref)
    s = lax.dot_general(qs, k4, NT, preferred_element_type=F32) * (HEAD_DIM ** -0.5)
    s = jnp.where(valid, s - slope * dist, NEG)
    mx = jnp.maximum(jnp.max(s, axis=-1, keepdims=True), sink)
    e = jnp.exp(s - mx)
    es = jnp.exp(sink - mx)
    inv = 1.0 / (jnp.sum(e, axis=-1, keepdims=True) + es)
    return qs, e * inv, es * inv


def _dist_valid(key_pos0):
    row = lax.broadcasted_iota(jnp.int32, (GROUP * BLOCK, 2 * BLOCK), 0)
    col = lax.broadcasted_iota(jnp.int32, (GROUP * BLOCK, 2 * BLOCK), 1)
    dist = (row % BLOCK) + BLOCK - col
    valid = (dist >= 0) & (dist < BLOCK) & (col + key_pos0 >= 0)
    return dist.astype(F32), valid


def _pool_forward(u_ext, g, t0):
    n = u_ext.shape[0] - POOL_HALO
    s = u_ext
    for step in range(g + 1):
        s = s + pltpu.roll(s, 1 << step, 0)
    w = 2 << g
    t = t0 + lax.broadcasted_iota(jnp.int32, (n, 1), 0)
    cnt = jnp.minimum(t + 1, w).astype(F32)
    return s[POOL_HALO:] / cnt - u_ext[POOL_HALO:]


def _mix0_fwd(proj, sinks, pool_w, pool_scale, tq=512):
    t = proj.shape[0]
    nblk = tq // BLOCK

    def body(main_ref, halo_ref, sink_ref, pw_ref, ps_ref, o_ref, kv_ref):
        i = pl.program_id(0)
        t0 = i * tq
        masks = _group_masks()
        kv_ref[0:BLOCK, :] = halo_ref[:, COL_K:COL_K + 256]
        kv_ref[BLOCK:, :] = main_ref[:, COL_K:COL_K + 256]

        def block(jb, carry):
            r0 = pl.multiple_of(jb * BLOCK, BLOCK)
            dist, valid = _dist_valid(t0 + r0 - BLOCK)
            q = main_ref[pl.ds(r0, BLOCK), COL_Q:COL_Q + 512]
            ga = main_ref[pl.ds(r0, BLOCK), COL_GA:COL_GA + 512]
            kk = kv_ref[pl.ds(r0, 2 * BLOCK), 0:128]
            vv = kv_ref[pl.ds(r0, 2 * BLOCK), 128:256]
            outs = []
            for kvh in range(KV_HEADS):
                k4 = _repeat_head(kk, kvh).astype(BF16)
                v4 = _repeat_head(vv, kvh).astype(BF16)
                _, p, _ = _attn_probs(q[:, kvh * 256:(kvh + 1) * 256], k4, kvh, sink_ref, dist, valid, masks)
                pv = jnp.dot(p.astype(BF16), v4, preferred_element_type=F32)
                outs.append(_unstack_groups(pv, masks))
            attn = jnp.concatenate(outs, axis=1)
            o_ref[pl.ds(r0, BLOCK), 0:512] = (attn * _silu(ga)).astype(BF16)
            return carry

        lax.fori_loop(0, nblk, block, 0)

        for g in range(POOL_GROUPS):
            cu = COL_U + g * POOL_GC
            cg = COL_GB + g * POOL_GC
            halo_u = jnp.where(i == 0, 0.0, halo_ref[BLOCK - POOL_HALO:BLOCK, cu:cu + POOL_GC])
            u_ext = jnp.concatenate([halo_u, main_ref[:, cu:cu + POOL_GC]], axis=0)
            pooled = _pool_forward(u_ext, g, t0)
            y = jnp.dot(pooled.astype(BF16), pw_ref[g].astype(BF16), preferred_element_type=F32)
            y = y * ps_ref[:, g * POOL_GC:(g + 1) * POOL_GC]
            o_ref[:, 512 + g * POOL_GC:512 + (g + 1) * POOL_GC] = (y * _silu(main_ref[:, cg:cg + POOL_GC])).astype(BF16)

    return pl.pallas_call(
        body, name="mix0_fwd", grid=(t // tq,),
        out_shape=jax.ShapeDtypeStruct((t, D_MODEL), BF16),
        in_specs=[pl.BlockSpec((tq, EVEN_IN), lambda i: (i, 0)),
                  pl.BlockSpec((BLOCK, EVEN_IN), lambda i: (jnp.maximum(i * nblk - 1, 0), 0)),
                  pl.BlockSpec(memory_space=pltpu.SMEM),
                  _const_spec((POOL_GROUPS, POOL_GC, POOL_GC)), _const_spec((1, 512))],
        out_specs=pl.BlockSpec((tq, D_MODEL), lambda i: (i, 0)),
        scratch_shapes=[pltpu.VMEM((tq + BLOCK, 256), F32)],
        compiler_params=_params("parallel"),
    )(proj, proj, sinks, pool_w, pool_scale)


def _mix0_bwd(proj, dmix, sinks, pool_w, pool_scale, tq=512):
    t = proj.shape[0]
    nt = t // tq
    nblk = tq // BLOCK

    def body(main_ref, halo_ref, next_ref, dm_ref, dmn_ref, sink_ref, pw_ref, ps_ref,
             o_ref, dsk_ref, dpw_ref, dps_ref, kv_ref, dkv_ref, carry_ref):
        i = pl.program_id(0)
        ii = nt - 1 - i
        t0 = ii * tq
        masks = _group_masks()

        @pl.when(i == 0)
        def _():
            dsk_ref[...] = jnp.zeros_like(dsk_ref)
            dpw_ref[...] = jnp.zeros_like(dpw_ref)
            dps_ref[...] = jnp.zeros_like(dps_ref)
            carry_ref[...] = jnp.zeros_like(carry_ref)

        kv_ref[0:BLOCK, :] = halo_ref[:, COL_K:COL_K + 256]
        kv_ref[BLOCK:, :] = main_ref[:, COL_K:COL_K + 256]
        dkv_ref[0:tq, :] = jnp.zeros((tq, 256), F32)
        dkv_ref[tq:, :] = carry_ref[...]

        def block(jb, carry):
            r0 = pl.multiple_of(jb * BLOCK, BLOCK)
            dist, valid = _dist_valid(t0 + r0 - BLOCK)
            q = main_ref[pl.ds(r0, BLOCK), COL_Q:COL_Q + 512]
            ga = main_ref[pl.ds(r0, BLOCK), COL_GA:COL_GA + 512]
            dya = dm_ref[pl.ds(r0, BLOCK), 0:512]
            kk = kv_ref[pl.ds(r0, 2 * BLOCK), 0:128]
            vv = kv_ref[pl.ds(r0, 2 * BLOCK), 128:256]
            silu_ga, dsilu_ga = _silu_and_grad(ga)
            do = dya * silu_ga
            first = lax.broadcasted_iota(jnp.int32, (2 * BLOCK, 128), 1) < HEAD_DIM
            attn, dq, dk, dv = [], [], [], []
            for kvh in range(KV_HEADS):
                k4 = _repeat_head(kk, kvh).astype(BF16)
                v4 = _repeat_head(vv, kvh).astype(BF16)
                qs, p, ps = _attn_probs(q[:, kvh * 256:(kvh + 1) * 256], k4, kvh, sink_ref, dist, valid, masks)
                pb = p.astype(BF16)
                o_k = _unstack_groups(jnp.dot(pb, v4, preferred_element_type=F32), masks)
                do_k = do[:, kvh * 256:(kvh + 1) * 256]
                dos = _stack_groups(do_k, masks).astype(BF16)
                prod = do_k * o_k
                delta = jnp.concatenate([jnp.sum(prod * m, axis=-1, keepdims=True) for m in masks], axis=0)
                dp = lax.dot_general(dos, v4, NT, preferred_element_type=F32)
                ds = (p * (dp - delta)).astype(BF16)
                sink_term = ps * delta
                for g in range(GROUP):
                    h = kvh * GROUP + g
                    dsk_ref[h:h + 1, :] -= jnp.sum(sink_term[g * BLOCK:(g + 1) * BLOCK], keepdims=True)
                scale = HEAD_DIM ** -0.5
                dq.append(_unstack_groups(jnp.dot(ds, k4, preferred_element_type=F32), masks) * scale)
                dk.append(_fold_head(lax.dot_general(ds, qs, TN, preferred_element_type=F32)) * scale)
                dv.append(_fold_head(lax.dot_general(pb, dos, TN, preferred_element_type=F32)))
                attn.append(o_k)
            o_ref[pl.ds(r0, BLOCK), COL_Q:COL_Q + 512] = jnp.concatenate(dq, axis=1).astype(BF16)
            o_all = jnp.concatenate(attn, axis=1)
            o_ref[pl.ds(r0, BLOCK), COL_GA:COL_GA + 512] = (dya * o_all * dsilu_ga).astype(BF16)
            dkv = jnp.concatenate([jnp.where(first, dk[0], dk[1]), jnp.where(first, dv[0], dv[1])], axis=1)
            dkv_ref[pl.ds(r0, 2 * BLOCK), :] += dkv
            return carry

        lax.fori_loop(0, nblk, block, 0)
        carry_ref[...] = dkv_ref[0:BLOCK, :]
        o_ref[:, COL_K:COL_K + 256] = dkv_ref[BLOCK:, :].astype(BF16)

        last = ii == nt - 1
        for g in range(POOL_GROUPS):
            cu = COL_U + g * POOL_GC
            cg = COL_GB + g * POOL_GC
            cm = 512 + g * POOL_GC
            pw = pw_ref[g].astype(BF16)
            sc = ps_ref[:, g * POOL_GC:(g + 1) * POOL_GC]
            halo_u = jnp.where(ii == 0, 0.0, halo_ref[BLOCK - POOL_HALO:BLOCK, cu:cu + POOL_GC])
            u_ext = jnp.concatenate([halo_u, main_ref[:, cu:cu + POOL_GC]], axis=0)
            pooled = _pool_forward(u_ext, g, t0).astype(BF16)
            y_raw = jnp.dot(pooled, pw, preferred_element_type=F32)
            gb = main_ref[:, cg:cg + POOL_GC]
            dyb = dm_ref[:, cm:cm + POOL_GC]
            silu_gb, dsilu_gb = _silu_and_grad(gb)
            dypool = dyb * silu_gb
            dps_ref[:, g * POOL_GC:(g + 1) * POOL_GC] += jnp.sum(dypool * y_raw, axis=0, keepdims=True)
            o_ref[:, cg:cg + POOL_GC] = (dyb * (y_raw * sc) * dsilu_gb).astype(BF16)
            dyraw = dypool * sc
            dyraw_n = jnp.where(last, 0.0, dmn_ref[:, cm:cm + POOL_GC] * _silu(next_ref[:, cg:cg + POOL_GC]) * sc)
            dpw_ref[g * POOL_GC:(g + 1) * POOL_GC, :] += lax.dot_general(pooled, dyraw.astype(BF16), TN,
                                                                         preferred_element_type=F32)
            dyraw_ext = jnp.concatenate([dyraw, dyraw_n], axis=0).astype(BF16)
            dpooled = lax.dot_general(dyraw_ext, pw, NT, preferred_element_type=F32)
            w = 2 << g
            tt = t0 + lax.broadcasted_iota(jnp.int32, (tq + POOL_HALO, 1), 0)
            s = dpooled / jnp.minimum(tt + 1, w).astype(F32)
            for step in range(g + 1):
                s = s + pltpu.roll(s, tq + POOL_HALO - (1 << step), 0)
            o_ref[:, cu:cu + POOL_GC] = (s[0:tq] - dpooled[0:tq]).astype(BF16)

    rev = lambda i: nt - 1 - i
    return pl.pallas_call(
        body, name="mix0_bwd", grid=(nt,),
        out_shape=[jax.ShapeDtypeStruct((t, EVEN_IN), BF16), jax.ShapeDtypeStruct((8, 128), F32),
                   jax.ShapeDtypeStruct((POOL_GROUPS * POOL_GC, POOL_GC), F32), jax.ShapeDtypeStruct((1, 512), F32)],
        in_specs=[pl.BlockSpec((tq, EVEN_IN), lambda i: (rev(i), 0)),
                  pl.BlockSpec((BLOCK, EVEN_IN), lambda i: (jnp.maximum(rev(i) * nblk - 1, 0), 0)),
                  pl.BlockSpec((POOL_HALO, EVEN_IN),
                               lambda i: (jnp.minimum((rev(i) + 1) * (tq // POOL_HALO), t // POOL_HALO - 1), 0)),
                  pl.BlockSpec((tq, D_MODEL), lambda i: (rev(i), 0)),
                  pl.BlockSpec((POOL_HALO, D_MODEL),
                               lambda i: (jnp.minimum((rev(i) + 1) * (tq // POOL_HALO), t // POOL_HALO - 1), 0)),
                  pl.BlockSpec(memory_space=pltpu.SMEM),
                  _const_spec((POOL_GROUPS, POOL_GC, POOL_GC)), _const_spec((1, 512))],
        out_specs=[pl.BlockSpec((tq, EVEN_IN), lambda i: (rev(i), 0)),
                   pl.BlockSpec((8, 128), lambda i: (0, 0)),
                   pl.BlockSpec((POOL_GROUPS * POOL_GC, POOL_GC), lambda i: (0, 0)),
                   pl.BlockSpec((1, 512), lambda i: (0, 0))],
        scratch_shapes=[pltpu.VMEM((tq + BLOCK, 256), F32), pltpu.VMEM((tq + BLOCK, 256), F32),
                        pltpu.VMEM((BLOCK, 256), F32)],
        compiler_params=_params("arbitrary"),
    )(proj, proj, proj, dmix, dmix, sinks, pool_w, pool_scale)


CONV_RC = 32
CONV_CC = 128
CONV_CHAINS = 4
CONV_UNROLL = 2


def _fill_shifted(s_ref, rows):
    for b in range(1, 8):
        s_ref[b, 0:rows - 8, :] = s_ref[0, b:b + rows - 8, :]


def _tap_blocks(s_ref, r, cols, lead):
    for b in range(8):
        taps = [(a, 8 * a + b - lead) for a in range(5) if 0 <= 8 * a + b - lead < CONV_K]
        span = 8 * max(a for a, _ in taps) + CONV_RC
        blk = s_ref[b, pl.ds(r, span), cols]
        for a, k in taps:
            yield k, blk[8 * a:8 * a + CONV_RC]


def _conv_taps(s_ref, w_ref, r, cols, lead, reverse):
    accs = [None] * CONV_CHAINS
    for n, (k, blk) in enumerate(_tap_blocks(s_ref, r, cols, lead)):
        kw = CONV_K - 1 - k if reverse else k
        term = blk * w_ref[kw:kw + 1, cols]
        accs[n % CONV_CHAINS] = term if accs[n % CONV_CHAINS] is None else accs[n % CONV_CHAINS] + term
    return (accs[0] + accs[1]) + (accs[2] + accs[3])


def _layer_norm_fwd(cf, lng, lnb):
    mu = jnp.mean(cf, axis=-1, keepdims=True)
    xc = cf - mu
    rstd = lax.rsqrt(jnp.mean(xc * xc, axis=-1, keepdims=True) + EPS)
    chat = xc * rstd
    return chat, rstd, chat * lng + lnb


def _conv_fwd(proj, dw, dwb, lng, lnb, tt=256):
    t = proj.shape[0]
    lead = CONV_HALO - (CONV_K - 1)

    def body(main_ref, halo_ref, w_ref, b_ref, g_ref, lb_ref, o_ref, c_ref, gs_ref):
        i = pl.program_id(0)
        hv = halo_ref[...]
        gs_ref[0, 0:CONV_HALO, :] = jnp.where(i == 0, 0.0, hv[:, 0:1024] * _sigmoid(hv[:, 1024:2048]))
        gs_ref[0, CONV_HALO:CONV_HALO + tt, :] = main_ref[:, 0:1024] * _sigmoid(main_ref[:, 1024:2048])
        _fill_shifted(gs_ref, tt + CONV_HALO)

        for c in range(D_MODEL // CONV_CC):
            cols = slice(c * CONV_CC, (c + 1) * CONV_CC)

            def chunk(j, carry):
                r = pl.multiple_of(j * CONV_RC, CONV_RC)
                c_ref[pl.ds(r, CONV_RC), cols] = _conv_taps(gs_ref, w_ref, r, cols, lead, False) + b_ref[:, cols]
                return carry
            lax.fori_loop(0, tt // CONV_RC, chunk, 0, unroll=CONV_UNROLL)

        _, _, cn = _layer_norm_fwd(c_ref[...], g_ref[...], lb_ref[...])
        o_ref[...] = (_silu(cn) * _silu(main_ref[:, 2048:3072])).astype(BF16)

    vec = _const_spec((1, D_MODEL))
    row = pl.BlockSpec((tt, D_MODEL), lambda i: (i, 0))
    return pl.pallas_call(
        body, name="conv_fwd", grid=(t // tt,),
        out_shape=[jax.ShapeDtypeStruct((t, D_MODEL), BF16), jax.ShapeDtypeStruct((t, D_MODEL), F32)],
        in_specs=[pl.BlockSpec((tt, 3 * D_MODEL), lambda i: (i, 0)),
                  pl.BlockSpec((CONV_HALO, 3 * D_MODEL), lambda i: (jnp.maximum(i * (tt // CONV_HALO) - 1, 0), 0)),
                  _const_spec((CONV_K, D_MODEL)), vec, vec, vec],
        out_specs=[row, row],
        scratch_shapes=[pltpu.VMEM((8, tt + CONV_HALO, D_MODEL), F32)],
        compiler_params=_params("parallel"),
    )(proj, proj, dw, dwb, lng, lnb)


def _conv_bwd(proj, cf, dz, dw, lng, lnb, tt=256):
    t = proj.shape[0]
    nt = t // tt
    te = tt + CONV_HALO

    def body(main_ref, next_ref, cf_ref, cfn_ref, dz_ref, dzn_ref, w_ref, g_ref, lb_ref,
             o_ref, ddw_ref, ddb_ref, dg_ref, dlb_ref, ds_ref, glu_ref, sb_ref):
        i = pl.program_id(0)

        @pl.when(i == 0)
        def _():
            ddw_ref[...] = jnp.zeros_like(ddw_ref)
            ddb_ref[...] = jnp.zeros_like(ddb_ref)
            dg_ref[...] = jnp.zeros_like(dg_ref)
            dlb_ref[...] = jnp.zeros_like(dlb_ref)

        lng = g_ref[...]
        chat, rstd, cn = _layer_norm_fwd(jnp.concatenate([cf_ref[...], cfn_ref[...]], axis=0), lng, lb_ref[...])
        gate = jnp.concatenate([main_ref[:, 2048:3072], next_ref[:, 2048:3072]], axis=0)
        dzv = jnp.concatenate([dz_ref[...], dzn_ref[...]], axis=0)
        rows = lax.broadcasted_iota(jnp.int32, (te, 1), 0)
        live = (rows < tt) | (i < nt - 1)
        own = (rows < tt).astype(F32)
        dzv = jnp.where(live, dzv, 0.0)
        silu_cn, dsilu_cn = _silu_and_grad(cn)
        silu_gate, dsilu_gate = _silu_and_grad(gate)
        o_ref[:, 2048:3072] = (dzv * silu_cn * dsilu_gate)[0:tt].astype(BF16)
        dcn = dzv * silu_gate * dsilu_cn
        dg_ref[...] += jnp.sum(dcn * chat * own, axis=0, keepdims=True)
        dlb_ref[...] += jnp.sum(dcn * own, axis=0, keepdims=True)
        dchat = dcn * lng
        dcf = rstd * (dchat - jnp.mean(dchat, axis=-1, keepdims=True) - chat * jnp.mean(dchat * chat, axis=-1, keepdims=True))
        ddb_ref[...] += jnp.sum(dcf * own, axis=0, keepdims=True)
        ds_ref[0, 0:te, :] = dcf
        ds_ref[0, te:, :] = jnp.zeros((8, D_MODEL), F32)
        _fill_shifted(ds_ref, te + 8)
        sb_ref[...] = _sigmoid(main_ref[:, 1024:2048])
        glu_ref[...] = main_ref[:, 0:1024] * sb_ref[...]

        for c in range(D_MODEL // CONV_CC):
            cols = slice(c * CONV_CC, (c + 1) * CONV_CC)
            gcols = slice(c * CONV_CC + 1024, (c + 1) * CONV_CC + 1024)

            def chunk(j, carry):
                r = pl.multiple_of(j * CONV_RC, CONV_RC)
                dglu = _conv_taps(ds_ref, w_ref, r, cols, 0, True)
                sb = sb_ref[pl.ds(r, CONV_RC), cols]
                o_ref[pl.ds(r, CONV_RC), cols] = (dglu * sb).astype(BF16)
                o_ref[pl.ds(r, CONV_RC), gcols] = (dglu * glu_ref[pl.ds(r, CONV_RC), cols] * (1.0 - sb)).astype(BF16)
                return carry
            lax.fori_loop(0, tt // CONV_RC, chunk, 0, unroll=CONV_UNROLL)

            def taps(j, accs):
                r = pl.multiple_of(j * CONV_RC, CONV_RC)
                gl = glu_ref[pl.ds(r, CONV_RC), cols]
                new = list(accs)
                for m, blk in _tap_blocks(ds_ref, r, cols, 0):
                    prod = blk * gl
                    part = prod[0:8]
                    for q in range(1, CONV_RC // 8):
                        part = part + prod[8 * q:8 * q + 8]
                    new[m] = new[m] + part
                return tuple(new)
            accs = lax.fori_loop(0, tt // CONV_RC, taps, tuple(jnp.zeros((8, CONV_CC), F32) for _ in range(CONV_K)))
            for m in range(CONV_K):
                k = CONV_K - 1 - m
                ddw_ref[k:k + 1, cols] += jnp.sum(accs[m], axis=0, keepdims=True)

    vec = _const_spec((1, D_MODEL))
    vec_out = pl.BlockSpec((1, D_MODEL), lambda i: (0, 0))
    row = pl.BlockSpec((tt, D_MODEL), lambda i: (i, 0))
    nxt = lambda i: (jnp.minimum((i + 1) * (tt // CONV_HALO), t // CONV_HALO - 1), 0)
    nxt_row = pl.BlockSpec((CONV_HALO, D_MODEL), nxt)
    return pl.pallas_call(
        body, name="conv_bwd", grid=(nt,),
        out_shape=[jax.ShapeDtypeStruct((t, 3 * D_MODEL), BF16), jax.ShapeDtypeStruct((CONV_K, D_MODEL), F32),
                   jax.ShapeDtypeStruct((1, D_MODEL), F32), jax.ShapeDtypeStruct((1, D_MODEL), F32),
                   jax.ShapeDtypeStruct((1, D_MODEL), F32)],
        in_specs=[pl.BlockSpec((tt, 3 * D_MODEL), lambda i: (i, 0)),
                  pl.BlockSpec((CONV_HALO, 3 * D_MODEL), nxt),
                  row, nxt_row, row, nxt_row,
                  _const_spec((CONV_K, D_MODEL)), vec, vec],
        out_specs=[pl.BlockSpec((tt, 3 * D_MODEL), lambda i: (i, 0)),
                   pl.BlockSpec((CONV_K, D_MODEL), lambda i: (0, 0)), vec_out, vec_out, vec_out],
        scratch_shapes=[pltpu.VMEM((8, te + 8, D_MODEL), F32), pltpu.VMEM((tt, D_MODEL), F32),
                        pltpu.VMEM((tt, D_MODEL), F32)],
        compiler_params=_params("arbitrary"),
    )(proj, proj, cf, cf, dz, dz, dw, lng, lnb)


def _local_step(x, target, pre, post, wa_t, sinks, pool_w, pool_scale, w_ab, wc_t, dw, dwb, lng, lnb, w_c):
    pre0, pre1 = pre[0:1], pre[1:2]
    post0, post1 = post[0:1], post[1:2]
    proj0 = _norm_matmul(x, pre0, wa_t, "proj0_fwd")
    mix0 = _mix0_fwd(proj0, sinks, pool_w, pool_scale)
    y0, x1 = _out_norm_res(mix0, w_ab, x, post0, "out0_fwd")
    proj1 = _norm_matmul(x1, pre1, wc_t, "proj1_fwd")
    z1, cf1 = _conv_fwd(proj1, dw, dwb, lng, lnb)
    y1, g2, loss = _out_norm_res(z1, w_c, x1, post1, "out1_fwd", target=target)

    dz1, d_wc, d_post1 = _post_bwd(g2, y1, post1, w_c, z1, "out1_bwd")
    dproj1, d_dw, d_dwb, d_lng, d_lnb = _conv_bwd(proj1, cf1, dz1, dw, lng, lnb)
    g1, d_wct, d_pre1 = _pre_bwd(dproj1, wc_t, x1, pre1, g2, "proj1_bwd")
    dmix0, d_wab, d_post0 = _post_bwd(g1, y0, post0, w_ab, mix0, "out0_bwd")
    dproj0, d_sinks, d_pw, d_ps = _mix0_bwd(proj0, dmix0, sinks, pool_w, pool_scale)
    gx, d_wat, d_pre0 = _pre_bwd(dproj0, wa_t, x, pre0, g1, "proj0_bwd")
    grads = dict(pre0=d_pre0, pre1=d_pre1, post0=d_post0, post1=d_post1, wa_t=d_wat, sinks=d_sinks, pool_w=d_pw, pool_scale=d_ps, w_ab=d_wab, wc_t=d_wct,
                 dw=d_dw, dwb=d_dwb, lng=d_lng, lnb=d_lnb, w_c=d_wc)
    return loss[0, 0], gx, grads


def _place():
    x, y, c = lax.axis_index("x"), lax.axis_index("y"), lax.axis_index("c")
    chips = [(1 - x, y), (x, 1 - y), (1 - x, 1 - y)]
    return x, y, c, chips


ANY = pl.BlockSpec(memory_space=pl.ANY)


def _all_gather(blocks, name, halve):
    n = len(blocks)
    shapes = [((b.shape[0] // 2) if halve else b.shape[0], b.shape[1]) for b in blocks]

    def body(*refs):
        ins, outs = refs[:n], refs[n:2 * n]
        send_sems, recv_sems, local_sems = refs[2 * n:]
        x, y, c, chips = _place()
        me, sibling = (x, y, c), (x, y, 1 - c)

        def piece(t, px, py, pc):
            return outs[t].at[4 * px + 2 * py + pc]

        def own(t):
            return ins[t].at[pl.ds(c * shapes[t][0], shapes[t][0])] if halve else ins[t]

        def copy(t, k, block, to, src=None):
            return pltpu.make_async_remote_copy(
                src_ref=piece(t, *block) if src is None else src, dst_ref=piece(t, *block),
                send_sem=send_sems.at[7 * t + k], recv_sem=recv_sems.at[7 * t + k],
                device_id=to, device_id_type=MESH_ID)

        mine = [pltpu.make_async_copy(own(t), piece(t, *me), local_sems.at[t]) for t in range(n)]
        for cp in mine:
            cp.start()
        first = []
        for t in range(n):
            first.append(copy(t, 0, me, sibling, src=own(t)))
            first += [copy(t, 1 + j, me, (*chip, c), src=own(t)) for j, chip in enumerate(chips)]
        for cp in first:
            cp.start()
        passed = []
        for j, chip in enumerate(chips):
            for t in range(n):
                copy(t, 1 + j, (*chip, c), me).wait_recv()
                fwd = copy(t, 4 + j, (*chip, c), sibling)
                fwd.start()
                passed.append(fwd)
        for t in range(n):
            copy(t, 0, sibling, me).wait_recv()
            for j, chip in enumerate(chips):
                copy(t, 4 + j, (*chip, 1 - c), me).wait_recv()
        for cp in first + passed:
            cp.wait_send()
        for cp in mine:
            cp.wait()

    return pl.pallas_call(
        body, name=name,
        out_shape=[jax.ShapeDtypeStruct((N_DEV, r, cols), b.dtype) for (r, cols), b in zip(shapes, blocks)],
        in_specs=[ANY] * n, out_specs=[ANY] * n,
        scratch_shapes=[pltpu.SemaphoreType.DMA((7 * n,)), pltpu.SemaphoreType.DMA((7 * n,)),
                        pltpu.SemaphoreType.DMA((n,))],
    )(*blocks)


def _exchange_halves(grads, name):
    n = len(grads)

    def body(*refs):
        ins, outs = refs[:n], refs[n:2 * n]
        send_sems, recv_sems = refs[2 * n:]
        x, y, c, _ = _place()
        copies = []
        for t in range(n):
            for b in range(4):
                copies.append(pltpu.make_async_remote_copy(
                    src_ref=ins[t].at[2 * b + 1 - c], dst_ref=outs[t].at[b],
                    send_sem=send_sems.at[4 * t + b], recv_sem=recv_sems.at[4 * t + b],
                    device_id=(x, y, 1 - c), device_id_type=MESH_ID))
        for cp in copies:
            cp.start()
        for cp in copies:
            cp.wait_recv()
        for cp in copies:
            cp.wait_send()

    return pl.pallas_call(
        body, name=name,
        out_shape=[jax.ShapeDtypeStruct((4,) + g.shape[1:], g.dtype) for g in grads],
        in_specs=[ANY] * n, out_specs=[ANY] * n,
        scratch_shapes=[pltpu.SemaphoreType.DMA((4 * n,)), pltpu.SemaphoreType.DMA((4 * n,))],
    )(*grads)


def _pair_sum(g, a, core, name):
    r = rt = g.shape[1]

    def body(c_ref, g_ref, a_ref, o_ref):
        o_ref[...] = (g_ref[...] + a_ref[...]).astype(BF16)

    blk = (1, rt, D_MODEL)
    return pl.pallas_call(
        body, name=name,
        grid_spec=pltpu.PrefetchScalarGridSpec(
            num_scalar_prefetch=1, grid=(4, r // rt),
            in_specs=[pl.BlockSpec(blk, lambda b, j, c_ref: (2 * b + c_ref[0], j, 0)),
                      pl.BlockSpec(blk, lambda b, j, c_ref: (b, j, 0))],
            out_specs=pl.BlockSpec(blk, lambda b, j, c_ref: (b, j, 0))),
        out_shape=jax.ShapeDtypeStruct((4, r, D_MODEL), BF16),
        compiler_params=_params("parallel", "parallel"),
    )(core, g, a)


def _send_to_owners(sums, name):
    n = len(sums)

    def body(*refs):
        ins, outs = refs[:n], refs[n:2 * n]
        send_sems, recv_sems = refs[2 * n:]
        x, y, c, chips = _place()
        copies = []
        for t in range(n):
            for j, (cx, cy) in enumerate(chips):
                copies.append(pltpu.make_async_remote_copy(
                    src_ref=ins[t].at[2 * cx + cy], dst_ref=outs[t].at[j],
                    send_sem=send_sems.at[3 * t + j], recv_sem=recv_sems.at[3 * t + j],
                    device_id=(cx, cy, c), device_id_type=MESH_ID))
        for cp in copies:
            cp.start()
        for cp in copies:
            cp.wait_recv()
        for cp in copies:
            cp.wait_send()

    return pl.pallas_call(
        body, name=name,
        out_shape=[jax.ShapeDtypeStruct((3,) + s.shape[1:], s.dtype) for s in sums],
        in_specs=[ANY] * n, out_specs=[ANY] * n,
        scratch_shapes=[pltpu.SemaphoreType.DMA((3 * n,)), pltpu.SemaphoreType.DMA((3 * n,))],
    )(*sums)


def _final_sum(g, a, recv, place, name):
    r = rt = g.shape[1]

    def body(p_ref, g_ref, a_ref, r_ref, o_ref):
        acc = g_ref[0] + a_ref[0]
        for j in range(3):
            acc = acc + r_ref[j].astype(F32)
        o_ref[...] = acc

    return pl.pallas_call(
        body, name=name,
        grid_spec=pltpu.PrefetchScalarGridSpec(
            num_scalar_prefetch=1, grid=(r // rt,),
            in_specs=[pl.BlockSpec((1, rt, D_MODEL), lambda j, p_ref: (p_ref[0], j, 0)),
                      pl.BlockSpec((1, rt, D_MODEL), lambda j, p_ref: (p_ref[1], j, 0)),
                      pl.BlockSpec((3, rt, D_MODEL), lambda j, p_ref: (0, j, 0))],
            out_specs=pl.BlockSpec((rt, D_MODEL), lambda j, p_ref: (j, 0))),
        out_shape=jax.ShapeDtypeStruct((r, D_MODEL), F32),
        compiler_params=_params("parallel"),
    )(place, g, a, recv)


def _share_with_sibling(pieces, name):
    n = len(pieces)

    def body(*refs):
        ins, outs = refs[:n], refs[n:2 * n]
        send_sems, recv_sems, local_sems = refs[2 * n:]
        x, y, c, _ = _place()
        local = [pltpu.make_async_copy(ins[t], outs[t].at[c], local_sems.at[t]) for t in range(n)]
        remote = [pltpu.make_async_remote_copy(
            src_ref=ins[t], dst_ref=outs[t].at[c], send_sem=send_sems.at[t], recv_sem=recv_sems.at[t],
            device_id=(x, y, 1 - c), device_id_type=MESH_ID) for t in range(n)]
        for cp in local + remote:
            cp.start()
        for cp in remote:
            cp.wait_recv()
        for cp in remote:
            cp.wait_send()
        for cp in local:
            cp.wait()

    return pl.pallas_call(
        body, name=name,
        out_shape=[jax.ShapeDtypeStruct((2,) + p.shape, p.dtype) for p in pieces],
        in_specs=[ANY] * n, out_specs=[ANY] * n,
        scratch_shapes=[pltpu.SemaphoreType.DMA((n,)), pltpu.SemaphoreType.DMA((n,)), pltpu.SemaphoreType.DMA((n,))],
    )(*pieces)


def _reduce_scatter(grads):
    x, y, c, _ = _place()
    core = jnp.reshape(c, (1,)).astype(jnp.int32)
    block = 2 * x + y
    place = jnp.stack([2 * block + c, block]).astype(jnp.int32)
    g8 = [g.reshape(N_DEV, g.shape[0] // N_DEV, D_MODEL) for g in grads]
    other = _exchange_halves(g8, "rs_pair_exchange")
    sums = [_pair_sum(g, a, core, f"rs_pair_sum{t}") for t, (g, a) in enumerate(zip(g8, other))]
    recv = _send_to_owners(sums, "rs_chip_exchange")
    pieces = [_final_sum(g, a, r, place, f"rs_final_sum{t}") for t, (g, a, r) in enumerate(zip(g8, other, recv))]
    both = _share_with_sibling(pieces, "rs_share")
    return [b.reshape(2 * b.shape[1], D_MODEL) for b in both]


def _sum8(parts, name):
    r = parts.shape[1]

    def body(p_ref, o_ref):
        acc = p_ref[0]
        for k in range(1, N_DEV):
            acc = acc + p_ref[k]
        o_ref[...] = acc

    return pl.pallas_call(
        body, name=name, out_shape=jax.ShapeDtypeStruct((r, 128), F32),
        in_specs=[pl.BlockSpec(memory_space=pltpu.VMEM)], out_specs=pl.BlockSpec(memory_space=pltpu.VMEM),
    )(parts)


def _adamw(w, g, m, v, name):
    shape = w.shape
    cols = shape[-1]
    rows = w.size // cols
    rt = 256 if rows % 256 == 0 else rows

    def body(w_ref, g_ref, m_ref, v_ref, d_ref, nm_ref, nv_ref):
        gv = g_ref[...]
        mn = ADAM_B1 * m_ref[...] + (1.0 - ADAM_B1) * gv
        vn = ADAM_B2 * v_ref[...] + (1.0 - ADAM_B2) * (gv * gv)
        m_hat = mn / (1.0 - ADAM_B1 ** ADAM_STEP)
        v_hat = vn / (1.0 - ADAM_B2 ** ADAM_STEP)
        d_ref[...] = -ADAM_LR * (m_hat / (jnp.sqrt(v_hat) + ADAM_EPS) + ADAM_WD * w_ref[...])
        nm_ref[...] = mn
        nv_ref[...] = vn

    spec = pl.BlockSpec((rt, cols), lambda i: (i, 0))
    outs = pl.pallas_call(
        body, name=name, grid=(rows // rt,),
        out_shape=[jax.ShapeDtypeStruct((rows, cols), F32)] * 3,
        in_specs=[spec] * 4, out_specs=[spec] * 3,
        compiler_params=_params("parallel"),
    )(*[a.reshape(rows, cols) for a in (w, g, m, v)])
    return [o.reshape(shape) for o in outs]


SMALL_ROWS = 832


def _pack_small(g):
    parts = [g["pre0"].reshape(8, 128), g["pre1"].reshape(8, 128), g["post0"].reshape(8, 128),
             g["post1"].reshape(8, 128), g["sinks"], jnp.pad(g["pool_scale"].reshape(4, 128), ((0, 4), (0, 0))),
             g["pool_w"], g["dw"].reshape(248, 128), g["dwb"].reshape(8, 128), g["lng"].reshape(8, 128),
             g["lnb"].reshape(8, 128)]
    assert sum(p.shape[0] for p in parts) == SMALL_ROWS
    return jnp.concatenate(parts, axis=0)


def _unpack_small(s):
    out, r = {}, 0
    for key, rows, shape in (("pre", 16, (2, D_MODEL)), ("post", 16, (2, D_MODEL)), ("sinks", 8, (8, 128)),
                             ("pool_scale", 4, (1, 512)), ("pad", 4, (4, 128)), ("pool_w", 512, (1, 4, 128, 128)),
                             ("dw", 248, (CONV_K, D_MODEL)), ("dwb", 8, (1, D_MODEL)), ("lng", 8, (1, D_MODEL)),
                             ("lnb", 8, (1, D_MODEL))):
        out[key] = s[r:r + rows].reshape(shape)
        r += rows
    return out


def kernel(x, pre_norm, post_norm, a_w_in, a_sinks, b_pool_w, b_pool_scale, ab_w_out, c_w_in, c_dw_w, c_dw_b, c_ln_g, c_ln_b, c_w_out, loss_target, m_pre_norm, m_post_norm, m_a_w_in, m_a_sinks, m_b_pool_w, m_b_pool_scale, m_ab_w_out, m_c_w_in, m_c_dw_w, m_c_dw_b, m_c_ln_g, m_c_ln_b, m_c_w_out, v_pre_norm, v_post_norm, v_a_w_in, v_a_sinks, v_b_pool_w, v_b_pool_scale, v_ab_w_out, v_c_w_in, v_c_dw_w, v_c_dw_b, v_c_ln_g, v_c_ln_b, v_c_w_out):
    ix, iy = lax.axis_index("x"), lax.axis_index("y")
    chip_cols = (2 * ix + iy) * 256

    pad8 = lambda v: jnp.pad(v, ((0, -v.shape[0] % 8), (0, 0)))
    vec_shard = jnp.concatenate([pad8(c_dw_w.reshape(CONV_K, 256)), pad8(c_dw_b), pad8(c_ln_g), pad8(c_ln_b),
                                 jnp.zeros((8, 256), F32)], axis=0)
    mats = [a_w_in[0].T.astype(BF16), ab_w_out[0].astype(BF16), c_w_in[0].T.astype(BF16), c_w_out[0].astype(BF16)]
    wa_t, w_ab, wc_t, w_c, vecs = _all_gather(mats + [vec_shard], "weights_all_gather", halve=True)
    wa_t = wa_t.reshape(EVEN_IN, D_MODEL)
    w_ab = w_ab.reshape(D_MODEL, D_MODEL)
    wc_t = wc_t.reshape(3 * D_MODEL, D_MODEL)
    w_c = w_c.reshape(D_MODEL, D_MODEL)
    vecs = vecs.reshape(4, 64, 256).transpose(1, 0, 2).reshape(64, D_MODEL)
    dw, dwb, lng, lnb = vecs[0:CONV_K], vecs[32:33], vecs[40:41], vecs[48:49]

    loss, gx, g = _local_step(x[0], loss_target[0], pre_norm, post_norm, wa_t, a_sinks, b_pool_w[0], b_pool_scale,
                              w_ab, wc_t, dw, dwb, lng, lnb, w_c)
    loss = lax.psum(loss, ("x", "y", "c"))

    g_wa_t, g_wab, g_wc_t, g_wc = _reduce_scatter([g["wa_t"], g["w_ab"], g["wc_t"], g["w_c"]])
    g_a_w_in = g_wa_t.T[None]
    g_c_w_in = g_wc_t.T[None]
    g_ab_w_out = g_wab[None]
    g_c_w_out = g_wc[None]

    small = _sum8(_all_gather([_pack_small(g)], "small_all_gather", halve=False)[0], "small_sum")
    s = _unpack_small(small)
    g_pre, g_post = s["pre"], s["post"]
    g_sinks = s["sinks"][:, 0].reshape(1, 8)
    g_pool_w, g_pool_scale = s["pool_w"], s["pool_scale"]
    g_dw = lax.dynamic_slice_in_dim(s["dw"], chip_cols, 256, axis=1).reshape(1, CONV_K, 1, 256)
    g_dwb = lax.dynamic_slice_in_dim(s["dwb"], chip_cols, 256, axis=1)
    g_lng = lax.dynamic_slice_in_dim(s["lng"], chip_cols, 256, axis=1)
    g_lnb = lax.dynamic_slice_in_dim(s["lnb"], chip_cols, 256, axis=1)

    grads = [g_pre, g_post, g_a_w_in, g_sinks, g_pool_w, g_pool_scale, g_ab_w_out, g_c_w_in, g_dw, g_dwb, g_lng, g_lnb,
             g_c_w_out]
    weights = [pre_norm, post_norm, a_w_in, a_sinks, b_pool_w, b_pool_scale, ab_w_out, c_w_in, c_dw_w, c_dw_b, c_ln_g,
               c_ln_b, c_w_out]
    moms = [m_pre_norm, m_post_norm, m_a_w_in, m_a_sinks, m_b_pool_w, m_b_pool_scale, m_ab_w_out, m_c_w_in, m_c_dw_w,
            m_c_dw_b, m_c_ln_g, m_c_ln_b, m_c_w_out]
    vars_ = [v_pre_norm, v_post_norm, v_a_w_in, v_a_sinks, v_b_pool_w, v_b_pool_scale, v_ab_w_out, v_c_w_in, v_c_dw_w,
             v_c_dw_b, v_c_ln_g, v_c_ln_b, v_c_w_out]
    deltas, new_m, new_v = [], [], []
    for k, (w, gr, m, v) in enumerate(zip(weights, grads, moms, vars_)):
        d, nm, nv = _adamw(w, gr, m, v, f"adamw{k}")
        deltas.append(d)
        new_m.append(nm)
        new_v.append(nv)
    return (loss, gx[None], *grads, *deltas, *new_m, *new_v)
```

```python
import functools

import jax
import jax.numpy as jnp
from jax import lax
from jax.experimental import pallas as pl
from jax.experimental.pallas import tpu as pltpu

F32 = jnp.float32
BF16 = jnp.bfloat16

D_MODEL = 1024
EPS = 1e-6
NEG = -1e30
HEAD_DIM = 64
GROUP = 4
KV_HEADS = 2
BLOCK = 128
EVEN_IN = 2304
COL_Q, COL_K, COL_GA, COL_U, COL_GB = 0, 512, 768, 1280, 1792
POOL_GROUPS = 4
POOL_GC = 128
POOL_HALO = 16
CONV_K = 31
CONV_HALO = 32
N_DEV = 8

ADAM_LR = 0.001
ADAM_B1 = 0.9
ADAM_B2 = 0.999
ADAM_EPS = 1e-08
ADAM_WD = 0.01
ADAM_STEP = 10

VMEM_LIMIT_BYTES = 56 * 1024 * 1024

NT = (((1,), (1,)), ((), ()))
TN = (((0,), (0,)), ((), ()))
MESH_ID = pl.DeviceIdType.MESH


def _params(*sem):
    return pltpu.CompilerParams(dimension_semantics=sem, vmem_limit_bytes=VMEM_LIMIT_BYTES)


def _const_spec(shape):
    nd = len(shape)
    return pl.BlockSpec(shape, lambda *_: (0,) * nd, pipeline_mode=pl.Buffered(1))


def _sigmoid(v):
    return 0.5 * jnp.tanh(0.5 * v) + 0.5


def _silu(v):
    return v * _sigmoid(v)


def _silu_and_grad(v):
    s = _sigmoid(v)
    return v * s, s * (1.0 + v * (1.0 - s))


ANY = pl.BlockSpec(memory_space=pl.ANY)


def _place():
    x, y, c = lax.axis_index("x"), lax.axis_index("y"), lax.axis_index("c")
    chips = [(1 - x, y), (x, 1 - y), (1 - x, 1 - y)]
    return x, y, c, chips


class _Gather:
    def __init__(self, blocks, halve):
        self.ins = list(blocks)
        self.halve = halve
        self.n = n = len(blocks)
        self.shapes = [((b.shape[0] // 2) if halve else b.shape[0], b.shape[1]) for b in blocks]
        self.out_shape = [jax.ShapeDtypeStruct((N_DEV, r, cols), b.dtype) for (r, cols), b in zip(self.shapes, blocks)]
        self.scratch = [pltpu.SemaphoreType.DMA((7 * n,)), pltpu.SemaphoreType.DMA((7 * n,)),
                        pltpu.SemaphoreType.DMA((n,))]

    def _copies(self, ins, outs, sems):
        send_sems, recv_sems, local_sems = sems
        x, y, c, chips = _place()
        me, sibling = (x, y, c), (x, y, 1 - c)

        def piece(t, px, py, pc):
            return outs[t].at[4 * px + 2 * py + pc]

        def own(t):
            return ins[t].at[pl.ds(c * self.shapes[t][0], self.shapes[t][0])] if self.halve else ins[t]

        def copy(t, k, block, to, src=None):
            return pltpu.make_async_remote_copy(
                src_ref=piece(t, *block) if src is None else src, dst_ref=piece(t, *block),
                send_sem=send_sems.at[7 * t + k], recv_sem=recv_sems.at[7 * t + k],
                device_id=to, device_id_type=MESH_ID)

        rng = range(self.n)
        return dict(
            mine=[pltpu.make_async_copy(own(t), piece(t, *me), local_sems.at[t]) for t in rng],
            first=[copy(t, 0, me, sibling, src=own(t)) for t in rng]
            + [copy(t, 1 + j, me, (*chip, c), src=own(t)) for t in rng for j, chip in enumerate(chips)],
            landed=[copy(t, 1 + j, (*chip, c), me) for j, chip in enumerate(chips) for t in rng],
            passed=[copy(t, 4 + j, (*chip, c), sibling) for j, chip in enumerate(chips) for t in rng],
            from_sibling=[copy(t, 0, sibling, me) for t in rng]
            + [copy(t, 4 + j, (*chip, 1 - c), me) for t in rng for j, chip in enumerate(chips)])

    def start(self, ins, outs, sems):
        d = self._copies(ins, outs, sems)
        for cp in d["mine"] + d["first"]:
            cp.start()

    def middle(self, ins, outs, sems):
        d = self._copies(ins, outs, sems)
        for got, fwd in zip(d["landed"], d["passed"]):
            got.wait_recv()
            fwd.start()

    def finish(self, ins, outs, sems):
        d = self._copies(ins, outs, sems)
        for cp in d["from_sibling"]:
            cp.wait_recv()
        for cp in d["first"] + d["passed"]:
            cp.wait_send()
        for cp in d["mine"]:
            cp.wait()


class _Scatter:
    def __init__(self, tensors):
        self.ins = list(tensors)
        self.n = n = len(tensors)
        self.out_shape = [jax.ShapeDtypeStruct(t.shape, t.dtype) for t in tensors]
        self.scratch = [pltpu.SemaphoreType.DMA((7 * n,)), pltpu.SemaphoreType.DMA((7 * n,))]

    def _copies(self, ins, outs, sems):
        send_sems, recv_sems = sems
        x, y, c, _ = _place()
        me = 4 * x + 2 * y + c
        sends, recvs = [], []
        for t in range(self.n):
            for m in range(1, N_DEV):
                px, py, pc = x ^ (m >> 2), y ^ ((m >> 1) & 1), c ^ (m & 1)
                q = 4 * px + 2 * py + pc
                sems_k = dict(send_sem=send_sems.at[7 * t + m - 1], recv_sem=recv_sems.at[7 * t + m - 1],
                              device_id=(px, py, pc), device_id_type=MESH_ID)
                sends.append(pltpu.make_async_remote_copy(src_ref=ins[t].at[q], dst_ref=outs[t].at[me], **sems_k))
                recvs.append(pltpu.make_async_remote_copy(src_ref=ins[t].at[me], dst_ref=outs[t].at[q], **sems_k))
        return sends, recvs

    def start(self, ins, outs, sems):
        for cp in self._copies(ins, outs, sems)[0]:
            cp.start()

    def middle(self, ins, outs, sems):
        pass

    def finish(self, ins, outs, sems):
        sends, recvs = self._copies(ins, outs, sems)
        for cp in recvs:
            cp.wait_recv()
        for cp in sends:
            cp.wait_send()


def _run_comm(comm, name):
    n = len(comm.ins)

    def body(*refs):
        parts = refs[:n], refs[n:2 * n], refs[2 * n:]
        comm.start(*parts)
        comm.middle(*parts)
        comm.finish(*parts)

    return pl.pallas_call(body, name=name, out_shape=comm.out_shape, in_specs=[ANY] * n, out_specs=[ANY] * n,
                          scratch_shapes=comm.scratch)(*comm.ins)


def _fused_call(body, comm, args, *, name, grid, out_shape, in_specs, out_specs, scratch_shapes=(), params):
    single = not isinstance(out_shape, (list, tuple))
    out_shape = [out_shape] if single else list(out_shape)
    out_specs = [out_specs] if single else list(out_specs)
    if comm is None:
        res = pl.pallas_call(body, name=name, grid=grid, out_shape=out_shape, in_specs=in_specs, out_specs=out_specs,
                             scratch_shapes=list(scratch_shapes), compiler_params=params)(*args)
        return (res[0] if single else res), []
    n_in, n_out, n_scr = len(in_specs), len(out_shape), len(scratch_shapes)
    c_in, c_out = len(comm.ins), len(comm.out_shape)
    steps = grid[0]

    def fused(*refs):
        pos = 0
        groups = []
        for size in (n_in, c_in, n_out, c_out, n_scr, len(comm.scratch)):
            groups.append(refs[pos:pos + size])
            pos += size
        ins, c_ins, outs, c_outs, scr, c_sems = groups
        i = pl.program_id(0)

        @pl.when(i == 0)
        def _():
            comm.start(c_ins, c_outs, c_sems)

        @pl.when(i == steps // 2)
        def _():
            comm.middle(c_ins, c_outs, c_sems)

        body(*ins, *outs, *scr)

        @pl.when(i == steps - 1)
        def _():
            comm.finish(c_ins, c_outs, c_sems)

    res = pl.pallas_call(
        fused, name=name, grid=grid, out_shape=out_shape + list(comm.out_shape),
        in_specs=list(in_specs) + [ANY] * c_in, out_specs=out_specs + [ANY] * c_out,
        scratch_shapes=list(scratch_shapes) + list(comm.scratch), compiler_params=params)(*args, *comm.ins)
    main = res[:n_out]
    return (main[0] if single else main), list(res[n_out:])


def _norm_matmul(x, gain, wt, name, comm=None, tm=512):
    t, n = x.shape[0], wt.shape[0]

    def body(x_ref, g_ref, wt_ref, o_ref):
        xv = x_ref[...]
        r = lax.rsqrt(jnp.mean(xv * xv, axis=-1, keepdims=True) + EPS)
        h = (xv * r * g_ref[...]).astype(BF16)
        o_ref[...] = lax.dot_general(h, wt_ref[...], NT, preferred_element_type=F32)

    return _fused_call(
        body, comm, (x, gain, wt), name=name, grid=(t // tm,),
        out_shape=jax.ShapeDtypeStruct((t, n), F32),
        in_specs=[pl.BlockSpec((tm, D_MODEL), lambda i: (i, 0)), _const_spec((1, D_MODEL)), _const_spec((n, D_MODEL))],
        out_specs=pl.BlockSpec((tm, n), lambda i: (i, 0)),
        params=_params("arbitrary"))


def _out_norm_res(a, w, x_in, post, name, target=None, tm=512):
    t = a.shape[0]
    with_loss = target is not None

    def body(*refs):
        if with_loss:
            a_ref, w_ref, x_ref, p_ref, t_ref, y_ref, o_ref, l_ref = refs
        else:
            a_ref, w_ref, x_ref, p_ref, y_ref, o_ref = refs
        y = jnp.dot(a_ref[...], w_ref[...], preferred_element_type=F32)
        y_ref[...] = y
        ry = lax.rsqrt(jnp.mean(y * y, axis=-1, keepdims=True) + EPS)
        xo = x_ref[...] + (y * ry) * p_ref[...]
        if with_loss:
            d = xo - t_ref[...]
            o_ref[...] = d * (1.0 / D_MODEL)

            @pl.when(pl.program_id(0) == 0)
            def _():
                l_ref[...] = jnp.zeros_like(l_ref)

            l_ref[...] += 0.5 * jnp.sum(jnp.mean(d * d, axis=-1, keepdims=True))
        else:
            o_ref[...] = xo

    row = pl.BlockSpec((tm, D_MODEL), lambda i: (i, 0))
    in_specs = [row, _const_spec((D_MODEL, D_MODEL)), row, _const_spec((1, D_MODEL))]
    out_shape = [jax.ShapeDtypeStruct((t, D_MODEL), F32), jax.ShapeDtypeStruct((t, D_MODEL), F32)]
    out_specs = [row, row]
    args = [a, w, x_in, post]
    if with_loss:
        in_specs.append(row)
        args.append(target)
        out_shape.append(jax.ShapeDtypeStruct((8, 128), F32))
        out_specs.append(pl.BlockSpec((8, 128), lambda i: (0, 0)))
    return pl.pallas_call(
        body, name=name, grid=(t // tm,), out_shape=out_shape, in_specs=in_specs, out_specs=out_specs,
        compiler_params=_params("arbitrary"),
    )(*args)


def _post_bwd(g, y, post, w, a, name, tm=512):
    t = g.shape[0]
    steps = t // tm

    def body(g_ref, y_ref, p_ref, w_ref, a_ref, da_ref, dw16_ref, dp_ref, dw_ref):
        @pl.when(pl.program_id(0) == 0)
        def _():
            dw_ref[...] = jnp.zeros_like(dw_ref)
            dp_ref[...] = jnp.zeros_like(dp_ref)

        gv = g_ref[...]
        yv = y_ref[...]
        ry = lax.rsqrt(jnp.mean(yv * yv, axis=-1, keepdims=True) + EPS)
        nv = yv * ry
        dp_ref[...] += jnp.sum(gv * nv, axis=0, keepdims=True)
        dn = gv * p_ref[...]
        dy = (ry * (dn - nv * jnp.mean(dn * nv, axis=-1, keepdims=True))).astype(BF16)
        da_ref[...] = lax.dot_general(dy, w_ref[...], NT, preferred_element_type=F32)
        dw_ref[...] += lax.dot_general(a_ref[...], dy, TN, preferred_element_type=F32)

        @pl.when(pl.program_id(0) == steps - 1)
        def _():
            dw16_ref[...] = dw_ref[...].astype(BF16)

    row = pl.BlockSpec((tm, D_MODEL), lambda i: (i, 0))
    return pl.pallas_call(
        body, name=name, grid=(steps,),
        out_shape=[jax.ShapeDtypeStruct((t, D_MODEL), F32), jax.ShapeDtypeStruct((D_MODEL, D_MODEL), BF16),
                   jax.ShapeDtypeStruct((1, D_MODEL), F32)],
        in_specs=[row, row, _const_spec((1, D_MODEL)), _const_spec((D_MODEL, D_MODEL)), row],
        out_specs=[row, pl.BlockSpec((D_MODEL, D_MODEL), lambda i: (0, 0)), pl.BlockSpec((1, D_MODEL), lambda i: (0, 0))],
        scratch_shapes=[pltpu.VMEM((D_MODEL, D_MODEL), F32)],
        compiler_params=_params("arbitrary"),
    )(g, y, post, w, a)


def _pre_bwd(dproj, wt, x_in, pre, g, name, tm=256):
    t, n = dproj.shape
    steps = t // tm

    def body(dp_ref, wt_ref, x_ref, pre_ref, g_ref, dx_ref, dwt16_ref, dpre_ref, dwt_ref):
        @pl.when(pl.program_id(0) == 0)
        def _():
            dwt_ref[...] = jnp.zeros_like(dwt_ref)
            dpre_ref[...] = jnp.zeros_like(dpre_ref)

        dpv = dp_ref[...]
        dh = jnp.dot(dpv, wt_ref[...], preferred_element_type=F32)
        xv = x_ref[...]
        r = lax.rsqrt(jnp.mean(xv * xv, axis=-1, keepdims=True) + EPS)
        xn = xv * r
        pv = pre_ref[...]
        dpre_ref[...] += jnp.sum(dh * xn, axis=0, keepdims=True)
        dxn = dh * pv
        dx_ref[...] = g_ref[...] + r * (dxn - xn * jnp.mean(dxn * xn, axis=-1, keepdims=True))
        h = (xn * pv).astype(BF16)
        dwt_ref[...] += lax.dot_general(dpv, h, TN, preferred_element_type=F32)

        @pl.when(pl.program_id(0) == steps - 1)
        def _():
            dwt16_ref[...] = dwt_ref[...].astype(BF16)

    row = pl.BlockSpec((tm, D_MODEL), lambda i: (i, 0))
    return pl.pallas_call(
        body, name=name, grid=(steps,),
        out_shape=[jax.ShapeDtypeStruct((t, D_MODEL), F32), jax.ShapeDtypeStruct((n, D_MODEL), BF16),
                   jax.ShapeDtypeStruct((1, D_MODEL), F32)],
        in_specs=[pl.BlockSpec((tm, n), lambda i: (i, 0)), _const_spec((n, D_MODEL)), row, _const_spec((1, D_MODEL)), row],
        out_specs=[row, pl.BlockSpec((n, D_MODEL), lambda i: (0, 0)), pl.BlockSpec((1, D_MODEL), lambda i: (0, 0))],
        scratch_shapes=[pltpu.VMEM((n, D_MODEL), F32)],
        compiler_params=_params("arbitrary"),
    )(dproj, wt, x_in, pre, g)


def _group_masks():
    lane = lax.broadcasted_iota(jnp.int32, (1, GROUP * HEAD_DIM), 1)
    return [(lane // HEAD_DIM == g).astype(F32) for g in range(GROUP)]


def _stack_groups(v, masks):
    return jnp.concatenate([v * m for m in masks], axis=0)


def _unstack_groups(v, masks):
    out = v[0:BLOCK] * masks[0]
    for g in range(1, GROUP):
        out = out + v[g * BLOCK:(g + 1) * BLOCK] * masks[g]
    return out


def _repeat_head(kv2, kvh):
    first = lax.broadcasted_iota(jnp.int32, kv2.shape, 1) < HEAD_DIM
    rolled = pltpu.roll(kv2, HEAD_DIM, 1)
    one = jnp.where(first, kv2, rolled) if kvh == 0 else jnp.where(first, rolled, kv2)
    return jnp.concatenate([one, one], axis=1)


def _fold_head(v4):
    a = v4[:, 0:128] + v4[:, 128:256]
    return a + pltpu.roll(a, HEAD_DIM, 1)


def _row_consts(kvh, sink_ref):
    rb = lax.broadcasted_iota(jnp.int32, (GROUP * BLOCK, 1), 0) // BLOCK
    slope = jnp.zeros((GROUP * BLOCK, 1), F32)
    sink = jnp.zeros((GROUP * BLOCK, 1), F32)
    for g in range(GROUP):
        h = kvh * GROUP + g
        slope = jnp.where(rb == g, 2.0 ** (-(h + 1)), slope)
        sink = jnp.where(rb == g, sink_ref[0, h], sink)
    return slope, sink


def _attn_probs(qk, k4, kvh, sink_ref, dist, valid, masks):
    qs = _stack_groups(qk, masks).astype(BF16)
    slope, sink = _row_consts(kvh, sink_ref)
    s = lax.dot_general(qs, k4, NT, preferred_element_type=F32) * (HEAD_DIM ** -0.5)
    s = jnp.where(valid, s - slope * dist, NEG)
    mx = jnp.maximum(jnp.max(s, axis=-1, keepdims=True), sink)
    e = jnp.exp(s - mx)
    es = jnp.exp(sink - mx)
    inv = 1.0 / (jnp.sum(e, axis=-1, keepdims=True) + es)
    return qs, e * inv, es * inv


def _dist_valid(key_pos0):
    row = lax.broadcasted_iota(jnp.int32, (GROUP * BLOCK, 2 * BLOCK), 0)
    col = lax.broadcasted_iota(jnp.int32, (GROUP * BLOCK, 2 * BLOCK), 1)
    dist = (row % BLOCK) + BLOCK - col
    valid = (dist >= 0) & (dist < BLOCK) & (col + key_pos0 >= 0)
    return dist.astype(F32), valid


def _pool_forward(u_ext, g, t0):
    n = u_ext.shape[0] - POOL_HALO
    s = u_ext
    for step in range(g + 1):
        s = s + pltpu.roll(s, 1 << step, 0)
    w = 2 << g
    t = t0 + lax.broadcasted_iota(jnp.int32, (n, 1), 0)
    cnt = jnp.minimum(t + 1, w).astype(F32)
    return s[POOL_HALO:] / cnt - u_ext[POOL_HALO:]


def _mix0_fwd(proj, sinks, pool_w, pool_scale, comm=None, tq=512):
    t = proj.shape[0]
    nblk = tq // BLOCK

    def body(main_ref, halo_ref, sink_ref, pw_ref, ps_ref, o_ref, kv_ref):
        i = pl.program_id(0)
        t0 = i * tq
        masks = _group_masks()
        kv_ref[0:BLOCK, :] = halo_ref[:, COL_K:COL_K + 256]
        kv_ref[BLOCK:, :] = main_ref[:, COL_K:COL_K + 256]

        def block(jb, carry):
            r0 = pl.multiple_of(jb * BLOCK, BLOCK)
            dist, valid = _dist_valid(t0 + r0 - BLOCK)
            q = main_ref[pl.ds(r0, BLOCK), COL_Q:COL_Q + 512]
            ga = main_ref[pl.ds(r0, BLOCK), COL_GA:COL_GA + 512]
            kk = kv_ref[pl.ds(r0, 2 * BLOCK), 0:128]
            vv = kv_ref[pl.ds(r0, 2 * BLOCK), 128:256]
            outs = []
            for kvh in range(KV_HEADS):
                k4 = _repeat_head(kk, kvh).astype(BF16)
                v4 = _repeat_head(vv, kvh).astype(BF16)
                _, p, _ = _attn_probs(q[:, kvh * 256:(kvh + 1) * 256], k4, kvh, sink_ref, dist, valid, masks)
                pv = jnp.dot(p.astype(BF16), v4, preferred_element_type=F32)
                outs.append(_unstack_groups(pv, masks))
            attn = jnp.concatenate(outs, axis=1)
            o_ref[pl.ds(r0, BLOCK), 0:512] = (attn * _silu(ga)).astype(BF16)
            return carry

        lax.fori_loop(0, nblk, block, 0)

        for g in range(POOL_GROUPS):
            cu = COL_U + g * POOL_GC
            cg = COL_GB + g * POOL_GC
            halo_u = jnp.where(i == 0, 0.0, halo_ref[BLOCK - POOL_HALO:BLOCK, cu:cu + POOL_GC])
            u_ext = jnp.concatenate([halo_u, main_ref[:, cu:cu + POOL_GC]], axis=0)
            pooled = _pool_forward(u_ext, g, t0)
            y = jnp.dot(pooled.astype(BF16), pw_ref[g].astype(BF16), preferred_element_type=F32)
            y = y * ps_ref[:, g * POOL_GC:(g + 1) * POOL_GC]
            o_ref[:, 512 + g * POOL_GC:512 + (g + 1) * POOL_GC] = (y * _silu(main_ref[:, cg:cg + POOL_GC])).astype(BF16)

    return _fused_call(
        body, comm, (proj, proj, sinks, pool_w, pool_scale), name="mix0_fwd", grid=(t // tq,),
        out_shape=jax.ShapeDtypeStruct((t, D_MODEL), BF16),
        in_specs=[pl.BlockSpec((tq, EVEN_IN), lambda i: (i, 0)),
                  pl.BlockSpec((BLOCK, EVEN_IN), lambda i: (jnp.maximum(i * nblk - 1, 0), 0)),
                  pl.BlockSpec(memory_space=pltpu.SMEM),
                  _const_spec((POOL_GROUPS, POOL_GC, POOL_GC)), _const_spec((1, 512))],
        out_specs=pl.BlockSpec((tq, D_MODEL), lambda i: (i, 0)),
        scratch_shapes=[pltpu.VMEM((tq + BLOCK, 256), F32)],
        params=_params("arbitrary"))


def _mix0_bwd(proj, dmix, sinks, pool_w, pool_scale, comm=None, tq=512):
    t = proj.shape[0]
    nt = t // tq
    nblk = tq // BLOCK

    def body(main_ref, halo_ref, next_ref, dm_ref, dmn_ref, sink_ref, pw_ref, ps_ref,
             o_ref, dsk_ref, dpw_ref, dps_ref, kv_ref, dkv_ref, carry_ref):
        i = pl.program_id(0)
        ii = nt - 1 - i
        t0 = ii * tq
        masks = _group_masks()

        @pl.when(i == 0)
        def _():
            dsk_ref[...] = jnp.zeros_like(dsk_ref)
            dpw_ref[...] = jnp.zeros_like(dpw_ref)
            dps_ref[...] = jnp.zeros_like(dps_ref)
            carry_ref[...] = jnp.zeros_like(carry_ref)

        kv_ref[0:BLOCK, :] = halo_ref[:, COL_K:COL_K + 256]
        kv_ref[BLOCK:, :] = main_ref[:, COL_K:COL_K + 256]
        dkv_ref[0:tq, :] = jnp.zeros((tq, 256), F32)
        dkv_ref[tq:, :] = carry_ref[...]

        def block(jb, carry):
            r0 = pl.multiple_of(jb * BLOCK, BLOCK)
            dist, valid = _dist_valid(t0 + r0 - BLOCK)
            q = main_ref[pl.ds(r0, BLOCK), COL_Q:COL_Q + 512]
            ga = main_ref[pl.ds(r0, BLOCK), COL_GA:COL_GA + 512]
            dya = dm_ref[pl.ds(r0, BLOCK), 0:512]
            kk = kv_ref[pl.ds(r0, 2 * BLOCK), 0:128]
            vv = kv_ref[pl.ds(r0, 2 * BLOCK), 128:256]
            silu_ga, dsilu_ga = _silu_and_grad(ga)
            do = dya * silu_ga
            first = lax.broadcasted_iota(jnp.int32, (2 * BLOCK, 128), 1) < HEAD_DIM
            attn, dq, dk, dv = [], [], [], []
            for kvh in range(KV_HEADS):
                k4 = _repeat_head(kk, kvh).astype(BF16)
                v4 = _repeat_head(vv, kvh).astype(BF16)
                qs, p, ps = _attn_probs(q[:, kvh * 256:(kvh + 1) * 256], k4, kvh, sink_ref, dist, valid, masks)
                pb = p.astype(BF16)
                o_k = _unstack_groups(jnp.dot(pb, v4, preferred_element_type=F32), masks)
                do_k = do[:, kvh * 256:(kvh + 1) * 256]
                dos = _stack_groups(do_k, masks).astype(BF16)
                prod = do_k * o_k
                delta = jnp.concatenate([jnp.sum(prod * m, axis=-1, keepdims=True) for m in masks], axis=0)
                dp = lax.dot_general(dos, v4, NT, preferred_element_type=F32)
                ds = (p * (dp - delta)).astype(BF16)
                sink_term = ps * delta
                for g in range(GROUP):
                    h = kvh * GROUP + g
                    dsk_ref[h:h + 1, :] -= jnp.sum(sink_term[g * BLOCK:(g + 1) * BLOCK], keepdims=True)
                scale = HEAD_DIM ** -0.5
                dq.append(_unstack_groups(jnp.dot(ds, k4, preferred_element_type=F32), masks) * scale)
                dk.append(_fold_head(lax.dot_general(ds, qs, TN, preferred_element_type=F32)) * scale)
                dv.append(_fold_head(lax.dot_general(pb, dos, TN, preferred_element_type=F32)))
                attn.append(o_k)
            o_ref[pl.ds(r0, BLOCK), COL_Q:COL_Q + 512] = jnp.concatenate(dq, axis=1).astype(BF16)
            o_all = jnp.concatenate(attn, axis=1)
            o_ref[pl.ds(r0, BLOCK), COL_GA:COL_GA + 512] = (dya * o_all * dsilu_ga).astype(BF16)
            dkv = jnp.concatenate([jnp.where(first, dk[0], dk[1]), jnp.where(first, dv[0], dv[1])], axis=1)
            dkv_ref[pl.ds(r0, 2 * BLOCK), :] += dkv
            return carry

        lax.fori_loop(0, nblk, block, 0)
        carry_ref[...] = dkv_ref[0:BLOCK, :]
        o_ref[:, COL_K:COL_K + 256] = dkv_ref[BLOCK:, :].astype(BF16)

        last = ii == nt - 1
        for g in range(POOL_GROUPS):
            cu = COL_U + g * POOL_GC
            cg = COL_GB + g * POOL_GC
            cm = 512 + g * POOL_GC
            pw = pw_ref[g].astype(BF16)
            sc = ps_ref[:, g * POOL_GC:(g + 1) * POOL_GC]
            halo_u = jnp.where(ii == 0, 0.0, halo_ref[BLOCK - POOL_HALO:BLOCK, cu:cu + POOL_GC])
            u_ext = jnp.concatenate([halo_u, main_ref[:, cu:cu + POOL_GC]], axis=0)
            pooled = _pool_forward(u_ext, g, t0).astype(BF16)
            y_raw = jnp.dot(pooled, pw, preferred_element_type=F32)
            gb = main_ref[:, cg:cg + POOL_GC]
            dyb = dm_ref[:, cm:cm + POOL_GC]
            silu_gb, dsilu_gb = _silu_and_grad(gb)
            dypool = dyb * silu_gb
            dps_ref[:, g * POOL_GC:(g + 1) * POOL_GC] += jnp.sum(dypool * y_raw, axis=0, keepdims=True)
            o_ref[:, cg:cg + POOL_GC] = (dyb * (y_raw * sc) * dsilu_gb).astype(BF16)
            dyraw = dypool * sc
            dyraw_n = jnp.where(last, 0.0, dmn_ref[:, cm:cm + POOL_GC] * _silu(next_ref[:, cg:cg + POOL_GC]) * sc)
            dpw_ref[g * POOL_GC:(g + 1) * POOL_GC, :] += lax.dot_general(pooled, dyraw.astype(BF16), TN,
                                                                         preferred_element_type=F32)
            dyraw_ext = jnp.concatenate([dyraw, dyraw_n], axis=0).astype(BF16)
            dpooled = lax.dot_general(dyraw_ext, pw, NT, preferred_element_type=F32)
            w = 2 << g
            tt = t0 + lax.broadcasted_iota(jnp.int32, (tq + POOL_HALO, 1), 0)
            s = dpooled / jnp.minimum(tt + 1, w).astype(F32)
            for step in range(g + 1):
                s = s + pltpu.roll(s, tq + POOL_HALO - (1 << step), 0)
            o_ref[:, cu:cu + POOL_GC] = (s[0:tq] - dpooled[0:tq]).astype(BF16)

    rev = lambda i: nt - 1 - i
    return _fused_call(
        body, comm, (proj, proj, proj, dmix, dmix, sinks, pool_w, pool_scale), name="mix0_bwd", grid=(nt,),
        out_shape=[jax.ShapeDtypeStruct((t, EVEN_IN), BF16), jax.ShapeDtypeStruct((8, 128), F32),
                   jax.ShapeDtypeStruct((POOL_GROUPS * POOL_GC, POOL_GC), F32), jax.ShapeDtypeStruct((1, 512), F32)],
        in_specs=[pl.BlockSpec((tq, EVEN_IN), lambda i: (rev(i), 0)),
                  pl.BlockSpec((BLOCK, EVEN_IN), lambda i: (jnp.maximum(rev(i) * nblk - 1, 0), 0)),
                  pl.BlockSpec((POOL_HALO, EVEN_IN),
                               lambda i: (jnp.minimum((rev(i) + 1) * (tq // POOL_HALO), t // POOL_HALO - 1), 0)),
                  pl.BlockSpec((tq, D_MODEL), lambda i: (rev(i), 0)),
                  pl.BlockSpec((POOL_HALO, D_MODEL),
                               lambda i: (jnp.minimum((rev(i) + 1) * (tq // POOL_HALO), t // POOL_HALO - 1), 0)),
                  pl.BlockSpec(memory_space=pltpu.SMEM),
                  _const_spec((POOL_GROUPS, POOL_GC, POOL_GC)), _const_spec((1, 512))],
        out_specs=[pl.BlockSpec((tq, EVEN_IN), lambda i: (rev(i), 0)),
                   pl.BlockSpec((8, 128), lambda i: (0, 0)),
                   pl.BlockSpec((POOL_GROUPS * POOL_GC, POOL_GC), lambda i: (0, 0)),
                   pl.BlockSpec((1, 512), lambda i: (0, 0))],
        scratch_shapes=[pltpu.VMEM((tq + BLOCK, 256), F32), pltpu.VMEM((tq + BLOCK, 256), F32),
                        pltpu.VMEM((BLOCK, 256), F32)],
        params=_params("arbitrary"))


CONV_RC = 32
CONV_CC = 128
CONV_CHAINS = 4
CONV_UNROLL = 2


def _fill_shifted(s_ref, rows):
    for b in range(1, 8):
        s_ref[b, 0:rows - 8, :] = s_ref[0, b:b + rows - 8, :]


def _tap_blocks(s_ref, r, cols, lead):
    for b in range(8):
        taps = [(a, 8 * a + b - lead) for a in range(5) if 0 <= 8 * a + b - lead < CONV_K]
        span = 8 * max(a for a, _ in taps) + CONV_RC
        blk = s_ref[b, pl.ds(r, span), cols]
        for a, k in taps:
            yield k, blk[8 * a:8 * a + CONV_RC]


def _conv_taps(s_ref, w_ref, r, cols, lead, reverse):
    accs = [None] * CONV_CHAINS
    for n, (k, blk) in enumerate(_tap_blocks(s_ref, r, cols, lead)):
        kw = CONV_K - 1 - k if reverse else k
        term = blk * w_ref[kw:kw + 1, cols]
        accs[n % CONV_CHAINS] = term if accs[n % CONV_CHAINS] is None else accs[n % CONV_CHAINS] + term
    return (accs[0] + accs[1]) + (accs[2] + accs[3])


def _layer_norm_fwd(cf, lng, lnb):
    mu = jnp.mean(cf, axis=-1, keepdims=True)
    xc = cf - mu
    rstd = lax.rsqrt(jnp.mean(xc * xc, axis=-1, keepdims=True) + EPS)
    chat = xc * rstd
    return chat, rstd, chat * lng + lnb


def _conv_fwd(proj, dw, dwb, lng, lnb, tt=256):
    t = proj.shape[0]
    lead = CONV_HALO - (CONV_K - 1)

    def body(main_ref, halo_ref, w_ref, b_ref, g_ref, lb_ref, o_ref, c_ref, gs_ref):
        i = pl.program_id(0)
        hv = halo_ref[...]
        gs_ref[0, 0:CONV_HALO, :] = jnp.where(i == 0, 0.0, hv[:, 0:1024] * _sigmoid(hv[:, 1024:2048]))
        gs_ref[0, CONV_HALO:CONV_HALO + tt, :] = main_ref[:, 0:1024] * _sigmoid(main_ref[:, 1024:2048])
        _fill_shifted(gs_ref, tt + CONV_HALO)

        for c in range(D_MODEL // CONV_CC):
            cols = slice(c * CONV_CC, (c + 1) * CONV_CC)

            def chunk(j, carry):
                r = pl.multiple_of(j * CONV_RC, CONV_RC)
                c_ref[pl.ds(r, CONV_RC), cols] = _conv_taps(gs_ref, w_ref, r, cols, lead, False) + b_ref[:, cols]
                return carry
            lax.fori_loop(0, tt // CONV_RC, chunk, 0, unroll=CONV_UNROLL)

        _, _, cn = _layer_norm_fwd(c_ref[...], g_ref[...], lb_ref[...])
        o_ref[...] = (_silu(cn) * _silu(main_ref[:, 2048:3072])).astype(BF16)

    vec = _const_spec((1, D_MODEL))
    row = pl.BlockSpec((tt, D_MODEL), lambda i: (i, 0))
    return pl.pallas_call(
        body, name="conv_fwd", grid=(t // tt,),
        out_shape=[jax.ShapeDtypeStruct((t, D_MODEL), BF16), jax.ShapeDtypeStruct((t, D_MODEL), F32)],
        in_specs=[pl.BlockSpec((tt, 3 * D_MODEL), lambda i: (i, 0)),
                  pl.BlockSpec((CONV_HALO, 3 * D_MODEL), lambda i: (jnp.maximum(i * (tt // CONV_HALO) - 1, 0), 0)),
                  _const_spec((CONV_K, D_MODEL)), vec, vec, vec],
        out_specs=[row, row],
        scratch_shapes=[pltpu.VMEM((8, tt + CONV_HALO, D_MODEL), F32)],
        compiler_params=_params("parallel"),
    )(proj, proj, dw, dwb, lng, lnb)


def _conv_bwd(proj, cf, dz, dw, lng, lnb, comm=None, tt=256):
    t = proj.shape[0]
    nt = t // tt
    te = tt + CONV_HALO

    def body(main_ref, next_ref, cf_ref, cfn_ref, dz_ref, dzn_ref, w_ref, g_ref, lb_ref,
             o_ref, ddw_ref, ddb_ref, dg_ref, dlb_ref, ds_ref, glu_ref, sb_ref):
        i = pl.program_id(0)

        @pl.when(i == 0)
        def _():
            ddw_ref[...] = jnp.zeros_like(ddw_ref)
            ddb_ref[...] = jnp.zeros_like(ddb_ref)
            dg_ref[...] = jnp.zeros_like(dg_ref)
            dlb_ref[...] = jnp.zeros_like(dlb_ref)

        lng = g_ref[...]
        chat, rstd, cn = _layer_norm_fwd(jnp.concatenate([cf_ref[...], cfn_ref[...]], axis=0), lng, lb_ref[...])
        gate = jnp.concatenate([main_ref[:, 2048:3072], next_ref[:, 2048:3072]], axis=0)
        dzv = jnp.concatenate([dz_ref[...], dzn_ref[...]], axis=0)
        rows = lax.broadcasted_iota(jnp.int32, (te, 1), 0)
        live = (rows < tt) | (i < nt - 1)
        own = (rows < tt).astype(F32)
        dzv = jnp.where(live, dzv, 0.0)
        silu_cn, dsilu_cn = _silu_and_grad(cn)
        silu_gate, dsilu_gate = _silu_and_grad(gate)
        o_ref[:, 2048:3072] = (dzv * silu_cn * dsilu_gate)[0:tt].astype(BF16)
        dcn = dzv * silu_gate * dsilu_cn
        dg_ref[...] += jnp.sum(dcn * chat * own, axis=0, keepdims=True)
        dlb_ref[...] += jnp.sum(dcn * own, axis=0, keepdims=True)
        dchat = dcn * lng
        dcf = rstd * (dchat - jnp.mean(dchat, axis=-1, keepdims=True) - chat * jnp.mean(dchat * chat, axis=-1, keepdims=True))
        ddb_ref[...] += jnp.sum(dcf * own, axis=0, keepdims=True)
        ds_ref[0, 0:te, :] = dcf
        ds_ref[0, te:, :] = jnp.zeros((8, D_MODEL), F32)
        _fill_shifted(ds_ref, te + 8)
        sb_ref[...] = _sigmoid(main_ref[:, 1024:2048])
        glu_ref[...] = main_ref[:, 0:1024] * sb_ref[...]

        for c in range(D_MODEL // CONV_CC):
            cols = slice(c * CONV_CC, (c + 1) * CONV_CC)
            gcols = slice(c * CONV_CC + 1024, (c + 1) * CONV_CC + 1024)

            def chunk(j, carry):
                r = pl.multiple_of(j * CONV_RC, CONV_RC)
                dglu = _conv_taps(ds_ref, w_ref, r, cols, 0, True)
                sb = sb_ref[pl.ds(r, CONV_RC), cols]
                o_ref[pl.ds(r, CONV_RC), cols] = (dglu * sb).astype(BF16)
                o_ref[pl.ds(r, CONV_RC), gcols] = (dglu * glu_ref[pl.ds(r, CONV_RC), cols] * (1.0 - sb)).astype(BF16)
                return carry
            lax.fori_loop(0, tt // CONV_RC, chunk, 0, unroll=CONV_UNROLL)

            def taps(j, accs):
                r = pl.multiple_of(j * CONV_RC, CONV_RC)
                gl = glu_ref[pl.ds(r, CONV_RC), cols]
                new = list(accs)
                for m, blk in _tap_blocks(ds_ref, r, cols, 0):
                    prod = blk * gl
                    part = prod[0:8]
                    for q in range(1, CONV_RC // 8):
                        part = part + prod[8 * q:8 * q + 8]
                    new[m] = new[m] + part
                return tuple(new)
            accs = lax.fori_loop(0, tt // CONV_RC, taps, tuple(jnp.zeros((8, CONV_CC), F32) for _ in range(CONV_K)))
            for m in range(CONV_K):
                k = CONV_K - 1 - m
                ddw_ref[k:k + 1, cols] += jnp.sum(accs[m], axis=0, keepdims=True)

    vec = _const_spec((1, D_MODEL))
    vec_out = pl.BlockSpec((1, D_MODEL), lambda i: (0, 0))
    row = pl.BlockSpec((tt, D_MODEL), lambda i: (i, 0))
    nxt = lambda i: (jnp.minimum((i + 1) * (tt // CONV_HALO), t // CONV_HALO - 1), 0)
    nxt_row = pl.BlockSpec((CONV_HALO, D_MODEL), nxt)
    return _fused_call(
        body, comm, (proj, proj, cf, cf, dz, dz, dw, lng, lnb), name="conv_bwd", grid=(nt,),
        out_shape=[jax.ShapeDtypeStruct((t, 3 * D_MODEL), BF16), jax.ShapeDtypeStruct((CONV_K, D_MODEL), F32),
                   jax.ShapeDtypeStruct((1, D_MODEL), F32), jax.ShapeDtypeStruct((1, D_MODEL), F32),
                   jax.ShapeDtypeStruct((1, D_MODEL), F32)],
        in_specs=[pl.BlockSpec((tt, 3 * D_MODEL), lambda i: (i, 0)),
                  pl.BlockSpec((CONV_HALO, 3 * D_MODEL), nxt),
                  row, nxt_row, row, nxt_row,
                  _const_spec((CONV_K, D_MODEL)), vec, vec],
        out_specs=[pl.BlockSpec((tt, 3 * D_MODEL), lambda i: (i, 0)),
                   pl.BlockSpec((CONV_K, D_MODEL), lambda i: (0, 0)), vec_out, vec_out, vec_out],
        scratch_shapes=[pltpu.VMEM((8, te + 8, D_MODEL), F32), pltpu.VMEM((tt, D_MODEL), F32),
                        pltpu.VMEM((tt, D_MODEL), F32)],
        params=_params("arbitrary"))


def _piece_sum(own, recv, place, name):
    r = own.shape[1]

    def body(p_ref, *refs):
        o_ref = refs[-1]
        acc = refs[0][0].astype(F32)
        for part in refs[1:-1]:
            acc = acc + part[0].astype(F32)
        o_ref[0] = acc

    blk = (1, r, D_MODEL)
    peer = lambda m: pl.BlockSpec(blk, lambda j, p_ref: (p_ref[0] ^ m, 0, 0))
    return pl.pallas_call(
        body, name=name,
        grid_spec=pltpu.PrefetchScalarGridSpec(
            num_scalar_prefetch=1, grid=(1,),
            in_specs=[peer(0)] + [peer(m) for m in range(1, N_DEV)],
            out_specs=pl.BlockSpec(blk, lambda j, p_ref: (p_ref[1], 0, 0))),
        out_shape=jax.ShapeDtypeStruct((2, r, D_MODEL), F32),
        compiler_params=_params("arbitrary"),
    )(place, own, *([recv] * (N_DEV - 1)))


def _share_with_sibling(halves, name):
    n = len(halves)

    def body(*refs):
        outs = refs[n:2 * n]
        send_sems, recv_sems = refs[2 * n:]
        x, y, c, _ = _place()
        send = [pltpu.make_async_remote_copy(
            src_ref=outs[t].at[c], dst_ref=outs[t].at[c], send_sem=send_sems.at[t], recv_sem=recv_sems.at[t],
            device_id=(x, y, 1 - c), device_id_type=MESH_ID) for t in range(n)]
        recv = [pltpu.make_async_remote_copy(
            src_ref=outs[t].at[c], dst_ref=outs[t].at[1 - c], send_sem=send_sems.at[t], recv_sem=recv_sems.at[t],
            device_id=(x, y, 1 - c), device_id_type=MESH_ID) for t in range(n)]
        for cp in send:
            cp.start()
        for cp in recv:
            cp.wait_recv()
        for cp in send:
            cp.wait_send()

    return pl.pallas_call(
        body, name=name,
        out_shape=[jax.ShapeDtypeStruct(h.shape, h.dtype) for h in halves],
        in_specs=[ANY] * n, out_specs=[ANY] * n,
        input_output_aliases={t: t for t in range(n)},
        scratch_shapes=[pltpu.SemaphoreType.DMA((n,)), pltpu.SemaphoreType.DMA((n,))],
    )(*halves)


def _sum8(parts, name):
    r = parts.shape[1]

    def body(p_ref, o_ref):
        acc = p_ref[0]
        for k in range(1, N_DEV):
            acc = acc + p_ref[k]
        o_ref[...] = acc

    return pl.pallas_call(
        body, name=name, out_shape=jax.ShapeDtypeStruct((r, 128), F32),
        in_specs=[pl.BlockSpec(memory_space=pltpu.VMEM)], out_specs=pl.BlockSpec(memory_space=pltpu.VMEM),
    )(parts)


def _adamw(w, g, m, v, name):
    shape = w.shape
    cols = shape[-1]
    rows = w.size // cols
    rt = 256 if rows % 256 == 0 else rows

    def body(w_ref, g_ref, m_ref, v_ref, d_ref, nm_ref, nv_ref):
        gv = g_ref[...]
        mn = ADAM_B1 * m_ref[...] + (1.0 - ADAM_B1) * gv
        vn = ADAM_B2 * v_ref[...] + (1.0 - ADAM_B2) * (gv * gv)
        m_hat = mn / (1.0 - ADAM_B1 ** ADAM_STEP)
        v_hat = vn / (1.0 - ADAM_B2 ** ADAM_STEP)
        d_ref[...] = -ADAM_LR * (m_hat / (jnp.sqrt(v_hat) + ADAM_EPS) + ADAM_WD * w_ref[...])
        nm_ref[...] = mn
        nv_ref[...] = vn

    spec = pl.BlockSpec((rt, cols), lambda i: (i, 0))
    outs = pl.pallas_call(
        body, name=name, grid=(rows // rt,),
        out_shape=[jax.ShapeDtypeStruct((rows, cols), F32)] * 3,
        in_specs=[spec] * 4, out_specs=[spec] * 3,
        compiler_params=_params("parallel"),
    )(*[a.reshape(rows, cols) for a in (w, g, m, v)])
    return [o.reshape(shape) for o in outs]


SMALL_ROWS = 832


def _pack_small(g):
    parts = [g["pre0"].reshape(8, 128), g["pre1"].reshape(8, 128), g["post0"].reshape(8, 128),
             g["post1"].reshape(8, 128), g["sinks"], jnp.pad(g["pool_scale"].reshape(4, 128), ((0, 4), (0, 0))),
             g["pool_w"], g["dw"].reshape(248, 128), g["dwb"].reshape(8, 128), g["lng"].reshape(8, 128),
             g["lnb"].reshape(8, 128)]
    assert sum(p.shape[0] for p in parts) == SMALL_ROWS
    return jnp.concatenate(parts, axis=0)


def _unpack_small(s):
    out, r = {}, 0
    for key, rows, shape in (("pre", 16, (2, D_MODEL)), ("post", 16, (2, D_MODEL)), ("sinks", 8, (8, 128)),
                             ("pool_scale", 4, (1, 512)), ("pad", 4, (4, 128)), ("pool_w", 512, (1, 4, 128, 128)),
                             ("dw", 248, (CONV_K, D_MODEL)), ("dwb", 8, (1, D_MODEL)), ("lng", 8, (1, D_MODEL)),
                             ("lnb", 8, (1, D_MODEL))):
        out[key] = s[r:r + rows].reshape(shape)
        r += rows
    return out


def kernel(x, pre_norm, post_norm, a_w_in, a_sinks, b_pool_w, b_pool_scale, ab_w_out, c_w_in, c_dw_w, c_dw_b, c_ln_g, c_ln_b, c_w_out, loss_target, m_pre_norm, m_post_norm, m_a_w_in, m_a_sinks, m_b_pool_w, m_b_pool_scale, m_ab_w_out, m_c_w_in, m_c_dw_w, m_c_dw_b, m_c_ln_g, m_c_ln_b, m_c_w_out, v_pre_norm, v_post_norm, v_a_w_in, v_a_sinks, v_b_pool_w, v_b_pool_scale, v_ab_w_out, v_c_w_in, v_c_dw_w, v_c_dw_b, v_c_ln_g, v_c_ln_b, v_c_w_out):
    ix, iy = lax.axis_index("x"), lax.axis_index("y")
    chip_cols = (2 * ix + iy) * 256

    pad8 = lambda v: jnp.pad(v, ((0, -v.shape[0] % 8), (0, 0)))
    vec_shard = jnp.concatenate([pad8(c_dw_w.reshape(CONV_K, 256)), pad8(c_dw_b), pad8(c_ln_g), pad8(c_ln_b),
                                 jnp.zeros((8, 256), F32)], axis=0)
    x0, target = x[0], loss_target[0]
    pre0, pre1 = pre_norm[0:1], pre_norm[1:2]
    post0, post1 = post_norm[0:1], post_norm[1:2]
    pool_w = b_pool_w[0]

    (wa_t,) = _run_comm(_Gather([a_w_in[0].T.astype(BF16)], halve=True), "gather_a_w_in")
    wa_t = wa_t.reshape(EVEN_IN, D_MODEL)
    proj0, (w_ab,) = _norm_matmul(x0, pre0, wa_t, "proj0_fwd", comm=_Gather([ab_w_out[0].astype(BF16)], halve=True))
    w_ab = w_ab.reshape(D_MODEL, D_MODEL)
    mix0, (wc_t, w_c, vecs) = _mix0_fwd(
        proj0, a_sinks, pool_w, b_pool_scale,
        comm=_Gather([c_w_in[0].T.astype(BF16), c_w_out[0].astype(BF16), vec_shard], halve=True))
    wc_t = wc_t.reshape(3 * D_MODEL, D_MODEL)
    w_c = w_c.reshape(D_MODEL, D_MODEL)
    vecs = vecs.reshape(4, 64, 256).transpose(1, 0, 2).reshape(64, D_MODEL)
    dw, dwb, lng, lnb = vecs[0:CONV_K], vecs[32:33], vecs[40:41], vecs[48:49]
    y0, x1 = _out_norm_res(mix0, w_ab, x0, post0, "out0_fwd")
    proj1, _ = _norm_matmul(x1, pre1, wc_t, "proj1_fwd")
    z1, cf1 = _conv_fwd(proj1, dw, dwb, lng, lnb)
    y1, g2, loss = _out_norm_res(z1, w_c, x1, post1, "out1_fwd", target=target)
    loss = lax.psum(loss[0, 0], ("x", "y", "c"))

    pieces = lambda m: m.reshape(N_DEV, m.shape[0] // N_DEV, D_MODEL)
    dz1, d_wc, d_post1 = _post_bwd(g2, y1, post1, w_c, z1, "out1_bwd")
    (dproj1, d_dw, d_dwb, d_lng, d_lnb), (r_wc,) = _conv_bwd(proj1, cf1, dz1, dw, lng, lnb,
                                                             comm=_Scatter([pieces(d_wc)]))
    g1, d_wct, d_pre1 = _pre_bwd(dproj1, wc_t, x1, pre1, g2, "proj1_bwd")
    dmix0, d_wab, d_post0 = _post_bwd(g1, y0, post0, w_ab, mix0, "out0_bwd")
    (dproj0, d_sinks, d_pw, d_ps), (r_wct, r_wab) = _mix0_bwd(proj0, dmix0, a_sinks, pool_w, b_pool_scale,
                                                             comm=_Scatter([pieces(d_wct), pieces(d_wab)]))
    gx, d_wat, d_pre0 = _pre_bwd(dproj0, wa_t, x0, pre0, g1, "proj0_bwd")
    (r_wat,) = _run_comm(_Scatter([pieces(d_wat)]), "scatter_a_w_in")

    ic = lax.axis_index("c")
    place = jnp.stack([4 * ix + 2 * iy + ic, ic]).astype(jnp.int32)
    sent_recv = ((d_wat, r_wat), (d_wab, r_wab), (d_wct, r_wct), (d_wc, r_wc))
    halves = [_piece_sum(pieces(own), recv, place, f"grad_sum{t}") for t, (own, recv) in enumerate(sent_recv)]
    g_wa_t, g_wab, g_wc_t, g_wc = [h.reshape(2 * h.shape[1], D_MODEL) for h in _share_with_sibling(halves, "grad_share")]
    g_a_w_in = g_wa_t.T[None]
    g_c_w_in = g_wc_t.T[None]
    g_ab_w_out = g_wab[None]
    g_c_w_out = g_wc[None]

    g = dict(pre0=d_pre0, pre1=d_pre1, post0=d_post0, post1=d_post1, sinks=d_sinks, pool_w=d_pw, pool_scale=d_ps,
             dw=d_dw, dwb=d_dwb, lng=d_lng, lnb=d_lnb)
    small = _sum8(_run_comm(_Gather([_pack_small(g)], halve=False), "small_all_gather")[0], "small_sum")
    s = _unpack_small(small)
    g_pre, g_post = s["pre"], s["post"]
    g_sinks = s["sinks"][:, 0].reshape(1, 8)
    g_pool_w, g_pool_scale = s["pool_w"], s["pool_scale"]
    g_dw = lax.dynamic_slice_in_dim(s["dw"], chip_cols, 256, axis=1).reshape(1, CONV_K, 1, 256)
    g_dwb = lax.dynamic_slice_in_dim(s["dwb"], chip_cols, 256, axis=1)
    g_lng = lax.dynamic_slice_in_dim(s["lng"], chip_cols, 256, axis=1)
    g_lnb = lax.dynamic_slice_in_dim(s["lnb"], chip_cols, 256, axis=1)

    grads = [g_pre, g_post, g_a_w_in, g_sinks, g_pool_w, g_pool_scale, g_ab_w_out, g_c_w_in, g_dw, g_dwb, g_lng, g_lnb,
             g_c_w_out]
    weights = [pre_norm, post_norm, a_w_in, a_sinks, b_pool_w, b_pool_scale, ab_w_out, c_w_in, c_dw_w, c_dw_b, c_ln_g,
               c_ln_b, c_w_out]
    moms = [m_pre_norm, m_post_norm, m_a_w_in, m_a_sinks, m_b_pool_w, m_b_pool_scale, m_ab_w_out, m_c_w_in, m_c_dw_w,
            m_c_dw_b, m_c_ln_g, m_c_ln_b, m_c_w_out]
    vars_ = [v_pre_norm, v_post_norm, v_a_w_in, v_a_sinks, v_b_pool_w, v_b_pool_scale, v_ab_w_out, v_c_w_in, v_c_dw_w,
             v_c_dw_b, v_c_ln_g, v_c_ln_b, v_c_w_out]
    deltas, new_m, new_v = [], [], []
    for k, (w, gr, m, v) in enumerate(zip(weights, grads, moms, vars_)):
        d, nm, nv = _adamw(w, gr, m, v, f"adamw{k}")
        deltas.append(d)
        new_m.append(nm)
        new_v.append(nv)
    return (loss, gx[None], *grads, *deltas, *new_m, *new_v)
```

```python
import functools

import jax
import jax.numpy as jnp
from jax import lax
from jax.experimental import pallas as pl
from jax.experimental.pallas import tpu as pltpu

F32 = jnp.float32
BF16 = jnp.bfloat16

D_MODEL = 1024
EPS = 1e-6
NEG = -1e30
HEAD_DIM = 64
GROUP = 4
KV_HEADS = 2
BLOCK = 128
EVEN_IN = 2304
COL_Q, COL_K, COL_GA, COL_U, COL_GB = 0, 512, 768, 1280, 1792
POOL_GROUPS = 4
POOL_GC = 128
POOL_HALO = 16
CONV_K = 31
CONV_HALO = 32
N_DEV = 8

ADAM_LR = 0.001
ADAM_B1 = 0.9
ADAM_B2 = 0.999
ADAM_EPS = 1e-08
ADAM_WD = 0.01
ADAM_STEP = 10

VMEM_LIMIT_BYTES = 56 * 1024 * 1024

NT = (((1,), (1,)), ((), ()))
TN = (((0,), (0,)), ((), ()))
MESH_ID = pl.DeviceIdType.MESH


def _params(*sem):
    return pltpu.CompilerParams(dimension_semantics=sem, vmem_limit_bytes=VMEM_LIMIT_BYTES)


def _const_spec(shape):
    nd = len(shape)
    return pl.BlockSpec(shape, lambda *_: (0,) * nd, pipeline_mode=pl.Buffered(1))


def _sigmoid(v):
    return 0.5 * jnp.tanh(0.5 * v) + 0.5


def _silu(v):
    return v * _sigmoid(v)


def _silu_and_grad(v):
    s = _sigmoid(v)
    return v * s, s * (1.0 + v * (1.0 - s))


ANY = pl.BlockSpec(memory_space=pl.ANY)


def _place():
    x, y, c = lax.axis_index("x"), lax.axis_index("y"), lax.axis_index("c")
    chips = [(1 - x, y), (x, 1 - y), (1 - x, 1 - y)]
    return x, y, c, chips


class _Gather:
    def __init__(self, blocks, halve):
        self.ins = list(blocks)
        self.halve = halve
        self.n = n = len(blocks)
        self.shapes = [((b.shape[0] // 2) if halve else b.shape[0], b.shape[1]) for b in blocks]
        self.out_shape = [jax.ShapeDtypeStruct((N_DEV, r, cols), b.dtype) for (r, cols), b in zip(self.shapes, blocks)]
        self.scratch = [pltpu.SemaphoreType.DMA((7 * n,)), pltpu.SemaphoreType.DMA((7 * n,)),
                        pltpu.SemaphoreType.DMA((n,))]

    def _copies(self, ins, outs, sems):
        send_sems, recv_sems, local_sems = sems
        x, y, c, chips = _place()
        me, sibling = (x, y, c), (x, y, 1 - c)

        def piece(t, px, py, pc):
            return outs[t].at[4 * px + 2 * py + pc]

        def own(t):
            return ins[t].at[pl.ds(c * self.shapes[t][0], self.shapes[t][0])] if self.halve else ins[t]

        def copy(t, k, block, to, src=None):
            return pltpu.make_async_remote_copy(
                src_ref=piece(t, *block) if src is None else src, dst_ref=piece(t, *block),
                send_sem=send_sems.at[7 * t + k], recv_sem=recv_sems.at[7 * t + k],
                device_id=to, device_id_type=MESH_ID)

        rng = range(self.n)
        return dict(
            mine=[pltpu.make_async_copy(own(t), piece(t, *me), local_sems.at[t]) for t in rng],
            first=[copy(t, 0, me, sibling, src=own(t)) for t in rng]
            + [copy(t, 1 + j, me, (*chip, c), src=own(t)) for t in rng for j, chip in enumerate(chips)],
            landed=[copy(t, 1 + j, (*chip, c), me) for j, chip in enumerate(chips) for t in rng],
            passed=[copy(t, 4 + j, (*chip, c), sibling) for j, chip in enumerate(chips) for t in rng],
            from_sibling=[copy(t, 0, sibling, me) for t in rng]
            + [copy(t, 4 + j, (*chip, 1 - c), me) for t in rng for j, chip in enumerate(chips)])

    def start(self, ins, outs, sems):
        d = self._copies(ins, outs, sems)
        for cp in d["mine"] + d["first"]:
            cp.start()

    def middle(self, ins, outs, sems):
        d = self._copies(ins, outs, sems)
        for got, fwd in zip(d["landed"], d["passed"]):
            got.wait_recv()
            fwd.start()

    def finish(self, ins, outs, sems):
        d = self._copies(ins, outs, sems)
        for cp in d["from_sibling"]:
            cp.wait_recv()
        for cp in d["first"] + d["passed"]:
            cp.wait_send()
        for cp in d["mine"]:
            cp.wait()


class _Scatter:
    def __init__(self, tensors):
        self.ins = list(tensors)
        self.n = n = len(tensors)
        self.out_shape = [jax.ShapeDtypeStruct(t.shape, t.dtype) for t in tensors]
        self.scratch = [pltpu.SemaphoreType.DMA((7 * n,)), pltpu.SemaphoreType.DMA((7 * n,))]

    def _copies(self, ins, outs, sems):
        send_sems, recv_sems = sems
        x, y, c, _ = _place()
        me = 4 * x + 2 * y + c
        sends, recvs = [], []
        for t in range(self.n):
            for m in range(1, N_DEV):
                px, py, pc = x ^ (m >> 2), y ^ ((m >> 1) & 1), c ^ (m & 1)
                q = 4 * px + 2 * py + pc
                sems_k = dict(send_sem=send_sems.at[7 * t + m - 1], recv_sem=recv_sems.at[7 * t + m - 1],
                              device_id=(px, py, pc), device_id_type=MESH_ID)
                sends.append(pltpu.make_async_remote_copy(src_ref=ins[t].at[q], dst_ref=outs[t].at[me], **sems_k))
                recvs.append(pltpu.make_async_remote_copy(src_ref=ins[t].at[me], dst_ref=outs[t].at[q], **sems_k))
        return sends, recvs

    def start(self, ins, outs, sems):
        for cp in self._copies(ins, outs, sems)[0]:
            cp.start()

    def middle(self, ins, outs, sems):
        pass

    def finish(self, ins, outs, sems):
        sends, recvs = self._copies(ins, outs, sems)
        for cp in recvs:
            cp.wait_recv()
        for cp in sends:
            cp.wait_send()


class _Comms:
    def __init__(self, *comms):
        self.comms = comms
        self.ins = [a for c in comms for a in c.ins]
        self.out_shape = [s for c in comms for s in c.out_shape]
        self.scratch = [s for c in comms for s in c.scratch]

    def _each(self, phase, ins, outs, sems):
        i = o = s = 0
        for c in self.comms:
            ni, no, ns = len(c.ins), len(c.out_shape), len(c.scratch)
            getattr(c, phase)(ins[i:i + ni], outs[o:o + no], sems[s:s + ns])
            i, o, s = i + ni, o + no, s + ns

    def start(self, ins, outs, sems):
        self._each("start", ins, outs, sems)

    def middle(self, ins, outs, sems):
        self._each("middle", ins, outs, sems)

    def finish(self, ins, outs, sems):
        self._each("finish", ins, outs, sems)


def _run_comm(comm, name):
    n = len(comm.ins)

    def body(*refs):
        parts = refs[:n], refs[n:2 * n], refs[2 * n:]
        comm.start(*parts)
        comm.middle(*parts)
        comm.finish(*parts)

    return pl.pallas_call(body, name=name, out_shape=comm.out_shape, in_specs=[ANY] * n, out_specs=[ANY] * n,
                          scratch_shapes=comm.scratch)(*comm.ins)


def _fused_call(body, comm, args, *, name, grid, out_shape, in_specs, out_specs, scratch_shapes=(), params):
    single = not isinstance(out_shape, (list, tuple))
    out_shape = [out_shape] if single else list(out_shape)
    out_specs = [out_specs] if single else list(out_specs)
    if comm is None:
        res = pl.pallas_call(body, name=name, grid=grid, out_shape=out_shape, in_specs=in_specs, out_specs=out_specs,
                             scratch_shapes=list(scratch_shapes), compiler_params=params)(*args)
        return (res[0] if single else res), []
    n_in, n_out, n_scr = len(in_specs), len(out_shape), len(scratch_shapes)
    c_in, c_out = len(comm.ins), len(comm.out_shape)
    steps = grid[0]

    def fused(*refs):
        pos = 0
        groups = []
        for size in (n_in, c_in, n_out, c_out, n_scr, len(comm.scratch)):
            groups.append(refs[pos:pos + size])
            pos += size
        ins, c_ins, outs, c_outs, scr, c_sems = groups
        i = pl.program_id(0)

        @pl.when(i == 0)
        def _():
            comm.start(c_ins, c_outs, c_sems)

        @pl.when(i == steps // 2)
        def _():
            comm.middle(c_ins, c_outs, c_sems)

        body(*ins, *outs, *scr)

        @pl.when(i == steps - 1)
        def _():
            comm.finish(c_ins, c_outs, c_sems)

    res = pl.pallas_call(
        fused, name=name, grid=grid, out_shape=out_shape + list(comm.out_shape),
        in_specs=list(in_specs) + [ANY] * c_in, out_specs=out_specs + [ANY] * c_out,
        scratch_shapes=list(scratch_shapes) + list(comm.scratch), compiler_params=params)(*args, *comm.ins)
    main = res[:n_out]
    return (main[0] if single else main), list(res[n_out:])


def _norm_matmul(x, gain, wt, name, comm=None, tm=512):
    t, n = x.shape[0], wt.shape[0]

    def body(x_ref, g_ref, wt_ref, o_ref):
        xv = x_ref[...]
        r = lax.rsqrt(jnp.mean(xv * xv, axis=-1, keepdims=True) + EPS)
        h = (xv * r * g_ref[...]).astype(BF16)
        o_ref[...] = lax.dot_general(h, wt_ref[...], NT, preferred_element_type=F32)

    return _fused_call(
        body, comm, (x, gain, wt), name=name, grid=(t // tm,),
        out_shape=jax.ShapeDtypeStruct((t, n), F32),
        in_specs=[pl.BlockSpec((tm, D_MODEL), lambda i: (i, 0)), _const_spec((1, D_MODEL)), _const_spec((n, D_MODEL))],
        out_specs=pl.BlockSpec((tm, n), lambda i: (i, 0)),
        params=_params("arbitrary"))


def _out_norm_res(a, w, x_in, post, name, target=None, tm=512):
    t = a.shape[0]
    with_loss = target is not None

    def body(*refs):
        if with_loss:
            a_ref, w_ref, x_ref, p_ref, t_ref, y_ref, o_ref, l_ref = refs
        else:
            a_ref, w_ref, x_ref, p_ref, y_ref, o_ref = refs
        y = jnp.dot(a_ref[...], w_ref[...], preferred_element_type=F32)
        y_ref[...] = y
        ry = lax.rsqrt(jnp.mean(y * y, axis=-1, keepdims=True) + EPS)
        xo = x_ref[...] + (y * ry) * p_ref[...]
        if with_loss:
            d = xo - t_ref[...]
            o_ref[...] = d * (1.0 / D_MODEL)

            @pl.when(pl.program_id(0) == 0)
            def _():
                l_ref[...] = jnp.zeros_like(l_ref)

            l_ref[...] += 0.5 * jnp.sum(jnp.mean(d * d, axis=-1, keepdims=True))
        else:
            o_ref[...] = xo

    row = pl.BlockSpec((tm, D_MODEL), lambda i: (i, 0))
    in_specs = [row, _const_spec((D_MODEL, D_MODEL)), row, _const_spec((1, D_MODEL))]
    out_shape = [jax.ShapeDtypeStruct((t, D_MODEL), F32), jax.ShapeDtypeStruct((t, D_MODEL), F32)]
    out_specs = [row, row]
    args = [a, w, x_in, post]
    if with_loss:
        in_specs.append(row)
        args.append(target)
        out_shape.append(jax.ShapeDtypeStruct((8, 128), F32))
        out_specs.append(pl.BlockSpec((8, 128), lambda i: (0, 0)))
    return pl.pallas_call(
        body, name=name, grid=(t // tm,), out_shape=out_shape, in_specs=in_specs, out_specs=out_specs,
        compiler_params=_params("arbitrary"),
    )(*args)


def _post_bwd(g, y, post, w, a, name, tm=512):
    t = g.shape[0]
    steps = t // tm

    def body(g_ref, y_ref, p_ref, w_ref, a_ref, da_ref, dw16_ref, dp_ref, dw_ref):
        @pl.when(pl.program_id(0) == 0)
        def _():
            dw_ref[...] = jnp.zeros_like(dw_ref)
            dp_ref[...] = jnp.zeros_like(dp_ref)

        gv = g_ref[...]
        yv = y_ref[...]
        ry = lax.rsqrt(jnp.mean(yv * yv, axis=-1, keepdims=True) + EPS)
        nv = yv * ry
        dp_ref[...] += jnp.sum(gv * nv, axis=0, keepdims=True)
        dn = gv * p_ref[...]
        dy = (ry * (dn - nv * jnp.mean(dn * nv, axis=-1, keepdims=True))).astype(BF16)
        da_ref[...] = lax.dot_general(dy, w_ref[...], NT, preferred_element_type=F32)
        dw_ref[...] += lax.dot_general(a_ref[...], dy, TN, preferred_element_type=F32)

        @pl.when(pl.program_id(0) == steps - 1)
        def _():
            dw16_ref[...] = dw_ref[...].astype(BF16)

    row = pl.BlockSpec((tm, D_MODEL), lambda i: (i, 0))
    return pl.pallas_call(
        body, name=name, grid=(steps,),
        out_shape=[jax.ShapeDtypeStruct((t, D_MODEL), F32), jax.ShapeDtypeStruct((D_MODEL, D_MODEL), BF16),
                   jax.ShapeDtypeStruct((1, D_MODEL), F32)],
        in_specs=[row, row, _const_spec((1, D_MODEL)), _const_spec((D_MODEL, D_MODEL)), row],
        out_specs=[row, pl.BlockSpec((D_MODEL, D_MODEL), lambda i: (0, 0)), pl.BlockSpec((1, D_MODEL), lambda i: (0, 0))],
        scratch_shapes=[pltpu.VMEM((D_MODEL, D_MODEL), F32)],
        compiler_params=_params("arbitrary"),
    )(g, y, post, w, a)


def _pre_bwd(dproj, wt, x_in, pre, g, name, tm=256):
    t, n = dproj.shape
    steps = t // tm

    def body(dp_ref, wt_ref, x_ref, pre_ref, g_ref, dx_ref, dwt16_ref, dpre_ref, dwt_ref):
        @pl.when(pl.program_id(0) == 0)
        def _():
            dwt_ref[...] = jnp.zeros_like(dwt_ref)
            dpre_ref[...] = jnp.zeros_like(dpre_ref)

        dpv = dp_ref[...]
        dh = jnp.dot(dpv, wt_ref[...], preferred_element_type=F32)
        xv = x_ref[...]
        r = lax.rsqrt(jnp.mean(xv * xv, axis=-1, keepdims=True) + EPS)
        xn = xv * r
        pv = pre_ref[...]
        dpre_ref[...] += jnp.sum(dh * xn, axis=0, keepdims=True)
        dxn = dh * pv
        dx_ref[...] = g_ref[...] + r * (dxn - xn * jnp.mean(dxn * xn, axis=-1, keepdims=True))
        h = (xn * pv).astype(BF16)
        dwt_ref[...] += lax.dot_general(dpv, h, TN, preferred_element_type=F32)

        @pl.when(pl.program_id(0) == steps - 1)
        def _():
            dwt16_ref[...] = dwt_ref[...].astype(BF16)

    row = pl.BlockSpec((tm, D_MODEL), lambda i: (i, 0))
    return pl.pallas_call(
        body, name=name, grid=(steps,),
        out_shape=[jax.ShapeDtypeStruct((t, D_MODEL), F32), jax.ShapeDtypeStruct((n, D_MODEL), BF16),
                   jax.ShapeDtypeStruct((1, D_MODEL), F32)],
        in_specs=[pl.BlockSpec((tm, n), lambda i: (i, 0)), _const_spec((n, D_MODEL)), row, _const_spec((1, D_MODEL)), row],
        out_specs=[row, pl.BlockSpec((n, D_MODEL), lambda i: (0, 0)), pl.BlockSpec((1, D_MODEL), lambda i: (0, 0))],
        scratch_shapes=[pltpu.VMEM((n, D_MODEL), F32)],
        compiler_params=_params("arbitrary"),
    )(dproj, wt, x_in, pre, g)


def _proj_dw(dproj, x_in, pre, name, tm=512):
    t, n = dproj.shape
    steps = t // tm

    def body(dp_ref, x_ref, pre_ref, dwt16_ref, dwt_ref):
        @pl.when(pl.program_id(0) == 0)
        def _():
            dwt_ref[...] = jnp.zeros_like(dwt_ref)

        xv = x_ref[...]
        r = lax.rsqrt(jnp.mean(xv * xv, axis=-1, keepdims=True) + EPS)
        h = (xv * r * pre_ref[...]).astype(BF16)
        dwt_ref[...] += lax.dot_general(dp_ref[...], h, TN, preferred_element_type=F32)

        @pl.when(pl.program_id(0) == steps - 1)
        def _():
            dwt16_ref[...] = dwt_ref[...].astype(BF16)

    return pl.pallas_call(
        body, name=name, grid=(steps,),
        out_shape=jax.ShapeDtypeStruct((n, D_MODEL), BF16),
        in_specs=[pl.BlockSpec((tm, n), lambda i: (i, 0)), pl.BlockSpec((tm, D_MODEL), lambda i: (i, 0)),
                  _const_spec((1, D_MODEL))],
        out_specs=pl.BlockSpec((n, D_MODEL), lambda i: (0, 0)),
        scratch_shapes=[pltpu.VMEM((n, D_MODEL), F32)],
        compiler_params=_params("arbitrary"),
    )(dproj, x_in, pre)


def _proj_dx(dproj, wt, x_in, pre, g, name, comm=None, tm=512):
    t, n = dproj.shape

    def body(dp_ref, wt_ref, x_ref, pre_ref, g_ref, dx_ref, dpre_ref):
        @pl.when(pl.program_id(0) == 0)
        def _():
            dpre_ref[...] = jnp.zeros_like(dpre_ref)

        dh = jnp.dot(dp_ref[...], wt_ref[...], preferred_element_type=F32)
        xv = x_ref[...]
        r = lax.rsqrt(jnp.mean(xv * xv, axis=-1, keepdims=True) + EPS)
        xn = xv * r
        dpre_ref[...] += jnp.sum(dh * xn, axis=0, keepdims=True)
        dxn = dh * pre_ref[...]
        dx_ref[...] = g_ref[...] + r * (dxn - xn * jnp.mean(dxn * xn, axis=-1, keepdims=True))

    row = pl.BlockSpec((tm, D_MODEL), lambda i: (i, 0))
    return _fused_call(
        body, comm, (dproj, wt, x_in, pre, g), name=name, grid=(t // tm,),
        out_shape=[jax.ShapeDtypeStruct((t, D_MODEL), F32), jax.ShapeDtypeStruct((1, D_MODEL), F32)],
        in_specs=[pl.BlockSpec((tm, n), lambda i: (i, 0)), _const_spec((n, D_MODEL)), row, _const_spec((1, D_MODEL)), row],
        out_specs=[row, pl.BlockSpec((1, D_MODEL), lambda i: (0, 0))],
        params=_params("arbitrary"))


def _group_masks():
    lane = lax.broadcasted_iota(jnp.int32, (1, GROUP * HEAD_DIM), 1)
    return [(lane // HEAD_DIM == g).astype(F32) for g in range(GROUP)]


def _stack_groups(v, masks):
    return jnp.concatenate([v * m for m in masks], axis=0)


def _unstack_groups(v, masks):
    out = v[0:BLOCK] * masks[0]
    for g in range(1, GROUP):
        out = out + v[g * BLOCK:(g + 1) * BLOCK] * masks[g]
    return out


def _repeat_head(kv2, kvh):
    first = lax.broadcasted_iota(jnp.int32, kv2.shape, 1) < HEAD_DIM
    rolled = pltpu.roll(kv2, HEAD_DIM, 1)
    one = jnp.where(first, kv2, rolled) if kvh == 0 else jnp.where(first, rolled, kv2)
    return jnp.concatenate([one, one], axis=1)


def _fold_head(v4):
    a = v4[:, 0:128] + v4[:, 128:256]
    return a + pltpu.roll(a, HEAD_DIM, 1)


ATTN_CONSTS = [pltpu.VMEM((KV_HEADS, GROUP * BLOCK, 2 * BLOCK), F32)]


def _fill_attn_bias(bias_ref):
    row = lax.broadcasted_iota(jnp.int32, (GROUP * BLOCK, 2 * BLOCK), 0)
    col = lax.broadcasted_iota(jnp.int32, (GROUP * BLOCK, 2 * BLOCK), 1)
    dist = (row % BLOCK) + BLOCK - col
    band = (dist >= 0) & (dist < BLOCK)
    rb = lax.broadcasted_iota(jnp.int32, (GROUP * BLOCK, 1), 0) // BLOCK
    for kvh in range(KV_HEADS):
        slope = jnp.zeros((GROUP * BLOCK, 1), F32)
        for g in range(GROUP):
            slope = jnp.where(rb == g, 2.0 ** (-(kvh * GROUP + g + 1)), slope)
        bias_ref[kvh] = jnp.where(band, -slope * dist.astype(F32), NEG)


def _row_sinks(kvh, sink_ref):
    rb = lax.broadcasted_iota(jnp.int32, (GROUP * BLOCK, 1), 0) // BLOCK
    sink = jnp.zeros((GROUP * BLOCK, 1), F32)
    for g in range(GROUP):
        sink = jnp.where(rb == g, sink_ref[0, kvh * GROUP + g], sink)
    return sink


def _attn_probs(qk, k4, bias, sink, no_past, masks):
    qs = _stack_groups(qk, masks).astype(BF16)
    s = lax.dot_general(qs, k4, NT, preferred_element_type=F32) * (HEAD_DIM ** -0.5) + bias
    s = jnp.concatenate([jnp.where(no_past, NEG, s[:, 0:BLOCK]), s[:, BLOCK:]], axis=1)
    mx = jnp.maximum(jnp.max(s, axis=-1, keepdims=True), sink)
    e = jnp.exp(s - mx)
    es = jnp.exp(sink - mx)
    inv = 1.0 / (jnp.sum(e, axis=-1, keepdims=True) + es)
    return qs, e * inv, es * inv


def _pool_forward(u_ext, g, t0):
    n = u_ext.shape[0] - POOL_HALO
    s = u_ext
    for step in range(g + 1):
        s = s + pltpu.roll(s, 1 << step, 0)
    w = 2 << g
    t = t0 + lax.broadcasted_iota(jnp.int32, (n, 1), 0)
    cnt = jnp.minimum(t + 1, w).astype(F32)
    return s[POOL_HALO:] / cnt - u_ext[POOL_HALO:]


def _mix0_fwd(proj, sinks, pool_w, pool_scale, comm=None, tq=512):
    t = proj.shape[0]
    nblk = tq // BLOCK

    def body(main_ref, halo_ref, sink_ref, pw_ref, ps_ref, o_ref, kv_ref, bias_ref):
        i = pl.program_id(0)
        t0 = i * tq
        masks = _group_masks()

        @pl.when(i == 0)
        def _():
            _fill_attn_bias(bias_ref)

        kv_ref[0:BLOCK, :] = halo_ref[:, COL_K:COL_K + 256]
        kv_ref[BLOCK:, :] = main_ref[:, COL_K:COL_K + 256]

        def block(jb, carry):
            r0 = pl.multiple_of(jb * BLOCK, BLOCK)
            no_past = t0 + r0 == 0
            q = main_ref[pl.ds(r0, BLOCK), COL_Q:COL_Q + 512]
            ga = main_ref[pl.ds(r0, BLOCK), COL_GA:COL_GA + 512]
            kk = kv_ref[pl.ds(r0, 2 * BLOCK), 0:128]
            vv = kv_ref[pl.ds(r0, 2 * BLOCK), 128:256]
            outs = []
            for kvh in range(KV_HEADS):
                k4 = _repeat_head(kk, kvh).astype(BF16)
                v4 = _repeat_head(vv, kvh).astype(BF16)
                _, p, _ = _attn_probs(q[:, kvh * 256:(kvh + 1) * 256], k4, bias_ref[kvh], _row_sinks(kvh, sink_ref), no_past, masks)
                pv = jnp.dot(p.astype(BF16), v4, preferred_element_type=F32)
                outs.append(_unstack_groups(pv, masks))
            attn = jnp.concatenate(outs, axis=1)
            o_ref[pl.ds(r0, BLOCK), 0:512] = (attn * _silu(ga)).astype(BF16)
            return carry

        lax.fori_loop(0, nblk, block, 0, unroll=2)

        for g in range(POOL_GROUPS):
            cu = COL_U + g * POOL_GC
            cg = COL_GB + g * POOL_GC
            halo_u = jnp.where(i == 0, 0.0, halo_ref[BLOCK - POOL_HALO:BLOCK, cu:cu + POOL_GC])
            u_ext = jnp.concatenate([halo_u, main_ref[:, cu:cu + POOL_GC]], axis=0)
            pooled = _pool_forward(u_ext, g, t0)
            y = jnp.dot(pooled.astype(BF16), pw_ref[g].astype(BF16), preferred_element_type=F32)
            y = y * ps_ref[:, g * POOL_GC:(g + 1) * POOL_GC]
            o_ref[:, 512 + g * POOL_GC:512 + (g + 1) * POOL_GC] = (y * _silu(main_ref[:, cg:cg + POOL_GC])).astype(BF16)

    return _fused_call(
        body, comm, (proj, proj, sinks, pool_w, pool_scale), name="mix0_fwd", grid=(t // tq,),
        out_shape=jax.ShapeDtypeStruct((t, D_MODEL), BF16),
        in_specs=[pl.BlockSpec((tq, EVEN_IN), lambda i: (i, 0)),
                  pl.BlockSpec((BLOCK, EVEN_IN), lambda i: (jnp.maximum(i * nblk - 1, 0), 0)),
                  pl.BlockSpec(memory_space=pltpu.SMEM),
                  _const_spec((POOL_GROUPS, POOL_GC, POOL_GC)), _const_spec((1, 512))],
        out_specs=pl.BlockSpec((tq, D_MODEL), lambda i: (i, 0)),
        scratch_shapes=[pltpu.VMEM((tq + BLOCK, 256), F32)] + ATTN_CONSTS,
        params=_params("arbitrary"))


def _mix0_bwd(proj, dmix, sinks, pool_w, pool_scale, comm=None, tq=512):
    t = proj.shape[0]
    nt = t // tq
    nblk = tq // BLOCK

    def body(main_ref, halo_ref, next_ref, dm_ref, dmn_ref, sink_ref, pw_ref, ps_ref,
             o_ref, dsk_ref, dpw_ref, dps_ref, kv_ref, dkv_ref, carry_ref, bias_ref):
        i = pl.program_id(0)
        ii = nt - 1 - i
        t0 = ii * tq
        masks = _group_masks()

        @pl.when(i == 0)
        def _():
            _fill_attn_bias(bias_ref)
            dsk_ref[...] = jnp.zeros_like(dsk_ref)
            dpw_ref[...] = jnp.zeros_like(dpw_ref)
            dps_ref[...] = jnp.zeros_like(dps_ref)
            carry_ref[...] = jnp.zeros_like(carry_ref)

        kv_ref[0:BLOCK, :] = halo_ref[:, COL_K:COL_K + 256]
        kv_ref[BLOCK:, :] = main_ref[:, COL_K:COL_K + 256]
        dkv_ref[0:tq, :] = jnp.zeros((tq, 256), F32)
        dkv_ref[tq:, :] = carry_ref[...]

        def block(jb, carry):
            r0 = pl.multiple_of(jb * BLOCK, BLOCK)
            no_past = t0 + r0 == 0
            q = main_ref[pl.ds(r0, BLOCK), COL_Q:COL_Q + 512]
            ga = main_ref[pl.ds(r0, BLOCK), COL_GA:COL_GA + 512]
            dya = dm_ref[pl.ds(r0, BLOCK), 0:512]
            kk = kv_ref[pl.ds(r0, 2 * BLOCK), 0:128]
            vv = kv_ref[pl.ds(r0, 2 * BLOCK), 128:256]
            silu_ga, dsilu_ga = _silu_and_grad(ga)
            do = dya * silu_ga
            first = lax.broadcasted_iota(jnp.int32, (2 * BLOCK, 128), 1) < HEAD_DIM
            attn, dq, dk, dv = [], [], [], []
            for kvh in range(KV_HEADS):
                k4 = _repeat_head(kk, kvh).astype(BF16)
                v4 = _repeat_head(vv, kvh).astype(BF16)
                qs, p, ps = _attn_probs(q[:, kvh * 256:(kvh + 1) * 256], k4, bias_ref[kvh], _row_sinks(kvh, sink_ref), no_past, masks)
                pb = p.astype(BF16)
                o_k = _unstack_groups(jnp.dot(pb, v4, preferred_element_type=F32), masks)
                do_k = do[:, kvh * 256:(kvh + 1) * 256]
                dos = _stack_groups(do_k, masks).astype(BF16)
                prod = do_k * o_k
                delta = jnp.concatenate([jnp.sum(prod * m, axis=-1, keepdims=True) for m in masks], axis=0)
                dp = lax.dot_general(dos, v4, NT, preferred_element_type=F32)
                ds = (p * (dp - delta)).astype(BF16)
                sink_term = ps * delta
                for g in range(GROUP):
                    h = kvh * GROUP + g
                    dsk_ref[h:h + 1, :] -= jnp.sum(sink_term[g * BLOCK:(g + 1) * BLOCK], keepdims=True)
                scale = HEAD_DIM ** -0.5
                dq.append(_unstack_groups(jnp.dot(ds, k4, preferred_element_type=F32), masks) * scale)
                dk.append(_fold_head(lax.dot_general(ds, qs, TN, preferred_element_type=F32)) * scale)
                dv.append(_fold_head(lax.dot_general(pb, dos, TN, preferred_element_type=F32)))
                attn.append(o_k)
            o_ref[pl.ds(r0, BLOCK), COL_Q:COL_Q + 512] = jnp.concatenate(dq, axis=1).astype(BF16)
            o_all = jnp.concatenate(attn, axis=1)
            o_ref[pl.ds(r0, BLOCK), COL_GA:COL_GA + 512] = (dya * o_all * dsilu_ga).astype(BF16)
            dkv = jnp.concatenate([jnp.where(first, dk[0], dk[1]), jnp.where(first, dv[0], dv[1])], axis=1)
            dkv_ref[pl.ds(r0, 2 * BLOCK), :] += dkv
            return carry

        lax.fori_loop(0, nblk, block, 0, unroll=2)
        carry_ref[...] = dkv_ref[0:BLOCK, :]
        o_ref[:, COL_K:COL_K + 256] = dkv_ref[BLOCK:, :].astype(BF16)

        last = ii == nt - 1
        for g in range(POOL_GROUPS):
            cu = COL_U + g * POOL_GC
            cg = COL_GB + g * POOL_GC
            cm = 512 + g * POOL_GC
            pw = pw_ref[g].astype(BF16)
            sc = ps_ref[:, g * POOL_GC:(g + 1) * POOL_GC]
            halo_u = jnp.where(ii == 0, 0.0, halo_ref[BLOCK - POOL_HALO:BLOCK, cu:cu + POOL_GC])
            u_ext = jnp.concatenate([halo_u, main_ref[:, cu:cu + POOL_GC]], axis=0)
            pooled = _pool_forward(u_ext, g, t0).astype(BF16)
            y_raw = jnp.dot(pooled, pw, preferred_element_type=F32)
            gb = main_ref[:, cg:cg + POOL_GC]
            dyb = dm_ref[:, cm:cm + POOL_GC]
            silu_gb, dsilu_gb = _silu_and_grad(gb)
            dypool = dyb * silu_gb
            dps_ref[:, g * POOL_GC:(g + 1) * POOL_GC] += jnp.sum(dypool * y_raw, axis=0, keepdims=True)
            o_ref[:, cg:cg + POOL_GC] = (dyb * (y_raw * sc) * dsilu_gb).astype(BF16)
            dyraw = dypool * sc
            dyraw_n = jnp.where(last, 0.0, dmn_ref[:, cm:cm + POOL_GC] * _silu(next_ref[:, cg:cg + POOL_GC]) * sc)
            dpw_ref[g * POOL_GC:(g + 1) * POOL_GC, :] += lax.dot_general(pooled, dyraw.astype(BF16), TN,
                                                                         preferred_element_type=F32)
            dyraw_ext = jnp.concatenate([dyraw, dyraw_n], axis=0).astype(BF16)
            dpooled = lax.dot_general(dyraw_ext, pw, NT, preferred_element_type=F32)
            w = 2 << g
            tt = t0 + lax.broadcasted_iota(jnp.int32, (tq + POOL_HALO, 1), 0)
            s = dpooled / jnp.minimum(tt + 1, w).astype(F32)
            for step in range(g + 1):
                s = s + pltpu.roll(s, tq + POOL_HALO - (1 << step), 0)
            o_ref[:, cu:cu + POOL_GC] = (s[0:tq] - dpooled[0:tq]).astype(BF16)

    rev = lambda i: nt - 1 - i
    return _fused_call(
        body, comm, (proj, proj, proj, dmix, dmix, sinks, pool_w, pool_scale), name="mix0_bwd", grid=(nt,),
        out_shape=[jax.ShapeDtypeStruct((t, EVEN_IN), BF16), jax.ShapeDtypeStruct((8, 128), F32),
                   jax.ShapeDtypeStruct((POOL_GROUPS * POOL_GC, POOL_GC), F32), jax.ShapeDtypeStruct((1, 512), F32)],
        in_specs=[pl.BlockSpec((tq, EVEN_IN), lambda i: (rev(i), 0)),
                  pl.BlockSpec((BLOCK, EVEN_IN), lambda i: (jnp.maximum(rev(i) * nblk - 1, 0), 0)),
                  pl.BlockSpec((POOL_HALO, EVEN_IN),
                               lambda i: (jnp.minimum((rev(i) + 1) * (tq // POOL_HALO), t // POOL_HALO - 1), 0)),
                  pl.BlockSpec((tq, D_MODEL), lambda i: (rev(i), 0)),
                  pl.BlockSpec((POOL_HALO, D_MODEL),
                               lambda i: (jnp.minimum((rev(i) + 1) * (tq // POOL_HALO), t // POOL_HALO - 1), 0)),
                  pl.BlockSpec(memory_space=pltpu.SMEM),
                  _const_spec((POOL_GROUPS, POOL_GC, POOL_GC)), _const_spec((1, 512))],
        out_specs=[pl.BlockSpec((tq, EVEN_IN), lambda i: (rev(i), 0)),
                   pl.BlockSpec((8, 128), lambda i: (0, 0)),
                   pl.BlockSpec((POOL_GROUPS * POOL_GC, POOL_GC), lambda i: (0, 0)),
                   pl.BlockSpec((1, 512), lambda i: (0, 0))],
        scratch_shapes=[pltpu.VMEM((tq + BLOCK, 256), F32), pltpu.VMEM((tq + BLOCK, 256), F32),
                        pltpu.VMEM((BLOCK, 256), F32)] + ATTN_CONSTS,
        params=_params("arbitrary"))


CONV_RC = 32
CONV_CC = 128
CONV_CHAINS = 4
CONV_UNROLL = 2


def _fill_shifted(s_ref, rows):
    for b in range(1, 8):
        s_ref[b, 0:rows - 8, :] = s_ref[0, b:b + rows - 8, :]


def _tap_blocks(s_ref, r, cols, lead):
    for b in range(8):
        taps = [(a, 8 * a + b - lead) for a in range(5) if 0 <= 8 * a + b - lead < CONV_K]
        span = 8 * max(a for a, _ in taps) + CONV_RC
        blk = s_ref[b, pl.ds(r, span), cols]
        for a, k in taps:
            yield k, blk[8 * a:8 * a + CONV_RC]


def _conv_taps(s_ref, w_ref, r, cols, lead, reverse):
    accs = [None] * CONV_CHAINS
    for n, (k, blk) in enumerate(_tap_blocks(s_ref, r, cols, lead)):
        kw = CONV_K - 1 - k if reverse else k
        term = blk * w_ref[kw:kw + 1, cols]
        accs[n % CONV_CHAINS] = term if accs[n % CONV_CHAINS] is None else accs[n % CONV_CHAINS] + term
    return (accs[0] + accs[1]) + (accs[2] + accs[3])


def _layer_norm_fwd(cf, lng, lnb):
    mu = jnp.mean(cf, axis=-1, keepdims=True)
    xc = cf - mu
    rstd = lax.rsqrt(jnp.mean(xc * xc, axis=-1, keepdims=True) + EPS)
    chat = xc * rstd
    return chat, rstd, chat * lng + lnb


def _conv_fwd(proj, dw, dwb, lng, lnb, tt=256):
    t = proj.shape[0]
    lead = CONV_HALO - (CONV_K - 1)

    def body(main_ref, halo_ref, w_ref, b_ref, g_ref, lb_ref, o_ref, c_ref, gs_ref):
        i = pl.program_id(0)
        hv = halo_ref[...]
        gs_ref[0, 0:CONV_HALO, :] = jnp.where(i == 0, 0.0, hv[:, 0:1024] * _sigmoid(hv[:, 1024:2048]))
        gs_ref[0, CONV_HALO:CONV_HALO + tt, :] = main_ref[:, 0:1024] * _sigmoid(main_ref[:, 1024:2048])
        _fill_shifted(gs_ref, tt + CONV_HALO)

        for c in range(D_MODEL // CONV_CC):
            cols = slice(c * CONV_CC, (c + 1) * CONV_CC)

            def chunk(j, carry):
                r = pl.multiple_of(j * CONV_RC, CONV_RC)
                c_ref[pl.ds(r, CONV_RC), cols] = _conv_taps(gs_ref, w_ref, r, cols, lead, False) + b_ref[:, cols]
                return carry
            lax.fori_loop(0, tt // CONV_RC, chunk, 0, unroll=CONV_UNROLL)

        _, _, cn = _layer_norm_fwd(c_ref[...], g_ref[...], lb_ref[...])
        o_ref[...] = (_silu(cn) * _silu(main_ref[:, 2048:3072])).astype(BF16)

    vec = _const_spec((1, D_MODEL))
    row = pl.BlockSpec((tt, D_MODEL), lambda i: (i, 0))
    return pl.pallas_call(
        body, name="conv_fwd", grid=(t // tt,),
        out_shape=[jax.ShapeDtypeStruct((t, D_MODEL), BF16), jax.ShapeDtypeStruct((t, D_MODEL), F32)],
        in_specs=[pl.BlockSpec((tt, 3 * D_MODEL), lambda i: (i, 0)),
                  pl.BlockSpec((CONV_HALO, 3 * D_MODEL), lambda i: (jnp.maximum(i * (tt // CONV_HALO) - 1, 0), 0)),
                  _const_spec((CONV_K, D_MODEL)), vec, vec, vec],
        out_specs=[row, row],
        scratch_shapes=[pltpu.VMEM((8, tt + CONV_HALO, D_MODEL), F32)],
        compiler_params=_params("parallel"),
    )(proj, proj, dw, dwb, lng, lnb)


def _conv_bwd(proj, cf, dz, dw, lng, lnb, comm=None, tt=256):
    t = proj.shape[0]
    nt = t // tt
    te = tt + CONV_HALO

    def body(main_ref, next_ref, cf_ref, cfn_ref, dz_ref, dzn_ref, w_ref, g_ref, lb_ref,
             o_ref, ddw_ref, ddb_ref, dg_ref, dlb_ref, ds_ref, glu_ref, sb_ref):
        i = pl.program_id(0)

        @pl.when(i == 0)
        def _():
            ddw_ref[...] = jnp.zeros_like(ddw_ref)
            ddb_ref[...] = jnp.zeros_like(ddb_ref)
            dg_ref[...] = jnp.zeros_like(dg_ref)
            dlb_ref[...] = jnp.zeros_like(dlb_ref)

        lng = g_ref[...]
        chat, rstd, cn = _layer_norm_fwd(jnp.concatenate([cf_ref[...], cfn_ref[...]], axis=0), lng, lb_ref[...])
        gate = jnp.concatenate([main_ref[:, 2048:3072], next_ref[:, 2048:3072]], axis=0)
        dzv = jnp.concatenate([dz_ref[...], dzn_ref[...]], axis=0)
        rows = lax.broadcasted_iota(jnp.int32, (te, 1), 0)
        live = (rows < tt) | (i < nt - 1)
        own = (rows < tt).astype(F32)
        dzv = jnp.where(live, dzv, 0.0)
        silu_cn, dsilu_cn = _silu_and_grad(cn)
        silu_gate, dsilu_gate = _silu_and_grad(gate)
        o_ref[:, 2048:3072] = (dzv * silu_cn * dsilu_gate)[0:tt].astype(BF16)
        dcn = dzv * silu_gate * dsilu_cn
        dg_ref[...] += jnp.sum(dcn * chat * own, axis=0, keepdims=True)
        dlb_ref[...] += jnp.sum(dcn * own, axis=0, keepdims=True)
        dchat = dcn * lng
        dcf = rstd * (dchat - jnp.mean(dchat, axis=-1, keepdims=True) - chat * jnp.mean(dchat * chat, axis=-1, keepdims=True))
        ddb_ref[...] += jnp.sum(dcf * own, axis=0, keepdims=True)
        ds_ref[0, 0:te, :] = dcf
        ds_ref[0, te:, :] = jnp.zeros((8, D_MODEL), F32)
        _fill_shifted(ds_ref, te + 8)
        sb_ref[...] = _sigmoid(main_ref[:, 1024:2048])
        glu_ref[...] = main_ref[:, 0:1024] * sb_ref[...]

        for c in range(D_MODEL // CONV_CC):
            cols = slice(c * CONV_CC, (c + 1) * CONV_CC)
            gcols = slice(c * CONV_CC + 1024, (c + 1) * CONV_CC + 1024)

            def chunk(j, carry):
                r = pl.multiple_of(j * CONV_RC, CONV_RC)
                dglu = _conv_taps(ds_ref, w_ref, r, cols, 0, True)
                sb = sb_ref[pl.ds(r, CONV_RC), cols]
                o_ref[pl.ds(r, CONV_RC), cols] = (dglu * sb).astype(BF16)
                o_ref[pl.ds(r, CONV_RC), gcols] = (dglu * glu_ref[pl.ds(r, CONV_RC), cols] * (1.0 - sb)).astype(BF16)
                return carry
            lax.fori_loop(0, tt // CONV_RC, chunk, 0, unroll=CONV_UNROLL)

            def taps(j, accs):
                r = pl.multiple_of(j * CONV_RC, CONV_RC)
                gl = glu_ref[pl.ds(r, CONV_RC), cols]
                new = list(accs)
                for m, blk in _tap_blocks(ds_ref, r, cols, 0):
                    prod = blk * gl
                    part = prod[0:8]
                    for q in range(1, CONV_RC // 8):
                        part = part + prod[8 * q:8 * q + 8]
                    new[m] = new[m] + part
                return tuple(new)
            accs = lax.fori_loop(0, tt // CONV_RC, taps, tuple(jnp.zeros((8, CONV_CC), F32) for _ in range(CONV_K)))
            for m in range(CONV_K):
                k = CONV_K - 1 - m
                ddw_ref[k:k + 1, cols] += jnp.sum(accs[m], axis=0, keepdims=True)

    vec = _const_spec((1, D_MODEL))
    vec_out = pl.BlockSpec((1, D_MODEL), lambda i: (0, 0))
    row = pl.BlockSpec((tt, D_MODEL), lambda i: (i, 0))
    nxt = lambda i: (jnp.minimum((i + 1) * (tt // CONV_HALO), t // CONV_HALO - 1), 0)
    nxt_row = pl.BlockSpec((CONV_HALO, D_MODEL), nxt)
    return _fused_call(
        body, comm, (proj, proj, cf, cf, dz, dz, dw, lng, lnb), name="conv_bwd", grid=(nt,),
        out_shape=[jax.ShapeDtypeStruct((t, 3 * D_MODEL), BF16), jax.ShapeDtypeStruct((CONV_K, D_MODEL), F32),
                   jax.ShapeDtypeStruct((1, D_MODEL), F32), jax.ShapeDtypeStruct((1, D_MODEL), F32),
                   jax.ShapeDtypeStruct((1, D_MODEL), F32)],
        in_specs=[pl.BlockSpec((tt, 3 * D_MODEL), lambda i: (i, 0)),
                  pl.BlockSpec((CONV_HALO, 3 * D_MODEL), nxt),
                  row, nxt_row, row, nxt_row,
                  _const_spec((CONV_K, D_MODEL)), vec, vec],
        out_specs=[pl.BlockSpec((tt, 3 * D_MODEL), lambda i: (i, 0)),
                   pl.BlockSpec((CONV_K, D_MODEL), lambda i: (0, 0)), vec_out, vec_out, vec_out],
        scratch_shapes=[pltpu.VMEM((8, te + 8, D_MODEL), F32), pltpu.VMEM((tt, D_MODEL), F32),
                        pltpu.VMEM((tt, D_MODEL), F32)],
        params=_params("arbitrary"))


def _piece_sum(own, recv, place, name):
    r = own.shape[1]

    def body(p_ref, *refs):
        o_ref = refs[-1]
        acc = refs[0][0].astype(F32)
        for part in refs[1:-1]:
            acc = acc + part[0].astype(F32)
        o_ref[0] = acc

    blk = (1, r, D_MODEL)
    peer = lambda m: pl.BlockSpec(blk, lambda j, p_ref: (p_ref[0] ^ m, 0, 0))
    return pl.pallas_call(
        body, name=name,
        grid_spec=pltpu.PrefetchScalarGridSpec(
            num_scalar_prefetch=1, grid=(1,),
            in_specs=[peer(0)] + [peer(m) for m in range(1, N_DEV)],
            out_specs=pl.BlockSpec(blk, lambda j, p_ref: (p_ref[1], 0, 0))),
        out_shape=jax.ShapeDtypeStruct((2, r, D_MODEL), F32),
        compiler_params=_params("arbitrary"),
    )(place, own, *([recv] * (N_DEV - 1)))


def _share_with_sibling(halves, name):
    n = len(halves)

    def body(*refs):
        outs = refs[n:2 * n]
        send_sems, recv_sems = refs[2 * n:]
        x, y, c, _ = _place()
        send = [pltpu.make_async_remote_copy(
            src_ref=outs[t].at[c], dst_ref=outs[t].at[c], send_sem=send_sems.at[t], recv_sem=recv_sems.at[t],
            device_id=(x, y, 1 - c), device_id_type=MESH_ID) for t in range(n)]
        recv = [pltpu.make_async_remote_copy(
            src_ref=outs[t].at[c], dst_ref=outs[t].at[1 - c], send_sem=send_sems.at[t], recv_sem=recv_sems.at[t],
            device_id=(x, y, 1 - c), device_id_type=MESH_ID) for t in range(n)]
        for cp in send:
            cp.start()
        for cp in recv:
            cp.wait_recv()
        for cp in send:
            cp.wait_send()

    return pl.pallas_call(
        body, name=name,
        out_shape=[jax.ShapeDtypeStruct(h.shape, h.dtype) for h in halves],
        in_specs=[ANY] * n, out_specs=[ANY] * n,
        input_output_aliases={t: t for t in range(n)},
        scratch_shapes=[pltpu.SemaphoreType.DMA((n,)), pltpu.SemaphoreType.DMA((n,))],
    )(*halves)


def _sum8(parts, name):
    r = parts.shape[1]

    def body(p_ref, o_ref):
        acc = p_ref[0]
        for k in range(1, N_DEV):
            acc = acc + p_ref[k]
        o_ref[...] = acc

    return pl.pallas_call(
        body, name=name, out_shape=jax.ShapeDtypeStruct((r, 128), F32),
        in_specs=[pl.BlockSpec(memory_space=pltpu.VMEM)], out_specs=pl.BlockSpec(memory_space=pltpu.VMEM),
    )(parts)


def _adamw(w, g, m, v, name):
    shape = w.shape
    cols = shape[-1]
    rows = w.size // cols
    rt = 256 if rows % 256 == 0 else rows

    def body(w_ref, g_ref, m_ref, v_ref, d_ref, nm_ref, nv_ref):
        gv = g_ref[...]
        mn = ADAM_B1 * m_ref[...] + (1.0 - ADAM_B1) * gv
        vn = ADAM_B2 * v_ref[...] + (1.0 - ADAM_B2) * (gv * gv)
        m_hat = mn / (1.0 - ADAM_B1 ** ADAM_STEP)
        v_hat = vn / (1.0 - ADAM_B2 ** ADAM_STEP)
        d_ref[...] = -ADAM_LR * (m_hat / (jnp.sqrt(v_hat) + ADAM_EPS) + ADAM_WD * w_ref[...])
        nm_ref[...] = mn
        nv_ref[...] = vn

    spec = pl.BlockSpec((rt, cols), lambda i: (i, 0))
    outs = pl.pallas_call(
        body, name=name, grid=(rows // rt,),
        out_shape=[jax.ShapeDtypeStruct((rows, cols), F32)] * 3,
        in_specs=[spec] * 4, out_specs=[spec] * 3,
        compiler_params=_params("parallel"),
    )(*[a.reshape(rows, cols) for a in (w, g, m, v)])
    return [o.reshape(shape) for o in outs]


SMALL_ROWS = 824


def _pack_small(g):
    parts = [g["pre1"].reshape(8, 128), g["post0"].reshape(8, 128),
             g["post1"].reshape(8, 128), g["sinks"], jnp.pad(g["pool_scale"].reshape(4, 128), ((0, 4), (0, 0))),
             g["pool_w"], g["dw"].reshape(248, 128), g["dwb"].reshape(8, 128), g["lng"].reshape(8, 128),
             g["lnb"].reshape(8, 128)]
    assert sum(p.shape[0] for p in parts) == SMALL_ROWS
    return jnp.concatenate(parts, axis=0)


def _unpack_small(s):
    out, r = {}, 0
    for key, rows, shape in (("pre1", 8, (1, D_MODEL)), ("post", 16, (2, D_MODEL)), ("sinks", 8, (8, 128)),
                             ("pool_scale", 4, (1, 512)), ("pad", 4, (4, 128)), ("pool_w", 512, (1, 4, 128, 128)),
                             ("dw", 248, (CONV_K, D_MODEL)), ("dwb", 8, (1, D_MODEL)), ("lng", 8, (1, D_MODEL)),
                             ("lnb", 8, (1, D_MODEL))):
        out[key] = s[r:r + rows].reshape(shape)
        r += rows
    return out


def kernel(x, pre_norm, post_norm, a_w_in, a_sinks, b_pool_w, b_pool_scale, ab_w_out, c_w_in, c_dw_w, c_dw_b, c_ln_g, c_ln_b, c_w_out, loss_target, m_pre_norm, m_post_norm, m_a_w_in, m_a_sinks, m_b_pool_w, m_b_pool_scale, m_ab_w_out, m_c_w_in, m_c_dw_w, m_c_dw_b, m_c_ln_g, m_c_ln_b, m_c_w_out, v_pre_norm, v_post_norm, v_a_w_in, v_a_sinks, v_b_pool_w, v_b_pool_scale, v_ab_w_out, v_c_w_in, v_c_dw_w, v_c_dw_b, v_c_ln_g, v_c_ln_b, v_c_w_out):
    ix, iy = lax.axis_index("x"), lax.axis_index("y")
    chip_cols = (2 * ix + iy) * 256

    pad8 = lambda v: jnp.pad(v, ((0, -v.shape[0] % 8), (0, 0)))
    vec_shard = jnp.concatenate([pad8(c_dw_w.reshape(CONV_K, 256)), pad8(c_dw_b), pad8(c_ln_g), pad8(c_ln_b),
                                 jnp.zeros((8, 256), F32)], axis=0)
    x0, target = x[0], loss_target[0]
    pre0, pre1 = pre_norm[0:1], pre_norm[1:2]
    post0, post1 = post_norm[0:1], post_norm[1:2]
    pool_w = b_pool_w[0]

    (wa_t,) = _run_comm(_Gather([a_w_in[0].T.astype(BF16)], halve=True), "gather_a_w_in")
    wa_t = wa_t.reshape(EVEN_IN, D_MODEL)
    proj0, (w_ab,) = _norm_matmul(x0, pre0, wa_t, "proj0_fwd", comm=_Gather([ab_w_out[0].astype(BF16)], halve=True))
    w_ab = w_ab.reshape(D_MODEL, D_MODEL)
    mix0, (wc_t, w_c, vecs) = _mix0_fwd(
        proj0, a_sinks, pool_w, b_pool_scale,
        comm=_Gather([c_w_in[0].T.astype(BF16), c_w_out[0].astype(BF16), vec_shard], halve=True))
    wc_t = wc_t.reshape(3 * D_MODEL, D_MODEL)
    w_c = w_c.reshape(D_MODEL, D_MODEL)
    vecs = vecs.reshape(4, 64, 256).transpose(1, 0, 2).reshape(64, D_MODEL)
    dw, dwb, lng, lnb = vecs[0:CONV_K], vecs[32:33], vecs[40:41], vecs[48:49]
    y0, x1 = _out_norm_res(mix0, w_ab, x0, post0, "out0_fwd")
    proj1, _ = _norm_matmul(x1, pre1, wc_t, "proj1_fwd")
    z1, cf1 = _conv_fwd(proj1, dw, dwb, lng, lnb)
    y1, g2, loss = _out_norm_res(z1, w_c, x1, post1, "out1_fwd", target=target)
    loss = lax.psum(loss[0, 0], ("x", "y", "c"))

    pieces = lambda m: m.reshape(N_DEV, m.shape[0] // N_DEV, D_MODEL)
    dz1, d_wc, d_post1 = _post_bwd(g2, y1, post1, w_c, z1, "out1_bwd")
    (dproj1, d_dw, d_dwb, d_lng, d_lnb), (r_wc,) = _conv_bwd(proj1, cf1, dz1, dw, lng, lnb,
                                                             comm=_Scatter([pieces(d_wc)]))
    g1, d_wct, d_pre1 = _pre_bwd(dproj1, wc_t, x1, pre1, g2, "proj1_bwd")
    dmix0, d_wab, d_post0 = _post_bwd(g1, y0, post0, w_ab, mix0, "out0_bwd")
    (dproj0, d_sinks, d_pw, d_ps), (r_wct, r_wab) = _mix0_bwd(proj0, dmix0, a_sinks, pool_w, b_pool_scale,
                                                             comm=_Scatter([pieces(d_wct), pieces(d_wab)]))
    d_wat = _proj_dw(dproj0, x0, pre0, "proj0_dw")
    g = dict(pre1=d_pre1, post0=d_post0, post1=d_post1, sinks=d_sinks, pool_w=d_pw, pool_scale=d_ps,
             dw=d_dw, dwb=d_dwb, lng=d_lng, lnb=d_lnb)
    (gx, d_pre0), (r_wat, small8) = _proj_dx(dproj0, wa_t, x0, pre0, g1, "proj0_dx",
                                            comm=_Comms(_Scatter([pieces(d_wat)]), _Gather([_pack_small(g)], halve=False)))
    (pre0_8,) = _run_comm(_Gather([d_pre0.reshape(8, 128)], halve=False), "pre0_all_gather")

    ic = lax.axis_index("c")
    place = jnp.stack([4 * ix + 2 * iy + ic, ic]).astype(jnp.int32)
    sent_recv = ((d_wat, r_wat), (d_wab, r_wab), (d_wct, r_wct), (d_wc, r_wc))
    halves = [_piece_sum(pieces(own), recv, place, f"grad_sum{t}") for t, (own, recv) in enumerate(sent_recv)]
    g_wa_t, g_wab, g_wc_t, g_wc = [h.reshape(2 * h.shape[1], D_MODEL) for h in _share_with_sibling(halves, "grad_share")]
    g_a_w_in = g_wa_t.T[None]
    g_c_w_in = g_wc_t.T[None]
    g_ab_w_out = g_wab[None]
    g_c_w_out = g_wc[None]

    s = _unpack_small(_sum8(small8, "small_sum"))
    layer = lax.broadcasted_iota(jnp.int32, (2, D_MODEL), 0)
    g_pre = jnp.where(layer == 0, _sum8(pre0_8, "pre0_sum").reshape(1, D_MODEL), s["pre1"])
    g_post = s["post"]
    g_sinks = s["sinks"][:, 0].reshape(1, 8)
    g_pool_w, g_pool_scale = s["pool_w"], s["pool_scale"]
    g_dw = lax.dynamic_slice_in_dim(s["dw"], chip_cols, 256, axis=1).reshape(1, CONV_K, 1, 256)
    g_dwb = lax.dynamic_slice_in_dim(s["dwb"], chip_cols, 256, axis=1)
    g_lng = lax.dynamic_slice_in_dim(s["lng"], chip_cols, 256, axis=1)
    g_lnb = lax.dynamic_slice_in_dim(s["lnb"], chip_cols, 256, axis=1)

    grads = [g_pre, g_post, g_a_w_in, g_sinks, g_pool_w, g_pool_scale, g_ab_w_out, g_c_w_in, g_dw, g_dwb, g_lng, g_lnb,
             g_c_w_out]
    weights = [pre_norm, post_norm, a_w_in, a_sinks, b_pool_w, b_pool_scale, ab_w_out, c_w_in, c_dw_w, c_dw_b, c_ln_g,
               c_ln_b, c_w_out]
    moms = [m_pre_norm, m_post_norm, m_a_w_in, m_a_sinks, m_b_pool_w, m_b_pool_scale, m_ab_w_out, m_c_w_in, m_c_dw_w,
            m_c_dw_b, m_c_ln_g, m_c_ln_b, m_c_w_out]
    vars_ = [v_pre_norm, v_post_norm, v_a_w_in, v_a_sinks, v_b_pool_w, v_b_pool_scale, v_ab_w_out, v_c_w_in, v_c_dw_w,
             v_c_dw_b, v_c_ln_g, v_c_ln_b, v_c_w_out]
    deltas, new_m, new_v = [], [], []
    for k, (w, gr, m, v) in enumerate(zip(weights, grads, moms, vars_)):
        d, nm, nv = _adamw(w, gr, m, v, f"adamw{k}")
        deltas.append(d)
        new_m.append(nm)
        new_v.append(nv)
    return (loss, gx[None], *grads, *deltas, *new_m, *new_v)
```

```python
import functools

import jax
import jax.numpy as jnp
from jax import lax
from jax.experimental import pallas as pl
from jax.experimental.pallas import tpu as pltpu

F32 = jnp.float32
BF16 = jnp.bfloat16

D_MODEL = 1024
EPS = 1e-6
NEG = -1e30
HEAD_DIM = 64
GROUP = 4
KV_HEADS = 2
BLOCK = 128
EVEN_IN = 2304
COL_Q, COL_K, COL_GA, COL_U, COL_GB = 0, 512, 768, 1280, 1792
POOL_GROUPS = 4
POOL_GC = 128
POOL_HALO = 16
CONV_K = 31
CONV_HALO = 32
N_DEV = 8

ADAM_LR = 0.001
ADAM_B1 = 0.9
ADAM_B2 = 0.999
ADAM_EPS = 1e-08
ADAM_WD = 0.01
ADAM_STEP = 10

VMEM_LIMIT_BYTES = 56 * 1024 * 1024

NT = (((1,), (1,)), ((), ()))
TN = (((0,), (0,)), ((), ()))
MESH_ID = pl.DeviceIdType.MESH


def _params(*sem):
    return pltpu.CompilerParams(dimension_semantics=sem, vmem_limit_bytes=VMEM_LIMIT_BYTES)


def _const_spec(shape):
    nd = len(shape)
    return pl.BlockSpec(shape, lambda *_: (0,) * nd, pipeline_mode=pl.Buffered(1))


def _sigmoid(v):
    return 0.5 * jnp.tanh(0.5 * v) + 0.5


def _silu(v):
    return v * _sigmoid(v)


def _silu_and_grad(v):
    s = _sigmoid(v)
    return v * s, s * (1.0 + v * (1.0 - s))


ANY = pl.BlockSpec(memory_space=pl.ANY)


def _place():
    x, y, c = lax.axis_index("x"), lax.axis_index("y"), lax.axis_index("c")
    chips = [(1 - x, y), (x, 1 - y), (1 - x, 1 - y)]
    return x, y, c, chips


class _Gather:
    def __init__(self, blocks, halve):
        self.ins = list(blocks)
        self.halve = halve
        self.n = n = len(blocks)
        self.shapes = [((b.shape[0] // 2) if halve else b.shape[0], b.shape[1]) for b in blocks]
        self.out_shape = [jax.ShapeDtypeStruct((N_DEV, r, cols), b.dtype) for (r, cols), b in zip(self.shapes, blocks)]
        self.scratch = [pltpu.SemaphoreType.DMA((7 * n,)), pltpu.SemaphoreType.DMA((7 * n,)),
                        pltpu.SemaphoreType.DMA((n,))]

    def _copies(self, ins, outs, sems):
        send_sems, recv_sems, local_sems = sems
        x, y, c, chips = _place()
        me, sibling = (x, y, c), (x, y, 1 - c)

        def piece(t, px, py, pc):
            return outs[t].at[4 * px + 2 * py + pc]

        def own(t):
            return ins[t].at[pl.ds(c * self.shapes[t][0], self.shapes[t][0])] if self.halve else ins[t]

        def copy(t, k, block, to, src=None):
            return pltpu.make_async_remote_copy(
                src_ref=piece(t, *block) if src is None else src, dst_ref=piece(t, *block),
                send_sem=send_sems.at[7 * t + k], recv_sem=recv_sems.at[7 * t + k],
                device_id=to, device_id_type=MESH_ID)

        rng = range(self.n)
        return dict(
            mine=[pltpu.make_async_copy(own(t), piece(t, *me), local_sems.at[t]) for t in rng],
            first=[copy(t, 0, me, sibling, src=own(t)) for t in rng]
            + [copy(t, 1 + j, me, (*chip, c), src=own(t)) for t in rng for j, chip in enumerate(chips)],
            landed=[copy(t, 1 + j, (*chip, c), me) for j, chip in enumerate(chips) for t in rng],
            passed=[copy(t, 4 + j, (*chip, c), sibling) for j, chip in enumerate(chips) for t in rng],
            from_sibling=[copy(t, 0, sibling, me) for t in rng]
            + [copy(t, 4 + j, (*chip, 1 - c), me) for t in rng for j, chip in enumerate(chips)])

    def start(self, ins, outs, sems):
        d = self._copies(ins, outs, sems)
        for cp in d["mine"] + d["first"]:
            cp.start()

    def middle(self, ins, outs, sems):
        d = self._copies(ins, outs, sems)
        for got, fwd in zip(d["landed"], d["passed"]):
            got.wait_recv()
            fwd.start()

    def finish(self, ins, outs, sems):
        d = self._copies(ins, outs, sems)
        for cp in d["from_sibling"]:
            cp.wait_recv()
        for cp in d["first"] + d["passed"]:
            cp.wait_send()
        for cp in d["mine"]:
            cp.wait()


class _Scatter:
    def __init__(self, tensors):
        self.ins = list(tensors)
        self.n = n = len(tensors)
        self.out_shape = [jax.ShapeDtypeStruct(t.shape, t.dtype) for t in tensors]
        self.scratch = [pltpu.SemaphoreType.DMA((7 * n,)), pltpu.SemaphoreType.DMA((7 * n,))]

    def _copies(self, ins, outs, sems):
        send_sems, recv_sems = sems
        x, y, c, _ = _place()
        me = 4 * x + 2 * y + c
        sends, recvs = [], []
        for t in range(self.n):
            for m in range(1, N_DEV):
                px, py, pc = x ^ (m >> 2), y ^ ((m >> 1) & 1), c ^ (m & 1)
                q = 4 * px + 2 * py + pc
                sems_k = dict(send_sem=send_sems.at[7 * t + m - 1], recv_sem=recv_sems.at[7 * t + m - 1],
                              device_id=(px, py, pc), device_id_type=MESH_ID)
                sends.append(pltpu.make_async_remote_copy(src_ref=ins[t].at[q], dst_ref=outs[t].at[me], **sems_k))
                recvs.append(pltpu.make_async_remote_copy(src_ref=ins[t].at[me], dst_ref=outs[t].at[q], **sems_k))
        return sends, recvs

    def start(self, ins, outs, sems):
        for cp in self._copies(ins, outs, sems)[0]:
            cp.start()

    def middle(self, ins, outs, sems):
        pass

    def finish(self, ins, outs, sems):
        sends, recvs = self._copies(ins, outs, sems)
        for cp in recvs:
            cp.wait_recv()
        for cp in sends:
            cp.wait_send()


class _PairScatter:
    def __init__(self, tensors):
        self.ins = list(tensors)
        self.n = n = len(tensors)
        r = tensors[0].shape[1]
        assert all(t.shape == (N_DEV, r, D_MODEL) and t.dtype == BF16 for t in tensors)
        quarter = lambda k: [jax.ShapeDtypeStruct((k, r, D_MODEL), BF16) for _ in tensors]
        self.out_shape = quarter(4) + quarter(4) + quarter(3)
        self.scratch = [pltpu.SemaphoreType.DMA((4 * n,)), pltpu.SemaphoreType.DMA((4 * n,)),
                        pltpu.SemaphoreType.DMA((3 * n,)), pltpu.SemaphoreType.DMA((3 * n,)),
                        pltpu.SemaphoreType.DMA((2,)), pltpu.VMEM((r, D_MODEL), BF16), pltpu.VMEM((r, D_MODEL), BF16)]

    def _copies(self, ins, outs, sems):
        n = self.n
        pair, sums, recv = outs[:n], outs[n:2 * n], outs[2 * n:]
        x, y, c, chips = _place()
        swap = [pltpu.make_async_remote_copy(
            src_ref=ins[t].at[2 * b + 1 - c], dst_ref=pair[t].at[b], send_sem=sems[0].at[4 * t + b],
            recv_sem=sems[1].at[4 * t + b], device_id=(x, y, 1 - c), device_id_type=MESH_ID)
            for t in range(n) for b in range(4)]
        to_owner = [pltpu.make_async_remote_copy(
            src_ref=sums[t].at[2 * cx + cy], dst_ref=recv[t].at[j], send_sem=sems[2].at[3 * t + j],
            recv_sem=sems[3].at[3 * t + j], device_id=(cx, cy, c), device_id_type=MESH_ID)
            for t in range(n) for j, (cx, cy) in enumerate(chips)]
        return swap, to_owner

    def start(self, ins, outs, sems):
        for cp in self._copies(ins, outs, sems)[0]:
            cp.start()

    def middle(self, ins, outs, sems):
        n = self.n
        pair, sums = outs[:n], outs[n:2 * n]
        local, mine_ref, theirs_ref = sems[4], sems[5], sems[6]
        c = lax.axis_index("c")
        swap, to_owner = self._copies(ins, outs, sems)
        for cp in swap:
            cp.wait_recv()
        for t in range(n):
            for b in range(4):
                loads = [pltpu.make_async_copy(ins[t].at[2 * b + c], mine_ref, local.at[0]),
                         pltpu.make_async_copy(pair[t].at[b], theirs_ref, local.at[1])]
                for cp in loads:
                    cp.start()
                for cp in loads:
                    cp.wait()
                mine_ref[...] = (mine_ref[...].astype(F32) + theirs_ref[...].astype(F32)).astype(BF16)
                store = pltpu.make_async_copy(mine_ref, sums[t].at[b], local.at[0])
                store.start()
                store.wait()
        for cp in to_owner:
            cp.start()
        for cp in swap:
            cp.wait_send()

    def finish(self, ins, outs, sems):
        to_owner = self._copies(ins, outs, sems)[1]
        for cp in to_owner:
            cp.wait_recv()
        for cp in to_owner:
            cp.wait_send()


class _Comms:
    def __init__(self, *comms):
        self.comms = comms
        self.ins = [a for c in comms for a in c.ins]
        self.out_shape = [s for c in comms for s in c.out_shape]
        self.scratch = [s for c in comms for s in c.scratch]

    def _each(self, phase, ins, outs, sems):
        i = o = s = 0
        for c in self.comms:
            ni, no, ns = len(c.ins), len(c.out_shape), len(c.scratch)
            getattr(c, phase)(ins[i:i + ni], outs[o:o + no], sems[s:s + ns])
            i, o, s = i + ni, o + no, s + ns

    def start(self, ins, outs, sems):
        self._each("start", ins, outs, sems)

    def middle(self, ins, outs, sems):
        self._each("middle", ins, outs, sems)

    def finish(self, ins, outs, sems):
        self._each("finish", ins, outs, sems)


def _run_comm(comm, name):
    n = len(comm.ins)

    def body(*refs):
        parts = refs[:n], refs[n:2 * n], refs[2 * n:]
        comm.start(*parts)
        comm.middle(*parts)
        comm.finish(*parts)

    return pl.pallas_call(body, name=name, out_shape=comm.out_shape, in_specs=[ANY] * n, out_specs=[ANY] * n,
                          scratch_shapes=comm.scratch)(*comm.ins)


def _fused_call(body, comm, args, *, name, grid, out_shape, in_specs, out_specs, scratch_shapes=(), params):
    single = not isinstance(out_shape, (list, tuple))
    out_shape = [out_shape] if single else list(out_shape)
    out_specs = [out_specs] if single else list(out_specs)
    if comm is None:
        res = pl.pallas_call(body, name=name, grid=grid, out_shape=out_shape, in_specs=in_specs, out_specs=out_specs,
                             scratch_shapes=list(scratch_shapes), compiler_params=params)(*args)
        return (res[0] if single else res), []
    n_in, n_out, n_scr = len(in_specs), len(out_shape), len(scratch_shapes)
    c_in, c_out = len(comm.ins), len(comm.out_shape)
    steps = grid[0]

    def fused(*refs):
        pos = 0
        groups = []
        for size in (n_in, c_in, n_out, c_out, n_scr, len(comm.scratch)):
            groups.append(refs[pos:pos + size])
            pos += size
        ins, c_ins, outs, c_outs, scr, c_sems = groups
        i = pl.program_id(0)

        @pl.when(i == 0)
        def _():
            comm.start(c_ins, c_outs, c_sems)

        @pl.when(i == steps // 2)
        def _():
            comm.middle(c_ins, c_outs, c_sems)

        body(*ins, *outs, *scr)

        @pl.when(i == steps - 1)
        def _():
            comm.finish(c_ins, c_outs, c_sems)

    res = pl.pallas_call(
        fused, name=name, grid=grid, out_shape=out_shape + list(comm.out_shape),
        in_specs=list(in_specs) + [ANY] * c_in, out_specs=out_specs + [ANY] * c_out,
        scratch_shapes=list(scratch_shapes) + list(comm.scratch), compiler_params=params)(*args, *comm.ins)
    main = res[:n_out]
    return (main[0] if single else main), list(res[n_out:])


def _norm_matmul(x, gain, wt, name, comm=None, tm=512):
    t, n = x.shape[0], wt.shape[0]

    def body(x_ref, g_ref, wt_ref, o_ref):
        xv = x_ref[...]
        r = lax.rsqrt(jnp.mean(xv * xv, axis=-1, keepdims=True) + EPS)
        h = (xv * r * g_ref[...]).astype(BF16)
        o_ref[...] = lax.dot_general(h, wt_ref[...], NT, preferred_element_type=F32)

    return _fused_call(
        body, comm, (x, gain, wt), name=name, grid=(t // tm,),
        out_shape=jax.ShapeDtypeStruct((t, n), F32),
        in_specs=[pl.BlockSpec((tm, D_MODEL), lambda i: (i, 0)), _const_spec((1, D_MODEL)), _const_spec((n, D_MODEL))],
        out_specs=pl.BlockSpec((tm, n), lambda i: (i, 0)),
        params=_params("arbitrary"))


def _out_norm_res(a, w, x_in, post, name, target=None, tm=512):
    t = a.shape[0]
    with_loss = target is not None

    def body(*refs):
        if with_loss:
            a_ref, w_ref, x_ref, p_ref, t_ref, y_ref, o_ref, l_ref = refs
        else:
            a_ref, w_ref, x_ref, p_ref, y_ref, o_ref = refs
        y = jnp.dot(a_ref[...], w_ref[...], preferred_element_type=F32)
        y_ref[...] = y
        ry = lax.rsqrt(jnp.mean(y * y, axis=-1, keepdims=True) + EPS)
        xo = x_ref[...] + (y * ry) * p_ref[...]
        if with_loss:
            d = xo - t_ref[...]
            o_ref[...] = d * (1.0 / D_MODEL)

            @pl.when(pl.program_id(0) == 0)
            def _():
                l_ref[...] = jnp.zeros_like(l_ref)

            l_ref[...] += 0.5 * jnp.sum(jnp.mean(d * d, axis=-1, keepdims=True))
        else:
            o_ref[...] = xo

    row = pl.BlockSpec((tm, D_MODEL), lambda i: (i, 0))
    in_specs = [row, _const_spec((D_MODEL, D_MODEL)), row, _const_spec((1, D_MODEL))]
    out_shape = [jax.ShapeDtypeStruct((t, D_MODEL), F32), jax.ShapeDtypeStruct((t, D_MODEL), F32)]
    out_specs = [row, row]
    args = [a, w, x_in, post]
    if with_loss:
        in_specs.append(row)
        args.append(target)
        out_shape.append(jax.ShapeDtypeStruct((8, 128), F32))
        out_specs.append(pl.BlockSpec((8, 128), lambda i: (0, 0)))
    return pl.pallas_call(
        body, name=name, grid=(t // tm,), out_shape=out_shape, in_specs=in_specs, out_specs=out_specs,
        compiler_params=_params("arbitrary"),
    )(*args)


def _post_bwd(g, y, post, w, a, name, tm=512):
    t = g.shape[0]
    steps = t // tm

    def body(g_ref, y_ref, p_ref, w_ref, a_ref, da_ref, dw16_ref, dp_ref, dw_ref):
        @pl.when(pl.program_id(0) == 0)
        def _():
            dw_ref[...] = jnp.zeros_like(dw_ref)
            dp_ref[...] = jnp.zeros_like(dp_ref)

        gv = g_ref[...]
        yv = y_ref[...]
        ry = lax.rsqrt(jnp.mean(yv * yv, axis=-1, keepdims=True) + EPS)
        nv = yv * ry
        dp_ref[...] += jnp.sum(gv * nv, axis=0, keepdims=True)
        dn = gv * p_ref[...]
        dy = (ry * (dn - nv * jnp.mean(dn * nv, axis=-1, keepdims=True))).astype(BF16)
        da_ref[...] = lax.dot_general(dy, w_ref[...], NT, preferred_element_type=F32)
        dw_ref[...] += lax.dot_general(a_ref[...], dy, TN, preferred_element_type=F32)

        @pl.when(pl.program_id(0) == steps - 1)
        def _():
            dw16_ref[...] = dw_ref[...].astype(BF16)

    row = pl.BlockSpec((tm, D_MODEL), lambda i: (i, 0))
    return pl.pallas_call(
        body, name=name, grid=(steps,),
        out_shape=[jax.ShapeDtypeStruct((t, D_MODEL), F32), jax.ShapeDtypeStruct((D_MODEL, D_MODEL), BF16),
                   jax.ShapeDtypeStruct((1, D_MODEL), F32)],
        in_specs=[row, row, _const_spec((1, D_MODEL)), _const_spec((D_MODEL, D_MODEL)), row],
        out_specs=[row, pl.BlockSpec((D_MODEL, D_MODEL), lambda i: (0, 0)), pl.BlockSpec((1, D_MODEL), lambda i: (0, 0))],
        scratch_shapes=[pltpu.VMEM((D_MODEL, D_MODEL), F32)],
        compiler_params=_params("arbitrary"),
    )(g, y, post, w, a)


def _pre_bwd(dproj, wt, x_in, pre, g, name, tm=256):
    t, n = dproj.shape
    steps = t // tm

    def body(dp_ref, wt_ref, x_ref, pre_ref, g_ref, dx_ref, dwt16_ref, dpre_ref, dwt_ref):
        @pl.when(pl.program_id(0) == 0)
        def _():
            dwt_ref[...] = jnp.zeros_like(dwt_ref)
            dpre_ref[...] = jnp.zeros_like(dpre_ref)

        dpv = dp_ref[...]
        dh = jnp.dot(dpv, wt_ref[...], preferred_element_type=F32)
        xv = x_ref[...]
        r = lax.rsqrt(jnp.mean(xv * xv, axis=-1, keepdims=True) + EPS)
        xn = xv * r
        pv = pre_ref[...]
        dpre_ref[...] += jnp.sum(dh * xn, axis=0, keepdims=True)
        dxn = dh * pv
        dx_ref[...] = g_ref[...] + r * (dxn - xn * jnp.mean(dxn * xn, axis=-1, keepdims=True))
        h = (xn * pv).astype(BF16)
        dwt_ref[...] += lax.dot_general(dpv, h, TN, preferred_element_type=F32)

        @pl.when(pl.program_id(0) == steps - 1)
        def _():
            dwt16_ref[...] = dwt_ref[...].astype(BF16)

    row = pl.BlockSpec((tm, D_MODEL), lambda i: (i, 0))
    return pl.pallas_call(
        body, name=name, grid=(steps,),
        out_shape=[jax.ShapeDtypeStruct((t, D_MODEL), F32), jax.ShapeDtypeStruct((n, D_MODEL), BF16),
                   jax.ShapeDtypeStruct((1, D_MODEL), F32)],
        in_specs=[pl.BlockSpec((tm, n), lambda i: (i, 0)), _const_spec((n, D_MODEL)), row, _const_spec((1, D_MODEL)), row],
        out_specs=[row, pl.BlockSpec((n, D_MODEL), lambda i: (0, 0)), pl.BlockSpec((1, D_MODEL), lambda i: (0, 0))],
        scratch_shapes=[pltpu.VMEM((n, D_MODEL), F32)],
        compiler_params=_params("arbitrary"),
    )(dproj, wt, x_in, pre, g)


def _proj_dw(dproj, x_in, pre, name, comm=None, tm=512):
    t, n = dproj.shape
    steps = t // tm

    def body(dp_ref, x_ref, pre_ref, dwt16_ref, dwt_ref):
        @pl.when(pl.program_id(0) == 0)
        def _():
            dwt_ref[...] = jnp.zeros_like(dwt_ref)

        xv = x_ref[...]
        r = lax.rsqrt(jnp.mean(xv * xv, axis=-1, keepdims=True) + EPS)
        h = (xv * r * pre_ref[...]).astype(BF16)
        dwt_ref[...] += lax.dot_general(dp_ref[...], h, TN, preferred_element_type=F32)

        @pl.when(pl.program_id(0) == steps - 1)
        def _():
            dwt16_ref[...] = dwt_ref[...].astype(BF16)

    return _fused_call(
        body, comm, (dproj, x_in, pre), name=name, grid=(steps,),
        out_shape=jax.ShapeDtypeStruct((n, D_MODEL), BF16),
        in_specs=[pl.BlockSpec((tm, n), lambda i: (i, 0)), pl.BlockSpec((tm, D_MODEL), lambda i: (i, 0)),
                  _const_spec((1, D_MODEL))],
        out_specs=pl.BlockSpec((n, D_MODEL), lambda i: (0, 0)),
        scratch_shapes=[pltpu.VMEM((n, D_MODEL), F32)],
        params=_params("arbitrary"))


def _proj_dx(dproj, wt, x_in, pre, g, name, comm=None, tm=512):
    t, n = dproj.shape

    def body(dp_ref, wt_ref, x_ref, pre_ref, g_ref, dx_ref, dpre_ref):
        @pl.when(pl.program_id(0) == 0)
        def _():
            dpre_ref[...] = jnp.zeros_like(dpre_ref)

        dh = jnp.dot(dp_ref[...], wt_ref[...], preferred_element_type=F32)
        xv = x_ref[...]
        r = lax.rsqrt(jnp.mean(xv * xv, axis=-1, keepdims=True) + EPS)
        xn = xv * r
        dpre_ref[...] += jnp.sum(dh * xn, axis=0, keepdims=True)
        dxn = dh * pre_ref[...]
        dx_ref[...] = g_ref[...] + r * (dxn - xn * jnp.mean(dxn * xn, axis=-1, keepdims=True))

    row = pl.BlockSpec((tm, D_MODEL), lambda i: (i, 0))
    return _fused_call(
        body, comm, (dproj, wt, x_in, pre, g), name=name, grid=(t // tm,),
        out_shape=[jax.ShapeDtypeStruct((t, D_MODEL), F32), jax.ShapeDtypeStruct((1, D_MODEL), F32)],
        in_specs=[pl.BlockSpec((tm, n), lambda i: (i, 0)), _const_spec((n, D_MODEL)), row, _const_spec((1, D_MODEL)), row],
        out_specs=[row, pl.BlockSpec((1, D_MODEL), lambda i: (0, 0))],
        params=_params("arbitrary"))


def _group_masks():
    lane = lax.broadcasted_iota(jnp.int32, (1, GROUP * HEAD_DIM), 1)
    return [(lane // HEAD_DIM == g).astype(F32) for g in range(GROUP)]


def _stack_groups(v, masks):
    return jnp.concatenate([v * m for m in masks], axis=0)


def _unstack_groups(v, masks):
    out = v[0:BLOCK] * masks[0]
    for g in range(1, GROUP):
        out = out + v[g * BLOCK:(g + 1) * BLOCK] * masks[g]
    return out


def _repeat_head(kv2, kvh):
    first = lax.broadcasted_iota(jnp.int32, kv2.shape, 1) < HEAD_DIM
    rolled = pltpu.roll(kv2, HEAD_DIM, 1)
    one = jnp.where(first, kv2, rolled) if kvh == 0 else jnp.where(first, rolled, kv2)
    return jnp.concatenate([one, one], axis=1)


def _fold_head(v4):
    a = v4[:, 0:128] + v4[:, 128:256]
    return a + pltpu.roll(a, HEAD_DIM, 1)


ATTN_CONSTS = [pltpu.VMEM((KV_HEADS, GROUP * BLOCK, 2 * BLOCK), F32)]


def _fill_attn_bias(bias_ref):
    row = lax.broadcasted_iota(jnp.int32, (GROUP * BLOCK, 2 * BLOCK), 0)
    col = lax.broadcasted_iota(jnp.int32, (GROUP * BLOCK, 2 * BLOCK), 1)
    dist = (row % BLOCK) + BLOCK - col
    band = (dist >= 0) & (dist < BLOCK)
    rb = lax.broadcasted_iota(jnp.int32, (GROUP * BLOCK, 1), 0) // BLOCK
    for kvh in range(KV_HEADS):
        slope = jnp.zeros((GROUP * BLOCK, 1), F32)
        for g in range(GROUP):
            slope = jnp.where(rb == g, 2.0 ** (-(kvh * GROUP + g + 1)), slope)
        bias_ref[kvh] = jnp.where(band, -slope * dist.astype(F32), NEG)


def _row_sinks(kvh, sink_ref):
    rb = lax.broadcasted_iota(jnp.int32, (GROUP * BLOCK, 1), 0) // BLOCK
    sink = jnp.zeros((GROUP * BLOCK, 1), F32)
    for g in range(GROUP):
        sink = jnp.where(rb == g, sink_ref[0, kvh * GROUP + g], sink)
    return sink


def _attn_probs(qk, k4, bias, sink, no_past, masks):
    qs = _stack_groups(qk, masks).astype(BF16)
    s = lax.dot_general(qs, k4, NT, preferred_element_type=F32) * (HEAD_DIM ** -0.5) + bias
    s = jnp.concatenate([jnp.where(no_past, NEG, s[:, 0:BLOCK]), s[:, BLOCK:]], axis=1)
    mx = jnp.maximum(jnp.max(s, axis=-1, keepdims=True), sink)
    e = jnp.exp(s - mx)
    es = jnp.exp(sink - mx)
    inv = 1.0 / (jnp.sum(e, axis=-1, keepdims=True) + es)
    return qs, e * inv, es * inv


def _pool_forward(u_ext, g, t0):
    n = u_ext.shape[0] - POOL_HALO
    s = u_ext
    for step in range(g + 1):
        s = s + pltpu.roll(s, 1 << step, 0)
    w = 2 << g
    t = t0 + lax.broadcasted_iota(jnp.int32, (n, 1), 0)
    cnt = jnp.minimum(t + 1, w).astype(F32)
    return s[POOL_HALO:] / cnt - u_ext[POOL_HALO:]


def _mix0_fwd(proj, sinks, pool_w, pool_scale, comm=None, tq=512):
    t = proj.shape[0]
    nblk = tq // BLOCK

    def body(main_ref, halo_ref, sink_ref, pw_ref, ps_ref, o_ref, kv_ref, bias_ref):
        i = pl.program_id(0)
        t0 = i * tq
        masks = _group_masks()

        @pl.when(i == 0)
        def _():
            _fill_attn_bias(bias_ref)

        kv_ref[0:BLOCK, :] = halo_ref[:, COL_K:COL_K + 256]
        kv_ref[BLOCK:, :] = main_ref[:, COL_K:COL_K + 256]

        def block(jb, carry):
            r0 = pl.multiple_of(jb * BLOCK, BLOCK)
            no_past = t0 + r0 == 0
            q = main_ref[pl.ds(r0, BLOCK), COL_Q:COL_Q + 512]
            ga = main_ref[pl.ds(r0, BLOCK), COL_GA:COL_GA + 512]
            kk = kv_ref[pl.ds(r0, 2 * BLOCK), 0:128]
            vv = kv_ref[pl.ds(r0, 2 * BLOCK), 128:256]
            outs = []
            for kvh in range(KV_HEADS):
                k4 = _repeat_head(kk, kvh).astype(BF16)
                v4 = _repeat_head(vv, kvh).astype(BF16)
                _, p, _ = _attn_probs(q[:, kvh * 256:(kvh + 1) * 256], k4, bias_ref[kvh], _row_sinks(kvh, sink_ref), no_past, masks)
                pv = jnp.dot(p.astype(BF16), v4, preferred_element_type=F32)
                outs.append(_unstack_groups(pv, masks))
            attn = jnp.concatenate(outs, axis=1)
            o_ref[pl.ds(r0, BLOCK), 0:512] = (attn * _silu(ga)).astype(BF16)
            return carry

        lax.fori_loop(0, nblk, block, 0, unroll=2)

        for g in range(POOL_GROUPS):
            cu = COL_U + g * POOL_GC
            cg = COL_GB + g * POOL_GC
            halo_u = jnp.where(i == 0, 0.0, halo_ref[BLOCK - POOL_HALO:BLOCK, cu:cu + POOL_GC])
            u_ext = jnp.concatenate([halo_u, main_ref[:, cu:cu + POOL_GC]], axis=0)
            pooled = _pool_forward(u_ext, g, t0)
            y = jnp.dot(pooled.astype(BF16), pw_ref[g].astype(BF16), preferred_element_type=F32)
            y = y * ps_ref[:, g * POOL_GC:(g + 1) * POOL_GC]
            o_ref[:, 512 + g * POOL_GC:512 + (g + 1) * POOL_GC] = (y * _silu(main_ref[:, cg:cg + POOL_GC])).astype(BF16)

    return _fused_call(
        body, comm, (proj, proj, sinks, pool_w, pool_scale), name="mix0_fwd", grid=(t // tq,),
        out_shape=jax.ShapeDtypeStruct((t, D_MODEL), BF16),
        in_specs=[pl.BlockSpec((tq, EVEN_IN), lambda i: (i, 0)),
                  pl.BlockSpec((BLOCK, EVEN_IN), lambda i: (jnp.maximum(i * nblk - 1, 0), 0)),
                  pl.BlockSpec(memory_space=pltpu.SMEM),
                  _const_spec((POOL_GROUPS, POOL_GC, POOL_GC)), _const_spec((1, 512))],
        out_specs=pl.BlockSpec((tq, D_MODEL), lambda i: (i, 0)),
        scratch_shapes=[pltpu.VMEM((tq + BLOCK, 256), F32)] + ATTN_CONSTS,
        params=_params("arbitrary"))


def _mix0_bwd(proj, dmix, sinks, pool_w, pool_scale, comm=None, tq=512):
    t = proj.shape[0]
    nt = t // tq
    nblk = tq // BLOCK

    def body(main_ref, halo_ref, next_ref, dm_ref, dmn_ref, sink_ref, pw_ref, ps_ref,
             o_ref, dsk_ref, dpw_ref, dps_ref, kv_ref, dkv_ref, carry_ref, bias_ref):
        i = pl.program_id(0)
        ii = nt - 1 - i
        t0 = ii * tq
        masks = _group_masks()

        @pl.when(i == 0)
        def _():
            _fill_attn_bias(bias_ref)
            dsk_ref[...] = jnp.zeros_like(dsk_ref)
            dpw_ref[...] = jnp.zeros_like(dpw_ref)
            dps_ref[...] = jnp.zeros_like(dps_ref)
            carry_ref[...] = jnp.zeros_like(carry_ref)

        kv_ref[0:BLOCK, :] = halo_ref[:, COL_K:COL_K + 256]
        kv_ref[BLOCK:, :] = main_ref[:, COL_K:COL_K + 256]
        dkv_ref[0:tq, :] = jnp.zeros((tq, 256), F32)
        dkv_ref[tq:, :] = carry_ref[...]

        def block(jb, carry):
            r0 = pl.multiple_of(jb * BLOCK, BLOCK)
            no_past = t0 + r0 == 0
            q = main_ref[pl.ds(r0, BLOCK), COL_Q:COL_Q + 512]
            ga = main_ref[pl.ds(r0, BLOCK), COL_GA:COL_GA + 512]
            dya = dm_ref[pl.ds(r0, BLOCK), 0:512]
            kk = kv_ref[pl.ds(r0, 2 * BLOCK), 0:128]
            vv = kv_ref[pl.ds(r0, 2 * BLOCK), 128:256]
            silu_ga, dsilu_ga = _silu_and_grad(ga)
            do = dya * silu_ga
            first = lax.broadcasted_iota(jnp.int32, (2 * BLOCK, 128), 1) < HEAD_DIM
            attn, dq, dk, dv = [], [], [], []
            for kvh in range(KV_HEADS):
                k4 = _repeat_head(kk, kvh).astype(BF16)
                v4 = _repeat_head(vv, kvh).astype(BF16)
                qs, p, ps = _attn_probs(q[:, kvh * 256:(kvh + 1) * 256], k4, bias_ref[kvh], _row_sinks(kvh, sink_ref), no_past, masks)
                pb = p.astype(BF16)
                o_k = _unstack_groups(jnp.dot(pb, v4, preferred_element_type=F32), masks)
                do_k = do[:, kvh * 256:(kvh + 1) * 256]
                dos = _stack_groups(do_k, masks).astype(BF16)
                prod = do_k * o_k
                delta = jnp.concatenate([jnp.sum(prod * m, axis=-1, keepdims=True) for m in masks], axis=0)
                dp = lax.dot_general(dos, v4, NT, preferred_element_type=F32)
                ds = (p * (dp - delta)).astype(BF16)
                sink_term = ps * delta
                for g in range(GROUP):
                    h = kvh * GROUP + g
                    dsk_ref[h:h + 1, :] -= jnp.sum(sink_term[g * BLOCK:(g + 1) * BLOCK], keepdims=True)
                scale = HEAD_DIM ** -0.5
                dq.append(_unstack_groups(jnp.dot(ds, k4, preferred_element_type=F32), masks) * scale)
                dk.append(_fold_head(lax.dot_general(ds, qs, TN, preferred_element_type=F32)) * scale)
                dv.append(_fold_head(lax.dot_general(pb, dos, TN, preferred_element_type=F32)))
                attn.append(o_k)
            o_ref[pl.ds(r0, BLOCK), COL_Q:COL_Q + 512] = jnp.concatenate(dq, axis=1).astype(BF16)
            o_all = jnp.concatenate(attn, axis=1)
            o_ref[pl.ds(r0, BLOCK), COL_GA:COL_GA + 512] = (dya * o_all * dsilu_ga).astype(BF16)
            dkv = jnp.concatenate([jnp.where(first, dk[0], dk[1]), jnp.where(first, dv[0], dv[1])], axis=1)
            dkv_ref[pl.ds(r0, 2 * BLOCK), :] += dkv
            return carry

        lax.fori_loop(0, nblk, block, 0, unroll=2)
        carry_ref[...] = dkv_ref[0:BLOCK, :]
        o_ref[:, COL_K:COL_K + 256] = dkv_ref[BLOCK:, :].astype(BF16)

        last = ii == nt - 1
        for g in range(POOL_GROUPS):
            cu = COL_U + g * POOL_GC
            cg = COL_GB + g * POOL_GC
            cm = 512 + g * POOL_GC
            pw = pw_ref[g].astype(BF16)
            sc = ps_ref[:, g * POOL_GC:(g + 1) * POOL_GC]
            halo_u = jnp.where(ii == 0, 0.0, halo_ref[BLOCK - POOL_HALO:BLOCK, cu:cu + POOL_GC])
            u_ext = jnp.concatenate([halo_u, main_ref[:, cu:cu + POOL_GC]], axis=0)
            pooled = _pool_forward(u_ext, g, t0).astype(BF16)
            y_raw = jnp.dot(pooled, pw, preferred_element_type=F32)
            gb = main_ref[:, cg:cg + POOL_GC]
            dyb = dm_ref[:, cm:cm + POOL_GC]
            silu_gb, dsilu_gb = _silu_and_grad(gb)
            dypool = dyb * silu_gb
            dps_ref[:, g * POOL_GC:(g + 1) * POOL_GC] += jnp.sum(dypool * y_raw, axis=0, keepdims=True)
            o_ref[:, cg:cg + POOL_GC] = (dyb * (y_raw * sc) * dsilu_gb).astype(BF16)
            dyraw = dypool * sc
            dyraw_n = jnp.where(last, 0.0, dmn_ref[:, cm:cm + POOL_GC] * _silu(next_ref[:, cg:cg + POOL_GC]) * sc)
            dpw_ref[g * POOL_GC:(g + 1) * POOL_GC, :] += lax.dot_general(pooled, dyraw.astype(BF16), TN,
                                                                         preferred_element_type=F32)
            dyraw_ext = jnp.concatenate([dyraw, dyraw_n], axis=0).astype(BF16)
            dpooled = lax.dot_general(dyraw_ext, pw, NT, preferred_element_type=F32)
            w = 2 << g
            tt = t0 + lax.broadcasted_iota(jnp.int32, (tq + POOL_HALO, 1), 0)
            s = dpooled / jnp.minimum(tt + 1, w).astype(F32)
            for step in range(g + 1):
                s = s + pltpu.roll(s, tq + POOL_HALO - (1 << step), 0)
            o_ref[:, cu:cu + POOL_GC] = (s[0:tq] - dpooled[0:tq]).astype(BF16)

    rev = lambda i: nt - 1 - i
    return _fused_call(
        body, comm, (proj, proj, proj, dmix, dmix, sinks, pool_w, pool_scale), name="mix0_bwd", grid=(nt,),
        out_shape=[jax.ShapeDtypeStruct((t, EVEN_IN), BF16), jax.ShapeDtypeStruct((8, 128), F32),
                   jax.ShapeDtypeStruct((POOL_GROUPS * POOL_GC, POOL_GC), F32), jax.ShapeDtypeStruct((1, 512), F32)],
        in_specs=[pl.BlockSpec((tq, EVEN_IN), lambda i: (rev(i), 0)),
                  pl.BlockSpec((BLOCK, EVEN_IN), lambda i: (jnp.maximum(rev(i) * nblk - 1, 0), 0)),
                  pl.BlockSpec((POOL_HALO, EVEN_IN),
                               lambda i: (jnp.minimum((rev(i) + 1) * (tq // POOL_HALO), t // POOL_HALO - 1), 0)),
                  pl.BlockSpec((tq, D_MODEL), lambda i: (rev(i), 0)),
                  pl.BlockSpec((POOL_HALO, D_MODEL),
                               lambda i: (jnp.minimum((rev(i) + 1) * (tq // POOL_HALO), t // POOL_HALO - 1), 0)),
                  pl.BlockSpec(memory_space=pltpu.SMEM),
                  _const_spec((POOL_GROUPS, POOL_GC, POOL_GC)), _const_spec((1, 512))],
        out_specs=[pl.BlockSpec((tq, EVEN_IN), lambda i: (rev(i), 0)),
                   pl.BlockSpec((8, 128), lambda i: (0, 0)),
                   pl.BlockSpec((POOL_GROUPS * POOL_GC, POOL_GC), lambda i: (0, 0)),
                   pl.BlockSpec((1, 512), lambda i: (0, 0))],
        scratch_shapes=[pltpu.VMEM((tq + BLOCK, 256), F32), pltpu.VMEM((tq + BLOCK, 256), F32),
                        pltpu.VMEM((BLOCK, 256), F32)] + ATTN_CONSTS,
        params=_params("arbitrary"))


CONV_RC = 32
CONV_CC = 128
CONV_CHAINS = 4
CONV_UNROLL = 2


def _fill_shifted(s_ref, rows):
    for b in range(1, 8):
        s_ref[b, 0:rows - 8, :] = s_ref[0, b:b + rows - 8, :]


def _tap_blocks(s_ref, r, cols, lead):
    for b in range(8):
        taps = [(a, 8 * a + b - lead) for a in range(5) if 0 <= 8 * a + b - lead < CONV_K]
        span = 8 * max(a for a, _ in taps) + CONV_RC
        blk = s_ref[b, pl.ds(r, span), cols]
        for a, k in taps:
            yield k, blk[8 * a:8 * a + CONV_RC]


def _conv_taps(s_ref, w_ref, r, cols, lead, reverse):
    accs = [None] * CONV_CHAINS
    for n, (k, blk) in enumerate(_tap_blocks(s_ref, r, cols, lead)):
        kw = CONV_K - 1 - k if reverse else k
        term = blk * w_ref[kw:kw + 1, cols]
        accs[n % CONV_CHAINS] = term if accs[n % CONV_CHAINS] is None else accs[n % CONV_CHAINS] + term
    return (accs[0] + accs[1]) + (accs[2] + accs[3])


def _layer_norm_fwd(cf, lng, lnb):
    mu = jnp.mean(cf, axis=-1, keepdims=True)
    xc = cf - mu
    rstd = lax.rsqrt(jnp.mean(xc * xc, axis=-1, keepdims=True) + EPS)
    chat = xc * rstd
    return chat, rstd, chat * lng + lnb


def _conv_fwd(proj, dw, dwb, lng, lnb, tt=256):
    t = proj.shape[0]
    lead = CONV_HALO - (CONV_K - 1)

    def body(main_ref, halo_ref, w_ref, b_ref, g_ref, lb_ref, o_ref, c_ref, gs_ref):
        i = pl.program_id(0)
        hv = halo_ref[...]
        gs_ref[0, 0:CONV_HALO, :] = jnp.where(i == 0, 0.0, hv[:, 0:1024] * _sigmoid(hv[:, 1024:2048]))
        gs_ref[0, CONV_HALO:CONV_HALO + tt, :] = main_ref[:, 0:1024] * _sigmoid(main_ref[:, 1024:2048])
        _fill_shifted(gs_ref, tt + CONV_HALO)

        for c in range(D_MODEL // CONV_CC):
            cols = slice(c * CONV_CC, (c + 1) * CONV_CC)

            def chunk(j, carry):
                r = pl.multiple_of(j * CONV_RC, CONV_RC)
                c_ref[pl.ds(r, CONV_RC), cols] = _conv_taps(gs_ref, w_ref, r, cols, lead, False) + b_ref[:, cols]
                return carry
            lax.fori_loop(0, tt // CONV_RC, chunk, 0, unroll=CONV_UNROLL)

        _, _, cn = _layer_norm_fwd(c_ref[...], g_ref[...], lb_ref[...])
        o_ref[...] = (_silu(cn) * _silu(main_ref[:, 2048:3072])).astype(BF16)

    vec = _const_spec((1, D_MODEL))
    row = pl.BlockSpec((tt, D_MODEL), lambda i: (i, 0))
    return pl.pallas_call(
        body, name="conv_fwd", grid=(t // tt,),
        out_shape=[jax.ShapeDtypeStruct((t, D_MODEL), BF16), jax.ShapeDtypeStruct((t, D_MODEL), F32)],
        in_specs=[pl.BlockSpec((tt, 3 * D_MODEL), lambda i: (i, 0)),
                  pl.BlockSpec((CONV_HALO, 3 * D_MODEL), lambda i: (jnp.maximum(i * (tt // CONV_HALO) - 1, 0), 0)),
                  _const_spec((CONV_K, D_MODEL)), vec, vec, vec],
        out_specs=[row, row],
        scratch_shapes=[pltpu.VMEM((8, tt + CONV_HALO, D_MODEL), F32)],
        compiler_params=_params("parallel"),
    )(proj, proj, dw, dwb, lng, lnb)


def _conv_bwd(proj, cf, dz, dw, lng, lnb, comm=None, tt=256):
    t = proj.shape[0]
    nt = t // tt
    te = tt + CONV_HALO

    def body(main_ref, next_ref, cf_ref, cfn_ref, dz_ref, dzn_ref, w_ref, g_ref, lb_ref,
             o_ref, ddw_ref, ddb_ref, dg_ref, dlb_ref, ds_ref, glu_ref, sb_ref):
        i = pl.program_id(0)

        @pl.when(i == 0)
        def _():
            ddw_ref[...] = jnp.zeros_like(ddw_ref)
            ddb_ref[...] = jnp.zeros_like(ddb_ref)
            dg_ref[...] = jnp.zeros_like(dg_ref)
            dlb_ref[...] = jnp.zeros_like(dlb_ref)

        lng = g_ref[...]
        chat, rstd, cn = _layer_norm_fwd(jnp.concatenate([cf_ref[...], cfn_ref[...]], axis=0), lng, lb_ref[...])
        gate = jnp.concatenate([main_ref[:, 2048:3072], next_ref[:, 2048:3072]], axis=0)
        dzv = jnp.concatenate([dz_ref[...], dzn_ref[...]], axis=0)
        rows = lax.broadcasted_iota(jnp.int32, (te, 1), 0)
        live = (rows < tt) | (i < nt - 1)
        own = (rows < tt).astype(F32)
        dzv = jnp.where(live, dzv, 0.0)
        silu_cn, dsilu_cn = _silu_and_grad(cn)
        silu_gate, dsilu_gate = _silu_and_grad(gate)
        o_ref[:, 2048:3072] = (dzv * silu_cn * dsilu_gate)[0:tt].astype(BF16)
        dcn = dzv * silu_gate * dsilu_cn
        dg_ref[...] += jnp.sum(dcn * chat * own, axis=0, keepdims=True)
        dlb_ref[...] += jnp.sum(dcn * own, axis=0, keepdims=True)
        dchat = dcn * lng
        dcf = rstd * (dchat - jnp.mean(dchat, axis=-1, keepdims=True) - chat * jnp.mean(dchat * chat, axis=-1, keepdims=True))
        ddb_ref[...] += jnp.sum(dcf * own, axis=0, keepdims=True)
        ds_ref[0, 0:te, :] = dcf
        ds_ref[0, te:, :] = jnp.zeros((8, D_MODEL), F32)
        _fill_shifted(ds_ref, te + 8)
        sb_ref[...] = _sigmoid(main_ref[:, 1024:2048])
        glu_ref[...] = main_ref[:, 0:1024] * sb_ref[...]

        for c in range(D_MODEL // CONV_CC):
            cols = slice(c * CONV_CC, (c + 1) * CONV_CC)
            gcols = slice(c * CONV_CC + 1024, (c + 1) * CONV_CC + 1024)

            def chunk(j, carry):
                r = pl.multiple_of(j * CONV_RC, CONV_RC)
                dglu = _conv_taps(ds_ref, w_ref, r, cols, 0, True)
                sb = sb_ref[pl.ds(r, CONV_RC), cols]
                o_ref[pl.ds(r, CONV_RC), cols] = (dglu * sb).astype(BF16)
                o_ref[pl.ds(r, CONV_RC), gcols] = (dglu * glu_ref[pl.ds(r, CONV_RC), cols] * (1.0 - sb)).astype(BF16)
                return carry
            lax.fori_loop(0, tt // CONV_RC, chunk, 0, unroll=CONV_UNROLL)

            def taps(j, accs):
                r = pl.multiple_of(j * CONV_RC, CONV_RC)
                gl = glu_ref[pl.ds(r, CONV_RC), cols]
                new = list(accs)
                for m, blk in _tap_blocks(ds_ref, r, cols, 0):
                    prod = blk * gl
                    part = prod[0:8]
                    for q in range(1, CONV_RC // 8):
                        part = part + prod[8 * q:8 * q + 8]
                    new[m] = new[m] + part
                return tuple(new)
            accs = lax.fori_loop(0, tt // CONV_RC, taps, tuple(jnp.zeros((8, CONV_CC), F32) for _ in range(CONV_K)))
            for m in range(CONV_K):
                k = CONV_K - 1 - m
                ddw_ref[k:k + 1, cols] += jnp.sum(accs[m], axis=0, keepdims=True)

    vec = _const_spec((1, D_MODEL))
    vec_out = pl.BlockSpec((1, D_MODEL), lambda i: (0, 0))
    row = pl.BlockSpec((tt, D_MODEL), lambda i: (i, 0))
    nxt = lambda i: (jnp.minimum((i + 1) * (tt // CONV_HALO), t // CONV_HALO - 1), 0)
    nxt_row = pl.BlockSpec((CONV_HALO, D_MODEL), nxt)
    return _fused_call(
        body, comm, (proj, proj, cf, cf, dz, dz, dw, lng, lnb), name="conv_bwd", grid=(nt,),
        out_shape=[jax.ShapeDtypeStruct((t, 3 * D_MODEL), BF16), jax.ShapeDtypeStruct((CONV_K, D_MODEL), F32),
                   jax.ShapeDtypeStruct((1, D_MODEL), F32), jax.ShapeDtypeStruct((1, D_MODEL), F32),
                   jax.ShapeDtypeStruct((1, D_MODEL), F32)],
        in_specs=[pl.BlockSpec((tt, 3 * D_MODEL), lambda i: (i, 0)),
                  pl.BlockSpec((CONV_HALO, 3 * D_MODEL), nxt),
                  row, nxt_row, row, nxt_row,
                  _const_spec((CONV_K, D_MODEL)), vec, vec],
        out_specs=[pl.BlockSpec((tt, 3 * D_MODEL), lambda i: (i, 0)),
                   pl.BlockSpec((CONV_K, D_MODEL), lambda i: (0, 0)), vec_out, vec_out, vec_out],
        scratch_shapes=[pltpu.VMEM((8, te + 8, D_MODEL), F32), pltpu.VMEM((tt, D_MODEL), F32),
                        pltpu.VMEM((tt, D_MODEL), F32)],
        params=_params("arbitrary"))


def _piece_sum(parts, place, name):
    r = parts[0][0].shape[1]

    def body(p_ref, *refs):
        o_ref = refs[-1]
        acc = refs[0][0].astype(F32)
        for part in refs[1:-1]:
            acc = acc + part[0].astype(F32)
        o_ref[0] = acc

    blk = (1, r, D_MODEL)
    spec = lambda slot: pl.BlockSpec(blk, lambda j, p_ref: (slot(p_ref), 0, 0))
    return pl.pallas_call(
        body, name=name,
        grid_spec=pltpu.PrefetchScalarGridSpec(
            num_scalar_prefetch=1, grid=(1,),
            in_specs=[spec(slot) for _, slot in parts],
            out_specs=pl.BlockSpec(blk, lambda j, p_ref: (p_ref[1], 0, 0))),
        out_shape=jax.ShapeDtypeStruct((2, r, D_MODEL), F32),
        compiler_params=_params("arbitrary"),
    )(place, *[a for a, _ in parts])


def _direct_parts(own, recv):
    peer = lambda m: (lambda p: p[0] ^ m)
    return [(own, peer(0))] + [(recv, peer(m)) for m in range(1, N_DEV)]


def _pair_parts(sums, recv):
    const = lambda j: (lambda p: j)
    return [(sums, lambda p: p[2])] + [(recv, const(j)) for j in range(3)]


def _share_with_sibling(halves, name):
    n = len(halves)

    def body(*refs):
        outs = refs[n:2 * n]
        send_sems, recv_sems = refs[2 * n:]
        x, y, c, _ = _place()
        send = [pltpu.make_async_remote_copy(
            src_ref=outs[t].at[c], dst_ref=outs[t].at[c], send_sem=send_sems.at[t], recv_sem=recv_sems.at[t],
            device_id=(x, y, 1 - c), device_id_type=MESH_ID) for t in range(n)]
        recv = [pltpu.make_async_remote_copy(
            src_ref=outs[t].at[c], dst_ref=outs[t].at[1 - c], send_sem=send_sems.at[t], recv_sem=recv_sems.at[t],
            device_id=(x, y, 1 - c), device_id_type=MESH_ID) for t in range(n)]
        for cp in send:
            cp.start()
        for cp in recv:
            cp.wait_recv()
        for cp in send:
            cp.wait_send()

    return pl.pallas_call(
        body, name=name,
        out_shape=[jax.ShapeDtypeStruct(h.shape, h.dtype) for h in halves],
        in_specs=[ANY] * n, out_specs=[ANY] * n,
        input_output_aliases={t: t for t in range(n)},
        scratch_shapes=[pltpu.SemaphoreType.DMA((n,)), pltpu.SemaphoreType.DMA((n,))],
    )(*halves)


def _sum8(parts, name):
    r = parts.shape[1]

    def body(p_ref, o_ref):
        acc = p_ref[0]
        for k in range(1, N_DEV):
            acc = acc + p_ref[k]
        o_ref[...] = acc

    return pl.pallas_call(
        body, name=name, out_shape=jax.ShapeDtypeStruct((r, 128), F32),
        in_specs=[pl.BlockSpec(memory_space=pltpu.VMEM)], out_specs=pl.BlockSpec(memory_space=pltpu.VMEM),
    )(parts)


def _adamw(w, g, m, v, name):
    shape = w.shape
    cols = shape[-1]
    rows = w.size // cols
    rt = 256 if rows % 256 == 0 else rows

    def body(w_ref, g_ref, m_ref, v_ref, d_ref, nm_ref, nv_ref):
        gv = g_ref[...]
        mn = ADAM_B1 * m_ref[...] + (1.0 - ADAM_B1) * gv
        vn = ADAM_B2 * v_ref[...] + (1.0 - ADAM_B2) * (gv * gv)
        m_hat = mn / (1.0 - ADAM_B1 ** ADAM_STEP)
        v_hat = vn / (1.0 - ADAM_B2 ** ADAM_STEP)
        d_ref[...] = -ADAM_LR * (m_hat / (jnp.sqrt(v_hat) + ADAM_EPS) + ADAM_WD * w_ref[...])
        nm_ref[...] = mn
        nv_ref[...] = vn

    spec = pl.BlockSpec((rt, cols), lambda i: (i, 0))
    outs = pl.pallas_call(
        body, name=name, grid=(rows // rt,),
        out_shape=[jax.ShapeDtypeStruct((rows, cols), F32)] * 3,
        in_specs=[spec] * 4, out_specs=[spec] * 3,
        compiler_params=_params("parallel"),
    )(*[a.reshape(rows, cols) for a in (w, g, m, v)])
    return [o.reshape(shape) for o in outs]


SMALL_ROWS = 824


def _pack_small(g):
    parts = [g["pre1"].reshape(8, 128), g["post0"].reshape(8, 128),
             g["post1"].reshape(8, 128), g["sinks"], jnp.pad(g["pool_scale"].reshape(4, 128), ((0, 4), (0, 0))),
             g["pool_w"], g["dw"].reshape(248, 128), g["dwb"].reshape(8, 128), g["lng"].reshape(8, 128),
             g["lnb"].reshape(8, 128)]
    assert sum(p.shape[0] for p in parts) == SMALL_ROWS
    return jnp.concatenate(parts, axis=0)


def _unpack_small(s):
    out, r = {}, 0
    for key, rows, shape in (("pre1", 8, (1, D_MODEL)), ("post", 16, (2, D_MODEL)), ("sinks", 8, (8, 128)),
                             ("pool_scale", 4, (1, 512)), ("pad", 4, (4, 128)), ("pool_w", 512, (1, 4, 128, 128)),
                             ("dw", 248, (CONV_K, D_MODEL)), ("dwb", 8, (1, D_MODEL)), ("lng", 8, (1, D_MODEL)),
                             ("lnb", 8, (1, D_MODEL))):
        out[key] = s[r:r + rows].reshape(shape)
        r += rows
    return out


def kernel(x, pre_norm, post_norm, a_w_in, a_sinks, b_pool_w, b_pool_scale, ab_w_out, c_w_in, c_dw_w, c_dw_b, c_ln_g, c_ln_b, c_w_out, loss_target, m_pre_norm, m_post_norm, m_a_w_in, m_a_sinks, m_b_pool_w, m_b_pool_scale, m_ab_w_out, m_c_w_in, m_c_dw_w, m_c_dw_b, m_c_ln_g, m_c_ln_b, m_c_w_out, v_pre_norm, v_post_norm, v_a_w_in, v_a_sinks, v_b_pool_w, v_b_pool_scale, v_ab_w_out, v_c_w_in, v_c_dw_w, v_c_dw_b, v_c_ln_g, v_c_ln_b, v_c_w_out):
    ix, iy = lax.axis_index("x"), lax.axis_index("y")
    chip_cols = (2 * ix + iy) * 256

    pad8 = lambda v: jnp.pad(v, ((0, -v.shape[0] % 8), (0, 0)))
    vec_shard = jnp.concatenate([pad8(c_dw_w.reshape(CONV_K, 256)), pad8(c_dw_b), pad8(c_ln_g), pad8(c_ln_b),
                                 jnp.zeros((8, 256), F32)], axis=0)
    x0, target = x[0], loss_target[0]
    pre0, pre1 = pre_norm[0:1], pre_norm[1:2]
    post0, post1 = post_norm[0:1], post_norm[1:2]
    pool_w = b_pool_w[0]

    (wa_t,) = _run_comm(_Gather([a_w_in[0].T.astype(BF16)], halve=True), "gather_a_w_in")
    wa_t = wa_t.reshape(EVEN_IN, D_MODEL)
    proj0, (w_ab,) = _norm_matmul(x0, pre0, wa_t, "proj0_fwd", comm=_Gather([ab_w_out[0].astype(BF16)], halve=True))
    w_ab = w_ab.reshape(D_MODEL, D_MODEL)
    mix0, (wc_t, w_c, vecs) = _mix0_fwd(
        proj0, a_sinks, pool_w, b_pool_scale,
        comm=_Gather([c_w_in[0].T.astype(BF16), c_w_out[0].astype(BF16), vec_shard], halve=True))
    wc_t = wc_t.reshape(3 * D_MODEL, D_MODEL)
    w_c = w_c.reshape(D_MODEL, D_MODEL)
    vecs = vecs.reshape(4, 64, 256).transpose(1, 0, 2).reshape(64, D_MODEL)
    dw, dwb, lng, lnb = vecs[0:CONV_K], vecs[32:33], vecs[40:41], vecs[48:49]
    y0, x1 = _out_norm_res(mix0, w_ab, x0, post0, "out0_fwd")
    proj1, _ = _norm_matmul(x1, pre1, wc_t, "proj1_fwd")
    z1, cf1 = _conv_fwd(proj1, dw, dwb, lng, lnb)
    y1, g2, loss = _out_norm_res(z1, w_c, x1, post1, "out1_fwd", target=target)
    loss = lax.psum(loss[0, 0], ("x", "y", "c"))

    pieces = lambda m: m.reshape(N_DEV, m.shape[0] // N_DEV, D_MODEL)
    dz1, d_wc, d_post1 = _post_bwd(g2, y1, post1, w_c, z1, "out1_bwd")
    (dproj1, d_dw, d_dwb, d_lng, d_lnb), (r_wc,) = _conv_bwd(proj1, cf1, dz1, dw, lng, lnb,
                                                             comm=_Scatter([pieces(d_wc)]))
    g1, d_wct, d_pre1 = _pre_bwd(dproj1, wc_t, x1, pre1, g2, "proj1_bwd")
    dmix0, d_wab, d_post0 = _post_bwd(g1, y0, post0, w_ab, mix0, "out0_bwd")
    (dproj0, d_sinks, d_pw, d_ps), (r_wct, r_wab) = _mix0_bwd(proj0, dmix0, a_sinks, pool_w, b_pool_scale,
                                                             comm=_Scatter([pieces(d_wct), pieces(d_wab)]))
    g = dict(pre1=d_pre1, post0=d_post0, post1=d_post1, sinks=d_sinks, pool_w=d_pw, pool_scale=d_ps,
             dw=d_dw, dwb=d_dwb, lng=d_lng, lnb=d_lnb)
    d_wat, (small8,) = _proj_dw(dproj0, x0, pre0, "proj0_dw", comm=_Gather([_pack_small(g)], halve=False))
    (gx, d_pre0), (_, s_wat, r_wat) = _proj_dx(dproj0, wa_t, x0, pre0, g1, "proj0_dx",
                                               comm=_PairScatter([pieces(d_wat)]))
    (pre0_8,) = _run_comm(_Gather([d_pre0.reshape(8, 128)], halve=False), "pre0_all_gather")

    ic = lax.axis_index("c")
    place = jnp.stack([4 * ix + 2 * iy + ic, ic, 2 * ix + iy]).astype(jnp.int32)
    parts = [_pair_parts(s_wat, r_wat), _direct_parts(pieces(d_wab), r_wab), _direct_parts(pieces(d_wct), r_wct),
             _direct_parts(pieces(d_wc), r_wc)]
    halves = [_piece_sum(p, place, f"grad_sum{t}") for t, p in enumerate(parts)]
    g_wa_t, g_wab, g_wc_t, g_wc = [h.reshape(2 * h.shape[1], D_MODEL) for h in _share_with_sibling(halves, "grad_share")]
    g_a_w_in = g_wa_t.T[None]
    g_c_w_in = g_wc_t.T[None]
    g_ab_w_out = g_wab[None]
    g_c_w_out = g_wc[None]

    s = _unpack_small(_sum8(small8, "small_sum"))
    layer = lax.broadcasted_iota(jnp.int32, (2, D_MODEL), 0)
    g_pre = jnp.where(layer == 0, _sum8(pre0_8, "pre0_sum").reshape(1, D_MODEL), s["pre1"])
    g_post = s["post"]
    g_sinks = s["sinks"][:, 0].reshape(1, 8)
    g_pool_w, g_pool_scale = s["pool_w"], s["pool_scale"]
    g_dw = lax.dynamic_slice_in_dim(s["dw"], chip_cols, 256, axis=1).reshape(1, CONV_K, 1, 256)
    g_dwb = lax.dynamic_slice_in_dim(s["dwb"], chip_cols, 256, axis=1)
    g_lng = lax.dynamic_slice_in_dim(s["lng"], chip_cols, 256, axis=1)
    g_lnb = lax.dynamic_slice_in_dim(s["lnb"], chip_cols, 256, axis=1)

    grads = [g_pre, g_post, g_a_w_in, g_sinks, g_pool_w, g_pool_scale, g_ab_w_out, g_c_w_in, g_dw, g_dwb, g_lng, g_lnb,
             g_c_w_out]
    weights = [pre_norm, post_norm, a_w_in, a_sinks, b_pool_w, b_pool_scale, ab_w_out, c_w_in, c_dw_w, c_dw_b, c_ln_g,
               c_ln_b, c_w_out]
    moms = [m_pre_norm, m_post_norm, m_a_w_in, m_a_sinks, m_b_pool_w, m_b_pool_scale, m_ab_w_out, m_c_w_in, m_c_dw_w,
            m_c_dw_b, m_c_ln_g, m_c_ln_b, m_c_w_out]
    vars_ = [v_pre_norm, v_post_norm, v_a_w_in, v_a_sinks, v_b_pool_w, v_b_pool_scale, v_ab_w_out, v_c_w_in, v_c_dw_w,
             v_c_dw_b, v_c_ln_g, v_c_ln_b, v_c_w_out]
    deltas, new_m, new_v = [], [], []
    for k, (w, gr, m, v) in enumerate(zip(weights, grads, moms, vars_)):
        d, nm, nv = _adamw(w, gr, m, v, f"adamw{k}")
        deltas.append(d)
        new_m.append(nm)
        new_v.append(nv)
    return (loss, gx[None], *grads, *deltas, *new_m, *new_v)
```

```python
import functools

import jax
import jax.numpy as jnp
from jax import lax
from jax.experimental import pallas as pl
from jax.experimental.pallas import tpu as pltpu

F32 = jnp.float32
BF16 = jnp.bfloat16

D_MODEL = 1024
EPS = 1e-6
NEG = -1e30
HEAD_DIM = 64
GROUP = 4
KV_HEADS = 2
BLOCK = 128
EVEN_IN = 2304
COL_Q, COL_K, COL_GA, COL_U, COL_GB = 0, 512, 768, 1280, 1792
POOL_GROUPS = 4
POOL_GC = 128
POOL_HALO = 16
CONV_K = 31
CONV_HALO = 32
N_DEV = 8

ADAM_LR = 0.001
ADAM_B1 = 0.9
ADAM_B2 = 0.999
ADAM_EPS = 1e-08
ADAM_WD = 0.01
ADAM_STEP = 10

VMEM_LIMIT_BYTES = 56 * 1024 * 1024

NT = (((1,), (1,)), ((), ()))
TN = (((0,), (0,)), ((), ()))
MESH_ID = pl.DeviceIdType.MESH


def _params(*sem):
    return pltpu.CompilerParams(dimension_semantics=sem, vmem_limit_bytes=VMEM_LIMIT_BYTES)


def _const_spec(shape):
    nd = len(shape)
    return pl.BlockSpec(shape, lambda *_: (0,) * nd, pipeline_mode=pl.Buffered(1))


def _sigmoid(v):
    return 0.5 * jnp.tanh(0.5 * v) + 0.5


def _silu(v):
    return v * _sigmoid(v)


def _silu_and_grad(v):
    s = _sigmoid(v)
    return v * s, s * (1.0 + v * (1.0 - s))


ANY = pl.BlockSpec(memory_space=pl.ANY)


def _place():
    x, y, c = lax.axis_index("x"), lax.axis_index("y"), lax.axis_index("c")
    chips = [(1 - x, y), (x, 1 - y), (1 - x, 1 - y)]
    return x, y, c, chips


class _Gather:
    def __init__(self, blocks, halve):
        self.ins = list(blocks)
        self.halve = halve
        self.n = n = len(blocks)
        self.shapes = [((b.shape[0] // 2) if halve else b.shape[0], b.shape[1]) for b in blocks]
        self.out_shape = [jax.ShapeDtypeStruct((N_DEV, r, cols), b.dtype) for (r, cols), b in zip(self.shapes, blocks)]
        self.scratch = [pltpu.SemaphoreType.DMA((7 * n,)), pltpu.SemaphoreType.DMA((7 * n,)),
                        pltpu.SemaphoreType.DMA((n,))]

    def _copies(self, ins, outs, sems):
        send_sems, recv_sems, local_sems = sems
        x, y, c, chips = _place()
        me, sibling = (x, y, c), (x, y, 1 - c)

        def piece(t, px, py, pc):
            return outs[t].at[4 * px + 2 * py + pc]

        def own(t):
            return ins[t].at[pl.ds(c * self.shapes[t][0], self.shapes[t][0])] if self.halve else ins[t]

        def copy(t, k, block, to, src=None):
            return pltpu.make_async_remote_copy(
                src_ref=piece(t, *block) if src is None else src, dst_ref=piece(t, *block),
                send_sem=send_sems.at[7 * t + k], recv_sem=recv_sems.at[7 * t + k],
                device_id=to, device_id_type=MESH_ID)

        rng = range(self.n)
        return dict(
            mine=[pltpu.make_async_copy(own(t), piece(t, *me), local_sems.at[t]) for t in rng],
            first=[copy(t, 0, me, sibling, src=own(t)) for t in rng]
            + [copy(t, 1 + j, me, (*chip, c), src=own(t)) for t in rng for j, chip in enumerate(chips)],
            landed=[copy(t, 1 + j, (*chip, c), me) for j, chip in enumerate(chips) for t in rng],
            passed=[copy(t, 4 + j, (*chip, c), sibling) for j, chip in enumerate(chips) for t in rng],
            from_sibling=[copy(t, 0, sibling, me) for t in rng]
            + [copy(t, 4 + j, (*chip, 1 - c), me) for t in rng for j, chip in enumerate(chips)])

    def start(self, ins, outs, sems):
        d = self._copies(ins, outs, sems)
        for cp in d["mine"] + d["first"]:
            cp.start()

    def middle(self, ins, outs, sems):
        d = self._copies(ins, outs, sems)
        for got, fwd in zip(d["landed"], d["passed"]):
            got.wait_recv()
            fwd.start()

    def finish(self, ins, outs, sems):
        d = self._copies(ins, outs, sems)
        for cp in d["from_sibling"]:
            cp.wait_recv()
        for cp in d["first"] + d["passed"]:
            cp.wait_send()
        for cp in d["mine"]:
            cp.wait()


class _Scatter:
    def __init__(self, tensors):
        self.ins = list(tensors)
        self.n = n = len(tensors)
        self.out_shape = [jax.ShapeDtypeStruct(t.shape, t.dtype) for t in tensors]
        self.scratch = [pltpu.SemaphoreType.DMA((7 * n,)), pltpu.SemaphoreType.DMA((7 * n,))]

    def _copies(self, ins, outs, sems):
        send_sems, recv_sems = sems
        x, y, c, _ = _place()
        me = 4 * x + 2 * y + c
        sends, recvs = [], []
        for t in range(self.n):
            for m in range(1, N_DEV):
                px, py, pc = x ^ (m >> 2), y ^ ((m >> 1) & 1), c ^ (m & 1)
                q = 4 * px + 2 * py + pc
                sems_k = dict(send_sem=send_sems.at[7 * t + m - 1], recv_sem=recv_sems.at[7 * t + m - 1],
                              device_id=(px, py, pc), device_id_type=MESH_ID)
                sends.append(pltpu.make_async_remote_copy(src_ref=ins[t].at[q], dst_ref=outs[t].at[me], **sems_k))
                recvs.append(pltpu.make_async_remote_copy(src_ref=ins[t].at[me], dst_ref=outs[t].at[q], **sems_k))
        return sends, recvs

    def start(self, ins, outs, sems):
        for cp in self._copies(ins, outs, sems)[0]:
            cp.start()

    def middle(self, ins, outs, sems):
        pass

    def finish(self, ins, outs, sems):
        sends, recvs = self._copies(ins, outs, sems)
        for cp in recvs:
            cp.wait_recv()
        for cp in sends:
            cp.wait_send()


class _PairScatter:
    middle_step = 2

    def __init__(self, tensors):
        self.ins = list(tensors)
        self.n = n = len(tensors)
        r = tensors[0].shape[1]
        assert all(t.shape == (N_DEV, r, D_MODEL) and t.dtype == BF16 for t in tensors)
        quarter = lambda k: [jax.ShapeDtypeStruct((k, r, D_MODEL), BF16) for _ in tensors]
        self.out_shape = quarter(4) + quarter(4) + quarter(3)
        self.scratch = [pltpu.SemaphoreType.DMA((4 * n,)), pltpu.SemaphoreType.DMA((4 * n,)),
                        pltpu.SemaphoreType.DMA((3 * n,)), pltpu.SemaphoreType.DMA((3 * n,)),
                        pltpu.SemaphoreType.DMA((2,)), pltpu.VMEM((r, D_MODEL), BF16), pltpu.VMEM((r, D_MODEL), BF16)]

    def _copies(self, ins, outs, sems):
        n = self.n
        pair, sums, recv = outs[:n], outs[n:2 * n], outs[2 * n:]
        x, y, c, chips = _place()
        swap = [pltpu.make_async_remote_copy(
            src_ref=ins[t].at[2 * b + 1 - c], dst_ref=pair[t].at[b], send_sem=sems[0].at[4 * t + b],
            recv_sem=sems[1].at[4 * t + b], device_id=(x, y, 1 - c), device_id_type=MESH_ID)
            for t in range(n) for b in range(4)]
        to_owner = [pltpu.make_async_remote_copy(
            src_ref=sums[t].at[2 * cx + cy], dst_ref=recv[t].at[j], send_sem=sems[2].at[3 * t + j],
            recv_sem=sems[3].at[3 * t + j], device_id=(cx, cy, c), device_id_type=MESH_ID)
            for t in range(n) for j, (cx, cy) in enumerate(chips)]
        return swap, to_owner

    def start(self, ins, outs, sems):
        for cp in self._copies(ins, outs, sems)[0]:
            cp.start()

    def middle(self, ins, outs, sems):
        n = self.n
        pair, sums = outs[:n], outs[n:2 * n]
        local, mine_ref, theirs_ref = sems[4], sems[5], sems[6]
        c = lax.axis_index("c")
        swap, to_owner = self._copies(ins, outs, sems)
        for cp in swap:
            cp.wait_recv()
        for t in range(n):
            for b in range(4):
                loads = [pltpu.make_async_copy(ins[t].at[2 * b + c], mine_ref, local.at[0]),
                         pltpu.make_async_copy(pair[t].at[b], theirs_ref, local.at[1])]
                for cp in loads:
                    cp.start()
                for cp in loads:
                    cp.wait()
                mine_ref[...] = (mine_ref[...].astype(F32) + theirs_ref[...].astype(F32)).astype(BF16)
                store = pltpu.make_async_copy(mine_ref, sums[t].at[b], local.at[0])
                store.start()
                store.wait()
        for cp in to_owner:
            cp.start()
        for cp in swap:
            cp.wait_send()

    def finish(self, ins, outs, sems):
        to_owner = self._copies(ins, outs, sems)[1]
        for cp in to_owner:
            cp.wait_recv()
        for cp in to_owner:
            cp.wait_send()


class _Comms:
    def __init__(self, *comms):
        self.comms = comms
        self.ins = [a for c in comms for a in c.ins]
        self.out_shape = [s for c in comms for s in c.out_shape]
        self.scratch = [s for c in comms for s in c.scratch]

    def _each(self, phase, ins, outs, sems):
        i = o = s = 0
        for c in self.comms:
            ni, no, ns = len(c.ins), len(c.out_shape), len(c.scratch)
            getattr(c, phase)(ins[i:i + ni], outs[o:o + no], sems[s:s + ns])
            i, o, s = i + ni, o + no, s + ns

    def start(self, ins, outs, sems):
        self._each("start", ins, outs, sems)

    def middle(self, ins, outs, sems):
        self._each("middle", ins, outs, sems)

    def finish(self, ins, outs, sems):
        self._each("finish", ins, outs, sems)


def _run_comm(comm, name):
    n = len(comm.ins)

    def body(*refs):
        parts = refs[:n], refs[n:2 * n], refs[2 * n:]
        comm.start(*parts)
        comm.middle(*parts)
        comm.finish(*parts)

    return pl.pallas_call(body, name=name, out_shape=comm.out_shape, in_specs=[ANY] * n, out_specs=[ANY] * n,
                          scratch_shapes=comm.scratch)(*comm.ins)


def _fused_call(body, comm, args, *, name, grid, out_shape, in_specs, out_specs, scratch_shapes=(), params):
    single = not isinstance(out_shape, (list, tuple))
    out_shape = [out_shape] if single else list(out_shape)
    out_specs = [out_specs] if single else list(out_specs)
    if comm is None:
        res = pl.pallas_call(body, name=name, grid=grid, out_shape=out_shape, in_specs=in_specs, out_specs=out_specs,
                             scratch_shapes=list(scratch_shapes), compiler_params=params)(*args)
        return (res[0] if single else res), []
    n_in, n_out, n_scr = len(in_specs), len(out_shape), len(scratch_shapes)
    c_in, c_out = len(comm.ins), len(comm.out_shape)
    steps = grid[0]

    def fused(*refs):
        pos = 0
        groups = []
        for size in (n_in, c_in, n_out, c_out, n_scr, len(comm.scratch)):
            groups.append(refs[pos:pos + size])
            pos += size
        ins, c_ins, outs, c_outs, scr, c_sems = groups
        i = pl.program_id(0)

        @pl.when(i == 0)
        def _():
            comm.start(c_ins, c_outs, c_sems)

        @pl.when(i == min(getattr(comm, "middle_step", steps // 2), steps - 1))
        def _():
            comm.middle(c_ins, c_outs, c_sems)

        body(*ins, *outs, *scr)

        @pl.when(i == steps - 1)
        def _():
            comm.finish(c_ins, c_outs, c_sems)

    res = pl.pallas_call(
        fused, name=name, grid=grid, out_shape=out_shape + list(comm.out_shape),
        in_specs=list(in_specs) + [ANY] * c_in, out_specs=out_specs + [ANY] * c_out,
        scratch_shapes=list(scratch_shapes) + list(comm.scratch), compiler_params=params)(*args, *comm.ins)
    main = res[:n_out]
    return (main[0] if single else main), list(res[n_out:])


def _norm_matmul(x, gain, wt, name, comm=None, tm=512):
    t, n = x.shape[0], wt.shape[0]

    def body(x_ref, g_ref, wt_ref, o_ref):
        xv = x_ref[...]
        r = lax.rsqrt(jnp.mean(xv * xv, axis=-1, keepdims=True) + EPS)
        h = (xv * r * g_ref[...]).astype(BF16)
        o_ref[...] = lax.dot_general(h, wt_ref[...], NT, preferred_element_type=F32)

    return _fused_call(
        body, comm, (x, gain, wt), name=name, grid=(t // tm,),
        out_shape=jax.ShapeDtypeStruct((t, n), F32),
        in_specs=[pl.BlockSpec((tm, D_MODEL), lambda i: (i, 0)), _const_spec((1, D_MODEL)), _const_spec((n, D_MODEL))],
        out_specs=pl.BlockSpec((tm, n), lambda i: (i, 0)),
        params=_params("arbitrary"))


def _out_norm_res(a, w, x_in, post, name, target=None, tm=512):
    t = a.shape[0]
    with_loss = target is not None

    def body(*refs):
        if with_loss:
            a_ref, w_ref, x_ref, p_ref, t_ref, y_ref, o_ref, l_ref = refs
        else:
            a_ref, w_ref, x_ref, p_ref, y_ref, o_ref = refs
        y = jnp.dot(a_ref[...], w_ref[...], preferred_element_type=F32)
        y_ref[...] = y
        ry = lax.rsqrt(jnp.mean(y * y, axis=-1, keepdims=True) + EPS)
        xo = x_ref[...] + (y * ry) * p_ref[...]
        if with_loss:
            d = xo - t_ref[...]
            o_ref[...] = d * (1.0 / D_MODEL)

            @pl.when(pl.program_id(0) == 0)
            def _():
                l_ref[...] = jnp.zeros_like(l_ref)

            l_ref[...] += 0.5 * jnp.sum(jnp.mean(d * d, axis=-1, keepdims=True))
        else:
            o_ref[...] = xo

    row = pl.BlockSpec((tm, D_MODEL), lambda i: (i, 0))
    in_specs = [row, _const_spec((D_MODEL, D_MODEL)), row, _const_spec((1, D_MODEL))]
    out_shape = [jax.ShapeDtypeStruct((t, D_MODEL), F32), jax.ShapeDtypeStruct((t, D_MODEL), F32)]
    out_specs = [row, row]
    args = [a, w, x_in, post]
    if with_loss:
        in_specs.append(row)
        args.append(target)
        out_shape.append(jax.ShapeDtypeStruct((8, 128), F32))
        out_specs.append(pl.BlockSpec((8, 128), lambda i: (0, 0)))
    return pl.pallas_call(
        body, name=name, grid=(t // tm,), out_shape=out_shape, in_specs=in_specs, out_specs=out_specs,
        compiler_params=_params("arbitrary"),
    )(*args)


def _post_bwd(g, y, post, w, a, name, tm=512):
    t = g.shape[0]
    steps = t // tm

    def body(g_ref, y_ref, p_ref, w_ref, a_ref, da_ref, dw16_ref, dp_ref, dw_ref):
        @pl.when(pl.program_id(0) == 0)
        def _():
            dw_ref[...] = jnp.zeros_like(dw_ref)
            dp_ref[...] = jnp.zeros_like(dp_ref)

        gv = g_ref[...]
        yv = y_ref[...]
        ry = lax.rsqrt(jnp.mean(yv * yv, axis=-1, keepdims=True) + EPS)
        nv = yv * ry
        dp_ref[...] += jnp.sum(gv * nv, axis=0, keepdims=True)
        dn = gv * p_ref[...]
        dy = (ry * (dn - nv * jnp.mean(dn * nv, axis=-1, keepdims=True))).astype(BF16)
        da_ref[...] = lax.dot_general(dy, w_ref[...], NT, preferred_element_type=F32)
        dw_ref[...] += lax.dot_general(a_ref[...], dy, TN, preferred_element_type=F32)

        @pl.when(pl.program_id(0) == steps - 1)
        def _():
            dw16_ref[...] = dw_ref[...].astype(BF16)

    row = pl.BlockSpec((tm, D_MODEL), lambda i: (i, 0))
    return pl.pallas_call(
        body, name=name, grid=(steps,),
        out_shape=[jax.ShapeDtypeStruct((t, D_MODEL), F32), jax.ShapeDtypeStruct((D_MODEL, D_MODEL), BF16),
                   jax.ShapeDtypeStruct((1, D_MODEL), F32)],
        in_specs=[row, row, _const_spec((1, D_MODEL)), _const_spec((D_MODEL, D_MODEL)), row],
        out_specs=[row, pl.BlockSpec((D_MODEL, D_MODEL), lambda i: (0, 0)), pl.BlockSpec((1, D_MODEL), lambda i: (0, 0))],
        scratch_shapes=[pltpu.VMEM((D_MODEL, D_MODEL), F32)],
        compiler_params=_params("arbitrary"),
    )(g, y, post, w, a)


def _pre_bwd(dproj, wt, x_in, pre, g, name, tm=256):
    t, n = dproj.shape
    steps = t // tm

    def body(dp_ref, wt_ref, x_ref, pre_ref, g_ref, dx_ref, dwt16_ref, dpre_ref, dwt_ref):
        @pl.when(pl.program_id(0) == 0)
        def _():
            dwt_ref[...] = jnp.zeros_like(dwt_ref)
            dpre_ref[...] = jnp.zeros_like(dpre_ref)

        dpv = dp_ref[...]
        dh = jnp.dot(dpv, wt_ref[...], preferred_element_type=F32)
        xv = x_ref[...]
        r = lax.rsqrt(jnp.mean(xv * xv, axis=-1, keepdims=True) + EPS)
        xn = xv * r
        pv = pre_ref[...]
        dpre_ref[...] += jnp.sum(dh * xn, axis=0, keepdims=True)
        dxn = dh * pv
        dx_ref[...] = g_ref[...] + r * (dxn - xn * jnp.mean(dxn * xn, axis=-1, keepdims=True))
        h = (xn * pv).astype(BF16)
        dwt_ref[...] += lax.dot_general(dpv, h, TN, preferred_element_type=F32)

        @pl.when(pl.program_id(0) == steps - 1)
        def _():
            dwt16_ref[...] = dwt_ref[...].astype(BF16)

    row = pl.BlockSpec((tm, D_MODEL), lambda i: (i, 0))
    return pl.pallas_call(
        body, name=name, grid=(steps,),
        out_shape=[jax.ShapeDtypeStruct((t, D_MODEL), F32), jax.ShapeDtypeStruct((n, D_MODEL), BF16),
                   jax.ShapeDtypeStruct((1, D_MODEL), F32)],
        in_specs=[pl.BlockSpec((tm, n), lambda i: (i, 0)), _const_spec((n, D_MODEL)), row, _const_spec((1, D_MODEL)), row],
        out_specs=[row, pl.BlockSpec((n, D_MODEL), lambda i: (0, 0)), pl.BlockSpec((1, D_MODEL), lambda i: (0, 0))],
        scratch_shapes=[pltpu.VMEM((n, D_MODEL), F32)],
        compiler_params=_params("arbitrary"),
    )(dproj, wt, x_in, pre, g)


def _proj_dw(dproj, x_in, pre, name, comm=None, tm=512):
    t, n = dproj.shape
    steps = t // tm

    def body(dp_ref, x_ref, pre_ref, dwt16_ref, dwt_ref):
        @pl.when(pl.program_id(0) == 0)
        def _():
            dwt_ref[...] = jnp.zeros_like(dwt_ref)

        xv = x_ref[...]
        r = lax.rsqrt(jnp.mean(xv * xv, axis=-1, keepdims=True) + EPS)
        h = (xv * r * pre_ref[...]).astype(BF16)
        dwt_ref[...] += lax.dot_general(dp_ref[...], h, TN, preferred_element_type=F32)

        @pl.when(pl.program_id(0) == steps - 1)
        def _():
            dwt16_ref[...] = dwt_ref[...].astype(BF16)

    return _fused_call(
        body, comm, (dproj, x_in, pre), name=name, grid=(steps,),
        out_shape=jax.ShapeDtypeStruct((n, D_MODEL), BF16),
        in_specs=[pl.BlockSpec((tm, n), lambda i: (i, 0)), pl.BlockSpec((tm, D_MODEL), lambda i: (i, 0)),
                  _const_spec((1, D_MODEL))],
        out_specs=pl.BlockSpec((n, D_MODEL), lambda i: (0, 0)),
        scratch_shapes=[pltpu.VMEM((n, D_MODEL), F32)],
        params=_params("arbitrary"))


def _proj_dx(dproj, wt, x_in, pre, g, name, comm=None, tm=512):
    t, n = dproj.shape

    def body(dp_ref, wt_ref, x_ref, pre_ref, g_ref, dx_ref, dpre_ref):
        @pl.when(pl.program_id(0) == 0)
        def _():
            dpre_ref[...] = jnp.zeros_like(dpre_ref)

        dh = jnp.dot(dp_ref[...], wt_ref[...], preferred_element_type=F32)
        xv = x_ref[...]
        r = lax.rsqrt(jnp.mean(xv * xv, axis=-1, keepdims=True) + EPS)
        xn = xv * r
        dpre_ref[...] += jnp.sum(dh * xn, axis=0, keepdims=True)
        dxn = dh * pre_ref[...]
        dx_ref[...] = g_ref[...] + r * (dxn - xn * jnp.mean(dxn * xn, axis=-1, keepdims=True))

    row = pl.BlockSpec((tm, D_MODEL), lambda i: (i, 0))
    return _fused_call(
        body, comm, (dproj, wt, x_in, pre, g), name=name, grid=(t // tm,),
        out_shape=[jax.ShapeDtypeStruct((t, D_MODEL), F32), jax.ShapeDtypeStruct((1, D_MODEL), F32)],
        in_specs=[pl.BlockSpec((tm, n), lambda i: (i, 0)), _const_spec((n, D_MODEL)), row, _const_spec((1, D_MODEL)), row],
        out_specs=[row, pl.BlockSpec((1, D_MODEL), lambda i: (0, 0))],
        params=_params("arbitrary"))


def _group_masks():
    lane = lax.broadcasted_iota(jnp.int32, (1, GROUP * HEAD_DIM), 1)
    return [(lane // HEAD_DIM == g).astype(F32) for g in range(GROUP)]


def _stack_groups(v, masks):
    return jnp.concatenate([v * m for m in masks], axis=0)


def _unstack_groups(v, masks):
    out = v[0:BLOCK] * masks[0]
    for g in range(1, GROUP):
        out = out + v[g * BLOCK:(g + 1) * BLOCK] * masks[g]
    return out


def _repeat_head(kv2, kvh):
    first = lax.broadcasted_iota(jnp.int32, kv2.shape, 1) < HEAD_DIM
    rolled = pltpu.roll(kv2, HEAD_DIM, 1)
    one = jnp.where(first, kv2, rolled) if kvh == 0 else jnp.where(first, rolled, kv2)
    return jnp.concatenate([one, one], axis=1)


def _fold_head(v4):
    a = v4[:, 0:128] + v4[:, 128:256]
    return a + pltpu.roll(a, HEAD_DIM, 1)


ATTN_CONSTS = [pltpu.VMEM((KV_HEADS, GROUP * BLOCK, 2 * BLOCK), F32)]


def _fill_attn_bias(bias_ref):
    row = lax.broadcasted_iota(jnp.int32, (GROUP * BLOCK, 2 * BLOCK), 0)
    col = lax.broadcasted_iota(jnp.int32, (GROUP * BLOCK, 2 * BLOCK), 1)
    dist = (row % BLOCK) + BLOCK - col
    band = (dist >= 0) & (dist < BLOCK)
    rb = lax.broadcasted_iota(jnp.int32, (GROUP * BLOCK, 1), 0) // BLOCK
    for kvh in range(KV_HEADS):
        slope = jnp.zeros((GROUP * BLOCK, 1), F32)
        for g in range(GROUP):
            slope = jnp.where(rb == g, 2.0 ** (-(kvh * GROUP + g + 1)), slope)
        bias_ref[kvh] = jnp.where(band, -slope * dist.astype(F32), NEG)


def _row_sinks(kvh, sink_ref):
    rb = lax.broadcasted_iota(jnp.int32, (GROUP * BLOCK, 1), 0) // BLOCK
    sink = jnp.zeros((GROUP * BLOCK, 1), F32)
    for g in range(GROUP):
        sink = jnp.where(rb == g, sink_ref[0, kvh * GROUP + g], sink)
    return sink


def _attn_probs(qk, k4, bias, sink, no_past, masks):
    qs = _stack_groups(qk, masks).astype(BF16)
    s = lax.dot_general(qs, k4, NT, preferred_element_type=F32) * (HEAD_DIM ** -0.5) + bias
    s = jnp.concatenate([jnp.where(no_past, NEG, s[:, 0:BLOCK]), s[:, BLOCK:]], axis=1)
    mx = jnp.maximum(jnp.max(s, axis=-1, keepdims=True), sink)
    e = jnp.exp(s - mx)
    es = jnp.exp(sink - mx)
    inv = 1.0 / (jnp.sum(e, axis=-1, keepdims=True) + es)
    return qs, e * inv, es * inv


def _pool_forward(u_ext, g, t0):
    n = u_ext.shape[0] - POOL_HALO
    s = u_ext
    for step in range(g + 1):
        s = s + pltpu.roll(s, 1 << step, 0)
    w = 2 << g
    t = t0 + lax.broadcasted_iota(jnp.int32, (n, 1), 0)
    cnt = jnp.minimum(t + 1, w).astype(F32)
    return s[POOL_HALO:] / cnt - u_ext[POOL_HALO:]


def _mix0_fwd(proj, sinks, pool_w, pool_scale, comm=None, tq=512):
    t = proj.shape[0]
    nblk = tq // BLOCK

    def body(main_ref, halo_ref, sink_ref, pw_ref, ps_ref, o_ref, kv_ref, bias_ref):
        i = pl.program_id(0)
        t0 = i * tq
        masks = _group_masks()

        @pl.when(i == 0)
        def _():
            _fill_attn_bias(bias_ref)

        kv_ref[0:BLOCK, :] = halo_ref[:, COL_K:COL_K + 256]
        kv_ref[BLOCK:, :] = main_ref[:, COL_K:COL_K + 256]

        def block(jb, carry):
            r0 = pl.multiple_of(jb * BLOCK, BLOCK)
            no_past = t0 + r0 == 0
            q = main_ref[pl.ds(r0, BLOCK), COL_Q:COL_Q + 512]
            ga = main_ref[pl.ds(r0, BLOCK), COL_GA:COL_GA + 512]
            kk = kv_ref[pl.ds(r0, 2 * BLOCK), 0:128]
            vv = kv_ref[pl.ds(r0, 2 * BLOCK), 128:256]
            outs = []
            for kvh in range(KV_HEADS):
                k4 = _repeat_head(kk, kvh).astype(BF16)
                v4 = _repeat_head(vv, kvh).astype(BF16)
                _, p, _ = _attn_probs(q[:, kvh * 256:(kvh + 1) * 256], k4, bias_ref[kvh], _row_sinks(kvh, sink_ref), no_past, masks)
                pv = jnp.dot(p.astype(BF16), v4, preferred_element_type=F32)
                outs.append(_unstack_groups(pv, masks))
            attn = jnp.concatenate(outs, axis=1)
            o_ref[pl.ds(r0, BLOCK), 0:512] = (attn * _silu(ga)).astype(BF16)
            return carry

        lax.fori_loop(0, nblk, block, 0, unroll=2)

        for g in range(POOL_GROUPS):
            cu = COL_U + g * POOL_GC
            cg = COL_GB + g * POOL_GC
            halo_u = jnp.where(i == 0, 0.0, halo_ref[BLOCK - POOL_HALO:BLOCK, cu:cu + POOL_GC])
            u_ext = jnp.concatenate([halo_u, main_ref[:, cu:cu + POOL_GC]], axis=0)
            pooled = _pool_forward(u_ext, g, t0)
            y = jnp.dot(pooled.astype(BF16), pw_ref[g].astype(BF16), preferred_element_type=F32)
            y = y * ps_ref[:, g * POOL_GC:(g + 1) * POOL_GC]
            o_ref[:, 512 + g * POOL_GC:512 + (g + 1) * POOL_GC] = (y * _silu(main_ref[:, cg:cg + POOL_GC])).astype(BF16)

    return _fused_call(
        body, comm, (proj, proj, sinks, pool_w, pool_scale), name="mix0_fwd", grid=(t // tq,),
        out_shape=jax.ShapeDtypeStruct((t, D_MODEL), BF16),
        in_specs=[pl.BlockSpec((tq, EVEN_IN), lambda i: (i, 0)),
                  pl.BlockSpec((BLOCK, EVEN_IN), lambda i: (jnp.maximum(i * nblk - 1, 0), 0)),
                  pl.BlockSpec(memory_space=pltpu.SMEM),
                  _const_spec((POOL_GROUPS, POOL_GC, POOL_GC)), _const_spec((1, 512))],
        out_specs=pl.BlockSpec((tq, D_MODEL), lambda i: (i, 0)),
        scratch_shapes=[pltpu.VMEM((tq + BLOCK, 256), F32)] + ATTN_CONSTS,
        params=_params("arbitrary"))


def _mix0_bwd(proj, dmix, sinks, pool_w, pool_scale, comm=None, tq=512):
    t = proj.shape[0]
    nt = t // tq
    nblk = tq // BLOCK

    def body(main_ref, halo_ref, next_ref, dm_ref, dmn_ref, sink_ref, pw_ref, ps_ref,
             o_ref, dsk_ref, dpw_ref, dps_ref, kv_ref, dkv_ref, carry_ref, bias_ref):
        i = pl.program_id(0)
        ii = nt - 1 - i
        t0 = ii * tq
        masks = _group_masks()

        @pl.when(i == 0)
        def _():
            _fill_attn_bias(bias_ref)
            dsk_ref[...] = jnp.zeros_like(dsk_ref)
            dpw_ref[...] = jnp.zeros_like(dpw_ref)
            dps_ref[...] = jnp.zeros_like(dps_ref)
            carry_ref[...] = jnp.zeros_like(carry_ref)

        kv_ref[0:BLOCK, :] = halo_ref[:, COL_K:COL_K + 256]
        kv_ref[BLOCK:, :] = main_ref[:, COL_K:COL_K + 256]
        dkv_ref[0:tq, :] = jnp.zeros((tq, 256), F32)
        dkv_ref[tq:, :] = carry_ref[...]

        def block(jb, carry):
            r0 = pl.multiple_of(jb * BLOCK, BLOCK)
            no_past = t0 + r0 == 0
            q = main_ref[pl.ds(r0, BLOCK), COL_Q:COL_Q + 512]
            ga = main_ref[pl.ds(r0, BLOCK), COL_GA:COL_GA + 512]
            dya = dm_ref[pl.ds(r0, BLOCK), 0:512]
            kk = kv_ref[pl.ds(r0, 2 * BLOCK), 0:128]
            vv = kv_ref[pl.ds(r0, 2 * BLOCK), 128:256]
            silu_ga, dsilu_ga = _silu_and_grad(ga)
            do = dya * silu_ga
            first = lax.broadcasted_iota(jnp.int32, (2 * BLOCK, 128), 1) < HEAD_DIM
            attn, dq, dk, dv = [], [], [], []
            for kvh in range(KV_HEADS):
                k4 = _repeat_head(kk, kvh).astype(BF16)
                v4 = _repeat_head(vv, kvh).astype(BF16)
                qs, p, ps = _attn_probs(q[:, kvh * 256:(kvh + 1) * 256], k4, bias_ref[kvh], _row_sinks(kvh, sink_ref), no_past, masks)
                pb = p.astype(BF16)
                o_k = _unstack_groups(jnp.dot(pb, v4, preferred_element_type=F32), masks)
                do_k = do[:, kvh * 256:(kvh + 1) * 256]
                dos = _stack_groups(do_k, masks).astype(BF16)
                prod = do_k * o_k
                delta = jnp.concatenate([jnp.sum(prod * m, axis=-1, keepdims=True) for m in masks], axis=0)
                dp = lax.dot_general(dos, v4, NT, preferred_element_type=F32)
                ds = (p * (dp - delta)).astype(BF16)
                sink_term = ps * delta
                for g in range(GROUP):
                    h = kvh * GROUP + g
                    dsk_ref[h:h + 1, :] -= jnp.sum(sink_term[g * BLOCK:(g + 1) * BLOCK], keepdims=True)
                scale = HEAD_DIM ** -0.5
                dq.append(_unstack_groups(jnp.dot(ds, k4, preferred_element_type=F32), masks) * scale)
                dk.append(_fold_head(lax.dot_general(ds, qs, TN, preferred_element_type=F32)) * scale)
                dv.append(_fold_head(lax.dot_general(pb, dos, TN, preferred_element_type=F32)))
                attn.append(o_k)
            o_ref[pl.ds(r0, BLOCK), COL_Q:COL_Q + 512] = jnp.concatenate(dq, axis=1).astype(BF16)
            o_all = jnp.concatenate(attn, axis=1)
            o_ref[pl.ds(r0, BLOCK), COL_GA:COL_GA + 512] = (dya * o_all * dsilu_ga).astype(BF16)
            dkv = jnp.concatenate([jnp.where(first, dk[0], dk[1]), jnp.where(first, dv[0], dv[1])], axis=1)
            dkv_ref[pl.ds(r0, 2 * BLOCK), :] += dkv
            return carry

        lax.fori_loop(0, nblk, block, 0, unroll=2)
        carry_ref[...] = dkv_ref[0:BLOCK, :]
        o_ref[:, COL_K:COL_K + 256] = dkv_ref[BLOCK:, :].astype(BF16)

        last = ii == nt - 1
        for g in range(POOL_GROUPS):
            cu = COL_U + g * POOL_GC
            cg = COL_GB + g * POOL_GC
            cm = 512 + g * POOL_GC
            pw = pw_ref[g].astype(BF16)
            sc = ps_ref[:, g * POOL_GC:(g + 1) * POOL_GC]
            halo_u = jnp.where(ii == 0, 0.0, halo_ref[BLOCK - POOL_HALO:BLOCK, cu:cu + POOL_GC])
            u_ext = jnp.concatenate([halo_u, main_ref[:, cu:cu + POOL_GC]], axis=0)
            pooled = _pool_forward(u_ext, g, t0).astype(BF16)
            y_raw = jnp.dot(pooled, pw, preferred_element_type=F32)
            gb = main_ref[:, cg:cg + POOL_GC]
            dyb = dm_ref[:, cm:cm + POOL_GC]
            silu_gb, dsilu_gb = _silu_and_grad(gb)
            dypool = dyb * silu_gb
            dps_ref[:, g * POOL_GC:(g + 1) * POOL_GC] += jnp.sum(dypool * y_raw, axis=0, keepdims=True)
            o_ref[:, cg:cg + POOL_GC] = (dyb * (y_raw * sc) * dsilu_gb).astype(BF16)
            dyraw = dypool * sc
            dyraw_n = jnp.where(last, 0.0, dmn_ref[:, cm:cm + POOL_GC] * _silu(next_ref[:, cg:cg + POOL_GC]) * sc)
            dpw_ref[g * POOL_GC:(g + 1) * POOL_GC, :] += lax.dot_general(pooled, dyraw.astype(BF16), TN,
                                                                         preferred_element_type=F32)
            dyraw_ext = jnp.concatenate([dyraw, dyraw_n], axis=0).astype(BF16)
            dpooled = lax.dot_general(dyraw_ext, pw, NT, preferred_element_type=F32)
            w = 2 << g
            tt = t0 + lax.broadcasted_iota(jnp.int32, (tq + POOL_HALO, 1), 0)
            s = dpooled / jnp.minimum(tt + 1, w).astype(F32)
            for step in range(g + 1):
                s = s + pltpu.roll(s, tq + POOL_HALO - (1 << step), 0)
            o_ref[:, cu:cu + POOL_GC] = (s[0:tq] - dpooled[0:tq]).astype(BF16)

    rev = lambda i: nt - 1 - i
    return _fused_call(
        body, comm, (proj, proj, proj, dmix, dmix, sinks, pool_w, pool_scale), name="mix0_bwd", grid=(nt,),
        out_shape=[jax.ShapeDtypeStruct((t, EVEN_IN), BF16), jax.ShapeDtypeStruct((8, 128), F32),
                   jax.ShapeDtypeStruct((POOL_GROUPS * POOL_GC, POOL_GC), F32), jax.ShapeDtypeStruct((1, 512), F32)],
        in_specs=[pl.BlockSpec((tq, EVEN_IN), lambda i: (rev(i), 0)),
                  pl.BlockSpec((BLOCK, EVEN_IN), lambda i: (jnp.maximum(rev(i) * nblk - 1, 0), 0)),
                  pl.BlockSpec((POOL_HALO, EVEN_IN),
                               lambda i: (jnp.minimum((rev(i) + 1) * (tq // POOL_HALO), t // POOL_HALO - 1), 0)),
                  pl.BlockSpec((tq, D_MODEL), lambda i: (rev(i), 0)),
                  pl.BlockSpec((POOL_HALO, D_MODEL),
                               lambda i: (jnp.minimum((rev(i) + 1) * (tq // POOL_HALO), t // POOL_HALO - 1), 0)),
                  pl.BlockSpec(memory_space=pltpu.SMEM),
                  _const_spec((POOL_GROUPS, POOL_GC, POOL_GC)), _const_spec((1, 512))],
        out_specs=[pl.BlockSpec((tq, EVEN_IN), lambda i: (rev(i), 0)),
                   pl.BlockSpec((8, 128), lambda i: (0, 0)),
                   pl.BlockSpec((POOL_GROUPS * POOL_GC, POOL_GC), lambda i: (0, 0)),
                   pl.BlockSpec((1, 512), lambda i: (0, 0))],
        scratch_shapes=[pltpu.VMEM((tq + BLOCK, 256), F32), pltpu.VMEM((tq + BLOCK, 256), F32),
                        pltpu.VMEM((BLOCK, 256), F32)] + ATTN_CONSTS,
        params=_params("arbitrary"))


CONV_RC = 32
CONV_CC = 128
CONV_CHAINS = 4
CONV_UNROLL = 2


def _fill_shifted(s_ref, rows):
    for b in range(1, 8):
        s_ref[b, 0:rows - 8, :] = s_ref[0, b:b + rows - 8, :]


def _tap_blocks(s_ref, r, cols, lead):
    for b in range(8):
        taps = [(a, 8 * a + b - lead) for a in range(5) if 0 <= 8 * a + b - lead < CONV_K]
        span = 8 * max(a for a, _ in taps) + CONV_RC
        blk = s_ref[b, pl.ds(r, span), cols]
        for a, k in taps:
            yield k, blk[8 * a:8 * a + CONV_RC]


def _conv_taps(s_ref, w_ref, r, cols, lead, reverse):
    accs = [None] * CONV_CHAINS
    for n, (k, blk) in enumerate(_tap_blocks(s_ref, r, cols, lead)):
        kw = CONV_K - 1 - k if reverse else k
        term = blk * w_ref[kw:kw + 1, cols]
        accs[n % CONV_CHAINS] = term if accs[n % CONV_CHAINS] is None else accs[n % CONV_CHAINS] + term
    return (accs[0] + accs[1]) + (accs[2] + accs[3])


def _layer_norm_fwd(cf, lng, lnb):
    mu = jnp.mean(cf, axis=-1, keepdims=True)
    xc = cf - mu
    rstd = lax.rsqrt(jnp.mean(xc * xc, axis=-1, keepdims=True) + EPS)
    chat = xc * rstd
    return chat, rstd, chat * lng + lnb


def _conv_fwd(proj, dw, dwb, lng, lnb, tt=256):
    t = proj.shape[0]
    lead = CONV_HALO - (CONV_K - 1)

    def body(main_ref, halo_ref, w_ref, b_ref, g_ref, lb_ref, o_ref, c_ref, gs_ref):
        i = pl.program_id(0)
        hv = halo_ref[...]
        gs_ref[0, 0:CONV_HALO, :] = jnp.where(i == 0, 0.0, hv[:, 0:1024] * _sigmoid(hv[:, 1024:2048]))
        gs_ref[0, CONV_HALO:CONV_HALO + tt, :] = main_ref[:, 0:1024] * _sigmoid(main_ref[:, 1024:2048])
        _fill_shifted(gs_ref, tt + CONV_HALO)

        for c in range(D_MODEL // CONV_CC):
            cols = slice(c * CONV_CC, (c + 1) * CONV_CC)

            def chunk(j, carry):
                r = pl.multiple_of(j * CONV_RC, CONV_RC)
                c_ref[pl.ds(r, CONV_RC), cols] = _conv_taps(gs_ref, w_ref, r, cols, lead, False) + b_ref[:, cols]
                return carry
            lax.fori_loop(0, tt // CONV_RC, chunk, 0, unroll=CONV_UNROLL)

        _, _, cn = _layer_norm_fwd(c_ref[...], g_ref[...], lb_ref[...])
        o_ref[...] = (_silu(cn) * _silu(main_ref[:, 2048:3072])).astype(BF16)

    vec = _const_spec((1, D_MODEL))
    row = pl.BlockSpec((tt, D_MODEL), lambda i: (i, 0))
    return pl.pallas_call(
        body, name="conv_fwd", grid=(t // tt,),
        out_shape=[jax.ShapeDtypeStruct((t, D_MODEL), BF16), jax.ShapeDtypeStruct((t, D_MODEL), F32)],
        in_specs=[pl.BlockSpec((tt, 3 * D_MODEL), lambda i: (i, 0)),
                  pl.BlockSpec((CONV_HALO, 3 * D_MODEL), lambda i: (jnp.maximum(i * (tt // CONV_HALO) - 1, 0), 0)),
                  _const_spec((CONV_K, D_MODEL)), vec, vec, vec],
        out_specs=[row, row],
        scratch_shapes=[pltpu.VMEM((8, tt + CONV_HALO, D_MODEL), F32)],
        compiler_params=_params("parallel"),
    )(proj, proj, dw, dwb, lng, lnb)


def _conv_bwd(proj, cf, dz, dw, lng, lnb, comm=None, tt=256):
    t = proj.shape[0]
    nt = t // tt
    te = tt + CONV_HALO

    def body(main_ref, next_ref, cf_ref, cfn_ref, dz_ref, dzn_ref, w_ref, g_ref, lb_ref,
             o_ref, ddw_ref, ddb_ref, dg_ref, dlb_ref, ds_ref, glu_ref, sb_ref):
        i = pl.program_id(0)

        @pl.when(i == 0)
        def _():
            ddw_ref[...] = jnp.zeros_like(ddw_ref)
            ddb_ref[...] = jnp.zeros_like(ddb_ref)
            dg_ref[...] = jnp.zeros_like(dg_ref)
            dlb_ref[...] = jnp.zeros_like(dlb_ref)

        lng = g_ref[...]
        chat, rstd, cn = _layer_norm_fwd(jnp.concatenate([cf_ref[...], cfn_ref[...]], axis=0), lng, lb_ref[...])
        gate = jnp.concatenate([main_ref[:, 2048:3072], next_ref[:, 2048:3072]], axis=0)
        dzv = jnp.concatenate([dz_ref[...], jnp.where(i < nt - 1, dzn_ref[...], 0.0)], axis=0)
        silu_cn, dsilu_cn = _silu_and_grad(cn)
        silu_gate, dsilu_gate = _silu_and_grad(gate)
        o_ref[:, 2048:3072] = (dzv * silu_cn * dsilu_gate)[0:tt].astype(BF16)
        dcn = dzv * silu_gate * dsilu_cn
        dg_ref[...] += jnp.sum((dcn * chat)[0:tt], axis=0, keepdims=True)
        dlb_ref[...] += jnp.sum(dcn[0:tt], axis=0, keepdims=True)
        dchat = dcn * lng
        dcf = rstd * (dchat - jnp.mean(dchat, axis=-1, keepdims=True) - chat * jnp.mean(dchat * chat, axis=-1, keepdims=True))
        ddb_ref[...] += jnp.sum(dcf[0:tt], axis=0, keepdims=True)
        ds_ref[0, 0:te, :] = dcf
        ds_ref[0, te:, :] = jnp.zeros((8, D_MODEL), F32)
        _fill_shifted(ds_ref, te + 8)
        sb_ref[...] = _sigmoid(main_ref[:, 1024:2048])
        glu_ref[...] = main_ref[:, 0:1024] * sb_ref[...]

        for c in range(D_MODEL // CONV_CC):
            cols = slice(c * CONV_CC, (c + 1) * CONV_CC)
            gcols = slice(c * CONV_CC + 1024, (c + 1) * CONV_CC + 1024)

            def chunk(j, carry):
                r = pl.multiple_of(j * CONV_RC, CONV_RC)
                dglu = _conv_taps(ds_ref, w_ref, r, cols, 0, True)
                sb = sb_ref[pl.ds(r, CONV_RC), cols]
                o_ref[pl.ds(r, CONV_RC), cols] = (dglu * sb).astype(BF16)
                o_ref[pl.ds(r, CONV_RC), gcols] = (dglu * glu_ref[pl.ds(r, CONV_RC), cols] * (1.0 - sb)).astype(BF16)
                return carry
            lax.fori_loop(0, tt // CONV_RC, chunk, 0, unroll=CONV_UNROLL)

            def taps(j, accs):
                r = pl.multiple_of(j * CONV_RC, CONV_RC)
                gl = glu_ref[pl.ds(r, CONV_RC), cols]
                new = list(accs)
                for m, blk in _tap_blocks(ds_ref, r, cols, 0):
                    prod = blk * gl
                    part = prod[0:8]
                    for q in range(1, CONV_RC // 8):
                        part = part + prod[8 * q:8 * q + 8]
                    new[m] = new[m] + part
                return tuple(new)
            accs = lax.fori_loop(0, tt // CONV_RC, taps, tuple(jnp.zeros((8, CONV_CC), F32) for _ in range(CONV_K)))
            for m in range(CONV_K):
                k = CONV_K - 1 - m
                ddw_ref[k:k + 1, cols] += jnp.sum(accs[m], axis=0, keepdims=True)

    vec = _const_spec((1, D_MODEL))
    vec_out = pl.BlockSpec((1, D_MODEL), lambda i: (0, 0))
    row = pl.BlockSpec((tt, D_MODEL), lambda i: (i, 0))
    nxt = lambda i: (jnp.minimum((i + 1) * (tt // CONV_HALO), t // CONV_HALO - 1), 0)
    nxt_row = pl.BlockSpec((CONV_HALO, D_MODEL), nxt)
    return _fused_call(
        body, comm, (proj, proj, cf, cf, dz, dz, dw, lng, lnb), name="conv_bwd", grid=(nt,),
        out_shape=[jax.ShapeDtypeStruct((t, 3 * D_MODEL), BF16), jax.ShapeDtypeStruct((CONV_K, D_MODEL), F32),
                   jax.ShapeDtypeStruct((1, D_MODEL), F32), jax.ShapeDtypeStruct((1, D_MODEL), F32),
                   jax.ShapeDtypeStruct((1, D_MODEL), F32)],
        in_specs=[pl.BlockSpec((tt, 3 * D_MODEL), lambda i: (i, 0)),
                  pl.BlockSpec((CONV_HALO, 3 * D_MODEL), nxt),
                  row, nxt_row, row, nxt_row,
                  _const_spec((CONV_K, D_MODEL)), vec, vec],
        out_specs=[pl.BlockSpec((tt, 3 * D_MODEL), lambda i: (i, 0)),
                   pl.BlockSpec((CONV_K, D_MODEL), lambda i: (0, 0)), vec_out, vec_out, vec_out],
        scratch_shapes=[pltpu.VMEM((8, te + 8, D_MODEL), F32), pltpu.VMEM((tt, D_MODEL), F32),
                        pltpu.VMEM((tt, D_MODEL), F32)],
        params=_params("arbitrary"))


def _piece_sum(parts, place, name):
    r = parts[0][0].shape[1]

    def body(p_ref, *refs):
        o_ref = refs[-1]
        acc = refs[0][0].astype(F32)
        for part in refs[1:-1]:
            acc = acc + part[0].astype(F32)
        o_ref[0] = acc

    blk = (1, r, D_MODEL)
    spec = lambda slot: pl.BlockSpec(blk, lambda j, p_ref: (slot(p_ref), 0, 0))
    return pl.pallas_call(
        body, name=name,
        grid_spec=pltpu.PrefetchScalarGridSpec(
            num_scalar_prefetch=1, grid=(1,),
            in_specs=[spec(slot) for _, slot in parts],
            out_specs=pl.BlockSpec(blk, lambda j, p_ref: (p_ref[1], 0, 0))),
        out_shape=jax.ShapeDtypeStruct((2, r, D_MODEL), F32),
        compiler_params=_params("arbitrary"),
    )(place, *[a for a, _ in parts])


def _direct_parts(own, recv):
    peer = lambda m: (lambda p: p[0] ^ m)
    return [(own, peer(0))] + [(recv, peer(m)) for m in range(1, N_DEV)]


def _pair_parts(sums, recv):
    const = lambda j: (lambda p: j)
    return [(sums, lambda p: p[2])] + [(recv, const(j)) for j in range(3)]


def _share_with_sibling(halves, name):
    n = len(halves)

    def body(*refs):
        outs = refs[n:2 * n]
        send_sems, recv_sems = refs[2 * n:]
        x, y, c, _ = _place()
        send = [pltpu.make_async_remote_copy(
            src_ref=outs[t].at[c], dst_ref=outs[t].at[c], send_sem=send_sems.at[t], recv_sem=recv_sems.at[t],
            device_id=(x, y, 1 - c), device_id_type=MESH_ID) for t in range(n)]
        recv = [pltpu.make_async_remote_copy(
            src_ref=outs[t].at[c], dst_ref=outs[t].at[1 - c], send_sem=send_sems.at[t], recv_sem=recv_sems.at[t],
            device_id=(x, y, 1 - c), device_id_type=MESH_ID) for t in range(n)]
        for cp in send:
            cp.start()
        for cp in recv:
            cp.wait_recv()
        for cp in send:
            cp.wait_send()

    return pl.pallas_call(
        body, name=name,
        out_shape=[jax.ShapeDtypeStruct(h.shape, h.dtype) for h in halves],
        in_specs=[ANY] * n, out_specs=[ANY] * n,
        input_output_aliases={t: t for t in range(n)},
        scratch_shapes=[pltpu.SemaphoreType.DMA((n,)), pltpu.SemaphoreType.DMA((n,))],
    )(*halves)


def _sum8(parts, name):
    r = parts.shape[1]

    def body(p_ref, o_ref):
        acc = p_ref[0]
        for k in range(1, N_DEV):
            acc = acc + p_ref[k]
        o_ref[...] = acc

    return pl.pallas_call(
        body, name=name, out_shape=jax.ShapeDtypeStruct((r, 128), F32),
        in_specs=[pl.BlockSpec(memory_space=pltpu.VMEM)], out_specs=pl.BlockSpec(memory_space=pltpu.VMEM),
    )(parts)


def _adamw(w, g, m, v, name):
    shape = w.shape
    cols = shape[-1]
    rows = w.size // cols
    rt = 256 if rows % 256 == 0 else rows

    def body(w_ref, g_ref, m_ref, v_ref, d_ref, nm_ref, nv_ref):
        gv = g_ref[...]
        mn = ADAM_B1 * m_ref[...] + (1.0 - ADAM_B1) * gv
        vn = ADAM_B2 * v_ref[...] + (1.0 - ADAM_B2) * (gv * gv)
        m_hat = mn / (1.0 - ADAM_B1 ** ADAM_STEP)
        v_hat = vn / (1.0 - ADAM_B2 ** ADAM_STEP)
        d_ref[...] = -ADAM_LR * (m_hat / (jnp.sqrt(v_hat) + ADAM_EPS) + ADAM_WD * w_ref[...])
        nm_ref[...] = mn
        nv_ref[...] = vn

    spec = pl.BlockSpec((rt, cols), lambda i: (i, 0))
    outs = pl.pallas_call(
        body, name=name, grid=(rows // rt,),
        out_shape=[jax.ShapeDtypeStruct((rows, cols), F32)] * 3,
        in_specs=[spec] * 4, out_specs=[spec] * 3,
        compiler_params=_params("parallel"),
    )(*[a.reshape(rows, cols) for a in (w, g, m, v)])
    return [o.reshape(shape) for o in outs]


SMALL_ROWS = 832


def _pack_small(g):
    parts = [g["loss"], g["pre1"].reshape(8, 128), g["post0"].reshape(8, 128),
             g["post1"].reshape(8, 128), g["sinks"], jnp.pad(g["pool_scale"].reshape(4, 128), ((0, 4), (0, 0))),
             g["pool_w"], g["dw"].reshape(248, 128), g["dwb"].reshape(8, 128), g["lng"].reshape(8, 128),
             g["lnb"].reshape(8, 128)]
    assert sum(p.shape[0] for p in parts) == SMALL_ROWS
    return jnp.concatenate(parts, axis=0)


def _unpack_small(s):
    out, r = {}, 0
    for key, rows, shape in (("loss", 8, (8, 128)), ("pre1", 8, (1, D_MODEL)), ("post", 16, (2, D_MODEL)),
                             ("sinks", 8, (8, 128)),
                             ("pool_scale", 4, (1, 512)), ("pad", 4, (4, 128)), ("pool_w", 512, (1, 4, 128, 128)),
                             ("dw", 248, (CONV_K, D_MODEL)), ("dwb", 8, (1, D_MODEL)), ("lng", 8, (1, D_MODEL)),
                             ("lnb", 8, (1, D_MODEL))):
        out[key] = s[r:r + rows].reshape(shape)
        r += rows
    return out


def kernel(x, pre_norm, post_norm, a_w_in, a_sinks, b_pool_w, b_pool_scale, ab_w_out, c_w_in, c_dw_w, c_dw_b, c_ln_g, c_ln_b, c_w_out, loss_target, m_pre_norm, m_post_norm, m_a_w_in, m_a_sinks, m_b_pool_w, m_b_pool_scale, m_ab_w_out, m_c_w_in, m_c_dw_w, m_c_dw_b, m_c_ln_g, m_c_ln_b, m_c_w_out, v_pre_norm, v_post_norm, v_a_w_in, v_a_sinks, v_b_pool_w, v_b_pool_scale, v_ab_w_out, v_c_w_in, v_c_dw_w, v_c_dw_b, v_c_ln_g, v_c_ln_b, v_c_w_out):
    ix, iy = lax.axis_index("x"), lax.axis_index("y")
    chip_cols = (2 * ix + iy) * 256

    pad8 = lambda v: jnp.pad(v, ((0, -v.shape[0] % 8), (0, 0)))
    vec_shard = jnp.concatenate([pad8(c_dw_w.reshape(CONV_K, 256)), pad8(c_dw_b), pad8(c_ln_g), pad8(c_ln_b),
                                 jnp.zeros((8, 256), F32)], axis=0)
    x0, target = x[0], loss_target[0]
    pre0, pre1 = pre_norm[0:1], pre_norm[1:2]
    post0, post1 = post_norm[0:1], post_norm[1:2]
    pool_w = b_pool_w[0]

    (wa_t,) = _run_comm(_Gather([a_w_in[0].T.astype(BF16)], halve=True), "gather_a_w_in")
    wa_t = wa_t.reshape(EVEN_IN, D_MODEL)
    proj0, (w_ab,) = _norm_matmul(x0, pre0, wa_t, "proj0_fwd", comm=_Gather([ab_w_out[0].astype(BF16)], halve=True))
    w_ab = w_ab.reshape(D_MODEL, D_MODEL)
    mix0, (wc_t, w_c, vecs) = _mix0_fwd(
        proj0, a_sinks, pool_w, b_pool_scale,
        comm=_Gather([c_w_in[0].T.astype(BF16), c_w_out[0].astype(BF16), vec_shard], halve=True))
    wc_t = wc_t.reshape(3 * D_MODEL, D_MODEL)
    w_c = w_c.reshape(D_MODEL, D_MODEL)
    vecs = vecs.reshape(4, 64, 256).transpose(1, 0, 2).reshape(64, D_MODEL)
    dw, dwb, lng, lnb = vecs[0:CONV_K], vecs[32:33], vecs[40:41], vecs[48:49]
    y0, x1 = _out_norm_res(mix0, w_ab, x0, post0, "out0_fwd")
    proj1, _ = _norm_matmul(x1, pre1, wc_t, "proj1_fwd")
    z1, cf1 = _conv_fwd(proj1, dw, dwb, lng, lnb)
    y1, g2, loss = _out_norm_res(z1, w_c, x1, post1, "out1_fwd", target=target)

    pieces = lambda m: m.reshape(N_DEV, m.shape[0] // N_DEV, D_MODEL)
    dz1, d_wc, d_post1 = _post_bwd(g2, y1, post1, w_c, z1, "out1_bwd")
    (dproj1, d_dw, d_dwb, d_lng, d_lnb), (r_wc,) = _conv_bwd(proj1, cf1, dz1, dw, lng, lnb,
                                                             comm=_Scatter([pieces(d_wc)]))
    g1, d_wct, d_pre1 = _pre_bwd(dproj1, wc_t, x1, pre1, g2, "proj1_bwd")
    dmix0, d_wab, d_post0 = _post_bwd(g1, y0, post0, w_ab, mix0, "out0_bwd")
    (dproj0, d_sinks, d_pw, d_ps), (r_wct, r_wab) = _mix0_bwd(proj0, dmix0, a_sinks, pool_w, b_pool_scale,
                                                             comm=_Scatter([pieces(d_wct), pieces(d_wab)]))
    g = dict(loss=loss, pre1=d_pre1, post0=d_post0, post1=d_post1, sinks=d_sinks, pool_w=d_pw, pool_scale=d_ps,
             dw=d_dw, dwb=d_dwb, lng=d_lng, lnb=d_lnb)
    d_wat, (small8,) = _proj_dw(dproj0, x0, pre0, "proj0_dw", comm=_Gather([_pack_small(g)], halve=False))
    (gx, d_pre0), (_, s_wat, r_wat) = _proj_dx(dproj0, wa_t, x0, pre0, g1, "proj0_dx",
                                               comm=_PairScatter([pieces(d_wat)]))
    (pre0_8,) = _run_comm(_Gather([d_pre0.reshape(8, 128)], halve=False), "pre0_all_gather")

    ic = lax.axis_index("c")
    place = jnp.stack([4 * ix + 2 * iy + ic, ic, 2 * ix + iy]).astype(jnp.int32)
    parts = [_pair_parts(s_wat, r_wat), _direct_parts(pieces(d_wab), r_wab), _direct_parts(pieces(d_wct), r_wct),
             _direct_parts(pieces(d_wc), r_wc)]
    halves = [_piece_sum(p, place, f"grad_sum{t}") for t, p in enumerate(parts)]
    g_wa_t, g_wab, g_wc_t, g_wc = [h.reshape(2 * h.shape[1], D_MODEL) for h in _share_with_sibling(halves, "grad_share")]
    g_a_w_in = g_wa_t.T[None]
    g_c_w_in = g_wc_t.T[None]
    g_ab_w_out = g_wab[None]
    g_c_w_out = g_wc[None]

    s = _unpack_small(_sum8(small8, "small_sum"))
    layer = lax.broadcasted_iota(jnp.int32, (2, D_MODEL), 0)
    g_pre = jnp.where(layer == 0, _sum8(pre0_8, "pre0_sum").reshape(1, D_MODEL), s["pre1"])
    g_post = s["post"]
    g_sinks = s["sinks"][:, 0].reshape(1, 8)
    g_pool_w, g_pool_scale = s["pool_w"], s["pool_scale"]
    g_dw = lax.dynamic_slice_in_dim(s["dw"], chip_cols, 256, axis=1).reshape(1, CONV_K, 1, 256)
    g_dwb = lax.dynamic_slice_in_dim(s["dwb"], chip_cols, 256, axis=1)
    g_lng = lax.dynamic_slice_in_dim(s["lng"], chip_cols, 256, axis=1)
    g_lnb = lax.dynamic_slice_in_dim(s["lnb"], chip_cols, 256, axis=1)

    grads = [g_pre, g_post, g_a_w_in, g_sinks, g_pool_w, g_pool_scale, g_ab_w_out, g_c_w_in, g_dw, g_dwb, g_lng, g_lnb,
             g_c_w_out]
    weights = [pre_norm, post_norm, a_w_in, a_sinks, b_pool_w, b_pool_scale, ab_w_out, c_w_in, c_dw_w, c_dw_b, c_ln_g,
               c_ln_b, c_w_out]
    moms = [m_pre_norm, m_post_norm, m_a_w_in, m_a_sinks, m_b_pool_w, m_b_pool_scale, m_ab_w_out, m_c_w_in, m_c_dw_w,
            m_c_dw_b, m_c_ln_g, m_c_ln_b, m_c_w_out]
    vars_ = [v_pre_norm, v_post_norm, v_a_w_in, v_a_sinks, v_b_pool_w, v_b_pool_scale, v_ab_w_out, v_c_w_in, v_c_dw_w,
             v_c_dw_b, v_c_ln_g, v_c_ln_b, v_c_w_out]
    deltas, new_m, new_v = [], [], []
    for k, (w, gr, m, v) in enumerate(zip(weights, grads, moms, vars_)):
        d, nm, nv = _adamw(w, gr, m, v, f"adamw{k}")
        deltas.append(d)
        new_m.append(nm)
        new_v.append(nv)
    return (s["loss"][0, 0], gx[None], *grads, *deltas, *new_m, *new_v)
```

```python
import functools

import jax
import jax.numpy as jnp
from jax import lax
from jax.experimental import pallas as pl
from jax.experimental.pallas import tpu as pltpu

F32 = jnp.float32
BF16 = jnp.bfloat16

D_MODEL = 1024
EPS = 1e-6
NEG = -1e30
HEAD_DIM = 64
GROUP = 4
KV_HEADS = 2
BLOCK = 128
EVEN_IN = 2304
COL_Q, COL_K, COL_GA, COL_U, COL_GB = 0, 512, 768, 1280, 1792
POOL_GROUPS = 4
POOL_GC = 128
POOL_HALO = 16
CONV_K = 31
CONV_HALO = 32
N_DEV = 8

ADAM_LR = 0.001
ADAM_B1 = 0.9
ADAM_B2 = 0.999
ADAM_EPS = 1e-08
ADAM_WD = 0.01
ADAM_STEP = 10

VMEM_LIMIT_BYTES = 56 * 1024 * 1024

NT = (((1,), (1,)), ((), ()))
TN = (((0,), (0,)), ((), ()))
MESH_ID = pl.DeviceIdType.MESH


def _params(*sem):
    return pltpu.CompilerParams(dimension_semantics=sem, vmem_limit_bytes=VMEM_LIMIT_BYTES)


def _const_spec(shape):
    nd = len(shape)
    return pl.BlockSpec(shape, lambda *_: (0,) * nd, pipeline_mode=pl.Buffered(1))


def _sigmoid(v):
    return 0.5 * jnp.tanh(0.5 * v) + 0.5


def _silu(v):
    return v * _sigmoid(v)


def _silu_and_grad(v):
    s = _sigmoid(v)
    return v * s, s * (1.0 + v * (1.0 - s))


ANY = pl.BlockSpec(memory_space=pl.ANY)


def _place():
    x, y, c = lax.axis_index("x"), lax.axis_index("y"), lax.axis_index("c")
    chips = [(1 - x, y), (x, 1 - y), (1 - x, 1 - y)]
    return x, y, c, chips


class _Gather:
    def __init__(self, blocks, halve):
        self.ins = list(blocks)
        self.halve = halve
        self.n = n = len(blocks)
        self.shapes = [((b.shape[0] // 2) if halve else b.shape[0], b.shape[1]) for b in blocks]
        self.out_shape = [jax.ShapeDtypeStruct((N_DEV, r, cols), b.dtype) for (r, cols), b in zip(self.shapes, blocks)]
        self.scratch = [pltpu.SemaphoreType.DMA((7 * n,)), pltpu.SemaphoreType.DMA((7 * n,)),
                        pltpu.SemaphoreType.DMA((n,))]

    def _copies(self, ins, outs, sems):
        send_sems, recv_sems, local_sems = sems
        x, y, c, chips = _place()
        me, sibling = (x, y, c), (x, y, 1 - c)

        def piece(t, px, py, pc):
            return outs[t].at[4 * px + 2 * py + pc]

        def own(t):
            return ins[t].at[pl.ds(c * self.shapes[t][0], self.shapes[t][0])] if self.halve else ins[t]

        def copy(t, k, block, to, src=None):
            return pltpu.make_async_remote_copy(
                src_ref=piece(t, *block) if src is None else src, dst_ref=piece(t, *block),
                send_sem=send_sems.at[7 * t + k], recv_sem=recv_sems.at[7 * t + k],
                device_id=to, device_id_type=MESH_ID)

        rng = range(self.n)
        return dict(
            mine=[pltpu.make_async_copy(own(t), piece(t, *me), local_sems.at[t]) for t in rng],
            first=[copy(t, 0, me, sibling, src=own(t)) for t in rng]
            + [copy(t, 1 + j, me, (*chip, c), src=own(t)) for t in rng for j, chip in enumerate(chips)],
            landed=[copy(t, 1 + j, (*chip, c), me) for j, chip in enumerate(chips) for t in rng],
            passed=[copy(t, 4 + j, (*chip, c), sibling) for j, chip in enumerate(chips) for t in rng],
            from_sibling=[copy(t, 0, sibling, me) for t in rng]
            + [copy(t, 4 + j, (*chip, 1 - c), me) for t in rng for j, chip in enumerate(chips)])

    def start(self, ins, outs, sems):
        d = self._copies(ins, outs, sems)
        for cp in d["mine"] + d["first"]:
            cp.start()

    def middle(self, ins, outs, sems):
        d = self._copies(ins, outs, sems)
        for got, fwd in zip(d["landed"], d["passed"]):
            got.wait_recv()
            fwd.start()

    def finish(self, ins, outs, sems):
        d = self._copies(ins, outs, sems)
        for cp in d["from_sibling"]:
            cp.wait_recv()
        for cp in d["first"] + d["passed"]:
            cp.wait_send()
        for cp in d["mine"]:
            cp.wait()


class _Scatter:
    def __init__(self, tensors):
        self.ins = list(tensors)
        self.n = n = len(tensors)
        self.out_shape = [jax.ShapeDtypeStruct(t.shape, t.dtype) for t in tensors]
        self.scratch = [pltpu.SemaphoreType.DMA((7 * n,)), pltpu.SemaphoreType.DMA((7 * n,))]

    def _copies(self, ins, outs, sems):
        send_sems, recv_sems = sems
        x, y, c, _ = _place()
        me = 4 * x + 2 * y + c
        sends, recvs = [], []
        for t in range(self.n):
            for m in range(1, N_DEV):
                px, py, pc = x ^ (m >> 2), y ^ ((m >> 1) & 1), c ^ (m & 1)
                q = 4 * px + 2 * py + pc
                sems_k = dict(send_sem=send_sems.at[7 * t + m - 1], recv_sem=recv_sems.at[7 * t + m - 1],
                              device_id=(px, py, pc), device_id_type=MESH_ID)
                sends.append(pltpu.make_async_remote_copy(src_ref=ins[t].at[q], dst_ref=outs[t].at[me], **sems_k))
                recvs.append(pltpu.make_async_remote_copy(src_ref=ins[t].at[me], dst_ref=outs[t].at[q], **sems_k))
        return sends, recvs

    def start(self, ins, outs, sems):
        for cp in self._copies(ins, outs, sems)[0]:
            cp.start()

    def middle(self, ins, outs, sems):
        pass

    def finish(self, ins, outs, sems):
        sends, recvs = self._copies(ins, outs, sems)
        for cp in recvs:
            cp.wait_recv()
        for cp in sends:
            cp.wait_send()


class _PairScatter:
    middle_step = 2

    def __init__(self, tensors):
        self.ins = list(tensors)
        self.n = n = len(tensors)
        r = tensors[0].shape[1]
        assert all(t.shape == (N_DEV, r, D_MODEL) and t.dtype == BF16 for t in tensors)
        quarter = lambda k: [jax.ShapeDtypeStruct((k, r, D_MODEL), BF16) for _ in tensors]
        self.out_shape = quarter(4) + quarter(4) + quarter(3)
        self.scratch = [pltpu.SemaphoreType.DMA((4 * n,)), pltpu.SemaphoreType.DMA((4 * n,)),
                        pltpu.SemaphoreType.DMA((3 * n,)), pltpu.SemaphoreType.DMA((3 * n,)),
                        pltpu.SemaphoreType.DMA((8,)), pltpu.VMEM((4, r, D_MODEL), BF16), pltpu.VMEM((4, r, D_MODEL), BF16)]

    def _copies(self, ins, outs, sems):
        n = self.n
        pair, sums, recv = outs[:n], outs[n:2 * n], outs[2 * n:]
        x, y, c, chips = _place()
        swap = [pltpu.make_async_remote_copy(
            src_ref=ins[t].at[2 * b + 1 - c], dst_ref=pair[t].at[b], send_sem=sems[0].at[4 * t + b],
            recv_sem=sems[1].at[4 * t + b], device_id=(x, y, 1 - c), device_id_type=MESH_ID)
            for t in range(n) for b in range(4)]
        to_owner = [pltpu.make_async_remote_copy(
            src_ref=sums[t].at[2 * cx + cy], dst_ref=recv[t].at[j], send_sem=sems[2].at[3 * t + j],
            recv_sem=sems[3].at[3 * t + j], device_id=(cx, cy, c), device_id_type=MESH_ID)
            for t in range(n) for j, (cx, cy) in enumerate(chips)]
        return swap, to_owner

    def start(self, ins, outs, sems):
        for cp in self._copies(ins, outs, sems)[0]:
            cp.start()

    def middle(self, ins, outs, sems):
        n = self.n
        pair, sums = outs[:n], outs[n:2 * n]
        local, mine_ref, theirs_ref = sems[4], sems[5], sems[6]
        c = lax.axis_index("c")
        swap, to_owner = self._copies(ins, outs, sems)
        for cp in swap:
            cp.wait_recv()
        for t in range(n):
            loads = [pltpu.make_async_copy(ins[t].at[2 * b + c], mine_ref.at[b], local.at[b]) for b in range(4)]
            loads += [pltpu.make_async_copy(pair[t].at[b], theirs_ref.at[b], local.at[4 + b]) for b in range(4)]
            for cp in loads:
                cp.start()
            for cp in loads:
                cp.wait()
            mine_ref[...] = (mine_ref[...].astype(F32) + theirs_ref[...].astype(F32)).astype(BF16)
            stores = [pltpu.make_async_copy(mine_ref.at[b], sums[t].at[b], local.at[b]) for b in range(4)]
            for cp in stores:
                cp.start()
            for cp in stores:
                cp.wait()
        for cp in to_owner:
            cp.start()
        for cp in swap:
            cp.wait_send()

    def finish(self, ins, outs, sems):
        to_owner = self._copies(ins, outs, sems)[1]
        for cp in to_owner:
            cp.wait_recv()
        for cp in to_owner:
            cp.wait_send()


class _Comms:
    def __init__(self, *comms):
        self.comms = comms
        self.ins = [a for c in comms for a in c.ins]
        self.out_shape = [s for c in comms for s in c.out_shape]
        self.scratch = [s for c in comms for s in c.scratch]

    def _each(self, phase, ins, outs, sems):
        i = o = s = 0
        for c in self.comms:
            ni, no, ns = len(c.ins), len(c.out_shape), len(c.scratch)
            getattr(c, phase)(ins[i:i + ni], outs[o:o + no], sems[s:s + ns])
            i, o, s = i + ni, o + no, s + ns

    def start(self, ins, outs, sems):
        self._each("start", ins, outs, sems)

    def middle(self, ins, outs, sems):
        self._each("middle", ins, outs, sems)

    def finish(self, ins, outs, sems):
        self._each("finish", ins, outs, sems)


def _run_comm(comm, name):
    n = len(comm.ins)

    def body(*refs):
        parts = refs[:n], refs[n:2 * n], refs[2 * n:]
        comm.start(*parts)
        comm.middle(*parts)
        comm.finish(*parts)

    return pl.pallas_call(body, name=name, out_shape=comm.out_shape, in_specs=[ANY] * n, out_specs=[ANY] * n,
                          scratch_shapes=comm.scratch)(*comm.ins)


def _fused_call(body, comm, args, *, name, grid, out_shape, in_specs, out_specs, scratch_shapes=(), params):
    single = not isinstance(out_shape, (list, tuple))
    out_shape = [out_shape] if single else list(out_shape)
    out_specs = [out_specs] if single else list(out_specs)
    if comm is None:
        res = pl.pallas_call(body, name=name, grid=grid, out_shape=out_shape, in_specs=in_specs, out_specs=out_specs,
                             scratch_shapes=list(scratch_shapes), compiler_params=params)(*args)
        return (res[0] if single else res), []
    n_in, n_out, n_scr = len(in_specs), len(out_shape), len(scratch_shapes)
    c_in, c_out = len(comm.ins), len(comm.out_shape)
    steps = grid[0]

    def fused(*refs):
        pos = 0
        groups = []
        for size in (n_in, c_in, n_out, c_out, n_scr, len(comm.scratch)):
            groups.append(refs[pos:pos + size])
            pos += size
        ins, c_ins, outs, c_outs, scr, c_sems = groups
        i = pl.program_id(0)

        @pl.when(i == 0)
        def _():
            comm.start(c_ins, c_outs, c_sems)

        @pl.when(i == min(getattr(comm, "middle_step", steps // 2), steps - 1))
        def _():
            comm.middle(c_ins, c_outs, c_sems)

        body(*ins, *outs, *scr)

        @pl.when(i == steps - 1)
        def _():
            comm.finish(c_ins, c_outs, c_sems)

    res = pl.pallas_call(
        fused, name=name, grid=grid, out_shape=out_shape + list(comm.out_shape),
        in_specs=list(in_specs) + [ANY] * c_in, out_specs=out_specs + [ANY] * c_out,
        scratch_shapes=list(scratch_shapes) + list(comm.scratch), compiler_params=params)(*args, *comm.ins)
    main = res[:n_out]
    return (main[0] if single else main), list(res[n_out:])


def _norm_matmul(x, gain, wt, name, comm=None, tm=512):
    t, n = x.shape[0], wt.shape[0]

    def body(x_ref, g_ref, wt_ref, o_ref):
        xv = x_ref[...]
        r = lax.rsqrt(jnp.mean(xv * xv, axis=-1, keepdims=True) + EPS)
        h = (xv * r * g_ref[...]).astype(BF16)
        o_ref[...] = lax.dot_general(h, wt_ref[...], NT, preferred_element_type=F32)

    return _fused_call(
        body, comm, (x, gain, wt), name=name, grid=(t // tm,),
        out_shape=jax.ShapeDtypeStruct((t, n), F32),
        in_specs=[pl.BlockSpec((tm, D_MODEL), lambda i: (i, 0)), _const_spec((1, D_MODEL)), _const_spec((n, D_MODEL))],
        out_specs=pl.BlockSpec((tm, n), lambda i: (i, 0)),
        params=_params("arbitrary"))


def _out_norm_res(a, w, x_in, post, name, target=None, tm=512):
    t = a.shape[0]
    with_loss = target is not None

    def body(*refs):
        if with_loss:
            a_ref, w_ref, x_ref, p_ref, t_ref, y_ref, o_ref, l_ref = refs
        else:
            a_ref, w_ref, x_ref, p_ref, y_ref, o_ref = refs
        y = jnp.dot(a_ref[...], w_ref[...], preferred_element_type=F32)
        y_ref[...] = y
        ry = lax.rsqrt(jnp.mean(y * y, axis=-1, keepdims=True) + EPS)
        xo = x_ref[...] + (y * ry) * p_ref[...]
        if with_loss:
            d = xo - t_ref[...]
            o_ref[...] = d * (1.0 / D_MODEL)

            @pl.when(pl.program_id(0) == 0)
            def _():
                l_ref[...] = jnp.zeros_like(l_ref)

            l_ref[...] += 0.5 * jnp.sum(jnp.mean(d * d, axis=-1, keepdims=True))
        else:
            o_ref[...] = xo

    row = pl.BlockSpec((tm, D_MODEL), lambda i: (i, 0))
    in_specs = [row, _const_spec((D_MODEL, D_MODEL)), row, _const_spec((1, D_MODEL))]
    out_shape = [jax.ShapeDtypeStruct((t, D_MODEL), F32), jax.ShapeDtypeStruct((t, D_MODEL), F32)]
    out_specs = [row, row]
    args = [a, w, x_in, post]
    if with_loss:
        in_specs.append(row)
        args.append(target)
        out_shape.append(jax.ShapeDtypeStruct((8, 128), F32))
        out_specs.append(pl.BlockSpec((8, 128), lambda i: (0, 0)))
    return pl.pallas_call(
        body, name=name, grid=(t // tm,), out_shape=out_shape, in_specs=in_specs, out_specs=out_specs,
        compiler_params=_params("arbitrary"),
    )(*args)


def _post_bwd(g, y, post, w, a, name, tm=1024):
    t = g.shape[0]
    steps = t // tm

    def body(g_ref, y_ref, p_ref, w_ref, a_ref, da_ref, dw16_ref, dp_ref, dw_ref):
        @pl.when(pl.program_id(0) == 0)
        def _():
            dw_ref[...] = jnp.zeros_like(dw_ref)
            dp_ref[...] = jnp.zeros_like(dp_ref)

        gv = g_ref[...]
        yv = y_ref[...]
        ry = lax.rsqrt(jnp.mean(yv * yv, axis=-1, keepdims=True) + EPS)
        nv = yv * ry
        dp_ref[...] += jnp.sum(gv * nv, axis=0, keepdims=True)
        dn = gv * p_ref[...]
        dy = (ry * (dn - nv * jnp.mean(dn * nv, axis=-1, keepdims=True))).astype(BF16)
        da_ref[...] = lax.dot_general(dy, w_ref[...], NT, preferred_element_type=F32)
        dw_ref[...] += lax.dot_general(a_ref[...], dy, TN, preferred_element_type=F32)

        @pl.when(pl.program_id(0) == steps - 1)
        def _():
            dw16_ref[...] = dw_ref[...].astype(BF16)

    row = pl.BlockSpec((tm, D_MODEL), lambda i: (i, 0))
    return pl.pallas_call(
        body, name=name, grid=(steps,),
        out_shape=[jax.ShapeDtypeStruct((t, D_MODEL), F32), jax.ShapeDtypeStruct((D_MODEL, D_MODEL), BF16),
                   jax.ShapeDtypeStruct((1, D_MODEL), F32)],
        in_specs=[row, row, _const_spec((1, D_MODEL)), _const_spec((D_MODEL, D_MODEL)), row],
        out_specs=[row, pl.BlockSpec((D_MODEL, D_MODEL), lambda i: (0, 0)), pl.BlockSpec((1, D_MODEL), lambda i: (0, 0))],
        scratch_shapes=[pltpu.VMEM((D_MODEL, D_MODEL), F32)],
        compiler_params=_params("arbitrary"),
    )(g, y, post, w, a)


def _pre_bwd(dproj, wt, x_in, pre, g, name, tm=512):
    t, n = dproj.shape
    steps = t // tm

    def body(dp_ref, wt_ref, x_ref, pre_ref, g_ref, dx_ref, dwt16_ref, dpre_ref, dwt_ref):
        @pl.when(pl.program_id(0) == 0)
        def _():
            dwt_ref[...] = jnp.zeros_like(dwt_ref)
            dpre_ref[...] = jnp.zeros_like(dpre_ref)

        dpv = dp_ref[...]
        dh = jnp.dot(dpv, wt_ref[...], preferred_element_type=F32)
        xv = x_ref[...]
        r = lax.rsqrt(jnp.mean(xv * xv, axis=-1, keepdims=True) + EPS)
        xn = xv * r
        pv = pre_ref[...]
        dpre_ref[...] += jnp.sum(dh * xn, axis=0, keepdims=True)
        dxn = dh * pv
        dx_ref[...] = g_ref[...] + r * (dxn - xn * jnp.mean(dxn * xn, axis=-1, keepdims=True))
        h = (xn * pv).astype(BF16)
        dwt_ref[...] += lax.dot_general(dpv, h, TN, preferred_element_type=F32)

        @pl.when(pl.program_id(0) == steps - 1)
        def _():
            dwt16_ref[...] = dwt_ref[...].astype(BF16)

    row = pl.BlockSpec((tm, D_MODEL), lambda i: (i, 0))
    return pl.pallas_call(
        body, name=name, grid=(steps,),
        out_shape=[jax.ShapeDtypeStruct((t, D_MODEL), F32), jax.ShapeDtypeStruct((n, D_MODEL), BF16),
                   jax.ShapeDtypeStruct((1, D_MODEL), F32)],
        in_specs=[pl.BlockSpec((tm, n), lambda i: (i, 0)), _const_spec((n, D_MODEL)), row, _const_spec((1, D_MODEL)), row],
        out_specs=[row, _const_spec((n, D_MODEL)), pl.BlockSpec((1, D_MODEL), lambda i: (0, 0))],
        scratch_shapes=[pltpu.VMEM((n, D_MODEL), F32)],
        compiler_params=_params("arbitrary"),
    )(dproj, wt, x_in, pre, g)


def _proj_dw(dproj, x_in, pre, name, comm=None, tm=1024):
    t, n = dproj.shape
    steps = t // tm

    def body(dp_ref, x_ref, pre_ref, dwt16_ref, dwt_ref):
        @pl.when(pl.program_id(0) == 0)
        def _():
            dwt_ref[...] = jnp.zeros_like(dwt_ref)

        xv = x_ref[...]
        r = lax.rsqrt(jnp.mean(xv * xv, axis=-1, keepdims=True) + EPS)
        h = (xv * r * pre_ref[...]).astype(BF16)
        dwt_ref[...] += lax.dot_general(dp_ref[...], h, TN, preferred_element_type=F32)

        @pl.when(pl.program_id(0) == steps - 1)
        def _():
            dwt16_ref[...] = dwt_ref[...].astype(BF16)

    return _fused_call(
        body, comm, (dproj, x_in, pre), name=name, grid=(steps,),
        out_shape=jax.ShapeDtypeStruct((n, D_MODEL), BF16),
        in_specs=[pl.BlockSpec((tm, n), lambda i: (i, 0)), pl.BlockSpec((tm, D_MODEL), lambda i: (i, 0)),
                  _const_spec((1, D_MODEL))],
        out_specs=pl.BlockSpec((n, D_MODEL), lambda i: (0, 0)),
        scratch_shapes=[pltpu.VMEM((n, D_MODEL), F32)],
        params=_params("arbitrary"))


def _proj_dx(dproj, wt, x_in, pre, g, name, comm=None, tm=512):
    t, n = dproj.shape

    def body(dp_ref, wt_ref, x_ref, pre_ref, g_ref, dx_ref, dpre_ref):
        @pl.when(pl.program_id(0) == 0)
        def _():
            dpre_ref[...] = jnp.zeros_like(dpre_ref)

        dh = jnp.dot(dp_ref[...], wt_ref[...], preferred_element_type=F32)
        xv = x_ref[...]
        r = lax.rsqrt(jnp.mean(xv * xv, axis=-1, keepdims=True) + EPS)
        xn = xv * r
        dpre_ref[...] += jnp.sum(dh * xn, axis=0, keepdims=True)
        dxn = dh * pre_ref[...]
        dx_ref[...] = g_ref[...] + r * (dxn - xn * jnp.mean(dxn * xn, axis=-1, keepdims=True))

    row = pl.BlockSpec((tm, D_MODEL), lambda i: (i, 0))
    return _fused_call(
        body, comm, (dproj, wt, x_in, pre, g), name=name, grid=(t // tm,),
        out_shape=[jax.ShapeDtypeStruct((t, D_MODEL), F32), jax.ShapeDtypeStruct((1, D_MODEL), F32)],
        in_specs=[pl.BlockSpec((tm, n), lambda i: (i, 0)), _const_spec((n, D_MODEL)), row, _const_spec((1, D_MODEL)), row],
        out_specs=[row, pl.BlockSpec((1, D_MODEL), lambda i: (0, 0))],
        params=_params("arbitrary"))


def _group_masks():
    lane = lax.broadcasted_iota(jnp.int32, (1, GROUP * HEAD_DIM), 1)
    return [(lane // HEAD_DIM == g).astype(F32) for g in range(GROUP)]


def _stack_groups(v, masks):
    return jnp.concatenate([v * m for m in masks], axis=0)


def _unstack_groups(v, masks):
    out = v[0:BLOCK] * masks[0]
    for g in range(1, GROUP):
        out = out + v[g * BLOCK:(g + 1) * BLOCK] * masks[g]
    return out


def _repeat_head(kv2, kvh):
    first = lax.broadcasted_iota(jnp.int32, kv2.shape, 1) < HEAD_DIM
    rolled = pltpu.roll(kv2, HEAD_DIM, 1)
    one = jnp.where(first, kv2, rolled) if kvh == 0 else jnp.where(first, rolled, kv2)
    return jnp.concatenate([one, one], axis=1)


def _fold_head(v4):
    a = v4[:, 0:128] + v4[:, 128:256]
    return a + pltpu.roll(a, HEAD_DIM, 1)


ATTN_CONSTS = [pltpu.VMEM((KV_HEADS, GROUP * BLOCK, 2 * BLOCK), F32)]


def _fill_attn_bias(bias_ref):
    row = lax.broadcasted_iota(jnp.int32, (GROUP * BLOCK, 2 * BLOCK), 0)
    col = lax.broadcasted_iota(jnp.int32, (GROUP * BLOCK, 2 * BLOCK), 1)
    dist = (row % BLOCK) + BLOCK - col
    band = (dist >= 0) & (dist < BLOCK)
    rb = lax.broadcasted_iota(jnp.int32, (GROUP * BLOCK, 1), 0) // BLOCK
    for kvh in range(KV_HEADS):
        slope = jnp.zeros((GROUP * BLOCK, 1), F32)
        for g in range(GROUP):
            slope = jnp.where(rb == g, 2.0 ** (-(kvh * GROUP + g + 1)), slope)
        bias_ref[kvh] = jnp.where(band, -slope * dist.astype(F32), NEG)


def _row_sinks(kvh, sink_ref):
    rb = lax.broadcasted_iota(jnp.int32, (GROUP * BLOCK, 1), 0) // BLOCK
    sink = jnp.zeros((GROUP * BLOCK, 1), F32)
    for g in range(GROUP):
        sink = jnp.where(rb == g, sink_ref[0, kvh * GROUP + g], sink)
    return sink


def _attn_probs(qk, k4, bias, sink, no_past, masks):
    qs = _stack_groups(qk, masks).astype(BF16)
    s = lax.dot_general(qs, k4, NT, preferred_element_type=F32) * (HEAD_DIM ** -0.5) + bias
    s = jnp.concatenate([jnp.where(no_past, NEG, s[:, 0:BLOCK]), s[:, BLOCK:]], axis=1)
    mx = jnp.maximum(jnp.max(s, axis=-1, keepdims=True), sink)
    e = jnp.exp(s - mx)
    es = jnp.exp(sink - mx)
    inv = 1.0 / (jnp.sum(e, axis=-1, keepdims=True) + es)
    return qs, e * inv, es * inv


def _pool_forward(u_ext, g, t0):
    n = u_ext.shape[0] - POOL_HALO
    s = u_ext
    for step in range(g + 1):
        s = s + pltpu.roll(s, 1 << step, 0)
    w = 2 << g
    t = t0 + lax.broadcasted_iota(jnp.int32, (n, 1), 0)
    cnt = jnp.minimum(t + 1, w).astype(F32)
    return s[POOL_HALO:] / cnt - u_ext[POOL_HALO:]


def _mix0_fwd(proj, sinks, pool_w, pool_scale, comm=None, tq=512):
    t = proj.shape[0]
    nblk = tq // BLOCK

    def body(main_ref, halo_ref, sink_ref, pw_ref, ps_ref, o_ref, kv_ref, bias_ref):
        i = pl.program_id(0)
        t0 = i * tq
        masks = _group_masks()

        @pl.when(i == 0)
        def _():
            _fill_attn_bias(bias_ref)

        kv_ref[0:BLOCK, :] = halo_ref[:, COL_K:COL_K + 256]
        kv_ref[BLOCK:, :] = main_ref[:, COL_K:COL_K + 256]

        def block(jb, carry):
            r0 = pl.multiple_of(jb * BLOCK, BLOCK)
            no_past = t0 + r0 == 0
            q = main_ref[pl.ds(r0, BLOCK), COL_Q:COL_Q + 512]
            ga = main_ref[pl.ds(r0, BLOCK), COL_GA:COL_GA + 512]
            kk = kv_ref[pl.ds(r0, 2 * BLOCK), 0:128]
            vv = kv_ref[pl.ds(r0, 2 * BLOCK), 128:256]
            outs = []
            for kvh in range(KV_HEADS):
                k4 = _repeat_head(kk, kvh).astype(BF16)
                v4 = _repeat_head(vv, kvh).astype(BF16)
                _, p, _ = _attn_probs(q[:, kvh * 256:(kvh + 1) * 256], k4, bias_ref[kvh], _row_sinks(kvh, sink_ref), no_past, masks)
                pv = jnp.dot(p.astype(BF16), v4, preferred_element_type=F32)
                outs.append(_unstack_groups(pv, masks))
            attn = jnp.concatenate(outs, axis=1)
            o_ref[pl.ds(r0, BLOCK), 0:512] = (attn * _silu(ga)).astype(BF16)
            return carry

        lax.fori_loop(0, nblk, block, 0, unroll=2)

        for g in range(POOL_GROUPS):
            cu = COL_U + g * POOL_GC
            cg = COL_GB + g * POOL_GC
            halo_u = jnp.where(i == 0, 0.0, halo_ref[BLOCK - POOL_HALO:BLOCK, cu:cu + POOL_GC])
            u_ext = jnp.concatenate([halo_u, main_ref[:, cu:cu + POOL_GC]], axis=0)
            pooled = _pool_forward(u_ext, g, t0)
            y = jnp.dot(pooled.astype(BF16), pw_ref[g].astype(BF16), preferred_element_type=F32)
            y = y * ps_ref[:, g * POOL_GC:(g + 1) * POOL_GC]
            o_ref[:, 512 + g * POOL_GC:512 + (g + 1) * POOL_GC] = (y * _silu(main_ref[:, cg:cg + POOL_GC])).astype(BF16)

    return _fused_call(
        body, comm, (proj, proj, sinks, pool_w, pool_scale), name="mix0_fwd", grid=(t // tq,),
        out_shape=jax.ShapeDtypeStruct((t, D_MODEL), BF16),
        in_specs=[pl.BlockSpec((tq, EVEN_IN), lambda i: (i, 0)),
                  pl.BlockSpec((BLOCK, EVEN_IN), lambda i: (jnp.maximum(i * nblk - 1, 0), 0)),
                  pl.BlockSpec(memory_space=pltpu.SMEM),
                  _const_spec((POOL_GROUPS, POOL_GC, POOL_GC)), _const_spec((1, 512))],
        out_specs=pl.BlockSpec((tq, D_MODEL), lambda i: (i, 0)),
        scratch_shapes=[pltpu.VMEM((tq + BLOCK, 256), F32)] + ATTN_CONSTS,
        params=_params("arbitrary"))


def _mix0_bwd(proj, dmix, sinks, pool_w, pool_scale, comm=None, tq=512):
    t = proj.shape[0]
    nt = t // tq
    nblk = tq // BLOCK

    def body(main_ref, halo_ref, next_ref, dm_ref, dmn_ref, sink_ref, pw_ref, ps_ref,
             o_ref, dsk_ref, dpw_ref, dps_ref, kv_ref, dkv_ref, carry_ref, bias_ref):
        i = pl.program_id(0)
        ii = nt - 1 - i
        t0 = ii * tq
        masks = _group_masks()

        @pl.when(i == 0)
        def _():
            _fill_attn_bias(bias_ref)
            dsk_ref[...] = jnp.zeros_like(dsk_ref)
            dpw_ref[...] = jnp.zeros_like(dpw_ref)
            dps_ref[...] = jnp.zeros_like(dps_ref)
            carry_ref[...] = jnp.zeros_like(carry_ref)

        kv_ref[0:BLOCK, :] = halo_ref[:, COL_K:COL_K + 256]
        kv_ref[BLOCK:, :] = main_ref[:, COL_K:COL_K + 256]
        dkv_ref[0:tq, :] = jnp.zeros((tq, 256), F32)
        dkv_ref[tq:, :] = carry_ref[...]

        def block(jb, carry):
            r0 = pl.multiple_of(jb * BLOCK, BLOCK)
            no_past = t0 + r0 == 0
            q = main_ref[pl.ds(r0, BLOCK), COL_Q:COL_Q + 512]
            ga = main_ref[pl.ds(r0, BLOCK), COL_GA:COL_GA + 512]
            dya = dm_ref[pl.ds(r0, BLOCK), 0:512]
            kk = kv_ref[pl.ds(r0, 2 * BLOCK), 0:128]
            vv = kv_ref[pl.ds(r0, 2 * BLOCK), 128:256]
            silu_ga, dsilu_ga = _silu_and_grad(ga)
            do = dya * silu_ga
            first = lax.broadcasted_iota(jnp.int32, (2 * BLOCK, 128), 1) < HEAD_DIM
            attn, dq, dk, dv = [], [], [], []
            for kvh in range(KV_HEADS):
                k4 = _repeat_head(kk, kvh).astype(BF16)
                v4 = _repeat_head(vv, kvh).astype(BF16)
                qs, p, ps = _attn_probs(q[:, kvh * 256:(kvh + 1) * 256], k4, bias_ref[kvh], _row_sinks(kvh, sink_ref), no_past, masks)
                pb = p.astype(BF16)
                o_k = _unstack_groups(jnp.dot(pb, v4, preferred_element_type=F32), masks)
                do_k = do[:, kvh * 256:(kvh + 1) * 256]
                dos = _stack_groups(do_k, masks).astype(BF16)
                prod = do_k * o_k
                delta = jnp.concatenate([jnp.sum(prod * m, axis=-1, keepdims=True) for m in masks], axis=0)
                dp = lax.dot_general(dos, v4, NT, preferred_element_type=F32)
                ds = (p * (dp - delta)).astype(BF16)
                sink_term = ps * delta
                for g in range(GROUP):
                    h = kvh * GROUP + g
                    dsk_ref[h:h + 1, :] -= jnp.sum(sink_term[g * BLOCK:(g + 1) * BLOCK], keepdims=True)
                scale = HEAD_DIM ** -0.5
                dq.append(_unstack_groups(jnp.dot(ds, k4, preferred_element_type=F32), masks) * scale)
                dk.append(_fold_head(lax.dot_general(ds, qs, TN, preferred_element_type=F32)) * scale)
                dv.append(_fold_head(lax.dot_general(pb, dos, TN, preferred_element_type=F32)))
                attn.append(o_k)
            o_ref[pl.ds(r0, BLOCK), COL_Q:COL_Q + 512] = jnp.concatenate(dq, axis=1).astype(BF16)
            o_all = jnp.concatenate(attn, axis=1)
            o_ref[pl.ds(r0, BLOCK), COL_GA:COL_GA + 512] = (dya * o_all * dsilu_ga).astype(BF16)
            dkv = jnp.concatenate([jnp.where(first, dk[0], dk[1]), jnp.where(first, dv[0], dv[1])], axis=1)
            dkv_ref[pl.ds(r0, 2 * BLOCK), :] += dkv
            return carry

        lax.fori_loop(0, nblk, block, 0, unroll=2)
        carry_ref[...] = dkv_ref[0:BLOCK, :]
        o_ref[:, COL_K:COL_K + 256] = dkv_ref[BLOCK:, :].astype(BF16)

        last = ii == nt - 1
        for g in range(POOL_GROUPS):
            cu = COL_U + g * POOL_GC
            cg = COL_GB + g * POOL_GC
            cm = 512 + g * POOL_GC
            pw = pw_ref[g].astype(BF16)
            sc = ps_ref[:, g * POOL_GC:(g + 1) * POOL_GC]
            halo_u = jnp.where(ii == 0, 0.0, halo_ref[BLOCK - POOL_HALO:BLOCK, cu:cu + POOL_GC])
            u_ext = jnp.concatenate([halo_u, main_ref[:, cu:cu + POOL_GC]], axis=0)
            pooled = _pool_forward(u_ext, g, t0).astype(BF16)
            y_raw = jnp.dot(pooled, pw, preferred_element_type=F32)
            gb = main_ref[:, cg:cg + POOL_GC]
            dyb = dm_ref[:, cm:cm + POOL_GC]
            silu_gb, dsilu_gb = _silu_and_grad(gb)
            dypool = dyb * silu_gb
            dps_ref[:, g * POOL_GC:(g + 1) * POOL_GC] += jnp.sum(dypool * y_raw, axis=0, keepdims=True)
            o_ref[:, cg:cg + POOL_GC] = (dyb * (y_raw * sc) * dsilu_gb).astype(BF16)
            dyraw = dypool * sc
            dyraw_n = jnp.where(last, 0.0, dmn_ref[:, cm:cm + POOL_GC] * _silu(next_ref[:, cg:cg + POOL_GC]) * sc)
            dpw_ref[g * POOL_GC:(g + 1) * POOL_GC, :] += lax.dot_general(pooled, dyraw.astype(BF16), TN,
                                                                         preferred_element_type=F32)
            dyraw_ext = jnp.concatenate([dyraw, dyraw_n], axis=0).astype(BF16)
            dpooled = lax.dot_general(dyraw_ext, pw, NT, preferred_element_type=F32)
            w = 2 << g
            tt = t0 + lax.broadcasted_iota(jnp.int32, (tq + POOL_HALO, 1), 0)
            s = dpooled / jnp.minimum(tt + 1, w).astype(F32)
            for step in range(g + 1):
                s = s + pltpu.roll(s, tq + POOL_HALO - (1 << step), 0)
            o_ref[:, cu:cu + POOL_GC] = (s[0:tq] - dpooled[0:tq]).astype(BF16)

    rev = lambda i: nt - 1 - i
    return _fused_call(
        body, comm, (proj, proj, proj, dmix, dmix, sinks, pool_w, pool_scale), name="mix0_bwd", grid=(nt,),
        out_shape=[jax.ShapeDtypeStruct((t, EVEN_IN), BF16), jax.ShapeDtypeStruct((8, 128), F32),
                   jax.ShapeDtypeStruct((POOL_GROUPS * POOL_GC, POOL_GC), F32), jax.ShapeDtypeStruct((1, 512), F32)],
        in_specs=[pl.BlockSpec((tq, EVEN_IN), lambda i: (rev(i), 0)),
                  pl.BlockSpec((BLOCK, EVEN_IN), lambda i: (jnp.maximum(rev(i) * nblk - 1, 0), 0)),
                  pl.BlockSpec((POOL_HALO, EVEN_IN),
                               lambda i: (jnp.minimum((rev(i) + 1) * (tq // POOL_HALO), t // POOL_HALO - 1), 0)),
                  pl.BlockSpec((tq, D_MODEL), lambda i: (rev(i), 0)),
                  pl.BlockSpec((POOL_HALO, D_MODEL),
                               lambda i: (jnp.minimum((rev(i) + 1) * (tq // POOL_HALO), t // POOL_HALO - 1), 0)),
                  pl.BlockSpec(memory_space=pltpu.SMEM),
                  _const_spec((POOL_GROUPS, POOL_GC, POOL_GC)), _const_spec((1, 512))],
        out_specs=[pl.BlockSpec((tq, EVEN_IN), lambda i: (rev(i), 0)),
                   pl.BlockSpec((8, 128), lambda i: (0, 0)),
                   pl.BlockSpec((POOL_GROUPS * POOL_GC, POOL_GC), lambda i: (0, 0)),
                   pl.BlockSpec((1, 512), lambda i: (0, 0))],
        scratch_shapes=[pltpu.VMEM((tq + BLOCK, 256), F32), pltpu.VMEM((tq + BLOCK, 256), F32),
                        pltpu.VMEM((BLOCK, 256), F32)] + ATTN_CONSTS,
        params=_params("arbitrary"))


CONV_RC = 32
CONV_CC = 128
CONV_CHAINS = 4
CONV_UNROLL = 2


def _fill_shifted(s_ref, rows):
    for b in range(1, 8):
        s_ref[b, 0:rows - 8, :] = s_ref[0, b:b + rows - 8, :]


def _tap_blocks(s_ref, r, cols, lead):
    for b in range(8):
        taps = [(a, 8 * a + b - lead) for a in range(5) if 0 <= 8 * a + b - lead < CONV_K]
        span = 8 * max(a for a, _ in taps) + CONV_RC
        blk = s_ref[b, pl.ds(r, span), cols]
        for a, k in taps:
            yield k, blk[8 * a:8 * a + CONV_RC]


def _conv_taps(s_ref, w_ref, r, cols, lead, reverse):
    accs = [None] * CONV_CHAINS
    for n, (k, blk) in enumerate(_tap_blocks(s_ref, r, cols, lead)):
        kw = CONV_K - 1 - k if reverse else k
        term = blk * w_ref[kw:kw + 1, cols]
        accs[n % CONV_CHAINS] = term if accs[n % CONV_CHAINS] is None else accs[n % CONV_CHAINS] + term
    return (accs[0] + accs[1]) + (accs[2] + accs[3])


def _layer_norm_fwd(cf, lng, lnb):
    mu = jnp.mean(cf, axis=-1, keepdims=True)
    xc = cf - mu
    rstd = lax.rsqrt(jnp.mean(xc * xc, axis=-1, keepdims=True) + EPS)
    chat = xc * rstd
    return chat, rstd, chat * lng + lnb


def _conv_fwd(proj, dw, dwb, lng, lnb, tt=256):
    t = proj.shape[0]
    lead = CONV_HALO - (CONV_K - 1)

    def body(main_ref, halo_ref, w_ref, b_ref, g_ref, lb_ref, o_ref, c_ref, gs_ref):
        i = pl.program_id(0)
        hv = halo_ref[...]
        gs_ref[0, 0:CONV_HALO, :] = jnp.where(i == 0, 0.0, hv[:, 0:1024] * _sigmoid(hv[:, 1024:2048]))
        gs_ref[0, CONV_HALO:CONV_HALO + tt, :] = main_ref[:, 0:1024] * _sigmoid(main_ref[:, 1024:2048])
        _fill_shifted(gs_ref, tt + CONV_HALO)

        for c in range(D_MODEL // CONV_CC):
            cols = slice(c * CONV_CC, (c + 1) * CONV_CC)

            def chunk(j, carry):
                r = pl.multiple_of(j * CONV_RC, CONV_RC)
                c_ref[pl.ds(r, CONV_RC), cols] = _conv_taps(gs_ref, w_ref, r, cols, lead, False) + b_ref[:, cols]
                return carry
            lax.fori_loop(0, tt // CONV_RC, chunk, 0, unroll=CONV_UNROLL)

        _, _, cn = _layer_norm_fwd(c_ref[...], g_ref[...], lb_ref[...])
        o_ref[...] = (_silu(cn) * _silu(main_ref[:, 2048:3072])).astype(BF16)

    vec = _const_spec((1, D_MODEL))
    row = pl.BlockSpec((tt, D_MODEL), lambda i: (i, 0))
    return pl.pallas_call(
        body, name="conv_fwd", grid=(t // tt,),
        out_shape=[jax.ShapeDtypeStruct((t, D_MODEL), BF16), jax.ShapeDtypeStruct((t, D_MODEL), F32)],
        in_specs=[pl.BlockSpec((tt, 3 * D_MODEL), lambda i: (i, 0)),
                  pl.BlockSpec((CONV_HALO, 3 * D_MODEL), lambda i: (jnp.maximum(i * (tt // CONV_HALO) - 1, 0), 0)),
                  _const_spec((CONV_K, D_MODEL)), vec, vec, vec],
        out_specs=[row, row],
        scratch_shapes=[pltpu.VMEM((8, tt + CONV_HALO, D_MODEL), F32)],
        compiler_params=_params("parallel"),
    )(proj, proj, dw, dwb, lng, lnb)


def _conv_bwd(proj, cf, dz, dw, lng, lnb, comm=None, tt=256):
    t = proj.shape[0]
    nt = t // tt
    te = tt + CONV_HALO

    def body(main_ref, next_ref, cf_ref, cfn_ref, dz_ref, dzn_ref, w_ref, g_ref, lb_ref,
             o_ref, ddw_ref, ddb_ref, dg_ref, dlb_ref, ds_ref, glu_ref, sb_ref):
        i = pl.program_id(0)

        @pl.when(i == 0)
        def _():
            ddw_ref[...] = jnp.zeros_like(ddw_ref)
            ddb_ref[...] = jnp.zeros_like(ddb_ref)
            dg_ref[...] = jnp.zeros_like(dg_ref)
            dlb_ref[...] = jnp.zeros_like(dlb_ref)

        lng = g_ref[...]
        chat, rstd, cn = _layer_norm_fwd(jnp.concatenate([cf_ref[...], cfn_ref[...]], axis=0), lng, lb_ref[...])
        gate = jnp.concatenate([main_ref[:, 2048:3072], next_ref[:, 2048:3072]], axis=0)
        dzv = jnp.concatenate([dz_ref[...], jnp.where(i < nt - 1, dzn_ref[...], 0.0)], axis=0)
        silu_cn, dsilu_cn = _silu_and_grad(cn)
        silu_gate, dsilu_gate = _silu_and_grad(gate)
        o_ref[:, 2048:3072] = (dzv * silu_cn * dsilu_gate)[0:tt].astype(BF16)
        dcn = dzv * silu_gate * dsilu_cn
        dg_ref[...] += jnp.sum((dcn * chat)[0:tt], axis=0, keepdims=True)
        dlb_ref[...] += jnp.sum(dcn[0:tt], axis=0, keepdims=True)
        dchat = dcn * lng
        dcf = rstd * (dchat - jnp.mean(dchat, axis=-1, keepdims=True) - chat * jnp.mean(dchat * chat, axis=-1, keepdims=True))
        ddb_ref[...] += jnp.sum(dcf[0:tt], axis=0, keepdims=True)
        ds_ref[0, 0:te, :] = dcf
        ds_ref[0, te:, :] = jnp.zeros((8, D_MODEL), F32)
        _fill_shifted(ds_ref, te + 8)
        sb_ref[...] = _sigmoid(main_ref[:, 1024:2048])
        glu_ref[...] = main_ref[:, 0:1024] * sb_ref[...]

        for c in range(D_MODEL // CONV_CC):
            cols = slice(c * CONV_CC, (c + 1) * CONV_CC)
            gcols = slice(c * CONV_CC + 1024, (c + 1) * CONV_CC + 1024)

            def chunk(j, carry):
                r = pl.multiple_of(j * CONV_RC, CONV_RC)
                dglu = _conv_taps(ds_ref, w_ref, r, cols, 0, True)
                sb = sb_ref[pl.ds(r, CONV_RC), cols]
                o_ref[pl.ds(r, CONV_RC), cols] = (dglu * sb).astype(BF16)
                o_ref[pl.ds(r, CONV_RC), gcols] = (dglu * glu_ref[pl.ds(r, CONV_RC), cols] * (1.0 - sb)).astype(BF16)
                return carry
            lax.fori_loop(0, tt // CONV_RC, chunk, 0, unroll=CONV_UNROLL)

            def taps(j, accs):
                r = pl.multiple_of(j * CONV_RC, CONV_RC)
                gl = glu_ref[pl.ds(r, CONV_RC), cols]
                new = list(accs)
                for m, blk in _tap_blocks(ds_ref, r, cols, 0):
                    prod = blk * gl
                    part = prod[0:8]
                    for q in range(1, CONV_RC // 8):
                        part = part + prod[8 * q:8 * q + 8]
                    new[m] = new[m] + part
                return tuple(new)
            accs = lax.fori_loop(0, tt // CONV_RC, taps, tuple(jnp.zeros((8, CONV_CC), F32) for _ in range(CONV_K)))
            for m in range(CONV_K):
                k = CONV_K - 1 - m
                ddw_ref[k:k + 1, cols] += jnp.sum(accs[m], axis=0, keepdims=True)

    vec = _const_spec((1, D_MODEL))
    vec_out = pl.BlockSpec((1, D_MODEL), lambda i: (0, 0))
    row = pl.BlockSpec((tt, D_MODEL), lambda i: (i, 0))
    nxt = lambda i: (jnp.minimum((i + 1) * (tt // CONV_HALO), t // CONV_HALO - 1), 0)
    nxt_row = pl.BlockSpec((CONV_HALO, D_MODEL), nxt)
    return _fused_call(
        body, comm, (proj, proj, cf, cf, dz, dz, dw, lng, lnb), name="conv_bwd", grid=(nt,),
        out_shape=[jax.ShapeDtypeStruct((t, 3 * D_MODEL), BF16), jax.ShapeDtypeStruct((CONV_K, D_MODEL), F32),
                   jax.ShapeDtypeStruct((1, D_MODEL), F32), jax.ShapeDtypeStruct((1, D_MODEL), F32),
                   jax.ShapeDtypeStruct((1, D_MODEL), F32)],
        in_specs=[pl.BlockSpec((tt, 3 * D_MODEL), lambda i: (i, 0)),
                  pl.BlockSpec((CONV_HALO, 3 * D_MODEL), nxt),
                  row, nxt_row, row, nxt_row,
                  _const_spec((CONV_K, D_MODEL)), vec, vec],
        out_specs=[pl.BlockSpec((tt, 3 * D_MODEL), lambda i: (i, 0)),
                   pl.BlockSpec((CONV_K, D_MODEL), lambda i: (0, 0)), vec_out, vec_out, vec_out],
        scratch_shapes=[pltpu.VMEM((8, te + 8, D_MODEL), F32), pltpu.VMEM((tt, D_MODEL), F32),
                        pltpu.VMEM((tt, D_MODEL), F32)],
        params=_params("arbitrary"))


def _piece_sum(parts, place, name):
    r = parts[0][0].shape[1]

    def body(p_ref, *refs):
        o_ref = refs[-1]
        acc = refs[0][0].astype(F32)
        for part in refs[1:-1]:
            acc = acc + part[0].astype(F32)
        o_ref[0] = acc

    blk = (1, r, D_MODEL)
    spec = lambda slot: pl.BlockSpec(blk, lambda j, p_ref: (slot(p_ref), 0, 0))
    return pl.pallas_call(
        body, name=name,
        grid_spec=pltpu.PrefetchScalarGridSpec(
            num_scalar_prefetch=1, grid=(1,),
            in_specs=[spec(slot) for _, slot in parts],
            out_specs=pl.BlockSpec(blk, lambda j, p_ref: (p_ref[1], 0, 0))),
        out_shape=jax.ShapeDtypeStruct((2, r, D_MODEL), F32),
        compiler_params=_params("arbitrary"),
    )(place, *[a for a, _ in parts])


def _direct_parts(own, recv):
    peer = lambda m: (lambda p: p[0] ^ m)
    return [(own, peer(0))] + [(recv, peer(m)) for m in range(1, N_DEV)]


def _pair_parts(sums, recv):
    const = lambda j: (lambda p: j)
    return [(sums, lambda p: p[2])] + [(recv, const(j)) for j in range(3)]


def _share_with_sibling(halves, name):
    n = len(halves)

    def body(*refs):
        outs = refs[n:2 * n]
        send_sems, recv_sems = refs[2 * n:]
        x, y, c, _ = _place()
        send = [pltpu.make_async_remote_copy(
            src_ref=outs[t].at[c], dst_ref=outs[t].at[c], send_sem=send_sems.at[t], recv_sem=recv_sems.at[t],
            device_id=(x, y, 1 - c), device_id_type=MESH_ID) for t in range(n)]
        recv = [pltpu.make_async_remote_copy(
            src_ref=outs[t].at[c], dst_ref=outs[t].at[1 - c], send_sem=send_sems.at[t], recv_sem=recv_sems.at[t],
            device_id=(x, y, 1 - c), device_id_type=MESH_ID) for t in range(n)]
        for cp in send:
            cp.start()
        for cp in recv:
            cp.wait_recv()
        for cp in send:
            cp.wait_send()

    return pl.pallas_call(
        body, name=name,
        out_shape=[jax.ShapeDtypeStruct(h.shape, h.dtype) for h in halves],
        in_specs=[ANY] * n, out_specs=[ANY] * n,
        input_output_aliases={t: t for t in range(n)},
        scratch_shapes=[pltpu.SemaphoreType.DMA((n,)), pltpu.SemaphoreType.DMA((n,))],
    )(*halves)


def _sum8(parts, name):
    r = parts.shape[1]

    def body(p_ref, o_ref):
        acc = p_ref[0]
        for k in range(1, N_DEV):
            acc = acc + p_ref[k]
        o_ref[...] = acc

    return pl.pallas_call(
        body, name=name, out_shape=jax.ShapeDtypeStruct((r, 128), F32),
        in_specs=[pl.BlockSpec(memory_space=pltpu.VMEM)], out_specs=pl.BlockSpec(memory_space=pltpu.VMEM),
    )(parts)


def _adamw(w, g, m, v, name):
    shape = w.shape
    cols = shape[-1]
    rows = w.size // cols
    rt = 256 if rows % 256 == 0 else rows

    def body(w_ref, g_ref, m_ref, v_ref, d_ref, nm_ref, nv_ref):
        gv = g_ref[...]
        mn = ADAM_B1 * m_ref[...] + (1.0 - ADAM_B1) * gv
        vn = ADAM_B2 * v_ref[...] + (1.0 - ADAM_B2) * (gv * gv)
        m_hat = mn / (1.0 - ADAM_B1 ** ADAM_STEP)
        v_hat = vn / (1.0 - ADAM_B2 ** ADAM_STEP)
        d_ref[...] = -ADAM_LR * (m_hat / (jnp.sqrt(v_hat) + ADAM_EPS) + ADAM_WD * w_ref[...])
        nm_ref[...] = mn
        nv_ref[...] = vn

    spec = pl.BlockSpec((rt, cols), lambda i: (i, 0))
    outs = pl.pallas_call(
        body, name=name, grid=(rows // rt,),
        out_shape=[jax.ShapeDtypeStruct((rows, cols), F32)] * 3,
        in_specs=[spec] * 4, out_specs=[spec] * 3,
        compiler_params=_params("parallel"),
    )(*[a.reshape(rows, cols) for a in (w, g, m, v)])
    return [o.reshape(shape) for o in outs]


SMALL_ROWS = 832


def _pack_small(g):
    parts = [g["loss"], g["pre1"].reshape(8, 128), g["post0"].reshape(8, 128),
             g["post1"].reshape(8, 128), g["sinks"], jnp.pad(g["pool_scale"].reshape(4, 128), ((0, 4), (0, 0))),
             g["pool_w"], g["dw"].reshape(248, 128), g["dwb"].reshape(8, 128), g["lng"].reshape(8, 128),
             g["lnb"].reshape(8, 128)]
    assert sum(p.shape[0] for p in parts) == SMALL_ROWS
    return jnp.concatenate(parts, axis=0)


def _unpack_small(s):
    out, r = {}, 0
    for key, rows, shape in (("loss", 8, (8, 128)), ("pre1", 8, (1, D_MODEL)), ("post", 16, (2, D_MODEL)),
                             ("sinks", 8, (8, 128)),
                             ("pool_scale", 4, (1, 512)), ("pad", 4, (4, 128)), ("pool_w", 512, (1, 4, 128, 128)),
                             ("dw", 248, (CONV_K, D_MODEL)), ("dwb", 8, (1, D_MODEL)), ("lng", 8, (1, D_MODEL)),
                             ("lnb", 8, (1, D_MODEL))):
        out[key] = s[r:r + rows].reshape(shape)
        r += rows
    return out


def kernel(x, pre_norm, post_norm, a_w_in, a_sinks, b_pool_w, b_pool_scale, ab_w_out, c_w_in, c_dw_w, c_dw_b, c_ln_g, c_ln_b, c_w_out, loss_target, m_pre_norm, m_post_norm, m_a_w_in, m_a_sinks, m_b_pool_w, m_b_pool_scale, m_ab_w_out, m_c_w_in, m_c_dw_w, m_c_dw_b, m_c_ln_g, m_c_ln_b, m_c_w_out, v_pre_norm, v_post_norm, v_a_w_in, v_a_sinks, v_b_pool_w, v_b_pool_scale, v_ab_w_out, v_c_w_in, v_c_dw_w, v_c_dw_b, v_c_ln_g, v_c_ln_b, v_c_w_out):
    ix, iy = lax.axis_index("x"), lax.axis_index("y")
    chip_cols = (2 * ix + iy) * 256

    pad8 = lambda v: jnp.pad(v, ((0, -v.shape[0] % 8), (0, 0)))
    vec_shard = jnp.concatenate([pad8(c_dw_w.reshape(CONV_K, 256)), pad8(c_dw_b), pad8(c_ln_g), pad8(c_ln_b),
                                 jnp.zeros((8, 256), F32)], axis=0)
    x0, target = x[0], loss_target[0]
    pre0, pre1 = pre_norm[0:1], pre_norm[1:2]
    post0, post1 = post_norm[0:1], post_norm[1:2]
    pool_w = b_pool_w[0]

    (wa_t,) = _run_comm(_Gather([a_w_in[0].T.astype(BF16)], halve=True), "gather_a_w_in")
    wa_t = wa_t.reshape(EVEN_IN, D_MODEL)
    proj0, (w_ab,) = _norm_matmul(x0, pre0, wa_t, "proj0_fwd", comm=_Gather([ab_w_out[0].astype(BF16)], halve=True))
    w_ab = w_ab.reshape(D_MODEL, D_MODEL)
    mix0, (wc_t, w_c, vecs) = _mix0_fwd(
        proj0, a_sinks, pool_w, b_pool_scale,
        comm=_Gather([c_w_in[0].T.astype(BF16), c_w_out[0].astype(BF16), vec_shard], halve=True))
    wc_t = wc_t.reshape(3 * D_MODEL, D_MODEL)
    w_c = w_c.reshape(D_MODEL, D_MODEL)
    vecs = vecs.reshape(4, 64, 256).transpose(1, 0, 2).reshape(64, D_MODEL)
    dw, dwb, lng, lnb = vecs[0:CONV_K], vecs[32:33], vecs[40:41], vecs[48:49]
    y0, x1 = _out_norm_res(mix0, w_ab, x0, post0, "out0_fwd")
    proj1, _ = _norm_matmul(x1, pre1, wc_t, "proj1_fwd")
    z1, cf1 = _conv_fwd(proj1, dw, dwb, lng, lnb)
    y1, g2, loss = _out_norm_res(z1, w_c, x1, post1, "out1_fwd", target=target)

    pieces = lambda m: m.reshape(N_DEV, m.shape[0] // N_DEV, D_MODEL)
    dz1, d_wc, d_post1 = _post_bwd(g2, y1, post1, w_c, z1, "out1_bwd")
    (dproj1, d_dw, d_dwb, d_lng, d_lnb), (r_wc,) = _conv_bwd(proj1, cf1, dz1, dw, lng, lnb,
                                                             comm=_Scatter([pieces(d_wc)]))
    g1, d_wct, d_pre1 = _pre_bwd(dproj1, wc_t, x1, pre1, g2, "proj1_bwd")
    dmix0, d_wab, d_post0 = _post_bwd(g1, y0, post0, w_ab, mix0, "out0_bwd")
    (dproj0, d_sinks, d_pw, d_ps), (r_wct, r_wab) = _mix0_bwd(proj0, dmix0, a_sinks, pool_w, b_pool_scale,
                                                             comm=_Scatter([pieces(d_wct), pieces(d_wab)]))
    g = dict(loss=loss, pre1=d_pre1, post0=d_post0, post1=d_post1, sinks=d_sinks, pool_w=d_pw, pool_scale=d_ps,
             dw=d_dw, dwb=d_dwb, lng=d_lng, lnb=d_lnb)
    d_wat, (small8,) = _proj_dw(dproj0, x0, pre0, "proj0_dw", comm=_Gather([_pack_small(g)], halve=False))
    (gx, d_pre0), (_, s_wat, r_wat) = _proj_dx(dproj0, wa_t, x0, pre0, g1, "proj0_dx",
                                               comm=_PairScatter([pieces(d_wat)]))
    (pre0_8,) = _run_comm(_Gather([d_pre0.reshape(8, 128)], halve=False), "pre0_all_gather")

    ic = lax.axis_index("c")
    place = jnp.stack([4 * ix + 2 * iy + ic, ic, 2 * ix + iy]).astype(jnp.int32)
    parts = [_pair_parts(s_wat, r_wat), _direct_parts(pieces(d_wab), r_wab), _direct_parts(pieces(d_wct), r_wct),
             _direct_parts(pieces(d_wc), r_wc)]
    halves = [_piece_sum(p, place, f"grad_sum{t}") for t, p in enumerate(parts)]
    g_wa_t, g_wab, g_wc_t, g_wc = [h.reshape(2 * h.shape[1], D_MODEL) for h in _share_with_sibling(halves, "grad_share")]
    g_a_w_in = g_wa_t.T[None]
    g_c_w_in = g_wc_t.T[None]
    g_ab_w_out = g_wab[None]
    g_c_w_out = g_wc[None]

    s = _unpack_small(_sum8(small8, "small_sum"))
    layer = lax.broadcasted_iota(jnp.int32, (2, D_MODEL), 0)
    g_pre = jnp.where(layer == 0, _sum8(pre0_8, "pre0_sum").reshape(1, D_MODEL), s["pre1"])
    g_post = s["post"]
    g_sinks = s["sinks"][:, 0].reshape(1, 8)
    g_pool_w, g_pool_scale = s["pool_w"], s["pool_scale"]
    g_dw = lax.dynamic_slice_in_dim(s["dw"], chip_cols, 256, axis=1).reshape(1, CONV_K, 1, 256)
    g_dwb = lax.dynamic_slice_in_dim(s["dwb"], chip_cols, 256, axis=1)
    g_lng = lax.dynamic_slice_in_dim(s["lng"], chip_cols, 256, axis=1)
    g_lnb = lax.dynamic_slice_in_dim(s["lnb"], chip_cols, 256, axis=1)

    grads = [g_pre, g_post, g_a_w_in, g_sinks, g_pool_w, g_pool_scale, g_ab_w_out, g_c_w_in, g_dw, g_dwb, g_lng, g_lnb,
             g_c_w_out]
    weights = [pre_norm, post_norm, a_w_in, a_sinks, b_pool_w, b_pool_scale, ab_w_out, c_w_in, c_dw_w, c_dw_b, c_ln_g,
               c_ln_b, c_w_out]
    moms = [m_pre_norm, m_post_norm, m_a_w_in, m_a_sinks, m_b_pool_w, m_b_pool_scale, m_ab_w_out, m_c_w_in, m_c_dw_w,
            m_c_dw_b, m_c_ln_g, m_c_ln_b, m_c_w_out]
    vars_ = [v_pre_norm, v_post_norm, v_a_w_in, v_a_sinks, v_b_pool_w, v_b_pool_scale, v_ab_w_out, v_c_w_in, v_c_dw_w,
             v_c_dw_b, v_c_ln_g, v_c_ln_b, v_c_w_out]
    deltas, new_m, new_v = [], [], []
    for k, (w, gr, m, v) in enumerate(zip(weights, grads, moms, vars_)):
        d, nm, nv = _adamw(w, gr, m, v, f"adamw{k}")
        deltas.append(d)
        new_m.append(nm)
        new_v.append(nv)
    return (s["loss"][0, 0], gx[None], *grads, *deltas, *new_m, *new_v)
```

```python
import functools

import jax
import jax.numpy as jnp
from jax import lax
from jax.experimental import pallas as pl
from jax.experimental.pallas import tpu as pltpu

F32 = jnp.float32
BF16 = jnp.bfloat16

D_MODEL = 1024
EPS = 1e-6
NEG = -1e30
HEAD_DIM = 64
GROUP = 4
KV_HEADS = 2
BLOCK = 128
EVEN_IN = 2304
COL_Q, COL_K, COL_GA, COL_U, COL_GB = 0, 512, 768, 1280, 1792
POOL_GROUPS = 4
POOL_GC = 128
POOL_HALO = 16
CONV_K = 31
CONV_HALO = 32
N_DEV = 8

ADAM_LR = 0.001
ADAM_B1 = 0.9
ADAM_B2 = 0.999
ADAM_EPS = 1e-08
ADAM_WD = 0.01
ADAM_STEP = 10

VMEM_LIMIT_BYTES = 56 * 1024 * 1024

NT = (((1,), (1,)), ((), ()))
TN = (((0,), (0,)), ((), ()))
MESH_ID = pl.DeviceIdType.MESH


def _params(*sem):
    return pltpu.CompilerParams(dimension_semantics=sem, vmem_limit_bytes=VMEM_LIMIT_BYTES)


def _const_spec(shape):
    nd = len(shape)
    return pl.BlockSpec(shape, lambda *_: (0,) * nd, pipeline_mode=pl.Buffered(1))


def _sigmoid(v):
    return 0.5 * jnp.tanh(0.5 * v) + 0.5


def _silu(v):
    return v * _sigmoid(v)


def _silu_and_grad(v):
    s = _sigmoid(v)
    return v * s, s * (1.0 + v * (1.0 - s))


ANY = pl.BlockSpec(memory_space=pl.ANY)


def _place():
    x, y, c = lax.axis_index("x"), lax.axis_index("y"), lax.axis_index("c")
    chips = [(1 - x, y), (x, 1 - y), (1 - x, 1 - y)]
    return x, y, c, chips


class _Gather:
    def __init__(self, blocks, halve):
        self.ins = list(blocks)
        self.halve = halve
        self.n = n = len(blocks)
        self.shapes = [((b.shape[0] // 2) if halve else b.shape[0], b.shape[1]) for b in blocks]
        self.out_shape = [jax.ShapeDtypeStruct((N_DEV, r, cols), b.dtype) for (r, cols), b in zip(self.shapes, blocks)]
        self.scratch = [pltpu.SemaphoreType.DMA((7 * n,)), pltpu.SemaphoreType.DMA((7 * n,)),
                        pltpu.SemaphoreType.DMA((n,))]

    def _copies(self, ins, outs, sems):
        send_sems, recv_sems, local_sems = sems
        x, y, c, chips = _place()
        me, sibling = (x, y, c), (x, y, 1 - c)

        def piece(t, px, py, pc):
            return outs[t].at[4 * px + 2 * py + pc]

        def own(t):
            return ins[t].at[pl.ds(c * self.shapes[t][0], self.shapes[t][0])] if self.halve else ins[t]

        def copy(t, k, block, to, src=None):
            return pltpu.make_async_remote_copy(
                src_ref=piece(t, *block) if src is None else src, dst_ref=piece(t, *block),
                send_sem=send_sems.at[7 * t + k], recv_sem=recv_sems.at[7 * t + k],
                device_id=to, device_id_type=MESH_ID)

        rng = range(self.n)
        return dict(
            mine=[pltpu.make_async_copy(own(t), piece(t, *me), local_sems.at[t]) for t in rng],
            first=[copy(t, 0, me, sibling, src=own(t)) for t in rng]
            + [copy(t, 1 + j, me, (*chip, c), src=own(t)) for t in rng for j, chip in enumerate(chips)],
            landed=[copy(t, 1 + j, (*chip, c), me) for j, chip in enumerate(chips) for t in rng],
            passed=[copy(t, 4 + j, (*chip, c), sibling) for j, chip in enumerate(chips) for t in rng],
            from_sibling=[copy(t, 0, sibling, me) for t in rng]
            + [copy(t, 4 + j, (*chip, 1 - c), me) for t in rng for j, chip in enumerate(chips)])

    def start(self, ins, outs, sems):
        d = self._copies(ins, outs, sems)
        for cp in d["mine"] + d["first"]:
            cp.start()

    def middle(self, ins, outs, sems):
        d = self._copies(ins, outs, sems)
        for got, fwd in zip(d["landed"], d["passed"]):
            got.wait_recv()
            fwd.start()

    def finish(self, ins, outs, sems):
        d = self._copies(ins, outs, sems)
        for cp in d["from_sibling"]:
            cp.wait_recv()
        for cp in d["first"] + d["passed"]:
            cp.wait_send()
        for cp in d["mine"]:
            cp.wait()


class _Scatter:
    def __init__(self, tensors):
        self.ins = list(tensors)
        self.n = n = len(tensors)
        self.out_shape = [jax.ShapeDtypeStruct(t.shape, t.dtype) for t in tensors]
        self.scratch = [pltpu.SemaphoreType.DMA((7 * n,)), pltpu.SemaphoreType.DMA((7 * n,))]

    def _copies(self, ins, outs, sems):
        send_sems, recv_sems = sems
        x, y, c, _ = _place()
        me = 4 * x + 2 * y + c
        sends, recvs = [], []
        for t in range(self.n):
            for m in range(1, N_DEV):
                px, py, pc = x ^ (m >> 2), y ^ ((m >> 1) & 1), c ^ (m & 1)
                q = 4 * px + 2 * py + pc
                sems_k = dict(send_sem=send_sems.at[7 * t + m - 1], recv_sem=recv_sems.at[7 * t + m - 1],
                              device_id=(px, py, pc), device_id_type=MESH_ID)
                sends.append(pltpu.make_async_remote_copy(src_ref=ins[t].at[q], dst_ref=outs[t].at[me], **sems_k))
                recvs.append(pltpu.make_async_remote_copy(src_ref=ins[t].at[me], dst_ref=outs[t].at[q], **sems_k))
        return sends, recvs

    def start(self, ins, outs, sems):
        for cp in self._copies(ins, outs, sems)[0]:
            cp.start()

    def middle(self, ins, outs, sems):
        pass

    def finish(self, ins, outs, sems):
        sends, recvs = self._copies(ins, outs, sems)
        for cp in recvs:
            cp.wait_recv()
        for cp in sends:
            cp.wait_send()


class _PairScatter:
    middle_step = 2

    def __init__(self, tensors):
        self.ins = list(tensors)
        self.n = n = len(tensors)
        r = tensors[0].shape[1]
        assert all(t.shape == (N_DEV, r, D_MODEL) and t.dtype == BF16 for t in tensors)
        quarter = lambda k: [jax.ShapeDtypeStruct((k, r, D_MODEL), BF16) for _ in tensors]
        self.out_shape = quarter(4) + quarter(4) + quarter(3)
        self.scratch = [pltpu.SemaphoreType.DMA((4 * n,)), pltpu.SemaphoreType.DMA((4 * n,)),
                        pltpu.SemaphoreType.DMA((3 * n,)), pltpu.SemaphoreType.DMA((3 * n,)),
                        pltpu.SemaphoreType.DMA((8,)), pltpu.VMEM((4, r, D_MODEL), BF16), pltpu.VMEM((4, r, D_MODEL), BF16)]

    def _copies(self, ins, outs, sems):
        n = self.n
        pair, sums, recv = outs[:n], outs[n:2 * n], outs[2 * n:]
        x, y, c, chips = _place()
        swap = [pltpu.make_async_remote_copy(
            src_ref=ins[t].at[2 * b + 1 - c], dst_ref=pair[t].at[b], send_sem=sems[0].at[4 * t + b],
            recv_sem=sems[1].at[4 * t + b], device_id=(x, y, 1 - c), device_id_type=MESH_ID)
            for t in range(n) for b in range(4)]
        to_owner = [pltpu.make_async_remote_copy(
            src_ref=sums[t].at[2 * cx + cy], dst_ref=recv[t].at[j], send_sem=sems[2].at[3 * t + j],
            recv_sem=sems[3].at[3 * t + j], device_id=(cx, cy, c), device_id_type=MESH_ID)
            for t in range(n) for j, (cx, cy) in enumerate(chips)]
        return swap, to_owner

    def start(self, ins, outs, sems):
        for cp in self._copies(ins, outs, sems)[0]:
            cp.start()

    def middle(self, ins, outs, sems):
        n = self.n
        pair, sums = outs[:n], outs[n:2 * n]
        local, mine_ref, theirs_ref = sems[4], sems[5], sems[6]
        c = lax.axis_index("c")
        swap, to_owner = self._copies(ins, outs, sems)
        for cp in swap:
            cp.wait_recv()
        for t in range(n):
            loads = [pltpu.make_async_copy(ins[t].at[2 * b + c], mine_ref.at[b], local.at[b]) for b in range(4)]
            loads += [pltpu.make_async_copy(pair[t].at[b], theirs_ref.at[b], local.at[4 + b]) for b in range(4)]
            for cp in loads:
                cp.start()
            for cp in loads:
                cp.wait()
            mine_ref[...] = (mine_ref[...].astype(F32) + theirs_ref[...].astype(F32)).astype(BF16)
            stores = [pltpu.make_async_copy(mine_ref.at[b], sums[t].at[b], local.at[b]) for b in range(4)]
            for cp in stores:
                cp.start()
            for cp in stores:
                cp.wait()
        for cp in to_owner:
            cp.start()
        for cp in swap:
            cp.wait_send()

    def finish(self, ins, outs, sems):
        to_owner = self._copies(ins, outs, sems)[1]
        for cp in to_owner:
            cp.wait_recv()
        for cp in to_owner:
            cp.wait_send()


class _Comms:
    def __init__(self, *comms):
        self.comms = comms
        self.ins = [a for c in comms for a in c.ins]
        self.out_shape = [s for c in comms for s in c.out_shape]
        self.scratch = [s for c in comms for s in c.scratch]

    def _each(self, phase, ins, outs, sems):
        i = o = s = 0
        for c in self.comms:
            ni, no, ns = len(c.ins), len(c.out_shape), len(c.scratch)
            getattr(c, phase)(ins[i:i + ni], outs[o:o + no], sems[s:s + ns])
            i, o, s = i + ni, o + no, s + ns

    def start(self, ins, outs, sems):
        self._each("start", ins, outs, sems)

    def middle(self, ins, outs, sems):
        self._each("middle", ins, outs, sems)

    def finish(self, ins, outs, sems):
        self._each("finish", ins, outs, sems)


def _run_comm(comm, name):
    n = len(comm.ins)

    def body(*refs):
        parts = refs[:n], refs[n:2 * n], refs[2 * n:]
        comm.start(*parts)
        comm.middle(*parts)
        comm.finish(*parts)

    return pl.pallas_call(body, name=name, out_shape=comm.out_shape, in_specs=[ANY] * n, out_specs=[ANY] * n,
                          scratch_shapes=comm.scratch)(*comm.ins)


def _fused_call(body, comm, args, *, name, grid, out_shape, in_specs, out_specs, scratch_shapes=(), params):
    single = not isinstance(out_shape, (list, tuple))
    out_shape = [out_shape] if single else list(out_shape)
    out_specs = [out_specs] if single else list(out_specs)
    if comm is None:
        res = pl.pallas_call(body, name=name, grid=grid, out_shape=out_shape, in_specs=in_specs, out_specs=out_specs,
                             scratch_shapes=list(scratch_shapes), compiler_params=params)(*args)
        return (res[0] if single else res), []
    n_in, n_out, n_scr = len(in_specs), len(out_shape), len(scratch_shapes)
    c_in, c_out = len(comm.ins), len(comm.out_shape)
    steps = grid[0]

    def fused(*refs):
        pos = 0
        groups = []
        for size in (n_in, c_in, n_out, c_out, n_scr, len(comm.scratch)):
            groups.append(refs[pos:pos + size])
            pos += size
        ins, c_ins, outs, c_outs, scr, c_sems = groups
        i = pl.program_id(0)

        @pl.when(i == 0)
        def _():
            comm.start(c_ins, c_outs, c_sems)

        @pl.when(i == min(getattr(comm, "middle_step", steps // 2), steps - 1))
        def _():
            comm.middle(c_ins, c_outs, c_sems)

        body(*ins, *outs, *scr)

        @pl.when(i == steps - 1)
        def _():
            comm.finish(c_ins, c_outs, c_sems)

    res = pl.pallas_call(
        fused, name=name, grid=grid, out_shape=out_shape + list(comm.out_shape),
        in_specs=list(in_specs) + [ANY] * c_in, out_specs=out_specs + [ANY] * c_out,
        scratch_shapes=list(scratch_shapes) + list(comm.scratch), compiler_params=params)(*args, *comm.ins)
    main = res[:n_out]
    return (main[0] if single else main), list(res[n_out:])


def _norm_matmul(x, gain, wt, name, comm=None, tm=512):
    t, n = x.shape[0], wt.shape[0]

    def body(x_ref, g_ref, wt_ref, o_ref):
        xv = x_ref[...]
        r = lax.rsqrt(jnp.mean(xv * xv, axis=-1, keepdims=True) + EPS)
        h = (xv * r * g_ref[...]).astype(BF16)
        o_ref[...] = lax.dot_general(h, wt_ref[...], NT, preferred_element_type=F32)

    return _fused_call(
        body, comm, (x, gain, wt), name=name, grid=(t // tm,),
        out_shape=jax.ShapeDtypeStruct((t, n), F32),
        in_specs=[pl.BlockSpec((tm, D_MODEL), lambda i: (i, 0)), _const_spec((1, D_MODEL)), _const_spec((n, D_MODEL))],
        out_specs=pl.BlockSpec((tm, n), lambda i: (i, 0)),
        params=_params("arbitrary"))


def _project_out(a, w_ref, x_ref, p_ref, y_ref):
    y = jnp.dot(a, w_ref[...], preferred_element_type=F32)
    y_ref[...] = y
    ry = lax.rsqrt(jnp.mean(y * y, axis=-1, keepdims=True) + EPS)
    return x_ref[...] + (y * ry) * p_ref[...]


def _post_bwd(g, y, post, w, a, name, tm=1024):
    t = g.shape[0]
    steps = t // tm

    def body(g_ref, y_ref, p_ref, w_ref, a_ref, da_ref, dw16_ref, dp_ref, dw_ref):
        @pl.when(pl.program_id(0) == 0)
        def _():
            dw_ref[...] = jnp.zeros_like(dw_ref)
            dp_ref[...] = jnp.zeros_like(dp_ref)

        gv = g_ref[...]
        yv = y_ref[...]
        ry = lax.rsqrt(jnp.mean(yv * yv, axis=-1, keepdims=True) + EPS)
        nv = yv * ry
        dp_ref[...] += jnp.sum(gv * nv, axis=0, keepdims=True)
        dn = gv * p_ref[...]
        dy = (ry * (dn - nv * jnp.mean(dn * nv, axis=-1, keepdims=True))).astype(BF16)
        da_ref[...] = lax.dot_general(dy, w_ref[...], NT, preferred_element_type=F32)
        dw_ref[...] += lax.dot_general(a_ref[...], dy, TN, preferred_element_type=F32)

        @pl.when(pl.program_id(0) == steps - 1)
        def _():
            dw16_ref[...] = dw_ref[...].astype(BF16)

    row = pl.BlockSpec((tm, D_MODEL), lambda i: (i, 0))
    return pl.pallas_call(
        body, name=name, grid=(steps,),
        out_shape=[jax.ShapeDtypeStruct((t, D_MODEL), F32), jax.ShapeDtypeStruct((D_MODEL, D_MODEL), BF16),
                   jax.ShapeDtypeStruct((1, D_MODEL), F32)],
        in_specs=[row, row, _const_spec((1, D_MODEL)), _const_spec((D_MODEL, D_MODEL)), row],
        out_specs=[row, pl.BlockSpec((D_MODEL, D_MODEL), lambda i: (0, 0)), pl.BlockSpec((1, D_MODEL), lambda i: (0, 0))],
        scratch_shapes=[pltpu.VMEM((D_MODEL, D_MODEL), F32)],
        compiler_params=_params("arbitrary"),
    )(g, y, post, w, a)


def _pre_bwd(dproj, wt, x_in, pre, g, name, tm=512):
    t, n = dproj.shape
    steps = t // tm

    def body(dp_ref, wt_ref, x_ref, pre_ref, g_ref, dx_ref, dwt16_ref, dpre_ref, dwt_ref):
        @pl.when(pl.program_id(0) == 0)
        def _():
            dwt_ref[...] = jnp.zeros_like(dwt_ref)
            dpre_ref[...] = jnp.zeros_like(dpre_ref)

        dpv = dp_ref[...]
        dh = jnp.dot(dpv, wt_ref[...], preferred_element_type=F32)
        xv = x_ref[...]
        r = lax.rsqrt(jnp.mean(xv * xv, axis=-1, keepdims=True) + EPS)
        xn = xv * r
        pv = pre_ref[...]
        dpre_ref[...] += jnp.sum(dh * xn, axis=0, keepdims=True)
        dxn = dh * pv
        dx_ref[...] = g_ref[...] + r * (dxn - xn * jnp.mean(dxn * xn, axis=-1, keepdims=True))
        h = (xn * pv).astype(BF16)
        dwt_ref[...] += lax.dot_general(dpv, h, TN, preferred_element_type=F32)

        @pl.when(pl.program_id(0) == steps - 1)
        def _():
            dwt16_ref[...] = dwt_ref[...].astype(BF16)

    row = pl.BlockSpec((tm, D_MODEL), lambda i: (i, 0))
    return pl.pallas_call(
        body, name=name, grid=(steps,),
        out_shape=[jax.ShapeDtypeStruct((t, D_MODEL), F32), jax.ShapeDtypeStruct((n, D_MODEL), BF16),
                   jax.ShapeDtypeStruct((1, D_MODEL), F32)],
        in_specs=[pl.BlockSpec((tm, n), lambda i: (i, 0)), _const_spec((n, D_MODEL)), row, _const_spec((1, D_MODEL)), row],
        out_specs=[row, _const_spec((n, D_MODEL)), pl.BlockSpec((1, D_MODEL), lambda i: (0, 0))],
        scratch_shapes=[pltpu.VMEM((n, D_MODEL), F32)],
        compiler_params=_params("arbitrary"),
    )(dproj, wt, x_in, pre, g)


def _proj_dw(dproj, x_in, pre, name, comm=None, tm=1024):
    t, n = dproj.shape
    steps = t // tm

    def body(dp_ref, x_ref, pre_ref, dwt16_ref, dwt_ref):
        @pl.when(pl.program_id(0) == 0)
        def _():
            dwt_ref[...] = jnp.zeros_like(dwt_ref)

        xv = x_ref[...]
        r = lax.rsqrt(jnp.mean(xv * xv, axis=-1, keepdims=True) + EPS)
        h = (xv * r * pre_ref[...]).astype(BF16)
        dwt_ref[...] += lax.dot_general(dp_ref[...], h, TN, preferred_element_type=F32)

        @pl.when(pl.program_id(0) == steps - 1)
        def _():
            dwt16_ref[...] = dwt_ref[...].astype(BF16)

    return _fused_call(
        body, comm, (dproj, x_in, pre), name=name, grid=(steps,),
        out_shape=jax.ShapeDtypeStruct((n, D_MODEL), BF16),
        in_specs=[pl.BlockSpec((tm, n), lambda i: (i, 0)), pl.BlockSpec((tm, D_MODEL), lambda i: (i, 0)),
                  _const_spec((1, D_MODEL))],
        out_specs=pl.BlockSpec((n, D_MODEL), lambda i: (0, 0)),
        scratch_shapes=[pltpu.VMEM((n, D_MODEL), F32)],
        params=_params("arbitrary"))


def _proj_dx(dproj, wt, x_in, pre, g, name, comm=None, tm=512):
    t, n = dproj.shape

    def body(dp_ref, wt_ref, x_ref, pre_ref, g_ref, dx_ref, dpre_ref):
        @pl.when(pl.program_id(0) == 0)
        def _():
            dpre_ref[...] = jnp.zeros_like(dpre_ref)

        dh = jnp.dot(dp_ref[...], wt_ref[...], preferred_element_type=F32)
        xv = x_ref[...]
        r = lax.rsqrt(jnp.mean(xv * xv, axis=-1, keepdims=True) + EPS)
        xn = xv * r
        dpre_ref[...] += jnp.sum(dh * xn, axis=0, keepdims=True)
        dxn = dh * pre_ref[...]
        dx_ref[...] = g_ref[...] + r * (dxn - xn * jnp.mean(dxn * xn, axis=-1, keepdims=True))

    row = pl.BlockSpec((tm, D_MODEL), lambda i: (i, 0))
    return _fused_call(
        body, comm, (dproj, wt, x_in, pre, g), name=name, grid=(t // tm,),
        out_shape=[jax.ShapeDtypeStruct((t, D_MODEL), F32), jax.ShapeDtypeStruct((1, D_MODEL), F32)],
        in_specs=[pl.BlockSpec((tm, n), lambda i: (i, 0)), _const_spec((n, D_MODEL)), row, _const_spec((1, D_MODEL)), row],
        out_specs=[row, pl.BlockSpec((1, D_MODEL), lambda i: (0, 0))],
        params=_params("arbitrary"))


def _group_masks():
    lane = lax.broadcasted_iota(jnp.int32, (1, GROUP * HEAD_DIM), 1)
    return [(lane // HEAD_DIM == g).astype(F32) for g in range(GROUP)]


def _stack_groups(v, masks):
    return jnp.concatenate([v * m for m in masks], axis=0)


def _unstack_groups(v, masks):
    out = v[0:BLOCK] * masks[0]
    for g in range(1, GROUP):
        out = out + v[g * BLOCK:(g + 1) * BLOCK] * masks[g]
    return out


def _repeat_head(kv2, kvh):
    first = lax.broadcasted_iota(jnp.int32, kv2.shape, 1) < HEAD_DIM
    rolled = pltpu.roll(kv2, HEAD_DIM, 1)
    one = jnp.where(first, kv2, rolled) if kvh == 0 else jnp.where(first, rolled, kv2)
    return jnp.concatenate([one, one], axis=1)


def _fold_head(v4):
    a = v4[:, 0:128] + v4[:, 128:256]
    return a + pltpu.roll(a, HEAD_DIM, 1)


ATTN_CONSTS = [pltpu.VMEM((KV_HEADS, GROUP * BLOCK, 2 * BLOCK), F32)]


def _fill_attn_bias(bias_ref):
    row = lax.broadcasted_iota(jnp.int32, (GROUP * BLOCK, 2 * BLOCK), 0)
    col = lax.broadcasted_iota(jnp.int32, (GROUP * BLOCK, 2 * BLOCK), 1)
    dist = (row % BLOCK) + BLOCK - col
    band = (dist >= 0) & (dist < BLOCK)
    rb = lax.broadcasted_iota(jnp.int32, (GROUP * BLOCK, 1), 0) // BLOCK
    for kvh in range(KV_HEADS):
        slope = jnp.zeros((GROUP * BLOCK, 1), F32)
        for g in range(GROUP):
            slope = jnp.where(rb == g, 2.0 ** (-(kvh * GROUP + g + 1)), slope)
        bias_ref[kvh] = jnp.where(band, -slope * dist.astype(F32), NEG)


def _row_sinks(kvh, sink_ref):
    rb = lax.broadcasted_iota(jnp.int32, (GROUP * BLOCK, 1), 0) // BLOCK
    sink = jnp.zeros((GROUP * BLOCK, 1), F32)
    for g in range(GROUP):
        sink = jnp.where(rb == g, sink_ref[0, kvh * GROUP + g], sink)
    return sink


def _attn_probs(qk, k4, bias, sink, no_past, masks):
    qs = _stack_groups(qk, masks).astype(BF16)
    s = lax.dot_general(qs, k4, NT, preferred_element_type=F32) * (HEAD_DIM ** -0.5) + bias
    s = jnp.concatenate([jnp.where(no_past, NEG, s[:, 0:BLOCK]), s[:, BLOCK:]], axis=1)
    mx = jnp.maximum(jnp.max(s, axis=-1, keepdims=True), sink)
    e = jnp.exp(s - mx)
    es = jnp.exp(sink - mx)
    inv = 1.0 / (jnp.sum(e, axis=-1, keepdims=True) + es)
    return qs, e * inv, es * inv


def _pool_forward(u_ext, g, t0):
    n = u_ext.shape[0] - POOL_HALO
    s = u_ext
    for step in range(g + 1):
        s = s + pltpu.roll(s, 1 << step, 0)
    w = 2 << g
    t = t0 + lax.broadcasted_iota(jnp.int32, (n, 1), 0)
    cnt = jnp.minimum(t + 1, w).astype(F32)
    return s[POOL_HALO:] / cnt - u_ext[POOL_HALO:]


def _layer0_fwd(proj, sinks, pool_w, pool_scale, w_out, x_in, post, comm=None, tq=512):
    t = proj.shape[0]
    nblk = tq // BLOCK

    def body(main_ref, halo_ref, sink_ref, pw_ref, ps_ref, w_ref, x_ref, p_ref, o_ref, y_ref, xo_ref, kv_ref, bias_ref):
        i = pl.program_id(0)
        t0 = i * tq
        masks = _group_masks()

        @pl.when(i == 0)
        def _():
            _fill_attn_bias(bias_ref)

        kv_ref[0:BLOCK, :] = halo_ref[:, COL_K:COL_K + 256]
        kv_ref[BLOCK:, :] = main_ref[:, COL_K:COL_K + 256]

        def block(jb, carry):
            r0 = pl.multiple_of(jb * BLOCK, BLOCK)
            no_past = t0 + r0 == 0
            q = main_ref[pl.ds(r0, BLOCK), COL_Q:COL_Q + 512]
            ga = main_ref[pl.ds(r0, BLOCK), COL_GA:COL_GA + 512]
            kk = kv_ref[pl.ds(r0, 2 * BLOCK), 0:128]
            vv = kv_ref[pl.ds(r0, 2 * BLOCK), 128:256]
            outs = []
            for kvh in range(KV_HEADS):
                k4 = _repeat_head(kk, kvh).astype(BF16)
                v4 = _repeat_head(vv, kvh).astype(BF16)
                _, p, _ = _attn_probs(q[:, kvh * 256:(kvh + 1) * 256], k4, bias_ref[kvh], _row_sinks(kvh, sink_ref), no_past, masks)
                pv = jnp.dot(p.astype(BF16), v4, preferred_element_type=F32)
                outs.append(_unstack_groups(pv, masks))
            attn = jnp.concatenate(outs, axis=1)
            o_ref[pl.ds(r0, BLOCK), 0:512] = (attn * _silu(ga)).astype(BF16)
            return carry

        lax.fori_loop(0, nblk, block, 0, unroll=2)

        for g in range(POOL_GROUPS):
            cu = COL_U + g * POOL_GC
            cg = COL_GB + g * POOL_GC
            halo_u = jnp.where(i == 0, 0.0, halo_ref[BLOCK - POOL_HALO:BLOCK, cu:cu + POOL_GC])
            u_ext = jnp.concatenate([halo_u, main_ref[:, cu:cu + POOL_GC]], axis=0)
            pooled = _pool_forward(u_ext, g, t0)
            y = jnp.dot(pooled.astype(BF16), pw_ref[g].astype(BF16), preferred_element_type=F32)
            y = y * ps_ref[:, g * POOL_GC:(g + 1) * POOL_GC]
            o_ref[:, 512 + g * POOL_GC:512 + (g + 1) * POOL_GC] = (y * _silu(main_ref[:, cg:cg + POOL_GC])).astype(BF16)

        xo_ref[...] = _project_out(o_ref[...], w_ref, x_ref, p_ref, y_ref)

    row = pl.BlockSpec((tq, D_MODEL), lambda i: (i, 0))
    return _fused_call(
        body, comm, (proj, proj, sinks, pool_w, pool_scale, w_out, x_in, post), name="layer0_fwd", grid=(t // tq,),
        out_shape=[jax.ShapeDtypeStruct((t, D_MODEL), BF16), jax.ShapeDtypeStruct((t, D_MODEL), F32),
                   jax.ShapeDtypeStruct((t, D_MODEL), F32)],
        in_specs=[pl.BlockSpec((tq, EVEN_IN), lambda i: (i, 0)),
                  pl.BlockSpec((BLOCK, EVEN_IN), lambda i: (jnp.maximum(i * nblk - 1, 0), 0)),
                  pl.BlockSpec(memory_space=pltpu.SMEM),
                  _const_spec((POOL_GROUPS, POOL_GC, POOL_GC)), _const_spec((1, 512)),
                  _const_spec((D_MODEL, D_MODEL)), row, _const_spec((1, D_MODEL))],
        out_specs=[row, row, row],
        scratch_shapes=[pltpu.VMEM((tq + BLOCK, 256), F32)] + ATTN_CONSTS,
        params=_params("arbitrary"))


def _mix0_bwd(proj, dmix, sinks, pool_w, pool_scale, comm=None, tq=512):
    t = proj.shape[0]
    nt = t // tq
    nblk = tq // BLOCK

    def body(main_ref, halo_ref, next_ref, dm_ref, dmn_ref, sink_ref, pw_ref, ps_ref,
             o_ref, dsk_ref, dpw_ref, dps_ref, kv_ref, dkv_ref, carry_ref, bias_ref):
        i = pl.program_id(0)
        ii = nt - 1 - i
        t0 = ii * tq
        masks = _group_masks()

        @pl.when(i == 0)
        def _():
            _fill_attn_bias(bias_ref)
            dsk_ref[...] = jnp.zeros_like(dsk_ref)
            dpw_ref[...] = jnp.zeros_like(dpw_ref)
            dps_ref[...] = jnp.zeros_like(dps_ref)
            carry_ref[...] = jnp.zeros_like(carry_ref)

        kv_ref[0:BLOCK, :] = halo_ref[:, COL_K:COL_K + 256]
        kv_ref[BLOCK:, :] = main_ref[:, COL_K:COL_K + 256]
        dkv_ref[0:tq, :] = jnp.zeros((tq, 256), F32)
        dkv_ref[tq:, :] = carry_ref[...]

        def block(jb, carry):
            r0 = pl.multiple_of(jb * BLOCK, BLOCK)
            no_past = t0 + r0 == 0
            q = main_ref[pl.ds(r0, BLOCK), COL_Q:COL_Q + 512]
            ga = main_ref[pl.ds(r0, BLOCK), COL_GA:COL_GA + 512]
            dya = dm_ref[pl.ds(r0, BLOCK), 0:512]
            kk = kv_ref[pl.ds(r0, 2 * BLOCK), 0:128]
            vv = kv_ref[pl.ds(r0, 2 * BLOCK), 128:256]
            silu_ga, dsilu_ga = _silu_and_grad(ga)
            do = dya * silu_ga
            first = lax.broadcasted_iota(jnp.int32, (2 * BLOCK, 128), 1) < HEAD_DIM
            attn, dq, dk, dv = [], [], [], []
            for kvh in range(KV_HEADS):
                k4 = _repeat_head(kk, kvh).astype(BF16)
                v4 = _repeat_head(vv, kvh).astype(BF16)
                qs, p, ps = _attn_probs(q[:, kvh * 256:(kvh + 1) * 256], k4, bias_ref[kvh], _row_sinks(kvh, sink_ref), no_past, masks)
                pb = p.astype(BF16)
                o_k = _unstack_groups(jnp.dot(pb, v4, preferred_element_type=F32), masks)
                do_k = do[:, kvh * 256:(kvh + 1) * 256]
                dos = _stack_groups(do_k, masks).astype(BF16)
                prod = do_k * o_k
                delta = jnp.concatenate([jnp.sum(prod * m, axis=-1, keepdims=True) for m in masks], axis=0)
                dp = lax.dot_general(dos, v4, NT, preferred_element_type=F32)
                ds = (p * (dp - delta)).astype(BF16)
                sink_term = ps * delta
                for g in range(GROUP):
                    h = kvh * GROUP + g
                    dsk_ref[h:h + 1, :] -= jnp.sum(sink_term[g * BLOCK:(g + 1) * BLOCK], keepdims=True)
                scale = HEAD_DIM ** -0.5
                dq.append(_unstack_groups(jnp.dot(ds, k4, preferred_element_type=F32), masks) * scale)
                dk.append(_fold_head(lax.dot_general(ds, qs, TN, preferred_element_type=F32)) * scale)
                dv.append(_fold_head(lax.dot_general(pb, dos, TN, preferred_element_type=F32)))
                attn.append(o_k)
            o_ref[pl.ds(r0, BLOCK), COL_Q:COL_Q + 512] = jnp.concatenate(dq, axis=1).astype(BF16)
            o_all = jnp.concatenate(attn, axis=1)
            o_ref[pl.ds(r0, BLOCK), COL_GA:COL_GA + 512] = (dya * o_all * dsilu_ga).astype(BF16)
            dkv = jnp.concatenate([jnp.where(first, dk[0], dk[1]), jnp.where(first, dv[0], dv[1])], axis=1)
            dkv_ref[pl.ds(r0, 2 * BLOCK), :] += dkv
            return carry

        lax.fori_loop(0, nblk, block, 0, unroll=2)
        carry_ref[...] = dkv_ref[0:BLOCK, :]
        o_ref[:, COL_K:COL_K + 256] = dkv_ref[BLOCK:, :].astype(BF16)

        last = ii == nt - 1
        for g in range(POOL_GROUPS):
            cu = COL_U + g * POOL_GC
            cg = COL_GB + g * POOL_GC
            cm = 512 + g * POOL_GC
            pw = pw_ref[g].astype(BF16)
            sc = ps_ref[:, g * POOL_GC:(g + 1) * POOL_GC]
            halo_u = jnp.where(ii == 0, 0.0, halo_ref[BLOCK - POOL_HALO:BLOCK, cu:cu + POOL_GC])
            u_ext = jnp.concatenate([halo_u, main_ref[:, cu:cu + POOL_GC]], axis=0)
            pooled = _pool_forward(u_ext, g, t0).astype(BF16)
            y_raw = jnp.dot(pooled, pw, preferred_element_type=F32)
            gb = main_ref[:, cg:cg + POOL_GC]
            dyb = dm_ref[:, cm:cm + POOL_GC]
            silu_gb, dsilu_gb = _silu_and_grad(gb)
            dypool = dyb * silu_gb
            dps_ref[:, g * POOL_GC:(g + 1) * POOL_GC] += jnp.sum(dypool * y_raw, axis=0, keepdims=True)
            o_ref[:, cg:cg + POOL_GC] = (dyb * (y_raw * sc) * dsilu_gb).astype(BF16)
            dyraw = dypool * sc
            dyraw_n = jnp.where(last, 0.0, dmn_ref[:, cm:cm + POOL_GC] * _silu(next_ref[:, cg:cg + POOL_GC]) * sc)
            dpw_ref[g * POOL_GC:(g + 1) * POOL_GC, :] += lax.dot_general(pooled, dyraw.astype(BF16), TN,
                                                                         preferred_element_type=F32)
            dyraw_ext = jnp.concatenate([dyraw, dyraw_n], axis=0).astype(BF16)
            dpooled = lax.dot_general(dyraw_ext, pw, NT, preferred_element_type=F32)
            w = 2 << g
            tt = t0 + lax.broadcasted_iota(jnp.int32, (tq + POOL_HALO, 1), 0)
            s = dpooled / jnp.minimum(tt + 1, w).astype(F32)
            for step in range(g + 1):
                s = s + pltpu.roll(s, tq + POOL_HALO - (1 << step), 0)
            o_ref[:, cu:cu + POOL_GC] = (s[0:tq] - dpooled[0:tq]).astype(BF16)

    rev = lambda i: nt - 1 - i
    return _fused_call(
        body, comm, (proj, proj, proj, dmix, dmix, sinks, pool_w, pool_scale), name="mix0_bwd", grid=(nt,),
        out_shape=[jax.ShapeDtypeStruct((t, EVEN_IN), BF16), jax.ShapeDtypeStruct((8, 128), F32),
                   jax.ShapeDtypeStruct((POOL_GROUPS * POOL_GC, POOL_GC), F32), jax.ShapeDtypeStruct((1, 512), F32)],
        in_specs=[pl.BlockSpec((tq, EVEN_IN), lambda i: (rev(i), 0)),
                  pl.BlockSpec((BLOCK, EVEN_IN), lambda i: (jnp.maximum(rev(i) * nblk - 1, 0), 0)),
                  pl.BlockSpec((POOL_HALO, EVEN_IN),
                               lambda i: (jnp.minimum((rev(i) + 1) * (tq // POOL_HALO), t // POOL_HALO - 1), 0)),
                  pl.BlockSpec((tq, D_MODEL), lambda i: (rev(i), 0)),
                  pl.BlockSpec((POOL_HALO, D_MODEL),
                               lambda i: (jnp.minimum((rev(i) + 1) * (tq // POOL_HALO), t // POOL_HALO - 1), 0)),
                  pl.BlockSpec(memory_space=pltpu.SMEM),
                  _const_spec((POOL_GROUPS, POOL_GC, POOL_GC)), _const_spec((1, 512))],
        out_specs=[pl.BlockSpec((tq, EVEN_IN), lambda i: (rev(i), 0)),
                   pl.BlockSpec((8, 128), lambda i: (0, 0)),
                   pl.BlockSpec((POOL_GROUPS * POOL_GC, POOL_GC), lambda i: (0, 0)),
                   pl.BlockSpec((1, 512), lambda i: (0, 0))],
        scratch_shapes=[pltpu.VMEM((tq + BLOCK, 256), F32), pltpu.VMEM((tq + BLOCK, 256), F32),
                        pltpu.VMEM((BLOCK, 256), F32)] + ATTN_CONSTS,
        params=_params("arbitrary"))


CONV_RC = 32
CONV_CC = 128
CONV_CHAINS = 4
CONV_UNROLL = 2


def _fill_shifted(s_ref, rows):
    for b in range(1, 8):
        s_ref[b, 0:rows - 8, :] = s_ref[0, b:b + rows - 8, :]


def _tap_blocks(s_ref, r, cols, lead):
    for b in range(8):
        taps = [(a, 8 * a + b - lead) for a in range(5) if 0 <= 8 * a + b - lead < CONV_K]
        span = 8 * max(a for a, _ in taps) + CONV_RC
        blk = s_ref[b, pl.ds(r, span), cols]
        for a, k in taps:
            yield k, blk[8 * a:8 * a + CONV_RC]


def _conv_taps(s_ref, w_ref, r, cols, lead, reverse):
    accs = [None] * CONV_CHAINS
    for n, (k, blk) in enumerate(_tap_blocks(s_ref, r, cols, lead)):
        kw = CONV_K - 1 - k if reverse else k
        term = blk * w_ref[kw:kw + 1, cols]
        accs[n % CONV_CHAINS] = term if accs[n % CONV_CHAINS] is None else accs[n % CONV_CHAINS] + term
    return (accs[0] + accs[1]) + (accs[2] + accs[3])


def _layer_norm_fwd(cf, lng, lnb):
    mu = jnp.mean(cf, axis=-1, keepdims=True)
    xc = cf - mu
    rstd = lax.rsqrt(jnp.mean(xc * xc, axis=-1, keepdims=True) + EPS)
    chat = xc * rstd
    return chat, rstd, chat * lng + lnb


def _layer1_fwd(proj, dw, dwb, lng, lnb, w_out, x_in, post, target, tt=256):
    t = proj.shape[0]
    lead = CONV_HALO - (CONV_K - 1)

    def body(main_ref, halo_ref, w_ref, b_ref, g_ref, lb_ref, wo_ref, x_ref, p_ref, t_ref,
             o_ref, c_ref, y_ref, dl_ref, l_ref, gs_ref):
        i = pl.program_id(0)
        hv = halo_ref[...]
        gs_ref[0, 0:CONV_HALO, :] = jnp.where(i == 0, 0.0, hv[:, 0:1024] * _sigmoid(hv[:, 1024:2048]))
        gs_ref[0, CONV_HALO:CONV_HALO + tt, :] = main_ref[:, 0:1024] * _sigmoid(main_ref[:, 1024:2048])
        _fill_shifted(gs_ref, tt + CONV_HALO)

        for c in range(D_MODEL // CONV_CC):
            cols = slice(c * CONV_CC, (c + 1) * CONV_CC)

            def chunk(j, carry):
                r = pl.multiple_of(j * CONV_RC, CONV_RC)
                c_ref[pl.ds(r, CONV_RC), cols] = _conv_taps(gs_ref, w_ref, r, cols, lead, False) + b_ref[:, cols]
                return carry
            lax.fori_loop(0, tt // CONV_RC, chunk, 0, unroll=CONV_UNROLL)

        _, _, cn = _layer_norm_fwd(c_ref[...], g_ref[...], lb_ref[...])
        o_ref[...] = (_silu(cn) * _silu(main_ref[:, 2048:3072])).astype(BF16)

        d = _project_out(o_ref[...], wo_ref, x_ref, p_ref, y_ref) - t_ref[...]
        dl_ref[...] = d * (1.0 / D_MODEL)

        @pl.when(i == 0)
        def _():
            l_ref[...] = jnp.zeros_like(l_ref)

        l_ref[...] += 0.5 * jnp.sum(jnp.mean(d * d, axis=-1, keepdims=True))

    vec = _const_spec((1, D_MODEL))
    row = pl.BlockSpec((tt, D_MODEL), lambda i: (i, 0))
    f32_rows = jax.ShapeDtypeStruct((t, D_MODEL), F32)
    return pl.pallas_call(
        body, name="layer1_fwd", grid=(t // tt,),
        out_shape=[jax.ShapeDtypeStruct((t, D_MODEL), BF16), f32_rows, f32_rows, f32_rows,
                   jax.ShapeDtypeStruct((8, 128), F32)],
        in_specs=[pl.BlockSpec((tt, 3 * D_MODEL), lambda i: (i, 0)),
                  pl.BlockSpec((CONV_HALO, 3 * D_MODEL), lambda i: (jnp.maximum(i * (tt // CONV_HALO) - 1, 0), 0)),
                  _const_spec((CONV_K, D_MODEL)), vec, vec, vec,
                  _const_spec((D_MODEL, D_MODEL)), row, vec, row],
        out_specs=[row, row, row, row, pl.BlockSpec((8, 128), lambda i: (0, 0))],
        scratch_shapes=[pltpu.VMEM((8, tt + CONV_HALO, D_MODEL), F32)],
        compiler_params=_params("arbitrary"),
    )(proj, proj, dw, dwb, lng, lnb, w_out, x_in, post, target)


def _conv_bwd(proj, cf, dz, dw, lng, lnb, comm=None, tt=256):
    t = proj.shape[0]
    nt = t // tt
    te = tt + CONV_HALO

    def body(main_ref, next_ref, cf_ref, cfn_ref, dz_ref, dzn_ref, w_ref, g_ref, lb_ref,
             o_ref, ddw_ref, ddb_ref, dg_ref, dlb_ref, ds_ref, glu_ref, sb_ref):
        i = pl.program_id(0)

        @pl.when(i == 0)
        def _():
            ddw_ref[...] = jnp.zeros_like(ddw_ref)
            ddb_ref[...] = jnp.zeros_like(ddb_ref)
            dg_ref[...] = jnp.zeros_like(dg_ref)
            dlb_ref[...] = jnp.zeros_like(dlb_ref)

        lng = g_ref[...]
        chat, rstd, cn = _layer_norm_fwd(jnp.concatenate([cf_ref[...], cfn_ref[...]], axis=0), lng, lb_ref[...])
        gate = jnp.concatenate([main_ref[:, 2048:3072], next_ref[:, 2048:3072]], axis=0)
        dzv = jnp.concatenate([dz_ref[...], jnp.where(i < nt - 1, dzn_ref[...], 0.0)], axis=0)
        silu_cn, dsilu_cn = _silu_and_grad(cn)
        silu_gate, dsilu_gate = _silu_and_grad(gate)
        o_ref[:, 2048:3072] = (dzv * silu_cn * dsilu_gate)[0:tt].astype(BF16)
        dcn = dzv * silu_gate * dsilu_cn
        dg_ref[...] += jnp.sum((dcn * chat)[0:tt], axis=0, keepdims=True)
        dlb_ref[...] += jnp.sum(dcn[0:tt], axis=0, keepdims=True)
        dchat = dcn * lng
        dcf = rstd * (dchat - jnp.mean(dchat, axis=-1, keepdims=True) - chat * jnp.mean(dchat * chat, axis=-1, keepdims=True))
        ddb_ref[...] += jnp.sum(dcf[0:tt], axis=0, keepdims=True)
        ds_ref[0, 0:te, :] = dcf
        ds_ref[0, te:, :] = jnp.zeros((8, D_MODEL), F32)
        _fill_shifted(ds_ref, te + 8)
        sb_ref[...] = _sigmoid(main_ref[:, 1024:2048])
        glu_ref[...] = main_ref[:, 0:1024] * sb_ref[...]

        for c in range(D_MODEL // CONV_CC):
            cols = slice(c * CONV_CC, (c + 1) * CONV_CC)
            gcols = slice(c * CONV_CC + 1024, (c + 1) * CONV_CC + 1024)

            def chunk(j, carry):
                r = pl.multiple_of(j * CONV_RC, CONV_RC)
                dglu = _conv_taps(ds_ref, w_ref, r, cols, 0, True)
                sb = sb_ref[pl.ds(r, CONV_RC), cols]
                o_ref[pl.ds(r, CONV_RC), cols] = (dglu * sb).astype(BF16)
                o_ref[pl.ds(r, CONV_RC), gcols] = (dglu * glu_ref[pl.ds(r, CONV_RC), cols] * (1.0 - sb)).astype(BF16)
                return carry
            lax.fori_loop(0, tt // CONV_RC, chunk, 0, unroll=CONV_UNROLL)

            def taps(j, accs):
                r = pl.multiple_of(j * CONV_RC, CONV_RC)
                gl = glu_ref[pl.ds(r, CONV_RC), cols]
                new = list(accs)
                for m, blk in _tap_blocks(ds_ref, r, cols, 0):
                    prod = blk * gl
                    part = prod[0:8]
                    for q in range(1, CONV_RC // 8):
                        part = part + prod[8 * q:8 * q + 8]
                    new[m] = new[m] + part
                return tuple(new)
            accs = lax.fori_loop(0, tt // CONV_RC, taps, tuple(jnp.zeros((8, CONV_CC), F32) for _ in range(CONV_K)))
            for m in range(CONV_K):
                k = CONV_K - 1 - m
                ddw_ref[k:k + 1, cols] += jnp.sum(accs[m], axis=0, keepdims=True)

    vec = _const_spec((1, D_MODEL))
    vec_out = pl.BlockSpec((1, D_MODEL), lambda i: (0, 0))
    row = pl.BlockSpec((tt, D_MODEL), lambda i: (i, 0))
    nxt = lambda i: (jnp.minimum((i + 1) * (tt // CONV_HALO), t // CONV_HALO - 1), 0)
    nxt_row = pl.BlockSpec((CONV_HALO, D_MODEL), nxt)
    return _fused_call(
        body, comm, (proj, proj, cf, cf, dz, dz, dw, lng, lnb), name="conv_bwd", grid=(nt,),
        out_shape=[jax.ShapeDtypeStruct((t, 3 * D_MODEL), BF16), jax.ShapeDtypeStruct((CONV_K, D_MODEL), F32),
                   jax.ShapeDtypeStruct((1, D_MODEL), F32), jax.ShapeDtypeStruct((1, D_MODEL), F32),
                   jax.ShapeDtypeStruct((1, D_MODEL), F32)],
        in_specs=[pl.BlockSpec((tt, 3 * D_MODEL), lambda i: (i, 0)),
                  pl.BlockSpec((CONV_HALO, 3 * D_MODEL), nxt),
                  row, nxt_row, row, nxt_row,
                  _const_spec((CONV_K, D_MODEL)), vec, vec],
        out_specs=[pl.BlockSpec((tt, 3 * D_MODEL), lambda i: (i, 0)),
                   pl.BlockSpec((CONV_K, D_MODEL), lambda i: (0, 0)), vec_out, vec_out, vec_out],
        scratch_shapes=[pltpu.VMEM((8, te + 8, D_MODEL), F32), pltpu.VMEM((tt, D_MODEL), F32),
                        pltpu.VMEM((tt, D_MODEL), F32)],
        params=_params("arbitrary"))


def _piece_sum(parts, place, name):
    r = parts[0][0].shape[1]

    def body(p_ref, *refs):
        o_ref = refs[-1]
        acc = refs[0][0].astype(F32)
        for part in refs[1:-1]:
            acc = acc + part[0].astype(F32)
        o_ref[0] = acc

    blk = (1, r, D_MODEL)
    spec = lambda slot: pl.BlockSpec(blk, lambda j, p_ref: (slot(p_ref), 0, 0))
    return pl.pallas_call(
        body, name=name,
        grid_spec=pltpu.PrefetchScalarGridSpec(
            num_scalar_prefetch=1, grid=(1,),
            in_specs=[spec(slot) for _, slot in parts],
            out_specs=pl.BlockSpec(blk, lambda j, p_ref: (p_ref[1], 0, 0))),
        out_shape=jax.ShapeDtypeStruct((2, r, D_MODEL), F32),
        compiler_params=_params("arbitrary"),
    )(place, *[a for a, _ in parts])


def _direct_parts(own, recv):
    peer = lambda m: (lambda p: p[0] ^ m)
    return [(own, peer(0))] + [(recv, peer(m)) for m in range(1, N_DEV)]


def _pair_parts(sums, recv):
    const = lambda j: (lambda p: j)
    return [(sums, lambda p: p[2])] + [(recv, const(j)) for j in range(3)]


def _share_with_sibling(halves, name):
    n = len(halves)

    def body(*refs):
        outs = refs[n:2 * n]
        send_sems, recv_sems = refs[2 * n:]
        x, y, c, _ = _place()
        send = [pltpu.make_async_remote_copy(
            src_ref=outs[t].at[c], dst_ref=outs[t].at[c], send_sem=send_sems.at[t], recv_sem=recv_sems.at[t],
            device_id=(x, y, 1 - c), device_id_type=MESH_ID) for t in range(n)]
        recv = [pltpu.make_async_remote_copy(
            src_ref=outs[t].at[c], dst_ref=outs[t].at[1 - c], send_sem=send_sems.at[t], recv_sem=recv_sems.at[t],
            device_id=(x, y, 1 - c), device_id_type=MESH_ID) for t in range(n)]
        for cp in send:
            cp.start()
        for cp in recv:
            cp.wait_recv()
        for cp in send:
            cp.wait_send()

    return pl.pallas_call(
        body, name=name,
        out_shape=[jax.ShapeDtypeStruct(h.shape, h.dtype) for h in halves],
        in_specs=[ANY] * n, out_specs=[ANY] * n,
        input_output_aliases={t: t for t in range(n)},
        scratch_shapes=[pltpu.SemaphoreType.DMA((n,)), pltpu.SemaphoreType.DMA((n,))],
    )(*halves)


def _sum8(parts, name):
    r = parts.shape[1]

    def body(p_ref, o_ref):
        acc = p_ref[0]
        for k in range(1, N_DEV):
            acc = acc + p_ref[k]
        o_ref[...] = acc

    return pl.pallas_call(
        body, name=name, out_shape=jax.ShapeDtypeStruct((r, 128), F32),
        in_specs=[pl.BlockSpec(memory_space=pltpu.VMEM)], out_specs=pl.BlockSpec(memory_space=pltpu.VMEM),
    )(parts)


def _adamw(w, g, m, v, name):
    shape = w.shape
    cols = shape[-1]
    rows = w.size // cols
    rt = 256 if rows % 256 == 0 else rows

    def body(w_ref, g_ref, m_ref, v_ref, d_ref, nm_ref, nv_ref):
        gv = g_ref[...]
        mn = ADAM_B1 * m_ref[...] + (1.0 - ADAM_B1) * gv
        vn = ADAM_B2 * v_ref[...] + (1.0 - ADAM_B2) * (gv * gv)
        m_hat = mn / (1.0 - ADAM_B1 ** ADAM_STEP)
        v_hat = vn / (1.0 - ADAM_B2 ** ADAM_STEP)
        d_ref[...] = -ADAM_LR * (m_hat / (jnp.sqrt(v_hat) + ADAM_EPS) + ADAM_WD * w_ref[...])
        nm_ref[...] = mn
        nv_ref[...] = vn

    spec = pl.BlockSpec((rt, cols), lambda i: (i, 0))
    outs = pl.pallas_call(
        body, name=name, grid=(rows // rt,),
        out_shape=[jax.ShapeDtypeStruct((rows, cols), F32)] * 3,
        in_specs=[spec] * 4, out_specs=[spec] * 3,
        compiler_params=_params("parallel"),
    )(*[a.reshape(rows, cols) for a in (w, g, m, v)])
    return [o.reshape(shape) for o in outs]


SMALL_ROWS = 832


def _pack_small(g):
    parts = [g["loss"], g["pre1"].reshape(8, 128), g["post0"].reshape(8, 128),
             g["post1"].reshape(8, 128), g["sinks"], jnp.pad(g["pool_scale"].reshape(4, 128), ((0, 4), (0, 0))),
             g["pool_w"], g["dw"].reshape(248, 128), g["dwb"].reshape(8, 128), g["lng"].reshape(8, 128),
             g["lnb"].reshape(8, 128)]
    assert sum(p.shape[0] for p in parts) == SMALL_ROWS
    return jnp.concatenate(parts, axis=0)


def _unpack_small(s):
    out, r = {}, 0
    for key, rows, shape in (("loss", 8, (8, 128)), ("pre1", 8, (1, D_MODEL)), ("post", 16, (2, D_MODEL)),
                             ("sinks", 8, (8, 128)),
                             ("pool_scale", 4, (1, 512)), ("pad", 4, (4, 128)), ("pool_w", 512, (1, 4, 128, 128)),
                             ("dw", 248, (CONV_K, D_MODEL)), ("dwb", 8, (1, D_MODEL)), ("lng", 8, (1, D_MODEL)),
                             ("lnb", 8, (1, D_MODEL))):
        out[key] = s[r:r + rows].reshape(shape)
        r += rows
    return out


def kernel(x, pre_norm, post_norm, a_w_in, a_sinks, b_pool_w, b_pool_scale, ab_w_out, c_w_in, c_dw_w, c_dw_b, c_ln_g, c_ln_b, c_w_out, loss_target, m_pre_norm, m_post_norm, m_a_w_in, m_a_sinks, m_b_pool_w, m_b_pool_scale, m_ab_w_out, m_c_w_in, m_c_dw_w, m_c_dw_b, m_c_ln_g, m_c_ln_b, m_c_w_out, v_pre_norm, v_post_norm, v_a_w_in, v_a_sinks, v_b_pool_w, v_b_pool_scale, v_ab_w_out, v_c_w_in, v_c_dw_w, v_c_dw_b, v_c_ln_g, v_c_ln_b, v_c_w_out):
    ix, iy = lax.axis_index("x"), lax.axis_index("y")
    chip_cols = (2 * ix + iy) * 256

    pad8 = lambda v: jnp.pad(v, ((0, -v.shape[0] % 8), (0, 0)))
    vec_shard = jnp.concatenate([pad8(c_dw_w.reshape(CONV_K, 256)), pad8(c_dw_b), pad8(c_ln_g), pad8(c_ln_b),
                                 jnp.zeros((8, 256), F32)], axis=0)
    x0, target = x[0], loss_target[0]
    pre0, pre1 = pre_norm[0:1], pre_norm[1:2]
    post0, post1 = post_norm[0:1], post_norm[1:2]
    pool_w = b_pool_w[0]

    (wa_t,) = _run_comm(_Gather([a_w_in[0].T.astype(BF16)], halve=True), "gather_a_w_in")
    wa_t = wa_t.reshape(EVEN_IN, D_MODEL)
    proj0, (w_ab,) = _norm_matmul(x0, pre0, wa_t, "proj0_fwd", comm=_Gather([ab_w_out[0].astype(BF16)], halve=True))
    w_ab = w_ab.reshape(D_MODEL, D_MODEL)
    (mix0, y0, x1), (wc_t, w_c, vecs) = _layer0_fwd(
        proj0, a_sinks, pool_w, b_pool_scale, w_ab, x0, post0,
        comm=_Gather([c_w_in[0].T.astype(BF16), c_w_out[0].astype(BF16), vec_shard], halve=True))
    wc_t = wc_t.reshape(3 * D_MODEL, D_MODEL)
    w_c = w_c.reshape(D_MODEL, D_MODEL)
    vecs = vecs.reshape(4, 64, 256).transpose(1, 0, 2).reshape(64, D_MODEL)
    dw, dwb, lng, lnb = vecs[0:CONV_K], vecs[32:33], vecs[40:41], vecs[48:49]
    proj1, _ = _norm_matmul(x1, pre1, wc_t, "proj1_fwd")
    z1, cf1, y1, g2, loss = _layer1_fwd(proj1, dw, dwb, lng, lnb, w_c, x1, post1, target)

    pieces = lambda m: m.reshape(N_DEV, m.shape[0] // N_DEV, D_MODEL)
    dz1, d_wc, d_post1 = _post_bwd(g2, y1, post1, w_c, z1, "out1_bwd")
    (dproj1, d_dw, d_dwb, d_lng, d_lnb), (r_wc,) = _conv_bwd(proj1, cf1, dz1, dw, lng, lnb,
                                                             comm=_Scatter([pieces(d_wc)]))
    g1, d_wct, d_pre1 = _pre_bwd(dproj1, wc_t, x1, pre1, g2, "proj1_bwd")
    dmix0, d_wab, d_post0 = _post_bwd(g1, y0, post0, w_ab, mix0, "out0_bwd")
    (dproj0, d_sinks, d_pw, d_ps), (r_wct, r_wab) = _mix0_bwd(proj0, dmix0, a_sinks, pool_w, b_pool_scale,
                                                             comm=_Scatter([pieces(d_wct), pieces(d_wab)]))
    g = dict(loss=loss, pre1=d_pre1, post0=d_post0, post1=d_post1, sinks=d_sinks, pool_w=d_pw, pool_scale=d_ps,
             dw=d_dw, dwb=d_dwb, lng=d_lng, lnb=d_lnb)
    d_wat, (small8,) = _proj_dw(dproj0, x0, pre0, "proj0_dw", comm=_Gather([_pack_small(g)], halve=False))
    (gx, d_pre0), (_, s_wat, r_wat) = _proj_dx(dproj0, wa_t, x0, pre0, g1, "proj0_dx",
                                               comm=_PairScatter([pieces(d_wat)]))
    (pre0_8,) = _run_comm(_Gather([d_pre0.reshape(8, 128)], halve=False), "pre0_all_gather")

    ic = lax.axis_index("c")
    place = jnp.stack([4 * ix + 2 * iy + ic, ic, 2 * ix + iy]).astype(jnp.int32)
    parts = [_pair_parts(s_wat, r_wat), _direct_parts(pieces(d_wab), r_wab), _direct_parts(pieces(d_wct), r_wct),
             _direct_parts(pieces(d_wc), r_wc)]
    halves = [_piece_sum(p, place, f"grad_sum{t}") for t, p in enumerate(parts)]
    g_wa_t, g_wab, g_wc_t, g_wc = [h.reshape(2 * h.shape[1], D_MODEL) for h in _share_with_sibling(halves, "grad_share")]
    g_a_w_in = g_wa_t.T[None]
    g_c_w_in = g_wc_t.T[None]
    g_ab_w_out = g_wab[None]
    g_c_w_out = g_wc[None]

    s = _unpack_small(_sum8(small8, "small_sum"))
    layer = lax.broadcasted_iota(jnp.int32, (2, D_MODEL), 0)
    g_pre = jnp.where(layer == 0, _sum8(pre0_8, "pre0_sum").reshape(1, D_MODEL), s["pre1"])
    g_post = s["post"]
    g_sinks = s["sinks"][:, 0].reshape(1, 8)
    g_pool_w, g_pool_scale = s["pool_w"], s["pool_scale"]
    g_dw = lax.dynamic_slice_in_dim(s["dw"], chip_cols, 256, axis=1).reshape(1, CONV_K, 1, 256)
    g_dwb = lax.dynamic_slice_in_dim(s["dwb"], chip_cols, 256, axis=1)
    g_lng = lax.dynamic_slice_in_dim(s["lng"], chip_cols, 256, axis=1)
    g_lnb = lax.dynamic_slice_in_dim(s["lnb"], chip_cols, 256, axis=1)

    grads = [g_pre, g_post, g_a_w_in, g_sinks, g_pool_w, g_pool_scale, g_ab_w_out, g_c_w_in, g_dw, g_dwb, g_lng, g_lnb,
             g_c_w_out]
    weights = [pre_norm, post_norm, a_w_in, a_sinks, b_pool_w, b_pool_scale, ab_w_out, c_w_in, c_dw_w, c_dw_b, c_ln_g,
               c_ln_b, c_w_out]
    moms = [m_pre_norm, m_post_norm, m_a_w_in, m_a_sinks, m_b_pool_w, m_b_pool_scale, m_ab_w_out, m_c_w_in, m_c_dw_w,
            m_c_dw_b, m_c_ln_g, m_c_ln_b, m_c_w_out]
    vars_ = [v_pre_norm, v_post_norm, v_a_w_in, v_a_sinks, v_b_pool_w, v_b_pool_scale, v_ab_w_out, v_c_w_in, v_c_dw_w,
             v_c_dw_b, v_c_ln_g, v_c_ln_b, v_c_w_out]
    deltas, new_m, new_v = [], [], []
    for k, (w, gr, m, v) in enumerate(zip(weights, grads, moms, vars_)):
        d, nm, nv = _adamw(w, gr, m, v, f"adamw{k}")
        deltas.append(d)
        new_m.append(nm)
        new_v.append(nv)
    return (s["loss"][0, 0], gx[None], *grads, *deltas, *new_m, *new_v)
```

```python
import functools

import jax
import jax.numpy as jnp
from jax import lax
from jax.experimental import pallas as pl
from jax.experimental.pallas import tpu as pltpu

F32 = jnp.float32
BF16 = jnp.bfloat16

D_MODEL = 1024
EPS = 1e-6
NEG = -1e30
HEAD_DIM = 64
GROUP = 4
KV_HEADS = 2
BLOCK = 128
EVEN_IN = 2304
COL_Q, COL_K, COL_GA, COL_U, COL_GB = 0, 512, 768, 1280, 1792
POOL_GROUPS = 4
POOL_GC = 128
POOL_HALO = 16
CONV_K = 31
CONV_HALO = 32
N_DEV = 8

ADAM_LR = 0.001
ADAM_B1 = 0.9
ADAM_B2 = 0.999
ADAM_EPS = 1e-08
ADAM_WD = 0.01
ADAM_STEP = 10

VMEM_LIMIT_BYTES = 56 * 1024 * 1024

NT = (((1,), (1,)), ((), ()))
TN = (((0,), (0,)), ((), ()))
MESH_ID = pl.DeviceIdType.MESH


def _params(*sem):
    return pltpu.CompilerParams(dimension_semantics=sem, vmem_limit_bytes=VMEM_LIMIT_BYTES)


def _const_spec(shape):
    nd = len(shape)
    return pl.BlockSpec(shape, lambda *_: (0,) * nd, pipeline_mode=pl.Buffered(1))


def _sigmoid(v):
    return 0.5 * jnp.tanh(0.5 * v) + 0.5


def _silu(v):
    return v * _sigmoid(v)


def _silu_and_grad(v):
    s = _sigmoid(v)
    return v * s, s * (1.0 + v * (1.0 - s))


ANY = pl.BlockSpec(memory_space=pl.ANY)


def _place():
    x, y, c = lax.axis_index("x"), lax.axis_index("y"), lax.axis_index("c")
    chips = [(1 - x, y), (x, 1 - y), (1 - x, 1 - y)]
    return x, y, c, chips


class _Gather:
    def __init__(self, blocks, halve):
        self.ins = list(blocks)
        self.halve = halve
        self.n = n = len(blocks)
        self.shapes = [((b.shape[0] // 2) if halve else b.shape[0], b.shape[1]) for b in blocks]
        self.out_shape = [jax.ShapeDtypeStruct((N_DEV, r, cols), b.dtype) for (r, cols), b in zip(self.shapes, blocks)]
        self.scratch = [pltpu.SemaphoreType.DMA((7 * n,)), pltpu.SemaphoreType.DMA((7 * n,)),
                        pltpu.SemaphoreType.DMA((n,))]

    def _copies(self, ins, outs, sems):
        send_sems, recv_sems, local_sems = sems
        x, y, c, chips = _place()
        me, sibling = (x, y, c), (x, y, 1 - c)

        def piece(t, px, py, pc):
            return outs[t].at[4 * px + 2 * py + pc]

        def own(t):
            return ins[t].at[pl.ds(c * self.shapes[t][0], self.shapes[t][0])] if self.halve else ins[t]

        def copy(t, k, block, to, src=None):
            return pltpu.make_async_remote_copy(
                src_ref=piece(t, *block) if src is None else src, dst_ref=piece(t, *block),
                send_sem=send_sems.at[7 * t + k], recv_sem=recv_sems.at[7 * t + k],
                device_id=to, device_id_type=MESH_ID)

        rng = range(self.n)
        return dict(
            mine=[pltpu.make_async_copy(own(t), piece(t, *me), local_sems.at[t]) for t in rng],
            first=[copy(t, 0, me, sibling, src=own(t)) for t in rng]
            + [copy(t, 1 + j, me, (*chip, c), src=own(t)) for t in rng for j, chip in enumerate(chips)],
            landed=[copy(t, 1 + j, (*chip, c), me) for j, chip in enumerate(chips) for t in rng],
            passed=[copy(t, 4 + j, (*chip, c), sibling) for j, chip in enumerate(chips) for t in rng],
            from_sibling=[copy(t, 0, sibling, me) for t in rng]
            + [copy(t, 4 + j, (*chip, 1 - c), me) for t in rng for j, chip in enumerate(chips)])

    def start(self, ins, outs, sems):
        d = self._copies(ins, outs, sems)
        for cp in d["mine"] + d["first"]:
            cp.start()

    def middle(self, ins, outs, sems):
        d = self._copies(ins, outs, sems)
        for got, fwd in zip(d["landed"], d["passed"]):
            got.wait_recv()
            fwd.start()

    def finish(self, ins, outs, sems):
        d = self._copies(ins, outs, sems)
        for cp in d["from_sibling"]:
            cp.wait_recv()
        for cp in d["first"] + d["passed"]:
            cp.wait_send()
        for cp in d["mine"]:
            cp.wait()


class _Scatter:
    def __init__(self, tensors):
        self.ins = list(tensors)
        self.n = n = len(tensors)
        self.out_shape = [jax.ShapeDtypeStruct(t.shape, t.dtype) for t in tensors]
        self.scratch = [pltpu.SemaphoreType.DMA((7 * n,)), pltpu.SemaphoreType.DMA((7 * n,))]

    def _copies(self, ins, outs, sems):
        send_sems, recv_sems = sems
        x, y, c, _ = _place()
        me = 4 * x + 2 * y + c
        sends, recvs = [], []
        for t in range(self.n):
            for m in range(1, N_DEV):
                px, py, pc = x ^ (m >> 2), y ^ ((m >> 1) & 1), c ^ (m & 1)
                q = 4 * px + 2 * py + pc
                sems_k = dict(send_sem=send_sems.at[7 * t + m - 1], recv_sem=recv_sems.at[7 * t + m - 1],
                              device_id=(px, py, pc), device_id_type=MESH_ID)
                sends.append(pltpu.make_async_remote_copy(src_ref=ins[t].at[q], dst_ref=outs[t].at[me], **sems_k))
                recvs.append(pltpu.make_async_remote_copy(src_ref=ins[t].at[me], dst_ref=outs[t].at[q], **sems_k))
        return sends, recvs

    def start(self, ins, outs, sems):
        for cp in self._copies(ins, outs, sems)[0]:
            cp.start()

    def middle(self, ins, outs, sems):
        pass

    def finish(self, ins, outs, sems):
        sends, recvs = self._copies(ins, outs, sems)
        for cp in recvs:
            cp.wait_recv()
        for cp in sends:
            cp.wait_send()


class _PairScatter:
    middle_step = 2

    def __init__(self, tensors):
        self.ins = list(tensors)
        self.n = n = len(tensors)
        r = tensors[0].shape[1]
        assert all(t.shape == (N_DEV, r, D_MODEL) and t.dtype == BF16 for t in tensors)
        quarter = lambda k: [jax.ShapeDtypeStruct((k, r, D_MODEL), BF16) for _ in tensors]
        self.out_shape = quarter(4) + quarter(4) + quarter(3)
        self.scratch = [pltpu.SemaphoreType.DMA((4 * n,)), pltpu.SemaphoreType.DMA((4 * n,)),
                        pltpu.SemaphoreType.DMA((3 * n,)), pltpu.SemaphoreType.DMA((3 * n,)),
                        pltpu.SemaphoreType.DMA((8,)), pltpu.VMEM((4, r, D_MODEL), BF16), pltpu.VMEM((4, r, D_MODEL), BF16)]

    def _copies(self, ins, outs, sems):
        n = self.n
        pair, sums, recv = outs[:n], outs[n:2 * n], outs[2 * n:]
        x, y, c, chips = _place()
        swap = [pltpu.make_async_remote_copy(
            src_ref=ins[t].at[2 * b + 1 - c], dst_ref=pair[t].at[b], send_sem=sems[0].at[4 * t + b],
            recv_sem=sems[1].at[4 * t + b], device_id=(x, y, 1 - c), device_id_type=MESH_ID)
            for t in range(n) for b in range(4)]
        to_owner = [pltpu.make_async_remote_copy(
            src_ref=sums[t].at[2 * cx + cy], dst_ref=recv[t].at[j], send_sem=sems[2].at[3 * t + j],
            recv_sem=sems[3].at[3 * t + j], device_id=(cx, cy, c), device_id_type=MESH_ID)
            for t in range(n) for j, (cx, cy) in enumerate(chips)]
        return swap, to_owner

    def start(self, ins, outs, sems):
        for cp in self._copies(ins, outs, sems)[0]:
            cp.start()

    def middle(self, ins, outs, sems):
        n = self.n
        pair, sums = outs[:n], outs[n:2 * n]
        local, mine_ref, theirs_ref = sems[4], sems[5], sems[6]
        c = lax.axis_index("c")
        swap, to_owner = self._copies(ins, outs, sems)
        for cp in swap:
            cp.wait_recv()
        for t in range(n):
            loads = [pltpu.make_async_copy(ins[t].at[2 * b + c], mine_ref.at[b], local.at[b]) for b in range(4)]
            loads += [pltpu.make_async_copy(pair[t].at[b], theirs_ref.at[b], local.at[4 + b]) for b in range(4)]
            for cp in loads:
                cp.start()
            for cp in loads:
                cp.wait()
            mine_ref[...] = (mine_ref[...].astype(F32) + theirs_ref[...].astype(F32)).astype(BF16)
            stores = [pltpu.make_async_copy(mine_ref.at[b], sums[t].at[b], local.at[b]) for b in range(4)]
            for cp in stores:
                cp.start()
            for cp in stores:
                cp.wait()
        for cp in to_owner:
            cp.start()
        for cp in swap:
            cp.wait_send()

    def finish(self, ins, outs, sems):
        to_owner = self._copies(ins, outs, sems)[1]
        for cp in to_owner:
            cp.wait_recv()
        for cp in to_owner:
            cp.wait_send()


class _Comms:
    def __init__(self, *comms):
        self.comms = comms
        self.ins = [a for c in comms for a in c.ins]
        self.out_shape = [s for c in comms for s in c.out_shape]
        self.scratch = [s for c in comms for s in c.scratch]

    def _each(self, phase, ins, outs, sems):
        i = o = s = 0
        for c in self.comms:
            ni, no, ns = len(c.ins), len(c.out_shape), len(c.scratch)
            getattr(c, phase)(ins[i:i + ni], outs[o:o + no], sems[s:s + ns])
            i, o, s = i + ni, o + no, s + ns

    def start(self, ins, outs, sems):
        self._each("start", ins, outs, sems)

    def middle(self, ins, outs, sems):
        self._each("middle", ins, outs, sems)

    def finish(self, ins, outs, sems):
        self._each("finish", ins, outs, sems)


def _run_comm(comm, name):
    n = len(comm.ins)

    def body(*refs):
        parts = refs[:n], refs[n:2 * n], refs[2 * n:]
        comm.start(*parts)
        comm.middle(*parts)
        comm.finish(*parts)

    return pl.pallas_call(body, name=name, out_shape=comm.out_shape, in_specs=[ANY] * n, out_specs=[ANY] * n,
                          scratch_shapes=comm.scratch)(*comm.ins)


def _fused_call(body, comm, args, *, name, grid, out_shape, in_specs, out_specs, scratch_shapes=(), params):
    single = not isinstance(out_shape, (list, tuple))
    out_shape = [out_shape] if single else list(out_shape)
    out_specs = [out_specs] if single else list(out_specs)
    if comm is None:
        res = pl.pallas_call(body, name=name, grid=grid, out_shape=out_shape, in_specs=in_specs, out_specs=out_specs,
                             scratch_shapes=list(scratch_shapes), compiler_params=params)(*args)
        return (res[0] if single else res), []
    n_in, n_out, n_scr = len(in_specs), len(out_shape), len(scratch_shapes)
    c_in, c_out = len(comm.ins), len(comm.out_shape)
    steps = grid[0]

    def fused(*refs):
        pos = 0
        groups = []
        for size in (n_in, c_in, n_out, c_out, n_scr, len(comm.scratch)):
            groups.append(refs[pos:pos + size])
            pos += size
        ins, c_ins, outs, c_outs, scr, c_sems = groups
        i = pl.program_id(0)

        @pl.when(i == 0)
        def _():
            comm.start(c_ins, c_outs, c_sems)

        @pl.when(i == min(getattr(comm, "middle_step", steps // 2), steps - 1))
        def _():
            comm.middle(c_ins, c_outs, c_sems)

        body(*ins, *outs, *scr)

        @pl.when(i == steps - 1)
        def _():
            comm.finish(c_ins, c_outs, c_sems)

    res = pl.pallas_call(
        fused, name=name, grid=grid, out_shape=out_shape + list(comm.out_shape),
        in_specs=list(in_specs) + [ANY] * c_in, out_specs=out_specs + [ANY] * c_out,
        scratch_shapes=list(scratch_shapes) + list(comm.scratch), compiler_params=params)(*args, *comm.ins)
    main = res[:n_out]
    return (main[0] if single else main), list(res[n_out:])


def _norm_matmul(x, gain, wt, name, comm=None, tm=512):
    t, n = x.shape[0], wt.shape[0]

    def body(x_ref, g_ref, wt_ref, o_ref):
        xv = x_ref[...]
        r = lax.rsqrt(jnp.mean(xv * xv, axis=-1, keepdims=True) + EPS)
        h = (xv * r * g_ref[...]).astype(BF16)
        o_ref[...] = lax.dot_general(h, wt_ref[...], NT, preferred_element_type=F32)

    return _fused_call(
        body, comm, (x, gain, wt), name=name, grid=(t // tm,),
        out_shape=jax.ShapeDtypeStruct((t, n), F32),
        in_specs=[pl.BlockSpec((tm, D_MODEL), lambda i: (i, 0)), _const_spec((1, D_MODEL)), _const_spec((n, D_MODEL))],
        out_specs=pl.BlockSpec((tm, n), lambda i: (i, 0)),
        params=_params("arbitrary"))


def _project_out(a, w_ref, x_ref, p_ref, y_ref):
    y = jnp.dot(a, w_ref[...], preferred_element_type=F32)
    y_ref[...] = y
    ry = lax.rsqrt(jnp.mean(y * y, axis=-1, keepdims=True) + EPS)
    return x_ref[...] + (y * ry) * p_ref[...]


def _post_bwd_rows(g, y, a, n_own, first, last, p_ref, w_ref, dw_ref, dw16_ref, dp_ref):
    @pl.when(first)
    def _():
        dw_ref[...] = jnp.zeros_like(dw_ref)
        dp_ref[...] = jnp.zeros_like(dp_ref)

    ry = lax.rsqrt(jnp.mean(y * y, axis=-1, keepdims=True) + EPS)
    nv = y * ry
    dp_ref[...] += jnp.sum((g * nv)[0:n_own], axis=0, keepdims=True)
    dn = g * p_ref[...]
    dy = (ry * (dn - nv * jnp.mean(dn * nv, axis=-1, keepdims=True))).astype(BF16)
    dw_ref[...] += lax.dot_general(a, dy[0:n_own], TN, preferred_element_type=F32)

    @pl.when(last)
    def _():
        dw16_ref[...] = dw_ref[...].astype(BF16)

    return lax.dot_general(dy, w_ref[...], NT, preferred_element_type=F32)


def _pre_bwd(dproj, wt, x_in, pre, g, name, comm=None, tm=512):
    t, n = dproj.shape
    steps = t // tm

    def body(dp_ref, wt_ref, x_ref, pre_ref, g_ref, dx_ref, dwt16_ref, dpre_ref, dwt_ref):
        @pl.when(pl.program_id(0) == 0)
        def _():
            dwt_ref[...] = jnp.zeros_like(dwt_ref)
            dpre_ref[...] = jnp.zeros_like(dpre_ref)

        dpv = dp_ref[...]
        dh = jnp.dot(dpv, wt_ref[...], preferred_element_type=F32)
        xv = x_ref[...]
        r = lax.rsqrt(jnp.mean(xv * xv, axis=-1, keepdims=True) + EPS)
        xn = xv * r
        pv = pre_ref[...]
        dpre_ref[...] += jnp.sum(dh * xn, axis=0, keepdims=True)
        dxn = dh * pv
        dx_ref[...] = g_ref[...] + r * (dxn - xn * jnp.mean(dxn * xn, axis=-1, keepdims=True))
        h = (xn * pv).astype(BF16)
        dwt_ref[...] += lax.dot_general(dpv, h, TN, preferred_element_type=F32)

        @pl.when(pl.program_id(0) == steps - 1)
        def _():
            dwt16_ref[...] = dwt_ref[...].astype(BF16)

    row = pl.BlockSpec((tm, D_MODEL), lambda i: (i, 0))
    return _fused_call(
        body, comm, (dproj, wt, x_in, pre, g), name=name, grid=(steps,),
        out_shape=[jax.ShapeDtypeStruct((t, D_MODEL), F32), jax.ShapeDtypeStruct((n, D_MODEL), BF16),
                   jax.ShapeDtypeStruct((1, D_MODEL), F32)],
        in_specs=[pl.BlockSpec((tm, n), lambda i: (i, 0)), _const_spec((n, D_MODEL)), row, _const_spec((1, D_MODEL)), row],
        out_specs=[row, _const_spec((n, D_MODEL)), pl.BlockSpec((1, D_MODEL), lambda i: (0, 0))],
        scratch_shapes=[pltpu.VMEM((n, D_MODEL), F32)],
        params=_params("arbitrary"))


def _proj_dw(dproj, x_in, pre, name, comm=None, tm=1024):
    t, n = dproj.shape
    steps = t // tm

    def body(dp_ref, x_ref, pre_ref, dwt16_ref, dwt_ref):
        @pl.when(pl.program_id(0) == 0)
        def _():
            dwt_ref[...] = jnp.zeros_like(dwt_ref)

        xv = x_ref[...]
        r = lax.rsqrt(jnp.mean(xv * xv, axis=-1, keepdims=True) + EPS)
        h = (xv * r * pre_ref[...]).astype(BF16)
        dwt_ref[...] += lax.dot_general(dp_ref[...], h, TN, preferred_element_type=F32)

        @pl.when(pl.program_id(0) == steps - 1)
        def _():
            dwt16_ref[...] = dwt_ref[...].astype(BF16)

    return _fused_call(
        body, comm, (dproj, x_in, pre), name=name, grid=(steps,),
        out_shape=jax.ShapeDtypeStruct((n, D_MODEL), BF16),
        in_specs=[pl.BlockSpec((tm, n), lambda i: (i, 0)), pl.BlockSpec((tm, D_MODEL), lambda i: (i, 0)),
                  _const_spec((1, D_MODEL))],
        out_specs=pl.BlockSpec((n, D_MODEL), lambda i: (0, 0)),
        scratch_shapes=[pltpu.VMEM((n, D_MODEL), F32)],
        params=_params("arbitrary"))


def _proj_dx(dproj, wt, x_in, pre, g, name, comm=None, tm=512):
    t, n = dproj.shape

    def body(dp_ref, wt_ref, x_ref, pre_ref, g_ref, dx_ref, dpre_ref):
        @pl.when(pl.program_id(0) == 0)
        def _():
            dpre_ref[...] = jnp.zeros_like(dpre_ref)

        dh = jnp.dot(dp_ref[...], wt_ref[...], preferred_element_type=F32)
        xv = x_ref[...]
        r = lax.rsqrt(jnp.mean(xv * xv, axis=-1, keepdims=True) + EPS)
        xn = xv * r
        dpre_ref[...] += jnp.sum(dh * xn, axis=0, keepdims=True)
        dxn = dh * pre_ref[...]
        dx_ref[...] = g_ref[...] + r * (dxn - xn * jnp.mean(dxn * xn, axis=-1, keepdims=True))

    row = pl.BlockSpec((tm, D_MODEL), lambda i: (i, 0))
    return _fused_call(
        body, comm, (dproj, wt, x_in, pre, g), name=name, grid=(t // tm,),
        out_shape=[jax.ShapeDtypeStruct((t, D_MODEL), F32), jax.ShapeDtypeStruct((1, D_MODEL), F32)],
        in_specs=[pl.BlockSpec((tm, n), lambda i: (i, 0)), _const_spec((n, D_MODEL)), row, _const_spec((1, D_MODEL)), row],
        out_specs=[row, pl.BlockSpec((1, D_MODEL), lambda i: (0, 0))],
        params=_params("arbitrary"))


def _group_masks():
    lane = lax.broadcasted_iota(jnp.int32, (1, GROUP * HEAD_DIM), 1)
    return [(lane // HEAD_DIM == g).astype(F32) for g in range(GROUP)]


def _stack_groups(v, masks):
    return jnp.concatenate([v * m for m in masks], axis=0)


def _unstack_groups(v, masks):
    out = v[0:BLOCK] * masks[0]
    for g in range(1, GROUP):
        out = out + v[g * BLOCK:(g + 1) * BLOCK] * masks[g]
    return out


def _repeat_head(kv2, kvh):
    first = lax.broadcasted_iota(jnp.int32, kv2.shape, 1) < HEAD_DIM
    rolled = pltpu.roll(kv2, HEAD_DIM, 1)
    one = jnp.where(first, kv2, rolled) if kvh == 0 else jnp.where(first, rolled, kv2)
    return jnp.concatenate([one, one], axis=1)


def _fold_head(v4):
    a = v4[:, 0:128] + v4[:, 128:256]
    return a + pltpu.roll(a, HEAD_DIM, 1)


ATTN_CONSTS = [pltpu.VMEM((KV_HEADS, GROUP * BLOCK, 2 * BLOCK), F32)]


def _fill_attn_bias(bias_ref):
    row = lax.broadcasted_iota(jnp.int32, (GROUP * BLOCK, 2 * BLOCK), 0)
    col = lax.broadcasted_iota(jnp.int32, (GROUP * BLOCK, 2 * BLOCK), 1)
    dist = (row % BLOCK) + BLOCK - col
    band = (dist >= 0) & (dist < BLOCK)
    rb = lax.broadcasted_iota(jnp.int32, (GROUP * BLOCK, 1), 0) // BLOCK
    for kvh in range(KV_HEADS):
        slope = jnp.zeros((GROUP * BLOCK, 1), F32)
        for g in range(GROUP):
            slope = jnp.where(rb == g, 2.0 ** (-(kvh * GROUP + g + 1)), slope)
        bias_ref[kvh] = jnp.where(band, -slope * dist.astype(F32), NEG)


def _row_sinks(kvh, sink_ref):
    rb = lax.broadcasted_iota(jnp.int32, (GROUP * BLOCK, 1), 0) // BLOCK
    sink = jnp.zeros((GROUP * BLOCK, 1), F32)
    for g in range(GROUP):
        sink = jnp.where(rb == g, sink_ref[0, kvh * GROUP + g], sink)
    return sink


def _attn_probs(qk, k4, bias, sink, no_past, masks):
    qs = _stack_groups(qk, masks).astype(BF16)
    s = lax.dot_general(qs, k4, NT, preferred_element_type=F32) * (HEAD_DIM ** -0.5) + bias
    s = jnp.concatenate([jnp.where(no_past, NEG, s[:, 0:BLOCK]), s[:, BLOCK:]], axis=1)
    mx = jnp.maximum(jnp.max(s, axis=-1, keepdims=True), sink)
    e = jnp.exp(s - mx)
    es = jnp.exp(sink - mx)
    inv = 1.0 / (jnp.sum(e, axis=-1, keepdims=True) + es)
    return qs, e * inv, es * inv


def _pool_forward(u_ext, g, t0):
    n = u_ext.shape[0] - POOL_HALO
    s = u_ext
    for step in range(g + 1):
        s = s + pltpu.roll(s, 1 << step, 0)
    w = 2 << g
    t = t0 + lax.broadcasted_iota(jnp.int32, (n, 1), 0)
    cnt = jnp.minimum(t + 1, w).astype(F32)
    return s[POOL_HALO:] / cnt - u_ext[POOL_HALO:]


def _layer0_fwd(proj, sinks, pool_w, pool_scale, w_out, x_in, post, comm=None, tq=512):
    t = proj.shape[0]
    nblk = tq // BLOCK

    def body(main_ref, halo_ref, sink_ref, pw_ref, ps_ref, w_ref, x_ref, p_ref, o_ref, y_ref, xo_ref, kv_ref, bias_ref):
        i = pl.program_id(0)
        t0 = i * tq
        masks = _group_masks()

        @pl.when(i == 0)
        def _():
            _fill_attn_bias(bias_ref)

        kv_ref[0:BLOCK, :] = halo_ref[:, COL_K:COL_K + 256]
        kv_ref[BLOCK:, :] = main_ref[:, COL_K:COL_K + 256]

        def block(jb, carry):
            r0 = pl.multiple_of(jb * BLOCK, BLOCK)
            no_past = t0 + r0 == 0
            q = main_ref[pl.ds(r0, BLOCK), COL_Q:COL_Q + 512]
            ga = main_ref[pl.ds(r0, BLOCK), COL_GA:COL_GA + 512]
            kk = kv_ref[pl.ds(r0, 2 * BLOCK), 0:128]
            vv = kv_ref[pl.ds(r0, 2 * BLOCK), 128:256]
            outs = []
            for kvh in range(KV_HEADS):
                k4 = _repeat_head(kk, kvh).astype(BF16)
                v4 = _repeat_head(vv, kvh).astype(BF16)
                _, p, _ = _attn_probs(q[:, kvh * 256:(kvh + 1) * 256], k4, bias_ref[kvh], _row_sinks(kvh, sink_ref), no_past, masks)
                pv = jnp.dot(p.astype(BF16), v4, preferred_element_type=F32)
                outs.append(_unstack_groups(pv, masks))
            attn = jnp.concatenate(outs, axis=1)
            o_ref[pl.ds(r0, BLOCK), 0:512] = (attn * _silu(ga)).astype(BF16)
            return carry

        lax.fori_loop(0, nblk, block, 0, unroll=2)

        for g in range(POOL_GROUPS):
            cu = COL_U + g * POOL_GC
            cg = COL_GB + g * POOL_GC
            halo_u = jnp.where(i == 0, 0.0, halo_ref[BLOCK - POOL_HALO:BLOCK, cu:cu + POOL_GC])
            u_ext = jnp.concatenate([halo_u, main_ref[:, cu:cu + POOL_GC]], axis=0)
            pooled = _pool_forward(u_ext, g, t0)
            y = jnp.dot(pooled.astype(BF16), pw_ref[g].astype(BF16), preferred_element_type=F32)
            y = y * ps_ref[:, g * POOL_GC:(g + 1) * POOL_GC]
            o_ref[:, 512 + g * POOL_GC:512 + (g + 1) * POOL_GC] = (y * _silu(main_ref[:, cg:cg + POOL_GC])).astype(BF16)

        xo_ref[...] = _project_out(o_ref[...], w_ref, x_ref, p_ref, y_ref)

    row = pl.BlockSpec((tq, D_MODEL), lambda i: (i, 0))
    return _fused_call(
        body, comm, (proj, proj, sinks, pool_w, pool_scale, w_out, x_in, post), name="layer0_fwd", grid=(t // tq,),
        out_shape=[jax.ShapeDtypeStruct((t, D_MODEL), BF16), jax.ShapeDtypeStruct((t, D_MODEL), F32),
                   jax.ShapeDtypeStruct((t, D_MODEL), F32)],
        in_specs=[pl.BlockSpec((tq, EVEN_IN), lambda i: (i, 0)),
                  pl.BlockSpec((BLOCK, EVEN_IN), lambda i: (jnp.maximum(i * nblk - 1, 0), 0)),
                  pl.BlockSpec(memory_space=pltpu.SMEM),
                  _const_spec((POOL_GROUPS, POOL_GC, POOL_GC)), _const_spec((1, 512)),
                  _const_spec((D_MODEL, D_MODEL)), row, _const_spec((1, D_MODEL))],
        out_specs=[row, row, row],
        scratch_shapes=[pltpu.VMEM((tq + BLOCK, 256), F32)] + ATTN_CONSTS,
        params=_params("arbitrary"))


def _layer0_bwd(proj, gy, y, mix, w_out, post, sinks, pool_w, pool_scale, comm=None, tq=512):
    t = proj.shape[0]
    nt = t // tq
    nblk = tq // BLOCK

    def body(main_ref, halo_ref, next_ref, gy_ref, gyn_ref, y_ref, yn_ref, mix_ref, wo_ref, po_ref,
             sink_ref, pw_ref, ps_ref,
             o_ref, dsk_ref, dpw_ref, dps_ref, dwo16_ref, dpo_ref,
             kv_ref, dkv_ref, carry_ref, bias_ref, dwo_ref, dmix_ref):
        i = pl.program_id(0)
        ii = nt - 1 - i
        t0 = ii * tq
        masks = _group_masks()

        @pl.when(i == 0)
        def _():
            _fill_attn_bias(bias_ref)
            dsk_ref[...] = jnp.zeros_like(dsk_ref)
            dpw_ref[...] = jnp.zeros_like(dpw_ref)
            dps_ref[...] = jnp.zeros_like(dps_ref)
            carry_ref[...] = jnp.zeros_like(carry_ref)

        dmix_ref[...] = _post_bwd_rows(jnp.concatenate([gy_ref[...], gyn_ref[...]], axis=0),
                                       jnp.concatenate([y_ref[...], yn_ref[...]], axis=0), mix_ref[...], tq,
                                       i == 0, i == nt - 1, po_ref, wo_ref, dwo_ref, dwo16_ref, dpo_ref)
        dm_ref = dmix_ref.at[pl.ds(0, tq)]
        dmn_ref = dmix_ref.at[pl.ds(tq, POOL_HALO)]

        kv_ref[0:BLOCK, :] = halo_ref[:, COL_K:COL_K + 256]
        kv_ref[BLOCK:, :] = main_ref[:, COL_K:COL_K + 256]
        dkv_ref[0:tq, :] = jnp.zeros((tq, 256), F32)
        dkv_ref[tq:, :] = carry_ref[...]

        def block(jb, carry):
            r0 = pl.multiple_of(jb * BLOCK, BLOCK)
            no_past = t0 + r0 == 0
            q = main_ref[pl.ds(r0, BLOCK), COL_Q:COL_Q + 512]
            ga = main_ref[pl.ds(r0, BLOCK), COL_GA:COL_GA + 512]
            dya = dm_ref[pl.ds(r0, BLOCK), 0:512]
            kk = kv_ref[pl.ds(r0, 2 * BLOCK), 0:128]
            vv = kv_ref[pl.ds(r0, 2 * BLOCK), 128:256]
            silu_ga, dsilu_ga = _silu_and_grad(ga)
            do = dya * silu_ga
            first = lax.broadcasted_iota(jnp.int32, (2 * BLOCK, 128), 1) < HEAD_DIM
            attn, dq, dk, dv = [], [], [], []
            for kvh in range(KV_HEADS):
                k4 = _repeat_head(kk, kvh).astype(BF16)
                v4 = _repeat_head(vv, kvh).astype(BF16)
                qs, p, ps = _attn_probs(q[:, kvh * 256:(kvh + 1) * 256], k4, bias_ref[kvh], _row_sinks(kvh, sink_ref), no_past, masks)
                pb = p.astype(BF16)
                o_k = _unstack_groups(jnp.dot(pb, v4, preferred_element_type=F32), masks)
                do_k = do[:, kvh * 256:(kvh + 1) * 256]
                dos = _stack_groups(do_k, masks).astype(BF16)
                prod = do_k * o_k
                delta = jnp.concatenate([jnp.sum(prod * m, axis=-1, keepdims=True) for m in masks], axis=0)
                dp = lax.dot_general(dos, v4, NT, preferred_element_type=F32)
                ds = (p * (dp - delta)).astype(BF16)
                sink_term = ps * delta
                for g in range(GROUP):
                    h = kvh * GROUP + g
                    dsk_ref[h:h + 1, :] -= jnp.sum(sink_term[g * BLOCK:(g + 1) * BLOCK], keepdims=True)
                scale = HEAD_DIM ** -0.5
                dq.append(_unstack_groups(jnp.dot(ds, k4, preferred_element_type=F32), masks) * scale)
                dk.append(_fold_head(lax.dot_general(ds, qs, TN, preferred_element_type=F32)) * scale)
                dv.append(_fold_head(lax.dot_general(pb, dos, TN, preferred_element_type=F32)))
                attn.append(o_k)
            o_ref[pl.ds(r0, BLOCK), COL_Q:COL_Q + 512] = jnp.concatenate(dq, axis=1).astype(BF16)
            o_all = jnp.concatenate(attn, axis=1)
            o_ref[pl.ds(r0, BLOCK), COL_GA:COL_GA + 512] = (dya * o_all * dsilu_ga).astype(BF16)
            dkv = jnp.concatenate([jnp.where(first, dk[0], dk[1]), jnp.where(first, dv[0], dv[1])], axis=1)
            dkv_ref[pl.ds(r0, 2 * BLOCK), :] += dkv
            return carry

        lax.fori_loop(0, nblk, block, 0, unroll=2)
        carry_ref[...] = dkv_ref[0:BLOCK, :]
        o_ref[:, COL_K:COL_K + 256] = dkv_ref[BLOCK:, :].astype(BF16)

        last = ii == nt - 1
        for g in range(POOL_GROUPS):
            cu = COL_U + g * POOL_GC
            cg = COL_GB + g * POOL_GC
            cm = 512 + g * POOL_GC
            pw = pw_ref[g].astype(BF16)
            sc = ps_ref[:, g * POOL_GC:(g + 1) * POOL_GC]
            halo_u = jnp.where(ii == 0, 0.0, halo_ref[BLOCK - POOL_HALO:BLOCK, cu:cu + POOL_GC])
            u_ext = jnp.concatenate([halo_u, main_ref[:, cu:cu + POOL_GC]], axis=0)
            pooled = _pool_forward(u_ext, g, t0).astype(BF16)
            y_raw = jnp.dot(pooled, pw, preferred_element_type=F32)
            gb = main_ref[:, cg:cg + POOL_GC]
            dyb = dm_ref[:, cm:cm + POOL_GC]
            silu_gb, dsilu_gb = _silu_and_grad(gb)
            dypool = dyb * silu_gb
            dps_ref[:, g * POOL_GC:(g + 1) * POOL_GC] += jnp.sum(dypool * y_raw, axis=0, keepdims=True)
            o_ref[:, cg:cg + POOL_GC] = (dyb * (y_raw * sc) * dsilu_gb).astype(BF16)
            dyraw = dypool * sc
            dyraw_n = jnp.where(last, 0.0, dmn_ref[:, cm:cm + POOL_GC] * _silu(next_ref[:, cg:cg + POOL_GC]) * sc)
            dpw_ref[g * POOL_GC:(g + 1) * POOL_GC, :] += lax.dot_general(pooled, dyraw.astype(BF16), TN,
                                                                         preferred_element_type=F32)
            dyraw_ext = jnp.concatenate([dyraw, dyraw_n], axis=0).astype(BF16)
            dpooled = lax.dot_general(dyraw_ext, pw, NT, preferred_element_type=F32)
            w = 2 << g
            tt = t0 + lax.broadcasted_iota(jnp.int32, (tq + POOL_HALO, 1), 0)
            s = dpooled / jnp.minimum(tt + 1, w).astype(F32)
            for step in range(g + 1):
                s = s + pltpu.roll(s, tq + POOL_HALO - (1 << step), 0)
            o_ref[:, cu:cu + POOL_GC] = (s[0:tq] - dpooled[0:tq]).astype(BF16)

    rev = lambda i: nt - 1 - i
    nxt = lambda i: (jnp.minimum((rev(i) + 1) * (tq // POOL_HALO), t // POOL_HALO - 1), 0)
    row = pl.BlockSpec((tq, D_MODEL), lambda i: (rev(i), 0))
    nxt_row = pl.BlockSpec((POOL_HALO, D_MODEL), nxt)
    square = _const_spec((D_MODEL, D_MODEL))
    return _fused_call(
        body, comm, (proj, proj, proj, gy, gy, y, y, mix, w_out, post, sinks, pool_w, pool_scale),
        name="layer0_bwd", grid=(nt,),
        out_shape=[jax.ShapeDtypeStruct((t, EVEN_IN), BF16), jax.ShapeDtypeStruct((8, 128), F32),
                   jax.ShapeDtypeStruct((POOL_GROUPS * POOL_GC, POOL_GC), F32), jax.ShapeDtypeStruct((1, 512), F32),
                   jax.ShapeDtypeStruct((D_MODEL, D_MODEL), BF16), jax.ShapeDtypeStruct((1, D_MODEL), F32)],
        in_specs=[pl.BlockSpec((tq, EVEN_IN), lambda i: (rev(i), 0)),
                  pl.BlockSpec((BLOCK, EVEN_IN), lambda i: (jnp.maximum(rev(i) * nblk - 1, 0), 0)),
                  pl.BlockSpec((POOL_HALO, EVEN_IN), nxt),
                  row, nxt_row, row, nxt_row, row, square, _const_spec((1, D_MODEL)),
                  pl.BlockSpec(memory_space=pltpu.SMEM),
                  _const_spec((POOL_GROUPS, POOL_GC, POOL_GC)), _const_spec((1, 512))],
        out_specs=[pl.BlockSpec((tq, EVEN_IN), lambda i: (rev(i), 0)),
                   pl.BlockSpec((8, 128), lambda i: (0, 0)),
                   pl.BlockSpec((POOL_GROUPS * POOL_GC, POOL_GC), lambda i: (0, 0)),
                   pl.BlockSpec((1, 512), lambda i: (0, 0)), square, pl.BlockSpec((1, D_MODEL), lambda i: (0, 0))],
        scratch_shapes=[pltpu.VMEM((tq + BLOCK, 256), F32), pltpu.VMEM((tq + BLOCK, 256), F32),
                        pltpu.VMEM((BLOCK, 256), F32)] + ATTN_CONSTS
        + [pltpu.VMEM((D_MODEL, D_MODEL), F32), pltpu.VMEM((tq + POOL_HALO, D_MODEL), F32)],
        params=_params("arbitrary"))


CONV_RC = 32
CONV_CC = 128
CONV_CHAINS = 4
CONV_UNROLL = 2


def _fill_shifted(s_ref, rows):
    for b in range(1, 8):
        s_ref[b, 0:rows - 8, :] = s_ref[0, b:b + rows - 8, :]


def _tap_blocks(s_ref, r, cols, lead):
    for b in range(8):
        taps = [(a, 8 * a + b - lead) for a in range(5) if 0 <= 8 * a + b - lead < CONV_K]
        span = 8 * max(a for a, _ in taps) + CONV_RC
        blk = s_ref[b, pl.ds(r, span), cols]
        for a, k in taps:
            yield k, blk[8 * a:8 * a + CONV_RC]


def _conv_taps(s_ref, w_ref, r, cols, lead, reverse):
    accs = [None] * CONV_CHAINS
    for n, (k, blk) in enumerate(_tap_blocks(s_ref, r, cols, lead)):
        kw = CONV_K - 1 - k if reverse else k
        term = blk * w_ref[kw:kw + 1, cols]
        accs[n % CONV_CHAINS] = term if accs[n % CONV_CHAINS] is None else accs[n % CONV_CHAINS] + term
    return (accs[0] + accs[1]) + (accs[2] + accs[3])


def _layer_norm_fwd(cf, lng, lnb):
    mu = jnp.mean(cf, axis=-1, keepdims=True)
    xc = cf - mu
    rstd = lax.rsqrt(jnp.mean(xc * xc, axis=-1, keepdims=True) + EPS)
    chat = xc * rstd
    return chat, rstd, chat * lng + lnb


def _layer1_fwd(proj, dw, dwb, lng, lnb, w_out, x_in, post, target, tt=256):
    t = proj.shape[0]
    lead = CONV_HALO - (CONV_K - 1)

    def body(main_ref, halo_ref, w_ref, b_ref, g_ref, lb_ref, wo_ref, x_ref, p_ref, t_ref,
             o_ref, c_ref, y_ref, dl_ref, l_ref, gs_ref):
        i = pl.program_id(0)
        hv = halo_ref[...]
        gs_ref[0, 0:CONV_HALO, :] = jnp.where(i == 0, 0.0, hv[:, 0:1024] * _sigmoid(hv[:, 1024:2048]))
        gs_ref[0, CONV_HALO:CONV_HALO + tt, :] = main_ref[:, 0:1024] * _sigmoid(main_ref[:, 1024:2048])
        _fill_shifted(gs_ref, tt + CONV_HALO)

        for c in range(D_MODEL // CONV_CC):
            cols = slice(c * CONV_CC, (c + 1) * CONV_CC)

            def chunk(j, carry):
                r = pl.multiple_of(j * CONV_RC, CONV_RC)
                c_ref[pl.ds(r, CONV_RC), cols] = _conv_taps(gs_ref, w_ref, r, cols, lead, False) + b_ref[:, cols]
                return carry
            lax.fori_loop(0, tt // CONV_RC, chunk, 0, unroll=CONV_UNROLL)

        _, _, cn = _layer_norm_fwd(c_ref[...], g_ref[...], lb_ref[...])
        o_ref[...] = (_silu(cn) * _silu(main_ref[:, 2048:3072])).astype(BF16)

        d = _project_out(o_ref[...], wo_ref, x_ref, p_ref, y_ref) - t_ref[...]
        dl_ref[...] = d * (1.0 / D_MODEL)

        @pl.when(i == 0)
        def _():
            l_ref[...] = jnp.zeros_like(l_ref)

        l_ref[...] += 0.5 * jnp.sum(jnp.mean(d * d, axis=-1, keepdims=True))

    vec = _const_spec((1, D_MODEL))
    row = pl.BlockSpec((tt, D_MODEL), lambda i: (i, 0))
    f32_rows = jax.ShapeDtypeStruct((t, D_MODEL), F32)
    return pl.pallas_call(
        body, name="layer1_fwd", grid=(t // tt,),
        out_shape=[jax.ShapeDtypeStruct((t, D_MODEL), BF16), f32_rows, f32_rows, f32_rows,
                   jax.ShapeDtypeStruct((8, 128), F32)],
        in_specs=[pl.BlockSpec((tt, 3 * D_MODEL), lambda i: (i, 0)),
                  pl.BlockSpec((CONV_HALO, 3 * D_MODEL), lambda i: (jnp.maximum(i * (tt // CONV_HALO) - 1, 0), 0)),
                  _const_spec((CONV_K, D_MODEL)), vec, vec, vec,
                  _const_spec((D_MODEL, D_MODEL)), row, vec, row],
        out_specs=[row, row, row, row, pl.BlockSpec((8, 128), lambda i: (0, 0))],
        scratch_shapes=[pltpu.VMEM((8, tt + CONV_HALO, D_MODEL), F32)],
        compiler_params=_params("arbitrary"),
    )(proj, proj, dw, dwb, lng, lnb, w_out, x_in, post, target)


def _layer1_bwd(proj, cf, gy, y, z, w_out, post, dw, lng, lnb, comm=None, tt=256):
    t = proj.shape[0]
    nt = t // tt
    te = tt + CONV_HALO

    def body(main_ref, next_ref, cf_ref, cfn_ref, gy_ref, gyn_ref, y_ref, yn_ref, z_ref, wo_ref, po_ref,
             w_ref, g_ref, lb_ref,
             o_ref, ddw_ref, ddb_ref, dg_ref, dlb_ref, dwo16_ref, dpo_ref, ds_ref, glu_ref, sb_ref, dwo_ref):
        i = pl.program_id(0)

        @pl.when(i == 0)
        def _():
            ddw_ref[...] = jnp.zeros_like(ddw_ref)
            ddb_ref[...] = jnp.zeros_like(ddb_ref)
            dg_ref[...] = jnp.zeros_like(dg_ref)
            dlb_ref[...] = jnp.zeros_like(dlb_ref)

        dzv = _post_bwd_rows(jnp.concatenate([gy_ref[...], gyn_ref[...]], axis=0),
                             jnp.concatenate([y_ref[...], yn_ref[...]], axis=0), z_ref[...], tt, i == 0, i == nt - 1,
                             po_ref, wo_ref, dwo_ref, dwo16_ref, dpo_ref)
        dzv = jnp.concatenate([dzv[0:tt], jnp.where(i < nt - 1, dzv[tt:], 0.0)], axis=0)
        lng = g_ref[...]
        chat, rstd, cn = _layer_norm_fwd(jnp.concatenate([cf_ref[...], cfn_ref[...]], axis=0), lng, lb_ref[...])
        gate = jnp.concatenate([main_ref[:, 2048:3072], next_ref[:, 2048:3072]], axis=0)
        silu_cn, dsilu_cn = _silu_and_grad(cn)
        silu_gate, dsilu_gate = _silu_and_grad(gate)
        o_ref[:, 2048:3072] = (dzv * silu_cn * dsilu_gate)[0:tt].astype(BF16)
        dcn = dzv * silu_gate * dsilu_cn
        dg_ref[...] += jnp.sum((dcn * chat)[0:tt], axis=0, keepdims=True)
        dlb_ref[...] += jnp.sum(dcn[0:tt], axis=0, keepdims=True)
        dchat = dcn * lng
        dcf = rstd * (dchat - jnp.mean(dchat, axis=-1, keepdims=True) - chat * jnp.mean(dchat * chat, axis=-1, keepdims=True))
        ddb_ref[...] += jnp.sum(dcf[0:tt], axis=0, keepdims=True)
        ds_ref[0, 0:te, :] = dcf
        ds_ref[0, te:, :] = jnp.zeros((8, D_MODEL), F32)
        _fill_shifted(ds_ref, te + 8)
        sb_ref[...] = _sigmoid(main_ref[:, 1024:2048])
        glu_ref[...] = main_ref[:, 0:1024] * sb_ref[...]

        for c in range(D_MODEL // CONV_CC):
            cols = slice(c * CONV_CC, (c + 1) * CONV_CC)
            gcols = slice(c * CONV_CC + 1024, (c + 1) * CONV_CC + 1024)

            def chunk(j, carry):
                r = pl.multiple_of(j * CONV_RC, CONV_RC)
                dglu = _conv_taps(ds_ref, w_ref, r, cols, 0, True)
                sb = sb_ref[pl.ds(r, CONV_RC), cols]
                o_ref[pl.ds(r, CONV_RC), cols] = (dglu * sb).astype(BF16)
                o_ref[pl.ds(r, CONV_RC), gcols] = (dglu * glu_ref[pl.ds(r, CONV_RC), cols] * (1.0 - sb)).astype(BF16)
                return carry
            lax.fori_loop(0, tt // CONV_RC, chunk, 0, unroll=CONV_UNROLL)

            def taps(j, accs):
                r = pl.multiple_of(j * CONV_RC, CONV_RC)
                gl = glu_ref[pl.ds(r, CONV_RC), cols]
                new = list(accs)
                for m, blk in _tap_blocks(ds_ref, r, cols, 0):
                    prod = blk * gl
                    part = prod[0:8]
                    for q in range(1, CONV_RC // 8):
                        part = part + prod[8 * q:8 * q + 8]
                    new[m] = new[m] + part
                return tuple(new)
            accs = lax.fori_loop(0, tt // CONV_RC, taps, tuple(jnp.zeros((8, CONV_CC), F32) for _ in range(CONV_K)))
            for m in range(CONV_K):
                k = CONV_K - 1 - m
                ddw_ref[k:k + 1, cols] += jnp.sum(accs[m], axis=0, keepdims=True)

    vec = _const_spec((1, D_MODEL))
    vec_out = pl.BlockSpec((1, D_MODEL), lambda i: (0, 0))
    row = pl.BlockSpec((tt, D_MODEL), lambda i: (i, 0))
    nxt = lambda i: (jnp.minimum((i + 1) * (tt // CONV_HALO), t // CONV_HALO - 1), 0)
    nxt_row = pl.BlockSpec((CONV_HALO, D_MODEL), nxt)
    vec_f32 = jax.ShapeDtypeStruct((1, D_MODEL), F32)
    square = _const_spec((D_MODEL, D_MODEL))
    return _fused_call(
        body, comm, (proj, proj, cf, cf, gy, gy, y, y, z, w_out, post, dw, lng, lnb), name="layer1_bwd", grid=(nt,),
        out_shape=[jax.ShapeDtypeStruct((t, 3 * D_MODEL), BF16), jax.ShapeDtypeStruct((CONV_K, D_MODEL), F32),
                   vec_f32, vec_f32, vec_f32, jax.ShapeDtypeStruct((D_MODEL, D_MODEL), BF16), vec_f32],
        in_specs=[pl.BlockSpec((tt, 3 * D_MODEL), lambda i: (i, 0)),
                  pl.BlockSpec((CONV_HALO, 3 * D_MODEL), nxt),
                  row, nxt_row, row, nxt_row, row, nxt_row, row, square, vec,
                  _const_spec((CONV_K, D_MODEL)), vec, vec],
        out_specs=[pl.BlockSpec((tt, 3 * D_MODEL), lambda i: (i, 0)),
                   pl.BlockSpec((CONV_K, D_MODEL), lambda i: (0, 0)), vec_out, vec_out, vec_out, square, vec_out],
        scratch_shapes=[pltpu.VMEM((8, te + 8, D_MODEL), F32), pltpu.VMEM((tt, D_MODEL), F32),
                        pltpu.VMEM((tt, D_MODEL), F32), pltpu.VMEM((D_MODEL, D_MODEL), F32)],
        params=_params("arbitrary"))


def _piece_sum(parts, place, name):
    r = parts[0][0].shape[1]

    def body(p_ref, *refs):
        o_ref = refs[-1]
        acc = refs[0][0].astype(F32)
        for part in refs[1:-1]:
            acc = acc + part[0].astype(F32)
        o_ref[0] = acc

    blk = (1, r, D_MODEL)
    spec = lambda slot: pl.BlockSpec(blk, lambda j, p_ref: (slot(p_ref), 0, 0))
    return pl.pallas_call(
        body, name=name,
        grid_spec=pltpu.PrefetchScalarGridSpec(
            num_scalar_prefetch=1, grid=(1,),
            in_specs=[spec(slot) for _, slot in parts],
            out_specs=pl.BlockSpec(blk, lambda j, p_ref: (p_ref[1], 0, 0))),
        out_shape=jax.ShapeDtypeStruct((2, r, D_MODEL), F32),
        compiler_params=_params("arbitrary"),
    )(place, *[a for a, _ in parts])


def _direct_parts(own, recv):
    peer = lambda m: (lambda p: p[0] ^ m)
    return [(own, peer(0))] + [(recv, peer(m)) for m in range(1, N_DEV)]


def _pair_parts(sums, recv):
    const = lambda j: (lambda p: j)
    return [(sums, lambda p: p[2])] + [(recv, const(j)) for j in range(3)]


def _share_with_sibling(halves, name):
    n = len(halves)

    def body(*refs):
        outs = refs[n:2 * n]
        send_sems, recv_sems = refs[2 * n:]
        x, y, c, _ = _place()
        send = [pltpu.make_async_remote_copy(
            src_ref=outs[t].at[c], dst_ref=outs[t].at[c], send_sem=send_sems.at[t], recv_sem=recv_sems.at[t],
            device_id=(x, y, 1 - c), device_id_type=MESH_ID) for t in range(n)]
        recv = [pltpu.make_async_remote_copy(
            src_ref=outs[t].at[c], dst_ref=outs[t].at[1 - c], send_sem=send_sems.at[t], recv_sem=recv_sems.at[t],
            device_id=(x, y, 1 - c), device_id_type=MESH_ID) for t in range(n)]
        for cp in send:
            cp.start()
        for cp in recv:
            cp.wait_recv()
        for cp in send:
            cp.wait_send()

    return pl.pallas_call(
        body, name=name,
        out_shape=[jax.ShapeDtypeStruct(h.shape, h.dtype) for h in halves],
        in_specs=[ANY] * n, out_specs=[ANY] * n,
        input_output_aliases={t: t for t in range(n)},
        scratch_shapes=[pltpu.SemaphoreType.DMA((n,)), pltpu.SemaphoreType.DMA((n,))],
    )(*halves)


def _sum8(parts, name):
    r = parts.shape[1]

    def body(p_ref, o_ref):
        acc = p_ref[0]
        for k in range(1, N_DEV):
            acc = acc + p_ref[k]
        o_ref[...] = acc

    return pl.pallas_call(
        body, name=name, out_shape=jax.ShapeDtypeStruct((r, 128), F32),
        in_specs=[pl.BlockSpec(memory_space=pltpu.VMEM)], out_specs=pl.BlockSpec(memory_space=pltpu.VMEM),
    )(parts)


def _adamw(w, g, m, v, name):
    shape = w.shape
    cols = shape[-1]
    rows = w.size // cols
    rt = 256 if rows % 256 == 0 else rows

    def body(w_ref, g_ref, m_ref, v_ref, d_ref, nm_ref, nv_ref):
        gv = g_ref[...]
        mn = ADAM_B1 * m_ref[...] + (1.0 - ADAM_B1) * gv
        vn = ADAM_B2 * v_ref[...] + (1.0 - ADAM_B2) * (gv * gv)
        m_hat = mn / (1.0 - ADAM_B1 ** ADAM_STEP)
        v_hat = vn / (1.0 - ADAM_B2 ** ADAM_STEP)
        d_ref[...] = -ADAM_LR * (m_hat / (jnp.sqrt(v_hat) + ADAM_EPS) + ADAM_WD * w_ref[...])
        nm_ref[...] = mn
        nv_ref[...] = vn

    spec = pl.BlockSpec((rt, cols), lambda i: (i, 0))
    outs = pl.pallas_call(
        body, name=name, grid=(rows // rt,),
        out_shape=[jax.ShapeDtypeStruct((rows, cols), F32)] * 3,
        in_specs=[spec] * 4, out_specs=[spec] * 3,
        compiler_params=_params("parallel"),
    )(*[a.reshape(rows, cols) for a in (w, g, m, v)])
    return [o.reshape(shape) for o in outs]


SMALL_ROWS = 832


def _pack_small(g):
    parts = [g["loss"], g["pre1"].reshape(8, 128), g["post0"].reshape(8, 128),
             g["post1"].reshape(8, 128), g["sinks"], jnp.pad(g["pool_scale"].reshape(4, 128), ((0, 4), (0, 0))),
             g["pool_w"], g["dw"].reshape(248, 128), g["dwb"].reshape(8, 128), g["lng"].reshape(8, 128),
             g["lnb"].reshape(8, 128)]
    assert sum(p.shape[0] for p in parts) == SMALL_ROWS
    return jnp.concatenate(parts, axis=0)


def _unpack_small(s):
    out, r = {}, 0
    for key, rows, shape in (("loss", 8, (8, 128)), ("pre1", 8, (1, D_MODEL)), ("post", 16, (2, D_MODEL)),
                             ("sinks", 8, (8, 128)),
                             ("pool_scale", 4, (1, 512)), ("pad", 4, (4, 128)), ("pool_w", 512, (1, 4, 128, 128)),
                             ("dw", 248, (CONV_K, D_MODEL)), ("dwb", 8, (1, D_MODEL)), ("lng", 8, (1, D_MODEL)),
                             ("lnb", 8, (1, D_MODEL))):
        out[key] = s[r:r + rows].reshape(shape)
        r += rows
    return out


def kernel(x, pre_norm, post_norm, a_w_in, a_sinks, b_pool_w, b_pool_scale, ab_w_out, c_w_in, c_dw_w, c_dw_b, c_ln_g, c_ln_b, c_w_out, loss_target, m_pre_norm, m_post_norm, m_a_w_in, m_a_sinks, m_b_pool_w, m_b_pool_scale, m_ab_w_out, m_c_w_in, m_c_dw_w, m_c_dw_b, m_c_ln_g, m_c_ln_b, m_c_w_out, v_pre_norm, v_post_norm, v_a_w_in, v_a_sinks, v_b_pool_w, v_b_pool_scale, v_ab_w_out, v_c_w_in, v_c_dw_w, v_c_dw_b, v_c_ln_g, v_c_ln_b, v_c_w_out):
    ix, iy = lax.axis_index("x"), lax.axis_index("y")
    chip_cols = (2 * ix + iy) * 256

    pad8 = lambda v: jnp.pad(v, ((0, -v.shape[0] % 8), (0, 0)))
    vec_shard = jnp.concatenate([pad8(c_dw_w.reshape(CONV_K, 256)), pad8(c_dw_b), pad8(c_ln_g), pad8(c_ln_b),
                                 jnp.zeros((8, 256), F32)], axis=0)
    x0, target = x[0], loss_target[0]
    pre0, pre1 = pre_norm[0:1], pre_norm[1:2]
    post0, post1 = post_norm[0:1], post_norm[1:2]
    pool_w = b_pool_w[0]

    (wa_t,) = _run_comm(_Gather([a_w_in[0].T.astype(BF16)], halve=True), "gather_a_w_in")
    wa_t = wa_t.reshape(EVEN_IN, D_MODEL)
    proj0, (w_ab,) = _norm_matmul(x0, pre0, wa_t, "proj0_fwd", comm=_Gather([ab_w_out[0].astype(BF16)], halve=True))
    w_ab = w_ab.reshape(D_MODEL, D_MODEL)
    (mix0, y0, x1), (wc_t, w_c, vecs) = _layer0_fwd(
        proj0, a_sinks, pool_w, b_pool_scale, w_ab, x0, post0,
        comm=_Gather([c_w_in[0].T.astype(BF16), c_w_out[0].astype(BF16), vec_shard], halve=True))
    wc_t = wc_t.reshape(3 * D_MODEL, D_MODEL)
    w_c = w_c.reshape(D_MODEL, D_MODEL)
    vecs = vecs.reshape(4, 64, 256).transpose(1, 0, 2).reshape(64, D_MODEL)
    dw, dwb, lng, lnb = vecs[0:CONV_K], vecs[32:33], vecs[40:41], vecs[48:49]
    proj1, _ = _norm_matmul(x1, pre1, wc_t, "proj1_fwd")
    z1, cf1, y1, g2, loss = _layer1_fwd(proj1, dw, dwb, lng, lnb, w_c, x1, post1, target)

    pieces = lambda m: m.reshape(N_DEV, m.shape[0] // N_DEV, D_MODEL)
    (dproj1, d_dw, d_dwb, d_lng, d_lnb, d_wc, d_post1), _ = _layer1_bwd(proj1, cf1, g2, y1, z1, w_c, post1, dw, lng, lnb)
    (g1, d_wct, d_pre1), (r_wc,) = _pre_bwd(dproj1, wc_t, x1, pre1, g2, "proj1_bwd", comm=_Scatter([pieces(d_wc)]))
    (dproj0, d_sinks, d_pw, d_ps, d_wab, d_post0), (r_wct,) = _layer0_bwd(
        proj0, g1, y0, mix0, w_ab, post0, a_sinks, pool_w, b_pool_scale, comm=_Scatter([pieces(d_wct)]))
    g = dict(loss=loss, pre1=d_pre1, post0=d_post0, post1=d_post1, sinks=d_sinks, pool_w=d_pw, pool_scale=d_ps,
             dw=d_dw, dwb=d_dwb, lng=d_lng, lnb=d_lnb)
    d_wat, (small8, r_wab) = _proj_dw(dproj0, x0, pre0, "proj0_dw",
                                      comm=_Comms(_Gather([_pack_small(g)], halve=False), _Scatter([pieces(d_wab)])))
    (gx, d_pre0), (_, s_wat, r_wat) = _proj_dx(dproj0, wa_t, x0, pre0, g1, "proj0_dx",
                                               comm=_PairScatter([pieces(d_wat)]))
    (pre0_8,) = _run_comm(_Gather([d_pre0.reshape(8, 128)], halve=False), "pre0_all_gather")

    ic = lax.axis_index("c")
    place = jnp.stack([4 * ix + 2 * iy + ic, ic, 2 * ix + iy]).astype(jnp.int32)
    parts = [_pair_parts(s_wat, r_wat), _direct_parts(pieces(d_wab), r_wab), _direct_parts(pieces(d_wct), r_wct),
             _direct_parts(pieces(d_wc), r_wc)]
    halves = [_piece_sum(p, place, f"grad_sum{t}") for t, p in enumerate(parts)]
    g_wa_t, g_wab, g_wc_t, g_wc = [h.reshape(2 * h.shape[1], D_MODEL) for h in _share_with_sibling(halves, "grad_share")]
    g_a_w_in = g_wa_t.T[None]
    g_c_w_in = g_wc_t.T[None]
    g_ab_w_out = g_wab[None]
    g_c_w_out = g_wc[None]

    s = _unpack_small(_sum8(small8, "small_sum"))
    layer = lax.broadcasted_iota(jnp.int32, (2, D_MODEL), 0)
    g_pre = jnp.where(layer == 0, _sum8(pre0_8, "pre0_sum").reshape(1, D_MODEL), s["pre1"])
    g_post = s["post"]
    g_sinks = s["sinks"][:, 0].reshape(1, 8)
    g_pool_w, g_pool_scale = s["pool_w"], s["pool_scale"]
    g_dw = lax.dynamic_slice_in_dim(s["dw"], chip_cols, 256, axis=1).reshape(1, CONV_K, 1, 256)
    g_dwb = lax.dynamic_slice_in_dim(s["dwb"], chip_cols, 256, axis=1)
    g_lng = lax.dynamic_slice_in_dim(s["lng"], chip_cols, 256, axis=1)
    g_lnb = lax.dynamic_slice_in_dim(s["lnb"], chip_cols, 256, axis=1)

    grads = [g_pre, g_post, g_a_w_in, g_sinks, g_pool_w, g_pool_scale, g_ab_w_out, g_c_w_in, g_dw, g_dwb, g_lng, g_lnb,
             g_c_w_out]
    weights = [pre_norm, post_norm, a_w_in, a_sinks, b_pool_w, b_pool_scale, ab_w_out, c_w_in, c_dw_w, c_dw_b, c_ln_g,
               c_ln_b, c_w_out]
    moms = [m_pre_norm, m_post_norm, m_a_w_in, m_a_sinks, m_b_pool_w, m_b_pool_scale, m_ab_w_out, m_c_w_in, m_c_dw_w,
            m_c_dw_b, m_c_ln_g, m_c_ln_b, m_c_w_out]
    vars_ = [v_pre_norm, v_post_norm, v_a_w_in, v_a_sinks, v_b_pool_w, v_b_pool_scale, v_ab_w_out, v_c_w_in, v_c_dw_w,
             v_c_dw_b, v_c_ln_g, v_c_ln_b, v_c_w_out]
    deltas, new_m, new_v = [], [], []
    for k, (w, gr, m, v) in enumerate(zip(weights, grads, moms, vars_)):
        d, nm, nv = _adamw(w, gr, m, v, f"adamw{k}")
        deltas.append(d)
        new_m.append(nm)
        new_v.append(nv)
    return (s["loss"][0, 0], gx[None], *grads, *deltas, *new_m, *new_v)
```

```python
import functools

import jax
import jax.numpy as jnp
from jax import lax
from jax.experimental import pallas as pl
from jax.experimental.pallas import tpu as pltpu

F32 = jnp.float32
BF16 = jnp.bfloat16

D_MODEL = 1024
EPS = 1e-6
NEG = -1e30
HEAD_DIM = 64
GROUP = 4
KV_HEADS = 2
BLOCK = 128
EVEN_IN = 2304
COL_Q, COL_K, COL_GA, COL_U, COL_GB = 0, 512, 768, 1280, 1792
POOL_GROUPS = 4
POOL_GC = 128
POOL_HALO = 16
CONV_K = 31
CONV_HALO = 32
N_DEV = 8

ADAM_LR = 0.001
ADAM_B1 = 0.9
ADAM_B2 = 0.999
ADAM_EPS = 1e-08
ADAM_WD = 0.01
ADAM_STEP = 10

VMEM_LIMIT_BYTES = 56 * 1024 * 1024

NT = (((1,), (1,)), ((), ()))
TN = (((0,), (0,)), ((), ()))
MESH_ID = pl.DeviceIdType.MESH


def _params(*sem):
    return pltpu.CompilerParams(dimension_semantics=sem, vmem_limit_bytes=VMEM_LIMIT_BYTES)


def _const_spec(shape):
    nd = len(shape)
    return pl.BlockSpec(shape, lambda *_: (0,) * nd, pipeline_mode=pl.Buffered(1))


def _sigmoid(v):
    return 0.5 * jnp.tanh(0.5 * v) + 0.5


def _silu(v):
    return v * _sigmoid(v)


def _silu_and_grad(v):
    s = _sigmoid(v)
    return v * s, s * (1.0 + v * (1.0 - s))


ANY = pl.BlockSpec(memory_space=pl.ANY)


def _place():
    x, y, c = lax.axis_index("x"), lax.axis_index("y"), lax.axis_index("c")
    chips = [(1 - x, y), (x, 1 - y), (1 - x, 1 - y)]
    return x, y, c, chips


class _Gather:
    def __init__(self, blocks, halve):
        self.ins = list(blocks)
        self.halve = halve
        self.n = n = len(blocks)
        self.shapes = [((b.shape[0] // 2) if halve else b.shape[0], b.shape[1]) for b in blocks]
        self.out_shape = [jax.ShapeDtypeStruct((N_DEV, r, cols), b.dtype) for (r, cols), b in zip(self.shapes, blocks)]
        self.scratch = [pltpu.SemaphoreType.DMA((7 * n,)), pltpu.SemaphoreType.DMA((7 * n,)),
                        pltpu.SemaphoreType.DMA((n,))]

    def _copies(self, ins, outs, sems):
        send_sems, recv_sems, local_sems = sems
        x, y, c, chips = _place()
        me, sibling = (x, y, c), (x, y, 1 - c)

        def piece(t, px, py, pc):
            return outs[t].at[4 * px + 2 * py + pc]

        def own(t):
            return ins[t].at[pl.ds(c * self.shapes[t][0], self.shapes[t][0])] if self.halve else ins[t]

        def copy(t, k, block, to, src=None):
            return pltpu.make_async_remote_copy(
                src_ref=piece(t, *block) if src is None else src, dst_ref=piece(t, *block),
                send_sem=send_sems.at[7 * t + k], recv_sem=recv_sems.at[7 * t + k],
                device_id=to, device_id_type=MESH_ID)

        rng = range(self.n)
        return dict(
            mine=[pltpu.make_async_copy(own(t), piece(t, *me), local_sems.at[t]) for t in rng],
            first=[copy(t, 0, me, sibling, src=own(t)) for t in rng]
            + [copy(t, 1 + j, me, (*chip, c), src=own(t)) for t in rng for j, chip in enumerate(chips)],
            landed=[copy(t, 1 + j, (*chip, c), me) for j, chip in enumerate(chips) for t in rng],
            passed=[copy(t, 4 + j, (*chip, c), sibling) for j, chip in enumerate(chips) for t in rng],
            from_sibling=[copy(t, 0, sibling, me) for t in rng]
            + [copy(t, 4 + j, (*chip, 1 - c), me) for t in rng for j, chip in enumerate(chips)])

    def start(self, ins, outs, sems):
        d = self._copies(ins, outs, sems)
        for cp in d["mine"] + d["first"]:
            cp.start()

    def middle(self, ins, outs, sems):
        d = self._copies(ins, outs, sems)
        for got, fwd in zip(d["landed"], d["passed"]):
            got.wait_recv()
            fwd.start()

    def finish(self, ins, outs, sems):
        d = self._copies(ins, outs, sems)
        for cp in d["from_sibling"]:
            cp.wait_recv()
        for cp in d["first"] + d["passed"]:
            cp.wait_send()
        for cp in d["mine"]:
            cp.wait()


class _Scatter:
    def __init__(self, tensors):
        self.ins = list(tensors)
        self.n = n = len(tensors)
        self.out_shape = [jax.ShapeDtypeStruct(t.shape, t.dtype) for t in tensors]
        self.scratch = [pltpu.SemaphoreType.DMA((7 * n,)), pltpu.SemaphoreType.DMA((7 * n,))]

    def _copies(self, ins, outs, sems):
        send_sems, recv_sems = sems
        x, y, c, _ = _place()
        me = 4 * x + 2 * y + c
        sends, recvs = [], []
        for t in range(self.n):
            for m in range(1, N_DEV):
                px, py, pc = x ^ (m >> 2), y ^ ((m >> 1) & 1), c ^ (m & 1)
                q = 4 * px + 2 * py + pc
                sems_k = dict(send_sem=send_sems.at[7 * t + m - 1], recv_sem=recv_sems.at[7 * t + m - 1],
                              device_id=(px, py, pc), device_id_type=MESH_ID)
                sends.append(pltpu.make_async_remote_copy(src_ref=ins[t].at[q], dst_ref=outs[t].at[me], **sems_k))
                recvs.append(pltpu.make_async_remote_copy(src_ref=ins[t].at[me], dst_ref=outs[t].at[q], **sems_k))
        return sends, recvs

    def start(self, ins, outs, sems):
        for cp in self._copies(ins, outs, sems)[0]:
            cp.start()

    def middle(self, ins, outs, sems):
        pass

    def finish(self, ins, outs, sems):
        sends, recvs = self._copies(ins, outs, sems)
        for cp in recvs:
            cp.wait_recv()
        for cp in sends:
            cp.wait_send()


class _PairScatter:
    middle_step = 2

    def __init__(self, tensors):
        self.ins = list(tensors)
        self.n = n = len(tensors)
        r = tensors[0].shape[1]
        assert all(t.shape == (N_DEV, r, D_MODEL) and t.dtype == BF16 for t in tensors)
        quarter = lambda k: [jax.ShapeDtypeStruct((k, r, D_MODEL), BF16) for _ in tensors]
        self.out_shape = quarter(4) + quarter(4) + quarter(3)
        self.scratch = [pltpu.SemaphoreType.DMA((4 * n,)), pltpu.SemaphoreType.DMA((4 * n,)),
                        pltpu.SemaphoreType.DMA((3 * n,)), pltpu.SemaphoreType.DMA((3 * n,)),
                        pltpu.SemaphoreType.DMA((8,)), pltpu.VMEM((4, r, D_MODEL), BF16), pltpu.VMEM((4, r, D_MODEL), BF16)]

    def _copies(self, ins, outs, sems):
        n = self.n
        pair, sums, recv = outs[:n], outs[n:2 * n], outs[2 * n:]
        x, y, c, chips = _place()
        swap = [pltpu.make_async_remote_copy(
            src_ref=ins[t].at[2 * b + 1 - c], dst_ref=pair[t].at[b], send_sem=sems[0].at[4 * t + b],
            recv_sem=sems[1].at[4 * t + b], device_id=(x, y, 1 - c), device_id_type=MESH_ID)
            for t in range(n) for b in range(4)]
        to_owner = [pltpu.make_async_remote_copy(
            src_ref=sums[t].at[2 * cx + cy], dst_ref=recv[t].at[j], send_sem=sems[2].at[3 * t + j],
            recv_sem=sems[3].at[3 * t + j], device_id=(cx, cy, c), device_id_type=MESH_ID)
            for t in range(n) for j, (cx, cy) in enumerate(chips)]
        return swap, to_owner

    def start(self, ins, outs, sems):
        for cp in self._copies(ins, outs, sems)[0]:
            cp.start()

    def middle(self, ins, outs, sems):
        n = self.n
        pair, sums = outs[:n], outs[n:2 * n]
        local, mine_ref, theirs_ref = sems[4], sems[5], sems[6]
        c = lax.axis_index("c")
        swap, to_owner = self._copies(ins, outs, sems)
        for cp in swap:
            cp.wait_recv()
        for t in range(n):
            loads = [pltpu.make_async_copy(ins[t].at[2 * b + c], mine_ref.at[b], local.at[b]) for b in range(4)]
            loads += [pltpu.make_async_copy(pair[t].at[b], theirs_ref.at[b], local.at[4 + b]) for b in range(4)]
            for cp in loads:
                cp.start()
            for cp in loads:
                cp.wait()
            mine_ref[...] = (mine_ref[...].astype(F32) + theirs_ref[...].astype(F32)).astype(BF16)
            stores = [pltpu.make_async_copy(mine_ref.at[b], sums[t].at[b], local.at[b]) for b in range(4)]
            for cp in stores:
                cp.start()
            for cp in stores:
                cp.wait()
        for cp in to_owner:
            cp.start()
        for cp in swap:
            cp.wait_send()

    def finish(self, ins, outs, sems):
        to_owner = self._copies(ins, outs, sems)[1]
        for cp in to_owner:
            cp.wait_recv()
        for cp in to_owner:
            cp.wait_send()


class _Comms:
    def __init__(self, *comms):
        self.comms = comms
        self.ins = [a for c in comms for a in c.ins]
        self.out_shape = [s for c in comms for s in c.out_shape]
        self.scratch = [s for c in comms for s in c.scratch]

    def _each(self, phase, ins, outs, sems):
        i = o = s = 0
        for c in self.comms:
            ni, no, ns = len(c.ins), len(c.out_shape), len(c.scratch)
            getattr(c, phase)(ins[i:i + ni], outs[o:o + no], sems[s:s + ns])
            i, o, s = i + ni, o + no, s + ns

    def start(self, ins, outs, sems):
        self._each("start", ins, outs, sems)

    def middle(self, ins, outs, sems):
        self._each("middle", ins, outs, sems)

    def finish(self, ins, outs, sems):
        self._each("finish", ins, outs, sems)


def _run_comm(comm, name):
    n = len(comm.ins)

    def body(*refs):
        parts = refs[:n], refs[n:2 * n], refs[2 * n:]
        comm.start(*parts)
        comm.middle(*parts)
        comm.finish(*parts)

    return pl.pallas_call(body, name=name, out_shape=comm.out_shape, in_specs=[ANY] * n, out_specs=[ANY] * n,
                          scratch_shapes=comm.scratch)(*comm.ins)


HBM_SPEC = pl.BlockSpec(memory_space=pltpu.HBM)
SEM_SPEC = pl.BlockSpec(memory_space=pltpu.SEMAPHORE)
DATAFLOW = pltpu.SideEffectType.DATAFLOW_SIDE_EFFECTING


def _scatter_copies(own_ref, land_ref, send_sems, recv_sems):
    x, y, c, _ = _place()
    me = 4 * x + 2 * y + c
    pairs = []
    for m in range(1, N_DEV):
        px, py, pc = x ^ (m >> 2), y ^ ((m >> 1) & 1), c ^ (m & 1)
        q = 4 * px + 2 * py + pc
        sems = dict(send_sem=send_sems.at[m - 1], recv_sem=recv_sems.at[m - 1], device_id=(px, py, pc),
                    device_id_type=MESH_ID)
        pairs.append((pltpu.make_async_remote_copy(src_ref=own_ref.at[q], dst_ref=land_ref.at[me], **sems),
                      pltpu.make_async_remote_copy(src_ref=own_ref.at[me], dst_ref=land_ref.at[q], **sems)))
    return pairs


def _scatter_start(own, name):
    def body(own_ref, land_ref, send_sems, recv_sems, own_thru, land_thru, token):
        for send, _ in _scatter_copies(own_ref, land_ref, send_sems, recv_sems):
            send.start()
        token[...] = jnp.zeros_like(token)

    buf = pltpu.HBM(own.shape, own.dtype)
    return pl.pallas_call(
        body, name=name,
        out_shape=(pltpu.SemaphoreType.DMA((N_DEV - 1,)), pltpu.SemaphoreType.DMA((N_DEV - 1,)), buf, buf,
                   jax.ShapeDtypeStruct((8, 128), F32)),
        in_specs=(HBM_SPEC, HBM_SPEC),
        out_specs=(SEM_SPEC, SEM_SPEC, HBM_SPEC, HBM_SPEC, pl.BlockSpec(memory_space=pltpu.VMEM)),
        input_output_aliases={0: 2, 1: 3},
        compiler_params=pltpu.CompilerParams(has_side_effects=DATAFLOW),
    )(pltpu.with_memory_space_constraint(own, pltpu.HBM),
      pltpu.with_memory_space_constraint(lax.empty(own.shape, own.dtype), pltpu.HBM))


def _scatter_wait(send_sems, recv_sems, own_thru, land_thru, after, name):
    def body(own_ref, land_ref, send_sems, recv_sems, after_ref, own_out, land_out):
        for send, recv in _scatter_copies(own_ref, land_ref, send_sems, recv_sems):
            send.wait_send()
            recv.wait_recv()

    buf = pltpu.HBM(own_thru.shape, own_thru.dtype)
    return pl.pallas_call(
        body, name=name, out_shape=(buf, buf),
        in_specs=(HBM_SPEC, HBM_SPEC, SEM_SPEC, SEM_SPEC, ANY), out_specs=(HBM_SPEC, HBM_SPEC),
        input_output_aliases={0: 0, 1: 1},
        compiler_params=pltpu.CompilerParams(has_side_effects=DATAFLOW),
    )(own_thru, land_thru, send_sems, recv_sems, after)


def _fused_call(body, comm, args, *, name, grid, out_shape, in_specs, out_specs, scratch_shapes=(), params):
    single = not isinstance(out_shape, (list, tuple))
    out_shape = [out_shape] if single else list(out_shape)
    out_specs = [out_specs] if single else list(out_specs)
    if comm is None:
        res = pl.pallas_call(body, name=name, grid=grid, out_shape=out_shape, in_specs=in_specs, out_specs=out_specs,
                             scratch_shapes=list(scratch_shapes), compiler_params=params)(*args)
        return (res[0] if single else res), []
    n_in, n_out, n_scr = len(in_specs), len(out_shape), len(scratch_shapes)
    c_in, c_out = len(comm.ins), len(comm.out_shape)
    steps = grid[0]

    def fused(*refs):
        pos = 0
        groups = []
        for size in (n_in, c_in, n_out, c_out, n_scr, len(comm.scratch)):
            groups.append(refs[pos:pos + size])
            pos += size
        ins, c_ins, outs, c_outs, scr, c_sems = groups
        i = pl.program_id(0)

        @pl.when(i == 0)
        def _():
            comm.start(c_ins, c_outs, c_sems)

        @pl.when(i == min(getattr(comm, "middle_step", steps // 2), steps - 1))
        def _():
            comm.middle(c_ins, c_outs, c_sems)

        body(*ins, *outs, *scr)

        @pl.when(i == steps - 1)
        def _():
            comm.finish(c_ins, c_outs, c_sems)

    res = pl.pallas_call(
        fused, name=name, grid=grid, out_shape=out_shape + list(comm.out_shape),
        in_specs=list(in_specs) + [ANY] * c_in, out_specs=out_specs + [ANY] * c_out,
        scratch_shapes=list(scratch_shapes) + list(comm.scratch), compiler_params=params)(*args, *comm.ins)
    main = res[:n_out]
    return (main[0] if single else main), list(res[n_out:])


def _norm_matmul(x, gain, wt, name, comm=None, tm=512):
    t, n = x.shape[0], wt.shape[0]

    def body(x_ref, g_ref, wt_ref, o_ref):
        xv = x_ref[...]
        r = lax.rsqrt(jnp.mean(xv * xv, axis=-1, keepdims=True) + EPS)
        h = (xv * r * g_ref[...]).astype(BF16)
        o_ref[...] = lax.dot_general(h, wt_ref[...], NT, preferred_element_type=F32)

    return _fused_call(
        body, comm, (x, gain, wt), name=name, grid=(t // tm,),
        out_shape=jax.ShapeDtypeStruct((t, n), F32),
        in_specs=[pl.BlockSpec((tm, D_MODEL), lambda i: (i, 0)), _const_spec((1, D_MODEL)), _const_spec((n, D_MODEL))],
        out_specs=pl.BlockSpec((tm, n), lambda i: (i, 0)),
        params=_params("arbitrary"))


def _project_out(a, w_ref, x_ref, p_ref, y_ref):
    y = jnp.dot(a, w_ref[...], preferred_element_type=F32)
    y_ref[...] = y
    ry = lax.rsqrt(jnp.mean(y * y, axis=-1, keepdims=True) + EPS)
    return x_ref[...] + (y * ry) * p_ref[...]


def _post_bwd_rows(g, y, a, n_own, first, last, p_ref, w_ref, dw_ref, dw16_ref, dp_ref):
    @pl.when(first)
    def _():
        dw_ref[...] = jnp.zeros_like(dw_ref)
        dp_ref[...] = jnp.zeros_like(dp_ref)

    ry = lax.rsqrt(jnp.mean(y * y, axis=-1, keepdims=True) + EPS)
    nv = y * ry
    dp_ref[...] += jnp.sum((g * nv)[0:n_own], axis=0, keepdims=True)
    dn = g * p_ref[...]
    dy = (ry * (dn - nv * jnp.mean(dn * nv, axis=-1, keepdims=True))).astype(BF16)
    dw_ref[...] += lax.dot_general(a, dy[0:n_own], TN, preferred_element_type=F32)

    @pl.when(last)
    def _():
        dw16_ref[...] = dw_ref[...].astype(BF16)

    return lax.dot_general(dy, w_ref[...], NT, preferred_element_type=F32)


def _pre_bwd(dproj, wt, x_in, pre, g, name, comm=None, tm=512):
    t, n = dproj.shape
    steps = t // tm

    def body(dp_ref, wt_ref, x_ref, pre_ref, g_ref, dx_ref, dwt16_ref, dpre_ref, dwt_ref):
        @pl.when(pl.program_id(0) == 0)
        def _():
            dwt_ref[...] = jnp.zeros_like(dwt_ref)
            dpre_ref[...] = jnp.zeros_like(dpre_ref)

        dpv = dp_ref[...]
        dh = jnp.dot(dpv, wt_ref[...], preferred_element_type=F32)
        xv = x_ref[...]
        r = lax.rsqrt(jnp.mean(xv * xv, axis=-1, keepdims=True) + EPS)
        xn = xv * r
        pv = pre_ref[...]
        dpre_ref[...] += jnp.sum(dh * xn, axis=0, keepdims=True)
        dxn = dh * pv
        dx_ref[...] = g_ref[...] + r * (dxn - xn * jnp.mean(dxn * xn, axis=-1, keepdims=True))
        h = (xn * pv).astype(BF16)
        dwt_ref[...] += lax.dot_general(dpv, h, TN, preferred_element_type=F32)

        @pl.when(pl.program_id(0) == steps - 1)
        def _():
            dwt16_ref[...] = dwt_ref[...].astype(BF16)

    row = pl.BlockSpec((tm, D_MODEL), lambda i: (i, 0))
    return _fused_call(
        body, comm, (dproj, wt, x_in, pre, g), name=name, grid=(steps,),
        out_shape=[jax.ShapeDtypeStruct((t, D_MODEL), F32), jax.ShapeDtypeStruct((n, D_MODEL), BF16),
                   jax.ShapeDtypeStruct((1, D_MODEL), F32)],
        in_specs=[pl.BlockSpec((tm, n), lambda i: (i, 0)), _const_spec((n, D_MODEL)), row, _const_spec((1, D_MODEL)), row],
        out_specs=[row, _const_spec((n, D_MODEL)), pl.BlockSpec((1, D_MODEL), lambda i: (0, 0))],
        scratch_shapes=[pltpu.VMEM((n, D_MODEL), F32)],
        params=_params("arbitrary"))


def _proj_dw(dproj, x_in, pre, name, comm=None, tm=1024):
    t, n = dproj.shape
    steps = t // tm

    def body(dp_ref, x_ref, pre_ref, dwt16_ref, dwt_ref):
        @pl.when(pl.program_id(0) == 0)
        def _():
            dwt_ref[...] = jnp.zeros_like(dwt_ref)

        xv = x_ref[...]
        r = lax.rsqrt(jnp.mean(xv * xv, axis=-1, keepdims=True) + EPS)
        h = (xv * r * pre_ref[...]).astype(BF16)
        dwt_ref[...] += lax.dot_general(dp_ref[...], h, TN, preferred_element_type=F32)

        @pl.when(pl.program_id(0) == steps - 1)
        def _():
            dwt16_ref[...] = dwt_ref[...].astype(BF16)

    return _fused_call(
        body, comm, (dproj, x_in, pre), name=name, grid=(steps,),
        out_shape=jax.ShapeDtypeStruct((n, D_MODEL), BF16),
        in_specs=[pl.BlockSpec((tm, n), lambda i: (i, 0)), pl.BlockSpec((tm, D_MODEL), lambda i: (i, 0)),
                  _const_spec((1, D_MODEL))],
        out_specs=pl.BlockSpec((n, D_MODEL), lambda i: (0, 0)),
        scratch_shapes=[pltpu.VMEM((n, D_MODEL), F32)],
        params=_params("arbitrary"))


def _proj_dx(dproj, wt, x_in, pre, g, name, comm=None, tm=512):
    t, n = dproj.shape

    def body(dp_ref, wt_ref, x_ref, pre_ref, g_ref, dx_ref, dpre_ref):
        @pl.when(pl.program_id(0) == 0)
        def _():
            dpre_ref[...] = jnp.zeros_like(dpre_ref)

        dh = jnp.dot(dp_ref[...], wt_ref[...], preferred_element_type=F32)
        xv = x_ref[...]
        r = lax.rsqrt(jnp.mean(xv * xv, axis=-1, keepdims=True) + EPS)
        xn = xv * r
        dpre_ref[...] += jnp.sum(dh * xn, axis=0, keepdims=True)
        dxn = dh * pre_ref[...]
        dx_ref[...] = g_ref[...] + r * (dxn - xn * jnp.mean(dxn * xn, axis=-1, keepdims=True))

    row = pl.BlockSpec((tm, D_MODEL), lambda i: (i, 0))
    return _fused_call(
        body, comm, (dproj, wt, x_in, pre, g), name=name, grid=(t // tm,),
        out_shape=[jax.ShapeDtypeStruct((t, D_MODEL), F32), jax.ShapeDtypeStruct((1, D_MODEL), F32)],
        in_specs=[pl.BlockSpec((tm, n), lambda i: (i, 0)), _const_spec((n, D_MODEL)), row, _const_spec((1, D_MODEL)), row],
        out_specs=[row, pl.BlockSpec((1, D_MODEL), lambda i: (0, 0))],
        params=_params("arbitrary"))


def _group_masks():
    lane = lax.broadcasted_iota(jnp.int32, (1, GROUP * HEAD_DIM), 1)
    return [(lane // HEAD_DIM == g).astype(F32) for g in range(GROUP)]


def _stack_groups(v, masks):
    return jnp.concatenate([v * m for m in masks], axis=0)


def _unstack_groups(v, masks):
    out = v[0:BLOCK] * masks[0]
    for g in range(1, GROUP):
        out = out + v[g * BLOCK:(g + 1) * BLOCK] * masks[g]
    return out


def _repeat_head(kv2, kvh):
    first = lax.broadcasted_iota(jnp.int32, kv2.shape, 1) < HEAD_DIM
    rolled = pltpu.roll(kv2, HEAD_DIM, 1)
    one = jnp.where(first, kv2, rolled) if kvh == 0 else jnp.where(first, rolled, kv2)
    return jnp.concatenate([one, one], axis=1)


def _fold_head(v4):
    a = v4[:, 0:128] + v4[:, 128:256]
    return a + pltpu.roll(a, HEAD_DIM, 1)


ATTN_CONSTS = [pltpu.VMEM((KV_HEADS, GROUP * BLOCK, 2 * BLOCK), F32)]


def _fill_attn_bias(bias_ref):
    row = lax.broadcasted_iota(jnp.int32, (GROUP * BLOCK, 2 * BLOCK), 0)
    col = lax.broadcasted_iota(jnp.int32, (GROUP * BLOCK, 2 * BLOCK), 1)
    dist = (row % BLOCK) + BLOCK - col
    band = (dist >= 0) & (dist < BLOCK)
    rb = lax.broadcasted_iota(jnp.int32, (GROUP * BLOCK, 1), 0) // BLOCK
    for kvh in range(KV_HEADS):
        slope = jnp.zeros((GROUP * BLOCK, 1), F32)
        for g in range(GROUP):
            slope = jnp.where(rb == g, 2.0 ** (-(kvh * GROUP + g + 1)), slope)
        bias_ref[kvh] = jnp.where(band, -slope * dist.astype(F32), NEG)


def _row_sinks(kvh, sink_ref):
    rb = lax.broadcasted_iota(jnp.int32, (GROUP * BLOCK, 1), 0) // BLOCK
    sink = jnp.zeros((GROUP * BLOCK, 1), F32)
    for g in range(GROUP):
        sink = jnp.where(rb == g, sink_ref[0, kvh * GROUP + g], sink)
    return sink


def _attn_probs(qk, k4, bias, sink, no_past, masks):
    qs = _stack_groups(qk, masks).astype(BF16)
    s = lax.dot_general(qs, k4, NT, preferred_element_type=F32) * (HEAD_DIM ** -0.5) + bias
    s = jnp.concatenate([jnp.where(no_past, NEG, s[:, 0:BLOCK]), s[:, BLOCK:]], axis=1)
    mx = jnp.maximum(jnp.max(s, axis=-1, keepdims=True), sink)
    e = jnp.exp(s - mx)
    es = jnp.exp(sink - mx)
    inv = 1.0 / (jnp.sum(e, axis=-1, keepdims=True) + es)
    return qs, e * inv, es * inv


def _pool_forward(u_ext, g, t0):
    n = u_ext.shape[0] - POOL_HALO
    s = u_ext
    for step in range(g + 1):
        s = s + pltpu.roll(s, 1 << step, 0)
    w = 2 << g
    t = t0 + lax.broadcasted_iota(jnp.int32, (n, 1), 0)
    cnt = jnp.minimum(t + 1, w).astype(F32)
    return s[POOL_HALO:] / cnt - u_ext[POOL_HALO:]


def _layer0_fwd(proj, sinks, pool_w, pool_scale, w_out, x_in, post, comm=None, tq=512):
    t = proj.shape[0]
    nblk = tq // BLOCK

    def body(main_ref, halo_ref, sink_ref, pw_ref, ps_ref, w_ref, x_ref, p_ref, o_ref, y_ref, xo_ref, kv_ref, bias_ref):
        i = pl.program_id(0)
        t0 = i * tq
        masks = _group_masks()

        @pl.when(i == 0)
        def _():
            _fill_attn_bias(bias_ref)

        kv_ref[0:BLOCK, :] = halo_ref[:, COL_K:COL_K + 256]
        kv_ref[BLOCK:, :] = main_ref[:, COL_K:COL_K + 256]

        def block(jb, carry):
            r0 = pl.multiple_of(jb * BLOCK, BLOCK)
            no_past = t0 + r0 == 0
            q = main_ref[pl.ds(r0, BLOCK), COL_Q:COL_Q + 512]
            ga = main_ref[pl.ds(r0, BLOCK), COL_GA:COL_GA + 512]
            kk = kv_ref[pl.ds(r0, 2 * BLOCK), 0:128]
            vv = kv_ref[pl.ds(r0, 2 * BLOCK), 128:256]
            outs = []
            for kvh in range(KV_HEADS):
                k4 = _repeat_head(kk, kvh).astype(BF16)
                v4 = _repeat_head(vv, kvh).astype(BF16)
                _, p, _ = _attn_probs(q[:, kvh * 256:(kvh + 1) * 256], k4, bias_ref[kvh], _row_sinks(kvh, sink_ref), no_past, masks)
                pv = jnp.dot(p.astype(BF16), v4, preferred_element_type=F32)
                outs.append(_unstack_groups(pv, masks))
            attn = jnp.concatenate(outs, axis=1)
            o_ref[pl.ds(r0, BLOCK), 0:512] = (attn * _silu(ga)).astype(BF16)
            return carry

        lax.fori_loop(0, nblk, block, 0, unroll=2)

        for g in range(POOL_GROUPS):
            cu = COL_U + g * POOL_GC
            cg = COL_GB + g * POOL_GC
            halo_u = jnp.where(i == 0, 0.0, halo_ref[BLOCK - POOL_HALO:BLOCK, cu:cu + POOL_GC])
            u_ext = jnp.concatenate([halo_u, main_ref[:, cu:cu + POOL_GC]], axis=0)
            pooled = _pool_forward(u_ext, g, t0)
            y = jnp.dot(pooled.astype(BF16), pw_ref[g].astype(BF16), preferred_element_type=F32)
            y = y * ps_ref[:, g * POOL_GC:(g + 1) * POOL_GC]
            o_ref[:, 512 + g * POOL_GC:512 + (g + 1) * POOL_GC] = (y * _silu(main_ref[:, cg:cg + POOL_GC])).astype(BF16)

        xo_ref[...] = _project_out(o_ref[...], w_ref, x_ref, p_ref, y_ref)

    row = pl.BlockSpec((tq, D_MODEL), lambda i: (i, 0))
    return _fused_call(
        body, comm, (proj, proj, sinks, pool_w, pool_scale, w_out, x_in, post), name="layer0_fwd", grid=(t // tq,),
        out_shape=[jax.ShapeDtypeStruct((t, D_MODEL), BF16), jax.ShapeDtypeStruct((t, D_MODEL), F32),
                   jax.ShapeDtypeStruct((t, D_MODEL), F32)],
        in_specs=[pl.BlockSpec((tq, EVEN_IN), lambda i: (i, 0)),
                  pl.BlockSpec((BLOCK, EVEN_IN), lambda i: (jnp.maximum(i * nblk - 1, 0), 0)),
                  pl.BlockSpec(memory_space=pltpu.SMEM),
                  _const_spec((POOL_GROUPS, POOL_GC, POOL_GC)), _const_spec((1, 512)),
                  _const_spec((D_MODEL, D_MODEL)), row, _const_spec((1, D_MODEL))],
        out_specs=[row, row, row],
        scratch_shapes=[pltpu.VMEM((tq + BLOCK, 256), F32)] + ATTN_CONSTS,
        params=_params("arbitrary"))


def _layer0_bwd(proj, gy, y, mix, w_out, post, sinks, pool_w, pool_scale, comm=None, tq=512):
    t = proj.shape[0]
    nt = t // tq
    nblk = tq // BLOCK

    def body(main_ref, halo_ref, next_ref, gy_ref, gyn_ref, y_ref, yn_ref, mix_ref, wo_ref, po_ref,
             sink_ref, pw_ref, ps_ref,
             o_ref, dsk_ref, dpw_ref, dps_ref, dwo16_ref, dpo_ref,
             kv_ref, dkv_ref, carry_ref, bias_ref, dwo_ref, dmix_ref):
        i = pl.program_id(0)
        ii = nt - 1 - i
        t0 = ii * tq
        masks = _group_masks()

        @pl.when(i == 0)
        def _():
            _fill_attn_bias(bias_ref)
            dsk_ref[...] = jnp.zeros_like(dsk_ref)
            dpw_ref[...] = jnp.zeros_like(dpw_ref)
            dps_ref[...] = jnp.zeros_like(dps_ref)
            carry_ref[...] = jnp.zeros_like(carry_ref)

        dmix_ref[...] = _post_bwd_rows(jnp.concatenate([gy_ref[...], gyn_ref[...]], axis=0),
                                       jnp.concatenate([y_ref[...], yn_ref[...]], axis=0), mix_ref[...], tq,
                                       i == 0, i == nt - 1, po_ref, wo_ref, dwo_ref, dwo16_ref, dpo_ref)
        dm_ref = dmix_ref.at[pl.ds(0, tq)]
        dmn_ref = dmix_ref.at[pl.ds(tq, POOL_HALO)]

        kv_ref[0:BLOCK, :] = halo_ref[:, COL_K:COL_K + 256]
        kv_ref[BLOCK:, :] = main_ref[:, COL_K:COL_K + 256]
        dkv_ref[0:tq, :] = jnp.zeros((tq, 256), F32)
        dkv_ref[tq:, :] = carry_ref[...]

        def block(jb, carry):
            r0 = pl.multiple_of(jb * BLOCK, BLOCK)
            no_past = t0 + r0 == 0
            q = main_ref[pl.ds(r0, BLOCK), COL_Q:COL_Q + 512]
            ga = main_ref[pl.ds(r0, BLOCK), COL_GA:COL_GA + 512]
            dya = dm_ref[pl.ds(r0, BLOCK), 0:512]
            kk = kv_ref[pl.ds(r0, 2 * BLOCK), 0:128]
            vv = kv_ref[pl.ds(r0, 2 * BLOCK), 128:256]
            silu_ga, dsilu_ga = _silu_and_grad(ga)
            do = dya * silu_ga
            first = lax.broadcasted_iota(jnp.int32, (2 * BLOCK, 128), 1) < HEAD_DIM
            attn, dq, dk, dv = [], [], [], []
            for kvh in range(KV_HEADS):
                k4 = _repeat_head(kk, kvh).astype(BF16)
                v4 = _repeat_head(vv, kvh).astype(BF16)
                qs, p, ps = _attn_probs(q[:, kvh * 256:(kvh + 1) * 256], k4, bias_ref[kvh], _row_sinks(kvh, sink_ref), no_past, masks)
                pb = p.astype(BF16)
                o_k = _unstack_groups(jnp.dot(pb, v4, preferred_element_type=F32), masks)
                do_k = do[:, kvh * 256:(kvh + 1) * 256]
                dos = _stack_groups(do_k, masks).astype(BF16)
                prod = do_k * o_k
                delta = jnp.concatenate([jnp.sum(prod * m, axis=-1, keepdims=True) for m in masks], axis=0)
                dp = lax.dot_general(dos, v4, NT, preferred_element_type=F32)
                ds = (p * (dp - delta)).astype(BF16)
                sink_term = ps * delta
                for g in range(GROUP):
                    h = kvh * GROUP + g
                    dsk_ref[h:h + 1, :] -= jnp.sum(sink_term[g * BLOCK:(g + 1) * BLOCK], keepdims=True)
                scale = HEAD_DIM ** -0.5
                dq.append(_unstack_groups(jnp.dot(ds, k4, preferred_element_type=F32), masks) * scale)
                dk.append(_fold_head(lax.dot_general(ds, qs, TN, preferred_element_type=F32)) * scale)
                dv.append(_fold_head(lax.dot_general(pb, dos, TN, preferred_element_type=F32)))
                attn.append(o_k)
            o_ref[pl.ds(r0, BLOCK), COL_Q:COL_Q + 512] = jnp.concatenate(dq, axis=1).astype(BF16)
            o_all = jnp.concatenate(attn, axis=1)
            o_ref[pl.ds(r0, BLOCK), COL_GA:COL_GA + 512] = (dya * o_all * dsilu_ga).astype(BF16)
            dkv = jnp.concatenate([jnp.where(first, dk[0], dk[1]), jnp.where(first, dv[0], dv[1])], axis=1)
            dkv_ref[pl.ds(r0, 2 * BLOCK), :] += dkv
            return carry

        lax.fori_loop(0, nblk, block, 0, unroll=2)
        carry_ref[...] = dkv_ref[0:BLOCK, :]
        o_ref[:, COL_K:COL_K + 256] = dkv_ref[BLOCK:, :].astype(BF16)

        last = ii == nt - 1
        for g in range(POOL_GROUPS):
            cu = COL_U + g * POOL_GC
            cg = COL_GB + g * POOL_GC
            cm = 512 + g * POOL_GC
            pw = pw_ref[g].astype(BF16)
            sc = ps_ref[:, g * POOL_GC:(g + 1) * POOL_GC]
            halo_u = jnp.where(ii == 0, 0.0, halo_ref[BLOCK - POOL_HALO:BLOCK, cu:cu + POOL_GC])
            u_ext = jnp.concatenate([halo_u, main_ref[:, cu:cu + POOL_GC]], axis=0)
            pooled = _pool_forward(u_ext, g, t0).astype(BF16)
            y_raw = jnp.dot(pooled, pw, preferred_element_type=F32)
            gb = main_ref[:, cg:cg + POOL_GC]
            dyb = dm_ref[:, cm:cm + POOL_GC]
            silu_gb, dsilu_gb = _silu_and_grad(gb)
            dypool = dyb * silu_gb
            dps_ref[:, g * POOL_GC:(g + 1) * POOL_GC] += jnp.sum(dypool * y_raw, axis=0, keepdims=True)
            o_ref[:, cg:cg + POOL_GC] = (dyb * (y_raw * sc) * dsilu_gb).astype(BF16)
            dyraw = dypool * sc
            dyraw_n = jnp.where(last, 0.0, dmn_ref[:, cm:cm + POOL_GC] * _silu(next_ref[:, cg:cg + POOL_GC]) * sc)
            dpw_ref[g * POOL_GC:(g + 1) * POOL_GC, :] += lax.dot_general(pooled, dyraw.astype(BF16), TN,
                                                                         preferred_element_type=F32)
            dyraw_ext = jnp.concatenate([dyraw, dyraw_n], axis=0).astype(BF16)
            dpooled = lax.dot_general(dyraw_ext, pw, NT, preferred_element_type=F32)
            w = 2 << g
            tt = t0 + lax.broadcasted_iota(jnp.int32, (tq + POOL_HALO, 1), 0)
            s = dpooled / jnp.minimum(tt + 1, w).astype(F32)
            for step in range(g + 1):
                s = s + pltpu.roll(s, tq + POOL_HALO - (1 << step), 0)
            o_ref[:, cu:cu + POOL_GC] = (s[0:tq] - dpooled[0:tq]).astype(BF16)

    rev = lambda i: nt - 1 - i
    nxt = lambda i: (jnp.minimum((rev(i) + 1) * (tq // POOL_HALO), t // POOL_HALO - 1), 0)
    row = pl.BlockSpec((tq, D_MODEL), lambda i: (rev(i), 0))
    nxt_row = pl.BlockSpec((POOL_HALO, D_MODEL), nxt)
    square = _const_spec((D_MODEL, D_MODEL))
    return _fused_call(
        body, comm, (proj, proj, proj, gy, gy, y, y, mix, w_out, post, sinks, pool_w, pool_scale),
        name="layer0_bwd", grid=(nt,),
        out_shape=[jax.ShapeDtypeStruct((t, EVEN_IN), BF16), jax.ShapeDtypeStruct((8, 128), F32),
                   jax.ShapeDtypeStruct((POOL_GROUPS * POOL_GC, POOL_GC), F32), jax.ShapeDtypeStruct((1, 512), F32),
                   jax.ShapeDtypeStruct((D_MODEL, D_MODEL), BF16), jax.ShapeDtypeStruct((1, D_MODEL), F32)],
        in_specs=[pl.BlockSpec((tq, EVEN_IN), lambda i: (rev(i), 0)),
                  pl.BlockSpec((BLOCK, EVEN_IN), lambda i: (jnp.maximum(rev(i) * nblk - 1, 0), 0)),
                  pl.BlockSpec((POOL_HALO, EVEN_IN), nxt),
                  row, nxt_row, row, nxt_row, row, square, _const_spec((1, D_MODEL)),
                  pl.BlockSpec(memory_space=pltpu.SMEM),
                  _const_spec((POOL_GROUPS, POOL_GC, POOL_GC)), _const_spec((1, 512))],
        out_specs=[pl.BlockSpec((tq, EVEN_IN), lambda i: (rev(i), 0)),
                   pl.BlockSpec((8, 128), lambda i: (0, 0)),
                   pl.BlockSpec((POOL_GROUPS * POOL_GC, POOL_GC), lambda i: (0, 0)),
                   pl.BlockSpec((1, 512), lambda i: (0, 0)), square, pl.BlockSpec((1, D_MODEL), lambda i: (0, 0))],
        scratch_shapes=[pltpu.VMEM((tq + BLOCK, 256), F32), pltpu.VMEM((tq + BLOCK, 256), F32),
                        pltpu.VMEM((BLOCK, 256), F32)] + ATTN_CONSTS
        + [pltpu.VMEM((D_MODEL, D_MODEL), F32), pltpu.VMEM((tq + POOL_HALO, D_MODEL), F32)],
        params=_params("arbitrary"))


CONV_RC = 32
CONV_CC = 128
CONV_CHAINS = 4
CONV_UNROLL = 2


def _fill_shifted(s_ref, rows):
    for b in range(1, 8):
        s_ref[b, 0:rows - 8, :] = s_ref[0, b:b + rows - 8, :]


def _tap_blocks(s_ref, r, cols, lead):
    for b in range(8):
        taps = [(a, 8 * a + b - lead) for a in range(5) if 0 <= 8 * a + b - lead < CONV_K]
        span = 8 * max(a for a, _ in taps) + CONV_RC
        blk = s_ref[b, pl.ds(r, span), cols]
        for a, k in taps:
            yield k, blk[8 * a:8 * a + CONV_RC]


def _conv_taps(s_ref, w_ref, r, cols, lead, reverse):
    accs = [None] * CONV_CHAINS
    for n, (k, blk) in enumerate(_tap_blocks(s_ref, r, cols, lead)):
        kw = CONV_K - 1 - k if reverse else k
        term = blk * w_ref[kw:kw + 1, cols]
        accs[n % CONV_CHAINS] = term if accs[n % CONV_CHAINS] is None else accs[n % CONV_CHAINS] + term
    return (accs[0] + accs[1]) + (accs[2] + accs[3])


def _layer_norm_fwd(cf, lng, lnb):
    mu = jnp.mean(cf, axis=-1, keepdims=True)
    xc = cf - mu
    rstd = lax.rsqrt(jnp.mean(xc * xc, axis=-1, keepdims=True) + EPS)
    chat = xc * rstd
    return chat, rstd, chat * lng + lnb


def _layer1_fwd(proj, dw, dwb, lng, lnb, w_out, x_in, post, target, tt=256):
    t = proj.shape[0]
    lead = CONV_HALO - (CONV_K - 1)

    def body(main_ref, halo_ref, w_ref, b_ref, g_ref, lb_ref, wo_ref, x_ref, p_ref, t_ref,
             o_ref, c_ref, y_ref, dl_ref, l_ref, gs_ref):
        i = pl.program_id(0)
        hv = halo_ref[...]
        gs_ref[0, 0:CONV_HALO, :] = jnp.where(i == 0, 0.0, hv[:, 0:1024] * _sigmoid(hv[:, 1024:2048]))
        gs_ref[0, CONV_HALO:CONV_HALO + tt, :] = main_ref[:, 0:1024] * _sigmoid(main_ref[:, 1024:2048])
        _fill_shifted(gs_ref, tt + CONV_HALO)

        for c in range(D_MODEL // CONV_CC):
            cols = slice(c * CONV_CC, (c + 1) * CONV_CC)

            def chunk(j, carry):
                r = pl.multiple_of(j * CONV_RC, CONV_RC)
                c_ref[pl.ds(r, CONV_RC), cols] = _conv_taps(gs_ref, w_ref, r, cols, lead, False) + b_ref[:, cols]
                return carry
            lax.fori_loop(0, tt // CONV_RC, chunk, 0, unroll=CONV_UNROLL)

        _, _, cn = _layer_norm_fwd(c_ref[...], g_ref[...], lb_ref[...])
        o_ref[...] = (_silu(cn) * _silu(main_ref[:, 2048:3072])).astype(BF16)

        d = _project_out(o_ref[...], wo_ref, x_ref, p_ref, y_ref) - t_ref[...]
        dl_ref[...] = d * (1.0 / D_MODEL)

        @pl.when(i == 0)
        def _():
            l_ref[...] = jnp.zeros_like(l_ref)

        l_ref[...] += 0.5 * jnp.sum(jnp.mean(d * d, axis=-1, keepdims=True))

    vec = _const_spec((1, D_MODEL))
    row = pl.BlockSpec((tt, D_MODEL), lambda i: (i, 0))
    f32_rows = jax.ShapeDtypeStruct((t, D_MODEL), F32)
    return pl.pallas_call(
        body, name="layer1_fwd", grid=(t // tt,),
        out_shape=[jax.ShapeDtypeStruct((t, D_MODEL), BF16), f32_rows, f32_rows, f32_rows,
                   jax.ShapeDtypeStruct((8, 128), F32)],
        in_specs=[pl.BlockSpec((tt, 3 * D_MODEL), lambda i: (i, 0)),
                  pl.BlockSpec((CONV_HALO, 3 * D_MODEL), lambda i: (jnp.maximum(i * (tt // CONV_HALO) - 1, 0), 0)),
                  _const_spec((CONV_K, D_MODEL)), vec, vec, vec,
                  _const_spec((D_MODEL, D_MODEL)), row, vec, row],
        out_specs=[row, row, row, row, pl.BlockSpec((8, 128), lambda i: (0, 0))],
        scratch_shapes=[pltpu.VMEM((8, tt + CONV_HALO, D_MODEL), F32)],
        compiler_params=_params("arbitrary"),
    )(proj, proj, dw, dwb, lng, lnb, w_out, x_in, post, target)


def _layer1_bwd(proj, cf, gy, y, z, w_out, post, dw, lng, lnb, comm=None, tt=256):
    t = proj.shape[0]
    nt = t // tt
    te = tt + CONV_HALO

    def body(main_ref, next_ref, cf_ref, cfn_ref, gy_ref, gyn_ref, y_ref, yn_ref, z_ref, wo_ref, po_ref,
             w_ref, g_ref, lb_ref,
             o_ref, ddw_ref, ddb_ref, dg_ref, dlb_ref, dwo16_ref, dpo_ref, ds_ref, glu_ref, sb_ref, dwo_ref):
        i = pl.program_id(0)

        @pl.when(i == 0)
        def _():
            ddw_ref[...] = jnp.zeros_like(ddw_ref)
            ddb_ref[...] = jnp.zeros_like(ddb_ref)
            dg_ref[...] = jnp.zeros_like(dg_ref)
            dlb_ref[...] = jnp.zeros_like(dlb_ref)

        dzv = _post_bwd_rows(jnp.concatenate([gy_ref[...], gyn_ref[...]], axis=0),
                             jnp.concatenate([y_ref[...], yn_ref[...]], axis=0), z_ref[...], tt, i == 0, i == nt - 1,
                             po_ref, wo_ref, dwo_ref, dwo16_ref, dpo_ref)
        dzv = jnp.concatenate([dzv[0:tt], jnp.where(i < nt - 1, dzv[tt:], 0.0)], axis=0)
        lng = g_ref[...]
        chat, rstd, cn = _layer_norm_fwd(jnp.concatenate([cf_ref[...], cfn_ref[...]], axis=0), lng, lb_ref[...])
        gate = jnp.concatenate([main_ref[:, 2048:3072], next_ref[:, 2048:3072]], axis=0)
        silu_cn, dsilu_cn = _silu_and_grad(cn)
        silu_gate, dsilu_gate = _silu_and_grad(gate)
        o_ref[:, 2048:3072] = (dzv * silu_cn * dsilu_gate)[0:tt].astype(BF16)
        dcn = dzv * silu_gate * dsilu_cn
        dg_ref[...] += jnp.sum((dcn * chat)[0:tt], axis=0, keepdims=True)
        dlb_ref[...] += jnp.sum(dcn[0:tt], axis=0, keepdims=True)
        dchat = dcn * lng
        dcf = rstd * (dchat - jnp.mean(dchat, axis=-1, keepdims=True) - chat * jnp.mean(dchat * chat, axis=-1, keepdims=True))
        ddb_ref[...] += jnp.sum(dcf[0:tt], axis=0, keepdims=True)
        ds_ref[0, 0:te, :] = dcf
        ds_ref[0, te:, :] = jnp.zeros((8, D_MODEL), F32)
        _fill_shifted(ds_ref, te + 8)
        sb_ref[...] = _sigmoid(main_ref[:, 1024:2048])
        glu_ref[...] = main_ref[:, 0:1024] * sb_ref[...]

        for c in range(D_MODEL // CONV_CC):
            cols = slice(c * CONV_CC, (c + 1) * CONV_CC)
            gcols = slice(c * CONV_CC + 1024, (c + 1) * CONV_CC + 1024)

            def chunk(j, carry):
                r = pl.multiple_of(j * CONV_RC, CONV_RC)
                dglu = _conv_taps(ds_ref, w_ref, r, cols, 0, True)
                sb = sb_ref[pl.ds(r, CONV_RC), cols]
                o_ref[pl.ds(r, CONV_RC), cols] = (dglu * sb).astype(BF16)
                o_ref[pl.ds(r, CONV_RC), gcols] = (dglu * glu_ref[pl.ds(r, CONV_RC), cols] * (1.0 - sb)).astype(BF16)
                return carry
            lax.fori_loop(0, tt // CONV_RC, chunk, 0, unroll=CONV_UNROLL)

            def taps(j, accs):
                r = pl.multiple_of(j * CONV_RC, CONV_RC)
                gl = glu_ref[pl.ds(r, CONV_RC), cols]
                new = list(accs)
                for m, blk in _tap_blocks(ds_ref, r, cols, 0):
                    prod = blk * gl
                    part = prod[0:8]
                    for q in range(1, CONV_RC // 8):
                        part = part + prod[8 * q:8 * q + 8]
                    new[m] = new[m] + part
                return tuple(new)
            accs = lax.fori_loop(0, tt // CONV_RC, taps, tuple(jnp.zeros((8, CONV_CC), F32) for _ in range(CONV_K)))
            for m in range(CONV_K):
                k = CONV_K - 1 - m
                ddw_ref[k:k + 1, cols] += jnp.sum(accs[m], axis=0, keepdims=True)

    vec = _const_spec((1, D_MODEL))
    vec_out = pl.BlockSpec((1, D_MODEL), lambda i: (0, 0))
    row = pl.BlockSpec((tt, D_MODEL), lambda i: (i, 0))
    nxt = lambda i: (jnp.minimum((i + 1) * (tt // CONV_HALO), t // CONV_HALO - 1), 0)
    nxt_row = pl.BlockSpec((CONV_HALO, D_MODEL), nxt)
    vec_f32 = jax.ShapeDtypeStruct((1, D_MODEL), F32)
    square = _const_spec((D_MODEL, D_MODEL))
    return _fused_call(
        body, comm, (proj, proj, cf, cf, gy, gy, y, y, z, w_out, post, dw, lng, lnb), name="layer1_bwd", grid=(nt,),
        out_shape=[jax.ShapeDtypeStruct((t, 3 * D_MODEL), BF16), jax.ShapeDtypeStruct((CONV_K, D_MODEL), F32),
                   vec_f32, vec_f32, vec_f32, jax.ShapeDtypeStruct((D_MODEL, D_MODEL), BF16), vec_f32],
        in_specs=[pl.BlockSpec((tt, 3 * D_MODEL), lambda i: (i, 0)),
                  pl.BlockSpec((CONV_HALO, 3 * D_MODEL), nxt),
                  row, nxt_row, row, nxt_row, row, nxt_row, row, square, vec,
                  _const_spec((CONV_K, D_MODEL)), vec, vec],
        out_specs=[pl.BlockSpec((tt, 3 * D_MODEL), lambda i: (i, 0)),
                   pl.BlockSpec((CONV_K, D_MODEL), lambda i: (0, 0)), vec_out, vec_out, vec_out, square, vec_out],
        scratch_shapes=[pltpu.VMEM((8, te + 8, D_MODEL), F32), pltpu.VMEM((tt, D_MODEL), F32),
                        pltpu.VMEM((tt, D_MODEL), F32), pltpu.VMEM((D_MODEL, D_MODEL), F32)],
        params=_params("arbitrary"))


def _piece_sum(parts, place, name):
    r = parts[0][0].shape[1]

    def body(p_ref, *refs):
        o_ref = refs[-1]
        acc = refs[0][0].astype(F32)
        for part in refs[1:-1]:
            acc = acc + part[0].astype(F32)
        o_ref[0] = acc

    blk = (1, r, D_MODEL)
    spec = lambda slot: pl.BlockSpec(blk, lambda j, p_ref: (slot(p_ref), 0, 0))
    return pl.pallas_call(
        body, name=name,
        grid_spec=pltpu.PrefetchScalarGridSpec(
            num_scalar_prefetch=1, grid=(1,),
            in_specs=[spec(slot) for _, slot in parts],
            out_specs=pl.BlockSpec(blk, lambda j, p_ref: (p_ref[1], 0, 0))),
        out_shape=jax.ShapeDtypeStruct((2, r, D_MODEL), F32),
        compiler_params=_params("arbitrary"),
    )(place, *[a for a, _ in parts])


def _direct_parts(own, recv):
    peer = lambda m: (lambda p: p[0] ^ m)
    return [(own, peer(0))] + [(recv, peer(m)) for m in range(1, N_DEV)]


def _pair_parts(sums, recv):
    const = lambda j: (lambda p: j)
    return [(sums, lambda p: p[2])] + [(recv, const(j)) for j in range(3)]


def _share_with_sibling(halves, name):
    n = len(halves)

    def body(*refs):
        outs = refs[n:2 * n]
        send_sems, recv_sems = refs[2 * n:]
        x, y, c, _ = _place()
        send = [pltpu.make_async_remote_copy(
            src_ref=outs[t].at[c], dst_ref=outs[t].at[c], send_sem=send_sems.at[t], recv_sem=recv_sems.at[t],
            device_id=(x, y, 1 - c), device_id_type=MESH_ID) for t in range(n)]
        recv = [pltpu.make_async_remote_copy(
            src_ref=outs[t].at[c], dst_ref=outs[t].at[1 - c], send_sem=send_sems.at[t], recv_sem=recv_sems.at[t],
            device_id=(x, y, 1 - c), device_id_type=MESH_ID) for t in range(n)]
        for cp in send:
            cp.start()
        for cp in recv:
            cp.wait_recv()
        for cp in send:
            cp.wait_send()

    return pl.pallas_call(
        body, name=name,
        out_shape=[jax.ShapeDtypeStruct(h.shape, h.dtype) for h in halves],
        in_specs=[ANY] * n, out_specs=[ANY] * n,
        input_output_aliases={t: t for t in range(n)},
        scratch_shapes=[pltpu.SemaphoreType.DMA((n,)), pltpu.SemaphoreType.DMA((n,))],
    )(*halves)


def _sum8(parts, name):
    r = parts.shape[1]

    def body(p_ref, o_ref):
        acc = p_ref[0]
        for k in range(1, N_DEV):
            acc = acc + p_ref[k]
        o_ref[...] = acc

    return pl.pallas_call(
        body, name=name, out_shape=jax.ShapeDtypeStruct((r, 128), F32),
        in_specs=[pl.BlockSpec(memory_space=pltpu.VMEM)], out_specs=pl.BlockSpec(memory_space=pltpu.VMEM),
    )(parts)


def _adamw(w, g, m, v, name):
    shape = w.shape
    cols = shape[-1]
    rows = w.size // cols
    rt = 256 if rows % 256 == 0 else rows

    def body(w_ref, g_ref, m_ref, v_ref, d_ref, nm_ref, nv_ref):
        gv = g_ref[...]
        mn = ADAM_B1 * m_ref[...] + (1.0 - ADAM_B1) * gv
        vn = ADAM_B2 * v_ref[...] + (1.0 - ADAM_B2) * (gv * gv)
        m_hat = mn / (1.0 - ADAM_B1 ** ADAM_STEP)
        v_hat = vn / (1.0 - ADAM_B2 ** ADAM_STEP)
        d_ref[...] = -ADAM_LR * (m_hat / (jnp.sqrt(v_hat) + ADAM_EPS) + ADAM_WD * w_ref[...])
        nm_ref[...] = mn
        nv_ref[...] = vn

    spec = pl.BlockSpec((rt, cols), lambda i: (i, 0))
    outs = pl.pallas_call(
        body, name=name, grid=(rows // rt,),
        out_shape=[jax.ShapeDtypeStruct((rows, cols), F32)] * 3,
        in_specs=[spec] * 4, out_specs=[spec] * 3,
        compiler_params=_params("parallel"),
    )(*[a.reshape(rows, cols) for a in (w, g, m, v)])
    return [o.reshape(shape) for o in outs]


SMALL_ROWS = 832


def _pack_small(g):
    parts = [g["loss"], g["pre1"].reshape(8, 128), g["post0"].reshape(8, 128),
             g["post1"].reshape(8, 128), g["sinks"], jnp.pad(g["pool_scale"].reshape(4, 128), ((0, 4), (0, 0))),
             g["pool_w"], g["dw"].reshape(248, 128), g["dwb"].reshape(8, 128), g["lng"].reshape(8, 128),
             g["lnb"].reshape(8, 128)]
    assert sum(p.shape[0] for p in parts) == SMALL_ROWS
    return jnp.concatenate(parts, axis=0)


def _unpack_small(s):
    out, r = {}, 0
    for key, rows, shape in (("loss", 8, (8, 128)), ("pre1", 8, (1, D_MODEL)), ("post", 16, (2, D_MODEL)),
                             ("sinks", 8, (8, 128)),
                             ("pool_scale", 4, (1, 512)), ("pad", 4, (4, 128)), ("pool_w", 512, (1, 4, 128, 128)),
                             ("dw", 248, (CONV_K, D_MODEL)), ("dwb", 8, (1, D_MODEL)), ("lng", 8, (1, D_MODEL)),
                             ("lnb", 8, (1, D_MODEL))):
        out[key] = s[r:r + rows].reshape(shape)
        r += rows
    return out


def kernel(x, pre_norm, post_norm, a_w_in, a_sinks, b_pool_w, b_pool_scale, ab_w_out, c_w_in, c_dw_w, c_dw_b, c_ln_g, c_ln_b, c_w_out, loss_target, m_pre_norm, m_post_norm, m_a_w_in, m_a_sinks, m_b_pool_w, m_b_pool_scale, m_ab_w_out, m_c_w_in, m_c_dw_w, m_c_dw_b, m_c_ln_g, m_c_ln_b, m_c_w_out, v_pre_norm, v_post_norm, v_a_w_in, v_a_sinks, v_b_pool_w, v_b_pool_scale, v_ab_w_out, v_c_w_in, v_c_dw_w, v_c_dw_b, v_c_ln_g, v_c_ln_b, v_c_w_out):
    ix, iy = lax.axis_index("x"), lax.axis_index("y")
    chip_cols = (2 * ix + iy) * 256

    pad8 = lambda v: jnp.pad(v, ((0, -v.shape[0] % 8), (0, 0)))
    vec_shard = jnp.concatenate([pad8(c_dw_w.reshape(CONV_K, 256)), pad8(c_dw_b), pad8(c_ln_g), pad8(c_ln_b),
                                 jnp.zeros((8, 256), F32)], axis=0)
    x0, target = x[0], loss_target[0]
    pre0, pre1 = pre_norm[0:1], pre_norm[1:2]
    post0, post1 = post_norm[0:1], post_norm[1:2]
    pool_w = b_pool_w[0]

    (wa_t,) = _run_comm(_Gather([a_w_in[0].T.astype(BF16)], halve=True), "gather_a_w_in")
    wa_t = wa_t.reshape(EVEN_IN, D_MODEL)
    proj0, (w_ab,) = _norm_matmul(x0, pre0, wa_t, "proj0_fwd", comm=_Gather([ab_w_out[0].astype(BF16)], halve=True))
    w_ab = w_ab.reshape(D_MODEL, D_MODEL)
    (mix0, y0, x1), (wc_t, w_c, vecs) = _layer0_fwd(
        proj0, a_sinks, pool_w, b_pool_scale, w_ab, x0, post0,
        comm=_Gather([c_w_in[0].T.astype(BF16), c_w_out[0].astype(BF16), vec_shard], halve=True))
    wc_t = wc_t.reshape(3 * D_MODEL, D_MODEL)
    w_c = w_c.reshape(D_MODEL, D_MODEL)
    vecs = vecs.reshape(4, 64, 256).transpose(1, 0, 2).reshape(64, D_MODEL)
    dw, dwb, lng, lnb = vecs[0:CONV_K], vecs[32:33], vecs[40:41], vecs[48:49]
    proj1, _ = _norm_matmul(x1, pre1, wc_t, "proj1_fwd")
    z1, cf1, y1, g2, loss = _layer1_fwd(proj1, dw, dwb, lng, lnb, w_c, x1, post1, target)

    pieces = lambda m: m.reshape(N_DEV, m.shape[0] // N_DEV, D_MODEL)
    (dproj1, d_dw, d_dwb, d_lng, d_lnb, d_wc, d_post1), _ = _layer1_bwd(proj1, cf1, g2, y1, z1, w_c, post1, dw, lng, lnb)
    (g1, d_wct, d_pre1), (r_wc,) = _pre_bwd(dproj1, wc_t, x1, pre1, g2, "proj1_bwd", comm=_Scatter([pieces(d_wc)]))
    (dproj0, d_sinks, d_pw, d_ps, d_wab, d_post0), (r_wct,) = _layer0_bwd(
        proj0, g1, y0, mix0, w_ab, post0, a_sinks, pool_w, b_pool_scale, comm=_Scatter([pieces(d_wct)]))
    g = dict(loss=loss, pre1=d_pre1, post0=d_post0, post1=d_post1, sinks=d_sinks, pool_w=d_pw, pool_scale=d_ps,
             dw=d_dw, dwb=d_dwb, lng=d_lng, lnb=d_lnb)
    d_wat, (small8, r_wab) = _proj_dw(dproj0, x0, pre0, "proj0_dw",
                                      comm=_Comms(_Gather([_pack_small(g)], halve=False), _Scatter([pieces(d_wab)])))
    sent = _scatter_start(pieces(d_wat), "scatter_a_start")
    (gx, d_pre0), _ = _proj_dx(dproj0, wa_t, x0, pre0 + sent[4][0:1, 0:1], g1, "proj0_dx")
    (pre0_8,) = _run_comm(_Gather([d_pre0.reshape(8, 128)], halve=False), "pre0_all_gather")
    own_wat, r_wat = _scatter_wait(*sent[:4], pre0_8, "scatter_a_wait")

    ic = lax.axis_index("c")
    place = jnp.stack([4 * ix + 2 * iy + ic, ic, 2 * ix + iy]).astype(jnp.int32)
    parts = [_direct_parts(own_wat, r_wat), _direct_parts(pieces(d_wab), r_wab), _direct_parts(pieces(d_wct), r_wct),
             _direct_parts(pieces(d_wc), r_wc)]
    halves = [_piece_sum(p, place, f"grad_sum{t}") for t, p in enumerate(parts)]
    g_wa_t, g_wab, g_wc_t, g_wc = [h.reshape(2 * h.shape[1], D_MODEL) for h in _share_with_sibling(halves, "grad_share")]
    g_a_w_in = g_wa_t.T[None]
    g_c_w_in = g_wc_t.T[None]
    g_ab_w_out = g_wab[None]
    g_c_w_out = g_wc[None]

    s = _unpack_small(_sum8(small8, "small_sum"))
    layer = lax.broadcasted_iota(jnp.int32, (2, D_MODEL), 0)
    g_pre = jnp.where(layer == 0, _sum8(pre0_8, "pre0_sum").reshape(1, D_MODEL), s["pre1"])
    g_post = s["post"]
    g_sinks = s["sinks"][:, 0].reshape(1, 8)
    g_pool_w, g_pool_scale = s["pool_w"], s["pool_scale"]
    g_dw = lax.dynamic_slice_in_dim(s["dw"], chip_cols, 256, axis=1).reshape(1, CONV_K, 1, 256)
    g_dwb = lax.dynamic_slice_in_dim(s["dwb"], chip_cols, 256, axis=1)
    g_lng = lax.dynamic_slice_in_dim(s["lng"], chip_cols, 256, axis=1)
    g_lnb = lax.dynamic_slice_in_dim(s["lnb"], chip_cols, 256, axis=1)

    grads = [g_pre, g_post, g_a_w_in, g_sinks, g_pool_w, g_pool_scale, g_ab_w_out, g_c_w_in, g_dw, g_dwb, g_lng, g_lnb,
             g_c_w_out]
    weights = [pre_norm, post_norm, a_w_in, a_sinks, b_pool_w, b_pool_scale, ab_w_out, c_w_in, c_dw_w, c_dw_b, c_ln_g,
               c_ln_b, c_w_out]
    moms = [m_pre_norm, m_post_norm, m_a_w_in, m_a_sinks, m_b_pool_w, m_b_pool_scale, m_ab_w_out, m_c_w_in, m_c_dw_w,
            m_c_dw_b, m_c_ln_g, m_c_ln_b, m_c_w_out]
    vars_ = [v_pre_norm, v_post_norm, v_a_w_in, v_a_sinks, v_b_pool_w, v_b_pool_scale, v_ab_w_out, v_c_w_in, v_c_dw_w,
             v_c_dw_b, v_c_ln_g, v_c_ln_b, v_c_w_out]
    deltas, new_m, new_v = [], [], []
    for k, (w, gr, m, v) in enumerate(zip(weights, grads, moms, vars_)):
        d, nm, nv = _adamw(w, gr, m, v, f"adamw{k}")
        deltas.append(d)
        new_m.append(nm)
        new_v.append(nv)
    return (s["loss"][0, 0], gx[None], *grads, *deltas, *new_m, *new_v)
```

```python
import functools

import jax
import jax.numpy as jnp
from jax import lax
from jax.experimental import pallas as pl
from jax.experimental.pallas import tpu as pltpu

F32 = jnp.float32
BF16 = jnp.bfloat16

D_MODEL = 1024
EPS = 1e-6
NEG = -1e30
HEAD_DIM = 64
GROUP = 4
KV_HEADS = 2
BLOCK = 128
EVEN_IN = 2304
COL_Q, COL_K, COL_GA, COL_U, COL_GB = 0, 512, 768, 1280, 1792
POOL_GROUPS = 4
POOL_GC = 128
POOL_HALO = 16
CONV_K = 31
CONV_HALO = 32
N_DEV = 8

ADAM_LR = 0.001
ADAM_B1 = 0.9
ADAM_B2 = 0.999
ADAM_EPS = 1e-08
ADAM_WD = 0.01
ADAM_STEP = 10

VMEM_LIMIT_BYTES = 56 * 1024 * 1024

NT = (((1,), (1,)), ((), ()))
TN = (((0,), (0,)), ((), ()))
MESH_ID = pl.DeviceIdType.MESH


def _params(*sem):
    return pltpu.CompilerParams(dimension_semantics=sem, vmem_limit_bytes=VMEM_LIMIT_BYTES)


def _const_spec(shape):
    nd = len(shape)
    return pl.BlockSpec(shape, lambda *_: (0,) * nd, pipeline_mode=pl.Buffered(1))


def _sigmoid(v):
    return 0.5 * jnp.tanh(0.5 * v) + 0.5


def _silu(v):
    return v * _sigmoid(v)


def _silu_and_grad(v):
    s = _sigmoid(v)
    return v * s, s * (1.0 + v * (1.0 - s))


ANY = pl.BlockSpec(memory_space=pl.ANY)


def _place():
    x, y, c = lax.axis_index("x"), lax.axis_index("y"), lax.axis_index("c")
    chips = [(1 - x, y), (x, 1 - y), (1 - x, 1 - y)]
    return x, y, c, chips


class _Gather:
    def __init__(self, blocks, halve):
        self.ins = list(blocks)
        self.halve = halve
        self.n = n = len(blocks)
        self.shapes = [((b.shape[0] // 2) if halve else b.shape[0], b.shape[1]) for b in blocks]
        self.out_shape = [jax.ShapeDtypeStruct((N_DEV, r, cols), b.dtype) for (r, cols), b in zip(self.shapes, blocks)]
        self.scratch = [pltpu.SemaphoreType.DMA((7 * n,)), pltpu.SemaphoreType.DMA((7 * n,)),
                        pltpu.SemaphoreType.DMA((n,))]

    def _copies(self, ins, outs, sems):
        send_sems, recv_sems, local_sems = sems
        x, y, c, chips = _place()
        me, sibling = (x, y, c), (x, y, 1 - c)

        def piece(t, px, py, pc):
            return outs[t].at[4 * px + 2 * py + pc]

        def own(t):
            return ins[t].at[pl.ds(c * self.shapes[t][0], self.shapes[t][0])] if self.halve else ins[t]

        def copy(t, k, block, to, src=None):
            return pltpu.make_async_remote_copy(
                src_ref=piece(t, *block) if src is None else src, dst_ref=piece(t, *block),
                send_sem=send_sems.at[7 * t + k], recv_sem=recv_sems.at[7 * t + k],
                device_id=to, device_id_type=MESH_ID)

        rng = range(self.n)
        return dict(
            mine=[pltpu.make_async_copy(own(t), piece(t, *me), local_sems.at[t]) for t in rng],
            first=[copy(t, 0, me, sibling, src=own(t)) for t in rng]
            + [copy(t, 1 + j, me, (*chip, c), src=own(t)) for t in rng for j, chip in enumerate(chips)],
            landed=[copy(t, 1 + j, (*chip, c), me) for j, chip in enumerate(chips) for t in rng],
            passed=[copy(t, 4 + j, (*chip, c), sibling) for j, chip in enumerate(chips) for t in rng],
            from_sibling=[copy(t, 0, sibling, me) for t in rng]
            + [copy(t, 4 + j, (*chip, 1 - c), me) for t in rng for j, chip in enumerate(chips)])

    def start(self, ins, outs, sems):
        d = self._copies(ins, outs, sems)
        for cp in d["mine"] + d["first"]:
            cp.start()

    def middle(self, ins, outs, sems):
        d = self._copies(ins, outs, sems)
        for got, fwd in zip(d["landed"], d["passed"]):
            got.wait_recv()
            fwd.start()

    def finish(self, ins, outs, sems):
        d = self._copies(ins, outs, sems)
        for cp in d["from_sibling"]:
            cp.wait_recv()
        for cp in d["first"] + d["passed"]:
            cp.wait_send()
        for cp in d["mine"]:
            cp.wait()


class _Scatter:
    def __init__(self, tensors):
        self.ins = list(tensors)
        self.n = n = len(tensors)
        self.out_shape = [jax.ShapeDtypeStruct(t.shape, t.dtype) for t in tensors]
        self.scratch = [pltpu.SemaphoreType.DMA((7 * n,)), pltpu.SemaphoreType.DMA((7 * n,))]

    def _copies(self, ins, outs, sems):
        send_sems, recv_sems = sems
        x, y, c, _ = _place()
        me = 4 * x + 2 * y + c
        sends, recvs = [], []
        for t in range(self.n):
            for m in range(1, N_DEV):
                px, py, pc = x ^ (m >> 2), y ^ ((m >> 1) & 1), c ^ (m & 1)
                q = 4 * px + 2 * py + pc
                sems_k = dict(send_sem=send_sems.at[7 * t + m - 1], recv_sem=recv_sems.at[7 * t + m - 1],
                              device_id=(px, py, pc), device_id_type=MESH_ID)
                sends.append(pltpu.make_async_remote_copy(src_ref=ins[t].at[q], dst_ref=outs[t].at[me], **sems_k))
                recvs.append(pltpu.make_async_remote_copy(src_ref=ins[t].at[me], dst_ref=outs[t].at[q], **sems_k))
        return sends, recvs

    def start(self, ins, outs, sems):
        for cp in self._copies(ins, outs, sems)[0]:
            cp.start()

    def middle(self, ins, outs, sems):
        pass

    def finish(self, ins, outs, sems):
        sends, recvs = self._copies(ins, outs, sems)
        for cp in recvs:
            cp.wait_recv()
        for cp in sends:
            cp.wait_send()


class _Comms:
    def __init__(self, *comms):
        self.comms = comms
        self.ins = [a for c in comms for a in c.ins]
        self.out_shape = [s for c in comms for s in c.out_shape]
        self.scratch = [s for c in comms for s in c.scratch]

    def _each(self, phase, ins, outs, sems):
        i = o = s = 0
        for c in self.comms:
            ni, no, ns = len(c.ins), len(c.out_shape), len(c.scratch)
            getattr(c, phase)(ins[i:i + ni], outs[o:o + no], sems[s:s + ns])
            i, o, s = i + ni, o + no, s + ns

    def start(self, ins, outs, sems):
        self._each("start", ins, outs, sems)

    def middle(self, ins, outs, sems):
        self._each("middle", ins, outs, sems)

    def finish(self, ins, outs, sems):
        self._each("finish", ins, outs, sems)


def _run_comm(comm, name):
    n = len(comm.ins)

    def body(*refs):
        parts = refs[:n], refs[n:2 * n], refs[2 * n:]
        comm.start(*parts)
        comm.middle(*parts)
        comm.finish(*parts)

    return pl.pallas_call(body, name=name, out_shape=comm.out_shape, in_specs=[ANY] * n, out_specs=[ANY] * n,
                          scratch_shapes=comm.scratch)(*comm.ins)


HBM_SPEC = pl.BlockSpec(memory_space=pltpu.HBM)
SEM_SPEC = pl.BlockSpec(memory_space=pltpu.SEMAPHORE)
DATAFLOW = pltpu.SideEffectType.DATAFLOW_SIDE_EFFECTING


def _scatter_copies(own_ref, land_ref, send_sems, recv_sems):
    x, y, c, _ = _place()
    me = 4 * x + 2 * y + c
    pairs = []
    for m in range(1, N_DEV):
        px, py, pc = x ^ (m >> 2), y ^ ((m >> 1) & 1), c ^ (m & 1)
        q = 4 * px + 2 * py + pc
        sems = dict(send_sem=send_sems.at[m - 1], recv_sem=recv_sems.at[m - 1], device_id=(px, py, pc),
                    device_id_type=MESH_ID)
        pairs.append((pltpu.make_async_remote_copy(src_ref=own_ref.at[q], dst_ref=land_ref.at[me], **sems),
                      pltpu.make_async_remote_copy(src_ref=own_ref.at[me], dst_ref=land_ref.at[q], **sems)))
    return pairs


def _scatter_start(own, name):
    def body(own_ref, land_ref, send_sems, recv_sems, own_thru, land_thru, token):
        for send, _ in _scatter_copies(own_ref, land_ref, send_sems, recv_sems):
            send.start()
        token[...] = jnp.zeros_like(token)

    buf = pltpu.HBM(own.shape, own.dtype)
    return pl.pallas_call(
        body, name=name,
        out_shape=(pltpu.SemaphoreType.DMA((N_DEV - 1,)), pltpu.SemaphoreType.DMA((N_DEV - 1,)), buf, buf,
                   jax.ShapeDtypeStruct((8, 128), F32)),
        in_specs=(HBM_SPEC, HBM_SPEC),
        out_specs=(SEM_SPEC, SEM_SPEC, HBM_SPEC, HBM_SPEC, pl.BlockSpec(memory_space=pltpu.VMEM)),
        input_output_aliases={0: 2, 1: 3},
        compiler_params=pltpu.CompilerParams(has_side_effects=DATAFLOW),
    )(pltpu.with_memory_space_constraint(own, pltpu.HBM),
      pltpu.with_memory_space_constraint(lax.empty(own.shape, own.dtype), pltpu.HBM))


def _scatter_wait(send_sems, recv_sems, own_thru, land_thru, after, name):
    def body(own_ref, land_ref, send_sems, recv_sems, after_ref, own_out, land_out):
        for send, recv in _scatter_copies(own_ref, land_ref, send_sems, recv_sems):
            send.wait_send()
            recv.wait_recv()

    buf = pltpu.HBM(own_thru.shape, own_thru.dtype)
    return pl.pallas_call(
        body, name=name, out_shape=(buf, buf),
        in_specs=(HBM_SPEC, HBM_SPEC, SEM_SPEC, SEM_SPEC, ANY), out_specs=(HBM_SPEC, HBM_SPEC),
        input_output_aliases={0: 0, 1: 1},
        compiler_params=pltpu.CompilerParams(has_side_effects=DATAFLOW),
    )(own_thru, land_thru, send_sems, recv_sems, after)


def _fused_call(body, comm, args, *, name, grid, out_shape, in_specs, out_specs, scratch_shapes=(), params):
    single = not isinstance(out_shape, (list, tuple))
    out_shape = [out_shape] if single else list(out_shape)
    out_specs = [out_specs] if single else list(out_specs)
    if comm is None:
        res = pl.pallas_call(body, name=name, grid=grid, out_shape=out_shape, in_specs=in_specs, out_specs=out_specs,
                             scratch_shapes=list(scratch_shapes), compiler_params=params)(*args)
        return (res[0] if single else res), []
    n_in, n_out, n_scr = len(in_specs), len(out_shape), len(scratch_shapes)
    c_in, c_out = len(comm.ins), len(comm.out_shape)
    steps = grid[0]

    def fused(*refs):
        pos = 0
        groups = []
        for size in (n_in, c_in, n_out, c_out, n_scr, len(comm.scratch)):
            groups.append(refs[pos:pos + size])
            pos += size
        ins, c_ins, outs, c_outs, scr, c_sems = groups
        i = pl.program_id(0)

        @pl.when(i == 0)
        def _():
            comm.start(c_ins, c_outs, c_sems)

        @pl.when(i == steps // 2)
        def _():
            comm.middle(c_ins, c_outs, c_sems)

        body(*ins, *outs, *scr)

        @pl.when(i == steps - 1)
        def _():
            comm.finish(c_ins, c_outs, c_sems)

    res = pl.pallas_call(
        fused, name=name, grid=grid, out_shape=out_shape + list(comm.out_shape),
        in_specs=list(in_specs) + [ANY] * c_in, out_specs=out_specs + [ANY] * c_out,
        scratch_shapes=list(scratch_shapes) + list(comm.scratch), compiler_params=params)(*args, *comm.ins)
    main = res[:n_out]
    return (main[0] if single else main), list(res[n_out:])


def _norm_matmul(x, gain, wt, name, comm=None, tm=512):
    t, n = x.shape[0], wt.shape[0]

    def body(x_ref, g_ref, wt_ref, o_ref):
        xv = x_ref[...]
        r = lax.rsqrt(jnp.mean(xv * xv, axis=-1, keepdims=True) + EPS)
        h = (xv * r * g_ref[...]).astype(BF16)
        o_ref[...] = lax.dot_general(h, wt_ref[...], NT, preferred_element_type=F32)

    return _fused_call(
        body, comm, (x, gain, wt), name=name, grid=(t // tm,),
        out_shape=jax.ShapeDtypeStruct((t, n), F32),
        in_specs=[pl.BlockSpec((tm, D_MODEL), lambda i: (i, 0)), _const_spec((1, D_MODEL)), _const_spec((n, D_MODEL))],
        out_specs=pl.BlockSpec((tm, n), lambda i: (i, 0)),
        params=_params("arbitrary"))


def _project_out(a, w_ref, x_ref, p_ref, y_ref):
    y = jnp.dot(a, w_ref[...], preferred_element_type=F32)
    y_ref[...] = y
    ry = lax.rsqrt(jnp.mean(y * y, axis=-1, keepdims=True) + EPS)
    return x_ref[...] + (y * ry) * p_ref[...]


def _post_bwd_rows(g, y, a, n_own, first, last, p_ref, w_ref, dw_ref, dw16_ref, dp_ref):
    @pl.when(first)
    def _():
        dw_ref[...] = jnp.zeros_like(dw_ref)
        dp_ref[...] = jnp.zeros_like(dp_ref)

    ry = lax.rsqrt(jnp.mean(y * y, axis=-1, keepdims=True) + EPS)
    nv = y * ry
    dp_ref[...] += jnp.sum((g * nv)[0:n_own], axis=0, keepdims=True)
    dn = g * p_ref[...]
    dy = (ry * (dn - nv * jnp.mean(dn * nv, axis=-1, keepdims=True))).astype(BF16)
    dw_ref[...] += lax.dot_general(a, dy[0:n_own], TN, preferred_element_type=F32)

    @pl.when(last)
    def _():
        dw16_ref[...] = dw_ref[...].astype(BF16)

    return lax.dot_general(dy, w_ref[...], NT, preferred_element_type=F32)


def _pre_bwd(dproj, wt, x_in, pre, g, name, comm=None, tm=512):
    t, n = dproj.shape
    steps = t // tm

    def body(dp_ref, wt_ref, x_ref, pre_ref, g_ref, dx_ref, dwt16_ref, dpre_ref, dwt_ref):
        @pl.when(pl.program_id(0) == 0)
        def _():
            dwt_ref[...] = jnp.zeros_like(dwt_ref)
            dpre_ref[...] = jnp.zeros_like(dpre_ref)

        dpv = dp_ref[...]
        dh = jnp.dot(dpv, wt_ref[...], preferred_element_type=F32)
        xv = x_ref[...]
        r = lax.rsqrt(jnp.mean(xv * xv, axis=-1, keepdims=True) + EPS)
        xn = xv * r
        pv = pre_ref[...]
        dpre_ref[...] += jnp.sum(dh * xn, axis=0, keepdims=True)
        dxn = dh * pv
        dx_ref[...] = g_ref[...] + r * (dxn - xn * jnp.mean(dxn * xn, axis=-1, keepdims=True))
        h = (xn * pv).astype(BF16)
        dwt_ref[...] += lax.dot_general(dpv, h, TN, preferred_element_type=F32)

        @pl.when(pl.program_id(0) == steps - 1)
        def _():
            dwt16_ref[...] = dwt_ref[...].astype(BF16)

    row = pl.BlockSpec((tm, D_MODEL), lambda i: (i, 0))
    return _fused_call(
        body, comm, (dproj, wt, x_in, pre, g), name=name, grid=(steps,),
        out_shape=[jax.ShapeDtypeStruct((t, D_MODEL), F32), jax.ShapeDtypeStruct((n, D_MODEL), BF16),
                   jax.ShapeDtypeStruct((1, D_MODEL), F32)],
        in_specs=[pl.BlockSpec((tm, n), lambda i: (i, 0)), _const_spec((n, D_MODEL)), row, _const_spec((1, D_MODEL)), row],
        out_specs=[row, _const_spec((n, D_MODEL)), pl.BlockSpec((1, D_MODEL), lambda i: (0, 0))],
        scratch_shapes=[pltpu.VMEM((n, D_MODEL), F32)],
        params=_params("arbitrary"))


def _proj_dw(dproj, x_in, pre, name, comm=None, tm=1024):
    t, n = dproj.shape
    steps = t // tm

    def body(dp_ref, x_ref, pre_ref, dwt16_ref, dwt_ref):
        @pl.when(pl.program_id(0) == 0)
        def _():
            dwt_ref[...] = jnp.zeros_like(dwt_ref)

        xv = x_ref[...]
        r = lax.rsqrt(jnp.mean(xv * xv, axis=-1, keepdims=True) + EPS)
        h = (xv * r * pre_ref[...]).astype(BF16)
        dwt_ref[...] += lax.dot_general(dp_ref[...], h, TN, preferred_element_type=F32)

        @pl.when(pl.program_id(0) == steps - 1)
        def _():
            dwt16_ref[...] = dwt_ref[...].astype(BF16)

    return _fused_call(
        body, comm, (dproj, x_in, pre), name=name, grid=(steps,),
        out_shape=jax.ShapeDtypeStruct((n, D_MODEL), BF16),
        in_specs=[pl.BlockSpec((tm, n), lambda i: (i, 0)), pl.BlockSpec((tm, D_MODEL), lambda i: (i, 0)),
                  _const_spec((1, D_MODEL))],
        out_specs=pl.BlockSpec((n, D_MODEL), lambda i: (0, 0)),
        scratch_shapes=[pltpu.VMEM((n, D_MODEL), F32)],
        params=_params("arbitrary"))


def _proj_dx(dproj, wt, x_in, pre, g, name, comm=None, tm=512):
    t, n = dproj.shape

    def body(dp_ref, wt_ref, x_ref, pre_ref, g_ref, dx_ref, dpre_ref):
        @pl.when(pl.program_id(0) == 0)
        def _():
            dpre_ref[...] = jnp.zeros_like(dpre_ref)

        dh = jnp.dot(dp_ref[...], wt_ref[...], preferred_element_type=F32)
        xv = x_ref[...]
        r = lax.rsqrt(jnp.mean(xv * xv, axis=-1, keepdims=True) + EPS)
        xn = xv * r
        dpre_ref[...] += jnp.sum(dh * xn, axis=0, keepdims=True)
        dxn = dh * pre_ref[...]
        dx_ref[...] = g_ref[...] + r * (dxn - xn * jnp.mean(dxn * xn, axis=-1, keepdims=True))

    row = pl.BlockSpec((tm, D_MODEL), lambda i: (i, 0))
    return _fused_call(
        body, comm, (dproj, wt, x_in, pre, g), name=name, grid=(t // tm,),
        out_shape=[jax.ShapeDtypeStruct((t, D_MODEL), F32), jax.ShapeDtypeStruct((1, D_MODEL), F32)],
        in_specs=[pl.BlockSpec((tm, n), lambda i: (i, 0)), _const_spec((n, D_MODEL)), row, _const_spec((1, D_MODEL)), row],
        out_specs=[row, pl.BlockSpec((1, D_MODEL), lambda i: (0, 0))],
        params=_params("arbitrary"))


def _group_masks():
    lane = lax.broadcasted_iota(jnp.int32, (1, GROUP * HEAD_DIM), 1)
    return [(lane // HEAD_DIM == g).astype(F32) for g in range(GROUP)]


def _stack_groups(v, masks):
    return jnp.concatenate([v * m for m in masks], axis=0)


def _unstack_groups(v, masks):
    out = v[0:BLOCK] * masks[0]
    for g in range(1, GROUP):
        out = out + v[g * BLOCK:(g + 1) * BLOCK] * masks[g]
    return out


def _repeat_head(kv2, kvh):
    first = lax.broadcasted_iota(jnp.int32, kv2.shape, 1) < HEAD_DIM
    rolled = pltpu.roll(kv2, HEAD_DIM, 1)
    one = jnp.where(first, kv2, rolled) if kvh == 0 else jnp.where(first, rolled, kv2)
    return jnp.concatenate([one, one], axis=1)


def _fold_head(v4):
    a = v4[:, 0:128] + v4[:, 128:256]
    return a + pltpu.roll(a, HEAD_DIM, 1)


ATTN_CONSTS = [pltpu.VMEM((KV_HEADS, GROUP * BLOCK, 2 * BLOCK), F32)]


def _fill_attn_bias(bias_ref):
    row = lax.broadcasted_iota(jnp.int32, (GROUP * BLOCK, 2 * BLOCK), 0)
    col = lax.broadcasted_iota(jnp.int32, (GROUP * BLOCK, 2 * BLOCK), 1)
    dist = (row % BLOCK) + BLOCK - col
    band = (dist >= 0) & (dist < BLOCK)
    rb = lax.broadcasted_iota(jnp.int32, (GROUP * BLOCK, 1), 0) // BLOCK
    for kvh in range(KV_HEADS):
        slope = jnp.zeros((GROUP * BLOCK, 1), F32)
        for g in range(GROUP):
            slope = jnp.where(rb == g, 2.0 ** (-(kvh * GROUP + g + 1)), slope)
        bias_ref[kvh] = jnp.where(band, -slope * dist.astype(F32), NEG)


def _row_sinks(kvh, sink_ref):
    rb = lax.broadcasted_iota(jnp.int32, (GROUP * BLOCK, 1), 0) // BLOCK
    sink = jnp.zeros((GROUP * BLOCK, 1), F32)
    for g in range(GROUP):
        sink = jnp.where(rb == g, sink_ref[0, kvh * GROUP + g], sink)
    return sink


def _attn_probs(qk, k4, bias, sink, no_past, masks):
    qs = _stack_groups(qk, masks).astype(BF16)
    s = lax.dot_general(qs, k4, NT, preferred_element_type=F32) * (HEAD_DIM ** -0.5) + bias
    s = jnp.concatenate([jnp.where(no_past, NEG, s[:, 0:BLOCK]), s[:, BLOCK:]], axis=1)
    mx = jnp.maximum(jnp.max(s, axis=-1, keepdims=True), sink)
    e = jnp.exp(s - mx)
    es = jnp.exp(sink - mx)
    inv = 1.0 / (jnp.sum(e, axis=-1, keepdims=True) + es)
    return qs, e * inv, es * inv


def _pool_forward(u_ext, g, t0):
    n = u_ext.shape[0] - POOL_HALO
    s = u_ext
    for step in range(g + 1):
        s = s + pltpu.roll(s, 1 << step, 0)
    w = 2 << g
    t = t0 + lax.broadcasted_iota(jnp.int32, (n, 1), 0)
    cnt = jnp.minimum(t + 1, w).astype(F32)
    return s[POOL_HALO:] / cnt - u_ext[POOL_HALO:]


def _layer0_fwd(proj, sinks, pool_w, pool_scale, w_out, x_in, post, comm=None, tq=512):
    t = proj.shape[0]
    nblk = tq // BLOCK

    def body(main_ref, halo_ref, sink_ref, pw_ref, ps_ref, w_ref, x_ref, p_ref, o_ref, y_ref, xo_ref, kv_ref, bias_ref):
        i = pl.program_id(0)
        t0 = i * tq
        masks = _group_masks()

        @pl.when(i == 0)
        def _():
            _fill_attn_bias(bias_ref)

        kv_ref[0:BLOCK, :] = halo_ref[:, COL_K:COL_K + 256]
        kv_ref[BLOCK:, :] = main_ref[:, COL_K:COL_K + 256]

        def block(jb, carry):
            r0 = pl.multiple_of(jb * BLOCK, BLOCK)
            no_past = t0 + r0 == 0
            q = main_ref[pl.ds(r0, BLOCK), COL_Q:COL_Q + 512]
            ga = main_ref[pl.ds(r0, BLOCK), COL_GA:COL_GA + 512]
            kk = kv_ref[pl.ds(r0, 2 * BLOCK), 0:128]
            vv = kv_ref[pl.ds(r0, 2 * BLOCK), 128:256]
            outs = []
            for kvh in range(KV_HEADS):
                k4 = _repeat_head(kk, kvh).astype(BF16)
                v4 = _repeat_head(vv, kvh).astype(BF16)
                _, p, _ = _attn_probs(q[:, kvh * 256:(kvh + 1) * 256], k4, bias_ref[kvh], _row_sinks(kvh, sink_ref), no_past, masks)
                pv = jnp.dot(p.astype(BF16), v4, preferred_element_type=F32)
                outs.append(_unstack_groups(pv, masks))
            attn = jnp.concatenate(outs, axis=1)
            o_ref[pl.ds(r0, BLOCK), 0:512] = (attn * _silu(ga)).astype(BF16)
            return carry

        lax.fori_loop(0, nblk, block, 0, unroll=2)

        for g in range(POOL_GROUPS):
            cu = COL_U + g * POOL_GC
            cg = COL_GB + g * POOL_GC
            halo_u = jnp.where(i == 0, 0.0, halo_ref[BLOCK - POOL_HALO:BLOCK, cu:cu + POOL_GC])
            u_ext = jnp.concatenate([halo_u, main_ref[:, cu:cu + POOL_GC]], axis=0)
            pooled = _pool_forward(u_ext, g, t0)
            y = jnp.dot(pooled.astype(BF16), pw_ref[g].astype(BF16), preferred_element_type=F32)
            y = y * ps_ref[:, g * POOL_GC:(g + 1) * POOL_GC]
            o_ref[:, 512 + g * POOL_GC:512 + (g + 1) * POOL_GC] = (y * _silu(main_ref[:, cg:cg + POOL_GC])).astype(BF16)

        xo_ref[...] = _project_out(o_ref[...], w_ref, x_ref, p_ref, y_ref)

    row = pl.BlockSpec((tq, D_MODEL), lambda i: (i, 0))
    return _fused_call(
        body, comm, (proj, proj, sinks, pool_w, pool_scale, w_out, x_in, post), name="layer0_fwd", grid=(t // tq,),
        out_shape=[jax.ShapeDtypeStruct((t, D_MODEL), BF16), jax.ShapeDtypeStruct((t, D_MODEL), F32),
                   jax.ShapeDtypeStruct((t, D_MODEL), F32)],
        in_specs=[pl.BlockSpec((tq, EVEN_IN), lambda i: (i, 0)),
                  pl.BlockSpec((BLOCK, EVEN_IN), lambda i: (jnp.maximum(i * nblk - 1, 0), 0)),
                  pl.BlockSpec(memory_space=pltpu.SMEM),
                  _const_spec((POOL_GROUPS, POOL_GC, POOL_GC)), _const_spec((1, 512)),
                  _const_spec((D_MODEL, D_MODEL)), row, _const_spec((1, D_MODEL))],
        out_specs=[row, row, row],
        scratch_shapes=[pltpu.VMEM((tq + BLOCK, 256), F32)] + ATTN_CONSTS,
        params=_params("arbitrary"))


def _layer0_bwd(proj, gy, y, mix, w_out, post, sinks, pool_w, pool_scale, comm=None, tq=512):
    t = proj.shape[0]
    nt = t // tq
    nblk = tq // BLOCK

    def body(main_ref, halo_ref, next_ref, gy_ref, gyn_ref, y_ref, yn_ref, mix_ref, wo_ref, po_ref,
             sink_ref, pw_ref, ps_ref,
             o_ref, dsk_ref, dpw_ref, dps_ref, dwo16_ref, dpo_ref,
             kv_ref, dkv_ref, carry_ref, bias_ref, dwo_ref, dmix_ref):
        i = pl.program_id(0)
        ii = nt - 1 - i
        t0 = ii * tq
        masks = _group_masks()

        @pl.when(i == 0)
        def _():
            _fill_attn_bias(bias_ref)
            dsk_ref[...] = jnp.zeros_like(dsk_ref)
            dpw_ref[...] = jnp.zeros_like(dpw_ref)
            dps_ref[...] = jnp.zeros_like(dps_ref)
            carry_ref[...] = jnp.zeros_like(carry_ref)

        dmix_ref[...] = _post_bwd_rows(jnp.concatenate([gy_ref[...], gyn_ref[...]], axis=0),
                                       jnp.concatenate([y_ref[...], yn_ref[...]], axis=0), mix_ref[...], tq,
                                       i == 0, i == nt - 1, po_ref, wo_ref, dwo_ref, dwo16_ref, dpo_ref)
        dm_ref = dmix_ref.at[pl.ds(0, tq)]
        dmn_ref = dmix_ref.at[pl.ds(tq, POOL_HALO)]

        kv_ref[0:BLOCK, :] = halo_ref[:, COL_K:COL_K + 256]
        kv_ref[BLOCK:, :] = main_ref[:, COL_K:COL_K + 256]
        dkv_ref[0:tq, :] = jnp.zeros((tq, 256), F32)
        dkv_ref[tq:, :] = carry_ref[...]

        def block(jb, carry):
            r0 = pl.multiple_of(jb * BLOCK, BLOCK)
            no_past = t0 + r0 == 0
            q = main_ref[pl.ds(r0, BLOCK), COL_Q:COL_Q + 512]
            ga = main_ref[pl.ds(r0, BLOCK), COL_GA:COL_GA + 512]
            dya = dm_ref[pl.ds(r0, BLOCK), 0:512]
            kk = kv_ref[pl.ds(r0, 2 * BLOCK), 0:128]
            vv = kv_ref[pl.ds(r0, 2 * BLOCK), 128:256]
            silu_ga, dsilu_ga = _silu_and_grad(ga)
            do = dya * silu_ga
            first = lax.broadcasted_iota(jnp.int32, (2 * BLOCK, 128), 1) < HEAD_DIM
            attn, dq, dk, dv = [], [], [], []
            for kvh in range(KV_HEADS):
                k4 = _repeat_head(kk, kvh).astype(BF16)
                v4 = _repeat_head(vv, kvh).astype(BF16)
                qs, p, ps = _attn_probs(q[:, kvh * 256:(kvh + 1) * 256], k4, bias_ref[kvh], _row_sinks(kvh, sink_ref), no_past, masks)
                pb = p.astype(BF16)
                o_k = _unstack_groups(jnp.dot(pb, v4, preferred_element_type=F32), masks)
                do_k = do[:, kvh * 256:(kvh + 1) * 256]
                dos = _stack_groups(do_k, masks).astype(BF16)
                prod = do_k * o_k
                delta = jnp.concatenate([jnp.sum(prod * m, axis=-1, keepdims=True) for m in masks], axis=0)
                dp = lax.dot_general(dos, v4, NT, preferred_element_type=F32)
                ds = (p * (dp - delta)).astype(BF16)
                sink_term = ps * delta
                for g in range(GROUP):
                    h = kvh * GROUP + g
                    dsk_ref[h:h + 1, :] -= jnp.sum(sink_term[g * BLOCK:(g + 1) * BLOCK], keepdims=True)
                scale = HEAD_DIM ** -0.5
                dq.append(_unstack_groups(jnp.dot(ds, k4, preferred_element_type=F32), masks) * scale)
                dk.append(_fold_head(lax.dot_general(ds, qs, TN, preferred_element_type=F32)) * scale)
                dv.append(_fold_head(lax.dot_general(pb, dos, TN, preferred_element_type=F32)))
                attn.append(o_k)
            o_ref[pl.ds(r0, BLOCK), COL_Q:COL_Q + 512] = jnp.concatenate(dq, axis=1).astype(BF16)
            o_all = jnp.concatenate(attn, axis=1)
            o_ref[pl.ds(r0, BLOCK), COL_GA:COL_GA + 512] = (dya * o_all * dsilu_ga).astype(BF16)
            dkv = jnp.concatenate([jnp.where(first, dk[0], dk[1]), jnp.where(first, dv[0], dv[1])], axis=1)
            dkv_ref[pl.ds(r0, 2 * BLOCK), :] += dkv
            return carry

        lax.fori_loop(0, nblk, block, 0, unroll=2)
        carry_ref[...] = dkv_ref[0:BLOCK, :]
        o_ref[:, COL_K:COL_K + 256] = dkv_ref[BLOCK:, :].astype(BF16)

        last = ii == nt - 1
        for g in range(POOL_GROUPS):
            cu = COL_U + g * POOL_GC
            cg = COL_GB + g * POOL_GC
            cm = 512 + g * POOL_GC
            pw = pw_ref[g].astype(BF16)
            sc = ps_ref[:, g * POOL_GC:(g + 1) * POOL_GC]
            halo_u = jnp.where(ii == 0, 0.0, halo_ref[BLOCK - POOL_HALO:BLOCK, cu:cu + POOL_GC])
            u_ext = jnp.concatenate([halo_u, main_ref[:, cu:cu + POOL_GC]], axis=0)
            pooled = _pool_forward(u_ext, g, t0).astype(BF16)
            y_raw = jnp.dot(pooled, pw, preferred_element_type=F32)
            gb = main_ref[:, cg:cg + POOL_GC]
            dyb = dm_ref[:, cm:cm + POOL_GC]
            silu_gb, dsilu_gb = _silu_and_grad(gb)
            dypool = dyb * silu_gb
            dps_ref[:, g * POOL_GC:(g + 1) * POOL_GC] += jnp.sum(dypool * y_raw, axis=0, keepdims=True)
            o_ref[:, cg:cg + POOL_GC] = (dyb * (y_raw * sc) * dsilu_gb).astype(BF16)
            dyraw = dypool * sc
            dyraw_n = jnp.where(last, 0.0, dmn_ref[:, cm:cm + POOL_GC] * _silu(next_ref[:, cg:cg + POOL_GC]) * sc)
            dpw_ref[g * POOL_GC:(g + 1) * POOL_GC, :] += lax.dot_general(pooled, dyraw.astype(BF16), TN,
                                                                         preferred_element_type=F32)
            dyraw_ext = jnp.concatenate([dyraw, dyraw_n], axis=0).astype(BF16)
            dpooled = lax.dot_general(dyraw_ext, pw, NT, preferred_element_type=F32)
            w = 2 << g
            tt = t0 + lax.broadcasted_iota(jnp.int32, (tq + POOL_HALO, 1), 0)
            s = dpooled / jnp.minimum(tt + 1, w).astype(F32)
            for step in range(g + 1):
                s = s + pltpu.roll(s, tq + POOL_HALO - (1 << step), 0)
            o_ref[:, cu:cu + POOL_GC] = (s[0:tq] - dpooled[0:tq]).astype(BF16)

    rev = lambda i: nt - 1 - i
    nxt = lambda i: (jnp.minimum((rev(i) + 1) * (tq // POOL_HALO), t // POOL_HALO - 1), 0)
    row = pl.BlockSpec((tq, D_MODEL), lambda i: (rev(i), 0))
    nxt_row = pl.BlockSpec((POOL_HALO, D_MODEL), nxt)
    square = _const_spec((D_MODEL, D_MODEL))
    return _fused_call(
        body, comm, (proj, proj, proj, gy, gy, y, y, mix, w_out, post, sinks, pool_w, pool_scale),
        name="layer0_bwd", grid=(nt,),
        out_shape=[jax.ShapeDtypeStruct((t, EVEN_IN), BF16), jax.ShapeDtypeStruct((8, 128), F32),
                   jax.ShapeDtypeStruct((POOL_GROUPS * POOL_GC, POOL_GC), F32), jax.ShapeDtypeStruct((1, 512), F32),
                   jax.ShapeDtypeStruct((D_MODEL, D_MODEL), BF16), jax.ShapeDtypeStruct((1, D_MODEL), F32)],
        in_specs=[pl.BlockSpec((tq, EVEN_IN), lambda i: (rev(i), 0)),
                  pl.BlockSpec((BLOCK, EVEN_IN), lambda i: (jnp.maximum(rev(i) * nblk - 1, 0), 0)),
                  pl.BlockSpec((POOL_HALO, EVEN_IN), nxt),
                  row, nxt_row, row, nxt_row, row, square, _const_spec((1, D_MODEL)),
                  pl.BlockSpec(memory_space=pltpu.SMEM),
                  _const_spec((POOL_GROUPS, POOL_GC, POOL_GC)), _const_spec((1, 512))],
        out_specs=[pl.BlockSpec((tq, EVEN_IN), lambda i: (rev(i), 0)),
                   pl.BlockSpec((8, 128), lambda i: (0, 0)),
                   pl.BlockSpec((POOL_GROUPS * POOL_GC, POOL_GC), lambda i: (0, 0)),
                   pl.BlockSpec((1, 512), lambda i: (0, 0)), square, pl.BlockSpec((1, D_MODEL), lambda i: (0, 0))],
        scratch_shapes=[pltpu.VMEM((tq + BLOCK, 256), F32), pltpu.VMEM((tq + BLOCK, 256), F32),
                        pltpu.VMEM((BLOCK, 256), F32)] + ATTN_CONSTS
        + [pltpu.VMEM((D_MODEL, D_MODEL), F32), pltpu.VMEM((tq + POOL_HALO, D_MODEL), F32)],
        params=_params("arbitrary"))


CONV_RC = 32
CONV_CC = 128
CONV_CHAINS = 4
CONV_UNROLL = 2


def _fill_shifted(s_ref, rows):
    for b in range(1, 8):
        s_ref[b, 0:rows - 8, :] = s_ref[0, b:b + rows - 8, :]


def _tap_blocks(s_ref, r, cols, lead):
    for b in range(8):
        taps = [(a, 8 * a + b - lead) for a in range(5) if 0 <= 8 * a + b - lead < CONV_K]
        span = 8 * max(a for a, _ in taps) + CONV_RC
        blk = s_ref[b, pl.ds(r, span), cols]
        for a, k in taps:
            yield k, blk[8 * a:8 * a + CONV_RC]


def _conv_taps(s_ref, w_ref, r, cols, lead, reverse):
    accs = [None] * CONV_CHAINS
    for n, (k, blk) in enumerate(_tap_blocks(s_ref, r, cols, lead)):
        kw = CONV_K - 1 - k if reverse else k
        term = blk * w_ref[kw:kw + 1, cols]
        accs[n % CONV_CHAINS] = term if accs[n % CONV_CHAINS] is None else accs[n % CONV_CHAINS] + term
    return (accs[0] + accs[1]) + (accs[2] + accs[3])


def _layer_norm_fwd(cf, lng, lnb):
    mu = jnp.mean(cf, axis=-1, keepdims=True)
    xc = cf - mu
    rstd = lax.rsqrt(jnp.mean(xc * xc, axis=-1, keepdims=True) + EPS)
    chat = xc * rstd
    return chat, rstd, chat * lng + lnb


def _layer1_fwd(proj, dw, dwb, lng, lnb, w_out, x_in, post, target, tt=256):
    t = proj.shape[0]
    lead = CONV_HALO - (CONV_K - 1)

    def body(main_ref, halo_ref, w_ref, b_ref, g_ref, lb_ref, wo_ref, x_ref, p_ref, t_ref,
             o_ref, c_ref, y_ref, dl_ref, l_ref, gs_ref):
        i = pl.program_id(0)
        hv = halo_ref[...]
        gs_ref[0, 0:CONV_HALO, :] = jnp.where(i == 0, 0.0, hv[:, 0:1024] * _sigmoid(hv[:, 1024:2048]))
        gs_ref[0, CONV_HALO:CONV_HALO + tt, :] = main_ref[:, 0:1024] * _sigmoid(main_ref[:, 1024:2048])
        _fill_shifted(gs_ref, tt + CONV_HALO)

        for c in range(D_MODEL // CONV_CC):
            cols = slice(c * CONV_CC, (c + 1) * CONV_CC)

            def chunk(j, carry):
                r = pl.multiple_of(j * CONV_RC, CONV_RC)
                c_ref[pl.ds(r, CONV_RC), cols] = _conv_taps(gs_ref, w_ref, r, cols, lead, False) + b_ref[:, cols]
                return carry
            lax.fori_loop(0, tt // CONV_RC, chunk, 0, unroll=CONV_UNROLL)

        _, _, cn = _layer_norm_fwd(c_ref[...], g_ref[...], lb_ref[...])
        o_ref[...] = (_silu(cn) * _silu(main_ref[:, 2048:3072])).astype(BF16)

        d = _project_out(o_ref[...], wo_ref, x_ref, p_ref, y_ref) - t_ref[...]
        dl_ref[...] = d * (1.0 / D_MODEL)

        @pl.when(i == 0)
        def _():
            l_ref[...] = jnp.zeros_like(l_ref)

        l_ref[...] += 0.5 * jnp.sum(jnp.mean(d * d, axis=-1, keepdims=True))

    vec = _const_spec((1, D_MODEL))
    row = pl.BlockSpec((tt, D_MODEL), lambda i: (i, 0))
    f32_rows = jax.ShapeDtypeStruct((t, D_MODEL), F32)
    return pl.pallas_call(
        body, name="layer1_fwd", grid=(t // tt,),
        out_shape=[jax.ShapeDtypeStruct((t, D_MODEL), BF16), f32_rows, f32_rows, f32_rows,
                   jax.ShapeDtypeStruct((8, 128), F32)],
        in_specs=[pl.BlockSpec((tt, 3 * D_MODEL), lambda i: (i, 0)),
                  pl.BlockSpec((CONV_HALO, 3 * D_MODEL), lambda i: (jnp.maximum(i * (tt // CONV_HALO) - 1, 0), 0)),
                  _const_spec((CONV_K, D_MODEL)), vec, vec, vec,
                  _const_spec((D_MODEL, D_MODEL)), row, vec, row],
        out_specs=[row, row, row, row, pl.BlockSpec((8, 128), lambda i: (0, 0))],
        scratch_shapes=[pltpu.VMEM((8, tt + CONV_HALO, D_MODEL), F32)],
        compiler_params=_params("arbitrary"),
    )(proj, proj, dw, dwb, lng, lnb, w_out, x_in, post, target)


def _layer1_bwd(proj, cf, gy, y, z, w_out, post, dw, lng, lnb, comm=None, tt=256):
    t = proj.shape[0]
    nt = t // tt
    te = tt + CONV_HALO

    def body(main_ref, next_ref, cf_ref, cfn_ref, gy_ref, gyn_ref, y_ref, yn_ref, z_ref, wo_ref, po_ref,
             w_ref, g_ref, lb_ref,
             o_ref, ddw_ref, ddb_ref, dg_ref, dlb_ref, dwo16_ref, dpo_ref, ds_ref, glu_ref, sb_ref, dwo_ref):
        i = pl.program_id(0)

        @pl.when(i == 0)
        def _():
            ddw_ref[...] = jnp.zeros_like(ddw_ref)
            ddb_ref[...] = jnp.zeros_like(ddb_ref)
            dg_ref[...] = jnp.zeros_like(dg_ref)
            dlb_ref[...] = jnp.zeros_like(dlb_ref)

        dzv = _post_bwd_rows(jnp.concatenate([gy_ref[...], gyn_ref[...]], axis=0),
                             jnp.concatenate([y_ref[...], yn_ref[...]], axis=0), z_ref[...], tt, i == 0, i == nt - 1,
                             po_ref, wo_ref, dwo_ref, dwo16_ref, dpo_ref)
        dzv = jnp.concatenate([dzv[0:tt], jnp.where(i < nt - 1, dzv[tt:], 0.0)], axis=0)
        lng = g_ref[...]
        chat, rstd, cn = _layer_norm_fwd(jnp.concatenate([cf_ref[...], cfn_ref[...]], axis=0), lng, lb_ref[...])
        gate = jnp.concatenate([main_ref[:, 2048:3072], next_ref[:, 2048:3072]], axis=0)
        silu_cn, dsilu_cn = _silu_and_grad(cn)
        silu_gate, dsilu_gate = _silu_and_grad(gate)
        o_ref[:, 2048:3072] = (dzv * silu_cn * dsilu_gate)[0:tt].astype(BF16)
        dcn = dzv * silu_gate * dsilu_cn
        dg_ref[...] += jnp.sum((dcn * chat)[0:tt], axis=0, keepdims=True)
        dlb_ref[...] += jnp.sum(dcn[0:tt], axis=0, keepdims=True)
        dchat = dcn * lng
        dcf = rstd * (dchat - jnp.mean(dchat, axis=-1, keepdims=True) - chat * jnp.mean(dchat * chat, axis=-1, keepdims=True))
        ddb_ref[...] += jnp.sum(dcf[0:tt], axis=0, keepdims=True)
        ds_ref[0, 0:te, :] = dcf
        ds_ref[0, te:, :] = jnp.zeros((8, D_MODEL), F32)
        _fill_shifted(ds_ref, te + 8)
        sb_ref[...] = _sigmoid(main_ref[:, 1024:2048])
        glu_ref[...] = main_ref[:, 0:1024] * sb_ref[...]

        for c in range(D_MODEL // CONV_CC):
            cols = slice(c * CONV_CC, (c + 1) * CONV_CC)
            gcols = slice(c * CONV_CC + 1024, (c + 1) * CONV_CC + 1024)

            def chunk(j, carry):
                r = pl.multiple_of(j * CONV_RC, CONV_RC)
                dglu = _conv_taps(ds_ref, w_ref, r, cols, 0, True)
                sb = sb_ref[pl.ds(r, CONV_RC), cols]
                o_ref[pl.ds(r, CONV_RC), cols] = (dglu * sb).astype(BF16)
                o_ref[pl.ds(r, CONV_RC), gcols] = (dglu * glu_ref[pl.ds(r, CONV_RC), cols] * (1.0 - sb)).astype(BF16)
                return carry
            lax.fori_loop(0, tt // CONV_RC, chunk, 0, unroll=CONV_UNROLL)

            def taps(j, accs):
                r = pl.multiple_of(j * CONV_RC, CONV_RC)
                gl = glu_ref[pl.ds(r, CONV_RC), cols]
                new = list(accs)
                for m, blk in _tap_blocks(ds_ref, r, cols, 0):
                    prod = blk * gl
                    part = prod[0:8]
                    for q in range(1, CONV_RC // 8):
                        part = part + prod[8 * q:8 * q + 8]
                    new[m] = new[m] + part
                return tuple(new)
            accs = lax.fori_loop(0, tt // CONV_RC, taps, tuple(jnp.zeros((8, CONV_CC), F32) for _ in range(CONV_K)))
            for m in range(CONV_K):
                k = CONV_K - 1 - m
                ddw_ref[k:k + 1, cols] += jnp.sum(accs[m], axis=0, keepdims=True)

    vec = _const_spec((1, D_MODEL))
    vec_out = pl.BlockSpec((1, D_MODEL), lambda i: (0, 0))
    row = pl.BlockSpec((tt, D_MODEL), lambda i: (i, 0))
    nxt = lambda i: (jnp.minimum((i + 1) * (tt // CONV_HALO), t // CONV_HALO - 1), 0)
    nxt_row = pl.BlockSpec((CONV_HALO, D_MODEL), nxt)
    vec_f32 = jax.ShapeDtypeStruct((1, D_MODEL), F32)
    square = _const_spec((D_MODEL, D_MODEL))
    return _fused_call(
        body, comm, (proj, proj, cf, cf, gy, gy, y, y, z, w_out, post, dw, lng, lnb), name="layer1_bwd", grid=(nt,),
        out_shape=[jax.ShapeDtypeStruct((t, 3 * D_MODEL), BF16), jax.ShapeDtypeStruct((CONV_K, D_MODEL), F32),
                   vec_f32, vec_f32, vec_f32, jax.ShapeDtypeStruct((D_MODEL, D_MODEL), BF16), vec_f32],
        in_specs=[pl.BlockSpec((tt, 3 * D_MODEL), lambda i: (i, 0)),
                  pl.BlockSpec((CONV_HALO, 3 * D_MODEL), nxt),
                  row, nxt_row, row, nxt_row, row, nxt_row, row, square, vec,
                  _const_spec((CONV_K, D_MODEL)), vec, vec],
        out_specs=[pl.BlockSpec((tt, 3 * D_MODEL), lambda i: (i, 0)),
                   pl.BlockSpec((CONV_K, D_MODEL), lambda i: (0, 0)), vec_out, vec_out, vec_out, square, vec_out],
        scratch_shapes=[pltpu.VMEM((8, te + 8, D_MODEL), F32), pltpu.VMEM((tt, D_MODEL), F32),
                        pltpu.VMEM((tt, D_MODEL), F32), pltpu.VMEM((D_MODEL, D_MODEL), F32)],
        params=_params("arbitrary"))


def _piece_sum(parts, place, name):
    r = parts[0][0].shape[1]

    def body(p_ref, *refs):
        o_ref = refs[-1]
        acc = refs[0][0].astype(F32)
        for part in refs[1:-1]:
            acc = acc + part[0].astype(F32)
        o_ref[0] = acc

    blk = (1, r, D_MODEL)
    spec = lambda slot: pl.BlockSpec(blk, lambda j, p_ref: (slot(p_ref), 0, 0))
    return pl.pallas_call(
        body, name=name,
        grid_spec=pltpu.PrefetchScalarGridSpec(
            num_scalar_prefetch=1, grid=(1,),
            in_specs=[spec(slot) for _, slot in parts],
            out_specs=pl.BlockSpec(blk, lambda j, p_ref: (p_ref[1], 0, 0))),
        out_shape=jax.ShapeDtypeStruct((2, r, D_MODEL), F32),
        compiler_params=_params("arbitrary"),
    )(place, *[a for a, _ in parts])


def _direct_parts(own, recv):
    peer = lambda m: (lambda p: p[0] ^ m)
    return [(own, peer(0))] + [(recv, peer(m)) for m in range(1, N_DEV)]


def _share_with_sibling(halves, name):
    n = len(halves)

    def body(*refs):
        outs = refs[n:2 * n]
        send_sems, recv_sems = refs[2 * n:]
        x, y, c, _ = _place()
        send = [pltpu.make_async_remote_copy(
            src_ref=outs[t].at[c], dst_ref=outs[t].at[c], send_sem=send_sems.at[t], recv_sem=recv_sems.at[t],
            device_id=(x, y, 1 - c), device_id_type=MESH_ID) for t in range(n)]
        recv = [pltpu.make_async_remote_copy(
            src_ref=outs[t].at[c], dst_ref=outs[t].at[1 - c], send_sem=send_sems.at[t], recv_sem=recv_sems.at[t],
            device_id=(x, y, 1 - c), device_id_type=MESH_ID) for t in range(n)]
        for cp in send:
            cp.start()
        for cp in recv:
            cp.wait_recv()
        for cp in send:
            cp.wait_send()

    return pl.pallas_call(
        body, name=name,
        out_shape=[jax.ShapeDtypeStruct(h.shape, h.dtype) for h in halves],
        in_specs=[ANY] * n, out_specs=[ANY] * n,
        input_output_aliases={t: t for t in range(n)},
        scratch_shapes=[pltpu.SemaphoreType.DMA((n,)), pltpu.SemaphoreType.DMA((n,))],
    )(*halves)


def _sum8(parts, name):
    r = parts.shape[1]

    def body(p_ref, o_ref):
        acc = p_ref[0]
        for k in range(1, N_DEV):
            acc = acc + p_ref[k]
        o_ref[...] = acc

    return pl.pallas_call(
        body, name=name, out_shape=jax.ShapeDtypeStruct((r, 128), F32),
        in_specs=[pl.BlockSpec(memory_space=pltpu.VMEM)], out_specs=pl.BlockSpec(memory_space=pltpu.VMEM),
    )(parts)


def _adamw(w, g, m, v, name):
    shape = w.shape
    cols = shape[-1]
    rows = w.size // cols
    rt = 256 if rows % 256 == 0 else rows

    def body(w_ref, g_ref, m_ref, v_ref, d_ref, nm_ref, nv_ref):
        gv = g_ref[...]
        mn = ADAM_B1 * m_ref[...] + (1.0 - ADAM_B1) * gv
        vn = ADAM_B2 * v_ref[...] + (1.0 - ADAM_B2) * (gv * gv)
        m_hat = mn / (1.0 - ADAM_B1 ** ADAM_STEP)
        v_hat = vn / (1.0 - ADAM_B2 ** ADAM_STEP)
        d_ref[...] = -ADAM_LR * (m_hat / (jnp.sqrt(v_hat) + ADAM_EPS) + ADAM_WD * w_ref[...])
        nm_ref[...] = mn
        nv_ref[...] = vn

    spec = pl.BlockSpec((rt, cols), lambda i: (i, 0))
    outs = pl.pallas_call(
        body, name=name, grid=(rows // rt,),
        out_shape=[jax.ShapeDtypeStruct((rows, cols), F32)] * 3,
        in_specs=[spec] * 4, out_specs=[spec] * 3,
        compiler_params=_params("parallel"),
    )(*[a.reshape(rows, cols) for a in (w, g, m, v)])
    return [o.reshape(shape) for o in outs]


SMALL_ROWS = 832


def _pack_small(g):
    parts = [g["loss"], g["pre1"].reshape(8, 128), g["post0"].reshape(8, 128),
             g["post1"].reshape(8, 128), g["sinks"], jnp.pad(g["pool_scale"].reshape(4, 128), ((0, 4), (0, 0))),
             g["pool_w"], g["dw"].reshape(248, 128), g["dwb"].reshape(8, 128), g["lng"].reshape(8, 128),
             g["lnb"].reshape(8, 128)]
    assert sum(p.shape[0] for p in parts) == SMALL_ROWS
    return jnp.concatenate(parts, axis=0)


def _unpack_small(s):
    out, r = {}, 0
    for key, rows, shape in (("loss", 8, (8, 128)), ("pre1", 8, (1, D_MODEL)), ("post", 16, (2, D_MODEL)),
                             ("sinks", 8, (8, 128)),
                             ("pool_scale", 4, (1, 512)), ("pad", 4, (4, 128)), ("pool_w", 512, (1, 4, 128, 128)),
                             ("dw", 248, (CONV_K, D_MODEL)), ("dwb", 8, (1, D_MODEL)), ("lng", 8, (1, D_MODEL)),
                             ("lnb", 8, (1, D_MODEL))):
        out[key] = s[r:r + rows].reshape(shape)
        r += rows
    return out


def kernel(x, pre_norm, post_norm, a_w_in, a_sinks, b_pool_w, b_pool_scale, ab_w_out, c_w_in, c_dw_w, c_dw_b, c_ln_g, c_ln_b, c_w_out, loss_target, m_pre_norm, m_post_norm, m_a_w_in, m_a_sinks, m_b_pool_w, m_b_pool_scale, m_ab_w_out, m_c_w_in, m_c_dw_w, m_c_dw_b, m_c_ln_g, m_c_ln_b, m_c_w_out, v_pre_norm, v_post_norm, v_a_w_in, v_a_sinks, v_b_pool_w, v_b_pool_scale, v_ab_w_out, v_c_w_in, v_c_dw_w, v_c_dw_b, v_c_ln_g, v_c_ln_b, v_c_w_out):
    ix, iy = lax.axis_index("x"), lax.axis_index("y")
    chip_cols = (2 * ix + iy) * 256

    pad8 = lambda v: jnp.pad(v, ((0, -v.shape[0] % 8), (0, 0)))
    vec_shard = jnp.concatenate([pad8(c_dw_w.reshape(CONV_K, 256)), pad8(c_dw_b), pad8(c_ln_g), pad8(c_ln_b),
                                 jnp.zeros((8, 256), F32)], axis=0)
    x0, target = x[0], loss_target[0]
    pre0, pre1 = pre_norm[0:1], pre_norm[1:2]
    post0, post1 = post_norm[0:1], post_norm[1:2]
    pool_w = b_pool_w[0]

    (wa_t,) = _run_comm(_Gather([a_w_in[0].T.astype(BF16)], halve=True), "gather_a_w_in")
    wa_t = wa_t.reshape(EVEN_IN, D_MODEL)
    proj0, (w_ab,) = _norm_matmul(x0, pre0, wa_t, "proj0_fwd", comm=_Gather([ab_w_out[0].astype(BF16)], halve=True))
    w_ab = w_ab.reshape(D_MODEL, D_MODEL)
    (mix0, y0, x1), (wc_t, w_c, vecs) = _layer0_fwd(
        proj0, a_sinks, pool_w, b_pool_scale, w_ab, x0, post0,
        comm=_Gather([c_w_in[0].T.astype(BF16), c_w_out[0].astype(BF16), vec_shard], halve=True))
    wc_t = wc_t.reshape(3 * D_MODEL, D_MODEL)
    w_c = w_c.reshape(D_MODEL, D_MODEL)
    vecs = vecs.reshape(4, 64, 256).transpose(1, 0, 2).reshape(64, D_MODEL)
    dw, dwb, lng, lnb = vecs[0:CONV_K], vecs[32:33], vecs[40:41], vecs[48:49]
    proj1, _ = _norm_matmul(x1, pre1, wc_t, "proj1_fwd")
    z1, cf1, y1, g2, loss = _layer1_fwd(proj1, dw, dwb, lng, lnb, w_c, x1, post1, target)

    pieces = lambda m: m.reshape(N_DEV, m.shape[0] // N_DEV, D_MODEL)
    (dproj1, d_dw, d_dwb, d_lng, d_lnb, d_wc, d_post1), _ = _layer1_bwd(proj1, cf1, g2, y1, z1, w_c, post1, dw, lng, lnb)
    (g1, d_wct, d_pre1), (r_wc,) = _pre_bwd(dproj1, wc_t, x1, pre1, g2, "proj1_bwd", comm=_Scatter([pieces(d_wc)]))
    (dproj0, d_sinks, d_pw, d_ps, d_wab, d_post0), (r_wct,) = _layer0_bwd(
        proj0, g1, y0, mix0, w_ab, post0, a_sinks, pool_w, b_pool_scale, comm=_Scatter([pieces(d_wct)]))
    g = dict(loss=loss, pre1=d_pre1, post0=d_post0, post1=d_post1, sinks=d_sinks, pool_w=d_pw, pool_scale=d_ps,
             dw=d_dw, dwb=d_dwb, lng=d_lng, lnb=d_lnb)
    d_wat, (small8, r_wab) = _proj_dw(dproj0, x0, pre0, "proj0_dw",
                                      comm=_Comms(_Gather([_pack_small(g)], halve=False), _Scatter([pieces(d_wab)])))
    sent = _scatter_start(pieces(d_wat), "scatter_a_start")
    (gx, d_pre0), _ = _proj_dx(dproj0, wa_t, x0, pre0 + sent[4][0:1, 0:1], g1, "proj0_dx")
    sent_pre0 = _scatter_start(jnp.broadcast_to(d_pre0.reshape(1, 8, 128), (N_DEV, 8, 128)), "pre0_start")
    own_wat, r_wat = _scatter_wait(*sent[:4], d_pre0, "scatter_a_wait")

    ic = lax.axis_index("c")
    me = 4 * ix + 2 * iy + ic
    place = jnp.stack([me, ic]).astype(jnp.int32)
    parts = [_direct_parts(own_wat, r_wat), _direct_parts(pieces(d_wab), r_wab), _direct_parts(pieces(d_wct), r_wct),
             _direct_parts(pieces(d_wc), r_wc)]
    halves = [_piece_sum(p, place, f"grad_sum{t}") for t, p in enumerate(parts)]
    _, pre0_8 = _scatter_wait(*sent_pre0[:4], halves[0], "pre0_wait")
    pre0_8 = lax.dynamic_update_slice(pre0_8, d_pre0.reshape(1, 8, 128), (me, 0, 0))
    g_wa_t, g_wab, g_wc_t, g_wc = [h.reshape(2 * h.shape[1], D_MODEL) for h in _share_with_sibling(halves, "grad_share")]
    g_a_w_in = g_wa_t.T[None]
    g_c_w_in = g_wc_t.T[None]
    g_ab_w_out = g_wab[None]
    g_c_w_out = g_wc[None]

    s = _unpack_small(_sum8(small8, "small_sum"))
    layer = lax.broadcasted_iota(jnp.int32, (2, D_MODEL), 0)
    g_pre = jnp.where(layer == 0, _sum8(pre0_8, "pre0_sum").reshape(1, D_MODEL), s["pre1"])
    g_post = s["post"]
    g_sinks = s["sinks"][:, 0].reshape(1, 8)
    g_pool_w, g_pool_scale = s["pool_w"], s["pool_scale"]
    g_dw = lax.dynamic_slice_in_dim(s["dw"], chip_cols, 256, axis=1).reshape(1, CONV_K, 1, 256)
    g_dwb = lax.dynamic_slice_in_dim(s["dwb"], chip_cols, 256, axis=1)
    g_lng = lax.dynamic_slice_in_dim(s["lng"], chip_cols, 256, axis=1)
    g_lnb = lax.dynamic_slice_in_dim(s["lnb"], chip_cols, 256, axis=1)

    grads = [g_pre, g_post, g_a_w_in, g_sinks, g_pool_w, g_pool_scale, g_ab_w_out, g_c_w_in, g_dw, g_dwb, g_lng, g_lnb,
             g_c_w_out]
    weights = [pre_norm, post_norm, a_w_in, a_sinks, b_pool_w, b_pool_scale, ab_w_out, c_w_in, c_dw_w, c_dw_b, c_ln_g,
               c_ln_b, c_w_out]
    moms = [m_pre_norm, m_post_norm, m_a_w_in, m_a_sinks, m_b_pool_w, m_b_pool_scale, m_ab_w_out, m_c_w_in, m_c_dw_w,
            m_c_dw_b, m_c_ln_g, m_c_ln_b, m_c_w_out]
    vars_ = [v_pre_norm, v_post_norm, v_a_w_in, v_a_sinks, v_b_pool_w, v_b_pool_scale, v_ab_w_out, v_c_w_in, v_c_dw_w,
             v_c_dw_b, v_c_ln_g, v_c_ln_b, v_c_w_out]
    deltas, new_m, new_v = [], [], []
    for k, (w, gr, m, v) in enumerate(zip(weights, grads, moms, vars_)):
        d, nm, nv = _adamw(w, gr, m, v, f"adamw{k}")
        deltas.append(d)
        new_m.append(nm)
        new_v.append(nv)
    return (s["loss"][0, 0], gx[None], *grads, *deltas, *new_m, *new_v)
```

```python
import functools

import jax
import jax.numpy as jnp
from jax import lax
from jax.experimental import pallas as pl
from jax.experimental.pallas import tpu as pltpu

F32 = jnp.float32
BF16 = jnp.bfloat16

D_MODEL = 1024
EPS = 1e-6
NEG = -1e30
HEAD_DIM = 64
GROUP = 4
KV_HEADS = 2
BLOCK = 128
EVEN_IN = 2304
COL_Q, COL_K, COL_GA, COL_U, COL_GB = 0, 512, 768, 1280, 1792
POOL_GROUPS = 4
POOL_GC = 128
POOL_HALO = 16
CONV_K = 31
CONV_HALO = 32
N_DEV = 8

ADAM_LR = 0.001
ADAM_B1 = 0.9
ADAM_B2 = 0.999
ADAM_EPS = 1e-08
ADAM_WD = 0.01
ADAM_STEP = 10

VMEM_LIMIT_BYTES = 56 * 1024 * 1024

NT = (((1,), (1,)), ((), ()))
TN = (((0,), (0,)), ((), ()))
MESH_ID = pl.DeviceIdType.MESH


def _params(*sem):
    return pltpu.CompilerParams(dimension_semantics=sem, vmem_limit_bytes=VMEM_LIMIT_BYTES)


def _const_spec(shape):
    nd = len(shape)
    return pl.BlockSpec(shape, lambda *_: (0,) * nd, pipeline_mode=pl.Buffered(1))


def _sigmoid(v):
    return 0.5 * jnp.tanh(0.5 * v) + 0.5


def _silu(v):
    return v * _sigmoid(v)


def _silu_and_grad(v):
    s = _sigmoid(v)
    return v * s, s * (1.0 + v * (1.0 - s))


ANY = pl.BlockSpec(memory_space=pl.ANY)


def _place():
    x, y, c = lax.axis_index("x"), lax.axis_index("y"), lax.axis_index("c")
    chips = [(1 - x, y), (x, 1 - y), (1 - x, 1 - y)]
    return x, y, c, chips


class _Gather:
    def __init__(self, blocks, halve):
        self.ins = list(blocks)
        self.halve = halve
        self.n = n = len(blocks)
        self.shapes = [((b.shape[0] // 2) if halve else b.shape[0], b.shape[1]) for b in blocks]
        self.out_shape = [jax.ShapeDtypeStruct((N_DEV, r, cols), b.dtype) for (r, cols), b in zip(self.shapes, blocks)]
        self.scratch = [pltpu.SemaphoreType.DMA((7 * n,)), pltpu.SemaphoreType.DMA((7 * n,)),
                        pltpu.SemaphoreType.DMA((n,))]

    def _copies(self, ins, outs, sems):
        send_sems, recv_sems, local_sems = sems
        x, y, c, chips = _place()
        me, sibling = (x, y, c), (x, y, 1 - c)

        def piece(t, px, py, pc):
            return outs[t].at[4 * px + 2 * py + pc]

        def own(t):
            return ins[t].at[pl.ds(c * self.shapes[t][0], self.shapes[t][0])] if self.halve else ins[t]

        def copy(t, k, block, to, src=None):
            return pltpu.make_async_remote_copy(
                src_ref=piece(t, *block) if src is None else src, dst_ref=piece(t, *block),
                send_sem=send_sems.at[7 * t + k], recv_sem=recv_sems.at[7 * t + k],
                device_id=to, device_id_type=MESH_ID)

        rng = range(self.n)
        return dict(
            mine=[pltpu.make_async_copy(own(t), piece(t, *me), local_sems.at[t]) for t in rng],
            first=[copy(t, 0, me, sibling, src=own(t)) for t in rng]
            + [copy(t, 1 + j, me, (*chip, c), src=own(t)) for t in rng for j, chip in enumerate(chips)],
            landed=[copy(t, 1 + j, (*chip, c), me) for j, chip in enumerate(chips) for t in rng],
            passed=[copy(t, 4 + j, (*chip, c), sibling) for j, chip in enumerate(chips) for t in rng],
            from_sibling=[copy(t, 0, sibling, me) for t in rng]
            + [copy(t, 4 + j, (*chip, 1 - c), me) for t in rng for j, chip in enumerate(chips)])

    def start(self, ins, outs, sems):
        d = self._copies(ins, outs, sems)
        for cp in d["mine"] + d["first"]:
            cp.start()

    def middle(self, ins, outs, sems):
        d = self._copies(ins, outs, sems)
        for got, fwd in zip(d["landed"], d["passed"]):
            got.wait_recv()
            fwd.start()

    def finish(self, ins, outs, sems):
        d = self._copies(ins, outs, sems)
        for cp in d["from_sibling"]:
            cp.wait_recv()
        for cp in d["first"] + d["passed"]:
            cp.wait_send()
        for cp in d["mine"]:
            cp.wait()


class _Scatter:
    def __init__(self, tensors):
        self.ins = list(tensors)
        self.n = n = len(tensors)
        self.out_shape = [jax.ShapeDtypeStruct(t.shape, t.dtype) for t in tensors]
        self.scratch = [pltpu.SemaphoreType.DMA((7 * n,)), pltpu.SemaphoreType.DMA((7 * n,))]

    def _copies(self, ins, outs, sems):
        send_sems, recv_sems = sems
        x, y, c, _ = _place()
        me = 4 * x + 2 * y + c
        sends, recvs = [], []
        for t in range(self.n):
            for m in range(1, N_DEV):
                px, py, pc = x ^ (m >> 2), y ^ ((m >> 1) & 1), c ^ (m & 1)
                q = 4 * px + 2 * py + pc
                sems_k = dict(send_sem=send_sems.at[7 * t + m - 1], recv_sem=recv_sems.at[7 * t + m - 1],
                              device_id=(px, py, pc), device_id_type=MESH_ID)
                sends.append(pltpu.make_async_remote_copy(src_ref=ins[t].at[q], dst_ref=outs[t].at[me], **sems_k))
                recvs.append(pltpu.make_async_remote_copy(src_ref=ins[t].at[me], dst_ref=outs[t].at[q], **sems_k))
        return sends, recvs

    def start(self, ins, outs, sems):
        for cp in self._copies(ins, outs, sems)[0]:
            cp.start()

    def middle(self, ins, outs, sems):
        pass

    def finish(self, ins, outs, sems):
        sends, recvs = self._copies(ins, outs, sems)
        for cp in recvs:
            cp.wait_recv()
        for cp in sends:
            cp.wait_send()


class _Comms:
    def __init__(self, *comms):
        self.comms = comms
        self.ins = [a for c in comms for a in c.ins]
        self.out_shape = [s for c in comms for s in c.out_shape]
        self.scratch = [s for c in comms for s in c.scratch]

    def _each(self, phase, ins, outs, sems):
        i = o = s = 0
        for c in self.comms:
            ni, no, ns = len(c.ins), len(c.out_shape), len(c.scratch)
            getattr(c, phase)(ins[i:i + ni], outs[o:o + no], sems[s:s + ns])
            i, o, s = i + ni, o + no, s + ns

    def start(self, ins, outs, sems):
        self._each("start", ins, outs, sems)

    def middle(self, ins, outs, sems):
        self._each("middle", ins, outs, sems)

    def finish(self, ins, outs, sems):
        self._each("finish", ins, outs, sems)


def _run_comm(comm, name):
    n = len(comm.ins)

    def body(*refs):
        parts = refs[:n], refs[n:2 * n], refs[2 * n:]
        comm.start(*parts)
        comm.middle(*parts)
        comm.finish(*parts)

    return pl.pallas_call(body, name=name, out_shape=comm.out_shape, in_specs=[ANY] * n, out_specs=[ANY] * n,
                          scratch_shapes=comm.scratch)(*comm.ins)


HBM_SPEC = pl.BlockSpec(memory_space=pltpu.HBM)
SEM_SPEC = pl.BlockSpec(memory_space=pltpu.SEMAPHORE)
DATAFLOW = pltpu.SideEffectType.DATAFLOW_SIDE_EFFECTING


def _scatter_copies(own_ref, land_ref, send_sems, recv_sems):
    x, y, c, _ = _place()
    me = 4 * x + 2 * y + c
    pairs = []
    for m in range(1, N_DEV):
        px, py, pc = x ^ (m >> 2), y ^ ((m >> 1) & 1), c ^ (m & 1)
        q = 4 * px + 2 * py + pc
        sems = dict(send_sem=send_sems.at[m - 1], recv_sem=recv_sems.at[m - 1], device_id=(px, py, pc),
                    device_id_type=MESH_ID)
        pairs.append((pltpu.make_async_remote_copy(src_ref=own_ref.at[q], dst_ref=land_ref.at[me], **sems),
                      pltpu.make_async_remote_copy(src_ref=own_ref.at[me], dst_ref=land_ref.at[q], **sems)))
    return pairs


def _scatter_start(own, name):
    def body(own_ref, land_ref, send_sems, recv_sems, own_thru, land_thru, token):
        for send, _ in _scatter_copies(own_ref, land_ref, send_sems, recv_sems):
            send.start()
        token[...] = jnp.zeros_like(token)

    buf = pltpu.HBM(own.shape, own.dtype)
    return pl.pallas_call(
        body, name=name,
        out_shape=(pltpu.SemaphoreType.DMA((N_DEV - 1,)), pltpu.SemaphoreType.DMA((N_DEV - 1,)), buf, buf,
                   jax.ShapeDtypeStruct((8, 128), F32)),
        in_specs=(HBM_SPEC, HBM_SPEC),
        out_specs=(SEM_SPEC, SEM_SPEC, HBM_SPEC, HBM_SPEC, pl.BlockSpec(memory_space=pltpu.VMEM)),
        input_output_aliases={0: 2, 1: 3},
        compiler_params=pltpu.CompilerParams(has_side_effects=DATAFLOW),
    )(pltpu.with_memory_space_constraint(own, pltpu.HBM),
      pltpu.with_memory_space_constraint(lax.empty(own.shape, own.dtype), pltpu.HBM))


def _scatter_wait(send_sems, recv_sems, own_thru, land_thru, after, name):
    def body(own_ref, land_ref, send_sems, recv_sems, after_ref, own_out, land_out):
        for send, recv in _scatter_copies(own_ref, land_ref, send_sems, recv_sems):
            send.wait_send()
            recv.wait_recv()

    buf = pltpu.HBM(own_thru.shape, own_thru.dtype)
    return pl.pallas_call(
        body, name=name, out_shape=(buf, buf),
        in_specs=(HBM_SPEC, HBM_SPEC, SEM_SPEC, SEM_SPEC, ANY), out_specs=(HBM_SPEC, HBM_SPEC),
        input_output_aliases={0: 0, 1: 1},
        compiler_params=pltpu.CompilerParams(has_side_effects=DATAFLOW),
    )(own_thru, land_thru, send_sems, recv_sems, after)


def _fused_call(body, comm, args, *, name, grid, out_shape, in_specs, out_specs, scratch_shapes=(), params):
    single = not isinstance(out_shape, (list, tuple))
    out_shape = [out_shape] if single else list(out_shape)
    out_specs = [out_specs] if single else list(out_specs)
    if comm is None:
        res = pl.pallas_call(body, name=name, grid=grid, out_shape=out_shape, in_specs=in_specs, out_specs=out_specs,
                             scratch_shapes=list(scratch_shapes), compiler_params=params)(*args)
        return (res[0] if single else res), []
    n_in, n_out, n_scr = len(in_specs), len(out_shape), len(scratch_shapes)
    c_in, c_out = len(comm.ins), len(comm.out_shape)
    steps = grid[0]

    def fused(*refs):
        pos = 0
        groups = []
        for size in (n_in, c_in, n_out, c_out, n_scr, len(comm.scratch)):
            groups.append(refs[pos:pos + size])
            pos += size
        ins, c_ins, outs, c_outs, scr, c_sems = groups
        i = pl.program_id(0)

        @pl.when(i == 0)
        def _():
            comm.start(c_ins, c_outs, c_sems)

        @pl.when(i == steps // 2)
        def _():
            comm.middle(c_ins, c_outs, c_sems)

        body(*ins, *outs, *scr)

        @pl.when(i == steps - 1)
        def _():
            comm.finish(c_ins, c_outs, c_sems)

    res = pl.pallas_call(
        fused, name=name, grid=grid, out_shape=out_shape + list(comm.out_shape),
        in_specs=list(in_specs) + [ANY] * c_in, out_specs=out_specs + [ANY] * c_out,
        scratch_shapes=list(scratch_shapes) + list(comm.scratch), compiler_params=params)(*args, *comm.ins)
    main = res[:n_out]
    return (main[0] if single else main), list(res[n_out:])


def _norm_matmul(x, gain, wt, name, comm=None, tm=1024):
    t, n = x.shape[0], wt.shape[0]

    def body(x_ref, g_ref, wt_ref, o_ref):
        xv = x_ref[...]
        r = lax.rsqrt(jnp.mean(xv * xv, axis=-1, keepdims=True) + EPS)
        h = (xv * r * g_ref[...]).astype(BF16)
        o_ref[...] = lax.dot_general(h, wt_ref[...], NT, preferred_element_type=F32)

    return _fused_call(
        body, comm, (x, gain, wt), name=name, grid=(t // tm,),
        out_shape=jax.ShapeDtypeStruct((t, n), F32),
        in_specs=[pl.BlockSpec((tm, D_MODEL), lambda i: (i, 0)), _const_spec((1, D_MODEL)), _const_spec((n, D_MODEL))],
        out_specs=pl.BlockSpec((tm, n), lambda i: (i, 0)),
        params=_params("arbitrary"))


def _project_out(a, w_ref, x_ref, p_ref, y_ref):
    y = jnp.dot(a, w_ref[...], preferred_element_type=F32)
    y_ref[...] = y
    ry = lax.rsqrt(jnp.mean(y * y, axis=-1, keepdims=True) + EPS)
    return x_ref[...] + (y * ry) * p_ref[...]


def _post_bwd_rows(g, y, a, n_own, first, last, p_ref, w_ref, dw_ref, dw16_ref, dp_ref):
    @pl.when(first)
    def _():
        dw_ref[...] = jnp.zeros_like(dw_ref)
        dp_ref[...] = jnp.zeros_like(dp_ref)

    ry = lax.rsqrt(jnp.mean(y * y, axis=-1, keepdims=True) + EPS)
    nv = y * ry
    dp_ref[...] += jnp.sum((g * nv)[0:n_own], axis=0, keepdims=True)
    dn = g * p_ref[...]
    dy = (ry * (dn - nv * jnp.mean(dn * nv, axis=-1, keepdims=True))).astype(BF16)
    dw_ref[...] += lax.dot_general(a, dy[0:n_own], TN, preferred_element_type=F32)

    @pl.when(last)
    def _():
        dw16_ref[...] = dw_ref[...].astype(BF16)

    return lax.dot_general(dy, w_ref[...], NT, preferred_element_type=F32)


def _pre_bwd(dproj, wt, x_in, pre, g, name, comm=None, tm=512):
    t, n = dproj.shape
    steps = t // tm

    def body(dp_ref, wt_ref, x_ref, pre_ref, g_ref, dx_ref, dwt16_ref, dpre_ref, dwt_ref):
        @pl.when(pl.program_id(0) == 0)
        def _():
            dwt_ref[...] = jnp.zeros_like(dwt_ref)
            dpre_ref[...] = jnp.zeros_like(dpre_ref)

        dpv = dp_ref[...]
        dh = jnp.dot(dpv, wt_ref[...], preferred_element_type=F32)
        xv = x_ref[...]
        r = lax.rsqrt(jnp.mean(xv * xv, axis=-1, keepdims=True) + EPS)
        xn = xv * r
        pv = pre_ref[...]
        dpre_ref[...] += jnp.sum(dh * xn, axis=0, keepdims=True)
        dxn = dh * pv
        dx_ref[...] = g_ref[...] + r * (dxn - xn * jnp.mean(dxn * xn, axis=-1, keepdims=True))
        h = (xn * pv).astype(BF16)
        dwt_ref[...] += lax.dot_general(dpv, h, TN, preferred_element_type=F32)

        @pl.when(pl.program_id(0) == steps - 1)
        def _():
            dwt16_ref[...] = dwt_ref[...].astype(BF16)

    row = pl.BlockSpec((tm, D_MODEL), lambda i: (i, 0))
    return _fused_call(
        body, comm, (dproj, wt, x_in, pre, g), name=name, grid=(steps,),
        out_shape=[jax.ShapeDtypeStruct((t, D_MODEL), F32), jax.ShapeDtypeStruct((n, D_MODEL), BF16),
                   jax.ShapeDtypeStruct((1, D_MODEL), F32)],
        in_specs=[pl.BlockSpec((tm, n), lambda i: (i, 0)), _const_spec((n, D_MODEL)), row, _const_spec((1, D_MODEL)), row],
        out_specs=[row, _const_spec((n, D_MODEL)), pl.BlockSpec((1, D_MODEL), lambda i: (0, 0))],
        scratch_shapes=[pltpu.VMEM((n, D_MODEL), F32)],
        params=_params("arbitrary"))


def _proj_dw(dproj, x_in, pre, name, comm=None, tm=1024):
    t, n = dproj.shape
    steps = t // tm

    def body(dp_ref, x_ref, pre_ref, dwt16_ref, dwt_ref):
        @pl.when(pl.program_id(0) == 0)
        def _():
            dwt_ref[...] = jnp.zeros_like(dwt_ref)

        xv = x_ref[...]
        r = lax.rsqrt(jnp.mean(xv * xv, axis=-1, keepdims=True) + EPS)
        h = (xv * r * pre_ref[...]).astype(BF16)
        dwt_ref[...] += lax.dot_general(dp_ref[...], h, TN, preferred_element_type=F32)

        @pl.when(pl.program_id(0) == steps - 1)
        def _():
            dwt16_ref[...] = dwt_ref[...].astype(BF16)

    return _fused_call(
        body, comm, (dproj, x_in, pre), name=name, grid=(steps,),
        out_shape=jax.ShapeDtypeStruct((n, D_MODEL), BF16),
        in_specs=[pl.BlockSpec((tm, n), lambda i: (i, 0)), pl.BlockSpec((tm, D_MODEL), lambda i: (i, 0)),
                  _const_spec((1, D_MODEL))],
        out_specs=pl.BlockSpec((n, D_MODEL), lambda i: (0, 0)),
        scratch_shapes=[pltpu.VMEM((n, D_MODEL), F32)],
        params=_params("arbitrary"))


def _proj_dx(dproj, wt, x_in, pre, g, name, comm=None, tm=512):
    t, n = dproj.shape

    def body(dp_ref, wt_ref, x_ref, pre_ref, g_ref, dx_ref, dpre_ref):
        @pl.when(pl.program_id(0) == 0)
        def _():
            dpre_ref[...] = jnp.zeros_like(dpre_ref)

        dh = jnp.dot(dp_ref[...], wt_ref[...], preferred_element_type=F32)
        xv = x_ref[...]
        r = lax.rsqrt(jnp.mean(xv * xv, axis=-1, keepdims=True) + EPS)
        xn = xv * r
        dpre_ref[...] += jnp.sum(dh * xn, axis=0, keepdims=True)
        dxn = dh * pre_ref[...]
        dx_ref[...] = g_ref[...] + r * (dxn - xn * jnp.mean(dxn * xn, axis=-1, keepdims=True))

    row = pl.BlockSpec((tm, D_MODEL), lambda i: (i, 0))
    return _fused_call(
        body, comm, (dproj, wt, x_in, pre, g), name=name, grid=(t // tm,),
        out_shape=[jax.ShapeDtypeStruct((t, D_MODEL), F32), jax.ShapeDtypeStruct((1, D_MODEL), F32)],
        in_specs=[pl.BlockSpec((tm, n), lambda i: (i, 0)), _const_spec((n, D_MODEL)), row, _const_spec((1, D_MODEL)), row],
        out_specs=[row, pl.BlockSpec((1, D_MODEL), lambda i: (0, 0))],
        params=_params("arbitrary"))


def _group_masks():
    lane = lax.broadcasted_iota(jnp.int32, (1, GROUP * HEAD_DIM), 1)
    return [(lane // HEAD_DIM == g).astype(F32) for g in range(GROUP)]


def _stack_groups(v, masks):
    return jnp.concatenate([v * m for m in masks], axis=0)


def _unstack_groups(v, masks):
    out = v[0:BLOCK] * masks[0]
    for g in range(1, GROUP):
        out = out + v[g * BLOCK:(g + 1) * BLOCK] * masks[g]
    return out


def _repeat_head(kv2, kvh):
    first = lax.broadcasted_iota(jnp.int32, kv2.shape, 1) < HEAD_DIM
    rolled = pltpu.roll(kv2, HEAD_DIM, 1)
    one = jnp.where(first, kv2, rolled) if kvh == 0 else jnp.where(first, rolled, kv2)
    return jnp.concatenate([one, one], axis=1)


def _fold_head(v4):
    a = v4[:, 0:128] + v4[:, 128:256]
    return a + pltpu.roll(a, HEAD_DIM, 1)


ATTN_CONSTS = [pltpu.VMEM((KV_HEADS, GROUP * BLOCK, 2 * BLOCK), F32)]


def _fill_attn_bias(bias_ref):
    row = lax.broadcasted_iota(jnp.int32, (GROUP * BLOCK, 2 * BLOCK), 0)
    col = lax.broadcasted_iota(jnp.int32, (GROUP * BLOCK, 2 * BLOCK), 1)
    dist = (row % BLOCK) + BLOCK - col
    band = (dist >= 0) & (dist < BLOCK)
    rb = lax.broadcasted_iota(jnp.int32, (GROUP * BLOCK, 1), 0) // BLOCK
    for kvh in range(KV_HEADS):
        slope = jnp.zeros((GROUP * BLOCK, 1), F32)
        for g in range(GROUP):
            slope = jnp.where(rb == g, 2.0 ** (-(kvh * GROUP + g + 1)), slope)
        bias_ref[kvh] = jnp.where(band, -slope * dist.astype(F32), NEG)


def _row_sinks(kvh, sink_ref):
    rb = lax.broadcasted_iota(jnp.int32, (GROUP * BLOCK, 1), 0) // BLOCK
    sink = jnp.zeros((GROUP * BLOCK, 1), F32)
    for g in range(GROUP):
        sink = jnp.where(rb == g, sink_ref[0, kvh * GROUP + g], sink)
    return sink


def _attn_probs(qk, k4, bias, sink, no_past, masks):
    qs = _stack_groups(qk, masks).astype(BF16)
    s = lax.dot_general(qs, k4, NT, preferred_element_type=F32) * (HEAD_DIM ** -0.5) + bias
    s = jnp.concatenate([jnp.where(no_past, NEG, s[:, 0:BLOCK]), s[:, BLOCK:]], axis=1)
    mx = jnp.maximum(jnp.max(s, axis=-1, keepdims=True), sink)
    e = jnp.exp(s - mx)
    es = jnp.exp(sink - mx)
    inv = 1.0 / (jnp.sum(e, axis=-1, keepdims=True) + es)
    return qs, e * inv, es * inv


def _pool_forward(u_ext, g, t0):
    n = u_ext.shape[0] - POOL_HALO
    s = u_ext
    for step in range(g + 1):
        s = s + pltpu.roll(s, 1 << step, 0)
    w = 2 << g
    t = t0 + lax.broadcasted_iota(jnp.int32, (n, 1), 0)
    cnt = jnp.minimum(t + 1, w).astype(F32)
    return s[POOL_HALO:] / cnt - u_ext[POOL_HALO:]


def _layer0_fwd(proj, sinks, pool_w, pool_scale, w_out, x_in, post, comm=None, tq=512):
    t = proj.shape[0]
    nblk = tq // BLOCK

    def body(main_ref, halo_ref, sink_ref, pw_ref, ps_ref, w_ref, x_ref, p_ref, o_ref, y_ref, xo_ref, kv_ref, bias_ref):
        i = pl.program_id(0)
        t0 = i * tq
        masks = _group_masks()

        @pl.when(i == 0)
        def _():
            _fill_attn_bias(bias_ref)

        kv_ref[0:BLOCK, :] = halo_ref[:, COL_K:COL_K + 256]
        kv_ref[BLOCK:, :] = main_ref[:, COL_K:COL_K + 256]

        def block(jb, carry):
            r0 = pl.multiple_of(jb * BLOCK, BLOCK)
            no_past = t0 + r0 == 0
            q = main_ref[pl.ds(r0, BLOCK), COL_Q:COL_Q + 512]
            ga = main_ref[pl.ds(r0, BLOCK), COL_GA:COL_GA + 512]
            kk = kv_ref[pl.ds(r0, 2 * BLOCK), 0:128]
            vv = kv_ref[pl.ds(r0, 2 * BLOCK), 128:256]
            outs = []
            for kvh in range(KV_HEADS):
                k4 = _repeat_head(kk, kvh).astype(BF16)
                v4 = _repeat_head(vv, kvh).astype(BF16)
                _, p, _ = _attn_probs(q[:, kvh * 256:(kvh + 1) * 256], k4, bias_ref[kvh], _row_sinks(kvh, sink_ref), no_past, masks)
                pv = jnp.dot(p.astype(BF16), v4, preferred_element_type=F32)
                outs.append(_unstack_groups(pv, masks))
            attn = jnp.concatenate(outs, axis=1)
            o_ref[pl.ds(r0, BLOCK), 0:512] = (attn * _silu(ga)).astype(BF16)
            return carry

        lax.fori_loop(0, nblk, block, 0, unroll=True)

        for g in range(POOL_GROUPS):
            cu = COL_U + g * POOL_GC
            cg = COL_GB + g * POOL_GC
            halo_u = jnp.where(i == 0, 0.0, halo_ref[BLOCK - POOL_HALO:BLOCK, cu:cu + POOL_GC])
            u_ext = jnp.concatenate([halo_u, main_ref[:, cu:cu + POOL_GC]], axis=0)
            pooled = _pool_forward(u_ext, g, t0)
            y = jnp.dot(pooled.astype(BF16), pw_ref[g].astype(BF16), preferred_element_type=F32)
            y = y * ps_ref[:, g * POOL_GC:(g + 1) * POOL_GC]
            o_ref[:, 512 + g * POOL_GC:512 + (g + 1) * POOL_GC] = (y * _silu(main_ref[:, cg:cg + POOL_GC])).astype(BF16)

        xo_ref[...] = _project_out(o_ref[...], w_ref, x_ref, p_ref, y_ref)

    row = pl.BlockSpec((tq, D_MODEL), lambda i: (i, 0))
    return _fused_call(
        body, comm, (proj, proj, sinks, pool_w, pool_scale, w_out, x_in, post), name="layer0_fwd", grid=(t // tq,),
        out_shape=[jax.ShapeDtypeStruct((t, D_MODEL), BF16), jax.ShapeDtypeStruct((t, D_MODEL), F32),
                   jax.ShapeDtypeStruct((t, D_MODEL), F32)],
        in_specs=[pl.BlockSpec((tq, EVEN_IN), lambda i: (i, 0)),
                  pl.BlockSpec((BLOCK, EVEN_IN), lambda i: (jnp.maximum(i * nblk - 1, 0), 0)),
                  pl.BlockSpec(memory_space=pltpu.SMEM),
                  _const_spec((POOL_GROUPS, POOL_GC, POOL_GC)), _const_spec((1, 512)),
                  _const_spec((D_MODEL, D_MODEL)), row, _const_spec((1, D_MODEL))],
        out_specs=[row, row, row],
        scratch_shapes=[pltpu.VMEM((tq + BLOCK, 256), F32)] + ATTN_CONSTS,
        params=_params("arbitrary"))


def _layer0_bwd(proj, gy, y, mix, w_out, post, sinks, pool_w, pool_scale, comm=None, tq=512):
    t = proj.shape[0]
    nt = t // tq
    nblk = tq // BLOCK

    def body(main_ref, halo_ref, next_ref, gy_ref, gyn_ref, y_ref, yn_ref, mix_ref, wo_ref, po_ref,
             sink_ref, pw_ref, ps_ref,
             o_ref, dsk_ref, dpw_ref, dps_ref, dwo16_ref, dpo_ref,
             kv_ref, dkv_ref, carry_ref, bias_ref, dwo_ref, dmix_ref):
        i = pl.program_id(0)
        ii = nt - 1 - i
        t0 = ii * tq
        masks = _group_masks()

        @pl.when(i == 0)
        def _():
            _fill_attn_bias(bias_ref)
            dsk_ref[...] = jnp.zeros_like(dsk_ref)
            dpw_ref[...] = jnp.zeros_like(dpw_ref)
            dps_ref[...] = jnp.zeros_like(dps_ref)
            carry_ref[...] = jnp.zeros_like(carry_ref)

        dmix_ref[...] = _post_bwd_rows(jnp.concatenate([gy_ref[...], gyn_ref[...]], axis=0),
                                       jnp.concatenate([y_ref[...], yn_ref[...]], axis=0), mix_ref[...], tq,
                                       i == 0, i == nt - 1, po_ref, wo_ref, dwo_ref, dwo16_ref, dpo_ref)
        dm_ref = dmix_ref.at[pl.ds(0, tq)]
        dmn_ref = dmix_ref.at[pl.ds(tq, POOL_HALO)]

        kv_ref[0:BLOCK, :] = halo_ref[:, COL_K:COL_K + 256]
        kv_ref[BLOCK:, :] = main_ref[:, COL_K:COL_K + 256]
        dkv_ref[0:tq, :] = jnp.zeros((tq, 256), F32)
        dkv_ref[tq:, :] = carry_ref[...]

        def block(jb, carry):
            r0 = pl.multiple_of(jb * BLOCK, BLOCK)
            no_past = t0 + r0 == 0
            q = main_ref[pl.ds(r0, BLOCK), COL_Q:COL_Q + 512]
            ga = main_ref[pl.ds(r0, BLOCK), COL_GA:COL_GA + 512]
            dya = dm_ref[pl.ds(r0, BLOCK), 0:512]
            kk = kv_ref[pl.ds(r0, 2 * BLOCK), 0:128]
            vv = kv_ref[pl.ds(r0, 2 * BLOCK), 128:256]
            silu_ga, dsilu_ga = _silu_and_grad(ga)
            do = dya * silu_ga
            first = lax.broadcasted_iota(jnp.int32, (2 * BLOCK, 128), 1) < HEAD_DIM
            attn, dq, dk, dv = [], [], [], []
            for kvh in range(KV_HEADS):
                k4 = _repeat_head(kk, kvh).astype(BF16)
                v4 = _repeat_head(vv, kvh).astype(BF16)
                qs, p, ps = _attn_probs(q[:, kvh * 256:(kvh + 1) * 256], k4, bias_ref[kvh], _row_sinks(kvh, sink_ref), no_past, masks)
                pb = p.astype(BF16)
                o_k = _unstack_groups(jnp.dot(pb, v4, preferred_element_type=F32), masks)
                do_k = do[:, kvh * 256:(kvh + 1) * 256]
                dos = _stack_groups(do_k, masks).astype(BF16)
                prod = do_k * o_k
                delta = jnp.concatenate([jnp.sum(prod * m, axis=-1, keepdims=True) for m in masks], axis=0)
                dp = lax.dot_general(dos, v4, NT, preferred_element_type=F32)
                ds = (p * (dp - delta)).astype(BF16)
                sink_term = ps * delta
                for g in range(GROUP):
                    h = kvh * GROUP + g
                    dsk_ref[h:h + 1, :] -= jnp.sum(sink_term[g * BLOCK:(g + 1) * BLOCK], keepdims=True)
                scale = HEAD_DIM ** -0.5
                dq.append(_unstack_groups(jnp.dot(ds, k4, preferred_element_type=F32), masks) * scale)
                dk.append(_fold_head(lax.dot_general(ds, qs, TN, preferred_element_type=F32)) * scale)
                dv.append(_fold_head(lax.dot_general(pb, dos, TN, preferred_element_type=F32)))
                attn.append(o_k)
            o_ref[pl.ds(r0, BLOCK), COL_Q:COL_Q + 512] = jnp.concatenate(dq, axis=1).astype(BF16)
            o_all = jnp.concatenate(attn, axis=1)
            o_ref[pl.ds(r0, BLOCK), COL_GA:COL_GA + 512] = (dya * o_all * dsilu_ga).astype(BF16)
            dkv = jnp.concatenate([jnp.where(first, dk[0], dk[1]), jnp.where(first, dv[0], dv[1])], axis=1)
            dkv_ref[pl.ds(r0, 2 * BLOCK), :] += dkv
            return carry

        lax.fori_loop(0, nblk, block, 0, unroll=True)
        carry_ref[...] = dkv_ref[0:BLOCK, :]
        o_ref[:, COL_K:COL_K + 256] = dkv_ref[BLOCK:, :].astype(BF16)

        last = ii == nt - 1
        for g in range(POOL_GROUPS):
            cu = COL_U + g * POOL_GC
            cg = COL_GB + g * POOL_GC
            cm = 512 + g * POOL_GC
            pw = pw_ref[g].astype(BF16)
            sc = ps_ref[:, g * POOL_GC:(g + 1) * POOL_GC]
            halo_u = jnp.where(ii == 0, 0.0, halo_ref[BLOCK - POOL_HALO:BLOCK, cu:cu + POOL_GC])
            u_ext = jnp.concatenate([halo_u, main_ref[:, cu:cu + POOL_GC]], axis=0)
            pooled = _pool_forward(u_ext, g, t0).astype(BF16)
            y_raw = jnp.dot(pooled, pw, preferred_element_type=F32)
            gb = main_ref[:, cg:cg + POOL_GC]
            dyb = dm_ref[:, cm:cm + POOL_GC]
            silu_gb, dsilu_gb = _silu_and_grad(gb)
            dypool = dyb * silu_gb
            dps_ref[:, g * POOL_GC:(g + 1) * POOL_GC] += jnp.sum(dypool * y_raw, axis=0, keepdims=True)
            o_ref[:, cg:cg + POOL_GC] = (dyb * (y_raw * sc) * dsilu_gb).astype(BF16)
            dyraw = dypool * sc
            dyraw_n = jnp.where(last, 0.0, dmn_ref[:, cm:cm + POOL_GC] * _silu(next_ref[:, cg:cg + POOL_GC]) * sc)
            dpw_ref[g * POOL_GC:(g + 1) * POOL_GC, :] += lax.dot_general(pooled, dyraw.astype(BF16), TN,
                                                                         preferred_element_type=F32)
            dyraw_ext = jnp.concatenate([dyraw, dyraw_n], axis=0).astype(BF16)
            dpooled = lax.dot_general(dyraw_ext, pw, NT, preferred_element_type=F32)
            w = 2 << g
            tt = t0 + lax.broadcasted_iota(jnp.int32, (tq + POOL_HALO, 1), 0)
            s = dpooled / jnp.minimum(tt + 1, w).astype(F32)
            for step in range(g + 1):
                s = s + pltpu.roll(s, tq + POOL_HALO - (1 << step), 0)
            o_ref[:, cu:cu + POOL_GC] = (s[0:tq] - dpooled[0:tq]).astype(BF16)

    rev = lambda i: nt - 1 - i
    nxt = lambda i: (jnp.minimum((rev(i) + 1) * (tq // POOL_HALO), t // POOL_HALO - 1), 0)
    row = pl.BlockSpec((tq, D_MODEL), lambda i: (rev(i), 0))
    nxt_row = pl.BlockSpec((POOL_HALO, D_MODEL), nxt)
    square = _const_spec((D_MODEL, D_MODEL))
    return _fused_call(
        body, comm, (proj, proj, proj, gy, gy, y, y, mix, w_out, post, sinks, pool_w, pool_scale),
        name="layer0_bwd", grid=(nt,),
        out_shape=[jax.ShapeDtypeStruct((t, EVEN_IN), BF16), jax.ShapeDtypeStruct((8, 128), F32),
                   jax.ShapeDtypeStruct((POOL_GROUPS * POOL_GC, POOL_GC), F32), jax.ShapeDtypeStruct((1, 512), F32),
                   jax.ShapeDtypeStruct((D_MODEL, D_MODEL), BF16), jax.ShapeDtypeStruct((1, D_MODEL), F32)],
        in_specs=[pl.BlockSpec((tq, EVEN_IN), lambda i: (rev(i), 0)),
                  pl.BlockSpec((BLOCK, EVEN_IN), lambda i: (jnp.maximum(rev(i) * nblk - 1, 0), 0)),
                  pl.BlockSpec((POOL_HALO, EVEN_IN), nxt),
                  row, nxt_row, row, nxt_row, row, square, _const_spec((1, D_MODEL)),
                  pl.BlockSpec(memory_space=pltpu.SMEM),
                  _const_spec((POOL_GROUPS, POOL_GC, POOL_GC)), _const_spec((1, 512))],
        out_specs=[pl.BlockSpec((tq, EVEN_IN), lambda i: (rev(i), 0)),
                   pl.BlockSpec((8, 128), lambda i: (0, 0)),
                   pl.BlockSpec((POOL_GROUPS * POOL_GC, POOL_GC), lambda i: (0, 0)),
                   pl.BlockSpec((1, 512), lambda i: (0, 0)), square, pl.BlockSpec((1, D_MODEL), lambda i: (0, 0))],
        scratch_shapes=[pltpu.VMEM((tq + BLOCK, 256), F32), pltpu.VMEM((tq + BLOCK, 256), F32),
                        pltpu.VMEM((BLOCK, 256), F32)] + ATTN_CONSTS
        + [pltpu.VMEM((D_MODEL, D_MODEL), F32), pltpu.VMEM((tq + POOL_HALO, D_MODEL), F32)],
        params=_params("arbitrary"))


CONV_RC = 32
CONV_CC = 128
CONV_CHAINS = 4
CONV_UNROLL = 2


def _fill_shifted(s_ref, rows):
    for b in range(1, 8):
        s_ref[b, 0:rows - 8, :] = s_ref[0, b:b + rows - 8, :]


def _tap_blocks(s_ref, r, cols, lead):
    for b in range(8):
        taps = [(a, 8 * a + b - lead) for a in range(5) if 0 <= 8 * a + b - lead < CONV_K]
        span = 8 * max(a for a, _ in taps) + CONV_RC
        blk = s_ref[b, pl.ds(r, span), cols]
        for a, k in taps:
            yield k, blk[8 * a:8 * a + CONV_RC]


def _conv_taps(s_ref, w_ref, r, cols, lead, reverse):
    accs = [None] * CONV_CHAINS
    for n, (k, blk) in enumerate(_tap_blocks(s_ref, r, cols, lead)):
        kw = CONV_K - 1 - k if reverse else k
        term = blk * w_ref[kw:kw + 1, cols]
        accs[n % CONV_CHAINS] = term if accs[n % CONV_CHAINS] is None else accs[n % CONV_CHAINS] + term
    return (accs[0] + accs[1]) + (accs[2] + accs[3])


def _layer_norm_fwd(cf, lng, lnb):
    mu = jnp.mean(cf, axis=-1, keepdims=True)
    xc = cf - mu
    rstd = lax.rsqrt(jnp.mean(xc * xc, axis=-1, keepdims=True) + EPS)
    chat = xc * rstd
    return chat, rstd, chat * lng + lnb


def _layer1_fwd(proj, dw, dwb, lng, lnb, w_out, x_in, post, target, tt=256):
    t = proj.shape[0]
    lead = CONV_HALO - (CONV_K - 1)

    def body(main_ref, halo_ref, w_ref, b_ref, g_ref, lb_ref, wo_ref, x_ref, p_ref, t_ref,
             o_ref, c_ref, y_ref, dl_ref, l_ref, gs_ref):
        i = pl.program_id(0)
        hv = halo_ref[...]
        gs_ref[0, 0:CONV_HALO, :] = jnp.where(i == 0, 0.0, hv[:, 0:1024] * _sigmoid(hv[:, 1024:2048]))
        gs_ref[0, CONV_HALO:CONV_HALO + tt, :] = main_ref[:, 0:1024] * _sigmoid(main_ref[:, 1024:2048])
        _fill_shifted(gs_ref, tt + CONV_HALO)

        for c in range(D_MODEL // CONV_CC):
            cols = slice(c * CONV_CC, (c + 1) * CONV_CC)

            def chunk(j, carry):
                r = pl.multiple_of(j * CONV_RC, CONV_RC)
                c_ref[pl.ds(r, CONV_RC), cols] = _conv_taps(gs_ref, w_ref, r, cols, lead, False) + b_ref[:, cols]
                return carry
            lax.fori_loop(0, tt // CONV_RC, chunk, 0, unroll=CONV_UNROLL)

        _, _, cn = _layer_norm_fwd(c_ref[...], g_ref[...], lb_ref[...])
        o_ref[...] = (_silu(cn) * _silu(main_ref[:, 2048:3072])).astype(BF16)

        d = _project_out(o_ref[...], wo_ref, x_ref, p_ref, y_ref) - t_ref[...]
        dl_ref[...] = d * (1.0 / D_MODEL)

        @pl.when(i == 0)
        def _():
            l_ref[...] = jnp.zeros_like(l_ref)

        l_ref[...] += 0.5 * jnp.sum(jnp.mean(d * d, axis=-1, keepdims=True))

    vec = _const_spec((1, D_MODEL))
    row = pl.BlockSpec((tt, D_MODEL), lambda i: (i, 0))
    f32_rows = jax.ShapeDtypeStruct((t, D_MODEL), F32)
    return pl.pallas_call(
        body, name="layer1_fwd", grid=(t // tt,),
        out_shape=[jax.ShapeDtypeStruct((t, D_MODEL), BF16), f32_rows, f32_rows, f32_rows,
                   jax.ShapeDtypeStruct((8, 128), F32)],
        in_specs=[pl.BlockSpec((tt, 3 * D_MODEL), lambda i: (i, 0)),
                  pl.BlockSpec((CONV_HALO, 3 * D_MODEL), lambda i: (jnp.maximum(i * (tt // CONV_HALO) - 1, 0), 0)),
                  _const_spec((CONV_K, D_MODEL)), vec, vec, vec,
                  _const_spec((D_MODEL, D_MODEL)), row, vec, row],
        out_specs=[row, row, row, row, pl.BlockSpec((8, 128), lambda i: (0, 0))],
        scratch_shapes=[pltpu.VMEM((8, tt + CONV_HALO, D_MODEL), F32)],
        compiler_params=_params("arbitrary"),
    )(proj, proj, dw, dwb, lng, lnb, w_out, x_in, post, target)


def _layer1_bwd(proj, cf, gy, y, z, w_out, post, dw, lng, lnb, comm=None, tt=256):
    t = proj.shape[0]
    nt = t // tt
    te = tt + CONV_HALO

    def body(main_ref, next_ref, cf_ref, cfn_ref, gy_ref, gyn_ref, y_ref, yn_ref, z_ref, wo_ref, po_ref,
             w_ref, g_ref, lb_ref,
             o_ref, ddw_ref, ddb_ref, dg_ref, dlb_ref, dwo16_ref, dpo_ref, ds_ref, glu_ref, sb_ref, dwo_ref):
        i = pl.program_id(0)

        @pl.when(i == 0)
        def _():
            ddw_ref[...] = jnp.zeros_like(ddw_ref)
            ddb_ref[...] = jnp.zeros_like(ddb_ref)
            dg_ref[...] = jnp.zeros_like(dg_ref)
            dlb_ref[...] = jnp.zeros_like(dlb_ref)

        dzv = _post_bwd_rows(jnp.concatenate([gy_ref[...], gyn_ref[...]], axis=0),
                             jnp.concatenate([y_ref[...], yn_ref[...]], axis=0), z_ref[...], tt, i == 0, i == nt - 1,
                             po_ref, wo_ref, dwo_ref, dwo16_ref, dpo_ref)
        dzv = jnp.concatenate([dzv[0:tt], jnp.where(i < nt - 1, dzv[tt:], 0.0)], axis=0)
        lng = g_ref[...]
        chat, rstd, cn = _layer_norm_fwd(jnp.concatenate([cf_ref[...], cfn_ref[...]], axis=0), lng, lb_ref[...])
        gate = jnp.concatenate([main_ref[:, 2048:3072], next_ref[:, 2048:3072]], axis=0)
        silu_cn, dsilu_cn = _silu_and_grad(cn)
        silu_gate, dsilu_gate = _silu_and_grad(gate)
        o_ref[:, 2048:3072] = (dzv * silu_cn * dsilu_gate)[0:tt].astype(BF16)
        dcn = dzv * silu_gate * dsilu_cn
        dg_ref[...] += jnp.sum((dcn * chat)[0:tt], axis=0, keepdims=True)
        dlb_ref[...] += jnp.sum(dcn[0:tt], axis=0, keepdims=True)
        dchat = dcn * lng
        dcf = rstd * (dchat - jnp.mean(dchat, axis=-1, keepdims=True) - chat * jnp.mean(dchat * chat, axis=-1, keepdims=True))
        ddb_ref[...] += jnp.sum(dcf[0:tt], axis=0, keepdims=True)
        ds_ref[0, 0:te, :] = dcf
        ds_ref[0, te:, :] = jnp.zeros((8, D_MODEL), F32)
        _fill_shifted(ds_ref, te + 8)
        sb_ref[...] = _sigmoid(main_ref[:, 1024:2048])
        glu_ref[...] = main_ref[:, 0:1024] * sb_ref[...]

        for c in range(D_MODEL // CONV_CC):
            cols = slice(c * CONV_CC, (c + 1) * CONV_CC)
            gcols = slice(c * CONV_CC + 1024, (c + 1) * CONV_CC + 1024)

            def chunk(j, carry):
                r = pl.multiple_of(j * CONV_RC, CONV_RC)
                dglu = _conv_taps(ds_ref, w_ref, r, cols, 0, True)
                sb = sb_ref[pl.ds(r, CONV_RC), cols]
                o_ref[pl.ds(r, CONV_RC), cols] = (dglu * sb).astype(BF16)
                o_ref[pl.ds(r, CONV_RC), gcols] = (dglu * glu_ref[pl.ds(r, CONV_RC), cols] * (1.0 - sb)).astype(BF16)
                return carry
            lax.fori_loop(0, tt // CONV_RC, chunk, 0, unroll=CONV_UNROLL)

            def taps(j, accs):
                r = pl.multiple_of(j * CONV_RC, CONV_RC)
                gl = glu_ref[pl.ds(r, CONV_RC), cols]
                new = list(accs)
                for m, blk in _tap_blocks(ds_ref, r, cols, 0):
                    prod = blk * gl
                    part = prod[0:8]
                    for q in range(1, CONV_RC // 8):
                        part = part + prod[8 * q:8 * q + 8]
                    new[m] = new[m] + part
                return tuple(new)
            accs = lax.fori_loop(0, tt // CONV_RC, taps, tuple(jnp.zeros((8, CONV_CC), F32) for _ in range(CONV_K)))
            for m in range(CONV_K):
                k = CONV_K - 1 - m
                ddw_ref[k:k + 1, cols] += jnp.sum(accs[m], axis=0, keepdims=True)

    vec = _const_spec((1, D_MODEL))
    vec_out = pl.BlockSpec((1, D_MODEL), lambda i: (0, 0))
    row = pl.BlockSpec((tt, D_MODEL), lambda i: (i, 0))
    nxt = lambda i: (jnp.minimum((i + 1) * (tt // CONV_HALO), t // CONV_HALO - 1), 0)
    nxt_row = pl.BlockSpec((CONV_HALO, D_MODEL), nxt)
    vec_f32 = jax.ShapeDtypeStruct((1, D_MODEL), F32)
    square = _const_spec((D_MODEL, D_MODEL))
    return _fused_call(
        body, comm, (proj, proj, cf, cf, gy, gy, y, y, z, w_out, post, dw, lng, lnb), name="layer1_bwd", grid=(nt,),
        out_shape=[jax.ShapeDtypeStruct((t, 3 * D_MODEL), BF16), jax.ShapeDtypeStruct((CONV_K, D_MODEL), F32),
                   vec_f32, vec_f32, vec_f32, jax.ShapeDtypeStruct((D_MODEL, D_MODEL), BF16), vec_f32],
        in_specs=[pl.BlockSpec((tt, 3 * D_MODEL), lambda i: (i, 0)),
                  pl.BlockSpec((CONV_HALO, 3 * D_MODEL), nxt),
                  row, nxt_row, row, nxt_row, row, nxt_row, row, square, vec,
                  _const_spec((CONV_K, D_MODEL)), vec, vec],
        out_specs=[pl.BlockSpec((tt, 3 * D_MODEL), lambda i: (i, 0)),
                   pl.BlockSpec((CONV_K, D_MODEL), lambda i: (0, 0)), vec_out, vec_out, vec_out, square, vec_out],
        scratch_shapes=[pltpu.VMEM((8, te + 8, D_MODEL), F32), pltpu.VMEM((tt, D_MODEL), F32),
                        pltpu.VMEM((tt, D_MODEL), F32), pltpu.VMEM((D_MODEL, D_MODEL), F32)],
        params=_params("arbitrary"))


def _piece_sum(parts, place, name):
    r = parts[0][0].shape[1]

    def body(p_ref, *refs):
        o_ref = refs[-1]
        acc = refs[0][0].astype(F32)
        for part in refs[1:-1]:
            acc = acc + part[0].astype(F32)
        o_ref[0] = acc

    blk = (1, r, D_MODEL)
    spec = lambda slot: pl.BlockSpec(blk, lambda j, p_ref: (slot(p_ref), 0, 0))
    return pl.pallas_call(
        body, name=name,
        grid_spec=pltpu.PrefetchScalarGridSpec(
            num_scalar_prefetch=1, grid=(1,),
            in_specs=[spec(slot) for _, slot in parts],
            out_specs=pl.BlockSpec(blk, lambda j, p_ref: (p_ref[1], 0, 0))),
        out_shape=jax.ShapeDtypeStruct((2, r, D_MODEL), F32),
        compiler_params=_params("arbitrary"),
    )(place, *[a for a, _ in parts])


def _direct_parts(own, recv):
    peer = lambda m: (lambda p: p[0] ^ m)
    return [(own, peer(0))] + [(recv, peer(m)) for m in range(1, N_DEV)]


def _share_with_sibling(halves, name):
    n = len(halves)

    def body(*refs):
        outs = refs[n:2 * n]
        send_sems, recv_sems = refs[2 * n:]
        x, y, c, _ = _place()
        send = [pltpu.make_async_remote_copy(
            src_ref=outs[t].at[c], dst_ref=outs[t].at[c], send_sem=send_sems.at[t], recv_sem=recv_sems.at[t],
            device_id=(x, y, 1 - c), device_id_type=MESH_ID) for t in range(n)]
        recv = [pltpu.make_async_remote_copy(
            src_ref=outs[t].at[c], dst_ref=outs[t].at[1 - c], send_sem=send_sems.at[t], recv_sem=recv_sems.at[t],
            device_id=(x, y, 1 - c), device_id_type=MESH_ID) for t in range(n)]
        for cp in send:
            cp.start()
        for cp in recv:
            cp.wait_recv()
        for cp in send:
            cp.wait_send()

    return pl.pallas_call(
        body, name=name,
        out_shape=[jax.ShapeDtypeStruct(h.shape, h.dtype) for h in halves],
        in_specs=[ANY] * n, out_specs=[ANY] * n,
        input_output_aliases={t: t for t in range(n)},
        scratch_shapes=[pltpu.SemaphoreType.DMA((n,)), pltpu.SemaphoreType.DMA((n,))],
    )(*halves)


def _sum8(parts, name):
    r = parts.shape[1]

    def body(p_ref, o_ref):
        acc = p_ref[0]
        for k in range(1, N_DEV):
            acc = acc + p_ref[k]
        o_ref[...] = acc

    return pl.pallas_call(
        body, name=name, out_shape=jax.ShapeDtypeStruct((r, 128), F32),
        in_specs=[pl.BlockSpec(memory_space=pltpu.VMEM)], out_specs=pl.BlockSpec(memory_space=pltpu.VMEM),
    )(parts)


def _adamw(w, g, m, v, name):
    shape = w.shape
    cols = shape[-1]
    rows = w.size // cols
    rt = 256 if rows % 256 == 0 else rows

    def body(w_ref, g_ref, m_ref, v_ref, d_ref, nm_ref, nv_ref):
        gv = g_ref[...]
        mn = ADAM_B1 * m_ref[...] + (1.0 - ADAM_B1) * gv
        vn = ADAM_B2 * v_ref[...] + (1.0 - ADAM_B2) * (gv * gv)
        m_hat = mn / (1.0 - ADAM_B1 ** ADAM_STEP)
        v_hat = vn / (1.0 - ADAM_B2 ** ADAM_STEP)
        d_ref[...] = -ADAM_LR * (m_hat / (jnp.sqrt(v_hat) + ADAM_EPS) + ADAM_WD * w_ref[...])
        nm_ref[...] = mn
        nv_ref[...] = vn

    spec = pl.BlockSpec((rt, cols), lambda i: (i, 0))
    outs = pl.pallas_call(
        body, name=name, grid=(rows // rt,),
        out_shape=[jax.ShapeDtypeStruct((rows, cols), F32)] * 3,
        in_specs=[spec] * 4, out_specs=[spec] * 3,
        compiler_params=_params("parallel"),
    )(*[a.reshape(rows, cols) for a in (w, g, m, v)])
    return [o.reshape(shape) for o in outs]


SMALL_ROWS = 832


def _pack_small(g):
    parts = [g["loss"], g["pre1"].reshape(8, 128), g["post0"].reshape(8, 128),
             g["post1"].reshape(8, 128), g["sinks"], jnp.pad(g["pool_scale"].reshape(4, 128), ((0, 4), (0, 0))),
             g["pool_w"], g["dw"].reshape(248, 128), g["dwb"].reshape(8, 128), g["lng"].reshape(8, 128),
             g["lnb"].reshape(8, 128)]
    assert sum(p.shape[0] for p in parts) == SMALL_ROWS
    return jnp.concatenate(parts, axis=0)


def _unpack_small(s):
    out, r = {}, 0
    for key, rows, shape in (("loss", 8, (8, 128)), ("pre1", 8, (1, D_MODEL)), ("post", 16, (2, D_MODEL)),
                             ("sinks", 8, (8, 128)),
                             ("pool_scale", 4, (1, 512)), ("pad", 4, (4, 128)), ("pool_w", 512, (1, 4, 128, 128)),
                             ("dw", 248, (CONV_K, D_MODEL)), ("dwb", 8, (1, D_MODEL)), ("lng", 8, (1, D_MODEL)),
                             ("lnb", 8, (1, D_MODEL))):
        out[key] = s[r:r + rows].reshape(shape)
        r += rows
    return out


def kernel(x, pre_norm, post_norm, a_w_in, a_sinks, b_pool_w, b_pool_scale, ab_w_out, c_w_in, c_dw_w, c_dw_b, c_ln_g, c_ln_b, c_w_out, loss_target, m_pre_norm, m_post_norm, m_a_w_in, m_a_sinks, m_b_pool_w, m_b_pool_scale, m_ab_w_out, m_c_w_in, m_c_dw_w, m_c_dw_b, m_c_ln_g, m_c_ln_b, m_c_w_out, v_pre_norm, v_post_norm, v_a_w_in, v_a_sinks, v_b_pool_w, v_b_pool_scale, v_ab_w_out, v_c_w_in, v_c_dw_w, v_c_dw_b, v_c_ln_g, v_c_ln_b, v_c_w_out):
    ix, iy = lax.axis_index("x"), lax.axis_index("y")
    chip_cols = (2 * ix + iy) * 256

    pad8 = lambda v: jnp.pad(v, ((0, -v.shape[0] % 8), (0, 0)))
    vec_shard = jnp.concatenate([pad8(c_dw_w.reshape(CONV_K, 256)), pad8(c_dw_b), pad8(c_ln_g), pad8(c_ln_b),
                                 jnp.zeros((8, 256), F32)], axis=0)
    x0, target = x[0], loss_target[0]
    pre0, pre1 = pre_norm[0:1], pre_norm[1:2]
    post0, post1 = post_norm[0:1], post_norm[1:2]
    pool_w = b_pool_w[0]

    (wa_t,) = _run_comm(_Gather([a_w_in[0].T.astype(BF16)], halve=True), "gather_a_w_in")
    wa_t = wa_t.reshape(EVEN_IN, D_MODEL)
    proj0, (w_ab,) = _norm_matmul(x0, pre0, wa_t, "proj0_fwd", comm=_Gather([ab_w_out[0].astype(BF16)], halve=True))
    w_ab = w_ab.reshape(D_MODEL, D_MODEL)
    (mix0, y0, x1), (wc_t, w_c, vecs) = _layer0_fwd(
        proj0, a_sinks, pool_w, b_pool_scale, w_ab, x0, post0,
        comm=_Gather([c_w_in[0].T.astype(BF16), c_w_out[0].astype(BF16), vec_shard], halve=True))
    wc_t = wc_t.reshape(3 * D_MODEL, D_MODEL)
    w_c = w_c.reshape(D_MODEL, D_MODEL)
    vecs = vecs.reshape(4, 64, 256).transpose(1, 0, 2).reshape(64, D_MODEL)
    dw, dwb, lng, lnb = vecs[0:CONV_K], vecs[32:33], vecs[40:41], vecs[48:49]
    proj1, _ = _norm_matmul(x1, pre1, wc_t, "proj1_fwd")
    z1, cf1, y1, g2, loss = _layer1_fwd(proj1, dw, dwb, lng, lnb, w_c, x1, post1, target)

    pieces = lambda m: m.reshape(N_DEV, m.shape[0] // N_DEV, D_MODEL)
    (dproj1, d_dw, d_dwb, d_lng, d_lnb, d_wc, d_post1), _ = _layer1_bwd(proj1, cf1, g2, y1, z1, w_c, post1, dw, lng, lnb)
    (g1, d_wct, d_pre1), (r_wc,) = _pre_bwd(dproj1, wc_t, x1, pre1, g2, "proj1_bwd", comm=_Scatter([pieces(d_wc)]))
    (dproj0, d_sinks, d_pw, d_ps, d_wab, d_post0), (r_wct,) = _layer0_bwd(
        proj0, g1, y0, mix0, w_ab, post0, a_sinks, pool_w, b_pool_scale, comm=_Scatter([pieces(d_wct)]))
    g = dict(loss=loss, pre1=d_pre1, post0=d_post0, post1=d_post1, sinks=d_sinks, pool_w=d_pw, pool_scale=d_ps,
             dw=d_dw, dwb=d_dwb, lng=d_lng, lnb=d_lnb)
    d_wat, (small8, r_wab) = _proj_dw(dproj0, x0, pre0, "proj0_dw",
                                      comm=_Comms(_Gather([_pack_small(g)], halve=False), _Scatter([pieces(d_wab)])))
    sent = _scatter_start(pieces(d_wat), "scatter_a_start")
    (gx, d_pre0), _ = _proj_dx(dproj0, wa_t, x0, pre0 + sent[4][0:1, 0:1], g1, "proj0_dx")
    sent_pre0 = _scatter_start(jnp.broadcast_to(d_pre0.reshape(1, 8, 128), (N_DEV, 8, 128)), "pre0_start")
    own_wat, r_wat = _scatter_wait(*sent[:4], d_pre0, "scatter_a_wait")

    ic = lax.axis_index("c")
    me = 4 * ix + 2 * iy + ic
    place = jnp.stack([me, ic]).astype(jnp.int32)
    parts = [_direct_parts(own_wat, r_wat), _direct_parts(pieces(d_wab), r_wab), _direct_parts(pieces(d_wct), r_wct),
             _direct_parts(pieces(d_wc), r_wc)]
    halves = [_piece_sum(p, place, f"grad_sum{t}") for t, p in enumerate(parts)]
    _, pre0_8 = _scatter_wait(*sent_pre0[:4], halves[0], "pre0_wait")
    pre0_8 = lax.dynamic_update_slice(pre0_8, d_pre0.reshape(1, 8, 128), (me, 0, 0))
    g_wa_t, g_wab, g_wc_t, g_wc = [h.reshape(2 * h.shape[1], D_MODEL) for h in _share_with_sibling(halves, "grad_share")]
    g_a_w_in = g_wa_t.T[None]
    g_c_w_in = g_wc_t.T[None]
    g_ab_w_out = g_wab[None]
    g_c_w_out = g_wc[None]

    s = _unpack_small(_sum8(small8, "small_sum"))
    layer = lax.broadcasted_iota(jnp.int32, (2, D_MODEL), 0)
    g_pre = jnp.where(layer == 0, _sum8(pre0_8, "pre0_sum").reshape(1, D_MODEL), s["pre1"])
    g_post = s["post"]
    g_sinks = s["sinks"][:, 0].reshape(1, 8)
    g_pool_w, g_pool_scale = s["pool_w"], s["pool_scale"]
    g_dw = lax.dynamic_slice_in_dim(s["dw"], chip_cols, 256, axis=1).reshape(1, CONV_K, 1, 256)
    g_dwb = lax.dynamic_slice_in_dim(s["dwb"], chip_cols, 256, axis=1)
    g_lng = lax.dynamic_slice_in_dim(s["lng"], chip_cols, 256, axis=1)
    g_lnb = lax.dynamic_slice_in_dim(s["lnb"], chip_cols, 256, axis=1)

    grads = [g_pre, g_post, g_a_w_in, g_sinks, g_pool_w, g_pool_scale, g_ab_w_out, g_c_w_in, g_dw, g_dwb, g_lng, g_lnb,
             g_c_w_out]
    weights = [pre_norm, post_norm, a_w_in, a_sinks, b_pool_w, b_pool_scale, ab_w_out, c_w_in, c_dw_w, c_dw_b, c_ln_g,
               c_ln_b, c_w_out]
    moms = [m_pre_norm, m_post_norm, m_a_w_in, m_a_sinks, m_b_pool_w, m_b_pool_scale, m_ab_w_out, m_c_w_in, m_c_dw_w,
            m_c_dw_b, m_c_ln_g, m_c_ln_b, m_c_w_out]
    vars_ = [v_pre_norm, v_post_norm, v_a_w_in, v_a_sinks, v_b_pool_w, v_b_pool_scale, v_ab_w_out, v_c_w_in, v_c_dw_w,
             v_c_dw_b, v_c_ln_g, v_c_ln_b, v_c_w_out]
    deltas, new_m, new_v = [], [], []
    for k, (w, gr, m, v) in enumerate(zip(weights, grads, moms, vars_)):
        d, nm, nv = _adamw(w, gr, m, v, f"adamw{k}")
        deltas.append(d)
        new_m.append(nm)
        new_v.append(nv)
    return (s["loss"][0, 0], gx[None], *grads, *deltas, *new_m, *new_v)
```

```python
import functools

import jax
import jax.numpy as jnp
from jax import lax
from jax.experimental import pallas as pl
from jax.experimental.pallas import tpu as pltpu

F32 = jnp.float32
BF16 = jnp.bfloat16

D_MODEL = 1024
EPS = 1e-6
NEG = -1e30
HEAD_DIM = 64
GROUP = 4
KV_HEADS = 2
BLOCK = 128
EVEN_IN = 2304
COL_Q, COL_K, COL_GA, COL_U, COL_GB = 0, 512, 768, 1280, 1792
POOL_GROUPS = 4
POOL_GC = 128
POOL_HALO = 16
CONV_K = 31
CONV_HALO = 32
N_DEV = 8

ADAM_LR = 0.001
ADAM_B1 = 0.9
ADAM_B2 = 0.999
ADAM_EPS = 1e-08
ADAM_WD = 0.01
ADAM_STEP = 10

VMEM_LIMIT_BYTES = 56 * 1024 * 1024

NT = (((1,), (1,)), ((), ()))
TN = (((0,), (0,)), ((), ()))
MESH_ID = pl.DeviceIdType.MESH


def _params(*sem):
    return pltpu.CompilerParams(dimension_semantics=sem, vmem_limit_bytes=VMEM_LIMIT_BYTES)


def _const_spec(shape):
    nd = len(shape)
    return pl.BlockSpec(shape, lambda *_: (0,) * nd, pipeline_mode=pl.Buffered(1))


def _sigmoid(v):
    return 0.5 * jnp.tanh(0.5 * v) + 0.5


def _silu(v):
    return v * _sigmoid(v)


def _silu_and_grad(v):
    s = _sigmoid(v)
    return v * s, s * (1.0 + v * (1.0 - s))


ANY = pl.BlockSpec(memory_space=pl.ANY)


def _place():
    x, y, c = lax.axis_index("x"), lax.axis_index("y"), lax.axis_index("c")
    chips = [(1 - x, y), (x, 1 - y), (1 - x, 1 - y)]
    return x, y, c, chips


class _Gather:
    def __init__(self, blocks, halve):
        self.ins = list(blocks)
        self.halve = halve
        self.n = n = len(blocks)
        self.shapes = [((b.shape[0] // 2) if halve else b.shape[0], b.shape[1]) for b in blocks]
        self.out_shape = [jax.ShapeDtypeStruct((N_DEV, r, cols), b.dtype) for (r, cols), b in zip(self.shapes, blocks)]
        self.scratch = [pltpu.SemaphoreType.DMA((7 * n,)), pltpu.SemaphoreType.DMA((7 * n,)),
                        pltpu.SemaphoreType.DMA((n,))]

    def _copies(self, ins, outs, sems):
        send_sems, recv_sems, local_sems = sems
        x, y, c, chips = _place()
        me, sibling = (x, y, c), (x, y, 1 - c)

        def piece(t, px, py, pc):
            return outs[t].at[4 * px + 2 * py + pc]

        def own(t):
            return ins[t].at[pl.ds(c * self.shapes[t][0], self.shapes[t][0])] if self.halve else ins[t]

        def copy(t, k, block, to, src=None):
            return pltpu.make_async_remote_copy(
                src_ref=piece(t, *block) if src is None else src, dst_ref=piece(t, *block),
                send_sem=send_sems.at[7 * t + k], recv_sem=recv_sems.at[7 * t + k],
                device_id=to, device_id_type=MESH_ID)

        rng = range(self.n)
        return dict(
            mine=[pltpu.make_async_copy(own(t), piece(t, *me), local_sems.at[t]) for t in rng],
            first=[copy(t, 0, me, sibling, src=own(t)) for t in rng]
            + [copy(t, 1 + j, me, (*chip, c), src=own(t)) for t in rng for j, chip in enumerate(chips)],
            landed=[copy(t, 1 + j, (*chip, c), me) for j, chip in enumerate(chips) for t in rng],
            passed=[copy(t, 4 + j, (*chip, c), sibling) for j, chip in enumerate(chips) for t in rng],
            from_sibling=[copy(t, 0, sibling, me) for t in rng]
            + [copy(t, 4 + j, (*chip, 1 - c), me) for t in rng for j, chip in enumerate(chips)])

    def start(self, ins, outs, sems):
        d = self._copies(ins, outs, sems)
        for cp in d["mine"] + d["first"]:
            cp.start()

    def middle(self, ins, outs, sems):
        d = self._copies(ins, outs, sems)
        for got, fwd in zip(d["landed"], d["passed"]):
            got.wait_recv()
            fwd.start()

    def finish(self, ins, outs, sems):
        d = self._copies(ins, outs, sems)
        for cp in d["from_sibling"]:
            cp.wait_recv()
        for cp in d["first"] + d["passed"]:
            cp.wait_send()
        for cp in d["mine"]:
            cp.wait()


class _Scatter:
    def __init__(self, tensors):
        self.ins = list(tensors)
        self.n = n = len(tensors)
        self.out_shape = [jax.ShapeDtypeStruct(t.shape, t.dtype) for t in tensors]
        self.scratch = [pltpu.SemaphoreType.DMA((7 * n,)), pltpu.SemaphoreType.DMA((7 * n,))]

    def _copies(self, ins, outs, sems):
        send_sems, recv_sems = sems
        x, y, c, _ = _place()
        me = 4 * x + 2 * y + c
        sends, recvs = [], []
        for t in range(self.n):
            for m in range(1, N_DEV):
                px, py, pc = x ^ (m >> 2), y ^ ((m >> 1) & 1), c ^ (m & 1)
                q = 4 * px + 2 * py + pc
                sems_k = dict(send_sem=send_sems.at[7 * t + m - 1], recv_sem=recv_sems.at[7 * t + m - 1],
                              device_id=(px, py, pc), device_id_type=MESH_ID)
                sends.append(pltpu.make_async_remote_copy(src_ref=ins[t].at[q], dst_ref=outs[t].at[me], **sems_k))
                recvs.append(pltpu.make_async_remote_copy(src_ref=ins[t].at[me], dst_ref=outs[t].at[q], **sems_k))
        return sends, recvs

    def start(self, ins, outs, sems):
        for cp in self._copies(ins, outs, sems)[0]:
            cp.start()

    def middle(self, ins, outs, sems):
        pass

    def finish(self, ins, outs, sems):
        sends, recvs = self._copies(ins, outs, sems)
        for cp in recvs:
            cp.wait_recv()
        for cp in sends:
            cp.wait_send()


class _Comms:
    def __init__(self, *comms):
        self.comms = comms
        self.ins = [a for c in comms for a in c.ins]
        self.out_shape = [s for c in comms for s in c.out_shape]
        self.scratch = [s for c in comms for s in c.scratch]

    def _each(self, phase, ins, outs, sems):
        i = o = s = 0
        for c in self.comms:
            ni, no, ns = len(c.ins), len(c.out_shape), len(c.scratch)
            getattr(c, phase)(ins[i:i + ni], outs[o:o + no], sems[s:s + ns])
            i, o, s = i + ni, o + no, s + ns

    def start(self, ins, outs, sems):
        self._each("start", ins, outs, sems)

    def middle(self, ins, outs, sems):
        self._each("middle", ins, outs, sems)

    def finish(self, ins, outs, sems):
        self._each("finish", ins, outs, sems)


def _run_comm(comm, name):
    n = len(comm.ins)

    def body(*refs):
        parts = refs[:n], refs[n:2 * n], refs[2 * n:]
        comm.start(*parts)
        comm.middle(*parts)
        comm.finish(*parts)

    return pl.pallas_call(body, name=name, out_shape=comm.out_shape, in_specs=[ANY] * n, out_specs=[ANY] * n,
                          scratch_shapes=comm.scratch)(*comm.ins)


HBM_SPEC = pl.BlockSpec(memory_space=pltpu.HBM)
SEM_SPEC = pl.BlockSpec(memory_space=pltpu.SEMAPHORE)
DATAFLOW = pltpu.SideEffectType.DATAFLOW_SIDE_EFFECTING


def _scatter_copies(own_ref, land_ref, send_sems, recv_sems):
    x, y, c, _ = _place()
    me = 4 * x + 2 * y + c
    pairs = []
    for m in range(1, N_DEV):
        px, py, pc = x ^ (m >> 2), y ^ ((m >> 1) & 1), c ^ (m & 1)
        q = 4 * px + 2 * py + pc
        sems = dict(send_sem=send_sems.at[m - 1], recv_sem=recv_sems.at[m - 1], device_id=(px, py, pc),
                    device_id_type=MESH_ID)
        pairs.append((pltpu.make_async_remote_copy(src_ref=own_ref.at[q], dst_ref=land_ref.at[me], **sems),
                      pltpu.make_async_remote_copy(src_ref=own_ref.at[me], dst_ref=land_ref.at[q], **sems)))
    return pairs


def _scatter_start(own, name):
    def body(own_ref, land_ref, send_sems, recv_sems, own_thru, land_thru, token):
        for send, _ in _scatter_copies(own_ref, land_ref, send_sems, recv_sems):
            send.start()
        token[...] = jnp.zeros_like(token)

    buf = pltpu.HBM(own.shape, own.dtype)
    return pl.pallas_call(
        body, name=name,
        out_shape=(pltpu.SemaphoreType.DMA((N_DEV - 1,)), pltpu.SemaphoreType.DMA((N_DEV - 1,)), buf, buf,
                   jax.ShapeDtypeStruct((8, 128), F32)),
        in_specs=(HBM_SPEC, HBM_SPEC),
        out_specs=(SEM_SPEC, SEM_SPEC, HBM_SPEC, HBM_SPEC, pl.BlockSpec(memory_space=pltpu.VMEM)),
        input_output_aliases={0: 2, 1: 3},
        compiler_params=pltpu.CompilerParams(has_side_effects=DATAFLOW),
    )(pltpu.with_memory_space_constraint(own, pltpu.HBM),
      pltpu.with_memory_space_constraint(lax.empty(own.shape, own.dtype), pltpu.HBM))


def _scatter_wait(send_sems, recv_sems, own_thru, land_thru, after, name):
    def body(own_ref, land_ref, send_sems, recv_sems, after_ref, own_out, land_out):
        for send, recv in _scatter_copies(own_ref, land_ref, send_sems, recv_sems):
            send.wait_send()
            recv.wait_recv()

    buf = pltpu.HBM(own_thru.shape, own_thru.dtype)
    return pl.pallas_call(
        body, name=name, out_shape=(buf, buf),
        in_specs=(HBM_SPEC, HBM_SPEC, SEM_SPEC, SEM_SPEC, ANY), out_specs=(HBM_SPEC, HBM_SPEC),
        input_output_aliases={0: 0, 1: 1},
        compiler_params=pltpu.CompilerParams(has_side_effects=DATAFLOW),
    )(own_thru, land_thru, send_sems, recv_sems, after)


def _fused_call(body, comm, args, *, name, grid, out_shape, in_specs, out_specs, scratch_shapes=(), params):
    single = not isinstance(out_shape, (list, tuple))
    out_shape = [out_shape] if single else list(out_shape)
    out_specs = [out_specs] if single else list(out_specs)
    if comm is None:
        res = pl.pallas_call(body, name=name, grid=grid, out_shape=out_shape, in_specs=in_specs, out_specs=out_specs,
                             scratch_shapes=list(scratch_shapes), compiler_params=params)(*args)
        return (res[0] if single else res), []
    n_in, n_out, n_scr = len(in_specs), len(out_shape), len(scratch_shapes)
    c_in, c_out = len(comm.ins), len(comm.out_shape)
    steps = grid[0]

    def fused(*refs):
        pos = 0
        groups = []
        for size in (n_in, c_in, n_out, c_out, n_scr, len(comm.scratch)):
            groups.append(refs[pos:pos + size])
            pos += size
        ins, c_ins, outs, c_outs, scr, c_sems = groups
        i = pl.program_id(0)

        @pl.when(i == 0)
        def _():
            comm.start(c_ins, c_outs, c_sems)

        @pl.when(i == steps // 2)
        def _():
            comm.middle(c_ins, c_outs, c_sems)

        body(*ins, *outs, *scr)

        @pl.when(i == steps - 1)
        def _():
            comm.finish(c_ins, c_outs, c_sems)

    res = pl.pallas_call(
        fused, name=name, grid=grid, out_shape=out_shape + list(comm.out_shape),
        in_specs=list(in_specs) + [ANY] * c_in, out_specs=out_specs + [ANY] * c_out,
        scratch_shapes=list(scratch_shapes) + list(comm.scratch), compiler_params=params)(*args, *comm.ins)
    main = res[:n_out]
    return (main[0] if single else main), list(res[n_out:])


def _norm_matmul(x, gain, wt, name, comm=None, tm=1024):
    t, n = x.shape[0], wt.shape[0]

    def body(x_ref, g_ref, wt_ref, o_ref):
        xv = x_ref[...]
        r = lax.rsqrt(jnp.mean(xv * xv, axis=-1, keepdims=True) + EPS)
        h = (xv * r * g_ref[...]).astype(BF16)
        o_ref[...] = lax.dot_general(h, wt_ref[...], NT, preferred_element_type=F32)

    return _fused_call(
        body, comm, (x, gain, wt), name=name, grid=(t // tm,),
        out_shape=jax.ShapeDtypeStruct((t, n), F32),
        in_specs=[pl.BlockSpec((tm, D_MODEL), lambda i: (i, 0)), _const_spec((1, D_MODEL)), _const_spec((n, D_MODEL))],
        out_specs=pl.BlockSpec((tm, n), lambda i: (i, 0)),
        params=_params("arbitrary"))


def _project_out(a, w_ref, x_ref, p_ref, y_ref):
    y = jnp.dot(a, w_ref[...], preferred_element_type=F32)
    y_ref[...] = y
    ry = lax.rsqrt(jnp.mean(y * y, axis=-1, keepdims=True) + EPS)
    return x_ref[...] + (y * ry) * p_ref[...]


def _post_bwd_rows(g, y, a, n_own, first, last, p_ref, w_ref, dw_ref, dw16_ref, dp_ref):
    @pl.when(first)
    def _():
        dw_ref[...] = jnp.zeros_like(dw_ref)
        dp_ref[...] = jnp.zeros_like(dp_ref)

    ry = lax.rsqrt(jnp.mean(y * y, axis=-1, keepdims=True) + EPS)
    nv = y * ry
    dp_ref[...] += jnp.sum((g * nv)[0:n_own], axis=0, keepdims=True)
    dn = g * p_ref[...]
    dy = (ry * (dn - nv * jnp.mean(dn * nv, axis=-1, keepdims=True))).astype(BF16)
    dw_ref[...] += lax.dot_general(a, dy[0:n_own], TN, preferred_element_type=F32)

    @pl.when(last)
    def _():
        dw16_ref[...] = dw_ref[...].astype(BF16)

    return lax.dot_general(dy, w_ref[...], NT, preferred_element_type=F32)


def _pre_bwd(dproj, wt, x_in, pre, g, name, comm=None, tm=512):
    t, n = dproj.shape
    steps = t // tm

    def body(dp_ref, wt_ref, x_ref, pre_ref, g_ref, dx_ref, dwt16_ref, dpre_ref, dwt_ref):
        @pl.when(pl.program_id(0) == 0)
        def _():
            dwt_ref[...] = jnp.zeros_like(dwt_ref)
            dpre_ref[...] = jnp.zeros_like(dpre_ref)

        dpv = dp_ref[...]
        dh = jnp.dot(dpv, wt_ref[...], preferred_element_type=F32)
        xv = x_ref[...]
        r = lax.rsqrt(jnp.mean(xv * xv, axis=-1, keepdims=True) + EPS)
        xn = xv * r
        pv = pre_ref[...]
        dpre_ref[...] += jnp.sum(dh * xn, axis=0, keepdims=True)
        dxn = dh * pv
        dx_ref[...] = g_ref[...] + r * (dxn - xn * jnp.mean(dxn * xn, axis=-1, keepdims=True))
        h = (xn * pv).astype(BF16)
        dwt_ref[...] += lax.dot_general(dpv, h, TN, preferred_element_type=F32)

        @pl.when(pl.program_id(0) == steps - 1)
        def _():
            dwt16_ref[...] = dwt_ref[...].astype(BF16)

    row = pl.BlockSpec((tm, D_MODEL), lambda i: (i, 0))
    return _fused_call(
        body, comm, (dproj, wt, x_in, pre, g), name=name, grid=(steps,),
        out_shape=[jax.ShapeDtypeStruct((t, D_MODEL), F32), jax.ShapeDtypeStruct((n, D_MODEL), BF16),
                   jax.ShapeDtypeStruct((1, D_MODEL), F32)],
        in_specs=[pl.BlockSpec((tm, n), lambda i: (i, 0)), _const_spec((n, D_MODEL)), row, _const_spec((1, D_MODEL)), row],
        out_specs=[row, _const_spec((n, D_MODEL)), pl.BlockSpec((1, D_MODEL), lambda i: (0, 0))],
        scratch_shapes=[pltpu.VMEM((n, D_MODEL), F32)],
        params=_params("arbitrary"))


def _proj_dw(dproj, x_in, pre, name, comm=None, tm=1024):
    t, n = dproj.shape
    steps = t // tm

    def body(dp_ref, x_ref, pre_ref, dwt16_ref, dwt_ref):
        @pl.when(pl.program_id(0) == 0)
        def _():
            dwt_ref[...] = jnp.zeros_like(dwt_ref)

        xv = x_ref[...]
        r = lax.rsqrt(jnp.mean(xv * xv, axis=-1, keepdims=True) + EPS)
        h = (xv * r * pre_ref[...]).astype(BF16)
        dwt_ref[...] += lax.dot_general(dp_ref[...], h, TN, preferred_element_type=F32)

        @pl.when(pl.program_id(0) == steps - 1)
        def _():
            dwt16_ref[...] = dwt_ref[...].astype(BF16)

    return _fused_call(
        body, comm, (dproj, x_in, pre), name=name, grid=(steps,),
        out_shape=jax.ShapeDtypeStruct((n, D_MODEL), BF16),
        in_specs=[pl.BlockSpec((tm, n), lambda i: (i, 0)), pl.BlockSpec((tm, D_MODEL), lambda i: (i, 0)),
                  _const_spec((1, D_MODEL))],
        out_specs=pl.BlockSpec((n, D_MODEL), lambda i: (0, 0)),
        scratch_shapes=[pltpu.VMEM((n, D_MODEL), F32)],
        params=_params("arbitrary"))


def _proj_dx(dproj, wt, x_in, pre, g, name, comm=None, tm=512):
    t, n = dproj.shape

    def body(dp_ref, wt_ref, x_ref, pre_ref, g_ref, dx_ref, dpre_ref):
        @pl.when(pl.program_id(0) == 0)
        def _():
            dpre_ref[...] = jnp.zeros_like(dpre_ref)

        dh = jnp.dot(dp_ref[...], wt_ref[...], preferred_element_type=F32)
        xv = x_ref[...]
        r = lax.rsqrt(jnp.mean(xv * xv, axis=-1, keepdims=True) + EPS)
        xn = xv * r
        dpre_ref[...] += jnp.sum(dh * xn, axis=0, keepdims=True)
        dxn = dh * pre_ref[...]
        dx_ref[...] = g_ref[...] + r * (dxn - xn * jnp.mean(dxn * xn, axis=-1, keepdims=True))

    row = pl.BlockSpec((tm, D_MODEL), lambda i: (i, 0))
    return _fused_call(
        body, comm, (dproj, wt, x_in, pre, g), name=name, grid=(t // tm,),
        out_shape=[jax.ShapeDtypeStruct((t, D_MODEL), F32), jax.ShapeDtypeStruct((1, D_MODEL), F32)],
        in_specs=[pl.BlockSpec((tm, n), lambda i: (i, 0)), _const_spec((n, D_MODEL)), row, _const_spec((1, D_MODEL)), row],
        out_specs=[row, pl.BlockSpec((1, D_MODEL), lambda i: (0, 0))],
        params=_params("arbitrary"))


def _group_masks():
    lane = lax.broadcasted_iota(jnp.int32, (1, GROUP * HEAD_DIM), 1)
    return [(lane // HEAD_DIM == g).astype(F32) for g in range(GROUP)]


def _stack_groups(v, masks):
    return jnp.concatenate([v * m for m in masks], axis=0)


def _unstack_groups(v, masks):
    out = v[0:BLOCK] * masks[0]
    for g in range(1, GROUP):
        out = out + v[g * BLOCK:(g + 1) * BLOCK] * masks[g]
    return out


def _repeat_head(kv2, kvh):
    first = lax.broadcasted_iota(jnp.int32, kv2.shape, 1) < HEAD_DIM
    rolled = pltpu.roll(kv2, HEAD_DIM, 1)
    one = jnp.where(first, kv2, rolled) if kvh == 0 else jnp.where(first, rolled, kv2)
    return jnp.concatenate([one, one], axis=1)


def _fold_head(v4):
    a = v4[:, 0:128] + v4[:, 128:256]
    return a + pltpu.roll(a, HEAD_DIM, 1)


ATTN_CONSTS = [pltpu.VMEM((KV_HEADS, GROUP * BLOCK, 2 * BLOCK), F32)]


def _fill_attn_bias(bias_ref):
    row = lax.broadcasted_iota(jnp.int32, (GROUP * BLOCK, 2 * BLOCK), 0)
    col = lax.broadcasted_iota(jnp.int32, (GROUP * BLOCK, 2 * BLOCK), 1)
    dist = (row % BLOCK) + BLOCK - col
    band = (dist >= 0) & (dist < BLOCK)
    rb = lax.broadcasted_iota(jnp.int32, (GROUP * BLOCK, 1), 0) // BLOCK
    for kvh in range(KV_HEADS):
        slope = jnp.zeros((GROUP * BLOCK, 1), F32)
        for g in range(GROUP):
            slope = jnp.where(rb == g, 2.0 ** (-(kvh * GROUP + g + 1)), slope)
        bias_ref[kvh] = jnp.where(band, -slope * dist.astype(F32), NEG)


def _row_sinks(kvh, sink_ref):
    rb = lax.broadcasted_iota(jnp.int32, (GROUP * BLOCK, 1), 0) // BLOCK
    sink = jnp.zeros((GROUP * BLOCK, 1), F32)
    for g in range(GROUP):
        sink = jnp.where(rb == g, sink_ref[0, kvh * GROUP + g], sink)
    return sink


def _attn_probs(qk, k4, bias, sink, no_past, masks):
    qs = _stack_groups(qk, masks).astype(BF16)
    s = lax.dot_general(qs, k4, NT, preferred_element_type=F32) * (HEAD_DIM ** -0.5) + bias
    s = jnp.concatenate([jnp.where(no_past, NEG, s[:, 0:BLOCK]), s[:, BLOCK:]], axis=1)
    mx = jnp.maximum(jnp.max(s, axis=-1, keepdims=True), sink)
    e = jnp.exp(s - mx)
    es = jnp.exp(sink - mx)
    inv = 1.0 / (jnp.sum(e, axis=-1, keepdims=True) + es)
    return qs, e * inv, es * inv


def _pool_forward(u_ext, g, t0):
    n = u_ext.shape[0] - POOL_HALO
    s = u_ext
    for step in range(g + 1):
        s = s + pltpu.roll(s, 1 << step, 0)
    w = 2 << g
    t = t0 + lax.broadcasted_iota(jnp.int32, (n, 1), 0)
    cnt = jnp.minimum(t + 1, w).astype(F32)
    return s[POOL_HALO:] / cnt - u_ext[POOL_HALO:]


def _layer0_fwd(proj, sinks, pool_w, pool_scale, w_out, x_in, post, comm=None, tq=512):
    t = proj.shape[0]
    nblk = tq // BLOCK

    def body(main_ref, halo_ref, sink_ref, pw_ref, ps_ref, w_ref, x_ref, p_ref, o_ref, y_ref, xo_ref, kv_ref, bias_ref):
        i = pl.program_id(0)
        t0 = i * tq
        masks = _group_masks()

        @pl.when(i == 0)
        def _():
            _fill_attn_bias(bias_ref)

        kv_ref[0:BLOCK, :] = halo_ref[:, COL_K:COL_K + 256]
        kv_ref[BLOCK:, :] = main_ref[:, COL_K:COL_K + 256]

        def block(jb, carry):
            r0 = pl.multiple_of(jb * BLOCK, BLOCK)
            no_past = t0 + r0 == 0
            q = main_ref[pl.ds(r0, BLOCK), COL_Q:COL_Q + 512]
            ga = main_ref[pl.ds(r0, BLOCK), COL_GA:COL_GA + 512]
            kk = kv_ref[pl.ds(r0, 2 * BLOCK), 0:128]
            vv = kv_ref[pl.ds(r0, 2 * BLOCK), 128:256]
            outs = []
            for kvh in range(KV_HEADS):
                k4 = _repeat_head(kk, kvh).astype(BF16)
                v4 = _repeat_head(vv, kvh).astype(BF16)
                _, p, _ = _attn_probs(q[:, kvh * 256:(kvh + 1) * 256], k4, bias_ref[kvh], _row_sinks(kvh, sink_ref), no_past, masks)
                pv = jnp.dot(p.astype(BF16), v4, preferred_element_type=F32)
                outs.append(_unstack_groups(pv, masks))
            attn = jnp.concatenate(outs, axis=1)
            o_ref[pl.ds(r0, BLOCK), 0:512] = (attn * _silu(ga)).astype(BF16)
            return carry

        lax.fori_loop(0, nblk, block, 0, unroll=True)

        for g in range(POOL_GROUPS):
            cu = COL_U + g * POOL_GC
            cg = COL_GB + g * POOL_GC
            halo_u = jnp.where(i == 0, 0.0, halo_ref[BLOCK - POOL_HALO:BLOCK, cu:cu + POOL_GC])
            u_ext = jnp.concatenate([halo_u, main_ref[:, cu:cu + POOL_GC]], axis=0)
            pooled = _pool_forward(u_ext, g, t0)
            y = jnp.dot(pooled.astype(BF16), pw_ref[g].astype(BF16), preferred_element_type=F32)
            y = y * ps_ref[:, g * POOL_GC:(g + 1) * POOL_GC]
            o_ref[:, 512 + g * POOL_GC:512 + (g + 1) * POOL_GC] = (y * _silu(main_ref[:, cg:cg + POOL_GC])).astype(BF16)

        xo_ref[...] = _project_out(o_ref[...], w_ref, x_ref, p_ref, y_ref)

    row = pl.BlockSpec((tq, D_MODEL), lambda i: (i, 0))
    return _fused_call(
        body, comm, (proj, proj, sinks, pool_w, pool_scale, w_out, x_in, post), name="layer0_fwd", grid=(t // tq,),
        out_shape=[jax.ShapeDtypeStruct((t, D_MODEL), BF16), jax.ShapeDtypeStruct((t, D_MODEL), F32),
                   jax.ShapeDtypeStruct((t, D_MODEL), F32)],
        in_specs=[pl.BlockSpec((tq, EVEN_IN), lambda i: (i, 0)),
                  pl.BlockSpec((BLOCK, EVEN_IN), lambda i: (jnp.maximum(i * nblk - 1, 0), 0)),
                  pl.BlockSpec(memory_space=pltpu.SMEM),
                  _const_spec((POOL_GROUPS, POOL_GC, POOL_GC)), _const_spec((1, 512)),
                  _const_spec((D_MODEL, D_MODEL)), row, _const_spec((1, D_MODEL))],
        out_specs=[row, row, row],
        scratch_shapes=[pltpu.VMEM((tq + BLOCK, 256), F32)] + ATTN_CONSTS,
        params=_params("arbitrary"))


def _layer0_bwd(proj, gy, y, mix, w_out, post, sinks, pool_w, pool_scale, comm=None, tq=512):
    t = proj.shape[0]
    nt = t // tq
    nblk = tq // BLOCK

    def body(main_ref, halo_ref, next_ref, gy_ref, gyn_ref, y_ref, yn_ref, mix_ref, wo_ref, po_ref,
             sink_ref, pw_ref, ps_ref,
             o_ref, dsk_ref, dpw_ref, dps_ref, dwo16_ref, dpo_ref,
             kv_ref, dkv_ref, carry_ref, bias_ref, dwo_ref, dmix_ref):
        i = pl.program_id(0)
        ii = nt - 1 - i
        t0 = ii * tq
        masks = _group_masks()

        @pl.when(i == 0)
        def _():
            _fill_attn_bias(bias_ref)
            dsk_ref[...] = jnp.zeros_like(dsk_ref)
            dpw_ref[...] = jnp.zeros_like(dpw_ref)
            dps_ref[...] = jnp.zeros_like(dps_ref)
            carry_ref[...] = jnp.zeros_like(carry_ref)

        dmix_ref[...] = _post_bwd_rows(jnp.concatenate([gy_ref[...], gyn_ref[...]], axis=0),
                                       jnp.concatenate([y_ref[...], yn_ref[...]], axis=0), mix_ref[...], tq,
                                       i == 0, i == nt - 1, po_ref, wo_ref, dwo_ref, dwo16_ref, dpo_ref)
        dm_ref = dmix_ref.at[pl.ds(0, tq)]
        dmn_ref = dmix_ref.at[pl.ds(tq, POOL_HALO)]

        kv_ref[0:BLOCK, :] = halo_ref[:, COL_K:COL_K + 256]
        kv_ref[BLOCK:, :] = main_ref[:, COL_K:COL_K + 256]
        dkv_ref[0:tq, :] = jnp.zeros((tq, 256), F32)
        dkv_ref[tq:, :] = carry_ref[...]

        def block(jb, carry):
            r0 = pl.multiple_of(jb * BLOCK, BLOCK)
            no_past = t0 + r0 == 0
            q = main_ref[pl.ds(r0, BLOCK), COL_Q:COL_Q + 512]
            ga = main_ref[pl.ds(r0, BLOCK), COL_GA:COL_GA + 512]
            dya = dm_ref[pl.ds(r0, BLOCK), 0:512]
            kk = kv_ref[pl.ds(r0, 2 * BLOCK), 0:128]
            vv = kv_ref[pl.ds(r0, 2 * BLOCK), 128:256]
            silu_ga, dsilu_ga = _silu_and_grad(ga)
            do = dya * silu_ga
            first = lax.broadcasted_iota(jnp.int32, (2 * BLOCK, 128), 1) < HEAD_DIM
            attn, dq, dk, dv = [], [], [], []
            for kvh in range(KV_HEADS):
                k4 = _repeat_head(kk, kvh).astype(BF16)
                v4 = _repeat_head(vv, kvh).astype(BF16)
                qs, p, ps = _attn_probs(q[:, kvh * 256:(kvh + 1) * 256], k4, bias_ref[kvh], _row_sinks(kvh, sink_ref), no_past, masks)
                pb = p.astype(BF16)
                o_k = _unstack_groups(jnp.dot(pb, v4, preferred_element_type=F32), masks)
                do_k = do[:, kvh * 256:(kvh + 1) * 256]
                dos = _stack_groups(do_k, masks).astype(BF16)
                prod = do_k * o_k
                delta = jnp.concatenate([jnp.sum(prod * m, axis=-1, keepdims=True) for m in masks], axis=0)
                dp = lax.dot_general(dos, v4, NT, preferred_element_type=F32)
                ds = (p * (dp - delta)).astype(BF16)
                sink_term = ps * delta
                for g in range(GROUP):
                    h = kvh * GROUP + g
                    dsk_ref[h:h + 1, :] -= jnp.sum(sink_term[g * BLOCK:(g + 1) * BLOCK], keepdims=True)
                scale = HEAD_DIM ** -0.5
                dq.append(_unstack_groups(jnp.dot(ds, k4, preferred_element_type=F32), masks) * scale)
                dk.append(_fold_head(lax.dot_general(ds, qs, TN, preferred_element_type=F32)) * scale)
                dv.append(_fold_head(lax.dot_general(pb, dos, TN, preferred_element_type=F32)))
                attn.append(o_k)
            o_ref[pl.ds(r0, BLOCK), COL_Q:COL_Q + 512] = jnp.concatenate(dq, axis=1).astype(BF16)
            o_all = jnp.concatenate(attn, axis=1)
            o_ref[pl.ds(r0, BLOCK), COL_GA:COL_GA + 512] = (dya * o_all * dsilu_ga).astype(BF16)
            dkv = jnp.concatenate([jnp.where(first, dk[0], dk[1]), jnp.where(first, dv[0], dv[1])], axis=1)
            dkv_ref[pl.ds(r0, 2 * BLOCK), :] += dkv
            return carry

        lax.fori_loop(0, nblk, block, 0, unroll=True)
        carry_ref[...] = dkv_ref[0:BLOCK, :]
        o_ref[:, COL_K:COL_K + 256] = dkv_ref[BLOCK:, :].astype(BF16)

        last = ii == nt - 1
        for g in range(POOL_GROUPS):
            cu = COL_U + g * POOL_GC
            cg = COL_GB + g * POOL_GC
            cm = 512 + g * POOL_GC
            pw = pw_ref[g].astype(BF16)
            sc = ps_ref[:, g * POOL_GC:(g + 1) * POOL_GC]
            halo_u = jnp.where(ii == 0, 0.0, halo_ref[BLOCK - POOL_HALO:BLOCK, cu:cu + POOL_GC])
            u_ext = jnp.concatenate([halo_u, main_ref[:, cu:cu + POOL_GC]], axis=0)
            pooled = _pool_forward(u_ext, g, t0).astype(BF16)
            y_raw = jnp.dot(pooled, pw, preferred_element_type=F32)
            gb = main_ref[:, cg:cg + POOL_GC]
            dyb = dm_ref[:, cm:cm + POOL_GC]
            silu_gb, dsilu_gb = _silu_and_grad(gb)
            dypool = dyb * silu_gb
            dps_ref[:, g * POOL_GC:(g + 1) * POOL_GC] += jnp.sum(dypool * y_raw, axis=0, keepdims=True)
            o_ref[:, cg:cg + POOL_GC] = (dyb * (y_raw * sc) * dsilu_gb).astype(BF16)
            dyraw = dypool * sc
            dyraw_n = jnp.where(last, 0.0, dmn_ref[:, cm:cm + POOL_GC] * _silu(next_ref[:, cg:cg + POOL_GC]) * sc)
            dpw_ref[g * POOL_GC:(g + 1) * POOL_GC, :] += lax.dot_general(pooled, dyraw.astype(BF16), TN,
                                                                         preferred_element_type=F32)
            dyraw_ext = jnp.concatenate([dyraw, dyraw_n], axis=0).astype(BF16)
            dpooled = lax.dot_general(dyraw_ext, pw, NT, preferred_element_type=F32)
            w = 2 << g
            tt = t0 + lax.broadcasted_iota(jnp.int32, (tq + POOL_HALO, 1), 0)
            s = dpooled / jnp.minimum(tt + 1, w).astype(F32)
            for step in range(g + 1):
                s = s + pltpu.roll(s, tq + POOL_HALO - (1 << step), 0)
            o_ref[:, cu:cu + POOL_GC] = (s[0:tq] - dpooled[0:tq]).astype(BF16)

    rev = lambda i: nt - 1 - i
    nxt = lambda i: (jnp.minimum((rev(i) + 1) * (tq // POOL_HALO), t // POOL_HALO - 1), 0)
    row = pl.BlockSpec((tq, D_MODEL), lambda i: (rev(i), 0))
    nxt_row = pl.BlockSpec((POOL_HALO, D_MODEL), nxt)
    square = _const_spec((D_MODEL, D_MODEL))
    return _fused_call(
        body, comm, (proj, proj, proj, gy, gy, y, y, mix, w_out, post, sinks, pool_w, pool_scale),
        name="layer0_bwd", grid=(nt,),
        out_shape=[jax.ShapeDtypeStruct((t, EVEN_IN), BF16), jax.ShapeDtypeStruct((8, 128), F32),
                   jax.ShapeDtypeStruct((POOL_GROUPS * POOL_GC, POOL_GC), F32), jax.ShapeDtypeStruct((1, 512), F32),
                   jax.ShapeDtypeStruct((D_MODEL, D_MODEL), BF16), jax.ShapeDtypeStruct((1, D_MODEL), F32)],
        in_specs=[pl.BlockSpec((tq, EVEN_IN), lambda i: (rev(i), 0)),
                  pl.BlockSpec((BLOCK, EVEN_IN), lambda i: (jnp.maximum(rev(i) * nblk - 1, 0), 0)),
                  pl.BlockSpec((POOL_HALO, EVEN_IN), nxt),
                  row, nxt_row, row, nxt_row, row, square, _const_spec((1, D_MODEL)),
                  pl.BlockSpec(memory_space=pltpu.SMEM),
                  _const_spec((POOL_GROUPS, POOL_GC, POOL_GC)), _const_spec((1, 512))],
        out_specs=[pl.BlockSpec((tq, EVEN_IN), lambda i: (rev(i), 0)),
                   pl.BlockSpec((8, 128), lambda i: (0, 0)),
                   pl.BlockSpec((POOL_GROUPS * POOL_GC, POOL_GC), lambda i: (0, 0)),
                   pl.BlockSpec((1, 512), lambda i: (0, 0)), square, pl.BlockSpec((1, D_MODEL), lambda i: (0, 0))],
        scratch_shapes=[pltpu.VMEM((tq + BLOCK, 256), F32), pltpu.VMEM((tq + BLOCK, 256), F32),
                        pltpu.VMEM((BLOCK, 256), F32)] + ATTN_CONSTS
        + [pltpu.VMEM((D_MODEL, D_MODEL), F32), pltpu.VMEM((tq + POOL_HALO, D_MODEL), F32)],
        params=_params("arbitrary"))


CONV_RC = 32
CONV_CC = 128
CONV_CHAINS = 4
CONV_UNROLL = 2


def _fill_shifted(s_ref, rows):
    for b in range(1, 8):
        s_ref[b, 0:rows - 8, :] = s_ref[0, b:b + rows - 8, :]


def _tap_blocks(s_ref, r, cols, lead):
    for b in range(8):
        taps = [(a, 8 * a + b - lead) for a in range(5) if 0 <= 8 * a + b - lead < CONV_K]
        span = 8 * max(a for a, _ in taps) + CONV_RC
        blk = s_ref[b, pl.ds(r, span), cols]
        for a, k in taps:
            yield k, blk[8 * a:8 * a + CONV_RC]


def _conv_taps(s_ref, w_ref, r, cols, lead, reverse):
    accs = [None] * CONV_CHAINS
    for n, (k, blk) in enumerate(_tap_blocks(s_ref, r, cols, lead)):
        kw = CONV_K - 1 - k if reverse else k
        term = blk * w_ref[kw:kw + 1, cols]
        accs[n % CONV_CHAINS] = term if accs[n % CONV_CHAINS] is None else accs[n % CONV_CHAINS] + term
    return (accs[0] + accs[1]) + (accs[2] + accs[3])


def _layer_norm_fwd(cf, lng, lnb):
    mu = jnp.mean(cf, axis=-1, keepdims=True)
    xc = cf - mu
    rstd = lax.rsqrt(jnp.mean(xc * xc, axis=-1, keepdims=True) + EPS)
    chat = xc * rstd
    return chat, rstd, chat * lng + lnb


def _layer1_fwd(proj, dw, dwb, lng, lnb, w_out, x_in, post, target, tt=256):
    t = proj.shape[0]
    lead = CONV_HALO - (CONV_K - 1)

    def body(main_ref, halo_ref, w_ref, b_ref, g_ref, lb_ref, wo_ref, x_ref, p_ref, t_ref,
             o_ref, c_ref, y_ref, dl_ref, l_ref, gs_ref):
        i = pl.program_id(0)
        hv = halo_ref[...]
        gs_ref[0, 0:CONV_HALO, :] = jnp.where(i == 0, 0.0, hv[:, 0:1024] * _sigmoid(hv[:, 1024:2048]))
        gs_ref[0, CONV_HALO:CONV_HALO + tt, :] = main_ref[:, 0:1024] * _sigmoid(main_ref[:, 1024:2048])
        _fill_shifted(gs_ref, tt + CONV_HALO)

        for c in range(D_MODEL // CONV_CC):
            cols = slice(c * CONV_CC, (c + 1) * CONV_CC)

            def chunk(j, carry):
                r = pl.multiple_of(j * CONV_RC, CONV_RC)
                c_ref[pl.ds(r, CONV_RC), cols] = _conv_taps(gs_ref, w_ref, r, cols, lead, False) + b_ref[:, cols]
                return carry
            lax.fori_loop(0, tt // CONV_RC, chunk, 0, unroll=CONV_UNROLL)

        _, _, cn = _layer_norm_fwd(c_ref[...], g_ref[...], lb_ref[...])
        o_ref[...] = (_silu(cn) * _silu(main_ref[:, 2048:3072])).astype(BF16)

        d = _project_out(o_ref[...], wo_ref, x_ref, p_ref, y_ref) - t_ref[...]
        dl_ref[...] = d * (1.0 / D_MODEL)

        @pl.when(i == 0)
        def _():
            l_ref[...] = jnp.zeros_like(l_ref)

        l_ref[...] += 0.5 * jnp.sum(jnp.mean(d * d, axis=-1, keepdims=True))

    vec = _const_spec((1, D_MODEL))
    row = pl.BlockSpec((tt, D_MODEL), lambda i: (i, 0))
    f32_rows = jax.ShapeDtypeStruct((t, D_MODEL), F32)
    return pl.pallas_call(
        body, name="layer1_fwd", grid=(t // tt,),
        out_shape=[jax.ShapeDtypeStruct((t, D_MODEL), BF16), f32_rows, f32_rows, f32_rows,
                   jax.ShapeDtypeStruct((8, 128), F32)],
        in_specs=[pl.BlockSpec((tt, 3 * D_MODEL), lambda i: (i, 0)),
                  pl.BlockSpec((CONV_HALO, 3 * D_MODEL), lambda i: (jnp.maximum(i * (tt // CONV_HALO) - 1, 0), 0)),
                  _const_spec((CONV_K, D_MODEL)), vec, vec, vec,
                  _const_spec((D_MODEL, D_MODEL)), row, vec, row],
        out_specs=[row, row, row, row, pl.BlockSpec((8, 128), lambda i: (0, 0))],
        scratch_shapes=[pltpu.VMEM((8, tt + CONV_HALO, D_MODEL), F32)],
        compiler_params=_params("arbitrary"),
    )(proj, proj, dw, dwb, lng, lnb, w_out, x_in, post, target)


def _layer1_bwd(proj, cf, gy, y, z, w_out, post, dw, lng, lnb, comm=None, tt=256):
    t = proj.shape[0]
    nt = t // tt
    te = tt + CONV_HALO

    def body(main_ref, next_ref, cf_ref, cfn_ref, gy_ref, gyn_ref, y_ref, yn_ref, z_ref, wo_ref, po_ref,
             w_ref, g_ref, lb_ref,
             o_ref, ddw_ref, ddb_ref, dg_ref, dlb_ref, dwo16_ref, dpo_ref, ds_ref, glu_ref, sb_ref, dwo_ref):
        i = pl.program_id(0)

        @pl.when(i == 0)
        def _():
            ddw_ref[...] = jnp.zeros_like(ddw_ref)
            ddb_ref[...] = jnp.zeros_like(ddb_ref)
            dg_ref[...] = jnp.zeros_like(dg_ref)
            dlb_ref[...] = jnp.zeros_like(dlb_ref)

        dzv = _post_bwd_rows(jnp.concatenate([gy_ref[...], gyn_ref[...]], axis=0),
                             jnp.concatenate([y_ref[...], yn_ref[...]], axis=0), z_ref[...], tt, i == 0, i == nt - 1,
                             po_ref, wo_ref, dwo_ref, dwo16_ref, dpo_ref)
        dzv = jnp.concatenate([dzv[0:tt], jnp.where(i < nt - 1, dzv[tt:], 0.0)], axis=0)
        lng = g_ref[...]
        chat, rstd, cn = _layer_norm_fwd(jnp.concatenate([cf_ref[...], cfn_ref[...]], axis=0), lng, lb_ref[...])
        gate = jnp.concatenate([main_ref[:, 2048:3072], next_ref[:, 2048:3072]], axis=0)
        silu_cn, dsilu_cn = _silu_and_grad(cn)
        silu_gate, dsilu_gate = _silu_and_grad(gate)
        o_ref[:, 2048:3072] = (dzv * silu_cn * dsilu_gate)[0:tt].astype(BF16)
        dcn = dzv * silu_gate * dsilu_cn
        dg_ref[...] += jnp.sum((dcn * chat)[0:tt], axis=0, keepdims=True)
        dlb_ref[...] += jnp.sum(dcn[0:tt], axis=0, keepdims=True)
        dchat = dcn * lng
        dcf = rstd * (dchat - jnp.mean(dchat, axis=-1, keepdims=True) - chat * jnp.mean(dchat * chat, axis=-1, keepdims=True))
        ddb_ref[...] += jnp.sum(dcf[0:tt], axis=0, keepdims=True)
        ds_ref[0, 0:te, :] = dcf
        ds_ref[0, te:, :] = jnp.zeros((8, D_MODEL), F32)
        _fill_shifted(ds_ref, te + 8)
        sb_ref[...] = _sigmoid(main_ref[:, 1024:2048])
        glu_ref[...] = main_ref[:, 0:1024] * sb_ref[...]

        for c in range(D_MODEL // CONV_CC):
            cols = slice(c * CONV_CC, (c + 1) * CONV_CC)
            gcols = slice(c * CONV_CC + 1024, (c + 1) * CONV_CC + 1024)

            def chunk(j, carry):
                r = pl.multiple_of(j * CONV_RC, CONV_RC)
                dglu = _conv_taps(ds_ref, w_ref, r, cols, 0, True)
                sb = sb_ref[pl.ds(r, CONV_RC), cols]
                o_ref[pl.ds(r, CONV_RC), cols] = (dglu * sb).astype(BF16)
                o_ref[pl.ds(r, CONV_RC), gcols] = (dglu * glu_ref[pl.ds(r, CONV_RC), cols] * (1.0 - sb)).astype(BF16)
                return carry
            lax.fori_loop(0, tt // CONV_RC, chunk, 0, unroll=CONV_UNROLL)

            def taps(j, accs):
                r = pl.multiple_of(j * CONV_RC, CONV_RC)
                gl = glu_ref[pl.ds(r, CONV_RC), cols]
                new = list(accs)
                for m, blk in _tap_blocks(ds_ref, r, cols, 0):
                    prod = blk * gl
                    part = prod[0:8]
                    for q in range(1, CONV_RC // 8):
                        part = part + prod[8 * q:8 * q + 8]
                    new[m] = new[m] + part
                return tuple(new)
            accs = lax.fori_loop(0, tt // CONV_RC, taps, tuple(jnp.zeros((8, CONV_CC), F32) for _ in range(CONV_K)))
            for m in range(CONV_K):
                k = CONV_K - 1 - m
                ddw_ref[k:k + 1, cols] += jnp.sum(accs[m], axis=0, keepdims=True)

    vec = _const_spec((1, D_MODEL))
    vec_out = pl.BlockSpec((1, D_MODEL), lambda i: (0, 0))
    row = pl.BlockSpec((tt, D_MODEL), lambda i: (i, 0))
    nxt = lambda i: (jnp.minimum((i + 1) * (tt // CONV_HALO), t // CONV_HALO - 1), 0)
    nxt_row = pl.BlockSpec((CONV_HALO, D_MODEL), nxt)
    vec_f32 = jax.ShapeDtypeStruct((1, D_MODEL), F32)
    square = _const_spec((D_MODEL, D_MODEL))
    return _fused_call(
        body, comm, (proj, proj, cf, cf, gy, gy, y, y, z, w_out, post, dw, lng, lnb), name="layer1_bwd", grid=(nt,),
        out_shape=[jax.ShapeDtypeStruct((t, 3 * D_MODEL), BF16), jax.ShapeDtypeStruct((CONV_K, D_MODEL), F32),
                   vec_f32, vec_f32, vec_f32, jax.ShapeDtypeStruct((D_MODEL, D_MODEL), BF16), vec_f32],
        in_specs=[pl.BlockSpec((tt, 3 * D_MODEL), lambda i: (i, 0)),
                  pl.BlockSpec((CONV_HALO, 3 * D_MODEL), nxt),
                  row, nxt_row, row, nxt_row, row, nxt_row, row, square, vec,
                  _const_spec((CONV_K, D_MODEL)), vec, vec],
        out_specs=[pl.BlockSpec((tt, 3 * D_MODEL), lambda i: (i, 0)),
                   pl.BlockSpec((CONV_K, D_MODEL), lambda i: (0, 0)), vec_out, vec_out, vec_out, square, vec_out],
        scratch_shapes=[pltpu.VMEM((8, te + 8, D_MODEL), F32), pltpu.VMEM((tt, D_MODEL), F32),
                        pltpu.VMEM((tt, D_MODEL), F32), pltpu.VMEM((D_MODEL, D_MODEL), F32)],
        params=_params("arbitrary"))


def _piece_sum(parts, place, name):
    r = parts[0][0].shape[1]

    def body(p_ref, *refs):
        o_ref = refs[-1]
        acc = refs[0][0].astype(F32)
        for part in refs[1:-1]:
            acc = acc + part[0].astype(F32)
        o_ref[0] = acc

    blk = (1, r, D_MODEL)
    spec = lambda slot: pl.BlockSpec(blk, lambda j, p_ref: (slot(p_ref), 0, 0))
    return pl.pallas_call(
        body, name=name,
        grid_spec=pltpu.PrefetchScalarGridSpec(
            num_scalar_prefetch=1, grid=(1,),
            in_specs=[spec(slot) for _, slot in parts],
            out_specs=pl.BlockSpec(blk, lambda j, p_ref: (p_ref[1], 0, 0))),
        out_shape=jax.ShapeDtypeStruct((2, r, D_MODEL), F32),
        compiler_params=_params("arbitrary"),
    )(place, *[a for a, _ in parts])


def _direct_parts(own, recv):
    peer = lambda m: (lambda p: p[0] ^ m)
    return [(own, peer(0))] + [(recv, peer(m)) for m in range(1, N_DEV)]


def _share_with_sibling(halves, name):
    n = len(halves)

    def body(*refs):
        outs = refs[n:2 * n]
        send_sems, recv_sems = refs[2 * n:]
        x, y, c, _ = _place()
        send = [pltpu.make_async_remote_copy(
            src_ref=outs[t].at[c], dst_ref=outs[t].at[c], send_sem=send_sems.at[t], recv_sem=recv_sems.at[t],
            device_id=(x, y, 1 - c), device_id_type=MESH_ID) for t in range(n)]
        recv = [pltpu.make_async_remote_copy(
            src_ref=outs[t].at[c], dst_ref=outs[t].at[1 - c], send_sem=send_sems.at[t], recv_sem=recv_sems.at[t],
            device_id=(x, y, 1 - c), device_id_type=MESH_ID) for t in range(n)]
        for cp in send:
            cp.start()
        for cp in recv:
            cp.wait_recv()
        for cp in send:
            cp.wait_send()

    return pl.pallas_call(
        body, name=name,
        out_shape=[jax.ShapeDtypeStruct(h.shape, h.dtype) for h in halves],
        in_specs=[ANY] * n, out_specs=[ANY] * n,
        input_output_aliases={t: t for t in range(n)},
        scratch_shapes=[pltpu.SemaphoreType.DMA((n,)), pltpu.SemaphoreType.DMA((n,))],
    )(*halves)


def _sum8(parts, name):
    r = parts.shape[1]

    def body(p_ref, o_ref):
        acc = p_ref[0]
        for k in range(1, N_DEV):
            acc = acc + p_ref[k]
        o_ref[...] = acc

    return pl.pallas_call(
        body, name=name, out_shape=jax.ShapeDtypeStruct((r, 128), F32),
        in_specs=[pl.BlockSpec(memory_space=pltpu.VMEM)], out_specs=pl.BlockSpec(memory_space=pltpu.VMEM),
    )(parts)


def _adamw(w, g, m, v, name):
    shape = w.shape
    cols = shape[-1]
    rows = w.size // cols
    rt = 256 if rows % 256 == 0 else rows

    def body(w_ref, g_ref, m_ref, v_ref, d_ref, nm_ref, nv_ref):
        gv = g_ref[...]
        mn = ADAM_B1 * m_ref[...] + (1.0 - ADAM_B1) * gv
        vn = ADAM_B2 * v_ref[...] + (1.0 - ADAM_B2) * (gv * gv)
        m_hat = mn / (1.0 - ADAM_B1 ** ADAM_STEP)
        v_hat = vn / (1.0 - ADAM_B2 ** ADAM_STEP)
        d_ref[...] = -ADAM_LR * (m_hat / (jnp.sqrt(v_hat) + ADAM_EPS) + ADAM_WD * w_ref[...])
        nm_ref[...] = mn
        nv_ref[...] = vn

    spec = pl.BlockSpec((rt, cols), lambda i: (i, 0))
    outs = pl.pallas_call(
        body, name=name, grid=(rows // rt,),
        out_shape=[jax.ShapeDtypeStruct((rows, cols), F32)] * 3,
        in_specs=[spec] * 4, out_specs=[spec] * 3,
        compiler_params=_params("parallel"),
    )(*[a.reshape(rows, cols) for a in (w, g, m, v)])
    return [o.reshape(shape) for o in outs]


SMALL_ROWS = 832


def _pack_small(g):
    parts = [g["loss"], g["pre1"].reshape(8, 128), g["post0"].reshape(8, 128),
             g["post1"].reshape(8, 128), g["sinks"], jnp.pad(g["pool_scale"].reshape(4, 128), ((0, 4), (0, 0))),
             g["pool_w"], g["dw"].reshape(248, 128), g["dwb"].reshape(8, 128), g["lng"].reshape(8, 128),
             g["lnb"].reshape(8, 128)]
    assert sum(p.shape[0] for p in parts) == SMALL_ROWS
    return jnp.concatenate(parts, axis=0)


def _unpack_small(s):
    out, r = {}, 0
    for key, rows, shape in (("loss", 8, (8, 128)), ("pre1", 8, (1, D_MODEL)), ("post", 16, (2, D_MODEL)),
                             ("sinks", 8, (8, 128)),
                             ("pool_scale", 4, (1, 512)), ("pad", 4, (4, 128)), ("pool_w", 512, (1, 4, 128, 128)),
                             ("dw", 248, (CONV_K, D_MODEL)), ("dwb", 8, (1, D_MODEL)), ("lng", 8, (1, D_MODEL)),
                             ("lnb", 8, (1, D_MODEL))):
        out[key] = s[r:r + rows].reshape(shape)
        r += rows
    return out


def kernel(x, pre_norm, post_norm, a_w_in, a_sinks, b_pool_w, b_pool_scale, ab_w_out, c_w_in, c_dw_w, c_dw_b, c_ln_g, c_ln_b, c_w_out, loss_target, m_pre_norm, m_post_norm, m_a_w_in, m_a_sinks, m_b_pool_w, m_b_pool_scale, m_ab_w_out, m_c_w_in, m_c_dw_w, m_c_dw_b, m_c_ln_g, m_c_ln_b, m_c_w_out, v_pre_norm, v_post_norm, v_a_w_in, v_a_sinks, v_b_pool_w, v_b_pool_scale, v_ab_w_out, v_c_w_in, v_c_dw_w, v_c_dw_b, v_c_ln_g, v_c_ln_b, v_c_w_out):
    ix, iy = lax.axis_index("x"), lax.axis_index("y")
    chip_cols = (2 * ix + iy) * 256

    pad8 = lambda v: jnp.pad(v, ((0, -v.shape[0] % 8), (0, 0)))
    vec_shard = jnp.concatenate([pad8(c_dw_w.reshape(CONV_K, 256)), pad8(c_dw_b), pad8(c_ln_g), pad8(c_ln_b),
                                 jnp.zeros((8, 256), F32)], axis=0)
    x0, target = x[0], loss_target[0]
    pre0, pre1 = pre_norm[0:1], pre_norm[1:2]
    post0, post1 = post_norm[0:1], post_norm[1:2]
    pool_w = b_pool_w[0]

    (wa_t,) = _run_comm(_Gather([a_w_in[0].T.astype(BF16)], halve=True), "gather_a_w_in")
    wa_t = wa_t.reshape(EVEN_IN, D_MODEL)
    proj0, (w_ab,) = _norm_matmul(x0, pre0, wa_t, "proj0_fwd", comm=_Gather([ab_w_out[0].astype(BF16)], halve=True))
    w_ab = w_ab.reshape(D_MODEL, D_MODEL)
    (mix0, y0, x1), (wc_t, w_c, vecs) = _layer0_fwd(
        proj0, a_sinks, pool_w, b_pool_scale, w_ab, x0, post0,
        comm=_Gather([c_w_in[0].T.astype(BF16), c_w_out[0].astype(BF16), vec_shard], halve=True))
    wc_t = wc_t.reshape(3 * D_MODEL, D_MODEL)
    w_c = w_c.reshape(D_MODEL, D_MODEL)
    vecs = vecs.reshape(4, 64, 256).transpose(1, 0, 2).reshape(64, D_MODEL)
    dw, dwb, lng, lnb = vecs[0:CONV_K], vecs[32:33], vecs[40:41], vecs[48:49]
    proj1, _ = _norm_matmul(x1, pre1, wc_t, "proj1_fwd")
    z1, cf1, y1, g2, loss = _layer1_fwd(proj1, dw, dwb, lng, lnb, w_c, x1, post1, target)

    pieces = lambda m: m.reshape(N_DEV, m.shape[0] // N_DEV, D_MODEL)
    (dproj1, d_dw, d_dwb, d_lng, d_lnb, d_wc, d_post1), _ = _layer1_bwd(proj1, cf1, g2, y1, z1, w_c, post1, dw, lng, lnb)
    (g1, d_wct, d_pre1), (r_wc,) = _pre_bwd(dproj1, wc_t, x1, pre1, g2, "proj1_bwd", comm=_Scatter([pieces(d_wc)]))
    (dproj0, d_sinks, d_pw, d_ps, d_wab, d_post0), (r_wct,) = _layer0_bwd(
        proj0, g1, y0, mix0, w_ab, post0, a_sinks, pool_w, b_pool_scale, comm=_Scatter([pieces(d_wct)]))
    g = dict(loss=loss, pre1=d_pre1, post0=d_post0, post1=d_post1, sinks=d_sinks, pool_w=d_pw, pool_scale=d_ps,
             dw=d_dw, dwb=d_dwb, lng=d_lng, lnb=d_lnb)
    d_wat, (small8, r_wab) = _proj_dw(dproj0, x0, pre0, "proj0_dw",
                                      comm=_Comms(_Gather([_pack_small(g)], halve=False), _Scatter([pieces(d_wab)])))
    sent = _scatter_start(pieces(d_wat), "scatter_a_start")
    (gx, d_pre0), _ = _proj_dx(dproj0, wa_t, x0, pre0 + sent[4][0:1, 0:1], g1, "proj0_dx")
    sent_pre0 = _scatter_start(jnp.broadcast_to(d_pre0.reshape(1, 8, 128), (N_DEV, 8, 128)), "pre0_start")
    own_wat, r_wat = _scatter_wait(*sent[:4], d_pre0, "scatter_a_wait")

    ic = lax.axis_index("c")
    me = 4 * ix + 2 * iy + ic
    place = jnp.stack([me, ic]).astype(jnp.int32)
    parts = [_direct_parts(own_wat, r_wat), _direct_parts(pieces(d_wab), r_wab), _direct_parts(pieces(d_wct), r_wct),
             _direct_parts(pieces(d_wc), r_wc)]
    halves = [_piece_sum(p, place, f"grad_sum{t}") for t, p in enumerate(parts)]
    _, pre0_8 = _scatter_wait(*sent_pre0[:4], halves[0], "pre0_wait")
    pre0_8 = lax.dynamic_update_slice(pre0_8, d_pre0.reshape(1, 8, 128), (me, 0, 0))
    g_wa_t, g_wab, g_wc_t, g_wc = [h.reshape(2 * h.shape[1], D_MODEL) for h in _share_with_sibling(halves, "grad_share")]
    g_c_w_in = g_wc_t.T[None]
    g_ab_w_out = g_wab[None]
    g_c_w_out = g_wc[None]

    s = _unpack_small(_sum8(small8, "small_sum"))
    layer = lax.broadcasted_iota(jnp.int32, (2, D_MODEL), 0)
    g_pre = jnp.where(layer == 0, _sum8(pre0_8, "pre0_sum").reshape(1, D_MODEL), s["pre1"])
    g_post = s["post"]
    g_sinks = s["sinks"][:, 0].reshape(1, 8)
    g_pool_w, g_pool_scale = s["pool_w"], s["pool_scale"]
    g_dw = lax.dynamic_slice_in_dim(s["dw"], chip_cols, 256, axis=1).reshape(1, CONV_K, 1, 256)
    g_dwb = lax.dynamic_slice_in_dim(s["dwb"], chip_cols, 256, axis=1)
    g_lng = lax.dynamic_slice_in_dim(s["lng"], chip_cols, 256, axis=1)
    g_lnb = lax.dynamic_slice_in_dim(s["lnb"], chip_cols, 256, axis=1)

    turn = lambda a: jnp.swapaxes(a, 1, 2)
    a_w_in, m_a_w_in, v_a_w_in = turn(a_w_in), turn(m_a_w_in), turn(v_a_w_in)
    grads = [g_pre, g_post, g_wa_t[None], g_sinks, g_pool_w, g_pool_scale, g_ab_w_out, g_c_w_in, g_dw, g_dwb, g_lng,
             g_lnb, g_c_w_out]
    weights = [pre_norm, post_norm, a_w_in, a_sinks, b_pool_w, b_pool_scale, ab_w_out, c_w_in, c_dw_w, c_dw_b, c_ln_g,
               c_ln_b, c_w_out]
    moms = [m_pre_norm, m_post_norm, m_a_w_in, m_a_sinks, m_b_pool_w, m_b_pool_scale, m_ab_w_out, m_c_w_in, m_c_dw_w,
            m_c_dw_b, m_c_ln_g, m_c_ln_b, m_c_w_out]
    vars_ = [v_pre_norm, v_post_norm, v_a_w_in, v_a_sinks, v_b_pool_w, v_b_pool_scale, v_ab_w_out, v_c_w_in, v_c_dw_w,
             v_c_dw_b, v_c_ln_g, v_c_ln_b, v_c_w_out]
    deltas, new_m, new_v = [], [], []
    for k, (w, gr, m, v) in enumerate(zip(weights, grads, moms, vars_)):
        d, nm, nv = _adamw(w, gr, m, v, f"adamw{k}")
        deltas.append(d)
        new_m.append(nm)
        new_v.append(nv)
    for outs in (grads, deltas, new_m, new_v):
        outs[2] = turn(outs[2])
    return (s["loss"][0, 0], gx[None], *grads, *deltas, *new_m, *new_v)
```

```python
import jax
import jax.numpy as jnp
from jax import lax
from jax.experimental import pallas as pl
from jax.experimental.pallas import tpu as pltpu

F32 = jnp.float32
BF16 = jnp.bfloat16

D_MODEL = 1024
EPS = 1e-6
NEG = -1e30
HEAD_DIM = 64
GROUP = 4
KV_HEADS = 2
BLOCK = 128
EVEN_IN = 2304
ATTN_WIDTH = 512
POOL_WIDTH = 512
COL_Q, COL_K, COL_GA, COL_U, COL_GB = 0, 512, 768, 1280, 1792
POOL_GROUPS = 4
POOL_GC = 128
POOL_HALO = 16
CONV_K = 31
CONV_HALO = 32
GLU_A = slice(0, D_MODEL)
GLU_B = slice(D_MODEL, 2 * D_MODEL)
GATE = slice(2 * D_MODEL, 3 * D_MODEL)
N_DEV = 8

ADAM_LR = 0.001
ADAM_B1 = 0.9
ADAM_B2 = 0.999
ADAM_EPS = 1e-08
ADAM_WD = 0.01
ADAM_STEP = 10

VMEM_LIMIT_BYTES = 56 * 1024 * 1024

NT = (((1,), (1,)), ((), ()))
TN = (((0,), (0,)), ((), ()))
MESH_ID = pl.DeviceIdType.MESH


def _params(*sem):
    return pltpu.CompilerParams(dimension_semantics=sem, vmem_limit_bytes=VMEM_LIMIT_BYTES)


def _const_spec(shape):
    nd = len(shape)
    return pl.BlockSpec(shape, lambda *_: (0,) * nd, pipeline_mode=pl.Buffered(1))


def _sigmoid(v):
    return 0.5 * jnp.tanh(0.5 * v) + 0.5


def _silu(v):
    return v * _sigmoid(v)


def _silu_and_grad(v):
    s = _sigmoid(v)
    return v * s, s * (1.0 + v * (1.0 - s))


ANY = pl.BlockSpec(memory_space=pl.ANY)


def _place():
    x, y, c = lax.axis_index("x"), lax.axis_index("y"), lax.axis_index("c")
    chips = [(1 - x, y), (x, 1 - y), (1 - x, 1 - y)]
    return x, y, c, chips


class _Gather:
    def __init__(self, blocks, halve):
        self.ins = list(blocks)
        self.halve = halve
        self.n = n = len(blocks)
        self.shapes = [((b.shape[0] // 2) if halve else b.shape[0], b.shape[1]) for b in blocks]
        self.out_shape = [jax.ShapeDtypeStruct((N_DEV, r, cols), b.dtype) for (r, cols), b in zip(self.shapes, blocks)]
        self.scratch = [pltpu.SemaphoreType.DMA((7 * n,)), pltpu.SemaphoreType.DMA((7 * n,)),
                        pltpu.SemaphoreType.DMA((n,))]

    def _copies(self, ins, outs, sems):
        send_sems, recv_sems, local_sems = sems
        x, y, c, chips = _place()
        me, sibling = (x, y, c), (x, y, 1 - c)

        def piece(t, px, py, pc):
            return outs[t].at[4 * px + 2 * py + pc]

        def own(t):
            return ins[t].at[pl.ds(c * self.shapes[t][0], self.shapes[t][0])] if self.halve else ins[t]

        def copy(t, k, block, to, src=None):
            return pltpu.make_async_remote_copy(
                src_ref=piece(t, *block) if src is None else src, dst_ref=piece(t, *block),
                send_sem=send_sems.at[7 * t + k], recv_sem=recv_sems.at[7 * t + k],
                device_id=to, device_id_type=MESH_ID)

        rng = range(self.n)
        return dict(
            mine=[pltpu.make_async_copy(own(t), piece(t, *me), local_sems.at[t]) for t in rng],
            first=[copy(t, 0, me, sibling, src=own(t)) for t in rng]
            + [copy(t, 1 + j, me, (*chip, c), src=own(t)) for t in rng for j, chip in enumerate(chips)],
            landed=[copy(t, 1 + j, (*chip, c), me) for j, chip in enumerate(chips) for t in rng],
            passed=[copy(t, 4 + j, (*chip, c), sibling) for j, chip in enumerate(chips) for t in rng],
            from_sibling=[copy(t, 0, sibling, me) for t in rng]
            + [copy(t, 4 + j, (*chip, 1 - c), me) for t in rng for j, chip in enumerate(chips)])

    def start(self, ins, outs, sems):
        d = self._copies(ins, outs, sems)
        for cp in d["mine"] + d["first"]:
            cp.start()

    def middle(self, ins, outs, sems):
        d = self._copies(ins, outs, sems)
        for got, fwd in zip(d["landed"], d["passed"]):
            got.wait_recv()
            fwd.start()

    def finish(self, ins, outs, sems):
        d = self._copies(ins, outs, sems)
        for cp in d["from_sibling"]:
            cp.wait_recv()
        for cp in d["first"] + d["passed"]:
            cp.wait_send()
        for cp in d["mine"]:
            cp.wait()


class _Scatter:
    def __init__(self, tensors):
        self.ins = list(tensors)
        self.n = n = len(tensors)
        self.out_shape = [jax.ShapeDtypeStruct(t.shape, t.dtype) for t in tensors]
        self.scratch = [pltpu.SemaphoreType.DMA((7 * n,)), pltpu.SemaphoreType.DMA((7 * n,))]

    def _copies(self, ins, outs, sems):
        send_sems, recv_sems = sems
        x, y, c, _ = _place()
        me = 4 * x + 2 * y + c
        sends, recvs = [], []
        for t in range(self.n):
            for m in range(1, N_DEV):
                px, py, pc = x ^ (m >> 2), y ^ ((m >> 1) & 1), c ^ (m & 1)
                q = 4 * px + 2 * py + pc
                sems_k = dict(send_sem=send_sems.at[7 * t + m - 1], recv_sem=recv_sems.at[7 * t + m - 1],
                              device_id=(px, py, pc), device_id_type=MESH_ID)
                sends.append(pltpu.make_async_remote_copy(src_ref=ins[t].at[q], dst_ref=outs[t].at[me], **sems_k))
                recvs.append(pltpu.make_async_remote_copy(src_ref=ins[t].at[me], dst_ref=outs[t].at[q], **sems_k))
        return sends, recvs

    def start(self, ins, outs, sems):
        for cp in self._copies(ins, outs, sems)[0]:
            cp.start()

    def middle(self, ins, outs, sems):
        pass

    def finish(self, ins, outs, sems):
        sends, recvs = self._copies(ins, outs, sems)
        for cp in recvs:
            cp.wait_recv()
        for cp in sends:
            cp.wait_send()


class _Comms:
    def __init__(self, *comms):
        self.comms = comms
        self.ins = [a for c in comms for a in c.ins]
        self.out_shape = [s for c in comms for s in c.out_shape]
        self.scratch = [s for c in comms for s in c.scratch]

    def _each(self, phase, ins, outs, sems):
        i = o = s = 0
        for c in self.comms:
            ni, no, ns = len(c.ins), len(c.out_shape), len(c.scratch)
            getattr(c, phase)(ins[i:i + ni], outs[o:o + no], sems[s:s + ns])
            i, o, s = i + ni, o + no, s + ns

    def start(self, ins, outs, sems):
        self._each("start", ins, outs, sems)

    def middle(self, ins, outs, sems):
        self._each("middle", ins, outs, sems)

    def finish(self, ins, outs, sems):
        self._each("finish", ins, outs, sems)


def _run_comm(comm, name):
    n = len(comm.ins)

    def body(*refs):
        parts = refs[:n], refs[n:2 * n], refs[2 * n:]
        comm.start(*parts)
        comm.middle(*parts)
        comm.finish(*parts)

    return pl.pallas_call(body, name=name, out_shape=comm.out_shape, in_specs=[ANY] * n, out_specs=[ANY] * n,
                          scratch_shapes=comm.scratch)(*comm.ins)


HBM_SPEC = pl.BlockSpec(memory_space=pltpu.HBM)
SEM_SPEC = pl.BlockSpec(memory_space=pltpu.SEMAPHORE)
DATAFLOW = pltpu.SideEffectType.DATAFLOW_SIDE_EFFECTING


def _scatter_copies(own_ref, land_ref, send_sems, recv_sems):
    x, y, c, _ = _place()
    me = 4 * x + 2 * y + c
    pairs = []
    for m in range(1, N_DEV):
        px, py, pc = x ^ (m >> 2), y ^ ((m >> 1) & 1), c ^ (m & 1)
        q = 4 * px + 2 * py + pc
        sems = dict(send_sem=send_sems.at[m - 1], recv_sem=recv_sems.at[m - 1], device_id=(px, py, pc),
                    device_id_type=MESH_ID)
        pairs.append((pltpu.make_async_remote_copy(src_ref=own_ref.at[q], dst_ref=land_ref.at[me], **sems),
                      pltpu.make_async_remote_copy(src_ref=own_ref.at[me], dst_ref=land_ref.at[q], **sems)))
    return pairs


def _scatter_start(own, name):
    def body(own_ref, land_ref, send_sems, recv_sems, own_thru, land_thru, token):
        for send, _ in _scatter_copies(own_ref, land_ref, send_sems, recv_sems):
            send.start()
        token[...] = jnp.zeros_like(token)

    buf = pltpu.HBM(own.shape, own.dtype)
    return pl.pallas_call(
        body, name=name,
        out_shape=(pltpu.SemaphoreType.DMA((N_DEV - 1,)), pltpu.SemaphoreType.DMA((N_DEV - 1,)), buf, buf,
                   jax.ShapeDtypeStruct((8, 128), F32)),
        in_specs=(HBM_SPEC, HBM_SPEC),
        out_specs=(SEM_SPEC, SEM_SPEC, HBM_SPEC, HBM_SPEC, pl.BlockSpec(memory_space=pltpu.VMEM)),
        input_output_aliases={0: 2, 1: 3},
        compiler_params=pltpu.CompilerParams(has_side_effects=DATAFLOW),
    )(pltpu.with_memory_space_constraint(own, pltpu.HBM),
      pltpu.with_memory_space_constraint(lax.empty(own.shape, own.dtype), pltpu.HBM))


def _scatter_wait(send_sems, recv_sems, own_thru, land_thru, after, name):
    def body(own_ref, land_ref, send_sems, recv_sems, after_ref, own_out, land_out):
        for send, recv in _scatter_copies(own_ref, land_ref, send_sems, recv_sems):
            send.wait_send()
            recv.wait_recv()

    buf = pltpu.HBM(own_thru.shape, own_thru.dtype)
    return pl.pallas_call(
        body, name=name, out_shape=(buf, buf),
        in_specs=(HBM_SPEC, HBM_SPEC, SEM_SPEC, SEM_SPEC, ANY), out_specs=(HBM_SPEC, HBM_SPEC),
        input_output_aliases={0: 0, 1: 1},
        compiler_params=pltpu.CompilerParams(has_side_effects=DATAFLOW),
    )(own_thru, land_thru, send_sems, recv_sems, after)


def _fused_call(body, comm, args, *, name, grid, out_shape, in_specs, out_specs, scratch_shapes=(), params):
    single = not isinstance(out_shape, (list, tuple))
    out_shape = [out_shape] if single else list(out_shape)
    out_specs = [out_specs] if single else list(out_specs)
    if comm is None:
        res = pl.pallas_call(body, name=name, grid=grid, out_shape=out_shape, in_specs=in_specs, out_specs=out_specs,
                             scratch_shapes=list(scratch_shapes), compiler_params=params)(*args)
        return (res[0] if single else res), []
    n_in, n_out, n_scr = len(in_specs), len(out_shape), len(scratch_shapes)
    c_in, c_out = len(comm.ins), len(comm.out_shape)
    steps = grid[0]

    def fused(*refs):
        pos = 0
        groups = []
        for size in (n_in, c_in, n_out, c_out, n_scr, len(comm.scratch)):
            groups.append(refs[pos:pos + size])
            pos += size
        ins, c_ins, outs, c_outs, scr, c_sems = groups
        i = pl.program_id(0)

        @pl.when(i == 0)
        def _():
            comm.start(c_ins, c_outs, c_sems)

        @pl.when(i == steps // 2)
        def _():
            comm.middle(c_ins, c_outs, c_sems)

        body(*ins, *outs, *scr)

        @pl.when(i == steps - 1)
        def _():
            comm.finish(c_ins, c_outs, c_sems)

    res = pl.pallas_call(
        fused, name=name, grid=grid, out_shape=out_shape + list(comm.out_shape),
        in_specs=list(in_specs) + [ANY] * c_in, out_specs=out_specs + [ANY] * c_out,
        scratch_shapes=list(scratch_shapes) + list(comm.scratch), compiler_params=params)(*args, *comm.ins)
    main = res[:n_out]
    return (main[0] if single else main), list(res[n_out:])


def _norm_matmul(x, gain, wt, name, comm=None, tm=1024):
    t, n = x.shape[0], wt.shape[0]

    def body(x_ref, g_ref, wt_ref, o_ref):
        xv = x_ref[...]
        r = lax.rsqrt(jnp.mean(xv * xv, axis=-1, keepdims=True) + EPS)
        h = (xv * r * g_ref[...]).astype(BF16)
        o_ref[...] = lax.dot_general(h, wt_ref[...], NT, preferred_element_type=F32)

    return _fused_call(
        body, comm, (x, gain, wt), name=name, grid=(t // tm,),
        out_shape=jax.ShapeDtypeStruct((t, n), F32),
        in_specs=[pl.BlockSpec((tm, D_MODEL), lambda i: (i, 0)), _const_spec((1, D_MODEL)), _const_spec((n, D_MODEL))],
        out_specs=pl.BlockSpec((tm, n), lambda i: (i, 0)),
        params=_params("arbitrary"))


def _project_out(a, w_ref, x_ref, p_ref, y_ref):
    y = jnp.dot(a, w_ref[...], preferred_element_type=F32)
    y_ref[...] = y
    ry = lax.rsqrt(jnp.mean(y * y, axis=-1, keepdims=True) + EPS)
    return x_ref[...] + (y * ry) * p_ref[...]


def _post_bwd_rows(g, y, a, n_own, first, last, p_ref, w_ref, dw_ref, dw16_ref, dp_ref):
    @pl.when(first)
    def _():
        dw_ref[...] = jnp.zeros_like(dw_ref)
        dp_ref[...] = jnp.zeros_like(dp_ref)

    ry = lax.rsqrt(jnp.mean(y * y, axis=-1, keepdims=True) + EPS)
    nv = y * ry
    dp_ref[...] += jnp.sum((g * nv)[0:n_own], axis=0, keepdims=True)
    dn = g * p_ref[...]
    dy = (ry * (dn - nv * jnp.mean(dn * nv, axis=-1, keepdims=True))).astype(BF16)
    dw_ref[...] += lax.dot_general(a, dy[0:n_own], TN, preferred_element_type=F32)

    @pl.when(last)
    def _():
        dw16_ref[...] = dw_ref[...].astype(BF16)

    return lax.dot_general(dy, w_ref[...], NT, preferred_element_type=F32)


def _pre_bwd(dproj, wt, x_in, pre, g, name, comm=None, tm=512):
    t, n = dproj.shape
    steps = t // tm

    def body(dp_ref, wt_ref, x_ref, pre_ref, g_ref, dx_ref, dwt16_ref, dpre_ref, dwt_ref):
        @pl.when(pl.program_id(0) == 0)
        def _():
            dwt_ref[...] = jnp.zeros_like(dwt_ref)
            dpre_ref[...] = jnp.zeros_like(dpre_ref)

        dpv = dp_ref[...]
        dh = jnp.dot(dpv, wt_ref[...], preferred_element_type=F32)
        xv = x_ref[...]
        r = lax.rsqrt(jnp.mean(xv * xv, axis=-1, keepdims=True) + EPS)
        xn = xv * r
        pv = pre_ref[...]
        dpre_ref[...] += jnp.sum(dh * xn, axis=0, keepdims=True)
        dxn = dh * pv
        dx_ref[...] = g_ref[...] + r * (dxn - xn * jnp.mean(dxn * xn, axis=-1, keepdims=True))
        h = (xn * pv).astype(BF16)
        dwt_ref[...] += lax.dot_general(dpv, h, TN, preferred_element_type=F32)

        @pl.when(pl.program_id(0) == steps - 1)
        def _():
            dwt16_ref[...] = dwt_ref[...].astype(BF16)

    row = pl.BlockSpec((tm, D_MODEL), lambda i: (i, 0))
    return _fused_call(
        body, comm, (dproj, wt, x_in, pre, g), name=name, grid=(steps,),
        out_shape=[jax.ShapeDtypeStruct((t, D_MODEL), F32), jax.ShapeDtypeStruct((n, D_MODEL), BF16),
                   jax.ShapeDtypeStruct((1, D_MODEL), F32)],
        in_specs=[pl.BlockSpec((tm, n), lambda i: (i, 0)), _const_spec((n, D_MODEL)), row, _const_spec((1, D_MODEL)), row],
        out_specs=[row, _const_spec((n, D_MODEL)), pl.BlockSpec((1, D_MODEL), lambda i: (0, 0))],
        scratch_shapes=[pltpu.VMEM((n, D_MODEL), F32)],
        params=_params("arbitrary"))


def _proj_dw(dproj, x_in, pre, name, comm=None, tm=1024):
    t, n = dproj.shape
    steps = t // tm

    def body(dp_ref, x_ref, pre_ref, dwt16_ref, dwt_ref):
        @pl.when(pl.program_id(0) == 0)
        def _():
            dwt_ref[...] = jnp.zeros_like(dwt_ref)

        xv = x_ref[...]
        r = lax.rsqrt(jnp.mean(xv * xv, axis=-1, keepdims=True) + EPS)
        h = (xv * r * pre_ref[...]).astype(BF16)
        dwt_ref[...] += lax.dot_general(dp_ref[...], h, TN, preferred_element_type=F32)

        @pl.when(pl.program_id(0) == steps - 1)
        def _():
            dwt16_ref[...] = dwt_ref[...].astype(BF16)

    return _fused_call(
        body, comm, (dproj, x_in, pre), name=name, grid=(steps,),
        out_shape=jax.ShapeDtypeStruct((n, D_MODEL), BF16),
        in_specs=[pl.BlockSpec((tm, n), lambda i: (i, 0)), pl.BlockSpec((tm, D_MODEL), lambda i: (i, 0)),
                  _const_spec((1, D_MODEL))],
        out_specs=pl.BlockSpec((n, D_MODEL), lambda i: (0, 0)),
        scratch_shapes=[pltpu.VMEM((n, D_MODEL), F32)],
        params=_params("arbitrary"))


def _proj_dx(dproj, wt, x_in, pre, g, name, comm=None, tm=512):
    t, n = dproj.shape

    def body(dp_ref, wt_ref, x_ref, pre_ref, g_ref, dx_ref, dpre_ref):
        @pl.when(pl.program_id(0) == 0)
        def _():
            dpre_ref[...] = jnp.zeros_like(dpre_ref)

        dh = jnp.dot(dp_ref[...], wt_ref[...], preferred_element_type=F32)
        xv = x_ref[...]
        r = lax.rsqrt(jnp.mean(xv * xv, axis=-1, keepdims=True) + EPS)
        xn = xv * r
        dpre_ref[...] += jnp.sum(dh * xn, axis=0, keepdims=True)
        dxn = dh * pre_ref[...]
        dx_ref[...] = g_ref[...] + r * (dxn - xn * jnp.mean(dxn * xn, axis=-1, keepdims=True))

    row = pl.BlockSpec((tm, D_MODEL), lambda i: (i, 0))
    return _fused_call(
        body, comm, (dproj, wt, x_in, pre, g), name=name, grid=(t // tm,),
        out_shape=[jax.ShapeDtypeStruct((t, D_MODEL), F32), jax.ShapeDtypeStruct((1, D_MODEL), F32)],
        in_specs=[pl.BlockSpec((tm, n), lambda i: (i, 0)), _const_spec((n, D_MODEL)), row, _const_spec((1, D_MODEL)), row],
        out_specs=[row, pl.BlockSpec((1, D_MODEL), lambda i: (0, 0))],
        params=_params("arbitrary"))


def _group_masks():
    lane = lax.broadcasted_iota(jnp.int32, (1, GROUP * HEAD_DIM), 1)
    return [(lane // HEAD_DIM == g).astype(F32) for g in range(GROUP)]


def _stack_groups(v, masks):
    return jnp.concatenate([v * m for m in masks], axis=0)


def _unstack_groups(v, masks):
    out = v[0:BLOCK] * masks[0]
    for g in range(1, GROUP):
        out = out + v[g * BLOCK:(g + 1) * BLOCK] * masks[g]
    return out


def _repeat_head(kv2, kvh):
    first = lax.broadcasted_iota(jnp.int32, kv2.shape, 1) < HEAD_DIM
    rolled = pltpu.roll(kv2, HEAD_DIM, 1)
    one = jnp.where(first, kv2, rolled) if kvh == 0 else jnp.where(first, rolled, kv2)
    return jnp.concatenate([one, one], axis=1)


def _fold_head(v4):
    a = v4[:, 0:128] + v4[:, 128:256]
    return a + pltpu.roll(a, HEAD_DIM, 1)


ATTN_CONSTS = [pltpu.VMEM((KV_HEADS, GROUP * BLOCK, 2 * BLOCK), F32)]


def _fill_attn_bias(bias_ref):
    row = lax.broadcasted_iota(jnp.int32, (GROUP * BLOCK, 2 * BLOCK), 0)
    col = lax.broadcasted_iota(jnp.int32, (GROUP * BLOCK, 2 * BLOCK), 1)
    dist = (row % BLOCK) + BLOCK - col
    band = (dist >= 0) & (dist < BLOCK)
    rb = lax.broadcasted_iota(jnp.int32, (GROUP * BLOCK, 1), 0) // BLOCK
    for kvh in range(KV_HEADS):
        slope = jnp.zeros((GROUP * BLOCK, 1), F32)
        for g in range(GROUP):
            slope = jnp.where(rb == g, 2.0 ** (-(kvh * GROUP + g + 1)), slope)
        bias_ref[kvh] = jnp.where(band, -slope * dist.astype(F32), NEG)


def _row_sinks(kvh, sink_ref):
    rb = lax.broadcasted_iota(jnp.int32, (GROUP * BLOCK, 1), 0) // BLOCK
    sink = jnp.zeros((GROUP * BLOCK, 1), F32)
    for g in range(GROUP):
        sink = jnp.where(rb == g, sink_ref[0, kvh * GROUP + g], sink)
    return sink


def _attn_probs(qk, k4, bias, sink, no_past, masks):
    qs = _stack_groups(qk, masks).astype(BF16)
    s = lax.dot_general(qs, k4, NT, preferred_element_type=F32) * (HEAD_DIM ** -0.5) + bias
    s = jnp.concatenate([jnp.where(no_past, NEG, s[:, 0:BLOCK]), s[:, BLOCK:]], axis=1)
    mx = jnp.maximum(jnp.max(s, axis=-1, keepdims=True), sink)
    e = jnp.exp(s - mx)
    es = jnp.exp(sink - mx)
    inv = 1.0 / (jnp.sum(e, axis=-1, keepdims=True) + es)
    return qs, e * inv, es * inv


def _pool_forward(u_ext, g, t0):
    n = u_ext.shape[0] - POOL_HALO
    s = u_ext
    for step in range(g + 1):
        s = s + pltpu.roll(s, 1 << step, 0)
    w = 2 << g
    t = t0 + lax.broadcasted_iota(jnp.int32, (n, 1), 0)
    cnt = jnp.minimum(t + 1, w).astype(F32)
    return s[POOL_HALO:] / cnt - u_ext[POOL_HALO:]


def _layer0_fwd(proj, sinks, pool_w, pool_scale, w_out, x_in, post, comm=None, tq=512):
    t = proj.shape[0]
    nblk = tq // BLOCK

    def body(main_ref, halo_ref, sink_ref, pw_ref, ps_ref, w_ref, x_ref, p_ref, o_ref, y_ref, xo_ref, kv_ref, bias_ref):
        i = pl.program_id(0)
        t0 = i * tq
        masks = _group_masks()

        @pl.when(i == 0)
        def _():
            _fill_attn_bias(bias_ref)

        kv_ref[0:BLOCK, :] = halo_ref[:, COL_K:COL_K + 256]
        kv_ref[BLOCK:, :] = main_ref[:, COL_K:COL_K + 256]

        def block(jb, carry):
            r0 = pl.multiple_of(jb * BLOCK, BLOCK)
            no_past = t0 + r0 == 0
            q = main_ref[pl.ds(r0, BLOCK), COL_Q:COL_Q + ATTN_WIDTH]
            ga = main_ref[pl.ds(r0, BLOCK), COL_GA:COL_GA + ATTN_WIDTH]
            kk = kv_ref[pl.ds(r0, 2 * BLOCK), 0:128]
            vv = kv_ref[pl.ds(r0, 2 * BLOCK), 128:256]
            outs = []
            for kvh in range(KV_HEADS):
                k4 = _repeat_head(kk, kvh).astype(BF16)
                v4 = _repeat_head(vv, kvh).astype(BF16)
                _, p, _ = _attn_probs(q[:, kvh * 256:(kvh + 1) * 256], k4, bias_ref[kvh], _row_sinks(kvh, sink_ref), no_past, masks)
                pv = jnp.dot(p.astype(BF16), v4, preferred_element_type=F32)
                outs.append(_unstack_groups(pv, masks))
            attn = jnp.concatenate(outs, axis=1)
            o_ref[pl.ds(r0, BLOCK), 0:ATTN_WIDTH] = (attn * _silu(ga)).astype(BF16)
            return carry

        lax.fori_loop(0, nblk, block, 0, unroll=True)

        for g in range(POOL_GROUPS):
            cu = COL_U + g * POOL_GC
            cg = COL_GB + g * POOL_GC
            halo_u = jnp.where(i == 0, 0.0, halo_ref[BLOCK - POOL_HALO:BLOCK, cu:cu + POOL_GC])
            u_ext = jnp.concatenate([halo_u, main_ref[:, cu:cu + POOL_GC]], axis=0)
            pooled = _pool_forward(u_ext, g, t0)
            y = jnp.dot(pooled.astype(BF16), pw_ref[g].astype(BF16), preferred_element_type=F32)
            y = y * ps_ref[:, g * POOL_GC:(g + 1) * POOL_GC]
            o_ref[:, ATTN_WIDTH + g * POOL_GC:ATTN_WIDTH + (g + 1) * POOL_GC] =(y * _silu(main_ref[:, cg:cg + POOL_GC])).astype(BF16)

        xo_ref[...] = _project_out(o_ref[...], w_ref, x_ref, p_ref, y_ref)

    row = pl.BlockSpec((tq, D_MODEL), lambda i: (i, 0))
    return _fused_call(
        body, comm, (proj, proj, sinks, pool_w, pool_scale, w_out, x_in, post), name="layer0_fwd", grid=(t // tq,),
        out_shape=[jax.ShapeDtypeStruct((t, D_MODEL), BF16), jax.ShapeDtypeStruct((t, D_MODEL), F32),
                   jax.ShapeDtypeStruct((t, D_MODEL), F32)],
        in_specs=[pl.BlockSpec((tq, EVEN_IN), lambda i: (i, 0)),
                  pl.BlockSpec((BLOCK, EVEN_IN), lambda i: (jnp.maximum(i * nblk - 1, 0), 0)),
                  pl.BlockSpec(memory_space=pltpu.SMEM),
                  _const_spec((POOL_GROUPS, POOL_GC, POOL_GC)), _const_spec((1, POOL_WIDTH)),
                  _const_spec((D_MODEL, D_MODEL)), row, _const_spec((1, D_MODEL))],
        out_specs=[row, row, row],
        scratch_shapes=[pltpu.VMEM((tq + BLOCK, 256), F32)] + ATTN_CONSTS,
        params=_params("arbitrary"))


def _layer0_bwd(proj, gy, y, mix, w_out, post, sinks, pool_w, pool_scale, comm=None, tq=512):
    t = proj.shape[0]
    nt = t // tq
    nblk = tq // BLOCK

    def body(main_ref, halo_ref, next_ref, gy_ref, gyn_ref, y_ref, yn_ref, mix_ref, wo_ref, po_ref,
             sink_ref, pw_ref, ps_ref,
             o_ref, dsk_ref, dpw_ref, dps_ref, dwo16_ref, dpo_ref,
             kv_ref, dkv_ref, carry_ref, bias_ref, dwo_ref, dmix_ref):
        i = pl.program_id(0)
        ii = nt - 1 - i
        t0 = ii * tq
        masks = _group_masks()

        @pl.when(i == 0)
        def _():
            _fill_attn_bias(bias_ref)
            dsk_ref[...] = jnp.zeros_like(dsk_ref)
            dpw_ref[...] = jnp.zeros_like(dpw_ref)
            dps_ref[...] = jnp.zeros_like(dps_ref)
            carry_ref[...] = jnp.zeros_like(carry_ref)

        dmix_ref[...] = _post_bwd_rows(jnp.concatenate([gy_ref[...], gyn_ref[...]], axis=0),
                                       jnp.concatenate([y_ref[...], yn_ref[...]], axis=0), mix_ref[...], tq,
                                       i == 0, i == nt - 1, po_ref, wo_ref, dwo_ref, dwo16_ref, dpo_ref)
        dm_ref = dmix_ref.at[pl.ds(0, tq)]
        dmn_ref = dmix_ref.at[pl.ds(tq, POOL_HALO)]

        kv_ref[0:BLOCK, :] = halo_ref[:, COL_K:COL_K + 256]
        kv_ref[BLOCK:, :] = main_ref[:, COL_K:COL_K + 256]
        dkv_ref[0:tq, :] = jnp.zeros((tq, 256), F32)
        dkv_ref[tq:, :] = carry_ref[...]

        def block(jb, carry):
            r0 = pl.multiple_of(jb * BLOCK, BLOCK)
            no_past = t0 + r0 == 0
            q = main_ref[pl.ds(r0, BLOCK), COL_Q:COL_Q + ATTN_WIDTH]
            ga = main_ref[pl.ds(r0, BLOCK), COL_GA:COL_GA + ATTN_WIDTH]
            dya = dm_ref[pl.ds(r0, BLOCK), 0:ATTN_WIDTH]
            kk = kv_ref[pl.ds(r0, 2 * BLOCK), 0:128]
            vv = kv_ref[pl.ds(r0, 2 * BLOCK), 128:256]
            silu_ga, dsilu_ga = _silu_and_grad(ga)
            do = dya * silu_ga
            first = lax.broadcasted_iota(jnp.int32, (2 * BLOCK, 128), 1) < HEAD_DIM
            attn, dq, dk, dv = [], [], [], []
            for kvh in range(KV_HEADS):
                k4 = _repeat_head(kk, kvh).astype(BF16)
                v4 = _repeat_head(vv, kvh).astype(BF16)
                qs, p, ps = _attn_probs(q[:, kvh * 256:(kvh + 1) * 256], k4, bias_ref[kvh], _row_sinks(kvh, sink_ref), no_past, masks)
                pb = p.astype(BF16)
                o_k = _unstack_groups(jnp.dot(pb, v4, preferred_element_type=F32), masks)
                do_k = do[:, kvh * 256:(kvh + 1) * 256]
                dos = _stack_groups(do_k, masks).astype(BF16)
                prod = do_k * o_k
                delta = jnp.concatenate([jnp.sum(prod * m, axis=-1, keepdims=True) for m in masks], axis=0)
                dp = lax.dot_general(dos, v4, NT, preferred_element_type=F32)
                ds = (p * (dp - delta)).astype(BF16)
                sink_term = ps * delta
                for g in range(GROUP):
                    h = kvh * GROUP + g
                    dsk_ref[h:h + 1, :] -= jnp.sum(sink_term[g * BLOCK:(g + 1) * BLOCK], keepdims=True)
                scale = HEAD_DIM ** -0.5
                dq.append(_unstack_groups(jnp.dot(ds, k4, preferred_element_type=F32), masks) * scale)
                dk.append(_fold_head(lax.dot_general(ds, qs, TN, preferred_element_type=F32)) * scale)
                dv.append(_fold_head(lax.dot_general(pb, dos, TN, preferred_element_type=F32)))
                attn.append(o_k)
            o_ref[pl.ds(r0, BLOCK), COL_Q:COL_Q + ATTN_WIDTH] = jnp.concatenate(dq, axis=1).astype(BF16)
            o_all = jnp.concatenate(attn, axis=1)
            o_ref[pl.ds(r0, BLOCK), COL_GA:COL_GA + ATTN_WIDTH] = (dya * o_all * dsilu_ga).astype(BF16)
            dkv = jnp.concatenate([jnp.where(first, dk[0], dk[1]), jnp.where(first, dv[0], dv[1])], axis=1)
            dkv_ref[pl.ds(r0, 2 * BLOCK), :] += dkv
            return carry

        lax.fori_loop(0, nblk, block, 0, unroll=True)
        carry_ref[...] = dkv_ref[0:BLOCK, :]
        o_ref[:, COL_K:COL_K + 256] = dkv_ref[BLOCK:, :].astype(BF16)

        last = ii == nt - 1
        for g in range(POOL_GROUPS):
            cu = COL_U + g * POOL_GC
            cg = COL_GB + g * POOL_GC
            cm = ATTN_WIDTH + g * POOL_GC
            pw = pw_ref[g].astype(BF16)
            sc = ps_ref[:, g * POOL_GC:(g + 1) * POOL_GC]
            halo_u = jnp.where(ii == 0, 0.0, halo_ref[BLOCK - POOL_HALO:BLOCK, cu:cu + POOL_GC])
            u_ext = jnp.concatenate([halo_u, main_ref[:, cu:cu + POOL_GC]], axis=0)
            pooled = _pool_forward(u_ext, g, t0).astype(BF16)
            y_raw = jnp.dot(pooled, pw, preferred_element_type=F32)
            gb = main_ref[:, cg:cg + POOL_GC]
            dyb = dm_ref[:, cm:cm + POOL_GC]
            silu_gb, dsilu_gb = _silu_and_grad(gb)
            dypool = dyb * silu_gb
            dps_ref[:, g * POOL_GC:(g + 1) * POOL_GC] += jnp.sum(dypool * y_raw, axis=0, keepdims=True)
            o_ref[:, cg:cg + POOL_GC] = (dyb * (y_raw * sc) * dsilu_gb).astype(BF16)
            dyraw = dypool * sc
            dyraw_n = jnp.where(last, 0.0, dmn_ref[:, cm:cm + POOL_GC] * _silu(next_ref[:, cg:cg + POOL_GC]) * sc)
            dpw_ref[g * POOL_GC:(g + 1) * POOL_GC, :] += lax.dot_general(pooled, dyraw.astype(BF16), TN,
                                                                         preferred_element_type=F32)
            dyraw_ext = jnp.concatenate([dyraw, dyraw_n], axis=0).astype(BF16)
            dpooled = lax.dot_general(dyraw_ext, pw, NT, preferred_element_type=F32)
            w = 2 << g
            tt = t0 + lax.broadcasted_iota(jnp.int32, (tq + POOL_HALO, 1), 0)
            s = dpooled / jnp.minimum(tt + 1, w).astype(F32)
            for step in range(g + 1):
                s = s + pltpu.roll(s, tq + POOL_HALO - (1 << step), 0)
            o_ref[:, cu:cu + POOL_GC] = (s[0:tq] - dpooled[0:tq]).astype(BF16)

    rev = lambda i: nt - 1 - i
    nxt = lambda i: (jnp.minimum((rev(i) + 1) * (tq // POOL_HALO), t // POOL_HALO - 1), 0)
    row = pl.BlockSpec((tq, D_MODEL), lambda i: (rev(i), 0))
    nxt_row = pl.BlockSpec((POOL_HALO, D_MODEL), nxt)
    square = _const_spec((D_MODEL, D_MODEL))
    return _fused_call(
        body, comm, (proj, proj, proj, gy, gy, y, y, mix, w_out, post, sinks, pool_w, pool_scale),
        name="layer0_bwd", grid=(nt,),
        out_shape=[jax.ShapeDtypeStruct((t, EVEN_IN), BF16), jax.ShapeDtypeStruct((8, 128), F32),
                   jax.ShapeDtypeStruct((POOL_GROUPS * POOL_GC, POOL_GC), F32), jax.ShapeDtypeStruct((1, POOL_WIDTH), F32),
                   jax.ShapeDtypeStruct((D_MODEL, D_MODEL), BF16), jax.ShapeDtypeStruct((1, D_MODEL), F32)],
        in_specs=[pl.BlockSpec((tq, EVEN_IN), lambda i: (rev(i), 0)),
                  pl.BlockSpec((BLOCK, EVEN_IN), lambda i: (jnp.maximum(rev(i) * nblk - 1, 0), 0)),
                  pl.BlockSpec((POOL_HALO, EVEN_IN), nxt),
                  row, nxt_row, row, nxt_row, row, square, _const_spec((1, D_MODEL)),
                  pl.BlockSpec(memory_space=pltpu.SMEM),
                  _const_spec((POOL_GROUPS, POOL_GC, POOL_GC)), _const_spec((1, POOL_WIDTH))],
        out_specs=[pl.BlockSpec((tq, EVEN_IN), lambda i: (rev(i), 0)),
                   pl.BlockSpec((8, 128), lambda i: (0, 0)),
                   pl.BlockSpec((POOL_GROUPS * POOL_GC, POOL_GC), lambda i: (0, 0)),
                   pl.BlockSpec((1, POOL_WIDTH), lambda i: (0, 0)), square, pl.BlockSpec((1, D_MODEL), lambda i: (0, 0))],
        scratch_shapes=[pltpu.VMEM((tq + BLOCK, 256), F32), pltpu.VMEM((tq + BLOCK, 256), F32),
                        pltpu.VMEM((BLOCK, 256), F32)] + ATTN_CONSTS
        + [pltpu.VMEM((D_MODEL, D_MODEL), F32), pltpu.VMEM((tq + POOL_HALO, D_MODEL), F32)],
        params=_params("arbitrary"))


CONV_RC = 32
CONV_CC = 128
CONV_CHAINS = 4
CONV_UNROLL = 2


def _fill_shifted(s_ref, rows):
    for b in range(1, 8):
        s_ref[b, 0:rows - 8, :] = s_ref[0, b:b + rows - 8, :]


def _tap_blocks(s_ref, r, cols, lead):
    for b in range(8):
        taps = [(a, 8 * a + b - lead) for a in range(5) if 0 <= 8 * a + b - lead < CONV_K]
        span = 8 * max(a for a, _ in taps) + CONV_RC
        blk = s_ref[b, pl.ds(r, span), cols]
        for a, k in taps:
            yield k, blk[8 * a:8 * a + CONV_RC]


def _conv_taps(s_ref, w_ref, r, cols, lead, reverse):
    accs = [None] * CONV_CHAINS
    for n, (k, blk) in enumerate(_tap_blocks(s_ref, r, cols, lead)):
        kw = CONV_K - 1 - k if reverse else k
        term = blk * w_ref[kw:kw + 1, cols]
        accs[n % CONV_CHAINS] = term if accs[n % CONV_CHAINS] is None else accs[n % CONV_CHAINS] + term
    while len(accs) > 1:
        accs = [a + b for a, b in zip(accs[0::2], accs[1::2])]
    return accs[0]


def _layer_norm_fwd(cf, lng, lnb):
    mu = jnp.mean(cf, axis=-1, keepdims=True)
    xc = cf - mu
    rstd = lax.rsqrt(jnp.mean(xc * xc, axis=-1, keepdims=True) + EPS)
    chat = xc * rstd
    return chat, rstd, chat * lng + lnb


def _layer1_fwd(proj, dw, dwb, lng, lnb, w_out, x_in, post, target, tt=256):
    t = proj.shape[0]
    lead = CONV_HALO - (CONV_K - 1)

    def body(main_ref, halo_ref, w_ref, b_ref, g_ref, lb_ref, wo_ref, x_ref, p_ref, t_ref,
             o_ref, c_ref, y_ref, dl_ref, l_ref, gs_ref):
        i = pl.program_id(0)
        hv = halo_ref[...]
        gs_ref[0, 0:CONV_HALO, :] = jnp.where(i == 0, 0.0, hv[:, GLU_A] * _sigmoid(hv[:, GLU_B]))
        gs_ref[0, CONV_HALO:CONV_HALO + tt, :] = main_ref[:, GLU_A] * _sigmoid(main_ref[:, GLU_B])
        _fill_shifted(gs_ref, tt + CONV_HALO)

        for c in range(D_MODEL // CONV_CC):
            cols = slice(c * CONV_CC, (c + 1) * CONV_CC)

            def chunk(j, carry):
                r = pl.multiple_of(j * CONV_RC, CONV_RC)
                c_ref[pl.ds(r, CONV_RC), cols] = _conv_taps(gs_ref, w_ref, r, cols, lead, False) + b_ref[:, cols]
                return carry
            lax.fori_loop(0, tt // CONV_RC, chunk, 0, unroll=CONV_UNROLL)

        _, _, cn = _layer_norm_fwd(c_ref[...], g_ref[...], lb_ref[...])
        o_ref[...] = (_silu(cn) * _silu(main_ref[:, GATE])).astype(BF16)

        d = _project_out(o_ref[...], wo_ref, x_ref, p_ref, y_ref) - t_ref[...]
        dl_ref[...] = d * (1.0 / D_MODEL)

        @pl.when(i == 0)
        def _():
            l_ref[...] = jnp.zeros_like(l_ref)

        l_ref[...] += 0.5 * jnp.sum(jnp.mean(d * d, axis=-1, keepdims=True))

    vec = _const_spec((1, D_MODEL))
    row = pl.BlockSpec((tt, D_MODEL), lambda i: (i, 0))
    f32_rows = jax.ShapeDtypeStruct((t, D_MODEL), F32)
    return pl.pallas_call(
        body, name="layer1_fwd", grid=(t // tt,),
        out_shape=[jax.ShapeDtypeStruct((t, D_MODEL), BF16), f32_rows, f32_rows, f32_rows,
                   jax.ShapeDtypeStruct((8, 128), F32)],
        in_specs=[pl.BlockSpec((tt, 3 * D_MODEL), lambda i: (i, 0)),
                  pl.BlockSpec((CONV_HALO, 3 * D_MODEL), lambda i: (jnp.maximum(i * (tt // CONV_HALO) - 1, 0), 0)),
                  _const_spec((CONV_K, D_MODEL)), vec, vec, vec,
                  _const_spec((D_MODEL, D_MODEL)), row, vec, row],
        out_specs=[row, row, row, row, pl.BlockSpec((8, 128), lambda i: (0, 0))],
        scratch_shapes=[pltpu.VMEM((8, tt + CONV_HALO, D_MODEL), F32)],
        compiler_params=_params("arbitrary"),
    )(proj, proj, dw, dwb, lng, lnb, w_out, x_in, post, target)


def _layer1_bwd(proj, cf, gy, y, z, w_out, post, dw, lng, lnb, comm=None, tt=256):
    t = proj.shape[0]
    nt = t // tt
    te = tt + CONV_HALO

    def body(main_ref, next_ref, cf_ref, cfn_ref, gy_ref, gyn_ref, y_ref, yn_ref, z_ref, wo_ref, po_ref,
             w_ref, g_ref, lb_ref,
             o_ref, ddw_ref, ddb_ref, dg_ref, dlb_ref, dwo16_ref, dpo_ref, ds_ref, glu_ref, sb_ref, dwo_ref):
        i = pl.program_id(0)

        @pl.when(i == 0)
        def _():
            ddw_ref[...] = jnp.zeros_like(ddw_ref)
            ddb_ref[...] = jnp.zeros_like(ddb_ref)
            dg_ref[...] = jnp.zeros_like(dg_ref)
            dlb_ref[...] = jnp.zeros_like(dlb_ref)

        dzv = _post_bwd_rows(jnp.concatenate([gy_ref[...], gyn_ref[...]], axis=0),
                             jnp.concatenate([y_ref[...], yn_ref[...]], axis=0), z_ref[...], tt, i == 0, i == nt - 1,
                             po_ref, wo_ref, dwo_ref, dwo16_ref, dpo_ref)
        dzv = jnp.concatenate([dzv[0:tt], jnp.where(i < nt - 1, dzv[tt:], 0.0)], axis=0)
        lng = g_ref[...]
        chat, rstd, cn = _layer_norm_fwd(jnp.concatenate([cf_ref[...], cfn_ref[...]], axis=0), lng, lb_ref[...])
        gate = jnp.concatenate([main_ref[:, GATE], next_ref[:, GATE]], axis=0)
        silu_cn, dsilu_cn = _silu_and_grad(cn)
        silu_gate, dsilu_gate = _silu_and_grad(gate)
        o_ref[:, GATE] = (dzv * silu_cn * dsilu_gate)[0:tt].astype(BF16)
        dcn = dzv * silu_gate * dsilu_cn
        dg_ref[...] += jnp.sum((dcn * chat)[0:tt], axis=0, keepdims=True)
        dlb_ref[...] += jnp.sum(dcn[0:tt], axis=0, keepdims=True)
        dchat = dcn * lng
        dcf = rstd * (dchat - jnp.mean(dchat, axis=-1, keepdims=True) - chat * jnp.mean(dchat * chat, axis=-1, keepdims=True))
        ddb_ref[...] += jnp.sum(dcf[0:tt], axis=0, keepdims=True)
        ds_ref[0, 0:te, :] = dcf
        ds_ref[0, te:, :] = jnp.zeros((8, D_MODEL), F32)
        _fill_shifted(ds_ref, te + 8)
        sb_ref[...] = _sigmoid(main_ref[:, GLU_B])
        glu_ref[...] = main_ref[:, GLU_A] * sb_ref[...]

        for c in range(D_MODEL // CONV_CC):
            cols = slice(c * CONV_CC, (c + 1) * CONV_CC)
            gcols = slice(D_MODEL + c * CONV_CC, D_MODEL + (c + 1) * CONV_CC)

            def chunk(j, carry):
                r = pl.multiple_of(j * CONV_RC, CONV_RC)
                dglu = _conv_taps(ds_ref, w_ref, r, cols, 0, True)
                sb = sb_ref[pl.ds(r, CONV_RC), cols]
                o_ref[pl.ds(r, CONV_RC), cols] = (dglu * sb).astype(BF16)
                o_ref[pl.ds(r, CONV_RC), gcols] = (dglu * glu_ref[pl.ds(r, CONV_RC), cols] * (1.0 - sb)).astype(BF16)
                return carry
            lax.fori_loop(0, tt // CONV_RC, chunk, 0, unroll=CONV_UNROLL)

            def taps(j, accs):
                r = pl.multiple_of(j * CONV_RC, CONV_RC)
                gl = glu_ref[pl.ds(r, CONV_RC), cols]
                new = list(accs)
                for m, blk in _tap_blocks(ds_ref, r, cols, 0):
                    prod = blk * gl
                    part = prod[0:8]
                    for q in range(1, CONV_RC // 8):
                        part = part + prod[8 * q:8 * q + 8]
                    new[m] = new[m] + part
                return tuple(new)
            accs = lax.fori_loop(0, tt // CONV_RC, taps, tuple(jnp.zeros((8, CONV_CC), F32) for _ in range(CONV_K)))
            for m in range(CONV_K):
                k = CONV_K - 1 - m
                ddw_ref[k:k + 1, cols] += jnp.sum(accs[m], axis=0, keepdims=True)

    vec = _const_spec((1, D_MODEL))
    vec_out = pl.BlockSpec((1, D_MODEL), lambda i: (0, 0))
    row = pl.BlockSpec((tt, D_MODEL), lambda i: (i, 0))
    nxt = lambda i: (jnp.minimum((i + 1) * (tt // CONV_HALO), t // CONV_HALO - 1), 0)
    nxt_row = pl.BlockSpec((CONV_HALO, D_MODEL), nxt)
    vec_f32 = jax.ShapeDtypeStruct((1, D_MODEL), F32)
    square = _const_spec((D_MODEL, D_MODEL))
    return _fused_call(
        body, comm, (proj, proj, cf, cf, gy, gy, y, y, z, w_out, post, dw, lng, lnb), name="layer1_bwd", grid=(nt,),
        out_shape=[jax.ShapeDtypeStruct((t, 3 * D_MODEL), BF16), jax.ShapeDtypeStruct((CONV_K, D_MODEL), F32),
                   vec_f32, vec_f32, vec_f32, jax.ShapeDtypeStruct((D_MODEL, D_MODEL), BF16), vec_f32],
        in_specs=[pl.BlockSpec((tt, 3 * D_MODEL), lambda i: (i, 0)),
                  pl.BlockSpec((CONV_HALO, 3 * D_MODEL), nxt),
                  row, nxt_row, row, nxt_row, row, nxt_row, row, square, vec,
                  _const_spec((CONV_K, D_MODEL)), vec, vec],
        out_specs=[pl.BlockSpec((tt, 3 * D_MODEL), lambda i: (i, 0)),
                   pl.BlockSpec((CONV_K, D_MODEL), lambda i: (0, 0)), vec_out, vec_out, vec_out, square, vec_out],
        scratch_shapes=[pltpu.VMEM((8, te + 8, D_MODEL), F32), pltpu.VMEM((tt, D_MODEL), F32),
                        pltpu.VMEM((tt, D_MODEL), F32), pltpu.VMEM((D_MODEL, D_MODEL), F32)],
        params=_params("arbitrary"))


def _piece_sum(parts, place, name):
    r = parts[0][0].shape[1]

    def body(p_ref, *refs):
        o_ref = refs[-1]
        acc = refs[0][0].astype(F32)
        for part in refs[1:-1]:
            acc = acc + part[0].astype(F32)
        o_ref[0] = acc

    blk = (1, r, D_MODEL)
    spec = lambda slot: pl.BlockSpec(blk, lambda j, p_ref: (slot(p_ref), 0, 0))
    return pl.pallas_call(
        body, name=name,
        grid_spec=pltpu.PrefetchScalarGridSpec(
            num_scalar_prefetch=1, grid=(1,),
            in_specs=[spec(slot) for _, slot in parts],
            out_specs=pl.BlockSpec(blk, lambda j, p_ref: (p_ref[1], 0, 0))),
        out_shape=jax.ShapeDtypeStruct((2, r, D_MODEL), F32),
        compiler_params=_params("arbitrary"),
    )(place, *[a for a, _ in parts])


def _direct_parts(own, recv):
    peer = lambda m: (lambda p: p[0] ^ m)
    return [(own, peer(0))] + [(recv, peer(m)) for m in range(1, N_DEV)]


def _share_with_sibling(halves, name):
    n = len(halves)

    def body(*refs):
        outs = refs[n:2 * n]
        send_sems, recv_sems = refs[2 * n:]
        x, y, c, _ = _place()
        send = [pltpu.make_async_remote_copy(
            src_ref=outs[t].at[c], dst_ref=outs[t].at[c], send_sem=send_sems.at[t], recv_sem=recv_sems.at[t],
            device_id=(x, y, 1 - c), device_id_type=MESH_ID) for t in range(n)]
        recv = [pltpu.make_async_remote_copy(
            src_ref=outs[t].at[c], dst_ref=outs[t].at[1 - c], send_sem=send_sems.at[t], recv_sem=recv_sems.at[t],
            device_id=(x, y, 1 - c), device_id_type=MESH_ID) for t in range(n)]
        for cp in send:
            cp.start()
        for cp in recv:
            cp.wait_recv()
        for cp in send:
            cp.wait_send()

    return pl.pallas_call(
        body, name=name,
        out_shape=[jax.ShapeDtypeStruct(h.shape, h.dtype) for h in halves],
        in_specs=[ANY] * n, out_specs=[ANY] * n,
        input_output_aliases={t: t for t in range(n)},
        scratch_shapes=[pltpu.SemaphoreType.DMA((n,)), pltpu.SemaphoreType.DMA((n,))],
    )(*halves)


def _sum8(parts, name):
    r = parts.shape[1]

    def body(p_ref, o_ref):
        acc = p_ref[0]
        for k in range(1, N_DEV):
            acc = acc + p_ref[k]
        o_ref[...] = acc

    return pl.pallas_call(
        body, name=name, out_shape=jax.ShapeDtypeStruct((r, 128), F32),
        in_specs=[pl.BlockSpec(memory_space=pltpu.VMEM)], out_specs=pl.BlockSpec(memory_space=pltpu.VMEM),
    )(parts)


def _adamw(w, g, m, v, name):
    shape = w.shape
    cols = shape[-1]
    rows = w.size // cols
    rt = 256 if rows % 256 == 0 else rows

    def body(w_ref, g_ref, m_ref, v_ref, d_ref, nm_ref, nv_ref):
        gv = g_ref[...]
        mn = ADAM_B1 * m_ref[...] + (1.0 - ADAM_B1) * gv
        vn = ADAM_B2 * v_ref[...] + (1.0 - ADAM_B2) * (gv * gv)
        m_hat = mn / (1.0 - ADAM_B1 ** ADAM_STEP)
        v_hat = vn / (1.0 - ADAM_B2 ** ADAM_STEP)
        d_ref[...] = -ADAM_LR * (m_hat / (jnp.sqrt(v_hat) + ADAM_EPS) + ADAM_WD * w_ref[...])
        nm_ref[...] = mn
        nv_ref[...] = vn

    spec = pl.BlockSpec((rt, cols), lambda i: (i, 0))
    outs = pl.pallas_call(
        body, name=name, grid=(rows // rt,),
        out_shape=[jax.ShapeDtypeStruct((rows, cols), F32)] * 3,
        in_specs=[spec] * 4, out_specs=[spec] * 3,
        compiler_params=_params("parallel"),
    )(*[a.reshape(rows, cols) for a in (w, g, m, v)])
    return [o.reshape(shape) for o in outs]


SMALL_ROWS = 832


def _pack_small(g):
    parts = [g["loss"], g["pre1"].reshape(8, 128), g["post0"].reshape(8, 128),
             g["post1"].reshape(8, 128), g["sinks"], jnp.pad(g["pool_scale"].reshape(4, 128), ((0, 4), (0, 0))),
             g["pool_w"], g["dw"].reshape(248, 128), g["dwb"].reshape(8, 128), g["lng"].reshape(8, 128),
             g["lnb"].reshape(8, 128)]
    assert sum(p.shape[0] for p in parts) == SMALL_ROWS
    return jnp.concatenate(parts, axis=0)


def _unpack_small(s):
    out, r = {}, 0
    for key, rows, shape in (("loss", 8, (8, 128)), ("pre1", 8, (1, D_MODEL)), ("post", 16, (2, D_MODEL)),
                             ("sinks", 8, (8, 128)),
                             ("pool_scale", 4, (1, POOL_WIDTH)), ("pad", 4, (4, 128)), ("pool_w", 512, (1, 4, 128, 128)),
                             ("dw", 248, (CONV_K, D_MODEL)), ("dwb", 8, (1, D_MODEL)), ("lng", 8, (1, D_MODEL)),
                             ("lnb", 8, (1, D_MODEL))):
        out[key] = s[r:r + rows].reshape(shape)
        r += rows
    return out


def kernel(x, pre_norm, post_norm, a_w_in, a_sinks, b_pool_w, b_pool_scale, ab_w_out, c_w_in, c_dw_w, c_dw_b, c_ln_g, c_ln_b, c_w_out, loss_target, m_pre_norm, m_post_norm, m_a_w_in, m_a_sinks, m_b_pool_w, m_b_pool_scale, m_ab_w_out, m_c_w_in, m_c_dw_w, m_c_dw_b, m_c_ln_g, m_c_ln_b, m_c_w_out, v_pre_norm, v_post_norm, v_a_w_in, v_a_sinks, v_b_pool_w, v_b_pool_scale, v_ab_w_out, v_c_w_in, v_c_dw_w, v_c_dw_b, v_c_ln_g, v_c_ln_b, v_c_w_out):
    ix, iy = lax.axis_index("x"), lax.axis_index("y")
    chip_cols = (2 * ix + iy) * 256

    pad8 = lambda v: jnp.pad(v, ((0, -v.shape[0] % 8), (0, 0)))
    vec_shard = jnp.concatenate([pad8(c_dw_w.reshape(CONV_K, 256)), pad8(c_dw_b), pad8(c_ln_g), pad8(c_ln_b),
                                 jnp.zeros((8, 256), F32)], axis=0)
    x0, target = x[0], loss_target[0]
    pre0, pre1 = pre_norm[0:1], pre_norm[1:2]
    post0, post1 = post_norm[0:1], post_norm[1:2]
    pool_w = b_pool_w[0]

    (wa_t,) = _run_comm(_Gather([a_w_in[0].T.astype(BF16)], halve=True), "gather_a_w_in")
    wa_t = wa_t.reshape(EVEN_IN, D_MODEL)
    proj0, (w_ab,) = _norm_matmul(x0, pre0, wa_t, "proj0_fwd", comm=_Gather([ab_w_out[0].astype(BF16)], halve=True))
    w_ab = w_ab.reshape(D_MODEL, D_MODEL)
    (mix0, y0, x1), (wc_t, w_c, vecs) = _layer0_fwd(
        proj0, a_sinks, pool_w, b_pool_scale, w_ab, x0, post0,
        comm=_Gather([c_w_in[0].T.astype(BF16), c_w_out[0].astype(BF16), vec_shard], halve=True))
    wc_t = wc_t.reshape(3 * D_MODEL, D_MODEL)
    w_c = w_c.reshape(D_MODEL, D_MODEL)
    vecs = vecs.reshape(4, 64, 256).transpose(1, 0, 2).reshape(64, D_MODEL)
    dw, dwb, lng, lnb = vecs[0:CONV_K], vecs[32:33], vecs[40:41], vecs[48:49]
    proj1, _ = _norm_matmul(x1, pre1, wc_t, "proj1_fwd")
    z1, cf1, y1, g2, loss = _layer1_fwd(proj1, dw, dwb, lng, lnb, w_c, x1, post1, target)

    pieces = lambda m: m.reshape(N_DEV, m.shape[0] // N_DEV, D_MODEL)
    (dproj1, d_dw, d_dwb, d_lng, d_lnb, d_wc, d_post1), _ = _layer1_bwd(proj1, cf1, g2, y1, z1, w_c, post1, dw, lng, lnb)
    (g1, d_wct, d_pre1), (r_wc,) = _pre_bwd(dproj1, wc_t, x1, pre1, g2, "proj1_bwd", comm=_Scatter([pieces(d_wc)]))
    (dproj0, d_sinks, d_pw, d_ps, d_wab, d_post0), (r_wct,) = _layer0_bwd(
        proj0, g1, y0, mix0, w_ab, post0, a_sinks, pool_w, b_pool_scale, comm=_Scatter([pieces(d_wct)]))
    g = dict(loss=loss, pre1=d_pre1, post0=d_post0, post1=d_post1, sinks=d_sinks, pool_w=d_pw, pool_scale=d_ps,
             dw=d_dw, dwb=d_dwb, lng=d_lng, lnb=d_lnb)
    d_wat, (small8, r_wab) = _proj_dw(dproj0, x0, pre0, "proj0_dw",
                                      comm=_Comms(_Gather([_pack_small(g)], halve=False), _Scatter([pieces(d_wab)])))
    sent = _scatter_start(pieces(d_wat), "scatter_a_start")
    (gx, d_pre0), _ = _proj_dx(dproj0, wa_t, x0, pre0 + sent[4][0:1, 0:1], g1, "proj0_dx")
    sent_pre0 = _scatter_start(jnp.broadcast_to(d_pre0.reshape(1, 8, 128), (N_DEV, 8, 128)), "pre0_start")
    own_wat, r_wat = _scatter_wait(*sent[:4], d_pre0, "scatter_a_wait")

    ic = lax.axis_index("c")
    me = 4 * ix + 2 * iy + ic
    place = jnp.stack([me, ic]).astype(jnp.int32)
    parts = [_direct_parts(own_wat, r_wat), _direct_parts(pieces(d_wab), r_wab), _direct_parts(pieces(d_wct), r_wct),
             _direct_parts(pieces(d_wc), r_wc)]
    halves = [_piece_sum(p, place, f"grad_sum{t}") for t, p in enumerate(parts)]
    g_wa_t, g_wab, g_wc_t, g_wc = [h.reshape(2 * h.shape[1], D_MODEL) for h in _share_with_sibling(halves, "grad_share")]
    _, pre0_8 = _scatter_wait(*sent_pre0[:4], g_wc, "pre0_wait")
    pre0_8 = lax.dynamic_update_slice(pre0_8, d_pre0.reshape(1, 8, 128), (me, 0, 0))
    g_c_w_in = g_wc_t.T[None]
    g_ab_w_out = g_wab[None]
    g_c_w_out = g_wc[None]

    s = _unpack_small(_sum8(small8, "small_sum"))
    layer = lax.broadcasted_iota(jnp.int32, (2, D_MODEL), 0)
    g_pre = jnp.where(layer == 0, _sum8(pre0_8, "pre0_sum").reshape(1, D_MODEL), s["pre1"])
    g_post = s["post"]
    g_sinks = s["sinks"][:, 0].reshape(1, 8)
    g_pool_w, g_pool_scale = s["pool_w"], s["pool_scale"]
    g_dw = lax.dynamic_slice_in_dim(s["dw"], chip_cols, 256, axis=1).reshape(1, CONV_K, 1, 256)
    g_dwb = lax.dynamic_slice_in_dim(s["dwb"], chip_cols, 256, axis=1)
    g_lng = lax.dynamic_slice_in_dim(s["lng"], chip_cols, 256, axis=1)
    g_lnb = lax.dynamic_slice_in_dim(s["lnb"], chip_cols, 256, axis=1)

    turn = lambda a: jnp.swapaxes(a, 1, 2)
    a_w_in, m_a_w_in, v_a_w_in = turn(a_w_in), turn(m_a_w_in), turn(v_a_w_in)
    grads = [g_pre, g_post, g_wa_t[None], g_sinks, g_pool_w, g_pool_scale, g_ab_w_out, g_c_w_in, g_dw, g_dwb, g_lng,
             g_lnb, g_c_w_out]
    weights = [pre_norm, post_norm, a_w_in, a_sinks, b_pool_w, b_pool_scale, ab_w_out, c_w_in, c_dw_w, c_dw_b, c_ln_g,
               c_ln_b, c_w_out]
    moms = [m_pre_norm, m_post_norm, m_a_w_in, m_a_sinks, m_b_pool_w, m_b_pool_scale, m_ab_w_out, m_c_w_in, m_c_dw_w,
            m_c_dw_b, m_c_ln_g, m_c_ln_b, m_c_w_out]
    vars_ = [v_pre_norm, v_post_norm, v_a_w_in, v_a_sinks, v_b_pool_w, v_b_pool_scale, v_ab_w_out, v_c_w_in, v_c_dw_w,
             v_c_dw_b, v_c_ln_g, v_c_ln_b, v_c_w_out]
    deltas, new_m, new_v = [], [], []
    for k, (w, gr, m, v) in enumerate(zip(weights, grads, moms, vars_)):
        d, nm, nv = _adamw(w, gr, m, v, f"adamw{k}")
        deltas.append(d)
        new_m.append(nm)
        new_v.append(nv)
    for outs in (grads, deltas, new_m, new_v):
        outs[2] = turn(outs[2])
    return (s["loss"][0, 0], gx[None], *grads, *deltas, *new_m, *new_v)
```

```python
import jax
import jax.numpy as jnp
from jax import lax
from jax.experimental import pallas as pl
from jax.experimental.pallas import tpu as pltpu

F32 = jnp.float32
BF16 = jnp.bfloat16

D_MODEL = 1024
EPS = 1e-6
NEG = -1e30
HEAD_DIM = 64
GROUP = 4
KV_HEADS = 2
BLOCK = 128
EVEN_IN = 2304
ATTN_WIDTH = 512
POOL_WIDTH = 512
COL_Q, COL_K, COL_GA, COL_U, COL_GB = 0, 512, 768, 1280, 1792
POOL_GROUPS = 4
POOL_GC = 128
POOL_HALO = 16
CONV_K = 31
CONV_HALO = 32
GLU_A = slice(0, D_MODEL)
GLU_B = slice(D_MODEL, 2 * D_MODEL)
GATE = slice(2 * D_MODEL, 3 * D_MODEL)
N_DEV = 8

ADAM_LR = 0.001
ADAM_B1 = 0.9
ADAM_B2 = 0.999
ADAM_EPS = 1e-08
ADAM_WD = 0.01
ADAM_STEP = 10

VMEM_LIMIT_BYTES = 56 * 1024 * 1024

NT = (((1,), (1,)), ((), ()))
TN = (((0,), (0,)), ((), ()))
MESH_ID = pl.DeviceIdType.MESH


def _params(*sem):
    return pltpu.CompilerParams(dimension_semantics=sem, vmem_limit_bytes=VMEM_LIMIT_BYTES)


def _const_spec(shape):
    nd = len(shape)
    return pl.BlockSpec(shape, lambda *_: (0,) * nd, pipeline_mode=pl.Buffered(1))


def _sigmoid(v):
    return 0.5 * jnp.tanh(0.5 * v) + 0.5


def _silu(v):
    return v * _sigmoid(v)


def _silu_and_grad(v):
    s = _sigmoid(v)
    return v * s, s * (1.0 + v * (1.0 - s))


ANY = pl.BlockSpec(memory_space=pl.ANY)


def _place():
    x, y, c = lax.axis_index("x"), lax.axis_index("y"), lax.axis_index("c")
    chips = [(1 - x, y), (x, 1 - y), (1 - x, 1 - y)]
    return x, y, c, chips


class _Gather:
    def __init__(self, blocks, halve):
        self.ins = list(blocks)
        self.halve = halve
        self.n = n = len(blocks)
        self.shapes = [((b.shape[0] // 2) if halve else b.shape[0], b.shape[1]) for b in blocks]
        self.out_shape = [jax.ShapeDtypeStruct((N_DEV, r, cols), b.dtype) for (r, cols), b in zip(self.shapes, blocks)]
        self.scratch = [pltpu.SemaphoreType.DMA((7 * n,)), pltpu.SemaphoreType.DMA((7 * n,)),
                        pltpu.SemaphoreType.DMA((n,))]

    def _copies(self, ins, outs, sems):
        send_sems, recv_sems, local_sems = sems
        x, y, c, chips = _place()
        me, sibling = (x, y, c), (x, y, 1 - c)

        def piece(t, px, py, pc):
            return outs[t].at[4 * px + 2 * py + pc]

        def own(t):
            return ins[t].at[pl.ds(c * self.shapes[t][0], self.shapes[t][0])] if self.halve else ins[t]

        def copy(t, k, block, to, src=None):
            return pltpu.make_async_remote_copy(
                src_ref=piece(t, *block) if src is None else src, dst_ref=piece(t, *block),
                send_sem=send_sems.at[7 * t + k], recv_sem=recv_sems.at[7 * t + k],
                device_id=to, device_id_type=MESH_ID)

        rng = range(self.n)
        return dict(
            mine=[pltpu.make_async_copy(own(t), piece(t, *me), local_sems.at[t]) for t in rng],
            first=[copy(t, 0, me, sibling, src=own(t)) for t in rng]
            + [copy(t, 1 + j, me, (*chip, c), src=own(t)) for t in rng for j, chip in enumerate(chips)],
            landed=[copy(t, 1 + j, (*chip, c), me) for j, chip in enumerate(chips) for t in rng],
            passed=[copy(t, 4 + j, (*chip, c), sibling) for j, chip in enumerate(chips) for t in rng],
            from_sibling=[copy(t, 0, sibling, me) for t in rng]
            + [copy(t, 4 + j, (*chip, 1 - c), me) for t in rng for j, chip in enumerate(chips)])

    def start(self, ins, outs, sems):
        d = self._copies(ins, outs, sems)
        for cp in d["mine"] + d["first"]:
            cp.start()

    def middle(self, ins, outs, sems):
        d = self._copies(ins, outs, sems)
        for got, fwd in zip(d["landed"], d["passed"]):
            got.wait_recv()
            fwd.start()

    def finish(self, ins, outs, sems):
        d = self._copies(ins, outs, sems)
        for cp in d["from_sibling"]:
            cp.wait_recv()
        for cp in d["first"] + d["passed"]:
            cp.wait_send()
        for cp in d["mine"]:
            cp.wait()


class _Scatter:
    def __init__(self, tensors):
        self.ins = list(tensors)
        self.n = n = len(tensors)
        self.out_shape = [jax.ShapeDtypeStruct(t.shape, t.dtype) for t in tensors]
        self.scratch = [pltpu.SemaphoreType.DMA((7 * n,)), pltpu.SemaphoreType.DMA((7 * n,))]

    def _copies(self, ins, outs, sems):
        send_sems, recv_sems = sems
        x, y, c, _ = _place()
        me = 4 * x + 2 * y + c
        sends, recvs = [], []
        for t in range(self.n):
            for m in range(1, N_DEV):
                px, py, pc = x ^ (m >> 2), y ^ ((m >> 1) & 1), c ^ (m & 1)
                q = 4 * px + 2 * py + pc
                sems_k = dict(send_sem=send_sems.at[7 * t + m - 1], recv_sem=recv_sems.at[7 * t + m - 1],
                              device_id=(px, py, pc), device_id_type=MESH_ID)
                sends.append(pltpu.make_async_remote_copy(src_ref=ins[t].at[q], dst_ref=outs[t].at[me], **sems_k))
                recvs.append(pltpu.make_async_remote_copy(src_ref=ins[t].at[me], dst_ref=outs[t].at[q], **sems_k))
        return sends, recvs

    def start(self, ins, outs, sems):
        for cp in self._copies(ins, outs, sems)[0]:
            cp.start()

    def middle(self, ins, outs, sems):
        pass

    def finish(self, ins, outs, sems):
        sends, recvs = self._copies(ins, outs, sems)
        for cp in recvs:
            cp.wait_recv()
        for cp in sends:
            cp.wait_send()


class _Comms:
    def __init__(self, *comms):
        self.comms = comms
        self.ins = [a for c in comms for a in c.ins]
        self.out_shape = [s for c in comms for s in c.out_shape]
        self.scratch = [s for c in comms for s in c.scratch]

    def _each(self, phase, ins, outs, sems):
        i = o = s = 0
        for c in self.comms:
            ni, no, ns = len(c.ins), len(c.out_shape), len(c.scratch)
            getattr(c, phase)(ins[i:i + ni], outs[o:o + no], sems[s:s + ns])
            i, o, s = i + ni, o + no, s + ns

    def start(self, ins, outs, sems):
        self._each("start", ins, outs, sems)

    def middle(self, ins, outs, sems):
        self._each("middle", ins, outs, sems)

    def finish(self, ins, outs, sems):
        self._each("finish", ins, outs, sems)


def _run_comm(comm, name):
    n = len(comm.ins)

    def body(*refs):
        parts = refs[:n], refs[n:2 * n], refs[2 * n:]
        comm.start(*parts)
        comm.middle(*parts)
        comm.finish(*parts)

    return pl.pallas_call(body, name=name, out_shape=comm.out_shape, in_specs=[ANY] * n, out_specs=[ANY] * n,
                          scratch_shapes=comm.scratch)(*comm.ins)


HBM_SPEC = pl.BlockSpec(memory_space=pltpu.HBM)
SEM_SPEC = pl.BlockSpec(memory_space=pltpu.SEMAPHORE)
DATAFLOW = pltpu.SideEffectType.DATAFLOW_SIDE_EFFECTING


def _scatter_copies(own_ref, land_ref, send_sems, recv_sems):
    x, y, c, _ = _place()
    me = 4 * x + 2 * y + c
    pairs = []
    for m in range(1, N_DEV):
        px, py, pc = x ^ (m >> 2), y ^ ((m >> 1) & 1), c ^ (m & 1)
        q = 4 * px + 2 * py + pc
        sems = dict(send_sem=send_sems.at[m - 1], recv_sem=recv_sems.at[m - 1], device_id=(px, py, pc),
                    device_id_type=MESH_ID)
        pairs.append((pltpu.make_async_remote_copy(src_ref=own_ref.at[q], dst_ref=land_ref.at[me], **sems),
                      pltpu.make_async_remote_copy(src_ref=own_ref.at[me], dst_ref=land_ref.at[q], **sems)))
    return pairs


def _scatter_start(own, name):
    def body(own_ref, land_ref, send_sems, recv_sems, own_thru, land_thru, token):
        for send, _ in _scatter_copies(own_ref, land_ref, send_sems, recv_sems):
            send.start()
        token[...] = jnp.zeros_like(token)

    buf = pltpu.HBM(own.shape, own.dtype)
    return pl.pallas_call(
        body, name=name,
        out_shape=(pltpu.SemaphoreType.DMA((N_DEV - 1,)), pltpu.SemaphoreType.DMA((N_DEV - 1,)), buf, buf,
                   jax.ShapeDtypeStruct((8, 128), F32)),
        in_specs=(HBM_SPEC, HBM_SPEC),
        out_specs=(SEM_SPEC, SEM_SPEC, HBM_SPEC, HBM_SPEC, pl.BlockSpec(memory_space=pltpu.VMEM)),
        input_output_aliases={0: 2, 1: 3},
        compiler_params=pltpu.CompilerParams(has_side_effects=DATAFLOW),
    )(pltpu.with_memory_space_constraint(own, pltpu.HBM),
      pltpu.with_memory_space_constraint(lax.empty(own.shape, own.dtype), pltpu.HBM))


def _scatter_wait(send_sems, recv_sems, own_thru, land_thru, after, name):
    def body(own_ref, land_ref, send_sems, recv_sems, after_ref, own_out, land_out):
        for send, recv in _scatter_copies(own_ref, land_ref, send_sems, recv_sems):
            send.wait_send()
            recv.wait_recv()

    buf = pltpu.HBM(own_thru.shape, own_thru.dtype)
    return pl.pallas_call(
        body, name=name, out_shape=(buf, buf),
        in_specs=(HBM_SPEC, HBM_SPEC, SEM_SPEC, SEM_SPEC, ANY), out_specs=(HBM_SPEC, HBM_SPEC),
        input_output_aliases={0: 0, 1: 1},
        compiler_params=pltpu.CompilerParams(has_side_effects=DATAFLOW),
    )(own_thru, land_thru, send_sems, recv_sems, after)


def _fused_call(body, comm, args, *, name, grid, out_shape, in_specs, out_specs, scratch_shapes=(), params):
    single = not isinstance(out_shape, (list, tuple))
    out_shape = [out_shape] if single else list(out_shape)
    out_specs = [out_specs] if single else list(out_specs)
    if comm is None:
        res = pl.pallas_call(body, name=name, grid=grid, out_shape=out_shape, in_specs=in_specs, out_specs=out_specs,
                             scratch_shapes=list(scratch_shapes), compiler_params=params)(*args)
        return (res[0] if single else res), []
    n_in, n_out, n_scr = len(in_specs), len(out_shape), len(scratch_shapes)
    c_in, c_out = len(comm.ins), len(comm.out_shape)
    steps = grid[0]

    def fused(*refs):
        pos = 0
        groups = []
        for size in (n_in, c_in, n_out, c_out, n_scr, len(comm.scratch)):
            groups.append(refs[pos:pos + size])
            pos += size
        ins, c_ins, outs, c_outs, scr, c_sems = groups
        i = pl.program_id(0)

        @pl.when(i == 0)
        def _():
            comm.start(c_ins, c_outs, c_sems)

        @pl.when(i == steps // 2)
        def _():
            comm.middle(c_ins, c_outs, c_sems)

        body(*ins, *outs, *scr)

        @pl.when(i == steps - 1)
        def _():
            comm.finish(c_ins, c_outs, c_sems)

    res = pl.pallas_call(
        fused, name=name, grid=grid, out_shape=out_shape + list(comm.out_shape),
        in_specs=list(in_specs) + [ANY] * c_in, out_specs=out_specs + [ANY] * c_out,
        scratch_shapes=list(scratch_shapes) + list(comm.scratch), compiler_params=params)(*args, *comm.ins)
    main = res[:n_out]
    return (main[0] if single else main), list(res[n_out:])


def _norm_matmul(x, gain, wt, name, comm=None, tm=1024):
    t, n = x.shape[0], wt.shape[0]

    def body(x_ref, g_ref, wt_ref, o_ref):
        xv = x_ref[...]
        r = lax.rsqrt(jnp.mean(xv * xv, axis=-1, keepdims=True) + EPS)
        h = (xv * r * g_ref[...]).astype(BF16)
        o_ref[...] = lax.dot_general(h, wt_ref[...], NT, preferred_element_type=F32)

    return _fused_call(
        body, comm, (x, gain, wt), name=name, grid=(t // tm,),
        out_shape=jax.ShapeDtypeStruct((t, n), F32),
        in_specs=[pl.BlockSpec((tm, D_MODEL), lambda i: (i, 0)), _const_spec((1, D_MODEL)), _const_spec((n, D_MODEL))],
        out_specs=pl.BlockSpec((tm, n), lambda i: (i, 0)),
        params=_params("arbitrary"))


def _project_out(a, w_ref, x_ref, p_ref, y_ref):
    y = jnp.dot(a, w_ref[...], preferred_element_type=F32)
    y_ref[...] = y
    ry = lax.rsqrt(jnp.mean(y * y, axis=-1, keepdims=True) + EPS)
    return x_ref[...] + (y * ry) * p_ref[...]


def _post_bwd_rows(g, y, a, n_own, first, last, p_ref, w_ref, dw_ref, dw16_ref, dp_ref):
    @pl.when(first)
    def _():
        dw_ref[...] = jnp.zeros_like(dw_ref)
        dp_ref[...] = jnp.zeros_like(dp_ref)

    ry = lax.rsqrt(jnp.mean(y * y, axis=-1, keepdims=True) + EPS)
    nv = y * ry
    dp_ref[...] += jnp.sum((g * nv)[0:n_own], axis=0, keepdims=True)
    dn = g * p_ref[...]
    dy = (ry * (dn - nv * jnp.mean(dn * nv, axis=-1, keepdims=True))).astype(BF16)
    dw_ref[...] += lax.dot_general(a, dy[0:n_own], TN, preferred_element_type=F32)

    @pl.when(last)
    def _():
        dw16_ref[...] = dw_ref[...].astype(BF16)

    return lax.dot_general(dy, w_ref[...], NT, preferred_element_type=F32)


def _post_bwd(g, y, post, w, a, name, tm=1024):
    t = g.shape[0]
    steps = t // tm

    def body(g_ref, y_ref, a_ref, w_ref, p_ref, da_ref, dw16_ref, dp_ref, dw_ref):
        i = pl.program_id(0)
        da_ref[...] = _post_bwd_rows(g_ref[...], y_ref[...], a_ref[...], tm, i == 0, i == steps - 1,
                                     p_ref, w_ref, dw_ref, dw16_ref, dp_ref)

    row = pl.BlockSpec((tm, D_MODEL), lambda i: (i, 0))
    square = _const_spec((D_MODEL, D_MODEL))
    return pl.pallas_call(
        body, name=name, grid=(steps,),
        out_shape=[jax.ShapeDtypeStruct((t, D_MODEL), F32), jax.ShapeDtypeStruct((D_MODEL, D_MODEL), BF16),
                   jax.ShapeDtypeStruct((1, D_MODEL), F32)],
        in_specs=[row, row, row, square, _const_spec((1, D_MODEL))],
        out_specs=[row, square, pl.BlockSpec((1, D_MODEL), lambda i: (0, 0))],
        scratch_shapes=[pltpu.VMEM((D_MODEL, D_MODEL), F32)],
        compiler_params=_params("arbitrary"),
    )(g, y, a, w, post)


def _pre_bwd(dproj, wt, x_in, pre, g, name, comm=None, tm=512):
    t, n = dproj.shape
    steps = t // tm

    def body(dp_ref, wt_ref, x_ref, pre_ref, g_ref, dx_ref, dwt16_ref, dpre_ref, dwt_ref):
        @pl.when(pl.program_id(0) == 0)
        def _():
            dwt_ref[...] = jnp.zeros_like(dwt_ref)
            dpre_ref[...] = jnp.zeros_like(dpre_ref)

        dpv = dp_ref[...]
        dh = jnp.dot(dpv, wt_ref[...], preferred_element_type=F32)
        xv = x_ref[...]
        r = lax.rsqrt(jnp.mean(xv * xv, axis=-1, keepdims=True) + EPS)
        xn = xv * r
        pv = pre_ref[...]
        dpre_ref[...] += jnp.sum(dh * xn, axis=0, keepdims=True)
        dxn = dh * pv
        dx_ref[...] = g_ref[...] + r * (dxn - xn * jnp.mean(dxn * xn, axis=-1, keepdims=True))
        h = (xn * pv).astype(BF16)
        dwt_ref[...] += lax.dot_general(dpv, h, TN, preferred_element_type=F32)

        @pl.when(pl.program_id(0) == steps - 1)
        def _():
            dwt16_ref[...] = dwt_ref[...].astype(BF16)

    row = pl.BlockSpec((tm, D_MODEL), lambda i: (i, 0))
    return _fused_call(
        body, comm, (dproj, wt, x_in, pre, g), name=name, grid=(steps,),
        out_shape=[jax.ShapeDtypeStruct((t, D_MODEL), F32), jax.ShapeDtypeStruct((n, D_MODEL), BF16),
                   jax.ShapeDtypeStruct((1, D_MODEL), F32)],
        in_specs=[pl.BlockSpec((tm, n), lambda i: (i, 0)), _const_spec((n, D_MODEL)), row, _const_spec((1, D_MODEL)), row],
        out_specs=[row, _const_spec((n, D_MODEL)), pl.BlockSpec((1, D_MODEL), lambda i: (0, 0))],
        scratch_shapes=[pltpu.VMEM((n, D_MODEL), F32)],
        params=_params("arbitrary"))


def _proj_dw(dproj, x_in, pre, name, comm=None, tm=1024):
    t, n = dproj.shape
    steps = t // tm

    def body(dp_ref, x_ref, pre_ref, dwt16_ref, dwt_ref):
        @pl.when(pl.program_id(0) == 0)
        def _():
            dwt_ref[...] = jnp.zeros_like(dwt_ref)

        xv = x_ref[...]
        r = lax.rsqrt(jnp.mean(xv * xv, axis=-1, keepdims=True) + EPS)
        h = (xv * r * pre_ref[...]).astype(BF16)
        dwt_ref[...] += lax.dot_general(dp_ref[...], h, TN, preferred_element_type=F32)

        @pl.when(pl.program_id(0) == steps - 1)
        def _():
            dwt16_ref[...] = dwt_ref[...].astype(BF16)

    return _fused_call(
        body, comm, (dproj, x_in, pre), name=name, grid=(steps,),
        out_shape=jax.ShapeDtypeStruct((n, D_MODEL), BF16),
        in_specs=[pl.BlockSpec((tm, n), lambda i: (i, 0)), pl.BlockSpec((tm, D_MODEL), lambda i: (i, 0)),
                  _const_spec((1, D_MODEL))],
        out_specs=pl.BlockSpec((n, D_MODEL), lambda i: (0, 0)),
        scratch_shapes=[pltpu.VMEM((n, D_MODEL), F32)],
        params=_params("arbitrary"))


def _proj_dx(dproj, wt, x_in, pre, g, name, comm=None, tm=512):
    t, n = dproj.shape

    def body(dp_ref, wt_ref, x_ref, pre_ref, g_ref, dx_ref, dpre_ref):
        @pl.when(pl.program_id(0) == 0)
        def _():
            dpre_ref[...] = jnp.zeros_like(dpre_ref)

        dh = jnp.dot(dp_ref[...], wt_ref[...], preferred_element_type=F32)
        xv = x_ref[...]
        r = lax.rsqrt(jnp.mean(xv * xv, axis=-1, keepdims=True) + EPS)
        xn = xv * r
        dpre_ref[...] += jnp.sum(dh * xn, axis=0, keepdims=True)
        dxn = dh * pre_ref[...]
        dx_ref[...] = g_ref[...] + r * (dxn - xn * jnp.mean(dxn * xn, axis=-1, keepdims=True))

    row = pl.BlockSpec((tm, D_MODEL), lambda i: (i, 0))
    return _fused_call(
        body, comm, (dproj, wt, x_in, pre, g), name=name, grid=(t // tm,),
        out_shape=[jax.ShapeDtypeStruct((t, D_MODEL), F32), jax.ShapeDtypeStruct((1, D_MODEL), F32)],
        in_specs=[pl.BlockSpec((tm, n), lambda i: (i, 0)), _const_spec((n, D_MODEL)), row, _const_spec((1, D_MODEL)), row],
        out_specs=[row, pl.BlockSpec((1, D_MODEL), lambda i: (0, 0))],
        params=_params("arbitrary"))


def _group_masks():
    lane = lax.broadcasted_iota(jnp.int32, (1, GROUP * HEAD_DIM), 1)
    return [(lane // HEAD_DIM == g).astype(F32) for g in range(GROUP)]


def _stack_groups(v, masks):
    return jnp.concatenate([v * m for m in masks], axis=0)


def _unstack_groups(v, masks):
    out = v[0:BLOCK] * masks[0]
    for g in range(1, GROUP):
        out = out + v[g * BLOCK:(g + 1) * BLOCK] * masks[g]
    return out


def _repeat_head(kv2, kvh):
    first = lax.broadcasted_iota(jnp.int32, kv2.shape, 1) < HEAD_DIM
    rolled = pltpu.roll(kv2, HEAD_DIM, 1)
    one = jnp.where(first, kv2, rolled) if kvh == 0 else jnp.where(first, rolled, kv2)
    return jnp.concatenate([one, one], axis=1)


def _fold_head(v4):
    a = v4[:, 0:128] + v4[:, 128:256]
    return a + pltpu.roll(a, HEAD_DIM, 1)


ATTN_CONSTS = [pltpu.VMEM((KV_HEADS, GROUP * BLOCK, 2 * BLOCK), F32)]


def _fill_attn_bias(bias_ref):
    row = lax.broadcasted_iota(jnp.int32, (GROUP * BLOCK, 2 * BLOCK), 0)
    col = lax.broadcasted_iota(jnp.int32, (GROUP * BLOCK, 2 * BLOCK), 1)
    dist = (row % BLOCK) + BLOCK - col
    band = (dist >= 0) & (dist < BLOCK)
    rb = lax.broadcasted_iota(jnp.int32, (GROUP * BLOCK, 1), 0) // BLOCK
    for kvh in range(KV_HEADS):
        slope = jnp.zeros((GROUP * BLOCK, 1), F32)
        for g in range(GROUP):
            slope = jnp.where(rb == g, 2.0 ** (-(kvh * GROUP + g + 1)), slope)
        bias_ref[kvh] = jnp.where(band, -slope * dist.astype(F32), NEG)


def _row_sinks(kvh, sink_ref):
    rb = lax.broadcasted_iota(jnp.int32, (GROUP * BLOCK, 1), 0) // BLOCK
    sink = jnp.zeros((GROUP * BLOCK, 1), F32)
    for g in range(GROUP):
        sink = jnp.where(rb == g, sink_ref[0, kvh * GROUP + g], sink)
    return sink


def _attn_probs(qk, k4, bias, sink, no_past, masks):
    qs = _stack_groups(qk, masks).astype(BF16)
    s = lax.dot_general(qs, k4, NT, preferred_element_type=F32) * (HEAD_DIM ** -0.5) + bias
    s = jnp.concatenate([jnp.where(no_past, NEG, s[:, 0:BLOCK]), s[:, BLOCK:]], axis=1)
    mx = jnp.maximum(jnp.max(s, axis=-1, keepdims=True), sink)
    e = jnp.exp(s - mx)
    es = jnp.exp(sink - mx)
    inv = 1.0 / (jnp.sum(e, axis=-1, keepdims=True) + es)
    return qs, e * inv, es * inv


def _pool_forward(u_ext, g, t0):
    n = u_ext.shape[0] - POOL_HALO
    s = u_ext
    for step in range(g + 1):
        s = s + pltpu.roll(s, 1 << step, 0)
    w = 2 << g
    t = t0 + lax.broadcasted_iota(jnp.int32, (n, 1), 0)
    cnt = jnp.minimum(t + 1, w).astype(F32)
    return s[POOL_HALO:] / cnt - u_ext[POOL_HALO:]


def _layer0_fwd(proj, sinks, pool_w, pool_scale, w_out, x_in, post, comm=None, tq=512):
    t = proj.shape[0]
    nblk = tq // BLOCK

    def body(main_ref, halo_ref, sink_ref, pw_ref, ps_ref, w_ref, x_ref, p_ref, o_ref, y_ref, xo_ref, kv_ref, bias_ref):
        i = pl.program_id(0)
        t0 = i * tq
        masks = _group_masks()

        @pl.when(i == 0)
        def _():
            _fill_attn_bias(bias_ref)

        kv_ref[0:BLOCK, :] = halo_ref[:, COL_K:COL_K + 256]
        kv_ref[BLOCK:, :] = main_ref[:, COL_K:COL_K + 256]

        def block(jb, carry):
            r0 = pl.multiple_of(jb * BLOCK, BLOCK)
            no_past = t0 + r0 == 0
            q = main_ref[pl.ds(r0, BLOCK), COL_Q:COL_Q + ATTN_WIDTH]
            ga = main_ref[pl.ds(r0, BLOCK), COL_GA:COL_GA + ATTN_WIDTH]
            kk = kv_ref[pl.ds(r0, 2 * BLOCK), 0:128]
            vv = kv_ref[pl.ds(r0, 2 * BLOCK), 128:256]
            outs = []
            for kvh in range(KV_HEADS):
                k4 = _repeat_head(kk, kvh).astype(BF16)
                v4 = _repeat_head(vv, kvh).astype(BF16)
                _, p, _ = _attn_probs(q[:, kvh * 256:(kvh + 1) * 256], k4, bias_ref[kvh], _row_sinks(kvh, sink_ref), no_past, masks)
                pv = jnp.dot(p.astype(BF16), v4, preferred_element_type=F32)
                outs.append(_unstack_groups(pv, masks))
            attn = jnp.concatenate(outs, axis=1)
            o_ref[pl.ds(r0, BLOCK), 0:ATTN_WIDTH] = (attn * _silu(ga)).astype(BF16)
            return carry

        lax.fori_loop(0, nblk, block, 0, unroll=True)

        for g in range(POOL_GROUPS):
            cu = COL_U + g * POOL_GC
            cg = COL_GB + g * POOL_GC
            halo_u = jnp.where(i == 0, 0.0, halo_ref[BLOCK - POOL_HALO:BLOCK, cu:cu + POOL_GC])
            u_ext = jnp.concatenate([halo_u, main_ref[:, cu:cu + POOL_GC]], axis=0)
            pooled = _pool_forward(u_ext, g, t0)
            y = jnp.dot(pooled.astype(BF16), pw_ref[g].astype(BF16), preferred_element_type=F32)
            y = y * ps_ref[:, g * POOL_GC:(g + 1) * POOL_GC]
            o_ref[:, ATTN_WIDTH + g * POOL_GC:ATTN_WIDTH + (g + 1) * POOL_GC] =(y * _silu(main_ref[:, cg:cg + POOL_GC])).astype(BF16)

        xo_ref[...] = _project_out(o_ref[...], w_ref, x_ref, p_ref, y_ref)

    row = pl.BlockSpec((tq, D_MODEL), lambda i: (i, 0))
    return _fused_call(
        body, comm, (proj, proj, sinks, pool_w, pool_scale, w_out, x_in, post), name="layer0_fwd", grid=(t // tq,),
        out_shape=[jax.ShapeDtypeStruct((t, D_MODEL), BF16), jax.ShapeDtypeStruct((t, D_MODEL), F32),
                   jax.ShapeDtypeStruct((t, D_MODEL), F32)],
        in_specs=[pl.BlockSpec((tq, EVEN_IN), lambda i: (i, 0)),
                  pl.BlockSpec((BLOCK, EVEN_IN), lambda i: (jnp.maximum(i * nblk - 1, 0), 0)),
                  pl.BlockSpec(memory_space=pltpu.SMEM),
                  _const_spec((POOL_GROUPS, POOL_GC, POOL_GC)), _const_spec((1, POOL_WIDTH)),
                  _const_spec((D_MODEL, D_MODEL)), row, _const_spec((1, D_MODEL))],
        out_specs=[row, row, row],
        scratch_shapes=[pltpu.VMEM((tq + BLOCK, 256), F32)] + ATTN_CONSTS,
        params=_params("arbitrary"))


def _mix0_bwd(proj, dmix, sinks, pool_w, pool_scale, comm=None, tq=512):
    t = proj.shape[0]
    nt = t // tq
    nblk = tq // BLOCK

    def body(main_ref, halo_ref, next_ref, dm_ref, dmn_ref, sink_ref, pw_ref, ps_ref,
             o_ref, dsk_ref, dpw_ref, dps_ref, kv_ref, dkv_ref, carry_ref, bias_ref):
        i = pl.program_id(0)
        ii = nt - 1 - i
        t0 = ii * tq
        masks = _group_masks()

        @pl.when(i == 0)
        def _():
            _fill_attn_bias(bias_ref)
            dsk_ref[...] = jnp.zeros_like(dsk_ref)
            dpw_ref[...] = jnp.zeros_like(dpw_ref)
            dps_ref[...] = jnp.zeros_like(dps_ref)
            carry_ref[...] = jnp.zeros_like(carry_ref)

        kv_ref[0:BLOCK, :] = halo_ref[:, COL_K:COL_K + 256]
        kv_ref[BLOCK:, :] = main_ref[:, COL_K:COL_K + 256]
        dkv_ref[0:tq, :] = jnp.zeros((tq, 256), F32)
        dkv_ref[tq:, :] = carry_ref[...]

        def block(jb, carry):
            r0 = pl.multiple_of(jb * BLOCK, BLOCK)
            no_past = t0 + r0 == 0
            q = main_ref[pl.ds(r0, BLOCK), COL_Q:COL_Q + ATTN_WIDTH]
            ga = main_ref[pl.ds(r0, BLOCK), COL_GA:COL_GA + ATTN_WIDTH]
            dya = dm_ref[pl.ds(r0, BLOCK), 0:ATTN_WIDTH]
            kk = kv_ref[pl.ds(r0, 2 * BLOCK), 0:128]
            vv = kv_ref[pl.ds(r0, 2 * BLOCK), 128:256]
            silu_ga, dsilu_ga = _silu_and_grad(ga)
            do = dya * silu_ga
            first = lax.broadcasted_iota(jnp.int32, (2 * BLOCK, 128), 1) < HEAD_DIM
            attn, dq, dk, dv = [], [], [], []
            for kvh in range(KV_HEADS):
                k4 = _repeat_head(kk, kvh).astype(BF16)
                v4 = _repeat_head(vv, kvh).astype(BF16)
                qs, p, ps = _attn_probs(q[:, kvh * 256:(kvh + 1) * 256], k4, bias_ref[kvh], _row_sinks(kvh, sink_ref), no_past, masks)
                pb = p.astype(BF16)
                o_k = _unstack_groups(jnp.dot(pb, v4, preferred_element_type=F32), masks)
                do_k = do[:, kvh * 256:(kvh + 1) * 256]
                dos = _stack_groups(do_k, masks).astype(BF16)
                prod = do_k * o_k
                delta = jnp.concatenate([jnp.sum(prod * m, axis=-1, keepdims=True) for m in masks], axis=0)
                dp = lax.dot_general(dos, v4, NT, preferred_element_type=F32)
                ds = (p * (dp - delta)).astype(BF16)
                sink_term = ps * delta
                for g in range(GROUP):
                    h = kvh * GROUP + g
                    dsk_ref[h:h + 1, :] -= jnp.sum(sink_term[g * BLOCK:(g + 1) * BLOCK], keepdims=True)
                scale = HEAD_DIM ** -0.5
                dq.append(_unstack_groups(jnp.dot(ds, k4, preferred_element_type=F32), masks) * scale)
                dk.append(_fold_head(lax.dot_general(ds, qs, TN, preferred_element_type=F32)) * scale)
                dv.append(_fold_head(lax.dot_general(pb, dos, TN, preferred_element_type=F32)))
                attn.append(o_k)
            o_ref[pl.ds(r0, BLOCK), COL_Q:COL_Q + ATTN_WIDTH] = jnp.concatenate(dq, axis=1).astype(BF16)
            o_all = jnp.concatenate(attn, axis=1)
            o_ref[pl.ds(r0, BLOCK), COL_GA:COL_GA + ATTN_WIDTH] = (dya * o_all * dsilu_ga).astype(BF16)
            dkv = jnp.concatenate([jnp.where(first, dk[0], dk[1]), jnp.where(first, dv[0], dv[1])], axis=1)
            dkv_ref[pl.ds(r0, 2 * BLOCK), :] += dkv
            return carry

        lax.fori_loop(0, nblk, block, 0, unroll=True)
        carry_ref[...] = dkv_ref[0:BLOCK, :]
        o_ref[:, COL_K:COL_K + 256] = dkv_ref[BLOCK:, :].astype(BF16)

        last = ii == nt - 1
        for g in range(POOL_GROUPS):
            cu = COL_U + g * POOL_GC
            cg = COL_GB + g * POOL_GC
            cm = ATTN_WIDTH + g * POOL_GC
            pw = pw_ref[g].astype(BF16)
            sc = ps_ref[:, g * POOL_GC:(g + 1) * POOL_GC]
            halo_u = jnp.where(ii == 0, 0.0, halo_ref[BLOCK - POOL_HALO:BLOCK, cu:cu + POOL_GC])
            u_ext = jnp.concatenate([halo_u, main_ref[:, cu:cu + POOL_GC]], axis=0)
            pooled = _pool_forward(u_ext, g, t0).astype(BF16)
            y_raw = jnp.dot(pooled, pw, preferred_element_type=F32)
            gb = main_ref[:, cg:cg + POOL_GC]
            dyb = dm_ref[:, cm:cm + POOL_GC]
            silu_gb, dsilu_gb = _silu_and_grad(gb)
            dypool = dyb * silu_gb
            dps_ref[:, g * POOL_GC:(g + 1) * POOL_GC] += jnp.sum(dypool * y_raw, axis=0, keepdims=True)
            o_ref[:, cg:cg + POOL_GC] = (dyb * (y_raw * sc) * dsilu_gb).astype(BF16)
            dyraw = dypool * sc
            dyraw_n = jnp.where(last, 0.0, dmn_ref[:, cm:cm + POOL_GC] * _silu(next_ref[:, cg:cg + POOL_GC]) * sc)
            dpw_ref[g * POOL_GC:(g + 1) * POOL_GC, :] += lax.dot_general(pooled, dyraw.astype(BF16), TN,
                                                                         preferred_element_type=F32)
            dyraw_ext = jnp.concatenate([dyraw, dyraw_n], axis=0).astype(BF16)
            dpooled = lax.dot_general(dyraw_ext, pw, NT, preferred_element_type=F32)
            w = 2 << g
            tt = t0 + lax.broadcasted_iota(jnp.int32, (tq + POOL_HALO, 1), 0)
            s = dpooled / jnp.minimum(tt + 1, w).astype(F32)
            for step in range(g + 1):
                s = s + pltpu.roll(s, tq + POOL_HALO - (1 << step), 0)
            o_ref[:, cu:cu + POOL_GC] = (s[0:tq] - dpooled[0:tq]).astype(BF16)

    rev = lambda i: nt - 1 - i
    nxt = lambda i: (jnp.minimum((rev(i) + 1) * (tq // POOL_HALO), t // POOL_HALO - 1), 0)
    row = pl.BlockSpec((tq, D_MODEL), lambda i: (rev(i), 0))
    nxt_row = pl.BlockSpec((POOL_HALO, D_MODEL), nxt)
    return _fused_call(
        body, comm, (proj, proj, proj, dmix, dmix, sinks, pool_w, pool_scale), name="mix0_bwd", grid=(nt,),
        out_shape=[jax.ShapeDtypeStruct((t, EVEN_IN), BF16), jax.ShapeDtypeStruct((8, 128), F32),
                   jax.ShapeDtypeStruct((POOL_GROUPS * POOL_GC, POOL_GC), F32), jax.ShapeDtypeStruct((1, POOL_WIDTH), F32)],
        in_specs=[pl.BlockSpec((tq, EVEN_IN), lambda i: (rev(i), 0)),
                  pl.BlockSpec((BLOCK, EVEN_IN), lambda i: (jnp.maximum(rev(i) * nblk - 1, 0), 0)),
                  pl.BlockSpec((POOL_HALO, EVEN_IN), nxt),
                  row, nxt_row,
                  pl.BlockSpec(memory_space=pltpu.SMEM),
                  _const_spec((POOL_GROUPS, POOL_GC, POOL_GC)), _const_spec((1, POOL_WIDTH))],
        out_specs=[pl.BlockSpec((tq, EVEN_IN), lambda i: (rev(i), 0)),
                   pl.BlockSpec((8, 128), lambda i: (0, 0)),
                   pl.BlockSpec((POOL_GROUPS * POOL_GC, POOL_GC), lambda i: (0, 0)),
                   pl.BlockSpec((1, POOL_WIDTH), lambda i: (0, 0))],
        scratch_shapes=[pltpu.VMEM((tq + BLOCK, 256), F32), pltpu.VMEM((tq + BLOCK, 256), F32),
                        pltpu.VMEM((BLOCK, 256), F32)] + ATTN_CONSTS,
        params=_params("arbitrary"))


CONV_RC = 32
CONV_CC = 128
CONV_CHAINS = 4
CONV_UNROLL = 2


def _fill_shifted(s_ref, rows):
    for b in range(1, 8):
        s_ref[b, 0:rows - 8, :] = s_ref[0, b:b + rows - 8, :]


def _tap_blocks(s_ref, r, cols, lead):
    for b in range(8):
        taps = [(a, 8 * a + b - lead) for a in range(5) if 0 <= 8 * a + b - lead < CONV_K]
        span = 8 * max(a for a, _ in taps) + CONV_RC
        blk = s_ref[b, pl.ds(r, span), cols]
        for a, k in taps:
            yield k, blk[8 * a:8 * a + CONV_RC]


def _conv_taps(s_ref, w_ref, r, cols, lead, reverse):
    accs = [None] * CONV_CHAINS
    for n, (k, blk) in enumerate(_tap_blocks(s_ref, r, cols, lead)):
        kw = CONV_K - 1 - k if reverse else k
        term = blk * w_ref[kw:kw + 1, cols]
        accs[n % CONV_CHAINS] = term if accs[n % CONV_CHAINS] is None else accs[n % CONV_CHAINS] + term
    while len(accs) > 1:
        accs = [a + b for a, b in zip(accs[0::2], accs[1::2])]
    return accs[0]


def _layer_norm_fwd(cf, lng, lnb):
    mu = jnp.mean(cf, axis=-1, keepdims=True)
    xc = cf - mu
    rstd = lax.rsqrt(jnp.mean(xc * xc, axis=-1, keepdims=True) + EPS)
    chat = xc * rstd
    return chat, rstd, chat * lng + lnb


def _layer1_fwd(proj, dw, dwb, lng, lnb, w_out, x_in, post, target, tt=256):
    t = proj.shape[0]
    lead = CONV_HALO - (CONV_K - 1)

    def body(main_ref, halo_ref, w_ref, b_ref, g_ref, lb_ref, wo_ref, x_ref, p_ref, t_ref,
             o_ref, c_ref, y_ref, dl_ref, l_ref, gs_ref):
        i = pl.program_id(0)
        hv = halo_ref[...]
        gs_ref[0, 0:CONV_HALO, :] = jnp.where(i == 0, 0.0, hv[:, GLU_A] * _sigmoid(hv[:, GLU_B]))
        gs_ref[0, CONV_HALO:CONV_HALO + tt, :] = main_ref[:, GLU_A] * _sigmoid(main_ref[:, GLU_B])
        _fill_shifted(gs_ref, tt + CONV_HALO)

        for c in range(D_MODEL // CONV_CC):
            cols = slice(c * CONV_CC, (c + 1) * CONV_CC)

            def chunk(j, carry):
                r = pl.multiple_of(j * CONV_RC, CONV_RC)
                c_ref[pl.ds(r, CONV_RC), cols] = _conv_taps(gs_ref, w_ref, r, cols, lead, False) + b_ref[:, cols]
                return carry
            lax.fori_loop(0, tt // CONV_RC, chunk, 0, unroll=CONV_UNROLL)

        _, _, cn = _layer_norm_fwd(c_ref[...], g_ref[...], lb_ref[...])
        o_ref[...] = (_silu(cn) * _silu(main_ref[:, GATE])).astype(BF16)

        d = _project_out(o_ref[...], wo_ref, x_ref, p_ref, y_ref) - t_ref[...]
        dl_ref[...] = d * (1.0 / D_MODEL)

        @pl.when(i == 0)
        def _():
            l_ref[...] = jnp.zeros_like(l_ref)

        l_ref[...] += 0.5 * jnp.sum(jnp.mean(d * d, axis=-1, keepdims=True))

    vec = _const_spec((1, D_MODEL))
    row = pl.BlockSpec((tt, D_MODEL), lambda i: (i, 0))
    f32_rows = jax.ShapeDtypeStruct((t, D_MODEL), F32)
    return pl.pallas_call(
        body, name="layer1_fwd", grid=(t // tt,),
        out_shape=[jax.ShapeDtypeStruct((t, D_MODEL), BF16), f32_rows, f32_rows, f32_rows,
                   jax.ShapeDtypeStruct((8, 128), F32)],
        in_specs=[pl.BlockSpec((tt, 3 * D_MODEL), lambda i: (i, 0)),
                  pl.BlockSpec((CONV_HALO, 3 * D_MODEL), lambda i: (jnp.maximum(i * (tt // CONV_HALO) - 1, 0), 0)),
                  _const_spec((CONV_K, D_MODEL)), vec, vec, vec,
                  _const_spec((D_MODEL, D_MODEL)), row, vec, row],
        out_specs=[row, row, row, row, pl.BlockSpec((8, 128), lambda i: (0, 0))],
        scratch_shapes=[pltpu.VMEM((8, tt + CONV_HALO, D_MODEL), F32)],
        compiler_params=_params("arbitrary"),
    )(proj, proj, dw, dwb, lng, lnb, w_out, x_in, post, target)


def _layer1_bwd(proj, cf, gy, y, z, w_out, post, dw, lng, lnb, comm=None, tt=256):
    t = proj.shape[0]
    nt = t // tt
    te = tt + CONV_HALO

    def body(main_ref, next_ref, cf_ref, cfn_ref, gy_ref, gyn_ref, y_ref, yn_ref, z_ref, wo_ref, po_ref,
             w_ref, g_ref, lb_ref,
             o_ref, ddw_ref, ddb_ref, dg_ref, dlb_ref, dwo16_ref, dpo_ref, ds_ref, glu_ref, sb_ref, dwo_ref):
        i = pl.program_id(0)

        @pl.when(i == 0)
        def _():
            ddw_ref[...] = jnp.zeros_like(ddw_ref)
            ddb_ref[...] = jnp.zeros_like(ddb_ref)
            dg_ref[...] = jnp.zeros_like(dg_ref)
            dlb_ref[...] = jnp.zeros_like(dlb_ref)

        dzv = _post_bwd_rows(jnp.concatenate([gy_ref[...], gyn_ref[...]], axis=0),
                             jnp.concatenate([y_ref[...], yn_ref[...]], axis=0), z_ref[...], tt, i == 0, i == nt - 1,
                             po_ref, wo_ref, dwo_ref, dwo16_ref, dpo_ref)
        dzv = jnp.concatenate([dzv[0:tt], jnp.where(i < nt - 1, dzv[tt:], 0.0)], axis=0)
        lng = g_ref[...]
        chat, rstd, cn = _layer_norm_fwd(jnp.concatenate([cf_ref[...], cfn_ref[...]], axis=0), lng, lb_ref[...])
        gate = jnp.concatenate([main_ref[:, GATE], next_ref[:, GATE]], axis=0)
        silu_cn, dsilu_cn = _silu_and_grad(cn)
        silu_gate, dsilu_gate = _silu_and_grad(gate)
        o_ref[:, GATE] = (dzv * silu_cn * dsilu_gate)[0:tt].astype(BF16)
        dcn = dzv * silu_gate * dsilu_cn
        dg_ref[...] += jnp.sum((dcn * chat)[0:tt], axis=0, keepdims=True)
        dlb_ref[...] += jnp.sum(dcn[0:tt], axis=0, keepdims=True)
        dchat = dcn * lng
        dcf = rstd * (dchat - jnp.mean(dchat, axis=-1, keepdims=True) - chat * jnp.mean(dchat * chat, axis=-1, keepdims=True))
        ddb_ref[...] += jnp.sum(dcf[0:tt], axis=0, keepdims=True)
        ds_ref[0, 0:te, :] = dcf
        ds_ref[0, te:, :] = jnp.zeros((8, D_MODEL), F32)
        _fill_shifted(ds_ref, te + 8)
        sb_ref[...] = _sigmoid(main_ref[:, GLU_B])
        glu_ref[...] = main_ref[:, GLU_A] * sb_ref[...]

        for c in range(D_MODEL // CONV_CC):
            cols = slice(c * CONV_CC, (c + 1) * CONV_CC)
            gcols = slice(D_MODEL + c * CONV_CC, D_MODEL + (c + 1) * CONV_CC)

            def chunk(j, carry):
                r = pl.multiple_of(j * CONV_RC, CONV_RC)
                dglu = _conv_taps(ds_ref, w_ref, r, cols, 0, True)
                sb = sb_ref[pl.ds(r, CONV_RC), cols]
                o_ref[pl.ds(r, CONV_RC), cols] = (dglu * sb).astype(BF16)
                o_ref[pl.ds(r, CONV_RC), gcols] = (dglu * glu_ref[pl.ds(r, CONV_RC), cols] * (1.0 - sb)).astype(BF16)
                return carry
            lax.fori_loop(0, tt // CONV_RC, chunk, 0, unroll=CONV_UNROLL)

            def taps(j, accs):
                r = pl.multiple_of(j * CONV_RC, CONV_RC)
                gl = glu_ref[pl.ds(r, CONV_RC), cols]
                new = list(accs)
                for m, blk in _tap_blocks(ds_ref, r, cols, 0):
                    prod = blk * gl
                    part = prod[0:8]
                    for q in range(1, CONV_RC // 8):
                        part = part + prod[8 * q:8 * q + 8]
                    new[m] = new[m] + part
                return tuple(new)
            accs = lax.fori_loop(0, tt // CONV_RC, taps, tuple(jnp.zeros((8, CONV_CC), F32) for _ in range(CONV_K)))
            for m in range(CONV_K):
                k = CONV_K - 1 - m
                ddw_ref[k:k + 1, cols] += jnp.sum(accs[m], axis=0, keepdims=True)

    vec = _const_spec((1, D_MODEL))
    vec_out = pl.BlockSpec((1, D_MODEL), lambda i: (0, 0))
    row = pl.BlockSpec((tt, D_MODEL), lambda i: (i, 0))
    nxt = lambda i: (jnp.minimum((i + 1) * (tt // CONV_HALO), t // CONV_HALO - 1), 0)
    nxt_row = pl.BlockSpec((CONV_HALO, D_MODEL), nxt)
    vec_f32 = jax.ShapeDtypeStruct((1, D_MODEL), F32)
    square = _const_spec((D_MODEL, D_MODEL))
    return _fused_call(
        body, comm, (proj, proj, cf, cf, gy, gy, y, y, z, w_out, post, dw, lng, lnb), name="layer1_bwd", grid=(nt,),
        out_shape=[jax.ShapeDtypeStruct((t, 3 * D_MODEL), BF16), jax.ShapeDtypeStruct((CONV_K, D_MODEL), F32),
                   vec_f32, vec_f32, vec_f32, jax.ShapeDtypeStruct((D_MODEL, D_MODEL), BF16), vec_f32],
        in_specs=[pl.BlockSpec((tt, 3 * D_MODEL), lambda i: (i, 0)),
                  pl.BlockSpec((CONV_HALO, 3 * D_MODEL), nxt),
                  row, nxt_row, row, nxt_row, row, nxt_row, row, square, vec,
                  _const_spec((CONV_K, D_MODEL)), vec, vec],
        out_specs=[pl.BlockSpec((tt, 3 * D_MODEL), lambda i: (i, 0)),
                   pl.BlockSpec((CONV_K, D_MODEL), lambda i: (0, 0)), vec_out, vec_out, vec_out, square, vec_out],
        scratch_shapes=[pltpu.VMEM((8, te + 8, D_MODEL), F32), pltpu.VMEM((tt, D_MODEL), F32),
                        pltpu.VMEM((tt, D_MODEL), F32), pltpu.VMEM((D_MODEL, D_MODEL), F32)],
        params=_params("arbitrary"))


def _piece_sum(parts, place, name):
    r = parts[0][0].shape[1]

    def body(p_ref, *refs):
        o_ref = refs[-1]
        acc = refs[0][0].astype(F32)
        for part in refs[1:-1]:
            acc = acc + part[0].astype(F32)
        o_ref[0] = acc

    blk = (1, r, D_MODEL)
    spec = lambda slot: pl.BlockSpec(blk, lambda j, p_ref: (slot(p_ref), 0, 0))
    return pl.pallas_call(
        body, name=name,
        grid_spec=pltpu.PrefetchScalarGridSpec(
            num_scalar_prefetch=1, grid=(1,),
            in_specs=[spec(slot) for _, slot in parts],
            out_specs=pl.BlockSpec(blk, lambda j, p_ref: (p_ref[1], 0, 0))),
        out_shape=jax.ShapeDtypeStruct((2, r, D_MODEL), F32),
        compiler_params=_params("arbitrary"),
    )(place, *[a for a, _ in parts])


def _direct_parts(own, recv):
    peer = lambda m: (lambda p: p[0] ^ m)
    return [(own, peer(0))] + [(recv, peer(m)) for m in range(1, N_DEV)]


def _share_with_sibling(halves, name):
    n = len(halves)

    def body(*refs):
        outs = refs[n:2 * n]
        send_sems, recv_sems = refs[2 * n:]
        x, y, c, _ = _place()
        send = [pltpu.make_async_remote_copy(
            src_ref=outs[t].at[c], dst_ref=outs[t].at[c], send_sem=send_sems.at[t], recv_sem=recv_sems.at[t],
            device_id=(x, y, 1 - c), device_id_type=MESH_ID) for t in range(n)]
        recv = [pltpu.make_async_remote_copy(
            src_ref=outs[t].at[c], dst_ref=outs[t].at[1 - c], send_sem=send_sems.at[t], recv_sem=recv_sems.at[t],
            device_id=(x, y, 1 - c), device_id_type=MESH_ID) for t in range(n)]
        for cp in send:
            cp.start()
        for cp in recv:
            cp.wait_recv()
        for cp in send:
            cp.wait_send()

    return pl.pallas_call(
        body, name=name,
        out_shape=[jax.ShapeDtypeStruct(h.shape, h.dtype) for h in halves],
        in_specs=[ANY] * n, out_specs=[ANY] * n,
        input_output_aliases={t: t for t in range(n)},
        scratch_shapes=[pltpu.SemaphoreType.DMA((n,)), pltpu.SemaphoreType.DMA((n,))],
    )(*halves)


def _sum8(parts, name):
    r = parts.shape[1]

    def body(p_ref, o_ref):
        acc = p_ref[0]
        for k in range(1, N_DEV):
            acc = acc + p_ref[k]
        o_ref[...] = acc

    return pl.pallas_call(
        body, name=name, out_shape=jax.ShapeDtypeStruct((r, 128), F32),
        in_specs=[pl.BlockSpec(memory_space=pltpu.VMEM)], out_specs=pl.BlockSpec(memory_space=pltpu.VMEM),
    )(parts)


def _adamw(w, g, m, v, name):
    shape = w.shape
    cols = shape[-1]
    rows = w.size // cols
    rt = 256 if rows % 256 == 0 else rows

    def body(w_ref, g_ref, m_ref, v_ref, d_ref, nm_ref, nv_ref):
        gv = g_ref[...]
        mn = ADAM_B1 * m_ref[...] + (1.0 - ADAM_B1) * gv
        vn = ADAM_B2 * v_ref[...] + (1.0 - ADAM_B2) * (gv * gv)
        m_hat = mn / (1.0 - ADAM_B1 ** ADAM_STEP)
        v_hat = vn / (1.0 - ADAM_B2 ** ADAM_STEP)
        d_ref[...] = -ADAM_LR * (m_hat / (jnp.sqrt(v_hat) + ADAM_EPS) + ADAM_WD * w_ref[...])
        nm_ref[...] = mn
        nv_ref[...] = vn

    spec = pl.BlockSpec((rt, cols), lambda i: (i, 0))
    outs = pl.pallas_call(
        body, name=name, grid=(rows // rt,),
        out_shape=[jax.ShapeDtypeStruct((rows, cols), F32)] * 3,
        in_specs=[spec] * 4, out_specs=[spec] * 3,
        compiler_params=_params("parallel"),
    )(*[a.reshape(rows, cols) for a in (w, g, m, v)])
    return [o.reshape(shape) for o in outs]


SMALL_ROWS = 832


def _pack_small(g):
    parts = [g["loss"], g["pre1"].reshape(8, 128), g["post0"].reshape(8, 128),
             g["post1"].reshape(8, 128), g["sinks"], jnp.pad(g["pool_scale"].reshape(4, 128), ((0, 4), (0, 0))),
             g["pool_w"], g["dw"].reshape(248, 128), g["dwb"].reshape(8, 128), g["lng"].reshape(8, 128),
             g["lnb"].reshape(8, 128)]
    assert sum(p.shape[0] for p in parts) == SMALL_ROWS
    return jnp.concatenate(parts, axis=0)


def _unpack_small(s):
    out, r = {}, 0
    for key, rows, shape in (("loss", 8, (8, 128)), ("pre1", 8, (1, D_MODEL)), ("post", 16, (2, D_MODEL)),
                             ("sinks", 8, (8, 128)),
                             ("pool_scale", 4, (1, POOL_WIDTH)), ("pad", 4, (4, 128)), ("pool_w", 512, (1, 4, 128, 128)),
                             ("dw", 248, (CONV_K, D_MODEL)), ("dwb", 8, (1, D_MODEL)), ("lng", 8, (1, D_MODEL)),
                             ("lnb", 8, (1, D_MODEL))):
        out[key] = s[r:r + rows].reshape(shape)
        r += rows
    return out


def kernel(x, pre_norm, post_norm, a_w_in, a_sinks, b_pool_w, b_pool_scale, ab_w_out, c_w_in, c_dw_w, c_dw_b, c_ln_g, c_ln_b, c_w_out, loss_target, m_pre_norm, m_post_norm, m_a_w_in, m_a_sinks, m_b_pool_w, m_b_pool_scale, m_ab_w_out, m_c_w_in, m_c_dw_w, m_c_dw_b, m_c_ln_g, m_c_ln_b, m_c_w_out, v_pre_norm, v_post_norm, v_a_w_in, v_a_sinks, v_b_pool_w, v_b_pool_scale, v_ab_w_out, v_c_w_in, v_c_dw_w, v_c_dw_b, v_c_ln_g, v_c_ln_b, v_c_w_out):
    ix, iy = lax.axis_index("x"), lax.axis_index("y")
    chip_cols = (2 * ix + iy) * 256

    pad8 = lambda v: jnp.pad(v, ((0, -v.shape[0] % 8), (0, 0)))
    vec_shard = jnp.concatenate([pad8(c_dw_w.reshape(CONV_K, 256)), pad8(c_dw_b), pad8(c_ln_g), pad8(c_ln_b),
                                 jnp.zeros((8, 256), F32)], axis=0)
    x0, target = x[0], loss_target[0]
    pre0, pre1 = pre_norm[0:1], pre_norm[1:2]
    post0, post1 = post_norm[0:1], post_norm[1:2]
    pool_w = b_pool_w[0]

    (wa_t,) = _run_comm(_Gather([a_w_in[0].T.astype(BF16)], halve=True), "gather_a_w_in")
    wa_t = wa_t.reshape(EVEN_IN, D_MODEL)
    proj0, (w_ab,) = _norm_matmul(x0, pre0, wa_t, "proj0_fwd", comm=_Gather([ab_w_out[0].astype(BF16)], halve=True))
    w_ab = w_ab.reshape(D_MODEL, D_MODEL)
    (mix0, y0, x1), (wc_t, w_c, vecs) = _layer0_fwd(
        proj0, a_sinks, pool_w, b_pool_scale, w_ab, x0, post0,
        comm=_Gather([c_w_in[0].T.astype(BF16), c_w_out[0].astype(BF16), vec_shard], halve=True))
    wc_t = wc_t.reshape(3 * D_MODEL, D_MODEL)
    w_c = w_c.reshape(D_MODEL, D_MODEL)
    vecs = vecs.reshape(4, 64, 256).transpose(1, 0, 2).reshape(64, D_MODEL)
    dw, dwb, lng, lnb = vecs[0:CONV_K], vecs[32:33], vecs[40:41], vecs[48:49]
    proj1, _ = _norm_matmul(x1, pre1, wc_t, "proj1_fwd")
    z1, cf1, y1, g2, loss = _layer1_fwd(proj1, dw, dwb, lng, lnb, w_c, x1, post1, target)

    pieces = lambda m: m.reshape(N_DEV, m.shape[0] // N_DEV, D_MODEL)
    (dproj1, d_dw, d_dwb, d_lng, d_lnb, d_wc, d_post1), _ = _layer1_bwd(proj1, cf1, g2, y1, z1, w_c, post1, dw, lng, lnb)
    (g1, d_wct, d_pre1), _ = _pre_bwd(dproj1, wc_t, x1, pre1, g2, "proj1_bwd")
    dmix0, d_wab, d_post0 = _post_bwd(g1, y0, post0, w_ab, mix0, "out0_bwd")
    (dproj0, d_sinks, d_pw, d_ps), (r_wc, r_wct) = _mix0_bwd(proj0, dmix0, a_sinks, pool_w, b_pool_scale,
                                                            comm=_Scatter([pieces(d_wc), pieces(d_wct)]))
    g = dict(loss=loss, pre1=d_pre1, post0=d_post0, post1=d_post1, sinks=d_sinks, pool_w=d_pw, pool_scale=d_ps,
             dw=d_dw, dwb=d_dwb, lng=d_lng, lnb=d_lnb)
    d_wat, (small8, r_wab) = _proj_dw(dproj0, x0, pre0, "proj0_dw",
                                      comm=_Comms(_Gather([_pack_small(g)], halve=False), _Scatter([pieces(d_wab)])))
    sent = _scatter_start(pieces(d_wat), "scatter_a_start")
    (gx, d_pre0), _ = _proj_dx(dproj0, wa_t, x0, pre0 + sent[4][0:1, 0:1], g1, "proj0_dx")
    sent_pre0 = _scatter_start(jnp.broadcast_to(d_pre0.reshape(1, 8, 128), (N_DEV, 8, 128)), "pre0_start")
    own_wat, r_wat = _scatter_wait(*sent[:4], d_pre0, "scatter_a_wait")

    ic = lax.axis_index("c")
    me = 4 * ix + 2 * iy + ic
    place = jnp.stack([me, ic]).astype(jnp.int32)
    parts = [_direct_parts(own_wat, r_wat), _direct_parts(pieces(d_wab), r_wab), _direct_parts(pieces(d_wct), r_wct),
             _direct_parts(pieces(d_wc), r_wc)]
    halves = [_piece_sum(p, place, f"grad_sum{t}") for t, p in enumerate(parts)]
    g_wa_t, g_wab, g_wc_t, g_wc = [h.reshape(2 * h.shape[1], D_MODEL) for h in _share_with_sibling(halves, "grad_share")]
    _, pre0_8 = _scatter_wait(*sent_pre0[:4], g_wc, "pre0_wait")
    pre0_8 = lax.dynamic_update_slice(pre0_8, d_pre0.reshape(1, 8, 128), (me, 0, 0))
    g_c_w_in = g_wc_t.T[None]
    g_ab_w_out = g_wab[None]
    g_c_w_out = g_wc[None]

    s = _unpack_small(_sum8(small8, "small_sum"))
    layer = lax.broadcasted_iota(jnp.int32, (2, D_MODEL), 0)
    g_pre = jnp.where(layer == 0, _sum8(pre0_8, "pre0_sum").reshape(1, D_MODEL), s["pre1"])
    g_post = s["post"]
    g_sinks = s["sinks"][:, 0].reshape(1, 8)
    g_pool_w, g_pool_scale = s["pool_w"], s["pool_scale"]
    g_dw = lax.dynamic_slice_in_dim(s["dw"], chip_cols, 256, axis=1).reshape(1, CONV_K, 1, 256)
    g_dwb = lax.dynamic_slice_in_dim(s["dwb"], chip_cols, 256, axis=1)
    g_lng = lax.dynamic_slice_in_dim(s["lng"], chip_cols, 256, axis=1)
    g_lnb = lax.dynamic_slice_in_dim(s["lnb"], chip_cols, 256, axis=1)

    turn = lambda a: jnp.swapaxes(a, 1, 2)
    a_w_in, m_a_w_in, v_a_w_in = turn(a_w_in), turn(m_a_w_in), turn(v_a_w_in)
    grads = [g_pre, g_post, g_wa_t[None], g_sinks, g_pool_w, g_pool_scale, g_ab_w_out, g_c_w_in, g_dw, g_dwb, g_lng,
             g_lnb, g_c_w_out]
    weights = [pre_norm, post_norm, a_w_in, a_sinks, b_pool_w, b_pool_scale, ab_w_out, c_w_in, c_dw_w, c_dw_b, c_ln_g,
               c_ln_b, c_w_out]
    moms = [m_pre_norm, m_post_norm, m_a_w_in, m_a_sinks, m_b_pool_w, m_b_pool_scale, m_ab_w_out, m_c_w_in, m_c_dw_w,
            m_c_dw_b, m_c_ln_g, m_c_ln_b, m_c_w_out]
    vars_ = [v_pre_norm, v_post_norm, v_a_w_in, v_a_sinks, v_b_pool_w, v_b_pool_scale, v_ab_w_out, v_c_w_in, v_c_dw_w,
             v_c_dw_b, v_c_ln_g, v_c_ln_b, v_c_w_out]
    deltas, new_m, new_v = [], [], []
    for k, (w, gr, m, v) in enumerate(zip(weights, grads, moms, vars_)):
        d, nm, nv = _adamw(w, gr, m, v, f"adamw{k}")
        deltas.append(d)
        new_m.append(nm)
        new_v.append(nv)
    for outs in (grads, deltas, new_m, new_v):
        outs[2] = turn(outs[2])
    return (s["loss"][0, 0], gx[None], *grads, *deltas, *new_m, *new_v)
```

```python
import jax
import jax.numpy as jnp
from jax import lax
from jax.experimental import pallas as pl
from jax.experimental.pallas import tpu as pltpu

F32 = jnp.float32
BF16 = jnp.bfloat16

D_MODEL = 1024
EPS = 1e-6
NEG = -1e30
HEAD_DIM = 64
GROUP = 4
KV_HEADS = 2
BLOCK = 128
EVEN_IN = 2304
ATTN_WIDTH = 512
POOL_WIDTH = 512
COL_Q, COL_K, COL_GA, COL_U, COL_GB = 0, 512, 768, 1280, 1792
POOL_GROUPS = 4
POOL_GC = 128
POOL_HALO = 16
CONV_K = 31
CONV_HALO = 32
GLU_A = slice(0, D_MODEL)
GLU_B = slice(D_MODEL, 2 * D_MODEL)
GATE = slice(2 * D_MODEL, 3 * D_MODEL)
N_DEV = 8

ADAM_LR = 0.001
ADAM_B1 = 0.9
ADAM_B2 = 0.999
ADAM_EPS = 1e-08
ADAM_WD = 0.01
ADAM_STEP = 10

VMEM_LIMIT_BYTES = 56 * 1024 * 1024

NT = (((1,), (1,)), ((), ()))
TN = (((0,), (0,)), ((), ()))
MESH_ID = pl.DeviceIdType.MESH


def _params(*sem):
    return pltpu.CompilerParams(dimension_semantics=sem, vmem_limit_bytes=VMEM_LIMIT_BYTES)


def _const_spec(shape):
    nd = len(shape)
    return pl.BlockSpec(shape, lambda *_: (0,) * nd, pipeline_mode=pl.Buffered(1))


def _sigmoid(v):
    return 0.5 * jnp.tanh(0.5 * v) + 0.5


def _silu(v):
    return v * _sigmoid(v)


def _silu_and_grad(v):
    s = _sigmoid(v)
    return v * s, s * (1.0 + v * (1.0 - s))


ANY = pl.BlockSpec(memory_space=pl.ANY)


def _place():
    x, y, c = lax.axis_index("x"), lax.axis_index("y"), lax.axis_index("c")
    chips = [(1 - x, y), (x, 1 - y), (1 - x, 1 - y)]
    return x, y, c, chips


class _Gather:
    def __init__(self, blocks, halve):
        self.ins = list(blocks)
        self.halve = halve
        self.n = n = len(blocks)
        self.shapes = [((b.shape[0] // 2) if halve else b.shape[0], b.shape[1]) for b in blocks]
        self.out_shape = [jax.ShapeDtypeStruct((N_DEV, r, cols), b.dtype) for (r, cols), b in zip(self.shapes, blocks)]
        self.scratch = [pltpu.SemaphoreType.DMA((7 * n,)), pltpu.SemaphoreType.DMA((7 * n,)),
                        pltpu.SemaphoreType.DMA((n,))]

    def _copies(self, ins, outs, sems):
        send_sems, recv_sems, local_sems = sems
        x, y, c, chips = _place()
        me, sibling = (x, y, c), (x, y, 1 - c)

        def piece(t, px, py, pc):
            return outs[t].at[4 * px + 2 * py + pc]

        def own(t):
            return ins[t].at[pl.ds(c * self.shapes[t][0], self.shapes[t][0])] if self.halve else ins[t]

        def copy(t, k, block, to, src=None):
            return pltpu.make_async_remote_copy(
                src_ref=piece(t, *block) if src is None else src, dst_ref=piece(t, *block),
                send_sem=send_sems.at[7 * t + k], recv_sem=recv_sems.at[7 * t + k],
                device_id=to, device_id_type=MESH_ID)

        rng = range(self.n)
        return dict(
            mine=[pltpu.make_async_copy(own(t), piece(t, *me), local_sems.at[t]) for t in rng],
            first=[copy(t, 0, me, sibling, src=own(t)) for t in rng]
            + [copy(t, 1 + j, me, (*chip, c), src=own(t)) for t in rng for j, chip in enumerate(chips)],
            landed=[copy(t, 1 + j, (*chip, c), me) for j, chip in enumerate(chips) for t in rng],
            passed=[copy(t, 4 + j, (*chip, c), sibling) for j, chip in enumerate(chips) for t in rng],
            from_sibling=[copy(t, 0, sibling, me) for t in rng]
            + [copy(t, 4 + j, (*chip, 1 - c), me) for t in rng for j, chip in enumerate(chips)])

    def start(self, ins, outs, sems):
        d = self._copies(ins, outs, sems)
        for cp in d["mine"] + d["first"]:
            cp.start()

    def middle(self, ins, outs, sems):
        d = self._copies(ins, outs, sems)
        for got, fwd in zip(d["landed"], d["passed"]):
            got.wait_recv()
            fwd.start()

    def finish(self, ins, outs, sems):
        d = self._copies(ins, outs, sems)
        for cp in d["from_sibling"]:
            cp.wait_recv()
        for cp in d["first"] + d["passed"]:
            cp.wait_send()
        for cp in d["mine"]:
            cp.wait()


class _Scatter:
    def __init__(self, tensors):
        self.ins = list(tensors)
        self.n = n = len(tensors)
        self.out_shape = [jax.ShapeDtypeStruct(t.shape, t.dtype) for t in tensors]
        self.scratch = [pltpu.SemaphoreType.DMA((7 * n,)), pltpu.SemaphoreType.DMA((7 * n,))]

    def _copies(self, ins, outs, sems):
        send_sems, recv_sems = sems
        x, y, c, _ = _place()
        me = 4 * x + 2 * y + c
        sends, recvs = [], []
        for t in range(self.n):
            for m in range(1, N_DEV):
                px, py, pc = x ^ (m >> 2), y ^ ((m >> 1) & 1), c ^ (m & 1)
                q = 4 * px + 2 * py + pc
                sems_k = dict(send_sem=send_sems.at[7 * t + m - 1], recv_sem=recv_sems.at[7 * t + m - 1],
                              device_id=(px, py, pc), device_id_type=MESH_ID)
                sends.append(pltpu.make_async_remote_copy(src_ref=ins[t].at[q], dst_ref=outs[t].at[me], **sems_k))
                recvs.append(pltpu.make_async_remote_copy(src_ref=ins[t].at[me], dst_ref=outs[t].at[q], **sems_k))
        return sends, recvs

    def start(self, ins, outs, sems):
        for cp in self._copies(ins, outs, sems)[0]:
            cp.start()

    def middle(self, ins, outs, sems):
        pass

    def finish(self, ins, outs, sems):
        sends, recvs = self._copies(ins, outs, sems)
        for cp in recvs:
            cp.wait_recv()
        for cp in sends:
            cp.wait_send()


class _Comms:
    def __init__(self, *comms):
        self.comms = comms
        self.ins = [a for c in comms for a in c.ins]
        self.out_shape = [s for c in comms for s in c.out_shape]
        self.scratch = [s for c in comms for s in c.scratch]

    def _each(self, phase, ins, outs, sems):
        i = o = s = 0
        for c in self.comms:
            ni, no, ns = len(c.ins), len(c.out_shape), len(c.scratch)
            getattr(c, phase)(ins[i:i + ni], outs[o:o + no], sems[s:s + ns])
            i, o, s = i + ni, o + no, s + ns

    def start(self, ins, outs, sems):
        self._each("start", ins, outs, sems)

    def middle(self, ins, outs, sems):
        self._each("middle", ins, outs, sems)

    def finish(self, ins, outs, sems):
        self._each("finish", ins, outs, sems)


def _run_comm(comm, name):
    n = len(comm.ins)

    def body(*refs):
        parts = refs[:n], refs[n:2 * n], refs[2 * n:]
        comm.start(*parts)
        comm.middle(*parts)
        comm.finish(*parts)

    return pl.pallas_call(body, name=name, out_shape=comm.out_shape, in_specs=[ANY] * n, out_specs=[ANY] * n,
                          scratch_shapes=comm.scratch)(*comm.ins)


HBM_SPEC = pl.BlockSpec(memory_space=pltpu.HBM)
SEM_SPEC = pl.BlockSpec(memory_space=pltpu.SEMAPHORE)
DATAFLOW = pltpu.SideEffectType.DATAFLOW_SIDE_EFFECTING


def _scatter_copies(own_ref, land_ref, send_sems, recv_sems):
    x, y, c, _ = _place()
    me = 4 * x + 2 * y + c
    pairs = []
    for m in range(1, N_DEV):
        px, py, pc = x ^ (m >> 2), y ^ ((m >> 1) & 1), c ^ (m & 1)
        q = 4 * px + 2 * py + pc
        sems = dict(send_sem=send_sems.at[m - 1], recv_sem=recv_sems.at[m - 1], device_id=(px, py, pc),
                    device_id_type=MESH_ID)
        pairs.append((pltpu.make_async_remote_copy(src_ref=own_ref.at[q], dst_ref=land_ref.at[me], **sems),
                      pltpu.make_async_remote_copy(src_ref=own_ref.at[me], dst_ref=land_ref.at[q], **sems)))
    return pairs


def _scatter_start(own, name):
    def body(own_ref, land_ref, send_sems, recv_sems, own_thru, land_thru, token):
        for send, _ in _scatter_copies(own_ref, land_ref, send_sems, recv_sems):
            send.start()
        token[...] = jnp.zeros_like(token)

    buf = pltpu.HBM(own.shape, own.dtype)
    return pl.pallas_call(
        body, name=name,
        out_shape=(pltpu.SemaphoreType.DMA((N_DEV - 1,)), pltpu.SemaphoreType.DMA((N_DEV - 1,)), buf, buf,
                   jax.ShapeDtypeStruct((8, 128), F32)),
        in_specs=(HBM_SPEC, HBM_SPEC),
        out_specs=(SEM_SPEC, SEM_SPEC, HBM_SPEC, HBM_SPEC, pl.BlockSpec(memory_space=pltpu.VMEM)),
        input_output_aliases={0: 2, 1: 3},
        compiler_params=pltpu.CompilerParams(has_side_effects=DATAFLOW),
    )(pltpu.with_memory_space_constraint(own, pltpu.HBM),
      pltpu.with_memory_space_constraint(lax.empty(own.shape, own.dtype), pltpu.HBM))


def _scatter_wait(send_sems, recv_sems, own_thru, land_thru, after, name):
    def body(own_ref, land_ref, send_sems, recv_sems, after_ref, own_out, land_out):
        for send, recv in _scatter_copies(own_ref, land_ref, send_sems, recv_sems):
            send.wait_send()
            recv.wait_recv()

    buf = pltpu.HBM(own_thru.shape, own_thru.dtype)
    return pl.pallas_call(
        body, name=name, out_shape=(buf, buf),
        in_specs=(HBM_SPEC, HBM_SPEC, SEM_SPEC, SEM_SPEC, ANY), out_specs=(HBM_SPEC, HBM_SPEC),
        input_output_aliases={0: 0, 1: 1},
        compiler_params=pltpu.CompilerParams(has_side_effects=DATAFLOW),
    )(own_thru, land_thru, send_sems, recv_sems, after)


def _fused_call(body, comm, args, *, name, grid, out_shape, in_specs, out_specs, scratch_shapes=(), params):
    single = not isinstance(out_shape, (list, tuple))
    out_shape = [out_shape] if single else list(out_shape)
    out_specs = [out_specs] if single else list(out_specs)
    if comm is None:
        res = pl.pallas_call(body, name=name, grid=grid, out_shape=out_shape, in_specs=in_specs, out_specs=out_specs,
                             scratch_shapes=list(scratch_shapes), compiler_params=params)(*args)
        return (res[0] if single else res), []
    n_in, n_out, n_scr = len(in_specs), len(out_shape), len(scratch_shapes)
    c_in, c_out = len(comm.ins), len(comm.out_shape)
    steps = grid[0]

    def fused(*refs):
        pos = 0
        groups = []
        for size in (n_in, c_in, n_out, c_out, n_scr, len(comm.scratch)):
            groups.append(refs[pos:pos + size])
            pos += size
        ins, c_ins, outs, c_outs, scr, c_sems = groups
        i = pl.program_id(0)

        @pl.when(i == 0)
        def _():
            comm.start(c_ins, c_outs, c_sems)

        @pl.when(i == steps // 2)
        def _():
            comm.middle(c_ins, c_outs, c_sems)

        body(*ins, *outs, *scr)

        @pl.when(i == steps - 1)
        def _():
            comm.finish(c_ins, c_outs, c_sems)

    res = pl.pallas_call(
        fused, name=name, grid=grid, out_shape=out_shape + list(comm.out_shape),
        in_specs=list(in_specs) + [ANY] * c_in, out_specs=out_specs + [ANY] * c_out,
        scratch_shapes=list(scratch_shapes) + list(comm.scratch), compiler_params=params)(*args, *comm.ins)
    main = res[:n_out]
    return (main[0] if single else main), list(res[n_out:])


def _norm_matmul(x, gain, wt, name, comm=None, tm=1024):
    t, n = x.shape[0], wt.shape[0]

    def body(x_ref, g_ref, wt_ref, o_ref):
        xv = x_ref[...]
        r = lax.rsqrt(jnp.mean(xv * xv, axis=-1, keepdims=True) + EPS)
        h = (xv * r * g_ref[...]).astype(BF16)
        o_ref[...] = lax.dot_general(h, wt_ref[...], NT, preferred_element_type=F32)

    return _fused_call(
        body, comm, (x, gain, wt), name=name, grid=(t // tm,),
        out_shape=jax.ShapeDtypeStruct((t, n), F32),
        in_specs=[pl.BlockSpec((tm, D_MODEL), lambda i: (i, 0)), _const_spec((1, D_MODEL)), _const_spec((n, D_MODEL))],
        out_specs=pl.BlockSpec((tm, n), lambda i: (i, 0)),
        params=_params("arbitrary"))


def _project_out(a, w_ref, x_ref, p_ref, y_ref):
    y = jnp.dot(a, w_ref[...], preferred_element_type=F32)
    y_ref[...] = y
    ry = lax.rsqrt(jnp.mean(y * y, axis=-1, keepdims=True) + EPS)
    return x_ref[...] + (y * ry) * p_ref[...]


def _post_bwd_rows(g, y, a, n_own, first, last, p_ref, w_ref, dw_ref, dw16_ref, dp_ref):
    @pl.when(first)
    def _():
        dw_ref[...] = jnp.zeros_like(dw_ref)
        dp_ref[...] = jnp.zeros_like(dp_ref)

    ry = lax.rsqrt(jnp.mean(y * y, axis=-1, keepdims=True) + EPS)
    nv = y * ry
    dp_ref[...] += jnp.sum((g * nv)[0:n_own], axis=0, keepdims=True)
    dn = g * p_ref[...]
    dy = (ry * (dn - nv * jnp.mean(dn * nv, axis=-1, keepdims=True))).astype(BF16)
    dw_ref[...] += lax.dot_general(a, dy[0:n_own], TN, preferred_element_type=F32)

    @pl.when(last)
    def _():
        dw16_ref[...] = dw_ref[...].astype(BF16)

    return lax.dot_general(dy, w_ref[...], NT, preferred_element_type=F32)


def _pre_bwd(dproj, wt, x_in, pre, g, name, comm=None, tm=512):
    t, n = dproj.shape
    steps = t // tm

    def body(dp_ref, wt_ref, x_ref, pre_ref, g_ref, dx_ref, dwt16_ref, dpre_ref, dwt_ref):
        @pl.when(pl.program_id(0) == 0)
        def _():
            dwt_ref[...] = jnp.zeros_like(dwt_ref)
            dpre_ref[...] = jnp.zeros_like(dpre_ref)

        dpv = dp_ref[...]
        dh = jnp.dot(dpv, wt_ref[...], preferred_element_type=F32)
        xv = x_ref[...]
        r = lax.rsqrt(jnp.mean(xv * xv, axis=-1, keepdims=True) + EPS)
        xn = xv * r
        pv = pre_ref[...]
        dpre_ref[...] += jnp.sum(dh * xn, axis=0, keepdims=True)
        dxn = dh * pv
        dx_ref[...] = g_ref[...] + r * (dxn - xn * jnp.mean(dxn * xn, axis=-1, keepdims=True))
        h = (xn * pv).astype(BF16)
        dwt_ref[...] += lax.dot_general(dpv, h, TN, preferred_element_type=F32)

        @pl.when(pl.program_id(0) == steps - 1)
        def _():
            dwt16_ref[...] = dwt_ref[...].astype(BF16)

    row = pl.BlockSpec((tm, D_MODEL), lambda i: (i, 0))
    return _fused_call(
        body, comm, (dproj, wt, x_in, pre, g), name=name, grid=(steps,),
        out_shape=[jax.ShapeDtypeStruct((t, D_MODEL), F32), jax.ShapeDtypeStruct((n, D_MODEL), BF16),
                   jax.ShapeDtypeStruct((1, D_MODEL), F32)],
        in_specs=[pl.BlockSpec((tm, n), lambda i: (i, 0)), _const_spec((n, D_MODEL)), row, _const_spec((1, D_MODEL)), row],
        out_specs=[row, _const_spec((n, D_MODEL)), pl.BlockSpec((1, D_MODEL), lambda i: (0, 0))],
        scratch_shapes=[pltpu.VMEM((n, D_MODEL), F32)],
        params=_params("arbitrary"))


def _proj_dw(dproj, x_in, pre, name, comm=None, tm=1024):
    t, n = dproj.shape
    steps = t // tm

    def body(dp_ref, x_ref, pre_ref, dwt16_ref, dwt_ref):
        @pl.when(pl.program_id(0) == 0)
        def _():
            dwt_ref[...] = jnp.zeros_like(dwt_ref)

        xv = x_ref[...]
        r = lax.rsqrt(jnp.mean(xv * xv, axis=-1, keepdims=True) + EPS)
        h = (xv * r * pre_ref[...]).astype(BF16)
        dwt_ref[...] += lax.dot_general(dp_ref[...], h, TN, preferred_element_type=F32)

        @pl.when(pl.program_id(0) == steps - 1)
        def _():
            dwt16_ref[...] = dwt_ref[...].astype(BF16)

    return _fused_call(
        body, comm, (dproj, x_in, pre), name=name, grid=(steps,),
        out_shape=jax.ShapeDtypeStruct((n, D_MODEL), BF16),
        in_specs=[pl.BlockSpec((tm, n), lambda i: (i, 0)), pl.BlockSpec((tm, D_MODEL), lambda i: (i, 0)),
                  _const_spec((1, D_MODEL))],
        out_specs=pl.BlockSpec((n, D_MODEL), lambda i: (0, 0)),
        scratch_shapes=[pltpu.VMEM((n, D_MODEL), F32)],
        params=_params("arbitrary"))


def _proj_dx(dproj, wt, x_in, pre, g, name, comm=None, tm=512):
    t, n = dproj.shape

    def body(dp_ref, wt_ref, x_ref, pre_ref, g_ref, dx_ref, dpre_ref):
        @pl.when(pl.program_id(0) == 0)
        def _():
            dpre_ref[...] = jnp.zeros_like(dpre_ref)

        dh = jnp.dot(dp_ref[...], wt_ref[...], preferred_element_type=F32)
        xv = x_ref[...]
        r = lax.rsqrt(jnp.mean(xv * xv, axis=-1, keepdims=True) + EPS)
        xn = xv * r
        dpre_ref[...] += jnp.sum(dh * xn, axis=0, keepdims=True)
        dxn = dh * pre_ref[...]
        dx_ref[...] = g_ref[...] + r * (dxn - xn * jnp.mean(dxn * xn, axis=-1, keepdims=True))

    row = pl.BlockSpec((tm, D_MODEL), lambda i: (i, 0))
    return _fused_call(
        body, comm, (dproj, wt, x_in, pre, g), name=name, grid=(t // tm,),
        out_shape=[jax.ShapeDtypeStruct((t, D_MODEL), F32), jax.ShapeDtypeStruct((1, D_MODEL), F32)],
        in_specs=[pl.BlockSpec((tm, n), lambda i: (i, 0)), _const_spec((n, D_MODEL)), row, _const_spec((1, D_MODEL)), row],
        out_specs=[row, pl.BlockSpec((1, D_MODEL), lambda i: (0, 0))],
        params=_params("arbitrary"))


def _group_masks():
    lane = lax.broadcasted_iota(jnp.int32, (1, GROUP * HEAD_DIM), 1)
    return [(lane // HEAD_DIM == g).astype(F32) for g in range(GROUP)]


def _stack_groups(v, masks):
    return jnp.concatenate([v * m for m in masks], axis=0)


def _unstack_groups(v, masks):
    out = v[0:BLOCK] * masks[0]
    for g in range(1, GROUP):
        out = out + v[g * BLOCK:(g + 1) * BLOCK] * masks[g]
    return out


def _repeat_head(kv2, kvh):
    first = lax.broadcasted_iota(jnp.int32, kv2.shape, 1) < HEAD_DIM
    rolled = pltpu.roll(kv2, HEAD_DIM, 1)
    one = jnp.where(first, kv2, rolled) if kvh == 0 else jnp.where(first, rolled, kv2)
    return jnp.concatenate([one, one], axis=1)


def _fold_head(v4):
    a = v4[:, 0:128] + v4[:, 128:256]
    return a + pltpu.roll(a, HEAD_DIM, 1)


ATTN_CONSTS = [pltpu.VMEM((KV_HEADS, GROUP * BLOCK, 2 * BLOCK), F32)]


def _fill_attn_bias(bias_ref):
    row = lax.broadcasted_iota(jnp.int32, (GROUP * BLOCK, 2 * BLOCK), 0)
    col = lax.broadcasted_iota(jnp.int32, (GROUP * BLOCK, 2 * BLOCK), 1)
    dist = (row % BLOCK) + BLOCK - col
    band = (dist >= 0) & (dist < BLOCK)
    rb = lax.broadcasted_iota(jnp.int32, (GROUP * BLOCK, 1), 0) // BLOCK
    for kvh in range(KV_HEADS):
        slope = jnp.zeros((GROUP * BLOCK, 1), F32)
        for g in range(GROUP):
            slope = jnp.where(rb == g, 2.0 ** (-(kvh * GROUP + g + 1)), slope)
        bias_ref[kvh] = jnp.where(band, -slope * dist.astype(F32), NEG)


def _row_sinks(kvh, sink_ref):
    rb = lax.broadcasted_iota(jnp.int32, (GROUP * BLOCK, 1), 0) // BLOCK
    sink = jnp.zeros((GROUP * BLOCK, 1), F32)
    for g in range(GROUP):
        sink = jnp.where(rb == g, sink_ref[0, kvh * GROUP + g], sink)
    return sink


def _attn_probs(qk, k4, bias, sink, no_past, masks):
    qs = _stack_groups(qk, masks).astype(BF16)
    s = lax.dot_general(qs, k4, NT, preferred_element_type=F32) * (HEAD_DIM ** -0.5) + bias
    s = jnp.concatenate([jnp.where(no_past, NEG, s[:, 0:BLOCK]), s[:, BLOCK:]], axis=1)
    mx = jnp.maximum(jnp.max(s, axis=-1, keepdims=True), sink)
    e = jnp.exp(s - mx)
    es = jnp.exp(sink - mx)
    inv = 1.0 / (jnp.sum(e, axis=-1, keepdims=True) + es)
    return qs, e * inv, es * inv


def _pool_forward(u_ext, g, t0):
    n = u_ext.shape[0] - POOL_HALO
    s = u_ext
    for step in range(g + 1):
        s = s + pltpu.roll(s, 1 << step, 0)
    w = 2 << g
    t = t0 + lax.broadcasted_iota(jnp.int32, (n, 1), 0)
    cnt = jnp.minimum(t + 1, w).astype(F32)
    return s[POOL_HALO:] / cnt - u_ext[POOL_HALO:]


def _layer0_fwd(proj, sinks, pool_w, pool_scale, w_out, x_in, post, comm=None, tq=512):
    t = proj.shape[0]
    nblk = tq // BLOCK

    def body(main_ref, halo_ref, sink_ref, pw_ref, ps_ref, w_ref, x_ref, p_ref, o_ref, y_ref, xo_ref, kv_ref, bias_ref):
        i = pl.program_id(0)
        t0 = i * tq
        masks = _group_masks()

        @pl.when(i == 0)
        def _():
            _fill_attn_bias(bias_ref)

        kv_ref[0:BLOCK, :] = halo_ref[:, COL_K:COL_K + 256]
        kv_ref[BLOCK:, :] = main_ref[:, COL_K:COL_K + 256]

        def block(jb, carry):
            r0 = pl.multiple_of(jb * BLOCK, BLOCK)
            no_past = t0 + r0 == 0
            q = main_ref[pl.ds(r0, BLOCK), COL_Q:COL_Q + ATTN_WIDTH]
            ga = main_ref[pl.ds(r0, BLOCK), COL_GA:COL_GA + ATTN_WIDTH]
            kk = kv_ref[pl.ds(r0, 2 * BLOCK), 0:128]
            vv = kv_ref[pl.ds(r0, 2 * BLOCK), 128:256]
            outs = []
            for kvh in range(KV_HEADS):
                k4 = _repeat_head(kk, kvh).astype(BF16)
                v4 = _repeat_head(vv, kvh).astype(BF16)
                _, p, _ = _attn_probs(q[:, kvh * 256:(kvh + 1) * 256], k4, bias_ref[kvh], _row_sinks(kvh, sink_ref), no_past, masks)
                pv = jnp.dot(p.astype(BF16), v4, preferred_element_type=F32)
                outs.append(_unstack_groups(pv, masks))
            attn = jnp.concatenate(outs, axis=1)
            o_ref[pl.ds(r0, BLOCK), 0:ATTN_WIDTH] = (attn * _silu(ga)).astype(BF16)
            return carry

        lax.fori_loop(0, nblk, block, 0, unroll=True)

        for g in range(POOL_GROUPS):
            cu = COL_U + g * POOL_GC
            cg = COL_GB + g * POOL_GC
            halo_u = jnp.where(i == 0, 0.0, halo_ref[BLOCK - POOL_HALO:BLOCK, cu:cu + POOL_GC])
            u_ext = jnp.concatenate([halo_u, main_ref[:, cu:cu + POOL_GC]], axis=0)
            pooled = _pool_forward(u_ext, g, t0)
            y = jnp.dot(pooled.astype(BF16), pw_ref[g].astype(BF16), preferred_element_type=F32)
            y = y * ps_ref[:, g * POOL_GC:(g + 1) * POOL_GC]
            o_ref[:, ATTN_WIDTH + g * POOL_GC:ATTN_WIDTH + (g + 1) * POOL_GC] =(y * _silu(main_ref[:, cg:cg + POOL_GC])).astype(BF16)

        xo_ref[...] = _project_out(o_ref[...], w_ref, x_ref, p_ref, y_ref)

    row = pl.BlockSpec((tq, D_MODEL), lambda i: (i, 0))
    return _fused_call(
        body, comm, (proj, proj, sinks, pool_w, pool_scale, w_out, x_in, post), name="layer0_fwd", grid=(t // tq,),
        out_shape=[jax.ShapeDtypeStruct((t, D_MODEL), BF16), jax.ShapeDtypeStruct((t, D_MODEL), F32),
                   jax.ShapeDtypeStruct((t, D_MODEL), F32)],
        in_specs=[pl.BlockSpec((tq, EVEN_IN), lambda i: (i, 0)),
                  pl.BlockSpec((BLOCK, EVEN_IN), lambda i: (jnp.maximum(i * nblk - 1, 0), 0)),
                  pl.BlockSpec(memory_space=pltpu.SMEM),
                  _const_spec((POOL_GROUPS, POOL_GC, POOL_GC)), _const_spec((1, POOL_WIDTH)),
                  _const_spec((D_MODEL, D_MODEL)), row, _const_spec((1, D_MODEL))],
        out_specs=[row, row, row],
        scratch_shapes=[pltpu.VMEM((tq + BLOCK, 256), F32)] + ATTN_CONSTS,
        params=_params("arbitrary"))


def _layer0_bwd(proj, gy, y, mix, w_out, post, sinks, pool_w, pool_scale, comm=None, tq=512):
    t = proj.shape[0]
    nt = t // tq
    nblk = tq // BLOCK

    def body(main_ref, halo_ref, next_ref, gy_ref, gyn_ref, y_ref, yn_ref, mix_ref, wo_ref, po_ref,
             sink_ref, pw_ref, ps_ref,
             o_ref, dsk_ref, dpw_ref, dps_ref, dwo16_ref, dpo_ref,
             kv_ref, dkv_ref, carry_ref, bias_ref, dwo_ref, dmix_ref):
        i = pl.program_id(0)
        ii = nt - 1 - i
        t0 = ii * tq
        masks = _group_masks()

        @pl.when(i == 0)
        def _():
            _fill_attn_bias(bias_ref)
            dsk_ref[...] = jnp.zeros_like(dsk_ref)
            dpw_ref[...] = jnp.zeros_like(dpw_ref)
            dps_ref[...] = jnp.zeros_like(dps_ref)
            carry_ref[...] = jnp.zeros_like(carry_ref)

        dmix_ref[...] = _post_bwd_rows(jnp.concatenate([gy_ref[...], gyn_ref[...]], axis=0),
                                       jnp.concatenate([y_ref[...], yn_ref[...]], axis=0), mix_ref[...], tq,
                                       i == 0, i == nt - 1, po_ref, wo_ref, dwo_ref, dwo16_ref, dpo_ref)
        dm_ref = dmix_ref.at[pl.ds(0, tq)]
        dmn_ref = dmix_ref.at[pl.ds(tq, POOL_HALO)]

        kv_ref[0:BLOCK, :] = halo_ref[:, COL_K:COL_K + 256]
        kv_ref[BLOCK:, :] = main_ref[:, COL_K:COL_K + 256]
        dkv_ref[0:tq, :] = jnp.zeros((tq, 256), F32)
        dkv_ref[tq:, :] = carry_ref[...]

        def block(jb, carry):
            r0 = pl.multiple_of(jb * BLOCK, BLOCK)
            no_past = t0 + r0 == 0
            q = main_ref[pl.ds(r0, BLOCK), COL_Q:COL_Q + ATTN_WIDTH]
            ga = main_ref[pl.ds(r0, BLOCK), COL_GA:COL_GA + ATTN_WIDTH]
            dya = dm_ref[pl.ds(r0, BLOCK), 0:ATTN_WIDTH]
            kk = kv_ref[pl.ds(r0, 2 * BLOCK), 0:128]
            vv = kv_ref[pl.ds(r0, 2 * BLOCK), 128:256]
            silu_ga, dsilu_ga = _silu_and_grad(ga)
            do = dya * silu_ga
            first = lax.broadcasted_iota(jnp.int32, (2 * BLOCK, 128), 1) < HEAD_DIM
            attn, dq, dk, dv = [], [], [], []
            for kvh in range(KV_HEADS):
                k4 = _repeat_head(kk, kvh).astype(BF16)
                v4 = _repeat_head(vv, kvh).astype(BF16)
                qs, p, ps = _attn_probs(q[:, kvh * 256:(kvh + 1) * 256], k4, bias_ref[kvh], _row_sinks(kvh, sink_ref), no_past, masks)
                pb = p.astype(BF16)
                o_k = _unstack_groups(jnp.dot(pb, v4, preferred_element_type=F32), masks)
                do_k = do[:, kvh * 256:(kvh + 1) * 256]
                dos = _stack_groups(do_k, masks).astype(BF16)
                prod = do_k * o_k
                delta = jnp.concatenate([jnp.sum(prod * m, axis=-1, keepdims=True) for m in masks], axis=0)
                dp = lax.dot_general(dos, v4, NT, preferred_element_type=F32)
                ds = (p * (dp - delta)).astype(BF16)
                sink_term = ps * delta
                for g in range(GROUP):
                    h = kvh * GROUP + g
                    dsk_ref[h:h + 1, :] -= jnp.sum(sink_term[g * BLOCK:(g + 1) * BLOCK], keepdims=True)
                scale = HEAD_DIM ** -0.5
                dq.append(_unstack_groups(jnp.dot(ds, k4, preferred_element_type=F32), masks) * scale)
                dk.append(_fold_head(lax.dot_general(ds, qs, TN, preferred_element_type=F32)) * scale)
                dv.append(_fold_head(lax.dot_general(pb, dos, TN, preferred_element_type=F32)))
                attn.append(o_k)
            o_ref[pl.ds(r0, BLOCK), COL_Q:COL_Q + ATTN_WIDTH] = jnp.concatenate(dq, axis=1).astype(BF16)
            o_all = jnp.concatenate(attn, axis=1)
            o_ref[pl.ds(r0, BLOCK), COL_GA:COL_GA + ATTN_WIDTH] = (dya * o_all * dsilu_ga).astype(BF16)
            dkv = jnp.concatenate([jnp.where(first, dk[0], dk[1]), jnp.where(first, dv[0], dv[1])], axis=1)
            dkv_ref[pl.ds(r0, 2 * BLOCK), :] += dkv
            return carry

        lax.fori_loop(0, nblk, block, 0, unroll=True)
        carry_ref[...] = dkv_ref[0:BLOCK, :]
        o_ref[:, COL_K:COL_K + 256] = dkv_ref[BLOCK:, :].astype(BF16)

        last = ii == nt - 1
        for g in range(POOL_GROUPS):
            cu = COL_U + g * POOL_GC
            cg = COL_GB + g * POOL_GC
            cm = ATTN_WIDTH + g * POOL_GC
            pw = pw_ref[g].astype(BF16)
            sc = ps_ref[:, g * POOL_GC:(g + 1) * POOL_GC]
            halo_u = jnp.where(ii == 0, 0.0, halo_ref[BLOCK - POOL_HALO:BLOCK, cu:cu + POOL_GC])
            u_ext = jnp.concatenate([halo_u, main_ref[:, cu:cu + POOL_GC]], axis=0)
            pooled = _pool_forward(u_ext, g, t0).astype(BF16)
            y_raw = jnp.dot(pooled, pw, preferred_element_type=F32)
            gb = main_ref[:, cg:cg + POOL_GC]
            dyb = dm_ref[:, cm:cm + POOL_GC]
            silu_gb, dsilu_gb = _silu_and_grad(gb)
            dypool = dyb * silu_gb
            dps_ref[:, g * POOL_GC:(g + 1) * POOL_GC] += jnp.sum(dypool * y_raw, axis=0, keepdims=True)
            o_ref[:, cg:cg + POOL_GC] = (dyb * (y_raw * sc) * dsilu_gb).astype(BF16)
            dyraw = dypool * sc
            dyraw_n = jnp.where(last, 0.0, dmn_ref[:, cm:cm + POOL_GC] * _silu(next_ref[:, cg:cg + POOL_GC]) * sc)
            dpw_ref[g * POOL_GC:(g + 1) * POOL_GC, :] += lax.dot_general(pooled, dyraw.astype(BF16), TN,
                                                                         preferred_element_type=F32)
            dyraw_ext = jnp.concatenate([dyraw, dyraw_n], axis=0).astype(BF16)
            dpooled = lax.dot_general(dyraw_ext, pw, NT, preferred_element_type=F32)
            w = 2 << g
            tt = t0 + lax.broadcasted_iota(jnp.int32, (tq + POOL_HALO, 1), 0)
            s = dpooled / jnp.minimum(tt + 1, w).astype(F32)
            for step in range(g + 1):
                s = s + pltpu.roll(s, tq + POOL_HALO - (1 << step), 0)
            o_ref[:, cu:cu + POOL_GC] = (s[0:tq] - dpooled[0:tq]).astype(BF16)

    rev = lambda i: nt - 1 - i
    nxt = lambda i: (jnp.minimum((rev(i) + 1) * (tq // POOL_HALO), t // POOL_HALO - 1), 0)
    row = pl.BlockSpec((tq, D_MODEL), lambda i: (rev(i), 0))
    nxt_row = pl.BlockSpec((POOL_HALO, D_MODEL), nxt)
    square = _const_spec((D_MODEL, D_MODEL))
    return _fused_call(
        body, comm, (proj, proj, proj, gy, gy, y, y, mix, w_out, post, sinks, pool_w, pool_scale),
        name="layer0_bwd", grid=(nt,),
        out_shape=[jax.ShapeDtypeStruct((t, EVEN_IN), BF16), jax.ShapeDtypeStruct((8, 128), F32),
                   jax.ShapeDtypeStruct((POOL_GROUPS * POOL_GC, POOL_GC), F32), jax.ShapeDtypeStruct((1, POOL_WIDTH), F32),
                   jax.ShapeDtypeStruct((D_MODEL, D_MODEL), BF16), jax.ShapeDtypeStruct((1, D_MODEL), F32)],
        in_specs=[pl.BlockSpec((tq, EVEN_IN), lambda i: (rev(i), 0)),
                  pl.BlockSpec((BLOCK, EVEN_IN), lambda i: (jnp.maximum(rev(i) * nblk - 1, 0), 0)),
                  pl.BlockSpec((POOL_HALO, EVEN_IN), nxt),
                  row, nxt_row, row, nxt_row, row, square, _const_spec((1, D_MODEL)),
                  pl.BlockSpec(memory_space=pltpu.SMEM),
                  _const_spec((POOL_GROUPS, POOL_GC, POOL_GC)), _const_spec((1, POOL_WIDTH))],
        out_specs=[pl.BlockSpec((tq, EVEN_IN), lambda i: (rev(i), 0)),
                   pl.BlockSpec((8, 128), lambda i: (0, 0)),
                   pl.BlockSpec((POOL_GROUPS * POOL_GC, POOL_GC), lambda i: (0, 0)),
                   pl.BlockSpec((1, POOL_WIDTH), lambda i: (0, 0)), square, pl.BlockSpec((1, D_MODEL), lambda i: (0, 0))],
        scratch_shapes=[pltpu.VMEM((tq + BLOCK, 256), F32), pltpu.VMEM((tq + BLOCK, 256), F32),
                        pltpu.VMEM((BLOCK, 256), F32)] + ATTN_CONSTS
        + [pltpu.VMEM((D_MODEL, D_MODEL), F32), pltpu.VMEM((tq + POOL_HALO, D_MODEL), F32)],
        params=_params("arbitrary"))


CONV_RC = 32
CONV_CC = 128
CONV_CHAINS = 4
CONV_UNROLL = 2


def _fill_shifted(s_ref, rows):
    for b in range(1, 8):
        s_ref[b, 0:rows - 8, :] = s_ref[0, b:b + rows - 8, :]


def _tap_blocks(s_ref, r, cols, lead):
    for b in range(8):
        taps = [(a, 8 * a + b - lead) for a in range(5) if 0 <= 8 * a + b - lead < CONV_K]
        span = 8 * max(a for a, _ in taps) + CONV_RC
        blk = s_ref[b, pl.ds(r, span), cols]
        for a, k in taps:
            yield k, blk[8 * a:8 * a + CONV_RC]


def _conv_taps(s_ref, w_ref, r, cols, lead, reverse):
    accs = [None] * CONV_CHAINS
    for n, (k, blk) in enumerate(_tap_blocks(s_ref, r, cols, lead)):
        kw = CONV_K - 1 - k if reverse else k
        term = blk * w_ref[kw:kw + 1, cols]
        accs[n % CONV_CHAINS] = term if accs[n % CONV_CHAINS] is None else accs[n % CONV_CHAINS] + term
    while len(accs) > 1:
        accs = [a + b for a, b in zip(accs[0::2], accs[1::2])]
    return accs[0]


def _layer_norm_fwd(cf, lng, lnb):
    mu = jnp.mean(cf, axis=-1, keepdims=True)
    xc = cf - mu
    rstd = lax.rsqrt(jnp.mean(xc * xc, axis=-1, keepdims=True) + EPS)
    chat = xc * rstd
    return chat, rstd, chat * lng + lnb


def _layer1_fwd(proj, dw, dwb, lng, lnb, w_out, x_in, post, target, tt=256):
    t = proj.shape[0]
    lead = CONV_HALO - (CONV_K - 1)

    def body(main_ref, halo_ref, w_ref, b_ref, g_ref, lb_ref, wo_ref, x_ref, p_ref, t_ref,
             o_ref, c_ref, y_ref, dl_ref, l_ref, gs_ref):
        i = pl.program_id(0)
        hv = halo_ref[...]
        gs_ref[0, 0:CONV_HALO, :] = jnp.where(i == 0, 0.0, hv[:, GLU_A] * _sigmoid(hv[:, GLU_B]))
        gs_ref[0, CONV_HALO:CONV_HALO + tt, :] = main_ref[:, GLU_A] * _sigmoid(main_ref[:, GLU_B])
        _fill_shifted(gs_ref, tt + CONV_HALO)

        for c in range(D_MODEL // CONV_CC):
            cols = slice(c * CONV_CC, (c + 1) * CONV_CC)

            def chunk(j, carry):
                r = pl.multiple_of(j * CONV_RC, CONV_RC)
                c_ref[pl.ds(r, CONV_RC), cols] = _conv_taps(gs_ref, w_ref, r, cols, lead, False) + b_ref[:, cols]
                return carry
            lax.fori_loop(0, tt // CONV_RC, chunk, 0, unroll=CONV_UNROLL)

        _, _, cn = _layer_norm_fwd(c_ref[...], g_ref[...], lb_ref[...])
        o_ref[...] = (_silu(cn) * _silu(main_ref[:, GATE])).astype(BF16)

        d = _project_out(o_ref[...], wo_ref, x_ref, p_ref, y_ref) - t_ref[...]
        dl_ref[...] = d * (1.0 / D_MODEL)

        @pl.when(i == 0)
        def _():
            l_ref[...] = jnp.zeros_like(l_ref)

        l_ref[...] += 0.5 * jnp.sum(jnp.mean(d * d, axis=-1, keepdims=True))

    vec = _const_spec((1, D_MODEL))
    row = pl.BlockSpec((tt, D_MODEL), lambda i: (i, 0))
    f32_rows = jax.ShapeDtypeStruct((t, D_MODEL), F32)
    return pl.pallas_call(
        body, name="layer1_fwd", grid=(t // tt,),
        out_shape=[jax.ShapeDtypeStruct((t, D_MODEL), BF16), f32_rows, f32_rows, f32_rows,
                   jax.ShapeDtypeStruct((8, 128), F32)],
        in_specs=[pl.BlockSpec((tt, 3 * D_MODEL), lambda i: (i, 0)),
                  pl.BlockSpec((CONV_HALO, 3 * D_MODEL), lambda i: (jnp.maximum(i * (tt // CONV_HALO) - 1, 0), 0)),
                  _const_spec((CONV_K, D_MODEL)), vec, vec, vec,
                  _const_spec((D_MODEL, D_MODEL)), row, vec, row],
        out_specs=[row, row, row, row, pl.BlockSpec((8, 128), lambda i: (0, 0))],
        scratch_shapes=[pltpu.VMEM((8, tt + CONV_HALO, D_MODEL), F32)],
        compiler_params=_params("arbitrary"),
    )(proj, proj, dw, dwb, lng, lnb, w_out, x_in, post, target)


def _layer1_bwd(proj, cf, gy, y, z, w_out, post, dw, lng, lnb, comm=None, tt=256):
    t = proj.shape[0]
    nt = t // tt
    te = tt + CONV_HALO

    def body(main_ref, next_ref, cf_ref, cfn_ref, gy_ref, gyn_ref, y_ref, yn_ref, z_ref, wo_ref, po_ref,
             w_ref, g_ref, lb_ref,
             o_ref, ddw_ref, ddb_ref, dg_ref, dlb_ref, dwo16_ref, dpo_ref, ds_ref, glu_ref, sb_ref, dwo_ref):
        i = pl.program_id(0)

        @pl.when(i == 0)
        def _():
            ddw_ref[...] = jnp.zeros_like(ddw_ref)
            ddb_ref[...] = jnp.zeros_like(ddb_ref)
            dg_ref[...] = jnp.zeros_like(dg_ref)
            dlb_ref[...] = jnp.zeros_like(dlb_ref)

        dzv = _post_bwd_rows(jnp.concatenate([gy_ref[...], gyn_ref[...]], axis=0),
                             jnp.concatenate([y_ref[...], yn_ref[...]], axis=0), z_ref[...], tt, i == 0, i == nt - 1,
                             po_ref, wo_ref, dwo_ref, dwo16_ref, dpo_ref)
        dzv = jnp.concatenate([dzv[0:tt], jnp.where(i < nt - 1, dzv[tt:], 0.0)], axis=0)
        lng = g_ref[...]
        chat, rstd, cn = _layer_norm_fwd(jnp.concatenate([cf_ref[...], cfn_ref[...]], axis=0), lng, lb_ref[...])
        gate = jnp.concatenate([main_ref[:, GATE], next_ref[:, GATE]], axis=0)
        silu_cn, dsilu_cn = _silu_and_grad(cn)
        silu_gate, dsilu_gate = _silu_and_grad(gate)
        o_ref[:, GATE] = (dzv * silu_cn * dsilu_gate)[0:tt].astype(BF16)
        dcn = dzv * silu_gate * dsilu_cn
        dg_ref[...] += jnp.sum((dcn * chat)[0:tt], axis=0, keepdims=True)
        dlb_ref[...] += jnp.sum(dcn[0:tt], axis=0, keepdims=True)
        dchat = dcn * lng
        dcf = rstd * (dchat - jnp.mean(dchat, axis=-1, keepdims=True) - chat * jnp.mean(dchat * chat, axis=-1, keepdims=True))
        ddb_ref[...] += jnp.sum(dcf[0:tt], axis=0, keepdims=True)
        ds_ref[0, 0:te, :] = dcf
        ds_ref[0, te:, :] = jnp.zeros((8, D_MODEL), F32)
        _fill_shifted(ds_ref, te + 8)
        sb_ref[...] = _sigmoid(main_ref[:, GLU_B])
        glu_ref[...] = main_ref[:, GLU_A] * sb_ref[...]

        for c in range(D_MODEL // CONV_CC):
            cols = slice(c * CONV_CC, (c + 1) * CONV_CC)
            gcols = slice(D_MODEL + c * CONV_CC, D_MODEL + (c + 1) * CONV_CC)

            def chunk(j, carry):
                r = pl.multiple_of(j * CONV_RC, CONV_RC)
                dglu = _conv_taps(ds_ref, w_ref, r, cols, 0, True)
                sb = sb_ref[pl.ds(r, CONV_RC), cols]
                o_ref[pl.ds(r, CONV_RC), cols] = (dglu * sb).astype(BF16)
                o_ref[pl.ds(r, CONV_RC), gcols] = (dglu * glu_ref[pl.ds(r, CONV_RC), cols] * (1.0 - sb)).astype(BF16)
                return carry
            lax.fori_loop(0, tt // CONV_RC, chunk, 0, unroll=CONV_UNROLL)

            def taps(j, accs):
                r = pl.multiple_of(j * CONV_RC, CONV_RC)
                gl = glu_ref[pl.ds(r, CONV_RC), cols]
                new = list(accs)
                for m, blk in _tap_blocks(ds_ref, r, cols, 0):
                    prod = blk * gl
                    part = prod[0:8]
                    for q in range(1, CONV_RC // 8):
                        part = part + prod[8 * q:8 * q + 8]
                    new[m] = new[m] + part
                return tuple(new)
            accs = lax.fori_loop(0, tt // CONV_RC, taps, tuple(jnp.zeros((8, CONV_CC), F32) for _ in range(CONV_K)))
            for m in range(CONV_K):
                k = CONV_K - 1 - m
                ddw_ref[k:k + 1, cols] += jnp.sum(accs[m], axis=0, keepdims=True)

    vec = _const_spec((1, D_MODEL))
    vec_out = pl.BlockSpec((1, D_MODEL), lambda i: (0, 0))
    row = pl.BlockSpec((tt, D_MODEL), lambda i: (i, 0))
    nxt = lambda i: (jnp.minimum((i + 1) * (tt // CONV_HALO), t // CONV_HALO - 1), 0)
    nxt_row = pl.BlockSpec((CONV_HALO, D_MODEL), nxt)
    vec_f32 = jax.ShapeDtypeStruct((1, D_MODEL), F32)
    square = _const_spec((D_MODEL, D_MODEL))
    return _fused_call(
        body, comm, (proj, proj, cf, cf, gy, gy, y, y, z, w_out, post, dw, lng, lnb), name="layer1_bwd", grid=(nt,),
        out_shape=[jax.ShapeDtypeStruct((t, 3 * D_MODEL), BF16), jax.ShapeDtypeStruct((CONV_K, D_MODEL), F32),
                   vec_f32, vec_f32, vec_f32, jax.ShapeDtypeStruct((D_MODEL, D_MODEL), BF16), vec_f32],
        in_specs=[pl.BlockSpec((tt, 3 * D_MODEL), lambda i: (i, 0)),
                  pl.BlockSpec((CONV_HALO, 3 * D_MODEL), nxt),
                  row, nxt_row, row, nxt_row, row, nxt_row, row, square, vec,
                  _const_spec((CONV_K, D_MODEL)), vec, vec],
        out_specs=[pl.BlockSpec((tt, 3 * D_MODEL), lambda i: (i, 0)),
                   pl.BlockSpec((CONV_K, D_MODEL), lambda i: (0, 0)), vec_out, vec_out, vec_out, square, vec_out],
        scratch_shapes=[pltpu.VMEM((8, te + 8, D_MODEL), F32), pltpu.VMEM((tt, D_MODEL), F32),
                        pltpu.VMEM((tt, D_MODEL), F32), pltpu.VMEM((D_MODEL, D_MODEL), F32)],
        params=_params("arbitrary"))


def _piece_sum(parts, place, name):
    r = parts[0][0].shape[1]

    def body(p_ref, *refs):
        o_ref = refs[-1]
        acc = refs[0][0].astype(F32)
        for part in refs[1:-1]:
            acc = acc + part[0].astype(F32)
        o_ref[0] = acc

    blk = (1, r, D_MODEL)
    spec = lambda slot: pl.BlockSpec(blk, lambda j, p_ref: (slot(p_ref), 0, 0))
    return pl.pallas_call(
        body, name=name,
        grid_spec=pltpu.PrefetchScalarGridSpec(
            num_scalar_prefetch=1, grid=(1,),
            in_specs=[spec(slot) for _, slot in parts],
            out_specs=pl.BlockSpec(blk, lambda j, p_ref: (p_ref[1], 0, 0))),
        out_shape=jax.ShapeDtypeStruct((2, r, D_MODEL), F32),
        compiler_params=_params("arbitrary"),
    )(place, *[a for a, _ in parts])


def _direct_parts(own, recv):
    peer = lambda m: (lambda p: p[0] ^ m)
    return [(own, peer(0))] + [(recv, peer(m)) for m in range(1, N_DEV)]


def _share_with_sibling(halves, name):
    n = len(halves)

    def body(*refs):
        outs = refs[n:2 * n]
        send_sems, recv_sems = refs[2 * n:]
        x, y, c, _ = _place()
        send = [pltpu.make_async_remote_copy(
            src_ref=outs[t].at[c], dst_ref=outs[t].at[c], send_sem=send_sems.at[t], recv_sem=recv_sems.at[t],
            device_id=(x, y, 1 - c), device_id_type=MESH_ID) for t in range(n)]
        recv = [pltpu.make_async_remote_copy(
            src_ref=outs[t].at[c], dst_ref=outs[t].at[1 - c], send_sem=send_sems.at[t], recv_sem=recv_sems.at[t],
            device_id=(x, y, 1 - c), device_id_type=MESH_ID) for t in range(n)]
        for cp in send:
            cp.start()
        for cp in recv:
            cp.wait_recv()
        for cp in send:
            cp.wait_send()

    return pl.pallas_call(
        body, name=name,
        out_shape=[jax.ShapeDtypeStruct(h.shape, h.dtype) for h in halves],
        in_specs=[ANY] * n, out_specs=[ANY] * n,
        input_output_aliases={t: t for t in range(n)},
        scratch_shapes=[pltpu.SemaphoreType.DMA((n,)), pltpu.SemaphoreType.DMA((n,))],
    )(*halves)


def _sum8(parts, name):
    r = parts.shape[1]

    def body(p_ref, o_ref):
        acc = p_ref[0]
        for k in range(1, N_DEV):
            acc = acc + p_ref[k]
        o_ref[...] = acc

    return pl.pallas_call(
        body, name=name, out_shape=jax.ShapeDtypeStruct((r, 128), F32),
        in_specs=[pl.BlockSpec(memory_space=pltpu.VMEM)], out_specs=pl.BlockSpec(memory_space=pltpu.VMEM),
    )(parts)


def _adamw(w, g, m, v, name):
    shape = w.shape
    cols = shape[-1]
    rows = w.size // cols
    rt = 256 if rows % 256 == 0 else rows
    if w.ndim >= 3 and shape[-2] == 1:
        view, blk, at = (rows, 1, cols), (rows, 1, cols), (lambda i: (0, 0, 0))
    else:
        view, blk, at = (rows, cols), (rt, cols), (lambda i: (i, 0))

    def body(w_ref, g_ref, m_ref, v_ref, d_ref, nm_ref, nv_ref):
        gv = g_ref[...]
        mn = ADAM_B1 * m_ref[...] + (1.0 - ADAM_B1) * gv
        vn = ADAM_B2 * v_ref[...] + (1.0 - ADAM_B2) * (gv * gv)
        m_hat = mn / (1.0 - ADAM_B1 ** ADAM_STEP)
        v_hat = vn / (1.0 - ADAM_B2 ** ADAM_STEP)
        d_ref[...] = -ADAM_LR * (m_hat / (jnp.sqrt(v_hat) + ADAM_EPS) + ADAM_WD * w_ref[...])
        nm_ref[...] = mn
        nv_ref[...] = vn

    spec = pl.BlockSpec(blk, at)
    outs = pl.pallas_call(
        body, name=name, grid=(view[0] // blk[0],),
        out_shape=[jax.ShapeDtypeStruct(view, F32)] * 3,
        in_specs=[spec] * 4, out_specs=[spec] * 3,
        compiler_params=_params("parallel"),
    )(*[a.reshape(view) for a in (w, g, m, v)])
    return [o.reshape(shape) for o in outs]


SMALL_ROWS = 832


def _pack_small(g):
    parts = [g["loss"], g["pre1"].reshape(8, 128), g["post0"].reshape(8, 128),
             g["post1"].reshape(8, 128), g["sinks"], jnp.pad(g["pool_scale"].reshape(4, 128), ((0, 4), (0, 0))),
             g["pool_w"], g["dw"].reshape(248, 128), g["dwb"].reshape(8, 128), g["lng"].reshape(8, 128),
             g["lnb"].reshape(8, 128)]
    assert sum(p.shape[0] for p in parts) == SMALL_ROWS
    return jnp.concatenate(parts, axis=0)


def _unpack_small(s):
    out, r = {}, 0
    for key, rows, shape in (("loss", 8, (8, 128)), ("pre1", 8, (1, D_MODEL)), ("post", 16, (2, D_MODEL)),
                             ("sinks", 8, (8, 128)),
                             ("pool_scale", 4, (1, POOL_WIDTH)), ("pad", 4, (4, 128)), ("pool_w", 512, (1, 4, 128, 128)),
                             ("dw", 248, (CONV_K, D_MODEL)), ("dwb", 8, (1, D_MODEL)), ("lng", 8, (1, D_MODEL)),
                             ("lnb", 8, (1, D_MODEL))):
        out[key] = s[r:r + rows].reshape(shape)
        r += rows
    return out


def kernel(x, pre_norm, post_norm, a_w_in, a_sinks, b_pool_w, b_pool_scale, ab_w_out, c_w_in, c_dw_w, c_dw_b, c_ln_g, c_ln_b, c_w_out, loss_target, m_pre_norm, m_post_norm, m_a_w_in, m_a_sinks, m_b_pool_w, m_b_pool_scale, m_ab_w_out, m_c_w_in, m_c_dw_w, m_c_dw_b, m_c_ln_g, m_c_ln_b, m_c_w_out, v_pre_norm, v_post_norm, v_a_w_in, v_a_sinks, v_b_pool_w, v_b_pool_scale, v_ab_w_out, v_c_w_in, v_c_dw_w, v_c_dw_b, v_c_ln_g, v_c_ln_b, v_c_w_out):
    ix, iy = lax.axis_index("x"), lax.axis_index("y")
    chip_cols = (2 * ix + iy) * 256

    pad8 = lambda v: jnp.pad(v, ((0, -v.shape[0] % 8), (0, 0)))
    vec_shard = jnp.concatenate([pad8(c_dw_w.reshape(CONV_K, 256)), pad8(c_dw_b), pad8(c_ln_g), pad8(c_ln_b),
                                 jnp.zeros((8, 256), F32)], axis=0)
    x0, target = x[0], loss_target[0]
    pre0, pre1 = pre_norm[0:1], pre_norm[1:2]
    post0, post1 = post_norm[0:1], post_norm[1:2]
    pool_w = b_pool_w[0]

    (wa_t,) = _run_comm(_Gather([a_w_in[0].T.astype(BF16)], halve=True), "gather_a_w_in")
    wa_t = wa_t.reshape(EVEN_IN, D_MODEL)
    proj0, (w_ab,) = _norm_matmul(x0, pre0, wa_t, "proj0_fwd", comm=_Gather([ab_w_out[0].astype(BF16)], halve=True))
    w_ab = w_ab.reshape(D_MODEL, D_MODEL)
    (mix0, y0, x1), (wc_t, w_c, vecs) = _layer0_fwd(
        proj0, a_sinks, pool_w, b_pool_scale, w_ab, x0, post0,
        comm=_Gather([c_w_in[0].T.astype(BF16), c_w_out[0].astype(BF16), vec_shard], halve=True))
    wc_t = wc_t.reshape(3 * D_MODEL, D_MODEL)
    w_c = w_c.reshape(D_MODEL, D_MODEL)
    vecs = vecs.reshape(4, 64, 256).transpose(1, 0, 2).reshape(64, D_MODEL)
    dw, dwb, lng, lnb = vecs[0:CONV_K], vecs[32:33], vecs[40:41], vecs[48:49]
    proj1, _ = _norm_matmul(x1, pre1, wc_t, "proj1_fwd")
    z1, cf1, y1, g2, loss = _layer1_fwd(proj1, dw, dwb, lng, lnb, w_c, x1, post1, target)

    pieces = lambda m: m.reshape(N_DEV, m.shape[0] // N_DEV, D_MODEL)
    (dproj1, d_dw, d_dwb, d_lng, d_lnb, d_wc, d_post1), _ = _layer1_bwd(proj1, cf1, g2, y1, z1, w_c, post1, dw, lng, lnb)
    (g1, d_wct, d_pre1), (r_wc,) = _pre_bwd(dproj1, wc_t, x1, pre1, g2, "proj1_bwd", comm=_Scatter([pieces(d_wc)]))
    (dproj0, d_sinks, d_pw, d_ps, d_wab, d_post0), (r_wct,) = _layer0_bwd(
        proj0, g1, y0, mix0, w_ab, post0, a_sinks, pool_w, b_pool_scale, comm=_Scatter([pieces(d_wct)]))
    g = dict(loss=loss, pre1=d_pre1, post0=d_post0, post1=d_post1, sinks=d_sinks, pool_w=d_pw, pool_scale=d_ps,
             dw=d_dw, dwb=d_dwb, lng=d_lng, lnb=d_lnb)
    d_wat, (small8, r_wab) = _proj_dw(dproj0, x0, pre0, "proj0_dw",
                                      comm=_Comms(_Gather([_pack_small(g)], halve=False), _Scatter([pieces(d_wab)])))
    sent = _scatter_start(pieces(d_wat), "scatter_a_start")
    (gx, d_pre0), _ = _proj_dx(dproj0, wa_t, x0, pre0 + sent[4][0:1, 0:1], g1, "proj0_dx")
    sent_pre0 = _scatter_start(jnp.broadcast_to(d_pre0.reshape(1, 8, 128), (N_DEV, 8, 128)), "pre0_start")
    own_wat, r_wat = _scatter_wait(*sent[:4], d_pre0, "scatter_a_wait")

    ic = lax.axis_index("c")
    me = 4 * ix + 2 * iy + ic
    place = jnp.stack([me, ic]).astype(jnp.int32)
    parts = [_direct_parts(own_wat, r_wat), _direct_parts(pieces(d_wab), r_wab), _direct_parts(pieces(d_wct), r_wct),
             _direct_parts(pieces(d_wc), r_wc)]
    halves = [_piece_sum(p, place, f"grad_sum{t}") for t, p in enumerate(parts)]
    g_wa_t, g_wab, g_wc_t, g_wc = [h.reshape(2 * h.shape[1], D_MODEL) for h in _share_with_sibling(halves, "grad_share")]
    _, pre0_8 = _scatter_wait(*sent_pre0[:4], g_wc, "pre0_wait")
    pre0_8 = lax.dynamic_update_slice(pre0_8, d_pre0.reshape(1, 8, 128), (me, 0, 0))
    g_c_w_in = g_wc_t.T[None]
    g_ab_w_out = g_wab[None]
    g_c_w_out = g_wc[None]

    s = _unpack_small(_sum8(small8, "small_sum"))
    layer = lax.broadcasted_iota(jnp.int32, (2, D_MODEL), 0)
    g_pre = jnp.where(layer == 0, _sum8(pre0_8, "pre0_sum").reshape(1, D_MODEL), s["pre1"])
    g_post = s["post"]
    g_sinks = s["sinks"][:, 0].reshape(1, 8)
    g_pool_w, g_pool_scale = s["pool_w"], s["pool_scale"]
    g_dw = lax.dynamic_slice_in_dim(s["dw"], chip_cols, 256, axis=1).reshape(1, CONV_K, 1, 256)
    g_dwb = lax.dynamic_slice_in_dim(s["dwb"], chip_cols, 256, axis=1)
    g_lng = lax.dynamic_slice_in_dim(s["lng"], chip_cols, 256, axis=1)
    g_lnb = lax.dynamic_slice_in_dim(s["lnb"], chip_cols, 256, axis=1)

    turn = lambda a: jnp.swapaxes(a, 1, 2)
    a_w_in, m_a_w_in, v_a_w_in = turn(a_w_in), turn(m_a_w_in), turn(v_a_w_in)
    grads = [g_pre, g_post, g_wa_t[None], g_sinks, g_pool_w, g_pool_scale, g_ab_w_out, g_c_w_in, g_dw, g_dwb, g_lng,
             g_lnb, g_c_w_out]
    weights = [pre_norm, post_norm, a_w_in, a_sinks, b_pool_w, b_pool_scale, ab_w_out, c_w_in, c_dw_w, c_dw_b, c_ln_g,
               c_ln_b, c_w_out]
    moms = [m_pre_norm, m_post_norm, m_a_w_in, m_a_sinks, m_b_pool_w, m_b_pool_scale, m_ab_w_out, m_c_w_in, m_c_dw_w,
            m_c_dw_b, m_c_ln_g, m_c_ln_b, m_c_w_out]
    vars_ = [v_pre_norm, v_post_norm, v_a_w_in, v_a_sinks, v_b_pool_w, v_b_pool_scale, v_ab_w_out, v_c_w_in, v_c_dw_w,
             v_c_dw_b, v_c_ln_g, v_c_ln_b, v_c_w_out]
    deltas, new_m, new_v = [], [], []
    for k, (w, gr, m, v) in enumerate(zip(weights, grads, moms, vars_)):
        d, nm, nv = _adamw(w, gr, m, v, f"adamw{k}")
        deltas.append(d)
        new_m.append(nm)
        new_v.append(nv)
    for outs in (grads, deltas, new_m, new_v):
        outs[2] = turn(outs[2])
    return (s["loss"][0, 0], gx[None], *grads, *deltas, *new_m, *new_v)
```

```python
import jax
import jax.numpy as jnp
from jax import lax
from jax.experimental import pallas as pl
from jax.experimental.pallas import tpu as pltpu

F32 = jnp.float32
BF16 = jnp.bfloat16

D_MODEL = 1024
EPS = 1e-6
NEG = -1e30
HEAD_DIM = 64
GROUP = 4
KV_HEADS = 2
BLOCK = 128
EVEN_IN = 2304
ATTN_WIDTH = 512
POOL_WIDTH = 512
COL_Q, COL_K, COL_GA, COL_U, COL_GB = 0, 512, 768, 1280, 1792
POOL_GROUPS = 4
POOL_GC = 128
POOL_HALO = 16
CONV_K = 31
CONV_HALO = 32
GLU_A = slice(0, D_MODEL)
GLU_B = slice(D_MODEL, 2 * D_MODEL)
GATE = slice(2 * D_MODEL, 3 * D_MODEL)
N_DEV = 8

ADAM_LR = 0.001
ADAM_B1 = 0.9
ADAM_B2 = 0.999
ADAM_EPS = 1e-08
ADAM_WD = 0.01
ADAM_STEP = 10

VMEM_LIMIT_BYTES = 56 * 1024 * 1024

NT = (((1,), (1,)), ((), ()))
TN = (((0,), (0,)), ((), ()))
MESH_ID = pl.DeviceIdType.MESH


def _params(*sem):
    return pltpu.CompilerParams(dimension_semantics=sem, vmem_limit_bytes=VMEM_LIMIT_BYTES)


def _const_spec(shape):
    nd = len(shape)
    return pl.BlockSpec(shape, lambda *_: (0,) * nd, pipeline_mode=pl.Buffered(1))


def _sigmoid(v):
    return 0.5 * jnp.tanh(0.5 * v) + 0.5


def _silu(v):
    return v * _sigmoid(v)


def _silu_and_grad(v):
    s = _sigmoid(v)
    silu = v * s
    return silu, s + silu * (1.0 - s)


ANY = pl.BlockSpec(memory_space=pl.ANY)


def _place():
    x, y, c = lax.axis_index("x"), lax.axis_index("y"), lax.axis_index("c")
    chips = [(1 - x, y), (x, 1 - y), (1 - x, 1 - y)]
    return x, y, c, chips


class _Gather:
    def __init__(self, blocks, halve):
        self.ins = list(blocks)
        self.halve = halve
        self.n = n = len(blocks)
        self.shapes = [((b.shape[0] // 2) if halve else b.shape[0], b.shape[1]) for b in blocks]
        self.out_shape = [jax.ShapeDtypeStruct((N_DEV, r, cols), b.dtype) for (r, cols), b in zip(self.shapes, blocks)]
        self.scratch = [pltpu.SemaphoreType.DMA((7 * n,)), pltpu.SemaphoreType.DMA((7 * n,)),
                        pltpu.SemaphoreType.DMA((n,))]

    def _copies(self, ins, outs, sems):
        send_sems, recv_sems, local_sems = sems
        x, y, c, chips = _place()
        me, sibling = (x, y, c), (x, y, 1 - c)

        def piece(t, px, py, pc):
            return outs[t].at[4 * px + 2 * py + pc]

        def own(t):
            return ins[t].at[pl.ds(c * self.shapes[t][0], self.shapes[t][0])] if self.halve else ins[t]

        def copy(t, k, block, to, src=None):
            return pltpu.make_async_remote_copy(
                src_ref=piece(t, *block) if src is None else src, dst_ref=piece(t, *block),
                send_sem=send_sems.at[7 * t + k], recv_sem=recv_sems.at[7 * t + k],
                device_id=to, device_id_type=MESH_ID)

        rng = range(self.n)
        return dict(
            mine=[pltpu.make_async_copy(own(t), piece(t, *me), local_sems.at[t]) for t in rng],
            first=[copy(t, 0, me, sibling, src=own(t)) for t in rng]
            + [copy(t, 1 + j, me, (*chip, c), src=own(t)) for t in rng for j, chip in enumerate(chips)],
            landed=[copy(t, 1 + j, (*chip, c), me) for j, chip in enumerate(chips) for t in rng],
            passed=[copy(t, 4 + j, (*chip, c), sibling) for j, chip in enumerate(chips) for t in rng],
            from_sibling=[copy(t, 0, sibling, me) for t in rng]
            + [copy(t, 4 + j, (*chip, 1 - c), me) for t in rng for j, chip in enumerate(chips)])

    def start(self, ins, outs, sems):
        d = self._copies(ins, outs, sems)
        for cp in d["mine"] + d["first"]:
            cp.start()

    def middle(self, ins, outs, sems):
        d = self._copies(ins, outs, sems)
        for got, fwd in zip(d["landed"], d["passed"]):
            got.wait_recv()
            fwd.start()

    def finish(self, ins, outs, sems):
        d = self._copies(ins, outs, sems)
        for cp in d["from_sibling"]:
            cp.wait_recv()
        for cp in d["first"] + d["passed"]:
            cp.wait_send()
        for cp in d["mine"]:
            cp.wait()


class _Scatter:
    def __init__(self, tensors):
        self.ins = list(tensors)
        self.n = n = len(tensors)
        self.out_shape = [jax.ShapeDtypeStruct(t.shape, t.dtype) for t in tensors]
        self.scratch = [pltpu.SemaphoreType.DMA((7 * n,)), pltpu.SemaphoreType.DMA((7 * n,))]

    def _copies(self, ins, outs, sems):
        send_sems, recv_sems = sems
        x, y, c, _ = _place()
        me = 4 * x + 2 * y + c
        sends, recvs = [], []
        for t in range(self.n):
            for m in range(1, N_DEV):
                px, py, pc = x ^ (m >> 2), y ^ ((m >> 1) & 1), c ^ (m & 1)
                q = 4 * px + 2 * py + pc
                sems_k = dict(send_sem=send_sems.at[7 * t + m - 1], recv_sem=recv_sems.at[7 * t + m - 1],
                              device_id=(px, py, pc), device_id_type=MESH_ID)
                sends.append(pltpu.make_async_remote_copy(src_ref=ins[t].at[q], dst_ref=outs[t].at[me], **sems_k))
                recvs.append(pltpu.make_async_remote_copy(src_ref=ins[t].at[me], dst_ref=outs[t].at[q], **sems_k))
        return sends, recvs

    def start(self, ins, outs, sems):
        for cp in self._copies(ins, outs, sems)[0]:
            cp.start()

    def middle(self, ins, outs, sems):
        pass

    def finish(self, ins, outs, sems):
        sends, recvs = self._copies(ins, outs, sems)
        for cp in recvs:
            cp.wait_recv()
        for cp in sends:
            cp.wait_send()


class _Comms:
    def __init__(self, *comms):
        self.comms = comms
        self.ins = [a for c in comms for a in c.ins]
        self.out_shape = [s for c in comms for s in c.out_shape]
        self.scratch = [s for c in comms for s in c.scratch]

    def _each(self, phase, ins, outs, sems):
        i = o = s = 0
        for c in self.comms:
            ni, no, ns = len(c.ins), len(c.out_shape), len(c.scratch)
            getattr(c, phase)(ins[i:i + ni], outs[o:o + no], sems[s:s + ns])
            i, o, s = i + ni, o + no, s + ns

    def start(self, ins, outs, sems):
        self._each("start", ins, outs, sems)

    def middle(self, ins, outs, sems):
        self._each("middle", ins, outs, sems)

    def finish(self, ins, outs, sems):
        self._each("finish", ins, outs, sems)


def _run_comm(comm, name):
    n = len(comm.ins)

    def body(*refs):
        parts = refs[:n], refs[n:2 * n], refs[2 * n:]
        comm.start(*parts)
        comm.middle(*parts)
        comm.finish(*parts)

    return pl.pallas_call(body, name=name, out_shape=comm.out_shape, in_specs=[ANY] * n, out_specs=[ANY] * n,
                          scratch_shapes=comm.scratch)(*comm.ins)


HBM_SPEC = pl.BlockSpec(memory_space=pltpu.HBM)
SEM_SPEC = pl.BlockSpec(memory_space=pltpu.SEMAPHORE)
DATAFLOW = pltpu.SideEffectType.DATAFLOW_SIDE_EFFECTING


def _scatter_copies(own_ref, land_ref, send_sems, recv_sems):
    x, y, c, _ = _place()
    me = 4 * x + 2 * y + c
    pairs = []
    for m in range(1, N_DEV):
        px, py, pc = x ^ (m >> 2), y ^ ((m >> 1) & 1), c ^ (m & 1)
        q = 4 * px + 2 * py + pc
        sems = dict(send_sem=send_sems.at[m - 1], recv_sem=recv_sems.at[m - 1], device_id=(px, py, pc),
                    device_id_type=MESH_ID)
        pairs.append((pltpu.make_async_remote_copy(src_ref=own_ref.at[q], dst_ref=land_ref.at[me], **sems),
                      pltpu.make_async_remote_copy(src_ref=own_ref.at[me], dst_ref=land_ref.at[q], **sems)))
    return pairs


def _scatter_start(own, name):
    def body(own_ref, land_ref, send_sems, recv_sems, own_thru, land_thru, token):
        for send, _ in _scatter_copies(own_ref, land_ref, send_sems, recv_sems):
            send.start()
        token[...] = jnp.zeros_like(token)

    buf = pltpu.HBM(own.shape, own.dtype)
    return pl.pallas_call(
        body, name=name,
        out_shape=(pltpu.SemaphoreType.DMA((N_DEV - 1,)), pltpu.SemaphoreType.DMA((N_DEV - 1,)), buf, buf,
                   jax.ShapeDtypeStruct((8, 128), F32)),
        in_specs=(HBM_SPEC, HBM_SPEC),
        out_specs=(SEM_SPEC, SEM_SPEC, HBM_SPEC, HBM_SPEC, pl.BlockSpec(memory_space=pltpu.VMEM)),
        input_output_aliases={0: 2, 1: 3},
        compiler_params=pltpu.CompilerParams(has_side_effects=DATAFLOW),
    )(pltpu.with_memory_space_constraint(own, pltpu.HBM),
      pltpu.with_memory_space_constraint(lax.empty(own.shape, own.dtype), pltpu.HBM))


def _scatter_wait(send_sems, recv_sems, own_thru, land_thru, after, name):
    def body(own_ref, land_ref, send_sems, recv_sems, after_ref, own_out, land_out):
        for send, recv in _scatter_copies(own_ref, land_ref, send_sems, recv_sems):
            send.wait_send()
            recv.wait_recv()

    buf = pltpu.HBM(own_thru.shape, own_thru.dtype)
    return pl.pallas_call(
        body, name=name, out_shape=(buf, buf),
        in_specs=(HBM_SPEC, HBM_SPEC, SEM_SPEC, SEM_SPEC, ANY), out_specs=(HBM_SPEC, HBM_SPEC),
        input_output_aliases={0: 0, 1: 1},
        compiler_params=pltpu.CompilerParams(has_side_effects=DATAFLOW),
    )(own_thru, land_thru, send_sems, recv_sems, after)


def _fused_call(body, comm, args, *, name, grid, out_shape, in_specs, out_specs, scratch_shapes=(), params):
    single = not isinstance(out_shape, (list, tuple))
    out_shape = [out_shape] if single else list(out_shape)
    out_specs = [out_specs] if single else list(out_specs)
    if comm is None:
        res = pl.pallas_call(body, name=name, grid=grid, out_shape=out_shape, in_specs=in_specs, out_specs=out_specs,
                             scratch_shapes=list(scratch_shapes), compiler_params=params)(*args)
        return (res[0] if single else res), []
    n_in, n_out, n_scr = len(in_specs), len(out_shape), len(scratch_shapes)
    c_in, c_out = len(comm.ins), len(comm.out_shape)
    steps = grid[0]

    def fused(*refs):
        pos = 0
        groups = []
        for size in (n_in, c_in, n_out, c_out, n_scr, len(comm.scratch)):
            groups.append(refs[pos:pos + size])
            pos += size
        ins, c_ins, outs, c_outs, scr, c_sems = groups
        i = pl.program_id(0)

        @pl.when(i == 0)
        def _():
            comm.start(c_ins, c_outs, c_sems)

        @pl.when(i == steps // 2)
        def _():
            comm.middle(c_ins, c_outs, c_sems)

        body(*ins, *outs, *scr)

        @pl.when(i == steps - 1)
        def _():
            comm.finish(c_ins, c_outs, c_sems)

    res = pl.pallas_call(
        fused, name=name, grid=grid, out_shape=out_shape + list(comm.out_shape),
        in_specs=list(in_specs) + [ANY] * c_in, out_specs=out_specs + [ANY] * c_out,
        scratch_shapes=list(scratch_shapes) + list(comm.scratch), compiler_params=params)(*args, *comm.ins)
    main = res[:n_out]
    return (main[0] if single else main), list(res[n_out:])


def _norm_matmul(x, gain, wt, name, comm=None, tm=1024):
    t, n = x.shape[0], wt.shape[0]

    def body(x_ref, g_ref, wt_ref, o_ref):
        xv = x_ref[...]
        r = lax.rsqrt(jnp.mean(xv * xv, axis=-1, keepdims=True) + EPS)
        h = (xv * r * g_ref[...]).astype(BF16)
        o_ref[...] = lax.dot_general(h, wt_ref[...], NT, preferred_element_type=F32)

    return _fused_call(
        body, comm, (x, gain, wt), name=name, grid=(t // tm,),
        out_shape=jax.ShapeDtypeStruct((t, n), F32),
        in_specs=[pl.BlockSpec((tm, D_MODEL), lambda i: (i, 0)), _const_spec((1, D_MODEL)), _const_spec((n, D_MODEL))],
        out_specs=pl.BlockSpec((tm, n), lambda i: (i, 0)),
        params=_params("arbitrary"))


def _project_out(a, w_ref, x_ref, p_ref, y_ref):
    y = jnp.dot(a, w_ref[...], preferred_element_type=F32)
    y_ref[...] = y
    ry = lax.rsqrt(jnp.mean(y * y, axis=-1, keepdims=True) + EPS)
    return x_ref[...] + (y * ry) * p_ref[...]


def _post_bwd_rows(g, y, a, n_own, first, last, p_ref, w_ref, dw_ref, dw16_ref, dp_ref):
    @pl.when(first)
    def _():
        dw_ref[...] = jnp.zeros_like(dw_ref)
        dp_ref[...] = jnp.zeros_like(dp_ref)

    ry = lax.rsqrt(jnp.mean(y * y, axis=-1, keepdims=True) + EPS)
    nv = y * ry
    dp_ref[...] += jnp.sum((g * nv)[0:n_own], axis=0, keepdims=True)
    dn = g * p_ref[...]
    dy = (ry * (dn - nv * jnp.mean(dn * nv, axis=-1, keepdims=True))).astype(BF16)
    dw_ref[...] += lax.dot_general(a, dy[0:n_own], TN, preferred_element_type=F32)

    @pl.when(last)
    def _():
        dw16_ref[...] = dw_ref[...].astype(BF16)

    return lax.dot_general(dy, w_ref[...], NT, preferred_element_type=F32)


def _pre_bwd(dproj, wt, x_in, pre, g, name, comm=None, tm=512):
    t, n = dproj.shape
    steps = t // tm

    def body(dp_ref, wt_ref, x_ref, pre_ref, g_ref, dx_ref, dwt16_ref, dpre_ref, dwt_ref):
        @pl.when(pl.program_id(0) == 0)
        def _():
            dwt_ref[...] = jnp.zeros_like(dwt_ref)
            dpre_ref[...] = jnp.zeros_like(dpre_ref)

        dpv = dp_ref[...]
        dh = jnp.dot(dpv, wt_ref[...], preferred_element_type=F32)
        xv = x_ref[...]
        r = lax.rsqrt(jnp.mean(xv * xv, axis=-1, keepdims=True) + EPS)
        xn = xv * r
        pv = pre_ref[...]
        dpre_ref[...] += jnp.sum(dh * xn, axis=0, keepdims=True)
        dxn = dh * pv
        dx_ref[...] = g_ref[...] + r * (dxn - xn * jnp.mean(dxn * xn, axis=-1, keepdims=True))
        h = (xn * pv).astype(BF16)
        dwt_ref[...] += lax.dot_general(dpv, h, TN, preferred_element_type=F32)

        @pl.when(pl.program_id(0) == steps - 1)
        def _():
            dwt16_ref[...] = dwt_ref[...].astype(BF16)

    row = pl.BlockSpec((tm, D_MODEL), lambda i: (i, 0))
    return _fused_call(
        body, comm, (dproj, wt, x_in, pre, g), name=name, grid=(steps,),
        out_shape=[jax.ShapeDtypeStruct((t, D_MODEL), F32), jax.ShapeDtypeStruct((n, D_MODEL), BF16),
                   jax.ShapeDtypeStruct((1, D_MODEL), F32)],
        in_specs=[pl.BlockSpec((tm, n), lambda i: (i, 0)), _const_spec((n, D_MODEL)), row, _const_spec((1, D_MODEL)), row],
        out_specs=[row, _const_spec((n, D_MODEL)), pl.BlockSpec((1, D_MODEL), lambda i: (0, 0))],
        scratch_shapes=[pltpu.VMEM((n, D_MODEL), F32)],
        params=_params("arbitrary"))


def _proj_dw(dproj, x_in, pre, name, comm=None, tm=1024):
    t, n = dproj.shape
    steps = t // tm

    def body(dp_ref, x_ref, pre_ref, dwt16_ref, dwt_ref):
        @pl.when(pl.program_id(0) == 0)
        def _():
            dwt_ref[...] = jnp.zeros_like(dwt_ref)

        xv = x_ref[...]
        r = lax.rsqrt(jnp.mean(xv * xv, axis=-1, keepdims=True) + EPS)
        h = (xv * r * pre_ref[...]).astype(BF16)
        dwt_ref[...] += lax.dot_general(dp_ref[...], h, TN, preferred_element_type=F32)

        @pl.when(pl.program_id(0) == steps - 1)
        def _():
            dwt16_ref[...] = dwt_ref[...].astype(BF16)

    return _fused_call(
        body, comm, (dproj, x_in, pre), name=name, grid=(steps,),
        out_shape=jax.ShapeDtypeStruct((n, D_MODEL), BF16),
        in_specs=[pl.BlockSpec((tm, n), lambda i: (i, 0)), pl.BlockSpec((tm, D_MODEL), lambda i: (i, 0)),
                  _const_spec((1, D_MODEL))],
        out_specs=pl.BlockSpec((n, D_MODEL), lambda i: (0, 0)),
        scratch_shapes=[pltpu.VMEM((n, D_MODEL), F32)],
        params=_params("arbitrary"))


def _proj_dx(dproj, wt, x_in, pre, g, name, comm=None, tm=512):
    t, n = dproj.shape

    def body(dp_ref, wt_ref, x_ref, pre_ref, g_ref, dx_ref, dpre_ref):
        @pl.when(pl.program_id(0) == 0)
        def _():
            dpre_ref[...] = jnp.zeros_like(dpre_ref)

        dh = jnp.dot(dp_ref[...], wt_ref[...], preferred_element_type=F32)
        xv = x_ref[...]
        r = lax.rsqrt(jnp.mean(xv * xv, axis=-1, keepdims=True) + EPS)
        xn = xv * r
        dpre_ref[...] += jnp.sum(dh * xn, axis=0, keepdims=True)
        dxn = dh * pre_ref[...]
        dx_ref[...] = g_ref[...] + r * (dxn - xn * jnp.mean(dxn * xn, axis=-1, keepdims=True))

    row = pl.BlockSpec((tm, D_MODEL), lambda i: (i, 0))
    return _fused_call(
        body, comm, (dproj, wt, x_in, pre, g), name=name, grid=(t // tm,),
        out_shape=[jax.ShapeDtypeStruct((t, D_MODEL), F32), jax.ShapeDtypeStruct((1, D_MODEL), F32)],
        in_specs=[pl.BlockSpec((tm, n), lambda i: (i, 0)), _const_spec((n, D_MODEL)), row, _const_spec((1, D_MODEL)), row],
        out_specs=[row, pl.BlockSpec((1, D_MODEL), lambda i: (0, 0))],
        params=_params("arbitrary"))


def _group_masks():
    lane = lax.broadcasted_iota(jnp.int32, (1, GROUP * HEAD_DIM), 1)
    return [(lane // HEAD_DIM == g).astype(F32) for g in range(GROUP)]


def _stack_groups(v, masks):
    return jnp.concatenate([v * m for m in masks], axis=0)


def _unstack_groups(v, masks):
    out = v[0:BLOCK] * masks[0]
    for g in range(1, GROUP):
        out = out + v[g * BLOCK:(g + 1) * BLOCK] * masks[g]
    return out


def _repeat_head(kv2, kvh):
    first = lax.broadcasted_iota(jnp.int32, kv2.shape, 1) < HEAD_DIM
    rolled = pltpu.roll(kv2, HEAD_DIM, 1)
    one = jnp.where(first, kv2, rolled) if kvh == 0 else jnp.where(first, rolled, kv2)
    return jnp.concatenate([one, one], axis=1)


def _fold_head(v4):
    a = v4[:, 0:128] + v4[:, 128:256]
    return a + pltpu.roll(a, HEAD_DIM, 1)


ATTN_CONSTS = [pltpu.VMEM((KV_HEADS, GROUP * BLOCK, 2 * BLOCK), F32)]


def _fill_attn_bias(bias_ref):
    row = lax.broadcasted_iota(jnp.int32, (GROUP * BLOCK, 2 * BLOCK), 0)
    col = lax.broadcasted_iota(jnp.int32, (GROUP * BLOCK, 2 * BLOCK), 1)
    dist = (row % BLOCK) + BLOCK - col
    band = (dist >= 0) & (dist < BLOCK)
    rb = lax.broadcasted_iota(jnp.int32, (GROUP * BLOCK, 1), 0) // BLOCK
    for kvh in range(KV_HEADS):
        slope = jnp.zeros((GROUP * BLOCK, 1), F32)
        for g in range(GROUP):
            slope = jnp.where(rb == g, 2.0 ** (-(kvh * GROUP + g + 1)), slope)
        bias_ref[kvh] = jnp.where(band, -slope * dist.astype(F32), NEG)


def _row_sinks(kvh, sink_ref):
    rb = lax.broadcasted_iota(jnp.int32, (GROUP * BLOCK, 1), 0) // BLOCK
    sink = jnp.zeros((GROUP * BLOCK, 1), F32)
    for g in range(GROUP):
        sink = jnp.where(rb == g, sink_ref[0, kvh * GROUP + g], sink)
    return sink


def _attn_probs(qk, k4, bias, sink, no_past, masks):
    qs = _stack_groups(qk, masks).astype(BF16)
    s = lax.dot_general(qs, k4, NT, preferred_element_type=F32) * (HEAD_DIM ** -0.5) + bias
    s = jnp.concatenate([jnp.where(no_past, NEG, s[:, 0:BLOCK]), s[:, BLOCK:]], axis=1)
    mx = jnp.maximum(jnp.max(s, axis=-1, keepdims=True), sink)
    e = jnp.exp(s - mx)
    es = jnp.exp(sink - mx)
    inv = 1.0 / (jnp.sum(e, axis=-1, keepdims=True) + es)
    return qs, e * inv, es * inv


def _pool_forward(u_ext, g, t0):
    n = u_ext.shape[0] - POOL_HALO
    s = u_ext
    for step in range(g + 1):
        s = s + pltpu.roll(s, 1 << step, 0)
    w = 2 << g
    t = t0 + lax.broadcasted_iota(jnp.int32, (n, 1), 0)
    cnt = jnp.minimum(t + 1, w).astype(F32)
    return s[POOL_HALO:] / cnt - u_ext[POOL_HALO:]


def _layer0_fwd(proj, sinks, pool_w, pool_scale, w_out, x_in, post, comm=None, tq=512):
    t = proj.shape[0]
    nblk = tq // BLOCK

    def body(main_ref, halo_ref, sink_ref, pw_ref, ps_ref, w_ref, x_ref, p_ref, o_ref, y_ref, xo_ref, kv_ref, bias_ref):
        i = pl.program_id(0)
        t0 = i * tq
        masks = _group_masks()

        @pl.when(i == 0)
        def _():
            _fill_attn_bias(bias_ref)

        kv_ref[0:BLOCK, :] = halo_ref[:, COL_K:COL_K + 256]
        kv_ref[BLOCK:, :] = main_ref[:, COL_K:COL_K + 256]

        def block(jb, carry):
            r0 = pl.multiple_of(jb * BLOCK, BLOCK)
            no_past = t0 + r0 == 0
            q = main_ref[pl.ds(r0, BLOCK), COL_Q:COL_Q + ATTN_WIDTH]
            ga = main_ref[pl.ds(r0, BLOCK), COL_GA:COL_GA + ATTN_WIDTH]
            kk = kv_ref[pl.ds(r0, 2 * BLOCK), 0:128]
            vv = kv_ref[pl.ds(r0, 2 * BLOCK), 128:256]
            outs = []
            for kvh in range(KV_HEADS):
                k4 = _repeat_head(kk, kvh).astype(BF16)
                v4 = _repeat_head(vv, kvh).astype(BF16)
                _, p, _ = _attn_probs(q[:, kvh * 256:(kvh + 1) * 256], k4, bias_ref[kvh], _row_sinks(kvh, sink_ref), no_past, masks)
                pv = jnp.dot(p.astype(BF16), v4, preferred_element_type=F32)
                outs.append(_unstack_groups(pv, masks))
            attn = jnp.concatenate(outs, axis=1)
            o_ref[pl.ds(r0, BLOCK), 0:ATTN_WIDTH] = (attn * _silu(ga)).astype(BF16)
            return carry

        lax.fori_loop(0, nblk, block, 0, unroll=True)

        for g in range(POOL_GROUPS):
            cu = COL_U + g * POOL_GC
            cg = COL_GB + g * POOL_GC
            halo_u = jnp.where(i == 0, 0.0, halo_ref[BLOCK - POOL_HALO:BLOCK, cu:cu + POOL_GC])
            u_ext = jnp.concatenate([halo_u, main_ref[:, cu:cu + POOL_GC]], axis=0)
            pooled = _pool_forward(u_ext, g, t0)
            y = jnp.dot(pooled.astype(BF16), pw_ref[g].astype(BF16), preferred_element_type=F32)
            y = y * ps_ref[:, g * POOL_GC:(g + 1) * POOL_GC]
            o_ref[:, ATTN_WIDTH + g * POOL_GC:ATTN_WIDTH + (g + 1) * POOL_GC] =(y * _silu(main_ref[:, cg:cg + POOL_GC])).astype(BF16)

        xo_ref[...] = _project_out(o_ref[...], w_ref, x_ref, p_ref, y_ref)

    row = pl.BlockSpec((tq, D_MODEL), lambda i: (i, 0))
    return _fused_call(
        body, comm, (proj, proj, sinks, pool_w, pool_scale, w_out, x_in, post), name="layer0_fwd", grid=(t // tq,),
        out_shape=[jax.ShapeDtypeStruct((t, D_MODEL), BF16), jax.ShapeDtypeStruct((t, D_MODEL), F32),
                   jax.ShapeDtypeStruct((t, D_MODEL), F32)],
        in_specs=[pl.BlockSpec((tq, EVEN_IN), lambda i: (i, 0)),
                  pl.BlockSpec((BLOCK, EVEN_IN), lambda i: (jnp.maximum(i * nblk - 1, 0), 0)),
                  pl.BlockSpec(memory_space=pltpu.SMEM),
                  _const_spec((POOL_GROUPS, POOL_GC, POOL_GC)), _const_spec((1, POOL_WIDTH)),
                  _const_spec((D_MODEL, D_MODEL)), row, _const_spec((1, D_MODEL))],
        out_specs=[row, row, row],
        scratch_shapes=[pltpu.VMEM((tq + BLOCK, 256), F32)] + ATTN_CONSTS,
        params=_params("arbitrary"))


def _layer0_bwd(proj, gy, y, mix, w_out, post, sinks, pool_w, pool_scale, comm=None, tq=512):
    t = proj.shape[0]
    nt = t // tq
    nblk = tq // BLOCK

    def body(main_ref, halo_ref, next_ref, gy_ref, gyn_ref, y_ref, yn_ref, mix_ref, wo_ref, po_ref,
             sink_ref, pw_ref, ps_ref,
             o_ref, dsk_ref, dpw_ref, dps_ref, dwo16_ref, dpo_ref,
             kv_ref, dkv_ref, carry_ref, bias_ref, dwo_ref, dmix_ref):
        i = pl.program_id(0)
        ii = nt - 1 - i
        t0 = ii * tq
        masks = _group_masks()

        @pl.when(i == 0)
        def _():
            _fill_attn_bias(bias_ref)
            dsk_ref[...] = jnp.zeros_like(dsk_ref)
            dpw_ref[...] = jnp.zeros_like(dpw_ref)
            dps_ref[...] = jnp.zeros_like(dps_ref)
            carry_ref[...] = jnp.zeros_like(carry_ref)

        dmix_ref[...] = _post_bwd_rows(jnp.concatenate([gy_ref[...], gyn_ref[...]], axis=0),
                                       jnp.concatenate([y_ref[...], yn_ref[...]], axis=0), mix_ref[...], tq,
                                       i == 0, i == nt - 1, po_ref, wo_ref, dwo_ref, dwo16_ref, dpo_ref)
        dm_ref = dmix_ref.at[pl.ds(0, tq)]
        dmn_ref = dmix_ref.at[pl.ds(tq, POOL_HALO)]

        kv_ref[0:BLOCK, :] = halo_ref[:, COL_K:COL_K + 256]
        kv_ref[BLOCK:, :] = main_ref[:, COL_K:COL_K + 256]
        dkv_ref[0:tq, :] = jnp.zeros((tq, 256), F32)
        dkv_ref[tq:, :] = carry_ref[...]

        def block(jb, carry):
            r0 = pl.multiple_of(jb * BLOCK, BLOCK)
            no_past = t0 + r0 == 0
            q = main_ref[pl.ds(r0, BLOCK), COL_Q:COL_Q + ATTN_WIDTH]
            ga = main_ref[pl.ds(r0, BLOCK), COL_GA:COL_GA + ATTN_WIDTH]
            dya = dm_ref[pl.ds(r0, BLOCK), 0:ATTN_WIDTH]
            kk = kv_ref[pl.ds(r0, 2 * BLOCK), 0:128]
            vv = kv_ref[pl.ds(r0, 2 * BLOCK), 128:256]
            silu_ga, dsilu_ga = _silu_and_grad(ga)
            do = dya * silu_ga
            first = lax.broadcasted_iota(jnp.int32, (2 * BLOCK, 128), 1) < HEAD_DIM
            attn, dq, dk, dv = [], [], [], []
            for kvh in range(KV_HEADS):
                k4 = _repeat_head(kk, kvh).astype(BF16)
                v4 = _repeat_head(vv, kvh).astype(BF16)
                qs, p, ps = _attn_probs(q[:, kvh * 256:(kvh + 1) * 256], k4, bias_ref[kvh], _row_sinks(kvh, sink_ref), no_past, masks)
                pb = p.astype(BF16)
                o_k = _unstack_groups(jnp.dot(pb, v4, preferred_element_type=F32), masks)
                do_k = do[:, kvh * 256:(kvh + 1) * 256]
                dos = _stack_groups(do_k, masks).astype(BF16)
                prod = do_k * o_k
                delta = jnp.concatenate([jnp.sum(prod * m, axis=-1, keepdims=True) for m in masks], axis=0)
                dp = lax.dot_general(dos, v4, NT, preferred_element_type=F32)
                ds = (p * (dp - delta)).astype(BF16)
                sink_term = ps * delta
                for g in range(GROUP):
                    h = kvh * GROUP + g
                    dsk_ref[h:h + 1, :] -= jnp.sum(sink_term[g * BLOCK:(g + 1) * BLOCK], keepdims=True)
                scale = HEAD_DIM ** -0.5
                dq.append(_unstack_groups(jnp.dot(ds, k4, preferred_element_type=F32), masks) * scale)
                dk.append(_fold_head(lax.dot_general(ds, qs, TN, preferred_element_type=F32)) * scale)
                dv.append(_fold_head(lax.dot_general(pb, dos, TN, preferred_element_type=F32)))
                attn.append(o_k)
            o_ref[pl.ds(r0, BLOCK), COL_Q:COL_Q + ATTN_WIDTH] = jnp.concatenate(dq, axis=1).astype(BF16)
            o_all = jnp.concatenate(attn, axis=1)
            o_ref[pl.ds(r0, BLOCK), COL_GA:COL_GA + ATTN_WIDTH] = (dya * o_all * dsilu_ga).astype(BF16)
            dkv = jnp.concatenate([jnp.where(first, dk[0], dk[1]), jnp.where(first, dv[0], dv[1])], axis=1)
            dkv_ref[pl.ds(r0, 2 * BLOCK), :] += dkv
            return carry

        lax.fori_loop(0, nblk, block, 0, unroll=True)
        carry_ref[...] = dkv_ref[0:BLOCK, :]
        o_ref[:, COL_K:COL_K + 256] = dkv_ref[BLOCK:, :].astype(BF16)

        last = ii == nt - 1
        for g in range(POOL_GROUPS):
            cu = COL_U + g * POOL_GC
            cg = COL_GB + g * POOL_GC
            cm = ATTN_WIDTH + g * POOL_GC
            pw = pw_ref[g].astype(BF16)
            sc = ps_ref[:, g * POOL_GC:(g + 1) * POOL_GC]
            halo_u = jnp.where(ii == 0, 0.0, halo_ref[BLOCK - POOL_HALO:BLOCK, cu:cu + POOL_GC])
            u_ext = jnp.concatenate([halo_u, main_ref[:, cu:cu + POOL_GC]], axis=0)
            pooled = _pool_forward(u_ext, g, t0).astype(BF16)
            y_raw = jnp.dot(pooled, pw, preferred_element_type=F32)
            gb = main_ref[:, cg:cg + POOL_GC]
            dyb = dm_ref[:, cm:cm + POOL_GC]
            silu_gb, dsilu_gb = _silu_and_grad(gb)
            dypool = dyb * silu_gb
            dps_ref[:, g * POOL_GC:(g + 1) * POOL_GC] += jnp.sum(dypool * y_raw, axis=0, keepdims=True)
            o_ref[:, cg:cg + POOL_GC] = (dyb * (y_raw * sc) * dsilu_gb).astype(BF16)
            dyraw = dypool * sc
            dyraw_n = jnp.where(last, 0.0, dmn_ref[:, cm:cm + POOL_GC] * _silu(next_ref[:, cg:cg + POOL_GC]) * sc)
            dpw_ref[g * POOL_GC:(g + 1) * POOL_GC, :] += lax.dot_general(pooled, dyraw.astype(BF16), TN,
                                                                         preferred_element_type=F32)
            dyraw_ext = jnp.concatenate([dyraw, dyraw_n], axis=0).astype(BF16)
            dpooled = lax.dot_general(dyraw_ext, pw, NT, preferred_element_type=F32)
            w = 2 << g
            tt = t0 + lax.broadcasted_iota(jnp.int32, (tq + POOL_HALO, 1), 0)
            s = dpooled / jnp.minimum(tt + 1, w).astype(F32)
            for step in range(g + 1):
                s = s + pltpu.roll(s, tq + POOL_HALO - (1 << step), 0)
            o_ref[:, cu:cu + POOL_GC] = (s[0:tq] - dpooled[0:tq]).astype(BF16)

    rev = lambda i: nt - 1 - i
    nxt = lambda i: (jnp.minimum((rev(i) + 1) * (tq // POOL_HALO), t // POOL_HALO - 1), 0)
    row = pl.BlockSpec((tq, D_MODEL), lambda i: (rev(i), 0))
    nxt_row = pl.BlockSpec((POOL_HALO, D_MODEL), nxt)
    square = _const_spec((D_MODEL, D_MODEL))
    return _fused_call(
        body, comm, (proj, proj, proj, gy, gy, y, y, mix, w_out, post, sinks, pool_w, pool_scale),
        name="layer0_bwd", grid=(nt,),
        out_shape=[jax.ShapeDtypeStruct((t, EVEN_IN), BF16), jax.ShapeDtypeStruct((8, 128), F32),
                   jax.ShapeDtypeStruct((POOL_GROUPS * POOL_GC, POOL_GC), F32), jax.ShapeDtypeStruct((1, POOL_WIDTH), F32),
                   jax.ShapeDtypeStruct((D_MODEL, D_MODEL), BF16), jax.ShapeDtypeStruct((1, D_MODEL), F32)],
        in_specs=[pl.BlockSpec((tq, EVEN_IN), lambda i: (rev(i), 0)),
                  pl.BlockSpec((BLOCK, EVEN_IN), lambda i: (jnp.maximum(rev(i) * nblk - 1, 0), 0)),
                  pl.BlockSpec((POOL_HALO, EVEN_IN), nxt),
                  row, nxt_row, row, nxt_row, row, square, _const_spec((1, D_MODEL)),
                  pl.BlockSpec(memory_space=pltpu.SMEM),
                  _const_spec((POOL_GROUPS, POOL_GC, POOL_GC)), _const_spec((1, POOL_WIDTH))],
        out_specs=[pl.BlockSpec((tq, EVEN_IN), lambda i: (rev(i), 0)),
                   pl.BlockSpec((8, 128), lambda i: (0, 0)),
                   pl.BlockSpec((POOL_GROUPS * POOL_GC, POOL_GC), lambda i: (0, 0)),
                   pl.BlockSpec((1, POOL_WIDTH), lambda i: (0, 0)), square, pl.BlockSpec((1, D_MODEL), lambda i: (0, 0))],
        scratch_shapes=[pltpu.VMEM((tq + BLOCK, 256), F32), pltpu.VMEM((tq + BLOCK, 256), F32),
                        pltpu.VMEM((BLOCK, 256), F32)] + ATTN_CONSTS
        + [pltpu.VMEM((D_MODEL, D_MODEL), F32), pltpu.VMEM((tq + POOL_HALO, D_MODEL), F32)],
        params=_params("arbitrary"))


CONV_RC = 32
CONV_CC = 128
CONV_CHAINS = 4
CONV_UNROLL = 2


def _fill_shifted(s_ref, rows):
    for b in range(1, 8):
        s_ref[b, 0:rows - 8, :] = s_ref[0, b:b + rows - 8, :]


def _tap_blocks(s_ref, r, cols, lead):
    for b in range(8):
        taps = [(a, 8 * a + b - lead) for a in range(5) if 0 <= 8 * a + b - lead < CONV_K]
        span = 8 * max(a for a, _ in taps) + CONV_RC
        blk = s_ref[b, pl.ds(r, span), cols]
        for a, k in taps:
            yield k, blk[8 * a:8 * a + CONV_RC]


def _conv_taps(s_ref, w_ref, r, cols, lead, reverse):
    accs = [None] * CONV_CHAINS
    for n, (k, blk) in enumerate(_tap_blocks(s_ref, r, cols, lead)):
        kw = CONV_K - 1 - k if reverse else k
        term = blk * w_ref[kw:kw + 1, cols]
        accs[n % CONV_CHAINS] = term if accs[n % CONV_CHAINS] is None else accs[n % CONV_CHAINS] + term
    while len(accs) > 1:
        accs = [a + b for a, b in zip(accs[0::2], accs[1::2])]
    return accs[0]


def _layer_norm_fwd(cf, lng, lnb):
    mu = jnp.mean(cf, axis=-1, keepdims=True)
    xc = cf - mu
    rstd = lax.rsqrt(jnp.mean(xc * xc, axis=-1, keepdims=True) + EPS)
    chat = xc * rstd
    return chat, rstd, chat * lng + lnb


def _layer1_fwd(proj, dw, dwb, lng, lnb, w_out, x_in, post, target, tt=256):
    t = proj.shape[0]
    lead = CONV_HALO - (CONV_K - 1)

    def body(main_ref, halo_ref, w_ref, b_ref, g_ref, lb_ref, wo_ref, x_ref, p_ref, t_ref,
             o_ref, c_ref, y_ref, dl_ref, l_ref, gs_ref):
        i = pl.program_id(0)
        hv = halo_ref[...]
        gs_ref[0, 0:CONV_HALO, :] = jnp.where(i == 0, 0.0, hv[:, GLU_A] * _sigmoid(hv[:, GLU_B]))
        gs_ref[0, CONV_HALO:CONV_HALO + tt, :] = main_ref[:, GLU_A] * _sigmoid(main_ref[:, GLU_B])
        _fill_shifted(gs_ref, tt + CONV_HALO)

        for c in range(D_MODEL // CONV_CC):
            cols = slice(c * CONV_CC, (c + 1) * CONV_CC)

            def chunk(j, carry):
                r = pl.multiple_of(j * CONV_RC, CONV_RC)
                c_ref[pl.ds(r, CONV_RC), cols] = _conv_taps(gs_ref, w_ref, r, cols, lead, False) + b_ref[:, cols]
                return carry
            lax.fori_loop(0, tt // CONV_RC, chunk, 0, unroll=CONV_UNROLL)

        _, _, cn = _layer_norm_fwd(c_ref[...], g_ref[...], lb_ref[...])
        o_ref[...] = (_silu(cn) * _silu(main_ref[:, GATE])).astype(BF16)

        d = _project_out(o_ref[...], wo_ref, x_ref, p_ref, y_ref) - t_ref[...]
        dl_ref[...] = d * (1.0 / D_MODEL)

        @pl.when(i == 0)
        def _():
            l_ref[...] = jnp.zeros_like(l_ref)

        l_ref[...] += 0.5 * jnp.sum(jnp.mean(d * d, axis=-1, keepdims=True))

    vec = _const_spec((1, D_MODEL))
    row = pl.BlockSpec((tt, D_MODEL), lambda i: (i, 0))
    f32_rows = jax.ShapeDtypeStruct((t, D_MODEL), F32)
    return pl.pallas_call(
        body, name="layer1_fwd", grid=(t // tt,),
        out_shape=[jax.ShapeDtypeStruct((t, D_MODEL), BF16), f32_rows, f32_rows, f32_rows,
                   jax.ShapeDtypeStruct((8, 128), F32)],
        in_specs=[pl.BlockSpec((tt, 3 * D_MODEL), lambda i: (i, 0)),
                  pl.BlockSpec((CONV_HALO, 3 * D_MODEL), lambda i: (jnp.maximum(i * (tt // CONV_HALO) - 1, 0), 0)),
                  _const_spec((CONV_K, D_MODEL)), vec, vec, vec,
                  _const_spec((D_MODEL, D_MODEL)), row, vec, row],
        out_specs=[row, row, row, row, pl.BlockSpec((8, 128), lambda i: (0, 0))],
        scratch_shapes=[pltpu.VMEM((8, tt + CONV_HALO, D_MODEL), F32)],
        compiler_params=_params("arbitrary"),
    )(proj, proj, dw, dwb, lng, lnb, w_out, x_in, post, target)


def _layer1_bwd(proj, cf, gy, y, z, w_out, post, dw, lng, lnb, comm=None, tt=256):
    t = proj.shape[0]
    nt = t // tt
    te = tt + CONV_HALO

    def body(main_ref, next_ref, cf_ref, cfn_ref, gy_ref, gyn_ref, y_ref, yn_ref, z_ref, wo_ref, po_ref,
             w_ref, g_ref, lb_ref,
             o_ref, ddw_ref, ddb_ref, dg_ref, dlb_ref, dwo16_ref, dpo_ref, ds_ref, glu_ref, sb_ref, dwo_ref):
        i = pl.program_id(0)

        @pl.when(i == 0)
        def _():
            ddw_ref[...] = jnp.zeros_like(ddw_ref)
            ddb_ref[...] = jnp.zeros_like(ddb_ref)
            dg_ref[...] = jnp.zeros_like(dg_ref)
            dlb_ref[...] = jnp.zeros_like(dlb_ref)

        dzv = _post_bwd_rows(jnp.concatenate([gy_ref[...], gyn_ref[...]], axis=0),
                             jnp.concatenate([y_ref[...], yn_ref[...]], axis=0), z_ref[...], tt, i == 0, i == nt - 1,
                             po_ref, wo_ref, dwo_ref, dwo16_ref, dpo_ref)
        dzv = jnp.concatenate([dzv[0:tt], jnp.where(i < nt - 1, dzv[tt:], 0.0)], axis=0)
        lng = g_ref[...]
        chat, rstd, cn = _layer_norm_fwd(jnp.concatenate([cf_ref[...], cfn_ref[...]], axis=0), lng, lb_ref[...])
        gate = jnp.concatenate([main_ref[:, GATE], next_ref[:, GATE]], axis=0)
        silu_cn, dsilu_cn = _silu_and_grad(cn)
        silu_gate, dsilu_gate = _silu_and_grad(gate)
        o_ref[:, GATE] = (dzv * silu_cn * dsilu_gate)[0:tt].astype(BF16)
        dcn = dzv * silu_gate * dsilu_cn
        dg_ref[...] += jnp.sum((dcn * chat)[0:tt], axis=0, keepdims=True)
        dlb_ref[...] += jnp.sum(dcn[0:tt], axis=0, keepdims=True)
        dchat = dcn * lng
        dcf = rstd * (dchat - jnp.mean(dchat, axis=-1, keepdims=True) - chat * jnp.mean(dchat * chat, axis=-1, keepdims=True))
        ddb_ref[...] += jnp.sum(dcf[0:tt], axis=0, keepdims=True)
        ds_ref[0, 0:te, :] = dcf
        ds_ref[0, te:, :] = jnp.zeros((8, D_MODEL), F32)
        _fill_shifted(ds_ref, te + 8)
        sb_ref[...] = _sigmoid(main_ref[:, GLU_B])
        glu_ref[...] = main_ref[:, GLU_A] * sb_ref[...]

        for c in range(D_MODEL // CONV_CC):
            cols = slice(c * CONV_CC, (c + 1) * CONV_CC)
            gcols = slice(D_MODEL + c * CONV_CC, D_MODEL + (c + 1) * CONV_CC)

            def chunk(j, carry):
                r = pl.multiple_of(j * CONV_RC, CONV_RC)
                dglu = _conv_taps(ds_ref, w_ref, r, cols, 0, True)
                sb = sb_ref[pl.ds(r, CONV_RC), cols]
                o_ref[pl.ds(r, CONV_RC), cols] = (dglu * sb).astype(BF16)
                o_ref[pl.ds(r, CONV_RC), gcols] = (dglu * glu_ref[pl.ds(r, CONV_RC), cols] * (1.0 - sb)).astype(BF16)
                return carry
            lax.fori_loop(0, tt // CONV_RC, chunk, 0, unroll=CONV_UNROLL)

            def taps(j, accs):
                r = pl.multiple_of(j * CONV_RC, CONV_RC)
                gl = glu_ref[pl.ds(r, CONV_RC), cols]
                new = list(accs)
                for m, blk in _tap_blocks(ds_ref, r, cols, 0):
                    prod = blk * gl
                    part = prod[0:8]
                    for q in range(1, CONV_RC // 8):
                        part = part + prod[8 * q:8 * q + 8]
                    new[m] = new[m] + part
                return tuple(new)
            accs = lax.fori_loop(0, tt // CONV_RC, taps, tuple(jnp.zeros((8, CONV_CC), F32) for _ in range(CONV_K)))
            for m in range(CONV_K):
                k = CONV_K - 1 - m
                ddw_ref[k:k + 1, cols] += jnp.sum(accs[m], axis=0, keepdims=True)

    vec = _const_spec((1, D_MODEL))
    vec_out = pl.BlockSpec((1, D_MODEL), lambda i: (0, 0))
    row = pl.BlockSpec((tt, D_MODEL), lambda i: (i, 0))
    nxt = lambda i: (jnp.minimum((i + 1) * (tt // CONV_HALO), t // CONV_HALO - 1), 0)
    nxt_row = pl.BlockSpec((CONV_HALO, D_MODEL), nxt)
    vec_f32 = jax.ShapeDtypeStruct((1, D_MODEL), F32)
    square = _const_spec((D_MODEL, D_MODEL))
    return _fused_call(
        body, comm, (proj, proj, cf, cf, gy, gy, y, y, z, w_out, post, dw, lng, lnb), name="layer1_bwd", grid=(nt,),
        out_shape=[jax.ShapeDtypeStruct((t, 3 * D_MODEL), BF16), jax.ShapeDtypeStruct((CONV_K, D_MODEL), F32),
                   vec_f32, vec_f32, vec_f32, jax.ShapeDtypeStruct((D_MODEL, D_MODEL), BF16), vec_f32],
        in_specs=[pl.BlockSpec((tt, 3 * D_MODEL), lambda i: (i, 0)),
                  pl.BlockSpec((CONV_HALO, 3 * D_MODEL), nxt),
                  row, nxt_row, row, nxt_row, row, nxt_row, row, square, vec,
                  _const_spec((CONV_K, D_MODEL)), vec, vec],
        out_specs=[pl.BlockSpec((tt, 3 * D_MODEL), lambda i: (i, 0)),
                   pl.BlockSpec((CONV_K, D_MODEL), lambda i: (0, 0)), vec_out, vec_out, vec_out, square, vec_out],
        scratch_shapes=[pltpu.VMEM((8, te + 8, D_MODEL), F32), pltpu.VMEM((tt, D_MODEL), F32),
                        pltpu.VMEM((tt, D_MODEL), F32), pltpu.VMEM((D_MODEL, D_MODEL), F32)],
        params=_params("arbitrary"))


def _piece_sum(parts, place, name):
    r = parts[0][0].shape[1]

    def body(p_ref, *refs):
        o_ref = refs[-1]
        acc = refs[0][0].astype(F32)
        for part in refs[1:-1]:
            acc = acc + part[0].astype(F32)
        o_ref[0] = acc

    blk = (1, r, D_MODEL)
    spec = lambda slot: pl.BlockSpec(blk, lambda j, p_ref: (slot(p_ref), 0, 0))
    return pl.pallas_call(
        body, name=name,
        grid_spec=pltpu.PrefetchScalarGridSpec(
            num_scalar_prefetch=1, grid=(1,),
            in_specs=[spec(slot) for _, slot in parts],
            out_specs=pl.BlockSpec(blk, lambda j, p_ref: (p_ref[1], 0, 0))),
        out_shape=jax.ShapeDtypeStruct((2, r, D_MODEL), F32),
        compiler_params=_params("arbitrary"),
    )(place, *[a for a, _ in parts])


def _direct_parts(own, recv):
    peer = lambda m: (lambda p: p[0] ^ m)
    return [(own, peer(0))] + [(recv, peer(m)) for m in range(1, N_DEV)]


def _share_with_sibling(halves, name):
    n = len(halves)

    def body(*refs):
        outs = refs[n:2 * n]
        send_sems, recv_sems = refs[2 * n:]
        x, y, c, _ = _place()
        send = [pltpu.make_async_remote_copy(
            src_ref=outs[t].at[c], dst_ref=outs[t].at[c], send_sem=send_sems.at[t], recv_sem=recv_sems.at[t],
            device_id=(x, y, 1 - c), device_id_type=MESH_ID) for t in range(n)]
        recv = [pltpu.make_async_remote_copy(
            src_ref=outs[t].at[c], dst_ref=outs[t].at[1 - c], send_sem=send_sems.at[t], recv_sem=recv_sems.at[t],
            device_id=(x, y, 1 - c), device_id_type=MESH_ID) for t in range(n)]
        for cp in send:
            cp.start()
        for cp in recv:
            cp.wait_recv()
        for cp in send:
            cp.wait_send()

    return pl.pallas_call(
        body, name=name,
        out_shape=[jax.ShapeDtypeStruct(h.shape, h.dtype) for h in halves],
        in_specs=[ANY] * n, out_specs=[ANY] * n,
        input_output_aliases={t: t for t in range(n)},
        scratch_shapes=[pltpu.SemaphoreType.DMA((n,)), pltpu.SemaphoreType.DMA((n,))],
    )(*halves)


def _sum8(parts, name):
    r = parts.shape[1]

    def body(p_ref, o_ref):
        acc = p_ref[0]
        for k in range(1, N_DEV):
            acc = acc + p_ref[k]
        o_ref[...] = acc

    return pl.pallas_call(
        body, name=name, out_shape=jax.ShapeDtypeStruct((r, 128), F32),
        in_specs=[pl.BlockSpec(memory_space=pltpu.VMEM)], out_specs=pl.BlockSpec(memory_space=pltpu.VMEM),
    )(parts)


def _adamw(w, g, m, v, name):
    shape = w.shape
    cols = shape[-1]
    rows = w.size // cols
    rt = 256 if rows % 256 == 0 else rows
    if w.ndim >= 3 and shape[-2] == 1:
        view, blk, at = (rows, 1, cols), (rows, 1, cols), (lambda i: (0, 0, 0))
    else:
        view, blk, at = (rows, cols), (rt, cols), (lambda i: (i, 0))

    def body(w_ref, g_ref, m_ref, v_ref, d_ref, nm_ref, nv_ref):
        gv = g_ref[...]
        mn = ADAM_B1 * m_ref[...] + (1.0 - ADAM_B1) * gv
        vn = ADAM_B2 * v_ref[...] + (1.0 - ADAM_B2) * (gv * gv)
        m_hat = mn / (1.0 - ADAM_B1 ** ADAM_STEP)
        v_hat = vn / (1.0 - ADAM_B2 ** ADAM_STEP)
        d_ref[...] = -ADAM_LR * (m_hat / (jnp.sqrt(v_hat) + ADAM_EPS) + ADAM_WD * w_ref[...])
        nm_ref[...] = mn
        nv_ref[...] = vn

    spec = pl.BlockSpec(blk, at)
    outs = pl.pallas_call(
        body, name=name, grid=(view[0] // blk[0],),
        out_shape=[jax.ShapeDtypeStruct(view, F32)] * 3,
        in_specs=[spec] * 4, out_specs=[spec] * 3,
        compiler_params=_params("parallel"),
    )(*[a.reshape(view) for a in (w, g, m, v)])
    return [o.reshape(shape) for o in outs]


SMALL_ROWS = 832


def _pack_small(g):
    parts = [g["loss"], g["pre1"].reshape(8, 128), g["post0"].reshape(8, 128),
             g["post1"].reshape(8, 128), g["sinks"], jnp.pad(g["pool_scale"].reshape(4, 128), ((0, 4), (0, 0))),
             g["pool_w"], g["dw"].reshape(248, 128), g["dwb"].reshape(8, 128), g["lng"].reshape(8, 128),
             g["lnb"].reshape(8, 128)]
    assert sum(p.shape[0] for p in parts) == SMALL_ROWS
    return jnp.concatenate(parts, axis=0)


def _unpack_small(s):
    out, r = {}, 0
    for key, rows, shape in (("loss", 8, (8, 128)), ("pre1", 8, (1, D_MODEL)), ("post", 16, (2, D_MODEL)),
                             ("sinks", 8, (8, 128)),
                             ("pool_scale", 4, (1, POOL_WIDTH)), ("pad", 4, (4, 128)), ("pool_w", 512, (1, 4, 128, 128)),
                             ("dw", 248, (CONV_K, D_MODEL)), ("dwb", 8, (1, D_MODEL)), ("lng", 8, (1, D_MODEL)),
                             ("lnb", 8, (1, D_MODEL))):
        out[key] = s[r:r + rows].reshape(shape)
        r += rows
    return out


def kernel(x, pre_norm, post_norm, a_w_in, a_sinks, b_pool_w, b_pool_scale, ab_w_out, c_w_in, c_dw_w, c_dw_b, c_ln_g, c_ln_b, c_w_out, loss_target, m_pre_norm, m_post_norm, m_a_w_in, m_a_sinks, m_b_pool_w, m_b_pool_scale, m_ab_w_out, m_c_w_in, m_c_dw_w, m_c_dw_b, m_c_ln_g, m_c_ln_b, m_c_w_out, v_pre_norm, v_post_norm, v_a_w_in, v_a_sinks, v_b_pool_w, v_b_pool_scale, v_ab_w_out, v_c_w_in, v_c_dw_w, v_c_dw_b, v_c_ln_g, v_c_ln_b, v_c_w_out):
    ix, iy = lax.axis_index("x"), lax.axis_index("y")
    chip_cols = (2 * ix + iy) * 256

    pad8 = lambda v: jnp.pad(v, ((0, -v.shape[0] % 8), (0, 0)))
    vec_shard = jnp.concatenate([pad8(c_dw_w.reshape(CONV_K, 256)), pad8(c_dw_b), pad8(c_ln_g), pad8(c_ln_b),
                                 jnp.zeros((8, 256), F32)], axis=0)
    x0, target = x[0], loss_target[0]
    pre0, pre1 = pre_norm[0:1], pre_norm[1:2]
    post0, post1 = post_norm[0:1], post_norm[1:2]
    pool_w = b_pool_w[0]

    (wa_t,) = _run_comm(_Gather([a_w_in[0].T.astype(BF16)], halve=True), "gather_a_w_in")
    wa_t = wa_t.reshape(EVEN_IN, D_MODEL)
    proj0, (w_ab,) = _norm_matmul(x0, pre0, wa_t, "proj0_fwd", comm=_Gather([ab_w_out[0].astype(BF16)], halve=True))
    w_ab = w_ab.reshape(D_MODEL, D_MODEL)
    (mix0, y0, x1), (wc_t, w_c, vecs) = _layer0_fwd(
        proj0, a_sinks, pool_w, b_pool_scale, w_ab, x0, post0,
        comm=_Gather([c_w_in[0].T.astype(BF16), c_w_out[0].astype(BF16), vec_shard], halve=True))
    wc_t = wc_t.reshape(3 * D_MODEL, D_MODEL)
    w_c = w_c.reshape(D_MODEL, D_MODEL)
    vecs = vecs.reshape(4, 64, 256).transpose(1, 0, 2).reshape(64, D_MODEL)
    dw, dwb, lng, lnb = vecs[0:CONV_K], vecs[32:33], vecs[40:41], vecs[48:49]
    proj1, _ = _norm_matmul(x1, pre1, wc_t, "proj1_fwd")
    z1, cf1, y1, g2, loss = _layer1_fwd(proj1, dw, dwb, lng, lnb, w_c, x1, post1, target)

    pieces = lambda m: m.reshape(N_DEV, m.shape[0] // N_DEV, D_MODEL)
    (dproj1, d_dw, d_dwb, d_lng, d_lnb, d_wc, d_post1), _ = _layer1_bwd(proj1, cf1, g2, y1, z1, w_c, post1, dw, lng, lnb)
    (g1, d_wct, d_pre1), (r_wc,) = _pre_bwd(dproj1, wc_t, x1, pre1, g2, "proj1_bwd", comm=_Scatter([pieces(d_wc)]))
    (dproj0, d_sinks, d_pw, d_ps, d_wab, d_post0), (r_wct,) = _layer0_bwd(
        proj0, g1, y0, mix0, w_ab, post0, a_sinks, pool_w, b_pool_scale, comm=_Scatter([pieces(d_wct)]))
    g = dict(loss=loss, pre1=d_pre1, post0=d_post0, post1=d_post1, sinks=d_sinks, pool_w=d_pw, pool_scale=d_ps,
             dw=d_dw, dwb=d_dwb, lng=d_lng, lnb=d_lnb)
    d_wat, (small8, r_wab) = _proj_dw(dproj0, x0, pre0, "proj0_dw",
                                      comm=_Comms(_Gather([_pack_small(g)], halve=False), _Scatter([pieces(d_wab)])))
    sent = _scatter_start(pieces(d_wat), "scatter_a_start")
    (gx, d_pre0), _ = _proj_dx(dproj0, wa_t, x0, pre0 + sent[4][0:1, 0:1], g1, "proj0_dx")
    sent_pre0 = _scatter_start(jnp.broadcast_to(d_pre0.reshape(1, 8, 128), (N_DEV, 8, 128)), "pre0_start")
    own_wat, r_wat = _scatter_wait(*sent[:4], d_pre0, "scatter_a_wait")

    ic = lax.axis_index("c")
    me = 4 * ix + 2 * iy + ic
    place = jnp.stack([me, ic]).astype(jnp.int32)
    parts = [_direct_parts(own_wat, r_wat), _direct_parts(pieces(d_wab), r_wab), _direct_parts(pieces(d_wct), r_wct),
             _direct_parts(pieces(d_wc), r_wc)]
    halves = [_piece_sum(p, place, f"grad_sum{t}") for t, p in enumerate(parts)]
    g_wa_t, g_wab, g_wc_t, g_wc = [h.reshape(2 * h.shape[1], D_MODEL) for h in _share_with_sibling(halves, "grad_share")]
    _, pre0_8 = _scatter_wait(*sent_pre0[:4], g_wc, "pre0_wait")
    pre0_8 = lax.dynamic_update_slice(pre0_8, d_pre0.reshape(1, 8, 128), (me, 0, 0))
    g_c_w_in = g_wc_t.T[None]
    g_ab_w_out = g_wab[None]
    g_c_w_out = g_wc[None]

    s = _unpack_small(_sum8(small8, "small_sum"))
    layer = lax.broadcasted_iota(jnp.int32, (2, D_MODEL), 0)
    g_pre = jnp.where(layer == 0, _sum8(pre0_8, "pre0_sum").reshape(1, D_MODEL), s["pre1"])
    g_post = s["post"]
    g_sinks = s["sinks"][:, 0].reshape(1, 8)
    g_pool_w, g_pool_scale = s["pool_w"], s["pool_scale"]
    g_dw = lax.dynamic_slice_in_dim(s["dw"], chip_cols, 256, axis=1).reshape(1, CONV_K, 1, 256)
    g_dwb = lax.dynamic_slice_in_dim(s["dwb"], chip_cols, 256, axis=1)
    g_lng = lax.dynamic_slice_in_dim(s["lng"], chip_cols, 256, axis=1)
    g_lnb = lax.dynamic_slice_in_dim(s["lnb"], chip_cols, 256, axis=1)

    turn = lambda a: jnp.swapaxes(a, 1, 2)
    a_w_in, m_a_w_in, v_a_w_in = turn(a_w_in), turn(m_a_w_in), turn(v_a_w_in)
    grads = [g_pre, g_post, g_wa_t[None], g_sinks, g_pool_w, g_pool_scale, g_ab_w_out, g_c_w_in, g_dw, g_dwb, g_lng,
             g_lnb, g_c_w_out]
    weights = [pre_norm, post_norm, a_w_in, a_sinks, b_pool_w, b_pool_scale, ab_w_out, c_w_in, c_dw_w, c_dw_b, c_ln_g,
               c_ln_b, c_w_out]
    moms = [m_pre_norm, m_post_norm, m_a_w_in, m_a_sinks, m_b_pool_w, m_b_pool_scale, m_ab_w_out, m_c_w_in, m_c_dw_w,
            m_c_dw_b, m_c_ln_g, m_c_ln_b, m_c_w_out]
    vars_ = [v_pre_norm, v_post_norm, v_a_w_in, v_a_sinks, v_b_pool_w, v_b_pool_scale, v_ab_w_out, v_c_w_in, v_c_dw_w,
             v_c_dw_b, v_c_ln_g, v_c_ln_b, v_c_w_out]
    deltas, new_m, new_v = [], [], []
    for k, (w, gr, m, v) in enumerate(zip(weights, grads, moms, vars_)):
        d, nm, nv = _adamw(w, gr, m, v, f"adamw{k}")
        deltas.append(d)
        new_m.append(nm)
        new_v.append(nv)
    for outs in (grads, deltas, new_m, new_v):
        outs[2] = turn(outs[2])
    return (s["loss"][0, 0], gx[None], *grads, *deltas, *new_m, *new_v)
```

```python
import jax
import jax.numpy as jnp
from jax import lax
from jax.experimental import pallas as pl
from jax.experimental.pallas import tpu as pltpu

F32 = jnp.float32
BF16 = jnp.bfloat16

D_MODEL = 1024
EPS = 1e-6
NEG = -1e30
HEAD_DIM = 64
GROUP = 4
KV_HEADS = 2
BLOCK = 128
EVEN_IN = 2304
ATTN_WIDTH = 512
POOL_WIDTH = 512
COL_Q, COL_K, COL_GA, COL_U, COL_GB = 0, 512, 768, 1280, 1792
POOL_GROUPS = 4
POOL_GC = 128
POOL_HALO = 16
CONV_K = 31
CONV_HALO = 32
GLU_A = slice(0, D_MODEL)
GLU_B = slice(D_MODEL, 2 * D_MODEL)
GATE = slice(2 * D_MODEL, 3 * D_MODEL)
N_DEV = 8

ADAM_LR = 0.001
ADAM_B1 = 0.9
ADAM_B2 = 0.999
ADAM_EPS = 1e-08
ADAM_WD = 0.01
ADAM_STEP = 10

VMEM_LIMIT_BYTES = 56 * 1024 * 1024

NT = (((1,), (1,)), ((), ()))
TN = (((0,), (0,)), ((), ()))
MESH_ID = pl.DeviceIdType.MESH


def _params(*sem):
    return pltpu.CompilerParams(dimension_semantics=sem, vmem_limit_bytes=VMEM_LIMIT_BYTES)


def _const_spec(shape):
    nd = len(shape)
    return pl.BlockSpec(shape, lambda *_: (0,) * nd, pipeline_mode=pl.Buffered(1))


def _sigmoid(v):
    return 0.5 * jnp.tanh(0.5 * v) + 0.5


def _silu(v):
    h = 0.5 * v
    return h * jnp.tanh(h) + h


def _silu_and_grad(v):
    s = _sigmoid(v)
    silu = v * s
    return silu, s + silu * (1.0 - s)


ANY = pl.BlockSpec(memory_space=pl.ANY)


def _place():
    x, y, c = lax.axis_index("x"), lax.axis_index("y"), lax.axis_index("c")
    chips = [(1 - x, y), (x, 1 - y), (1 - x, 1 - y)]
    return x, y, c, chips


class _Gather:
    def __init__(self, blocks, halve):
        self.ins = list(blocks)
        self.halve = halve
        self.n = n = len(blocks)
        self.shapes = [((b.shape[0] // 2) if halve else b.shape[0], b.shape[1]) for b in blocks]
        self.out_shape = [jax.ShapeDtypeStruct((N_DEV, r, cols), b.dtype) for (r, cols), b in zip(self.shapes, blocks)]
        self.scratch = [pltpu.SemaphoreType.DMA((7 * n,)), pltpu.SemaphoreType.DMA((7 * n,)),
                        pltpu.SemaphoreType.DMA((n,))]

    def _copies(self, ins, outs, sems):
        send_sems, recv_sems, local_sems = sems
        x, y, c, chips = _place()
        me, sibling = (x, y, c), (x, y, 1 - c)

        def piece(t, px, py, pc):
            return outs[t].at[4 * px + 2 * py + pc]

        def own(t):
            return ins[t].at[pl.ds(c * self.shapes[t][0], self.shapes[t][0])] if self.halve else ins[t]

        def copy(t, k, block, to, src=None):
            return pltpu.make_async_remote_copy(
                src_ref=piece(t, *block) if src is None else src, dst_ref=piece(t, *block),
                send_sem=send_sems.at[7 * t + k], recv_sem=recv_sems.at[7 * t + k],
                device_id=to, device_id_type=MESH_ID)

        rng = range(self.n)
        return dict(
            mine=[pltpu.make_async_copy(own(t), piece(t, *me), local_sems.at[t]) for t in rng],
            first=[copy(t, 0, me, sibling, src=own(t)) for t in rng]
            + [copy(t, 1 + j, me, (*chip, c), src=own(t)) for t in rng for j, chip in enumerate(chips)],
            landed=[copy(t, 1 + j, (*chip, c), me) for j, chip in enumerate(chips) for t in rng],
            passed=[copy(t, 4 + j, (*chip, c), sibling) for j, chip in enumerate(chips) for t in rng],
            from_sibling=[copy(t, 0, sibling, me) for t in rng]
            + [copy(t, 4 + j, (*chip, 1 - c), me) for t in rng for j, chip in enumerate(chips)])

    def start(self, ins, outs, sems):
        d = self._copies(ins, outs, sems)
        for cp in d["mine"] + d["first"]:
            cp.start()

    def middle(self, ins, outs, sems):
        d = self._copies(ins, outs, sems)
        for got, fwd in zip(d["landed"], d["passed"]):
            got.wait_recv()
            fwd.start()

    def finish(self, ins, outs, sems):
        d = self._copies(ins, outs, sems)
        for cp in d["from_sibling"]:
            cp.wait_recv()
        for cp in d["first"] + d["passed"]:
            cp.wait_send()
        for cp in d["mine"]:
            cp.wait()


class _Scatter:
    def __init__(self, tensors):
        self.ins = list(tensors)
        self.n = n = len(tensors)
        self.out_shape = [jax.ShapeDtypeStruct(t.shape, t.dtype) for t in tensors]
        self.scratch = [pltpu.SemaphoreType.DMA((7 * n,)), pltpu.SemaphoreType.DMA((7 * n,))]

    def _copies(self, ins, outs, sems):
        send_sems, recv_sems = sems
        x, y, c, _ = _place()
        me = 4 * x + 2 * y + c
        sends, recvs = [], []
        for t in range(self.n):
            for m in range(1, N_DEV):
                px, py, pc = x ^ (m >> 2), y ^ ((m >> 1) & 1), c ^ (m & 1)
                q = 4 * px + 2 * py + pc
                sems_k = dict(send_sem=send_sems.at[7 * t + m - 1], recv_sem=recv_sems.at[7 * t + m - 1],
                              device_id=(px, py, pc), device_id_type=MESH_ID)
                sends.append(pltpu.make_async_remote_copy(src_ref=ins[t].at[q], dst_ref=outs[t].at[me], **sems_k))
                recvs.append(pltpu.make_async_remote_copy(src_ref=ins[t].at[me], dst_ref=outs[t].at[q], **sems_k))
        return sends, recvs

    def start(self, ins, outs, sems):
        for cp in self._copies(ins, outs, sems)[0]:
            cp.start()

    def middle(self, ins, outs, sems):
        pass

    def finish(self, ins, outs, sems):
        sends, recvs = self._copies(ins, outs, sems)
        for cp in recvs:
            cp.wait_recv()
        for cp in sends:
            cp.wait_send()


class _Comms:
    def __init__(self, *comms):
        self.comms = comms
        self.ins = [a for c in comms for a in c.ins]
        self.out_shape = [s for c in comms for s in c.out_shape]
        self.scratch = [s for c in comms for s in c.scratch]

    def _each(self, phase, ins, outs, sems):
        i = o = s = 0
        for c in self.comms:
            ni, no, ns = len(c.ins), len(c.out_shape), len(c.scratch)
            getattr(c, phase)(ins[i:i + ni], outs[o:o + no], sems[s:s + ns])
            i, o, s = i + ni, o + no, s + ns

    def start(self, ins, outs, sems):
        self._each("start", ins, outs, sems)

    def middle(self, ins, outs, sems):
        self._each("middle", ins, outs, sems)

    def finish(self, ins, outs, sems):
        self._each("finish", ins, outs, sems)


def _run_comm(comm, name):
    n = len(comm.ins)

    def body(*refs):
        parts = refs[:n], refs[n:2 * n], refs[2 * n:]
        comm.start(*parts)
        comm.middle(*parts)
        comm.finish(*parts)

    return pl.pallas_call(body, name=name, out_shape=comm.out_shape, in_specs=[ANY] * n, out_specs=[ANY] * n,
                          scratch_shapes=comm.scratch)(*comm.ins)


HBM_SPEC = pl.BlockSpec(memory_space=pltpu.HBM)
SEM_SPEC = pl.BlockSpec(memory_space=pltpu.SEMAPHORE)
DATAFLOW = pltpu.SideEffectType.DATAFLOW_SIDE_EFFECTING


def _scatter_copies(own_ref, land_ref, send_sems, recv_sems):
    x, y, c, _ = _place()
    me = 4 * x + 2 * y + c
    pairs = []
    for m in range(1, N_DEV):
        px, py, pc = x ^ (m >> 2), y ^ ((m >> 1) & 1), c ^ (m & 1)
        q = 4 * px + 2 * py + pc
        sems = dict(send_sem=send_sems.at[m - 1], recv_sem=recv_sems.at[m - 1], device_id=(px, py, pc),
                    device_id_type=MESH_ID)
        pairs.append((pltpu.make_async_remote_copy(src_ref=own_ref.at[q], dst_ref=land_ref.at[me], **sems),
                      pltpu.make_async_remote_copy(src_ref=own_ref.at[me], dst_ref=land_ref.at[q], **sems)))
    return pairs


def _scatter_start(own, name):
    def body(own_ref, land_ref, send_sems, recv_sems, own_thru, land_thru, token):
        for send, _ in _scatter_copies(own_ref, land_ref, send_sems, recv_sems):
            send.start()
        token[...] = jnp.zeros_like(token)

    buf = pltpu.HBM(own.shape, own.dtype)
    return pl.pallas_call(
        body, name=name,
        out_shape=(pltpu.SemaphoreType.DMA((N_DEV - 1,)), pltpu.SemaphoreType.DMA((N_DEV - 1,)), buf, buf,
                   jax.ShapeDtypeStruct((8, 128), F32)),
        in_specs=(HBM_SPEC, HBM_SPEC),
        out_specs=(SEM_SPEC, SEM_SPEC, HBM_SPEC, HBM_SPEC, pl.BlockSpec(memory_space=pltpu.VMEM)),
        input_output_aliases={0: 2, 1: 3},
        compiler_params=pltpu.CompilerParams(has_side_effects=DATAFLOW),
    )(pltpu.with_memory_space_constraint(own, pltpu.HBM),
      pltpu.with_memory_space_constraint(lax.empty(own.shape, own.dtype), pltpu.HBM))


def _scatter_wait(send_sems, recv_sems, own_thru, land_thru, after, name):
    def body(own_ref, land_ref, send_sems, recv_sems, after_ref, own_out, land_out):
        for send, recv in _scatter_copies(own_ref, land_ref, send_sems, recv_sems):
            send.wait_send()
            recv.wait_recv()

    buf = pltpu.HBM(own_thru.shape, own_thru.dtype)
    return pl.pallas_call(
        body, name=name, out_shape=(buf, buf),
        in_specs=(HBM_SPEC, HBM_SPEC, SEM_SPEC, SEM_SPEC, ANY), out_specs=(HBM_SPEC, HBM_SPEC),
        input_output_aliases={0: 0, 1: 1},
        compiler_params=pltpu.CompilerParams(has_side_effects=DATAFLOW),
    )(own_thru, land_thru, send_sems, recv_sems, after)


def _fused_call(body, comm, args, *, name, grid, out_shape, in_specs, out_specs, scratch_shapes=(), params):
    single = not isinstance(out_shape, (list, tuple))
    out_shape = [out_shape] if single else list(out_shape)
    out_specs = [out_specs] if single else list(out_specs)
    if comm is None:
        res = pl.pallas_call(body, name=name, grid=grid, out_shape=out_shape, in_specs=in_specs, out_specs=out_specs,
                             scratch_shapes=list(scratch_shapes), compiler_params=params)(*args)
        return (res[0] if single else res), []
    n_in, n_out, n_scr = len(in_specs), len(out_shape), len(scratch_shapes)
    c_in, c_out = len(comm.ins), len(comm.out_shape)
    steps = grid[0]

    def fused(*refs):
        pos = 0
        groups = []
        for size in (n_in, c_in, n_out, c_out, n_scr, len(comm.scratch)):
            groups.append(refs[pos:pos + size])
            pos += size
        ins, c_ins, outs, c_outs, scr, c_sems = groups
        i = pl.program_id(0)

        @pl.when(i == 0)
        def _():
            comm.start(c_ins, c_outs, c_sems)

        @pl.when(i == steps // 2)
        def _():
            comm.middle(c_ins, c_outs, c_sems)

        body(*ins, *outs, *scr)

        @pl.when(i == steps - 1)
        def _():
            comm.finish(c_ins, c_outs, c_sems)

    res = pl.pallas_call(
        fused, name=name, grid=grid, out_shape=out_shape + list(comm.out_shape),
        in_specs=list(in_specs) + [ANY] * c_in, out_specs=out_specs + [ANY] * c_out,
        scratch_shapes=list(scratch_shapes) + list(comm.scratch), compiler_params=params)(*args, *comm.ins)
    main = res[:n_out]
    return (main[0] if single else main), list(res[n_out:])


def _norm_matmul(x, gain, wt, name, comm=None, tm=1024):
    t, n = x.shape[0], wt.shape[0]

    def body(x_ref, g_ref, wt_ref, o_ref):
        xv = x_ref[...]
        r = lax.rsqrt(jnp.mean(xv * xv, axis=-1, keepdims=True) + EPS)
        h = (xv * r * g_ref[...]).astype(BF16)
        o_ref[...] = lax.dot_general(h, wt_ref[...], NT, preferred_element_type=F32)

    return _fused_call(
        body, comm, (x, gain, wt), name=name, grid=(t // tm,),
        out_shape=jax.ShapeDtypeStruct((t, n), F32),
        in_specs=[pl.BlockSpec((tm, D_MODEL), lambda i: (i, 0)), _const_spec((1, D_MODEL)), _const_spec((n, D_MODEL))],
        out_specs=pl.BlockSpec((tm, n), lambda i: (i, 0)),
        params=_params("arbitrary"))


def _project_out(a, w_ref, x_ref, p_ref, y_ref):
    y = jnp.dot(a, w_ref[...], preferred_element_type=F32)
    y_ref[...] = y
    ry = lax.rsqrt(jnp.mean(y * y, axis=-1, keepdims=True) + EPS)
    return x_ref[...] + (y * ry) * p_ref[...]


def _post_bwd_rows(g, y, a, n_own, first, last, p_ref, w_ref, dw_ref, dw16_ref, dp_ref):
    @pl.when(first)
    def _():
        dw_ref[...] = jnp.zeros_like(dw_ref)
        dp_ref[...] = jnp.zeros_like(dp_ref)

    ry = lax.rsqrt(jnp.mean(y * y, axis=-1, keepdims=True) + EPS)
    nv = y * ry
    dp_ref[...] += jnp.sum((g * nv)[0:n_own], axis=0, keepdims=True)
    dn = g * p_ref[...]
    dy = (ry * (dn - nv * jnp.mean(dn * nv, axis=-1, keepdims=True))).astype(BF16)
    dw_ref[...] += lax.dot_general(a, dy[0:n_own], TN, preferred_element_type=F32)

    @pl.when(last)
    def _():
        dw16_ref[...] = dw_ref[...].astype(BF16)

    return lax.dot_general(dy, w_ref[...], NT, preferred_element_type=F32)


def _pre_bwd(dproj, wt, x_in, pre, g, name, comm=None, tm=512):
    t, n = dproj.shape
    steps = t // tm

    def body(dp_ref, wt_ref, x_ref, pre_ref, g_ref, dx_ref, dwt16_ref, dpre_ref, dwt_ref):
        @pl.when(pl.program_id(0) == 0)
        def _():
            dwt_ref[...] = jnp.zeros_like(dwt_ref)
            dpre_ref[...] = jnp.zeros_like(dpre_ref)

        dpv = dp_ref[...]
        dh = jnp.dot(dpv, wt_ref[...], preferred_element_type=F32)
        xv = x_ref[...]
        r = lax.rsqrt(jnp.mean(xv * xv, axis=-1, keepdims=True) + EPS)
        xn = xv * r
        pv = pre_ref[...]
        dpre_ref[...] += jnp.sum(dh * xn, axis=0, keepdims=True)
        dxn = dh * pv
        dx_ref[...] = g_ref[...] + r * (dxn - xn * jnp.mean(dxn * xn, axis=-1, keepdims=True))
        h = (xn * pv).astype(BF16)
        dwt_ref[...] += lax.dot_general(dpv, h, TN, preferred_element_type=F32)

        @pl.when(pl.program_id(0) == steps - 1)
        def _():
            dwt16_ref[...] = dwt_ref[...].astype(BF16)

    row = pl.BlockSpec((tm, D_MODEL), lambda i: (i, 0))
    return _fused_call(
        body, comm, (dproj, wt, x_in, pre, g), name=name, grid=(steps,),
        out_shape=[jax.ShapeDtypeStruct((t, D_MODEL), F32), jax.ShapeDtypeStruct((n, D_MODEL), BF16),
                   jax.ShapeDtypeStruct((1, D_MODEL), F32)],
        in_specs=[pl.BlockSpec((tm, n), lambda i: (i, 0)), _const_spec((n, D_MODEL)), row, _const_spec((1, D_MODEL)), row],
        out_specs=[row, _const_spec((n, D_MODEL)), pl.BlockSpec((1, D_MODEL), lambda i: (0, 0))],
        scratch_shapes=[pltpu.VMEM((n, D_MODEL), F32)],
        params=_params("arbitrary"))


def _proj_dw(dproj, x_in, pre, name, comm=None, tm=1024):
    t, n = dproj.shape
    steps = t // tm

    def body(dp_ref, x_ref, pre_ref, dwt16_ref, dwt_ref):
        @pl.when(pl.program_id(0) == 0)
        def _():
            dwt_ref[...] = jnp.zeros_like(dwt_ref)

        xv = x_ref[...]
        r = lax.rsqrt(jnp.mean(xv * xv, axis=-1, keepdims=True) + EPS)
        h = (xv * r * pre_ref[...]).astype(BF16)
        dwt_ref[...] += lax.dot_general(dp_ref[...], h, TN, preferred_element_type=F32)

        @pl.when(pl.program_id(0) == steps - 1)
        def _():
            dwt16_ref[...] = dwt_ref[...].astype(BF16)

    return _fused_call(
        body, comm, (dproj, x_in, pre), name=name, grid=(steps,),
        out_shape=jax.ShapeDtypeStruct((n, D_MODEL), BF16),
        in_specs=[pl.BlockSpec((tm, n), lambda i: (i, 0)), pl.BlockSpec((tm, D_MODEL), lambda i: (i, 0)),
                  _const_spec((1, D_MODEL))],
        out_specs=pl.BlockSpec((n, D_MODEL), lambda i: (0, 0)),
        scratch_shapes=[pltpu.VMEM((n, D_MODEL), F32)],
        params=_params("arbitrary"))


def _proj_dx(dproj, wt, x_in, pre, g, name, comm=None, tm=512):
    t, n = dproj.shape

    def body(dp_ref, wt_ref, x_ref, pre_ref, g_ref, dx_ref, dpre_ref):
        @pl.when(pl.program_id(0) == 0)
        def _():
            dpre_ref[...] = jnp.zeros_like(dpre_ref)

        dh = jnp.dot(dp_ref[...], wt_ref[...], preferred_element_type=F32)
        xv = x_ref[...]
        r = lax.rsqrt(jnp.mean(xv * xv, axis=-1, keepdims=True) + EPS)
        xn = xv * r
        dpre_ref[...] += jnp.sum(dh * xn, axis=0, keepdims=True)
        dxn = dh * pre_ref[...]
        dx_ref[...] = g_ref[...] + r * (dxn - xn * jnp.mean(dxn * xn, axis=-1, keepdims=True))

    row = pl.BlockSpec((tm, D_MODEL), lambda i: (i, 0))
    return _fused_call(
        body, comm, (dproj, wt, x_in, pre, g), name=name, grid=(t // tm,),
        out_shape=[jax.ShapeDtypeStruct((t, D_MODEL), F32), jax.ShapeDtypeStruct((1, D_MODEL), F32)],
        in_specs=[pl.BlockSpec((tm, n), lambda i: (i, 0)), _const_spec((n, D_MODEL)), row, _const_spec((1, D_MODEL)), row],
        out_specs=[row, pl.BlockSpec((1, D_MODEL), lambda i: (0, 0))],
        params=_params("arbitrary"))


def _group_masks():
    lane = lax.broadcasted_iota(jnp.int32, (1, GROUP * HEAD_DIM), 1)
    return [lane // HEAD_DIM == g for g in range(GROUP)]


def _stack_groups(v, masks, scale=1.0):
    return jnp.concatenate([v * jnp.where(m, scale, 0.0) for m in masks], axis=0)


def _unstack_groups(v, masks):
    out = v[(GROUP - 1) * BLOCK:GROUP * BLOCK]
    for g in range(GROUP - 2, -1, -1):
        out = jnp.where(masks[g], v[g * BLOCK:(g + 1) * BLOCK], out)
    return out


def _repeat_head(kv2, kvh):
    first = lax.broadcasted_iota(jnp.int32, kv2.shape, 1) < HEAD_DIM
    rolled = pltpu.roll(kv2, HEAD_DIM, 1)
    one = jnp.where(first, kv2, rolled) if kvh == 0 else jnp.where(first, rolled, kv2)
    return jnp.concatenate([one, one], axis=1)


def _fold_head(v4):
    a = v4[:, 0:128] + v4[:, 128:256]
    return a + pltpu.roll(a, HEAD_DIM, 1)


ATTN_CONSTS = [pltpu.VMEM((KV_HEADS, GROUP * BLOCK, 2 * BLOCK), F32)]


def _fill_attn_bias(bias_ref):
    row = lax.broadcasted_iota(jnp.int32, (GROUP * BLOCK, 2 * BLOCK), 0)
    col = lax.broadcasted_iota(jnp.int32, (GROUP * BLOCK, 2 * BLOCK), 1)
    dist = (row % BLOCK) + BLOCK - col
    band = (dist >= 0) & (dist < BLOCK)
    rb = lax.broadcasted_iota(jnp.int32, (GROUP * BLOCK, 1), 0) // BLOCK
    for kvh in range(KV_HEADS):
        slope = jnp.zeros((GROUP * BLOCK, 1), F32)
        for g in range(GROUP):
            slope = jnp.where(rb == g, 2.0 ** (-(kvh * GROUP + g + 1)), slope)
        bias_ref[kvh] = jnp.where(band, -slope * dist.astype(F32), NEG)


def _row_sinks(kvh, sink_ref):
    rb = lax.broadcasted_iota(jnp.int32, (GROUP * BLOCK, 1), 0) // BLOCK
    sink = jnp.zeros((GROUP * BLOCK, 1), F32)
    for g in range(GROUP):
        sink = jnp.where(rb == g, sink_ref[0, kvh * GROUP + g], sink)
    return sink


def _attn_probs(qk, k4, bias, sink, no_past, masks):
    qs = _stack_groups(qk, masks, HEAD_DIM ** -0.5).astype(BF16)
    s = lax.dot_general(qs, k4, NT, preferred_element_type=F32) + bias
    s = jnp.concatenate([jnp.where(no_past, NEG, s[:, 0:BLOCK]), s[:, BLOCK:]], axis=1)
    mx = jnp.maximum(jnp.max(s, axis=-1, keepdims=True), sink)
    e = jnp.exp(s - mx)
    es = jnp.exp(sink - mx)
    inv = 1.0 / (jnp.sum(e, axis=-1, keepdims=True) + es)
    return qs, e * inv, es * inv


def _pool_forward(u_ext, g, t0):
    n = u_ext.shape[0] - POOL_HALO
    s = u_ext
    for step in range(g + 1):
        s = s + pltpu.roll(s, 1 << step, 0)
    w = 2 << g
    t = t0 + lax.broadcasted_iota(jnp.int32, (n, 1), 0)
    cnt = jnp.minimum(t + 1, w).astype(F32)
    return s[POOL_HALO:] / cnt - u_ext[POOL_HALO:]


def _layer0_fwd(proj, sinks, pool_w, pool_scale, w_out, x_in, post, comm=None, tq=512):
    t = proj.shape[0]
    nblk = tq // BLOCK

    def body(main_ref, halo_ref, sink_ref, pw_ref, ps_ref, w_ref, x_ref, p_ref, o_ref, y_ref, xo_ref, kv_ref, bias_ref):
        i = pl.program_id(0)
        t0 = i * tq
        masks = _group_masks()

        @pl.when(i == 0)
        def _():
            _fill_attn_bias(bias_ref)

        kv_ref[0:BLOCK, :] = halo_ref[:, COL_K:COL_K + 256]
        kv_ref[BLOCK:, :] = main_ref[:, COL_K:COL_K + 256]

        def block(jb, carry):
            r0 = pl.multiple_of(jb * BLOCK, BLOCK)
            no_past = t0 + r0 == 0
            q = main_ref[pl.ds(r0, BLOCK), COL_Q:COL_Q + ATTN_WIDTH]
            ga = main_ref[pl.ds(r0, BLOCK), COL_GA:COL_GA + ATTN_WIDTH]
            kk = kv_ref[pl.ds(r0, 2 * BLOCK), 0:128]
            vv = kv_ref[pl.ds(r0, 2 * BLOCK), 128:256]
            outs = []
            for kvh in range(KV_HEADS):
                k4 = _repeat_head(kk, kvh).astype(BF16)
                v4 = _repeat_head(vv, kvh).astype(BF16)
                _, p, _ = _attn_probs(q[:, kvh * 256:(kvh + 1) * 256], k4, bias_ref[kvh], _row_sinks(kvh, sink_ref), no_past, masks)
                pv = jnp.dot(p.astype(BF16), v4, preferred_element_type=F32)
                outs.append(_unstack_groups(pv, masks))
            attn = jnp.concatenate(outs, axis=1)
            o_ref[pl.ds(r0, BLOCK), 0:ATTN_WIDTH] = (attn * _silu(ga)).astype(BF16)
            return carry

        lax.fori_loop(0, nblk, block, 0, unroll=True)

        for g in range(POOL_GROUPS):
            cu = COL_U + g * POOL_GC
            cg = COL_GB + g * POOL_GC
            halo_u = jnp.where(i == 0, 0.0, halo_ref[BLOCK - POOL_HALO:BLOCK, cu:cu + POOL_GC])
            u_ext = jnp.concatenate([halo_u, main_ref[:, cu:cu + POOL_GC]], axis=0)
            pooled = _pool_forward(u_ext, g, t0)
            y = jnp.dot(pooled.astype(BF16), pw_ref[g].astype(BF16), preferred_element_type=F32)
            y = y * ps_ref[:, g * POOL_GC:(g + 1) * POOL_GC]
            o_ref[:, ATTN_WIDTH + g * POOL_GC:ATTN_WIDTH + (g + 1) * POOL_GC] =(y * _silu(main_ref[:, cg:cg + POOL_GC])).astype(BF16)

        xo_ref[...] = _project_out(o_ref[...], w_ref, x_ref, p_ref, y_ref)

    row = pl.BlockSpec((tq, D_MODEL), lambda i: (i, 0))
    return _fused_call(
        body, comm, (proj, proj, sinks, pool_w, pool_scale, w_out, x_in, post), name="layer0_fwd", grid=(t // tq,),
        out_shape=[jax.ShapeDtypeStruct((t, D_MODEL), BF16), jax.ShapeDtypeStruct((t, D_MODEL), F32),
                   jax.ShapeDtypeStruct((t, D_MODEL), F32)],
        in_specs=[pl.BlockSpec((tq, EVEN_IN), lambda i: (i, 0)),
                  pl.BlockSpec((BLOCK, EVEN_IN), lambda i: (jnp.maximum(i * nblk - 1, 0), 0)),
                  pl.BlockSpec(memory_space=pltpu.SMEM),
                  _const_spec((POOL_GROUPS, POOL_GC, POOL_GC)), _const_spec((1, POOL_WIDTH)),
                  _const_spec((D_MODEL, D_MODEL)), row, _const_spec((1, D_MODEL))],
        out_specs=[row, row, row],
        scratch_shapes=[pltpu.VMEM((tq + BLOCK, 256), F32)] + ATTN_CONSTS,
        params=_params("arbitrary"))


def _layer0_bwd(proj, gy, y, mix, w_out, post, sinks, pool_w, pool_scale, comm=None, tq=512):
    t = proj.shape[0]
    nt = t // tq
    nblk = tq // BLOCK

    def body(main_ref, halo_ref, next_ref, gy_ref, gyn_ref, y_ref, yn_ref, mix_ref, wo_ref, po_ref,
             sink_ref, pw_ref, ps_ref,
             o_ref, dsk_ref, dpw_ref, dps_ref, dwo16_ref, dpo_ref,
             kv_ref, dkv_ref, carry_ref, bias_ref, dwo_ref, dmix_ref):
        i = pl.program_id(0)
        ii = nt - 1 - i
        t0 = ii * tq
        masks = _group_masks()

        @pl.when(i == 0)
        def _():
            _fill_attn_bias(bias_ref)
            dsk_ref[...] = jnp.zeros_like(dsk_ref)
            dpw_ref[...] = jnp.zeros_like(dpw_ref)
            dps_ref[...] = jnp.zeros_like(dps_ref)
            carry_ref[...] = jnp.zeros_like(carry_ref)

        dmix_ref[...] = _post_bwd_rows(jnp.concatenate([gy_ref[...], gyn_ref[...]], axis=0),
                                       jnp.concatenate([y_ref[...], yn_ref[...]], axis=0), mix_ref[...], tq,
                                       i == 0, i == nt - 1, po_ref, wo_ref, dwo_ref, dwo16_ref, dpo_ref)
        dm_ref = dmix_ref.at[pl.ds(0, tq)]
        dmn_ref = dmix_ref.at[pl.ds(tq, POOL_HALO)]

        kv_ref[0:BLOCK, :] = halo_ref[:, COL_K:COL_K + 256]
        kv_ref[BLOCK:, :] = main_ref[:, COL_K:COL_K + 256]
        dkv_ref[0:tq, :] = jnp.zeros((tq, 256), F32)
        dkv_ref[tq:, :] = carry_ref[...]

        def block(jb, carry):
            r0 = pl.multiple_of(jb * BLOCK, BLOCK)
            no_past = t0 + r0 == 0
            q = main_ref[pl.ds(r0, BLOCK), COL_Q:COL_Q + ATTN_WIDTH]
            ga = main_ref[pl.ds(r0, BLOCK), COL_GA:COL_GA + ATTN_WIDTH]
            dya = dm_ref[pl.ds(r0, BLOCK), 0:ATTN_WIDTH]
            kk = kv_ref[pl.ds(r0, 2 * BLOCK), 0:128]
            vv = kv_ref[pl.ds(r0, 2 * BLOCK), 128:256]
            silu_ga, dsilu_ga = _silu_and_grad(ga)
            do = dya * silu_ga
            first = lax.broadcasted_iota(jnp.int32, (2 * BLOCK, 128), 1) < HEAD_DIM
            attn, dq, dk, dv = [], [], [], []
            for kvh in range(KV_HEADS):
                k4 = _repeat_head(kk, kvh).astype(BF16)
                v4 = _repeat_head(vv, kvh).astype(BF16)
                qs, p, ps = _attn_probs(q[:, kvh * 256:(kvh + 1) * 256], k4, bias_ref[kvh], _row_sinks(kvh, sink_ref), no_past, masks)
                pb = p.astype(BF16)
                o_k = _unstack_groups(jnp.dot(pb, v4, preferred_element_type=F32), masks)
                do_k = do[:, kvh * 256:(kvh + 1) * 256]
                dos = _stack_groups(do_k, masks).astype(BF16)
                prod = do_k * o_k
                delta = jnp.concatenate([jnp.sum(jnp.where(m, prod, 0.0), axis=-1, keepdims=True) for m in masks], axis=0)
                dp = lax.dot_general(dos, v4, NT, preferred_element_type=F32)
                ds = (p * (dp - delta)).astype(BF16)
                sink_term = ps * delta
                for g in range(GROUP):
                    h = kvh * GROUP + g
                    dsk_ref[h:h + 1, :] -= jnp.sum(sink_term[g * BLOCK:(g + 1) * BLOCK], keepdims=True)
                dq.append(_unstack_groups(jnp.dot(ds, k4, preferred_element_type=F32), masks) * (HEAD_DIM ** -0.5))
                dk.append(_fold_head(lax.dot_general(ds, qs, TN, preferred_element_type=F32)))
                dv.append(_fold_head(lax.dot_general(pb, dos, TN, preferred_element_type=F32)))
                attn.append(o_k)
            o_ref[pl.ds(r0, BLOCK), COL_Q:COL_Q + ATTN_WIDTH] = jnp.concatenate(dq, axis=1).astype(BF16)
            o_all = jnp.concatenate(attn, axis=1)
            o_ref[pl.ds(r0, BLOCK), COL_GA:COL_GA + ATTN_WIDTH] = (dya * o_all * dsilu_ga).astype(BF16)
            dkv = jnp.concatenate([jnp.where(first, dk[0], dk[1]), jnp.where(first, dv[0], dv[1])], axis=1)
            dkv_ref[pl.ds(r0, 2 * BLOCK), :] += dkv
            return carry

        lax.fori_loop(0, nblk, block, 0, unroll=True)
        carry_ref[...] = dkv_ref[0:BLOCK, :]
        o_ref[:, COL_K:COL_K + 256] = dkv_ref[BLOCK:, :].astype(BF16)

        last = ii == nt - 1
        for g in range(POOL_GROUPS):
            cu = COL_U + g * POOL_GC
            cg = COL_GB + g * POOL_GC
            cm = ATTN_WIDTH + g * POOL_GC
            pw = pw_ref[g].astype(BF16)
            sc = ps_ref[:, g * POOL_GC:(g + 1) * POOL_GC]
            halo_u = jnp.where(ii == 0, 0.0, halo_ref[BLOCK - POOL_HALO:BLOCK, cu:cu + POOL_GC])
            u_ext = jnp.concatenate([halo_u, main_ref[:, cu:cu + POOL_GC]], axis=0)
            pooled = _pool_forward(u_ext, g, t0).astype(BF16)
            y_raw = jnp.dot(pooled, pw, preferred_element_type=F32)
            gb = main_ref[:, cg:cg + POOL_GC]
            dyb = dm_ref[:, cm:cm + POOL_GC]
            silu_gb, dsilu_gb = _silu_and_grad(gb)
            dypool = dyb * silu_gb
            dps_ref[:, g * POOL_GC:(g + 1) * POOL_GC] += jnp.sum(dypool * y_raw, axis=0, keepdims=True)
            o_ref[:, cg:cg + POOL_GC] = (dyb * (y_raw * sc) * dsilu_gb).astype(BF16)
            dyraw = dypool * sc
            dyraw_n = jnp.where(last, 0.0, dmn_ref[:, cm:cm + POOL_GC] * _silu(next_ref[:, cg:cg + POOL_GC]) * sc)
            dpw_ref[g * POOL_GC:(g + 1) * POOL_GC, :] += lax.dot_general(pooled, dyraw.astype(BF16), TN,
                                                                         preferred_element_type=F32)
            dyraw_ext = jnp.concatenate([dyraw, dyraw_n], axis=0).astype(BF16)
            dpooled = lax.dot_general(dyraw_ext, pw, NT, preferred_element_type=F32)
            w = 2 << g
            tt = t0 + lax.broadcasted_iota(jnp.int32, (tq + POOL_HALO, 1), 0)
            s = dpooled / jnp.minimum(tt + 1, w).astype(F32)
            for step in range(g + 1):
                s = s + pltpu.roll(s, tq + POOL_HALO - (1 << step), 0)
            o_ref[:, cu:cu + POOL_GC] = (s[0:tq] - dpooled[0:tq]).astype(BF16)

    rev = lambda i: nt - 1 - i
    nxt = lambda i: (jnp.minimum((rev(i) + 1) * (tq // POOL_HALO), t // POOL_HALO - 1), 0)
    row = pl.BlockSpec((tq, D_MODEL), lambda i: (rev(i), 0))
    nxt_row = pl.BlockSpec((POOL_HALO, D_MODEL), nxt)
    square = _const_spec((D_MODEL, D_MODEL))
    return _fused_call(
        body, comm, (proj, proj, proj, gy, gy, y, y, mix, w_out, post, sinks, pool_w, pool_scale),
        name="layer0_bwd", grid=(nt,),
        out_shape=[jax.ShapeDtypeStruct((t, EVEN_IN), BF16), jax.ShapeDtypeStruct((8, 128), F32),
                   jax.ShapeDtypeStruct((POOL_GROUPS * POOL_GC, POOL_GC), F32), jax.ShapeDtypeStruct((1, POOL_WIDTH), F32),
                   jax.ShapeDtypeStruct((D_MODEL, D_MODEL), BF16), jax.ShapeDtypeStruct((1, D_MODEL), F32)],
        in_specs=[pl.BlockSpec((tq, EVEN_IN), lambda i: (rev(i), 0)),
                  pl.BlockSpec((BLOCK, EVEN_IN), lambda i: (jnp.maximum(rev(i) * nblk - 1, 0), 0)),
                  pl.BlockSpec((POOL_HALO, EVEN_IN), nxt),
                  row, nxt_row, row, nxt_row, row, square, _const_spec((1, D_MODEL)),
                  pl.BlockSpec(memory_space=pltpu.SMEM),
                  _const_spec((POOL_GROUPS, POOL_GC, POOL_GC)), _const_spec((1, POOL_WIDTH))],
        out_specs=[pl.BlockSpec((tq, EVEN_IN), lambda i: (rev(i), 0)),
                   pl.BlockSpec((8, 128), lambda i: (0, 0)),
                   pl.BlockSpec((POOL_GROUPS * POOL_GC, POOL_GC), lambda i: (0, 0)),
                   pl.BlockSpec((1, POOL_WIDTH), lambda i: (0, 0)), square, pl.BlockSpec((1, D_MODEL), lambda i: (0, 0))],
        scratch_shapes=[pltpu.VMEM((tq + BLOCK, 256), F32), pltpu.VMEM((tq + BLOCK, 256), F32),
                        pltpu.VMEM((BLOCK, 256), F32)] + ATTN_CONSTS
        + [pltpu.VMEM((D_MODEL, D_MODEL), F32), pltpu.VMEM((tq + POOL_HALO, D_MODEL), F32)],
        params=_params("arbitrary"))


CONV_RC = 32
CONV_CC = 128
CONV_CHAINS = 4
CONV_UNROLL = 2


def _fill_shifted(s_ref, rows):
    for b in range(1, 8):
        s_ref[b, 0:rows - 8, :] = s_ref[0, b:b + rows - 8, :]


def _tap_blocks(s_ref, r, cols, lead):
    for b in range(8):
        taps = [(a, 8 * a + b - lead) for a in range(5) if 0 <= 8 * a + b - lead < CONV_K]
        span = 8 * max(a for a, _ in taps) + CONV_RC
        blk = s_ref[b, pl.ds(r, span), cols]
        for a, k in taps:
            yield k, blk[8 * a:8 * a + CONV_RC]


def _conv_taps(s_ref, w_ref, r, cols, lead, reverse):
    accs = [None] * CONV_CHAINS
    for n, (k, blk) in enumerate(_tap_blocks(s_ref, r, cols, lead)):
        kw = CONV_K - 1 - k if reverse else k
        term = blk * w_ref[kw:kw + 1, cols]
        accs[n % CONV_CHAINS] = term if accs[n % CONV_CHAINS] is None else accs[n % CONV_CHAINS] + term
    while len(accs) > 1:
        accs = [a + b for a, b in zip(accs[0::2], accs[1::2])]
    return accs[0]


def _layer_norm_fwd(cf, lng, lnb):
    mu = jnp.mean(cf, axis=-1, keepdims=True)
    xc = cf - mu
    rstd = lax.rsqrt(jnp.mean(xc * xc, axis=-1, keepdims=True) + EPS)
    chat = xc * rstd
    return chat, rstd, chat * lng + lnb


def _layer1_fwd(proj, dw, dwb, lng, lnb, w_out, x_in, post, target, tt=256):
    t = proj.shape[0]
    lead = CONV_HALO - (CONV_K - 1)

    def body(main_ref, halo_ref, w_ref, b_ref, g_ref, lb_ref, wo_ref, x_ref, p_ref, t_ref,
             o_ref, c_ref, y_ref, dl_ref, l_ref, gs_ref):
        i = pl.program_id(0)
        hv = halo_ref[...]
        gs_ref[0, 0:CONV_HALO, :] = jnp.where(i == 0, 0.0, hv[:, GLU_A] * _sigmoid(hv[:, GLU_B]))
        gs_ref[0, CONV_HALO:CONV_HALO + tt, :] = main_ref[:, GLU_A] * _sigmoid(main_ref[:, GLU_B])
        _fill_shifted(gs_ref, tt + CONV_HALO)

        for c in range(D_MODEL // CONV_CC):
            cols = slice(c * CONV_CC, (c + 1) * CONV_CC)

            def chunk(j, carry):
                r = pl.multiple_of(j * CONV_RC, CONV_RC)
                c_ref[pl.ds(r, CONV_RC), cols] = _conv_taps(gs_ref, w_ref, r, cols, lead, False) + b_ref[:, cols]
                return carry
            lax.fori_loop(0, tt // CONV_RC, chunk, 0, unroll=CONV_UNROLL)

        _, _, cn = _layer_norm_fwd(c_ref[...], g_ref[...], lb_ref[...])
        o_ref[...] = (_silu(cn) * _silu(main_ref[:, GATE])).astype(BF16)

        d = _project_out(o_ref[...], wo_ref, x_ref, p_ref, y_ref) - t_ref[...]
        dl_ref[...] = d * (1.0 / D_MODEL)

        @pl.when(i == 0)
        def _():
            l_ref[...] = jnp.zeros_like(l_ref)

        l_ref[...] += 0.5 * jnp.sum(jnp.mean(d * d, axis=-1, keepdims=True))

    vec = _const_spec((1, D_MODEL))
    row = pl.BlockSpec((tt, D_MODEL), lambda i: (i, 0))
    f32_rows = jax.ShapeDtypeStruct((t, D_MODEL), F32)
    return pl.pallas_call(
        body, name="layer1_fwd", grid=(t // tt,),
        out_shape=[jax.ShapeDtypeStruct((t, D_MODEL), BF16), f32_rows, f32_rows, f32_rows,
                   jax.ShapeDtypeStruct((8, 128), F32)],
        in_specs=[pl.BlockSpec((tt, 3 * D_MODEL), lambda i: (i, 0)),
                  pl.BlockSpec((CONV_HALO, 3 * D_MODEL), lambda i: (jnp.maximum(i * (tt // CONV_HALO) - 1, 0), 0)),
                  _const_spec((CONV_K, D_MODEL)), vec, vec, vec,
                  _const_spec((D_MODEL, D_MODEL)), row, vec, row],
        out_specs=[row, row, row, row, pl.BlockSpec((8, 128), lambda i: (0, 0))],
        scratch_shapes=[pltpu.VMEM((8, tt + CONV_HALO, D_MODEL), F32)],
        compiler_params=_params("arbitrary"),
    )(proj, proj, dw, dwb, lng, lnb, w_out, x_in, post, target)


def _layer1_bwd(proj, cf, gy, y, z, w_out, post, dw, lng, lnb, comm=None, tt=256):
    t = proj.shape[0]
    nt = t // tt
    te = tt + CONV_HALO

    def body(main_ref, next_ref, cf_ref, cfn_ref, gy_ref, gyn_ref, y_ref, yn_ref, z_ref, wo_ref, po_ref,
             w_ref, g_ref, lb_ref,
             o_ref, ddw_ref, ddb_ref, dg_ref, dlb_ref, dwo16_ref, dpo_ref, ds_ref, glu_ref, sb_ref, dwo_ref):
        i = pl.program_id(0)

        @pl.when(i == 0)
        def _():
            ddw_ref[...] = jnp.zeros_like(ddw_ref)
            ddb_ref[...] = jnp.zeros_like(ddb_ref)
            dg_ref[...] = jnp.zeros_like(dg_ref)
            dlb_ref[...] = jnp.zeros_like(dlb_ref)

        dzv = _post_bwd_rows(jnp.concatenate([gy_ref[...], gyn_ref[...]], axis=0),
                             jnp.concatenate([y_ref[...], yn_ref[...]], axis=0), z_ref[...], tt, i == 0, i == nt - 1,
                             po_ref, wo_ref, dwo_ref, dwo16_ref, dpo_ref)
        dzv = jnp.concatenate([dzv[0:tt], jnp.where(i < nt - 1, dzv[tt:], 0.0)], axis=0)
        lng = g_ref[...]
        chat, rstd, cn = _layer_norm_fwd(jnp.concatenate([cf_ref[...], cfn_ref[...]], axis=0), lng, lb_ref[...])
        gate = jnp.concatenate([main_ref[:, GATE], next_ref[:, GATE]], axis=0)
        silu_cn, dsilu_cn = _silu_and_grad(cn)
        silu_gate, dsilu_gate = _silu_and_grad(gate)
        o_ref[:, GATE] = (dzv * silu_cn * dsilu_gate)[0:tt].astype(BF16)
        dcn = dzv * silu_gate * dsilu_cn
        dg_ref[...] += jnp.sum((dcn * chat)[0:tt], axis=0, keepdims=True)
        dlb_ref[...] += jnp.sum(dcn[0:tt], axis=0, keepdims=True)
        dchat = dcn * lng
        dcf = rstd * (dchat - jnp.mean(dchat, axis=-1, keepdims=True) - chat * jnp.mean(dchat * chat, axis=-1, keepdims=True))
        ddb_ref[...] += jnp.sum(dcf[0:tt], axis=0, keepdims=True)
        ds_ref[0, 0:te, :] = dcf
        ds_ref[0, te:, :] = jnp.zeros((8, D_MODEL), F32)
        _fill_shifted(ds_ref, te + 8)
        sb_ref[...] = _sigmoid(main_ref[:, GLU_B])
        glu_ref[...] = main_ref[:, GLU_A] * sb_ref[...]

        for c in range(D_MODEL // CONV_CC):
            cols = slice(c * CONV_CC, (c + 1) * CONV_CC)
            gcols = slice(D_MODEL + c * CONV_CC, D_MODEL + (c + 1) * CONV_CC)

            def chunk(j, carry):
                r = pl.multiple_of(j * CONV_RC, CONV_RC)
                dglu = _conv_taps(ds_ref, w_ref, r, cols, 0, True)
                sb = sb_ref[pl.ds(r, CONV_RC), cols]
                o_ref[pl.ds(r, CONV_RC), cols] = (dglu * sb).astype(BF16)
                o_ref[pl.ds(r, CONV_RC), gcols] = (dglu * glu_ref[pl.ds(r, CONV_RC), cols] * (1.0 - sb)).astype(BF16)
                return carry
            lax.fori_loop(0, tt // CONV_RC, chunk, 0, unroll=CONV_UNROLL)

            def taps(j, accs):
                r = pl.multiple_of(j * CONV_RC, CONV_RC)
                gl = glu_ref[pl.ds(r, CONV_RC), cols]
                new = list(accs)
                for m, blk in _tap_blocks(ds_ref, r, cols, 0):
                    prod = blk * gl
                    part = prod[0:8]
                    for q in range(1, CONV_RC // 8):
                        part = part + prod[8 * q:8 * q + 8]
                    new[m] = new[m] + part
                return tuple(new)
            accs = lax.fori_loop(0, tt // CONV_RC, taps, tuple(jnp.zeros((8, CONV_CC), F32) for _ in range(CONV_K)))
            for m in range(CONV_K):
                k = CONV_K - 1 - m
                ddw_ref[k:k + 1, cols] += jnp.sum(accs[m], axis=0, keepdims=True)

    vec = _const_spec((1, D_MODEL))
    vec_out = pl.BlockSpec((1, D_MODEL), lambda i: (0, 0))
    row = pl.BlockSpec((tt, D_MODEL), lambda i: (i, 0))
    nxt = lambda i: (jnp.minimum((i + 1) * (tt // CONV_HALO), t // CONV_HALO - 1), 0)
    nxt_row = pl.BlockSpec((CONV_HALO, D_MODEL), nxt)
    vec_f32 = jax.ShapeDtypeStruct((1, D_MODEL), F32)
    square = _const_spec((D_MODEL, D_MODEL))
    return _fused_call(
        body, comm, (proj, proj, cf, cf, gy, gy, y, y, z, w_out, post, dw, lng, lnb), name="layer1_bwd", grid=(nt,),
        out_shape=[jax.ShapeDtypeStruct((t, 3 * D_MODEL), BF16), jax.ShapeDtypeStruct((CONV_K, D_MODEL), F32),
                   vec_f32, vec_f32, vec_f32, jax.ShapeDtypeStruct((D_MODEL, D_MODEL), BF16), vec_f32],
        in_specs=[pl.BlockSpec((tt, 3 * D_MODEL), lambda i: (i, 0)),
                  pl.BlockSpec((CONV_HALO, 3 * D_MODEL), nxt),
                  row, nxt_row, row, nxt_row, row, nxt_row, row, square, vec,
                  _const_spec((CONV_K, D_MODEL)), vec, vec],
        out_specs=[pl.BlockSpec((tt, 3 * D_MODEL), lambda i: (i, 0)),
                   pl.BlockSpec((CONV_K, D_MODEL), lambda i: (0, 0)), vec_out, vec_out, vec_out, square, vec_out],
        scratch_shapes=[pltpu.VMEM((8, te + 8, D_MODEL), F32), pltpu.VMEM((tt, D_MODEL), F32),
                        pltpu.VMEM((tt, D_MODEL), F32), pltpu.VMEM((D_MODEL, D_MODEL), F32)],
        params=_params("arbitrary"))


def _piece_sum(parts, place, name):
    r = parts[0][0].shape[1]

    def body(p_ref, *refs):
        o_ref = refs[-1]
        acc = refs[0][0].astype(F32)
        for part in refs[1:-1]:
            acc = acc + part[0].astype(F32)
        o_ref[0] = acc

    blk = (1, r, D_MODEL)
    spec = lambda slot: pl.BlockSpec(blk, lambda j, p_ref: (slot(p_ref), 0, 0))
    return pl.pallas_call(
        body, name=name,
        grid_spec=pltpu.PrefetchScalarGridSpec(
            num_scalar_prefetch=1, grid=(1,),
            in_specs=[spec(slot) for _, slot in parts],
            out_specs=pl.BlockSpec(blk, lambda j, p_ref: (p_ref[1], 0, 0))),
        out_shape=jax.ShapeDtypeStruct((2, r, D_MODEL), F32),
        compiler_params=_params("arbitrary"),
    )(place, *[a for a, _ in parts])


def _direct_parts(own, recv):
    peer = lambda m: (lambda p: p[0] ^ m)
    return [(own, peer(0))] + [(recv, peer(m)) for m in range(1, N_DEV)]


def _share_with_sibling(halves, name):
    n = len(halves)

    def body(*refs):
        outs = refs[n:2 * n]
        send_sems, recv_sems = refs[2 * n:]
        x, y, c, _ = _place()
        send = [pltpu.make_async_remote_copy(
            src_ref=outs[t].at[c], dst_ref=outs[t].at[c], send_sem=send_sems.at[t], recv_sem=recv_sems.at[t],
            device_id=(x, y, 1 - c), device_id_type=MESH_ID) for t in range(n)]
        recv = [pltpu.make_async_remote_copy(
            src_ref=outs[t].at[c], dst_ref=outs[t].at[1 - c], send_sem=send_sems.at[t], recv_sem=recv_sems.at[t],
            device_id=(x, y, 1 - c), device_id_type=MESH_ID) for t in range(n)]
        for cp in send:
            cp.start()
        for cp in recv:
            cp.wait_recv()
        for cp in send:
            cp.wait_send()

    return pl.pallas_call(
        body, name=name,
        out_shape=[jax.ShapeDtypeStruct(h.shape, h.dtype) for h in halves],
        in_specs=[ANY] * n, out_specs=[ANY] * n,
        input_output_aliases={t: t for t in range(n)},
        scratch_shapes=[pltpu.SemaphoreType.DMA((n,)), pltpu.SemaphoreType.DMA((n,))],
    )(*halves)


def _sum8(parts, name):
    r = parts.shape[1]

    def body(p_ref, o_ref):
        acc = p_ref[0]
        for k in range(1, N_DEV):
            acc = acc + p_ref[k]
        o_ref[...] = acc

    return pl.pallas_call(
        body, name=name, out_shape=jax.ShapeDtypeStruct((r, 128), F32),
        in_specs=[pl.BlockSpec(memory_space=pltpu.VMEM)], out_specs=pl.BlockSpec(memory_space=pltpu.VMEM),
    )(parts)


def _adamw(w, g, m, v, name):
    shape = w.shape
    cols = shape[-1]
    rows = w.size // cols
    rt = 256 if rows % 256 == 0 else rows
    if w.ndim >= 3 and shape[-2] == 1:
        view, blk, at = (rows, 1, cols), (rows, 1, cols), (lambda i: (0, 0, 0))
    else:
        view, blk, at = (rows, cols), (rt, cols), (lambda i: (i, 0))

    def body(w_ref, g_ref, m_ref, v_ref, d_ref, nm_ref, nv_ref):
        gv = g_ref[...]
        mn = ADAM_B1 * m_ref[...] + (1.0 - ADAM_B1) * gv
        vn = ADAM_B2 * v_ref[...] + (1.0 - ADAM_B2) * (gv * gv)
        m_hat = mn / (1.0 - ADAM_B1 ** ADAM_STEP)
        v_hat = vn / (1.0 - ADAM_B2 ** ADAM_STEP)
        d_ref[...] = -ADAM_LR * (m_hat / (jnp.sqrt(v_hat) + ADAM_EPS) + ADAM_WD * w_ref[...])
        nm_ref[...] = mn
        nv_ref[...] = vn

    spec = pl.BlockSpec(blk, at)
    outs = pl.pallas_call(
        body, name=name, grid=(view[0] // blk[0],),
        out_shape=[jax.ShapeDtypeStruct(view, F32)] * 3,
        in_specs=[spec] * 4, out_specs=[spec] * 3,
        compiler_params=_params("parallel"),
    )(*[a.reshape(view) for a in (w, g, m, v)])
    return [o.reshape(shape) for o in outs]


SMALL_ROWS = 832


def _pack_small(g):
    parts = [g["loss"], g["pre1"].reshape(8, 128), g["post0"].reshape(8, 128),
             g["post1"].reshape(8, 128), g["sinks"], jnp.pad(g["pool_scale"].reshape(4, 128), ((0, 4), (0, 0))),
             g["pool_w"], g["dw"].reshape(248, 128), g["dwb"].reshape(8, 128), g["lng"].reshape(8, 128),
             g["lnb"].reshape(8, 128)]
    assert sum(p.shape[0] for p in parts) == SMALL_ROWS
    return jnp.concatenate(parts, axis=0)


def _unpack_small(s):
    out, r = {}, 0
    for key, rows, shape in (("loss", 8, (8, 128)), ("pre1", 8, (1, D_MODEL)), ("post", 16, (2, D_MODEL)),
                             ("sinks", 8, (8, 128)),
                             ("pool_scale", 4, (1, POOL_WIDTH)), ("pad", 4, (4, 128)), ("pool_w", 512, (1, 4, 128, 128)),
                             ("dw", 248, (CONV_K, D_MODEL)), ("dwb", 8, (1, D_MODEL)), ("lng", 8, (1, D_MODEL)),
                             ("lnb", 8, (1, D_MODEL))):
        out[key] = s[r:r + rows].reshape(shape)
        r += rows
    return out


def kernel(x, pre_norm, post_norm, a_w_in, a_sinks, b_pool_w, b_pool_scale, ab_w_out, c_w_in, c_dw_w, c_dw_b, c_ln_g, c_ln_b, c_w_out, loss_target, m_pre_norm, m_post_norm, m_a_w_in, m_a_sinks, m_b_pool_w, m_b_pool_scale, m_ab_w_out, m_c_w_in, m_c_dw_w, m_c_dw_b, m_c_ln_g, m_c_ln_b, m_c_w_out, v_pre_norm, v_post_norm, v_a_w_in, v_a_sinks, v_b_pool_w, v_b_pool_scale, v_ab_w_out, v_c_w_in, v_c_dw_w, v_c_dw_b, v_c_ln_g, v_c_ln_b, v_c_w_out):
    ix, iy = lax.axis_index("x"), lax.axis_index("y")
    chip_cols = (2 * ix + iy) * 256

    pad8 = lambda v: jnp.pad(v, ((0, -v.shape[0] % 8), (0, 0)))
    vec_shard = jnp.concatenate([pad8(c_dw_w.reshape(CONV_K, 256)), pad8(c_dw_b), pad8(c_ln_g), pad8(c_ln_b),
                                 jnp.zeros((8, 256), F32)], axis=0)
    x0, target = x[0], loss_target[0]
    pre0, pre1 = pre_norm[0:1], pre_norm[1:2]
    post0, post1 = post_norm[0:1], post_norm[1:2]
    pool_w = b_pool_w[0]

    (wa_t,) = _run_comm(_Gather([a_w_in[0].T.astype(BF16)], halve=True), "gather_a_w_in")
    wa_t = wa_t.reshape(EVEN_IN, D_MODEL)
    proj0, (w_ab,) = _norm_matmul(x0, pre0, wa_t, "proj0_fwd", comm=_Gather([ab_w_out[0].astype(BF16)], halve=True))
    w_ab = w_ab.reshape(D_MODEL, D_MODEL)
    (mix0, y0, x1), (wc_t, w_c, vecs) = _layer0_fwd(
        proj0, a_sinks, pool_w, b_pool_scale, w_ab, x0, post0,
        comm=_Gather([c_w_in[0].T.astype(BF16), c_w_out[0].astype(BF16), vec_shard], halve=True))
    wc_t = wc_t.reshape(3 * D_MODEL, D_MODEL)
    w_c = w_c.reshape(D_MODEL, D_MODEL)
    vecs = vecs.reshape(4, 64, 256).transpose(1, 0, 2).reshape(64, D_MODEL)
    dw, dwb, lng, lnb = vecs[0:CONV_K], vecs[32:33], vecs[40:41], vecs[48:49]
    proj1, _ = _norm_matmul(x1, pre1, wc_t, "proj1_fwd")
    z1, cf1, y1, g2, loss = _layer1_fwd(proj1, dw, dwb, lng, lnb, w_c, x1, post1, target)

    pieces = lambda m: m.reshape(N_DEV, m.shape[0] // N_DEV, D_MODEL)
    (dproj1, d_dw, d_dwb, d_lng, d_lnb, d_wc, d_post1), _ = _layer1_bwd(proj1, cf1, g2, y1, z1, w_c, post1, dw, lng, lnb)
    (g1, d_wct, d_pre1), (r_wc,) = _pre_bwd(dproj1, wc_t, x1, pre1, g2, "proj1_bwd", comm=_Scatter([pieces(d_wc)]))
    (dproj0, d_sinks, d_pw, d_ps, d_wab, d_post0), (r_wct,) = _layer0_bwd(
        proj0, g1, y0, mix0, w_ab, post0, a_sinks, pool_w, b_pool_scale, comm=_Scatter([pieces(d_wct)]))
    g = dict(loss=loss, pre1=d_pre1, post0=d_post0, post1=d_post1, sinks=d_sinks, pool_w=d_pw, pool_scale=d_ps,
             dw=d_dw, dwb=d_dwb, lng=d_lng, lnb=d_lnb)
    d_wat, (small8, r_wab) = _proj_dw(dproj0, x0, pre0, "proj0_dw",
                                      comm=_Comms(_Gather([_pack_small(g)], halve=False), _Scatter([pieces(d_wab)])))
    sent = _scatter_start(pieces(d_wat), "scatter_a_start")
    (gx, d_pre0), _ = _proj_dx(dproj0, wa_t, x0, pre0 + sent[4][0:1, 0:1], g1, "proj0_dx")
    sent_pre0 = _scatter_start(jnp.broadcast_to(d_pre0.reshape(1, 8, 128), (N_DEV, 8, 128)), "pre0_start")
    own_wat, r_wat = _scatter_wait(*sent[:4], d_pre0, "scatter_a_wait")

    ic = lax.axis_index("c")
    me = 4 * ix + 2 * iy + ic
    place = jnp.stack([me, ic]).astype(jnp.int32)
    parts = [_direct_parts(own_wat, r_wat), _direct_parts(pieces(d_wab), r_wab), _direct_parts(pieces(d_wct), r_wct),
             _direct_parts(pieces(d_wc), r_wc)]
    halves = [_piece_sum(p, place, f"grad_sum{t}") for t, p in enumerate(parts)]
    g_wa_t, g_wab, g_wc_t, g_wc = [h.reshape(2 * h.shape[1], D_MODEL) for h in _share_with_sibling(halves, "grad_share")]
    _, pre0_8 = _scatter_wait(*sent_pre0[:4], g_wc, "pre0_wait")
    pre0_8 = lax.dynamic_update_slice(pre0_8, d_pre0.reshape(1, 8, 128), (me, 0, 0))
    g_c_w_in = g_wc_t.T[None]
    g_ab_w_out = g_wab[None]
    g_c_w_out = g_wc[None]

    s = _unpack_small(_sum8(small8, "small_sum"))
    layer = lax.broadcasted_iota(jnp.int32, (2, D_MODEL), 0)
    g_pre = jnp.where(layer == 0, _sum8(pre0_8, "pre0_sum").reshape(1, D_MODEL), s["pre1"])
    g_post = s["post"]
    g_sinks = s["sinks"][:, 0].reshape(1, 8)
    g_pool_w, g_pool_scale = s["pool_w"], s["pool_scale"]
    g_dw = lax.dynamic_slice_in_dim(s["dw"], chip_cols, 256, axis=1).reshape(1, CONV_K, 1, 256)
    g_dwb = lax.dynamic_slice_in_dim(s["dwb"], chip_cols, 256, axis=1)
    g_lng = lax.dynamic_slice_in_dim(s["lng"], chip_cols, 256, axis=1)
    g_lnb = lax.dynamic_slice_in_dim(s["lnb"], chip_cols, 256, axis=1)

    turn = lambda a: jnp.swapaxes(a, 1, 2)
    a_w_in, m_a_w_in, v_a_w_in = turn(a_w_in), turn(m_a_w_in), turn(v_a_w_in)
    grads = [g_pre, g_post, g_wa_t[None], g_sinks, g_pool_w, g_pool_scale, g_ab_w_out, g_c_w_in, g_dw, g_dwb, g_lng,
             g_lnb, g_c_w_out]
    weights = [pre_norm, post_norm, a_w_in, a_sinks, b_pool_w, b_pool_scale, ab_w_out, c_w_in, c_dw_w, c_dw_b, c_ln_g,
               c_ln_b, c_w_out]
    moms = [m_pre_norm, m_post_norm, m_a_w_in, m_a_sinks, m_b_pool_w, m_b_pool_scale, m_ab_w_out, m_c_w_in, m_c_dw_w,
            m_c_dw_b, m_c_ln_g, m_c_ln_b, m_c_w_out]
    vars_ = [v_pre_norm, v_post_norm, v_a_w_in, v_a_sinks, v_b_pool_w, v_b_pool_scale, v_ab_w_out, v_c_w_in, v_c_dw_w,
             v_c_dw_b, v_c_ln_g, v_c_ln_b, v_c_w_out]
    deltas, new_m, new_v = [], [], []
    for k, (w, gr, m, v) in enumerate(zip(weights, grads, moms, vars_)):
        d, nm, nv = _adamw(w, gr, m, v, f"adamw{k}")
        deltas.append(d)
        new_m.append(nm)
        new_v.append(nv)
    for outs in (grads, deltas, new_m, new_v):
        outs[2] = turn(outs[2])
    return (s["loss"][0, 0], gx[None], *grads, *deltas, *new_m, *new_v)
```

```python
import jax
import jax.numpy as jnp
from jax import lax
from jax.experimental import pallas as pl
from jax.experimental.pallas import tpu as pltpu

F32 = jnp.float32
BF16 = jnp.bfloat16

D_MODEL = 1024
EPS = 1e-6
NEG = -1e30
HEAD_DIM = 64
GROUP = 4
KV_HEADS = 2
BLOCK = 128
EVEN_IN = 2304
ATTN_WIDTH = 512
POOL_WIDTH = 512
COL_Q, COL_K, COL_GA, COL_U, COL_GB = 0, 512, 768, 1280, 1792
POOL_GROUPS = 4
POOL_GC = 128
POOL_HALO = 16
CONV_K = 31
CONV_HALO = 32
GLU_A = slice(0, D_MODEL)
GLU_B = slice(D_MODEL, 2 * D_MODEL)
GATE = slice(2 * D_MODEL, 3 * D_MODEL)
N_DEV = 8

ADAM_LR = 0.001
ADAM_B1 = 0.9
ADAM_B2 = 0.999
ADAM_EPS = 1e-08
ADAM_WD = 0.01
ADAM_STEP = 10

VMEM_LIMIT_BYTES = 56 * 1024 * 1024

NT = (((1,), (1,)), ((), ()))
TN = (((0,), (0,)), ((), ()))
MESH_ID = pl.DeviceIdType.MESH


def _params(*sem):
    return pltpu.CompilerParams(dimension_semantics=sem, vmem_limit_bytes=VMEM_LIMIT_BYTES)


def _const_spec(shape):
    nd = len(shape)
    return pl.BlockSpec(shape, lambda *_: (0,) * nd, pipeline_mode=pl.Buffered(1))


def _sigmoid(v):
    return 0.5 * jnp.tanh(0.5 * v) + 0.5


def _silu(v):
    h = 0.5 * v
    return h * jnp.tanh(h) + h


def _silu_and_grad(v):
    s = _sigmoid(v)
    silu = v * s
    return silu, s + silu * (1.0 - s)


ANY = pl.BlockSpec(memory_space=pl.ANY)


def _place():
    x, y, c = lax.axis_index("x"), lax.axis_index("y"), lax.axis_index("c")
    chips = [(1 - x, y), (x, 1 - y), (1 - x, 1 - y)]
    return x, y, c, chips


class _Gather:
    def __init__(self, blocks, halve):
        self.ins = list(blocks)
        self.halve = halve
        self.n = n = len(blocks)
        self.shapes = [((b.shape[0] // 2) if halve else b.shape[0], b.shape[1]) for b in blocks]
        self.out_shape = [jax.ShapeDtypeStruct((N_DEV, r, cols), b.dtype) for (r, cols), b in zip(self.shapes, blocks)]
        self.scratch = [pltpu.SemaphoreType.DMA((7 * n,)), pltpu.SemaphoreType.DMA((7 * n,)),
                        pltpu.SemaphoreType.DMA((n,))]

    def _copies(self, ins, outs, sems):
        send_sems, recv_sems, local_sems = sems
        x, y, c, chips = _place()
        me, sibling = (x, y, c), (x, y, 1 - c)

        def piece(t, px, py, pc):
            return outs[t].at[4 * px + 2 * py + pc]

        def own(t):
            return ins[t].at[pl.ds(c * self.shapes[t][0], self.shapes[t][0])] if self.halve else ins[t]

        def copy(t, k, block, to, src=None):
            return pltpu.make_async_remote_copy(
                src_ref=piece(t, *block) if src is None else src, dst_ref=piece(t, *block),
                send_sem=send_sems.at[7 * t + k], recv_sem=recv_sems.at[7 * t + k],
                device_id=to, device_id_type=MESH_ID)

        rng = range(self.n)
        return dict(
            mine=[pltpu.make_async_copy(own(t), piece(t, *me), local_sems.at[t]) for t in rng],
            first=[copy(t, 0, me, sibling, src=own(t)) for t in rng]
            + [copy(t, 1 + j, me, (*chip, c), src=own(t)) for t in rng for j, chip in enumerate(chips)],
            landed=[copy(t, 1 + j, (*chip, c), me) for j, chip in enumerate(chips) for t in rng],
            passed=[copy(t, 4 + j, (*chip, c), sibling) for j, chip in enumerate(chips) for t in rng],
            from_sibling=[copy(t, 0, sibling, me) for t in rng]
            + [copy(t, 4 + j, (*chip, 1 - c), me) for t in rng for j, chip in enumerate(chips)])

    def start(self, ins, outs, sems):
        d = self._copies(ins, outs, sems)
        for cp in d["mine"] + d["first"]:
            cp.start()

    def middle(self, ins, outs, sems):
        d = self._copies(ins, outs, sems)
        for got, fwd in zip(d["landed"], d["passed"]):
            got.wait_recv()
            fwd.start()

    def finish(self, ins, outs, sems):
        d = self._copies(ins, outs, sems)
        for cp in d["from_sibling"]:
            cp.wait_recv()
        for cp in d["first"] + d["passed"]:
            cp.wait_send()
        for cp in d["mine"]:
            cp.wait()


class _Scatter:
    def __init__(self, tensors):
        self.ins = list(tensors)
        self.n = n = len(tensors)
        self.out_shape = [jax.ShapeDtypeStruct(t.shape, t.dtype) for t in tensors]
        self.scratch = [pltpu.SemaphoreType.DMA((7 * n,)), pltpu.SemaphoreType.DMA((7 * n,))]

    def _copies(self, ins, outs, sems):
        send_sems, recv_sems = sems
        x, y, c, _ = _place()
        me = 4 * x + 2 * y + c
        sends, recvs = [], []
        for t in range(self.n):
            for m in range(1, N_DEV):
                px, py, pc = x ^ (m >> 2), y ^ ((m >> 1) & 1), c ^ (m & 1)
                q = 4 * px + 2 * py + pc
                sems_k = dict(send_sem=send_sems.at[7 * t + m - 1], recv_sem=recv_sems.at[7 * t + m - 1],
                              device_id=(px, py, pc), device_id_type=MESH_ID)
                sends.append(pltpu.make_async_remote_copy(src_ref=ins[t].at[q], dst_ref=outs[t].at[me], **sems_k))
                recvs.append(pltpu.make_async_remote_copy(src_ref=ins[t].at[me], dst_ref=outs[t].at[q], **sems_k))
        return sends, recvs

    def start(self, ins, outs, sems):
        for cp in self._copies(ins, outs, sems)[0]:
            cp.start()

    def middle(self, ins, outs, sems):
        pass

    def finish(self, ins, outs, sems):
        sends, recvs = self._copies(ins, outs, sems)
        for cp in recvs:
            cp.wait_recv()
        for cp in sends:
            cp.wait_send()


class _Comms:
    def __init__(self, *comms):
        self.comms = comms
        self.ins = [a for c in comms for a in c.ins]
        self.out_shape = [s for c in comms for s in c.out_shape]
        self.scratch = [s for c in comms for s in c.scratch]

    def _each(self, phase, ins, outs, sems):
        i = o = s = 0
        for c in self.comms:
            ni, no, ns = len(c.ins), len(c.out_shape), len(c.scratch)
            getattr(c, phase)(ins[i:i + ni], outs[o:o + no], sems[s:s + ns])
            i, o, s = i + ni, o + no, s + ns

    def start(self, ins, outs, sems):
        self._each("start", ins, outs, sems)

    def middle(self, ins, outs, sems):
        self._each("middle", ins, outs, sems)

    def finish(self, ins, outs, sems):
        self._each("finish", ins, outs, sems)


def _run_comm(comm, name):
    n = len(comm.ins)

    def body(*refs):
        parts = refs[:n], refs[n:2 * n], refs[2 * n:]
        comm.start(*parts)
        comm.middle(*parts)
        comm.finish(*parts)

    return pl.pallas_call(body, name=name, out_shape=comm.out_shape, in_specs=[ANY] * n, out_specs=[ANY] * n,
                          scratch_shapes=comm.scratch)(*comm.ins)


HBM_SPEC = pl.BlockSpec(memory_space=pltpu.HBM)
SEM_SPEC = pl.BlockSpec(memory_space=pltpu.SEMAPHORE)
DATAFLOW = pltpu.SideEffectType.DATAFLOW_SIDE_EFFECTING


def _scatter_copies(own_ref, land_ref, send_sems, recv_sems):
    x, y, c, _ = _place()
    me = 4 * x + 2 * y + c
    pairs = []
    for m in range(1, N_DEV):
        px, py, pc = x ^ (m >> 2), y ^ ((m >> 1) & 1), c ^ (m & 1)
        q = 4 * px + 2 * py + pc
        sems = dict(send_sem=send_sems.at[m - 1], recv_sem=recv_sems.at[m - 1], device_id=(px, py, pc),
                    device_id_type=MESH_ID)
        pairs.append((pltpu.make_async_remote_copy(src_ref=own_ref.at[q], dst_ref=land_ref.at[me], **sems),
                      pltpu.make_async_remote_copy(src_ref=own_ref.at[me], dst_ref=land_ref.at[q], **sems)))
    return pairs


def _scatter_start(own, name):
    def body(own_ref, land_ref, send_sems, recv_sems, own_thru, land_thru, token):
        for send, _ in _scatter_copies(own_ref, land_ref, send_sems, recv_sems):
            send.start()
        token[...] = jnp.zeros_like(token)

    buf = pltpu.HBM(own.shape, own.dtype)
    return pl.pallas_call(
        body, name=name,
        out_shape=(pltpu.SemaphoreType.DMA((N_DEV - 1,)), pltpu.SemaphoreType.DMA((N_DEV - 1,)), buf, buf,
                   jax.ShapeDtypeStruct((8, 128), F32)),
        in_specs=(HBM_SPEC, HBM_SPEC),
        out_specs=(SEM_SPEC, SEM_SPEC, HBM_SPEC, HBM_SPEC, pl.BlockSpec(memory_space=pltpu.VMEM)),
        input_output_aliases={0: 2, 1: 3},
        compiler_params=pltpu.CompilerParams(has_side_effects=DATAFLOW),
    )(pltpu.with_memory_space_constraint(own, pltpu.HBM),
      pltpu.with_memory_space_constraint(lax.empty(own.shape, own.dtype), pltpu.HBM))


def _scatter_wait(send_sems, recv_sems, own_thru, land_thru, after, name):
    def body(own_ref, land_ref, send_sems, recv_sems, after_ref, own_out, land_out):
        for send, recv in _scatter_copies(own_ref, land_ref, send_sems, recv_sems):
            send.wait_send()
            recv.wait_recv()

    buf = pltpu.HBM(own_thru.shape, own_thru.dtype)
    return pl.pallas_call(
        body, name=name, out_shape=(buf, buf),
        in_specs=(HBM_SPEC, HBM_SPEC, SEM_SPEC, SEM_SPEC, ANY), out_specs=(HBM_SPEC, HBM_SPEC),
        input_output_aliases={0: 0, 1: 1},
        compiler_params=pltpu.CompilerParams(has_side_effects=DATAFLOW),
    )(own_thru, land_thru, send_sems, recv_sems, after)


def _fused_call(body, comm, args, *, name, grid, out_shape, in_specs, out_specs, scratch_shapes=(), params):
    single = not isinstance(out_shape, (list, tuple))
    out_shape = [out_shape] if single else list(out_shape)
    out_specs = [out_specs] if single else list(out_specs)
    if comm is None:
        res = pl.pallas_call(body, name=name, grid=grid, out_shape=out_shape, in_specs=in_specs, out_specs=out_specs,
                             scratch_shapes=list(scratch_shapes), compiler_params=params)(*args)
        return (res[0] if single else res), []
    n_in, n_out, n_scr = len(in_specs), len(out_shape), len(scratch_shapes)
    c_in, c_out = len(comm.ins), len(comm.out_shape)
    steps = grid[0]

    def fused(*refs):
        pos = 0
        groups = []
        for size in (n_in, c_in, n_out, c_out, n_scr, len(comm.scratch)):
            groups.append(refs[pos:pos + size])
            pos += size
        ins, c_ins, outs, c_outs, scr, c_sems = groups
        i = pl.program_id(0)

        @pl.when(i == 0)
        def _():
            comm.start(c_ins, c_outs, c_sems)

        @pl.when(i == steps // 2)
        def _():
            comm.middle(c_ins, c_outs, c_sems)

        body(*ins, *outs, *scr)

        @pl.when(i == steps - 1)
        def _():
            comm.finish(c_ins, c_outs, c_sems)

    res = pl.pallas_call(
        fused, name=name, grid=grid, out_shape=out_shape + list(comm.out_shape),
        in_specs=list(in_specs) + [ANY] * c_in, out_specs=out_specs + [ANY] * c_out,
        scratch_shapes=list(scratch_shapes) + list(comm.scratch), compiler_params=params)(*args, *comm.ins)
    main = res[:n_out]
    return (main[0] if single else main), list(res[n_out:])


def _norm_matmul(x, gain, wt, name, comm=None, tm=1024):
    t, n = x.shape[0], wt.shape[0]

    def body(x_ref, g_ref, wt_ref, o_ref):
        xv = x_ref[...]
        r = lax.rsqrt(jnp.mean(xv * xv, axis=-1, keepdims=True) + EPS)
        h = (xv * r * g_ref[...]).astype(BF16)
        o_ref[...] = lax.dot_general(h, wt_ref[...], NT, preferred_element_type=F32)

    return _fused_call(
        body, comm, (x, gain, wt), name=name, grid=(t // tm,),
        out_shape=jax.ShapeDtypeStruct((t, n), F32),
        in_specs=[pl.BlockSpec((tm, D_MODEL), lambda i: (i, 0)), _const_spec((1, D_MODEL)), _const_spec((n, D_MODEL))],
        out_specs=pl.BlockSpec((tm, n), lambda i: (i, 0)),
        params=_params("arbitrary"))


def _project_out(a, w_ref, x_ref, p_ref, y_ref):
    y = jnp.dot(a, w_ref[...], preferred_element_type=F32)
    y_ref[...] = y
    ry = lax.rsqrt(jnp.mean(y * y, axis=-1, keepdims=True) + EPS)
    return x_ref[...] + (y * ry) * p_ref[...]


def _post_bwd_rows(g, y, a, n_own, first, last, p_ref, w_ref, dw_ref, dw16_ref, dp_ref):
    @pl.when(first)
    def _():
        dw_ref[...] = jnp.zeros_like(dw_ref)
        dp_ref[...] = jnp.zeros_like(dp_ref)

    ry = lax.rsqrt(jnp.mean(y * y, axis=-1, keepdims=True) + EPS)
    nv = y * ry
    dp_ref[...] += jnp.sum((g * nv)[0:n_own], axis=0, keepdims=True)
    dn = g * p_ref[...]
    dy = (ry * (dn - nv * jnp.mean(dn * nv, axis=-1, keepdims=True))).astype(BF16)
    dw_ref[...] += lax.dot_general(a, dy[0:n_own], TN, preferred_element_type=F32)

    @pl.when(last)
    def _():
        dw16_ref[...] = dw_ref[...].astype(BF16)

    return lax.dot_general(dy, w_ref[...], NT, preferred_element_type=F32)


def _pre_bwd(dproj, wt, x_in, pre, g, name, comm=None, tm=512):
    t, n = dproj.shape
    steps = t // tm

    def body(dp_ref, wt_ref, x_ref, pre_ref, g_ref, dx_ref, dwt16_ref, dpre_ref, dwt_ref):
        @pl.when(pl.program_id(0) == 0)
        def _():
            dwt_ref[...] = jnp.zeros_like(dwt_ref)
            dpre_ref[...] = jnp.zeros_like(dpre_ref)

        dpv = dp_ref[...]
        dh = jnp.dot(dpv, wt_ref[...], preferred_element_type=F32)
        xv = x_ref[...]
        r = lax.rsqrt(jnp.mean(xv * xv, axis=-1, keepdims=True) + EPS)
        xn = xv * r
        pv = pre_ref[...]
        dpre_ref[...] += jnp.sum(dh * xn, axis=0, keepdims=True)
        dxn = dh * pv
        dx_ref[...] = g_ref[...] + r * (dxn - xn * jnp.mean(dxn * xn, axis=-1, keepdims=True))
        h = (xn * pv).astype(BF16)
        dwt_ref[...] += lax.dot_general(dpv, h, TN, preferred_element_type=F32)

        @pl.when(pl.program_id(0) == steps - 1)
        def _():
            dwt16_ref[...] = dwt_ref[...].astype(BF16)

    row = pl.BlockSpec((tm, D_MODEL), lambda i: (i, 0))
    return _fused_call(
        body, comm, (dproj, wt, x_in, pre, g), name=name, grid=(steps,),
        out_shape=[jax.ShapeDtypeStruct((t, D_MODEL), F32), jax.ShapeDtypeStruct((n, D_MODEL), BF16),
                   jax.ShapeDtypeStruct((1, D_MODEL), F32)],
        in_specs=[pl.BlockSpec((tm, n), lambda i: (i, 0)), _const_spec((n, D_MODEL)), row, _const_spec((1, D_MODEL)), row],
        out_specs=[row, _const_spec((n, D_MODEL)), pl.BlockSpec((1, D_MODEL), lambda i: (0, 0))],
        scratch_shapes=[pltpu.VMEM((n, D_MODEL), F32)],
        params=_params("arbitrary"))


def _proj_dw(dproj, x_in, pre, name, comm=None, tm=1024):
    t, n = dproj.shape
    steps = t // tm

    def body(dp_ref, x_ref, pre_ref, dwt16_ref, dwt_ref):
        @pl.when(pl.program_id(0) == 0)
        def _():
            dwt_ref[...] = jnp.zeros_like(dwt_ref)

        xv = x_ref[...]
        r = lax.rsqrt(jnp.mean(xv * xv, axis=-1, keepdims=True) + EPS)
        h = (xv * r * pre_ref[...]).astype(BF16)
        dwt_ref[...] += lax.dot_general(dp_ref[...], h, TN, preferred_element_type=F32)

        @pl.when(pl.program_id(0) == steps - 1)
        def _():
            dwt16_ref[...] = dwt_ref[...].astype(BF16)

    return _fused_call(
        body, comm, (dproj, x_in, pre), name=name, grid=(steps,),
        out_shape=jax.ShapeDtypeStruct((n, D_MODEL), BF16),
        in_specs=[pl.BlockSpec((tm, n), lambda i: (i, 0)), pl.BlockSpec((tm, D_MODEL), lambda i: (i, 0)),
                  _const_spec((1, D_MODEL))],
        out_specs=pl.BlockSpec((n, D_MODEL), lambda i: (0, 0)),
        scratch_shapes=[pltpu.VMEM((n, D_MODEL), F32)],
        params=_params("arbitrary"))


def _proj_dx(dproj, wt, x_in, pre, g, name, comm=None, tm=512):
    t, n = dproj.shape

    def body(dp_ref, wt_ref, x_ref, pre_ref, g_ref, dx_ref, dpre_ref):
        @pl.when(pl.program_id(0) == 0)
        def _():
            dpre_ref[...] = jnp.zeros_like(dpre_ref)

        dh = jnp.dot(dp_ref[...], wt_ref[...], preferred_element_type=F32)
        xv = x_ref[...]
        r = lax.rsqrt(jnp.mean(xv * xv, axis=-1, keepdims=True) + EPS)
        xn = xv * r
        dpre_ref[...] += jnp.sum(dh * xn, axis=0, keepdims=True)
        dxn = dh * pre_ref[...]
        dx_ref[...] = g_ref[...] + r * (dxn - xn * jnp.mean(dxn * xn, axis=-1, keepdims=True))

    row = pl.BlockSpec((tm, D_MODEL), lambda i: (i, 0))
    return _fused_call(
        body, comm, (dproj, wt, x_in, pre, g), name=name, grid=(t // tm,),
        out_shape=[jax.ShapeDtypeStruct((t, D_MODEL), F32), jax.ShapeDtypeStruct((1, D_MODEL), F32)],
        in_specs=[pl.BlockSpec((tm, n), lambda i: (i, 0)), _const_spec((n, D_MODEL)), row, _const_spec((1, D_MODEL)), row],
        out_specs=[row, pl.BlockSpec((1, D_MODEL), lambda i: (0, 0))],
        params=_params("arbitrary"))


def _group_masks():
    lane = lax.broadcasted_iota(jnp.int32, (1, GROUP * HEAD_DIM), 1)
    return [lane // HEAD_DIM == g for g in range(GROUP)]


def _stack_groups(v, masks, scale=1.0):
    return jnp.concatenate([v * jnp.where(m, scale, 0.0) for m in masks], axis=0)


def _unstack_groups(v, masks):
    out = v[(GROUP - 1) * BLOCK:GROUP * BLOCK]
    for g in range(GROUP - 2, -1, -1):
        out = jnp.where(masks[g], v[g * BLOCK:(g + 1) * BLOCK], out)
    return out


def _repeat_head(kv2, kvh):
    first = lax.broadcasted_iota(jnp.int32, kv2.shape, 1) < HEAD_DIM
    rolled = pltpu.roll(kv2, HEAD_DIM, 1)
    one = jnp.where(first, kv2, rolled) if kvh == 0 else jnp.where(first, rolled, kv2)
    return jnp.concatenate([one, one], axis=1)


def _fold_head(v4):
    a = v4[:, 0:128] + v4[:, 128:256]
    return a + pltpu.roll(a, HEAD_DIM, 1)


ATTN_CONSTS = [pltpu.VMEM((KV_HEADS, GROUP * BLOCK, 2 * BLOCK), F32)]


def _fill_attn_bias(bias_ref):
    row = lax.broadcasted_iota(jnp.int32, (GROUP * BLOCK, 2 * BLOCK), 0)
    col = lax.broadcasted_iota(jnp.int32, (GROUP * BLOCK, 2 * BLOCK), 1)
    dist = (row % BLOCK) + BLOCK - col
    band = (dist >= 0) & (dist < BLOCK)
    rb = lax.broadcasted_iota(jnp.int32, (GROUP * BLOCK, 1), 0) // BLOCK
    for kvh in range(KV_HEADS):
        slope = jnp.zeros((GROUP * BLOCK, 1), F32)
        for g in range(GROUP):
            slope = jnp.where(rb == g, 2.0 ** (-(kvh * GROUP + g + 1)), slope)
        bias_ref[kvh] = jnp.where(band, -slope * dist.astype(F32), NEG)


def _row_sinks(kvh, sink_ref):
    rb = lax.broadcasted_iota(jnp.int32, (GROUP * BLOCK, 1), 0) // BLOCK
    sink = jnp.zeros((GROUP * BLOCK, 1), F32)
    for g in range(GROUP):
        sink = jnp.where(rb == g, sink_ref[0, kvh * GROUP + g], sink)
    return sink


def _attn_probs(qk, k4, bias, sink, no_past, masks):
    qs = _stack_groups(qk, masks, HEAD_DIM ** -0.5).astype(BF16)
    s = lax.dot_general(qs, k4, NT, preferred_element_type=F32) + bias
    s = jnp.concatenate([jnp.where(no_past, NEG, s[:, 0:BLOCK]), s[:, BLOCK:]], axis=1)
    mx = jnp.maximum(jnp.max(s, axis=-1, keepdims=True), sink)
    e = jnp.exp(s - mx)
    es = jnp.exp(sink - mx)
    inv = 1.0 / (jnp.sum(e, axis=-1, keepdims=True) + es)
    return qs, e * inv, es * inv


def _pool_forward(u_ext, g, t0):
    n = u_ext.shape[0] - POOL_HALO
    s = u_ext
    for step in range(g + 1):
        s = s + pltpu.roll(s, 1 << step, 0)
    w = 2 << g
    t = t0 + lax.broadcasted_iota(jnp.int32, (n, 1), 0)
    cnt = jnp.minimum(t + 1, w).astype(F32)
    return s[POOL_HALO:] / cnt - u_ext[POOL_HALO:]


def _layer0_fwd(proj, sinks, pool_w, pool_scale, w_out, x_in, post, comm=None, tq=512):
    t = proj.shape[0]
    nblk = tq // BLOCK

    def body(main_ref, halo_ref, sink_ref, pw_ref, ps_ref, w_ref, x_ref, p_ref, o_ref, y_ref, xo_ref, kv_ref, bias_ref):
        i = pl.program_id(0)
        t0 = i * tq
        masks = _group_masks()

        @pl.when(i == 0)
        def _():
            _fill_attn_bias(bias_ref)

        kv_ref[0:BLOCK, :] = halo_ref[:, COL_K:COL_K + 256]
        kv_ref[BLOCK:, :] = main_ref[:, COL_K:COL_K + 256]

        def block(jb, carry):
            r0 = pl.multiple_of(jb * BLOCK, BLOCK)
            no_past = t0 + r0 == 0
            q = main_ref[pl.ds(r0, BLOCK), COL_Q:COL_Q + ATTN_WIDTH]
            ga = main_ref[pl.ds(r0, BLOCK), COL_GA:COL_GA + ATTN_WIDTH]
            kk = kv_ref[pl.ds(r0, 2 * BLOCK), 0:128]
            vv = kv_ref[pl.ds(r0, 2 * BLOCK), 128:256]
            outs = []
            for kvh in range(KV_HEADS):
                k4 = _repeat_head(kk, kvh).astype(BF16)
                v4 = _repeat_head(vv, kvh).astype(BF16)
                _, p, _ = _attn_probs(q[:, kvh * 256:(kvh + 1) * 256], k4, bias_ref[kvh], _row_sinks(kvh, sink_ref), no_past, masks)
                pv = jnp.dot(p.astype(BF16), v4, preferred_element_type=F32)
                outs.append(_unstack_groups(pv, masks))
            attn = jnp.concatenate(outs, axis=1)
            o_ref[pl.ds(r0, BLOCK), 0:ATTN_WIDTH] = (attn * _silu(ga)).astype(BF16)
            return carry

        lax.fori_loop(0, nblk, block, 0, unroll=True)

        for g in range(POOL_GROUPS):
            cu = COL_U + g * POOL_GC
            cg = COL_GB + g * POOL_GC
            halo_u = jnp.where(i == 0, 0.0, halo_ref[BLOCK - POOL_HALO:BLOCK, cu:cu + POOL_GC])
            u_ext = jnp.concatenate([halo_u, main_ref[:, cu:cu + POOL_GC]], axis=0)
            pooled = _pool_forward(u_ext, g, t0)
            y = jnp.dot(pooled.astype(BF16), pw_ref[g].astype(BF16), preferred_element_type=F32)
            y = y * ps_ref[:, g * POOL_GC:(g + 1) * POOL_GC]
            o_ref[:, ATTN_WIDTH + g * POOL_GC:ATTN_WIDTH + (g + 1) * POOL_GC] =(y * _silu(main_ref[:, cg:cg + POOL_GC])).astype(BF16)

        xo_ref[...] = _project_out(o_ref[...], w_ref, x_ref, p_ref, y_ref)

    row = pl.BlockSpec((tq, D_MODEL), lambda i: (i, 0))
    return _fused_call(
        body, comm, (proj, proj, sinks, pool_w, pool_scale, w_out, x_in, post), name="layer0_fwd", grid=(t // tq,),
        out_shape=[jax.ShapeDtypeStruct((t, D_MODEL), BF16), jax.ShapeDtypeStruct((t, D_MODEL), F32),
                   jax.ShapeDtypeStruct((t, D_MODEL), F32)],
        in_specs=[pl.BlockSpec((tq, EVEN_IN), lambda i: (i, 0)),
                  pl.BlockSpec((BLOCK, EVEN_IN), lambda i: (jnp.maximum(i * nblk - 1, 0), 0)),
                  pl.BlockSpec(memory_space=pltpu.SMEM),
                  _const_spec((POOL_GROUPS, POOL_GC, POOL_GC)), _const_spec((1, POOL_WIDTH)),
                  _const_spec((D_MODEL, D_MODEL)), row, _const_spec((1, D_MODEL))],
        out_specs=[row, row, row],
        scratch_shapes=[pltpu.VMEM((tq + BLOCK, 256), F32)] + ATTN_CONSTS,
        params=_params("arbitrary"))


def _layer0_bwd(proj, gy, y, mix, w_out, post, sinks, pool_w, pool_scale, comm=None, tq=512):
    t = proj.shape[0]
    nt = t // tq
    nblk = tq // BLOCK

    def body(main_ref, halo_ref, next_ref, gy_ref, gyn_ref, y_ref, yn_ref, mix_ref, wo_ref, po_ref,
             sink_ref, pw_ref, ps_ref,
             o_ref, dsk_ref, dpw_ref, dps_ref, dwo16_ref, dpo_ref,
             kv_ref, dkv_ref, carry_ref, bias_ref, dwo_ref, dmix_ref):
        i = pl.program_id(0)
        ii = nt - 1 - i
        t0 = ii * tq
        masks = _group_masks()

        @pl.when(i == 0)
        def _():
            _fill_attn_bias(bias_ref)
            dsk_ref[...] = jnp.zeros_like(dsk_ref)
            dpw_ref[...] = jnp.zeros_like(dpw_ref)
            dps_ref[...] = jnp.zeros_like(dps_ref)
            carry_ref[...] = jnp.zeros_like(carry_ref)

        dmix_ref[...] = _post_bwd_rows(jnp.concatenate([gy_ref[...], gyn_ref[...]], axis=0),
                                       jnp.concatenate([y_ref[...], yn_ref[...]], axis=0), mix_ref[...], tq,
                                       i == 0, i == nt - 1, po_ref, wo_ref, dwo_ref, dwo16_ref, dpo_ref)
        dm_ref = dmix_ref.at[pl.ds(0, tq)]
        dmn_ref = dmix_ref.at[pl.ds(tq, POOL_HALO)]

        kv_ref[0:BLOCK, :] = halo_ref[:, COL_K:COL_K + 256]
        kv_ref[BLOCK:, :] = main_ref[:, COL_K:COL_K + 256]
        dkv_ref[0:tq, :] = jnp.zeros((tq, 256), F32)
        dkv_ref[tq:, :] = carry_ref[...]

        def block(jb, carry):
            r0 = pl.multiple_of(jb * BLOCK, BLOCK)
            no_past = t0 + r0 == 0
            q = main_ref[pl.ds(r0, BLOCK), COL_Q:COL_Q + ATTN_WIDTH]
            ga = main_ref[pl.ds(r0, BLOCK), COL_GA:COL_GA + ATTN_WIDTH]
            dya = dm_ref[pl.ds(r0, BLOCK), 0:ATTN_WIDTH]
            kk = kv_ref[pl.ds(r0, 2 * BLOCK), 0:128]
            vv = kv_ref[pl.ds(r0, 2 * BLOCK), 128:256]
            silu_ga, dsilu_ga = _silu_and_grad(ga)
            do = dya * silu_ga
            first = lax.broadcasted_iota(jnp.int32, (2 * BLOCK, 128), 1) < HEAD_DIM
            attn, dq, dk, dv = [], [], [], []
            for kvh in range(KV_HEADS):
                k4 = _repeat_head(kk, kvh).astype(BF16)
                v4 = _repeat_head(vv, kvh).astype(BF16)
                qs, p, ps = _attn_probs(q[:, kvh * 256:(kvh + 1) * 256], k4, bias_ref[kvh], _row_sinks(kvh, sink_ref), no_past, masks)
                pb = p.astype(BF16)
                o_k = _unstack_groups(jnp.dot(pb, v4, preferred_element_type=F32), masks)
                do_k = do[:, kvh * 256:(kvh + 1) * 256]
                dos = _stack_groups(do_k, masks).astype(BF16)
                prod = do_k * o_k
                delta = jnp.concatenate([jnp.sum(jnp.where(m, prod, 0.0), axis=-1, keepdims=True) for m in masks], axis=0)
                dp = lax.dot_general(dos, v4, NT, preferred_element_type=F32)
                ds = (p * (dp - delta)).astype(BF16)
                sink_term = ps * delta
                for g in range(GROUP):
                    h = kvh * GROUP + g
                    dsk_ref[h:h + 1, :] -= jnp.sum(sink_term[g * BLOCK:(g + 1) * BLOCK], keepdims=True)
                dq.append(_unstack_groups(jnp.dot(ds, k4, preferred_element_type=F32), masks) * (HEAD_DIM ** -0.5))
                dk.append(_fold_head(lax.dot_general(ds, qs, TN, preferred_element_type=F32)))
                dv.append(_fold_head(lax.dot_general(pb, dos, TN, preferred_element_type=F32)))
                attn.append(o_k)
            o_ref[pl.ds(r0, BLOCK), COL_Q:COL_Q + ATTN_WIDTH] = jnp.concatenate(dq, axis=1).astype(BF16)
            o_all = jnp.concatenate(attn, axis=1)
            o_ref[pl.ds(r0, BLOCK), COL_GA:COL_GA + ATTN_WIDTH] = (dya * o_all * dsilu_ga).astype(BF16)
            dkv = jnp.concatenate([jnp.where(first, dk[0], dk[1]), jnp.where(first, dv[0], dv[1])], axis=1)
            dkv_ref[pl.ds(r0, 2 * BLOCK), :] += dkv
            return carry

        lax.fori_loop(0, nblk, block, 0, unroll=True)
        carry_ref[...] = dkv_ref[0:BLOCK, :]
        o_ref[:, COL_K:COL_K + 256] = dkv_ref[BLOCK:, :].astype(BF16)

        last = ii == nt - 1
        for g in range(POOL_GROUPS):
            cu = COL_U + g * POOL_GC
            cg = COL_GB + g * POOL_GC
            cm = ATTN_WIDTH + g * POOL_GC
            pw = pw_ref[g].astype(BF16)
            sc = ps_ref[:, g * POOL_GC:(g + 1) * POOL_GC]
            halo_u = jnp.where(ii == 0, 0.0, halo_ref[BLOCK - POOL_HALO:BLOCK, cu:cu + POOL_GC])
            u_ext = jnp.concatenate([halo_u, main_ref[:, cu:cu + POOL_GC]], axis=0)
            pooled = _pool_forward(u_ext, g, t0).astype(BF16)
            y_raw = jnp.dot(pooled, pw, preferred_element_type=F32)
            gb = main_ref[:, cg:cg + POOL_GC]
            dyb = dm_ref[:, cm:cm + POOL_GC]
            silu_gb, dsilu_gb = _silu_and_grad(gb)
            dypool = dyb * silu_gb
            dps_ref[:, g * POOL_GC:(g + 1) * POOL_GC] += jnp.sum(dypool * y_raw, axis=0, keepdims=True)
            o_ref[:, cg:cg + POOL_GC] = (dyb * (y_raw * sc) * dsilu_gb).astype(BF16)
            dyraw = dypool * sc
            dyraw_n = jnp.where(last, 0.0, dmn_ref[:, cm:cm + POOL_GC] * _silu(next_ref[:, cg:cg + POOL_GC]) * sc)
            dpw_ref[g * POOL_GC:(g + 1) * POOL_GC, :] += lax.dot_general(pooled, dyraw.astype(BF16), TN,
                                                                         preferred_element_type=F32)
            dyraw_ext = jnp.concatenate([dyraw, dyraw_n], axis=0).astype(BF16)
            dpooled = lax.dot_general(dyraw_ext, pw, NT, preferred_element_type=F32)
            w = 2 << g
            tt = t0 + lax.broadcasted_iota(jnp.int32, (tq + POOL_HALO, 1), 0)
            s = dpooled / jnp.minimum(tt + 1, w).astype(F32)
            for step in range(g + 1):
                s = s + pltpu.roll(s, tq + POOL_HALO - (1 << step), 0)
            o_ref[:, cu:cu + POOL_GC] = (s[0:tq] - dpooled[0:tq]).astype(BF16)

    rev = lambda i: nt - 1 - i
    nxt = lambda i: (jnp.minimum((rev(i) + 1) * (tq // POOL_HALO), t // POOL_HALO - 1), 0)
    row = pl.BlockSpec((tq, D_MODEL), lambda i: (rev(i), 0))
    nxt_row = pl.BlockSpec((POOL_HALO, D_MODEL), nxt)
    square = _const_spec((D_MODEL, D_MODEL))
    return _fused_call(
        body, comm, (proj, proj, proj, gy, gy, y, y, mix, w_out, post, sinks, pool_w, pool_scale),
        name="layer0_bwd", grid=(nt,),
        out_shape=[jax.ShapeDtypeStruct((t, EVEN_IN), BF16), jax.ShapeDtypeStruct((8, 128), F32),
                   jax.ShapeDtypeStruct((POOL_GROUPS * POOL_GC, POOL_GC), F32), jax.ShapeDtypeStruct((1, POOL_WIDTH), F32),
                   jax.ShapeDtypeStruct((D_MODEL, D_MODEL), BF16), jax.ShapeDtypeStruct((1, D_MODEL), F32)],
        in_specs=[pl.BlockSpec((tq, EVEN_IN), lambda i: (rev(i), 0)),
                  pl.BlockSpec((BLOCK, EVEN_IN), lambda i: (jnp.maximum(rev(i) * nblk - 1, 0), 0)),
                  pl.BlockSpec((POOL_HALO, EVEN_IN), nxt),
                  row, nxt_row, row, nxt_row, row, square, _const_spec((1, D_MODEL)),
                  pl.BlockSpec(memory_space=pltpu.SMEM),
                  _const_spec((POOL_GROUPS, POOL_GC, POOL_GC)), _const_spec((1, POOL_WIDTH))],
        out_specs=[pl.BlockSpec((tq, EVEN_IN), lambda i: (rev(i), 0)),
                   pl.BlockSpec((8, 128), lambda i: (0, 0)),
                   pl.BlockSpec((POOL_GROUPS * POOL_GC, POOL_GC), lambda i: (0, 0)),
                   pl.BlockSpec((1, POOL_WIDTH), lambda i: (0, 0)), square, pl.BlockSpec((1, D_MODEL), lambda i: (0, 0))],
        scratch_shapes=[pltpu.VMEM((tq + BLOCK, 256), F32), pltpu.VMEM((tq + BLOCK, 256), F32),
                        pltpu.VMEM((BLOCK, 256), F32)] + ATTN_CONSTS
        + [pltpu.VMEM((D_MODEL, D_MODEL), F32), pltpu.VMEM((tq + POOL_HALO, D_MODEL), F32)],
        params=_params("arbitrary"))


CONV_RC = 32
CONV_CC = 128
CONV_CHAINS_FWD = 4
CONV_CHAINS_BWD = 2
CONV_UNROLL = 2


def _fill_shifted(s_ref, rows):
    for b in range(1, 8):
        s_ref[b, 0:rows - 8, :] = s_ref[0, b:b + rows - 8, :]


def _tap_blocks(s_ref, r, cols, lead):
    for b in range(8):
        taps = [(a, 8 * a + b - lead) for a in range(5) if 0 <= 8 * a + b - lead < CONV_K]
        span = 8 * max(a for a, _ in taps) + CONV_RC
        blk = s_ref[b, pl.ds(r, span), cols]
        for a, k in taps:
            yield k, blk[8 * a:8 * a + CONV_RC]


def _conv_taps(s_ref, w_ref, r, cols, lead, reverse, chains):
    accs = [None] * chains
    for n, (k, blk) in enumerate(_tap_blocks(s_ref, r, cols, lead)):
        kw = CONV_K - 1 - k if reverse else k
        term = blk * w_ref[kw:kw + 1, cols]
        accs[n % chains] = term if accs[n % chains] is None else accs[n % chains] + term
    while len(accs) > 1:
        accs = [a + b for a, b in zip(accs[0::2], accs[1::2])]
    return accs[0]


def _layer_norm_fwd(cf, lng, lnb):
    mu = jnp.mean(cf, axis=-1, keepdims=True)
    xc = cf - mu
    rstd = lax.rsqrt(jnp.mean(xc * xc, axis=-1, keepdims=True) + EPS)
    chat = xc * rstd
    return chat, rstd, chat * lng + lnb


def _layer1_fwd(proj, dw, dwb, lng, lnb, w_out, x_in, post, target, tt=256):
    t = proj.shape[0]
    lead = CONV_HALO - (CONV_K - 1)

    def body(main_ref, halo_ref, w_ref, b_ref, g_ref, lb_ref, wo_ref, x_ref, p_ref, t_ref,
             o_ref, c_ref, y_ref, dl_ref, l_ref, gs_ref):
        i = pl.program_id(0)
        hv = halo_ref[...]
        gs_ref[0, 0:CONV_HALO, :] = jnp.where(i == 0, 0.0, hv[:, GLU_A] * _sigmoid(hv[:, GLU_B]))
        gs_ref[0, CONV_HALO:CONV_HALO + tt, :] = main_ref[:, GLU_A] * _sigmoid(main_ref[:, GLU_B])
        _fill_shifted(gs_ref, tt + CONV_HALO)

        for c in range(D_MODEL // CONV_CC):
            cols = slice(c * CONV_CC, (c + 1) * CONV_CC)

            def chunk(j, carry):
                r = pl.multiple_of(j * CONV_RC, CONV_RC)
                c_ref[pl.ds(r, CONV_RC), cols] = _conv_taps(gs_ref, w_ref, r, cols, lead, False, CONV_CHAINS_FWD) + b_ref[:, cols]
                return carry
            lax.fori_loop(0, tt // CONV_RC, chunk, 0, unroll=CONV_UNROLL)

        _, _, cn = _layer_norm_fwd(c_ref[...], g_ref[...], lb_ref[...])
        o_ref[...] = (_silu(cn) * _silu(main_ref[:, GATE])).astype(BF16)

        d = _project_out(o_ref[...], wo_ref, x_ref, p_ref, y_ref) - t_ref[...]
        dl_ref[...] = d * (1.0 / D_MODEL)

        @pl.when(i == 0)
        def _():
            l_ref[...] = jnp.zeros_like(l_ref)

        l_ref[...] += 0.5 * jnp.sum(jnp.mean(d * d, axis=-1, keepdims=True))

    vec = _const_spec((1, D_MODEL))
    row = pl.BlockSpec((tt, D_MODEL), lambda i: (i, 0))
    f32_rows = jax.ShapeDtypeStruct((t, D_MODEL), F32)
    return pl.pallas_call(
        body, name="layer1_fwd", grid=(t // tt,),
        out_shape=[jax.ShapeDtypeStruct((t, D_MODEL), BF16), f32_rows, f32_rows, f32_rows,
                   jax.ShapeDtypeStruct((8, 128), F32)],
        in_specs=[pl.BlockSpec((tt, 3 * D_MODEL), lambda i: (i, 0)),
                  pl.BlockSpec((CONV_HALO, 3 * D_MODEL), lambda i: (jnp.maximum(i * (tt // CONV_HALO) - 1, 0), 0)),
                  _const_spec((CONV_K, D_MODEL)), vec, vec, vec,
                  _const_spec((D_MODEL, D_MODEL)), row, vec, row],
        out_specs=[row, row, row, row, pl.BlockSpec((8, 128), lambda i: (0, 0))],
        scratch_shapes=[pltpu.VMEM((8, tt + CONV_HALO, D_MODEL), F32)],
        compiler_params=_params("arbitrary"),
    )(proj, proj, dw, dwb, lng, lnb, w_out, x_in, post, target)


def _layer1_bwd(proj, cf, gy, y, z, w_out, post, dw, lng, lnb, comm=None, tt=256):
    t = proj.shape[0]
    nt = t // tt
    te = tt + CONV_HALO

    def body(main_ref, next_ref, cf_ref, cfn_ref, gy_ref, gyn_ref, y_ref, yn_ref, z_ref, wo_ref, po_ref,
             w_ref, g_ref, lb_ref,
             o_ref, ddw_ref, ddb_ref, dg_ref, dlb_ref, dwo16_ref, dpo_ref, ds_ref, glu_ref, sb_ref, dwo_ref):
        i = pl.program_id(0)

        @pl.when(i == 0)
        def _():
            ddw_ref[...] = jnp.zeros_like(ddw_ref)
            ddb_ref[...] = jnp.zeros_like(ddb_ref)
            dg_ref[...] = jnp.zeros_like(dg_ref)
            dlb_ref[...] = jnp.zeros_like(dlb_ref)

        dzv = _post_bwd_rows(jnp.concatenate([gy_ref[...], gyn_ref[...]], axis=0),
                             jnp.concatenate([y_ref[...], yn_ref[...]], axis=0), z_ref[...], tt, i == 0, i == nt - 1,
                             po_ref, wo_ref, dwo_ref, dwo16_ref, dpo_ref)
        dzv = jnp.concatenate([dzv[0:tt], jnp.where(i < nt - 1, dzv[tt:], 0.0)], axis=0)
        lng = g_ref[...]
        chat, rstd, cn = _layer_norm_fwd(jnp.concatenate([cf_ref[...], cfn_ref[...]], axis=0), lng, lb_ref[...])
        gate = jnp.concatenate([main_ref[:, GATE], next_ref[:, GATE]], axis=0)
        silu_cn, dsilu_cn = _silu_and_grad(cn)
        silu_gate, dsilu_gate = _silu_and_grad(gate)
        o_ref[:, GATE] = (dzv * silu_cn * dsilu_gate)[0:tt].astype(BF16)
        dcn = dzv * silu_gate * dsilu_cn
        dg_ref[...] += jnp.sum((dcn * chat)[0:tt], axis=0, keepdims=True)
        dlb_ref[...] += jnp.sum(dcn[0:tt], axis=0, keepdims=True)
        dchat = dcn * lng
        dcf = rstd * (dchat - jnp.mean(dchat, axis=-1, keepdims=True) - chat * jnp.mean(dchat * chat, axis=-1, keepdims=True))
        ddb_ref[...] += jnp.sum(dcf[0:tt], axis=0, keepdims=True)
        ds_ref[0, 0:te, :] = dcf
        ds_ref[0, te:, :] = jnp.zeros((8, D_MODEL), F32)
        _fill_shifted(ds_ref, te + 8)
        sb_ref[...] = _sigmoid(main_ref[:, GLU_B])
        glu_ref[...] = main_ref[:, GLU_A] * sb_ref[...]

        for c in range(D_MODEL // CONV_CC):
            cols = slice(c * CONV_CC, (c + 1) * CONV_CC)
            gcols = slice(D_MODEL + c * CONV_CC, D_MODEL + (c + 1) * CONV_CC)

            def chunk(j, carry):
                r = pl.multiple_of(j * CONV_RC, CONV_RC)
                dglu = _conv_taps(ds_ref, w_ref, r, cols, 0, True, CONV_CHAINS_BWD)
                sb = sb_ref[pl.ds(r, CONV_RC), cols]
                o_ref[pl.ds(r, CONV_RC), cols] = (dglu * sb).astype(BF16)
                o_ref[pl.ds(r, CONV_RC), gcols] = (dglu * glu_ref[pl.ds(r, CONV_RC), cols] * (1.0 - sb)).astype(BF16)
                return carry
            lax.fori_loop(0, tt // CONV_RC, chunk, 0, unroll=CONV_UNROLL)

            def taps(j, accs):
                r = pl.multiple_of(j * CONV_RC, CONV_RC)
                gl = glu_ref[pl.ds(r, CONV_RC), cols]
                new = list(accs)
                for m, blk in _tap_blocks(ds_ref, r, cols, 0):
                    prod = blk * gl
                    part = prod[0:8]
                    for q in range(1, CONV_RC // 8):
                        part = part + prod[8 * q:8 * q + 8]
                    new[m] = new[m] + part
                return tuple(new)
            accs = lax.fori_loop(0, tt // CONV_RC, taps, tuple(jnp.zeros((8, CONV_CC), F32) for _ in range(CONV_K)))
            for m in range(CONV_K):
                k = CONV_K - 1 - m
                ddw_ref[k:k + 1, cols] += jnp.sum(accs[m], axis=0, keepdims=True)

    vec = _const_spec((1, D_MODEL))
    vec_out = pl.BlockSpec((1, D_MODEL), lambda i: (0, 0))
    row = pl.BlockSpec((tt, D_MODEL), lambda i: (i, 0))
    nxt = lambda i: (jnp.minimum((i + 1) * (tt // CONV_HALO), t // CONV_HALO - 1), 0)
    nxt_row = pl.BlockSpec((CONV_HALO, D_MODEL), nxt)
    vec_f32 = jax.ShapeDtypeStruct((1, D_MODEL), F32)
    square = _const_spec((D_MODEL, D_MODEL))
    return _fused_call(
        body, comm, (proj, proj, cf, cf, gy, gy, y, y, z, w_out, post, dw, lng, lnb), name="layer1_bwd", grid=(nt,),
        out_shape=[jax.ShapeDtypeStruct((t, 3 * D_MODEL), BF16), jax.ShapeDtypeStruct((CONV_K, D_MODEL), F32),
                   vec_f32, vec_f32, vec_f32, jax.ShapeDtypeStruct((D_MODEL, D_MODEL), BF16), vec_f32],
        in_specs=[pl.BlockSpec((tt, 3 * D_MODEL), lambda i: (i, 0)),
                  pl.BlockSpec((CONV_HALO, 3 * D_MODEL), nxt),
                  row, nxt_row, row, nxt_row, row, nxt_row, row, square, vec,
                  _const_spec((CONV_K, D_MODEL)), vec, vec],
        out_specs=[pl.BlockSpec((tt, 3 * D_MODEL), lambda i: (i, 0)),
                   pl.BlockSpec((CONV_K, D_MODEL), lambda i: (0, 0)), vec_out, vec_out, vec_out, square, vec_out],
        scratch_shapes=[pltpu.VMEM((8, te + 8, D_MODEL), F32), pltpu.VMEM((tt, D_MODEL), F32),
                        pltpu.VMEM((tt, D_MODEL), F32), pltpu.VMEM((D_MODEL, D_MODEL), F32)],
        params=_params("arbitrary"))


def _piece_sum(parts, place, name):
    r = parts[0][0].shape[1]

    def body(p_ref, *refs):
        o_ref = refs[-1]
        acc = refs[0][0].astype(F32)
        for part in refs[1:-1]:
            acc = acc + part[0].astype(F32)
        o_ref[0] = acc

    blk = (1, r, D_MODEL)
    spec = lambda slot: pl.BlockSpec(blk, lambda j, p_ref: (slot(p_ref), 0, 0))
    return pl.pallas_call(
        body, name=name,
        grid_spec=pltpu.PrefetchScalarGridSpec(
            num_scalar_prefetch=1, grid=(1,),
            in_specs=[spec(slot) for _, slot in parts],
            out_specs=pl.BlockSpec(blk, lambda j, p_ref: (p_ref[1], 0, 0))),
        out_shape=jax.ShapeDtypeStruct((2, r, D_MODEL), F32),
        compiler_params=_params("arbitrary"),
    )(place, *[a for a, _ in parts])


def _direct_parts(own, recv):
    peer = lambda m: (lambda p: p[0] ^ m)
    return [(own, peer(0))] + [(recv, peer(m)) for m in range(1, N_DEV)]


def _share_with_sibling(halves, name):
    n = len(halves)

    def body(*refs):
        outs = refs[n:2 * n]
        send_sems, recv_sems = refs[2 * n:]
        x, y, c, _ = _place()
        send = [pltpu.make_async_remote_copy(
            src_ref=outs[t].at[c], dst_ref=outs[t].at[c], send_sem=send_sems.at[t], recv_sem=recv_sems.at[t],
            device_id=(x, y, 1 - c), device_id_type=MESH_ID) for t in range(n)]
        recv = [pltpu.make_async_remote_copy(
            src_ref=outs[t].at[c], dst_ref=outs[t].at[1 - c], send_sem=send_sems.at[t], recv_sem=recv_sems.at[t],
            device_id=(x, y, 1 - c), device_id_type=MESH_ID) for t in range(n)]
        for cp in send:
            cp.start()
        for cp in recv:
            cp.wait_recv()
        for cp in send:
            cp.wait_send()

    return pl.pallas_call(
        body, name=name,
        out_shape=[jax.ShapeDtypeStruct(h.shape, h.dtype) for h in halves],
        in_specs=[ANY] * n, out_specs=[ANY] * n,
        input_output_aliases={t: t for t in range(n)},
        scratch_shapes=[pltpu.SemaphoreType.DMA((n,)), pltpu.SemaphoreType.DMA((n,))],
    )(*halves)


def _sum8(parts, name):
    r = parts.shape[1]

    def body(p_ref, o_ref):
        acc = p_ref[0]
        for k in range(1, N_DEV):
            acc = acc + p_ref[k]
        o_ref[...] = acc

    return pl.pallas_call(
        body, name=name, out_shape=jax.ShapeDtypeStruct((r, 128), F32),
        in_specs=[pl.BlockSpec(memory_space=pltpu.VMEM)], out_specs=pl.BlockSpec(memory_space=pltpu.VMEM),
    )(parts)


def _adamw(w, g, m, v, name):
    shape = w.shape
    cols = shape[-1]
    rows = w.size // cols
    rt = 256 if rows % 256 == 0 else rows
    if w.ndim >= 3 and shape[-2] == 1:
        view, blk, at = (rows, 1, cols), (rows, 1, cols), (lambda i: (0, 0, 0))
    else:
        view, blk, at = (rows, cols), (rt, cols), (lambda i: (i, 0))

    def body(w_ref, g_ref, m_ref, v_ref, d_ref, nm_ref, nv_ref):
        gv = g_ref[...]
        mn = ADAM_B1 * m_ref[...] + (1.0 - ADAM_B1) * gv
        vn = ADAM_B2 * v_ref[...] + (1.0 - ADAM_B2) * (gv * gv)
        m_hat = mn / (1.0 - ADAM_B1 ** ADAM_STEP)
        v_hat = vn / (1.0 - ADAM_B2 ** ADAM_STEP)
        d_ref[...] = -ADAM_LR * (m_hat / (jnp.sqrt(v_hat) + ADAM_EPS) + ADAM_WD * w_ref[...])
        nm_ref[...] = mn
        nv_ref[...] = vn

    spec = pl.BlockSpec(blk, at)
    outs = pl.pallas_call(
        body, name=name, grid=(view[0] // blk[0],),
        out_shape=[jax.ShapeDtypeStruct(view, F32)] * 3,
        in_specs=[spec] * 4, out_specs=[spec] * 3,
        compiler_params=_params("parallel"),
    )(*[a.reshape(view) for a in (w, g, m, v)])
    return [o.reshape(shape) for o in outs]


SMALL_ROWS = 832


def _pack_small(g):
    parts = [g["loss"], g["pre1"].reshape(8, 128), g["post0"].reshape(8, 128),
             g["post1"].reshape(8, 128), g["sinks"], jnp.pad(g["pool_scale"].reshape(4, 128), ((0, 4), (0, 0))),
             g["pool_w"], g["dw"].reshape(248, 128), g["dwb"].reshape(8, 128), g["lng"].reshape(8, 128),
             g["lnb"].reshape(8, 128)]
    assert sum(p.shape[0] for p in parts) == SMALL_ROWS
    return jnp.concatenate(parts, axis=0)


def _unpack_small(s):
    out, r = {}, 0
    for key, rows, shape in (("loss", 8, (8, 128)), ("pre1", 8, (1, D_MODEL)), ("post", 16, (2, D_MODEL)),
                             ("sinks", 8, (8, 128)),
                             ("pool_scale", 4, (1, POOL_WIDTH)), ("pad", 4, (4, 128)), ("pool_w", 512, (1, 4, 128, 128)),
                             ("dw", 248, (CONV_K, D_MODEL)), ("dwb", 8, (1, D_MODEL)), ("lng", 8, (1, D_MODEL)),
                             ("lnb", 8, (1, D_MODEL))):
        out[key] = s[r:r + rows].reshape(shape)
        r += rows
    return out


def kernel(x, pre_norm, post_norm, a_w_in, a_sinks, b_pool_w, b_pool_scale, ab_w_out, c_w_in, c_dw_w, c_dw_b, c_ln_g, c_ln_b, c_w_out, loss_target, m_pre_norm, m_post_norm, m_a_w_in, m_a_sinks, m_b_pool_w, m_b_pool_scale, m_ab_w_out, m_c_w_in, m_c_dw_w, m_c_dw_b, m_c_ln_g, m_c_ln_b, m_c_w_out, v_pre_norm, v_post_norm, v_a_w_in, v_a_sinks, v_b_pool_w, v_b_pool_scale, v_ab_w_out, v_c_w_in, v_c_dw_w, v_c_dw_b, v_c_ln_g, v_c_ln_b, v_c_w_out):
    ix, iy = lax.axis_index("x"), lax.axis_index("y")
    chip_cols = (2 * ix + iy) * 256

    pad8 = lambda v: jnp.pad(v, ((0, -v.shape[0] % 8), (0, 0)))
    vec_shard = jnp.concatenate([pad8(c_dw_w.reshape(CONV_K, 256)), pad8(c_dw_b), pad8(c_ln_g), pad8(c_ln_b),
                                 jnp.zeros((8, 256), F32)], axis=0)
    x0, target = x[0], loss_target[0]
    pre0, pre1 = pre_norm[0:1], pre_norm[1:2]
    post0, post1 = post_norm[0:1], post_norm[1:2]
    pool_w = b_pool_w[0]

    (wa_t,) = _run_comm(_Gather([a_w_in[0].T.astype(BF16)], halve=True), "gather_a_w_in")
    wa_t = wa_t.reshape(EVEN_IN, D_MODEL)
    proj0, (w_ab,) = _norm_matmul(x0, pre0, wa_t, "proj0_fwd", comm=_Gather([ab_w_out[0].astype(BF16)], halve=True))
    w_ab = w_ab.reshape(D_MODEL, D_MODEL)
    (mix0, y0, x1), (wc_t, w_c, vecs) = _layer0_fwd(
        proj0, a_sinks, pool_w, b_pool_scale, w_ab, x0, post0,
        comm=_Gather([c_w_in[0].T.astype(BF16), c_w_out[0].astype(BF16), vec_shard], halve=True))
    wc_t = wc_t.reshape(3 * D_MODEL, D_MODEL)
    w_c = w_c.reshape(D_MODEL, D_MODEL)
    vecs = vecs.reshape(4, 64, 256).transpose(1, 0, 2).reshape(64, D_MODEL)
    dw, dwb, lng, lnb = vecs[0:CONV_K], vecs[32:33], vecs[40:41], vecs[48:49]
    proj1, _ = _norm_matmul(x1, pre1, wc_t, "proj1_fwd")
    z1, cf1, y1, g2, loss = _layer1_fwd(proj1, dw, dwb, lng, lnb, w_c, x1, post1, target)

    pieces = lambda m: m.reshape(N_DEV, m.shape[0] // N_DEV, D_MODEL)
    (dproj1, d_dw, d_dwb, d_lng, d_lnb, d_wc, d_post1), _ = _layer1_bwd(proj1, cf1, g2, y1, z1, w_c, post1, dw, lng, lnb)
    (g1, d_wct, d_pre1), _ = _pre_bwd(dproj1, wc_t, x1, pre1, g2, "proj1_bwd")
    (dproj0, d_sinks, d_pw, d_ps, d_wab, d_post0), (r_wc, r_wct) = _layer0_bwd(
        proj0, g1, y0, mix0, w_ab, post0, a_sinks, pool_w, b_pool_scale,
        comm=_Scatter([pieces(d_wc), pieces(d_wct)]))
    g = dict(loss=loss, pre1=d_pre1, post0=d_post0, post1=d_post1, sinks=d_sinks, pool_w=d_pw, pool_scale=d_ps,
             dw=d_dw, dwb=d_dwb, lng=d_lng, lnb=d_lnb)
    d_wat, (small8, r_wab) = _proj_dw(dproj0, x0, pre0, "proj0_dw",
                                      comm=_Comms(_Gather([_pack_small(g)], halve=False), _Scatter([pieces(d_wab)])))
    sent = _scatter_start(pieces(d_wat), "scatter_a_start")
    (gx, d_pre0), _ = _proj_dx(dproj0, wa_t, x0, pre0 + sent[4][0:1, 0:1], g1, "proj0_dx")
    sent_pre0 = _scatter_start(jnp.broadcast_to(d_pre0.reshape(1, 8, 128), (N_DEV, 8, 128)), "pre0_start")
    own_wat, r_wat = _scatter_wait(*sent[:4], d_pre0, "scatter_a_wait")

    ic = lax.axis_index("c")
    me = 4 * ix + 2 * iy + ic
    place = jnp.stack([me, ic]).astype(jnp.int32)
    parts = [_direct_parts(own_wat, r_wat), _direct_parts(pieces(d_wab), r_wab), _direct_parts(pieces(d_wct), r_wct),
             _direct_parts(pieces(d_wc), r_wc)]
    halves = [_piece_sum(p, place, f"grad_sum{t}") for t, p in enumerate(parts)]
    g_wa_t, g_wab, g_wc_t, g_wc = [h.reshape(2 * h.shape[1], D_MODEL) for h in _share_with_sibling(halves, "grad_share")]
    _, pre0_8 = _scatter_wait(*sent_pre0[:4], g_wc, "pre0_wait")
    pre0_8 = lax.dynamic_update_slice(pre0_8, d_pre0.reshape(1, 8, 128), (me, 0, 0))
    g_c_w_in = g_wc_t.T[None]
    g_ab_w_out = g_wab[None]
    g_c_w_out = g_wc[None]

    s = _unpack_small(_sum8(small8, "small_sum"))
    layer = lax.broadcasted_iota(jnp.int32, (2, D_MODEL), 0)
    g_pre = jnp.where(layer == 0, _sum8(pre0_8, "pre0_sum").reshape(1, D_MODEL), s["pre1"])
    g_post = s["post"]
    g_sinks = s["sinks"][:, 0].reshape(1, 8)
    g_pool_w, g_pool_scale = s["pool_w"], s["pool_scale"]
    g_dw = lax.dynamic_slice_in_dim(s["dw"], chip_cols, 256, axis=1).reshape(1, CONV_K, 1, 256)
    g_dwb = lax.dynamic_slice_in_dim(s["dwb"], chip_cols, 256, axis=1)
    g_lng = lax.dynamic_slice_in_dim(s["lng"], chip_cols, 256, axis=1)
    g_lnb = lax.dynamic_slice_in_dim(s["lnb"], chip_cols, 256, axis=1)

    turn = lambda a: jnp.swapaxes(a, 1, 2)
    a_w_in, m_a_w_in, v_a_w_in = turn(a_w_in), turn(m_a_w_in), turn(v_a_w_in)
    grads = [g_pre, g_post, g_wa_t[None], g_sinks, g_pool_w, g_pool_scale, g_ab_w_out, g_c_w_in, g_dw, g_dwb, g_lng,
             g_lnb, g_c_w_out]
    weights = [pre_norm, post_norm, a_w_in, a_sinks, b_pool_w, b_pool_scale, ab_w_out, c_w_in, c_dw_w, c_dw_b, c_ln_g,
               c_ln_b, c_w_out]
    moms = [m_pre_norm, m_post_norm, m_a_w_in, m_a_sinks, m_b_pool_w, m_b_pool_scale, m_ab_w_out, m_c_w_in, m_c_dw_w,
            m_c_dw_b, m_c_ln_g, m_c_ln_b, m_c_w_out]
    vars_ = [v_pre_norm, v_post_norm, v_a_w_in, v_a_sinks, v_b_pool_w, v_b_pool_scale, v_ab_w_out, v_c_w_in, v_c_dw_w,
             v_c_dw_b, v_c_ln_g, v_c_ln_b, v_c_w_out]
    deltas, new_m, new_v = [], [], []
    for k, (w, gr, m, v) in enumerate(zip(weights, grads, moms, vars_)):
        d, nm, nv = _adamw(w, gr, m, v, f"adamw{k}")
        deltas.append(d)
        new_m.append(nm)
        new_v.append(nv)
    for outs in (grads, deltas, new_m, new_v):
        outs[2] = turn(outs[2])
    return (s["loss"][0, 0], gx[None], *grads, *deltas, *new_m, *new_v)
```

```python
import jax
import jax.numpy as jnp
from jax import lax
from jax.experimental import pallas as pl
from jax.experimental.pallas import tpu as pltpu

F32 = jnp.float32
BF16 = jnp.bfloat16

D_MODEL = 1024
EPS = 1e-6
NEG = -1e30
HEAD_DIM = 64
GROUP = 4
KV_HEADS = 2
BLOCK = 128
EVEN_IN = 2304
ATTN_WIDTH = 512
POOL_WIDTH = 512
COL_Q, COL_K, COL_GA, COL_U, COL_GB = 0, 512, 768, 1280, 1792
POOL_GROUPS = 4
POOL_GC = 128
POOL_HALO = 16
CONV_K = 31
CONV_HALO = 32
GLU_A = slice(0, D_MODEL)
GLU_B = slice(D_MODEL, 2 * D_MODEL)
GATE = slice(2 * D_MODEL, 3 * D_MODEL)
N_DEV = 8

ADAM_LR = 0.001
ADAM_B1 = 0.9
ADAM_B2 = 0.999
ADAM_EPS = 1e-08
ADAM_WD = 0.01
ADAM_STEP = 10

VMEM_LIMIT_BYTES = 56 * 1024 * 1024

NT = (((1,), (1,)), ((), ()))
TN = (((0,), (0,)), ((), ()))
MESH_ID = pl.DeviceIdType.MESH


def _params(*sem):
    return pltpu.CompilerParams(dimension_semantics=sem, vmem_limit_bytes=VMEM_LIMIT_BYTES)


def _const_spec(shape):
    nd = len(shape)
    return pl.BlockSpec(shape, lambda *_: (0,) * nd, pipeline_mode=pl.Buffered(1))


def _sigmoid(v):
    return 0.5 * jnp.tanh(0.5 * v) + 0.5


def _silu(v):
    h = 0.5 * v
    return h * jnp.tanh(h) + h


def _silu_and_grad(v):
    s = _sigmoid(v)
    silu = v * s
    return silu, s + silu * (1.0 - s)


ANY = pl.BlockSpec(memory_space=pl.ANY)


def _place():
    x, y, c = lax.axis_index("x"), lax.axis_index("y"), lax.axis_index("c")
    chips = [(1 - x, y), (x, 1 - y), (1 - x, 1 - y)]
    return x, y, c, chips


class _Gather:
    def __init__(self, blocks, halve):
        self.ins = list(blocks)
        self.halve = halve
        self.n = n = len(blocks)
        self.shapes = [((b.shape[0] // 2) if halve else b.shape[0], b.shape[1]) for b in blocks]
        self.out_shape = [jax.ShapeDtypeStruct((N_DEV, r, cols), b.dtype) for (r, cols), b in zip(self.shapes, blocks)]
        self.scratch = [pltpu.SemaphoreType.DMA((7 * n,)), pltpu.SemaphoreType.DMA((7 * n,)),
                        pltpu.SemaphoreType.DMA((n,))]

    def _copies(self, ins, outs, sems):
        send_sems, recv_sems, local_sems = sems
        x, y, c, chips = _place()
        me, sibling = (x, y, c), (x, y, 1 - c)

        def piece(t, px, py, pc):
            return outs[t].at[4 * px + 2 * py + pc]

        def own(t):
            return ins[t].at[pl.ds(c * self.shapes[t][0], self.shapes[t][0])] if self.halve else ins[t]

        def copy(t, k, block, to, src=None):
            return pltpu.make_async_remote_copy(
                src_ref=piece(t, *block) if src is None else src, dst_ref=piece(t, *block),
                send_sem=send_sems.at[7 * t + k], recv_sem=recv_sems.at[7 * t + k],
                device_id=to, device_id_type=MESH_ID)

        rng = range(self.n)
        return dict(
            mine=[pltpu.make_async_copy(own(t), piece(t, *me), local_sems.at[t]) for t in rng],
            first=[copy(t, 0, me, sibling, src=own(t)) for t in rng]
            + [copy(t, 1 + j, me, (*chip, c), src=own(t)) for t in rng for j, chip in enumerate(chips)],
            landed=[copy(t, 1 + j, (*chip, c), me) for j, chip in enumerate(chips) for t in rng],
            passed=[copy(t, 4 + j, (*chip, c), sibling) for j, chip in enumerate(chips) for t in rng],
            from_sibling=[copy(t, 0, sibling, me) for t in rng]
            + [copy(t, 4 + j, (*chip, 1 - c), me) for t in rng for j, chip in enumerate(chips)])

    def start(self, ins, outs, sems):
        d = self._copies(ins, outs, sems)
        for cp in d["mine"] + d["first"]:
            cp.start()

    def middle(self, ins, outs, sems):
        d = self._copies(ins, outs, sems)
        for got, fwd in zip(d["landed"], d["passed"]):
            got.wait_recv()
            fwd.start()

    def finish(self, ins, outs, sems):
        d = self._copies(ins, outs, sems)
        for cp in d["from_sibling"]:
            cp.wait_recv()
        for cp in d["first"] + d["passed"]:
            cp.wait_send()
        for cp in d["mine"]:
            cp.wait()


class _Scatter:
    def __init__(self, tensors):
        self.ins = list(tensors)
        self.n = n = len(tensors)
        self.out_shape = [jax.ShapeDtypeStruct(t.shape, t.dtype) for t in tensors]
        self.scratch = [pltpu.SemaphoreType.DMA((7 * n,)), pltpu.SemaphoreType.DMA((7 * n,))]

    def _copies(self, ins, outs, sems):
        send_sems, recv_sems = sems
        x, y, c, _ = _place()
        me = 4 * x + 2 * y + c
        sends, recvs = [], []
        for t in range(self.n):
            for m in range(1, N_DEV):
                px, py, pc = x ^ (m >> 2), y ^ ((m >> 1) & 1), c ^ (m & 1)
                q = 4 * px + 2 * py + pc
                sems_k = dict(send_sem=send_sems.at[7 * t + m - 1], recv_sem=recv_sems.at[7 * t + m - 1],
                              device_id=(px, py, pc), device_id_type=MESH_ID)
                sends.append(pltpu.make_async_remote_copy(src_ref=ins[t].at[q], dst_ref=outs[t].at[me], **sems_k))
                recvs.append(pltpu.make_async_remote_copy(src_ref=ins[t].at[me], dst_ref=outs[t].at[q], **sems_k))
        return sends, recvs

    def start(self, ins, outs, sems):
        for cp in self._copies(ins, outs, sems)[0]:
            cp.start()

    def middle(self, ins, outs, sems):
        pass

    def finish(self, ins, outs, sems):
        sends, recvs = self._copies(ins, outs, sems)
        for cp in recvs:
            cp.wait_recv()
        for cp in sends:
            cp.wait_send()


class _Comms:
    def __init__(self, *comms):
        self.comms = comms
        self.ins = [a for c in comms for a in c.ins]
        self.out_shape = [s for c in comms for s in c.out_shape]
        self.scratch = [s for c in comms for s in c.scratch]

    def _each(self, phase, ins, outs, sems):
        i = o = s = 0
        for c in self.comms:
            ni, no, ns = len(c.ins), len(c.out_shape), len(c.scratch)
            getattr(c, phase)(ins[i:i + ni], outs[o:o + no], sems[s:s + ns])
            i, o, s = i + ni, o + no, s + ns

    def start(self, ins, outs, sems):
        self._each("start", ins, outs, sems)

    def middle(self, ins, outs, sems):
        self._each("middle", ins, outs, sems)

    def finish(self, ins, outs, sems):
        self._each("finish", ins, outs, sems)


def _run_comm(comm, name):
    n = len(comm.ins)

    def body(*refs):
        parts = refs[:n], refs[n:2 * n], refs[2 * n:]
        comm.start(*parts)
        comm.middle(*parts)
        comm.finish(*parts)

    return pl.pallas_call(body, name=name, out_shape=comm.out_shape, in_specs=[ANY] * n, out_specs=[ANY] * n,
                          scratch_shapes=comm.scratch)(*comm.ins)


HBM_SPEC = pl.BlockSpec(memory_space=pltpu.HBM)
SEM_SPEC = pl.BlockSpec(memory_space=pltpu.SEMAPHORE)
DATAFLOW = pltpu.SideEffectType.DATAFLOW_SIDE_EFFECTING


def _scatter_copies(own_ref, land_ref, send_sems, recv_sems):
    x, y, c, _ = _place()
    me = 4 * x + 2 * y + c
    pairs = []
    for m in range(1, N_DEV):
        px, py, pc = x ^ (m >> 2), y ^ ((m >> 1) & 1), c ^ (m & 1)
        q = 4 * px + 2 * py + pc
        sems = dict(send_sem=send_sems.at[m - 1], recv_sem=recv_sems.at[m - 1], device_id=(px, py, pc),
                    device_id_type=MESH_ID)
        pairs.append((pltpu.make_async_remote_copy(src_ref=own_ref.at[q], dst_ref=land_ref.at[me], **sems),
                      pltpu.make_async_remote_copy(src_ref=own_ref.at[me], dst_ref=land_ref.at[q], **sems)))
    return pairs


def _scatter_start(own, name):
    def body(own_ref, land_ref, send_sems, recv_sems, own_thru, land_thru, token):
        for send, _ in _scatter_copies(own_ref, land_ref, send_sems, recv_sems):
            send.start()
        token[...] = jnp.zeros_like(token)

    buf = pltpu.HBM(own.shape, own.dtype)
    return pl.pallas_call(
        body, name=name,
        out_shape=(pltpu.SemaphoreType.DMA((N_DEV - 1,)), pltpu.SemaphoreType.DMA((N_DEV - 1,)), buf, buf,
                   jax.ShapeDtypeStruct((8, 128), F32)),
        in_specs=(HBM_SPEC, HBM_SPEC),
        out_specs=(SEM_SPEC, SEM_SPEC, HBM_SPEC, HBM_SPEC, pl.BlockSpec(memory_space=pltpu.VMEM)),
        input_output_aliases={0: 2, 1: 3},
        compiler_params=pltpu.CompilerParams(has_side_effects=DATAFLOW),
    )(pltpu.with_memory_space_constraint(own, pltpu.HBM),
      pltpu.with_memory_space_constraint(lax.empty(own.shape, own.dtype), pltpu.HBM))


def _scatter_wait(send_sems, recv_sems, own_thru, land_thru, after, name):
    def body(own_ref, land_ref, send_sems, recv_sems, after_ref, own_out, land_out):
        for send, recv in _scatter_copies(own_ref, land_ref, send_sems, recv_sems):
            send.wait_send()
            recv.wait_recv()

    buf = pltpu.HBM(own_thru.shape, own_thru.dtype)
    return pl.pallas_call(
        body, name=name, out_shape=(buf, buf),
        in_specs=(HBM_SPEC, HBM_SPEC, SEM_SPEC, SEM_SPEC, ANY), out_specs=(HBM_SPEC, HBM_SPEC),
        input_output_aliases={0: 0, 1: 1},
        compiler_params=pltpu.CompilerParams(has_side_effects=DATAFLOW),
    )(own_thru, land_thru, send_sems, recv_sems, after)


def _fused_call(body, comm, args, *, name, grid, out_shape, in_specs, out_specs, scratch_shapes=(), params):
    single = not isinstance(out_shape, (list, tuple))
    out_shape = [out_shape] if single else list(out_shape)
    out_specs = [out_specs] if single else list(out_specs)
    if comm is None:
        res = pl.pallas_call(body, name=name, grid=grid, out_shape=out_shape, in_specs=in_specs, out_specs=out_specs,
                             scratch_shapes=list(scratch_shapes), compiler_params=params)(*args)
        return (res[0] if single else res), []
    n_in, n_out, n_scr = len(in_specs), len(out_shape), len(scratch_shapes)
    c_in, c_out = len(comm.ins), len(comm.out_shape)
    steps = grid[0]

    def fused(*refs):
        pos = 0
        groups = []
        for size in (n_in, c_in, n_out, c_out, n_scr, len(comm.scratch)):
            groups.append(refs[pos:pos + size])
            pos += size
        ins, c_ins, outs, c_outs, scr, c_sems = groups
        i = pl.program_id(0)

        @pl.when(i == 0)
        def _():
            comm.start(c_ins, c_outs, c_sems)

        @pl.when(i == steps // 2)
        def _():
            comm.middle(c_ins, c_outs, c_sems)

        body(*ins, *outs, *scr)

        @pl.when(i == steps - 1)
        def _():
            comm.finish(c_ins, c_outs, c_sems)

    res = pl.pallas_call(
        fused, name=name, grid=grid, out_shape=out_shape + list(comm.out_shape),
        in_specs=list(in_specs) + [ANY] * c_in, out_specs=out_specs + [ANY] * c_out,
        scratch_shapes=list(scratch_shapes) + list(comm.scratch), compiler_params=params)(*args, *comm.ins)
    main = res[:n_out]
    return (main[0] if single else main), list(res[n_out:])


def _norm_matmul(x, gain, wt, name, comm=None, tm=1024):
    t, n = x.shape[0], wt.shape[0]

    def body(x_ref, g_ref, wt_ref, o_ref):
        xv = x_ref[...]
        r = lax.rsqrt(jnp.mean(xv * xv, axis=-1, keepdims=True) + EPS)
        h = (xv * r * g_ref[...]).astype(BF16)
        o_ref[...] = lax.dot_general(h, wt_ref[...], NT, preferred_element_type=F32)

    return _fused_call(
        body, comm, (x, gain, wt), name=name, grid=(t // tm,),
        out_shape=jax.ShapeDtypeStruct((t, n), F32),
        in_specs=[pl.BlockSpec((tm, D_MODEL), lambda i: (i, 0)), _const_spec((1, D_MODEL)), _const_spec((n, D_MODEL))],
        out_specs=pl.BlockSpec((tm, n), lambda i: (i, 0)),
        params=_params("arbitrary"))


def _project_out(a, w_ref, x_ref, p_ref, y_ref):
    y = jnp.dot(a, w_ref[...], preferred_element_type=F32)
    y_ref[...] = y
    ry = lax.rsqrt(jnp.mean(y * y, axis=-1, keepdims=True) + EPS)
    return x_ref[...] + (y * ry) * p_ref[...]


def _post_bwd_rows(g, y, a, n_own, first, last, p_ref, w_ref, dw_ref, dw16_ref, dp_ref):
    @pl.when(first)
    def _():
        dw_ref[...] = jnp.zeros_like(dw_ref)
        dp_ref[...] = jnp.zeros_like(dp_ref)

    ry = lax.rsqrt(jnp.mean(y * y, axis=-1, keepdims=True) + EPS)
    nv = y * ry
    dp_ref[...] += jnp.sum((g * nv)[0:n_own], axis=0, keepdims=True)
    dn = g * p_ref[...]
    dy = (ry * (dn - nv * jnp.mean(dn * nv, axis=-1, keepdims=True))).astype(BF16)
    dw_ref[...] += lax.dot_general(a, dy[0:n_own], TN, preferred_element_type=F32)

    @pl.when(last)
    def _():
        dw16_ref[...] = dw_ref[...].astype(BF16)

    return lax.dot_general(dy, w_ref[...], NT, preferred_element_type=F32)


def _pre_bwd(dproj, wt, x_in, pre, g, name, comm=None, tm=512):
    t, n = dproj.shape
    steps = t // tm

    def body(dp_ref, wt_ref, x_ref, pre_ref, g_ref, dx_ref, dwt16_ref, dpre_ref, dwt_ref):
        @pl.when(pl.program_id(0) == 0)
        def _():
            dwt_ref[...] = jnp.zeros_like(dwt_ref)
            dpre_ref[...] = jnp.zeros_like(dpre_ref)

        dpv = dp_ref[...]
        dh = jnp.dot(dpv, wt_ref[...], preferred_element_type=F32)
        xv = x_ref[...]
        r = lax.rsqrt(jnp.mean(xv * xv, axis=-1, keepdims=True) + EPS)
        xn = xv * r
        pv = pre_ref[...]
        dpre_ref[...] += jnp.sum(dh * xn, axis=0, keepdims=True)
        dxn = dh * pv
        dx_ref[...] = g_ref[...] + r * (dxn - xn * jnp.mean(dxn * xn, axis=-1, keepdims=True))
        h = (xn * pv).astype(BF16)
        dwt_ref[...] += lax.dot_general(dpv, h, TN, preferred_element_type=F32)

        @pl.when(pl.program_id(0) == steps - 1)
        def _():
            dwt16_ref[...] = dwt_ref[...].astype(BF16)

    row = pl.BlockSpec((tm, D_MODEL), lambda i: (i, 0))
    return _fused_call(
        body, comm, (dproj, wt, x_in, pre, g), name=name, grid=(steps,),
        out_shape=[jax.ShapeDtypeStruct((t, D_MODEL), F32), jax.ShapeDtypeStruct((n, D_MODEL), BF16),
                   jax.ShapeDtypeStruct((1, D_MODEL), F32)],
        in_specs=[pl.BlockSpec((tm, n), lambda i: (i, 0)), _const_spec((n, D_MODEL)), row, _const_spec((1, D_MODEL)), row],
        out_specs=[row, _const_spec((n, D_MODEL)), pl.BlockSpec((1, D_MODEL), lambda i: (0, 0))],
        scratch_shapes=[pltpu.VMEM((n, D_MODEL), F32)],
        params=_params("arbitrary"))


def _proj_dw(dproj, x_in, pre, name, comm=None, tm=1024):
    t, n = dproj.shape
    steps = t // tm

    def body(dp_ref, x_ref, pre_ref, dwt16_ref, dwt_ref):
        @pl.when(pl.program_id(0) == 0)
        def _():
            dwt_ref[...] = jnp.zeros_like(dwt_ref)

        xv = x_ref[...]
        r = lax.rsqrt(jnp.mean(xv * xv, axis=-1, keepdims=True) + EPS)
        h = (xv * r * pre_ref[...]).astype(BF16)
        dwt_ref[...] += lax.dot_general(dp_ref[...], h, TN, preferred_element_type=F32)

        @pl.when(pl.program_id(0) == steps - 1)
        def _():
            dwt16_ref[...] = dwt_ref[...].astype(BF16)

    return _fused_call(
        body, comm, (dproj, x_in, pre), name=name, grid=(steps,),
        out_shape=jax.ShapeDtypeStruct((n, D_MODEL), BF16),
        in_specs=[pl.BlockSpec((tm, n), lambda i: (i, 0)), pl.BlockSpec((tm, D_MODEL), lambda i: (i, 0)),
                  _const_spec((1, D_MODEL))],
        out_specs=pl.BlockSpec((n, D_MODEL), lambda i: (0, 0)),
        scratch_shapes=[pltpu.VMEM((n, D_MODEL), F32)],
        params=_params("arbitrary"))


def _proj_dx(dproj, wt, x_in, pre, g, name, comm=None, tm=512):
    t, n = dproj.shape

    def body(dp_ref, wt_ref, x_ref, pre_ref, g_ref, dx_ref, dpre_ref):
        @pl.when(pl.program_id(0) == 0)
        def _():
            dpre_ref[...] = jnp.zeros_like(dpre_ref)

        dh = jnp.dot(dp_ref[...], wt_ref[...], preferred_element_type=F32)
        xv = x_ref[...]
        r = lax.rsqrt(jnp.mean(xv * xv, axis=-1, keepdims=True) + EPS)
        xn = xv * r
        dpre_ref[...] += jnp.sum(dh * xn, axis=0, keepdims=True)
        dxn = dh * pre_ref[...]
        dx_ref[...] = g_ref[...] + r * (dxn - xn * jnp.mean(dxn * xn, axis=-1, keepdims=True))

    row = pl.BlockSpec((tm, D_MODEL), lambda i: (i, 0))
    return _fused_call(
        body, comm, (dproj, wt, x_in, pre, g), name=name, grid=(t // tm,),
        out_shape=[jax.ShapeDtypeStruct((t, D_MODEL), F32), jax.ShapeDtypeStruct((1, D_MODEL), F32)],
        in_specs=[pl.BlockSpec((tm, n), lambda i: (i, 0)), _const_spec((n, D_MODEL)), row, _const_spec((1, D_MODEL)), row],
        out_specs=[row, pl.BlockSpec((1, D_MODEL), lambda i: (0, 0))],
        params=_params("arbitrary"))


def _group_masks():
    lane = lax.broadcasted_iota(jnp.int32, (1, GROUP * HEAD_DIM), 1)
    return [lane // HEAD_DIM == g for g in range(GROUP)]


def _stack_groups(v, masks, scale=1.0):
    return jnp.concatenate([v * jnp.where(m, scale, 0.0) for m in masks], axis=0)


def _unstack_groups(v, masks):
    out = v[(GROUP - 1) * BLOCK:GROUP * BLOCK]
    for g in range(GROUP - 2, -1, -1):
        out = jnp.where(masks[g], v[g * BLOCK:(g + 1) * BLOCK], out)
    return out


def _repeat_head(kv2, kvh):
    first = lax.broadcasted_iota(jnp.int32, kv2.shape, 1) < HEAD_DIM
    rolled = pltpu.roll(kv2, HEAD_DIM, 1)
    one = jnp.where(first, kv2, rolled) if kvh == 0 else jnp.where(first, rolled, kv2)
    return jnp.concatenate([one, one], axis=1)


def _fold_head(v4):
    a = v4[:, 0:128] + v4[:, 128:256]
    return a + pltpu.roll(a, HEAD_DIM, 1)


ATTN_CONSTS = [pltpu.VMEM((KV_HEADS, GROUP * BLOCK, 2 * BLOCK), F32)]


def _fill_attn_bias(bias_ref):
    row = lax.broadcasted_iota(jnp.int32, (GROUP * BLOCK, 2 * BLOCK), 0)
    col = lax.broadcasted_iota(jnp.int32, (GROUP * BLOCK, 2 * BLOCK), 1)
    dist = (row % BLOCK) + BLOCK - col
    band = (dist >= 0) & (dist < BLOCK)
    rb = lax.broadcasted_iota(jnp.int32, (GROUP * BLOCK, 1), 0) // BLOCK
    for kvh in range(KV_HEADS):
        slope = jnp.zeros((GROUP * BLOCK, 1), F32)
        for g in range(GROUP):
            slope = jnp.where(rb == g, 2.0 ** (-(kvh * GROUP + g + 1)), slope)
        bias_ref[kvh] = jnp.where(band, -slope * dist.astype(F32), NEG)


def _row_sinks(kvh, sink_ref):
    rb = lax.broadcasted_iota(jnp.int32, (GROUP * BLOCK, 1), 0) // BLOCK
    sink = jnp.zeros((GROUP * BLOCK, 1), F32)
    for g in range(GROUP):
        sink = jnp.where(rb == g, sink_ref[0, kvh * GROUP + g], sink)
    return sink


def _attn_probs(qk, k4, bias, sink, no_past, masks):
    qs = _stack_groups(qk, masks, HEAD_DIM ** -0.5).astype(BF16)
    s = lax.dot_general(qs, k4, NT, preferred_element_type=F32) + bias
    s = jnp.concatenate([jnp.where(no_past, NEG, s[:, 0:BLOCK]), s[:, BLOCK:]], axis=1)
    mx = jnp.maximum(jnp.max(s, axis=-1, keepdims=True), sink)
    e = jnp.exp(s - mx)
    es = jnp.exp(sink - mx)
    inv = 1.0 / (jnp.sum(e, axis=-1, keepdims=True) + es)
    return qs, e * inv, es * inv


def _pool_forward(u_ext, g, t0):
    n = u_ext.shape[0] - POOL_HALO
    s = u_ext
    for step in range(g + 1):
        s = s + pltpu.roll(s, 1 << step, 0)
    w = 2 << g
    t = t0 + lax.broadcasted_iota(jnp.int32, (n, 1), 0)
    cnt = jnp.minimum(t + 1, w).astype(F32)
    return s[POOL_HALO:] / cnt - u_ext[POOL_HALO:]


def _layer0_fwd(proj, sinks, pool_w, pool_scale, w_out, x_in, post, comm=None, tq=512):
    t = proj.shape[0]
    nblk = tq // BLOCK

    def body(main_ref, halo_ref, sink_ref, pw_ref, ps_ref, w_ref, x_ref, p_ref, o_ref, y_ref, xo_ref, kv_ref, bias_ref):
        i = pl.program_id(0)
        t0 = i * tq
        masks = _group_masks()

        @pl.when(i == 0)
        def _():
            _fill_attn_bias(bias_ref)

        kv_ref[0:BLOCK, :] = halo_ref[:, COL_K:COL_K + 256]
        kv_ref[BLOCK:, :] = main_ref[:, COL_K:COL_K + 256]

        def block(jb, carry):
            r0 = pl.multiple_of(jb * BLOCK, BLOCK)
            no_past = t0 + r0 == 0
            q = main_ref[pl.ds(r0, BLOCK), COL_Q:COL_Q + ATTN_WIDTH]
            ga = main_ref[pl.ds(r0, BLOCK), COL_GA:COL_GA + ATTN_WIDTH]
            kk = kv_ref[pl.ds(r0, 2 * BLOCK), 0:128]
            vv = kv_ref[pl.ds(r0, 2 * BLOCK), 128:256]
            outs = []
            for kvh in range(KV_HEADS):
                k4 = _repeat_head(kk, kvh).astype(BF16)
                v4 = _repeat_head(vv, kvh).astype(BF16)
                _, p, _ = _attn_probs(q[:, kvh * 256:(kvh + 1) * 256], k4, bias_ref[kvh], _row_sinks(kvh, sink_ref), no_past, masks)
                pv = jnp.dot(p.astype(BF16), v4, preferred_element_type=F32)
                outs.append(_unstack_groups(pv, masks))
            attn = jnp.concatenate(outs, axis=1)
            o_ref[pl.ds(r0, BLOCK), 0:ATTN_WIDTH] = (attn * _silu(ga)).astype(BF16)
            return carry

        lax.fori_loop(0, nblk, block, 0, unroll=True)

        for g in range(POOL_GROUPS):
            cu = COL_U + g * POOL_GC
            cg = COL_GB + g * POOL_GC
            halo_u = jnp.where(i == 0, 0.0, halo_ref[BLOCK - POOL_HALO:BLOCK, cu:cu + POOL_GC])
            u_ext = jnp.concatenate([halo_u, main_ref[:, cu:cu + POOL_GC]], axis=0)
            pooled = _pool_forward(u_ext, g, t0)
            y = jnp.dot(pooled.astype(BF16), pw_ref[g].astype(BF16), preferred_element_type=F32)
            y = y * ps_ref[:, g * POOL_GC:(g + 1) * POOL_GC]
            o_ref[:, ATTN_WIDTH + g * POOL_GC:ATTN_WIDTH + (g + 1) * POOL_GC] =(y * _silu(main_ref[:, cg:cg + POOL_GC])).astype(BF16)

        xo_ref[...] = _project_out(o_ref[...], w_ref, x_ref, p_ref, y_ref)

    row = pl.BlockSpec((tq, D_MODEL), lambda i: (i, 0))
    return _fused_call(
        body, comm, (proj, proj, sinks, pool_w, pool_scale, w_out, x_in, post), name="layer0_fwd", grid=(t // tq,),
        out_shape=[jax.ShapeDtypeStruct((t, D_MODEL), BF16), jax.ShapeDtypeStruct((t, D_MODEL), F32),
                   jax.ShapeDtypeStruct((t, D_MODEL), F32)],
        in_specs=[pl.BlockSpec((tq, EVEN_IN), lambda i: (i, 0)),
                  pl.BlockSpec((BLOCK, EVEN_IN), lambda i: (jnp.maximum(i * nblk - 1, 0), 0)),
                  pl.BlockSpec(memory_space=pltpu.SMEM),
                  _const_spec((POOL_GROUPS, POOL_GC, POOL_GC)), _const_spec((1, POOL_WIDTH)),
                  _const_spec((D_MODEL, D_MODEL)), row, _const_spec((1, D_MODEL))],
        out_specs=[row, row, row],
        scratch_shapes=[pltpu.VMEM((tq + BLOCK, 256), F32)] + ATTN_CONSTS,
        params=_params("arbitrary"))


def _layer0_bwd(proj, gy, y, mix, w_out, post, sinks, pool_w, pool_scale, comm=None, tq=512):
    t = proj.shape[0]
    nt = t // tq
    nblk = tq // BLOCK

    def body(main_ref, halo_ref, next_ref, gy_ref, gyn_ref, y_ref, yn_ref, mix_ref, wo_ref, po_ref,
             sink_ref, pw_ref, ps_ref,
             o_ref, dsk_ref, dpw_ref, dps_ref, dwo16_ref, dpo_ref,
             kv_ref, dkv_ref, carry_ref, bias_ref, dwo_ref, dmix_ref):
        i = pl.program_id(0)
        ii = nt - 1 - i
        t0 = ii * tq
        masks = _group_masks()

        @pl.when(i == 0)
        def _():
            _fill_attn_bias(bias_ref)
            dsk_ref[...] = jnp.zeros_like(dsk_ref)
            dpw_ref[...] = jnp.zeros_like(dpw_ref)
            dps_ref[...] = jnp.zeros_like(dps_ref)
            carry_ref[...] = jnp.zeros_like(carry_ref)

        dmix_ref[...] = _post_bwd_rows(jnp.concatenate([gy_ref[...], gyn_ref[...]], axis=0),
                                       jnp.concatenate([y_ref[...], yn_ref[...]], axis=0), mix_ref[...], tq,
                                       i == 0, i == nt - 1, po_ref, wo_ref, dwo_ref, dwo16_ref, dpo_ref)
        dm_ref = dmix_ref.at[pl.ds(0, tq)]
        dmn_ref = dmix_ref.at[pl.ds(tq, POOL_HALO)]

        kv_ref[0:BLOCK, :] = halo_ref[:, COL_K:COL_K + 256]
        kv_ref[BLOCK:, :] = main_ref[:, COL_K:COL_K + 256]
        dkv_ref[0:tq, :] = jnp.zeros((tq, 256), F32)
        dkv_ref[tq:, :] = carry_ref[...]

        def block(jb, carry):
            r0 = pl.multiple_of(jb * BLOCK, BLOCK)
            no_past = t0 + r0 == 0
            q = main_ref[pl.ds(r0, BLOCK), COL_Q:COL_Q + ATTN_WIDTH]
            ga = main_ref[pl.ds(r0, BLOCK), COL_GA:COL_GA + ATTN_WIDTH]
            dya = dm_ref[pl.ds(r0, BLOCK), 0:ATTN_WIDTH]
            kk = kv_ref[pl.ds(r0, 2 * BLOCK), 0:128]
            vv = kv_ref[pl.ds(r0, 2 * BLOCK), 128:256]
            silu_ga, dsilu_ga = _silu_and_grad(ga)
            do = dya * silu_ga
            first = lax.broadcasted_iota(jnp.int32, (2 * BLOCK, 128), 1) < HEAD_DIM
            attn, dq, dk, dv = [], [], [], []
            for kvh in range(KV_HEADS):
                k4 = _repeat_head(kk, kvh).astype(BF16)
                v4 = _repeat_head(vv, kvh).astype(BF16)
                qs, p, ps = _attn_probs(q[:, kvh * 256:(kvh + 1) * 256], k4, bias_ref[kvh], _row_sinks(kvh, sink_ref), no_past, masks)
                pb = p.astype(BF16)
                o_k = _unstack_groups(jnp.dot(pb, v4, preferred_element_type=F32), masks)
                do_k = do[:, kvh * 256:(kvh + 1) * 256]
                dos = _stack_groups(do_k, masks).astype(BF16)
                prod = do_k * o_k
                delta = jnp.concatenate([jnp.sum(jnp.where(m, prod, 0.0), axis=-1, keepdims=True) for m in masks], axis=0)
                dp = lax.dot_general(dos, v4, NT, preferred_element_type=F32)
                ds = (p * (dp - delta)).astype(BF16)
                sink_term = ps * delta
                for g in range(GROUP):
                    h = kvh * GROUP + g
                    dsk_ref[h:h + 1, :] -= jnp.sum(sink_term[g * BLOCK:(g + 1) * BLOCK], keepdims=True)
                dq.append(_unstack_groups(jnp.dot(ds, k4, preferred_element_type=F32), masks) * (HEAD_DIM ** -0.5))
                dk.append(_fold_head(lax.dot_general(ds, qs, TN, preferred_element_type=F32)))
                dv.append(_fold_head(lax.dot_general(pb, dos, TN, preferred_element_type=F32)))
                attn.append(o_k)
            o_ref[pl.ds(r0, BLOCK), COL_Q:COL_Q + ATTN_WIDTH] = jnp.concatenate(dq, axis=1).astype(BF16)
            o_all = jnp.concatenate(attn, axis=1)
            o_ref[pl.ds(r0, BLOCK), COL_GA:COL_GA + ATTN_WIDTH] = (dya * o_all * dsilu_ga).astype(BF16)
            dkv = jnp.concatenate([jnp.where(first, dk[0], dk[1]), jnp.where(first, dv[0], dv[1])], axis=1)
            dkv_ref[pl.ds(r0, 2 * BLOCK), :] += dkv
            return carry

        lax.fori_loop(0, nblk, block, 0, unroll=True)
        carry_ref[...] = dkv_ref[0:BLOCK, :]
        o_ref[:, COL_K:COL_K + 256] = dkv_ref[BLOCK:, :].astype(BF16)

        last = ii == nt - 1
        for g in range(POOL_GROUPS):
            cu = COL_U + g * POOL_GC
            cg = COL_GB + g * POOL_GC
            cm = ATTN_WIDTH + g * POOL_GC
            pw = pw_ref[g].astype(BF16)
            sc = ps_ref[:, g * POOL_GC:(g + 1) * POOL_GC]
            halo_u = jnp.where(ii == 0, 0.0, halo_ref[BLOCK - POOL_HALO:BLOCK, cu:cu + POOL_GC])
            u_ext = jnp.concatenate([halo_u, main_ref[:, cu:cu + POOL_GC]], axis=0)
            pooled = _pool_forward(u_ext, g, t0).astype(BF16)
            y_raw = jnp.dot(pooled, pw, preferred_element_type=F32)
            gb = main_ref[:, cg:cg + POOL_GC]
            dyb = dm_ref[:, cm:cm + POOL_GC]
            silu_gb, dsilu_gb = _silu_and_grad(gb)
            dypool = dyb * silu_gb
            dps_ref[:, g * POOL_GC:(g + 1) * POOL_GC] += jnp.sum(dypool * y_raw, axis=0, keepdims=True)
            o_ref[:, cg:cg + POOL_GC] = (dyb * (y_raw * sc) * dsilu_gb).astype(BF16)
            dyraw = dypool * sc
            dyraw_n = jnp.where(last, 0.0, dmn_ref[:, cm:cm + POOL_GC] * _silu(next_ref[:, cg:cg + POOL_GC]) * sc)
            dpw_ref[g * POOL_GC:(g + 1) * POOL_GC, :] += lax.dot_general(pooled, dyraw.astype(BF16), TN,
                                                                         preferred_element_type=F32)
            dyraw_ext = jnp.concatenate([dyraw, dyraw_n], axis=0).astype(BF16)
            dpooled = lax.dot_general(dyraw_ext, pw, NT, preferred_element_type=F32)
            w = 2 << g
            tt = t0 + lax.broadcasted_iota(jnp.int32, (tq + POOL_HALO, 1), 0)
            s = dpooled / jnp.minimum(tt + 1, w).astype(F32)
            for step in range(g + 1):
                s = s + pltpu.roll(s, tq + POOL_HALO - (1 << step), 0)
            o_ref[:, cu:cu + POOL_GC] = (s[0:tq] - dpooled[0:tq]).astype(BF16)

    rev = lambda i: nt - 1 - i
    nxt = lambda i: (jnp.minimum((rev(i) + 1) * (tq // POOL_HALO), t // POOL_HALO - 1), 0)
    row = pl.BlockSpec((tq, D_MODEL), lambda i: (rev(i), 0))
    nxt_row = pl.BlockSpec((POOL_HALO, D_MODEL), nxt)
    square = _const_spec((D_MODEL, D_MODEL))
    return _fused_call(
        body, comm, (proj, proj, proj, gy, gy, y, y, mix, w_out, post, sinks, pool_w, pool_scale),
        name="layer0_bwd", grid=(nt,),
        out_shape=[jax.ShapeDtypeStruct((t, EVEN_IN), BF16), jax.ShapeDtypeStruct((8, 128), F32),
                   jax.ShapeDtypeStruct((POOL_GROUPS * POOL_GC, POOL_GC), F32), jax.ShapeDtypeStruct((1, POOL_WIDTH), F32),
                   jax.ShapeDtypeStruct((D_MODEL, D_MODEL), BF16), jax.ShapeDtypeStruct((1, D_MODEL), F32)],
        in_specs=[pl.BlockSpec((tq, EVEN_IN), lambda i: (rev(i), 0)),
                  pl.BlockSpec((BLOCK, EVEN_IN), lambda i: (jnp.maximum(rev(i) * nblk - 1, 0), 0)),
                  pl.BlockSpec((POOL_HALO, EVEN_IN), nxt),
                  row, nxt_row, row, nxt_row, row, square, _const_spec((1, D_MODEL)),
                  pl.BlockSpec(memory_space=pltpu.SMEM),
                  _const_spec((POOL_GROUPS, POOL_GC, POOL_GC)), _const_spec((1, POOL_WIDTH))],
        out_specs=[pl.BlockSpec((tq, EVEN_IN), lambda i: (rev(i), 0)),
                   pl.BlockSpec((8, 128), lambda i: (0, 0)),
                   pl.BlockSpec((POOL_GROUPS * POOL_GC, POOL_GC), lambda i: (0, 0)),
                   pl.BlockSpec((1, POOL_WIDTH), lambda i: (0, 0)), square, pl.BlockSpec((1, D_MODEL), lambda i: (0, 0))],
        scratch_shapes=[pltpu.VMEM((tq + BLOCK, 256), F32), pltpu.VMEM((tq + BLOCK, 256), F32),
                        pltpu.VMEM((BLOCK, 256), F32)] + ATTN_CONSTS
        + [pltpu.VMEM((D_MODEL, D_MODEL), F32), pltpu.VMEM((tq + POOL_HALO, D_MODEL), F32)],
        params=_params("arbitrary"))


CONV_RC = 32
CONV_CC = 128
CONV_CHAINS_FWD = 4
CONV_CHAINS_BWD = 2
CONV_UNROLL = 2


def _fill_shifted(s_ref, rows):
    for b in range(1, 8):
        s_ref[b, 0:rows - 8, :] = s_ref[0, b:b + rows - 8, :]


def _tap_blocks(s_ref, r, cols, lead):
    for b in range(8):
        taps = [(a, 8 * a + b - lead) for a in range(5) if 0 <= 8 * a + b - lead < CONV_K]
        span = 8 * max(a for a, _ in taps) + CONV_RC
        blk = s_ref[b, pl.ds(r, span), cols]
        for a, k in taps:
            yield k, blk[8 * a:8 * a + CONV_RC]


def _conv_taps(s_ref, w_ref, r, cols, lead, reverse, chains):
    accs = [None] * chains
    for n, (k, blk) in enumerate(_tap_blocks(s_ref, r, cols, lead)):
        kw = CONV_K - 1 - k if reverse else k
        term = blk * w_ref[kw:kw + 1, cols]
        accs[n % chains] = term if accs[n % chains] is None else accs[n % chains] + term
    while len(accs) > 1:
        accs = [a + b for a, b in zip(accs[0::2], accs[1::2])]
    return accs[0]


def _layer_norm_fwd(cf, lng, lnb):
    mu = jnp.mean(cf, axis=-1, keepdims=True)
    xc = cf - mu
    rstd = lax.rsqrt(jnp.mean(xc * xc, axis=-1, keepdims=True) + EPS)
    chat = xc * rstd
    return chat, rstd, chat * lng + lnb


def _layer1_fwd(proj, dw, dwb, lng, lnb, w_out, x_in, post, target, tt=256):
    t = proj.shape[0]
    lead = CONV_HALO - (CONV_K - 1)

    def body(main_ref, halo_ref, w_ref, b_ref, g_ref, lb_ref, wo_ref, x_ref, p_ref, t_ref,
             o_ref, c_ref, y_ref, dl_ref, l_ref, gs_ref):
        i = pl.program_id(0)
        hv = halo_ref[...]
        gs_ref[0, 0:CONV_HALO, :] = jnp.where(i == 0, 0.0, hv[:, GLU_A] * _sigmoid(hv[:, GLU_B]))
        gs_ref[0, CONV_HALO:CONV_HALO + tt, :] = main_ref[:, GLU_A] * _sigmoid(main_ref[:, GLU_B])
        _fill_shifted(gs_ref, tt + CONV_HALO)

        for c in range(D_MODEL // CONV_CC):
            cols = slice(c * CONV_CC, (c + 1) * CONV_CC)

            def chunk(j, carry):
                r = pl.multiple_of(j * CONV_RC, CONV_RC)
                c_ref[pl.ds(r, CONV_RC), cols] = _conv_taps(gs_ref, w_ref, r, cols, lead, False, CONV_CHAINS_FWD) + b_ref[:, cols]
                return carry
            lax.fori_loop(0, tt // CONV_RC, chunk, 0, unroll=CONV_UNROLL)

        _, _, cn = _layer_norm_fwd(c_ref[...], g_ref[...], lb_ref[...])
        o_ref[...] = (_silu(cn) * _silu(main_ref[:, GATE])).astype(BF16)

        d = _project_out(o_ref[...], wo_ref, x_ref, p_ref, y_ref) - t_ref[...]
        dl_ref[...] = d * (1.0 / D_MODEL)

        @pl.when(i == 0)
        def _():
            l_ref[...] = jnp.zeros_like(l_ref)

        l_ref[...] += 0.5 * jnp.sum(jnp.mean(d * d, axis=-1, keepdims=True))

    vec = _const_spec((1, D_MODEL))
    row = pl.BlockSpec((tt, D_MODEL), lambda i: (i, 0))
    f32_rows = jax.ShapeDtypeStruct((t, D_MODEL), F32)
    return pl.pallas_call(
        body, name="layer1_fwd", grid=(t // tt,),
        out_shape=[jax.ShapeDtypeStruct((t, D_MODEL), BF16), f32_rows, f32_rows, f32_rows,
                   jax.ShapeDtypeStruct((8, 128), F32)],
        in_specs=[pl.BlockSpec((tt, 3 * D_MODEL), lambda i: (i, 0)),
                  pl.BlockSpec((CONV_HALO, 3 * D_MODEL), lambda i: (jnp.maximum(i * (tt // CONV_HALO) - 1, 0), 0)),
                  _const_spec((CONV_K, D_MODEL)), vec, vec, vec,
                  _const_spec((D_MODEL, D_MODEL)), row, vec, row],
        out_specs=[row, row, row, row, pl.BlockSpec((8, 128), lambda i: (0, 0))],
        scratch_shapes=[pltpu.VMEM((8, tt + CONV_HALO, D_MODEL), F32)],
        compiler_params=_params("arbitrary"),
    )(proj, proj, dw, dwb, lng, lnb, w_out, x_in, post, target)


def _layer1_bwd(proj, cf, gy, y, z, w_out, post, dw, lng, lnb, comm=None, tt=256):
    t = proj.shape[0]
    nt = t // tt
    te = tt + CONV_HALO

    def body(main_ref, next_ref, cf_ref, cfn_ref, gy_ref, gyn_ref, y_ref, yn_ref, z_ref, wo_ref, po_ref,
             w_ref, g_ref, lb_ref,
             o_ref, ddw_ref, ddb_ref, dg_ref, dlb_ref, dwo16_ref, dpo_ref, ds_ref, glu_ref, sb_ref, dwo_ref):
        i = pl.program_id(0)

        @pl.when(i == 0)
        def _():
            ddw_ref[...] = jnp.zeros_like(ddw_ref)
            ddb_ref[...] = jnp.zeros_like(ddb_ref)
            dg_ref[...] = jnp.zeros_like(dg_ref)
            dlb_ref[...] = jnp.zeros_like(dlb_ref)

        dzv = _post_bwd_rows(jnp.concatenate([gy_ref[...], gyn_ref[...]], axis=0),
                             jnp.concatenate([y_ref[...], yn_ref[...]], axis=0), z_ref[...], tt, i == 0, i == nt - 1,
                             po_ref, wo_ref, dwo_ref, dwo16_ref, dpo_ref)
        dzv = jnp.concatenate([dzv[0:tt], jnp.where(i < nt - 1, dzv[tt:], 0.0)], axis=0)
        lng = g_ref[...]
        chat, rstd, cn = _layer_norm_fwd(jnp.concatenate([cf_ref[...], cfn_ref[...]], axis=0), lng, lb_ref[...])
        gate = jnp.concatenate([main_ref[:, GATE], next_ref[:, GATE]], axis=0)
        silu_cn, dsilu_cn = _silu_and_grad(cn)
        silu_gate, dsilu_gate = _silu_and_grad(gate)
        o_ref[:, GATE] = (dzv * silu_cn * dsilu_gate)[0:tt].astype(BF16)
        dcn = dzv * silu_gate * dsilu_cn
        dg_ref[...] += jnp.sum((dcn * chat)[0:tt], axis=0, keepdims=True)
        dlb_ref[...] += jnp.sum(dcn[0:tt], axis=0, keepdims=True)
        dchat = dcn * lng
        dcf = rstd * (dchat - jnp.mean(dchat, axis=-1, keepdims=True) - chat * jnp.mean(dchat * chat, axis=-1, keepdims=True))
        ddb_ref[...] += jnp.sum(dcf[0:tt], axis=0, keepdims=True)
        ds_ref[0, 0:te, :] = dcf
        ds_ref[0, te:, :] = jnp.zeros((8, D_MODEL), F32)
        _fill_shifted(ds_ref, te + 8)
        sb_ref[...] = _sigmoid(main_ref[:, GLU_B])
        glu_ref[...] = main_ref[:, GLU_A] * sb_ref[...]

        for c in range(D_MODEL // CONV_CC):
            cols = slice(c * CONV_CC, (c + 1) * CONV_CC)
            gcols = slice(D_MODEL + c * CONV_CC, D_MODEL + (c + 1) * CONV_CC)

            def chunk(j, carry):
                r = pl.multiple_of(j * CONV_RC, CONV_RC)
                dglu = _conv_taps(ds_ref, w_ref, r, cols, 0, True, CONV_CHAINS_BWD)
                sb = sb_ref[pl.ds(r, CONV_RC), cols]
                o_ref[pl.ds(r, CONV_RC), cols] = (dglu * sb).astype(BF16)
                o_ref[pl.ds(r, CONV_RC), gcols] = (dglu * glu_ref[pl.ds(r, CONV_RC), cols] * (1.0 - sb)).astype(BF16)
                return carry
            lax.fori_loop(0, tt // CONV_RC, chunk, 0, unroll=CONV_UNROLL)

            def taps(j, accs):
                r = pl.multiple_of(j * CONV_RC, CONV_RC)
                gl = glu_ref[pl.ds(r, CONV_RC), cols]
                new = list(accs)
                for m, blk in _tap_blocks(ds_ref, r, cols, 0):
                    prod = blk * gl
                    part = prod[0:8]
                    for q in range(1, CONV_RC // 8):
                        part = part + prod[8 * q:8 * q + 8]
                    new[m] = new[m] + part
                return tuple(new)
            accs = lax.fori_loop(0, tt // CONV_RC, taps, tuple(jnp.zeros((8, CONV_CC), F32) for _ in range(CONV_K)))
            for m in range(CONV_K):
                k = CONV_K - 1 - m
                ddw_ref[k:k + 1, cols] += jnp.sum(accs[m], axis=0, keepdims=True)

    vec = _const_spec((1, D_MODEL))
    vec_out = pl.BlockSpec((1, D_MODEL), lambda i: (0, 0))
    row = pl.BlockSpec((tt, D_MODEL), lambda i: (i, 0))
    nxt = lambda i: (jnp.minimum((i + 1) * (tt // CONV_HALO), t // CONV_HALO - 1), 0)
    nxt_row = pl.BlockSpec((CONV_HALO, D_MODEL), nxt)
    vec_f32 = jax.ShapeDtypeStruct((1, D_MODEL), F32)
    square = _const_spec((D_MODEL, D_MODEL))
    return _fused_call(
        body, comm, (proj, proj, cf, cf, gy, gy, y, y, z, w_out, post, dw, lng, lnb), name="layer1_bwd", grid=(nt,),
        out_shape=[jax.ShapeDtypeStruct((t, 3 * D_MODEL), BF16), jax.ShapeDtypeStruct((CONV_K, D_MODEL), F32),
                   vec_f32, vec_f32, vec_f32, jax.ShapeDtypeStruct((D_MODEL, D_MODEL), BF16), vec_f32],
        in_specs=[pl.BlockSpec((tt, 3 * D_MODEL), lambda i: (i, 0)),
                  pl.BlockSpec((CONV_HALO, 3 * D_MODEL), nxt),
                  row, nxt_row, row, nxt_row, row, nxt_row, row, square, vec,
                  _const_spec((CONV_K, D_MODEL)), vec, vec],
        out_specs=[pl.BlockSpec((tt, 3 * D_MODEL), lambda i: (i, 0)),
                   pl.BlockSpec((CONV_K, D_MODEL), lambda i: (0, 0)), vec_out, vec_out, vec_out, square, vec_out],
        scratch_shapes=[pltpu.VMEM((8, te + 8, D_MODEL), F32), pltpu.VMEM((tt, D_MODEL), F32),
                        pltpu.VMEM((tt, D_MODEL), F32), pltpu.VMEM((D_MODEL, D_MODEL), F32)],
        params=_params("arbitrary"))


def _piece_sums(tensors, place, name):
    counts = [len(parts) for parts in tensors]

    def body(p_ref, *refs):
        ins, outs = refs[:sum(counts)], refs[sum(counts):]
        pos = 0
        for count, o_ref in zip(counts, outs):
            acc = ins[pos][0].astype(F32)
            for part in ins[pos + 1:pos + count]:
                acc = acc + part[0].astype(F32)
            o_ref[0] = acc
            pos += count

    spec = lambda a, slot: pl.BlockSpec((1,) + a.shape[1:], lambda j, p_ref: (slot(p_ref), 0, 0))
    return pl.pallas_call(
        body, name=name,
        grid_spec=pltpu.PrefetchScalarGridSpec(
            num_scalar_prefetch=1, grid=(1,),
            in_specs=[spec(a, slot) for parts in tensors for a, slot in parts],
            out_specs=[pl.BlockSpec((1,) + parts[0][0].shape[1:], lambda j, p_ref: (p_ref[1], 0, 0))
                       for parts in tensors]),
        out_shape=[jax.ShapeDtypeStruct((2,) + parts[0][0].shape[1:], F32) for parts in tensors],
        compiler_params=_params("arbitrary"),
    )(place, *[a for parts in tensors for a, _ in parts])


def _direct_parts(own, recv):
    peer = lambda m: (lambda p: p[0] ^ m)
    return [(own, peer(0))] + [(recv, peer(m)) for m in range(1, N_DEV)]


def _share_with_sibling(halves, small, name):
    n = len(halves)

    def body(*refs):
        small_ref, outs, all_ref = refs[n], refs[n + 1:2 * n + 1], refs[2 * n + 1]
        send_sems, recv_sems, local_sem = refs[2 * n + 2:]
        x, y, c, _ = _place()
        me = 4 * x + 2 * y + c
        send = [pltpu.make_async_remote_copy(
            src_ref=outs[t].at[c], dst_ref=outs[t].at[c], send_sem=send_sems.at[t], recv_sem=recv_sems.at[t],
            device_id=(x, y, 1 - c), device_id_type=MESH_ID) for t in range(n)]
        recv = [pltpu.make_async_remote_copy(
            src_ref=outs[t].at[c], dst_ref=outs[t].at[1 - c], send_sem=send_sems.at[t], recv_sem=recv_sems.at[t],
            device_id=(x, y, 1 - c), device_id_type=MESH_ID) for t in range(n)]
        for m in range(1, N_DEV):
            px, py, pc = x ^ (m >> 2), y ^ ((m >> 1) & 1), c ^ (m & 1)
            sems = dict(send_sem=send_sems.at[n + m - 1], recv_sem=recv_sems.at[n + m - 1], device_id=(px, py, pc),
                        device_id_type=MESH_ID)
            send.append(pltpu.make_async_remote_copy(src_ref=small_ref, dst_ref=all_ref.at[me], **sems))
            recv.append(pltpu.make_async_remote_copy(src_ref=small_ref, dst_ref=all_ref.at[4 * px + 2 * py + pc], **sems))
        mine = pltpu.make_async_copy(small_ref, all_ref.at[me], local_sem)
        mine.start()
        for cp in send:
            cp.start()
        for cp in recv:
            cp.wait_recv()
        for cp in send:
            cp.wait_send()
        mine.wait()

    k = n + N_DEV - 1
    return pl.pallas_call(
        body, name=name,
        out_shape=[jax.ShapeDtypeStruct(h.shape, h.dtype) for h in halves]
        + [jax.ShapeDtypeStruct((N_DEV,) + small.shape, small.dtype)],
        in_specs=[ANY] * (n + 1), out_specs=[ANY] * (n + 1),
        input_output_aliases={t: t for t in range(n)},
        scratch_shapes=[pltpu.SemaphoreType.DMA((k,)), pltpu.SemaphoreType.DMA((k,)), pltpu.SemaphoreType.DMA],
    )(*halves, small)


def _sum8(parts, name):
    n = len(parts)

    def body(*refs):
        for p_ref, o_ref in zip(refs[:n], refs[n:]):
            acc = p_ref[0]
            for k in range(1, N_DEV):
                acc = acc + p_ref[k]
            o_ref[...] = acc

    whole = pl.BlockSpec(memory_space=pltpu.VMEM)
    return pl.pallas_call(
        body, name=name, out_shape=[jax.ShapeDtypeStruct(p.shape[1:], F32) for p in parts],
        in_specs=[whole] * n, out_specs=[whole] * n,
    )(*parts)


def _adamw(w, g, m, v, name):
    shape = w.shape
    cols = shape[-1]
    rows = w.size // cols
    rt = 256 if rows % 256 == 0 else rows
    if w.ndim >= 3 and shape[-2] == 1:
        view, blk, at = (rows, 1, cols), (rows, 1, cols), (lambda i: (0, 0, 0))
    else:
        view, blk, at = (rows, cols), (rt, cols), (lambda i: (i, 0))

    def body(w_ref, g_ref, m_ref, v_ref, d_ref, nm_ref, nv_ref):
        gv = g_ref[...]
        mn = ADAM_B1 * m_ref[...] + (1.0 - ADAM_B1) * gv
        vn = ADAM_B2 * v_ref[...] + (1.0 - ADAM_B2) * (gv * gv)
        m_hat = mn / (1.0 - ADAM_B1 ** ADAM_STEP)
        v_hat = vn / (1.0 - ADAM_B2 ** ADAM_STEP)
        d_ref[...] = -ADAM_LR * (m_hat / (jnp.sqrt(v_hat) + ADAM_EPS) + ADAM_WD * w_ref[...])
        nm_ref[...] = mn
        nv_ref[...] = vn

    spec = pl.BlockSpec(blk, at)
    outs = pl.pallas_call(
        body, name=name, grid=(view[0] // blk[0],),
        out_shape=[jax.ShapeDtypeStruct(view, F32)] * 3,
        in_specs=[spec] * 4, out_specs=[spec] * 3,
        compiler_params=_params("parallel"),
    )(*[a.reshape(view) for a in (w, g, m, v)])
    return [o.reshape(shape) for o in outs]


SMALL_ROWS = 832


def _pack_small(g):
    parts = [g["loss"], g["pre1"].reshape(8, 128), g["post0"].reshape(8, 128),
             g["post1"].reshape(8, 128), g["sinks"], jnp.pad(g["pool_scale"].reshape(4, 128), ((0, 4), (0, 0))),
             g["pool_w"], g["dw"].reshape(248, 128), g["dwb"].reshape(8, 128), g["lng"].reshape(8, 128),
             g["lnb"].reshape(8, 128)]
    assert sum(p.shape[0] for p in parts) == SMALL_ROWS
    return jnp.concatenate(parts, axis=0)


def _unpack_small(s):
    out, r = {}, 0
    for key, rows, shape in (("loss", 8, (8, 128)), ("pre1", 8, (1, D_MODEL)), ("post", 16, (2, D_MODEL)),
                             ("sinks", 8, (8, 128)),
                             ("pool_scale", 4, (1, POOL_WIDTH)), ("pad", 4, (4, 128)), ("pool_w", 512, (1, 4, 128, 128)),
                             ("dw", 248, (CONV_K, D_MODEL)), ("dwb", 8, (1, D_MODEL)), ("lng", 8, (1, D_MODEL)),
                             ("lnb", 8, (1, D_MODEL))):
        out[key] = s[r:r + rows].reshape(shape)
        r += rows
    return out


def kernel(x, pre_norm, post_norm, a_w_in, a_sinks, b_pool_w, b_pool_scale, ab_w_out, c_w_in, c_dw_w, c_dw_b, c_ln_g, c_ln_b, c_w_out, loss_target, m_pre_norm, m_post_norm, m_a_w_in, m_a_sinks, m_b_pool_w, m_b_pool_scale, m_ab_w_out, m_c_w_in, m_c_dw_w, m_c_dw_b, m_c_ln_g, m_c_ln_b, m_c_w_out, v_pre_norm, v_post_norm, v_a_w_in, v_a_sinks, v_b_pool_w, v_b_pool_scale, v_ab_w_out, v_c_w_in, v_c_dw_w, v_c_dw_b, v_c_ln_g, v_c_ln_b, v_c_w_out):
    ix, iy = lax.axis_index("x"), lax.axis_index("y")
    chip_cols = (2 * ix + iy) * 256

    pad8 = lambda v: jnp.pad(v, ((0, -v.shape[0] % 8), (0, 0)))
    vec_shard = jnp.concatenate([pad8(c_dw_w.reshape(CONV_K, 256)), pad8(c_dw_b), pad8(c_ln_g), pad8(c_ln_b),
                                 jnp.zeros((8, 256), F32)], axis=0)
    x0, target = x[0], loss_target[0]
    pre0, pre1 = pre_norm[0:1], pre_norm[1:2]
    post0, post1 = post_norm[0:1], post_norm[1:2]
    pool_w = b_pool_w[0]

    (wa_t,) = _run_comm(_Gather([a_w_in[0].T.astype(BF16)], halve=True), "gather_a_w_in")
    wa_t = wa_t.reshape(EVEN_IN, D_MODEL)
    proj0, (w_ab,) = _norm_matmul(x0, pre0, wa_t, "proj0_fwd", comm=_Gather([ab_w_out[0].astype(BF16)], halve=True))
    w_ab = w_ab.reshape(D_MODEL, D_MODEL)
    (mix0, y0, x1), (wc_t, w_c, vecs) = _layer0_fwd(
        proj0, a_sinks, pool_w, b_pool_scale, w_ab, x0, post0,
        comm=_Gather([c_w_in[0].T.astype(BF16), c_w_out[0].astype(BF16), vec_shard], halve=True))
    wc_t = wc_t.reshape(3 * D_MODEL, D_MODEL)
    w_c = w_c.reshape(D_MODEL, D_MODEL)
    vecs = vecs.reshape(4, 64, 256).transpose(1, 0, 2).reshape(64, D_MODEL)
    dw, dwb, lng, lnb = vecs[0:CONV_K], vecs[32:33], vecs[40:41], vecs[48:49]
    proj1, _ = _norm_matmul(x1, pre1, wc_t, "proj1_fwd")
    z1, cf1, y1, g2, loss = _layer1_fwd(proj1, dw, dwb, lng, lnb, w_c, x1, post1, target)

    pieces = lambda m: m.reshape(N_DEV, m.shape[0] // N_DEV, D_MODEL)
    (dproj1, d_dw, d_dwb, d_lng, d_lnb, d_wc, d_post1), _ = _layer1_bwd(proj1, cf1, g2, y1, z1, w_c, post1, dw, lng, lnb)
    (g1, d_wct, d_pre1), _ = _pre_bwd(dproj1, wc_t, x1, pre1, g2, "proj1_bwd")
    (dproj0, d_sinks, d_pw, d_ps, d_wab, d_post0), (r_wc, r_wct) = _layer0_bwd(
        proj0, g1, y0, mix0, w_ab, post0, a_sinks, pool_w, b_pool_scale,
        comm=_Scatter([pieces(d_wc), pieces(d_wct)]))
    g = dict(loss=loss, pre1=d_pre1, post0=d_post0, post1=d_post1, sinks=d_sinks, pool_w=d_pw, pool_scale=d_ps,
             dw=d_dw, dwb=d_dwb, lng=d_lng, lnb=d_lnb)
    d_wat, (small8, r_wab) = _proj_dw(dproj0, x0, pre0, "proj0_dw",
                                      comm=_Comms(_Gather([_pack_small(g)], halve=False), _Scatter([pieces(d_wab)])))
    sent = _scatter_start(pieces(d_wat), "scatter_a_start")
    (gx, d_pre0), _ = _proj_dx(dproj0, wa_t, x0, pre0 + sent[4][0:1, 0:1], g1, "proj0_dx")
    own_wat, r_wat = _scatter_wait(*sent[:4], d_pre0, "scatter_a_wait")

    ic = lax.axis_index("c")
    me = 4 * ix + 2 * iy + ic
    place = jnp.stack([me, ic]).astype(jnp.int32)
    parts = [_direct_parts(own_wat, r_wat), _direct_parts(pieces(d_wab), r_wab), _direct_parts(pieces(d_wct), r_wct),
             _direct_parts(pieces(d_wc), r_wc)]
    halves = _piece_sums(parts, place, "grad_sums")
    *shared, pre0_8 = _share_with_sibling(halves, d_pre0.reshape(8, 128), "grad_share")
    g_wa_t, g_wab, g_wc_t, g_wc = [h.reshape(2 * h.shape[1], D_MODEL) for h in shared]
    g_c_w_in = g_wc_t.T[None]
    g_ab_w_out = g_wab[None]
    g_c_w_out = g_wc[None]

    small, pre0 = _sum8([small8, pre0_8], "small_sums")
    s = _unpack_small(small)
    layer = lax.broadcasted_iota(jnp.int32, (2, D_MODEL), 0)
    g_pre = jnp.where(layer == 0, pre0.reshape(1, D_MODEL), s["pre1"])
    g_post = s["post"]
    g_sinks = s["sinks"][:, 0].reshape(1, 8)
    g_pool_w, g_pool_scale = s["pool_w"], s["pool_scale"]
    g_dw = lax.dynamic_slice_in_dim(s["dw"], chip_cols, 256, axis=1).reshape(1, CONV_K, 1, 256)
    g_dwb = lax.dynamic_slice_in_dim(s["dwb"], chip_cols, 256, axis=1)
    g_lng = lax.dynamic_slice_in_dim(s["lng"], chip_cols, 256, axis=1)
    g_lnb = lax.dynamic_slice_in_dim(s["lnb"], chip_cols, 256, axis=1)

    turn = lambda a: jnp.swapaxes(a, 1, 2)
    a_w_in, m_a_w_in, v_a_w_in = turn(a_w_in), turn(m_a_w_in), turn(v_a_w_in)
    grads = [g_pre, g_post, g_wa_t[None], g_sinks, g_pool_w, g_pool_scale, g_ab_w_out, g_c_w_in, g_dw, g_dwb, g_lng,
             g_lnb, g_c_w_out]
    weights = [pre_norm, post_norm, a_w_in, a_sinks, b_pool_w, b_pool_scale, ab_w_out, c_w_in, c_dw_w, c_dw_b, c_ln_g,
               c_ln_b, c_w_out]
    moms = [m_pre_norm, m_post_norm, m_a_w_in, m_a_sinks, m_b_pool_w, m_b_pool_scale, m_ab_w_out, m_c_w_in, m_c_dw_w,
            m_c_dw_b, m_c_ln_g, m_c_ln_b, m_c_w_out]
    vars_ = [v_pre_norm, v_post_norm, v_a_w_in, v_a_sinks, v_b_pool_w, v_b_pool_scale, v_ab_w_out, v_c_w_in, v_c_dw_w,
             v_c_dw_b, v_c_ln_g, v_c_ln_b, v_c_w_out]
    deltas, new_m, new_v = [], [], []
    for k, (w, gr, m, v) in enumerate(zip(weights, grads, moms, vars_)):
        d, nm, nv = _adamw(w, gr, m, v, f"adamw{k}")
        deltas.append(d)
        new_m.append(nm)
        new_v.append(nv)
    for outs in (grads, deltas, new_m, new_v):
        outs[2] = turn(outs[2])
    return (s["loss"][0, 0], gx[None], *grads, *deltas, *new_m, *new_v)
```

```python
import jax
import jax.numpy as jnp
from jax import lax
from jax.experimental import pallas as pl
from jax.experimental.pallas import tpu as pltpu

F32 = jnp.float32
BF16 = jnp.bfloat16

D_MODEL = 1024
EPS = 1e-6
NEG = -1e30
HEAD_DIM = 64
GROUP = 4
KV_HEADS = 2
BLOCK = 128
EVEN_IN = 2304
ATTN_WIDTH = 512
POOL_WIDTH = 512
COL_Q, COL_K, COL_GA, COL_U, COL_GB = 0, 512, 768, 1280, 1792
POOL_GROUPS = 4
POOL_GC = 128
POOL_HALO = 16
CONV_K = 31
CONV_HALO = 32
GLU_A = slice(0, D_MODEL)
GLU_B = slice(D_MODEL, 2 * D_MODEL)
GATE = slice(2 * D_MODEL, 3 * D_MODEL)
N_DEV = 8

ADAM_LR = 0.001
ADAM_B1 = 0.9
ADAM_B2 = 0.999
ADAM_EPS = 1e-08
ADAM_WD = 0.01
ADAM_STEP = 10

VMEM_LIMIT_BYTES = 56 * 1024 * 1024

NT = (((1,), (1,)), ((), ()))
TN = (((0,), (0,)), ((), ()))
MESH_ID = pl.DeviceIdType.MESH


def _params(*sem):
    return pltpu.CompilerParams(dimension_semantics=sem, vmem_limit_bytes=VMEM_LIMIT_BYTES)


def _const_spec(shape):
    nd = len(shape)
    return pl.BlockSpec(shape, lambda *_: (0,) * nd, pipeline_mode=pl.Buffered(1))


def _sigmoid(v):
    return 0.5 * jnp.tanh(0.5 * v) + 0.5


def _silu(v):
    h = 0.5 * v
    return h * jnp.tanh(h) + h


def _silu_and_grad(v):
    s = _sigmoid(v)
    silu = v * s
    return silu, s + silu * (1.0 - s)


ANY = pl.BlockSpec(memory_space=pl.ANY)


def _place():
    x, y, c = lax.axis_index("x"), lax.axis_index("y"), lax.axis_index("c")
    chips = [(1 - x, y), (x, 1 - y), (1 - x, 1 - y)]
    return x, y, c, chips


class _Gather:
    def __init__(self, blocks, halve):
        self.ins = list(blocks)
        self.halve = halve
        self.n = n = len(blocks)
        self.shapes = [((b.shape[0] // 2) if halve else b.shape[0], b.shape[1]) for b in blocks]
        self.out_shape = [jax.ShapeDtypeStruct((N_DEV, r, cols), b.dtype) for (r, cols), b in zip(self.shapes, blocks)]
        self.scratch = [pltpu.SemaphoreType.DMA((7 * n,)), pltpu.SemaphoreType.DMA((7 * n,)),
                        pltpu.SemaphoreType.DMA((n,))]

    def _copies(self, ins, outs, sems):
        send_sems, recv_sems, local_sems = sems
        x, y, c, chips = _place()
        me, sibling = (x, y, c), (x, y, 1 - c)

        def piece(t, px, py, pc):
            return outs[t].at[4 * px + 2 * py + pc]

        def own(t):
            return ins[t].at[pl.ds(c * self.shapes[t][0], self.shapes[t][0])] if self.halve else ins[t]

        def copy(t, k, block, to, src=None):
            return pltpu.make_async_remote_copy(
                src_ref=piece(t, *block) if src is None else src, dst_ref=piece(t, *block),
                send_sem=send_sems.at[7 * t + k], recv_sem=recv_sems.at[7 * t + k],
                device_id=to, device_id_type=MESH_ID)

        rng = range(self.n)
        return dict(
            mine=[pltpu.make_async_copy(own(t), piece(t, *me), local_sems.at[t]) for t in rng],
            first=[copy(t, 0, me, sibling, src=own(t)) for t in rng]
            + [copy(t, 1 + j, me, (*chip, c), src=own(t)) for t in rng for j, chip in enumerate(chips)],
            landed=[copy(t, 1 + j, (*chip, c), me) for j, chip in enumerate(chips) for t in rng],
            passed=[copy(t, 4 + j, (*chip, c), sibling) for j, chip in enumerate(chips) for t in rng],
            from_sibling=[copy(t, 0, sibling, me) for t in rng]
            + [copy(t, 4 + j, (*chip, 1 - c), me) for t in rng for j, chip in enumerate(chips)])

    def start(self, ins, outs, sems):
        d = self._copies(ins, outs, sems)
        for cp in d["mine"] + d["first"]:
            cp.start()

    def middle(self, ins, outs, sems):
        d = self._copies(ins, outs, sems)
        for got, fwd in zip(d["landed"], d["passed"]):
            got.wait_recv()
            fwd.start()

    def finish(self, ins, outs, sems):
        d = self._copies(ins, outs, sems)
        for cp in d["from_sibling"]:
            cp.wait_recv()
        for cp in d["first"] + d["passed"]:
            cp.wait_send()
        for cp in d["mine"]:
            cp.wait()


class _Scatter:
    def __init__(self, tensors):
        self.ins = list(tensors)
        self.n = n = len(tensors)
        self.out_shape = [jax.ShapeDtypeStruct(t.shape, t.dtype) for t in tensors]
        self.scratch = [pltpu.SemaphoreType.DMA((7 * n,)), pltpu.SemaphoreType.DMA((7 * n,))]

    def _copies(self, ins, outs, sems):
        send_sems, recv_sems = sems
        x, y, c, _ = _place()
        me = 4 * x + 2 * y + c
        sends, recvs = [], []
        for t in range(self.n):
            for m in range(1, N_DEV):
                px, py, pc = x ^ (m >> 2), y ^ ((m >> 1) & 1), c ^ (m & 1)
                q = 4 * px + 2 * py + pc
                sems_k = dict(send_sem=send_sems.at[7 * t + m - 1], recv_sem=recv_sems.at[7 * t + m - 1],
                              device_id=(px, py, pc), device_id_type=MESH_ID)
                sends.append(pltpu.make_async_remote_copy(src_ref=ins[t].at[q], dst_ref=outs[t].at[me], **sems_k))
                recvs.append(pltpu.make_async_remote_copy(src_ref=ins[t].at[me], dst_ref=outs[t].at[q], **sems_k))
        return sends, recvs

    def start(self, ins, outs, sems):
        for cp in self._copies(ins, outs, sems)[0]:
            cp.start()

    def middle(self, ins, outs, sems):
        pass

    def finish(self, ins, outs, sems):
        sends, recvs = self._copies(ins, outs, sems)
        for cp in recvs:
            cp.wait_recv()
        for cp in sends:
            cp.wait_send()


class _Comms:
    def __init__(self, *comms):
        self.comms = comms
        self.ins = [a for c in comms for a in c.ins]
        self.out_shape = [s for c in comms for s in c.out_shape]
        self.scratch = [s for c in comms for s in c.scratch]

    def _each(self, phase, ins, outs, sems):
        i = o = s = 0
        for c in self.comms:
            ni, no, ns = len(c.ins), len(c.out_shape), len(c.scratch)
            getattr(c, phase)(ins[i:i + ni], outs[o:o + no], sems[s:s + ns])
            i, o, s = i + ni, o + no, s + ns

    def start(self, ins, outs, sems):
        self._each("start", ins, outs, sems)

    def middle(self, ins, outs, sems):
        self._each("middle", ins, outs, sems)

    def finish(self, ins, outs, sems):
        self._each("finish", ins, outs, sems)


def _run_comm(comm, name):
    n = len(comm.ins)

    def body(*refs):
        parts = refs[:n], refs[n:2 * n], refs[2 * n:]
        comm.start(*parts)
        comm.middle(*parts)
        comm.finish(*parts)

    return pl.pallas_call(body, name=name, out_shape=comm.out_shape, in_specs=[ANY] * n, out_specs=[ANY] * n,
                          scratch_shapes=comm.scratch)(*comm.ins)


HBM_SPEC = pl.BlockSpec(memory_space=pltpu.HBM)
SEM_SPEC = pl.BlockSpec(memory_space=pltpu.SEMAPHORE)
DATAFLOW = pltpu.SideEffectType.DATAFLOW_SIDE_EFFECTING


def _scatter_copies(own_ref, land_ref, send_sems, recv_sems):
    x, y, c, _ = _place()
    me = 4 * x + 2 * y + c
    pairs = []
    for m in range(1, N_DEV):
        px, py, pc = x ^ (m >> 2), y ^ ((m >> 1) & 1), c ^ (m & 1)
        q = 4 * px + 2 * py + pc
        sems = dict(send_sem=send_sems.at[m - 1], recv_sem=recv_sems.at[m - 1], device_id=(px, py, pc),
                    device_id_type=MESH_ID)
        pairs.append((pltpu.make_async_remote_copy(src_ref=own_ref.at[q], dst_ref=land_ref.at[me], **sems),
                      pltpu.make_async_remote_copy(src_ref=own_ref.at[me], dst_ref=land_ref.at[q], **sems)))
    return pairs


def _scatter_start(own, name):
    def body(own_ref, land_ref, send_sems, recv_sems, own_thru, land_thru, token):
        for send, _ in _scatter_copies(own_ref, land_ref, send_sems, recv_sems):
            send.start()
        token[...] = jnp.zeros_like(token)

    buf = pltpu.HBM(own.shape, own.dtype)
    return pl.pallas_call(
        body, name=name,
        out_shape=(pltpu.SemaphoreType.DMA((N_DEV - 1,)), pltpu.SemaphoreType.DMA((N_DEV - 1,)), buf, buf,
                   jax.ShapeDtypeStruct((8, 128), F32)),
        in_specs=(HBM_SPEC, HBM_SPEC),
        out_specs=(SEM_SPEC, SEM_SPEC, HBM_SPEC, HBM_SPEC, pl.BlockSpec(memory_space=pltpu.VMEM)),
        input_output_aliases={0: 2, 1: 3},
        compiler_params=pltpu.CompilerParams(has_side_effects=DATAFLOW),
    )(pltpu.with_memory_space_constraint(own, pltpu.HBM),
      pltpu.with_memory_space_constraint(lax.empty(own.shape, own.dtype), pltpu.HBM))


def _scatter_wait(send_sems, recv_sems, own_thru, land_thru, after, name):
    def body(own_ref, land_ref, send_sems, recv_sems, after_ref, own_out, land_out):
        for send, recv in _scatter_copies(own_ref, land_ref, send_sems, recv_sems):
            send.wait_send()
            recv.wait_recv()

    buf = pltpu.HBM(own_thru.shape, own_thru.dtype)
    return pl.pallas_call(
        body, name=name, out_shape=(buf, buf),
        in_specs=(HBM_SPEC, HBM_SPEC, SEM_SPEC, SEM_SPEC, ANY), out_specs=(HBM_SPEC, HBM_SPEC),
        input_output_aliases={0: 0, 1: 1},
        compiler_params=pltpu.CompilerParams(has_side_effects=DATAFLOW),
    )(own_thru, land_thru, send_sems, recv_sems, after)


def _fused_call(body, comm, args, *, name, grid, out_shape, in_specs, out_specs, scratch_shapes=(), params):
    single = not isinstance(out_shape, (list, tuple))
    out_shape = [out_shape] if single else list(out_shape)
    out_specs = [out_specs] if single else list(out_specs)
    if comm is None:
        res = pl.pallas_call(body, name=name, grid=grid, out_shape=out_shape, in_specs=in_specs, out_specs=out_specs,
                             scratch_shapes=list(scratch_shapes), compiler_params=params)(*args)
        return (res[0] if single else res), []
    n_in, n_out, n_scr = len(in_specs), len(out_shape), len(scratch_shapes)
    c_in, c_out = len(comm.ins), len(comm.out_shape)
    steps = grid[0]

    def fused(*refs):
        pos = 0
        groups = []
        for size in (n_in, c_in, n_out, c_out, n_scr, len(comm.scratch)):
            groups.append(refs[pos:pos + size])
            pos += size
        ins, c_ins, outs, c_outs, scr, c_sems = groups
        i = pl.program_id(0)

        @pl.when(i == 0)
        def _():
            comm.start(c_ins, c_outs, c_sems)

        @pl.when(i == steps // 2)
        def _():
            comm.middle(c_ins, c_outs, c_sems)

        body(*ins, *outs, *scr)

        @pl.when(i == steps - 1)
        def _():
            comm.finish(c_ins, c_outs, c_sems)

    res = pl.pallas_call(
        fused, name=name, grid=grid, out_shape=out_shape + list(comm.out_shape),
        in_specs=list(in_specs) + [ANY] * c_in, out_specs=out_specs + [ANY] * c_out,
        scratch_shapes=list(scratch_shapes) + list(comm.scratch), compiler_params=params)(*args, *comm.ins)
    main = res[:n_out]
    return (main[0] if single else main), list(res[n_out:])


def _norm_matmul(x, gain, wt, name, comm=None, tm=1024):
    t, n = x.shape[0], wt.shape[0]

    def body(x_ref, g_ref, wt_ref, o_ref):
        xv = x_ref[...]
        r = lax.rsqrt(jnp.mean(xv * xv, axis=-1, keepdims=True) + EPS)
        h = (xv * r * g_ref[...]).astype(BF16)
        o_ref[...] = lax.dot_general(h, wt_ref[...], NT, preferred_element_type=F32)

    return _fused_call(
        body, comm, (x, gain, wt), name=name, grid=(t // tm,),
        out_shape=jax.ShapeDtypeStruct((t, n), F32),
        in_specs=[pl.BlockSpec((tm, D_MODEL), lambda i: (i, 0)), _const_spec((1, D_MODEL)), _const_spec((n, D_MODEL))],
        out_specs=pl.BlockSpec((tm, n), lambda i: (i, 0)),
        params=_params("arbitrary"))


def _project_out(a, w_ref, x_ref, p_ref, y_ref):
    y = jnp.dot(a, w_ref[...], preferred_element_type=F32)
    y_ref[...] = y
    ry = lax.rsqrt(jnp.mean(y * y, axis=-1, keepdims=True) + EPS)
    return x_ref[...] + (y * ry) * p_ref[...]


def _post_bwd_rows(g, y, a, n_own, first, last, p_ref, w_ref, dw_ref, dw16_ref, dp_ref):
    @pl.when(first)
    def _():
        dw_ref[...] = jnp.zeros_like(dw_ref)
        dp_ref[...] = jnp.zeros_like(dp_ref)

    ry = lax.rsqrt(jnp.mean(y * y, axis=-1, keepdims=True) + EPS)
    nv = y * ry
    dp_ref[...] += jnp.sum((g * nv)[0:n_own], axis=0, keepdims=True)
    dn = g * p_ref[...]
    dy = (ry * (dn - nv * jnp.mean(dn * nv, axis=-1, keepdims=True))).astype(BF16)
    dw_ref[...] += lax.dot_general(a, dy[0:n_own], TN, preferred_element_type=F32)

    @pl.when(last)
    def _():
        dw16_ref[...] = dw_ref[...].astype(BF16)

    return lax.dot_general(dy, w_ref[...], NT, preferred_element_type=F32)


def _pre_bwd(dproj, wt, x_in, pre, g, name, comm=None, tm=512):
    t, n = dproj.shape
    steps = t // tm

    def body(dp_ref, wt_ref, x_ref, pre_ref, g_ref, dx_ref, dwt16_ref, dpre_ref, dwt_ref):
        @pl.when(pl.program_id(0) == 0)
        def _():
            dwt_ref[...] = jnp.zeros_like(dwt_ref)
            dpre_ref[...] = jnp.zeros_like(dpre_ref)

        dpv = dp_ref[...]
        dh = jnp.dot(dpv, wt_ref[...], preferred_element_type=F32)
        xv = x_ref[...]
        r = lax.rsqrt(jnp.mean(xv * xv, axis=-1, keepdims=True) + EPS)
        xn = xv * r
        pv = pre_ref[...]
        dpre_ref[...] += jnp.sum(dh * xn, axis=0, keepdims=True)
        dxn = dh * pv
        dx_ref[...] = g_ref[...] + r * (dxn - xn * jnp.mean(dxn * xn, axis=-1, keepdims=True))
        h = (xn * pv).astype(BF16)
        dwt_ref[...] += lax.dot_general(dpv, h, TN, preferred_element_type=F32)

        @pl.when(pl.program_id(0) == steps - 1)
        def _():
            dwt16_ref[...] = dwt_ref[...].astype(BF16)

    row = pl.BlockSpec((tm, D_MODEL), lambda i: (i, 0))
    return _fused_call(
        body, comm, (dproj, wt, x_in, pre, g), name=name, grid=(steps,),
        out_shape=[jax.ShapeDtypeStruct((t, D_MODEL), F32), jax.ShapeDtypeStruct((n, D_MODEL), BF16),
                   jax.ShapeDtypeStruct((1, D_MODEL), F32)],
        in_specs=[pl.BlockSpec((tm, n), lambda i: (i, 0)), _const_spec((n, D_MODEL)), row, _const_spec((1, D_MODEL)), row],
        out_specs=[row, _const_spec((n, D_MODEL)), pl.BlockSpec((1, D_MODEL), lambda i: (0, 0))],
        scratch_shapes=[pltpu.VMEM((n, D_MODEL), F32)],
        params=_params("arbitrary"))


def _proj_dw(dproj, x_in, pre, name, comm=None, tm=1024):
    t, n = dproj.shape
    steps = t // tm

    def body(dp_ref, x_ref, pre_ref, dwt16_ref, dwt_ref):
        @pl.when(pl.program_id(0) == 0)
        def _():
            dwt_ref[...] = jnp.zeros_like(dwt_ref)

        xv = x_ref[...]
        r = lax.rsqrt(jnp.mean(xv * xv, axis=-1, keepdims=True) + EPS)
        h = (xv * r * pre_ref[...]).astype(BF16)
        dwt_ref[...] += lax.dot_general(dp_ref[...], h, TN, preferred_element_type=F32)

        @pl.when(pl.program_id(0) == steps - 1)
        def _():
            dwt16_ref[...] = dwt_ref[...].astype(BF16)

    return _fused_call(
        body, comm, (dproj, x_in, pre), name=name, grid=(steps,),
        out_shape=jax.ShapeDtypeStruct((n, D_MODEL), BF16),
        in_specs=[pl.BlockSpec((tm, n), lambda i: (i, 0)), pl.BlockSpec((tm, D_MODEL), lambda i: (i, 0)),
                  _const_spec((1, D_MODEL))],
        out_specs=pl.BlockSpec((n, D_MODEL), lambda i: (0, 0)),
        scratch_shapes=[pltpu.VMEM((n, D_MODEL), F32)],
        params=_params("arbitrary"))


def _proj_dx(dproj, wt, x_in, pre, g, name, comm=None, tm=512):
    t, n = dproj.shape

    def body(dp_ref, wt_ref, x_ref, pre_ref, g_ref, dx_ref, dpre_ref):
        @pl.when(pl.program_id(0) == 0)
        def _():
            dpre_ref[...] = jnp.zeros_like(dpre_ref)

        dh = jnp.dot(dp_ref[...], wt_ref[...], preferred_element_type=F32)
        xv = x_ref[...]
        r = lax.rsqrt(jnp.mean(xv * xv, axis=-1, keepdims=True) + EPS)
        xn = xv * r
        dpre_ref[...] += jnp.sum(dh * xn, axis=0, keepdims=True)
        dxn = dh * pre_ref[...]
        dx_ref[...] = g_ref[...] + r * (dxn - xn * jnp.mean(dxn * xn, axis=-1, keepdims=True))

    row = pl.BlockSpec((tm, D_MODEL), lambda i: (i, 0))
    return _fused_call(
        body, comm, (dproj, wt, x_in, pre, g), name=name, grid=(t // tm,),
        out_shape=[jax.ShapeDtypeStruct((t, D_MODEL), F32), jax.ShapeDtypeStruct((1, D_MODEL), F32)],
        in_specs=[pl.BlockSpec((tm, n), lambda i: (i, 0)), _const_spec((n, D_MODEL)), row, _const_spec((1, D_MODEL)), row],
        out_specs=[row, pl.BlockSpec((1, D_MODEL), lambda i: (0, 0))],
        params=_params("arbitrary"))


def _group_masks():
    lane = lax.broadcasted_iota(jnp.int32, (1, GROUP * HEAD_DIM), 1)
    return [lane // HEAD_DIM == g for g in range(GROUP)]


def _stack_groups(v, masks, scale=1.0):
    return jnp.concatenate([v * jnp.where(m, scale, 0.0) for m in masks], axis=0)


def _unstack_groups(v, masks):
    out = v[(GROUP - 1) * BLOCK:GROUP * BLOCK]
    for g in range(GROUP - 2, -1, -1):
        out = jnp.where(masks[g], v[g * BLOCK:(g + 1) * BLOCK], out)
    return out


def _repeat_head(kv2, kvh):
    first = lax.broadcasted_iota(jnp.int32, kv2.shape, 1) < HEAD_DIM
    rolled = pltpu.roll(kv2, HEAD_DIM, 1)
    one = jnp.where(first, kv2, rolled) if kvh == 0 else jnp.where(first, rolled, kv2)
    return jnp.concatenate([one, one], axis=1)


def _fold_head(v4):
    a = v4[:, 0:128] + v4[:, 128:256]
    return a + pltpu.roll(a, HEAD_DIM, 1)


ATTN_CONSTS = [pltpu.VMEM((KV_HEADS, GROUP * BLOCK, 2 * BLOCK), F32)]


def _fill_attn_bias(bias_ref):
    row = lax.broadcasted_iota(jnp.int32, (GROUP * BLOCK, 2 * BLOCK), 0)
    col = lax.broadcasted_iota(jnp.int32, (GROUP * BLOCK, 2 * BLOCK), 1)
    dist = (row % BLOCK) + BLOCK - col
    band = (dist >= 0) & (dist < BLOCK)
    rb = lax.broadcasted_iota(jnp.int32, (GROUP * BLOCK, 1), 0) // BLOCK
    for kvh in range(KV_HEADS):
        slope = jnp.zeros((GROUP * BLOCK, 1), F32)
        for g in range(GROUP):
            slope = jnp.where(rb == g, 2.0 ** (-(kvh * GROUP + g + 1)), slope)
        bias_ref[kvh] = jnp.where(band, -slope * dist.astype(F32), NEG)


def _row_sinks(kvh, sink_ref):
    rb = lax.broadcasted_iota(jnp.int32, (GROUP * BLOCK, 1), 0) // BLOCK
    sink = jnp.zeros((GROUP * BLOCK, 1), F32)
    for g in range(GROUP):
        sink = jnp.where(rb == g, sink_ref[0, kvh * GROUP + g], sink)
    return sink


def _attn_probs(qk, k4, bias, sink, no_past, masks):
    qs = _stack_groups(qk, masks, HEAD_DIM ** -0.5).astype(BF16)
    s = lax.dot_general(qs, k4, NT, preferred_element_type=F32) + bias
    s = jnp.concatenate([jnp.where(no_past, NEG, s[:, 0:BLOCK]), s[:, BLOCK:]], axis=1)
    mx = jnp.maximum(jnp.max(s, axis=-1, keepdims=True), sink)
    e = jnp.exp(s - mx)
    es = jnp.exp(sink - mx)
    inv = 1.0 / (jnp.sum(e, axis=-1, keepdims=True) + es)
    return qs, e * inv, es * inv


def _pool_forward(u_ext, g, t0):
    n = u_ext.shape[0] - POOL_HALO
    s = u_ext
    for step in range(g + 1):
        s = s + pltpu.roll(s, 1 << step, 0)
    w = 2 << g
    t = t0 + lax.broadcasted_iota(jnp.int32, (n, 1), 0)
    cnt = jnp.minimum(t + 1, w).astype(F32)
    return s[POOL_HALO:] / cnt - u_ext[POOL_HALO:]


def _layer0_fwd(proj, sinks, pool_w, pool_scale, w_out, x_in, post, comm=None, tq=512):
    t = proj.shape[0]
    nblk = tq // BLOCK

    def body(main_ref, halo_ref, sink_ref, pw_ref, ps_ref, w_ref, x_ref, p_ref, o_ref, y_ref, xo_ref, kv_ref, bias_ref):
        i = pl.program_id(0)
        t0 = i * tq
        masks = _group_masks()

        @pl.when(i == 0)
        def _():
            _fill_attn_bias(bias_ref)

        kv_ref[0:BLOCK, :] = halo_ref[:, COL_K:COL_K + 256]
        kv_ref[BLOCK:, :] = main_ref[:, COL_K:COL_K + 256]

        def block(jb, carry):
            r0 = pl.multiple_of(jb * BLOCK, BLOCK)
            no_past = t0 + r0 == 0
            q = main_ref[pl.ds(r0, BLOCK), COL_Q:COL_Q + ATTN_WIDTH]
            ga = main_ref[pl.ds(r0, BLOCK), COL_GA:COL_GA + ATTN_WIDTH]
            kk = kv_ref[pl.ds(r0, 2 * BLOCK), 0:128]
            vv = kv_ref[pl.ds(r0, 2 * BLOCK), 128:256]
            outs = []
            for kvh in range(KV_HEADS):
                k4 = _repeat_head(kk, kvh).astype(BF16)
                v4 = _repeat_head(vv, kvh).astype(BF16)
                _, p, _ = _attn_probs(q[:, kvh * 256:(kvh + 1) * 256], k4, bias_ref[kvh], _row_sinks(kvh, sink_ref), no_past, masks)
                pv = jnp.dot(p.astype(BF16), v4, preferred_element_type=F32)
                outs.append(_unstack_groups(pv, masks))
            attn = jnp.concatenate(outs, axis=1)
            o_ref[pl.ds(r0, BLOCK), 0:ATTN_WIDTH] = (attn * _silu(ga)).astype(BF16)
            return carry

        lax.fori_loop(0, nblk, block, 0, unroll=True)

        for g in range(POOL_GROUPS):
            cu = COL_U + g * POOL_GC
            cg = COL_GB + g * POOL_GC
            halo_u = jnp.where(i == 0, 0.0, halo_ref[BLOCK - POOL_HALO:BLOCK, cu:cu + POOL_GC])
            u_ext = jnp.concatenate([halo_u, main_ref[:, cu:cu + POOL_GC]], axis=0)
            pooled = _pool_forward(u_ext, g, t0)
            y = jnp.dot(pooled.astype(BF16), pw_ref[g].astype(BF16), preferred_element_type=F32)
            y = y * ps_ref[:, g * POOL_GC:(g + 1) * POOL_GC]
            o_ref[:, ATTN_WIDTH + g * POOL_GC:ATTN_WIDTH + (g + 1) * POOL_GC] =(y * _silu(main_ref[:, cg:cg + POOL_GC])).astype(BF16)

        xo_ref[...] = _project_out(o_ref[...], w_ref, x_ref, p_ref, y_ref)

    row = pl.BlockSpec((tq, D_MODEL), lambda i: (i, 0))
    return _fused_call(
        body, comm, (proj, proj, sinks, pool_w, pool_scale, w_out, x_in, post), name="layer0_fwd", grid=(t // tq,),
        out_shape=[jax.ShapeDtypeStruct((t, D_MODEL), BF16), jax.ShapeDtypeStruct((t, D_MODEL), F32),
                   jax.ShapeDtypeStruct((t, D_MODEL), F32)],
        in_specs=[pl.BlockSpec((tq, EVEN_IN), lambda i: (i, 0)),
                  pl.BlockSpec((BLOCK, EVEN_IN), lambda i: (jnp.maximum(i * nblk - 1, 0), 0)),
                  pl.BlockSpec(memory_space=pltpu.SMEM),
                  _const_spec((POOL_GROUPS, POOL_GC, POOL_GC)), _const_spec((1, POOL_WIDTH)),
                  _const_spec((D_MODEL, D_MODEL)), row, _const_spec((1, D_MODEL))],
        out_specs=[row, row, row],
        scratch_shapes=[pltpu.VMEM((tq + BLOCK, 256), F32)] + ATTN_CONSTS,
        params=_params("arbitrary"))


def _layer0_bwd(proj, gy, y, mix, w_out, post, sinks, pool_w, pool_scale, comm=None, tq=512):
    t = proj.shape[0]
    nt = t // tq
    nblk = tq // BLOCK

    def body(main_ref, halo_ref, next_ref, gy_ref, gyn_ref, y_ref, yn_ref, mix_ref, wo_ref, po_ref,
             sink_ref, pw_ref, ps_ref,
             o_ref, dsk_ref, dpw_ref, dps_ref, dwo16_ref, dpo_ref,
             kv_ref, dkv_ref, carry_ref, bias_ref, dwo_ref, dmix_ref):
        i = pl.program_id(0)
        ii = nt - 1 - i
        t0 = ii * tq
        masks = _group_masks()

        @pl.when(i == 0)
        def _():
            _fill_attn_bias(bias_ref)
            dsk_ref[...] = jnp.zeros_like(dsk_ref)
            dpw_ref[...] = jnp.zeros_like(dpw_ref)
            dps_ref[...] = jnp.zeros_like(dps_ref)
            carry_ref[...] = jnp.zeros_like(carry_ref)

        dmix_ref[...] = _post_bwd_rows(jnp.concatenate([gy_ref[...], gyn_ref[...]], axis=0),
                                       jnp.concatenate([y_ref[...], yn_ref[...]], axis=0), mix_ref[...], tq,
                                       i == 0, i == nt - 1, po_ref, wo_ref, dwo_ref, dwo16_ref, dpo_ref)
        dm_ref = dmix_ref.at[pl.ds(0, tq)]
        dmn_ref = dmix_ref.at[pl.ds(tq, POOL_HALO)]

        kv_ref[0:BLOCK, :] = halo_ref[:, COL_K:COL_K + 256]
        kv_ref[BLOCK:, :] = main_ref[:, COL_K:COL_K + 256]
        dkv_ref[0:tq, :] = jnp.zeros((tq, 256), F32)
        dkv_ref[tq:, :] = carry_ref[...]

        def block(jb, carry):
            r0 = pl.multiple_of(jb * BLOCK, BLOCK)
            no_past = t0 + r0 == 0
            q = main_ref[pl.ds(r0, BLOCK), COL_Q:COL_Q + ATTN_WIDTH]
            ga = main_ref[pl.ds(r0, BLOCK), COL_GA:COL_GA + ATTN_WIDTH]
            dya = dm_ref[pl.ds(r0, BLOCK), 0:ATTN_WIDTH]
            kk = kv_ref[pl.ds(r0, 2 * BLOCK), 0:128]
            vv = kv_ref[pl.ds(r0, 2 * BLOCK), 128:256]
            silu_ga, dsilu_ga = _silu_and_grad(ga)
            do = dya * silu_ga
            first = lax.broadcasted_iota(jnp.int32, (2 * BLOCK, 128), 1) < HEAD_DIM
            attn, dq, dk, dv = [], [], [], []
            for kvh in range(KV_HEADS):
                k4 = _repeat_head(kk, kvh).astype(BF16)
                v4 = _repeat_head(vv, kvh).astype(BF16)
                qs, p, ps = _attn_probs(q[:, kvh * 256:(kvh + 1) * 256], k4, bias_ref[kvh], _row_sinks(kvh, sink_ref), no_past, masks)
                pb = p.astype(BF16)
                o_k = _unstack_groups(jnp.dot(pb, v4, preferred_element_type=F32), masks)
                do_k = do[:, kvh * 256:(kvh + 1) * 256]
                dos = _stack_groups(do_k, masks).astype(BF16)
                prod = do_k * o_k
                delta = jnp.concatenate([jnp.sum(jnp.where(m, prod, 0.0), axis=-1, keepdims=True) for m in masks], axis=0)
                dp = lax.dot_general(dos, v4, NT, preferred_element_type=F32)
                ds = (p * (dp - delta)).astype(BF16)
                sink_term = ps * delta
                for g in range(GROUP):
                    h = kvh * GROUP + g
                    dsk_ref[h:h + 1, :] -= jnp.sum(sink_term[g * BLOCK:(g + 1) * BLOCK], keepdims=True)
                dq.append(_unstack_groups(jnp.dot(ds, k4, preferred_element_type=F32), masks) * (HEAD_DIM ** -0.5))
                dk.append(_fold_head(lax.dot_general(ds, qs, TN, preferred_element_type=F32)))
                dv.append(_fold_head(lax.dot_general(pb, dos, TN, preferred_element_type=F32)))
                attn.append(o_k)
            o_ref[pl.ds(r0, BLOCK), COL_Q:COL_Q + ATTN_WIDTH] = jnp.concatenate(dq, axis=1).astype(BF16)
            o_all = jnp.concatenate(attn, axis=1)
            o_ref[pl.ds(r0, BLOCK), COL_GA:COL_GA + ATTN_WIDTH] = (dya * o_all * dsilu_ga).astype(BF16)
            dkv = jnp.concatenate([jnp.where(first, dk[0], dk[1]), jnp.where(first, dv[0], dv[1])], axis=1)
            dkv_ref[pl.ds(r0, 2 * BLOCK), :] += dkv
            return carry

        lax.fori_loop(0, nblk, block, 0, unroll=True)
        carry_ref[...] = dkv_ref[0:BLOCK, :]
        o_ref[:, COL_K:COL_K + 256] = dkv_ref[BLOCK:, :].astype(BF16)

        last = ii == nt - 1
        for g in range(POOL_GROUPS):
            cu = COL_U + g * POOL_GC
            cg = COL_GB + g * POOL_GC
            cm = ATTN_WIDTH + g * POOL_GC
            pw = pw_ref[g].astype(BF16)
            sc = ps_ref[:, g * POOL_GC:(g + 1) * POOL_GC]
            halo_u = jnp.where(ii == 0, 0.0, halo_ref[BLOCK - POOL_HALO:BLOCK, cu:cu + POOL_GC])
            u_ext = jnp.concatenate([halo_u, main_ref[:, cu:cu + POOL_GC]], axis=0)
            pooled = _pool_forward(u_ext, g, t0).astype(BF16)
            y_raw = jnp.dot(pooled, pw, preferred_element_type=F32)
            gb = main_ref[:, cg:cg + POOL_GC]
            dyb = dm_ref[:, cm:cm + POOL_GC]
            silu_gb, dsilu_gb = _silu_and_grad(gb)
            dypool = dyb * silu_gb
            dps_ref[:, g * POOL_GC:(g + 1) * POOL_GC] += jnp.sum(dypool * y_raw, axis=0, keepdims=True)
            o_ref[:, cg:cg + POOL_GC] = (dyb * (y_raw * sc) * dsilu_gb).astype(BF16)
            dyraw = dypool * sc
            dyraw_n = jnp.where(last, 0.0, dmn_ref[:, cm:cm + POOL_GC] * _silu(next_ref[:, cg:cg + POOL_GC]) * sc)
            dpw_ref[g * POOL_GC:(g + 1) * POOL_GC, :] += lax.dot_general(pooled, dyraw.astype(BF16), TN,
                                                                         preferred_element_type=F32)
            dyraw_ext = jnp.concatenate([dyraw, dyraw_n], axis=0).astype(BF16)
            dpooled = lax.dot_general(dyraw_ext, pw, NT, preferred_element_type=F32)
            w = 2 << g
            tt = t0 + lax.broadcasted_iota(jnp.int32, (tq + POOL_HALO, 1), 0)
            s = dpooled / jnp.minimum(tt + 1, w).astype(F32)
            for step in range(g + 1):
                s = s + pltpu.roll(s, tq + POOL_HALO - (1 << step), 0)
            o_ref[:, cu:cu + POOL_GC] = (s[0:tq] - dpooled[0:tq]).astype(BF16)

    rev = lambda i: nt - 1 - i
    nxt = lambda i: (jnp.minimum((rev(i) + 1) * (tq // POOL_HALO), t // POOL_HALO - 1), 0)
    row = pl.BlockSpec((tq, D_MODEL), lambda i: (rev(i), 0))
    nxt_row = pl.BlockSpec((POOL_HALO, D_MODEL), nxt)
    square = _const_spec((D_MODEL, D_MODEL))
    return _fused_call(
        body, comm, (proj, proj, proj, gy, gy, y, y, mix, w_out, post, sinks, pool_w, pool_scale),
        name="layer0_bwd", grid=(nt,),
        out_shape=[jax.ShapeDtypeStruct((t, EVEN_IN), BF16), jax.ShapeDtypeStruct((8, 128), F32),
                   jax.ShapeDtypeStruct((POOL_GROUPS * POOL_GC, POOL_GC), F32), jax.ShapeDtypeStruct((1, POOL_WIDTH), F32),
                   jax.ShapeDtypeStruct((D_MODEL, D_MODEL), BF16), jax.ShapeDtypeStruct((1, D_MODEL), F32)],
        in_specs=[pl.BlockSpec((tq, EVEN_IN), lambda i: (rev(i), 0)),
                  pl.BlockSpec((BLOCK, EVEN_IN), lambda i: (jnp.maximum(rev(i) * nblk - 1, 0), 0)),
                  pl.BlockSpec((POOL_HALO, EVEN_IN), nxt),
                  row, nxt_row, row, nxt_row, row, square, _const_spec((1, D_MODEL)),
                  pl.BlockSpec(memory_space=pltpu.SMEM),
                  _const_spec((POOL_GROUPS, POOL_GC, POOL_GC)), _const_spec((1, POOL_WIDTH))],
        out_specs=[pl.BlockSpec((tq, EVEN_IN), lambda i: (rev(i), 0)),
                   pl.BlockSpec((8, 128), lambda i: (0, 0)),
                   pl.BlockSpec((POOL_GROUPS * POOL_GC, POOL_GC), lambda i: (0, 0)),
                   pl.BlockSpec((1, POOL_WIDTH), lambda i: (0, 0)), square, pl.BlockSpec((1, D_MODEL), lambda i: (0, 0))],
        scratch_shapes=[pltpu.VMEM((tq + BLOCK, 256), F32), pltpu.VMEM((tq + BLOCK, 256), F32),
                        pltpu.VMEM((BLOCK, 256), F32)] + ATTN_CONSTS
        + [pltpu.VMEM((D_MODEL, D_MODEL), F32), pltpu.VMEM((tq + POOL_HALO, D_MODEL), F32)],
        params=_params("arbitrary"))


CONV_RC = 32
CONV_CC = 128
CONV_CHAINS_FWD = 4
CONV_CHAINS_BWD = 2
CONV_UNROLL = 2


def _fill_shifted(s_ref, rows):
    for b in range(1, 8):
        s_ref[b, 0:rows - 8, :] = s_ref[0, b:b + rows - 8, :]


def _tap_blocks(s_ref, r, cols, lead):
    for b in range(8):
        taps = [(a, 8 * a + b - lead) for a in range(5) if 0 <= 8 * a + b - lead < CONV_K]
        span = 8 * max(a for a, _ in taps) + CONV_RC
        blk = s_ref[b, pl.ds(r, span), cols]
        for a, k in taps:
            yield k, blk[8 * a:8 * a + CONV_RC]


def _conv_taps(s_ref, w_ref, r, cols, lead, reverse, chains):
    accs = [None] * chains
    for n, (k, blk) in enumerate(_tap_blocks(s_ref, r, cols, lead)):
        kw = CONV_K - 1 - k if reverse else k
        term = blk * w_ref[kw:kw + 1, cols]
        accs[n % chains] = term if accs[n % chains] is None else accs[n % chains] + term
    while len(accs) > 1:
        accs = [a + b for a, b in zip(accs[0::2], accs[1::2])]
    return accs[0]


def _layer_norm_fwd(cf, lng, lnb):
    mu = jnp.mean(cf, axis=-1, keepdims=True)
    xc = cf - mu
    rstd = lax.rsqrt(jnp.mean(xc * xc, axis=-1, keepdims=True) + EPS)
    chat = xc * rstd
    return chat, rstd, chat * lng + lnb


def _layer1_fwd(proj, dw, dwb, lng, lnb, w_out, x_in, post, target, tt=256):
    t = proj.shape[0]
    lead = CONV_HALO - (CONV_K - 1)

    def body(main_ref, halo_ref, w_ref, b_ref, g_ref, lb_ref, wo_ref, x_ref, p_ref, t_ref,
             o_ref, c_ref, y_ref, dl_ref, l_ref, gs_ref):
        i = pl.program_id(0)
        hv = halo_ref[...]
        gs_ref[0, 0:CONV_HALO, :] = jnp.where(i == 0, 0.0, hv[:, GLU_A] * _sigmoid(hv[:, GLU_B]))
        gs_ref[0, CONV_HALO:CONV_HALO + tt, :] = main_ref[:, GLU_A] * _sigmoid(main_ref[:, GLU_B])
        _fill_shifted(gs_ref, tt + CONV_HALO)

        for c in range(D_MODEL // CONV_CC):
            cols = slice(c * CONV_CC, (c + 1) * CONV_CC)

            def chunk(j, carry):
                r = pl.multiple_of(j * CONV_RC, CONV_RC)
                c_ref[pl.ds(r, CONV_RC), cols] = _conv_taps(gs_ref, w_ref, r, cols, lead, False, CONV_CHAINS_FWD) + b_ref[:, cols]
                return carry
            lax.fori_loop(0, tt // CONV_RC, chunk, 0, unroll=CONV_UNROLL)

        _, _, cn = _layer_norm_fwd(c_ref[...], g_ref[...], lb_ref[...])
        o_ref[...] = (_silu(cn) * _silu(main_ref[:, GATE])).astype(BF16)

        d = _project_out(o_ref[...], wo_ref, x_ref, p_ref, y_ref) - t_ref[...]
        dl_ref[...] = d * (1.0 / D_MODEL)

        @pl.when(i == 0)
        def _():
            l_ref[...] = jnp.zeros_like(l_ref)

        l_ref[...] += 0.5 * jnp.sum(jnp.mean(d * d, axis=-1, keepdims=True))

    vec = _const_spec((1, D_MODEL))
    row = pl.BlockSpec((tt, D_MODEL), lambda i: (i, 0))
    f32_rows = jax.ShapeDtypeStruct((t, D_MODEL), F32)
    return pl.pallas_call(
        body, name="layer1_fwd", grid=(t // tt,),
        out_shape=[jax.ShapeDtypeStruct((t, D_MODEL), BF16), f32_rows, f32_rows, f32_rows,
                   jax.ShapeDtypeStruct((8, 128), F32)],
        in_specs=[pl.BlockSpec((tt, 3 * D_MODEL), lambda i: (i, 0)),
                  pl.BlockSpec((CONV_HALO, 3 * D_MODEL), lambda i: (jnp.maximum(i * (tt // CONV_HALO) - 1, 0), 0)),
                  _const_spec((CONV_K, D_MODEL)), vec, vec, vec,
                  _const_spec((D_MODEL, D_MODEL)), row, vec, row],
        out_specs=[row, row, row, row, pl.BlockSpec((8, 128), lambda i: (0, 0))],
        scratch_shapes=[pltpu.VMEM((8, tt + CONV_HALO, D_MODEL), F32)],
        compiler_params=_params("arbitrary"),
    )(proj, proj, dw, dwb, lng, lnb, w_out, x_in, post, target)


def _layer1_bwd(proj, cf, gy, y, z, w_out, post, dw, lng, lnb, comm=None, tt=256):
    t = proj.shape[0]
    nt = t // tt
    te = tt + CONV_HALO

    def body(main_ref, next_ref, cf_ref, cfn_ref, gy_ref, gyn_ref, y_ref, yn_ref, z_ref, wo_ref, po_ref,
             w_ref, g_ref, lb_ref,
             o_ref, ddw_ref, ddb_ref, dg_ref, dlb_ref, dwo16_ref, dpo_ref, ds_ref, glu_ref, sb_ref, dwo_ref):
        i = pl.program_id(0)

        @pl.when(i == 0)
        def _():
            ddw_ref[...] = jnp.zeros_like(ddw_ref)
            ddb_ref[...] = jnp.zeros_like(ddb_ref)
            dg_ref[...] = jnp.zeros_like(dg_ref)
            dlb_ref[...] = jnp.zeros_like(dlb_ref)

        dzv = _post_bwd_rows(jnp.concatenate([gy_ref[...], gyn_ref[...]], axis=0),
                             jnp.concatenate([y_ref[...], yn_ref[...]], axis=0), z_ref[...], tt, i == 0, i == nt - 1,
                             po_ref, wo_ref, dwo_ref, dwo16_ref, dpo_ref)
        dzv = jnp.concatenate([dzv[0:tt], jnp.where(i < nt - 1, dzv[tt:], 0.0)], axis=0)
        lng = g_ref[...]
        chat, rstd, cn = _layer_norm_fwd(jnp.concatenate([cf_ref[...], cfn_ref[...]], axis=0), lng, lb_ref[...])
        gate = jnp.concatenate([main_ref[:, GATE], next_ref[:, GATE]], axis=0)
        silu_cn, dsilu_cn = _silu_and_grad(cn)
        silu_gate, dsilu_gate = _silu_and_grad(gate)
        o_ref[:, GATE] = (dzv * silu_cn * dsilu_gate)[0:tt].astype(BF16)
        dcn = dzv * silu_gate * dsilu_cn
        dg_ref[...] += jnp.sum((dcn * chat)[0:tt], axis=0, keepdims=True)
        dlb_ref[...] += jnp.sum(dcn[0:tt], axis=0, keepdims=True)
        dchat = dcn * lng
        dcf = rstd * (dchat - jnp.mean(dchat, axis=-1, keepdims=True) - chat * jnp.mean(dchat * chat, axis=-1, keepdims=True))
        ddb_ref[...] += jnp.sum(dcf[0:tt], axis=0, keepdims=True)
        ds_ref[0, 0:te, :] = dcf
        ds_ref[0, te:, :] = jnp.zeros((8, D_MODEL), F32)
        _fill_shifted(ds_ref, te + 8)
        sb_ref[...] = _sigmoid(main_ref[:, GLU_B])
        glu_ref[...] = main_ref[:, GLU_A] * sb_ref[...]

        for c in range(D_MODEL // CONV_CC):
            cols = slice(c * CONV_CC, (c + 1) * CONV_CC)
            gcols = slice(D_MODEL + c * CONV_CC, D_MODEL + (c + 1) * CONV_CC)

            def chunk(j, carry):
                r = pl.multiple_of(j * CONV_RC, CONV_RC)
                dglu = _conv_taps(ds_ref, w_ref, r, cols, 0, True, CONV_CHAINS_BWD)
                sb = sb_ref[pl.ds(r, CONV_RC), cols]
                o_ref[pl.ds(r, CONV_RC), cols] = (dglu * sb).astype(BF16)
                o_ref[pl.ds(r, CONV_RC), gcols] = (dglu * glu_ref[pl.ds(r, CONV_RC), cols] * (1.0 - sb)).astype(BF16)
                return carry
            lax.fori_loop(0, tt // CONV_RC, chunk, 0, unroll=CONV_UNROLL)

            def taps(j, accs):
                r = pl.multiple_of(j * CONV_RC, CONV_RC)
                gl = glu_ref[pl.ds(r, CONV_RC), cols]
                new = list(accs)
                for m, blk in _tap_blocks(ds_ref, r, cols, 0):
                    prod = blk * gl
                    part = prod[0:8]
                    for q in range(1, CONV_RC // 8):
                        part = part + prod[8 * q:8 * q + 8]
                    new[m] = new[m] + part
                return tuple(new)
            accs = lax.fori_loop(0, tt // CONV_RC, taps, tuple(jnp.zeros((8, CONV_CC), F32) for _ in range(CONV_K)))
            for m in range(CONV_K):
                k = CONV_K - 1 - m
                ddw_ref[k:k + 1, cols] += jnp.sum(accs[m], axis=0, keepdims=True)

    vec = _const_spec((1, D_MODEL))
    vec_out = pl.BlockSpec((1, D_MODEL), lambda i: (0, 0))
    row = pl.BlockSpec((tt, D_MODEL), lambda i: (i, 0))
    nxt = lambda i: (jnp.minimum((i + 1) * (tt // CONV_HALO), t // CONV_HALO - 1), 0)
    nxt_row = pl.BlockSpec((CONV_HALO, D_MODEL), nxt)
    vec_f32 = jax.ShapeDtypeStruct((1, D_MODEL), F32)
    square = _const_spec((D_MODEL, D_MODEL))
    return _fused_call(
        body, comm, (proj, proj, cf, cf, gy, gy, y, y, z, w_out, post, dw, lng, lnb), name="layer1_bwd", grid=(nt,),
        out_shape=[jax.ShapeDtypeStruct((t, 3 * D_MODEL), BF16), jax.ShapeDtypeStruct((CONV_K, D_MODEL), F32),
                   vec_f32, vec_f32, vec_f32, jax.ShapeDtypeStruct((D_MODEL, D_MODEL), BF16), vec_f32],
        in_specs=[pl.BlockSpec((tt, 3 * D_MODEL), lambda i: (i, 0)),
                  pl.BlockSpec((CONV_HALO, 3 * D_MODEL), nxt),
                  row, nxt_row, row, nxt_row, row, nxt_row, row, square, vec,
                  _const_spec((CONV_K, D_MODEL)), vec, vec],
        out_specs=[pl.BlockSpec((tt, 3 * D_MODEL), lambda i: (i, 0)),
                   pl.BlockSpec((CONV_K, D_MODEL), lambda i: (0, 0)), vec_out, vec_out, vec_out, square, vec_out],
        scratch_shapes=[pltpu.VMEM((8, te + 8, D_MODEL), F32), pltpu.VMEM((tt, D_MODEL), F32),
                        pltpu.VMEM((tt, D_MODEL), F32), pltpu.VMEM((D_MODEL, D_MODEL), F32)],
        params=_params("arbitrary"))


def _piece_sums(tensors, place, name):
    counts = [len(parts) for parts in tensors]

    def body(p_ref, *refs):
        ins, outs = refs[:sum(counts)], refs[sum(counts):]
        pos = 0
        for count, o_ref in zip(counts, outs):
            acc = ins[pos][0].astype(F32)
            for part in ins[pos + 1:pos + count]:
                acc = acc + part[0].astype(F32)
            o_ref[0] = acc
            pos += count

    spec = lambda a, slot: pl.BlockSpec((1,) + a.shape[1:], lambda j, p_ref: (slot(p_ref), 0, 0))
    return pl.pallas_call(
        body, name=name,
        grid_spec=pltpu.PrefetchScalarGridSpec(
            num_scalar_prefetch=1, grid=(1,),
            in_specs=[spec(a, slot) for parts in tensors for a, slot in parts],
            out_specs=[pl.BlockSpec((1,) + parts[0][0].shape[1:], lambda j, p_ref: (p_ref[1], 0, 0))
                       for parts in tensors]),
        out_shape=[jax.ShapeDtypeStruct((2,) + parts[0][0].shape[1:], F32) for parts in tensors],
        compiler_params=_params("arbitrary"),
    )(place, *[a for parts in tensors for a, _ in parts])


def _direct_parts(own, recv):
    peer = lambda m: (lambda p: p[0] ^ m)
    return [(own, peer(0))] + [(recv, peer(m)) for m in range(1, N_DEV)]


def _share_with_sibling(halves, small, name):
    n = len(halves)

    def body(*refs):
        small_ref, outs, all_ref = refs[n], refs[n + 1:2 * n + 1], refs[2 * n + 1]
        send_sems, recv_sems, local_sem = refs[2 * n + 2:]
        x, y, c, _ = _place()
        me = 4 * x + 2 * y + c
        send = [pltpu.make_async_remote_copy(
            src_ref=outs[t].at[c], dst_ref=outs[t].at[c], send_sem=send_sems.at[t], recv_sem=recv_sems.at[t],
            device_id=(x, y, 1 - c), device_id_type=MESH_ID) for t in range(n)]
        recv = [pltpu.make_async_remote_copy(
            src_ref=outs[t].at[c], dst_ref=outs[t].at[1 - c], send_sem=send_sems.at[t], recv_sem=recv_sems.at[t],
            device_id=(x, y, 1 - c), device_id_type=MESH_ID) for t in range(n)]
        for m in range(1, N_DEV):
            px, py, pc = x ^ (m >> 2), y ^ ((m >> 1) & 1), c ^ (m & 1)
            sems = dict(send_sem=send_sems.at[n + m - 1], recv_sem=recv_sems.at[n + m - 1], device_id=(px, py, pc),
                        device_id_type=MESH_ID)
            send.append(pltpu.make_async_remote_copy(src_ref=small_ref, dst_ref=all_ref.at[me], **sems))
            recv.append(pltpu.make_async_remote_copy(src_ref=small_ref, dst_ref=all_ref.at[4 * px + 2 * py + pc], **sems))
        mine = pltpu.make_async_copy(small_ref, all_ref.at[me], local_sem)
        mine.start()
        for cp in send:
            cp.start()
        for cp in recv:
            cp.wait_recv()
        for cp in send:
            cp.wait_send()
        mine.wait()

    k = n + N_DEV - 1
    return pl.pallas_call(
        body, name=name,
        out_shape=[jax.ShapeDtypeStruct(h.shape, h.dtype) for h in halves]
        + [jax.ShapeDtypeStruct((N_DEV,) + small.shape, small.dtype)],
        in_specs=[ANY] * (n + 1), out_specs=[ANY] * (n + 1),
        input_output_aliases={t: t for t in range(n)},
        scratch_shapes=[pltpu.SemaphoreType.DMA((k,)), pltpu.SemaphoreType.DMA((k,)), pltpu.SemaphoreType.DMA],
    )(*halves, small)


def _sum8(parts, name):
    n = len(parts)

    def body(*refs):
        for p_ref, o_ref in zip(refs[:n], refs[n:]):
            acc = p_ref[0]
            for k in range(1, N_DEV):
                acc = acc + p_ref[k]
            o_ref[...] = acc

    whole = pl.BlockSpec(memory_space=pltpu.VMEM)
    return pl.pallas_call(
        body, name=name, out_shape=[jax.ShapeDtypeStruct(p.shape[1:], F32) for p in parts],
        in_specs=[whole] * n, out_specs=[whole] * n,
    )(*parts)


def _adamw(tensors, steps, name):
    n = len(tensors)
    views, specs = [], []
    for w, _, _, _ in tensors:
        cols = w.shape[-1]
        rows = w.size // cols
        if w.ndim >= 3 and w.shape[-2] == 1:
            assert steps == 1
            views.append((rows, 1, cols))
            specs.append(pl.BlockSpec((rows, 1, cols), lambda i: (0, 0, 0)))
        else:
            views.append((rows, cols))
            specs.append(pl.BlockSpec((rows // steps, cols), lambda i: (i, 0)))

    def body(*refs):
        for t in range(n):
            w_ref, g_ref, m_ref, v_ref = refs[4 * t:4 * t + 4]
            d_ref, nm_ref, nv_ref = refs[4 * n + 3 * t:4 * n + 3 * t + 3]
            gv = g_ref[...]
            mn = ADAM_B1 * m_ref[...] + (1.0 - ADAM_B1) * gv
            vn = ADAM_B2 * v_ref[...] + (1.0 - ADAM_B2) * (gv * gv)
            m_hat = mn / (1.0 - ADAM_B1 ** ADAM_STEP)
            v_hat = vn / (1.0 - ADAM_B2 ** ADAM_STEP)
            d_ref[...] = -ADAM_LR * (m_hat / (jnp.sqrt(v_hat) + ADAM_EPS) + ADAM_WD * w_ref[...])
            nm_ref[...] = mn
            nv_ref[...] = vn

    outs = pl.pallas_call(
        body, name=name, grid=(steps,),
        out_shape=[jax.ShapeDtypeStruct(view, F32) for view in views for _ in range(3)],
        in_specs=[spec for spec in specs for _ in range(4)], out_specs=[spec for spec in specs for _ in range(3)],
        compiler_params=_params("parallel"),
    )(*[a.reshape(view) for group, view in zip(tensors, views) for a in group])
    return [tuple(o.reshape(group[0].shape) for o in outs[3 * t:3 * t + 3]) for t, group in enumerate(tensors)]


SMALL_ROWS = 832


def _pack_small(g):
    parts = [g["loss"], g["pre1"].reshape(8, 128), g["post0"].reshape(8, 128),
             g["post1"].reshape(8, 128), g["sinks"], jnp.pad(g["pool_scale"].reshape(4, 128), ((0, 4), (0, 0))),
             g["pool_w"], g["dw"].reshape(248, 128), g["dwb"].reshape(8, 128), g["lng"].reshape(8, 128),
             g["lnb"].reshape(8, 128)]
    assert sum(p.shape[0] for p in parts) == SMALL_ROWS
    return jnp.concatenate(parts, axis=0)


def _unpack_small(s):
    out, r = {}, 0
    for key, rows, shape in (("loss", 8, (8, 128)), ("pre1", 8, (1, D_MODEL)), ("post", 16, (2, D_MODEL)),
                             ("sinks", 8, (8, 128)),
                             ("pool_scale", 4, (1, POOL_WIDTH)), ("pad", 4, (4, 128)), ("pool_w", 512, (1, 4, 128, 128)),
                             ("dw", 248, (CONV_K, D_MODEL)), ("dwb", 8, (1, D_MODEL)), ("lng", 8, (1, D_MODEL)),
                             ("lnb", 8, (1, D_MODEL))):
        out[key] = s[r:r + rows].reshape(shape)
        r += rows
    return out


def kernel(x, pre_norm, post_norm, a_w_in, a_sinks, b_pool_w, b_pool_scale, ab_w_out, c_w_in, c_dw_w, c_dw_b, c_ln_g, c_ln_b, c_w_out, loss_target, m_pre_norm, m_post_norm, m_a_w_in, m_a_sinks, m_b_pool_w, m_b_pool_scale, m_ab_w_out, m_c_w_in, m_c_dw_w, m_c_dw_b, m_c_ln_g, m_c_ln_b, m_c_w_out, v_pre_norm, v_post_norm, v_a_w_in, v_a_sinks, v_b_pool_w, v_b_pool_scale, v_ab_w_out, v_c_w_in, v_c_dw_w, v_c_dw_b, v_c_ln_g, v_c_ln_b, v_c_w_out):
    ix, iy = lax.axis_index("x"), lax.axis_index("y")
    chip_cols = (2 * ix + iy) * 256

    pad8 = lambda v: jnp.pad(v, ((0, -v.shape[0] % 8), (0, 0)))
    vec_shard = jnp.concatenate([pad8(c_dw_w.reshape(CONV_K, 256)), pad8(c_dw_b), pad8(c_ln_g), pad8(c_ln_b),
                                 jnp.zeros((8, 256), F32)], axis=0)
    x0, target = x[0], loss_target[0]
    pre0, pre1 = pre_norm[0:1], pre_norm[1:2]
    post0, post1 = post_norm[0:1], post_norm[1:2]
    pool_w = b_pool_w[0]

    (wa_t,) = _run_comm(_Gather([a_w_in[0].T.astype(BF16)], halve=True), "gather_a_w_in")
    wa_t = wa_t.reshape(EVEN_IN, D_MODEL)
    proj0, (w_ab,) = _norm_matmul(x0, pre0, wa_t, "proj0_fwd", comm=_Gather([ab_w_out[0].astype(BF16)], halve=True))
    w_ab = w_ab.reshape(D_MODEL, D_MODEL)
    (mix0, y0, x1), (wc_t, w_c, vecs) = _layer0_fwd(
        proj0, a_sinks, pool_w, b_pool_scale, w_ab, x0, post0,
        comm=_Gather([c_w_in[0].T.astype(BF16), c_w_out[0].astype(BF16), vec_shard], halve=True))
    wc_t = wc_t.reshape(3 * D_MODEL, D_MODEL)
    w_c = w_c.reshape(D_MODEL, D_MODEL)
    vecs = vecs.reshape(4, 64, 256).transpose(1, 0, 2).reshape(64, D_MODEL)
    dw, dwb, lng, lnb = vecs[0:CONV_K], vecs[32:33], vecs[40:41], vecs[48:49]
    proj1, _ = _norm_matmul(x1, pre1, wc_t, "proj1_fwd")
    z1, cf1, y1, g2, loss = _layer1_fwd(proj1, dw, dwb, lng, lnb, w_c, x1, post1, target)

    pieces = lambda m: m.reshape(N_DEV, m.shape[0] // N_DEV, D_MODEL)
    (dproj1, d_dw, d_dwb, d_lng, d_lnb, d_wc, d_post1), _ = _layer1_bwd(proj1, cf1, g2, y1, z1, w_c, post1, dw, lng, lnb)
    (g1, d_wct, d_pre1), _ = _pre_bwd(dproj1, wc_t, x1, pre1, g2, "proj1_bwd")
    (dproj0, d_sinks, d_pw, d_ps, d_wab, d_post0), (r_wc, r_wct) = _layer0_bwd(
        proj0, g1, y0, mix0, w_ab, post0, a_sinks, pool_w, b_pool_scale,
        comm=_Scatter([pieces(d_wc), pieces(d_wct)]))
    g = dict(loss=loss, pre1=d_pre1, post0=d_post0, post1=d_post1, sinks=d_sinks, pool_w=d_pw, pool_scale=d_ps,
             dw=d_dw, dwb=d_dwb, lng=d_lng, lnb=d_lnb)
    d_wat, (small8, r_wab) = _proj_dw(dproj0, x0, pre0, "proj0_dw",
                                      comm=_Comms(_Gather([_pack_small(g)], halve=False), _Scatter([pieces(d_wab)])))
    sent = _scatter_start(pieces(d_wat), "scatter_a_start")
    (gx, d_pre0), _ = _proj_dx(dproj0, wa_t, x0, pre0 + sent[4][0:1, 0:1], g1, "proj0_dx")
    own_wat, r_wat = _scatter_wait(*sent[:4], d_pre0, "scatter_a_wait")

    ic = lax.axis_index("c")
    me = 4 * ix + 2 * iy + ic
    place = jnp.stack([me, ic]).astype(jnp.int32)
    parts = [_direct_parts(own_wat, r_wat), _direct_parts(pieces(d_wab), r_wab), _direct_parts(pieces(d_wct), r_wct),
             _direct_parts(pieces(d_wc), r_wc)]
    halves = _piece_sums(parts, place, "grad_sums")
    *shared, pre0_8 = _share_with_sibling(halves, d_pre0.reshape(8, 128), "grad_share")
    g_wa_t, g_wab, g_wc_t, g_wc = [h.reshape(2 * h.shape[1], D_MODEL) for h in shared]
    g_c_w_in = g_wc_t.T[None]
    g_ab_w_out = g_wab[None]
    g_c_w_out = g_wc[None]

    small, pre0 = _sum8([small8, pre0_8], "small_sums")
    s = _unpack_small(small)
    layer = lax.broadcasted_iota(jnp.int32, (2, D_MODEL), 0)
    g_pre = jnp.where(layer == 0, pre0.reshape(1, D_MODEL), s["pre1"])
    g_post = s["post"]
    g_sinks = s["sinks"][:, 0].reshape(1, 8)
    g_pool_w, g_pool_scale = s["pool_w"], s["pool_scale"]
    g_dw = lax.dynamic_slice_in_dim(s["dw"], chip_cols, 256, axis=1).reshape(1, CONV_K, 1, 256)
    g_dwb = lax.dynamic_slice_in_dim(s["dwb"], chip_cols, 256, axis=1)
    g_lng = lax.dynamic_slice_in_dim(s["lng"], chip_cols, 256, axis=1)
    g_lnb = lax.dynamic_slice_in_dim(s["lnb"], chip_cols, 256, axis=1)

    turn = lambda a: jnp.swapaxes(a, 1, 2)
    a_w_in, m_a_w_in, v_a_w_in = turn(a_w_in), turn(m_a_w_in), turn(v_a_w_in)
    grads = [g_pre, g_post, g_wa_t[None], g_sinks, g_pool_w, g_pool_scale, g_ab_w_out, g_c_w_in, g_dw, g_dwb, g_lng,
             g_lnb, g_c_w_out]
    weights = [pre_norm, post_norm, a_w_in, a_sinks, b_pool_w, b_pool_scale, ab_w_out, c_w_in, c_dw_w, c_dw_b, c_ln_g,
               c_ln_b, c_w_out]
    moms = [m_pre_norm, m_post_norm, m_a_w_in, m_a_sinks, m_b_pool_w, m_b_pool_scale, m_ab_w_out, m_c_w_in, m_c_dw_w,
            m_c_dw_b, m_c_ln_g, m_c_ln_b, m_c_w_out]
    vars_ = [v_pre_norm, v_post_norm, v_a_w_in, v_a_sinks, v_b_pool_w, v_b_pool_scale, v_ab_w_out, v_c_w_in, v_c_dw_w,
             v_c_dw_b, v_c_ln_g, v_c_ln_b, v_c_w_out]
    tensors = list(zip(weights, grads, moms, vars_))
    matrices = (2, 6, 7, 12)
    others = [k for k in range(len(tensors)) if k not in matrices]
    updates = dict(zip(matrices, _adamw([tensors[k] for k in matrices], 4, "adamw_matrices")))
    updates.update(zip(others, _adamw([tensors[k] for k in others], 1, "adamw_small")))
    deltas, new_m, new_v = ([updates[k][j] for k in range(len(tensors))] for j in range(3))
    for outs in (grads, deltas, new_m, new_v):
        outs[2] = turn(outs[2])
    return (s["loss"][0, 0], gx[None], *grads, *deltas, *new_m, *new_v)
```

```python
import jax
import jax.numpy as jnp
from jax import lax
from jax.experimental import pallas as pl
from jax.experimental.pallas import tpu as pltpu

F32 = jnp.float32
BF16 = jnp.bfloat16

D_MODEL = 1024
EPS = 1e-6
NEG = -1e30
HEAD_DIM = 64
GROUP = 4
KV_HEADS = 2
BLOCK = 128
EVEN_IN = 2304
ATTN_WIDTH = 512
POOL_WIDTH = 512
COL_Q, COL_K, COL_GA, COL_U, COL_GB = 0, 512, 768, 1280, 1792
POOL_GROUPS = 4
POOL_GC = 128
POOL_HALO = 16
CONV_K = 31
CONV_HALO = 32
GLU_A = slice(0, D_MODEL)
GLU_B = slice(D_MODEL, 2 * D_MODEL)
GATE = slice(2 * D_MODEL, 3 * D_MODEL)
N_DEV = 8

ADAM_LR = 0.001
ADAM_B1 = 0.9
ADAM_B2 = 0.999
ADAM_EPS = 1e-08
ADAM_WD = 0.01
ADAM_STEP = 10

VMEM_LIMIT_BYTES = 56 * 1024 * 1024

NT = (((1,), (1,)), ((), ()))
TN = (((0,), (0,)), ((), ()))
MESH_ID = pl.DeviceIdType.MESH


def _params(*sem):
    return pltpu.CompilerParams(dimension_semantics=sem, vmem_limit_bytes=VMEM_LIMIT_BYTES)


def _const_spec(shape):
    nd = len(shape)
    return pl.BlockSpec(shape, lambda *_: (0,) * nd, pipeline_mode=pl.Buffered(1))


def _sigmoid(v):
    return 0.5 * jnp.tanh(0.5 * v) + 0.5


def _silu(v):
    h = 0.5 * v
    return h * jnp.tanh(h) + h


def _silu_and_grad(v):
    s = _sigmoid(v)
    silu = v * s
    return silu, s + silu * (1.0 - s)


ANY = pl.BlockSpec(memory_space=pl.ANY)


def _place():
    x, y, c = lax.axis_index("x"), lax.axis_index("y"), lax.axis_index("c")
    chips = [(1 - x, y), (x, 1 - y), (1 - x, 1 - y)]
    return x, y, c, chips


class _Gather:
    def __init__(self, blocks, halve):
        self.ins = list(blocks)
        self.halve = halve
        self.n = n = len(blocks)
        self.shapes = [((b.shape[0] // 2) if halve else b.shape[0], b.shape[1]) for b in blocks]
        self.out_shape = [jax.ShapeDtypeStruct((N_DEV, r, cols), b.dtype) for (r, cols), b in zip(self.shapes, blocks)]
        self.scratch = [pltpu.SemaphoreType.DMA((7 * n,)), pltpu.SemaphoreType.DMA((7 * n,)),
                        pltpu.SemaphoreType.DMA((n,))]

    def _copies(self, ins, outs, sems):
        send_sems, recv_sems, local_sems = sems
        x, y, c, chips = _place()
        me, sibling = (x, y, c), (x, y, 1 - c)

        def piece(t, px, py, pc):
            return outs[t].at[4 * px + 2 * py + pc]

        def own(t):
            return ins[t].at[pl.ds(c * self.shapes[t][0], self.shapes[t][0])] if self.halve else ins[t]

        def copy(t, k, block, to, src=None):
            return pltpu.make_async_remote_copy(
                src_ref=piece(t, *block) if src is None else src, dst_ref=piece(t, *block),
                send_sem=send_sems.at[7 * t + k], recv_sem=recv_sems.at[7 * t + k],
                device_id=to, device_id_type=MESH_ID)

        rng = range(self.n)
        return dict(
            mine=[pltpu.make_async_copy(own(t), piece(t, *me), local_sems.at[t]) for t in rng],
            first=[copy(t, 0, me, sibling, src=own(t)) for t in rng]
            + [copy(t, 1 + j, me, (*chip, c), src=own(t)) for t in rng for j, chip in enumerate(chips)],
            landed=[copy(t, 1 + j, (*chip, c), me) for j, chip in enumerate(chips) for t in rng],
            passed=[copy(t, 4 + j, (*chip, c), sibling) for j, chip in enumerate(chips) for t in rng],
            from_sibling=[copy(t, 0, sibling, me) for t in rng]
            + [copy(t, 4 + j, (*chip, 1 - c), me) for t in rng for j, chip in enumerate(chips)])

    def start(self, ins, outs, sems):
        d = self._copies(ins, outs, sems)
        for cp in d["mine"] + d["first"]:
            cp.start()

    def middle(self, ins, outs, sems):
        d = self._copies(ins, outs, sems)
        for got, fwd in zip(d["landed"], d["passed"]):
            got.wait_recv()
            fwd.start()

    def finish(self, ins, outs, sems):
        d = self._copies(ins, outs, sems)
        for cp in d["from_sibling"]:
            cp.wait_recv()
        for cp in d["first"] + d["passed"]:
            cp.wait_send()
        for cp in d["mine"]:
            cp.wait()


class _Scatter:
    def __init__(self, tensors):
        self.ins = list(tensors)
        self.n = n = len(tensors)
        self.out_shape = [jax.ShapeDtypeStruct(t.shape, t.dtype) for t in tensors]
        self.scratch = [pltpu.SemaphoreType.DMA((7 * n,)), pltpu.SemaphoreType.DMA((7 * n,))]

    def _copies(self, ins, outs, sems):
        send_sems, recv_sems = sems
        x, y, c, _ = _place()
        me = 4 * x + 2 * y + c
        sends, recvs = [], []
        for t in range(self.n):
            for m in range(1, N_DEV):
                px, py, pc = x ^ (m >> 2), y ^ ((m >> 1) & 1), c ^ (m & 1)
                q = 4 * px + 2 * py + pc
                sems_k = dict(send_sem=send_sems.at[7 * t + m - 1], recv_sem=recv_sems.at[7 * t + m - 1],
                              device_id=(px, py, pc), device_id_type=MESH_ID)
                sends.append(pltpu.make_async_remote_copy(src_ref=ins[t].at[q], dst_ref=outs[t].at[me], **sems_k))
                recvs.append(pltpu.make_async_remote_copy(src_ref=ins[t].at[me], dst_ref=outs[t].at[q], **sems_k))
        return sends, recvs

    def start(self, ins, outs, sems):
        for cp in self._copies(ins, outs, sems)[0]:
            cp.start()

    def middle(self, ins, outs, sems):
        pass

    def finish(self, ins, outs, sems):
        sends, recvs = self._copies(ins, outs, sems)
        for cp in recvs:
            cp.wait_recv()
        for cp in sends:
            cp.wait_send()


class _Comms:
    def __init__(self, *comms):
        self.comms = comms
        self.ins = [a for c in comms for a in c.ins]
        self.out_shape = [s for c in comms for s in c.out_shape]
        self.scratch = [s for c in comms for s in c.scratch]

    def _each(self, phase, ins, outs, sems):
        i = o = s = 0
        for c in self.comms:
            ni, no, ns = len(c.ins), len(c.out_shape), len(c.scratch)
            getattr(c, phase)(ins[i:i + ni], outs[o:o + no], sems[s:s + ns])
            i, o, s = i + ni, o + no, s + ns

    def start(self, ins, outs, sems):
        self._each("start", ins, outs, sems)

    def middle(self, ins, outs, sems):
        self._each("middle", ins, outs, sems)

    def finish(self, ins, outs, sems):
        self._each("finish", ins, outs, sems)


def _run_comm(comm, name):
    n = len(comm.ins)

    def body(*refs):
        parts = refs[:n], refs[n:2 * n], refs[2 * n:]
        comm.start(*parts)
        comm.middle(*parts)
        comm.finish(*parts)

    return pl.pallas_call(body, name=name, out_shape=comm.out_shape, in_specs=[ANY] * n, out_specs=[ANY] * n,
                          scratch_shapes=comm.scratch)(*comm.ins)


HBM_SPEC = pl.BlockSpec(memory_space=pltpu.HBM)
SEM_SPEC = pl.BlockSpec(memory_space=pltpu.SEMAPHORE)
DATAFLOW = pltpu.SideEffectType.DATAFLOW_SIDE_EFFECTING


def _scatter_copies(own_ref, land_ref, send_sems, recv_sems):
    x, y, c, _ = _place()
    me = 4 * x + 2 * y + c
    pairs = []
    for m in range(1, N_DEV):
        px, py, pc = x ^ (m >> 2), y ^ ((m >> 1) & 1), c ^ (m & 1)
        q = 4 * px + 2 * py + pc
        sems = dict(send_sem=send_sems.at[m - 1], recv_sem=recv_sems.at[m - 1], device_id=(px, py, pc),
                    device_id_type=MESH_ID)
        pairs.append((pltpu.make_async_remote_copy(src_ref=own_ref.at[q], dst_ref=land_ref.at[me], **sems),
                      pltpu.make_async_remote_copy(src_ref=own_ref.at[me], dst_ref=land_ref.at[q], **sems)))
    return pairs


def _scatter_start(own, name):
    def body(own_ref, land_ref, send_sems, recv_sems, own_thru, land_thru, token):
        for send, _ in _scatter_copies(own_ref, land_ref, send_sems, recv_sems):
            send.start()
        token[...] = jnp.zeros_like(token)

    buf = pltpu.HBM(own.shape, own.dtype)
    return pl.pallas_call(
        body, name=name,
        out_shape=(pltpu.SemaphoreType.DMA((N_DEV - 1,)), pltpu.SemaphoreType.DMA((N_DEV - 1,)), buf, buf,
                   jax.ShapeDtypeStruct((8, 128), F32)),
        in_specs=(HBM_SPEC, HBM_SPEC),
        out_specs=(SEM_SPEC, SEM_SPEC, HBM_SPEC, HBM_SPEC, pl.BlockSpec(memory_space=pltpu.VMEM)),
        input_output_aliases={0: 2, 1: 3},
        compiler_params=pltpu.CompilerParams(has_side_effects=DATAFLOW),
    )(pltpu.with_memory_space_constraint(own, pltpu.HBM),
      pltpu.with_memory_space_constraint(lax.empty(own.shape, own.dtype), pltpu.HBM))


def _scatter_wait(send_sems, recv_sems, own_thru, land_thru, after, name):
    def body(own_ref, land_ref, send_sems, recv_sems, after_ref, own_out, land_out):
        for send, recv in _scatter_copies(own_ref, land_ref, send_sems, recv_sems):
            send.wait_send()
            recv.wait_recv()

    buf = pltpu.HBM(own_thru.shape, own_thru.dtype)
    return pl.pallas_call(
        body, name=name, out_shape=(buf, buf),
        in_specs=(HBM_SPEC, HBM_SPEC, SEM_SPEC, SEM_SPEC, ANY), out_specs=(HBM_SPEC, HBM_SPEC),
        input_output_aliases={0: 0, 1: 1},
        compiler_params=pltpu.CompilerParams(has_side_effects=DATAFLOW),
    )(own_thru, land_thru, send_sems, recv_sems, after)


def _fused_call(body, comm, args, *, name, grid, out_shape, in_specs, out_specs, scratch_shapes=(), params):
    single = not isinstance(out_shape, (list, tuple))
    out_shape = [out_shape] if single else list(out_shape)
    out_specs = [out_specs] if single else list(out_specs)
    if comm is None:
        res = pl.pallas_call(body, name=name, grid=grid, out_shape=out_shape, in_specs=in_specs, out_specs=out_specs,
                             scratch_shapes=list(scratch_shapes), compiler_params=params)(*args)
        return (res[0] if single else res), []
    n_in, n_out, n_scr = len(in_specs), len(out_shape), len(scratch_shapes)
    c_in, c_out = len(comm.ins), len(comm.out_shape)
    steps = grid[0]

    def fused(*refs):
        pos = 0
        groups = []
        for size in (n_in, c_in, n_out, c_out, n_scr, len(comm.scratch)):
            groups.append(refs[pos:pos + size])
            pos += size
        ins, c_ins, outs, c_outs, scr, c_sems = groups
        i = pl.program_id(0)

        @pl.when(i == 0)
        def _():
            comm.start(c_ins, c_outs, c_sems)

        @pl.when(i == steps // 2)
        def _():
            comm.middle(c_ins, c_outs, c_sems)

        body(*ins, *outs, *scr)

        @pl.when(i == steps - 1)
        def _():
            comm.finish(c_ins, c_outs, c_sems)

    res = pl.pallas_call(
        fused, name=name, grid=grid, out_shape=out_shape + list(comm.out_shape),
        in_specs=list(in_specs) + [ANY] * c_in, out_specs=out_specs + [ANY] * c_out,
        scratch_shapes=list(scratch_shapes) + list(comm.scratch), compiler_params=params)(*args, *comm.ins)
    main = res[:n_out]
    return (main[0] if single else main), list(res[n_out:])


def _norm_matmul(x, gain, wt, name, comm=None, tm=1024):
    t, n = x.shape[0], wt.shape[0]

    def body(x_ref, g_ref, wt_ref, o_ref):
        xv = x_ref[...]
        r = lax.rsqrt(jnp.mean(xv * xv, axis=-1, keepdims=True) + EPS)
        h = (xv * r * g_ref[...]).astype(BF16)
        o_ref[...] = lax.dot_general(h, wt_ref[...], NT, preferred_element_type=F32)

    return _fused_call(
        body, comm, (x, gain, wt), name=name, grid=(t // tm,),
        out_shape=jax.ShapeDtypeStruct((t, n), F32),
        in_specs=[pl.BlockSpec((tm, D_MODEL), lambda i: (i, 0)), _const_spec((1, D_MODEL)), _const_spec((n, D_MODEL))],
        out_specs=pl.BlockSpec((tm, n), lambda i: (i, 0)),
        params=_params("arbitrary"))


def _project_out(a, w_ref, x_ref, p_ref, y_ref):
    y = jnp.dot(a, w_ref[...], preferred_element_type=F32)
    y_ref[...] = y
    ry = lax.rsqrt(jnp.mean(y * y, axis=-1, keepdims=True) + EPS)
    return x_ref[...] + (y * ry) * p_ref[...]


def _post_bwd_rows(g, y, a, n_own, first, last, p_ref, w_ref, dw_ref, dw16_ref, dp_ref):
    @pl.when(first)
    def _():
        dw_ref[...] = jnp.zeros_like(dw_ref)
        dp_ref[...] = jnp.zeros_like(dp_ref)

    ry = lax.rsqrt(jnp.mean(y * y, axis=-1, keepdims=True) + EPS)
    nv = y * ry
    dp_ref[...] += jnp.sum((g * nv)[0:n_own], axis=0, keepdims=True)
    dn = g * p_ref[...]
    dy = (ry * (dn - nv * jnp.mean(dn * nv, axis=-1, keepdims=True))).astype(BF16)
    dw_ref[...] += lax.dot_general(a, dy[0:n_own], TN, preferred_element_type=F32)

    @pl.when(last)
    def _():
        dw16_ref[...] = dw_ref[...].astype(BF16)

    return lax.dot_general(dy, w_ref[...], NT, preferred_element_type=F32)


def _pre_bwd(dproj, wt, x_in, pre, g, name, comm=None, tm=512):
    t, n = dproj.shape
    steps = t // tm

    def body(dp_ref, wt_ref, x_ref, pre_ref, g_ref, dx_ref, dwt16_ref, dpre_ref, dwt_ref):
        @pl.when(pl.program_id(0) == 0)
        def _():
            dwt_ref[...] = jnp.zeros_like(dwt_ref)
            dpre_ref[...] = jnp.zeros_like(dpre_ref)

        dpv = dp_ref[...]
        dh = jnp.dot(dpv, wt_ref[...], preferred_element_type=F32)
        xv = x_ref[...]
        r = lax.rsqrt(jnp.mean(xv * xv, axis=-1, keepdims=True) + EPS)
        xn = xv * r
        pv = pre_ref[...]
        dpre_ref[...] += jnp.sum(dh * xn, axis=0, keepdims=True)
        dxn = dh * pv
        dx_ref[...] = g_ref[...] + r * (dxn - xn * jnp.mean(dxn * xn, axis=-1, keepdims=True))
        h = (xn * pv).astype(BF16)
        dwt_ref[...] += lax.dot_general(dpv, h, TN, preferred_element_type=F32)

        @pl.when(pl.program_id(0) == steps - 1)
        def _():
            dwt16_ref[...] = dwt_ref[...].astype(BF16)

    row = pl.BlockSpec((tm, D_MODEL), lambda i: (i, 0))
    return _fused_call(
        body, comm, (dproj, wt, x_in, pre, g), name=name, grid=(steps,),
        out_shape=[jax.ShapeDtypeStruct((t, D_MODEL), F32), jax.ShapeDtypeStruct((n, D_MODEL), BF16),
                   jax.ShapeDtypeStruct((1, D_MODEL), F32)],
        in_specs=[pl.BlockSpec((tm, n), lambda i: (i, 0)), _const_spec((n, D_MODEL)), row, _const_spec((1, D_MODEL)), row],
        out_specs=[row, _const_spec((n, D_MODEL)), pl.BlockSpec((1, D_MODEL), lambda i: (0, 0))],
        scratch_shapes=[pltpu.VMEM((n, D_MODEL), F32)],
        params=_params("arbitrary"))


def _proj_dw(dproj, x_in, pre, name, comm=None, tm=1024):
    t, n = dproj.shape
    steps = t // tm

    def body(dp_ref, x_ref, pre_ref, dwt16_ref, dwt_ref):
        @pl.when(pl.program_id(0) == 0)
        def _():
            dwt_ref[...] = jnp.zeros_like(dwt_ref)

        xv = x_ref[...]
        r = lax.rsqrt(jnp.mean(xv * xv, axis=-1, keepdims=True) + EPS)
        h = (xv * r * pre_ref[...]).astype(BF16)
        dwt_ref[...] += lax.dot_general(dp_ref[...], h, TN, preferred_element_type=F32)

        @pl.when(pl.program_id(0) == steps - 1)
        def _():
            dwt16_ref[...] = dwt_ref[...].astype(BF16)

    return _fused_call(
        body, comm, (dproj, x_in, pre), name=name, grid=(steps,),
        out_shape=jax.ShapeDtypeStruct((n, D_MODEL), BF16),
        in_specs=[pl.BlockSpec((tm, n), lambda i: (i, 0)), pl.BlockSpec((tm, D_MODEL), lambda i: (i, 0)),
                  _const_spec((1, D_MODEL))],
        out_specs=pl.BlockSpec((n, D_MODEL), lambda i: (0, 0)),
        scratch_shapes=[pltpu.VMEM((n, D_MODEL), F32)],
        params=_params("arbitrary"))


def _proj_dx(dproj, wt, x_in, pre, g, name, comm=None, tm=1024):
    t, n = dproj.shape

    def body(dp_ref, wt_ref, x_ref, pre_ref, g_ref, dx_ref, dpre_ref):
        @pl.when(pl.program_id(0) == 0)
        def _():
            dpre_ref[...] = jnp.zeros_like(dpre_ref)

        dh = jnp.dot(dp_ref[...], wt_ref[...], preferred_element_type=F32)
        xv = x_ref[...]
        r = lax.rsqrt(jnp.mean(xv * xv, axis=-1, keepdims=True) + EPS)
        xn = xv * r
        dpre_ref[...] += jnp.sum(dh * xn, axis=0, keepdims=True)
        dxn = dh * pre_ref[...]
        dx_ref[...] = g_ref[...] + r * (dxn - xn * jnp.mean(dxn * xn, axis=-1, keepdims=True))

    row = pl.BlockSpec((tm, D_MODEL), lambda i: (i, 0))
    return _fused_call(
        body, comm, (dproj, wt, x_in, pre, g), name=name, grid=(t // tm,),
        out_shape=[jax.ShapeDtypeStruct((t, D_MODEL), F32), jax.ShapeDtypeStruct((1, D_MODEL), F32)],
        in_specs=[pl.BlockSpec((tm, n), lambda i: (i, 0)), _const_spec((n, D_MODEL)), row, _const_spec((1, D_MODEL)), row],
        out_specs=[row, pl.BlockSpec((1, D_MODEL), lambda i: (0, 0))],
        params=_params("arbitrary"))


def _group_masks():
    lane = lax.broadcasted_iota(jnp.int32, (1, GROUP * HEAD_DIM), 1)
    return [lane // HEAD_DIM == g for g in range(GROUP)]


def _stack_groups(v, masks, scale=1.0):
    return jnp.concatenate([v * jnp.where(m, scale, 0.0) for m in masks], axis=0)


def _unstack_groups(v, masks):
    out = v[(GROUP - 1) * BLOCK:GROUP * BLOCK]
    for g in range(GROUP - 2, -1, -1):
        out = jnp.where(masks[g], v[g * BLOCK:(g + 1) * BLOCK], out)
    return out


def _repeat_head(kv2, kvh):
    first = lax.broadcasted_iota(jnp.int32, kv2.shape, 1) < HEAD_DIM
    rolled = pltpu.roll(kv2, HEAD_DIM, 1)
    one = jnp.where(first, kv2, rolled) if kvh == 0 else jnp.where(first, rolled, kv2)
    return jnp.concatenate([one, one], axis=1)


def _fold_head(v4):
    a = v4[:, 0:128] + v4[:, 128:256]
    return a + pltpu.roll(a, HEAD_DIM, 1)


ATTN_CONSTS = [pltpu.VMEM((KV_HEADS, GROUP * BLOCK, 2 * BLOCK), F32)]


def _fill_attn_bias(bias_ref):
    row = lax.broadcasted_iota(jnp.int32, (GROUP * BLOCK, 2 * BLOCK), 0)
    col = lax.broadcasted_iota(jnp.int32, (GROUP * BLOCK, 2 * BLOCK), 1)
    dist = (row % BLOCK) + BLOCK - col
    band = (dist >= 0) & (dist < BLOCK)
    rb = lax.broadcasted_iota(jnp.int32, (GROUP * BLOCK, 1), 0) // BLOCK
    for kvh in range(KV_HEADS):
        slope = jnp.zeros((GROUP * BLOCK, 1), F32)
        for g in range(GROUP):
            slope = jnp.where(rb == g, 2.0 ** (-(kvh * GROUP + g + 1)), slope)
        bias_ref[kvh] = jnp.where(band, -slope * dist.astype(F32), NEG)


def _row_sinks(kvh, sink_ref):
    rb = lax.broadcasted_iota(jnp.int32, (GROUP * BLOCK, 1), 0) // BLOCK
    sink = jnp.zeros((GROUP * BLOCK, 1), F32)
    for g in range(GROUP):
        sink = jnp.where(rb == g, sink_ref[0, kvh * GROUP + g], sink)
    return sink


def _attn_probs(qk, k4, bias, sink, no_past, masks):
    qs = _stack_groups(qk, masks, HEAD_DIM ** -0.5).astype(BF16)
    s = lax.dot_general(qs, k4, NT, preferred_element_type=F32) + bias
    s = jnp.concatenate([jnp.where(no_past, NEG, s[:, 0:BLOCK]), s[:, BLOCK:]], axis=1)
    mx = jnp.maximum(jnp.max(s, axis=-1, keepdims=True), sink)
    e = jnp.exp(s - mx)
    es = jnp.exp(sink - mx)
    inv = 1.0 / (jnp.sum(e, axis=-1, keepdims=True) + es)
    return qs, e * inv, es * inv


def _pool_forward(u_ext, g, t0):
    n = u_ext.shape[0] - POOL_HALO
    s = u_ext
    for step in range(g + 1):
        s = s + pltpu.roll(s, 1 << step, 0)
    w = 2 << g
    t = t0 + lax.broadcasted_iota(jnp.int32, (n, 1), 0)
    cnt = jnp.minimum(t + 1, w).astype(F32)
    return s[POOL_HALO:] / cnt - u_ext[POOL_HALO:]


def _layer0_fwd(proj, sinks, pool_w, pool_scale, w_out, x_in, post, comm=None, tq=512):
    t = proj.shape[0]
    nblk = tq // BLOCK

    def body(main_ref, halo_ref, sink_ref, pw_ref, ps_ref, w_ref, x_ref, p_ref, o_ref, y_ref, xo_ref, kv_ref, bias_ref):
        i = pl.program_id(0)
        t0 = i * tq
        masks = _group_masks()

        @pl.when(i == 0)
        def _():
            _fill_attn_bias(bias_ref)

        kv_ref[0:BLOCK, :] = halo_ref[:, COL_K:COL_K + 256]
        kv_ref[BLOCK:, :] = main_ref[:, COL_K:COL_K + 256]

        def block(jb, carry):
            r0 = pl.multiple_of(jb * BLOCK, BLOCK)
            no_past = t0 + r0 == 0
            q = main_ref[pl.ds(r0, BLOCK), COL_Q:COL_Q + ATTN_WIDTH]
            ga = main_ref[pl.ds(r0, BLOCK), COL_GA:COL_GA + ATTN_WIDTH]
            kk = kv_ref[pl.ds(r0, 2 * BLOCK), 0:128]
            vv = kv_ref[pl.ds(r0, 2 * BLOCK), 128:256]
            outs = []
            for kvh in range(KV_HEADS):
                k4 = _repeat_head(kk, kvh).astype(BF16)
                v4 = _repeat_head(vv, kvh).astype(BF16)
                _, p, _ = _attn_probs(q[:, kvh * 256:(kvh + 1) * 256], k4, bias_ref[kvh], _row_sinks(kvh, sink_ref), no_past, masks)
                pv = jnp.dot(p.astype(BF16), v4, preferred_element_type=F32)
                outs.append(_unstack_groups(pv, masks))
            attn = jnp.concatenate(outs, axis=1)
            o_ref[pl.ds(r0, BLOCK), 0:ATTN_WIDTH] = (attn * _silu(ga)).astype(BF16)
            return carry

        lax.fori_loop(0, nblk, block, 0, unroll=True)

        for g in range(POOL_GROUPS):
            cu = COL_U + g * POOL_GC
            cg = COL_GB + g * POOL_GC
            halo_u = jnp.where(i == 0, 0.0, halo_ref[BLOCK - POOL_HALO:BLOCK, cu:cu + POOL_GC])
            u_ext = jnp.concatenate([halo_u, main_ref[:, cu:cu + POOL_GC]], axis=0)
            pooled = _pool_forward(u_ext, g, t0)
            y = jnp.dot(pooled.astype(BF16), pw_ref[g].astype(BF16), preferred_element_type=F32)
            y = y * ps_ref[:, g * POOL_GC:(g + 1) * POOL_GC]
            o_ref[:, ATTN_WIDTH + g * POOL_GC:ATTN_WIDTH + (g + 1) * POOL_GC] =(y * _silu(main_ref[:, cg:cg + POOL_GC])).astype(BF16)

        xo_ref[...] = _project_out(o_ref[...], w_ref, x_ref, p_ref, y_ref)

    row = pl.BlockSpec((tq, D_MODEL), lambda i: (i, 0))
    return _fused_call(
        body, comm, (proj, proj, sinks, pool_w, pool_scale, w_out, x_in, post), name="layer0_fwd", grid=(t // tq,),
        out_shape=[jax.ShapeDtypeStruct((t, D_MODEL), BF16), jax.ShapeDtypeStruct((t, D_MODEL), F32),
                   jax.ShapeDtypeStruct((t, D_MODEL), F32)],
        in_specs=[pl.BlockSpec((tq, EVEN_IN), lambda i: (i, 0)),
                  pl.BlockSpec((BLOCK, EVEN_IN), lambda i: (jnp.maximum(i * nblk - 1, 0), 0)),
                  pl.BlockSpec(memory_space=pltpu.SMEM),
                  _const_spec((POOL_GROUPS, POOL_GC, POOL_GC)), _const_spec((1, POOL_WIDTH)),
                  _const_spec((D_MODEL, D_MODEL)), row, _const_spec((1, D_MODEL))],
        out_specs=[row, row, row],
        scratch_shapes=[pltpu.VMEM((tq + BLOCK, 256), F32)] + ATTN_CONSTS,
        params=_params("arbitrary"))


def _layer0_bwd(proj, gy, y, mix, w_out, post, sinks, pool_w, pool_scale, comm=None, tq=512):
    t = proj.shape[0]
    nt = t // tq
    nblk = tq // BLOCK

    def body(main_ref, halo_ref, next_ref, gy_ref, gyn_ref, y_ref, yn_ref, mix_ref, wo_ref, po_ref,
             sink_ref, pw_ref, ps_ref,
             o_ref, dsk_ref, dpw_ref, dps_ref, dwo16_ref, dpo_ref,
             kv_ref, dkv_ref, carry_ref, bias_ref, dwo_ref, dmix_ref):
        i = pl.program_id(0)
        ii = nt - 1 - i
        t0 = ii * tq
        masks = _group_masks()

        @pl.when(i == 0)
        def _():
            _fill_attn_bias(bias_ref)
            dsk_ref[...] = jnp.zeros_like(dsk_ref)
            dpw_ref[...] = jnp.zeros_like(dpw_ref)
            dps_ref[...] = jnp.zeros_like(dps_ref)
            carry_ref[...] = jnp.zeros_like(carry_ref)

        dmix_ref[...] = _post_bwd_rows(jnp.concatenate([gy_ref[...], gyn_ref[...]], axis=0),
                                       jnp.concatenate([y_ref[...], yn_ref[...]], axis=0), mix_ref[...], tq,
                                       i == 0, i == nt - 1, po_ref, wo_ref, dwo_ref, dwo16_ref, dpo_ref)
        dm_ref = dmix_ref.at[pl.ds(0, tq)]
        dmn_ref = dmix_ref.at[pl.ds(tq, POOL_HALO)]

        kv_ref[0:BLOCK, :] = halo_ref[:, COL_K:COL_K + 256]
        kv_ref[BLOCK:, :] = main_ref[:, COL_K:COL_K + 256]
        dkv_ref[0:tq, :] = jnp.zeros((tq, 256), F32)
        dkv_ref[tq:, :] = carry_ref[...]

        def block(jb, carry):
            r0 = pl.multiple_of(jb * BLOCK, BLOCK)
            no_past = t0 + r0 == 0
            q = main_ref[pl.ds(r0, BLOCK), COL_Q:COL_Q + ATTN_WIDTH]
            ga = main_ref[pl.ds(r0, BLOCK), COL_GA:COL_GA + ATTN_WIDTH]
            dya = dm_ref[pl.ds(r0, BLOCK), 0:ATTN_WIDTH]
            kk = kv_ref[pl.ds(r0, 2 * BLOCK), 0:128]
            vv = kv_ref[pl.ds(r0, 2 * BLOCK), 128:256]
            silu_ga, dsilu_ga = _silu_and_grad(ga)
            do = dya * silu_ga
            first = lax.broadcasted_iota(jnp.int32, (2 * BLOCK, 128), 1) < HEAD_DIM
            attn, dq, dk, dv = [], [], [], []
            for kvh in range(KV_HEADS):
                k4 = _repeat_head(kk, kvh).astype(BF16)
                v4 = _repeat_head(vv, kvh).astype(BF16)
                qs, p, ps = _attn_probs(q[:, kvh * 256:(kvh + 1) * 256], k4, bias_ref[kvh], _row_sinks(kvh, sink_ref), no_past, masks)
                pb = p.astype(BF16)
                o_k = _unstack_groups(jnp.dot(pb, v4, preferred_element_type=F32), masks)
                do_k = do[:, kvh * 256:(kvh + 1) * 256]
                dos = _stack_groups(do_k, masks).astype(BF16)
                prod = do_k * o_k
                delta = jnp.concatenate([jnp.sum(jnp.where(m, prod, 0.0), axis=-1, keepdims=True) for m in masks], axis=0)
                dp = lax.dot_general(dos, v4, NT, preferred_element_type=F32)
                ds = (p * (dp - delta)).astype(BF16)
                sink_term = ps * delta
                for g in range(GROUP):
                    h = kvh * GROUP + g
                    dsk_ref[h:h + 1, :] -= jnp.sum(sink_term[g * BLOCK:(g + 1) * BLOCK], keepdims=True)
                dq.append(_unstack_groups(jnp.dot(ds, k4, preferred_element_type=F32), masks) * (HEAD_DIM ** -0.5))
                dk.append(_fold_head(lax.dot_general(ds, qs, TN, preferred_element_type=F32)))
                dv.append(_fold_head(lax.dot_general(pb, dos, TN, preferred_element_type=F32)))
                attn.append(o_k)
            o_ref[pl.ds(r0, BLOCK), COL_Q:COL_Q + ATTN_WIDTH] = jnp.concatenate(dq, axis=1).astype(BF16)
            o_all = jnp.concatenate(attn, axis=1)
            o_ref[pl.ds(r0, BLOCK), COL_GA:COL_GA + ATTN_WIDTH] = (dya * o_all * dsilu_ga).astype(BF16)
            dkv = jnp.concatenate([jnp.where(first, dk[0], dk[1]), jnp.where(first, dv[0], dv[1])], axis=1)
            dkv_ref[pl.ds(r0, 2 * BLOCK), :] += dkv
            return carry

        lax.fori_loop(0, nblk, block, 0, unroll=True)
        carry_ref[...] = dkv_ref[0:BLOCK, :]
        o_ref[:, COL_K:COL_K + 256] = dkv_ref[BLOCK:, :].astype(BF16)

        last = ii == nt - 1
        for g in range(POOL_GROUPS):
            cu = COL_U + g * POOL_GC
            cg = COL_GB + g * POOL_GC
            cm = ATTN_WIDTH + g * POOL_GC
            pw = pw_ref[g].astype(BF16)
            sc = ps_ref[:, g * POOL_GC:(g + 1) * POOL_GC]
            halo_u = jnp.where(ii == 0, 0.0, halo_ref[BLOCK - POOL_HALO:BLOCK, cu:cu + POOL_GC])
            u_ext = jnp.concatenate([halo_u, main_ref[:, cu:cu + POOL_GC]], axis=0)
            pooled = _pool_forward(u_ext, g, t0).astype(BF16)
            y_raw = jnp.dot(pooled, pw, preferred_element_type=F32)
            gb = main_ref[:, cg:cg + POOL_GC]
            dyb = dm_ref[:, cm:cm + POOL_GC]
            silu_gb, dsilu_gb = _silu_and_grad(gb)
            dypool = dyb * silu_gb
            dps_ref[:, g * POOL_GC:(g + 1) * POOL_GC] += jnp.sum(dypool * y_raw, axis=0, keepdims=True)
            o_ref[:, cg:cg + POOL_GC] = (dyb * (y_raw * sc) * dsilu_gb).astype(BF16)
            dyraw = dypool * sc
            dyraw_n = jnp.where(last, 0.0, dmn_ref[:, cm:cm + POOL_GC] * _silu(next_ref[:, cg:cg + POOL_GC]) * sc)
            dpw_ref[g * POOL_GC:(g + 1) * POOL_GC, :] += lax.dot_general(pooled, dyraw.astype(BF16), TN,
                                                                         preferred_element_type=F32)
            dyraw_ext = jnp.concatenate([dyraw, dyraw_n], axis=0).astype(BF16)
            dpooled = lax.dot_general(dyraw_ext, pw, NT, preferred_element_type=F32)
            w = 2 << g
            tt = t0 + lax.broadcasted_iota(jnp.int32, (tq + POOL_HALO, 1), 0)
            s = dpooled / jnp.minimum(tt + 1, w).astype(F32)
            for step in range(g + 1):
                s = s + pltpu.roll(s, tq + POOL_HALO - (1 << step), 0)
            o_ref[:, cu:cu + POOL_GC] = (s[0:tq] - dpooled[0:tq]).astype(BF16)

    rev = lambda i: nt - 1 - i
    nxt = lambda i: (jnp.minimum((rev(i) + 1) * (tq // POOL_HALO), t // POOL_HALO - 1), 0)
    row = pl.BlockSpec((tq, D_MODEL), lambda i: (rev(i), 0))
    nxt_row = pl.BlockSpec((POOL_HALO, D_MODEL), nxt)
    square = _const_spec((D_MODEL, D_MODEL))
    return _fused_call(
        body, comm, (proj, proj, proj, gy, gy, y, y, mix, w_out, post, sinks, pool_w, pool_scale),
        name="layer0_bwd", grid=(nt,),
        out_shape=[jax.ShapeDtypeStruct((t, EVEN_IN), BF16), jax.ShapeDtypeStruct((8, 128), F32),
                   jax.ShapeDtypeStruct((POOL_GROUPS * POOL_GC, POOL_GC), F32), jax.ShapeDtypeStruct((1, POOL_WIDTH), F32),
                   jax.ShapeDtypeStruct((D_MODEL, D_MODEL), BF16), jax.ShapeDtypeStruct((1, D_MODEL), F32)],
        in_specs=[pl.BlockSpec((tq, EVEN_IN), lambda i: (rev(i), 0)),
                  pl.BlockSpec((BLOCK, EVEN_IN), lambda i: (jnp.maximum(rev(i) * nblk - 1, 0), 0)),
                  pl.BlockSpec((POOL_HALO, EVEN_IN), nxt),
                  row, nxt_row, row, nxt_row, row, square, _const_spec((1, D_MODEL)),
                  pl.BlockSpec(memory_space=pltpu.SMEM),
                  _const_spec((POOL_GROUPS, POOL_GC, POOL_GC)), _const_spec((1, POOL_WIDTH))],
        out_specs=[pl.BlockSpec((tq, EVEN_IN), lambda i: (rev(i), 0)),
                   pl.BlockSpec((8, 128), lambda i: (0, 0)),
                   pl.BlockSpec((POOL_GROUPS * POOL_GC, POOL_GC), lambda i: (0, 0)),
                   pl.BlockSpec((1, POOL_WIDTH), lambda i: (0, 0)), square, pl.BlockSpec((1, D_MODEL), lambda i: (0, 0))],
        scratch_shapes=[pltpu.VMEM((tq + BLOCK, 256), F32), pltpu.VMEM((tq + BLOCK, 256), F32),
                        pltpu.VMEM((BLOCK, 256), F32)] + ATTN_CONSTS
        + [pltpu.VMEM((D_MODEL, D_MODEL), F32), pltpu.VMEM((tq + POOL_HALO, D_MODEL), F32)],
        params=_params("arbitrary"))


CONV_RC = 32
CONV_CC = 128
CONV_CHAINS_FWD = 4
CONV_CHAINS_BWD = 2
CONV_UNROLL = 2


def _fill_shifted(s_ref, rows):
    for b in range(1, 8):
        s_ref[b, 0:rows - 8, :] = s_ref[0, b:b + rows - 8, :]


def _tap_blocks(s_ref, r, cols, lead):
    for b in range(8):
        taps = [(a, 8 * a + b - lead) for a in range(5) if 0 <= 8 * a + b - lead < CONV_K]
        span = 8 * max(a for a, _ in taps) + CONV_RC
        blk = s_ref[b, pl.ds(r, span), cols]
        for a, k in taps:
            yield k, blk[8 * a:8 * a + CONV_RC]


def _conv_taps(s_ref, w_ref, r, cols, lead, reverse, chains):
    accs = [None] * chains
    for n, (k, blk) in enumerate(_tap_blocks(s_ref, r, cols, lead)):
        kw = CONV_K - 1 - k if reverse else k
        term = blk * w_ref[kw:kw + 1, cols]
        accs[n % chains] = term if accs[n % chains] is None else accs[n % chains] + term
    while len(accs) > 1:
        accs = [a + b for a, b in zip(accs[0::2], accs[1::2])]
    return accs[0]


def _layer_norm_fwd(cf, lng, lnb):
    mu = jnp.mean(cf, axis=-1, keepdims=True)
    xc = cf - mu
    rstd = lax.rsqrt(jnp.mean(xc * xc, axis=-1, keepdims=True) + EPS)
    chat = xc * rstd
    return chat, rstd, chat * lng + lnb


def _layer1_fwd(proj, dw, dwb, lng, lnb, w_out, x_in, post, target, tt=256):
    t = proj.shape[0]
    lead = CONV_HALO - (CONV_K - 1)

    def body(main_ref, halo_ref, w_ref, b_ref, g_ref, lb_ref, wo_ref, x_ref, p_ref, t_ref,
             o_ref, c_ref, y_ref, dl_ref, l_ref, gs_ref):
        i = pl.program_id(0)
        hv = halo_ref[...]
        gs_ref[0, 0:CONV_HALO, :] = jnp.where(i == 0, 0.0, hv[:, GLU_A] * _sigmoid(hv[:, GLU_B]))
        gs_ref[0, CONV_HALO:CONV_HALO + tt, :] = main_ref[:, GLU_A] * _sigmoid(main_ref[:, GLU_B])
        _fill_shifted(gs_ref, tt + CONV_HALO)

        for c in range(D_MODEL // CONV_CC):
            cols = slice(c * CONV_CC, (c + 1) * CONV_CC)

            def chunk(j, carry):
                r = pl.multiple_of(j * CONV_RC, CONV_RC)
                c_ref[pl.ds(r, CONV_RC), cols] = _conv_taps(gs_ref, w_ref, r, cols, lead, False, CONV_CHAINS_FWD) + b_ref[:, cols]
                return carry
            lax.fori_loop(0, tt // CONV_RC, chunk, 0, unroll=CONV_UNROLL)

        _, _, cn = _layer_norm_fwd(c_ref[...], g_ref[...], lb_ref[...])
        o_ref[...] = (_silu(cn) * _silu(main_ref[:, GATE])).astype(BF16)

        d = _project_out(o_ref[...], wo_ref, x_ref, p_ref, y_ref) - t_ref[...]
        dl_ref[...] = d * (1.0 / D_MODEL)

        @pl.when(i == 0)
        def _():
            l_ref[...] = jnp.zeros_like(l_ref)

        l_ref[...] += 0.5 * jnp.sum(jnp.mean(d * d, axis=-1, keepdims=True))

    vec = _const_spec((1, D_MODEL))
    row = pl.BlockSpec((tt, D_MODEL), lambda i: (i, 0))
    f32_rows = jax.ShapeDtypeStruct((t, D_MODEL), F32)
    return pl.pallas_call(
        body, name="layer1_fwd", grid=(t // tt,),
        out_shape=[jax.ShapeDtypeStruct((t, D_MODEL), BF16), f32_rows, f32_rows, f32_rows,
                   jax.ShapeDtypeStruct((8, 128), F32)],
        in_specs=[pl.BlockSpec((tt, 3 * D_MODEL), lambda i: (i, 0)),
                  pl.BlockSpec((CONV_HALO, 3 * D_MODEL), lambda i: (jnp.maximum(i * (tt // CONV_HALO) - 1, 0), 0)),
                  _const_spec((CONV_K, D_MODEL)), vec, vec, vec,
                  _const_spec((D_MODEL, D_MODEL)), row, vec, row],
        out_specs=[row, row, row, row, pl.BlockSpec((8, 128), lambda i: (0, 0))],
        scratch_shapes=[pltpu.VMEM((8, tt + CONV_HALO, D_MODEL), F32)],
        compiler_params=_params("arbitrary"),
    )(proj, proj, dw, dwb, lng, lnb, w_out, x_in, post, target)


def _layer1_bwd(proj, cf, gy, y, z, w_out, post, dw, lng, lnb, comm=None, tt=256):
    t = proj.shape[0]
    nt = t // tt
    te = tt + CONV_HALO

    def body(main_ref, next_ref, cf_ref, cfn_ref, gy_ref, gyn_ref, y_ref, yn_ref, z_ref, wo_ref, po_ref,
             w_ref, g_ref, lb_ref,
             o_ref, ddw_ref, ddb_ref, dg_ref, dlb_ref, dwo16_ref, dpo_ref, ds_ref, glu_ref, sb_ref, dwo_ref):
        i = pl.program_id(0)

        @pl.when(i == 0)
        def _():
            ddw_ref[...] = jnp.zeros_like(ddw_ref)
            ddb_ref[...] = jnp.zeros_like(ddb_ref)
            dg_ref[...] = jnp.zeros_like(dg_ref)
            dlb_ref[...] = jnp.zeros_like(dlb_ref)

        dzv = _post_bwd_rows(jnp.concatenate([gy_ref[...], gyn_ref[...]], axis=0),
                             jnp.concatenate([y_ref[...], yn_ref[...]], axis=0), z_ref[...], tt, i == 0, i == nt - 1,
                             po_ref, wo_ref, dwo_ref, dwo16_ref, dpo_ref)
        dzv = jnp.concatenate([dzv[0:tt], jnp.where(i < nt - 1, dzv[tt:], 0.0)], axis=0)
        lng = g_ref[...]
        chat, rstd, cn = _layer_norm_fwd(jnp.concatenate([cf_ref[...], cfn_ref[...]], axis=0), lng, lb_ref[...])
        gate = jnp.concatenate([main_ref[:, GATE], next_ref[:, GATE]], axis=0)
        silu_cn, dsilu_cn = _silu_and_grad(cn)
        silu_gate, dsilu_gate = _silu_and_grad(gate)
        o_ref[:, GATE] = (dzv * silu_cn * dsilu_gate)[0:tt].astype(BF16)
        dcn = dzv * silu_gate * dsilu_cn
        dg_ref[...] += jnp.sum((dcn * chat)[0:tt], axis=0, keepdims=True)
        dlb_ref[...] += jnp.sum(dcn[0:tt], axis=0, keepdims=True)
        dchat = dcn * lng
        dcf = rstd * (dchat - jnp.mean(dchat, axis=-1, keepdims=True) - chat * jnp.mean(dchat * chat, axis=-1, keepdims=True))
        ddb_ref[...] += jnp.sum(dcf[0:tt], axis=0, keepdims=True)
        ds_ref[0, 0:te, :] = dcf
        ds_ref[0, te:, :] = jnp.zeros((8, D_MODEL), F32)
        _fill_shifted(ds_ref, te + 8)
        sb_ref[...] = _sigmoid(main_ref[:, GLU_B])
        glu_ref[...] = main_ref[:, GLU_A] * sb_ref[...]

        for c in range(D_MODEL // CONV_CC):
            cols = slice(c * CONV_CC, (c + 1) * CONV_CC)
            gcols = slice(D_MODEL + c * CONV_CC, D_MODEL + (c + 1) * CONV_CC)

            def chunk(j, carry):
                r = pl.multiple_of(j * CONV_RC, CONV_RC)
                dglu = _conv_taps(ds_ref, w_ref, r, cols, 0, True, CONV_CHAINS_BWD)
                sb = sb_ref[pl.ds(r, CONV_RC), cols]
                o_ref[pl.ds(r, CONV_RC), cols] = (dglu * sb).astype(BF16)
                o_ref[pl.ds(r, CONV_RC), gcols] = (dglu * glu_ref[pl.ds(r, CONV_RC), cols] * (1.0 - sb)).astype(BF16)
                return carry
            lax.fori_loop(0, tt // CONV_RC, chunk, 0, unroll=CONV_UNROLL)

            def taps(j, accs):
                r = pl.multiple_of(j * CONV_RC, CONV_RC)
                gl = glu_ref[pl.ds(r, CONV_RC), cols]
                new = list(accs)
                for m, blk in _tap_blocks(ds_ref, r, cols, 0):
                    prod = blk * gl
                    part = prod[0:8]
                    for q in range(1, CONV_RC // 8):
                        part = part + prod[8 * q:8 * q + 8]
                    new[m] = new[m] + part
                return tuple(new)
            accs = lax.fori_loop(0, tt // CONV_RC, taps, tuple(jnp.zeros((8, CONV_CC), F32) for _ in range(CONV_K)))
            for m in range(CONV_K):
                k = CONV_K - 1 - m
                ddw_ref[k:k + 1, cols] += jnp.sum(accs[m], axis=0, keepdims=True)

    vec = _const_spec((1, D_MODEL))
    vec_out = pl.BlockSpec((1, D_MODEL), lambda i: (0, 0))
    row = pl.BlockSpec((tt, D_MODEL), lambda i: (i, 0))
    nxt = lambda i: (jnp.minimum((i + 1) * (tt // CONV_HALO), t // CONV_HALO - 1), 0)
    nxt_row = pl.BlockSpec((CONV_HALO, D_MODEL), nxt)
    vec_f32 = jax.ShapeDtypeStruct((1, D_MODEL), F32)
    square = _const_spec((D_MODEL, D_MODEL))
    return _fused_call(
        body, comm, (proj, proj, cf, cf, gy, gy, y, y, z, w_out, post, dw, lng, lnb), name="layer1_bwd", grid=(nt,),
        out_shape=[jax.ShapeDtypeStruct((t, 3 * D_MODEL), BF16), jax.ShapeDtypeStruct((CONV_K, D_MODEL), F32),
                   vec_f32, vec_f32, vec_f32, jax.ShapeDtypeStruct((D_MODEL, D_MODEL), BF16), vec_f32],
        in_specs=[pl.BlockSpec((tt, 3 * D_MODEL), lambda i: (i, 0)),
                  pl.BlockSpec((CONV_HALO, 3 * D_MODEL), nxt),
                  row, nxt_row, row, nxt_row, row, nxt_row, row, square, vec,
                  _const_spec((CONV_K, D_MODEL)), vec, vec],
        out_specs=[pl.BlockSpec((tt, 3 * D_MODEL), lambda i: (i, 0)),
                   pl.BlockSpec((CONV_K, D_MODEL), lambda i: (0, 0)), vec_out, vec_out, vec_out, square, vec_out],
        scratch_shapes=[pltpu.VMEM((8, te + 8, D_MODEL), F32), pltpu.VMEM((tt, D_MODEL), F32),
                        pltpu.VMEM((tt, D_MODEL), F32), pltpu.VMEM((D_MODEL, D_MODEL), F32)],
        params=_params("arbitrary"))


def _piece_sums(tensors, place, name):
    counts = [len(parts) for parts in tensors]

    def body(p_ref, *refs):
        ins, outs = refs[:sum(counts)], refs[sum(counts):]
        pos = 0
        for count, o_ref in zip(counts, outs):
            acc = ins[pos][0].astype(F32)
            for part in ins[pos + 1:pos + count]:
                acc = acc + part[0].astype(F32)
            o_ref[0] = acc
            pos += count

    spec = lambda a, slot: pl.BlockSpec((1,) + a.shape[1:], lambda j, p_ref: (slot(p_ref), 0, 0))
    return pl.pallas_call(
        body, name=name,
        grid_spec=pltpu.PrefetchScalarGridSpec(
            num_scalar_prefetch=1, grid=(1,),
            in_specs=[spec(a, slot) for parts in tensors for a, slot in parts],
            out_specs=[pl.BlockSpec((1,) + parts[0][0].shape[1:], lambda j, p_ref: (p_ref[1], 0, 0))
                       for parts in tensors]),
        out_shape=[jax.ShapeDtypeStruct((2,) + parts[0][0].shape[1:], F32) for parts in tensors],
        compiler_params=_params("arbitrary"),
    )(place, *[a for parts in tensors for a, _ in parts])


def _direct_parts(own, recv):
    peer = lambda m: (lambda p: p[0] ^ m)
    return [(own, peer(0))] + [(recv, peer(m)) for m in range(1, N_DEV)]


def _share_with_sibling(halves, small, name):
    n = len(halves)

    def body(*refs):
        small_ref, outs, all_ref = refs[n], refs[n + 1:2 * n + 1], refs[2 * n + 1]
        send_sems, recv_sems, local_sem = refs[2 * n + 2:]
        x, y, c, _ = _place()
        me = 4 * x + 2 * y + c
        send = [pltpu.make_async_remote_copy(
            src_ref=outs[t].at[c], dst_ref=outs[t].at[c], send_sem=send_sems.at[t], recv_sem=recv_sems.at[t],
            device_id=(x, y, 1 - c), device_id_type=MESH_ID) for t in range(n)]
        recv = [pltpu.make_async_remote_copy(
            src_ref=outs[t].at[c], dst_ref=outs[t].at[1 - c], send_sem=send_sems.at[t], recv_sem=recv_sems.at[t],
            device_id=(x, y, 1 - c), device_id_type=MESH_ID) for t in range(n)]
        for m in range(1, N_DEV):
            px, py, pc = x ^ (m >> 2), y ^ ((m >> 1) & 1), c ^ (m & 1)
            sems = dict(send_sem=send_sems.at[n + m - 1], recv_sem=recv_sems.at[n + m - 1], device_id=(px, py, pc),
                        device_id_type=MESH_ID)
            send.append(pltpu.make_async_remote_copy(src_ref=small_ref, dst_ref=all_ref.at[me], **sems))
            recv.append(pltpu.make_async_remote_copy(src_ref=small_ref, dst_ref=all_ref.at[4 * px + 2 * py + pc], **sems))
        mine = pltpu.make_async_copy(small_ref, all_ref.at[me], local_sem)
        mine.start()
        for cp in send:
            cp.start()
        for cp in recv:
            cp.wait_recv()
        for cp in send:
            cp.wait_send()
        mine.wait()

    k = n + N_DEV - 1
    return pl.pallas_call(
        body, name=name,
        out_shape=[jax.ShapeDtypeStruct(h.shape, h.dtype) for h in halves]
        + [jax.ShapeDtypeStruct((N_DEV,) + small.shape, small.dtype)],
        in_specs=[ANY] * (n + 1), out_specs=[ANY] * (n + 1),
        input_output_aliases={t: t for t in range(n)},
        scratch_shapes=[pltpu.SemaphoreType.DMA((k,)), pltpu.SemaphoreType.DMA((k,)), pltpu.SemaphoreType.DMA],
    )(*halves, small)


def _sum8(parts, name):
    n = len(parts)

    def body(*refs):
        for p_ref, o_ref in zip(refs[:n], refs[n:]):
            acc = p_ref[0]
            for k in range(1, N_DEV):
                acc = acc + p_ref[k]
            o_ref[...] = acc

    whole = pl.BlockSpec(memory_space=pltpu.VMEM)
    return pl.pallas_call(
        body, name=name, out_shape=[jax.ShapeDtypeStruct(p.shape[1:], F32) for p in parts],
        in_specs=[whole] * n, out_specs=[whole] * n,
    )(*parts)


def _adamw(tensors, steps, name):
    n = len(tensors)
    views, specs = [], []
    for w, _, _, _ in tensors:
        cols = w.shape[-1]
        rows = w.size // cols
        if w.ndim >= 3 and w.shape[-2] == 1:
            assert steps == 1
            views.append((rows, 1, cols))
            specs.append(pl.BlockSpec((rows, 1, cols), lambda i: (0, 0, 0)))
        else:
            views.append((rows, cols))
            specs.append(pl.BlockSpec((rows // steps, cols), lambda i: (i, 0)))

    def body(*refs):
        for t in range(n):
            w_ref, g_ref, m_ref, v_ref = refs[4 * t:4 * t + 4]
            d_ref, nm_ref, nv_ref = refs[4 * n + 3 * t:4 * n + 3 * t + 3]
            gv = g_ref[...]
            mn = ADAM_B1 * m_ref[...] + (1.0 - ADAM_B1) * gv
            vn = ADAM_B2 * v_ref[...] + (1.0 - ADAM_B2) * (gv * gv)
            m_hat = mn / (1.0 - ADAM_B1 ** ADAM_STEP)
            v_hat = vn / (1.0 - ADAM_B2 ** ADAM_STEP)
            d_ref[...] = -ADAM_LR * (m_hat / (jnp.sqrt(v_hat) + ADAM_EPS) + ADAM_WD * w_ref[...])
            nm_ref[...] = mn
            nv_ref[...] = vn

    outs = pl.pallas_call(
        body, name=name, grid=(steps,),
        out_shape=[jax.ShapeDtypeStruct(view, F32) for view in views for _ in range(3)],
        in_specs=[spec for spec in specs for _ in range(4)], out_specs=[spec for spec in specs for _ in range(3)],
        compiler_params=_params("parallel"),
    )(*[a.reshape(view) for group, view in zip(tensors, views) for a in group])
    return [tuple(o.reshape(group[0].shape) for o in outs[3 * t:3 * t + 3]) for t, group in enumerate(tensors)]


SMALL_ROWS = 832


def _pack_small(g):
    parts = [g["loss"], g["pre1"].reshape(8, 128), g["post0"].reshape(8, 128),
             g["post1"].reshape(8, 128), g["sinks"], jnp.pad(g["pool_scale"].reshape(4, 128), ((0, 4), (0, 0))),
             g["pool_w"], g["dw"].reshape(248, 128), g["dwb"].reshape(8, 128), g["lng"].reshape(8, 128),
             g["lnb"].reshape(8, 128)]
    assert sum(p.shape[0] for p in parts) == SMALL_ROWS
    return jnp.concatenate(parts, axis=0)


def _unpack_small(s):
    out, r = {}, 0
    for key, rows, shape in (("loss", 8, (8, 128)), ("pre1", 8, (1, D_MODEL)), ("post", 16, (2, D_MODEL)),
                             ("sinks", 8, (8, 128)),
                             ("pool_scale", 4, (1, POOL_WIDTH)), ("pad", 4, (4, 128)), ("pool_w", 512, (1, 4, 128, 128)),
                             ("dw", 248, (CONV_K, D_MODEL)), ("dwb", 8, (1, D_MODEL)), ("lng", 8, (1, D_MODEL)),
                             ("lnb", 8, (1, D_MODEL))):
        out[key] = s[r:r + rows].reshape(shape)
        r += rows
    return out


def kernel(x, pre_norm, post_norm, a_w_in, a_sinks, b_pool_w, b_pool_scale, ab_w_out, c_w_in, c_dw_w, c_dw_b, c_ln_g, c_ln_b, c_w_out, loss_target, m_pre_norm, m_post_norm, m_a_w_in, m_a_sinks, m_b_pool_w, m_b_pool_scale, m_ab_w_out, m_c_w_in, m_c_dw_w, m_c_dw_b, m_c_ln_g, m_c_ln_b, m_c_w_out, v_pre_norm, v_post_norm, v_a_w_in, v_a_sinks, v_b_pool_w, v_b_pool_scale, v_ab_w_out, v_c_w_in, v_c_dw_w, v_c_dw_b, v_c_ln_g, v_c_ln_b, v_c_w_out):
    ix, iy = lax.axis_index("x"), lax.axis_index("y")
    chip_cols = (2 * ix + iy) * 256

    pad8 = lambda v: jnp.pad(v, ((0, -v.shape[0] % 8), (0, 0)))
    vec_shard = jnp.concatenate([pad8(c_dw_w.reshape(CONV_K, 256)), pad8(c_dw_b), pad8(c_ln_g), pad8(c_ln_b),
                                 jnp.zeros((8, 256), F32)], axis=0)
    x0, target = x[0], loss_target[0]
    pre0, pre1 = pre_norm[0:1], pre_norm[1:2]
    post0, post1 = post_norm[0:1], post_norm[1:2]
    pool_w = b_pool_w[0]

    (wa_t,) = _run_comm(_Gather([a_w_in[0].T.astype(BF16)], halve=True), "gather_a_w_in")
    wa_t = wa_t.reshape(EVEN_IN, D_MODEL)
    proj0, (w_ab,) = _norm_matmul(x0, pre0, wa_t, "proj0_fwd", comm=_Gather([ab_w_out[0].astype(BF16)], halve=True))
    w_ab = w_ab.reshape(D_MODEL, D_MODEL)
    (mix0, y0, x1), (wc_t, w_c, vecs) = _layer0_fwd(
        proj0, a_sinks, pool_w, b_pool_scale, w_ab, x0, post0,
        comm=_Gather([c_w_in[0].T.astype(BF16), c_w_out[0].astype(BF16), vec_shard], halve=True))
    wc_t = wc_t.reshape(3 * D_MODEL, D_MODEL)
    w_c = w_c.reshape(D_MODEL, D_MODEL)
    vecs = vecs.reshape(4, 64, 256).transpose(1, 0, 2).reshape(64, D_MODEL)
    dw, dwb, lng, lnb = vecs[0:CONV_K], vecs[32:33], vecs[40:41], vecs[48:49]
    proj1, _ = _norm_matmul(x1, pre1, wc_t, "proj1_fwd")
    z1, cf1, y1, g2, loss = _layer1_fwd(proj1, dw, dwb, lng, lnb, w_c, x1, post1, target)

    pieces = lambda m: m.reshape(N_DEV, m.shape[0] // N_DEV, D_MODEL)
    (dproj1, d_dw, d_dwb, d_lng, d_lnb, d_wc, d_post1), _ = _layer1_bwd(proj1, cf1, g2, y1, z1, w_c, post1, dw, lng, lnb)
    (g1, d_wct, d_pre1), _ = _pre_bwd(dproj1, wc_t, x1, pre1, g2, "proj1_bwd")
    (dproj0, d_sinks, d_pw, d_ps, d_wab, d_post0), (r_wc, r_wct) = _layer0_bwd(
        proj0, g1, y0, mix0, w_ab, post0, a_sinks, pool_w, b_pool_scale,
        comm=_Scatter([pieces(d_wc), pieces(d_wct)]))
    g = dict(loss=loss, pre1=d_pre1, post0=d_post0, post1=d_post1, sinks=d_sinks, pool_w=d_pw, pool_scale=d_ps,
             dw=d_dw, dwb=d_dwb, lng=d_lng, lnb=d_lnb)
    d_wat, (small8, r_wab) = _proj_dw(dproj0, x0, pre0, "proj0_dw",
                                      comm=_Comms(_Gather([_pack_small(g)], halve=False), _Scatter([pieces(d_wab)])))
    sent = _scatter_start(pieces(d_wat), "scatter_a_start")
    (gx, d_pre0), _ = _proj_dx(dproj0, wa_t, x0, pre0 + sent[4][0:1, 0:1], g1, "proj0_dx")
    own_wat, r_wat = _scatter_wait(*sent[:4], d_pre0, "scatter_a_wait")

    ic = lax.axis_index("c")
    me = 4 * ix + 2 * iy + ic
    place = jnp.stack([me, ic]).astype(jnp.int32)
    parts = [_direct_parts(own_wat, r_wat), _direct_parts(pieces(d_wab), r_wab), _direct_parts(pieces(d_wct), r_wct),
             _direct_parts(pieces(d_wc), r_wc)]
    halves = _piece_sums(parts, place, "grad_sums")
    *shared, pre0_8 = _share_with_sibling(halves, d_pre0.reshape(8, 128), "grad_share")
    g_wa_t, g_wab, g_wc_t, g_wc = [h.reshape(2 * h.shape[1], D_MODEL) for h in shared]
    g_c_w_in = g_wc_t.T[None]
    g_ab_w_out = g_wab[None]
    g_c_w_out = g_wc[None]

    small, pre0 = _sum8([small8, pre0_8], "small_sums")
    s = _unpack_small(small)
    layer = lax.broadcasted_iota(jnp.int32, (2, D_MODEL), 0)
    g_pre = jnp.where(layer == 0, pre0.reshape(1, D_MODEL), s["pre1"])
    g_post = s["post"]
    g_sinks = s["sinks"][:, 0].reshape(1, 8)
    g_pool_w, g_pool_scale = s["pool_w"], s["pool_scale"]
    g_dw = lax.dynamic_slice_in_dim(s["dw"], chip_cols, 256, axis=1).reshape(1, CONV_K, 1, 256)
    g_dwb = lax.dynamic_slice_in_dim(s["dwb"], chip_cols, 256, axis=1)
    g_lng = lax.dynamic_slice_in_dim(s["lng"], chip_cols, 256, axis=1)
    g_lnb = lax.dynamic_slice_in_dim(s["lnb"], chip_cols, 256, axis=1)

    turn = lambda a: jnp.swapaxes(a, 1, 2)
    a_w_in, m_a_w_in, v_a_w_in = turn(a_w_in), turn(m_a_w_in), turn(v_a_w_in)
    grads = [g_pre, g_post, g_wa_t[None], g_sinks, g_pool_w, g_pool_scale, g_ab_w_out, g_c_w_in, g_dw, g_dwb, g_lng,
             g_lnb, g_c_w_out]
    weights = [pre_norm, post_norm, a_w_in, a_sinks, b_pool_w, b_pool_scale, ab_w_out, c_w_in, c_dw_w, c_dw_b, c_ln_g,
               c_ln_b, c_w_out]
    moms = [m_pre_norm, m_post_norm, m_a_w_in, m_a_sinks, m_b_pool_w, m_b_pool_scale, m_ab_w_out, m_c_w_in, m_c_dw_w,
            m_c_dw_b, m_c_ln_g, m_c_ln_b, m_c_w_out]
    vars_ = [v_pre_norm, v_post_norm, v_a_w_in, v_a_sinks, v_b_pool_w, v_b_pool_scale, v_ab_w_out, v_c_w_in, v_c_dw_w,
             v_c_dw_b, v_c_ln_g, v_c_ln_b, v_c_w_out]
    tensors = list(zip(weights, grads, moms, vars_))
    matrices = (2, 6, 7, 12)
    others = [k for k in range(len(tensors)) if k not in matrices]
    updates = dict(zip(matrices, _adamw([tensors[k] for k in matrices], 4, "adamw_matrices")))
    updates.update(zip(others, _adamw([tensors[k] for k in others], 1, "adamw_small")))
    deltas, new_m, new_v = ([updates[k][j] for k in range(len(tensors))] for j in range(3))
    for outs in (grads, deltas, new_m, new_v):
        outs[2] = turn(outs[2])
    return (s["loss"][0, 0], gx[None], *grads, *deltas, *new_m, *new_v)
```

```python
import jax
import jax.numpy as jnp
from jax import lax
from jax.experimental import pallas as pl
from jax.experimental.pallas import tpu as pltpu

F32 = jnp.float32
BF16 = jnp.bfloat16

D_MODEL = 1024
EPS = 1e-6
NEG = -1e30
HEAD_DIM = 64
GROUP = 4
KV_HEADS = 2
BLOCK = 128
EVEN_IN = 2304
ATTN_WIDTH = 512
POOL_WIDTH = 512
COL_Q, COL_K, COL_GA, COL_U, COL_GB = 0, 512, 768, 1280, 1792
POOL_GROUPS = 4
POOL_GC = 128
POOL_HALO = 16
CONV_K = 31
CONV_HALO = 32
GLU_A = slice(0, D_MODEL)
GLU_B = slice(D_MODEL, 2 * D_MODEL)
GATE = slice(2 * D_MODEL, 3 * D_MODEL)
N_DEV = 8

ADAM_LR = 0.001
ADAM_B1 = 0.9
ADAM_B2 = 0.999
ADAM_EPS = 1e-08
ADAM_WD = 0.01
ADAM_STEP = 10

VMEM_LIMIT_BYTES = 56 * 1024 * 1024

NT = (((1,), (1,)), ((), ()))
TN = (((0,), (0,)), ((), ()))
MESH_ID = pl.DeviceIdType.MESH


def _params(*sem):
    return pltpu.CompilerParams(dimension_semantics=sem, vmem_limit_bytes=VMEM_LIMIT_BYTES)


def _const_spec(shape):
    nd = len(shape)
    return pl.BlockSpec(shape, lambda *_: (0,) * nd, pipeline_mode=pl.Buffered(1))


def _sigmoid(v):
    return 0.5 * jnp.tanh(0.5 * v) + 0.5


def _silu(v):
    h = 0.5 * v
    return h * jnp.tanh(h) + h


def _silu_and_grad(v):
    s = _sigmoid(v)
    silu = v * s
    return silu, s + silu * (1.0 - s)


ANY = pl.BlockSpec(memory_space=pl.ANY)


def _place():
    x, y, c = lax.axis_index("x"), lax.axis_index("y"), lax.axis_index("c")
    chips = [(1 - x, y), (x, 1 - y), (1 - x, 1 - y)]
    return x, y, c, chips


class _Gather:
    def __init__(self, blocks, halve):
        self.ins = list(blocks)
        self.halve = halve
        self.n = n = len(blocks)
        self.shapes = [((b.shape[0] // 2) if halve else b.shape[0], b.shape[1]) for b in blocks]
        self.out_shape = [jax.ShapeDtypeStruct((N_DEV, r, cols), b.dtype) for (r, cols), b in zip(self.shapes, blocks)]
        self.scratch = [pltpu.SemaphoreType.DMA((7 * n,)), pltpu.SemaphoreType.DMA((7 * n,)),
                        pltpu.SemaphoreType.DMA((n,))]

    def _copies(self, ins, outs, sems):
        send_sems, recv_sems, local_sems = sems
        x, y, c, chips = _place()
        me, sibling = (x, y, c), (x, y, 1 - c)

        def piece(t, px, py, pc):
            return outs[t].at[4 * px + 2 * py + pc]

        def own(t):
            return ins[t].at[pl.ds(c * self.shapes[t][0], self.shapes[t][0])] if self.halve else ins[t]

        def copy(t, k, block, to, src=None):
            return pltpu.make_async_remote_copy(
                src_ref=piece(t, *block) if src is None else src, dst_ref=piece(t, *block),
                send_sem=send_sems.at[7 * t + k], recv_sem=recv_sems.at[7 * t + k],
                device_id=to, device_id_type=MESH_ID)

        rng = range(self.n)
        return dict(
            mine=[pltpu.make_async_copy(own(t), piece(t, *me), local_sems.at[t]) for t in rng],
            first=[copy(t, 0, me, sibling, src=own(t)) for t in rng]
            + [copy(t, 1 + j, me, (*chip, c), src=own(t)) for t in rng for j, chip in enumerate(chips)],
            landed=[copy(t, 1 + j, (*chip, c), me) for j, chip in enumerate(chips) for t in rng],
            passed=[copy(t, 4 + j, (*chip, c), sibling) for j, chip in enumerate(chips) for t in rng],
            from_sibling=[copy(t, 0, sibling, me) for t in rng]
            + [copy(t, 4 + j, (*chip, 1 - c), me) for t in rng for j, chip in enumerate(chips)])

    def start(self, ins, outs, sems):
        d = self._copies(ins, outs, sems)
        for cp in d["mine"] + d["first"]:
            cp.start()

    def middle(self, ins, outs, sems):
        d = self._copies(ins, outs, sems)
        for got, fwd in zip(d["landed"], d["passed"]):
            got.wait_recv()
            fwd.start()

    def finish(self, ins, outs, sems):
        d = self._copies(ins, outs, sems)
        for cp in d["from_sibling"]:
            cp.wait_recv()
        for cp in d["first"] + d["passed"]:
            cp.wait_send()
        for cp in d["mine"]:
            cp.wait()


class _Scatter:
    def __init__(self, tensors):
        self.ins = list(tensors)
        self.n = n = len(tensors)
        self.out_shape = [jax.ShapeDtypeStruct(t.shape, t.dtype) for t in tensors]
        self.scratch = [pltpu.SemaphoreType.DMA((7 * n,)), pltpu.SemaphoreType.DMA((7 * n,))]

    def _copies(self, ins, outs, sems):
        send_sems, recv_sems = sems
        x, y, c, _ = _place()
        me = 4 * x + 2 * y + c
        sends, recvs = [], []
        for t in range(self.n):
            for m in range(1, N_DEV):
                px, py, pc = x ^ (m >> 2), y ^ ((m >> 1) & 1), c ^ (m & 1)
                q = 4 * px + 2 * py + pc
                sems_k = dict(send_sem=send_sems.at[7 * t + m - 1], recv_sem=recv_sems.at[7 * t + m - 1],
                              device_id=(px, py, pc), device_id_type=MESH_ID)
                sends.append(pltpu.make_async_remote_copy(src_ref=ins[t].at[q], dst_ref=outs[t].at[me], **sems_k))
                recvs.append(pltpu.make_async_remote_copy(src_ref=ins[t].at[me], dst_ref=outs[t].at[q], **sems_k))
        return sends, recvs

    def start(self, ins, outs, sems):
        for cp in self._copies(ins, outs, sems)[0]:
            cp.start()

    def middle(self, ins, outs, sems):
        pass

    def finish(self, ins, outs, sems):
        sends, recvs = self._copies(ins, outs, sems)
        for cp in recvs:
            cp.wait_recv()
        for cp in sends:
            cp.wait_send()


class _Comms:
    def __init__(self, *comms):
        self.comms = comms
        self.ins = [a for c in comms for a in c.ins]
        self.out_shape = [s for c in comms for s in c.out_shape]
        self.scratch = [s for c in comms for s in c.scratch]

    def _each(self, phase, ins, outs, sems):
        i = o = s = 0
        for c in self.comms:
            ni, no, ns = len(c.ins), len(c.out_shape), len(c.scratch)
            getattr(c, phase)(ins[i:i + ni], outs[o:o + no], sems[s:s + ns])
            i, o, s = i + ni, o + no, s + ns

    def start(self, ins, outs, sems):
        self._each("start", ins, outs, sems)

    def middle(self, ins, outs, sems):
        self._each("middle", ins, outs, sems)

    def finish(self, ins, outs, sems):
        self._each("finish", ins, outs, sems)


def _run_comm(comm, name):
    n = len(comm.ins)

    def body(*refs):
        parts = refs[:n], refs[n:2 * n], refs[2 * n:]
        comm.start(*parts)
        comm.middle(*parts)
        comm.finish(*parts)

    return pl.pallas_call(body, name=name, out_shape=comm.out_shape, in_specs=[ANY] * n, out_specs=[ANY] * n,
                          scratch_shapes=comm.scratch)(*comm.ins)


HBM_SPEC = pl.BlockSpec(memory_space=pltpu.HBM)
SEM_SPEC = pl.BlockSpec(memory_space=pltpu.SEMAPHORE)
DATAFLOW = pltpu.SideEffectType.DATAFLOW_SIDE_EFFECTING


def _scatter_copies(own_ref, land_ref, send_sems, recv_sems):
    x, y, c, _ = _place()
    me = 4 * x + 2 * y + c
    pairs = []
    for m in range(1, N_DEV):
        px, py, pc = x ^ (m >> 2), y ^ ((m >> 1) & 1), c ^ (m & 1)
        q = 4 * px + 2 * py + pc
        sems = dict(send_sem=send_sems.at[m - 1], recv_sem=recv_sems.at[m - 1], device_id=(px, py, pc),
                    device_id_type=MESH_ID)
        pairs.append((pltpu.make_async_remote_copy(src_ref=own_ref.at[q], dst_ref=land_ref.at[me], **sems),
                      pltpu.make_async_remote_copy(src_ref=own_ref.at[me], dst_ref=land_ref.at[q], **sems)))
    return pairs


def _scatter_start(own, name):
    def body(own_ref, land_ref, send_sems, recv_sems, own_thru, land_thru, token):
        for send, _ in _scatter_copies(own_ref, land_ref, send_sems, recv_sems):
            send.start()
        token[...] = jnp.zeros_like(token)

    buf = pltpu.HBM(own.shape, own.dtype)
    return pl.pallas_call(
        body, name=name,
        out_shape=(pltpu.SemaphoreType.DMA((N_DEV - 1,)), pltpu.SemaphoreType.DMA((N_DEV - 1,)), buf, buf,
                   jax.ShapeDtypeStruct((8, 128), F32)),
        in_specs=(HBM_SPEC, HBM_SPEC),
        out_specs=(SEM_SPEC, SEM_SPEC, HBM_SPEC, HBM_SPEC, pl.BlockSpec(memory_space=pltpu.VMEM)),
        input_output_aliases={0: 2, 1: 3},
        compiler_params=pltpu.CompilerParams(has_side_effects=DATAFLOW),
    )(pltpu.with_memory_space_constraint(own, pltpu.HBM),
      pltpu.with_memory_space_constraint(lax.empty(own.shape, own.dtype), pltpu.HBM))


def _scatter_wait(send_sems, recv_sems, own_thru, land_thru, after, name):
    def body(own_ref, land_ref, send_sems, recv_sems, after_ref, own_out, land_out):
        for send, recv in _scatter_copies(own_ref, land_ref, send_sems, recv_sems):
            send.wait_send()
            recv.wait_recv()

    buf = pltpu.HBM(own_thru.shape, own_thru.dtype)
    return pl.pallas_call(
        body, name=name, out_shape=(buf, buf),
        in_specs=(HBM_SPEC, HBM_SPEC, SEM_SPEC, SEM_SPEC, ANY), out_specs=(HBM_SPEC, HBM_SPEC),
        input_output_aliases={0: 0, 1: 1},
        compiler_params=pltpu.CompilerParams(has_side_effects=DATAFLOW),
    )(own_thru, land_thru, send_sems, recv_sems, after)


def _fused_call(body, comm, args, *, name, grid, out_shape, in_specs, out_specs, scratch_shapes=(), params):
    single = not isinstance(out_shape, (list, tuple))
    out_shape = [out_shape] if single else list(out_shape)
    out_specs = [out_specs] if single else list(out_specs)
    if comm is None:
        res = pl.pallas_call(body, name=name, grid=grid, out_shape=out_shape, in_specs=in_specs, out_specs=out_specs,
                             scratch_shapes=list(scratch_shapes), compiler_params=params)(*args)
        return (res[0] if single else res), []
    n_in, n_out, n_scr = len(in_specs), len(out_shape), len(scratch_shapes)
    c_in, c_out = len(comm.ins), len(comm.out_shape)
    steps = grid[0]

    def fused(*refs):
        pos = 0
        groups = []
        for size in (n_in, c_in, n_out, c_out, n_scr, len(comm.scratch)):
            groups.append(refs[pos:pos + size])
            pos += size
        ins, c_ins, outs, c_outs, scr, c_sems = groups
        i = pl.program_id(0)

        @pl.when(i == 0)
        def _():
            comm.start(c_ins, c_outs, c_sems)

        @pl.when(i == steps // 2)
        def _():
            comm.middle(c_ins, c_outs, c_sems)

        body(*ins, *outs, *scr)

        @pl.when(i == steps - 1)
        def _():
            comm.finish(c_ins, c_outs, c_sems)

    res = pl.pallas_call(
        fused, name=name, grid=grid, out_shape=out_shape + list(comm.out_shape),
        in_specs=list(in_specs) + [ANY] * c_in, out_specs=out_specs + [ANY] * c_out,
        scratch_shapes=list(scratch_shapes) + list(comm.scratch), compiler_params=params)(*args, *comm.ins)
    main = res[:n_out]
    return (main[0] if single else main), list(res[n_out:])


def _norm_matmul(x, gain, wt, name, comm=None, tm=1024):
    t, n = x.shape[0], wt.shape[0]

    def body(x_ref, g_ref, wt_ref, o_ref):
        xv = x_ref[...]
        r = lax.rsqrt(jnp.mean(xv * xv, axis=-1, keepdims=True) + EPS)
        h = (xv * r * g_ref[...]).astype(BF16)
        o_ref[...] = lax.dot_general(h, wt_ref[...], NT, preferred_element_type=F32)

    return _fused_call(
        body, comm, (x, gain, wt), name=name, grid=(t // tm,),
        out_shape=jax.ShapeDtypeStruct((t, n), F32),
        in_specs=[pl.BlockSpec((tm, D_MODEL), lambda i: (i, 0)), _const_spec((1, D_MODEL)), _const_spec((n, D_MODEL))],
        out_specs=pl.BlockSpec((tm, n), lambda i: (i, 0)),
        params=_params("arbitrary"))


def _project_out(a, w_ref, x_ref, p_ref, y_ref):
    y = jnp.dot(a, w_ref[...], preferred_element_type=F32)
    y_ref[...] = y
    ry = lax.rsqrt(jnp.mean(y * y, axis=-1, keepdims=True) + EPS)
    return x_ref[...] + (y * ry) * p_ref[...]


def _post_bwd_rows(g, y, a, n_own, first, last, p_ref, w_ref, dw_ref, dw16_ref, dp_ref):
    @pl.when(first)
    def _():
        dw_ref[...] = jnp.zeros_like(dw_ref)
        dp_ref[...] = jnp.zeros_like(dp_ref)

    ry = lax.rsqrt(jnp.mean(y * y, axis=-1, keepdims=True) + EPS)
    nv = y * ry
    dp_ref[...] += jnp.sum((g * nv)[0:n_own], axis=0, keepdims=True)
    dn = g * p_ref[...]
    dy = (ry * (dn - nv * jnp.mean(dn * nv, axis=-1, keepdims=True))).astype(BF16)
    dw_ref[...] += lax.dot_general(a, dy[0:n_own], TN, preferred_element_type=F32)

    @pl.when(last)
    def _():
        dw16_ref[...] = dw_ref[...].astype(BF16)

    return lax.dot_general(dy, w_ref[...], NT, preferred_element_type=F32)


def _pre_bwd(dproj, wt, x_in, pre, g, name, comm=None, tm=512):
    t, n = dproj.shape
    steps = t // tm

    def body(dp_ref, wt_ref, x_ref, pre_ref, g_ref, dx_ref, dwt16_ref, dpre_ref, dwt_ref):
        @pl.when(pl.program_id(0) == 0)
        def _():
            dwt_ref[...] = jnp.zeros_like(dwt_ref)
            dpre_ref[...] = jnp.zeros_like(dpre_ref)

        dpv = dp_ref[...]
        dh = jnp.dot(dpv, wt_ref[...], preferred_element_type=F32)
        xv = x_ref[...]
        r = lax.rsqrt(jnp.mean(xv * xv, axis=-1, keepdims=True) + EPS)
        xn = xv * r
        pv = pre_ref[...]
        dpre_ref[...] += jnp.sum(dh * xn, axis=0, keepdims=True)
        dxn = dh * pv
        dx_ref[...] = g_ref[...] + r * (dxn - xn * jnp.mean(dxn * xn, axis=-1, keepdims=True))
        h = (xn * pv).astype(BF16)
        dwt_ref[...] += lax.dot_general(dpv, h, TN, preferred_element_type=F32)

        @pl.when(pl.program_id(0) == steps - 1)
        def _():
            dwt16_ref[...] = dwt_ref[...].astype(BF16)

    row = pl.BlockSpec((tm, D_MODEL), lambda i: (i, 0))
    return _fused_call(
        body, comm, (dproj, wt, x_in, pre, g), name=name, grid=(steps,),
        out_shape=[jax.ShapeDtypeStruct((t, D_MODEL), F32), jax.ShapeDtypeStruct((n, D_MODEL), BF16),
                   jax.ShapeDtypeStruct((1, D_MODEL), F32)],
        in_specs=[pl.BlockSpec((tm, n), lambda i: (i, 0)), _const_spec((n, D_MODEL)), row, _const_spec((1, D_MODEL)), row],
        out_specs=[row, _const_spec((n, D_MODEL)), pl.BlockSpec((1, D_MODEL), lambda i: (0, 0))],
        scratch_shapes=[pltpu.VMEM((n, D_MODEL), F32)],
        params=_params("arbitrary"))


def _proj_dw(dproj, x_in, pre, name, comm=None, tm=1024):
    t, n = dproj.shape
    steps = t // tm

    def body(dp_ref, x_ref, pre_ref, dwt16_ref, dwt_ref):
        @pl.when(pl.program_id(0) == 0)
        def _():
            dwt_ref[...] = jnp.zeros_like(dwt_ref)

        xv = x_ref[...]
        r = lax.rsqrt(jnp.mean(xv * xv, axis=-1, keepdims=True) + EPS)
        h = (xv * r * pre_ref[...]).astype(BF16)
        dwt_ref[...] += lax.dot_general(dp_ref[...], h, TN, preferred_element_type=F32)

        @pl.when(pl.program_id(0) == steps - 1)
        def _():
            dwt16_ref[...] = dwt_ref[...].astype(BF16)

    return _fused_call(
        body, comm, (dproj, x_in, pre), name=name, grid=(steps,),
        out_shape=jax.ShapeDtypeStruct((n, D_MODEL), BF16),
        in_specs=[pl.BlockSpec((tm, n), lambda i: (i, 0)), pl.BlockSpec((tm, D_MODEL), lambda i: (i, 0)),
                  _const_spec((1, D_MODEL))],
        out_specs=pl.BlockSpec((n, D_MODEL), lambda i: (0, 0)),
        scratch_shapes=[pltpu.VMEM((n, D_MODEL), F32)],
        params=_params("arbitrary"))


def _proj_dx(dproj, wt, x_in, pre, g, name, comm=None, tm=1024):
    t, n = dproj.shape

    def body(dp_ref, wt_ref, x_ref, pre_ref, g_ref, dx_ref, dpre_ref):
        @pl.when(pl.program_id(0) == 0)
        def _():
            dpre_ref[...] = jnp.zeros_like(dpre_ref)

        dh = jnp.dot(dp_ref[...], wt_ref[...], preferred_element_type=F32)
        xv = x_ref[...]
        r = lax.rsqrt(jnp.mean(xv * xv, axis=-1, keepdims=True) + EPS)
        xn = xv * r
        dpre_ref[...] += jnp.sum(dh * xn, axis=0, keepdims=True)
        dxn = dh * pre_ref[...]
        dx_ref[...] = g_ref[...] + r * (dxn - xn * jnp.mean(dxn * xn, axis=-1, keepdims=True))

    row = pl.BlockSpec((tm, D_MODEL), lambda i: (i, 0))
    return _fused_call(
        body, comm, (dproj, wt, x_in, pre, g), name=name, grid=(t // tm,),
        out_shape=[jax.ShapeDtypeStruct((t, D_MODEL), F32), jax.ShapeDtypeStruct((1, D_MODEL), F32)],
        in_specs=[pl.BlockSpec((tm, n), lambda i: (i, 0)), _const_spec((n, D_MODEL)), row, _const_spec((1, D_MODEL)), row],
        out_specs=[row, pl.BlockSpec((1, D_MODEL), lambda i: (0, 0))],
        params=_params("arbitrary"))


def _group_masks():
    lane = lax.broadcasted_iota(jnp.int32, (1, GROUP * HEAD_DIM), 1)
    return [lane // HEAD_DIM == g for g in range(GROUP)]


def _stack_groups(v, masks, scale=1.0):
    return jnp.concatenate([v * jnp.where(m, scale, 0.0) for m in masks], axis=0)


def _unstack_groups(v, masks):
    out = v[(GROUP - 1) * BLOCK:GROUP * BLOCK]
    for g in range(GROUP - 2, -1, -1):
        out = jnp.where(masks[g], v[g * BLOCK:(g + 1) * BLOCK], out)
    return out


def _repeat_head(kv2, kvh):
    first = lax.broadcasted_iota(jnp.int32, kv2.shape, 1) < HEAD_DIM
    rolled = pltpu.roll(kv2, HEAD_DIM, 1)
    one = jnp.where(first, kv2, rolled) if kvh == 0 else jnp.where(first, rolled, kv2)
    return jnp.concatenate([one, one], axis=1)


def _fold_head(v4):
    a = v4[:, 0:128] + v4[:, 128:256]
    return a + pltpu.roll(a, HEAD_DIM, 1)


ATTN_CONSTS = [pltpu.VMEM((KV_HEADS, GROUP * BLOCK, 2 * BLOCK), F32)]


def _fill_attn_bias(bias_ref):
    row = lax.broadcasted_iota(jnp.int32, (GROUP * BLOCK, 2 * BLOCK), 0)
    col = lax.broadcasted_iota(jnp.int32, (GROUP * BLOCK, 2 * BLOCK), 1)
    dist = (row % BLOCK) + BLOCK - col
    band = (dist >= 0) & (dist < BLOCK)
    rb = lax.broadcasted_iota(jnp.int32, (GROUP * BLOCK, 1), 0) // BLOCK
    for kvh in range(KV_HEADS):
        slope = jnp.zeros((GROUP * BLOCK, 1), F32)
        for g in range(GROUP):
            slope = jnp.where(rb == g, 2.0 ** (-(kvh * GROUP + g + 1)), slope)
        bias_ref[kvh] = jnp.where(band, -slope * dist.astype(F32), NEG)


def _row_sinks(kvh, sink_ref):
    rb = lax.broadcasted_iota(jnp.int32, (GROUP * BLOCK, 1), 0) // BLOCK
    sink = jnp.zeros((GROUP * BLOCK, 1), F32)
    for g in range(GROUP):
        sink = jnp.where(rb == g, sink_ref[0, kvh * GROUP + g], sink)
    return sink


def _attn_probs(qk, k4, bias, sink, no_past, masks):
    qs = _stack_groups(qk, masks, HEAD_DIM ** -0.5).astype(BF16)
    s = lax.dot_general(qs, k4, NT, preferred_element_type=F32) + bias
    s = jnp.concatenate([jnp.where(no_past, NEG, s[:, 0:BLOCK]), s[:, BLOCK:]], axis=1)
    mx = jnp.maximum(jnp.max(s, axis=-1, keepdims=True), sink)
    e = jnp.exp(s - mx)
    es = jnp.exp(sink - mx)
    inv = 1.0 / (jnp.sum(e, axis=-1, keepdims=True) + es)
    return qs, e * inv, es * inv


def _pool_forward(u_ext, g, t0):
    n = u_ext.shape[0] - POOL_HALO
    s = u_ext
    for step in range(g + 1):
        s = s + pltpu.roll(s, 1 << step, 0)
    w = 2 << g
    t = t0 + lax.broadcasted_iota(jnp.int32, (n, 1), 0)
    cnt = jnp.minimum(t + 1, w).astype(F32)
    return s[POOL_HALO:] / cnt - u_ext[POOL_HALO:]


def _layer0_fwd(proj, sinks, pool_w, pool_scale, w_out, x_in, post, comm=None, tq=512):
    t = proj.shape[0]
    nblk = tq // BLOCK

    def body(main_ref, halo_ref, sink_ref, pw_ref, ps_ref, w_ref, x_ref, p_ref, o_ref, y_ref, xo_ref, kv_ref, bias_ref):
        i = pl.program_id(0)
        t0 = i * tq
        masks = _group_masks()

        @pl.when(i == 0)
        def _():
            _fill_attn_bias(bias_ref)

        kv_ref[0:BLOCK, :] = halo_ref[:, COL_K:COL_K + 256]
        kv_ref[BLOCK:, :] = main_ref[:, COL_K:COL_K + 256]

        def block(jb, carry):
            r0 = pl.multiple_of(jb * BLOCK, BLOCK)
            no_past = t0 + r0 == 0
            q = main_ref[pl.ds(r0, BLOCK), COL_Q:COL_Q + ATTN_WIDTH]
            ga = main_ref[pl.ds(r0, BLOCK), COL_GA:COL_GA + ATTN_WIDTH]
            kk = kv_ref[pl.ds(r0, 2 * BLOCK), 0:128]
            vv = kv_ref[pl.ds(r0, 2 * BLOCK), 128:256]
            outs = []
            for kvh in range(KV_HEADS):
                k4 = _repeat_head(kk, kvh).astype(BF16)
                v4 = _repeat_head(vv, kvh).astype(BF16)
                _, p, _ = _attn_probs(q[:, kvh * 256:(kvh + 1) * 256], k4, bias_ref[kvh], _row_sinks(kvh, sink_ref), no_past, masks)
                pv = jnp.dot(p.astype(BF16), v4, preferred_element_type=F32)
                outs.append(_unstack_groups(pv, masks))
            attn = jnp.concatenate(outs, axis=1)
            o_ref[pl.ds(r0, BLOCK), 0:ATTN_WIDTH] = (attn * _silu(ga)).astype(BF16)
            return carry

        lax.fori_loop(0, nblk, block, 0, unroll=True)

        for g in range(POOL_GROUPS):
            cu = COL_U + g * POOL_GC
            cg = COL_GB + g * POOL_GC
            halo_u = jnp.where(i == 0, 0.0, halo_ref[BLOCK - POOL_HALO:BLOCK, cu:cu + POOL_GC])
            u_ext = jnp.concatenate([halo_u, main_ref[:, cu:cu + POOL_GC]], axis=0)
            pooled = _pool_forward(u_ext, g, t0)
            y = jnp.dot(pooled.astype(BF16), pw_ref[g].astype(BF16), preferred_element_type=F32)
            y = y * ps_ref[:, g * POOL_GC:(g + 1) * POOL_GC]
            o_ref[:, ATTN_WIDTH + g * POOL_GC:ATTN_WIDTH + (g + 1) * POOL_GC] =(y * _silu(main_ref[:, cg:cg + POOL_GC])).astype(BF16)

        xo_ref[...] = _project_out(o_ref[...], w_ref, x_ref, p_ref, y_ref)

    row = pl.BlockSpec((tq, D_MODEL), lambda i: (i, 0))
    return _fused_call(
        body, comm, (proj, proj, sinks, pool_w, pool_scale, w_out, x_in, post), name="layer0_fwd", grid=(t // tq,),
        out_shape=[jax.ShapeDtypeStruct((t, D_MODEL), BF16), jax.ShapeDtypeStruct((t, D_MODEL), F32),
                   jax.ShapeDtypeStruct((t, D_MODEL), F32)],
        in_specs=[pl.BlockSpec((tq, EVEN_IN), lambda i: (i, 0)),
                  pl.BlockSpec((BLOCK, EVEN_IN), lambda i: (jnp.maximum(i * nblk - 1, 0), 0)),
                  pl.BlockSpec(memory_space=pltpu.SMEM),
                  _const_spec((POOL_GROUPS, POOL_GC, POOL_GC)), _const_spec((1, POOL_WIDTH)),
                  _const_spec((D_MODEL, D_MODEL)), row, _const_spec((1, D_MODEL))],
        out_specs=[row, row, row],
        scratch_shapes=[pltpu.VMEM((tq + BLOCK, 256), F32)] + ATTN_CONSTS,
        params=_params("arbitrary"))


def _layer0_bwd(proj, gy, y, mix, w_out, post, sinks, pool_w, pool_scale, comm=None, tq=512):
    t = proj.shape[0]
    nt = t // tq
    nblk = tq // BLOCK

    def body(main_ref, halo_ref, next_ref, gy_ref, gyn_ref, y_ref, yn_ref, mix_ref, wo_ref, po_ref,
             sink_ref, pw_ref, ps_ref,
             o_ref, dsk_ref, dpw_ref, dps_ref, dwo16_ref, dpo_ref,
             kv_ref, dkv_ref, carry_ref, bias_ref, dwo_ref, dmix_ref):
        i = pl.program_id(0)
        ii = nt - 1 - i
        t0 = ii * tq
        masks = _group_masks()

        @pl.when(i == 0)
        def _():
            _fill_attn_bias(bias_ref)
            dsk_ref[...] = jnp.zeros_like(dsk_ref)
            dpw_ref[...] = jnp.zeros_like(dpw_ref)
            dps_ref[...] = jnp.zeros_like(dps_ref)
            carry_ref[...] = jnp.zeros_like(carry_ref)

        dmix_ref[...] = _post_bwd_rows(jnp.concatenate([gy_ref[...], gyn_ref[...]], axis=0),
                                       jnp.concatenate([y_ref[...], yn_ref[...]], axis=0), mix_ref[...], tq,
                                       i == 0, i == nt - 1, po_ref, wo_ref, dwo_ref, dwo16_ref, dpo_ref)
        dm_ref = dmix_ref.at[pl.ds(0, tq)]
        dmn_ref = dmix_ref.at[pl.ds(tq, POOL_HALO)]

        kv_ref[0:BLOCK, :] = halo_ref[:, COL_K:COL_K + 256]
        kv_ref[BLOCK:, :] = main_ref[:, COL_K:COL_K + 256]
        dkv_ref[0:tq, :] = jnp.zeros((tq, 256), F32)
        dkv_ref[tq:, :] = carry_ref[...]

        def block(jb, carry):
            r0 = pl.multiple_of(jb * BLOCK, BLOCK)
            no_past = t0 + r0 == 0
            q = main_ref[pl.ds(r0, BLOCK), COL_Q:COL_Q + ATTN_WIDTH]
            ga = main_ref[pl.ds(r0, BLOCK), COL_GA:COL_GA + ATTN_WIDTH]
            dya = dm_ref[pl.ds(r0, BLOCK), 0:ATTN_WIDTH]
            kk = kv_ref[pl.ds(r0, 2 * BLOCK), 0:128]
            vv = kv_ref[pl.ds(r0, 2 * BLOCK), 128:256]
            silu_ga, dsilu_ga = _silu_and_grad(ga)
            do = dya * silu_ga
            first = lax.broadcasted_iota(jnp.int32, (2 * BLOCK, 128), 1) < HEAD_DIM
            attn, dq, dk, dv = [], [], [], []
            for kvh in range(KV_HEADS):
                k4 = _repeat_head(kk, kvh).astype(BF16)
                v4 = _repeat_head(vv, kvh).astype(BF16)
                qs, p, ps = _attn_probs(q[:, kvh * 256:(kvh + 1) * 256], k4, bias_ref[kvh], _row_sinks(kvh, sink_ref), no_past, masks)
                pb = p.astype(BF16)
                o_k = _unstack_groups(jnp.dot(pb, v4, preferred_element_type=F32), masks)
                do_k = do[:, kvh * 256:(kvh + 1) * 256]
                dos = _stack_groups(do_k, masks).astype(BF16)
                prod = do_k * o_k
                delta = jnp.concatenate([jnp.sum(jnp.where(m, prod, 0.0), axis=-1, keepdims=True) for m in masks], axis=0)
                dp = lax.dot_general(dos, v4, NT, preferred_element_type=F32)
                ds = (p * (dp - delta)).astype(BF16)
                sink_term = ps * delta
                for g in range(GROUP):
                    h = kvh * GROUP + g
                    dsk_ref[h:h + 1, :] -= jnp.sum(sink_term[g * BLOCK:(g + 1) * BLOCK], keepdims=True)
                dq.append(_unstack_groups(jnp.dot(ds, k4, preferred_element_type=F32), masks) * (HEAD_DIM ** -0.5))
                dk.append(_fold_head(lax.dot_general(ds, qs, TN, preferred_element_type=F32)))
                dv.append(_fold_head(lax.dot_general(pb, dos, TN, preferred_element_type=F32)))
                attn.append(o_k)
            o_ref[pl.ds(r0, BLOCK), COL_Q:COL_Q + ATTN_WIDTH] = jnp.concatenate(dq, axis=1).astype(BF16)
            o_all = jnp.concatenate(attn, axis=1)
            o_ref[pl.ds(r0, BLOCK), COL_GA:COL_GA + ATTN_WIDTH] = (dya * o_all * dsilu_ga).astype(BF16)
            dkv = jnp.concatenate([jnp.where(first, dk[0], dk[1]), jnp.where(first, dv[0], dv[1])], axis=1)
            dkv_ref[pl.ds(r0, 2 * BLOCK), :] += dkv
            return carry

        lax.fori_loop(0, nblk, block, 0, unroll=True)
        carry_ref[...] = dkv_ref[0:BLOCK, :]
        o_ref[:, COL_K:COL_K + 256] = dkv_ref[BLOCK:, :].astype(BF16)

        last = ii == nt - 1
        for g in range(POOL_GROUPS):
            cu = COL_U + g * POOL_GC
            cg = COL_GB + g * POOL_GC
            cm = ATTN_WIDTH + g * POOL_GC
            pw = pw_ref[g].astype(BF16)
            sc = ps_ref[:, g * POOL_GC:(g + 1) * POOL_GC]
            halo_u = jnp.where(ii == 0, 0.0, halo_ref[BLOCK - POOL_HALO:BLOCK, cu:cu + POOL_GC])
            u_ext = jnp.concatenate([halo_u, main_ref[:, cu:cu + POOL_GC]], axis=0)
            pooled = _pool_forward(u_ext, g, t0).astype(BF16)
            y_raw = jnp.dot(pooled, pw, preferred_element_type=F32)
            gb = main_ref[:, cg:cg + POOL_GC]
            dyb = dm_ref[:, cm:cm + POOL_GC]
            silu_gb, dsilu_gb = _silu_and_grad(gb)
            dypool = dyb * silu_gb
            dps_ref[:, g * POOL_GC:(g + 1) * POOL_GC] += jnp.sum(dypool * y_raw, axis=0, keepdims=True)
            o_ref[:, cg:cg + POOL_GC] = (dyb * (y_raw * sc) * dsilu_gb).astype(BF16)
            dyraw = dypool * sc
            dyraw_n = jnp.where(last, 0.0, dmn_ref[:, cm:cm + POOL_GC] * _silu(next_ref[:, cg:cg + POOL_GC]) * sc)
            dpw_ref[g * POOL_GC:(g + 1) * POOL_GC, :] += lax.dot_general(pooled, dyraw.astype(BF16), TN,
                                                                         preferred_element_type=F32)
            dyraw_ext = jnp.concatenate([dyraw, dyraw_n], axis=0).astype(BF16)
            dpooled = lax.dot_general(dyraw_ext, pw, NT, preferred_element_type=F32)
            w = 2 << g
            tt = t0 + lax.broadcasted_iota(jnp.int32, (tq + POOL_HALO, 1), 0)
            s = dpooled / jnp.minimum(tt + 1, w).astype(F32)
            for step in range(g + 1):
                s = s + pltpu.roll(s, tq + POOL_HALO - (1 << step), 0)
            o_ref[:, cu:cu + POOL_GC] = (s[0:tq] - dpooled[0:tq]).astype(BF16)

    rev = lambda i: nt - 1 - i
    nxt = lambda i: (jnp.minimum((rev(i) + 1) * (tq // POOL_HALO), t // POOL_HALO - 1), 0)
    row = pl.BlockSpec((tq, D_MODEL), lambda i: (rev(i), 0))
    nxt_row = pl.BlockSpec((POOL_HALO, D_MODEL), nxt)
    square = _const_spec((D_MODEL, D_MODEL))
    return _fused_call(
        body, comm, (proj, proj, proj, gy, gy, y, y, mix, w_out, post, sinks, pool_w, pool_scale),
        name="layer0_bwd", grid=(nt,),
        out_shape=[jax.ShapeDtypeStruct((t, EVEN_IN), BF16), jax.ShapeDtypeStruct((8, 128), F32),
                   jax.ShapeDtypeStruct((POOL_GROUPS * POOL_GC, POOL_GC), F32), jax.ShapeDtypeStruct((1, POOL_WIDTH), F32),
                   jax.ShapeDtypeStruct((D_MODEL, D_MODEL), BF16), jax.ShapeDtypeStruct((1, D_MODEL), F32)],
        in_specs=[pl.BlockSpec((tq, EVEN_IN), lambda i: (rev(i), 0)),
                  pl.BlockSpec((BLOCK, EVEN_IN), lambda i: (jnp.maximum(rev(i) * nblk - 1, 0), 0)),
                  pl.BlockSpec((POOL_HALO, EVEN_IN), nxt),
                  row, nxt_row, row, nxt_row, row, square, _const_spec((1, D_MODEL)),
                  pl.BlockSpec(memory_space=pltpu.SMEM),
                  _const_spec((POOL_GROUPS, POOL_GC, POOL_GC)), _const_spec((1, POOL_WIDTH))],
        out_specs=[pl.BlockSpec((tq, EVEN_IN), lambda i: (rev(i), 0)),
                   pl.BlockSpec((8, 128), lambda i: (0, 0)),
                   pl.BlockSpec((POOL_GROUPS * POOL_GC, POOL_GC), lambda i: (0, 0)),
                   pl.BlockSpec((1, POOL_WIDTH), lambda i: (0, 0)), square, pl.BlockSpec((1, D_MODEL), lambda i: (0, 0))],
        scratch_shapes=[pltpu.VMEM((tq + BLOCK, 256), F32), pltpu.VMEM((tq + BLOCK, 256), F32),
                        pltpu.VMEM((BLOCK, 256), F32)] + ATTN_CONSTS
        + [pltpu.VMEM((D_MODEL, D_MODEL), F32), pltpu.VMEM((tq + POOL_HALO, D_MODEL), F32)],
        params=_params("arbitrary"))


CONV_RC = 32
CONV_CC = 128
CONV_CHAINS_FWD = 4
CONV_CHAINS_BWD = 2
CONV_UNROLL = 2


def _fill_shifted(s_ref, rows):
    for b in range(1, 8):
        s_ref[b, 0:rows - 8, :] = s_ref[0, b:b + rows - 8, :]


def _tap_blocks(s_ref, r, cols, lead):
    for b in range(8):
        taps = [(a, 8 * a + b - lead) for a in range(5) if 0 <= 8 * a + b - lead < CONV_K]
        span = 8 * max(a for a, _ in taps) + CONV_RC
        blk = s_ref[b, pl.ds(r, span), cols]
        for a, k in taps:
            yield k, blk[8 * a:8 * a + CONV_RC]


def _conv_taps(s_ref, w_ref, r, cols, lead, reverse, chains):
    accs = [None] * chains
    for n, (k, blk) in enumerate(_tap_blocks(s_ref, r, cols, lead)):
        kw = CONV_K - 1 - k if reverse else k
        term = blk * w_ref[kw:kw + 1, cols]
        accs[n % chains] = term if accs[n % chains] is None else accs[n % chains] + term
    while len(accs) > 1:
        accs = [a + b for a, b in zip(accs[0::2], accs[1::2])]
    return accs[0]


def _layer_norm_fwd(cf, lng, lnb):
    mu = jnp.mean(cf, axis=-1, keepdims=True)
    xc = cf - mu
    rstd = lax.rsqrt(jnp.mean(xc * xc, axis=-1, keepdims=True) + EPS)
    chat = xc * rstd
    return chat, rstd, chat * lng + lnb


def _layer1_fwd(proj, dw, dwb, lng, lnb, w_out, x_in, post, target, tt=512):
    t = proj.shape[0]
    lead = CONV_HALO - (CONV_K - 1)

    def body(main_ref, halo_ref, w_ref, b_ref, g_ref, lb_ref, wo_ref, x_ref, p_ref, t_ref,
             o_ref, c_ref, y_ref, dl_ref, l_ref, gs_ref):
        i = pl.program_id(0)
        hv = halo_ref[...]
        gs_ref[0, 0:CONV_HALO, :] = jnp.where(i == 0, 0.0, hv[:, GLU_A] * _sigmoid(hv[:, GLU_B]))
        gs_ref[0, CONV_HALO:CONV_HALO + tt, :] = main_ref[:, GLU_A] * _sigmoid(main_ref[:, GLU_B])
        _fill_shifted(gs_ref, tt + CONV_HALO)

        for c in range(D_MODEL // CONV_CC):
            cols = slice(c * CONV_CC, (c + 1) * CONV_CC)

            def chunk(j, carry):
                r = pl.multiple_of(j * CONV_RC, CONV_RC)
                c_ref[pl.ds(r, CONV_RC), cols] = _conv_taps(gs_ref, w_ref, r, cols, lead, False, CONV_CHAINS_FWD) + b_ref[:, cols]
                return carry
            lax.fori_loop(0, tt // CONV_RC, chunk, 0, unroll=CONV_UNROLL)

        _, _, cn = _layer_norm_fwd(c_ref[...], g_ref[...], lb_ref[...])
        o_ref[...] = (_silu(cn) * _silu(main_ref[:, GATE])).astype(BF16)

        d = _project_out(o_ref[...], wo_ref, x_ref, p_ref, y_ref) - t_ref[...]
        dl_ref[...] = d * (1.0 / D_MODEL)

        @pl.when(i == 0)
        def _():
            l_ref[...] = jnp.zeros_like(l_ref)

        l_ref[...] += 0.5 * jnp.sum(jnp.mean(d * d, axis=-1, keepdims=True))

    vec = _const_spec((1, D_MODEL))
    row = pl.BlockSpec((tt, D_MODEL), lambda i: (i, 0))
    f32_rows = jax.ShapeDtypeStruct((t, D_MODEL), F32)
    return pl.pallas_call(
        body, name="layer1_fwd", grid=(t // tt,),
        out_shape=[jax.ShapeDtypeStruct((t, D_MODEL), BF16), f32_rows, f32_rows, f32_rows,
                   jax.ShapeDtypeStruct((8, 128), F32)],
        in_specs=[pl.BlockSpec((tt, 3 * D_MODEL), lambda i: (i, 0)),
                  pl.BlockSpec((CONV_HALO, 3 * D_MODEL), lambda i: (jnp.maximum(i * (tt // CONV_HALO) - 1, 0), 0)),
                  _const_spec((CONV_K, D_MODEL)), vec, vec, vec,
                  _const_spec((D_MODEL, D_MODEL)), row, vec, row],
        out_specs=[row, row, row, row, pl.BlockSpec((8, 128), lambda i: (0, 0))],
        scratch_shapes=[pltpu.VMEM((8, tt + CONV_HALO, D_MODEL), F32)],
        compiler_params=_params("arbitrary"),
    )(proj, proj, dw, dwb, lng, lnb, w_out, x_in, post, target)


def _layer1_bwd(proj, cf, gy, y, z, w_out, post, dw, lng, lnb, comm=None, tt=256):
    t = proj.shape[0]
    nt = t // tt
    te = tt + CONV_HALO

    def body(main_ref, next_ref, cf_ref, cfn_ref, gy_ref, gyn_ref, y_ref, yn_ref, z_ref, wo_ref, po_ref,
             w_ref, g_ref, lb_ref,
             o_ref, ddw_ref, ddb_ref, dg_ref, dlb_ref, dwo16_ref, dpo_ref, ds_ref, glu_ref, sb_ref, dwo_ref):
        i = pl.program_id(0)

        @pl.when(i == 0)
        def _():
            ddw_ref[...] = jnp.zeros_like(ddw_ref)
            ddb_ref[...] = jnp.zeros_like(ddb_ref)
            dg_ref[...] = jnp.zeros_like(dg_ref)
            dlb_ref[...] = jnp.zeros_like(dlb_ref)

        dzv = _post_bwd_rows(jnp.concatenate([gy_ref[...], gyn_ref[...]], axis=0),
                             jnp.concatenate([y_ref[...], yn_ref[...]], axis=0), z_ref[...], tt, i == 0, i == nt - 1,
                             po_ref, wo_ref, dwo_ref, dwo16_ref, dpo_ref)
        dzv = jnp.concatenate([dzv[0:tt], jnp.where(i < nt - 1, dzv[tt:], 0.0)], axis=0)
        lng = g_ref[...]
        chat, rstd, cn = _layer_norm_fwd(jnp.concatenate([cf_ref[...], cfn_ref[...]], axis=0), lng, lb_ref[...])
        gate = jnp.concatenate([main_ref[:, GATE], next_ref[:, GATE]], axis=0)
        silu_cn, dsilu_cn = _silu_and_grad(cn)
        silu_gate, dsilu_gate = _silu_and_grad(gate)
        o_ref[:, GATE] = (dzv * silu_cn * dsilu_gate)[0:tt].astype(BF16)
        dcn = dzv * silu_gate * dsilu_cn
        dg_ref[...] += jnp.sum((dcn * chat)[0:tt], axis=0, keepdims=True)
        dlb_ref[...] += jnp.sum(dcn[0:tt], axis=0, keepdims=True)
        dchat = dcn * lng
        dcf = rstd * (dchat - jnp.mean(dchat, axis=-1, keepdims=True) - chat * jnp.mean(dchat * chat, axis=-1, keepdims=True))
        ddb_ref[...] += jnp.sum(dcf[0:tt], axis=0, keepdims=True)
        ds_ref[0, 0:te, :] = dcf
        ds_ref[0, te:, :] = jnp.zeros((8, D_MODEL), F32)
        _fill_shifted(ds_ref, te + 8)
        sb_ref[...] = _sigmoid(main_ref[:, GLU_B])
        glu_ref[...] = main_ref[:, GLU_A] * sb_ref[...]

        for c in range(D_MODEL // CONV_CC):
            cols = slice(c * CONV_CC, (c + 1) * CONV_CC)
            gcols = slice(D_MODEL + c * CONV_CC, D_MODEL + (c + 1) * CONV_CC)

            def chunk(j, carry):
                r = pl.multiple_of(j * CONV_RC, CONV_RC)
                dglu = _conv_taps(ds_ref, w_ref, r, cols, 0, True, CONV_CHAINS_BWD)
                sb = sb_ref[pl.ds(r, CONV_RC), cols]
                o_ref[pl.ds(r, CONV_RC), cols] = (dglu * sb).astype(BF16)
                o_ref[pl.ds(r, CONV_RC), gcols] = (dglu * glu_ref[pl.ds(r, CONV_RC), cols] * (1.0 - sb)).astype(BF16)
                return carry
            lax.fori_loop(0, tt // CONV_RC, chunk, 0, unroll=CONV_UNROLL)

            def taps(j, accs):
                r = pl.multiple_of(j * CONV_RC, CONV_RC)
                gl = glu_ref[pl.ds(r, CONV_RC), cols]
                new = list(accs)
                for m, blk in _tap_blocks(ds_ref, r, cols, 0):
                    prod = blk * gl
                    part = prod[0:8]
                    for q in range(1, CONV_RC // 8):
                        part = part + prod[8 * q:8 * q + 8]
                    new[m] = new[m] + part
                return tuple(new)
            accs = lax.fori_loop(0, tt // CONV_RC, taps, tuple(jnp.zeros((8, CONV_CC), F32) for _ in range(CONV_K)))
            for m in range(CONV_K):
                k = CONV_K - 1 - m
                ddw_ref[k:k + 1, cols] += jnp.sum(accs[m], axis=0, keepdims=True)

    vec = _const_spec((1, D_MODEL))
    vec_out = pl.BlockSpec((1, D_MODEL), lambda i: (0, 0))
    row = pl.BlockSpec((tt, D_MODEL), lambda i: (i, 0))
    nxt = lambda i: (jnp.minimum((i + 1) * (tt // CONV_HALO), t // CONV_HALO - 1), 0)
    nxt_row = pl.BlockSpec((CONV_HALO, D_MODEL), nxt)
    vec_f32 = jax.ShapeDtypeStruct((1, D_MODEL), F32)
    square = _const_spec((D_MODEL, D_MODEL))
    return _fused_call(
        body, comm, (proj, proj, cf, cf, gy, gy, y, y, z, w_out, post, dw, lng, lnb), name="layer1_bwd", grid=(nt,),
        out_shape=[jax.ShapeDtypeStruct((t, 3 * D_MODEL), BF16), jax.ShapeDtypeStruct((CONV_K, D_MODEL), F32),
                   vec_f32, vec_f32, vec_f32, jax.ShapeDtypeStruct((D_MODEL, D_MODEL), BF16), vec_f32],
        in_specs=[pl.BlockSpec((tt, 3 * D_MODEL), lambda i: (i, 0)),
                  pl.BlockSpec((CONV_HALO, 3 * D_MODEL), nxt),
                  row, nxt_row, row, nxt_row, row, nxt_row, row, square, vec,
                  _const_spec((CONV_K, D_MODEL)), vec, vec],
        out_specs=[pl.BlockSpec((tt, 3 * D_MODEL), lambda i: (i, 0)),
                   pl.BlockSpec((CONV_K, D_MODEL), lambda i: (0, 0)), vec_out, vec_out, vec_out, square, vec_out],
        scratch_shapes=[pltpu.VMEM((8, te + 8, D_MODEL), F32), pltpu.VMEM((tt, D_MODEL), F32),
                        pltpu.VMEM((tt, D_MODEL), F32), pltpu.VMEM((D_MODEL, D_MODEL), F32)],
        params=_params("arbitrary"))


def _piece_sums(tensors, place, name):
    counts = [len(parts) for parts in tensors]

    def body(p_ref, *refs):
        ins, outs = refs[:sum(counts)], refs[sum(counts):]
        pos = 0
        for count, o_ref in zip(counts, outs):
            acc = ins[pos][0].astype(F32)
            for part in ins[pos + 1:pos + count]:
                acc = acc + part[0].astype(F32)
            o_ref[0] = acc
            pos += count

    spec = lambda a, slot: pl.BlockSpec((1,) + a.shape[1:], lambda j, p_ref: (slot(p_ref), 0, 0))
    return pl.pallas_call(
        body, name=name,
        grid_spec=pltpu.PrefetchScalarGridSpec(
            num_scalar_prefetch=1, grid=(1,),
            in_specs=[spec(a, slot) for parts in tensors for a, slot in parts],
            out_specs=[pl.BlockSpec((1,) + parts[0][0].shape[1:], lambda j, p_ref: (p_ref[1], 0, 0))
                       for parts in tensors]),
        out_shape=[jax.ShapeDtypeStruct((2,) + parts[0][0].shape[1:], F32) for parts in tensors],
        compiler_params=_params("arbitrary"),
    )(place, *[a for parts in tensors for a, _ in parts])


def _direct_parts(own, recv):
    peer = lambda m: (lambda p: p[0] ^ m)
    return [(own, peer(0))] + [(recv, peer(m)) for m in range(1, N_DEV)]


def _share_with_sibling(halves, small, name):
    n = len(halves)

    def body(*refs):
        small_ref, outs, all_ref = refs[n], refs[n + 1:2 * n + 1], refs[2 * n + 1]
        send_sems, recv_sems, local_sem = refs[2 * n + 2:]
        x, y, c, _ = _place()
        me = 4 * x + 2 * y + c
        send = [pltpu.make_async_remote_copy(
            src_ref=outs[t].at[c], dst_ref=outs[t].at[c], send_sem=send_sems.at[t], recv_sem=recv_sems.at[t],
            device_id=(x, y, 1 - c), device_id_type=MESH_ID) for t in range(n)]
        recv = [pltpu.make_async_remote_copy(
            src_ref=outs[t].at[c], dst_ref=outs[t].at[1 - c], send_sem=send_sems.at[t], recv_sem=recv_sems.at[t],
            device_id=(x, y, 1 - c), device_id_type=MESH_ID) for t in range(n)]
        for m in range(1, N_DEV):
            px, py, pc = x ^ (m >> 2), y ^ ((m >> 1) & 1), c ^ (m & 1)
            sems = dict(send_sem=send_sems.at[n + m - 1], recv_sem=recv_sems.at[n + m - 1], device_id=(px, py, pc),
                        device_id_type=MESH_ID)
            send.append(pltpu.make_async_remote_copy(src_ref=small_ref, dst_ref=all_ref.at[me], **sems))
            recv.append(pltpu.make_async_remote_copy(src_ref=small_ref, dst_ref=all_ref.at[4 * px + 2 * py + pc], **sems))
        mine = pltpu.make_async_copy(small_ref, all_ref.at[me], local_sem)
        mine.start()
        for cp in send:
            cp.start()
        for cp in recv:
            cp.wait_recv()
        for cp in send:
            cp.wait_send()
        mine.wait()

    k = n + N_DEV - 1
    return pl.pallas_call(
        body, name=name,
        out_shape=[jax.ShapeDtypeStruct(h.shape, h.dtype) for h in halves]
        + [jax.ShapeDtypeStruct((N_DEV,) + small.shape, small.dtype)],
        in_specs=[ANY] * (n + 1), out_specs=[ANY] * (n + 1),
        input_output_aliases={t: t for t in range(n)},
        scratch_shapes=[pltpu.SemaphoreType.DMA((k,)), pltpu.SemaphoreType.DMA((k,)), pltpu.SemaphoreType.DMA],
    )(*halves, small)


def _sum8(parts, name):
    n = len(parts)

    def body(*refs):
        for p_ref, o_ref in zip(refs[:n], refs[n:]):
            acc = p_ref[0]
            for k in range(1, N_DEV):
                acc = acc + p_ref[k]
            o_ref[...] = acc

    whole = pl.BlockSpec(memory_space=pltpu.VMEM)
    return pl.pallas_call(
        body, name=name, out_shape=[jax.ShapeDtypeStruct(p.shape[1:], F32) for p in parts],
        in_specs=[whole] * n, out_specs=[whole] * n,
    )(*parts)


def _adamw(tensors, steps, name):
    n = len(tensors)
    views, specs = [], []
    for w, _, _, _ in tensors:
        cols = w.shape[-1]
        rows = w.size // cols
        if w.ndim >= 3 and w.shape[-2] == 1:
            assert steps == 1
            views.append((rows, 1, cols))
            specs.append(pl.BlockSpec((rows, 1, cols), lambda i: (0, 0, 0)))
        else:
            views.append((rows, cols))
            specs.append(pl.BlockSpec((rows // steps, cols), lambda i: (i, 0)))

    def body(*refs):
        for t in range(n):
            w_ref, g_ref, m_ref, v_ref = refs[4 * t:4 * t + 4]
            d_ref, nm_ref, nv_ref = refs[4 * n + 3 * t:4 * n + 3 * t + 3]
            gv = g_ref[...]
            mn = ADAM_B1 * m_ref[...] + (1.0 - ADAM_B1) * gv
            vn = ADAM_B2 * v_ref[...] + (1.0 - ADAM_B2) * (gv * gv)
            m_hat = mn / (1.0 - ADAM_B1 ** ADAM_STEP)
            v_hat = vn / (1.0 - ADAM_B2 ** ADAM_STEP)
            d_ref[...] = -ADAM_LR * (m_hat / (jnp.sqrt(v_hat) + ADAM_EPS) + ADAM_WD * w_ref[...])
            nm_ref[...] = mn
            nv_ref[...] = vn

    outs = pl.pallas_call(
        body, name=name, grid=(steps,),
        out_shape=[jax.ShapeDtypeStruct(view, F32) for view in views for _ in range(3)],
        in_specs=[spec for spec in specs for _ in range(4)], out_specs=[spec for spec in specs for _ in range(3)],
        compiler_params=_params("parallel"),
    )(*[a.reshape(view) for group, view in zip(tensors, views) for a in group])
    return [tuple(o.reshape(group[0].shape) for o in outs[3 * t:3 * t + 3]) for t, group in enumerate(tensors)]


SMALL_ROWS = 832


def _pack_small(g):
    parts = [g["loss"], g["pre1"].reshape(8, 128), g["post0"].reshape(8, 128),
             g["post1"].reshape(8, 128), g["sinks"], jnp.pad(g["pool_scale"].reshape(4, 128), ((0, 4), (0, 0))),
             g["pool_w"], g["dw"].reshape(248, 128), g["dwb"].reshape(8, 128), g["lng"].reshape(8, 128),
             g["lnb"].reshape(8, 128)]
    assert sum(p.shape[0] for p in parts) == SMALL_ROWS
    return jnp.concatenate(parts, axis=0)


def _unpack_small(s):
    out, r = {}, 0
    for key, rows, shape in (("loss", 8, (8, 128)), ("pre1", 8, (1, D_MODEL)), ("post", 16, (2, D_MODEL)),
                             ("sinks", 8, (8, 128)),
                             ("pool_scale", 4, (1, POOL_WIDTH)), ("pad", 4, (4, 128)), ("pool_w", 512, (1, 4, 128, 128)),
                             ("dw", 248, (CONV_K, D_MODEL)), ("dwb", 8, (1, D_MODEL)), ("lng", 8, (1, D_MODEL)),
                             ("lnb", 8, (1, D_MODEL))):
        out[key] = s[r:r + rows].reshape(shape)
        r += rows
    return out


def kernel(x, pre_norm, post_norm, a_w_in, a_sinks, b_pool_w, b_pool_scale, ab_w_out, c_w_in, c_dw_w, c_dw_b, c_ln_g, c_ln_b, c_w_out, loss_target, m_pre_norm, m_post_norm, m_a_w_in, m_a_sinks, m_b_pool_w, m_b_pool_scale, m_ab_w_out, m_c_w_in, m_c_dw_w, m_c_dw_b, m_c_ln_g, m_c_ln_b, m_c_w_out, v_pre_norm, v_post_norm, v_a_w_in, v_a_sinks, v_b_pool_w, v_b_pool_scale, v_ab_w_out, v_c_w_in, v_c_dw_w, v_c_dw_b, v_c_ln_g, v_c_ln_b, v_c_w_out):
    ix, iy = lax.axis_index("x"), lax.axis_index("y")
    chip_cols = (2 * ix + iy) * 256

    pad8 = lambda v: jnp.pad(v, ((0, -v.shape[0] % 8), (0, 0)))
    vec_shard = jnp.concatenate([pad8(c_dw_w.reshape(CONV_K, 256)), pad8(c_dw_b), pad8(c_ln_g), pad8(c_ln_b),
                                 jnp.zeros((8, 256), F32)], axis=0)
    x0, target = x[0], loss_target[0]
    pre0, pre1 = pre_norm[0:1], pre_norm[1:2]
    post0, post1 = post_norm[0:1], post_norm[1:2]
    pool_w = b_pool_w[0]

    (wa_t,) = _run_comm(_Gather([a_w_in[0].T.astype(BF16)], halve=True), "gather_a_w_in")
    wa_t = wa_t.reshape(EVEN_IN, D_MODEL)
    proj0, (w_ab,) = _norm_matmul(x0, pre0, wa_t, "proj0_fwd", comm=_Gather([ab_w_out[0].astype(BF16)], halve=True))
    w_ab = w_ab.reshape(D_MODEL, D_MODEL)
    (mix0, y0, x1), (wc_t, w_c, vecs) = _layer0_fwd(
        proj0, a_sinks, pool_w, b_pool_scale, w_ab, x0, post0,
        comm=_Gather([c_w_in[0].T.astype(BF16), c_w_out[0].astype(BF16), vec_shard], halve=True))
    wc_t = wc_t.reshape(3 * D_MODEL, D_MODEL)
    w_c = w_c.reshape(D_MODEL, D_MODEL)
    vecs = vecs.reshape(4, 64, 256).transpose(1, 0, 2).reshape(64, D_MODEL)
    dw, dwb, lng, lnb = vecs[0:CONV_K], vecs[32:33], vecs[40:41], vecs[48:49]
    proj1, _ = _norm_matmul(x1, pre1, wc_t, "proj1_fwd")
    z1, cf1, y1, g2, loss = _layer1_fwd(proj1, dw, dwb, lng, lnb, w_c, x1, post1, target)

    pieces = lambda m: m.reshape(N_DEV, m.shape[0] // N_DEV, D_MODEL)
    (dproj1, d_dw, d_dwb, d_lng, d_lnb, d_wc, d_post1), _ = _layer1_bwd(proj1, cf1, g2, y1, z1, w_c, post1, dw, lng, lnb)
    (g1, d_wct, d_pre1), _ = _pre_bwd(dproj1, wc_t, x1, pre1, g2, "proj1_bwd")
    (dproj0, d_sinks, d_pw, d_ps, d_wab, d_post0), (r_wc, r_wct) = _layer0_bwd(
        proj0, g1, y0, mix0, w_ab, post0, a_sinks, pool_w, b_pool_scale,
        comm=_Scatter([pieces(d_wc), pieces(d_wct)]))
    g = dict(loss=loss, pre1=d_pre1, post0=d_post0, post1=d_post1, sinks=d_sinks, pool_w=d_pw, pool_scale=d_ps,
             dw=d_dw, dwb=d_dwb, lng=d_lng, lnb=d_lnb)
    d_wat, (small8, r_wab) = _proj_dw(dproj0, x0, pre0, "proj0_dw",
                                      comm=_Comms(_Gather([_pack_small(g)], halve=False), _Scatter([pieces(d_wab)])))
    sent = _scatter_start(pieces(d_wat), "scatter_a_start")
    (gx, d_pre0), _ = _proj_dx(dproj0, wa_t, x0, pre0 + sent[4][0:1, 0:1], g1, "proj0_dx")
    own_wat, r_wat = _scatter_wait(*sent[:4], d_pre0, "scatter_a_wait")

    ic = lax.axis_index("c")
    me = 4 * ix + 2 * iy + ic
    place = jnp.stack([me, ic]).astype(jnp.int32)
    parts = [_direct_parts(own_wat, r_wat), _direct_parts(pieces(d_wab), r_wab), _direct_parts(pieces(d_wct), r_wct),
             _direct_parts(pieces(d_wc), r_wc)]
    halves = _piece_sums(parts, place, "grad_sums")
    *shared, pre0_8 = _share_with_sibling(halves, d_pre0.reshape(8, 128), "grad_share")
    g_wa_t, g_wab, g_wc_t, g_wc = [h.reshape(2 * h.shape[1], D_MODEL) for h in shared]
    g_c_w_in = g_wc_t.T[None]
    g_ab_w_out = g_wab[None]
    g_c_w_out = g_wc[None]

    small, pre0 = _sum8([small8, pre0_8], "small_sums")
    s = _unpack_small(small)
    layer = lax.broadcasted_iota(jnp.int32, (2, D_MODEL), 0)
    g_pre = jnp.where(layer == 0, pre0.reshape(1, D_MODEL), s["pre1"])
    g_post = s["post"]
    g_sinks = s["sinks"][:, 0].reshape(1, 8)
    g_pool_w, g_pool_scale = s["pool_w"], s["pool_scale"]
    g_dw = lax.dynamic_slice_in_dim(s["dw"], chip_cols, 256, axis=1).reshape(1, CONV_K, 1, 256)
    g_dwb = lax.dynamic_slice_in_dim(s["dwb"], chip_cols, 256, axis=1)
    g_lng = lax.dynamic_slice_in_dim(s["lng"], chip_cols, 256, axis=1)
    g_lnb = lax.dynamic_slice_in_dim(s["lnb"], chip_cols, 256, axis=1)

    turn = lambda a: jnp.swapaxes(a, 1, 2)
    a_w_in, m_a_w_in, v_a_w_in = turn(a_w_in), turn(m_a_w_in), turn(v_a_w_in)
    grads = [g_pre, g_post, g_wa_t[None], g_sinks, g_pool_w, g_pool_scale, g_ab_w_out, g_c_w_in, g_dw, g_dwb, g_lng,
             g_lnb, g_c_w_out]
    weights = [pre_norm, post_norm, a_w_in, a_sinks, b_pool_w, b_pool_scale, ab_w_out, c_w_in, c_dw_w, c_dw_b, c_ln_g,
               c_ln_b, c_w_out]
    moms = [m_pre_norm, m_post_norm, m_a_w_in, m_a_sinks, m_b_pool_w, m_b_pool_scale, m_ab_w_out, m_c_w_in, m_c_dw_w,
            m_c_dw_b, m_c_ln_g, m_c_ln_b, m_c_w_out]
    vars_ = [v_pre_norm, v_post_norm, v_a_w_in, v_a_sinks, v_b_pool_w, v_b_pool_scale, v_ab_w_out, v_c_w_in, v_c_dw_w,
             v_c_dw_b, v_c_ln_g, v_c_ln_b, v_c_w_out]
    tensors = list(zip(weights, grads, moms, vars_))
    matrices = (2, 6, 7, 12)
    others = [k for k in range(len(tensors)) if k not in matrices]
    updates = dict(zip(matrices, _adamw([tensors[k] for k in matrices], 4, "adamw_matrices")))
    updates.update(zip(others, _adamw([tensors[k] for k in others], 1, "adamw_small")))
    deltas, new_m, new_v = ([updates[k][j] for k in range(len(tensors))] for j in range(3))
    for outs in (grads, deltas, new_m, new_v):
        outs[2] = turn(outs[2])
    return (s["loss"][0, 0], gx[None], *grads, *deltas, *new_m, *new_v)
```

```python
import jax
import jax.numpy as jnp
from jax import lax
from jax.experimental import pallas as pl
from jax.experimental.pallas import tpu as pltpu

F32 = jnp.float32
BF16 = jnp.bfloat16

D_MODEL = 1024
EPS = 1e-6
NEG = -1e30
HEAD_DIM = 64
GROUP = 4
KV_HEADS = 2
BLOCK = 128
EVEN_IN = 2304
ATTN_WIDTH = 512
POOL_WIDTH = 512
COL_Q, COL_K, COL_GA, COL_U, COL_GB = 0, 512, 768, 1280, 1792
POOL_GROUPS = 4
POOL_GC = 128
POOL_HALO = 16
CONV_K = 31
CONV_HALO = 32
GLU_A = slice(0, D_MODEL)
GLU_B = slice(D_MODEL, 2 * D_MODEL)
GATE = slice(2 * D_MODEL, 3 * D_MODEL)
N_DEV = 8

ADAM_LR = 0.001
ADAM_B1 = 0.9
ADAM_B2 = 0.999
ADAM_EPS = 1e-08
ADAM_WD = 0.01
ADAM_STEP = 10

VMEM_LIMIT_BYTES = 56 * 1024 * 1024

NT = (((1,), (1,)), ((), ()))
TN = (((0,), (0,)), ((), ()))
MESH_ID = pl.DeviceIdType.MESH


def _params(*sem):
    return pltpu.CompilerParams(dimension_semantics=sem, vmem_limit_bytes=VMEM_LIMIT_BYTES)


def _const_spec(shape):
    nd = len(shape)
    return pl.BlockSpec(shape, lambda *_: (0,) * nd, pipeline_mode=pl.Buffered(1))


def _sigmoid(v):
    return 0.5 * jnp.tanh(0.5 * v) + 0.5


def _silu(v):
    h = 0.5 * v
    return h * jnp.tanh(h) + h


def _silu_and_grad(v):
    s = _sigmoid(v)
    silu = v * s
    return silu, s + silu * (1.0 - s)


ANY = pl.BlockSpec(memory_space=pl.ANY)


def _place():
    x, y, c = lax.axis_index("x"), lax.axis_index("y"), lax.axis_index("c")
    chips = [(1 - x, y), (x, 1 - y), (1 - x, 1 - y)]
    return x, y, c, chips


class _Gather:
    def __init__(self, blocks, halve):
        self.ins = list(blocks)
        self.halve = halve
        self.n = n = len(blocks)
        self.shapes = [((b.shape[0] // 2) if halve else b.shape[0], b.shape[1]) for b in blocks]
        self.out_shape = [jax.ShapeDtypeStruct((N_DEV, r, cols), b.dtype) for (r, cols), b in zip(self.shapes, blocks)]
        self.scratch = [pltpu.SemaphoreType.DMA((7 * n,)), pltpu.SemaphoreType.DMA((7 * n,)),
                        pltpu.SemaphoreType.DMA((n,))]

    def _copies(self, ins, outs, sems):
        send_sems, recv_sems, local_sems = sems
        x, y, c, chips = _place()
        me, sibling = (x, y, c), (x, y, 1 - c)

        def piece(t, px, py, pc):
            return outs[t].at[4 * px + 2 * py + pc]

        def own(t):
            return ins[t].at[pl.ds(c * self.shapes[t][0], self.shapes[t][0])] if self.halve else ins[t]

        def copy(t, k, block, to, src=None):
            return pltpu.make_async_remote_copy(
                src_ref=piece(t, *block) if src is None else src, dst_ref=piece(t, *block),
                send_sem=send_sems.at[7 * t + k], recv_sem=recv_sems.at[7 * t + k],
                device_id=to, device_id_type=MESH_ID)

        rng = range(self.n)
        return dict(
            mine=[pltpu.make_async_copy(own(t), piece(t, *me), local_sems.at[t]) for t in rng],
            first=[copy(t, 0, me, sibling, src=own(t)) for t in rng]
            + [copy(t, 1 + j, me, (*chip, c), src=own(t)) for t in rng for j, chip in enumerate(chips)],
            landed=[copy(t, 1 + j, (*chip, c), me) for j, chip in enumerate(chips) for t in rng],
            passed=[copy(t, 4 + j, (*chip, c), sibling) for j, chip in enumerate(chips) for t in rng],
            from_sibling=[copy(t, 0, sibling, me) for t in rng]
            + [copy(t, 4 + j, (*chip, 1 - c), me) for t in rng for j, chip in enumerate(chips)])

    def start(self, ins, outs, sems):
        d = self._copies(ins, outs, sems)
        for cp in d["mine"] + d["first"]:
            cp.start()

    def middle(self, ins, outs, sems):
        d = self._copies(ins, outs, sems)
        for got, fwd in zip(d["landed"], d["passed"]):
            got.wait_recv()
            fwd.start()

    def finish(self, ins, outs, sems):
        d = self._copies(ins, outs, sems)
        for cp in d["from_sibling"]:
            cp.wait_recv()
        for cp in d["first"] + d["passed"]:
            cp.wait_send()
        for cp in d["mine"]:
            cp.wait()


class _Scatter:
    def __init__(self, tensors):
        self.ins = list(tensors)
        self.n = n = len(tensors)
        self.out_shape = [jax.ShapeDtypeStruct(t.shape, t.dtype) for t in tensors]
        self.scratch = [pltpu.SemaphoreType.DMA((7 * n,)), pltpu.SemaphoreType.DMA((7 * n,))]

    def _copies(self, ins, outs, sems):
        send_sems, recv_sems = sems
        x, y, c, _ = _place()
        me = 4 * x + 2 * y + c
        sends, recvs = [], []
        for t in range(self.n):
            for m in range(1, N_DEV):
                px, py, pc = x ^ (m >> 2), y ^ ((m >> 1) & 1), c ^ (m & 1)
                q = 4 * px + 2 * py + pc
                sems_k = dict(send_sem=send_sems.at[7 * t + m - 1], recv_sem=recv_sems.at[7 * t + m - 1],
                              device_id=(px, py, pc), device_id_type=MESH_ID)
                sends.append(pltpu.make_async_remote_copy(src_ref=ins[t].at[q], dst_ref=outs[t].at[me], **sems_k))
                recvs.append(pltpu.make_async_remote_copy(src_ref=ins[t].at[me], dst_ref=outs[t].at[q], **sems_k))
        return sends, recvs

    def start(self, ins, outs, sems):
        for cp in self._copies(ins, outs, sems)[0]:
            cp.start()

    def middle(self, ins, outs, sems):
        pass

    def finish(self, ins, outs, sems):
        sends, recvs = self._copies(ins, outs, sems)
        for cp in recvs:
            cp.wait_recv()
        for cp in sends:
            cp.wait_send()


class _Comms:
    def __init__(self, *comms):
        self.comms = comms
        self.ins = [a for c in comms for a in c.ins]
        self.out_shape = [s for c in comms for s in c.out_shape]
        self.scratch = [s for c in comms for s in c.scratch]

    def _each(self, phase, ins, outs, sems):
        i = o = s = 0
        for c in self.comms:
            ni, no, ns = len(c.ins), len(c.out_shape), len(c.scratch)
            getattr(c, phase)(ins[i:i + ni], outs[o:o + no], sems[s:s + ns])
            i, o, s = i + ni, o + no, s + ns

    def start(self, ins, outs, sems):
        self._each("start", ins, outs, sems)

    def middle(self, ins, outs, sems):
        self._each("middle", ins, outs, sems)

    def finish(self, ins, outs, sems):
        self._each("finish", ins, outs, sems)


def _run_comm(comm, name):
    n = len(comm.ins)

    def body(*refs):
        parts = refs[:n], refs[n:2 * n], refs[2 * n:]
        comm.start(*parts)
        comm.middle(*parts)
        comm.finish(*parts)

    return pl.pallas_call(body, name=name, out_shape=comm.out_shape, in_specs=[ANY] * n, out_specs=[ANY] * n,
                          scratch_shapes=comm.scratch)(*comm.ins)


HBM_SPEC = pl.BlockSpec(memory_space=pltpu.HBM)
SEM_SPEC = pl.BlockSpec(memory_space=pltpu.SEMAPHORE)
DATAFLOW = pltpu.SideEffectType.DATAFLOW_SIDE_EFFECTING


def _scatter_copies(own_ref, land_ref, send_sems, recv_sems):
    x, y, c, _ = _place()
    me = 4 * x + 2 * y + c
    pairs = []
    for m in range(1, N_DEV):
        px, py, pc = x ^ (m >> 2), y ^ ((m >> 1) & 1), c ^ (m & 1)
        q = 4 * px + 2 * py + pc
        sems = dict(send_sem=send_sems.at[m - 1], recv_sem=recv_sems.at[m - 1], device_id=(px, py, pc),
                    device_id_type=MESH_ID)
        pairs.append((pltpu.make_async_remote_copy(src_ref=own_ref.at[q], dst_ref=land_ref.at[me], **sems),
                      pltpu.make_async_remote_copy(src_ref=own_ref.at[me], dst_ref=land_ref.at[q], **sems)))
    return pairs


def _scatter_start(own, name):
    def body(own_ref, land_ref, send_sems, recv_sems, own_thru, land_thru, token):
        for send, _ in _scatter_copies(own_ref, land_ref, send_sems, recv_sems):
            send.start()
        token[...] = jnp.zeros_like(token)

    buf = pltpu.HBM(own.shape, own.dtype)
    return pl.pallas_call(
        body, name=name,
        out_shape=(pltpu.SemaphoreType.DMA((N_DEV - 1,)), pltpu.SemaphoreType.DMA((N_DEV - 1,)), buf, buf,
                   jax.ShapeDtypeStruct((8, 128), F32)),
        in_specs=(HBM_SPEC, HBM_SPEC),
        out_specs=(SEM_SPEC, SEM_SPEC, HBM_SPEC, HBM_SPEC, pl.BlockSpec(memory_space=pltpu.VMEM)),
        input_output_aliases={0: 2, 1: 3},
        compiler_params=pltpu.CompilerParams(has_side_effects=DATAFLOW),
    )(pltpu.with_memory_space_constraint(own, pltpu.HBM),
      pltpu.with_memory_space_constraint(lax.empty(own.shape, own.dtype), pltpu.HBM))


def _scatter_wait(send_sems, recv_sems, own_thru, land_thru, after, name):
    def body(own_ref, land_ref, send_sems, recv_sems, after_ref, own_out, land_out):
        for send, recv in _scatter_copies(own_ref, land_ref, send_sems, recv_sems):
            send.wait_send()
            recv.wait_recv()

    buf = pltpu.HBM(own_thru.shape, own_thru.dtype)
    return pl.pallas_call(
        body, name=name, out_shape=(buf, buf),
        in_specs=(HBM_SPEC, HBM_SPEC, SEM_SPEC, SEM_SPEC, ANY), out_specs=(HBM_SPEC, HBM_SPEC),
        input_output_aliases={0: 0, 1: 1},
        compiler_params=pltpu.CompilerParams(has_side_effects=DATAFLOW),
    )(own_thru, land_thru, send_sems, recv_sems, after)


def _fused_call(body, comm, args, *, name, grid, out_shape, in_specs, out_specs, scratch_shapes=(), params):
    single = not isinstance(out_shape, (list, tuple))
    out_shape = [out_shape] if single else list(out_shape)
    out_specs = [out_specs] if single else list(out_specs)
    if comm is None:
        res = pl.pallas_call(body, name=name, grid=grid, out_shape=out_shape, in_specs=in_specs, out_specs=out_specs,
                             scratch_shapes=list(scratch_shapes), compiler_params=params)(*args)
        return (res[0] if single else res), []
    n_in, n_out, n_scr = len(in_specs), len(out_shape), len(scratch_shapes)
    c_in, c_out = len(comm.ins), len(comm.out_shape)
    steps = grid[0]

    def fused(*refs):
        pos = 0
        groups = []
        for size in (n_in, c_in, n_out, c_out, n_scr, len(comm.scratch)):
            groups.append(refs[pos:pos + size])
            pos += size
        ins, c_ins, outs, c_outs, scr, c_sems = groups
        i = pl.program_id(0)

        @pl.when(i == 0)
        def _():
            comm.start(c_ins, c_outs, c_sems)

        @pl.when(i == steps // 2)
        def _():
            comm.middle(c_ins, c_outs, c_sems)

        body(*ins, *outs, *scr)

        @pl.when(i == steps - 1)
        def _():
            comm.finish(c_ins, c_outs, c_sems)

    res = pl.pallas_call(
        fused, name=name, grid=grid, out_shape=out_shape + list(comm.out_shape),
        in_specs=list(in_specs) + [ANY] * c_in, out_specs=out_specs + [ANY] * c_out,
        scratch_shapes=list(scratch_shapes) + list(comm.scratch), compiler_params=params)(*args, *comm.ins)
    main = res[:n_out]
    return (main[0] if single else main), list(res[n_out:])


def _norm_matmul(x, gain, wt, name, comm=None, tm=1024):
    t, n = x.shape[0], wt.shape[0]

    def body(x_ref, g_ref, wt_ref, o_ref):
        xv = x_ref[...]
        r = lax.rsqrt(jnp.mean(xv * xv, axis=-1, keepdims=True) + EPS)
        h = (xv * r * g_ref[...]).astype(BF16)
        o_ref[...] = lax.dot_general(h, wt_ref[...], NT, preferred_element_type=F32)

    return _fused_call(
        body, comm, (x, gain, wt), name=name, grid=(t // tm,),
        out_shape=jax.ShapeDtypeStruct((t, n), F32),
        in_specs=[pl.BlockSpec((tm, D_MODEL), lambda i: (i, 0)), _const_spec((1, D_MODEL)), _const_spec((n, D_MODEL))],
        out_specs=pl.BlockSpec((tm, n), lambda i: (i, 0)),
        params=_params("arbitrary"))


def _project_out(a, w_ref, x_ref, p_ref, y_ref):
    y = jnp.dot(a, w_ref[...], preferred_element_type=F32)
    y_ref[...] = y
    ry = lax.rsqrt(jnp.mean(y * y, axis=-1, keepdims=True) + EPS)
    return x_ref[...] + (y * ry) * p_ref[...]


def _post_bwd_rows(g, y, a, n_own, first, last, p_ref, w_ref, dw_ref, dw16_ref, dp_ref):
    @pl.when(first)
    def _():
        dw_ref[...] = jnp.zeros_like(dw_ref)
        dp_ref[...] = jnp.zeros_like(dp_ref)

    ry = lax.rsqrt(jnp.mean(y * y, axis=-1, keepdims=True) + EPS)
    nv = y * ry
    dp_ref[...] += jnp.sum((g * nv)[0:n_own], axis=0, keepdims=True)
    dn = g * p_ref[...]
    dy = (ry * (dn - nv * jnp.mean(dn * nv, axis=-1, keepdims=True))).astype(BF16)
    dw_ref[...] += lax.dot_general(a, dy[0:n_own], TN, preferred_element_type=F32)

    @pl.when(last)
    def _():
        dw16_ref[...] = dw_ref[...].astype(BF16)

    return lax.dot_general(dy, w_ref[...], NT, preferred_element_type=F32)


def _pre_bwd(dproj, wt, x_in, pre, g, name, comm=None, tm=512):
    t, n = dproj.shape
    steps = t // tm

    def body(dp_ref, wt_ref, x_ref, pre_ref, g_ref, dx_ref, dwt16_ref, dpre_ref, dwt_ref):
        @pl.when(pl.program_id(0) == 0)
        def _():
            dwt_ref[...] = jnp.zeros_like(dwt_ref)
            dpre_ref[...] = jnp.zeros_like(dpre_ref)

        dpv = dp_ref[...]
        dh = jnp.dot(dpv, wt_ref[...], preferred_element_type=F32)
        xv = x_ref[...]
        r = lax.rsqrt(jnp.mean(xv * xv, axis=-1, keepdims=True) + EPS)
        xn = xv * r
        pv = pre_ref[...]
        dpre_ref[...] += jnp.sum(dh * xn, axis=0, keepdims=True)
        dxn = dh * pv
        dx_ref[...] = g_ref[...] + r * (dxn - xn * jnp.mean(dxn * xn, axis=-1, keepdims=True))
        h = (xn * pv).astype(BF16)
        dwt_ref[...] += lax.dot_general(dpv, h, TN, preferred_element_type=F32)

        @pl.when(pl.program_id(0) == steps - 1)
        def _():
            dwt16_ref[...] = dwt_ref[...].astype(BF16)

    row = pl.BlockSpec((tm, D_MODEL), lambda i: (i, 0))
    return _fused_call(
        body, comm, (dproj, wt, x_in, pre, g), name=name, grid=(steps,),
        out_shape=[jax.ShapeDtypeStruct((t, D_MODEL), F32), jax.ShapeDtypeStruct((n, D_MODEL), BF16),
                   jax.ShapeDtypeStruct((1, D_MODEL), F32)],
        in_specs=[pl.BlockSpec((tm, n), lambda i: (i, 0)), _const_spec((n, D_MODEL)), row, _const_spec((1, D_MODEL)), row],
        out_specs=[row, _const_spec((n, D_MODEL)), pl.BlockSpec((1, D_MODEL), lambda i: (0, 0))],
        scratch_shapes=[pltpu.VMEM((n, D_MODEL), F32)],
        params=_params("arbitrary"))


def _proj_dw(dproj, x_in, pre, name, comm=None, tm=1024):
    t, n = dproj.shape
    steps = t // tm

    def body(dp_ref, x_ref, pre_ref, dwt16_ref, dwt_ref):
        @pl.when(pl.program_id(0) == 0)
        def _():
            dwt_ref[...] = jnp.zeros_like(dwt_ref)

        xv = x_ref[...]
        r = lax.rsqrt(jnp.mean(xv * xv, axis=-1, keepdims=True) + EPS)
        h = (xv * r * pre_ref[...]).astype(BF16)
        dwt_ref[...] += lax.dot_general(dp_ref[...], h, TN, preferred_element_type=F32)

        @pl.when(pl.program_id(0) == steps - 1)
        def _():
            dwt16_ref[...] = dwt_ref[...].astype(BF16)

    return _fused_call(
        body, comm, (dproj, x_in, pre), name=name, grid=(steps,),
        out_shape=jax.ShapeDtypeStruct((n, D_MODEL), BF16),
        in_specs=[pl.BlockSpec((tm, n), lambda i: (i, 0)), pl.BlockSpec((tm, D_MODEL), lambda i: (i, 0)),
                  _const_spec((1, D_MODEL))],
        out_specs=pl.BlockSpec((n, D_MODEL), lambda i: (0, 0)),
        scratch_shapes=[pltpu.VMEM((n, D_MODEL), F32)],
        params=_params("arbitrary"))


def _proj_dx(dproj, wt, x_in, pre, g, name, comm=None, tm=1024):
    t, n = dproj.shape

    def body(dp_ref, wt_ref, x_ref, pre_ref, g_ref, dx_ref, dpre_ref):
        @pl.when(pl.program_id(0) == 0)
        def _():
            dpre_ref[...] = jnp.zeros_like(dpre_ref)

        dh = jnp.dot(dp_ref[...], wt_ref[...], preferred_element_type=F32)
        xv = x_ref[...]
        r = lax.rsqrt(jnp.mean(xv * xv, axis=-1, keepdims=True) + EPS)
        xn = xv * r
        dpre_ref[...] += jnp.sum(dh * xn, axis=0, keepdims=True)
        dxn = dh * pre_ref[...]
        dx_ref[...] = g_ref[...] + r * (dxn - xn * jnp.mean(dxn * xn, axis=-1, keepdims=True))

    row = pl.BlockSpec((tm, D_MODEL), lambda i: (i, 0))
    return _fused_call(
        body, comm, (dproj, wt, x_in, pre, g), name=name, grid=(t // tm,),
        out_shape=[jax.ShapeDtypeStruct((t, D_MODEL), F32), jax.ShapeDtypeStruct((1, D_MODEL), F32)],
        in_specs=[pl.BlockSpec((tm, n), lambda i: (i, 0)), _const_spec((n, D_MODEL)), row, _const_spec((1, D_MODEL)), row],
        out_specs=[row, pl.BlockSpec((1, D_MODEL), lambda i: (0, 0))],
        params=_params("arbitrary"))


def _group_masks():
    lane = lax.broadcasted_iota(jnp.int32, (1, GROUP * HEAD_DIM), 1)
    return [lane // HEAD_DIM == g for g in range(GROUP)]


def _stack_groups(v, masks, scale=1.0):
    return jnp.concatenate([v * jnp.where(m, scale, 0.0) for m in masks], axis=0)


def _unstack_groups(v, masks):
    out = v[(GROUP - 1) * BLOCK:GROUP * BLOCK]
    for g in range(GROUP - 2, -1, -1):
        out = jnp.where(masks[g], v[g * BLOCK:(g + 1) * BLOCK], out)
    return out


def _repeat_head(kv2, kvh):
    first = lax.broadcasted_iota(jnp.int32, kv2.shape, 1) < HEAD_DIM
    rolled = pltpu.roll(kv2, HEAD_DIM, 1)
    one = jnp.where(first, kv2, rolled) if kvh == 0 else jnp.where(first, rolled, kv2)
    return jnp.concatenate([one, one], axis=1)


def _fold_head(v4):
    a = v4[:, 0:128] + v4[:, 128:256]
    return a + pltpu.roll(a, HEAD_DIM, 1)


ATTN_CONSTS = [pltpu.VMEM((KV_HEADS, GROUP * BLOCK, 2 * BLOCK), F32)]


def _fill_attn_bias(bias_ref):
    row = lax.broadcasted_iota(jnp.int32, (GROUP * BLOCK, 2 * BLOCK), 0)
    col = lax.broadcasted_iota(jnp.int32, (GROUP * BLOCK, 2 * BLOCK), 1)
    dist = (row % BLOCK) + BLOCK - col
    band = (dist >= 0) & (dist < BLOCK)
    rb = lax.broadcasted_iota(jnp.int32, (GROUP * BLOCK, 1), 0) // BLOCK
    for kvh in range(KV_HEADS):
        slope = jnp.zeros((GROUP * BLOCK, 1), F32)
        for g in range(GROUP):
            slope = jnp.where(rb == g, 2.0 ** (-(kvh * GROUP + g + 1)), slope)
        bias_ref[kvh] = jnp.where(band, -slope * dist.astype(F32), NEG)


def _row_sinks(kvh, sink_ref):
    rb = lax.broadcasted_iota(jnp.int32, (GROUP * BLOCK, 1), 0) // BLOCK
    sink = jnp.zeros((GROUP * BLOCK, 1), F32)
    for g in range(GROUP):
        sink = jnp.where(rb == g, sink_ref[0, kvh * GROUP + g], sink)
    return sink


def _attn_probs(qk, k4, bias, sink, no_past, masks):
    qs = _stack_groups(qk, masks, HEAD_DIM ** -0.5).astype(BF16)
    s = lax.dot_general(qs, k4, NT, preferred_element_type=F32) + bias
    s = jnp.concatenate([jnp.where(no_past, NEG, s[:, 0:BLOCK]), s[:, BLOCK:]], axis=1)
    mx = jnp.maximum(jnp.max(s, axis=-1, keepdims=True), sink)
    e = jnp.exp(s - mx)
    es = jnp.exp(sink - mx)
    inv = 1.0 / (jnp.sum(e, axis=-1, keepdims=True) + es)
    return qs, e * inv, es * inv


def _pool_forward(u_ext, g, t0):
    n = u_ext.shape[0] - POOL_HALO
    s = u_ext
    for step in range(g + 1):
        s = s + pltpu.roll(s, 1 << step, 0)
    w = 2 << g
    t = t0 + lax.broadcasted_iota(jnp.int32, (n, 1), 0)
    cnt = jnp.minimum(t + 1, w).astype(F32)
    return s[POOL_HALO:] / cnt - u_ext[POOL_HALO:]


def _layer0_fwd(proj, sinks, pool_w, pool_scale, w_out, x_in, post, comm=None, tq=1024):
    t = proj.shape[0]
    nblk = tq // BLOCK

    def body(main_ref, halo_ref, sink_ref, pw_ref, ps_ref, w_ref, x_ref, p_ref, o_ref, y_ref, xo_ref, kv_ref, bias_ref):
        i = pl.program_id(0)
        t0 = i * tq
        masks = _group_masks()

        @pl.when(i == 0)
        def _():
            _fill_attn_bias(bias_ref)

        kv_ref[0:BLOCK, :] = halo_ref[:, COL_K:COL_K + 256]
        kv_ref[BLOCK:, :] = main_ref[:, COL_K:COL_K + 256]

        def block(jb, carry):
            r0 = pl.multiple_of(jb * BLOCK, BLOCK)
            no_past = t0 + r0 == 0
            q = main_ref[pl.ds(r0, BLOCK), COL_Q:COL_Q + ATTN_WIDTH]
            ga = main_ref[pl.ds(r0, BLOCK), COL_GA:COL_GA + ATTN_WIDTH]
            kk = kv_ref[pl.ds(r0, 2 * BLOCK), 0:128]
            vv = kv_ref[pl.ds(r0, 2 * BLOCK), 128:256]
            outs = []
            for kvh in range(KV_HEADS):
                k4 = _repeat_head(kk, kvh).astype(BF16)
                v4 = _repeat_head(vv, kvh).astype(BF16)
                _, p, _ = _attn_probs(q[:, kvh * 256:(kvh + 1) * 256], k4, bias_ref[kvh], _row_sinks(kvh, sink_ref), no_past, masks)
                pv = jnp.dot(p.astype(BF16), v4, preferred_element_type=F32)
                outs.append(_unstack_groups(pv, masks))
            attn = jnp.concatenate(outs, axis=1)
            o_ref[pl.ds(r0, BLOCK), 0:ATTN_WIDTH] = (attn * _silu(ga)).astype(BF16)
            return carry

        lax.fori_loop(0, nblk, block, 0, unroll=True)

        for g in range(POOL_GROUPS):
            cu = COL_U + g * POOL_GC
            cg = COL_GB + g * POOL_GC
            halo_u = jnp.where(i == 0, 0.0, halo_ref[BLOCK - POOL_HALO:BLOCK, cu:cu + POOL_GC])
            u_ext = jnp.concatenate([halo_u, main_ref[:, cu:cu + POOL_GC]], axis=0)
            pooled = _pool_forward(u_ext, g, t0)
            y = jnp.dot(pooled.astype(BF16), pw_ref[g].astype(BF16), preferred_element_type=F32)
            y = y * ps_ref[:, g * POOL_GC:(g + 1) * POOL_GC]
            o_ref[:, ATTN_WIDTH + g * POOL_GC:ATTN_WIDTH + (g + 1) * POOL_GC] =(y * _silu(main_ref[:, cg:cg + POOL_GC])).astype(BF16)

        xo_ref[...] = _project_out(o_ref[...], w_ref, x_ref, p_ref, y_ref)

    row = pl.BlockSpec((tq, D_MODEL), lambda i: (i, 0))
    return _fused_call(
        body, comm, (proj, proj, sinks, pool_w, pool_scale, w_out, x_in, post), name="layer0_fwd", grid=(t // tq,),
        out_shape=[jax.ShapeDtypeStruct((t, D_MODEL), BF16), jax.ShapeDtypeStruct((t, D_MODEL), F32),
                   jax.ShapeDtypeStruct((t, D_MODEL), F32)],
        in_specs=[pl.BlockSpec((tq, EVEN_IN), lambda i: (i, 0)),
                  pl.BlockSpec((BLOCK, EVEN_IN), lambda i: (jnp.maximum(i * nblk - 1, 0), 0)),
                  pl.BlockSpec(memory_space=pltpu.SMEM),
                  _const_spec((POOL_GROUPS, POOL_GC, POOL_GC)), _const_spec((1, POOL_WIDTH)),
                  _const_spec((D_MODEL, D_MODEL)), row, _const_spec((1, D_MODEL))],
        out_specs=[row, row, row],
        scratch_shapes=[pltpu.VMEM((tq + BLOCK, 256), F32)] + ATTN_CONSTS,
        params=_params("arbitrary"))


def _layer0_bwd(proj, gy, y, mix, w_out, post, sinks, pool_w, pool_scale, comm=None, tq=512):
    t = proj.shape[0]
    nt = t // tq
    nblk = tq // BLOCK

    def body(main_ref, halo_ref, next_ref, gy_ref, gyn_ref, y_ref, yn_ref, mix_ref, wo_ref, po_ref,
             sink_ref, pw_ref, ps_ref,
             o_ref, dsk_ref, dpw_ref, dps_ref, dwo16_ref, dpo_ref,
             kv_ref, dkv_ref, carry_ref, bias_ref, dwo_ref, dmix_ref):
        i = pl.program_id(0)
        ii = nt - 1 - i
        t0 = ii * tq
        masks = _group_masks()

        @pl.when(i == 0)
        def _():
            _fill_attn_bias(bias_ref)
            dsk_ref[...] = jnp.zeros_like(dsk_ref)
            dpw_ref[...] = jnp.zeros_like(dpw_ref)
            dps_ref[...] = jnp.zeros_like(dps_ref)
            carry_ref[...] = jnp.zeros_like(carry_ref)

        dmix_ref[...] = _post_bwd_rows(jnp.concatenate([gy_ref[...], gyn_ref[...]], axis=0),
                                       jnp.concatenate([y_ref[...], yn_ref[...]], axis=0), mix_ref[...], tq,
                                       i == 0, i == nt - 1, po_ref, wo_ref, dwo_ref, dwo16_ref, dpo_ref)
        dm_ref = dmix_ref.at[pl.ds(0, tq)]
        dmn_ref = dmix_ref.at[pl.ds(tq, POOL_HALO)]

        kv_ref[0:BLOCK, :] = halo_ref[:, COL_K:COL_K + 256]
        kv_ref[BLOCK:, :] = main_ref[:, COL_K:COL_K + 256]
        dkv_ref[0:tq, :] = jnp.zeros((tq, 256), F32)
        dkv_ref[tq:, :] = carry_ref[...]

        def block(jb, carry):
            r0 = pl.multiple_of(jb * BLOCK, BLOCK)
            no_past = t0 + r0 == 0
            q = main_ref[pl.ds(r0, BLOCK), COL_Q:COL_Q + ATTN_WIDTH]
            ga = main_ref[pl.ds(r0, BLOCK), COL_GA:COL_GA + ATTN_WIDTH]
            dya = dm_ref[pl.ds(r0, BLOCK), 0:ATTN_WIDTH]
            kk = kv_ref[pl.ds(r0, 2 * BLOCK), 0:128]
            vv = kv_ref[pl.ds(r0, 2 * BLOCK), 128:256]
            silu_ga, dsilu_ga = _silu_and_grad(ga)
            do = dya * silu_ga
            first = lax.broadcasted_iota(jnp.int32, (2 * BLOCK, 128), 1) < HEAD_DIM
            attn, dq, dk, dv = [], [], [], []
            for kvh in range(KV_HEADS):
                k4 = _repeat_head(kk, kvh).astype(BF16)
                v4 = _repeat_head(vv, kvh).astype(BF16)
                qs, p, ps = _attn_probs(q[:, kvh * 256:(kvh + 1) * 256], k4, bias_ref[kvh], _row_sinks(kvh, sink_ref), no_past, masks)
                pb = p.astype(BF16)
                o_k = _unstack_groups(jnp.dot(pb, v4, preferred_element_type=F32), masks)
                do_k = do[:, kvh * 256:(kvh + 1) * 256]
                dos = _stack_groups(do_k, masks).astype(BF16)
                prod = do_k * o_k
                delta = jnp.concatenate([jnp.sum(jnp.where(m, prod, 0.0), axis=-1, keepdims=True) for m in masks], axis=0)
                dp = lax.dot_general(dos, v4, NT, preferred_element_type=F32)
                ds = (p * (dp - delta)).astype(BF16)
                sink_term = ps * delta
                for g in range(GROUP):
                    h = kvh * GROUP + g
                    dsk_ref[h:h + 1, :] -= jnp.sum(sink_term[g * BLOCK:(g + 1) * BLOCK], keepdims=True)
                dq.append(_unstack_groups(jnp.dot(ds, k4, preferred_element_type=F32), masks) * (HEAD_DIM ** -0.5))
                dk.append(_fold_head(lax.dot_general(ds, qs, TN, preferred_element_type=F32)))
                dv.append(_fold_head(lax.dot_general(pb, dos, TN, preferred_element_type=F32)))
                attn.append(o_k)
            o_ref[pl.ds(r0, BLOCK), COL_Q:COL_Q + ATTN_WIDTH] = jnp.concatenate(dq, axis=1).astype(BF16)
            o_all = jnp.concatenate(attn, axis=1)
            o_ref[pl.ds(r0, BLOCK), COL_GA:COL_GA + ATTN_WIDTH] = (dya * o_all * dsilu_ga).astype(BF16)
            dkv = jnp.concatenate([jnp.where(first, dk[0], dk[1]), jnp.where(first, dv[0], dv[1])], axis=1)
            dkv_ref[pl.ds(r0, 2 * BLOCK), :] += dkv
            return carry

        lax.fori_loop(0, nblk, block, 0, unroll=True)
        carry_ref[...] = dkv_ref[0:BLOCK, :]
        o_ref[:, COL_K:COL_K + 256] = dkv_ref[BLOCK:, :].astype(BF16)

        last = ii == nt - 1
        for g in range(POOL_GROUPS):
            cu = COL_U + g * POOL_GC
            cg = COL_GB + g * POOL_GC
            cm = ATTN_WIDTH + g * POOL_GC
            pw = pw_ref[g].astype(BF16)
            sc = ps_ref[:, g * POOL_GC:(g + 1) * POOL_GC]
            halo_u = jnp.where(ii == 0, 0.0, halo_ref[BLOCK - POOL_HALO:BLOCK, cu:cu + POOL_GC])
            u_ext = jnp.concatenate([halo_u, main_ref[:, cu:cu + POOL_GC]], axis=0)
            pooled = _pool_forward(u_ext, g, t0).astype(BF16)
            y_raw = jnp.dot(pooled, pw, preferred_element_type=F32)
            gb = main_ref[:, cg:cg + POOL_GC]
            dyb = dm_ref[:, cm:cm + POOL_GC]
            silu_gb, dsilu_gb = _silu_and_grad(gb)
            dypool = dyb * silu_gb
            dps_ref[:, g * POOL_GC:(g + 1) * POOL_GC] += jnp.sum(dypool * y_raw, axis=0, keepdims=True)
            o_ref[:, cg:cg + POOL_GC] = (dyb * (y_raw * sc) * dsilu_gb).astype(BF16)
            dyraw = dypool * sc
            dyraw_n = jnp.where(last, 0.0, dmn_ref[:, cm:cm + POOL_GC] * _silu(next_ref[:, cg:cg + POOL_GC]) * sc)
            dpw_ref[g * POOL_GC:(g + 1) * POOL_GC, :] += lax.dot_general(pooled, dyraw.astype(BF16), TN,
                                                                         preferred_element_type=F32)
            dyraw_ext = jnp.concatenate([dyraw, dyraw_n], axis=0).astype(BF16)
            dpooled = lax.dot_general(dyraw_ext, pw, NT, preferred_element_type=F32)
            w = 2 << g
            tt = t0 + lax.broadcasted_iota(jnp.int32, (tq + POOL_HALO, 1), 0)
            s = dpooled / jnp.minimum(tt + 1, w).astype(F32)
            for step in range(g + 1):
                s = s + pltpu.roll(s, tq + POOL_HALO - (1 << step), 0)
            o_ref[:, cu:cu + POOL_GC] = (s[0:tq] - dpooled[0:tq]).astype(BF16)

    rev = lambda i: nt - 1 - i
    nxt = lambda i: (jnp.minimum((rev(i) + 1) * (tq // POOL_HALO), t // POOL_HALO - 1), 0)
    row = pl.BlockSpec((tq, D_MODEL), lambda i: (rev(i), 0))
    nxt_row = pl.BlockSpec((POOL_HALO, D_MODEL), nxt)
    square = _const_spec((D_MODEL, D_MODEL))
    return _fused_call(
        body, comm, (proj, proj, proj, gy, gy, y, y, mix, w_out, post, sinks, pool_w, pool_scale),
        name="layer0_bwd", grid=(nt,),
        out_shape=[jax.ShapeDtypeStruct((t, EVEN_IN), BF16), jax.ShapeDtypeStruct((8, 128), F32),
                   jax.ShapeDtypeStruct((POOL_GROUPS * POOL_GC, POOL_GC), F32), jax.ShapeDtypeStruct((1, POOL_WIDTH), F32),
                   jax.ShapeDtypeStruct((D_MODEL, D_MODEL), BF16), jax.ShapeDtypeStruct((1, D_MODEL), F32)],
        in_specs=[pl.BlockSpec((tq, EVEN_IN), lambda i: (rev(i), 0)),
                  pl.BlockSpec((BLOCK, EVEN_IN), lambda i: (jnp.maximum(rev(i) * nblk - 1, 0), 0)),
                  pl.BlockSpec((POOL_HALO, EVEN_IN), nxt),
                  row, nxt_row, row, nxt_row, row, square, _const_spec((1, D_MODEL)),
                  pl.BlockSpec(memory_space=pltpu.SMEM),
                  _const_spec((POOL_GROUPS, POOL_GC, POOL_GC)), _const_spec((1, POOL_WIDTH))],
        out_specs=[pl.BlockSpec((tq, EVEN_IN), lambda i: (rev(i), 0)),
                   pl.BlockSpec((8, 128), lambda i: (0, 0)),
                   pl.BlockSpec((POOL_GROUPS * POOL_GC, POOL_GC), lambda i: (0, 0)),
                   pl.BlockSpec((1, POOL_WIDTH), lambda i: (0, 0)), square, pl.BlockSpec((1, D_MODEL), lambda i: (0, 0))],
        scratch_shapes=[pltpu.VMEM((tq + BLOCK, 256), F32), pltpu.VMEM((tq + BLOCK, 256), F32),
                        pltpu.VMEM((BLOCK, 256), F32)] + ATTN_CONSTS
        + [pltpu.VMEM((D_MODEL, D_MODEL), F32), pltpu.VMEM((tq + POOL_HALO, D_MODEL), F32)],
        params=_params("arbitrary"))


CONV_RC = 32
CONV_CC = 128
CONV_CHAINS_FWD = 4
CONV_CHAINS_BWD = 2
CONV_UNROLL = 2


def _fill_shifted(s_ref, rows):
    for b in range(1, 8):
        s_ref[b, 0:rows - 8, :] = s_ref[0, b:b + rows - 8, :]


def _tap_blocks(s_ref, r, cols, lead):
    for b in range(8):
        taps = [(a, 8 * a + b - lead) for a in range(5) if 0 <= 8 * a + b - lead < CONV_K]
        span = 8 * max(a for a, _ in taps) + CONV_RC
        blk = s_ref[b, pl.ds(r, span), cols]
        for a, k in taps:
            yield k, blk[8 * a:8 * a + CONV_RC]


def _conv_taps(s_ref, w_ref, r, cols, lead, reverse, chains):
    accs = [None] * chains
    for n, (k, blk) in enumerate(_tap_blocks(s_ref, r, cols, lead)):
        kw = CONV_K - 1 - k if reverse else k
        term = blk * w_ref[kw:kw + 1, cols]
        accs[n % chains] = term if accs[n % chains] is None else accs[n % chains] + term
    while len(accs) > 1:
        accs = [a + b for a, b in zip(accs[0::2], accs[1::2])]
    return accs[0]


def _layer_norm_fwd(cf, lng, lnb):
    mu = jnp.mean(cf, axis=-1, keepdims=True)
    xc = cf - mu
    rstd = lax.rsqrt(jnp.mean(xc * xc, axis=-1, keepdims=True) + EPS)
    chat = xc * rstd
    return chat, rstd, chat * lng + lnb


def _layer1_fwd(proj, dw, dwb, lng, lnb, w_out, x_in, post, target, tt=512):
    t = proj.shape[0]
    lead = CONV_HALO - (CONV_K - 1)

    def body(main_ref, halo_ref, w_ref, b_ref, g_ref, lb_ref, wo_ref, x_ref, p_ref, t_ref,
             o_ref, c_ref, y_ref, dl_ref, l_ref, gs_ref):
        i = pl.program_id(0)
        hv = halo_ref[...]
        gs_ref[0, 0:CONV_HALO, :] = jnp.where(i == 0, 0.0, hv[:, GLU_A] * _sigmoid(hv[:, GLU_B]))
        gs_ref[0, CONV_HALO:CONV_HALO + tt, :] = main_ref[:, GLU_A] * _sigmoid(main_ref[:, GLU_B])
        _fill_shifted(gs_ref, tt + CONV_HALO)

        for c in range(D_MODEL // CONV_CC):
            cols = slice(c * CONV_CC, (c + 1) * CONV_CC)

            def chunk(j, carry):
                r = pl.multiple_of(j * CONV_RC, CONV_RC)
                c_ref[pl.ds(r, CONV_RC), cols] = _conv_taps(gs_ref, w_ref, r, cols, lead, False, CONV_CHAINS_FWD) + b_ref[:, cols]
                return carry
            lax.fori_loop(0, tt // CONV_RC, chunk, 0, unroll=CONV_UNROLL)

        _, _, cn = _layer_norm_fwd(c_ref[...], g_ref[...], lb_ref[...])
        o_ref[...] = (_silu(cn) * _silu(main_ref[:, GATE])).astype(BF16)

        d = _project_out(o_ref[...], wo_ref, x_ref, p_ref, y_ref) - t_ref[...]
        dl_ref[...] = d * (1.0 / D_MODEL)

        @pl.when(i == 0)
        def _():
            l_ref[...] = jnp.zeros_like(l_ref)

        l_ref[...] += 0.5 * jnp.sum(jnp.mean(d * d, axis=-1, keepdims=True))

    vec = _const_spec((1, D_MODEL))
    row = pl.BlockSpec((tt, D_MODEL), lambda i: (i, 0))
    f32_rows = jax.ShapeDtypeStruct((t, D_MODEL), F32)
    return pl.pallas_call(
        body, name="layer1_fwd", grid=(t // tt,),
        out_shape=[jax.ShapeDtypeStruct((t, D_MODEL), BF16), f32_rows, f32_rows, f32_rows,
                   jax.ShapeDtypeStruct((8, 128), F32)],
        in_specs=[pl.BlockSpec((tt, 3 * D_MODEL), lambda i: (i, 0)),
                  pl.BlockSpec((CONV_HALO, 3 * D_MODEL), lambda i: (jnp.maximum(i * (tt // CONV_HALO) - 1, 0), 0)),
                  _const_spec((CONV_K, D_MODEL)), vec, vec, vec,
                  _const_spec((D_MODEL, D_MODEL)), row, vec, row],
        out_specs=[row, row, row, row, pl.BlockSpec((8, 128), lambda i: (0, 0))],
        scratch_shapes=[pltpu.VMEM((8, tt + CONV_HALO, D_MODEL), F32)],
        compiler_params=_params("arbitrary"),
    )(proj, proj, dw, dwb, lng, lnb, w_out, x_in, post, target)


def _layer1_bwd(proj, cf, gy, y, z, w_out, post, dw, lng, lnb, comm=None, tt=256):
    t = proj.shape[0]
    nt = t // tt
    te = tt + CONV_HALO

    def body(main_ref, next_ref, cf_ref, cfn_ref, gy_ref, gyn_ref, y_ref, yn_ref, z_ref, wo_ref, po_ref,
             w_ref, g_ref, lb_ref,
             o_ref, ddw_ref, ddb_ref, dg_ref, dlb_ref, dwo16_ref, dpo_ref, ds_ref, glu_ref, sb_ref, dwo_ref):
        i = pl.program_id(0)

        @pl.when(i == 0)
        def _():
            ddw_ref[...] = jnp.zeros_like(ddw_ref)
            ddb_ref[...] = jnp.zeros_like(ddb_ref)
            dg_ref[...] = jnp.zeros_like(dg_ref)
            dlb_ref[...] = jnp.zeros_like(dlb_ref)

        dzv = _post_bwd_rows(jnp.concatenate([gy_ref[...], gyn_ref[...]], axis=0),
                             jnp.concatenate([y_ref[...], yn_ref[...]], axis=0), z_ref[...], tt, i == 0, i == nt - 1,
                             po_ref, wo_ref, dwo_ref, dwo16_ref, dpo_ref)
        dzv = jnp.concatenate([dzv[0:tt], jnp.where(i < nt - 1, dzv[tt:], 0.0)], axis=0)
        lng = g_ref[...]
        chat, rstd, cn = _layer_norm_fwd(jnp.concatenate([cf_ref[...], cfn_ref[...]], axis=0), lng, lb_ref[...])
        gate = jnp.concatenate([main_ref[:, GATE], next_ref[:, GATE]], axis=0)
        silu_cn, dsilu_cn = _silu_and_grad(cn)
        silu_gate, dsilu_gate = _silu_and_grad(gate)
        o_ref[:, GATE] = (dzv * silu_cn * dsilu_gate)[0:tt].astype(BF16)
        dcn = dzv * silu_gate * dsilu_cn
        dg_ref[...] += jnp.sum((dcn * chat)[0:tt], axis=0, keepdims=True)
        dlb_ref[...] += jnp.sum(dcn[0:tt], axis=0, keepdims=True)
        dchat = dcn * lng
        dcf = rstd * (dchat - jnp.mean(dchat, axis=-1, keepdims=True) - chat * jnp.mean(dchat * chat, axis=-1, keepdims=True))
        ddb_ref[...] += jnp.sum(dcf[0:tt], axis=0, keepdims=True)
        ds_ref[0, 0:te, :] = dcf
        ds_ref[0, te:, :] = jnp.zeros((8, D_MODEL), F32)
        _fill_shifted(ds_ref, te + 8)
        sb_ref[...] = _sigmoid(main_ref[:, GLU_B])
        glu_ref[...] = main_ref[:, GLU_A] * sb_ref[...]

        for c in range(D_MODEL // CONV_CC):
            cols = slice(c * CONV_CC, (c + 1) * CONV_CC)
            gcols = slice(D_MODEL + c * CONV_CC, D_MODEL + (c + 1) * CONV_CC)

            def chunk(j, carry):
                r = pl.multiple_of(j * CONV_RC, CONV_RC)
                dglu = _conv_taps(ds_ref, w_ref, r, cols, 0, True, CONV_CHAINS_BWD)
                sb = sb_ref[pl.ds(r, CONV_RC), cols]
                o_ref[pl.ds(r, CONV_RC), cols] = (dglu * sb).astype(BF16)
                o_ref[pl.ds(r, CONV_RC), gcols] = (dglu * glu_ref[pl.ds(r, CONV_RC), cols] * (1.0 - sb)).astype(BF16)
                return carry
            lax.fori_loop(0, tt // CONV_RC, chunk, 0, unroll=CONV_UNROLL)

            def taps(j, accs):
                r = pl.multiple_of(j * CONV_RC, CONV_RC)
                gl = glu_ref[pl.ds(r, CONV_RC), cols]
                new = list(accs)
                for m, blk in _tap_blocks(ds_ref, r, cols, 0):
                    prod = blk * gl
                    part = prod[0:8]
                    for q in range(1, CONV_RC // 8):
                        part = part + prod[8 * q:8 * q + 8]
                    new[m] = new[m] + part
                return tuple(new)
            accs = lax.fori_loop(0, tt // CONV_RC, taps, tuple(jnp.zeros((8, CONV_CC), F32) for _ in range(CONV_K)))
            for m in range(CONV_K):
                k = CONV_K - 1 - m
                ddw_ref[k:k + 1, cols] += jnp.sum(accs[m], axis=0, keepdims=True)

    vec = _const_spec((1, D_MODEL))
    vec_out = pl.BlockSpec((1, D_MODEL), lambda i: (0, 0))
    row = pl.BlockSpec((tt, D_MODEL), lambda i: (i, 0))
    nxt = lambda i: (jnp.minimum((i + 1) * (tt // CONV_HALO), t // CONV_HALO - 1), 0)
    nxt_row = pl.BlockSpec((CONV_HALO, D_MODEL), nxt)
    vec_f32 = jax.ShapeDtypeStruct((1, D_MODEL), F32)
    square = _const_spec((D_MODEL, D_MODEL))
    return _fused_call(
        body, comm, (proj, proj, cf, cf, gy, gy, y, y, z, w_out, post, dw, lng, lnb), name="layer1_bwd", grid=(nt,),
        out_shape=[jax.ShapeDtypeStruct((t, 3 * D_MODEL), BF16), jax.ShapeDtypeStruct((CONV_K, D_MODEL), F32),
                   vec_f32, vec_f32, vec_f32, jax.ShapeDtypeStruct((D_MODEL, D_MODEL), BF16), vec_f32],
        in_specs=[pl.BlockSpec((tt, 3 * D_MODEL), lambda i: (i, 0)),
                  pl.BlockSpec((CONV_HALO, 3 * D_MODEL), nxt),
                  row, nxt_row, row, nxt_row, row, nxt_row, row, square, vec,
                  _const_spec((CONV_K, D_MODEL)), vec, vec],
        out_specs=[pl.BlockSpec((tt, 3 * D_MODEL), lambda i: (i, 0)),
                   pl.BlockSpec((CONV_K, D_MODEL), lambda i: (0, 0)), vec_out, vec_out, vec_out, square, vec_out],
        scratch_shapes=[pltpu.VMEM((8, te + 8, D_MODEL), F32), pltpu.VMEM((tt, D_MODEL), F32),
                        pltpu.VMEM((tt, D_MODEL), F32), pltpu.VMEM((D_MODEL, D_MODEL), F32)],
        params=_params("arbitrary"))


def _piece_sums(tensors, place, name):
    counts = [len(parts) for parts in tensors]

    def body(p_ref, *refs):
        ins, outs = refs[:sum(counts)], refs[sum(counts):]
        pos = 0
        for count, o_ref in zip(counts, outs):
            acc = ins[pos][0].astype(F32)
            for part in ins[pos + 1:pos + count]:
                acc = acc + part[0].astype(F32)
            o_ref[0] = acc
            pos += count

    spec = lambda a, slot: pl.BlockSpec((1,) + a.shape[1:], lambda j, p_ref: (slot(p_ref), 0, 0))
    return pl.pallas_call(
        body, name=name,
        grid_spec=pltpu.PrefetchScalarGridSpec(
            num_scalar_prefetch=1, grid=(1,),
            in_specs=[spec(a, slot) for parts in tensors for a, slot in parts],
            out_specs=[pl.BlockSpec((1,) + parts[0][0].shape[1:], lambda j, p_ref: (p_ref[1], 0, 0))
                       for parts in tensors]),
        out_shape=[jax.ShapeDtypeStruct((2,) + parts[0][0].shape[1:], F32) for parts in tensors],
        compiler_params=_params("arbitrary"),
    )(place, *[a for parts in tensors for a, _ in parts])


def _direct_parts(own, recv):
    peer = lambda m: (lambda p: p[0] ^ m)
    return [(own, peer(0))] + [(recv, peer(m)) for m in range(1, N_DEV)]


def _share_with_sibling(halves, small, name):
    n = len(halves)

    def body(*refs):
        small_ref, outs, all_ref = refs[n], refs[n + 1:2 * n + 1], refs[2 * n + 1]
        send_sems, recv_sems, local_sem = refs[2 * n + 2:]
        x, y, c, _ = _place()
        me = 4 * x + 2 * y + c
        send = [pltpu.make_async_remote_copy(
            src_ref=outs[t].at[c], dst_ref=outs[t].at[c], send_sem=send_sems.at[t], recv_sem=recv_sems.at[t],
            device_id=(x, y, 1 - c), device_id_type=MESH_ID) for t in range(n)]
        recv = [pltpu.make_async_remote_copy(
            src_ref=outs[t].at[c], dst_ref=outs[t].at[1 - c], send_sem=send_sems.at[t], recv_sem=recv_sems.at[t],
            device_id=(x, y, 1 - c), device_id_type=MESH_ID) for t in range(n)]
        for m in range(1, N_DEV):
            px, py, pc = x ^ (m >> 2), y ^ ((m >> 1) & 1), c ^ (m & 1)
            sems = dict(send_sem=send_sems.at[n + m - 1], recv_sem=recv_sems.at[n + m - 1], device_id=(px, py, pc),
                        device_id_type=MESH_ID)
            send.append(pltpu.make_async_remote_copy(src_ref=small_ref, dst_ref=all_ref.at[me], **sems))
            recv.append(pltpu.make_async_remote_copy(src_ref=small_ref, dst_ref=all_ref.at[4 * px + 2 * py + pc], **sems))
        mine = pltpu.make_async_copy(small_ref, all_ref.at[me], local_sem)
        mine.start()
        for cp in send:
            cp.start()
        for cp in recv:
            cp.wait_recv()
        for cp in send:
            cp.wait_send()
        mine.wait()

    k = n + N_DEV - 1
    return pl.pallas_call(
        body, name=name,
        out_shape=[jax.ShapeDtypeStruct(h.shape, h.dtype) for h in halves]
        + [jax.ShapeDtypeStruct((N_DEV,) + small.shape, small.dtype)],
        in_specs=[ANY] * (n + 1), out_specs=[ANY] * (n + 1),
        input_output_aliases={t: t for t in range(n)},
        scratch_shapes=[pltpu.SemaphoreType.DMA((k,)), pltpu.SemaphoreType.DMA((k,)), pltpu.SemaphoreType.DMA],
    )(*halves, small)


def _sum8(parts, name):
    n = len(parts)

    def body(*refs):
        for p_ref, o_ref in zip(refs[:n], refs[n:]):
            acc = p_ref[0]
            for k in range(1, N_DEV):
                acc = acc + p_ref[k]
            o_ref[...] = acc

    whole = pl.BlockSpec(memory_space=pltpu.VMEM)
    return pl.pallas_call(
        body, name=name, out_shape=[jax.ShapeDtypeStruct(p.shape[1:], F32) for p in parts],
        in_specs=[whole] * n, out_specs=[whole] * n,
    )(*parts)


def _adamw(tensors, steps, name):
    n = len(tensors)
    views, specs = [], []
    for w, _, _, _ in tensors:
        cols = w.shape[-1]
        rows = w.size // cols
        if w.ndim >= 3 and w.shape[-2] == 1:
            assert steps == 1
            views.append((rows, 1, cols))
            specs.append(pl.BlockSpec((rows, 1, cols), lambda i: (0, 0, 0)))
        else:
            views.append((rows, cols))
            specs.append(pl.BlockSpec((rows // steps, cols), lambda i: (i, 0)))

    def body(*refs):
        for t in range(n):
            w_ref, g_ref, m_ref, v_ref = refs[4 * t:4 * t + 4]
            d_ref, nm_ref, nv_ref = refs[4 * n + 3 * t:4 * n + 3 * t + 3]
            gv = g_ref[...]
            mn = ADAM_B1 * m_ref[...] + (1.0 - ADAM_B1) * gv
            vn = ADAM_B2 * v_ref[...] + (1.0 - ADAM_B2) * (gv * gv)
            m_hat = mn / (1.0 - ADAM_B1 ** ADAM_STEP)
            v_hat = vn / (1.0 - ADAM_B2 ** ADAM_STEP)
            d_ref[...] = -ADAM_LR * (m_hat / (jnp.sqrt(v_hat) + ADAM_EPS) + ADAM_WD * w_ref[...])
            nm_ref[...] = mn
            nv_ref[...] = vn

    outs = pl.pallas_call(
        body, name=name, grid=(steps,),
        out_shape=[jax.ShapeDtypeStruct(view, F32) for view in views for _ in range(3)],
        in_specs=[spec for spec in specs for _ in range(4)], out_specs=[spec for spec in specs for _ in range(3)],
        compiler_params=_params("parallel"),
    )(*[a.reshape(view) for group, view in zip(tensors, views) for a in group])
    return [tuple(o.reshape(group[0].shape) for o in outs[3 * t:3 * t + 3]) for t, group in enumerate(tensors)]


SMALL_ROWS = 832


def _pack_small(g):
    parts = [g["loss"], g["pre1"].reshape(8, 128), g["post0"].reshape(8, 128),
             g["post1"].reshape(8, 128), g["sinks"], jnp.pad(g["pool_scale"].reshape(4, 128), ((0, 4), (0, 0))),
             g["pool_w"], g["dw"].reshape(248, 128), g["dwb"].reshape(8, 128), g["lng"].reshape(8, 128),
             g["lnb"].reshape(8, 128)]
    assert sum(p.shape[0] for p in parts) == SMALL_ROWS
    return jnp.concatenate(parts, axis=0)


def _unpack_small(s):
    out, r = {}, 0
    for key, rows, shape in (("loss", 8, (8, 128)), ("pre1", 8, (1, D_MODEL)), ("post", 16, (2, D_MODEL)),
                             ("sinks", 8, (8, 128)),
                             ("pool_scale", 4, (1, POOL_WIDTH)), ("pad", 4, (4, 128)), ("pool_w", 512, (1, 4, 128, 128)),
                             ("dw", 248, (CONV_K, D_MODEL)), ("dwb", 8, (1, D_MODEL)), ("lng", 8, (1, D_MODEL)),
                             ("lnb", 8, (1, D_MODEL))):
        out[key] = s[r:r + rows].reshape(shape)
        r += rows
    return out


def kernel(x, pre_norm, post_norm, a_w_in, a_sinks, b_pool_w, b_pool_scale, ab_w_out, c_w_in, c_dw_w, c_dw_b, c_ln_g, c_ln_b, c_w_out, loss_target, m_pre_norm, m_post_norm, m_a_w_in, m_a_sinks, m_b_pool_w, m_b_pool_scale, m_ab_w_out, m_c_w_in, m_c_dw_w, m_c_dw_b, m_c_ln_g, m_c_ln_b, m_c_w_out, v_pre_norm, v_post_norm, v_a_w_in, v_a_sinks, v_b_pool_w, v_b_pool_scale, v_ab_w_out, v_c_w_in, v_c_dw_w, v_c_dw_b, v_c_ln_g, v_c_ln_b, v_c_w_out):
    ix, iy = lax.axis_index("x"), lax.axis_index("y")
    chip_cols = (2 * ix + iy) * 256

    pad8 = lambda v: jnp.pad(v, ((0, -v.shape[0] % 8), (0, 0)))
    vec_shard = jnp.concatenate([pad8(c_dw_w.reshape(CONV_K, 256)), pad8(c_dw_b), pad8(c_ln_g), pad8(c_ln_b),
                                 jnp.zeros((8, 256), F32)], axis=0)
    x0, target = x[0], loss_target[0]
    pre0, pre1 = pre_norm[0:1], pre_norm[1:2]
    post0, post1 = post_norm[0:1], post_norm[1:2]
    pool_w = b_pool_w[0]

    (wa_t,) = _run_comm(_Gather([a_w_in[0].T.astype(BF16)], halve=True), "gather_a_w_in")
    wa_t = wa_t.reshape(EVEN_IN, D_MODEL)
    proj0, (w_ab,) = _norm_matmul(x0, pre0, wa_t, "proj0_fwd", comm=_Gather([ab_w_out[0].astype(BF16)], halve=True))
    w_ab = w_ab.reshape(D_MODEL, D_MODEL)
    (mix0, y0, x1), (wc_t, w_c, vecs) = _layer0_fwd(
        proj0, a_sinks, pool_w, b_pool_scale, w_ab, x0, post0,
        comm=_Gather([c_w_in[0].T.astype(BF16), c_w_out[0].astype(BF16), vec_shard], halve=True))
    wc_t = wc_t.reshape(3 * D_MODEL, D_MODEL)
    w_c = w_c.reshape(D_MODEL, D_MODEL)
    vecs = vecs.reshape(4, 64, 256).transpose(1, 0, 2).reshape(64, D_MODEL)
    dw, dwb, lng, lnb = vecs[0:CONV_K], vecs[32:33], vecs[40:41], vecs[48:49]
    proj1, _ = _norm_matmul(x1, pre1, wc_t, "proj1_fwd")
    z1, cf1, y1, g2, loss = _layer1_fwd(proj1, dw, dwb, lng, lnb, w_c, x1, post1, target)

    pieces = lambda m: m.reshape(N_DEV, m.shape[0] // N_DEV, D_MODEL)
    (dproj1, d_dw, d_dwb, d_lng, d_lnb, d_wc, d_post1), _ = _layer1_bwd(proj1, cf1, g2, y1, z1, w_c, post1, dw, lng, lnb)
    (g1, d_wct, d_pre1), _ = _pre_bwd(dproj1, wc_t, x1, pre1, g2, "proj1_bwd")
    (dproj0, d_sinks, d_pw, d_ps, d_wab, d_post0), (r_wc, r_wct) = _layer0_bwd(
        proj0, g1, y0, mix0, w_ab, post0, a_sinks, pool_w, b_pool_scale,
        comm=_Scatter([pieces(d_wc), pieces(d_wct)]))
    g = dict(loss=loss, pre1=d_pre1, post0=d_post0, post1=d_post1, sinks=d_sinks, pool_w=d_pw, pool_scale=d_ps,
             dw=d_dw, dwb=d_dwb, lng=d_lng, lnb=d_lnb)
    d_wat, (small8, r_wab) = _proj_dw(dproj0, x0, pre0, "proj0_dw",
                                      comm=_Comms(_Gather([_pack_small(g)], halve=False), _Scatter([pieces(d_wab)])))
    sent = _scatter_start(pieces(d_wat), "scatter_a_start")
    (gx, d_pre0), _ = _proj_dx(dproj0, wa_t, x0, pre0 + sent[4][0:1, 0:1], g1, "proj0_dx")
    own_wat, r_wat = _scatter_wait(*sent[:4], d_pre0, "scatter_a_wait")

    ic = lax.axis_index("c")
    me = 4 * ix + 2 * iy + ic
    place = jnp.stack([me, ic]).astype(jnp.int32)
    parts = [_direct_parts(own_wat, r_wat), _direct_parts(pieces(d_wab), r_wab), _direct_parts(pieces(d_wct), r_wct),
             _direct_parts(pieces(d_wc), r_wc)]
    halves = _piece_sums(parts, place, "grad_sums")
    *shared, pre0_8 = _share_with_sibling(halves, d_pre0.reshape(8, 128), "grad_share")
    g_wa_t, g_wab, g_wc_t, g_wc = [h.reshape(2 * h.shape[1], D_MODEL) for h in shared]
    g_c_w_in = g_wc_t.T[None]
    g_ab_w_out = g_wab[None]
    g_c_w_out = g_wc[None]

    small, pre0 = _sum8([small8, pre0_8], "small_sums")
    s = _unpack_small(small)
    layer = lax.broadcasted_iota(jnp.int32, (2, D_MODEL), 0)
    g_pre = jnp.where(layer == 0, pre0.reshape(1, D_MODEL), s["pre1"])
    g_post = s["post"]
    g_sinks = s["sinks"][:, 0].reshape(1, 8)
    g_pool_w, g_pool_scale = s["pool_w"], s["pool_scale"]
    g_dw = lax.dynamic_slice_in_dim(s["dw"], chip_cols, 256, axis=1).reshape(1, CONV_K, 1, 256)
    g_dwb = lax.dynamic_slice_in_dim(s["dwb"], chip_cols, 256, axis=1)
    g_lng = lax.dynamic_slice_in_dim(s["lng"], chip_cols, 256, axis=1)
    g_lnb = lax.dynamic_slice_in_dim(s["lnb"], chip_cols, 256, axis=1)

    turn = lambda a: jnp.swapaxes(a, 1, 2)
    a_w_in, m_a_w_in, v_a_w_in = turn(a_w_in), turn(m_a_w_in), turn(v_a_w_in)
    grads = [g_pre, g_post, g_wa_t[None], g_sinks, g_pool_w, g_pool_scale, g_ab_w_out, g_c_w_in, g_dw, g_dwb, g_lng,
             g_lnb, g_c_w_out]
    weights = [pre_norm, post_norm, a_w_in, a_sinks, b_pool_w, b_pool_scale, ab_w_out, c_w_in, c_dw_w, c_dw_b, c_ln_g,
               c_ln_b, c_w_out]
    moms = [m_pre_norm, m_post_norm, m_a_w_in, m_a_sinks, m_b_pool_w, m_b_pool_scale, m_ab_w_out, m_c_w_in, m_c_dw_w,
            m_c_dw_b, m_c_ln_g, m_c_ln_b, m_c_w_out]
    vars_ = [v_pre_norm, v_post_norm, v_a_w_in, v_a_sinks, v_b_pool_w, v_b_pool_scale, v_ab_w_out, v_c_w_in, v_c_dw_w,
             v_c_dw_b, v_c_ln_g, v_c_ln_b, v_c_w_out]
    tensors = list(zip(weights, grads, moms, vars_))
    matrices = (2, 6, 7, 12)
    others = [k for k in range(len(tensors)) if k not in matrices]
    updates = dict(zip(matrices, _adamw([tensors[k] for k in matrices], 4, "adamw_matrices")))
    updates.update(zip(others, _adamw([tensors[k] for k in others], 1, "adamw_small")))
    deltas, new_m, new_v = ([updates[k][j] for k in range(len(tensors))] for j in range(3))
    for outs in (grads, deltas, new_m, new_v):
        outs[2] = turn(outs[2])
    return (s["loss"][0, 0], gx[None], *grads, *deltas, *new_m, *new_v)
```

```python
import jax
import jax.numpy as jnp
from jax import lax
from jax.experimental import pallas as pl
from jax.experimental.pallas import tpu as pltpu

F32 = jnp.float32
BF16 = jnp.bfloat16

D_MODEL = 1024
EPS = 1e-6
NEG = -1e30
HEAD_DIM = 64
GROUP = 4
KV_HEADS = 2
BLOCK = 128
EVEN_IN = 2304
ATTN_WIDTH = 512
POOL_WIDTH = 512
COL_Q, COL_K, COL_GA, COL_U, COL_GB = 0, 512, 768, 1280, 1792
POOL_GROUPS = 4
POOL_GC = 128
POOL_HALO = 16
CONV_K = 31
CONV_HALO = 32
GLU_A = slice(0, D_MODEL)
GLU_B = slice(D_MODEL, 2 * D_MODEL)
GATE = slice(2 * D_MODEL, 3 * D_MODEL)
N_DEV = 8

ADAM_LR = 0.001
ADAM_B1 = 0.9
ADAM_B2 = 0.999
ADAM_EPS = 1e-08
ADAM_WD = 0.01
ADAM_STEP = 10

VMEM_LIMIT_BYTES = 56 * 1024 * 1024

NT = (((1,), (1,)), ((), ()))
TN = (((0,), (0,)), ((), ()))
MESH_ID = pl.DeviceIdType.MESH


def _params(*sem):
    return pltpu.CompilerParams(dimension_semantics=sem, vmem_limit_bytes=VMEM_LIMIT_BYTES)


def _const_spec(shape):
    nd = len(shape)
    return pl.BlockSpec(shape, lambda *_: (0,) * nd, pipeline_mode=pl.Buffered(1))


def _sigmoid(v):
    return 0.5 * jnp.tanh(0.5 * v) + 0.5


def _silu(v):
    h = 0.5 * v
    return h * jnp.tanh(h) + h


def _silu_and_grad(v):
    s = _sigmoid(v)
    silu = v * s
    return silu, s + silu * (1.0 - s)


ANY = pl.BlockSpec(memory_space=pl.ANY)


def _place():
    x, y, c = lax.axis_index("x"), lax.axis_index("y"), lax.axis_index("c")
    chips = [(1 - x, y), (x, 1 - y), (1 - x, 1 - y)]
    return x, y, c, chips


class _Gather:
    def __init__(self, blocks, halve):
        self.ins = list(blocks)
        self.halve = halve
        self.n = n = len(blocks)
        self.shapes = [((b.shape[0] // 2) if halve else b.shape[0], b.shape[1]) for b in blocks]
        self.out_shape = [jax.ShapeDtypeStruct((N_DEV, r, cols), b.dtype) for (r, cols), b in zip(self.shapes, blocks)]
        self.scratch = [pltpu.SemaphoreType.DMA((7 * n,)), pltpu.SemaphoreType.DMA((7 * n,)),
                        pltpu.SemaphoreType.DMA((n,))]

    def _copies(self, ins, outs, sems):
        send_sems, recv_sems, local_sems = sems
        x, y, c, chips = _place()
        me, sibling = (x, y, c), (x, y, 1 - c)

        def piece(t, px, py, pc):
            return outs[t].at[4 * px + 2 * py + pc]

        def own(t):
            return ins[t].at[pl.ds(c * self.shapes[t][0], self.shapes[t][0])] if self.halve else ins[t]

        def copy(t, k, block, to, src=None):
            return pltpu.make_async_remote_copy(
                src_ref=piece(t, *block) if src is None else src, dst_ref=piece(t, *block),
                send_sem=send_sems.at[7 * t + k], recv_sem=recv_sems.at[7 * t + k],
                device_id=to, device_id_type=MESH_ID)

        rng = range(self.n)
        return dict(
            mine=[pltpu.make_async_copy(own(t), piece(t, *me), local_sems.at[t]) for t in rng],
            first=[copy(t, 0, me, sibling, src=own(t)) for t in rng]
            + [copy(t, 1 + j, me, (*chip, c), src=own(t)) for t in rng for j, chip in enumerate(chips)],
            landed=[copy(t, 1 + j, (*chip, c), me) for j, chip in enumerate(chips) for t in rng],
            passed=[copy(t, 4 + j, (*chip, c), sibling) for j, chip in enumerate(chips) for t in rng],
            from_sibling=[copy(t, 0, sibling, me) for t in rng]
            + [copy(t, 4 + j, (*chip, 1 - c), me) for t in rng for j, chip in enumerate(chips)])

    def start(self, ins, outs, sems):
        d = self._copies(ins, outs, sems)
        for cp in d["mine"] + d["first"]:
            cp.start()

    def middle(self, ins, outs, sems):
        d = self._copies(ins, outs, sems)
        for got, fwd in zip(d["landed"], d["passed"]):
            got.wait_recv()
            fwd.start()

    def finish(self, ins, outs, sems):
        d = self._copies(ins, outs, sems)
        for cp in d["from_sibling"]:
            cp.wait_recv()
        for cp in d["first"] + d["passed"]:
            cp.wait_send()
        for cp in d["mine"]:
            cp.wait()


class _Scatter:
    def __init__(self, tensors):
        self.ins = list(tensors)
        self.n = n = len(tensors)
        self.out_shape = [jax.ShapeDtypeStruct(t.shape, t.dtype) for t in tensors]
        self.scratch = [pltpu.SemaphoreType.DMA((7 * n,)), pltpu.SemaphoreType.DMA((7 * n,))]

    def _copies(self, ins, outs, sems):
        send_sems, recv_sems = sems
        x, y, c, _ = _place()
        me = 4 * x + 2 * y + c
        sends, recvs = [], []
        for t in range(self.n):
            for m in range(1, N_DEV):
                px, py, pc = x ^ (m >> 2), y ^ ((m >> 1) & 1), c ^ (m & 1)
                q = 4 * px + 2 * py + pc
                sems_k = dict(send_sem=send_sems.at[7 * t + m - 1], recv_sem=recv_sems.at[7 * t + m - 1],
                              device_id=(px, py, pc), device_id_type=MESH_ID)
                sends.append(pltpu.make_async_remote_copy(src_ref=ins[t].at[q], dst_ref=outs[t].at[me], **sems_k))
                recvs.append(pltpu.make_async_remote_copy(src_ref=ins[t].at[me], dst_ref=outs[t].at[q], **sems_k))
        return sends, recvs

    def start(self, ins, outs, sems):
        for cp in self._copies(ins, outs, sems)[0]:
            cp.start()

    def middle(self, ins, outs, sems):
        pass

    def finish(self, ins, outs, sems):
        sends, recvs = self._copies(ins, outs, sems)
        for cp in recvs:
            cp.wait_recv()
        for cp in sends:
            cp.wait_send()


class _Comms:
    def __init__(self, *comms):
        self.comms = comms
        self.ins = [a for c in comms for a in c.ins]
        self.out_shape = [s for c in comms for s in c.out_shape]
        self.scratch = [s for c in comms for s in c.scratch]

    def _each(self, phase, ins, outs, sems):
        i = o = s = 0
        for c in self.comms:
            ni, no, ns = len(c.ins), len(c.out_shape), len(c.scratch)
            getattr(c, phase)(ins[i:i + ni], outs[o:o + no], sems[s:s + ns])
            i, o, s = i + ni, o + no, s + ns

    def start(self, ins, outs, sems):
        self._each("start", ins, outs, sems)

    def middle(self, ins, outs, sems):
        self._each("middle", ins, outs, sems)

    def finish(self, ins, outs, sems):
        self._each("finish", ins, outs, sems)


HBM_SPEC = pl.BlockSpec(memory_space=pltpu.HBM)
SEM_SPEC = pl.BlockSpec(memory_space=pltpu.SEMAPHORE)
DATAFLOW = pltpu.SideEffectType.DATAFLOW_SIDE_EFFECTING


def _scatter_copies(own_ref, land_ref, send_sems, recv_sems):
    x, y, c, _ = _place()
    me = 4 * x + 2 * y + c
    pairs = []
    for m in range(1, N_DEV):
        px, py, pc = x ^ (m >> 2), y ^ ((m >> 1) & 1), c ^ (m & 1)
        q = 4 * px + 2 * py + pc
        sems = dict(send_sem=send_sems.at[m - 1], recv_sem=recv_sems.at[m - 1], device_id=(px, py, pc),
                    device_id_type=MESH_ID)
        pairs.append((pltpu.make_async_remote_copy(src_ref=own_ref.at[q], dst_ref=land_ref.at[me], **sems),
                      pltpu.make_async_remote_copy(src_ref=own_ref.at[me], dst_ref=land_ref.at[q], **sems)))
    return pairs


def _scatter_start(own, name):
    def body(own_ref, land_ref, send_sems, recv_sems, own_thru, land_thru, token):
        for send, _ in _scatter_copies(own_ref, land_ref, send_sems, recv_sems):
            send.start()
        token[...] = jnp.zeros_like(token)

    buf = pltpu.HBM(own.shape, own.dtype)
    return pl.pallas_call(
        body, name=name,
        out_shape=(pltpu.SemaphoreType.DMA((N_DEV - 1,)), pltpu.SemaphoreType.DMA((N_DEV - 1,)), buf, buf,
                   jax.ShapeDtypeStruct((8, 128), F32)),
        in_specs=(HBM_SPEC, HBM_SPEC),
        out_specs=(SEM_SPEC, SEM_SPEC, HBM_SPEC, HBM_SPEC, pl.BlockSpec(memory_space=pltpu.VMEM)),
        input_output_aliases={0: 2, 1: 3},
        compiler_params=pltpu.CompilerParams(has_side_effects=DATAFLOW),
    )(pltpu.with_memory_space_constraint(own, pltpu.HBM),
      pltpu.with_memory_space_constraint(lax.empty(own.shape, own.dtype), pltpu.HBM))


def _scatter_wait(send_sems, recv_sems, own_thru, land_thru, after, name):
    def body(own_ref, land_ref, send_sems, recv_sems, after_ref, own_out, land_out):
        for send, recv in _scatter_copies(own_ref, land_ref, send_sems, recv_sems):
            send.wait_send()
            recv.wait_recv()

    buf = pltpu.HBM(own_thru.shape, own_thru.dtype)
    return pl.pallas_call(
        body, name=name, out_shape=(buf, buf),
        in_specs=(HBM_SPEC, HBM_SPEC, SEM_SPEC, SEM_SPEC, ANY), out_specs=(HBM_SPEC, HBM_SPEC),
        input_output_aliases={0: 0, 1: 1},
        compiler_params=pltpu.CompilerParams(has_side_effects=DATAFLOW),
    )(own_thru, land_thru, send_sems, recv_sems, after)


def _fused_call(body, comm, args, *, name, grid, out_shape, in_specs, out_specs, scratch_shapes=(), params):
    single = not isinstance(out_shape, (list, tuple))
    out_shape = [out_shape] if single else list(out_shape)
    out_specs = [out_specs] if single else list(out_specs)
    if comm is None:
        res = pl.pallas_call(body, name=name, grid=grid, out_shape=out_shape, in_specs=in_specs, out_specs=out_specs,
                             scratch_shapes=list(scratch_shapes), compiler_params=params)(*args)
        return (res[0] if single else res), []
    n_in, n_out, n_scr = len(in_specs), len(out_shape), len(scratch_shapes)
    c_in, c_out = len(comm.ins), len(comm.out_shape)
    steps = grid[0]

    def fused(*refs):
        pos = 0
        groups = []
        for size in (n_in, c_in, n_out, c_out, n_scr, len(comm.scratch)):
            groups.append(refs[pos:pos + size])
            pos += size
        ins, c_ins, outs, c_outs, scr, c_sems = groups
        i = pl.program_id(0)

        @pl.when(i == 0)
        def _():
            comm.start(c_ins, c_outs, c_sems)

        @pl.when(i == steps // 2)
        def _():
            comm.middle(c_ins, c_outs, c_sems)

        body(*ins, *outs, *scr)

        @pl.when(i == steps - 1)
        def _():
            comm.finish(c_ins, c_outs, c_sems)

    res = pl.pallas_call(
        fused, name=name, grid=grid, out_shape=out_shape + list(comm.out_shape),
        in_specs=list(in_specs) + [ANY] * c_in, out_specs=out_specs + [ANY] * c_out,
        scratch_shapes=list(scratch_shapes) + list(comm.scratch), compiler_params=params)(*args, *comm.ins)
    main = res[:n_out]
    return (main[0] if single else main), list(res[n_out:])


def _norm_matmul(x, gain, wt, name, comm=None, tm=1024):
    t, n = x.shape[0], wt.shape[0]

    def body(x_ref, g_ref, wt_ref, o_ref):
        xv = x_ref[...]
        r = lax.rsqrt(jnp.mean(xv * xv, axis=-1, keepdims=True) + EPS)
        h = (xv * r * g_ref[...]).astype(BF16)
        o_ref[...] = lax.dot_general(h, wt_ref[...], NT, preferred_element_type=F32)

    return _fused_call(
        body, comm, (x, gain, wt), name=name, grid=(t // tm,),
        out_shape=jax.ShapeDtypeStruct((t, n), F32),
        in_specs=[pl.BlockSpec((tm, D_MODEL), lambda i: (i, 0)), _const_spec((1, D_MODEL)), _const_spec((n, D_MODEL))],
        out_specs=pl.BlockSpec((tm, n), lambda i: (i, 0)),
        params=_params("arbitrary"))


def _norm_cast(x, gain, name, comm=None, tm=1024):
    t = x.shape[0]

    def body(x_ref, g_ref, o_ref):
        xv = x_ref[...]
        r = lax.rsqrt(jnp.mean(xv * xv, axis=-1, keepdims=True) + EPS)
        o_ref[...] = (xv * r * g_ref[...]).astype(BF16)

    row = pl.BlockSpec((tm, D_MODEL), lambda i: (i, 0))
    return _fused_call(
        body, comm, (x, gain), name=name, grid=(t // tm,),
        out_shape=jax.ShapeDtypeStruct((t, D_MODEL), BF16),
        in_specs=[row, _const_spec((1, D_MODEL))], out_specs=row, params=_params("arbitrary"))


def _matmul_nt(h, wt, name, comm=None, tm=1024):
    t, n = h.shape[0], wt.shape[0]

    def body(h_ref, wt_ref, o_ref):
        o_ref[...] = lax.dot_general(h_ref[...], wt_ref[...], NT, preferred_element_type=F32)

    return _fused_call(
        body, comm, (h, wt), name=name, grid=(t // tm,),
        out_shape=jax.ShapeDtypeStruct((t, n), F32),
        in_specs=[pl.BlockSpec((tm, D_MODEL), lambda i: (i, 0)), _const_spec((n, D_MODEL))],
        out_specs=pl.BlockSpec((tm, n), lambda i: (i, 0)),
        params=_params("arbitrary"))


def _project_out(a, w_ref, x_ref, p_ref, y_ref):
    y = jnp.dot(a, w_ref[...], preferred_element_type=F32)
    y_ref[...] = y
    ry = lax.rsqrt(jnp.mean(y * y, axis=-1, keepdims=True) + EPS)
    return x_ref[...] + (y * ry) * p_ref[...]


def _post_bwd_rows(g, y, a, n_own, first, last, p_ref, w_ref, dw_ref, dw16_ref, dp_ref):
    @pl.when(first)
    def _():
        dw_ref[...] = jnp.zeros_like(dw_ref)
        dp_ref[...] = jnp.zeros_like(dp_ref)

    ry = lax.rsqrt(jnp.mean(y * y, axis=-1, keepdims=True) + EPS)
    nv = y * ry
    dp_ref[...] += jnp.sum((g * nv)[0:n_own], axis=0, keepdims=True)
    dn = g * p_ref[...]
    dy = (ry * (dn - nv * jnp.mean(dn * nv, axis=-1, keepdims=True))).astype(BF16)
    dw_ref[...] += lax.dot_general(a, dy[0:n_own], TN, preferred_element_type=F32)

    @pl.when(last)
    def _():
        dw16_ref[...] = dw_ref[...].astype(BF16)

    return lax.dot_general(dy, w_ref[...], NT, preferred_element_type=F32)


def _pre_bwd(dproj, wt, x_in, pre, g, name, comm=None, tm=512):
    t, n = dproj.shape
    steps = t // tm

    def body(dp_ref, wt_ref, x_ref, pre_ref, g_ref, dx_ref, dwt16_ref, dpre_ref, dwt_ref):
        @pl.when(pl.program_id(0) == 0)
        def _():
            dwt_ref[...] = jnp.zeros_like(dwt_ref)
            dpre_ref[...] = jnp.zeros_like(dpre_ref)

        dpv = dp_ref[...]
        dh = jnp.dot(dpv, wt_ref[...], preferred_element_type=F32)
        xv = x_ref[...]
        r = lax.rsqrt(jnp.mean(xv * xv, axis=-1, keepdims=True) + EPS)
        xn = xv * r
        pv = pre_ref[...]
        dpre_ref[...] += jnp.sum(dh * xn, axis=0, keepdims=True)
        dxn = dh * pv
        dx_ref[...] = g_ref[...] + r * (dxn - xn * jnp.mean(dxn * xn, axis=-1, keepdims=True))
        h = (xn * pv).astype(BF16)
        dwt_ref[...] += lax.dot_general(dpv, h, TN, preferred_element_type=F32)

        @pl.when(pl.program_id(0) == steps - 1)
        def _():
            dwt16_ref[...] = dwt_ref[...].astype(BF16)

    row = pl.BlockSpec((tm, D_MODEL), lambda i: (i, 0))
    return _fused_call(
        body, comm, (dproj, wt, x_in, pre, g), name=name, grid=(steps,),
        out_shape=[jax.ShapeDtypeStruct((t, D_MODEL), F32), jax.ShapeDtypeStruct((n, D_MODEL), BF16),
                   jax.ShapeDtypeStruct((1, D_MODEL), F32)],
        in_specs=[pl.BlockSpec((tm, n), lambda i: (i, 0)), _const_spec((n, D_MODEL)), row, _const_spec((1, D_MODEL)), row],
        out_specs=[row, _const_spec((n, D_MODEL)), pl.BlockSpec((1, D_MODEL), lambda i: (0, 0))],
        scratch_shapes=[pltpu.VMEM((n, D_MODEL), F32)],
        params=_params("arbitrary"))


def _proj_dw(dproj, h, name, comm=None, tm=1024):
    t, n = dproj.shape
    steps = t // tm

    def body(dp_ref, h_ref, dwt16_ref, dwt_ref):
        @pl.when(pl.program_id(0) == 0)
        def _():
            dwt_ref[...] = jnp.zeros_like(dwt_ref)

        dwt_ref[...] += lax.dot_general(dp_ref[...], h_ref[...], TN, preferred_element_type=F32)

        @pl.when(pl.program_id(0) == steps - 1)
        def _():
            dwt16_ref[...] = dwt_ref[...].astype(BF16)

    return _fused_call(
        body, comm, (dproj, h), name=name, grid=(steps,),
        out_shape=jax.ShapeDtypeStruct((n, D_MODEL), BF16),
        in_specs=[pl.BlockSpec((tm, n), lambda i: (i, 0)), pl.BlockSpec((tm, D_MODEL), lambda i: (i, 0))],
        out_specs=pl.BlockSpec((n, D_MODEL), lambda i: (0, 0)),
        scratch_shapes=[pltpu.VMEM((n, D_MODEL), F32)],
        params=_params("arbitrary"))


def _proj_dx(dproj, wt, x_in, pre, g, name, comm=None, tm=1024):
    t, n = dproj.shape

    def body(dp_ref, wt_ref, x_ref, pre_ref, g_ref, dx_ref, dpre_ref):
        @pl.when(pl.program_id(0) == 0)
        def _():
            dpre_ref[...] = jnp.zeros_like(dpre_ref)

        dh = jnp.dot(dp_ref[...], wt_ref[...], preferred_element_type=F32)
        xv = x_ref[...]
        r = lax.rsqrt(jnp.mean(xv * xv, axis=-1, keepdims=True) + EPS)
        xn = xv * r
        dpre_ref[...] += jnp.sum(dh * xn, axis=0, keepdims=True)
        dxn = dh * pre_ref[...]
        dx_ref[...] = g_ref[...] + r * (dxn - xn * jnp.mean(dxn * xn, axis=-1, keepdims=True))

    row = pl.BlockSpec((tm, D_MODEL), lambda i: (i, 0))
    return _fused_call(
        body, comm, (dproj, wt, x_in, pre, g), name=name, grid=(t // tm,),
        out_shape=[jax.ShapeDtypeStruct((t, D_MODEL), F32), jax.ShapeDtypeStruct((1, D_MODEL), F32)],
        in_specs=[pl.BlockSpec((tm, n), lambda i: (i, 0)), _const_spec((n, D_MODEL)), row, _const_spec((1, D_MODEL)), row],
        out_specs=[row, pl.BlockSpec((1, D_MODEL), lambda i: (0, 0))],
        params=_params("arbitrary"))


def _group_masks():
    lane = lax.broadcasted_iota(jnp.int32, (1, GROUP * HEAD_DIM), 1)
    return [lane // HEAD_DIM == g for g in range(GROUP)]


def _stack_groups(v, masks, scale=1.0):
    return jnp.concatenate([v * jnp.where(m, scale, 0.0) for m in masks], axis=0)


def _unstack_groups(v, masks):
    out = v[(GROUP - 1) * BLOCK:GROUP * BLOCK]
    for g in range(GROUP - 2, -1, -1):
        out = jnp.where(masks[g], v[g * BLOCK:(g + 1) * BLOCK], out)
    return out


def _repeat_head(kv2, kvh):
    first = lax.broadcasted_iota(jnp.int32, kv2.shape, 1) < HEAD_DIM
    rolled = pltpu.roll(kv2, HEAD_DIM, 1)
    one = jnp.where(first, kv2, rolled) if kvh == 0 else jnp.where(first, rolled, kv2)
    return jnp.concatenate([one, one], axis=1)


def _fold_head(v4):
    a = v4[:, 0:128] + v4[:, 128:256]
    return a + pltpu.roll(a, HEAD_DIM, 1)


ATTN_CONSTS = [pltpu.VMEM((KV_HEADS, GROUP * BLOCK, 2 * BLOCK), F32)]


def _fill_attn_bias(bias_ref):
    row = lax.broadcasted_iota(jnp.int32, (GROUP * BLOCK, 2 * BLOCK), 0)
    col = lax.broadcasted_iota(jnp.int32, (GROUP * BLOCK, 2 * BLOCK), 1)
    dist = (row % BLOCK) + BLOCK - col
    band = (dist >= 0) & (dist < BLOCK)
    rb = lax.broadcasted_iota(jnp.int32, (GROUP * BLOCK, 1), 0) // BLOCK
    for kvh in range(KV_HEADS):
        slope = jnp.zeros((GROUP * BLOCK, 1), F32)
        for g in range(GROUP):
            slope = jnp.where(rb == g, 2.0 ** (-(kvh * GROUP + g + 1)), slope)
        bias_ref[kvh] = jnp.where(band, -slope * dist.astype(F32), NEG)


def _row_sinks(kvh, sink_ref):
    rb = lax.broadcasted_iota(jnp.int32, (GROUP * BLOCK, 1), 0) // BLOCK
    sink = jnp.zeros((GROUP * BLOCK, 1), F32)
    for g in range(GROUP):
        sink = jnp.where(rb == g, sink_ref[0, kvh * GROUP + g], sink)
    return sink


def _attn_probs(qk, k4, bias, sink, no_past, masks):
    qs = _stack_groups(qk, masks, HEAD_DIM ** -0.5).astype(BF16)
    s = lax.dot_general(qs, k4, NT, preferred_element_type=F32) + bias
    s = jnp.concatenate([jnp.where(no_past, NEG, s[:, 0:BLOCK]), s[:, BLOCK:]], axis=1)
    mx = jnp.maximum(jnp.max(s, axis=-1, keepdims=True), sink)
    e = jnp.exp(s - mx)
    es = jnp.exp(sink - mx)
    inv = 1.0 / (jnp.sum(e, axis=-1, keepdims=True) + es)
    return qs, e * inv, es * inv


def _pool_forward(u_ext, g, t0):
    n = u_ext.shape[0] - POOL_HALO
    s = u_ext
    for step in range(g + 1):
        s = s + pltpu.roll(s, 1 << step, 0)
    w = 2 << g
    t = t0 + lax.broadcasted_iota(jnp.int32, (n, 1), 0)
    cnt = jnp.minimum(t + 1, w).astype(F32)
    return s[POOL_HALO:] / cnt - u_ext[POOL_HALO:]


def _layer0_fwd(proj, sinks, pool_w, pool_scale, w_out, x_in, post, comm=None, tq=512):
    t = proj.shape[0]
    nblk = tq // BLOCK

    def body(main_ref, halo_ref, sink_ref, pw_ref, ps_ref, w_ref, x_ref, p_ref, o_ref, y_ref, xo_ref, kv_ref, bias_ref):
        i = pl.program_id(0)
        t0 = i * tq
        masks = _group_masks()

        @pl.when(i == 0)
        def _():
            _fill_attn_bias(bias_ref)

        kv_ref[0:BLOCK, :] = halo_ref[:, COL_K:COL_K + 256]
        kv_ref[BLOCK:, :] = main_ref[:, COL_K:COL_K + 256]

        def block(jb, carry):
            r0 = pl.multiple_of(jb * BLOCK, BLOCK)
            no_past = t0 + r0 == 0
            q = main_ref[pl.ds(r0, BLOCK), COL_Q:COL_Q + ATTN_WIDTH]
            ga = main_ref[pl.ds(r0, BLOCK), COL_GA:COL_GA + ATTN_WIDTH]
            kk = kv_ref[pl.ds(r0, 2 * BLOCK), 0:128]
            vv = kv_ref[pl.ds(r0, 2 * BLOCK), 128:256]
            outs = []
            for kvh in range(KV_HEADS):
                k4 = _repeat_head(kk, kvh).astype(BF16)
                v4 = _repeat_head(vv, kvh).astype(BF16)
                _, p, _ = _attn_probs(q[:, kvh * 256:(kvh + 1) * 256], k4, bias_ref[kvh], _row_sinks(kvh, sink_ref), no_past, masks)
                pv = jnp.dot(p.astype(BF16), v4, preferred_element_type=F32)
                outs.append(_unstack_groups(pv, masks))
            attn = jnp.concatenate(outs, axis=1)
            o_ref[pl.ds(r0, BLOCK), 0:ATTN_WIDTH] = (attn * _silu(ga)).astype(BF16)
            return carry

        lax.fori_loop(0, nblk, block, 0, unroll=True)

        for g in range(POOL_GROUPS):
            cu = COL_U + g * POOL_GC
            cg = COL_GB + g * POOL_GC
            halo_u = jnp.where(i == 0, 0.0, halo_ref[BLOCK - POOL_HALO:BLOCK, cu:cu + POOL_GC])
            u_ext = jnp.concatenate([halo_u, main_ref[:, cu:cu + POOL_GC]], axis=0)
            pooled = _pool_forward(u_ext, g, t0)
            y = jnp.dot(pooled.astype(BF16), pw_ref[g].astype(BF16), preferred_element_type=F32)
            y = y * ps_ref[:, g * POOL_GC:(g + 1) * POOL_GC]
            o_ref[:, ATTN_WIDTH + g * POOL_GC:ATTN_WIDTH + (g + 1) * POOL_GC] =(y * _silu(main_ref[:, cg:cg + POOL_GC])).astype(BF16)

        xo_ref[...] = _project_out(o_ref[...], w_ref, x_ref, p_ref, y_ref)

    row = pl.BlockSpec((tq, D_MODEL), lambda i: (i, 0))
    return _fused_call(
        body, comm, (proj, proj, sinks, pool_w, pool_scale, w_out, x_in, post), name="layer0_fwd", grid=(t // tq,),
        out_shape=[jax.ShapeDtypeStruct((t, D_MODEL), BF16), jax.ShapeDtypeStruct((t, D_MODEL), F32),
                   jax.ShapeDtypeStruct((t, D_MODEL), F32)],
        in_specs=[pl.BlockSpec((tq, EVEN_IN), lambda i: (i, 0)),
                  pl.BlockSpec((BLOCK, EVEN_IN), lambda i: (jnp.maximum(i * nblk - 1, 0), 0)),
                  pl.BlockSpec(memory_space=pltpu.SMEM),
                  _const_spec((POOL_GROUPS, POOL_GC, POOL_GC)), _const_spec((1, POOL_WIDTH)),
                  _const_spec((D_MODEL, D_MODEL)), row, _const_spec((1, D_MODEL))],
        out_specs=[row, row, row],
        scratch_shapes=[pltpu.VMEM((tq + BLOCK, 256), F32)] + ATTN_CONSTS,
        params=_params("arbitrary"))


def _layer0_bwd(proj, gy, y, mix, w_out, post, sinks, pool_w, pool_scale, comm=None, tq=512):
    t = proj.shape[0]
    nt = t // tq
    nblk = tq // BLOCK

    def body(main_ref, halo_ref, next_ref, gy_ref, gyn_ref, y_ref, yn_ref, mix_ref, wo_ref, po_ref,
             sink_ref, pw_ref, ps_ref,
             o_ref, dsk_ref, dpw_ref, dps_ref, dwo16_ref, dpo_ref,
             kv_ref, dkv_ref, carry_ref, bias_ref, dwo_ref, dmix_ref):
        i = pl.program_id(0)
        ii = nt - 1 - i
        t0 = ii * tq
        masks = _group_masks()

        @pl.when(i == 0)
        def _():
            _fill_attn_bias(bias_ref)
            dsk_ref[...] = jnp.zeros_like(dsk_ref)
            dpw_ref[...] = jnp.zeros_like(dpw_ref)
            dps_ref[...] = jnp.zeros_like(dps_ref)
            carry_ref[...] = jnp.zeros_like(carry_ref)

        dmix_ref[...] = _post_bwd_rows(jnp.concatenate([gy_ref[...], gyn_ref[...]], axis=0),
                                       jnp.concatenate([y_ref[...], yn_ref[...]], axis=0), mix_ref[...], tq,
                                       i == 0, i == nt - 1, po_ref, wo_ref, dwo_ref, dwo16_ref, dpo_ref)
        dm_ref = dmix_ref.at[pl.ds(0, tq)]
        dmn_ref = dmix_ref.at[pl.ds(tq, POOL_HALO)]

        kv_ref[0:BLOCK, :] = halo_ref[:, COL_K:COL_K + 256]
        kv_ref[BLOCK:, :] = main_ref[:, COL_K:COL_K + 256]
        dkv_ref[0:tq, :] = jnp.zeros((tq, 256), F32)
        dkv_ref[tq:, :] = carry_ref[...]

        def block(jb, carry):
            r0 = pl.multiple_of(jb * BLOCK, BLOCK)
            no_past = t0 + r0 == 0
            q = main_ref[pl.ds(r0, BLOCK), COL_Q:COL_Q + ATTN_WIDTH]
            ga = main_ref[pl.ds(r0, BLOCK), COL_GA:COL_GA + ATTN_WIDTH]
            dya = dm_ref[pl.ds(r0, BLOCK), 0:ATTN_WIDTH]
            kk = kv_ref[pl.ds(r0, 2 * BLOCK), 0:128]
            vv = kv_ref[pl.ds(r0, 2 * BLOCK), 128:256]
            silu_ga, dsilu_ga = _silu_and_grad(ga)
            do = dya * silu_ga
            first = lax.broadcasted_iota(jnp.int32, (2 * BLOCK, 128), 1) < HEAD_DIM
            attn, dq, dk, dv = [], [], [], []
            for kvh in range(KV_HEADS):
                k4 = _repeat_head(kk, kvh).astype(BF16)
                v4 = _repeat_head(vv, kvh).astype(BF16)
                qs, p, ps = _attn_probs(q[:, kvh * 256:(kvh + 1) * 256], k4, bias_ref[kvh], _row_sinks(kvh, sink_ref), no_past, masks)
                pb = p.astype(BF16)
                o_k = _unstack_groups(jnp.dot(pb, v4, preferred_element_type=F32), masks)
                do_k = do[:, kvh * 256:(kvh + 1) * 256]
                dos = _stack_groups(do_k, masks).astype(BF16)
                prod = do_k * o_k
                delta = jnp.concatenate([jnp.sum(jnp.where(m, prod, 0.0), axis=-1, keepdims=True) for m in masks], axis=0)
                dp = lax.dot_general(dos, v4, NT, preferred_element_type=F32)
                ds = (p * (dp - delta)).astype(BF16)
                sink_term = ps * delta
                for g in range(GROUP):
                    h = kvh * GROUP + g
                    dsk_ref[h:h + 1, :] -= jnp.sum(sink_term[g * BLOCK:(g + 1) * BLOCK], keepdims=True)
                dq.append(_unstack_groups(jnp.dot(ds, k4, preferred_element_type=F32), masks) * (HEAD_DIM ** -0.5))
                dk.append(_fold_head(lax.dot_general(ds, qs, TN, preferred_element_type=F32)))
                dv.append(_fold_head(lax.dot_general(pb, dos, TN, preferred_element_type=F32)))
                attn.append(o_k)
            o_ref[pl.ds(r0, BLOCK), COL_Q:COL_Q + ATTN_WIDTH] = jnp.concatenate(dq, axis=1).astype(BF16)
            o_all = jnp.concatenate(attn, axis=1)
            o_ref[pl.ds(r0, BLOCK), COL_GA:COL_GA + ATTN_WIDTH] = (dya * o_all * dsilu_ga).astype(BF16)
            dkv = jnp.concatenate([jnp.where(first, dk[0], dk[1]), jnp.where(first, dv[0], dv[1])], axis=1)
            dkv_ref[pl.ds(r0, 2 * BLOCK), :] += dkv
            return carry

        lax.fori_loop(0, nblk, block, 0, unroll=True)
        carry_ref[...] = dkv_ref[0:BLOCK, :]
        o_ref[:, COL_K:COL_K + 256] = dkv_ref[BLOCK:, :].astype(BF16)

        last = ii == nt - 1
        for g in range(POOL_GROUPS):
            cu = COL_U + g * POOL_GC
            cg = COL_GB + g * POOL_GC
            cm = ATTN_WIDTH + g * POOL_GC
            pw = pw_ref[g].astype(BF16)
            sc = ps_ref[:, g * POOL_GC:(g + 1) * POOL_GC]
            halo_u = jnp.where(ii == 0, 0.0, halo_ref[BLOCK - POOL_HALO:BLOCK, cu:cu + POOL_GC])
            u_ext = jnp.concatenate([halo_u, main_ref[:, cu:cu + POOL_GC]], axis=0)
            pooled = _pool_forward(u_ext, g, t0).astype(BF16)
            y_raw = jnp.dot(pooled, pw, preferred_element_type=F32)
            gb = main_ref[:, cg:cg + POOL_GC]
            dyb = dm_ref[:, cm:cm + POOL_GC]
            silu_gb, dsilu_gb = _silu_and_grad(gb)
            dypool = dyb * silu_gb
            dps_ref[:, g * POOL_GC:(g + 1) * POOL_GC] += jnp.sum(dypool * y_raw, axis=0, keepdims=True)
            o_ref[:, cg:cg + POOL_GC] = (dyb * (y_raw * sc) * dsilu_gb).astype(BF16)
            dyraw = dypool * sc
            dyraw_n = jnp.where(last, 0.0, dmn_ref[:, cm:cm + POOL_GC] * _silu(next_ref[:, cg:cg + POOL_GC]) * sc)
            dpw_ref[g * POOL_GC:(g + 1) * POOL_GC, :] += lax.dot_general(pooled, dyraw.astype(BF16), TN,
                                                                         preferred_element_type=F32)
            dyraw_ext = jnp.concatenate([dyraw, dyraw_n], axis=0).astype(BF16)
            dpooled = lax.dot_general(dyraw_ext, pw, NT, preferred_element_type=F32)
            w = 2 << g
            tt = t0 + lax.broadcasted_iota(jnp.int32, (tq + POOL_HALO, 1), 0)
            s = dpooled / jnp.minimum(tt + 1, w).astype(F32)
            for step in range(g + 1):
                s = s + pltpu.roll(s, tq + POOL_HALO - (1 << step), 0)
            o_ref[:, cu:cu + POOL_GC] = (s[0:tq] - dpooled[0:tq]).astype(BF16)

    rev = lambda i: nt - 1 - i
    nxt = lambda i: (jnp.minimum((rev(i) + 1) * (tq // POOL_HALO), t // POOL_HALO - 1), 0)
    row = pl.BlockSpec((tq, D_MODEL), lambda i: (rev(i), 0))
    nxt_row = pl.BlockSpec((POOL_HALO, D_MODEL), nxt)
    square = _const_spec((D_MODEL, D_MODEL))
    return _fused_call(
        body, comm, (proj, proj, proj, gy, gy, y, y, mix, w_out, post, sinks, pool_w, pool_scale),
        name="layer0_bwd", grid=(nt,),
        out_shape=[jax.ShapeDtypeStruct((t, EVEN_IN), BF16), jax.ShapeDtypeStruct((8, 128), F32),
                   jax.ShapeDtypeStruct((POOL_GROUPS * POOL_GC, POOL_GC), F32), jax.ShapeDtypeStruct((1, POOL_WIDTH), F32),
                   jax.ShapeDtypeStruct((D_MODEL, D_MODEL), BF16), jax.ShapeDtypeStruct((1, D_MODEL), F32)],
        in_specs=[pl.BlockSpec((tq, EVEN_IN), lambda i: (rev(i), 0)),
                  pl.BlockSpec((BLOCK, EVEN_IN), lambda i: (jnp.maximum(rev(i) * nblk - 1, 0), 0)),
                  pl.BlockSpec((POOL_HALO, EVEN_IN), nxt),
                  row, nxt_row, row, nxt_row, row, square, _const_spec((1, D_MODEL)),
                  pl.BlockSpec(memory_space=pltpu.SMEM),
                  _const_spec((POOL_GROUPS, POOL_GC, POOL_GC)), _const_spec((1, POOL_WIDTH))],
        out_specs=[pl.BlockSpec((tq, EVEN_IN), lambda i: (rev(i), 0)),
                   pl.BlockSpec((8, 128), lambda i: (0, 0)),
                   pl.BlockSpec((POOL_GROUPS * POOL_GC, POOL_GC), lambda i: (0, 0)),
                   pl.BlockSpec((1, POOL_WIDTH), lambda i: (0, 0)), square, pl.BlockSpec((1, D_MODEL), lambda i: (0, 0))],
        scratch_shapes=[pltpu.VMEM((tq + BLOCK, 256), F32), pltpu.VMEM((tq + BLOCK, 256), F32),
                        pltpu.VMEM((BLOCK, 256), F32)] + ATTN_CONSTS
        + [pltpu.VMEM((D_MODEL, D_MODEL), F32), pltpu.VMEM((tq + POOL_HALO, D_MODEL), F32)],
        params=_params("arbitrary"))


CONV_RC = 32
CONV_CC = 128
CONV_CHAINS_FWD = 4
CONV_CHAINS_BWD = 2
CONV_UNROLL = 2


def _fill_shifted(s_ref, rows):
    for b in range(1, 8):
        s_ref[b, 0:rows - 8, :] = s_ref[0, b:b + rows - 8, :]


def _tap_blocks(s_ref, r, cols, lead):
    for b in range(8):
        taps = [(a, 8 * a + b - lead) for a in range(5) if 0 <= 8 * a + b - lead < CONV_K]
        span = 8 * max(a for a, _ in taps) + CONV_RC
        blk = s_ref[b, pl.ds(r, span), cols]
        for a, k in taps:
            yield k, blk[8 * a:8 * a + CONV_RC]


def _conv_taps(s_ref, w_ref, r, cols, lead, reverse, chains):
    accs = [None] * chains
    for n, (k, blk) in enumerate(_tap_blocks(s_ref, r, cols, lead)):
        kw = CONV_K - 1 - k if reverse else k
        term = blk * w_ref[kw:kw + 1, cols]
        accs[n % chains] = term if accs[n % chains] is None else accs[n % chains] + term
    while len(accs) > 1:
        accs = [a + b for a, b in zip(accs[0::2], accs[1::2])]
    return accs[0]


def _layer_norm_fwd(cf, lng, lnb):
    mu = jnp.mean(cf, axis=-1, keepdims=True)
    xc = cf - mu
    rstd = lax.rsqrt(jnp.mean(xc * xc, axis=-1, keepdims=True) + EPS)
    chat = xc * rstd
    return chat, rstd, chat * lng + lnb


def _layer1_fwd(proj, dw, dwb, lng, lnb, w_out, x_in, post, target, tt=512):
    t = proj.shape[0]
    lead = CONV_HALO - (CONV_K - 1)

    def body(main_ref, halo_ref, w_ref, b_ref, g_ref, lb_ref, wo_ref, x_ref, p_ref, t_ref,
             o_ref, c_ref, y_ref, dl_ref, l_ref, gs_ref):
        i = pl.program_id(0)
        hv = halo_ref[...]
        gs_ref[0, 0:CONV_HALO, :] = jnp.where(i == 0, 0.0, hv[:, GLU_A] * _sigmoid(hv[:, GLU_B]))
        gs_ref[0, CONV_HALO:CONV_HALO + tt, :] = main_ref[:, GLU_A] * _sigmoid(main_ref[:, GLU_B])
        _fill_shifted(gs_ref, tt + CONV_HALO)

        for c in range(D_MODEL // CONV_CC):
            cols = slice(c * CONV_CC, (c + 1) * CONV_CC)

            def chunk(j, carry):
                r = pl.multiple_of(j * CONV_RC, CONV_RC)
                c_ref[pl.ds(r, CONV_RC), cols] = _conv_taps(gs_ref, w_ref, r, cols, lead, False, CONV_CHAINS_FWD) + b_ref[:, cols]
                return carry
            lax.fori_loop(0, tt // CONV_RC, chunk, 0, unroll=CONV_UNROLL)

        _, _, cn = _layer_norm_fwd(c_ref[...], g_ref[...], lb_ref[...])
        o_ref[...] = (_silu(cn) * _silu(main_ref[:, GATE])).astype(BF16)

        d = _project_out(o_ref[...], wo_ref, x_ref, p_ref, y_ref) - t_ref[...]
        dl_ref[...] = d * (1.0 / D_MODEL)

        @pl.when(i == 0)
        def _():
            l_ref[...] = jnp.zeros_like(l_ref)

        l_ref[...] += 0.5 * jnp.sum(jnp.mean(d * d, axis=-1, keepdims=True))

    vec = _const_spec((1, D_MODEL))
    row = pl.BlockSpec((tt, D_MODEL), lambda i: (i, 0))
    f32_rows = jax.ShapeDtypeStruct((t, D_MODEL), F32)
    return pl.pallas_call(
        body, name="layer1_fwd", grid=(t // tt,),
        out_shape=[jax.ShapeDtypeStruct((t, D_MODEL), BF16), f32_rows, f32_rows, f32_rows,
                   jax.ShapeDtypeStruct((8, 128), F32)],
        in_specs=[pl.BlockSpec((tt, 3 * D_MODEL), lambda i: (i, 0)),
                  pl.BlockSpec((CONV_HALO, 3 * D_MODEL), lambda i: (jnp.maximum(i * (tt // CONV_HALO) - 1, 0), 0)),
                  _const_spec((CONV_K, D_MODEL)), vec, vec, vec,
                  _const_spec((D_MODEL, D_MODEL)), row, vec, row],
        out_specs=[row, row, row, row, pl.BlockSpec((8, 128), lambda i: (0, 0))],
        scratch_shapes=[pltpu.VMEM((8, tt + CONV_HALO, D_MODEL), F32)],
        compiler_params=_params("arbitrary"),
    )(proj, proj, dw, dwb, lng, lnb, w_out, x_in, post, target)


def _layer1_bwd(proj, cf, gy, y, z, w_out, post, dw, lng, lnb, comm=None, tt=256):
    t = proj.shape[0]
    nt = t // tt
    te = tt + CONV_HALO

    def body(main_ref, next_ref, cf_ref, cfn_ref, gy_ref, gyn_ref, y_ref, yn_ref, z_ref, wo_ref, po_ref,
             w_ref, g_ref, lb_ref,
             o_ref, ddw_ref, ddb_ref, dg_ref, dlb_ref, dwo16_ref, dpo_ref, ds_ref, glu_ref, sb_ref, dwo_ref):
        i = pl.program_id(0)

        @pl.when(i == 0)
        def _():
            ddw_ref[...] = jnp.zeros_like(ddw_ref)
            ddb_ref[...] = jnp.zeros_like(ddb_ref)
            dg_ref[...] = jnp.zeros_like(dg_ref)
            dlb_ref[...] = jnp.zeros_like(dlb_ref)

        dzv = _post_bwd_rows(jnp.concatenate([gy_ref[...], gyn_ref[...]], axis=0),
                             jnp.concatenate([y_ref[...], yn_ref[...]], axis=0), z_ref[...], tt, i == 0, i == nt - 1,
                             po_ref, wo_ref, dwo_ref, dwo16_ref, dpo_ref)
        dzv = jnp.concatenate([dzv[0:tt], jnp.where(i < nt - 1, dzv[tt:], 0.0)], axis=0)
        lng = g_ref[...]
        chat, rstd, cn = _layer_norm_fwd(jnp.concatenate([cf_ref[...], cfn_ref[...]], axis=0), lng, lb_ref[...])
        gate = jnp.concatenate([main_ref[:, GATE], next_ref[:, GATE]], axis=0)
        silu_cn, dsilu_cn = _silu_and_grad(cn)
        silu_gate, dsilu_gate = _silu_and_grad(gate)
        o_ref[:, GATE] = (dzv * silu_cn * dsilu_gate)[0:tt].astype(BF16)
        dcn = dzv * silu_gate * dsilu_cn
        dg_ref[...] += jnp.sum((dcn * chat)[0:tt], axis=0, keepdims=True)
        dlb_ref[...] += jnp.sum(dcn[0:tt], axis=0, keepdims=True)
        dchat = dcn * lng
        dcf = rstd * (dchat - jnp.mean(dchat, axis=-1, keepdims=True) - chat * jnp.mean(dchat * chat, axis=-1, keepdims=True))
        ddb_ref[...] += jnp.sum(dcf[0:tt], axis=0, keepdims=True)
        ds_ref[0, 0:te, :] = dcf
        ds_ref[0, te:, :] = jnp.zeros((8, D_MODEL), F32)
        _fill_shifted(ds_ref, te + 8)
        sb_ref[...] = _sigmoid(main_ref[:, GLU_B])
        glu_ref[...] = main_ref[:, GLU_A] * sb_ref[...]

        for c in range(D_MODEL // CONV_CC):
            cols = slice(c * CONV_CC, (c + 1) * CONV_CC)
            gcols = slice(D_MODEL + c * CONV_CC, D_MODEL + (c + 1) * CONV_CC)

            def chunk(j, carry):
                r = pl.multiple_of(j * CONV_RC, CONV_RC)
                dglu = _conv_taps(ds_ref, w_ref, r, cols, 0, True, CONV_CHAINS_BWD)
                sb = sb_ref[pl.ds(r, CONV_RC), cols]
                o_ref[pl.ds(r, CONV_RC), cols] = (dglu * sb).astype(BF16)
                o_ref[pl.ds(r, CONV_RC), gcols] = (dglu * glu_ref[pl.ds(r, CONV_RC), cols] * (1.0 - sb)).astype(BF16)
                return carry
            lax.fori_loop(0, tt // CONV_RC, chunk, 0, unroll=CONV_UNROLL)

            def taps(j, accs):
                r = pl.multiple_of(j * CONV_RC, CONV_RC)
                gl = glu_ref[pl.ds(r, CONV_RC), cols]
                new = list(accs)
                for m, blk in _tap_blocks(ds_ref, r, cols, 0):
                    prod = blk * gl
                    part = prod[0:8]
                    for q in range(1, CONV_RC // 8):
                        part = part + prod[8 * q:8 * q + 8]
                    new[m] = new[m] + part
                return tuple(new)
            accs = lax.fori_loop(0, tt // CONV_RC, taps, tuple(jnp.zeros((8, CONV_CC), F32) for _ in range(CONV_K)))
            for m in range(CONV_K):
                k = CONV_K - 1 - m
                ddw_ref[k:k + 1, cols] += jnp.sum(accs[m], axis=0, keepdims=True)

    vec = _const_spec((1, D_MODEL))
    vec_out = pl.BlockSpec((1, D_MODEL), lambda i: (0, 0))
    row = pl.BlockSpec((tt, D_MODEL), lambda i: (i, 0))
    nxt = lambda i: (jnp.minimum((i + 1) * (tt // CONV_HALO), t // CONV_HALO - 1), 0)
    nxt_row = pl.BlockSpec((CONV_HALO, D_MODEL), nxt)
    vec_f32 = jax.ShapeDtypeStruct((1, D_MODEL), F32)
    square = _const_spec((D_MODEL, D_MODEL))
    return _fused_call(
        body, comm, (proj, proj, cf, cf, gy, gy, y, y, z, w_out, post, dw, lng, lnb), name="layer1_bwd", grid=(nt,),
        out_shape=[jax.ShapeDtypeStruct((t, 3 * D_MODEL), BF16), jax.ShapeDtypeStruct((CONV_K, D_MODEL), F32),
                   vec_f32, vec_f32, vec_f32, jax.ShapeDtypeStruct((D_MODEL, D_MODEL), BF16), vec_f32],
        in_specs=[pl.BlockSpec((tt, 3 * D_MODEL), lambda i: (i, 0)),
                  pl.BlockSpec((CONV_HALO, 3 * D_MODEL), nxt),
                  row, nxt_row, row, nxt_row, row, nxt_row, row, square, vec,
                  _const_spec((CONV_K, D_MODEL)), vec, vec],
        out_specs=[pl.BlockSpec((tt, 3 * D_MODEL), lambda i: (i, 0)),
                   pl.BlockSpec((CONV_K, D_MODEL), lambda i: (0, 0)), vec_out, vec_out, vec_out, square, vec_out],
        scratch_shapes=[pltpu.VMEM((8, te + 8, D_MODEL), F32), pltpu.VMEM((tt, D_MODEL), F32),
                        pltpu.VMEM((tt, D_MODEL), F32), pltpu.VMEM((D_MODEL, D_MODEL), F32)],
        params=_params("arbitrary"))


def _piece_sums(tensors, place, name):
    counts = [len(parts) for parts in tensors]

    def body(p_ref, *refs):
        ins, outs = refs[:sum(counts)], refs[sum(counts):]
        pos = 0
        for count, o_ref in zip(counts, outs):
            acc = ins[pos][0].astype(F32)
            for part in ins[pos + 1:pos + count]:
                acc = acc + part[0].astype(F32)
            o_ref[0] = acc
            pos += count

    spec = lambda a, slot: pl.BlockSpec((1,) + a.shape[1:], lambda j, p_ref: (slot(p_ref), 0, 0))
    return pl.pallas_call(
        body, name=name,
        grid_spec=pltpu.PrefetchScalarGridSpec(
            num_scalar_prefetch=1, grid=(1,),
            in_specs=[spec(a, slot) for parts in tensors for a, slot in parts],
            out_specs=[pl.BlockSpec((1,) + parts[0][0].shape[1:], lambda j, p_ref: (p_ref[1], 0, 0))
                       for parts in tensors]),
        out_shape=[jax.ShapeDtypeStruct((2,) + parts[0][0].shape[1:], F32) for parts in tensors],
        compiler_params=_params("arbitrary"),
    )(place, *[a for parts in tensors for a, _ in parts])


def _direct_parts(own, recv):
    peer = lambda m: (lambda p: p[0] ^ m)
    return [(own, peer(0))] + [(recv, peer(m)) for m in range(1, N_DEV)]


def _share_with_sibling(halves, small, name):
    n = len(halves)

    def body(*refs):
        small_ref, outs, all_ref = refs[n], refs[n + 1:2 * n + 1], refs[2 * n + 1]
        send_sems, recv_sems, local_sem = refs[2 * n + 2:]
        x, y, c, _ = _place()
        me = 4 * x + 2 * y + c
        send = [pltpu.make_async_remote_copy(
            src_ref=outs[t].at[c], dst_ref=outs[t].at[c], send_sem=send_sems.at[t], recv_sem=recv_sems.at[t],
            device_id=(x, y, 1 - c), device_id_type=MESH_ID) for t in range(n)]
        recv = [pltpu.make_async_remote_copy(
            src_ref=outs[t].at[c], dst_ref=outs[t].at[1 - c], send_sem=send_sems.at[t], recv_sem=recv_sems.at[t],
            device_id=(x, y, 1 - c), device_id_type=MESH_ID) for t in range(n)]
        for m in range(1, N_DEV):
            px, py, pc = x ^ (m >> 2), y ^ ((m >> 1) & 1), c ^ (m & 1)
            sems = dict(send_sem=send_sems.at[n + m - 1], recv_sem=recv_sems.at[n + m - 1], device_id=(px, py, pc),
                        device_id_type=MESH_ID)
            send.append(pltpu.make_async_remote_copy(src_ref=small_ref, dst_ref=all_ref.at[me], **sems))
            recv.append(pltpu.make_async_remote_copy(src_ref=small_ref, dst_ref=all_ref.at[4 * px + 2 * py + pc], **sems))
        mine = pltpu.make_async_copy(small_ref, all_ref.at[me], local_sem)
        mine.start()
        for cp in send:
            cp.start()
        for cp in recv:
            cp.wait_recv()
        for cp in send:
            cp.wait_send()
        mine.wait()

    k = n + N_DEV - 1
    return pl.pallas_call(
        body, name=name,
        out_shape=[jax.ShapeDtypeStruct(h.shape, h.dtype) for h in halves]
        + [jax.ShapeDtypeStruct((N_DEV,) + small.shape, small.dtype)],
        in_specs=[ANY] * (n + 1), out_specs=[ANY] * (n + 1),
        input_output_aliases={t: t for t in range(n)},
        scratch_shapes=[pltpu.SemaphoreType.DMA((k,)), pltpu.SemaphoreType.DMA((k,)), pltpu.SemaphoreType.DMA],
    )(*halves, small)


def _sum8(parts, name):
    n = len(parts)

    def body(*refs):
        for p_ref, o_ref in zip(refs[:n], refs[n:]):
            acc = p_ref[0]
            for k in range(1, N_DEV):
                acc = acc + p_ref[k]
            o_ref[...] = acc

    whole = pl.BlockSpec(memory_space=pltpu.VMEM)
    return pl.pallas_call(
        body, name=name, out_shape=[jax.ShapeDtypeStruct(p.shape[1:], F32) for p in parts],
        in_specs=[whole] * n, out_specs=[whole] * n,
    )(*parts)


def _adamw(tensors, steps, name):
    n = len(tensors)
    views, specs = [], []
    for w, _, _, _ in tensors:
        cols = w.shape[-1]
        rows = w.size // cols
        if w.ndim >= 3 and w.shape[-2] == 1:
            assert steps == 1
            views.append((rows, 1, cols))
            specs.append(pl.BlockSpec((rows, 1, cols), lambda i: (0, 0, 0)))
        else:
            views.append((rows, cols))
            specs.append(pl.BlockSpec((rows // steps, cols), lambda i: (i, 0)))

    def body(*refs):
        for t in range(n):
            w_ref, g_ref, m_ref, v_ref = refs[4 * t:4 * t + 4]
            d_ref, nm_ref, nv_ref = refs[4 * n + 3 * t:4 * n + 3 * t + 3]
            gv = g_ref[...]
            mn = ADAM_B1 * m_ref[...] + (1.0 - ADAM_B1) * gv
            vn = ADAM_B2 * v_ref[...] + (1.0 - ADAM_B2) * (gv * gv)
            m_hat = mn / (1.0 - ADAM_B1 ** ADAM_STEP)
            v_hat = vn / (1.0 - ADAM_B2 ** ADAM_STEP)
            d_ref[...] = -ADAM_LR * (m_hat / (jnp.sqrt(v_hat) + ADAM_EPS) + ADAM_WD * w_ref[...])
            nm_ref[...] = mn
            nv_ref[...] = vn

    outs = pl.pallas_call(
        body, name=name, grid=(steps,),
        out_shape=[jax.ShapeDtypeStruct(view, F32) for view in views for _ in range(3)],
        in_specs=[spec for spec in specs for _ in range(4)], out_specs=[spec for spec in specs for _ in range(3)],
        compiler_params=_params("parallel"),
    )(*[a.reshape(view) for group, view in zip(tensors, views) for a in group])
    return [tuple(o.reshape(group[0].shape) for o in outs[3 * t:3 * t + 3]) for t, group in enumerate(tensors)]


SMALL_ROWS = 832


def _pack_small(g):
    parts = [g["loss"], g["pre1"].reshape(8, 128), g["post0"].reshape(8, 128),
             g["post1"].reshape(8, 128), g["sinks"], jnp.pad(g["pool_scale"].reshape(4, 128), ((0, 4), (0, 0))),
             g["pool_w"], g["dw"].reshape(248, 128), g["dwb"].reshape(8, 128), g["lng"].reshape(8, 128),
             g["lnb"].reshape(8, 128)]
    assert sum(p.shape[0] for p in parts) == SMALL_ROWS
    return jnp.concatenate(parts, axis=0)


def _unpack_small(s):
    out, r = {}, 0
    for key, rows, shape in (("loss", 8, (8, 128)), ("pre1", 8, (1, D_MODEL)), ("post", 16, (2, D_MODEL)),
                             ("sinks", 8, (8, 128)),
                             ("pool_scale", 4, (1, POOL_WIDTH)), ("pad", 4, (4, 128)), ("pool_w", 512, (1, 4, 128, 128)),
                             ("dw", 248, (CONV_K, D_MODEL)), ("dwb", 8, (1, D_MODEL)), ("lng", 8, (1, D_MODEL)),
                             ("lnb", 8, (1, D_MODEL))):
        out[key] = s[r:r + rows].reshape(shape)
        r += rows
    return out


def kernel(x, pre_norm, post_norm, a_w_in, a_sinks, b_pool_w, b_pool_scale, ab_w_out, c_w_in, c_dw_w, c_dw_b, c_ln_g, c_ln_b, c_w_out, loss_target, m_pre_norm, m_post_norm, m_a_w_in, m_a_sinks, m_b_pool_w, m_b_pool_scale, m_ab_w_out, m_c_w_in, m_c_dw_w, m_c_dw_b, m_c_ln_g, m_c_ln_b, m_c_w_out, v_pre_norm, v_post_norm, v_a_w_in, v_a_sinks, v_b_pool_w, v_b_pool_scale, v_ab_w_out, v_c_w_in, v_c_dw_w, v_c_dw_b, v_c_ln_g, v_c_ln_b, v_c_w_out):
    ix, iy = lax.axis_index("x"), lax.axis_index("y")
    chip_cols = (2 * ix + iy) * 256

    pad8 = lambda v: jnp.pad(v, ((0, -v.shape[0] % 8), (0, 0)))
    vec_shard = jnp.concatenate([pad8(c_dw_w.reshape(CONV_K, 256)), pad8(c_dw_b), pad8(c_ln_g), pad8(c_ln_b),
                                 jnp.zeros((8, 256), F32)], axis=0)
    x0, target = x[0], loss_target[0]
    pre0, pre1 = pre_norm[0:1], pre_norm[1:2]
    post0, post1 = post_norm[0:1], post_norm[1:2]
    pool_w = b_pool_w[0]

    h0, (wa_t,) = _norm_cast(x0, pre0, "norm0", comm=_Gather([a_w_in[0].T.astype(BF16)], halve=True))
    wa_t = wa_t.reshape(EVEN_IN, D_MODEL)
    proj0, (w_ab,) = _matmul_nt(h0, wa_t, "proj0_fwd", comm=_Gather([ab_w_out[0].astype(BF16)], halve=True))
    w_ab = w_ab.reshape(D_MODEL, D_MODEL)
    (mix0, y0, x1), (wc_t, w_c, vecs) = _layer0_fwd(
        proj0, a_sinks, pool_w, b_pool_scale, w_ab, x0, post0,
        comm=_Gather([c_w_in[0].T.astype(BF16), c_w_out[0].astype(BF16), vec_shard], halve=True))
    wc_t = wc_t.reshape(3 * D_MODEL, D_MODEL)
    w_c = w_c.reshape(D_MODEL, D_MODEL)
    vecs = vecs.reshape(4, 64, 256).transpose(1, 0, 2).reshape(64, D_MODEL)
    dw, dwb, lng, lnb = vecs[0:CONV_K], vecs[32:33], vecs[40:41], vecs[48:49]
    proj1, _ = _norm_matmul(x1, pre1, wc_t, "proj1_fwd")
    z1, cf1, y1, g2, loss = _layer1_fwd(proj1, dw, dwb, lng, lnb, w_c, x1, post1, target)

    pieces = lambda m: m.reshape(N_DEV, m.shape[0] // N_DEV, D_MODEL)
    (dproj1, d_dw, d_dwb, d_lng, d_lnb, d_wc, d_post1), _ = _layer1_bwd(proj1, cf1, g2, y1, z1, w_c, post1, dw, lng, lnb)
    (g1, d_wct, d_pre1), _ = _pre_bwd(dproj1, wc_t, x1, pre1, g2, "proj1_bwd")
    (dproj0, d_sinks, d_pw, d_ps, d_wab, d_post0), (r_wc, r_wct) = _layer0_bwd(
        proj0, g1, y0, mix0, w_ab, post0, a_sinks, pool_w, b_pool_scale,
        comm=_Scatter([pieces(d_wc), pieces(d_wct)]))
    g = dict(loss=loss, pre1=d_pre1, post0=d_post0, post1=d_post1, sinks=d_sinks, pool_w=d_pw, pool_scale=d_ps,
             dw=d_dw, dwb=d_dwb, lng=d_lng, lnb=d_lnb)
    d_wat, (small8, r_wab) = _proj_dw(dproj0, h0, "proj0_dw",
                                      comm=_Comms(_Gather([_pack_small(g)], halve=False), _Scatter([pieces(d_wab)])))
    sent = _scatter_start(pieces(d_wat), "scatter_a_start")
    (gx, d_pre0), _ = _proj_dx(dproj0, wa_t, x0, pre0 + sent[4][0:1, 0:1], g1, "proj0_dx")
    own_wat, r_wat = _scatter_wait(*sent[:4], d_pre0, "scatter_a_wait")

    ic = lax.axis_index("c")
    me = 4 * ix + 2 * iy + ic
    place = jnp.stack([me, ic]).astype(jnp.int32)
    parts = [_direct_parts(own_wat, r_wat), _direct_parts(pieces(d_wab), r_wab), _direct_parts(pieces(d_wct), r_wct),
             _direct_parts(pieces(d_wc), r_wc)]
    halves = _piece_sums(parts, place, "grad_sums")
    *shared, pre0_8 = _share_with_sibling(halves, d_pre0.reshape(8, 128), "grad_share")
    g_wa_t, g_wab, g_wc_t, g_wc = [h.reshape(2 * h.shape[1], D_MODEL) for h in shared]
    g_c_w_in = g_wc_t.T[None]
    g_ab_w_out = g_wab[None]
    g_c_w_out = g_wc[None]

    small, pre0 = _sum8([small8, pre0_8], "small_sums")
    s = _unpack_small(small)
    layer = lax.broadcasted_iota(jnp.int32, (2, D_MODEL), 0)
    g_pre = jnp.where(layer == 0, pre0.reshape(1, D_MODEL), s["pre1"])
    g_post = s["post"]
    g_sinks = s["sinks"][:, 0].reshape(1, 8)
    g_pool_w, g_pool_scale = s["pool_w"], s["pool_scale"]
    g_dw = lax.dynamic_slice_in_dim(s["dw"], chip_cols, 256, axis=1).reshape(1, CONV_K, 1, 256)
    g_dwb = lax.dynamic_slice_in_dim(s["dwb"], chip_cols, 256, axis=1)
    g_lng = lax.dynamic_slice_in_dim(s["lng"], chip_cols, 256, axis=1)
    g_lnb = lax.dynamic_slice_in_dim(s["lnb"], chip_cols, 256, axis=1)

    turn = lambda a: jnp.swapaxes(a, 1, 2)
    a_w_in, m_a_w_in, v_a_w_in = turn(a_w_in), turn(m_a_w_in), turn(v_a_w_in)
    grads = [g_pre, g_post, g_wa_t[None], g_sinks, g_pool_w, g_pool_scale, g_ab_w_out, g_c_w_in, g_dw, g_dwb, g_lng,
             g_lnb, g_c_w_out]
    weights = [pre_norm, post_norm, a_w_in, a_sinks, b_pool_w, b_pool_scale, ab_w_out, c_w_in, c_dw_w, c_dw_b, c_ln_g,
               c_ln_b, c_w_out]
    moms = [m_pre_norm, m_post_norm, m_a_w_in, m_a_sinks, m_b_pool_w, m_b_pool_scale, m_ab_w_out, m_c_w_in, m_c_dw_w,
            m_c_dw_b, m_c_ln_g, m_c_ln_b, m_c_w_out]
    vars_ = [v_pre_norm, v_post_norm, v_a_w_in, v_a_sinks, v_b_pool_w, v_b_pool_scale, v_ab_w_out, v_c_w_in, v_c_dw_w,
             v_c_dw_b, v_c_ln_g, v_c_ln_b, v_c_w_out]
    tensors = list(zip(weights, grads, moms, vars_))
    matrices = (2, 6, 7, 12)
    others = [k for k in range(len(tensors)) if k not in matrices]
    updates = dict(zip(matrices, _adamw([tensors[k] for k in matrices], 4, "adamw_matrices")))
    updates.update(zip(others, _adamw([tensors[k] for k in others], 1, "adamw_small")))
    deltas, new_m, new_v = ([updates[k][j] for k in range(len(tensors))] for j in range(3))
    for outs in (grads, deltas, new_m, new_v):
        outs[2] = turn(outs[2])
    return (s["loss"][0, 0], gx[None], *grads, *deltas, *new_m, *new_v)
```

```python
import jax
import jax.numpy as jnp
from jax import lax
from jax.experimental import pallas as pl
from jax.experimental.pallas import tpu as pltpu

F32 = jnp.float32
BF16 = jnp.bfloat16

D_MODEL = 1024
EPS = 1e-6
NEG = -1e30
HEAD_DIM = 64
GROUP = 4
KV_HEADS = 2
BLOCK = 128
EVEN_IN = 2304
ATTN_WIDTH = 512
POOL_WIDTH = 512
COL_Q, COL_K, COL_GA, COL_U, COL_GB = 0, 512, 768, 1280, 1792
POOL_GROUPS = 4
POOL_GC = 128
POOL_HALO = 16
CONV_K = 31
CONV_HALO = 32
GLU_A = slice(0, D_MODEL)
GLU_B = slice(D_MODEL, 2 * D_MODEL)
GATE = slice(2 * D_MODEL, 3 * D_MODEL)
N_DEV = 8

ADAM_LR = 0.001
ADAM_B1 = 0.9
ADAM_B2 = 0.999
ADAM_EPS = 1e-08
ADAM_WD = 0.01
ADAM_STEP = 10

VMEM_LIMIT_BYTES = 56 * 1024 * 1024

NT = (((1,), (1,)), ((), ()))
TN = (((0,), (0,)), ((), ()))
MESH_ID = pl.DeviceIdType.MESH


def _params(*sem):
    return pltpu.CompilerParams(dimension_semantics=sem, vmem_limit_bytes=VMEM_LIMIT_BYTES)


def _const_spec(shape):
    nd = len(shape)
    return pl.BlockSpec(shape, lambda *_: (0,) * nd, pipeline_mode=pl.Buffered(1))


def _sigmoid(v):
    return 0.5 * jnp.tanh(0.5 * v) + 0.5


def _silu(v):
    h = 0.5 * v
    return h * jnp.tanh(h) + h


def _silu_and_grad(v):
    s = _sigmoid(v)
    silu = v * s
    return silu, s + silu * (1.0 - s)


ANY = pl.BlockSpec(memory_space=pl.ANY)


def _place():
    x, y, c = lax.axis_index("x"), lax.axis_index("y"), lax.axis_index("c")
    chips = [(1 - x, y), (x, 1 - y), (1 - x, 1 - y)]
    return x, y, c, chips


class _Gather:
    def __init__(self, blocks, halve):
        self.ins = list(blocks)
        self.halve = halve
        self.n = n = len(blocks)
        self.shapes = [((b.shape[0] // 2) if halve else b.shape[0], b.shape[1]) for b in blocks]
        self.out_shape = [jax.ShapeDtypeStruct((N_DEV, r, cols), b.dtype) for (r, cols), b in zip(self.shapes, blocks)]
        self.scratch = [pltpu.SemaphoreType.DMA((7 * n,)), pltpu.SemaphoreType.DMA((7 * n,)),
                        pltpu.SemaphoreType.DMA((n,))]

    def _copies(self, ins, outs, sems):
        send_sems, recv_sems, local_sems = sems
        x, y, c, chips = _place()
        me, sibling = (x, y, c), (x, y, 1 - c)

        def piece(t, px, py, pc):
            return outs[t].at[4 * px + 2 * py + pc]

        def own(t):
            return ins[t].at[pl.ds(c * self.shapes[t][0], self.shapes[t][0])] if self.halve else ins[t]

        def copy(t, k, block, to, src=None):
            return pltpu.make_async_remote_copy(
                src_ref=piece(t, *block) if src is None else src, dst_ref=piece(t, *block),
                send_sem=send_sems.at[7 * t + k], recv_sem=recv_sems.at[7 * t + k],
                device_id=to, device_id_type=MESH_ID)

        rng = range(self.n)
        return dict(
            mine=[pltpu.make_async_copy(own(t), piece(t, *me), local_sems.at[t]) for t in rng],
            first=[copy(t, 0, me, sibling, src=own(t)) for t in rng]
            + [copy(t, 1 + j, me, (*chip, c), src=own(t)) for t in rng for j, chip in enumerate(chips)],
            landed=[copy(t, 1 + j, (*chip, c), me) for j, chip in enumerate(chips) for t in rng],
            passed=[copy(t, 4 + j, (*chip, c), sibling) for j, chip in enumerate(chips) for t in rng],
            from_sibling=[copy(t, 0, sibling, me) for t in rng]
            + [copy(t, 4 + j, (*chip, 1 - c), me) for t in rng for j, chip in enumerate(chips)])

    def start(self, ins, outs, sems):
        d = self._copies(ins, outs, sems)
        for cp in d["mine"] + d["first"]:
            cp.start()

    def middle(self, ins, outs, sems):
        d = self._copies(ins, outs, sems)
        for got, fwd in zip(d["landed"], d["passed"]):
            got.wait_recv()
            fwd.start()

    def finish(self, ins, outs, sems):
        d = self._copies(ins, outs, sems)
        for cp in d["from_sibling"]:
            cp.wait_recv()
        for cp in d["first"] + d["passed"]:
            cp.wait_send()
        for cp in d["mine"]:
            cp.wait()


class _Scatter:
    def __init__(self, tensors):
        self.ins = list(tensors)
        self.n = n = len(tensors)
        self.out_shape = [jax.ShapeDtypeStruct(t.shape, t.dtype) for t in tensors]
        self.scratch = [pltpu.SemaphoreType.DMA((7 * n,)), pltpu.SemaphoreType.DMA((7 * n,))]

    def _copies(self, ins, outs, sems):
        send_sems, recv_sems = sems
        x, y, c, _ = _place()
        me = 4 * x + 2 * y + c
        sends, recvs = [], []
        for t in range(self.n):
            for m in range(1, N_DEV):
                px, py, pc = x ^ (m >> 2), y ^ ((m >> 1) & 1), c ^ (m & 1)
                q = 4 * px + 2 * py + pc
                sems_k = dict(send_sem=send_sems.at[7 * t + m - 1], recv_sem=recv_sems.at[7 * t + m - 1],
                              device_id=(px, py, pc), device_id_type=MESH_ID)
                sends.append(pltpu.make_async_remote_copy(src_ref=ins[t].at[q], dst_ref=outs[t].at[me], **sems_k))
                recvs.append(pltpu.make_async_remote_copy(src_ref=ins[t].at[me], dst_ref=outs[t].at[q], **sems_k))
        return sends, recvs

    def start(self, ins, outs, sems):
        for cp in self._copies(ins, outs, sems)[0]:
            cp.start()

    def middle(self, ins, outs, sems):
        pass

    def finish(self, ins, outs, sems):
        sends, recvs = self._copies(ins, outs, sems)
        for cp in recvs:
            cp.wait_recv()
        for cp in sends:
            cp.wait_send()


class _Comms:
    def __init__(self, *comms):
        self.comms = comms
        self.ins = [a for c in comms for a in c.ins]
        self.out_shape = [s for c in comms for s in c.out_shape]
        self.scratch = [s for c in comms for s in c.scratch]

    def _each(self, phase, ins, outs, sems):
        i = o = s = 0
        for c in self.comms:
            ni, no, ns = len(c.ins), len(c.out_shape), len(c.scratch)
            getattr(c, phase)(ins[i:i + ni], outs[o:o + no], sems[s:s + ns])
            i, o, s = i + ni, o + no, s + ns

    def start(self, ins, outs, sems):
        self._each("start", ins, outs, sems)

    def middle(self, ins, outs, sems):
        self._each("middle", ins, outs, sems)

    def finish(self, ins, outs, sems):
        self._each("finish", ins, outs, sems)


def _run_comm(comm, name):
    n = len(comm.ins)

    def body(*refs):
        parts = refs[:n], refs[n:2 * n], refs[2 * n:]
        comm.start(*parts)
        comm.middle(*parts)
        comm.finish(*parts)

    return pl.pallas_call(body, name=name, out_shape=comm.out_shape, in_specs=[ANY] * n, out_specs=[ANY] * n,
                          scratch_shapes=comm.scratch)(*comm.ins)


HBM_SPEC = pl.BlockSpec(memory_space=pltpu.HBM)
SEM_SPEC = pl.BlockSpec(memory_space=pltpu.SEMAPHORE)
DATAFLOW = pltpu.SideEffectType.DATAFLOW_SIDE_EFFECTING


def _scatter_copies(own_ref, land_ref, send_sems, recv_sems):
    x, y, c, _ = _place()
    me = 4 * x + 2 * y + c
    pairs = []
    for m in range(1, N_DEV):
        px, py, pc = x ^ (m >> 2), y ^ ((m >> 1) & 1), c ^ (m & 1)
        q = 4 * px + 2 * py + pc
        sems = dict(send_sem=send_sems.at[m - 1], recv_sem=recv_sems.at[m - 1], device_id=(px, py, pc),
                    device_id_type=MESH_ID)
        pairs.append((pltpu.make_async_remote_copy(src_ref=own_ref.at[q], dst_ref=land_ref.at[me], **sems),
                      pltpu.make_async_remote_copy(src_ref=own_ref.at[me], dst_ref=land_ref.at[q], **sems)))
    return pairs


def _scatter_start(own, name):
    def body(own_ref, land_ref, send_sems, recv_sems, own_thru, land_thru, token):
        for send, _ in _scatter_copies(own_ref, land_ref, send_sems, recv_sems):
            send.start()
        token[...] = jnp.zeros_like(token)

    buf = pltpu.HBM(own.shape, own.dtype)
    return pl.pallas_call(
        body, name=name,
        out_shape=(pltpu.SemaphoreType.DMA((N_DEV - 1,)), pltpu.SemaphoreType.DMA((N_DEV - 1,)), buf, buf,
                   jax.ShapeDtypeStruct((8, 128), F32)),
        in_specs=(HBM_SPEC, HBM_SPEC),
        out_specs=(SEM_SPEC, SEM_SPEC, HBM_SPEC, HBM_SPEC, pl.BlockSpec(memory_space=pltpu.VMEM)),
        input_output_aliases={0: 2, 1: 3},
        compiler_params=pltpu.CompilerParams(has_side_effects=DATAFLOW),
    )(pltpu.with_memory_space_constraint(own, pltpu.HBM),
      pltpu.with_memory_space_constraint(lax.empty(own.shape, own.dtype), pltpu.HBM))


def _scatter_wait(send_sems, recv_sems, own_thru, land_thru, after, name):
    def body(own_ref, land_ref, send_sems, recv_sems, after_ref, own_out, land_out):
        for send, recv in _scatter_copies(own_ref, land_ref, send_sems, recv_sems):
            send.wait_send()
            recv.wait_recv()

    buf = pltpu.HBM(own_thru.shape, own_thru.dtype)
    return pl.pallas_call(
        body, name=name, out_shape=(buf, buf),
        in_specs=(HBM_SPEC, HBM_SPEC, SEM_SPEC, SEM_SPEC, ANY), out_specs=(HBM_SPEC, HBM_SPEC),
        input_output_aliases={0: 0, 1: 1},
        compiler_params=pltpu.CompilerParams(has_side_effects=DATAFLOW),
    )(own_thru, land_thru, send_sems, recv_sems, after)


def _fused_call(body, comm, args, *, name, grid, out_shape, in_specs, out_specs, scratch_shapes=(), params):
    single = not isinstance(out_shape, (list, tuple))
    out_shape = [out_shape] if single else list(out_shape)
    out_specs = [out_specs] if single else list(out_specs)
    if comm is None:
        res = pl.pallas_call(body, name=name, grid=grid, out_shape=out_shape, in_specs=in_specs, out_specs=out_specs,
                             scratch_shapes=list(scratch_shapes), compiler_params=params)(*args)
        return (res[0] if single else res), []
    n_in, n_out, n_scr = len(in_specs), len(out_shape), len(scratch_shapes)
    c_in, c_out = len(comm.ins), len(comm.out_shape)
    steps = grid[0]

    def fused(*refs):
        pos = 0
        groups = []
        for size in (n_in, c_in, n_out, c_out, n_scr, len(comm.scratch)):
            groups.append(refs[pos:pos + size])
            pos += size
        ins, c_ins, outs, c_outs, scr, c_sems = groups
        i = pl.program_id(0)

        @pl.when(i == 0)
        def _():
            comm.start(c_ins, c_outs, c_sems)

        @pl.when(i == steps // 2)
        def _():
            comm.middle(c_ins, c_outs, c_sems)

        body(*ins, *outs, *scr)

        @pl.when(i == steps - 1)
        def _():
            comm.finish(c_ins, c_outs, c_sems)

    res = pl.pallas_call(
        fused, name=name, grid=grid, out_shape=out_shape + list(comm.out_shape),
        in_specs=list(in_specs) + [ANY] * c_in, out_specs=out_specs + [ANY] * c_out,
        scratch_shapes=list(scratch_shapes) + list(comm.scratch), compiler_params=params)(*args, *comm.ins)
    main = res[:n_out]
    return (main[0] if single else main), list(res[n_out:])


def _norm_matmul(x, gain, wt, name, comm=None, tm=1024):
    t, n = x.shape[0], wt.shape[0]

    def body(x_ref, g_ref, wt_ref, o_ref):
        xv = x_ref[...]
        r = lax.rsqrt(jnp.mean(xv * xv, axis=-1, keepdims=True) + EPS)
        h = (xv * r * g_ref[...]).astype(BF16)
        o_ref[...] = lax.dot_general(h, wt_ref[...], NT, preferred_element_type=F32)

    return _fused_call(
        body, comm, (x, gain, wt), name=name, grid=(t // tm,),
        out_shape=jax.ShapeDtypeStruct((t, n), F32),
        in_specs=[pl.BlockSpec((tm, D_MODEL), lambda i: (i, 0)), _const_spec((1, D_MODEL)), _const_spec((n, D_MODEL))],
        out_specs=pl.BlockSpec((tm, n), lambda i: (i, 0)),
        params=_params("arbitrary"))


def _project_out(a, w_ref, x_ref, p_ref, y_ref):
    y = jnp.dot(a, w_ref[...], preferred_element_type=F32)
    y_ref[...] = y
    ry = lax.rsqrt(jnp.mean(y * y, axis=-1, keepdims=True) + EPS)
    return x_ref[...] + (y * ry) * p_ref[...]


def _post_bwd_rows(g, y, a, n_own, first, last, p_ref, w_ref, dw_ref, dw16_ref, dp_ref):
    @pl.when(first)
    def _():
        dw_ref[...] = jnp.zeros_like(dw_ref)
        dp_ref[...] = jnp.zeros_like(dp_ref)

    ry = lax.rsqrt(jnp.mean(y * y, axis=-1, keepdims=True) + EPS)
    nv = y * ry
    dp_ref[...] += jnp.sum((g * nv)[0:n_own], axis=0, keepdims=True)
    dn = g * p_ref[...]
    dy = (ry * (dn - nv * jnp.mean(dn * nv, axis=-1, keepdims=True))).astype(BF16)
    dw_ref[...] += lax.dot_general(a, dy[0:n_own], TN, preferred_element_type=F32)

    @pl.when(last)
    def _():
        dw16_ref[...] = dw_ref[...].astype(BF16)

    return lax.dot_general(dy, w_ref[...], NT, preferred_element_type=F32)


def _pre_bwd(dproj, wt, x_in, pre, g, name, comm=None, tm=512):
    t, n = dproj.shape
    steps = t // tm

    def body(dp_ref, wt_ref, x_ref, pre_ref, g_ref, dx_ref, dwt16_ref, dpre_ref, dwt_ref):
        @pl.when(pl.program_id(0) == 0)
        def _():
            dwt_ref[...] = jnp.zeros_like(dwt_ref)
            dpre_ref[...] = jnp.zeros_like(dpre_ref)

        dpv = dp_ref[...]
        dh = jnp.dot(dpv, wt_ref[...], preferred_element_type=F32)
        xv = x_ref[...]
        r = lax.rsqrt(jnp.mean(xv * xv, axis=-1, keepdims=True) + EPS)
        xn = xv * r
        pv = pre_ref[...]
        dpre_ref[...] += jnp.sum(dh * xn, axis=0, keepdims=True)
        dxn = dh * pv
        dx_ref[...] = g_ref[...] + r * (dxn - xn * jnp.mean(dxn * xn, axis=-1, keepdims=True))
        h = (xn * pv).astype(BF16)
        dwt_ref[...] += lax.dot_general(dpv, h, TN, preferred_element_type=F32)

        @pl.when(pl.program_id(0) == steps - 1)
        def _():
            dwt16_ref[...] = dwt_ref[...].astype(BF16)

    row = pl.BlockSpec((tm, D_MODEL), lambda i: (i, 0))
    return _fused_call(
        body, comm, (dproj, wt, x_in, pre, g), name=name, grid=(steps,),
        out_shape=[jax.ShapeDtypeStruct((t, D_MODEL), F32), jax.ShapeDtypeStruct((n, D_MODEL), BF16),
                   jax.ShapeDtypeStruct((1, D_MODEL), F32)],
        in_specs=[pl.BlockSpec((tm, n), lambda i: (i, 0)), _const_spec((n, D_MODEL)), row, _const_spec((1, D_MODEL)), row],
        out_specs=[row, _const_spec((n, D_MODEL)), pl.BlockSpec((1, D_MODEL), lambda i: (0, 0))],
        scratch_shapes=[pltpu.VMEM((n, D_MODEL), F32)],
        params=_params("arbitrary"))


def _proj_dw(dproj, x_in, pre, name, comm=None, tm=1024):
    t, n = dproj.shape
    steps = t // tm

    def body(dp_ref, x_ref, pre_ref, dwt16_ref, dwt_ref):
        @pl.when(pl.program_id(0) == 0)
        def _():
            dwt_ref[...] = jnp.zeros_like(dwt_ref)

        xv = x_ref[...]
        r = lax.rsqrt(jnp.mean(xv * xv, axis=-1, keepdims=True) + EPS)
        h = (xv * r * pre_ref[...]).astype(BF16)
        dwt_ref[...] += lax.dot_general(dp_ref[...], h, TN, preferred_element_type=F32)

        @pl.when(pl.program_id(0) == steps - 1)
        def _():
            dwt16_ref[...] = dwt_ref[...].astype(BF16)

    return _fused_call(
        body, comm, (dproj, x_in, pre), name=name, grid=(steps,),
        out_shape=jax.ShapeDtypeStruct((n, D_MODEL), BF16),
        in_specs=[pl.BlockSpec((tm, n), lambda i: (i, 0)), pl.BlockSpec((tm, D_MODEL), lambda i: (i, 0)),
                  _const_spec((1, D_MODEL))],
        out_specs=pl.BlockSpec((n, D_MODEL), lambda i: (0, 0)),
        scratch_shapes=[pltpu.VMEM((n, D_MODEL), F32)],
        params=_params("arbitrary"))


def _proj_dx(dproj, wt, x_in, pre, g, name, comm=None, tm=1024):
    t, n = dproj.shape

    def body(dp_ref, wt_ref, x_ref, pre_ref, g_ref, dx_ref, dpre_ref):
        @pl.when(pl.program_id(0) == 0)
        def _():
            dpre_ref[...] = jnp.zeros_like(dpre_ref)

        dh = jnp.dot(dp_ref[...], wt_ref[...], preferred_element_type=F32)
        xv = x_ref[...]
        r = lax.rsqrt(jnp.mean(xv * xv, axis=-1, keepdims=True) + EPS)
        xn = xv * r
        dpre_ref[...] += jnp.sum(dh * xn, axis=0, keepdims=True)
        dxn = dh * pre_ref[...]
        dx_ref[...] = g_ref[...] + r * (dxn - xn * jnp.mean(dxn * xn, axis=-1, keepdims=True))

    row = pl.BlockSpec((tm, D_MODEL), lambda i: (i, 0))
    return _fused_call(
        body, comm, (dproj, wt, x_in, pre, g), name=name, grid=(t // tm,),
        out_shape=[jax.ShapeDtypeStruct((t, D_MODEL), F32), jax.ShapeDtypeStruct((1, D_MODEL), F32)],
        in_specs=[pl.BlockSpec((tm, n), lambda i: (i, 0)), _const_spec((n, D_MODEL)), row, _const_spec((1, D_MODEL)), row],
        out_specs=[row, pl.BlockSpec((1, D_MODEL), lambda i: (0, 0))],
        params=_params("arbitrary"))


def _group_masks():
    lane = lax.broadcasted_iota(jnp.int32, (1, GROUP * HEAD_DIM), 1)
    return [lane // HEAD_DIM == g for g in range(GROUP)]


def _stack_groups(v, masks, scale=1.0):
    return jnp.concatenate([v * jnp.where(m, scale, 0.0) for m in masks], axis=0)


def _unstack_groups(v, masks):
    out = v[(GROUP - 1) * BLOCK:GROUP * BLOCK]
    for g in range(GROUP - 2, -1, -1):
        out = jnp.where(masks[g], v[g * BLOCK:(g + 1) * BLOCK], out)
    return out


def _repeat_head(kv2, kvh):
    first = lax.broadcasted_iota(jnp.int32, kv2.shape, 1) < HEAD_DIM
    rolled = pltpu.roll(kv2, HEAD_DIM, 1)
    one = jnp.where(first, kv2, rolled) if kvh == 0 else jnp.where(first, rolled, kv2)
    return jnp.concatenate([one, one], axis=1)


def _fold_head(v4):
    a = v4[:, 0:128] + v4[:, 128:256]
    return a + pltpu.roll(a, HEAD_DIM, 1)


ATTN_CONSTS = [pltpu.VMEM((KV_HEADS, GROUP * BLOCK, 2 * BLOCK), F32)]


def _fill_attn_bias(bias_ref):
    row = lax.broadcasted_iota(jnp.int32, (GROUP * BLOCK, 2 * BLOCK), 0)
    col = lax.broadcasted_iota(jnp.int32, (GROUP * BLOCK, 2 * BLOCK), 1)
    dist = (row % BLOCK) + BLOCK - col
    band = (dist >= 0) & (dist < BLOCK)
    rb = lax.broadcasted_iota(jnp.int32, (GROUP * BLOCK, 1), 0) // BLOCK
    for kvh in range(KV_HEADS):
        slope = jnp.zeros((GROUP * BLOCK, 1), F32)
        for g in range(GROUP):
            slope = jnp.where(rb == g, 2.0 ** (-(kvh * GROUP + g + 1)), slope)
        bias_ref[kvh] = jnp.where(band, -slope * dist.astype(F32), NEG)


def _row_sinks(kvh, sink_ref):
    rb = lax.broadcasted_iota(jnp.int32, (GROUP * BLOCK, 1), 0) // BLOCK
    sink = jnp.zeros((GROUP * BLOCK, 1), F32)
    for g in range(GROUP):
        sink = jnp.where(rb == g, sink_ref[0, kvh * GROUP + g], sink)
    return sink


def _attn_probs(qk, k4, bias, sink, no_past, masks):
    qs = _stack_groups(qk, masks, HEAD_DIM ** -0.5).astype(BF16)
    s = lax.dot_general(qs, k4, NT, preferred_element_type=F32) + bias
    s = jnp.concatenate([jnp.where(no_past, NEG, s[:, 0:BLOCK]), s[:, BLOCK:]], axis=1)
    mx = jnp.maximum(jnp.max(s, axis=-1, keepdims=True), sink)
    e = jnp.exp(s - mx)
    es = jnp.exp(sink - mx)
    inv = 1.0 / (jnp.sum(e, axis=-1, keepdims=True) + es)
    return qs, e * inv, es * inv


def _pool_forward(u_ext, g, t0):
    n = u_ext.shape[0] - POOL_HALO
    s = u_ext
    for step in range(g + 1):
        s = s + pltpu.roll(s, 1 << step, 0)
    w = 2 << g
    t = t0 + lax.broadcasted_iota(jnp.int32, (n, 1), 0)
    cnt = jnp.minimum(t + 1, w).astype(F32)
    return s[POOL_HALO:] / cnt - u_ext[POOL_HALO:]


def _layer0_fwd(proj, sinks, pool_w, pool_scale, w_out, x_in, post, comm=None, tq=512):
    t = proj.shape[0]
    nblk = tq // BLOCK

    def body(main_ref, halo_ref, sink_ref, pw_ref, ps_ref, w_ref, x_ref, p_ref, o_ref, y_ref, xo_ref, kv_ref, bias_ref):
        i = pl.program_id(0)
        t0 = i * tq
        masks = _group_masks()

        @pl.when(i == 0)
        def _():
            _fill_attn_bias(bias_ref)

        kv_ref[0:BLOCK, :] = halo_ref[:, COL_K:COL_K + 256]
        kv_ref[BLOCK:, :] = main_ref[:, COL_K:COL_K + 256]

        def block(jb, carry):
            r0 = pl.multiple_of(jb * BLOCK, BLOCK)
            no_past = t0 + r0 == 0
            q = main_ref[pl.ds(r0, BLOCK), COL_Q:COL_Q + ATTN_WIDTH]
            ga = main_ref[pl.ds(r0, BLOCK), COL_GA:COL_GA + ATTN_WIDTH]
            kk = kv_ref[pl.ds(r0, 2 * BLOCK), 0:128]
            vv = kv_ref[pl.ds(r0, 2 * BLOCK), 128:256]
            outs = []
            for kvh in range(KV_HEADS):
                k4 = _repeat_head(kk, kvh).astype(BF16)
                v4 = _repeat_head(vv, kvh).astype(BF16)
                _, p, _ = _attn_probs(q[:, kvh * 256:(kvh + 1) * 256], k4, bias_ref[kvh], _row_sinks(kvh, sink_ref), no_past, masks)
                pv = jnp.dot(p.astype(BF16), v4, preferred_element_type=F32)
                outs.append(_unstack_groups(pv, masks))
            attn = jnp.concatenate(outs, axis=1)
            o_ref[pl.ds(r0, BLOCK), 0:ATTN_WIDTH] = (attn * _silu(ga)).astype(BF16)
            return carry

        lax.fori_loop(0, nblk, block, 0, unroll=True)

        for g in range(POOL_GROUPS):
            cu = COL_U + g * POOL_GC
            cg = COL_GB + g * POOL_GC
            halo_u = jnp.where(i == 0, 0.0, halo_ref[BLOCK - POOL_HALO:BLOCK, cu:cu + POOL_GC])
            u_ext = jnp.concatenate([halo_u, main_ref[:, cu:cu + POOL_GC]], axis=0)
            pooled = _pool_forward(u_ext, g, t0)
            y = jnp.dot(pooled.astype(BF16), pw_ref[g].astype(BF16), preferred_element_type=F32)
            y = y * ps_ref[:, g * POOL_GC:(g + 1) * POOL_GC]
            o_ref[:, ATTN_WIDTH + g * POOL_GC:ATTN_WIDTH + (g + 1) * POOL_GC] =(y * _silu(main_ref[:, cg:cg + POOL_GC])).astype(BF16)

        xo_ref[...] = _project_out(o_ref[...], w_ref, x_ref, p_ref, y_ref)

    row = pl.BlockSpec((tq, D_MODEL), lambda i: (i, 0))
    return _fused_call(
        body, comm, (proj, proj, sinks, pool_w, pool_scale, w_out, x_in, post), name="layer0_fwd", grid=(t // tq,),
        out_shape=[jax.ShapeDtypeStruct((t, D_MODEL), BF16), jax.ShapeDtypeStruct((t, D_MODEL), F32),
                   jax.ShapeDtypeStruct((t, D_MODEL), F32)],
        in_specs=[pl.BlockSpec((tq, EVEN_IN), lambda i: (i, 0)),
                  pl.BlockSpec((BLOCK, EVEN_IN), lambda i: (jnp.maximum(i * nblk - 1, 0), 0)),
                  pl.BlockSpec(memory_space=pltpu.SMEM),
                  _const_spec((POOL_GROUPS, POOL_GC, POOL_GC)), _const_spec((1, POOL_WIDTH)),
                  _const_spec((D_MODEL, D_MODEL)), row, _const_spec((1, D_MODEL))],
        out_specs=[row, row, row],
        scratch_shapes=[pltpu.VMEM((tq + BLOCK, 256), F32)] + ATTN_CONSTS,
        params=_params("arbitrary"))


def _layer0_bwd(proj, gy, y, mix, w_out, post, sinks, pool_w, pool_scale, comm=None, tq=512):
    t = proj.shape[0]
    nt = t // tq
    nblk = tq // BLOCK

    def body(main_ref, halo_ref, next_ref, gy_ref, gyn_ref, y_ref, yn_ref, mix_ref, wo_ref, po_ref,
             sink_ref, pw_ref, ps_ref,
             o_ref, dsk_ref, dpw_ref, dps_ref, dwo16_ref, dpo_ref,
             kv_ref, dkv_ref, carry_ref, bias_ref, dwo_ref, dmix_ref):
        i = pl.program_id(0)
        ii = nt - 1 - i
        t0 = ii * tq
        masks = _group_masks()

        @pl.when(i == 0)
        def _():
            _fill_attn_bias(bias_ref)
            dsk_ref[...] = jnp.zeros_like(dsk_ref)
            dpw_ref[...] = jnp.zeros_like(dpw_ref)
            dps_ref[...] = jnp.zeros_like(dps_ref)
            carry_ref[...] = jnp.zeros_like(carry_ref)

        dmix_ref[...] = _post_bwd_rows(jnp.concatenate([gy_ref[...], gyn_ref[...]], axis=0),
                                       jnp.concatenate([y_ref[...], yn_ref[...]], axis=0), mix_ref[...], tq,
                                       i == 0, i == nt - 1, po_ref, wo_ref, dwo_ref, dwo16_ref, dpo_ref)
        dm_ref = dmix_ref.at[pl.ds(0, tq)]
        dmn_ref = dmix_ref.at[pl.ds(tq, POOL_HALO)]

        kv_ref[0:BLOCK, :] = halo_ref[:, COL_K:COL_K + 256]
        kv_ref[BLOCK:, :] = main_ref[:, COL_K:COL_K + 256]
        dkv_ref[0:tq, :] = jnp.zeros((tq, 256), F32)
        dkv_ref[tq:, :] = carry_ref[...]

        def block(jb, carry):
            r0 = pl.multiple_of(jb * BLOCK, BLOCK)
            no_past = t0 + r0 == 0
            q = main_ref[pl.ds(r0, BLOCK), COL_Q:COL_Q + ATTN_WIDTH]
            ga = main_ref[pl.ds(r0, BLOCK), COL_GA:COL_GA + ATTN_WIDTH]
            dya = dm_ref[pl.ds(r0, BLOCK), 0:ATTN_WIDTH]
            kk = kv_ref[pl.ds(r0, 2 * BLOCK), 0:128]
            vv = kv_ref[pl.ds(r0, 2 * BLOCK), 128:256]
            silu_ga, dsilu_ga = _silu_and_grad(ga)
            do = dya * silu_ga
            first = lax.broadcasted_iota(jnp.int32, (2 * BLOCK, 128), 1) < HEAD_DIM
            attn, dq, dk, dv = [], [], [], []
            for kvh in range(KV_HEADS):
                k4 = _repeat_head(kk, kvh).astype(BF16)
                v4 = _repeat_head(vv, kvh).astype(BF16)
                qs, p, ps = _attn_probs(q[:, kvh * 256:(kvh + 1) * 256], k4, bias_ref[kvh], _row_sinks(kvh, sink_ref), no_past, masks)
                pb = p.astype(BF16)
                o_k = _unstack_groups(jnp.dot(pb, v4, preferred_element_type=F32), masks)
                do_k = do[:, kvh * 256:(kvh + 1) * 256]
                dos = _stack_groups(do_k, masks).astype(BF16)
                prod = do_k * o_k
                delta = jnp.concatenate([jnp.sum(jnp.where(m, prod, 0.0), axis=-1, keepdims=True) for m in masks], axis=0)
                dp = lax.dot_general(dos, v4, NT, preferred_element_type=F32)
                ds = (p * (dp - delta)).astype(BF16)
                sink_term = ps * delta
                for g in range(GROUP):
                    h = kvh * GROUP + g
                    dsk_ref[h:h + 1, :] -= jnp.sum(sink_term[g * BLOCK:(g + 1) * BLOCK], keepdims=True)
                dq.append(_unstack_groups(jnp.dot(ds, k4, preferred_element_type=F32), masks) * (HEAD_DIM ** -0.5))
                dk.append(_fold_head(lax.dot_general(ds, qs, TN, preferred_element_type=F32)))
                dv.append(_fold_head(lax.dot_general(pb, dos, TN, preferred_element_type=F32)))
                attn.append(o_k)
            o_ref[pl.ds(r0, BLOCK), COL_Q:COL_Q + ATTN_WIDTH] = jnp.concatenate(dq, axis=1).astype(BF16)
            o_all = jnp.concatenate(attn, axis=1)
            o_ref[pl.ds(r0, BLOCK), COL_GA:COL_GA + ATTN_WIDTH] = (dya * o_all * dsilu_ga).astype(BF16)
            dkv = jnp.concatenate([jnp.where(first, dk[0], dk[1]), jnp.where(first, dv[0], dv[1])], axis=1)
            dkv_ref[pl.ds(r0, 2 * BLOCK), :] += dkv
            return carry

        lax.fori_loop(0, nblk, block, 0, unroll=True)
        carry_ref[...] = dkv_ref[0:BLOCK, :]
        o_ref[:, COL_K:COL_K + 256] = dkv_ref[BLOCK:, :].astype(BF16)

        last = ii == nt - 1
        for g in range(POOL_GROUPS):
            cu = COL_U + g * POOL_GC
            cg = COL_GB + g * POOL_GC
            cm = ATTN_WIDTH + g * POOL_GC
            pw = pw_ref[g].astype(BF16)
            sc = ps_ref[:, g * POOL_GC:(g + 1) * POOL_GC]
            halo_u = jnp.where(ii == 0, 0.0, halo_ref[BLOCK - POOL_HALO:BLOCK, cu:cu + POOL_GC])
            u_ext = jnp.concatenate([halo_u, main_ref[:, cu:cu + POOL_GC]], axis=0)
            pooled = _pool_forward(u_ext, g, t0).astype(BF16)
            y_raw = jnp.dot(pooled, pw, preferred_element_type=F32)
            gb = main_ref[:, cg:cg + POOL_GC]
            dyb = dm_ref[:, cm:cm + POOL_GC]
            silu_gb, dsilu_gb = _silu_and_grad(gb)
            dypool = dyb * silu_gb
            dps_ref[:, g * POOL_GC:(g + 1) * POOL_GC] += jnp.sum(dypool * y_raw, axis=0, keepdims=True)
            o_ref[:, cg:cg + POOL_GC] = (dyb * (y_raw * sc) * dsilu_gb).astype(BF16)
            dyraw = dypool * sc
            dyraw_n = jnp.where(last, 0.0, dmn_ref[:, cm:cm + POOL_GC] * _silu(next_ref[:, cg:cg + POOL_GC]) * sc)
            dpw_ref[g * POOL_GC:(g + 1) * POOL_GC, :] += lax.dot_general(pooled, dyraw.astype(BF16), TN,
                                                                         preferred_element_type=F32)
            dyraw_ext = jnp.concatenate([dyraw, dyraw_n], axis=0).astype(BF16)
            dpooled = lax.dot_general(dyraw_ext, pw, NT, preferred_element_type=F32)
            w = 2 << g
            tt = t0 + lax.broadcasted_iota(jnp.int32, (tq + POOL_HALO, 1), 0)
            s = dpooled / jnp.minimum(tt + 1, w).astype(F32)
            for step in range(g + 1):
                s = s + pltpu.roll(s, tq + POOL_HALO - (1 << step), 0)
            o_ref[:, cu:cu + POOL_GC] = (s[0:tq] - dpooled[0:tq]).astype(BF16)

    rev = lambda i: nt - 1 - i
    nxt = lambda i: (jnp.minimum((rev(i) + 1) * (tq // POOL_HALO), t // POOL_HALO - 1), 0)
    row = pl.BlockSpec((tq, D_MODEL), lambda i: (rev(i), 0))
    nxt_row = pl.BlockSpec((POOL_HALO, D_MODEL), nxt)
    square = _const_spec((D_MODEL, D_MODEL))
    return _fused_call(
        body, comm, (proj, proj, proj, gy, gy, y, y, mix, w_out, post, sinks, pool_w, pool_scale),
        name="layer0_bwd", grid=(nt,),
        out_shape=[jax.ShapeDtypeStruct((t, EVEN_IN), BF16), jax.ShapeDtypeStruct((8, 128), F32),
                   jax.ShapeDtypeStruct((POOL_GROUPS * POOL_GC, POOL_GC), F32), jax.ShapeDtypeStruct((1, POOL_WIDTH), F32),
                   jax.ShapeDtypeStruct((D_MODEL, D_MODEL), BF16), jax.ShapeDtypeStruct((1, D_MODEL), F32)],
        in_specs=[pl.BlockSpec((tq, EVEN_IN), lambda i: (rev(i), 0)),
                  pl.BlockSpec((BLOCK, EVEN_IN), lambda i: (jnp.maximum(rev(i) * nblk - 1, 0), 0)),
                  pl.BlockSpec((POOL_HALO, EVEN_IN), nxt),
                  row, nxt_row, row, nxt_row, row, square, _const_spec((1, D_MODEL)),
                  pl.BlockSpec(memory_space=pltpu.SMEM),
                  _const_spec((POOL_GROUPS, POOL_GC, POOL_GC)), _const_spec((1, POOL_WIDTH))],
        out_specs=[pl.BlockSpec((tq, EVEN_IN), lambda i: (rev(i), 0)),
                   pl.BlockSpec((8, 128), lambda i: (0, 0)),
                   pl.BlockSpec((POOL_GROUPS * POOL_GC, POOL_GC), lambda i: (0, 0)),
                   pl.BlockSpec((1, POOL_WIDTH), lambda i: (0, 0)), square, pl.BlockSpec((1, D_MODEL), lambda i: (0, 0))],
        scratch_shapes=[pltpu.VMEM((tq + BLOCK, 256), F32), pltpu.VMEM((tq + BLOCK, 256), F32),
                        pltpu.VMEM((BLOCK, 256), F32)] + ATTN_CONSTS
        + [pltpu.VMEM((D_MODEL, D_MODEL), F32), pltpu.VMEM((tq + POOL_HALO, D_MODEL), F32)],
        params=_params("arbitrary"))


CONV_RC = 32
CONV_CC = 128
CONV_CHAINS_FWD = 4
CONV_CHAINS_BWD = 2
CONV_UNROLL = 2


def _fill_shifted(s_ref, rows):
    for b in range(1, 8):
        s_ref[b, 0:rows - 8, :] = s_ref[0, b:b + rows - 8, :]


def _tap_blocks(s_ref, r, cols, lead):
    for b in range(8):
        taps = [(a, 8 * a + b - lead) for a in range(5) if 0 <= 8 * a + b - lead < CONV_K]
        span = 8 * max(a for a, _ in taps) + CONV_RC
        blk = s_ref[b, pl.ds(r, span), cols]
        for a, k in taps:
            yield k, blk[8 * a:8 * a + CONV_RC]


def _conv_taps(s_ref, w_ref, r, cols, lead, reverse, chains):
    accs = [None] * chains
    for n, (k, blk) in enumerate(_tap_blocks(s_ref, r, cols, lead)):
        kw = CONV_K - 1 - k if reverse else k
        term = blk * w_ref[kw:kw + 1, cols]
        accs[n % chains] = term if accs[n % chains] is None else accs[n % chains] + term
    while len(accs) > 1:
        accs = [a + b for a, b in zip(accs[0::2], accs[1::2])]
    return accs[0]


def _layer_norm_fwd(cf, lng, lnb):
    mu = jnp.mean(cf, axis=-1, keepdims=True)
    xc = cf - mu
    rstd = lax.rsqrt(jnp.mean(xc * xc, axis=-1, keepdims=True) + EPS)
    chat = xc * rstd
    return chat, rstd, chat * lng + lnb


def _layer1_fwd(proj, dw, dwb, lng, lnb, w_out, x_in, post, target, tt=512):
    t = proj.shape[0]
    lead = CONV_HALO - (CONV_K - 1)

    def body(main_ref, halo_ref, w_ref, b_ref, g_ref, lb_ref, wo_ref, x_ref, p_ref, t_ref,
             o_ref, c_ref, y_ref, dl_ref, l_ref, gs_ref):
        i = pl.program_id(0)
        hv = halo_ref[...]
        gs_ref[0, 0:CONV_HALO, :] = jnp.where(i == 0, 0.0, hv[:, GLU_A] * _sigmoid(hv[:, GLU_B]))
        gs_ref[0, CONV_HALO:CONV_HALO + tt, :] = main_ref[:, GLU_A] * _sigmoid(main_ref[:, GLU_B])
        _fill_shifted(gs_ref, tt + CONV_HALO)

        for c in range(D_MODEL // CONV_CC):
            cols = slice(c * CONV_CC, (c + 1) * CONV_CC)

            def chunk(j, carry):
                r = pl.multiple_of(j * CONV_RC, CONV_RC)
                c_ref[pl.ds(r, CONV_RC), cols] = _conv_taps(gs_ref, w_ref, r, cols, lead, False, CONV_CHAINS_FWD) + b_ref[:, cols]
                return carry
            lax.fori_loop(0, tt // CONV_RC, chunk, 0, unroll=2 * CONV_UNROLL)

        _, _, cn = _layer_norm_fwd(c_ref[...], g_ref[...], lb_ref[...])
        o_ref[...] = (_silu(cn) * _silu(main_ref[:, GATE])).astype(BF16)

        d = _project_out(o_ref[...], wo_ref, x_ref, p_ref, y_ref) - t_ref[...]
        dl_ref[...] = d * (1.0 / D_MODEL)

        @pl.when(i == 0)
        def _():
            l_ref[...] = jnp.zeros_like(l_ref)

        l_ref[...] += 0.5 * jnp.sum(jnp.mean(d * d, axis=-1, keepdims=True))

    vec = _const_spec((1, D_MODEL))
    row = pl.BlockSpec((tt, D_MODEL), lambda i: (i, 0))
    f32_rows = jax.ShapeDtypeStruct((t, D_MODEL), F32)
    return pl.pallas_call(
        body, name="layer1_fwd", grid=(t // tt,),
        out_shape=[jax.ShapeDtypeStruct((t, D_MODEL), BF16), f32_rows, f32_rows, f32_rows,
                   jax.ShapeDtypeStruct((8, 128), F32)],
        in_specs=[pl.BlockSpec((tt, 3 * D_MODEL), lambda i: (i, 0)),
                  pl.BlockSpec((CONV_HALO, 3 * D_MODEL), lambda i: (jnp.maximum(i * (tt // CONV_HALO) - 1, 0), 0)),
                  _const_spec((CONV_K, D_MODEL)), vec, vec, vec,
                  _const_spec((D_MODEL, D_MODEL)), row, vec, row],
        out_specs=[row, row, row, row, pl.BlockSpec((8, 128), lambda i: (0, 0))],
        scratch_shapes=[pltpu.VMEM((8, tt + CONV_HALO, D_MODEL), F32)],
        compiler_params=_params("arbitrary"),
    )(proj, proj, dw, dwb, lng, lnb, w_out, x_in, post, target)


def _layer1_bwd(proj, cf, gy, y, z, w_out, post, dw, lng, lnb, comm=None, tt=256):
    t = proj.shape[0]
    nt = t // tt
    te = tt + CONV_HALO

    def body(main_ref, next_ref, cf_ref, cfn_ref, gy_ref, gyn_ref, y_ref, yn_ref, z_ref, wo_ref, po_ref,
             w_ref, g_ref, lb_ref,
             o_ref, ddw_ref, ddb_ref, dg_ref, dlb_ref, dwo16_ref, dpo_ref, ds_ref, glu_ref, sb_ref, dwo_ref):
        i = pl.program_id(0)

        @pl.when(i == 0)
        def _():
            ddw_ref[...] = jnp.zeros_like(ddw_ref)
            ddb_ref[...] = jnp.zeros_like(ddb_ref)
            dg_ref[...] = jnp.zeros_like(dg_ref)
            dlb_ref[...] = jnp.zeros_like(dlb_ref)

        dzv = _post_bwd_rows(jnp.concatenate([gy_ref[...], gyn_ref[...]], axis=0),
                             jnp.concatenate([y_ref[...], yn_ref[...]], axis=0), z_ref[...], tt, i == 0, i == nt - 1,
                             po_ref, wo_ref, dwo_ref, dwo16_ref, dpo_ref)
        dzv = jnp.concatenate([dzv[0:tt], jnp.where(i < nt - 1, dzv[tt:], 0.0)], axis=0)
        lng = g_ref[...]
        chat, rstd, cn = _layer_norm_fwd(jnp.concatenate([cf_ref[...], cfn_ref[...]], axis=0), lng, lb_ref[...])
        gate = jnp.concatenate([main_ref[:, GATE], next_ref[:, GATE]], axis=0)
        silu_cn, dsilu_cn = _silu_and_grad(cn)
        silu_gate, dsilu_gate = _silu_and_grad(gate)
        o_ref[:, GATE] = (dzv * silu_cn * dsilu_gate)[0:tt].astype(BF16)
        dcn = dzv * silu_gate * dsilu_cn
        dg_ref[...] += jnp.sum((dcn * chat)[0:tt], axis=0, keepdims=True)
        dlb_ref[...] += jnp.sum(dcn[0:tt], axis=0, keepdims=True)
        dchat = dcn * lng
        dcf = rstd * (dchat - jnp.mean(dchat, axis=-1, keepdims=True) - chat * jnp.mean(dchat * chat, axis=-1, keepdims=True))
        ddb_ref[...] += jnp.sum(dcf[0:tt], axis=0, keepdims=True)
        ds_ref[0, 0:te, :] = dcf
        ds_ref[0, te:, :] = jnp.zeros((8, D_MODEL), F32)
        _fill_shifted(ds_ref, te + 8)
        sb_ref[...] = _sigmoid(main_ref[:, GLU_B])
        glu_ref[...] = main_ref[:, GLU_A] * sb_ref[...]

        for c in range(D_MODEL // CONV_CC):
            cols = slice(c * CONV_CC, (c + 1) * CONV_CC)
            gcols = slice(D_MODEL + c * CONV_CC, D_MODEL + (c + 1) * CONV_CC)

            def chunk(j, carry):
                r = pl.multiple_of(j * CONV_RC, CONV_RC)
                dglu = _conv_taps(ds_ref, w_ref, r, cols, 0, True, CONV_CHAINS_BWD)
                sb = sb_ref[pl.ds(r, CONV_RC), cols]
                o_ref[pl.ds(r, CONV_RC), cols] = (dglu * sb).astype(BF16)
                o_ref[pl.ds(r, CONV_RC), gcols] = (dglu * glu_ref[pl.ds(r, CONV_RC), cols] * (1.0 - sb)).astype(BF16)
                return carry
            lax.fori_loop(0, tt // CONV_RC, chunk, 0, unroll=CONV_UNROLL)

            def taps(j, accs):
                r = pl.multiple_of(j * CONV_RC, CONV_RC)
                gl = glu_ref[pl.ds(r, CONV_RC), cols]
                new = list(accs)
                for m, blk in _tap_blocks(ds_ref, r, cols, 0):
                    prod = blk * gl
                    part = prod[0:8]
                    for q in range(1, CONV_RC // 8):
                        part = part + prod[8 * q:8 * q + 8]
                    new[m] = new[m] + part
                return tuple(new)
            accs = lax.fori_loop(0, tt // CONV_RC, taps, tuple(jnp.zeros((8, CONV_CC), F32) for _ in range(CONV_K)))
            for m in range(CONV_K):
                k = CONV_K - 1 - m
                ddw_ref[k:k + 1, cols] += jnp.sum(accs[m], axis=0, keepdims=True)

    vec = _const_spec((1, D_MODEL))
    vec_out = pl.BlockSpec((1, D_MODEL), lambda i: (0, 0))
    row = pl.BlockSpec((tt, D_MODEL), lambda i: (i, 0))
    nxt = lambda i: (jnp.minimum((i + 1) * (tt // CONV_HALO), t // CONV_HALO - 1), 0)
    nxt_row = pl.BlockSpec((CONV_HALO, D_MODEL), nxt)
    vec_f32 = jax.ShapeDtypeStruct((1, D_MODEL), F32)
    square = _const_spec((D_MODEL, D_MODEL))
    return _fused_call(
        body, comm, (proj, proj, cf, cf, gy, gy, y, y, z, w_out, post, dw, lng, lnb), name="layer1_bwd", grid=(nt,),
        out_shape=[jax.ShapeDtypeStruct((t, 3 * D_MODEL), BF16), jax.ShapeDtypeStruct((CONV_K, D_MODEL), F32),
                   vec_f32, vec_f32, vec_f32, jax.ShapeDtypeStruct((D_MODEL, D_MODEL), BF16), vec_f32],
        in_specs=[pl.BlockSpec((tt, 3 * D_MODEL), lambda i: (i, 0)),
                  pl.BlockSpec((CONV_HALO, 3 * D_MODEL), nxt),
                  row, nxt_row, row, nxt_row, row, nxt_row, row, square, vec,
                  _const_spec((CONV_K, D_MODEL)), vec, vec],
        out_specs=[pl.BlockSpec((tt, 3 * D_MODEL), lambda i: (i, 0)),
                   pl.BlockSpec((CONV_K, D_MODEL), lambda i: (0, 0)), vec_out, vec_out, vec_out, square, vec_out],
        scratch_shapes=[pltpu.VMEM((8, te + 8, D_MODEL), F32), pltpu.VMEM((tt, D_MODEL), F32),
                        pltpu.VMEM((tt, D_MODEL), F32), pltpu.VMEM((D_MODEL, D_MODEL), F32)],
        params=_params("arbitrary"))


def _piece_sums(tensors, place, name):
    counts = [len(parts) for parts in tensors]

    def body(p_ref, *refs):
        ins, outs = refs[:sum(counts)], refs[sum(counts):]
        pos = 0
        for count, o_ref in zip(counts, outs):
            acc = ins[pos][0].astype(F32)
            for part in ins[pos + 1:pos + count]:
                acc = acc + part[0].astype(F32)
            o_ref[0] = acc
            pos += count

    spec = lambda a, slot: pl.BlockSpec((1,) + a.shape[1:], lambda j, p_ref: (slot(p_ref), 0, 0))
    return pl.pallas_call(
        body, name=name,
        grid_spec=pltpu.PrefetchScalarGridSpec(
            num_scalar_prefetch=1, grid=(1,),
            in_specs=[spec(a, slot) for parts in tensors for a, slot in parts],
            out_specs=[pl.BlockSpec((1,) + parts[0][0].shape[1:], lambda j, p_ref: (p_ref[1], 0, 0))
                       for parts in tensors]),
        out_shape=[jax.ShapeDtypeStruct((2,) + parts[0][0].shape[1:], F32) for parts in tensors],
        compiler_params=_params("arbitrary"),
    )(place, *[a for parts in tensors for a, _ in parts])


def _direct_parts(own, recv):
    peer = lambda m: (lambda p: p[0] ^ m)
    return [(own, peer(0))] + [(recv, peer(m)) for m in range(1, N_DEV)]


def _share_with_sibling(halves, small, name):
    n = len(halves)

    def body(*refs):
        small_ref, outs, all_ref = refs[n], refs[n + 1:2 * n + 1], refs[2 * n + 1]
        send_sems, recv_sems, local_sem = refs[2 * n + 2:]
        x, y, c, _ = _place()
        me = 4 * x + 2 * y + c
        send = [pltpu.make_async_remote_copy(
            src_ref=outs[t].at[c], dst_ref=outs[t].at[c], send_sem=send_sems.at[t], recv_sem=recv_sems.at[t],
            device_id=(x, y, 1 - c), device_id_type=MESH_ID) for t in range(n)]
        recv = [pltpu.make_async_remote_copy(
            src_ref=outs[t].at[c], dst_ref=outs[t].at[1 - c], send_sem=send_sems.at[t], recv_sem=recv_sems.at[t],
            device_id=(x, y, 1 - c), device_id_type=MESH_ID) for t in range(n)]
        for m in range(1, N_DEV):
            px, py, pc = x ^ (m >> 2), y ^ ((m >> 1) & 1), c ^ (m & 1)
            sems = dict(send_sem=send_sems.at[n + m - 1], recv_sem=recv_sems.at[n + m - 1], device_id=(px, py, pc),
                        device_id_type=MESH_ID)
            send.append(pltpu.make_async_remote_copy(src_ref=small_ref, dst_ref=all_ref.at[me], **sems))
            recv.append(pltpu.make_async_remote_copy(src_ref=small_ref, dst_ref=all_ref.at[4 * px + 2 * py + pc], **sems))
        mine = pltpu.make_async_copy(small_ref, all_ref.at[me], local_sem)
        mine.start()
        for cp in send:
            cp.start()
        for cp in recv:
            cp.wait_recv()
        for cp in send:
            cp.wait_send()
        mine.wait()

    k = n + N_DEV - 1
    return pl.pallas_call(
        body, name=name,
        out_shape=[jax.ShapeDtypeStruct(h.shape, h.dtype) for h in halves]
        + [jax.ShapeDtypeStruct((N_DEV,) + small.shape, small.dtype)],
        in_specs=[ANY] * (n + 1), out_specs=[ANY] * (n + 1),
        input_output_aliases={t: t for t in range(n)},
        scratch_shapes=[pltpu.SemaphoreType.DMA((k,)), pltpu.SemaphoreType.DMA((k,)), pltpu.SemaphoreType.DMA],
    )(*halves, small)


def _sum8(parts, name):
    n = len(parts)

    def body(*refs):
        for p_ref, o_ref in zip(refs[:n], refs[n:]):
            acc = p_ref[0]
            for k in range(1, N_DEV):
                acc = acc + p_ref[k]
            o_ref[...] = acc

    whole = pl.BlockSpec(memory_space=pltpu.VMEM)
    return pl.pallas_call(
        body, name=name, out_shape=[jax.ShapeDtypeStruct(p.shape[1:], F32) for p in parts],
        in_specs=[whole] * n, out_specs=[whole] * n,
    )(*parts)


def _adamw(tensors, steps, name):
    n = len(tensors)
    views, specs = [], []
    for w, _, _, _ in tensors:
        cols = w.shape[-1]
        rows = w.size // cols
        if w.ndim >= 3 and w.shape[-2] == 1:
            assert steps == 1
            views.append((rows, 1, cols))
            specs.append(pl.BlockSpec((rows, 1, cols), lambda i: (0, 0, 0)))
        else:
            views.append((rows, cols))
            specs.append(pl.BlockSpec((rows // steps, cols), lambda i: (i, 0)))

    def body(*refs):
        for t in range(n):
            w_ref, g_ref, m_ref, v_ref = refs[4 * t:4 * t + 4]
            d_ref, nm_ref, nv_ref = refs[4 * n + 3 * t:4 * n + 3 * t + 3]
            gv = g_ref[...]
            mn = ADAM_B1 * m_ref[...] + (1.0 - ADAM_B1) * gv
            vn = ADAM_B2 * v_ref[...] + (1.0 - ADAM_B2) * (gv * gv)
            m_hat = mn / (1.0 - ADAM_B1 ** ADAM_STEP)
            v_hat = vn / (1.0 - ADAM_B2 ** ADAM_STEP)
            d_ref[...] = -ADAM_LR * (m_hat / (jnp.sqrt(v_hat) + ADAM_EPS) + ADAM_WD * w_ref[...])
            nm_ref[...] = mn
            nv_ref[...] = vn

    outs = pl.pallas_call(
        body, name=name, grid=(steps,),
        out_shape=[jax.ShapeDtypeStruct(view, F32) for view in views for _ in range(3)],
        in_specs=[spec for spec in specs for _ in range(4)], out_specs=[spec for spec in specs for _ in range(3)],
        compiler_params=_params("parallel"),
    )(*[a.reshape(view) for group, view in zip(tensors, views) for a in group])
    return [tuple(o.reshape(group[0].shape) for o in outs[3 * t:3 * t + 3]) for t, group in enumerate(tensors)]


SMALL_ROWS = 832


def _pack_small(g):
    parts = [g["loss"], g["pre1"].reshape(8, 128), g["post0"].reshape(8, 128),
             g["post1"].reshape(8, 128), g["sinks"], jnp.pad(g["pool_scale"].reshape(4, 128), ((0, 4), (0, 0))),
             g["pool_w"], g["dw"].reshape(248, 128), g["dwb"].reshape(8, 128), g["lng"].reshape(8, 128),
             g["lnb"].reshape(8, 128)]
    assert sum(p.shape[0] for p in parts) == SMALL_ROWS
    return jnp.concatenate(parts, axis=0)


def _unpack_small(s):
    out, r = {}, 0
    for key, rows, shape in (("loss", 8, (8, 128)), ("pre1", 8, (1, D_MODEL)), ("post", 16, (2, D_MODEL)),
                             ("sinks", 8, (8, 128)),
                             ("pool_scale", 4, (1, POOL_WIDTH)), ("pad", 4, (4, 128)), ("pool_w", 512, (1, 4, 128, 128)),
                             ("dw", 248, (CONV_K, D_MODEL)), ("dwb", 8, (1, D_MODEL)), ("lng", 8, (1, D_MODEL)),
                             ("lnb", 8, (1, D_MODEL))):
        out[key] = s[r:r + rows].reshape(shape)
        r += rows
    return out


def kernel(x, pre_norm, post_norm, a_w_in, a_sinks, b_pool_w, b_pool_scale, ab_w_out, c_w_in, c_dw_w, c_dw_b, c_ln_g, c_ln_b, c_w_out, loss_target, m_pre_norm, m_post_norm, m_a_w_in, m_a_sinks, m_b_pool_w, m_b_pool_scale, m_ab_w_out, m_c_w_in, m_c_dw_w, m_c_dw_b, m_c_ln_g, m_c_ln_b, m_c_w_out, v_pre_norm, v_post_norm, v_a_w_in, v_a_sinks, v_b_pool_w, v_b_pool_scale, v_ab_w_out, v_c_w_in, v_c_dw_w, v_c_dw_b, v_c_ln_g, v_c_ln_b, v_c_w_out):
    ix, iy = lax.axis_index("x"), lax.axis_index("y")
    chip_cols = (2 * ix + iy) * 256

    pad8 = lambda v: jnp.pad(v, ((0, -v.shape[0] % 8), (0, 0)))
    vec_shard = jnp.concatenate([pad8(c_dw_w.reshape(CONV_K, 256)), pad8(c_dw_b), pad8(c_ln_g), pad8(c_ln_b),
                                 jnp.zeros((8, 256), F32)], axis=0)
    x0, target = x[0], loss_target[0]
    pre0, pre1 = pre_norm[0:1], pre_norm[1:2]
    post0, post1 = post_norm[0:1], post_norm[1:2]
    pool_w = b_pool_w[0]

    (wa_t,) = _run_comm(_Gather([a_w_in[0].T.astype(BF16)], halve=True), "gather_a_w_in")
    wa_t = wa_t.reshape(EVEN_IN, D_MODEL)
    proj0, (w_ab,) = _norm_matmul(x0, pre0, wa_t, "proj0_fwd", comm=_Gather([ab_w_out[0].astype(BF16)], halve=True))
    w_ab = w_ab.reshape(D_MODEL, D_MODEL)
    (mix0, y0, x1), (wc_t, w_c, vecs) = _layer0_fwd(
        proj0, a_sinks, pool_w, b_pool_scale, w_ab, x0, post0,
        comm=_Gather([c_w_in[0].T.astype(BF16), c_w_out[0].astype(BF16), vec_shard], halve=True))
    wc_t = wc_t.reshape(3 * D_MODEL, D_MODEL)
    w_c = w_c.reshape(D_MODEL, D_MODEL)
    vecs = vecs.reshape(4, 64, 256).transpose(1, 0, 2).reshape(64, D_MODEL)
    dw, dwb, lng, lnb = vecs[0:CONV_K], vecs[32:33], vecs[40:41], vecs[48:49]
    proj1, _ = _norm_matmul(x1, pre1, wc_t, "proj1_fwd")
    z1, cf1, y1, g2, loss = _layer1_fwd(proj1, dw, dwb, lng, lnb, w_c, x1, post1, target)

    pieces = lambda m: m.reshape(N_DEV, m.shape[0] // N_DEV, D_MODEL)
    (dproj1, d_dw, d_dwb, d_lng, d_lnb, d_wc, d_post1), _ = _layer1_bwd(proj1, cf1, g2, y1, z1, w_c, post1, dw, lng, lnb)
    (g1, d_wct, d_pre1), _ = _pre_bwd(dproj1, wc_t, x1, pre1, g2, "proj1_bwd")
    (dproj0, d_sinks, d_pw, d_ps, d_wab, d_post0), (r_wc, r_wct) = _layer0_bwd(
        proj0, g1, y0, mix0, w_ab, post0, a_sinks, pool_w, b_pool_scale,
        comm=_Scatter([pieces(d_wc), pieces(d_wct)]))
    g = dict(loss=loss, pre1=d_pre1, post0=d_post0, post1=d_post1, sinks=d_sinks, pool_w=d_pw, pool_scale=d_ps,
             dw=d_dw, dwb=d_dwb, lng=d_lng, lnb=d_lnb)
    d_wat, (small8, r_wab) = _proj_dw(dproj0, x0, pre0, "proj0_dw",
                                      comm=_Comms(_Gather([_pack_small(g)], halve=False), _Scatter([pieces(d_wab)])))
    sent = _scatter_start(pieces(d_wat), "scatter_a_start")
    (gx, d_pre0), _ = _proj_dx(dproj0, wa_t, x0, pre0 + sent[4][0:1, 0:1], g1, "proj0_dx")
    own_wat, r_wat = _scatter_wait(*sent[:4], d_pre0, "scatter_a_wait")

    ic = lax.axis_index("c")
    me = 4 * ix + 2 * iy + ic
    place = jnp.stack([me, ic]).astype(jnp.int32)
    parts = [_direct_parts(own_wat, r_wat), _direct_parts(pieces(d_wab), r_wab), _direct_parts(pieces(d_wct), r_wct),
             _direct_parts(pieces(d_wc), r_wc)]
    halves = _piece_sums(parts, place, "grad_sums")
    *shared, pre0_8 = _share_with_sibling(halves, d_pre0.reshape(8, 128), "grad_share")
    g_wa_t, g_wab, g_wc_t, g_wc = [h.reshape(2 * h.shape[1], D_MODEL) for h in shared]
    g_c_w_in = g_wc_t.T[None]
    g_ab_w_out = g_wab[None]
    g_c_w_out = g_wc[None]

    small, pre0 = _sum8([small8, pre0_8], "small_sums")
    s = _unpack_small(small)
    layer = lax.broadcasted_iota(jnp.int32, (2, D_MODEL), 0)
    g_pre = jnp.where(layer == 0, pre0.reshape(1, D_MODEL), s["pre1"])
    g_post = s["post"]
    g_sinks = s["sinks"][:, 0].reshape(1, 8)
    g_pool_w, g_pool_scale = s["pool_w"], s["pool_scale"]
    g_dw = lax.dynamic_slice_in_dim(s["dw"], chip_cols, 256, axis=1).reshape(1, CONV_K, 1, 256)
    g_dwb = lax.dynamic_slice_in_dim(s["dwb"], chip_cols, 256, axis=1)
    g_lng = lax.dynamic_slice_in_dim(s["lng"], chip_cols, 256, axis=1)
    g_lnb = lax.dynamic_slice_in_dim(s["lnb"], chip_cols, 256, axis=1)

    turn = lambda a: jnp.swapaxes(a, 1, 2)
    a_w_in, m_a_w_in, v_a_w_in = turn(a_w_in), turn(m_a_w_in), turn(v_a_w_in)
    grads = [g_pre, g_post, g_wa_t[None], g_sinks, g_pool_w, g_pool_scale, g_ab_w_out, g_c_w_in, g_dw, g_dwb, g_lng,
             g_lnb, g_c_w_out]
    weights = [pre_norm, post_norm, a_w_in, a_sinks, b_pool_w, b_pool_scale, ab_w_out, c_w_in, c_dw_w, c_dw_b, c_ln_g,
               c_ln_b, c_w_out]
    moms = [m_pre_norm, m_post_norm, m_a_w_in, m_a_sinks, m_b_pool_w, m_b_pool_scale, m_ab_w_out, m_c_w_in, m_c_dw_w,
            m_c_dw_b, m_c_ln_g, m_c_ln_b, m_c_w_out]
    vars_ = [v_pre_norm, v_post_norm, v_a_w_in, v_a_sinks, v_b_pool_w, v_b_pool_scale, v_ab_w_out, v_c_w_in, v_c_dw_w,
             v_c_dw_b, v_c_ln_g, v_c_ln_b, v_c_w_out]
    tensors = list(zip(weights, grads, moms, vars_))
    matrices = (2, 6, 7, 12)
    others = [k for k in range(len(tensors)) if k not in matrices]
    updates = dict(zip(matrices, _adamw([tensors[k] for k in matrices], 4, "adamw_matrices")))
    updates.update(zip(others, _adamw([tensors[k] for k in others], 1, "adamw_small")))
    deltas, new_m, new_v = ([updates[k][j] for k in range(len(tensors))] for j in range(3))
    for outs in (grads, deltas, new_m, new_v):
        outs[2] = turn(outs[2])
    return (s["loss"][0, 0], gx[None], *grads, *deltas, *new_m, *new_v)
```
